```python
import jax
import jax.numpy as jnp
from jax import lax
import numpy as np

D_MODEL = 1024
BATCH = 8
SEQ = 8192
DEPTH = 2

GRID_W = 64
CTX_LEN = 256
N_EVEN = (DEPTH + 1) // 2
N_ODD = DEPTH // 2

HEAD_DIM = 64
ATTN_HEADS = (D_MODEL // 2) // HEAD_DIM
ATTN_KV_HEADS = ATTN_HEADS // 4
ATTN_GROUP = ATTN_HEADS // ATTN_KV_HEADS
WINDOW = 128
ATTN_BLOCK = 128
ROPE_BASE = 10000.0

HGRN_EXPAND = 128
HGRN_WIDTH = D_MODEL // 2
HGRN_HEADS = HGRN_WIDTH // HGRN_EXPAND
HGRN_DK = HGRN_EXPAND
HGRN_DV = HGRN_EXPAND
CHUNK = 64

RET_HEADS = 4
RET_DK = D_MODEL // RET_HEADS
RET_DV = 2 * RET_DK
RET_BASE = 10000.0

FFN_HIDDEN = -(-8 * D_MODEL // (3 * 256)) * 256

A_Q = ATTN_HEADS * HEAD_DIM
A_KV = ATTN_KV_HEADS * HEAD_DIM
EVEN_PARTS = (('a_q', A_Q), ('a_k', A_KV), ('a_v', A_KV), ('b_q', HGRN_WIDTH), ('b_ff', HGRN_WIDTH),
              ('b_fb', HGRN_WIDTH), ('b_i', HGRN_WIDTH), ('b_g', HGRN_WIDTH))
EVEN_IN = A_Q + 2 * A_KV + 5 * HGRN_WIDTH
EVEN_OUT = A_Q + HGRN_HEADS * HGRN_DV
ODD_PARTS = (('q', RET_HEADS * RET_DK), ('k', RET_HEADS * RET_DK), ('v', RET_HEADS * RET_DV),
             ('g', RET_HEADS * RET_DV))
ODD_IN = 2 * RET_HEADS * RET_DK + 2 * RET_HEADS * RET_DV
ODD_OUT = RET_HEADS * RET_DV
EPS = 1e-6
F32 = jnp.float32

kernel_name = 'hybrid_swa_hgrn2_retention_dit_prefix'


def _offsets(parts):
    out, start = {}, 0
    for name, width in parts:
        out[name] = (start, start + width)
        start += width
    return out


def split_parts(u, parts):
    return {n: u[..., s:e] for n, (s, e) in _offsets(parts).items()}


def project_parts(h, w, parts, names):
    off = _offsets(parts)
    return {n: h @ w[:, off[n][0]:off[n][1]] for n in names}


def rms_norm(x, g=None):
    xf = x.astype(F32)
    y = xf * lax.rsqrt(jnp.mean(xf * xf, axis=-1, keepdims=True) + EPS)
    if g is not None:
        y = y * g.astype(F32)
    return y.astype(x.dtype)


def modulate(h, shift, scale):
    return h * (1.0 + scale) + shift


def to_heads(u, n_heads):
    b, l, _ = u.shape
    return u.reshape(b, l, n_heads, -1).transpose(0, 2, 1, 3)


def from_heads(o):
    return o.transpose(0, 2, 1, 3)


def flip(a):
    return jnp.flip(a, axis=2)


def axial_rope_tables(n_tok):
    t = jnp.arange(n_tok)
    row = (t // GRID_W).astype(F32)
    col = (t % GRID_W).astype(F32)
    n_freq = HEAD_DIM // 4
    inv = ROPE_BASE ** (-jnp.arange(n_freq, dtype=F32) / n_freq)
    ang = jnp.concatenate([row[:, None] * inv, col[:, None] * inv], axis=-1)
    return jnp.cos(ang), jnp.sin(ang)


def retention_rope_tables(n_tok):
    theta = 1.0 / (RET_BASE ** jnp.linspace(0.0, 1.0, RET_DK // 2, dtype=F32))
    ang = jnp.arange(n_tok, dtype=F32)[:, None] * theta
    return jnp.cos(ang), jnp.sin(ang)


def apply_rope(x, cos, sin):
    half = x.shape[-1] // 2
    xf = x.astype(F32)
    x1, x2 = xf[..., :half], xf[..., half:]
    return jnp.concatenate([x1 * cos - x2 * sin, x2 * cos + x1 * sin], axis=-1).astype(x.dtype)


def attn_q_heads(u, g):
    b, l, _ = u.shape
    q = rms_norm(u.reshape(b, l, ATTN_KV_HEADS, ATTN_GROUP, HEAD_DIM), g)
    return q.transpose(0, 2, 3, 1, 4)


def kv_heads(u, g=None):
    b, l, _ = u.shape
    h = u.reshape(b, l, ATTN_KV_HEADS, HEAD_DIM)
    if g is not None:
        h = rms_norm(h, g)
    return h.transpose(0, 2, 1, 3)


def window_attention(q, k, v, k_ctx, v_ctx, sink):
    b, kv, g, l, d = q.shape
    n_blocks = l // ATTN_BLOCK
    span = ATTN_BLOCK + 2 * WINDOW
    pad = ((0, 0), (0, 0), (WINDOW, WINDOW), (0, 0))
    kp, vp = jnp.pad(k, pad), jnp.pad(v, pad)
    scale = d ** -0.5
    sink_col = jnp.broadcast_to(sink[None, :, :, None, None], (b, kv, g, ATTN_BLOCK, 1))

    def block(i):
        start = i * ATTN_BLOCK
        qb = lax.dynamic_slice_in_dim(q, start, ATTN_BLOCK, axis=3)
        kb = lax.dynamic_slice_in_dim(kp, start, span, axis=2)
        vb = lax.dynamic_slice_in_dim(vp, start, span, axis=2)
        q_pos = start + jnp.arange(ATTN_BLOCK)
        k_pos = start - WINDOW + jnp.arange(span)
        valid = ((jnp.abs(k_pos[None, :] - q_pos[:, None]) <= WINDOW)
                 & (k_pos >= 0)[None, :] & (k_pos < l)[None, :])
        s_lat = jnp.einsum('bkgqd,bksd->bkgqs', qb, kb, preferred_element_type=F32) * scale
        s_lat = jnp.where(valid, s_lat, -jnp.inf)
        s_ctx = jnp.einsum('bkgqd,bkcd->bkgqc', qb, k_ctx, preferred_element_type=F32) * scale
        p = jax.nn.softmax(jnp.concatenate([sink_col, s_lat, s_ctx], axis=-1), axis=-1)
        p_lat = p[..., 1:1 + span].astype(v.dtype)
        p_ctx = p[..., 1 + span:].astype(v.dtype)
        return (jnp.einsum('bkgqs,bksd->bkgqd', p_lat, vb)
                + jnp.einsum('bkgqc,bkcd->bkgqd', p_ctx, v_ctx))

    o = lax.map(block, jnp.arange(n_blocks))
    return o.transpose(1, 0, 4, 2, 3, 5).reshape(b, l, kv * g * d)


def context_attention(q, k, v, sink):
    b, kv, g, lc, d = q.shape
    s = jnp.einsum('bkgqd,bkcd->bkgqc', q, k, preferred_element_type=F32) * d ** -0.5
    sink_col = jnp.broadcast_to(sink[None, :, :, None, None], (b, kv, g, lc, 1))
    p = jax.nn.softmax(jnp.concatenate([sink_col, s], axis=-1), axis=-1)[..., 1:].astype(v.dtype)
    o = jnp.einsum('bkgqc,bkcd->bkgqd', p, v)
    return o.transpose(0, 3, 1, 2, 4).reshape(b, lc, kv * g * d)


def chunk_scan(q_in, k_in, v, decay, s0):
    xs = (jnp.moveaxis(q_in, 2, 0), jnp.moveaxis(k_in, 2, 0), jnp.moveaxis(v, 2, 0), decay)

    def step(s, inp):
        q_n, k_n, v_n, a_n = inp
        o_n = jnp.einsum('bhcd,bhde->bhce', q_n, s)
        s = a_n * s + jnp.einsum('bhcd,bhce->bhde', k_n, v_n)
        return s, o_n

    s, o = lax.scan(step, s0, xs)
    return jnp.moveaxis(o, 0, 2), s


def gla_chunked(q, k, v, log_f, s0):
    b, h, l, dk = q.shape
    dv = v.shape[-1]
    n = l // CHUNK
    qc, kc, lc = (a.reshape(b, h, n, CHUNK, dk) for a in (q, k, log_f))
    vc = v.reshape(b, h, n, CHUNK, dv)
    cum = jnp.cumsum(lc, axis=3)
    ref = cum[:, :, :, CHUNK // 2:CHUNK // 2 + 1]
    scores = jnp.einsum('bhntd,bhnsd->bhnts', qc * jnp.exp(cum - ref), kc * jnp.exp(ref - cum))
    lower = jnp.tril(jnp.ones((CHUNK, CHUNK), dtype=bool))
    o_intra = jnp.einsum('bhnts,bhnse->bhnte', jnp.where(lower, scores, 0.0), vc)
    cum_last = cum[:, :, :, -1:]
    decay = jnp.moveaxis(jnp.exp(cum_last[:, :, :, 0]), 2, 0)[..., None]
    o_inter, s = chunk_scan(qc * jnp.exp(cum), kc * jnp.exp(cum_last - cum), vc, decay, s0)
    return (o_intra + o_inter).reshape(b, h, l, dv), s


def gla_final_state(k, v, log_f):
    cum = jnp.cumsum(log_f, axis=2)
    return jnp.einsum('bhld,bhle->bhde', k * jnp.exp(cum[:, :, -1:] - cum), v)


def retention_chunked(q, k, v, log_gamma, s0):
    b, h, l, dk = q.shape
    dv = v.shape[-1]
    n = l // CHUNK
    qc = q.reshape(b, h, n, CHUNK, dk)
    kc = k.reshape(b, h, n, CHUNK, dk)
    vc = v.reshape(b, h, n, CHUNK, dv)
    pos = jnp.arange(CHUNK, dtype=F32)
    rel = pos[:, None] - pos[None, :]
    dmat = jnp.where(rel >= 0, jnp.exp(log_gamma[:, None, None] * jnp.maximum(rel, 0.0)), 0.0)
    scores = jnp.einsum('bhntd,bhnsd->bhnts', qc, kc) * dmat[None, :, None]
    o_intra = jnp.einsum('bhnts,bhnse->bhnte', scores, vc)
    lg = log_gamma[:, None]
    q_in = qc * jnp.exp(lg * (pos + 1.0))[None, :, None, :, None]
    k_in = kc * jnp.exp(lg * (CHUNK - 1.0 - pos))[None, :, None, :, None]
    decay = jnp.broadcast_to(jnp.exp(log_gamma * CHUNK)[None, None, :, None, None], (n, 1, h, 1, 1))
    o_inter, s = chunk_scan(q_in, k_in, vc, decay, s0)
    return (o_intra + o_inter).reshape(b, h, l, dv), s


def retention_final_state(k, v, log_gamma):
    lc = k.shape[2]
    w = jnp.exp(log_gamma[:, None] * (lc - 1.0 - jnp.arange(lc, dtype=F32)))
    return jnp.einsum('bhld,bhle->bhde', k * w[None, :, :, None], v)


def gated_head_norm(o, g_raw, gain=None):
    b, h, l, dv = o.shape
    y = rms_norm(from_heads(o), gain) * jax.nn.silu(g_raw.reshape(b, l, h, dv).astype(F32))
    return y.reshape(b, l, h * dv)


def even_mixer(h_ctx, h_lat, w_in, w_out, qk_g, sink, out_g, lb, cos, sin, need_ctx):
    dt = h_lat.dtype
    n_b = h_ctx.shape[0]
    sink = sink.astype(F32).reshape(ATTN_KV_HEADS, ATTN_GROUP)
    lb = lb.reshape(HGRN_HEADS, 1, HGRN_DK)
    p = split_parts(h_lat @ w_in, EVEN_PARTS)
    names = [n for n, _ in EVEN_PARTS] if need_ctx else ['a_k', 'a_v', 'b_ff', 'b_fb', 'b_i']
    pc = project_parts(h_ctx, w_in, EVEN_PARTS, names)

    def gates(f_raw):
        f = lb + (1.0 - lb) * jax.nn.sigmoid(to_heads(f_raw, HGRN_HEADS).astype(F32))
        return 1.0 - f, jnp.log(f)

    k_ctx = kv_heads(pc['a_k'], qk_g[1])
    v_ctx = kv_heads(pc['a_v'])
    q_lat = apply_rope(attn_q_heads(p['a_q'], qk_g[0]), cos, sin)
    k_lat = apply_rope(kv_heads(p['a_k'], qk_g[1]), cos, sin)
    a_lat = window_attention(q_lat, k_lat, kv_heads(p['a_v']), k_ctx, v_ctx, sink)

    k_fw_c, lf_fw_c = gates(pc['b_ff'])
    k_bw_c, lf_bw_c = gates(pc['b_fb'])
    i_c = to_heads(pc['b_i'], HGRN_HEADS).astype(F32)
    if need_ctx:
        zeros = jnp.zeros((n_b, HGRN_HEADS, HGRN_DK, HGRN_DV), F32)
        q_c = jax.nn.silu(to_heads(pc['b_q'], HGRN_HEADS).astype(F32))
        o_fw_c, s_fw = gla_chunked(q_c, k_fw_c, i_c, lf_fw_c, zeros)
        o_bw_c, s_bw = gla_chunked(flip(q_c), flip(k_bw_c), flip(i_c), flip(lf_bw_c), zeros)
    else:
        s_fw = gla_final_state(k_fw_c, i_c, lf_fw_c)
        s_bw = gla_final_state(flip(k_bw_c), flip(i_c), flip(lf_bw_c))
    k_fw, lf_fw = gates(p['b_ff'])
    k_bw, lf_bw = gates(p['b_fb'])
    q_l = jax.nn.silu(to_heads(p['b_q'], HGRN_HEADS).astype(F32))
    i_l = to_heads(p['b_i'], HGRN_HEADS).astype(F32)
    o_fw, _ = gla_chunked(q_l, k_fw, i_l, lf_fw, s_fw)
    o_bw, _ = gla_chunked(flip(q_l), flip(k_bw), flip(i_l), flip(lf_bw), s_bw)
    b_lat = gated_head_norm(o_fw + flip(o_bw), p['b_g'], out_g)

    y_lat = jnp.concatenate([a_lat.astype(dt), b_lat.astype(dt)], axis=-1) @ w_out
    if not need_ctx:
        return None, y_lat
    a_ctx = context_attention(attn_q_heads(pc['a_q'], qk_g[0]), k_ctx, v_ctx, sink)
    b_ctx = gated_head_norm(o_fw_c + flip(o_bw_c), pc['b_g'], out_g)
    y_ctx = jnp.concatenate([a_ctx.astype(dt), b_ctx.astype(dt)], axis=-1) @ w_out
    return y_ctx, y_lat


def odd_mixer(h_ctx, h_lat, w_in, w_out, cos, sin, need_ctx):
    dt = h_lat.dtype
    n_b = h_ctx.shape[0]
    log_g_fw = jnp.log(1.0 - 2.0 ** (-5.0 - jnp.arange(RET_HEADS, dtype=F32)))
    log_g_bw = log_g_fw[::-1]
    k_scale = RET_DK ** -0.5
    p = split_parts(h_lat @ w_in, ODD_PARTS)
    names = [n for n, _ in ODD_PARTS] if need_ctx else ['k', 'v']
    pc = project_parts(h_ctx, w_in, ODD_PARTS, names)

    k_c = to_heads(pc['k'], RET_HEADS).astype(F32) * k_scale
    v_c = to_heads(pc['v'], RET_HEADS).astype(F32)
    if need_ctx:
        zeros = jnp.zeros((n_b, RET_HEADS, RET_DK, RET_DV), F32)
        q_c = to_heads(pc['q'], RET_HEADS).astype(F32)
        o_fw_c, s_fw = retention_chunked(q_c, k_c, v_c, log_g_fw, zeros)
        o_bw_c, s_bw = retention_chunked(flip(q_c), flip(k_c), flip(v_c), log_g_bw, zeros)
    else:
        s_fw = retention_final_state(k_c, v_c, log_g_fw)
        s_bw = retention_final_state(flip(k_c), flip(v_c), log_g_bw)

    q_l = apply_rope(to_heads(p['q'], RET_HEADS).astype(F32), cos, sin)
    k_l = apply_rope(to_heads(p['k'], RET_HEADS).astype(F32), cos, sin) * k_scale
    v_l = to_heads(p['v'], RET_HEADS).astype(F32)
    o_fw, _ = retention_chunked(q_l, k_l, v_l, log_g_fw, s_fw)
    o_bw, _ = retention_chunked(flip(q_l), flip(k_l), flip(v_l), log_g_bw, s_bw)
    y_lat = gated_head_norm(o_fw + flip(o_bw), p['g']).astype(dt) @ w_out
    if not need_ctx:
        return None, y_lat
    y_ctx = gated_head_norm(o_fw_c + flip(o_bw_c), pc['g']).astype(dt) @ w_out
    return y_ctx, y_lat


def swiglu(h, w_in, w_out):
    gate, up = jnp.split(h @ w_in, 2, axis=-1)
    return (jax.nn.silu(gate) * up) @ w_out


def _fwd_setup_inputs(seed: int = 0) -> dict:
    key = jax.random.key(seed)
    ks = jax.random.split(key, 17)
    d = D_MODEL

    def nrm(k, shape, scale):
        return jax.random.normal(k, shape, F32) * scale

    return {
        'x': nrm(ks[0], (BATCH, SEQ, d), 1.0),
        'c': nrm(ks[1], (BATCH, d), 1.0),
        'ctx': nrm(ks[2], (BATCH, CTX_LEN, d), 1.0),
        'c_ctx': nrm(ks[3], (d,), 1.0),
        'mod_w': nrm(ks[4], (DEPTH, d, 6 * d), 0.5 * d ** -0.5),
        'mod_b': nrm(ks[5], (DEPTH, 6 * d), 0.02),
        'norm_g': 1.0 + nrm(ks[6], (DEPTH, 2, d), 0.02),
        'ffn_w_in': nrm(ks[7], (DEPTH, d, 2 * FFN_HIDDEN), d ** -0.5),
        'ffn_w_out': nrm(ks[8], (DEPTH, FFN_HIDDEN, d), FFN_HIDDEN ** -0.5),
        'even_w_in': nrm(ks[9], (N_EVEN, d, EVEN_IN), d ** -0.5),
        'even_w_out': nrm(ks[10], (N_EVEN, EVEN_OUT, d), EVEN_OUT ** -0.5),
        'attn_qk_norm_g': 1.0 + nrm(ks[11], (N_EVEN, 2, HEAD_DIM), 0.02),
        'attn_sink': nrm(ks[12], (N_EVEN, ATTN_HEADS), 0.5),
        'hgrn_out_norm_g': 1.0 + nrm(ks[13], (N_EVEN, HGRN_DV), 0.02),
        'hgrn_lb': nrm(ks[14], (N_EVEN + 1, HGRN_WIDTH), 0.1),
        'odd_w_in': nrm(ks[15], (N_ODD, d, ODD_IN), d ** -0.5),
        'odd_w_out': nrm(ks[16], (N_ODD, ODD_OUT, d), ODD_OUT ** -0.5),
    }


def _fwd_reference(x, c, ctx, c_ctx, mod_w, mod_b, norm_g, ffn_w_in, ffn_w_out, even_w_in, even_w_out,
              attn_qk_norm_g, attn_sink, hgrn_out_norm_g, hgrn_lb, odd_w_in, odd_w_out):
    n_tok = x.shape[1]
    rope_cos, rope_sin = axial_rope_tables(n_tok)
    ret_cos, ret_sin = retention_rope_tables(n_tok)
    lower_bounds = jnp.cumsum(jax.nn.softmax(hgrn_lb.astype(F32), axis=0), axis=0)
    cond_lat = jax.nn.silu(c)
    cond_ctx = jax.nn.silu(c_ctx)
    for layer in range(DEPTH):
        last = layer == DEPTH - 1
        j = layer // 2
        m_lat = jnp.split((cond_lat @ mod_w[layer] + mod_b[layer])[:, None, :], 6, axis=-1)
        m_ctx = jnp.split((cond_ctx @ mod_w[layer] + mod_b[layer])[None, None, :], 6, axis=-1)
        h_lat = modulate(rms_norm(x, norm_g[layer, 0]), m_lat[0], m_lat[1])
        h_ctx = modulate(rms_norm(ctx, norm_g[layer, 0]), m_ctx[0], m_ctx[1])
        if layer % 2 == 0:
            y_ctx, y_lat = even_mixer(h_ctx, h_lat, even_w_in[j], even_w_out[j], attn_qk_norm_g[j],
                                      attn_sink[j], hgrn_out_norm_g[j], lower_bounds[j],
                                      rope_cos, rope_sin, not last)
        else:
            y_ctx, y_lat = odd_mixer(h_ctx, h_lat, odd_w_in[j], odd_w_out[j], ret_cos, ret_sin, not last)
        x = x + m_lat[2] * y_lat
        x = x + m_lat[5] * swiglu(modulate(rms_norm(x, norm_g[layer, 1]), m_lat[3], m_lat[4]),
                                  ffn_w_in[layer], ffn_w_out[layer])
        if not last:
            ctx = ctx + m_ctx[2] * y_ctx
            ctx = ctx + m_ctx[5] * swiglu(modulate(rms_norm(ctx, norm_g[layer, 1]), m_ctx[3], m_ctx[4]),
                                          ffn_w_in[layer], ffn_w_out[layer])
    return x


import jax as _jax
import jax.numpy as _jnp

TWIN_FORMAT = 'train_step'
FWD_PARAMS = ['x', 'c', 'ctx', 'c_ctx', 'mod_w', 'mod_b', 'norm_g', 'ffn_w_in', 'ffn_w_out', 'even_w_in', 'even_w_out', 'attn_qk_norm_g', 'attn_sink', 'hgrn_out_norm_g', 'hgrn_lb', 'odd_w_in', 'odd_w_out']
TWIN_WEIGHTS = ['c_ctx', 'mod_w', 'mod_b', 'norm_g', 'ffn_w_in', 'ffn_w_out', 'even_w_in', 'even_w_out', 'attn_qk_norm_g', 'attn_sink', 'hgrn_out_norm_g', 'hgrn_lb', 'odd_w_in', 'odd_w_out']
TWIN_DIFF_INPUT = 'x'
TWIN_INPUTS = ['x', 'c', 'ctx', 'c_ctx', 'mod_w', 'mod_b', 'norm_g', 'ffn_w_in', 'ffn_w_out', 'even_w_in', 'even_w_out', 'attn_qk_norm_g', 'attn_sink', 'hgrn_out_norm_g', 'hgrn_lb', 'odd_w_in', 'odd_w_out', 'loss_target', 'm_c_ctx', 'm_mod_w', 'm_mod_b', 'm_norm_g', 'm_ffn_w_in', 'm_ffn_w_out', 'm_even_w_in', 'm_even_w_out', 'm_attn_qk_norm_g', 'm_attn_sink', 'm_hgrn_out_norm_g', 'm_hgrn_lb', 'm_odd_w_in', 'm_odd_w_out', 'v_c_ctx', 'v_mod_w', 'v_mod_b', 'v_norm_g', 'v_ffn_w_in', 'v_ffn_w_out', 'v_even_w_in', 'v_even_w_out', 'v_attn_qk_norm_g', 'v_attn_sink', 'v_hgrn_out_norm_g', 'v_hgrn_lb', 'v_odd_w_in', 'v_odd_w_out']
TWIN_OUTPUTS = ['loss', 'grad_x', 'grad_c_ctx', 'grad_mod_w', 'grad_mod_b', 'grad_norm_g', 'grad_ffn_w_in', 'grad_ffn_w_out', 'grad_even_w_in', 'grad_even_w_out', 'grad_attn_qk_norm_g', 'grad_attn_sink', 'grad_hgrn_out_norm_g', 'grad_hgrn_lb', 'grad_odd_w_in', 'grad_odd_w_out', 'delta_c_ctx', 'delta_mod_w', 'delta_mod_b', 'delta_norm_g', 'delta_ffn_w_in', 'delta_ffn_w_out', 'delta_even_w_in', 'delta_even_w_out', 'delta_attn_qk_norm_g', 'delta_attn_sink', 'delta_hgrn_out_norm_g', 'delta_hgrn_lb', 'delta_odd_w_in', 'delta_odd_w_out', 'new_m_c_ctx', 'new_m_mod_w', 'new_m_mod_b', 'new_m_norm_g', 'new_m_ffn_w_in', 'new_m_ffn_w_out', 'new_m_even_w_in', 'new_m_even_w_out', 'new_m_attn_qk_norm_g', 'new_m_attn_sink', 'new_m_hgrn_out_norm_g', 'new_m_hgrn_lb', 'new_m_odd_w_in', 'new_m_odd_w_out', 'new_v_c_ctx', 'new_v_mod_w', 'new_v_mod_b', 'new_v_norm_g', 'new_v_ffn_w_in', 'new_v_ffn_w_out', 'new_v_even_w_in', 'new_v_even_w_out', 'new_v_attn_qk_norm_g', 'new_v_attn_sink', 'new_v_hgrn_out_norm_g', 'new_v_hgrn_lb', 'new_v_odd_w_in', 'new_v_odd_w_out']
TWIN_LEAF_KINDS = {'loss': 'loss', 'grad_x': 'grad_x', 'grad_c_ctx': 'grad_w', 'grad_mod_w': 'grad_w', 'grad_mod_b': 'grad_w', 'grad_norm_g': 'grad_w', 'grad_ffn_w_in': 'grad_w', 'grad_ffn_w_out': 'grad_w', 'grad_even_w_in': 'grad_w', 'grad_even_w_out': 'grad_w', 'grad_attn_qk_norm_g': 'grad_w', 'grad_attn_sink': 'grad_w', 'grad_hgrn_out_norm_g': 'grad_w', 'grad_hgrn_lb': 'grad_w', 'grad_odd_w_in': 'grad_w', 'grad_odd_w_out': 'grad_w', 'delta_c_ctx': 'delta_w', 'delta_mod_w': 'delta_w', 'delta_mod_b': 'delta_w', 'delta_norm_g': 'delta_w', 'delta_ffn_w_in': 'delta_w', 'delta_ffn_w_out': 'delta_w', 'delta_even_w_in': 'delta_w', 'delta_even_w_out': 'delta_w', 'delta_attn_qk_norm_g': 'delta_w', 'delta_attn_sink': 'delta_w', 'delta_hgrn_out_norm_g': 'delta_w', 'delta_hgrn_lb': 'delta_w', 'delta_odd_w_in': 'delta_w', 'delta_odd_w_out': 'delta_w', 'new_m_c_ctx': 'new_m', 'new_m_mod_w': 'new_m', 'new_m_mod_b': 'new_m', 'new_m_norm_g': 'new_m', 'new_m_ffn_w_in': 'new_m', 'new_m_ffn_w_out': 'new_m', 'new_m_even_w_in': 'new_m', 'new_m_even_w_out': 'new_m', 'new_m_attn_qk_norm_g': 'new_m', 'new_m_attn_sink': 'new_m', 'new_m_hgrn_out_norm_g': 'new_m', 'new_m_hgrn_lb': 'new_m', 'new_m_odd_w_in': 'new_m', 'new_m_odd_w_out': 'new_m', 'new_v_c_ctx': 'new_v', 'new_v_mod_w': 'new_v', 'new_v_mod_b': 'new_v', 'new_v_norm_g': 'new_v', 'new_v_ffn_w_in': 'new_v', 'new_v_ffn_w_out': 'new_v', 'new_v_even_w_in': 'new_v', 'new_v_even_w_out': 'new_v', 'new_v_attn_qk_norm_g': 'new_v', 'new_v_attn_sink': 'new_v', 'new_v_hgrn_out_norm_g': 'new_v', 'new_v_hgrn_lb': 'new_v', 'new_v_odd_w_in': 'new_v', 'new_v_odd_w_out': 'new_v'}


def _forward(args):
    return _fwd_reference(*[args[k] for k in FWD_PARAMS])


def _output_shape():
    out = _jax.eval_shape(lambda: _forward(_fwd_setup_inputs(0)))
    return out.shape, out.dtype

N_MICROBATCH = 1
ADAM_LR = 0.001
ADAM_B1 = 0.9
ADAM_B2 = 0.999
ADAM_EPS = 1e-08
ADAM_WD = 0.01
ADAM_STEP = 10
PER_EXAMPLE_BATCH_AXIS = {'x': 0, 'c': 0, 'ctx': 0, 'loss_target': 0}
SHARED_INPUTS = []
_WEIGHT_DTYPES = {'c_ctx': _jnp.float32, 'mod_w': _jnp.float32, 'mod_b': _jnp.float32, 'norm_g': _jnp.float32, 'ffn_w_in': _jnp.float32, 'ffn_w_out': _jnp.float32, 'even_w_in': _jnp.float32, 'even_w_out': _jnp.float32, 'attn_qk_norm_g': _jnp.float32, 'attn_sink': _jnp.float32, 'hgrn_out_norm_g': _jnp.float32, 'hgrn_lb': _jnp.float32, 'odd_w_in': _jnp.float32, 'odd_w_out': _jnp.float32}
MOMENT_SCALE = {'c_ctx': 9.942082e-02, 'mod_w': 1.745580e+00, 'mod_b': 3.877405e+00, 'norm_g': 5.034670e+00, 'ffn_w_in': 9.325223e-02, 'ffn_w_out': 1.184980e-01, 'even_w_in': 1.890918e-01, 'even_w_out': 1.816861e-01, 'attn_qk_norm_g': 3.117021e-01, 'attn_sink': 2.469725e-02, 'hgrn_out_norm_g': 1.077741e+01, 'hgrn_lb': 8.900817e-03, 'odd_w_in': 1.048983e-01, 'odd_w_out': 1.011178e-01}


def _to_microbatches(a, axis):
    t = _jnp.moveaxis(a, axis, 0)
    t = t.reshape((N_MICROBATCH, t.shape[0] // N_MICROBATCH) + t.shape[1:])
    return _jnp.moveaxis(t, 1, axis + 1)


def setup_inputs(seed: int = 0) -> dict:
    inp = _fwd_setup_inputs(seed)
    key = _jax.random.fold_in(_jax.random.key(seed), 7919)
    shape, _ = _output_shape()
    out = dict(inp)
    out["loss_target"] = _jax.random.normal(_jax.random.fold_in(key, 0), shape, _jnp.float32)
    for i, name in enumerate(TWIN_WEIGHTS):
        w = inp[name].astype(_jnp.float32)
        if MOMENT_SCALE is None:
            s = _jnp.sqrt(_jnp.mean(_jnp.square(w)) + 1e-30)
        else:
            s = MOMENT_SCALE[name]
        km, kv = _jax.random.split(_jax.random.fold_in(key, i + 1))
        out[name] = w
        out["m_" + name] = s * _jax.random.normal(km, w.shape, _jnp.float32)
        out["v_" + name] = (s * s) * _jax.random.uniform(kv, w.shape, _jnp.float32, 0.5, 1.5)
    if N_MICROBATCH > 1:
        for name, axis in PER_EXAMPLE_BATCH_AXIS.items():
            out[name] = _to_microbatches(out[name], axis)
    return {'x': out['x'], 'c': out['c'], 'ctx': out['ctx'], 'c_ctx': out['c_ctx'], 'mod_w': out['mod_w'], 'mod_b': out['mod_b'], 'norm_g': out['norm_g'], 'ffn_w_in': out['ffn_w_in'], 'ffn_w_out': out['ffn_w_out'], 'even_w_in': out['even_w_in'], 'even_w_out': out['even_w_out'], 'attn_qk_norm_g': out['attn_qk_norm_g'], 'attn_sink': out['attn_sink'], 'hgrn_out_norm_g': out['hgrn_out_norm_g'], 'hgrn_lb': out['hgrn_lb'], 'odd_w_in': out['odd_w_in'], 'odd_w_out': out['odd_w_out'], 'loss_target': out['loss_target'], 'm_c_ctx': out['m_c_ctx'], 'm_mod_w': out['m_mod_w'], 'm_mod_b': out['m_mod_b'], 'm_norm_g': out['m_norm_g'], 'm_ffn_w_in': out['m_ffn_w_in'], 'm_ffn_w_out': out['m_ffn_w_out'], 'm_even_w_in': out['m_even_w_in'], 'm_even_w_out': out['m_even_w_out'], 'm_attn_qk_norm_g': out['m_attn_qk_norm_g'], 'm_attn_sink': out['m_attn_sink'], 'm_hgrn_out_norm_g': out['m_hgrn_out_norm_g'], 'm_hgrn_lb': out['m_hgrn_lb'], 'm_odd_w_in': out['m_odd_w_in'], 'm_odd_w_out': out['m_odd_w_out'], 'v_c_ctx': out['v_c_ctx'], 'v_mod_w': out['v_mod_w'], 'v_mod_b': out['v_mod_b'], 'v_norm_g': out['v_norm_g'], 'v_ffn_w_in': out['v_ffn_w_in'], 'v_ffn_w_out': out['v_ffn_w_out'], 'v_even_w_in': out['v_even_w_in'], 'v_even_w_out': out['v_even_w_out'], 'v_attn_qk_norm_g': out['v_attn_qk_norm_g'], 'v_attn_sink': out['v_attn_sink'], 'v_hgrn_out_norm_g': out['v_hgrn_out_norm_g'], 'v_hgrn_lb': out['v_hgrn_lb'], 'v_odd_w_in': out['v_odd_w_in'], 'v_odd_w_out': out['v_odd_w_out']}


def _loss(weights, diff, rest, loss_target):
    with _jax.named_scope("forward"):
        args = {**rest, TWIN_DIFF_INPUT: diff, **{k: w.astype(_WEIGHT_DTYPES[k]) for k, w in weights.items()}}
        y = _forward(args)
    with _jax.named_scope("loss_head"):
        err = _jnp.square(y.astype(_jnp.float32) - loss_target)
        return 0.5 * _jnp.sum(_jnp.mean(err, axis=-1)) if err.ndim else 0.5 * err


def _adamw(w, g, m, v):
    m = ADAM_B1 * m + (1.0 - ADAM_B1) * g
    v = ADAM_B2 * v + (1.0 - ADAM_B2) * _jnp.square(g)
    m_hat = m / (1.0 - ADAM_B1 ** ADAM_STEP)
    v_hat = v / (1.0 - ADAM_B2 ** ADAM_STEP)
    delta = -ADAM_LR * (m_hat / (_jnp.sqrt(v_hat) + ADAM_EPS) + ADAM_WD * w)
    return delta, m, v


def reference(x, c, ctx, c_ctx, mod_w, mod_b, norm_g, ffn_w_in, ffn_w_out, even_w_in, even_w_out, attn_qk_norm_g, attn_sink, hgrn_out_norm_g, hgrn_lb, odd_w_in, odd_w_out, loss_target, m_c_ctx, m_mod_w, m_mod_b, m_norm_g, m_ffn_w_in, m_ffn_w_out, m_even_w_in, m_even_w_out, m_attn_qk_norm_g, m_attn_sink, m_hgrn_out_norm_g, m_hgrn_lb, m_odd_w_in, m_odd_w_out, v_c_ctx, v_mod_w, v_mod_b, v_norm_g, v_ffn_w_in, v_ffn_w_out, v_even_w_in, v_even_w_out, v_attn_qk_norm_g, v_attn_sink, v_hgrn_out_norm_g, v_hgrn_lb, v_odd_w_in, v_odd_w_out):
    given = dict(x=x, c=c, ctx=ctx, c_ctx=c_ctx, mod_w=mod_w, mod_b=mod_b, norm_g=norm_g, ffn_w_in=ffn_w_in, ffn_w_out=ffn_w_out, even_w_in=even_w_in, even_w_out=even_w_out, attn_qk_norm_g=attn_qk_norm_g, attn_sink=attn_sink, hgrn_out_norm_g=hgrn_out_norm_g, hgrn_lb=hgrn_lb, odd_w_in=odd_w_in, odd_w_out=odd_w_out, loss_target=loss_target, m_c_ctx=m_c_ctx, m_mod_w=m_mod_w, m_mod_b=m_mod_b, m_norm_g=m_norm_g, m_ffn_w_in=m_ffn_w_in, m_ffn_w_out=m_ffn_w_out, m_even_w_in=m_even_w_in, m_even_w_out=m_even_w_out, m_attn_qk_norm_g=m_attn_qk_norm_g, m_attn_sink=m_attn_sink, m_hgrn_out_norm_g=m_hgrn_out_norm_g, m_hgrn_lb=m_hgrn_lb, m_odd_w_in=m_odd_w_in, m_odd_w_out=m_odd_w_out, v_c_ctx=v_c_ctx, v_mod_w=v_mod_w, v_mod_b=v_mod_b, v_norm_g=v_norm_g, v_ffn_w_in=v_ffn_w_in, v_ffn_w_out=v_ffn_w_out, v_even_w_in=v_even_w_in, v_even_w_out=v_even_w_out, v_attn_qk_norm_g=v_attn_qk_norm_g, v_attn_sink=v_attn_sink, v_hgrn_out_norm_g=v_hgrn_out_norm_g, v_hgrn_lb=v_hgrn_lb, v_odd_w_in=v_odd_w_in, v_odd_w_out=v_odd_w_out)
    weights = {n: given[n] for n in TWIN_WEIGHTS}
    shared = {n: given[n] for n in SHARED_INPUTS}
    per_example = {n: given[n] for n in ['x', 'c', 'ctx']}
    grad_fn = _jax.value_and_grad(_loss, argnums=(0, 1))

    def one_microbatch(ex, loss_target):
        ex = dict(ex)
        diff = ex.pop(TWIN_DIFF_INPUT)
        return grad_fn(weights, diff, {**shared, **ex}, loss_target)

    if N_MICROBATCH == 1:
        loss, (grad_w, grad_x) = one_microbatch(per_example, given["loss_target"])
    else:
        def body(carry, xs):
            loss_sum, grad_sum = carry
            l_k, (gw_k, gx_k) = one_microbatch(xs[0], xs[1])
            with _jax.named_scope("update"):
                return (loss_sum + l_k, _jax.tree.map(_jnp.add, grad_sum, gw_k)), gx_k

        init = (_jnp.zeros((), _jnp.float32), _jax.tree.map(_jnp.zeros_like, weights))
        (loss, grad_w), grad_x = _jax.lax.scan(body, init, (per_example, given["loss_target"]))
    with _jax.named_scope("update"):
        delta_w, new_m, new_v = {}, {}, {}
        for n in TWIN_WEIGHTS:
            delta_w[n], new_m[n], new_v[n] = _adamw(weights[n], grad_w[n], given["m_" + n], given["v_" + n])
    return (loss, grad_x, *[grad_w[n] for n in TWIN_WEIGHTS], *[delta_w[n] for n in TWIN_WEIGHTS],
            *[new_m[n] for n in TWIN_WEIGHTS], *[new_v[n] for n in TWIN_WEIGHTS])
```

```python
import functools
import math

import numpy as np
import jax
import jax.numpy as jnp
from jax import lax
from jax.experimental import pallas as pl
from jax.experimental.pallas import tpu as pltpu

F32 = jnp.float32
BF16 = jnp.bfloat16
EPS = 1e-6
TM = 256
CHUNK = 64
QB = 128
WINDOW = 128
NEG = -1e30
MESH = pl.DeviceIdType.MESH

ADAM_LR, ADAM_B1, ADAM_B2, ADAM_EPS, ADAM_WD, ADAM_STEP = 0.001, 0.9, 0.999, 1e-08, 0.01, 10


def _pcall(body, **kw):
    return pl.pallas_call(body, **kw)


def _pick(n, cap):
    best = None
    for m in range(128, min(n, cap) + 1, 128):
        if n % m == 0:
            best = m
    assert best is not None, (n, cap)
    return best


def _bf(x):
    return x.astype(BF16)


def _dot(a, b):
    return jnp.dot(_bf(a), _bf(b), preferred_element_type=F32)


def _dot_nt(a, b):
    return lax.dot_general(_bf(a), _bf(b), (((1,), (1,)), ((), ())), preferred_element_type=F32)


def _dot_tn(a, b):
    return lax.dot_general(_bf(a), _bf(b), (((0,), (0,)), ((), ())), preferred_element_type=F32)


def _dot_exact(a, b):
    return jnp.dot(a, b, preferred_element_type=F32, precision=lax.Precision.HIGHEST)


def _sigmoid(x):
    return 1.0 / (1.0 + jnp.exp(-x))


def _iota(shape, dim):
    return lax.broadcasted_iota(jnp.int32, shape, dim)


def _mm_nn(a, b, *, lead=None, out_dtype=F32, name):
    m, k = a.shape
    n = b.shape[-1]
    bm = 768 if m % 768 == 0 else TM
    bn = _pick(n, 1024)

    def body(a_ref, b_ref, o_ref):
        o_ref[...] = _dot(a_ref[...], b_ref[...]).astype(o_ref.dtype)

    if lead is None:
        b_spec = pl.BlockSpec((k, bn), lambda i, j: (0, j))
    else:
        b_spec = pl.BlockSpec((None, k, bn), lambda i, j: (lead, 0, j))
    return _pcall(
        body, name=name, grid=(m // bm, n // bn),
        in_specs=[pl.BlockSpec((bm, k), lambda i, j: (i, 0)), b_spec],
        out_specs=pl.BlockSpec((bm, bn), lambda i, j: (i, j)),
        out_shape=jax.ShapeDtypeStruct((m, n), out_dtype),
    )(a, b)


def _mm_nt(a, b, *, lead=None, name):
    m, n = a.shape
    k = b.shape[-2]
    bm = 768 if m % 768 == 0 else TM
    bk = _pick(k, 512)

    def body(a_ref, b_ref, o_ref):
        o_ref[...] = _dot_nt(a_ref[...], b_ref[...])

    if lead is None:
        b_spec = pl.BlockSpec((bk, n), lambda i, j: (j, 0))
    else:
        b_spec = pl.BlockSpec((None, bk, n), lambda i, j: (lead, j, 0))
    return _pcall(
        body, name=name, grid=(m // bm, k // bk),
        in_specs=[pl.BlockSpec((bm, n), lambda i, j: (i, 0)), b_spec],
        out_specs=pl.BlockSpec((bm, bk), lambda i, j: (i, j)),
        out_shape=jax.ShapeDtypeStruct((m, k), F32),
    )(a, b)


def _mm_tn(a, b, *, name):
    t, k = a.shape
    n = b.shape[1]
    bt = 768 if t % 768 == 0 else TM
    bk = _pick(k, 1536)
    bn = _pick(n, 1024) if n % 1024 == 0 or n < 1664 else _pick(n, 1664)

    def body(a_ref, b_ref, o_ref):
        @pl.when(pl.program_id(2) == 0)
        def _():
            o_ref[...] = jnp.zeros_like(o_ref)
        o_ref[...] += _dot_tn(a_ref[...], b_ref[...])

    return _pcall(
        body, name=name, grid=(k // bk, n // bn, t // bt),
        in_specs=[pl.BlockSpec((bt, bk), lambda i, j, s: (s, i)),
                  pl.BlockSpec((bt, bn), lambda i, j, s: (s, j))],
        out_specs=pl.BlockSpec((bk, bn), lambda i, j, s: (i, j)),
        out_shape=jax.ShapeDtypeStruct((k, n), F32),
    )(a, b)


def _mod_row(mods_ref, lat, idx):
    return jnp.where(lat, mods_ref[idx + 6:idx + 7, :], mods_ref[idx:idx + 1, :])


def _row_fwd(x, mods, *, y=None, gate=None, g=None, shift=None, scale=None, name):
    t, d = x.shape
    has_y, has_n = y is not None, g is not None

    def body(*refs):
        refs = list(refs)
        x_ref, mods_ref = refs[0], refs[1]
        pos = 2
        if has_y:
            y_ref = refs[pos]; pos += 1
        if has_n:
            g_ref = refs[pos]; pos += 1
        outs = refs[pos:]
        lat = pl.program_id(0) > 0
        x1 = x_ref[...]
        o = 0
        if has_y:
            x1 = x1 + _mod_row(mods_ref, lat, gate) * y_ref[...]
            outs[o][...] = x1; o += 1
        if has_n:
            rs = lax.rsqrt(jnp.mean(x1 * x1, axis=-1, keepdims=True) + EPS)
            hn = x1 * rs * g_ref[...]
            h = hn * (1.0 + _mod_row(mods_ref, lat, scale)) + _mod_row(mods_ref, lat, shift)
            outs[o][...] = h.astype(BF16)

    row = pl.BlockSpec((TM, d), lambda i: (i, 0))
    ins, specs = [x, mods], [row, pl.BlockSpec(mods.shape, lambda i: (0, 0))]
    if has_y:
        ins.append(y); specs.append(row)
    if has_n:
        ins.append(g.reshape(1, d)); specs.append(pl.BlockSpec((1, d), lambda i: (0, 0)))
    out_shape, out_specs = [], []
    if has_y:
        out_shape.append(jax.ShapeDtypeStruct((t, d), F32)); out_specs.append(row)
    if has_n:
        out_shape.append(jax.ShapeDtypeStruct((t, d), BF16)); out_specs.append(row)
    res = _pcall(body, name=name, grid=(t // TM,), in_specs=specs, out_specs=out_specs,
                 out_shape=out_shape)(*ins)
    return res


def _acc_row(ref, r, val):
    ref[r:r + 1, :] += val


def _row_final(x, z, mods, target, *, gate, name):
    t, d = x.shape

    def body(x_ref, mods_ref, z_ref, t_ref, loss_ref, dx_ref, dz_ref, sums_ref):
        i = pl.program_id(0)
        lat = i > 0

        @pl.when(i == 0)
        def _():
            loss_ref[...] = jnp.zeros_like(loss_ref)
            sums_ref[...] = jnp.zeros_like(sums_ref)

        gt = _mod_row(mods_ref, lat, gate)
        zz = z_ref[...]
        yv = x_ref[...] + gt * zz
        keep = jnp.where(lat, 1.0, 0.0).astype(F32)
        diff = (yv - t_ref[...]) * keep
        part = jnp.sum(jnp.sum(diff * diff, axis=0, keepdims=True), axis=1, keepdims=True)
        loss_ref[...] += part * (0.5 / d)
        dy = diff * (1.0 / d)
        dx_ref[...] = dy
        dz_ref[...] = (gt * dy).astype(BF16)
        _acc_row(sums_ref, 6, jnp.sum(dy * zz, axis=0, keepdims=True))

    row = pl.BlockSpec((TM, d), lambda i: (i, 0))
    return _pcall(
        body, name=name, grid=(t // TM,),
        in_specs=[row, pl.BlockSpec(mods.shape, lambda i: (0, 0)), row,
                  pl.BlockSpec((TM, d), lambda i: (jnp.maximum(i - 1, 0), 0))],
        out_specs=[pl.BlockSpec((8, 128), lambda i: (0, 0)), row, row,
                   pl.BlockSpec((8, d), lambda i: (0, 0))],
        out_shape=[jax.ShapeDtypeStruct((8, 128), F32), jax.ShapeDtypeStruct((t, d), F32),
                   jax.ShapeDtypeStruct((t, d), BF16), jax.ShapeDtypeStruct((8, d), F32)],
    )(x, mods, z, target)


def _row_bwd(xn, dxo, dh, mods, g, *, shift, scale, y=None, gate=None, name):
    t, d = xn.shape
    has_y = y is not None

    def body(*refs):
        refs = list(refs)
        x_ref, dxo_ref, dh_ref, mods_ref, g_ref = refs[:5]
        pos = 5
        if has_y:
            y_ref = refs[pos]; pos += 1
        dx_ref = refs[pos]; pos += 1
        if has_y:
            dy_ref = refs[pos]; pos += 1
        sums_ref = refs[pos]
        i = pl.program_id(0)
        lat = i > 0

        @pl.when(i == 0)
        def _():
            sums_ref[...] = jnp.zeros_like(sums_ref)

        x1 = x_ref[...]
        gv = g_ref[...]
        rs = lax.rsqrt(jnp.mean(x1 * x1, axis=-1, keepdims=True) + EPS)
        xh = x1 * rs
        dhv = dh_ref[...]
        dn = dhv * (1.0 + _mod_row(mods_ref, lat, scale))
        dxh = dn * gv
        dx = dxo_ref[...] + rs * (dxh - xh * jnp.mean(dxh * xh, axis=-1, keepdims=True))
        dx_ref[...] = dx
        vals = [jnp.sum(dhv, axis=0, keepdims=True),
                jnp.sum(dhv * (xh * gv), axis=0, keepdims=True),
                None,
                jnp.sum(dn * xh, axis=0, keepdims=True)]
        if has_y:
            dy_ref[...] = (_mod_row(mods_ref, lat, gate) * dx).astype(BF16)
            vals[2] = jnp.sum(dx * y_ref[...], axis=0, keepdims=True)

        @pl.when(i == 0)
        def _():
            for r, v in enumerate(vals):
                if v is not None:
                    _acc_row(sums_ref, r, v)

        @pl.when(i > 0)
        def _():
            for r, v in enumerate(vals):
                if v is not None:
                    _acc_row(sums_ref, 4 + r, v)

    row = pl.BlockSpec((TM, d), lambda i: (i, 0))
    ins = [xn, dxo, dh, mods, g.reshape(1, d)]
    specs = [row, row, row, pl.BlockSpec(mods.shape, lambda i: (0, 0)), pl.BlockSpec((1, d), lambda i: (0, 0))]
    out_shape, out_specs = [jax.ShapeDtypeStruct((t, d), F32)], [row]
    if has_y:
        ins.append(y); specs.append(row)
        out_shape.append(jax.ShapeDtypeStruct((t, d), BF16)); out_specs.append(row)
    out_shape.append(jax.ShapeDtypeStruct((8, d), F32))
    out_specs.append(pl.BlockSpec((8, d), lambda i: (0, 0)))
    return _pcall(body, name=name, grid=(t // TM,), in_specs=specs, out_specs=out_specs,
                  out_shape=out_shape)(*ins)


def _swiglu_fwd(u, *, name):
    t, n2 = u.shape
    n = n2 // 2

    def body(g_ref, u_ref, o_ref):
        gv = g_ref[...]
        o_ref[...] = (gv * _sigmoid(gv) * u_ref[...]).astype(BF16)

    return _pcall(
        body, name=name, grid=(t // TM,),
        in_specs=[pl.BlockSpec((TM, n), lambda i: (i, 0)), pl.BlockSpec((TM, n), lambda i: (i, 1))],
        out_specs=pl.BlockSpec((TM, n), lambda i: (i, 0)),
        out_shape=jax.ShapeDtypeStruct((t, n), BF16),
    )(u, u)


def _swiglu_bwd(u, da, *, name):
    t, n2 = u.shape
    n = n2 // 2

    def body(g_ref, u_ref, da_ref, o_ref):
        gv, uv, dav = g_ref[...], u_ref[...], da_ref[...]
        s = _sigmoid(gv)
        o_ref[:, 0:n] = (dav * uv * (s * (1.0 + gv * (1.0 - s)))).astype(BF16)
        o_ref[:, n:n2] = (dav * gv * s).astype(BF16)

    return _pcall(
        body, name=name, grid=(t // TM,),
        in_specs=[pl.BlockSpec((TM, n), lambda i: (i, 0)), pl.BlockSpec((TM, n), lambda i: (i, 1)),
                  pl.BlockSpec((TM, n), lambda i: (i, 0))],
        out_specs=pl.BlockSpec((TM, n2), lambda i: (i, 0)),
        out_shape=jax.ShapeDtypeStruct((t, n2), BF16),
    )(u, u, da)


def _lane(shape):
    return _iota(shape, len(shape) - 1)


def _pair_norm(x, g):
    lo = _lane(x.shape) < 64
    x2 = x * x
    s_lo = jnp.sum(jnp.where(lo, x2, 0.0), axis=-1, keepdims=True)
    s_hi = jnp.sum(jnp.where(lo, 0.0, x2), axis=-1, keepdims=True)
    rs = lax.rsqrt(jnp.where(lo, s_lo, s_hi) * (1.0 / 64) + EPS)
    return x * rs, rs


def _pair_mean(v):
    lo = _lane(v.shape) < 64
    s_lo = jnp.sum(jnp.where(lo, v, 0.0), axis=-1, keepdims=True)
    s_hi = jnp.sum(jnp.where(lo, 0.0, v), axis=-1, keepdims=True)
    return jnp.where(lo, s_lo, s_hi) * (1.0 / 64)


def _rot64(x):
    r1 = pltpu.roll(x, 32, 1)
    r2 = pltpu.roll(x, 96, 1)
    even = ((_lane(x.shape) >> 5) & 1) == 0
    return jnp.where(even, -r2, r1)


def _rope64(x, cos, sin):
    return x * cos + _rot64(x) * sin


def _rope64_t(d, cos, sin):
    return d * cos - _rot64(d * sin)


def _kprep_fwd(p, gk, cos, sin, *, name):
    t = p.shape[0]

    def body(k_ref, g_ref, c_ref, s_ref, o_ref):
        xh, _ = _pair_norm(k_ref[...], None)
        o_ref[...] = _rope64(xh * g_ref[...], c_ref[...], s_ref[...])

    blk = pl.BlockSpec((TM, 128), lambda i: (i, 0))
    return _pcall(
        body, name=name, grid=(t // TM,),
        in_specs=[pl.BlockSpec((TM, 128), lambda i: (i, 4)), pl.BlockSpec((1, 128), lambda i: (0, 0)), blk, blk],
        out_specs=blk, out_shape=jax.ShapeDtypeStruct((t, 128), F32),
    )(p, gk, cos, sin)


def _kprep_bwd(p, gk, cos, sin, dkp, dv, *, name):
    t = p.shape[0]

    def body(k_ref, g_ref, c_ref, s_ref, dkp_ref, dv_ref, o_ref, dg_ref):
        @pl.when(pl.program_id(0) == 0)
        def _():
            dg_ref[...] = jnp.zeros_like(dg_ref)
        xh, rs = _pair_norm(k_ref[...], None)
        dn = _rope64_t(dkp_ref[...], c_ref[...], s_ref[...])
        _acc_row(dg_ref, 0, jnp.sum(dn * xh, axis=0, keepdims=True))
        dxh = dn * g_ref[...]
        o_ref[:, 0:128] = (rs * (dxh - xh * _pair_mean(dxh * xh))).astype(BF16)
        o_ref[:, 128:256] = dv_ref[...].astype(BF16)

    blk = pl.BlockSpec((TM, 128), lambda i: (i, 0))
    return _pcall(
        body, name=name, grid=(t // TM,),
        in_specs=[pl.BlockSpec((TM, 128), lambda i: (i, 4)), pl.BlockSpec((1, 128), lambda i: (0, 0)), blk, blk, blk, blk],
        out_specs=[pl.BlockSpec((TM, 256), lambda i: (i, 0)), pl.BlockSpec((8, 128), lambda i: (0, 0))],
        out_shape=[jax.ShapeDtypeStruct((t, 256), BF16), jax.ShapeDtypeStruct((8, 128), F32)],
    )(p, gk, cos, sin, dkp, dv)


def _attn_common(i, t, lc, kp_ref, v_ref):
    span = QB + 2 * WINDOW
    start = pl.multiple_of(jnp.clip((i - 1) * QB, lc, t - span), QB)
    kall = jnp.concatenate([kp_ref[0:lc, :], kp_ref[pl.ds(start, span), :]], axis=0)
    vall = jnp.concatenate([v_ref[0:lc, :], v_ref[pl.ds(start, span), :]], axis=0)
    nk = lc + span
    col = _iota((QB, nk), 1)
    krow = jnp.where(col < lc, col, start + col - lc)
    qrow = i * QB + _iota((QB, nk), 0)
    valid = (col < lc) | ((qrow >= lc) & (krow >= lc) & (jnp.abs(krow - qrow) <= WINDOW))
    lo = _lane(kall.shape) < 64
    kroll, vroll = pltpu.roll(kall, 64, 1), pltpu.roll(vall, 64, 1)
    zero = jnp.zeros_like(kall)
    kvar = [[_bf(jnp.where(lo, kall, zero)), _bf(jnp.where(lo, zero, kroll))],
            [_bf(jnp.where(lo, kroll, zero)), _bf(jnp.where(lo, zero, kall))]]
    vvar = [[_bf(jnp.where(lo, vall, zero)), _bf(jnp.where(lo, zero, vroll))],
            [_bf(jnp.where(lo, vroll, zero)), _bf(jnp.where(lo, zero, vall))]]
    return start, valid, kvar, vvar


def _softmax_sink(s, valid, snk):
    s = jnp.where(valid, s, NEG)
    m = jnp.maximum(jnp.max(s, axis=-1, keepdims=True), snk)
    e = jnp.exp(s - m)
    es = jnp.exp(snk - m)
    inv = 1.0 / (jnp.sum(e, axis=-1, keepdims=True) + es)
    return e * inv, es * inv


def _attn_fwd(p, kp, gq, sink, cos, sin, *, lc, name):
    t = p.shape[0]
    scale = 64 ** -0.5

    def body(q_ref, kp_ref, v_ref, g_ref, sink_ref, c_ref, s_ref, o_ref):
        i = pl.program_id(0)
        _, valid, kvar, vvar = _attn_common(i, t, lc, kp_ref, v_ref)
        cosv, sinv, gv = c_ref[...], s_ref[...], g_ref[...]
        for j in range(4):
            xh, _ = _pair_norm(q_ref[:, 128 * j:128 * j + 128], None)
            q2 = _bf(_rope64(xh * gv, cosv, sinv))
            acc = jnp.zeros((QB, 128), F32)
            for half in range(2):
                s = _dot_nt(q2, kvar[j // 2][half]) * scale
                pr, _ = _softmax_sink(s, valid, sink_ref[2 * j + half])
                acc = acc + _dot(pr, vvar[j // 2][half])
            o_ref[:, 128 * j:128 * j + 128] = acc.astype(BF16)

    qblk = pl.BlockSpec((QB, 128), lambda i: (i, 0))
    return _pcall(
        body, name=name, grid=(t // QB,),
        in_specs=[pl.BlockSpec((QB, 512), lambda i: (i, 0)),
                  pl.BlockSpec((t, 128), lambda i: (0, 0)),
                  pl.BlockSpec((t, 128), lambda i: (0, 5)),
                  pl.BlockSpec((1, 128), lambda i: (0, 0)),
                  pl.BlockSpec(memory_space=pltpu.SMEM), qblk, qblk],
        out_specs=pl.BlockSpec((QB, 512), lambda i: (i, 0)),
        out_shape=jax.ShapeDtypeStruct((t, 512), BF16),
    )(p, kp, p, gq, sink, cos, sin)


def _attn_bwd(p, kp, gq, sink, cos, sin, dmix, *, lc, name):
    t = p.shape[0]
    scale = 64 ** -0.5
    span = QB + 2 * WINDOW

    def body(q_ref, kp_ref, v_ref, g_ref, sink_ref, c_ref, s_ref, do_ref,
             dq_ref, dk_ref, dv_ref, dg_ref, dsink_ref):
        i = pl.program_id(0)

        @pl.when(i == 0)
        def _():
            dk_ref[...] = jnp.zeros_like(dk_ref)
            dv_ref[...] = jnp.zeros_like(dv_ref)
            dg_ref[...] = jnp.zeros_like(dg_ref)
            dsink_ref[...] = jnp.zeros_like(dsink_ref)

        start, valid, kvar, vvar = _attn_common(i, t, lc, kp_ref, v_ref)
        cosv, sinv, gv = c_ref[...], s_ref[...], g_ref[...]
        nk = lc + span
        lo = _lane((nk, 128)) < 64
        dk_all = jnp.zeros((nk, 128), F32)
        dv_all = jnp.zeros((nk, 128), F32)
        for j in range(4):
            kvh = j // 2
            xh, rs = _pair_norm(q_ref[:, 128 * j:128 * j + 128], None)
            q2 = _bf(_rope64(xh * gv, cosv, sinv))
            do2 = _bf(do_ref[:, 128 * j:128 * j + 128])
            dq2 = jnp.zeros((QB, 128), F32)
            for half in range(2):
                s = _dot_nt(q2, kvar[kvh][half]) * scale
                pr, ps = _softmax_sink(s, valid, sink_ref[2 * j + half])
                dp = _dot_nt(do2, vvar[kvh][half])
                delta = jnp.sum(pr * dp, axis=-1, keepdims=True)
                ds = pr * (dp - delta) * scale
                dsk = jnp.sum(jnp.sum(-ps * delta, axis=0, keepdims=True), axis=1, keepdims=True)
                _acc_row(dsink_ref, 2 * j + half, jnp.broadcast_to(dsk, (1, 128)))
                dq2 = dq2 + _dot(ds, kvar[kvh][half])
                gk_ = _dot_tn(ds, q2)
                gv_ = _dot_tn(pr, do2)
                if half == 0:
                    gk_, gv_ = jnp.where(lo, gk_, 0.0), jnp.where(lo, gv_, 0.0)
                else:
                    gk_, gv_ = jnp.where(lo, 0.0, gk_), jnp.where(lo, 0.0, gv_)
                if half != kvh:
                    gk_, gv_ = pltpu.roll(gk_, 64, 1), pltpu.roll(gv_, 64, 1)
                dk_all = dk_all + gk_
                dv_all = dv_all + gv_
            dn = _rope64_t(dq2, cosv, sinv)
            _acc_row(dg_ref, 0, jnp.sum(dn * xh, axis=0, keepdims=True))
            dxh = dn * gv
            dq_ref[:, 128 * j:128 * j + 128] = (rs * (dxh - xh * _pair_mean(dxh * xh))).astype(BF16)
        dk_ref[0:lc, :] += dk_all[0:lc]
        dv_ref[0:lc, :] += dv_all[0:lc]
        dk_ref[pl.ds(start, span), :] += dk_all[lc:nk]
        dv_ref[pl.ds(start, span), :] += dv_all[lc:nk]

    qblk = pl.BlockSpec((QB, 128), lambda i: (i, 0))
    full = pl.BlockSpec((t, 128), lambda i: (0, 0))
    small = pl.BlockSpec((8, 128), lambda i: (0, 0))
    return _pcall(
        body, name=name, grid=(t // QB,),
        in_specs=[pl.BlockSpec((QB, 512), lambda i: (i, 0)), full,
                  pl.BlockSpec((t, 128), lambda i: (0, 5)),
                  pl.BlockSpec((1, 128), lambda i: (0, 0)),
                  pl.BlockSpec(memory_space=pltpu.SMEM), qblk, qblk,
                  pl.BlockSpec((QB, 512), lambda i: (i, 0))],
        out_specs=[pl.BlockSpec((QB, 512), lambda i: (i, 0)), full, full, small, small],
        out_shape=[jax.ShapeDtypeStruct((t, 512), BF16), jax.ShapeDtypeStruct((t, 128), F32),
                   jax.ShapeDtypeStruct((t, 128), F32), jax.ShapeDtypeStruct((8, 128), F32),
                   jax.ShapeDtypeStruct((8, 128), F32)],
    )(p, kp, p, gq, sink, cos, sin, dmix)


def _tri(rev):
    r, c = _iota((CHUNK, CHUNK), 0), _iota((CHUNK, CHUNK), 1)
    return (c >= r) if rev else (c <= r)


def _blk_map(nb, rev, backward):
    if not rev:
        return (lambda n: nb - 1 - n) if backward else (lambda n: n)
    if backward:
        return lambda n: jnp.where(n < nb - 1, n + 1, 0)
    return lambda n: jnp.where(n == 0, 0, nb - n)


def _chunk_order(rev, backward):
    order = list(range(TM // CHUNK))
    return order[::-1] if (rev != backward) else order


def _hgrn_gates(qraw, fraw, lb):
    sq = _sigmoid(qraw)
    sf = _sigmoid(fraw)
    f = lb + (1.0 - lb) * sf
    return qraw * sq, 1.0 - f, jnp.log(f), sq, sf, f


def _gla_terms(q, k, lf, rev):
    tri = _tri(rev)
    b = _dot_exact(tri.astype(F32), lf)
    mid, last = (CHUNK // 2 - 1, 0) if rev else (CHUNK // 2, CHUNK - 1)
    r, bl = b[mid:mid + 1, :], b[last:last + 1, :]
    eq, ek, ei, eki = jnp.exp(b - r), jnp.exp(r - b), jnp.exp(b), jnp.exp(bl - b)
    return tri, last, eq, ek, ei, eki, jnp.exp(bl)


def _hgrn_fwd(p, lb, *, rev, name):
    t = p.shape[0]
    nb, nc = t // TM, TM // CHUNK
    bmap = _blk_map(nb, rev, False)
    fcol = 14 if rev else 10

    def body(q_ref, f_ref, v_ref, lb_ref, o_ref, sh_ref, st):
        @pl.when(pl.program_id(1) == 0)
        def _():
            st[...] = jnp.zeros_like(st)
        for cc in _chunk_order(rev, False):
            rows = slice(cc * CHUNK, (cc + 1) * CHUNK)
            q, k, lf, _, _, _ = _hgrn_gates(q_ref[rows, :], f_ref[rows, :], lb_ref[...])
            v = v_ref[rows, :]
            tri, _, eq, ek, ei, eki, eb = _gla_terms(q, k, lf, rev)
            s0 = st[...]
            sh_ref[cc] = s0
            a = jnp.where(tri, _dot_nt(q * eq, k * ek), 0.0)
            o_ref[rows, :] = _dot(a, v) + _dot_nt(q * ei, s0)
            st[...] = s0 * eb + _dot_tn(v, k * eki)

    def col(c0):
        return pl.BlockSpec((TM, 128), lambda h, n: (bmap(n), c0 + h))

    return _pcall(
        body, name=name, grid=(4, nb),
        in_specs=[col(6), col(fcol), col(18), pl.BlockSpec((1, 128), lambda h, n: (0, h))],
        out_specs=[pl.BlockSpec((TM, 128), lambda h, n: (bmap(n), h)),
                   pl.BlockSpec((None, nc, 128, 128), lambda h, n: (h, bmap(n), 0, 0))],
        out_shape=[jax.ShapeDtypeStruct((t, 512), F32), jax.ShapeDtypeStruct((4, t // CHUNK, 128, 128), F32)],
        scratch_shapes=[pltpu.VMEM((128, 128), F32)],
    )(p, p, p, lb)


def _hgrn_bwd(p, lb, sh, do, prev, *, rev, name):
    t = p.shape[0]
    nb, nc = t // TM, TM // CHUNK
    bmap = _blk_map(nb, rev, True)
    fcol = 14 if rev else 10
    has_prev = prev is not None
    odt = BF16 if has_prev else F32

    def body(*refs):
        refs = list(refs)
        q_ref, f_ref, v_ref, lb_ref, sh_ref, do_ref = refs[:6]
        pos = 6
        if has_prev:
            pq_ref, pv_ref = refs[6], refs[7]
            pos = 8
        dq_ref, df_ref, dv_ref, dlb_ref, dst = refs[pos:pos + 5]

        @pl.when(pl.program_id(1) == 0)
        def _():
            dst[...] = jnp.zeros_like(dst)
            dlb_ref[...] = jnp.zeros_like(dlb_ref)

        lbv = lb_ref[...]
        for cc in _chunk_order(rev, True):
            rows = slice(cc * CHUNK, (cc + 1) * CHUNK)
            qraw, fraw = q_ref[rows, :], f_ref[rows, :]
            q, k, lf, sq, sf, f = _hgrn_gates(qraw, fraw, lbv)
            v = v_ref[rows, :]
            dov = do_ref[rows, :]
            tri, last, eq, ek, ei, eki, eb = _gla_terms(q, k, lf, rev)
            s0 = sh_ref[cc]
            dsc = dst[...]
            qe, ke, qi, ki = q * eq, k * ek, q * ei, k * eki
            a = jnp.where(tri, _dot_nt(qe, ke), 0.0)
            da = jnp.where(tri, _dot_nt(dov, v), 0.0)
            dv = _dot_tn(a, dov) + _dot_nt(ki, dsc)
            dqe, dke = _dot(da, ke), _dot_tn(da, qe)
            dqi, dki = _dot(dov, s0), _dot(v, dsc)
            dst[...] = dsc * eb + _dot_tn(dov, qi)
            dq = dqe * eq + dqi * ei
            dk = dke * ek + dki * eki
            db = dqe * qe - dke * ke + dqi * qi - dki * ki
            dbl = jnp.sum(dki * ki, axis=0, keepdims=True) + jnp.sum(dsc * s0, axis=0, keepdims=True) * eb
            db = db + jnp.where(_iota(db.shape, 0) == last, dbl, 0.0)
            dlf = _dot_exact(_tri(not rev).astype(F32), db)
            dqr = dq * (sq * (1.0 + qraw * (1.0 - sq)))
            dfv = dlf / f - dk
            dfr = dfv * (1.0 - lbv) * (sf * (1.0 - sf))
            dlb_ref[...] += jnp.sum(dfv * (1.0 - sf), axis=0, keepdims=True)
            if has_prev:
                dqr = dqr + pq_ref[rows, :]
                dv = dv + pv_ref[rows, :]
            dq_ref[rows, :] = dqr.astype(odt)
            df_ref[rows, :] = dfr.astype(odt)
            dv_ref[rows, :] = dv.astype(odt)

    def col(c0):
        return pl.BlockSpec((TM, 128), lambda h, n: (bmap(n), c0 + h))

    oblk = pl.BlockSpec((TM, 128), lambda h, n: (bmap(n), h))
    ins = [p, p, p, lb, sh, do]
    specs = [col(6), col(fcol), col(18), pl.BlockSpec((1, 128), lambda h, n: (0, h)),
             pl.BlockSpec((None, nc, 128, 128), lambda h, n: (h, bmap(n), 0, 0)), oblk]
    if has_prev:
        ins += list(prev); specs += [oblk, oblk]
    return _pcall(
        body, name=name, grid=(4, nb), in_specs=specs,
        out_specs=[oblk, oblk, oblk, pl.BlockSpec((1, 128), lambda h, n: (0, h))],
        out_shape=[jax.ShapeDtypeStruct((t, 512), odt)] * 3 + [jax.ShapeDtypeStruct((1, 512), F32)],
        scratch_shapes=[pltpu.VMEM((128, 128), F32)],
    )(*ins)


def _rope256(x, cos, sin):
    x1, x2 = x[:, 0:128], x[:, 128:256]
    return jnp.concatenate([x1 * cos - x2 * sin, x2 * cos + x1 * sin], axis=-1)


def _rope256_t(d, cos, sin):
    d1, d2 = d[:, 0:128], d[:, 128:256]
    return jnp.concatenate([d1 * cos + d2 * sin, d2 * cos - d1 * sin], axis=-1)


def _ret_terms(lg, rev):
    r, c = _iota((CHUNK, CHUNK), 0), _iota((CHUNK, CHUNK), 1)
    rel = ((c - r) if rev else (r - c)).astype(F32)
    dmat = jnp.where(rel >= 0, jnp.exp(lg[:, 0:CHUNK] * jnp.maximum(rel, 0.0)), 0.0)
    pos = _iota((CHUNK, 1), 0).astype(F32)
    cnt = (CHUNK - pos) if rev else (pos + 1.0)
    ei = jnp.exp(lg * cnt)
    eki = jnp.exp(lg * (CHUNK - cnt))
    eb = jnp.exp(lg * float(CHUNK))
    return dmat, ei, eki, eb


RET_DK, RET_DV, RET_H = 256, 512, 4
RET_KSCALE = RET_DK ** -0.5


def _ret_fwd(p, lgt, cos, sin, *, rev, name):
    t = p.shape[0]
    nb, nc = t // TM, TM // CHUNK
    bmap = _blk_map(nb, rev, False)

    def body(q_ref, k_ref, v_ref, lg_ref, c_ref, s_ref, o_ref, sh_ref, st):
        @pl.when(pl.program_id(1) == 0)
        def _():
            st[...] = jnp.zeros_like(st)
        dmat, ei, eki, eb = _ret_terms(lg_ref[...], rev)
        for cc in _chunk_order(rev, False):
            rows = slice(cc * CHUNK, (cc + 1) * CHUNK)
            cosv, sinv = c_ref[rows, :], s_ref[rows, :]
            q = _rope256(q_ref[rows, :], cosv, sinv)
            k = _rope256(k_ref[rows, :], cosv, sinv) * RET_KSCALE
            v = v_ref[rows, :]
            s0 = st[...]
            sh_ref[cc] = s0.astype(BF16)
            a = _dot_nt(q, k) * dmat
            o_ref[rows, :] = _dot(a, v) + _dot_nt(q * ei, s0)
            st[...] = s0 * eb + _dot_tn(v, k * eki)

    tab = pl.BlockSpec((TM, 128), lambda h, n: (bmap(n), 0))
    return _pcall(
        body, name=name, grid=(RET_H, nb),
        in_specs=[pl.BlockSpec((TM, RET_DK), lambda h, n: (bmap(n), h)),
                  pl.BlockSpec((TM, RET_DK), lambda h, n: (bmap(n), 4 + h)),
                  pl.BlockSpec((TM, RET_DV), lambda h, n: (bmap(n), 4 + h)),
                  pl.BlockSpec((None, 1, RET_DK), lambda h, n: (h, 0, 0)), tab, tab],
        out_specs=[pl.BlockSpec((TM, RET_DV), lambda h, n: (bmap(n), h)),
                   pl.BlockSpec((None, nc, RET_DV, RET_DK), lambda h, n: (h, bmap(n), 0, 0))],
        out_shape=[jax.ShapeDtypeStruct((t, RET_H * RET_DV), F32),
                   jax.ShapeDtypeStruct((RET_H, t // CHUNK, RET_DV, RET_DK), BF16)],
        scratch_shapes=[pltpu.VMEM((RET_DV, RET_DK), F32)],
    )(p, p, p, lgt, cos, sin)


def _ret_bwd(p, lgt, cos, sin, sh, do, prev, *, rev, name):
    t = p.shape[0]
    nb, nc = t // TM, TM // CHUNK
    bmap = _blk_map(nb, rev, True)
    has_prev = prev is not None
    odt = BF16 if has_prev else F32

    def body(*refs):
        refs = list(refs)
        q_ref, k_ref, v_ref, lg_ref, c_ref, s_ref, sh_ref, do_ref = refs[:8]
        pos = 8
        if has_prev:
            pq_ref, pk_ref, pv_ref = refs[8:11]
            pos = 11
        dq_ref, dk_ref, dv_ref, dst = refs[pos:pos + 4]

        @pl.when(pl.program_id(1) == 0)
        def _():
            dst[...] = jnp.zeros_like(dst)

        dmat, ei, eki, eb = _ret_terms(lg_ref[...], rev)
        for cc in _chunk_order(rev, True):
            rows = slice(cc * CHUNK, (cc + 1) * CHUNK)
            cosv, sinv = c_ref[rows, :], s_ref[rows, :]
            q = _rope256(q_ref[rows, :], cosv, sinv)
            k = _rope256(k_ref[rows, :], cosv, sinv) * RET_KSCALE
            v = v_ref[rows, :]
            dov = do_ref[rows, :]
            s0 = sh_ref[cc]
            dsc = dst[...]
            qi, ki = q * ei, k * eki
            a = _dot_nt(q, k) * dmat
            da = _dot_nt(dov, v) * dmat
            dv = _dot_tn(a, dov) + _dot_nt(ki, dsc)
            dqs = _dot(da, k) + _dot(dov, s0) * ei
            dks = _dot_tn(da, q) + _dot(v, dsc) * eki
            dst[...] = dsc * eb + _dot_tn(dov, qi)
            dq = _rope256_t(dqs, cosv, sinv)
            dk = _rope256_t(dks * RET_KSCALE, cosv, sinv)
            if has_prev:
                dq = dq + pq_ref[rows, :]
                dk = dk + pk_ref[rows, :]
                dv = dv + pv_ref[rows, :]
            dq_ref[rows, :] = dq.astype(odt)
            dk_ref[rows, :] = dk.astype(odt)
            dv_ref[rows, :] = dv.astype(odt)

    tab = pl.BlockSpec((TM, 128), lambda h, n: (bmap(n), 0))
    qblk = pl.BlockSpec((TM, RET_DK), lambda h, n: (bmap(n), h))
    vblk = pl.BlockSpec((TM, RET_DV), lambda h, n: (bmap(n), h))
    ins = [p, p, p, lgt, cos, sin, sh, do]
    specs = [qblk, pl.BlockSpec((TM, RET_DK), lambda h, n: (bmap(n), 4 + h)),
             pl.BlockSpec((TM, RET_DV), lambda h, n: (bmap(n), 4 + h)),
             pl.BlockSpec((None, 1, RET_DK), lambda h, n: (h, 0, 0)), tab, tab,
             pl.BlockSpec((None, nc, RET_DV, RET_DK), lambda h, n: (h, bmap(n), 0, 0)), vblk]
    if has_prev:
        ins += list(prev); specs += [qblk, qblk, vblk]
    return _pcall(
        body, name=name, grid=(RET_H, nb), in_specs=specs,
        out_specs=[qblk, qblk, vblk],
        out_shape=[jax.ShapeDtypeStruct((t, RET_H * RET_DK), odt), jax.ShapeDtypeStruct((t, RET_H * RET_DK), odt),
                   jax.ShapeDtypeStruct((t, RET_H * RET_DV), odt)],
        scratch_shapes=[pltpu.VMEM((RET_DV, RET_DK), F32)],
    )(*ins)


def _headnorm_fwd(ofw, obw, p, gain, *, dv, gcol, name):
    t, w = ofw.shape
    nh = w // dv
    has_gain = gain is not None

    def body(*refs):
        a_ref, b_ref, g_ref = refs[:3]
        gain_ref = refs[3] if has_gain else None
        o_ref = refs[-1]
        o = a_ref[...] + b_ref[...]
        n = o * lax.rsqrt(jnp.mean(o * o, axis=-1, keepdims=True) + EPS)
        if has_gain:
            n = n * gain_ref[...]
        gv = g_ref[...]
        o_ref[...] = (n * (gv * _sigmoid(gv))).astype(BF16)

    blk = pl.BlockSpec((TM, dv), lambda i, h: (i, h))
    ins, specs = [ofw, obw, p], [blk, blk, pl.BlockSpec((TM, dv), lambda i, h: (i, gcol + h))]
    if has_gain:
        ins.append(gain); specs.append(pl.BlockSpec((1, dv), lambda i, h: (0, 0)))
    return _pcall(body, name=name, grid=(t // TM, nh), in_specs=specs, out_specs=blk,
                  out_shape=jax.ShapeDtypeStruct((t, w), BF16))(*ins)


def _headnorm_bwd(ofw, obw, p, gain, dmix, *, dv, gcol, mcol, name):
    t, w = ofw.shape
    nh = w // dv
    has_gain = gain is not None

    def body(*refs):
        a_ref, b_ref, g_ref, dm_ref = refs[:4]
        gain_ref = refs[4] if has_gain else None
        do_ref, dg_ref, dgain_ref = refs[-3:]

        @pl.when((pl.program_id(0) == 0) & (pl.program_id(1) == 0))
        def _():
            dgain_ref[...] = jnp.zeros_like(dgain_ref)

        o = a_ref[...] + b_ref[...]
        rs = lax.rsqrt(jnp.mean(o * o, axis=-1, keepdims=True) + EPS)
        xh = o * rs
        n = xh * gain_ref[...] if has_gain else xh
        gv = g_ref[...]
        sg = _sigmoid(gv)
        dy = dm_ref[...]
        dn = dy * (gv * sg)
        dg_ref[...] = (dy * n * (sg * (1.0 + gv * (1.0 - sg)))).astype(BF16)
        _acc_row(dgain_ref, 0, jnp.sum(dn * xh, axis=0, keepdims=True))
        dxh = dn * gain_ref[...] if has_gain else dn
        do_ref[...] = rs * (dxh - xh * jnp.mean(dxh * xh, axis=-1, keepdims=True))

    blk = pl.BlockSpec((TM, dv), lambda i, h: (i, h))
    ins = [ofw, obw, p, dmix]
    specs = [blk, blk, pl.BlockSpec((TM, dv), lambda i, h: (i, gcol + h)),
             pl.BlockSpec((TM, dv), lambda i, h: (i, mcol + h))]
    if has_gain:
        ins.append(gain); specs.append(pl.BlockSpec((1, dv), lambda i, h: (0, 0)))
    return _pcall(
        body, name=name, grid=(t // TM, nh), in_specs=specs,
        out_specs=[blk, blk, pl.BlockSpec((8, dv), lambda i, h: (0, 0))],
        out_shape=[jax.ShapeDtypeStruct((t, w), F32), jax.ShapeDtypeStruct((t, w), BF16),
                   jax.ShapeDtypeStruct((8, dv), F32)],
    )(*ins)


def _rope_tables(lc, l):
    tt = jnp.arange(l)
    row, colp = (tt // 64).astype(F32), (tt % 64).astype(F32)
    inv = 10000.0 ** (-jnp.arange(16, dtype=F32) / 16)
    ang = jnp.concatenate([row[:, None] * inv, colp[:, None] * inv], axis=-1)
    ang = jnp.concatenate([jnp.zeros((lc, 32), F32), ang], axis=0)
    acos, asin = jnp.tile(jnp.cos(ang), (1, 4)), jnp.tile(jnp.sin(ang), (1, 4))
    theta = 1.0 / (10000.0 ** jnp.linspace(0.0, 1.0, 128, dtype=F32))
    rang = jnp.arange(l, dtype=F32)[:, None] * theta
    rang = jnp.concatenate([jnp.zeros((lc, 128), F32), rang], axis=0)
    return acos, asin, jnp.cos(rang), jnp.sin(rang)


def _local_step(x0, target, mods, ng, w, small):
    t, d = x0.shape
    l = target.shape[0]
    lc = t - l
    acos, asin, rcos, rsin = _rope_tables(lc, l)
    lg_fw = jnp.log(1.0 - 2.0 ** (-5.0 - jnp.arange(RET_H, dtype=F32)))
    lgt_fw = jnp.broadcast_to(lg_fw[:, None, None], (RET_H, 1, RET_DK))
    lgt_bw = jnp.broadcast_to(lg_fw[::-1][:, None, None], (RET_H, 1, RET_DK))
    gq, gk, sink, gain, lb = small['gq'], small['gk'], small['sink'], small['gain'], small['lb']

    (h1,) = _row_fwd(x0, mods, g=ng[0], shift=0, scale=1, name='l0_norm1')
    p0 = _mm_nn(h1, w['even_in'], name='l0_in')
    kp = _kprep_fwd(p0, gk, acos, asin, name='l0_kprep')
    att = _attn_fwd(p0, kp, gq, sink, acos, asin, lc=lc, name='l0_attn')
    hof, hsf = _hgrn_fwd(p0, lb, rev=False, name='l0_hgrn_f')
    hob, hsb = _hgrn_fwd(p0, lb, rev=True, name='l0_hgrn_b')
    bmix = _headnorm_fwd(hof, hob, p0, gain, dv=128, gcol=22, name='l0_headnorm')
    mix0 = jnp.concatenate([att, bmix], axis=1)
    y0 = _mm_nn(mix0, w['even_out'], name='l0_out')
    x1, h2 = _row_fwd(x0, mods, y=y0, gate=2, g=ng[1], shift=3, scale=4, name='l0_norm2')
    u0 = _mm_nn(h2, w['ffn_in'], lead=0, name='ffn_in')
    a0 = _swiglu_fwd(u0, name='swiglu')
    z0 = _mm_nn(a0, w['ffn_out'], lead=0, name='ffn_out')
    x2, h3 = _row_fwd(x1, mods, y=z0, gate=5, g=ng[2], shift=12, scale=13, name='l1_norm1')
    p1 = _mm_nn(h3, w['odd_in'], name='l1_in')
    rof, rsf = _ret_fwd(p1, lgt_fw, rcos, rsin, rev=False, name='l1_ret_f')
    rob, rsb = _ret_fwd(p1, lgt_bw, rcos, rsin, rev=True, name='l1_ret_b')
    mix1 = _headnorm_fwd(rof, rob, p1, None, dv=RET_DV, gcol=8, name='l1_headnorm')
    y1 = _mm_nn(mix1, w['odd_out'], name='l1_out')
    x3, h4 = _row_fwd(x2, mods, y=y1, gate=14, g=ng[3], shift=15, scale=16, name='l1_norm2')
    u1 = _mm_nn(h4, w['ffn_in'], lead=1, name='ffn_in')
    a1 = _swiglu_fwd(u1, name='swiglu')
    z1 = _mm_nn(a1, w['ffn_out'], lead=1, name='ffn_out')
    loss, dx4, dz1, s_fin = _row_final(x3, z1, mods, target, gate=17, name='loss')

    da1 = _mm_nt(dz1, w['ffn_out'], lead=1, name='ffn_out_dx')
    g_ffn_out1 = _mm_tn(a1, dz1, name='ffn_out_dw')
    du1 = _swiglu_bwd(u1, da1, name='swiglu_bwd')
    dh4 = _mm_nt(du1, w['ffn_in'], lead=1, name='ffn_in_dx')
    g_ffn_in1 = _mm_tn(h4, du1, name='ffn_in_dw')
    dx3, dy1, s_l1n2 = _row_bwd(x3, dx4, dh4, mods, ng[3], shift=15, scale=16, y=y1, gate=14, name='l1_norm2_bwd')
    dmix1 = _mm_nt(dy1, w['odd_out'], name='l1_out_dx')
    g_odd_out = _mm_tn(mix1, dy1, name='l1_out_dw')
    rdo, rdg, _ = _headnorm_bwd(rof, rob, p1, None, dmix1, dv=RET_DV, gcol=8, mcol=0, name='l1_headnorm_bwd')
    part = _ret_bwd(p1, lgt_fw, rcos, rsin, rsf, rdo, None, rev=False, name='l1_ret_f_bwd')
    rdq, rdk, rdv = _ret_bwd(p1, lgt_bw, rcos, rsin, rsb, rdo, part, rev=True, name='l1_ret_b_bwd')
    dp1 = jnp.concatenate([rdq, rdk, rdv, rdg], axis=1)
    dh3 = _mm_nt(dp1, w['odd_in'], name='l1_in_dx')
    g_odd_in = _mm_tn(h3, dp1, name='l1_in_dw')
    dx2, dz0, s_l1n1 = _row_bwd(x2, dx3, dh3, mods, ng[2], shift=12, scale=13, y=z0, gate=5, name='l1_norm1_bwd')
    da0 = _mm_nt(dz0, w['ffn_out'], lead=0, name='ffn_out_dx')
    g_ffn_out0 = _mm_tn(a0, dz0, name='ffn_out_dw')
    du0 = _swiglu_bwd(u0, da0, name='swiglu_bwd')
    dh2 = _mm_nt(du0, w['ffn_in'], lead=0, name='ffn_in_dx')
    g_ffn_in0 = _mm_tn(h2, du0, name='ffn_in_dw')
    dx1, dy0, s_l0n2 = _row_bwd(x1, dx2, dh2, mods, ng[1], shift=3, scale=4, y=y0, gate=2, name='l0_norm2_bwd')
    dmix0 = _mm_nt(dy0, w['even_out'], name='l0_out_dx')
    g_even_out = _mm_tn(mix0, dy0, name='l0_out_dw')
    hdo, hdg, s_gain = _headnorm_bwd(hof, hob, p0, gain, dmix0, dv=128, gcol=22, mcol=4, name='l0_headnorm_bwd')
    hq, hff, hv, dlb_f = _hgrn_bwd(p0, lb, hsf, hdo, None, rev=False, name='l0_hgrn_f_bwd')
    hq, hfb, hv, dlb_b = _hgrn_bwd(p0, lb, hsb, hdo, (hq, hv), rev=True, name='l0_hgrn_b_bwd')
    adq, dkp, adv, s_gq, s_sink = _attn_bwd(p0, kp, gq, sink, acos, asin, dmix0, lc=lc, name='l0_attn_bwd')
    dkv, s_gk = _kprep_bwd(p0, gk, acos, asin, dkp, adv, name='l0_kprep_bwd')
    dp0 = jnp.concatenate([adq, dkv, hq, _bf(hff), hfb, hv, hdg], axis=1)
    dh1 = _mm_nt(dp0, w['even_in'], name='l0_in_dx')
    g_even_in = _mm_tn(h1, dp0, name='l0_in_dw')
    dx0, s_l0n1 = _row_bwd(x0, dx1, dh1, mods, ng[0], shift=0, scale=1, name='l0_norm1_bwd')

    grads = dict(ffn_in=jnp.stack([g_ffn_in0, g_ffn_in1]), ffn_out=jnp.stack([g_ffn_out0, g_ffn_out1]),
                 even_in=g_even_in, even_out=g_even_out, odd_in=g_odd_in, odd_out=g_odd_out)
    sums = dict(fin=s_fin, l1n2=s_l1n2, l1n1=s_l1n1, l0n2=s_l0n2, l0n1=s_l0n1, gain=s_gain, gq=s_gq, gk=s_gk,
                sink=s_sink, dlb=dlb_f + dlb_b)
    return loss, dx0, grads, sums


def _place():
    return lax.axis_index("x"), lax.axis_index("y"), lax.axis_index("c")


def _ag8(blk, *, name):
    r, c = blk.shape
    flips = [(dx, dy, dc) for dx in (0, 1) for dy in (0, 1) for dc in (0, 1) if (dx, dy, dc) != (0, 0, 0)]

    def body(x_ref, out_ref, send_sems, recv_sems, local_sem):
        ax, ay, ac = _place()
        me = 4 * ax + 2 * ay + ac
        mine = pltpu.make_async_copy(x_ref, out_ref.at[me], local_sem)
        mine.start()
        sent = []
        for k, (dx, dy, dc) in enumerate(flips):
            peer = (lax.rem(ax + dx, 2), lax.rem(ay + dy, 2), lax.rem(ac + dc, 2))
            cp = pltpu.make_async_remote_copy(src_ref=x_ref, dst_ref=out_ref.at[me], send_sem=send_sems.at[k],
                                              recv_sem=recv_sems.at[k], device_id=peer, device_id_type=MESH)
            cp.start()
            sent.append((cp, 4 * peer[0] + 2 * peer[1] + peer[2]))
        for k, (cp, pidx) in enumerate(sent):
            pltpu.make_async_remote_copy(src_ref=x_ref, dst_ref=out_ref.at[pidx], send_sem=send_sems.at[k],
                                         recv_sem=recv_sems.at[k], device_id=(ax, ay, ac),
                                         device_id_type=MESH).wait_recv()
        for cp, _ in sent:
            cp.wait_send()
        mine.wait()

    return _pcall(
        body, name=name,
        in_specs=[pl.BlockSpec(memory_space=pltpu.VMEM)],
        out_specs=pl.BlockSpec(memory_space=pltpu.VMEM),
        out_shape=jax.ShapeDtypeStruct((8, r, c), blk.dtype),
        scratch_shapes=[pltpu.SemaphoreType.DMA((7,)), pltpu.SemaphoreType.DMA((7,)), pltpu.SemaphoreType.DMA],
    )(blk)


def _chip_exchange(arrs, *, scatter, name):
    n = len(arrs)
    rel = [(1, 0), (0, 1), (1, 1)]

    def body(*refs):
        ins, outs = refs[:n], refs[n:2 * n]
        send_sems, recv_sems, local_sems = refs[2 * n:]
        ax, ay, ac = _place()
        s = 2 * ax + ay
        started, local = [], []
        for a in range(n):
            lcp = pltpu.make_async_copy(ins[a].at[s] if scatter else ins[a], outs[a].at[s], local_sems.at[a])
            lcp.start()
            local.append(lcp)
            for r, (dx, dy) in enumerate(rel):
                px, py = lax.rem(ax + dx, 2), lax.rem(ay + dy, 2)
                sp = 2 * px + py
                cp = pltpu.make_async_remote_copy(
                    src_ref=ins[a].at[sp] if scatter else ins[a], dst_ref=outs[a].at[s],
                    send_sem=send_sems.at[3 * a + r], recv_sem=recv_sems.at[3 * a + r],
                    device_id=(px, py, ac), device_id_type=MESH)
                cp.start()
                started.append((cp, a, r, sp))
        for cp, a, r, sp in started:
            pltpu.make_async_remote_copy(
                src_ref=ins[a].at[sp] if scatter else ins[a], dst_ref=outs[a].at[sp],
                send_sem=send_sems.at[3 * a + r], recv_sem=recv_sems.at[3 * a + r],
                device_id=(ax, ay, ac), device_id_type=MESH).wait_recv()
        for cp, _, _, _ in started:
            cp.wait_send()
        for lcp in local:
            lcp.wait()

    hbm = pl.BlockSpec(memory_space=pl.ANY)
    shapes = [jax.ShapeDtypeStruct(a.shape if scatter else (4,) + a.shape, a.dtype) for a in arrs]
    return _pcall(
        body, name=name, in_specs=[hbm] * n, out_specs=[hbm] * n, out_shape=shapes,
        scratch_shapes=[pltpu.SemaphoreType.DMA((3 * n,)), pltpu.SemaphoreType.DMA((3 * n,)),
                        pltpu.SemaphoreType.DMA((n,))],
    )(*arrs)


def _sib_swap(arrs, *, name):
    n = len(arrs)

    def body(*refs):
        ins, outs = refs[:n], refs[n:2 * n]
        send_sems, recv_sems = refs[2 * n:]
        ax, ay, ac = _place()
        cps = [pltpu.make_async_remote_copy(src_ref=ins[a], dst_ref=outs[a], send_sem=send_sems.at[a],
                                            recv_sem=recv_sems.at[a], device_id=(ax, ay, 1 - ac),
                                            device_id_type=MESH) for a in range(n)]
        for cp in cps:
            cp.start()
        for cp in cps:
            cp.wait_recv()
        for cp in cps:
            cp.wait_send()

    hbm = pl.BlockSpec(memory_space=pl.ANY)
    return _pcall(
        body, name=name, in_specs=[hbm] * n, out_specs=[hbm] * n,
        out_shape=[jax.ShapeDtypeStruct(a.shape, a.dtype) for a in arrs],
        scratch_shapes=[pltpu.SemaphoreType.DMA((n,)), pltpu.SemaphoreType.DMA((n,))],
    )(*arrs)


def _mod_fwd(cond_raw, mw, mb, *, name):
    _, d, n = mw.shape

    def body(c_ref, w_ref, b_ref, o_ref):
        cv = c_ref[...]
        o_ref[...] = _dot(cv * _sigmoid(cv), w_ref[...]) + b_ref[...]

    return _pcall(
        body, name=name, grid=(2,),
        in_specs=[pl.BlockSpec((16, d), lambda l: (0, 0)), pl.BlockSpec((None, d, n), lambda l: (l, 0, 0)),
                  pl.BlockSpec((None, 1, n), lambda l: (l, 0, 0))],
        out_specs=pl.BlockSpec((None, 16, n), lambda l: (l, 0, 0)),
        out_shape=jax.ShapeDtypeStruct((2, 16, n), F32),
    )(cond_raw, mw, mb)


def _mod_bwd(cond_raw, dms, mw, *, name):
    _, d, n = mw.shape

    def body(c_ref, dm_ref, w_ref, gw_ref, dc_ref):
        @pl.when(pl.program_id(0) == 0)
        def _():
            dc_ref[...] = jnp.zeros_like(dc_ref)
        cv = c_ref[...]
        gw_ref[...] = _dot_tn(cv * _sigmoid(cv), dm_ref[...])
        dc_ref[...] += _dot_nt(dm_ref[...], w_ref[...])

    return _pcall(
        body, name=name, grid=(2,),
        in_specs=[pl.BlockSpec((16, d), lambda l: (0, 0)), pl.BlockSpec((None, 16, n), lambda l: (l, 0, 0)),
                  pl.BlockSpec((None, d, n), lambda l: (l, 0, 0))],
        out_specs=[pl.BlockSpec((None, d, n), lambda l: (l, 0, 0)), pl.BlockSpec((16, d), lambda l: (0, 0))],
        out_shape=[jax.ShapeDtypeStruct((2, d, n), F32), jax.ShapeDtypeStruct((16, d), F32)],
    )(cond_raw, dms, mw)


def _lb_fwd(hgrn_lb, *, name):
    def body(a_ref, o_ref):
        a0, a1 = a_ref[0:1, :], a_ref[1:2, :]
        m = jnp.maximum(a0, a1)
        e0, e1 = jnp.exp(a0 - m), jnp.exp(a1 - m)
        o_ref[...] = e0 / (e0 + e1)

    return _pcall(body, name=name, out_shape=jax.ShapeDtypeStruct((1, hgrn_lb.shape[1]), F32))(hgrn_lb)


PACK_ROWS = 40


def _small_finalize(gath, lb_pad, *, name):
    d = gath.shape[2]

    def body(g_ref, lb_ref, small_ref, glb_ref, gmb_ref, dm_ref):
        tot = g_ref[0]
        for e in range(1, 8):
            tot = tot + g_ref[e]
        for k in range(4):
            small_ref[k:k + 1, :] = tot[24 + 2 * k:25 + 2 * k, :] + tot[25 + 2 * k:26 + 2 * k, :]
        for k, r in ((4, 32), (5, 33)):
            v = tot[r:r + 1, :]
            small_ref[k:k + 1, :] = v + pltpu.roll(v, d - 64, 1)
        small_ref[6:7, :] = tot[34:35, :]
        small_ref[7:8, :] = tot[36:37, :]
        lbv = lb_ref[...]
        g0 = (tot[35:36, :] + tot[37:38, :]) * lbv * (1.0 - lbv)
        glb_ref[...] = jnp.zeros_like(glb_ref)
        glb_ref[0:1, :] = g0
        glb_ref[1:2, :] = -g0
        dm_ref[...] = jnp.zeros_like(dm_ref)
        for l in range(2):
            for part in range(6):
                rc, rl = l * 12 + part, l * 12 + 6 + part
                gmb_ref[l * 6 + part:l * 6 + part + 1, :] = tot[rc:rc + 1, :] + tot[rl:rl + 1, :]
                for e in range(8):
                    dm_ref[l, part, e:e + 1, :] = g_ref[e, rl:rl + 1, :]
                dm_ref[l, part, 8:9, :] = tot[rc:rc + 1, :]

    return _pcall(
        body, name=name,
        out_shape=[jax.ShapeDtypeStruct((8, d), F32), jax.ShapeDtypeStruct((8, d), F32),
                   jax.ShapeDtypeStruct((12, d), F32), jax.ShapeDtypeStruct((2, 6, 16, d), F32)],
    )(gath, lb_pad)


def _cctx_grad(gath, c_ctx2, *, name):
    def body(g_ref, c_ref, o_ref):
        tot = ((g_ref[0, 0:1, :] + g_ref[2, 0:1, :]) + g_ref[4, 0:1, :]) + g_ref[6, 0:1, :]
        cv = c_ref[...]
        s = _sigmoid(cv)
        o_ref[...] = tot * (s * (1.0 + cv * (1.0 - s)))

    return _pcall(body, name=name, out_shape=jax.ShapeDtypeStruct(c_ctx2.shape, F32))(gath, c_ctx2)


def _row_block(r, c, limit=256 * 1024):
    best = None
    for br in range(8, r + 1, 8):
        if r % br == 0 and br * c <= limit:
            best = br
    return best if best is not None else r


def _sum4(parts, *, name):
    _, r, c = parts.shape
    br = _row_block(r, c)

    def body(p_ref, o_ref):
        o_ref[...] = ((p_ref[0] + p_ref[1]) + p_ref[2]) + p_ref[3]

    return _pcall(body, name=name, grid=(r // br,),
                  in_specs=[pl.BlockSpec((4, br, c), lambda i: (0, i, 0))],
                  out_specs=pl.BlockSpec((br, c), lambda i: (i, 0)),
                  out_shape=jax.ShapeDtypeStruct((r, c), F32))(parts)


def _adam(w, gs, m, v, *, name):
    r, c = w.shape
    br = _row_block(r, c)
    ng = len(gs)
    c1 = 1.0 - ADAM_B1 ** ADAM_STEP
    c2 = 1.0 - ADAM_B2 ** ADAM_STEP

    def body(*refs):
        w_ref, m_ref, v_ref = refs[0], refs[1 + ng], refs[2 + ng]
        g_out, d_out, m_out, v_out = refs[3 + ng:]
        g = refs[1][...]
        for k in range(1, ng):
            g = g + refs[1 + k][...]
        mn = ADAM_B1 * m_ref[...] + (1.0 - ADAM_B1) * g
        vn = ADAM_B2 * v_ref[...] + (1.0 - ADAM_B2) * (g * g)
        g_out[...] = g
        m_out[...] = mn
        v_out[...] = vn
        d_out[...] = -ADAM_LR * ((mn / c1) / (jnp.sqrt(vn / c2) + ADAM_EPS) + ADAM_WD * w_ref[...])

    blk = pl.BlockSpec((br, c), lambda i: (i, 0))
    return _pcall(body, name=name, grid=(r // br,), in_specs=[blk] * (3 + ng), out_specs=[blk] * 4,
                  out_shape=[jax.ShapeDtypeStruct((r, c), F32)] * 4)(w, *gs, m, v)


def _to_shards(name, g):
    if name == 'ffn_in':
        l, k, n4 = g.shape
        return g.reshape(l, k, 4, n4 // 4).transpose(2, 0, 1, 3).reshape(4, l * k, n4 // 4)
    if name == 'ffn_out':
        l, k4, n = g.shape
        return g.reshape(l, 4, k4 // 4, n).transpose(1, 0, 2, 3).reshape(4, l * k4 // 4, n)
    if name in ('even_in', 'odd_in'):
        k, n4 = g.shape
        return g.reshape(k, 4, n4 // 4).transpose(1, 0, 2)
    k4, n = g.shape
    return g.reshape(4, k4 // 4, n)


def _from_shards(name, g):
    if name == 'ffn_in':
        _, l, k, n = g.shape
        return g.transpose(1, 2, 0, 3).reshape(l, k, 4 * n)
    if name == 'ffn_out':
        _, l, k, n = g.shape
        return g.transpose(1, 0, 2, 3).reshape(l, 4 * k, n)
    if name in ('even_in', 'odd_in'):
        _, k, n = g.shape
        return g.transpose(1, 0, 2).reshape(k, 4 * n)
    _, k, n = g.shape
    return g.reshape(4 * k, n)


def kernel(x, c, ctx, c_ctx, mod_w, mod_b, norm_g, ffn_w_in, ffn_w_out, even_w_in, even_w_out, attn_qk_norm_g, attn_sink, hgrn_out_norm_g, hgrn_lb, odd_w_in, odd_w_out, loss_target, m_c_ctx, m_mod_w, m_mod_b, m_norm_g, m_ffn_w_in, m_ffn_w_out, m_even_w_in, m_even_w_out, m_attn_qk_norm_g, m_attn_sink, m_hgrn_out_norm_g, m_hgrn_lb, m_odd_w_in, m_odd_w_out, v_c_ctx, v_mod_w, v_mod_b, v_norm_g, v_ffn_w_in, v_ffn_w_out, v_even_w_in, v_even_w_out, v_attn_qk_norm_g, v_attn_sink, v_hgrn_out_norm_g, v_hgrn_lb, v_odd_w_in, v_odd_w_out):
    d = x.shape[-1]
    lc = ctx.shape[1]
    assert lc == TM and d == 1024
    ax, ay, ac = _place()
    s = 2 * ax + ay
    me = 4 * ax + 2 * ay + ac
    nmod = mod_w.shape[2]

    pack = jnp.concatenate([c, norm_g.reshape(1, d), jnp.zeros((6, d), F32)], axis=0)
    g1 = _ag8(pack, name='gather_cond')
    c_all = g1[:, 0, :]
    ng = g1[0::2, 1, :].reshape(4, 2, 2, d // 4).transpose(1, 2, 0, 3).reshape(4, d)

    names = ['ffn_in', 'ffn_out', 'even_in', 'even_out', 'odd_in', 'odd_out']
    shards = [_bf(ffn_w_in), _bf(ffn_w_out), _bf(even_w_in[0]), _bf(even_w_out[0]), _bf(odd_w_in[0]), _bf(odd_w_out[0])]
    gathered = _chip_exchange(shards, scatter=False, name='gather_weights')
    w = {nm: _from_shards(nm, g) for nm, g in zip(names, gathered)}

    cond_raw = jnp.concatenate([c_all, c_ctx.reshape(1, d), jnp.zeros((7, d), F32)], axis=0)
    mb_sh = lax.dynamic_slice_in_dim(mod_b, s * nmod, nmod, axis=1).reshape(2, 1, nmod)
    mpart = _mod_fwd(cond_raw, mod_w, mb_sh, name='mod_fwd')
    g3 = _ag8(mpart.reshape(32, nmod), name='gather_mods')
    mods_full = g3[0::2].reshape(4, 2, 16, nmod).transpose(1, 2, 0, 3).reshape(2, 16, 4 * nmod)
    m_lat = lax.dynamic_index_in_dim(mods_full, me, axis=1, keepdims=False)
    mods = jnp.stack([mods_full[:, 8], m_lat], axis=1).reshape(24, d)

    lb = _lb_fwd(hgrn_lb, name='hgrn_lower_bound')
    small = dict(gq=jnp.tile(attn_qk_norm_g[0, 0], 2).reshape(1, 128), gk=jnp.tile(attn_qk_norm_g[0, 1], 2).reshape(1, 128),
                 sink=attn_sink[0], gain=hgrn_out_norm_g, lb=lb)
    x0 = jnp.concatenate([ctx[0], x[0]], axis=0)
    loss_t, dx0, grads, sums = _local_step(x0, loss_target[0], mods, ng, w, small)
    loss = lax.psum(loss_t[0, 0], ("x", "y", "c"))
    grad_x = dx0[lc:][None]

    def pad(v):
        return jnp.pad(v, ((0, 0), (0, d - v.shape[1])))

    sm = sums
    dm_rows = [
        [sm['l0n1'][0], sm['l0n1'][1], sm['l0n2'][2], sm['l0n2'][0], sm['l0n2'][1], sm['l1n1'][2]],
        [sm['l0n1'][4], sm['l0n1'][5], sm['l0n2'][6], sm['l0n2'][4], sm['l0n2'][5], sm['l1n1'][6]],
        [sm['l1n1'][0], sm['l1n1'][1], sm['l1n2'][2], sm['l1n2'][0], sm['l1n2'][1], sm['fin'][2]],
        [sm['l1n1'][4], sm['l1n1'][5], sm['l1n2'][6], sm['l1n2'][4], sm['l1n2'][5], sm['fin'][6]],
    ]
    rows = [r for grp in dm_rows for r in grp]
    for key in ('l0n1', 'l0n2', 'l1n1', 'l1n2'):
        rows += [sm[key][3], sm[key][7]]
    rows = jnp.stack(rows)
    extra = jnp.concatenate([pad(sm['gq'][0:1]), pad(sm['gk'][0:1]), pad(sm['gain'][0:1]), pad(sm['dlb']),
                             pad(sm['sink'][:, 0].reshape(1, 8)), jnp.zeros((3, d), F32)], axis=0)
    g4 = _ag8(jnp.concatenate([rows, extra], axis=0), name='gather_row_sums')
    small_g, glb, gmb, dmat = _small_finalize(g4, pad(lb), name='small_grads')
    dms = lax.dynamic_slice_in_dim(dmat.transpose(0, 2, 1, 3).reshape(2, 16, 6 * d), s * nmod, nmod, axis=2)
    g_mod_w, dcond = _mod_bwd(cond_raw, dms, mod_w, name='mod_bwd')
    g5 = _ag8(dcond[8:16], name='gather_dcond')
    g_c_ctx = _cctx_grad(g5, c_ctx.reshape(8, d // 8).reshape(1, d), name='c_ctx_grad')

    parts = _chip_exchange([_to_shards(nm, grads[nm]) for nm in names], scatter=True, name='scatter_grads')
    psum = [_sum4(p, name='sum_chips') for p in parts]
    sib = _sib_swap(psum, name='swap_cores')

    def upd(wv, gs, mv, vv, name):
        shp = wv.shape
        c2 = shp[-1]
        out = _adam(wv.reshape(-1, c2), [g.reshape(-1, c2) for g in gs], mv.reshape(-1, c2), vv.reshape(-1, c2), name=name)
        return [o.reshape(shp) for o in out]

    res = {}
    res['c_ctx'] = upd(c_ctx.reshape(8, d // 8), [g_c_ctx.reshape(8, d // 8)], m_c_ctx.reshape(8, d // 8), v_c_ctx.reshape(8, d // 8), 'adam_c_ctx')
    res['c_ctx'] = [o.reshape(d) for o in res['c_ctx']]
    res['mod_w'] = upd(mod_w, [g_mod_w], m_mod_w, v_mod_w, 'adam_mod_w')
    res['mod_b'] = upd(mod_b, [gmb.reshape(2, 6 * d)], m_mod_b, v_mod_b, 'adam_mod_b')
    g_ng = lax.dynamic_slice_in_dim(small_g[0:4].reshape(2, 2, d), s * (d // 4), d // 4, axis=2)
    res['norm_g'] = upd(norm_g, [g_ng], m_norm_g, v_norm_g, 'adam_norm_g')
    big = dict(zip(names, zip(psum, sib)))
    res['ffn_w_in'] = upd(ffn_w_in, big['ffn_in'], m_ffn_w_in, v_ffn_w_in, 'adam_ffn_in')
    res['ffn_w_out'] = upd(ffn_w_out, big['ffn_out'], m_ffn_w_out, v_ffn_w_out, 'adam_ffn_out')
    res['even_w_in'] = upd(even_w_in, big['even_in'], m_even_w_in, v_even_w_in, 'adam_even_in')
    res['even_w_out'] = upd(even_w_out, big['even_out'], m_even_w_out, v_even_w_out, 'adam_even_out')
    g_qk = jnp.stack([small_g[4, 0:64], small_g[5, 0:64]]).reshape(1, 2, 64)
    res['attn_qk_norm_g'] = upd(attn_qk_norm_g, [g_qk], m_attn_qk_norm_g, v_attn_qk_norm_g, 'adam_qk_gain')
    res['attn_sink'] = upd(attn_sink, [small_g[7, 0:8].reshape(1, 8)], m_attn_sink, v_attn_sink, 'adam_sink')
    res['hgrn_out_norm_g'] = upd(hgrn_out_norm_g, [small_g[6, 0:128].reshape(1, 128)], m_hgrn_out_norm_g, v_hgrn_out_norm_g, 'adam_head_gain')
    res['hgrn_lb'] = upd(hgrn_lb, [glb[0:2, 0:hgrn_lb.shape[1]]], m_hgrn_lb, v_hgrn_lb, 'adam_hgrn_lb')
    res['odd_w_in'] = upd(odd_w_in, big['odd_in'], m_odd_w_in, v_odd_w_in, 'adam_odd_in')
    res['odd_w_out'] = upd(odd_w_out, big['odd_out'], m_odd_w_out, v_odd_w_out, 'adam_odd_out')

    order = ['c_ctx', 'mod_w', 'mod_b', 'norm_g', 'ffn_w_in', 'ffn_w_out', 'even_w_in', 'even_w_out',
             'attn_qk_norm_g', 'attn_sink', 'hgrn_out_norm_g', 'hgrn_lb', 'odd_w_in', 'odd_w_out']
    outs = [loss, grad_x]
    for k in range(4):
        outs += [res[nm][k] for nm in order]
    return tuple(outs)
```

```python
import functools
import math

import numpy as np
import jax
import jax.numpy as jnp
from jax import lax
from jax.experimental import pallas as pl
from jax.experimental.pallas import tpu as pltpu

F32 = jnp.float32
BF16 = jnp.bfloat16
EPS = 1e-6
TM = 256
CHUNK = 64
QB = 128
WINDOW = 128
NEG = -1e30
MESH = pl.DeviceIdType.MESH

ADAM_LR, ADAM_B1, ADAM_B2, ADAM_EPS, ADAM_WD, ADAM_STEP = 0.001, 0.9, 0.999, 1e-08, 0.01, 10


def _pcall(body, **kw):
    return pl.pallas_call(body, **kw)


def _pick(n, cap):
    best = None
    for m in range(128, min(n, cap) + 1, 128):
        if n % m == 0:
            best = m
    assert best is not None, (n, cap)
    return best


def _bf(x):
    return x.astype(BF16)


def _dot(a, b):
    return jnp.dot(_bf(a), _bf(b), preferred_element_type=F32)


def _dot_nt(a, b):
    return lax.dot_general(_bf(a), _bf(b), (((1,), (1,)), ((), ())), preferred_element_type=F32)


def _dot_tn(a, b):
    return lax.dot_general(_bf(a), _bf(b), (((0,), (0,)), ((), ())), preferred_element_type=F32)


def _dot_exact(a, b):
    return jnp.dot(a, b, preferred_element_type=F32, precision=lax.Precision.HIGHEST)


def _sigmoid(x):
    return 1.0 / (1.0 + jnp.exp(-x))


def _iota(shape, dim):
    return lax.broadcasted_iota(jnp.int32, shape, dim)


def _mm_nn(a, b, *, lead=None, out_dtype=F32, name):
    m, k = a.shape
    n = b.shape[-1]
    bm = 768 if m % 768 == 0 else TM
    bn = _pick(n, 1024)

    def body(a_ref, b_ref, o_ref):
        o_ref[...] = _dot(a_ref[...], b_ref[...]).astype(o_ref.dtype)

    if lead is None:
        b_spec = pl.BlockSpec((k, bn), lambda i, j: (0, j))
    else:
        b_spec = pl.BlockSpec((None, k, bn), lambda i, j: (lead, 0, j))
    return _pcall(
        body, name=name, grid=(m // bm, n // bn),
        in_specs=[pl.BlockSpec((bm, k), lambda i, j: (i, 0)), b_spec],
        out_specs=pl.BlockSpec((bm, bn), lambda i, j: (i, j)),
        out_shape=jax.ShapeDtypeStruct((m, n), out_dtype),
    )(a, b)


def _mm_nt(a, b, *, lead=None, name):
    m, n = a.shape
    k = b.shape[-2]
    bm = 768 if m % 768 == 0 else TM
    bk = _pick(k, 512)

    def body(a_ref, b_ref, o_ref):
        o_ref[...] = _dot_nt(a_ref[...], b_ref[...])

    if lead is None:
        b_spec = pl.BlockSpec((bk, n), lambda i, j: (j, 0))
    else:
        b_spec = pl.BlockSpec((None, bk, n), lambda i, j: (lead, j, 0))
    return _pcall(
        body, name=name, grid=(m // bm, k // bk),
        in_specs=[pl.BlockSpec((bm, n), lambda i, j: (i, 0)), b_spec],
        out_specs=pl.BlockSpec((bm, bk), lambda i, j: (i, j)),
        out_shape=jax.ShapeDtypeStruct((m, k), F32),
    )(a, b)


def _mm_tn(a, b, *, name):
    t, k = a.shape
    n = b.shape[1]
    bt = 768 if t % 768 == 0 else TM
    bk = _pick(k, 1536)
    bn = _pick(n, 1024) if n % 1024 == 0 or n < 1664 else _pick(n, 1664)

    def body(a_ref, b_ref, o_ref):
        @pl.when(pl.program_id(2) == 0)
        def _():
            o_ref[...] = jnp.zeros_like(o_ref)
        o_ref[...] += _dot_tn(a_ref[...], b_ref[...])

    return _pcall(
        body, name=name, grid=(k // bk, n // bn, t // bt),
        in_specs=[pl.BlockSpec((bt, bk), lambda i, j, s: (s, i)),
                  pl.BlockSpec((bt, bn), lambda i, j, s: (s, j))],
        out_specs=pl.BlockSpec((bk, bn), lambda i, j, s: (i, j)),
        out_shape=jax.ShapeDtypeStruct((k, n), F32),
    )(a, b)


def _mod_row(mods_ref, lat, idx):
    return jnp.where(lat, mods_ref[idx + 6:idx + 7, :], mods_ref[idx:idx + 1, :])


def _row_fwd(x, mods, *, y=None, gate=None, g=None, shift=None, scale=None, name):
    t, d = x.shape
    has_y, has_n = y is not None, g is not None

    def body(*refs):
        refs = list(refs)
        x_ref, mods_ref = refs[0], refs[1]
        pos = 2
        if has_y:
            y_ref = refs[pos]; pos += 1
        if has_n:
            g_ref = refs[pos]; pos += 1
        outs = refs[pos:]
        lat = pl.program_id(0) > 0
        x1 = x_ref[...]
        o = 0
        if has_y:
            x1 = x1 + _mod_row(mods_ref, lat, gate) * y_ref[...]
            outs[o][...] = x1; o += 1
        if has_n:
            rs = lax.rsqrt(jnp.mean(x1 * x1, axis=-1, keepdims=True) + EPS)
            hn = x1 * rs * g_ref[...]
            h = hn * (1.0 + _mod_row(mods_ref, lat, scale)) + _mod_row(mods_ref, lat, shift)
            outs[o][...] = h.astype(BF16)

    row = pl.BlockSpec((TM, d), lambda i: (i, 0))
    ins, specs = [x, mods], [row, pl.BlockSpec(mods.shape, lambda i: (0, 0))]
    if has_y:
        ins.append(y); specs.append(row)
    if has_n:
        ins.append(g.reshape(1, d)); specs.append(pl.BlockSpec((1, d), lambda i: (0, 0)))
    out_shape, out_specs = [], []
    if has_y:
        out_shape.append(jax.ShapeDtypeStruct((t, d), F32)); out_specs.append(row)
    if has_n:
        out_shape.append(jax.ShapeDtypeStruct((t, d), BF16)); out_specs.append(row)
    res = _pcall(body, name=name, grid=(t // TM,), in_specs=specs, out_specs=out_specs,
                 out_shape=out_shape)(*ins)
    return res


def _acc_row(ref, r, val):
    ref[r:r + 1, :] += val


def _row_final(x, z, mods, target, *, gate, name):
    t, d = x.shape

    def body(x_ref, mods_ref, z_ref, t_ref, loss_ref, dx_ref, dz_ref, sums_ref):
        i = pl.program_id(0)
        lat = i > 0

        @pl.when(i == 0)
        def _():
            loss_ref[...] = jnp.zeros_like(loss_ref)
            sums_ref[...] = jnp.zeros_like(sums_ref)

        gt = _mod_row(mods_ref, lat, gate)
        zz = z_ref[...]
        yv = x_ref[...] + gt * zz
        keep = jnp.where(lat, 1.0, 0.0).astype(F32)
        diff = (yv - t_ref[...]) * keep
        part = jnp.sum(jnp.sum(diff * diff, axis=0, keepdims=True), axis=1, keepdims=True)
        loss_ref[...] += part * (0.5 / d)
        dy = diff * (1.0 / d)
        dx_ref[...] = dy
        dz_ref[...] = (gt * dy).astype(BF16)
        _acc_row(sums_ref, 6, jnp.sum(dy * zz, axis=0, keepdims=True))

    row = pl.BlockSpec((TM, d), lambda i: (i, 0))
    return _pcall(
        body, name=name, grid=(t // TM,),
        in_specs=[row, pl.BlockSpec(mods.shape, lambda i: (0, 0)), row,
                  pl.BlockSpec((TM, d), lambda i: (jnp.maximum(i - 1, 0), 0))],
        out_specs=[pl.BlockSpec((8, 128), lambda i: (0, 0)), row, row,
                   pl.BlockSpec((8, d), lambda i: (0, 0))],
        out_shape=[jax.ShapeDtypeStruct((8, 128), F32), jax.ShapeDtypeStruct((t, d), F32),
                   jax.ShapeDtypeStruct((t, d), BF16), jax.ShapeDtypeStruct((8, d), F32)],
    )(x, mods, z, target)


def _row_bwd(xn, dxo, dh, mods, g, *, shift, scale, y=None, gate=None, name):
    t, d = xn.shape
    has_y = y is not None

    def body(*refs):
        refs = list(refs)
        x_ref, dxo_ref, dh_ref, mods_ref, g_ref = refs[:5]
        pos = 5
        if has_y:
            y_ref = refs[pos]; pos += 1
        dx_ref = refs[pos]; pos += 1
        if has_y:
            dy_ref = refs[pos]; pos += 1
        sums_ref = refs[pos]
        i = pl.program_id(0)
        lat = i > 0

        @pl.when(i == 0)
        def _():
            sums_ref[...] = jnp.zeros_like(sums_ref)

        x1 = x_ref[...]
        gv = g_ref[...]
        rs = lax.rsqrt(jnp.mean(x1 * x1, axis=-1, keepdims=True) + EPS)
        xh = x1 * rs
        dhv = dh_ref[...]
        dn = dhv * (1.0 + _mod_row(mods_ref, lat, scale))
        dxh = dn * gv
        dx = dxo_ref[...] + rs * (dxh - xh * jnp.mean(dxh * xh, axis=-1, keepdims=True))
        dx_ref[...] = dx
        vals = [jnp.sum(dhv, axis=0, keepdims=True),
                jnp.sum(dhv * (xh * gv), axis=0, keepdims=True),
                None,
                jnp.sum(dn * xh, axis=0, keepdims=True)]
        if has_y:
            dy_ref[...] = (_mod_row(mods_ref, lat, gate) * dx).astype(BF16)
            vals[2] = jnp.sum(dx * y_ref[...], axis=0, keepdims=True)

        @pl.when(i == 0)
        def _():
            for r, v in enumerate(vals):
                if v is not None:
                    _acc_row(sums_ref, r, v)

        @pl.when(i > 0)
        def _():
            for r, v in enumerate(vals):
                if v is not None:
                    _acc_row(sums_ref, 4 + r, v)

    row = pl.BlockSpec((TM, d), lambda i: (i, 0))
    ins = [xn, dxo, dh, mods, g.reshape(1, d)]
    specs = [row, row, row, pl.BlockSpec(mods.shape, lambda i: (0, 0)), pl.BlockSpec((1, d), lambda i: (0, 0))]
    out_shape, out_specs = [jax.ShapeDtypeStruct((t, d), F32)], [row]
    if has_y:
        ins.append(y); specs.append(row)
        out_shape.append(jax.ShapeDtypeStruct((t, d), BF16)); out_specs.append(row)
    out_shape.append(jax.ShapeDtypeStruct((8, d), F32))
    out_specs.append(pl.BlockSpec((8, d), lambda i: (0, 0)))
    return _pcall(body, name=name, grid=(t // TM,), in_specs=specs, out_specs=out_specs,
                  out_shape=out_shape)(*ins)


FFN_BK = 256


def _ffn_interleave(w):
    *lead, k, n2 = w.shape
    nb = n2 // (2 * FFN_BK)
    return jnp.swapaxes(w.reshape(*lead, k, 2, nb, FFN_BK), -3, -2).reshape(*lead, k, n2)


def _ffn_deinterleave(w):
    *lead, k, n2 = w.shape
    nb = n2 // (2 * FFN_BK)
    return jnp.swapaxes(w.reshape(*lead, k, nb, 2, FFN_BK), -3, -2).reshape(*lead, k, n2)


def _big_tile(t):
    for bm in (2816, 768):
        if t % bm == 0:
            return bm
    return TM


def _ffn_in(h, w, *, lead, name):
    t, d = h.shape
    n2 = w.shape[-1]
    bm, bk = _big_tile(t), FFN_BK

    def body(h_ref, w_ref, u_ref, a_ref):
        ub = _dot(h_ref[...], w_ref[...]).astype(BF16)
        u_ref[...] = ub
        uf = ub.astype(F32)
        gv, up = uf[:, 0:bk], uf[:, bk:2 * bk]
        a_ref[...] = (gv * _sigmoid(gv) * up).astype(BF16)

    return _pcall(
        body, name=name, grid=(t // bm, n2 // (2 * bk)),
        in_specs=[pl.BlockSpec((bm, d), lambda i, j: (i, 0)),
                  pl.BlockSpec((None, d, 2 * bk), lambda i, j: (lead, 0, j))],
        out_specs=[pl.BlockSpec((bm, 2 * bk), lambda i, j: (i, j)), pl.BlockSpec((bm, bk), lambda i, j: (i, j))],
        out_shape=[jax.ShapeDtypeStruct((t, n2), BF16), jax.ShapeDtypeStruct((t, n2 // 2), BF16)],
    )(h, w)


def _ffn_dx(dz, w_out, u, *, lead, name):
    t, d = dz.shape
    n2 = u.shape[1]
    bm, bk = _big_tile(t), FFN_BK

    def body(dz_ref, w_ref, u_ref, du_ref):
        da = _dot_nt(dz_ref[...], w_ref[...])
        uf = u_ref[...].astype(F32)
        gv, up = uf[:, 0:bk], uf[:, bk:2 * bk]
        s = _sigmoid(gv)
        du_ref[:, 0:bk] = (da * up * (s * (1.0 + gv * (1.0 - s)))).astype(BF16)
        du_ref[:, bk:2 * bk] = (da * gv * s).astype(BF16)

    ublk = pl.BlockSpec((bm, 2 * bk), lambda i, j: (i, j))
    return _pcall(
        body, name=name, grid=(t // bm, n2 // (2 * bk)),
        in_specs=[pl.BlockSpec((bm, d), lambda i, j: (i, 0)),
                  pl.BlockSpec((None, bk, d), lambda i, j: (lead, j, 0)), ublk],
        out_specs=ublk, out_shape=jax.ShapeDtypeStruct((t, n2), BF16),
    )(dz, w_out, u)


def _lane(shape):
    return _iota(shape, len(shape) - 1)


def _pair_norm(x, g):
    lo = _lane(x.shape) < 64
    x2 = x * x
    s_lo = jnp.sum(jnp.where(lo, x2, 0.0), axis=-1, keepdims=True)
    s_hi = jnp.sum(jnp.where(lo, 0.0, x2), axis=-1, keepdims=True)
    rs = lax.rsqrt(jnp.where(lo, s_lo, s_hi) * (1.0 / 64) + EPS)
    return x * rs, rs


def _pair_mean(v):
    lo = _lane(v.shape) < 64
    s_lo = jnp.sum(jnp.where(lo, v, 0.0), axis=-1, keepdims=True)
    s_hi = jnp.sum(jnp.where(lo, 0.0, v), axis=-1, keepdims=True)
    return jnp.where(lo, s_lo, s_hi) * (1.0 / 64)


def _rot64(x):
    r1 = pltpu.roll(x, 32, 1)
    r2 = pltpu.roll(x, 96, 1)
    even = ((_lane(x.shape) >> 5) & 1) == 0
    return jnp.where(even, -r2, r1)


def _rope64(x, cos, sin):
    return x * cos + _rot64(x) * sin


def _rope64_t(d, cos, sin):
    return d * cos - _rot64(d * sin)


def _kprep_fwd(p, gk, cos, sin, *, name):
    t = p.shape[0]

    def body(k_ref, g_ref, c_ref, s_ref, o_ref):
        xh, _ = _pair_norm(k_ref[...], None)
        o_ref[...] = _rope64(xh * g_ref[...], c_ref[...], s_ref[...])

    blk = pl.BlockSpec((TM, 128), lambda i: (i, 0))
    return _pcall(
        body, name=name, grid=(t // TM,),
        in_specs=[pl.BlockSpec((TM, 128), lambda i: (i, 4)), pl.BlockSpec((1, 128), lambda i: (0, 0)), blk, blk],
        out_specs=blk, out_shape=jax.ShapeDtypeStruct((t, 128), F32),
    )(p, gk, cos, sin)


def _kprep_bwd(p, gk, cos, sin, dkp, dv, *, name):
    t = p.shape[0]

    def body(k_ref, g_ref, c_ref, s_ref, dkp_ref, dv_ref, o_ref, dg_ref):
        @pl.when(pl.program_id(0) == 0)
        def _():
            dg_ref[...] = jnp.zeros_like(dg_ref)
        xh, rs = _pair_norm(k_ref[...], None)
        dn = _rope64_t(dkp_ref[...], c_ref[...], s_ref[...])
        _acc_row(dg_ref, 0, jnp.sum(dn * xh, axis=0, keepdims=True))
        dxh = dn * g_ref[...]
        o_ref[:, 0:128] = (rs * (dxh - xh * _pair_mean(dxh * xh))).astype(BF16)
        o_ref[:, 128:256] = dv_ref[...].astype(BF16)

    blk = pl.BlockSpec((TM, 128), lambda i: (i, 0))
    return _pcall(
        body, name=name, grid=(t // TM,),
        in_specs=[pl.BlockSpec((TM, 128), lambda i: (i, 4)), pl.BlockSpec((1, 128), lambda i: (0, 0)), blk, blk, blk, blk],
        out_specs=[pl.BlockSpec((TM, 256), lambda i: (i, 0)), pl.BlockSpec((8, 128), lambda i: (0, 0))],
        out_shape=[jax.ShapeDtypeStruct((t, 256), BF16), jax.ShapeDtypeStruct((8, 128), F32)],
    )(p, gk, cos, sin, dkp, dv)


def _attn_common(i, t, lc, kp_ref, v_ref):
    span = QB + 2 * WINDOW
    start = pl.multiple_of(jnp.clip((i - 1) * QB, lc, t - span), QB)
    kall = jnp.concatenate([kp_ref[0:lc, :], kp_ref[pl.ds(start, span), :]], axis=0)
    vall = jnp.concatenate([v_ref[0:lc, :], v_ref[pl.ds(start, span), :]], axis=0)
    nk = lc + span
    col = _iota((QB, nk), 1)
    krow = jnp.where(col < lc, col, start + col - lc)
    qrow = i * QB + _iota((QB, nk), 0)
    valid = (col < lc) | ((qrow >= lc) & (krow >= lc) & (jnp.abs(krow - qrow) <= WINDOW))
    lo = _lane(kall.shape) < 64
    kroll, vroll = pltpu.roll(kall, 64, 1), pltpu.roll(vall, 64, 1)
    zero = jnp.zeros_like(kall)
    kvar = [[_bf(jnp.where(lo, kall, zero)), _bf(jnp.where(lo, zero, kroll))],
            [_bf(jnp.where(lo, kroll, zero)), _bf(jnp.where(lo, zero, kall))]]
    vvar = [[_bf(jnp.where(lo, vall, zero)), _bf(jnp.where(lo, zero, vroll))],
            [_bf(jnp.where(lo, vroll, zero)), _bf(jnp.where(lo, zero, vall))]]
    return start, valid, kvar, vvar


def _softmax_sink(s, valid, snk):
    s = jnp.where(valid, s, NEG)
    m = jnp.maximum(jnp.max(s, axis=-1, keepdims=True), snk)
    e = jnp.exp(s - m)
    es = jnp.exp(snk - m)
    inv = 1.0 / (jnp.sum(e, axis=-1, keepdims=True) + es)
    return e * inv, es * inv


def _attn_fwd(p, kp, gq, sink, cos, sin, *, lc, name):
    t = p.shape[0]
    scale = 64 ** -0.5

    def body(q_ref, kp_ref, v_ref, g_ref, sink_ref, c_ref, s_ref, o_ref):
        i = pl.program_id(0)
        _, valid, kvar, vvar = _attn_common(i, t, lc, kp_ref, v_ref)
        cosv, sinv, gv = c_ref[...], s_ref[...], g_ref[...]
        for j in range(4):
            xh, _ = _pair_norm(q_ref[:, 128 * j:128 * j + 128], None)
            q2 = _bf(_rope64(xh * gv, cosv, sinv))
            acc = jnp.zeros((QB, 128), F32)
            for half in range(2):
                s = _dot_nt(q2, kvar[j // 2][half]) * scale
                pr, _ = _softmax_sink(s, valid, sink_ref[2 * j + half])
                acc = acc + _dot(pr, vvar[j // 2][half])
            o_ref[:, 128 * j:128 * j + 128] = acc.astype(BF16)

    qblk = pl.BlockSpec((QB, 128), lambda i: (i, 0))
    return _pcall(
        body, name=name, grid=(t // QB,),
        in_specs=[pl.BlockSpec((QB, 512), lambda i: (i, 0)),
                  pl.BlockSpec((t, 128), lambda i: (0, 0)),
                  pl.BlockSpec((t, 128), lambda i: (0, 5)),
                  pl.BlockSpec((1, 128), lambda i: (0, 0)),
                  pl.BlockSpec(memory_space=pltpu.SMEM), qblk, qblk],
        out_specs=pl.BlockSpec((QB, 512), lambda i: (i, 0)),
        out_shape=jax.ShapeDtypeStruct((t, 512), BF16),
    )(p, kp, p, gq, sink, cos, sin)


def _attn_bwd(p, kp, gq, sink, cos, sin, dmix, *, lc, name):
    t = p.shape[0]
    scale = 64 ** -0.5
    span = QB + 2 * WINDOW

    def body(q_ref, kp_ref, v_ref, g_ref, sink_ref, c_ref, s_ref, do_ref,
             dq_ref, dk_ref, dv_ref, dg_ref, dsink_ref):
        i = pl.program_id(0)

        @pl.when(i == 0)
        def _():
            dk_ref[...] = jnp.zeros_like(dk_ref)
            dv_ref[...] = jnp.zeros_like(dv_ref)
            dg_ref[...] = jnp.zeros_like(dg_ref)
            dsink_ref[...] = jnp.zeros_like(dsink_ref)

        start, valid, kvar, vvar = _attn_common(i, t, lc, kp_ref, v_ref)
        cosv, sinv, gv = c_ref[...], s_ref[...], g_ref[...]
        nk = lc + span
        lo = _lane((nk, 128)) < 64
        dk_all = jnp.zeros((nk, 128), F32)
        dv_all = jnp.zeros((nk, 128), F32)
        for j in range(4):
            kvh = j // 2
            xh, rs = _pair_norm(q_ref[:, 128 * j:128 * j + 128], None)
            q2 = _bf(_rope64(xh * gv, cosv, sinv))
            do2 = _bf(do_ref[:, 128 * j:128 * j + 128])
            dq2 = jnp.zeros((QB, 128), F32)
            for half in range(2):
                s = _dot_nt(q2, kvar[kvh][half]) * scale
                pr, ps = _softmax_sink(s, valid, sink_ref[2 * j + half])
                dp = _dot_nt(do2, vvar[kvh][half])
                delta = jnp.sum(pr * dp, axis=-1, keepdims=True)
                ds = pr * (dp - delta) * scale
                dsk = jnp.sum(jnp.sum(-ps * delta, axis=0, keepdims=True), axis=1, keepdims=True)
                _acc_row(dsink_ref, 2 * j + half, jnp.broadcast_to(dsk, (1, 128)))
                dq2 = dq2 + _dot(ds, kvar[kvh][half])
                gk_ = _dot_tn(ds, q2)
                gv_ = _dot_tn(pr, do2)
                if half == 0:
                    gk_, gv_ = jnp.where(lo, gk_, 0.0), jnp.where(lo, gv_, 0.0)
                else:
                    gk_, gv_ = jnp.where(lo, 0.0, gk_), jnp.where(lo, 0.0, gv_)
                if half != kvh:
                    gk_, gv_ = pltpu.roll(gk_, 64, 1), pltpu.roll(gv_, 64, 1)
                dk_all = dk_all + gk_
                dv_all = dv_all + gv_
            dn = _rope64_t(dq2, cosv, sinv)
            _acc_row(dg_ref, 0, jnp.sum(dn * xh, axis=0, keepdims=True))
            dxh = dn * gv
            dq_ref[:, 128 * j:128 * j + 128] = (rs * (dxh - xh * _pair_mean(dxh * xh))).astype(BF16)
        dk_ref[0:lc, :] += dk_all[0:lc]
        dv_ref[0:lc, :] += dv_all[0:lc]
        dk_ref[pl.ds(start, span), :] += dk_all[lc:nk]
        dv_ref[pl.ds(start, span), :] += dv_all[lc:nk]

    qblk = pl.BlockSpec((QB, 128), lambda i: (i, 0))
    full = pl.BlockSpec((t, 128), lambda i: (0, 0))
    small = pl.BlockSpec((8, 128), lambda i: (0, 0))
    return _pcall(
        body, name=name, grid=(t // QB,),
        in_specs=[pl.BlockSpec((QB, 512), lambda i: (i, 0)), full,
                  pl.BlockSpec((t, 128), lambda i: (0, 5)),
                  pl.BlockSpec((1, 128), lambda i: (0, 0)),
                  pl.BlockSpec(memory_space=pltpu.SMEM), qblk, qblk,
                  pl.BlockSpec((QB, 512), lambda i: (i, 0))],
        out_specs=[pl.BlockSpec((QB, 512), lambda i: (i, 0)), full, full, small, small],
        out_shape=[jax.ShapeDtypeStruct((t, 512), BF16), jax.ShapeDtypeStruct((t, 128), F32),
                   jax.ShapeDtypeStruct((t, 128), F32), jax.ShapeDtypeStruct((8, 128), F32),
                   jax.ShapeDtypeStruct((8, 128), F32)],
    )(p, kp, p, gq, sink, cos, sin, dmix)


def _tri(rev):
    r, c = _iota((CHUNK, CHUNK), 0), _iota((CHUNK, CHUNK), 1)
    return (c >= r) if rev else (c <= r)


def _blk_map(nb, rev, backward):
    if not rev:
        return (lambda n: nb - 1 - n) if backward else (lambda n: n)
    if backward:
        return lambda n: jnp.where(n < nb - 1, n + 1, 0)
    return lambda n: jnp.where(n == 0, 0, nb - n)


def _chunk_order(rev, backward, nc=TM // CHUNK):
    order = list(range(nc))
    return order[::-1] if (rev != backward) else order


def _hgrn_gates(qraw, fraw, lb):
    sq = _sigmoid(qraw)
    sf = _sigmoid(fraw)
    f = lb + (1.0 - lb) * sf
    return qraw * sq, 1.0 - f, jnp.log(f), sq, sf, f


def _gla_terms(q, k, lf, rev):
    tri = _tri(rev)
    b = _dot_exact(tri.astype(F32), lf)
    mid, last = (CHUNK // 2 - 1, 0) if rev else (CHUNK // 2, CHUNK - 1)
    r, bl = b[mid:mid + 1, :], b[last:last + 1, :]
    eq, ek, ei, eki = jnp.exp(b - r), jnp.exp(r - b), jnp.exp(b), jnp.exp(bl - b)
    return tri, last, eq, ek, ei, eki, jnp.exp(bl)


def _hgrn_fwd(p, lb, *, rev, name):
    t = p.shape[0]
    nb, nc = t // TM, TM // CHUNK
    bmap = _blk_map(nb, rev, False)
    fcol = 14 if rev else 10

    def body(q_ref, f_ref, v_ref, lb_ref, o_ref, sh_ref, st):
        @pl.when(pl.program_id(1) == 0)
        def _():
            st[...] = jnp.zeros_like(st)
        for cc in _chunk_order(rev, False):
            rows = slice(cc * CHUNK, (cc + 1) * CHUNK)
            q, k, lf, _, _, _ = _hgrn_gates(q_ref[rows, :], f_ref[rows, :], lb_ref[...])
            v = v_ref[rows, :]
            tri, _, eq, ek, ei, eki, eb = _gla_terms(q, k, lf, rev)
            s0 = st[...]
            sh_ref[cc] = s0
            a = jnp.where(tri, _dot_nt(q * eq, k * ek), 0.0)
            o_ref[rows, :] = _dot(a, v) + _dot_nt(q * ei, s0)
            st[...] = s0 * eb + _dot_tn(v, k * eki)

    def col(c0):
        return pl.BlockSpec((TM, 128), lambda h, n: (bmap(n), c0 + h))

    return _pcall(
        body, name=name, grid=(4, nb),
        in_specs=[col(6), col(fcol), col(18), pl.BlockSpec((1, 128), lambda h, n: (0, h))],
        out_specs=[pl.BlockSpec((TM, 128), lambda h, n: (bmap(n), h)),
                   pl.BlockSpec((None, nc, 128, 128), lambda h, n: (h, bmap(n), 0, 0))],
        out_shape=[jax.ShapeDtypeStruct((t, 512), F32), jax.ShapeDtypeStruct((4, t // CHUNK, 128, 128), F32)],
        scratch_shapes=[pltpu.VMEM((128, 128), F32)],
    )(p, p, p, lb)


def _hgrn_bwd(p, lb, sh, do, prev, *, rev, name):
    t = p.shape[0]
    nb, nc = t // TM, TM // CHUNK
    bmap = _blk_map(nb, rev, True)
    fcol = 14 if rev else 10
    has_prev = prev is not None
    odt = BF16 if has_prev else F32

    def body(*refs):
        refs = list(refs)
        q_ref, f_ref, v_ref, lb_ref, sh_ref, do_ref = refs[:6]
        pos = 6
        if has_prev:
            pq_ref, pv_ref = refs[6], refs[7]
            pos = 8
        dq_ref, df_ref, dv_ref, dlb_ref, dst = refs[pos:pos + 5]

        @pl.when(pl.program_id(1) == 0)
        def _():
            dst[...] = jnp.zeros_like(dst)
            dlb_ref[...] = jnp.zeros_like(dlb_ref)

        lbv = lb_ref[...]
        for cc in _chunk_order(rev, True):
            rows = slice(cc * CHUNK, (cc + 1) * CHUNK)
            qraw, fraw = q_ref[rows, :], f_ref[rows, :]
            q, k, lf, sq, sf, f = _hgrn_gates(qraw, fraw, lbv)
            v = v_ref[rows, :]
            dov = do_ref[rows, :]
            tri, last, eq, ek, ei, eki, eb = _gla_terms(q, k, lf, rev)
            s0 = sh_ref[cc]
            dsc = dst[...]
            qe, ke, qi, ki = q * eq, k * ek, q * ei, k * eki
            a = jnp.where(tri, _dot_nt(qe, ke), 0.0)
            da = jnp.where(tri, _dot_nt(dov, v), 0.0)
            dv = _dot_tn(a, dov) + _dot_nt(ki, dsc)
            dqe, dke = _dot(da, ke), _dot_tn(da, qe)
            dqi, dki = _dot(dov, s0), _dot(v, dsc)
            dst[...] = dsc * eb + _dot_tn(dov, qi)
            dq = dqe * eq + dqi * ei
            dk = dke * ek + dki * eki
            db = dqe * qe - dke * ke + dqi * qi - dki * ki
            dbl = jnp.sum(dki * ki, axis=0, keepdims=True) + jnp.sum(dsc * s0, axis=0, keepdims=True) * eb
            db = db + jnp.where(_iota(db.shape, 0) == last, dbl, 0.0)
            dlf = _dot_exact(_tri(not rev).astype(F32), db)
            dqr = dq * (sq * (1.0 + qraw * (1.0 - sq)))
            dfv = dlf / f - dk
            dfr = dfv * (1.0 - lbv) * (sf * (1.0 - sf))
            dlb_ref[...] += jnp.sum(dfv * (1.0 - sf), axis=0, keepdims=True)
            if has_prev:
                dqr = dqr + pq_ref[rows, :]
                dv = dv + pv_ref[rows, :]
            dq_ref[rows, :] = dqr.astype(odt)
            df_ref[rows, :] = dfr.astype(odt)
            dv_ref[rows, :] = dv.astype(odt)

    def col(c0):
        return pl.BlockSpec((TM, 128), lambda h, n: (bmap(n), c0 + h))

    oblk = pl.BlockSpec((TM, 128), lambda h, n: (bmap(n), h))
    ins = [p, p, p, lb, sh, do]
    specs = [col(6), col(fcol), col(18), pl.BlockSpec((1, 128), lambda h, n: (0, h)),
             pl.BlockSpec((None, nc, 128, 128), lambda h, n: (h, bmap(n), 0, 0)), oblk]
    if has_prev:
        ins += list(prev); specs += [oblk, oblk]
    return _pcall(
        body, name=name, grid=(4, nb), in_specs=specs,
        out_specs=[oblk, oblk, oblk, pl.BlockSpec((1, 128), lambda h, n: (0, h))],
        out_shape=[jax.ShapeDtypeStruct((t, 512), odt)] * 3 + [jax.ShapeDtypeStruct((1, 512), F32)],
        scratch_shapes=[pltpu.VMEM((128, 128), F32)],
    )(*ins)


def _rope256(x, cos, sin):
    x1, x2 = x[:, 0:128], x[:, 128:256]
    return jnp.concatenate([x1 * cos - x2 * sin, x2 * cos + x1 * sin], axis=-1)


def _rope256_t(d, cos, sin):
    d1, d2 = d[:, 0:128], d[:, 128:256]
    return jnp.concatenate([d1 * cos + d2 * sin, d2 * cos - d1 * sin], axis=-1)


RET_DK, RET_DV, RET_H = 256, 512, 4
RET_KSCALE = RET_DK ** -0.5
RCH = TM


def _ret_terms(lg, rev):
    r, c = _iota((RCH, RCH), 0), _iota((RCH, RCH), 1)
    rel = ((c - r) if rev else (r - c)).astype(F32)
    dmat = jnp.where(rel >= 0, jnp.exp(lg[:, 0:1] * jnp.maximum(rel, 0.0)), 0.0)
    pos = _iota((RCH, 1), 0).astype(F32)
    cnt = (RCH - pos) if rev else (pos + 1.0)
    ei = jnp.exp(lg * cnt)
    eki = jnp.exp(lg * (RCH - cnt))
    eb = jnp.exp(lg * float(RCH))
    return dmat, ei, eki, eb


def _ret_fwd(p, lgt, cos, sin, *, rev, name):
    t = p.shape[0]
    nb, nc = t // TM, TM // RCH
    bmap = _blk_map(nb, rev, False)

    def body(q_ref, k_ref, v_ref, lg_ref, c_ref, s_ref, o_ref, sh_ref, st):
        @pl.when(pl.program_id(1) == 0)
        def _():
            st[...] = jnp.zeros_like(st)
        dmat, ei, eki, eb = _ret_terms(lg_ref[...], rev)
        for cc in _chunk_order(rev, False, nc):
            rows = slice(cc * RCH, (cc + 1) * RCH)
            cosv, sinv = c_ref[rows, :], s_ref[rows, :]
            q = _rope256(q_ref[rows, :], cosv, sinv)
            k = _rope256(k_ref[rows, :], cosv, sinv) * RET_KSCALE
            v = v_ref[rows, :]
            s0 = st[...]
            sh_ref[cc] = s0.astype(BF16)
            a = _dot_nt(q, k) * dmat
            o_ref[rows, :] = _dot(a, v) + _dot_nt(q * ei, s0)
            st[...] = s0 * eb + _dot_tn(v, k * eki)

    tab = pl.BlockSpec((TM, 128), lambda h, n: (bmap(n), 0))
    return _pcall(
        body, name=name, grid=(RET_H, nb),
        in_specs=[pl.BlockSpec((TM, RET_DK), lambda h, n: (bmap(n), h)),
                  pl.BlockSpec((TM, RET_DK), lambda h, n: (bmap(n), 4 + h)),
                  pl.BlockSpec((TM, RET_DV), lambda h, n: (bmap(n), 4 + h)),
                  pl.BlockSpec((None, 1, RET_DK), lambda h, n: (h, 0, 0)), tab, tab],
        out_specs=[pl.BlockSpec((TM, RET_DV), lambda h, n: (bmap(n), h)),
                   pl.BlockSpec((None, nc, RET_DV, RET_DK), lambda h, n: (h, bmap(n), 0, 0))],
        out_shape=[jax.ShapeDtypeStruct((t, RET_H * RET_DV), F32),
                   jax.ShapeDtypeStruct((RET_H, t // RCH, RET_DV, RET_DK), BF16)],
        scratch_shapes=[pltpu.VMEM((RET_DV, RET_DK), F32)],
    )(p, p, p, lgt, cos, sin)


def _ret_bwd(p, lgt, cos, sin, sh, do, prev, *, rev, name):
    t = p.shape[0]
    nb, nc = t // TM, TM // RCH
    bmap = _blk_map(nb, rev, True)
    has_prev = prev is not None
    odt = BF16 if has_prev else F32

    def body(*refs):
        refs = list(refs)
        q_ref, k_ref, v_ref, lg_ref, c_ref, s_ref, sh_ref, do_ref = refs[:8]
        pos = 8
        if has_prev:
            pq_ref, pk_ref, pv_ref = refs[8:11]
            pos = 11
        dq_ref, dk_ref, dv_ref, dst = refs[pos:pos + 4]

        @pl.when(pl.program_id(1) == 0)
        def _():
            dst[...] = jnp.zeros_like(dst)

        dmat, ei, eki, eb = _ret_terms(lg_ref[...], rev)
        for cc in _chunk_order(rev, True, nc):
            rows = slice(cc * RCH, (cc + 1) * RCH)
            cosv, sinv = c_ref[rows, :], s_ref[rows, :]
            q = _rope256(q_ref[rows, :], cosv, sinv)
            k = _rope256(k_ref[rows, :], cosv, sinv) * RET_KSCALE
            v = v_ref[rows, :]
            dov = do_ref[rows, :]
            s0 = sh_ref[cc]
            dsc = dst[...]
            qi, ki = q * ei, k * eki
            a = _dot_nt(q, k) * dmat
            da = _dot_nt(dov, v) * dmat
            dv = _dot_tn(a, dov) + _dot_nt(ki, dsc)
            dqs = _dot(da, k) + _dot(dov, s0) * ei
            dks = _dot_tn(da, q) + _dot(v, dsc) * eki
            dst[...] = dsc * eb + _dot_tn(dov, qi)
            dq = _rope256_t(dqs, cosv, sinv)
            dk = _rope256_t(dks * RET_KSCALE, cosv, sinv)
            if has_prev:
                dq = dq + pq_ref[rows, :]
                dk = dk + pk_ref[rows, :]
                dv = dv + pv_ref[rows, :]
            dq_ref[rows, :] = dq.astype(odt)
            dk_ref[rows, :] = dk.astype(odt)
            dv_ref[rows, :] = dv.astype(odt)

    tab = pl.BlockSpec((TM, 128), lambda h, n: (bmap(n), 0))
    qblk = pl.BlockSpec((TM, RET_DK), lambda h, n: (bmap(n), h))
    vblk = pl.BlockSpec((TM, RET_DV), lambda h, n: (bmap(n), h))
    ins = [p, p, p, lgt, cos, sin, sh, do]
    specs = [qblk, pl.BlockSpec((TM, RET_DK), lambda h, n: (bmap(n), 4 + h)),
             pl.BlockSpec((TM, RET_DV), lambda h, n: (bmap(n), 4 + h)),
             pl.BlockSpec((None, 1, RET_DK), lambda h, n: (h, 0, 0)), tab, tab,
             pl.BlockSpec((None, nc, RET_DV, RET_DK), lambda h, n: (h, bmap(n), 0, 0)), vblk]
    if has_prev:
        ins += list(prev); specs += [qblk, qblk, vblk]
    return _pcall(
        body, name=name, grid=(RET_H, nb), in_specs=specs,
        out_specs=[qblk, qblk, vblk],
        out_shape=[jax.ShapeDtypeStruct((t, RET_H * RET_DK), odt), jax.ShapeDtypeStruct((t, RET_H * RET_DK), odt),
                   jax.ShapeDtypeStruct((t, RET_H * RET_DV), odt)],
        scratch_shapes=[pltpu.VMEM((RET_DV, RET_DK), F32)],
    )(*ins)


def _headnorm_fwd(ofw, obw, p, gain, *, dv, gcol, name):
    t, w = ofw.shape
    nh = w // dv
    has_gain = gain is not None

    def body(*refs):
        a_ref, b_ref, g_ref = refs[:3]
        gain_ref = refs[3] if has_gain else None
        o_ref = refs[-1]
        o = a_ref[...] + b_ref[...]
        n = o * lax.rsqrt(jnp.mean(o * o, axis=-1, keepdims=True) + EPS)
        if has_gain:
            n = n * gain_ref[...]
        gv = g_ref[...]
        o_ref[...] = (n * (gv * _sigmoid(gv))).astype(BF16)

    blk = pl.BlockSpec((TM, dv), lambda i, h: (i, h))
    ins, specs = [ofw, obw, p], [blk, blk, pl.BlockSpec((TM, dv), lambda i, h: (i, gcol + h))]
    if has_gain:
        ins.append(gain); specs.append(pl.BlockSpec((1, dv), lambda i, h: (0, 0)))
    return _pcall(body, name=name, grid=(t // TM, nh), in_specs=specs, out_specs=blk,
                  out_shape=jax.ShapeDtypeStruct((t, w), BF16))(*ins)


def _headnorm_bwd(ofw, obw, p, gain, dmix, *, dv, gcol, mcol, name):
    t, w = ofw.shape
    nh = w // dv
    has_gain = gain is not None

    def body(*refs):
        a_ref, b_ref, g_ref, dm_ref = refs[:4]
        gain_ref = refs[4] if has_gain else None
        do_ref, dg_ref, dgain_ref = refs[-3:]

        @pl.when((pl.program_id(0) == 0) & (pl.program_id(1) == 0))
        def _():
            dgain_ref[...] = jnp.zeros_like(dgain_ref)

        o = a_ref[...] + b_ref[...]
        rs = lax.rsqrt(jnp.mean(o * o, axis=-1, keepdims=True) + EPS)
        xh = o * rs
        n = xh * gain_ref[...] if has_gain else xh
        gv = g_ref[...]
        sg = _sigmoid(gv)
        dy = dm_ref[...]
        dn = dy * (gv * sg)
        dg_ref[...] = (dy * n * (sg * (1.0 + gv * (1.0 - sg)))).astype(BF16)
        _acc_row(dgain_ref, 0, jnp.sum(dn * xh, axis=0, keepdims=True))
        dxh = dn * gain_ref[...] if has_gain else dn
        do_ref[...] = rs * (dxh - xh * jnp.mean(dxh * xh, axis=-1, keepdims=True))

    blk = pl.BlockSpec((TM, dv), lambda i, h: (i, h))
    ins = [ofw, obw, p, dmix]
    specs = [blk, blk, pl.BlockSpec((TM, dv), lambda i, h: (i, gcol + h)),
             pl.BlockSpec((TM, dv), lambda i, h: (i, mcol + h))]
    if has_gain:
        ins.append(gain); specs.append(pl.BlockSpec((1, dv), lambda i, h: (0, 0)))
    return _pcall(
        body, name=name, grid=(t // TM, nh), in_specs=specs,
        out_specs=[blk, blk, pl.BlockSpec((8, dv), lambda i, h: (0, 0))],
        out_shape=[jax.ShapeDtypeStruct((t, w), F32), jax.ShapeDtypeStruct((t, w), BF16),
                   jax.ShapeDtypeStruct((8, dv), F32)],
    )(*ins)


def _rope_tables(lc, l):
    tt = jnp.arange(l)
    row, colp = (tt // 64).astype(F32), (tt % 64).astype(F32)
    inv = 10000.0 ** (-jnp.arange(16, dtype=F32) / 16)
    ang = jnp.concatenate([row[:, None] * inv, colp[:, None] * inv], axis=-1)
    ang = jnp.concatenate([jnp.zeros((lc, 32), F32), ang], axis=0)
    acos, asin = jnp.tile(jnp.cos(ang), (1, 4)), jnp.tile(jnp.sin(ang), (1, 4))
    theta = 1.0 / (10000.0 ** jnp.linspace(0.0, 1.0, 128, dtype=F32))
    rang = jnp.arange(l, dtype=F32)[:, None] * theta
    rang = jnp.concatenate([jnp.zeros((lc, 128), F32), rang], axis=0)
    return acos, asin, jnp.cos(rang), jnp.sin(rang)


def _local_step(x0, target, mods, ng, w, small):
    t, d = x0.shape
    l = target.shape[0]
    lc = t - l
    acos, asin, rcos, rsin = _rope_tables(lc, l)
    lg_fw = jnp.log(1.0 - 2.0 ** (-5.0 - jnp.arange(RET_H, dtype=F32)))
    lgt_fw = jnp.broadcast_to(lg_fw[:, None, None], (RET_H, 1, RET_DK))
    lgt_bw = jnp.broadcast_to(lg_fw[::-1][:, None, None], (RET_H, 1, RET_DK))
    gq, gk, sink, gain, lb = small['gq'], small['gk'], small['sink'], small['gain'], small['lb']

    (h1,) = _row_fwd(x0, mods, g=ng[0], shift=0, scale=1, name='l0_norm1')
    p0 = _mm_nn(h1, w['even_in'], name='l0_in')
    kp = _kprep_fwd(p0, gk, acos, asin, name='l0_kprep')
    att = _attn_fwd(p0, kp, gq, sink, acos, asin, lc=lc, name='l0_attn')
    hof, hsf = _hgrn_fwd(p0, lb, rev=False, name='l0_hgrn_f')
    hob, hsb = _hgrn_fwd(p0, lb, rev=True, name='l0_hgrn_b')
    bmix = _headnorm_fwd(hof, hob, p0, gain, dv=128, gcol=22, name='l0_headnorm')
    mix0 = jnp.concatenate([att, bmix], axis=1)
    y0 = _mm_nn(mix0, w['even_out'], name='l0_out')
    x1, h2 = _row_fwd(x0, mods, y=y0, gate=2, g=ng[1], shift=3, scale=4, name='l0_norm2')
    u0, a0 = _ffn_in(h2, w['ffn_in'], lead=0, name='ffn_in')
    z0 = _mm_nn(a0, w['ffn_out'], lead=0, name='ffn_out')
    x2, h3 = _row_fwd(x1, mods, y=z0, gate=5, g=ng[2], shift=12, scale=13, name='l1_norm1')
    p1 = _mm_nn(h3, w['odd_in'], name='l1_in')
    rof, rsf = _ret_fwd(p1, lgt_fw, rcos, rsin, rev=False, name='l1_ret_f')
    rob, rsb = _ret_fwd(p1, lgt_bw, rcos, rsin, rev=True, name='l1_ret_b')
    mix1 = _headnorm_fwd(rof, rob, p1, None, dv=RET_DV, gcol=8, name='l1_headnorm')
    y1 = _mm_nn(mix1, w['odd_out'], name='l1_out')
    x3, h4 = _row_fwd(x2, mods, y=y1, gate=14, g=ng[3], shift=15, scale=16, name='l1_norm2')
    u1, a1 = _ffn_in(h4, w['ffn_in'], lead=1, name='ffn_in')
    z1 = _mm_nn(a1, w['ffn_out'], lead=1, name='ffn_out')
    loss, dx4, dz1, s_fin = _row_final(x3, z1, mods, target, gate=17, name='loss')

    du1 = _ffn_dx(dz1, w['ffn_out'], u1, lead=1, name='ffn_out_dx')
    g_ffn_out1 = _mm_tn(a1, dz1, name='ffn_out_dw')
    dh4 = _mm_nt(du1, w['ffn_in'], lead=1, name='ffn_in_dx')
    g_ffn_in1 = _mm_tn(h4, du1, name='ffn_in_dw')
    dx3, dy1, s_l1n2 = _row_bwd(x3, dx4, dh4, mods, ng[3], shift=15, scale=16, y=y1, gate=14, name='l1_norm2_bwd')
    dmix1 = _mm_nt(dy1, w['odd_out'], name='l1_out_dx')
    g_odd_out = _mm_tn(mix1, dy1, name='l1_out_dw')
    rdo, rdg, _ = _headnorm_bwd(rof, rob, p1, None, dmix1, dv=RET_DV, gcol=8, mcol=0, name='l1_headnorm_bwd')
    part = _ret_bwd(p1, lgt_fw, rcos, rsin, rsf, rdo, None, rev=False, name='l1_ret_f_bwd')
    rdq, rdk, rdv = _ret_bwd(p1, lgt_bw, rcos, rsin, rsb, rdo, part, rev=True, name='l1_ret_b_bwd')
    dp1 = jnp.concatenate([rdq, rdk, rdv, rdg], axis=1)
    dh3 = _mm_nt(dp1, w['odd_in'], name='l1_in_dx')
    g_odd_in = _mm_tn(h3, dp1, name='l1_in_dw')
    dx2, dz0, s_l1n1 = _row_bwd(x2, dx3, dh3, mods, ng[2], shift=12, scale=13, y=z0, gate=5, name='l1_norm1_bwd')
    du0 = _ffn_dx(dz0, w['ffn_out'], u0, lead=0, name='ffn_out_dx')
    g_ffn_out0 = _mm_tn(a0, dz0, name='ffn_out_dw')
    dh2 = _mm_nt(du0, w['ffn_in'], lead=0, name='ffn_in_dx')
    g_ffn_in0 = _mm_tn(h2, du0, name='ffn_in_dw')
    dx1, dy0, s_l0n2 = _row_bwd(x1, dx2, dh2, mods, ng[1], shift=3, scale=4, y=y0, gate=2, name='l0_norm2_bwd')
    dmix0 = _mm_nt(dy0, w['even_out'], name='l0_out_dx')
    g_even_out = _mm_tn(mix0, dy0, name='l0_out_dw')
    hdo, hdg, s_gain = _headnorm_bwd(hof, hob, p0, gain, dmix0, dv=128, gcol=22, mcol=4, name='l0_headnorm_bwd')
    hq, hff, hv, dlb_f = _hgrn_bwd(p0, lb, hsf, hdo, None, rev=False, name='l0_hgrn_f_bwd')
    hq, hfb, hv, dlb_b = _hgrn_bwd(p0, lb, hsb, hdo, (hq, hv), rev=True, name='l0_hgrn_b_bwd')
    adq, dkp, adv, s_gq, s_sink = _attn_bwd(p0, kp, gq, sink, acos, asin, dmix0, lc=lc, name='l0_attn_bwd')
    dkv, s_gk = _kprep_bwd(p0, gk, acos, asin, dkp, adv, name='l0_kprep_bwd')
    dp0 = jnp.concatenate([adq, dkv, hq, _bf(hff), hfb, hv, hdg], axis=1)
    dh1 = _mm_nt(dp0, w['even_in'], name='l0_in_dx')
    g_even_in = _mm_tn(h1, dp0, name='l0_in_dw')
    dx0, s_l0n1 = _row_bwd(x0, dx1, dh1, mods, ng[0], shift=0, scale=1, name='l0_norm1_bwd')

    grads = dict(ffn_in=jnp.stack([g_ffn_in0, g_ffn_in1]), ffn_out=jnp.stack([g_ffn_out0, g_ffn_out1]),
                 even_in=g_even_in, even_out=g_even_out, odd_in=g_odd_in, odd_out=g_odd_out)
    sums = dict(fin=s_fin, l1n2=s_l1n2, l1n1=s_l1n1, l0n2=s_l0n2, l0n1=s_l0n1, gain=s_gain, gq=s_gq, gk=s_gk,
                sink=s_sink, dlb=dlb_f + dlb_b)
    return loss, dx0, grads, sums


def _place():
    return lax.axis_index("x"), lax.axis_index("y"), lax.axis_index("c")


def _ag8(blk, *, name):
    r, c = blk.shape
    flips = [(dx, dy, dc) for dx in (0, 1) for dy in (0, 1) for dc in (0, 1) if (dx, dy, dc) != (0, 0, 0)]

    def body(x_ref, out_ref, send_sems, recv_sems, local_sem):
        ax, ay, ac = _place()
        me = 4 * ax + 2 * ay + ac
        mine = pltpu.make_async_copy(x_ref, out_ref.at[me], local_sem)
        mine.start()
        sent = []
        for k, (dx, dy, dc) in enumerate(flips):
            peer = (lax.rem(ax + dx, 2), lax.rem(ay + dy, 2), lax.rem(ac + dc, 2))
            cp = pltpu.make_async_remote_copy(src_ref=x_ref, dst_ref=out_ref.at[me], send_sem=send_sems.at[k],
                                              recv_sem=recv_sems.at[k], device_id=peer, device_id_type=MESH)
            cp.start()
            sent.append((cp, 4 * peer[0] + 2 * peer[1] + peer[2]))
        for k, (cp, pidx) in enumerate(sent):
            pltpu.make_async_remote_copy(src_ref=x_ref, dst_ref=out_ref.at[pidx], send_sem=send_sems.at[k],
                                         recv_sem=recv_sems.at[k], device_id=(ax, ay, ac),
                                         device_id_type=MESH).wait_recv()
        for cp, _ in sent:
            cp.wait_send()
        mine.wait()

    return _pcall(
        body, name=name,
        in_specs=[pl.BlockSpec(memory_space=pltpu.VMEM)],
        out_specs=pl.BlockSpec(memory_space=pltpu.VMEM),
        out_shape=jax.ShapeDtypeStruct((8, r, c), blk.dtype),
        scratch_shapes=[pltpu.SemaphoreType.DMA((7,)), pltpu.SemaphoreType.DMA((7,)), pltpu.SemaphoreType.DMA],
    )(blk)


def _chip_exchange(arrs, *, scatter, name):
    n = len(arrs)
    rel = [(1, 0), (0, 1), (1, 1)]

    def body(*refs):
        ins, outs = refs[:n], refs[n:2 * n]
        send_sems, recv_sems, local_sems = refs[2 * n:]
        ax, ay, ac = _place()
        s = 2 * ax + ay
        started, local = [], []
        for a in range(n):
            lcp = pltpu.make_async_copy(ins[a].at[s] if scatter else ins[a], outs[a].at[s], local_sems.at[a])
            lcp.start()
            local.append(lcp)
            for r, (dx, dy) in enumerate(rel):
                px, py = lax.rem(ax + dx, 2), lax.rem(ay + dy, 2)
                sp = 2 * px + py
                cp = pltpu.make_async_remote_copy(
                    src_ref=ins[a].at[sp] if scatter else ins[a], dst_ref=outs[a].at[s],
                    send_sem=send_sems.at[3 * a + r], recv_sem=recv_sems.at[3 * a + r],
                    device_id=(px, py, ac), device_id_type=MESH)
                cp.start()
                started.append((cp, a, r, sp))
        for cp, a, r, sp in started:
            pltpu.make_async_remote_copy(
                src_ref=ins[a].at[sp] if scatter else ins[a], dst_ref=outs[a].at[sp],
                send_sem=send_sems.at[3 * a + r], recv_sem=recv_sems.at[3 * a + r],
                device_id=(ax, ay, ac), device_id_type=MESH).wait_recv()
        for cp, _, _, _ in started:
            cp.wait_send()
        for lcp in local:
            lcp.wait()

    hbm = pl.BlockSpec(memory_space=pl.ANY)
    shapes = [jax.ShapeDtypeStruct(a.shape if scatter else (4,) + a.shape, a.dtype) for a in arrs]
    return _pcall(
        body, name=name, in_specs=[hbm] * n, out_specs=[hbm] * n, out_shape=shapes,
        scratch_shapes=[pltpu.SemaphoreType.DMA((3 * n,)), pltpu.SemaphoreType.DMA((3 * n,)),
                        pltpu.SemaphoreType.DMA((n,))],
    )(*arrs)


def _gather_weights(arrs, *, name):
    n = len(arrs)
    rel = [(1, 0), (0, 1), (1, 1)]

    def body(*refs):
        ins, outs = refs[:n], refs[n:2 * n]
        ici_send, ici_recv, d2d_send, d2d_recv, local_sems = refs[2 * n:]
        ax, ay, ac = _place()
        s = 2 * ax + ay
        sib = (ax, ay, 1 - ac)
        peers = [(lax.rem(ax + dx, 2), lax.rem(ay + dy, 2)) for dx, dy in rel]

        def half(a, slot, c):
            hr = arrs[a].shape[0] // 2
            return outs[a].at[slot, pl.ds(c * hr, hr), :]

        def ici(a, r, src, slot, to):
            return pltpu.make_async_remote_copy(src_ref=src, dst_ref=half(a, slot, ac), send_sem=ici_send.at[3 * a + r],
                                                recv_sem=ici_recv.at[3 * a + r], device_id=to, device_id_type=MESH)

        def d2d(a, r, slot, c):
            return pltpu.make_async_remote_copy(src_ref=half(a, slot, c), dst_ref=half(a, slot, c),
                                                send_sem=d2d_send.at[3 * a + r], recv_sem=d2d_recv.at[3 * a + r],
                                                device_id=sib, device_id_type=MESH)

        local, sent = [], []
        for a in range(n):
            lcp = pltpu.make_async_copy(ins[a], outs[a].at[s], local_sems.at[a])
            lcp.start()
            local.append(lcp)
            hr = arrs[a].shape[0] // 2
            for r, (px, py) in enumerate(peers):
                cp = ici(a, r, ins[a].at[pl.ds(ac * hr, hr), :], s, (px, py, ac))
                cp.start()
                sent.append(cp)
        for a in range(n):
            for r, (px, py) in enumerate(peers):
                sp = 2 * px + py
                ici(a, r, half(a, sp, ac), sp, (ax, ay, ac)).wait_recv()
                fw = d2d(a, r, sp, ac)
                fw.start()
                sent.append(fw)
        for a in range(n):
            for r, (px, py) in enumerate(peers):
                d2d(a, r, 2 * px + py, 1 - ac).wait_recv()
        for cp in sent:
            cp.wait_send()
        for lcp in local:
            lcp.wait()

    hbm = pl.BlockSpec(memory_space=pl.ANY)
    return _pcall(
        body, name=name, in_specs=[hbm] * n, out_specs=[hbm] * n,
        out_shape=[jax.ShapeDtypeStruct((4,) + a.shape, a.dtype) for a in arrs],
        scratch_shapes=[pltpu.SemaphoreType.DMA((3 * n,))] * 4 + [pltpu.SemaphoreType.DMA((n,))],
    )(*arrs)


def _swap_halves(arrs, *, name):
    n = len(arrs)

    def body(*refs):
        ins, mine, theirs = refs[:n], refs[n:2 * n], refs[2 * n:3 * n]
        send_sems, recv_sems, local_sems = refs[3 * n:]
        ax, ay, ac = _place()
        cps, local = [], []
        for a in range(n):
            hr = arrs[a].shape[1] // 2
            lcp = pltpu.make_async_copy(ins[a].at[:, pl.ds(ac * hr, hr), :], mine[a], local_sems.at[a])
            lcp.start()
            local.append(lcp)
            cp = pltpu.make_async_remote_copy(src_ref=ins[a].at[:, pl.ds((1 - ac) * hr, hr), :], dst_ref=theirs[a],
                                              send_sem=send_sems.at[a], recv_sem=recv_sems.at[a],
                                              device_id=(ax, ay, 1 - ac), device_id_type=MESH)
            cp.start()
            cps.append(cp)
        for cp in cps:
            cp.wait_recv()
        for cp in cps:
            cp.wait_send()
        for lcp in local:
            lcp.wait()

    hbm = pl.BlockSpec(memory_space=pl.ANY)
    shapes = [jax.ShapeDtypeStruct((4, a.shape[1] // 2, a.shape[2]), a.dtype) for a in arrs]
    res = _pcall(
        body, name=name, in_specs=[hbm] * n, out_specs=[hbm] * (2 * n), out_shape=shapes + shapes,
        scratch_shapes=[pltpu.SemaphoreType.DMA((n,))] * 3,
    )(*arrs)
    return res[:n], res[n:]


def _gather_halves(arrs, *, name):
    n = len(arrs)

    def body(*refs):
        ins, outs = refs[:n], refs[n:2 * n]
        send_sems, recv_sems, local_sems = refs[2 * n:]
        ax, ay, ac = _place()
        cps, local = [], []
        for a in range(n):
            hr = arrs[a].shape[0]
            dst = outs[a].at[pl.ds(ac * hr, hr), :]
            lcp = pltpu.make_async_copy(ins[a], dst, local_sems.at[a])
            lcp.start()
            local.append(lcp)
            cp = pltpu.make_async_remote_copy(src_ref=ins[a], dst_ref=dst, send_sem=send_sems.at[a],
                                              recv_sem=recv_sems.at[a], device_id=(ax, ay, 1 - ac),
                                              device_id_type=MESH)
            cp.start()
            cps.append(cp)
        for a, cp in enumerate(cps):
            hr = arrs[a].shape[0]
            pltpu.make_async_remote_copy(src_ref=ins[a], dst_ref=outs[a].at[pl.ds((1 - ac) * hr, hr), :],
                                         send_sem=send_sems.at[a], recv_sem=recv_sems.at[a],
                                         device_id=(ax, ay, ac), device_id_type=MESH).wait_recv()
        for cp in cps:
            cp.wait_send()
        for lcp in local:
            lcp.wait()

    hbm = pl.BlockSpec(memory_space=pl.ANY)
    return _pcall(
        body, name=name, in_specs=[hbm] * n, out_specs=[hbm] * n,
        out_shape=[jax.ShapeDtypeStruct((2 * a.shape[0], a.shape[1]), a.dtype) for a in arrs],
        scratch_shapes=[pltpu.SemaphoreType.DMA((n,))] * 3,
    )(*arrs)


def _mod_fwd(cond_raw, mw, mb, *, name):
    _, d, n = mw.shape

    def body(c_ref, w_ref, b_ref, o_ref):
        cv = c_ref[...]
        o_ref[...] = _dot(cv * _sigmoid(cv), w_ref[...]) + b_ref[...]

    return _pcall(
        body, name=name, grid=(2,),
        in_specs=[pl.BlockSpec((16, d), lambda l: (0, 0)), pl.BlockSpec((None, d, n), lambda l: (l, 0, 0)),
                  pl.BlockSpec((None, 1, n), lambda l: (l, 0, 0))],
        out_specs=pl.BlockSpec((None, 16, n), lambda l: (l, 0, 0)),
        out_shape=jax.ShapeDtypeStruct((2, 16, n), F32),
    )(cond_raw, mw, mb)


def _mod_bwd(cond_raw, dms, mw, *, name):
    _, d, n = mw.shape

    def body(c_ref, dm_ref, w_ref, gw_ref, dc_ref):
        @pl.when(pl.program_id(0) == 0)
        def _():
            dc_ref[...] = jnp.zeros_like(dc_ref)
        cv = c_ref[...]
        gw_ref[...] = _dot_tn(cv * _sigmoid(cv), dm_ref[...])
        dc_ref[...] += _dot_nt(dm_ref[...], w_ref[...])

    return _pcall(
        body, name=name, grid=(2,),
        in_specs=[pl.BlockSpec((16, d), lambda l: (0, 0)), pl.BlockSpec((None, 16, n), lambda l: (l, 0, 0)),
                  pl.BlockSpec((None, d, n), lambda l: (l, 0, 0))],
        out_specs=[pl.BlockSpec((None, d, n), lambda l: (l, 0, 0)), pl.BlockSpec((16, d), lambda l: (0, 0))],
        out_shape=[jax.ShapeDtypeStruct((2, d, n), F32), jax.ShapeDtypeStruct((16, d), F32)],
    )(cond_raw, dms, mw)


def _lb_fwd(hgrn_lb, *, name):
    def body(a_ref, o_ref):
        a0, a1 = a_ref[0:1, :], a_ref[1:2, :]
        m = jnp.maximum(a0, a1)
        e0, e1 = jnp.exp(a0 - m), jnp.exp(a1 - m)
        o_ref[...] = e0 / (e0 + e1)

    return _pcall(body, name=name, out_shape=jax.ShapeDtypeStruct((1, hgrn_lb.shape[1]), F32))(hgrn_lb)


PACK_ROWS = 40


def _small_finalize(gath, lb_pad, *, name):
    d = gath.shape[2]

    def body(g_ref, lb_ref, small_ref, glb_ref, gmb_ref, dm_ref):
        tot = g_ref[0]
        for e in range(1, 8):
            tot = tot + g_ref[e]
        for k in range(4):
            small_ref[k:k + 1, :] = tot[24 + 2 * k:25 + 2 * k, :] + tot[25 + 2 * k:26 + 2 * k, :]
        for k, r in ((4, 32), (5, 33)):
            v = tot[r:r + 1, :]
            small_ref[k:k + 1, :] = v + pltpu.roll(v, d - 64, 1)
        small_ref[6:7, :] = tot[34:35, :]
        small_ref[7:8, :] = tot[36:37, :]
        lbv = lb_ref[...]
        g0 = (tot[35:36, :] + tot[37:38, :]) * lbv * (1.0 - lbv)
        glb_ref[...] = jnp.zeros_like(glb_ref)
        glb_ref[0:1, :] = g0
        glb_ref[1:2, :] = -g0
        dm_ref[...] = jnp.zeros_like(dm_ref)
        for l in range(2):
            for part in range(6):
                rc, rl = l * 12 + part, l * 12 + 6 + part
                gmb_ref[l * 6 + part:l * 6 + part + 1, :] = tot[rc:rc + 1, :] + tot[rl:rl + 1, :]
                for e in range(8):
                    dm_ref[l, part, e:e + 1, :] = g_ref[e, rl:rl + 1, :]
                dm_ref[l, part, 8:9, :] = tot[rc:rc + 1, :]

    return _pcall(
        body, name=name,
        out_shape=[jax.ShapeDtypeStruct((8, d), F32), jax.ShapeDtypeStruct((8, d), F32),
                   jax.ShapeDtypeStruct((12, d), F32), jax.ShapeDtypeStruct((2, 6, 16, d), F32)],
    )(gath, lb_pad)


def _cctx_grad(gath, c_ctx2, *, name):
    def body(g_ref, c_ref, o_ref):
        tot = ((g_ref[0, 0:1, :] + g_ref[2, 0:1, :]) + g_ref[4, 0:1, :]) + g_ref[6, 0:1, :]
        cv = c_ref[...]
        s = _sigmoid(cv)
        o_ref[...] = tot * (s * (1.0 + cv * (1.0 - s)))

    return _pcall(body, name=name, out_shape=jax.ShapeDtypeStruct(c_ctx2.shape, F32))(gath, c_ctx2)


def _row_block(r, c, limit=256 * 1024):
    best = None
    for br in range(16, r + 1, 16):
        if r % br == 0 and br * c <= limit:
            best = br
    return best if best is not None else r


def _sum4(parts, *, name):
    _, r, c = parts.shape
    br = _row_block(r, c)

    def body(p_ref, o_ref):
        p = [p_ref[k].astype(F32) for k in range(4)]
        o_ref[...] = ((p[0] + p[1]) + p[2]) + p[3]

    return _pcall(body, name=name, grid=(r // br,),
                  in_specs=[pl.BlockSpec((4, br, c), lambda i: (0, i, 0))],
                  out_specs=pl.BlockSpec((br, c), lambda i: (i, 0)),
                  out_shape=jax.ShapeDtypeStruct((r, c), F32))(parts)


def _add2(a, b, *, name):
    r, c = a.shape
    br = _row_block(r, c)

    def body(a_ref, b_ref, o_ref):
        o_ref[...] = (a_ref[...].astype(F32) + b_ref[...].astype(F32)).astype(BF16)

    blk = pl.BlockSpec((br, c), lambda i: (i, 0))
    return _pcall(body, name=name, grid=(r // br,), in_specs=[blk, blk], out_specs=blk,
                  out_shape=jax.ShapeDtypeStruct((r, c), BF16))(a, b)


def _adam(w, gs, m, v, *, name):
    r, c = w.shape
    br = _row_block(r, c)
    ng = len(gs)
    c1 = 1.0 - ADAM_B1 ** ADAM_STEP
    c2 = 1.0 - ADAM_B2 ** ADAM_STEP

    def body(*refs):
        w_ref, m_ref, v_ref = refs[0], refs[1 + ng], refs[2 + ng]
        outs = refs[3 + ng:]
        g = refs[1][...]
        for k in range(1, ng):
            g = g + refs[1 + k][...]
        mn = ADAM_B1 * m_ref[...] + (1.0 - ADAM_B1) * g
        vn = ADAM_B2 * v_ref[...] + (1.0 - ADAM_B2) * (g * g)
        if ng > 1:
            outs[0][...] = g
        d_out, m_out, v_out = outs[-3:]
        m_out[...] = mn
        v_out[...] = vn
        d_out[...] = -ADAM_LR * ((mn / c1) / (jnp.sqrt(vn / c2) + ADAM_EPS) + ADAM_WD * w_ref[...])

    blk = pl.BlockSpec((br, c), lambda i: (i, 0))
    nout = 4 if ng > 1 else 3
    res = _pcall(body, name=name, grid=(r // br,), in_specs=[blk] * (3 + ng), out_specs=[blk] * nout,
                 out_shape=[jax.ShapeDtypeStruct((r, c), F32)] * nout)(w, *gs, m, v)
    return list(res) if ng > 1 else [gs[0]] + list(res)


def _to_shards(name, g):
    if name == 'ffn_in':
        l, k, n4 = g.shape
        return _ffn_deinterleave(g).reshape(l, k, 4, n4 // 4).transpose(2, 0, 1, 3).reshape(4, l * k, n4 // 4)
    if name == 'ffn_out':
        l, k4, n = g.shape
        return g.reshape(l, 4, k4 // 4, n).transpose(1, 0, 2, 3).reshape(4, l * k4 // 4, n)
    if name in ('even_in', 'odd_in'):
        k, n4 = g.shape
        return g.reshape(k, 4, n4 // 4).transpose(1, 0, 2)
    k4, n = g.shape
    return g.reshape(4, k4 // 4, n)


def _from_shards(name, g):
    _, r, n = g.shape
    if name == 'ffn_in':
        return _ffn_interleave(g.reshape(4, 2, r // 2, n).transpose(1, 2, 0, 3).reshape(2, r // 2, 4 * n))
    if name == 'ffn_out':
        return g.reshape(4, 2, r // 2, n).transpose(1, 0, 2, 3).reshape(2, 2 * r, n)
    if name in ('even_in', 'odd_in'):
        return g.transpose(1, 0, 2).reshape(r, 4 * n)
    return g.reshape(4 * r, n)


def kernel(x, c, ctx, c_ctx, mod_w, mod_b, norm_g, ffn_w_in, ffn_w_out, even_w_in, even_w_out, attn_qk_norm_g, attn_sink, hgrn_out_norm_g, hgrn_lb, odd_w_in, odd_w_out, loss_target, m_c_ctx, m_mod_w, m_mod_b, m_norm_g, m_ffn_w_in, m_ffn_w_out, m_even_w_in, m_even_w_out, m_attn_qk_norm_g, m_attn_sink, m_hgrn_out_norm_g, m_hgrn_lb, m_odd_w_in, m_odd_w_out, v_c_ctx, v_mod_w, v_mod_b, v_norm_g, v_ffn_w_in, v_ffn_w_out, v_even_w_in, v_even_w_out, v_attn_qk_norm_g, v_attn_sink, v_hgrn_out_norm_g, v_hgrn_lb, v_odd_w_in, v_odd_w_out):
    d = x.shape[-1]
    lc = ctx.shape[1]
    assert lc == TM and d == 1024
    ax, ay, ac = _place()
    s = 2 * ax + ay
    me = 4 * ax + 2 * ay + ac
    nmod = mod_w.shape[2]

    def pad8(v):
        return jnp.pad(v, ((0, 8 - v.shape[0]), (0, 0)))

    pack = jnp.concatenate([pad8(c), pad8(norm_g.reshape(1, d))], axis=0)
    g1 = _ag8(pack, name='gather_cond')
    c_all = g1[:, 0, :]
    ng = g1[0::2, 8, :].reshape(4, 2, 2, d // 4).transpose(1, 2, 0, 3).reshape(4, d)

    names = ['ffn_in', 'ffn_out', 'even_in', 'even_out', 'odd_in', 'odd_out']
    shards = [_bf(v.reshape(-1, v.shape[-1])) for v in (ffn_w_in, ffn_w_out, even_w_in, even_w_out, odd_w_in, odd_w_out)]
    gathered = _gather_weights(shards, name='gather_weights')
    w = {nm: _from_shards(nm, g) for nm, g in zip(names, gathered)}

    cond_raw = jnp.concatenate([c_all, pad8(c_ctx.reshape(1, d))], axis=0)
    mb_sh = lax.dynamic_slice_in_dim(mod_b, s * nmod, nmod, axis=1).reshape(2, 1, nmod)
    mpart = _mod_fwd(cond_raw, mod_w, mb_sh, name='mod_fwd')
    g3 = _ag8(mpart.reshape(32, nmod), name='gather_mods')
    mods_full = g3[0::2].reshape(4, 2, 16, nmod).transpose(1, 2, 0, 3).reshape(2, 16, 4 * nmod)
    m_lat = lax.dynamic_index_in_dim(mods_full, me, axis=1, keepdims=False)
    mods = jnp.stack([mods_full[:, 8], m_lat], axis=1).reshape(24, d)

    lb = _lb_fwd(hgrn_lb, name='hgrn_lower_bound')
    small = dict(gq=jnp.tile(attn_qk_norm_g[0, 0], 2).reshape(1, 128), gk=jnp.tile(attn_qk_norm_g[0, 1], 2).reshape(1, 128),
                 sink=attn_sink[0], gain=hgrn_out_norm_g, lb=lb)
    x0 = jnp.concatenate([ctx[0], x[0]], axis=0)
    loss_t, dx0, grads, sums = _local_step(x0, loss_target[0], mods, ng, w, small)
    loss = lax.psum(loss_t[0, 0], ("x", "y", "c"))
    grad_x = dx0[lc:][None]

    def pad(v):
        return jnp.pad(v, ((0, 0), (0, d - v.shape[1])))

    sm = sums
    dm_rows = [
        [sm['l0n1'][0], sm['l0n1'][1], sm['l0n2'][2], sm['l0n2'][0], sm['l0n2'][1], sm['l1n1'][2]],
        [sm['l0n1'][4], sm['l0n1'][5], sm['l0n2'][6], sm['l0n2'][4], sm['l0n2'][5], sm['l1n1'][6]],
        [sm['l1n1'][0], sm['l1n1'][1], sm['l1n2'][2], sm['l1n2'][0], sm['l1n2'][1], sm['fin'][2]],
        [sm['l1n1'][4], sm['l1n1'][5], sm['l1n2'][6], sm['l1n2'][4], sm['l1n2'][5], sm['fin'][6]],
    ]
    rows = [r for grp in dm_rows for r in grp]
    for key in ('l0n1', 'l0n2', 'l1n1', 'l1n2'):
        rows += [sm[key][3], sm[key][7]]
    rows = jnp.stack(rows)
    extra = jnp.concatenate([pad(sm['gq'][0:1]), pad(sm['gk'][0:1]), pad(sm['gain'][0:1]), pad(sm['dlb']),
                             pad(sm['sink'][:, 0].reshape(1, 8)), jnp.zeros((3, d), F32)], axis=0)
    g4 = _ag8(jnp.concatenate([rows, extra], axis=0), name='gather_row_sums')
    small_g, glb, gmb, dmat = _small_finalize(g4, pad(lb), name='small_grads')
    dms = lax.dynamic_slice_in_dim(dmat.transpose(0, 2, 1, 3).reshape(2, 16, 6 * d), s * nmod, nmod, axis=2)
    g_mod_w, dcond = _mod_bwd(cond_raw, dms, mod_w, name='mod_bwd')
    g5 = _ag8(dcond[8:16], name='gather_dcond')
    g_c_ctx = _cctx_grad(g5, c_ctx.reshape(8, d // 8).reshape(1, d), name='c_ctx_grad')

    mine, theirs = _swap_halves([_bf(_to_shards(nm, grads[nm])) for nm in names], name='swap_core_halves')
    pair = [_add2(a.reshape(-1, a.shape[-1]), b.reshape(-1, b.shape[-1]), name='add_cores').reshape(a.shape)
            for a, b in zip(mine, theirs)]
    parts = _chip_exchange(pair, scatter=True, name='scatter_grads')
    full = _gather_halves([_sum4(p, name='sum_chips') for p in parts], name='gather_core_halves')

    def upd(wv, gs, mv, vv, name):
        shp = wv.shape
        c2 = shp[-1]
        out = _adam(wv.reshape(-1, c2), [g.reshape(-1, c2) for g in gs], mv.reshape(-1, c2), vv.reshape(-1, c2), name=name)
        return [o.reshape(shp) for o in out]

    res = {}
    res['c_ctx'] = upd(c_ctx.reshape(8, d // 8), [g_c_ctx.reshape(8, d // 8)], m_c_ctx.reshape(8, d // 8), v_c_ctx.reshape(8, d // 8), 'adam_c_ctx')
    res['c_ctx'] = [o.reshape(d) for o in res['c_ctx']]
    res['mod_w'] = upd(mod_w, [g_mod_w], m_mod_w, v_mod_w, 'adam_mod_w')
    res['mod_b'] = upd(mod_b, [gmb.reshape(2, 6 * d)], m_mod_b, v_mod_b, 'adam_mod_b')
    g_ng = lax.dynamic_slice_in_dim(small_g[0:4].reshape(2, 2, d), s * (d // 4), d // 4, axis=2)
    res['norm_g'] = upd(norm_g, [g_ng], m_norm_g, v_norm_g, 'adam_norm_g')
    big = {nm: [g] for nm, g in zip(names, full)}
    res['ffn_w_in'] = upd(ffn_w_in, big['ffn_in'], m_ffn_w_in, v_ffn_w_in, 'adam_ffn_in')
    res['ffn_w_out'] = upd(ffn_w_out, big['ffn_out'], m_ffn_w_out, v_ffn_w_out, 'adam_ffn_out')
    res['even_w_in'] = upd(even_w_in, big['even_in'], m_even_w_in, v_even_w_in, 'adam_even_in')
    res['even_w_out'] = upd(even_w_out, big['even_out'], m_even_w_out, v_even_w_out, 'adam_even_out')
    g_qk = jnp.stack([small_g[4, 0:64], small_g[5, 0:64]]).reshape(1, 2, 64)
    res['attn_qk_norm_g'] = upd(attn_qk_norm_g, [g_qk], m_attn_qk_norm_g, v_attn_qk_norm_g, 'adam_qk_gain')
    res['attn_sink'] = upd(attn_sink, [small_g[7, 0:8].reshape(1, 8)], m_attn_sink, v_attn_sink, 'adam_sink')
    res['hgrn_out_norm_g'] = upd(hgrn_out_norm_g, [small_g[6, 0:128].reshape(1, 128)], m_hgrn_out_norm_g, v_hgrn_out_norm_g, 'adam_head_gain')
    res['hgrn_lb'] = upd(hgrn_lb, [glb[0:2, 0:hgrn_lb.shape[1]]], m_hgrn_lb, v_hgrn_lb, 'adam_hgrn_lb')
    res['odd_w_in'] = upd(odd_w_in, big['odd_in'], m_odd_w_in, v_odd_w_in, 'adam_odd_in')
    res['odd_w_out'] = upd(odd_w_out, big['odd_out'], m_odd_w_out, v_odd_w_out, 'adam_odd_out')

    order = ['c_ctx', 'mod_w', 'mod_b', 'norm_g', 'ffn_w_in', 'ffn_w_out', 'even_w_in', 'even_w_out',
             'attn_qk_norm_g', 'attn_sink', 'hgrn_out_norm_g', 'hgrn_lb', 'odd_w_in', 'odd_w_out']
    outs = [loss, grad_x]
    for k in range(4):
        outs += [res[nm][k] for nm in order]
    return tuple(outs)
```

```python
import functools
import math

import numpy as np
import jax
import jax.numpy as jnp
from jax import lax
from jax.experimental import pallas as pl
from jax.experimental.pallas import tpu as pltpu

F32 = jnp.float32
BF16 = jnp.bfloat16
EPS = 1e-6
TM = 256
CHUNK = 64
QB = 128
WINDOW = 128
NEG = -1e30
MESH = pl.DeviceIdType.MESH

ADAM_LR, ADAM_B1, ADAM_B2, ADAM_EPS, ADAM_WD, ADAM_STEP = 0.001, 0.9, 0.999, 1e-08, 0.01, 10


def _pcall(body, **kw):
    return pl.pallas_call(body, **kw)


def _pick(n, cap):
    best = None
    for m in range(128, min(n, cap) + 1, 128):
        if n % m == 0:
            best = m
    assert best is not None, (n, cap)
    return best


def _bf(x):
    return x.astype(BF16)


def _dot(a, b):
    return jnp.dot(_bf(a), _bf(b), preferred_element_type=F32)


def _dot_nt(a, b):
    return lax.dot_general(_bf(a), _bf(b), (((1,), (1,)), ((), ())), preferred_element_type=F32)


def _dot_tn(a, b):
    return lax.dot_general(_bf(a), _bf(b), (((0,), (0,)), ((), ())), preferred_element_type=F32)


def _dot_exact(a, b):
    return jnp.dot(a, b, preferred_element_type=F32, precision=lax.Precision.HIGHEST)


def _sigmoid(x):
    return 1.0 / (1.0 + jnp.exp(-x))


def _iota(shape, dim):
    return lax.broadcasted_iota(jnp.int32, shape, dim)


def _mm_nn(a, b, *, lead=None, out_dtype=F32, name):
    m, k = a.shape
    n = b.shape[-1]
    bm = 768 if m % 768 == 0 else TM
    bn = _pick(n, 1024)

    def body(a_ref, b_ref, o_ref):
        o_ref[...] = _dot(a_ref[...], b_ref[...]).astype(o_ref.dtype)

    if lead is None:
        b_spec = pl.BlockSpec((k, bn), lambda i, j: (0, j))
    else:
        b_spec = pl.BlockSpec((None, k, bn), lambda i, j: (lead, 0, j))
    return _pcall(
        body, name=name, grid=(m // bm, n // bn),
        in_specs=[pl.BlockSpec((bm, k), lambda i, j: (i, 0)), b_spec],
        out_specs=pl.BlockSpec((bm, bn), lambda i, j: (i, j)),
        out_shape=jax.ShapeDtypeStruct((m, n), out_dtype),
    )(a, b)


def _mm_nt(a, b, *, lead=None, name):
    m, n = a.shape
    k = b.shape[-2]
    bm = 768 if m % 768 == 0 else TM
    bk = _pick(k, 512)

    def body(a_ref, b_ref, o_ref):
        o_ref[...] = _dot_nt(a_ref[...], b_ref[...])

    if lead is None:
        b_spec = pl.BlockSpec((bk, n), lambda i, j: (j, 0))
    else:
        b_spec = pl.BlockSpec((None, bk, n), lambda i, j: (lead, j, 0))
    return _pcall(
        body, name=name, grid=(m // bm, k // bk),
        in_specs=[pl.BlockSpec((bm, n), lambda i, j: (i, 0)), b_spec],
        out_specs=pl.BlockSpec((bm, bk), lambda i, j: (i, j)),
        out_shape=jax.ShapeDtypeStruct((m, k), F32),
    )(a, b)


def _mm_tn(a, b, *, name):
    t, k = a.shape
    n = b.shape[1]
    bt = 768 if t % 768 == 0 else TM
    bk = _pick(k, 1536)
    bn = _pick(n, 1024) if n % 1024 == 0 or n < 1664 else _pick(n, 1664)

    def body(a_ref, b_ref, o_ref):
        @pl.when(pl.program_id(2) == 0)
        def _():
            o_ref[...] = jnp.zeros_like(o_ref)
        o_ref[...] += _dot_tn(a_ref[...], b_ref[...])

    return _pcall(
        body, name=name, grid=(k // bk, n // bn, t // bt),
        in_specs=[pl.BlockSpec((bt, bk), lambda i, j, s: (s, i)),
                  pl.BlockSpec((bt, bn), lambda i, j, s: (s, j))],
        out_specs=pl.BlockSpec((bk, bn), lambda i, j, s: (i, j)),
        out_shape=jax.ShapeDtypeStruct((k, n), F32),
    )(a, b)


def _mod_row(mods_ref, lat, idx):
    return jnp.where(lat, mods_ref[idx + 6:idx + 7, :], mods_ref[idx:idx + 1, :])


def _row_fwd(x, mods, *, y=None, gate=None, g=None, shift=None, scale=None, name):
    t, d = x.shape
    has_y, has_n = y is not None, g is not None

    def body(*refs):
        refs = list(refs)
        x_ref, mods_ref = refs[0], refs[1]
        pos = 2
        if has_y:
            y_ref = refs[pos]; pos += 1
        if has_n:
            g_ref = refs[pos]; pos += 1
        outs = refs[pos:]
        lat = pl.program_id(0) > 0
        x1 = x_ref[...]
        o = 0
        if has_y:
            x1 = x1 + _mod_row(mods_ref, lat, gate) * y_ref[...]
            outs[o][...] = x1; o += 1
        if has_n:
            rs = lax.rsqrt(jnp.mean(x1 * x1, axis=-1, keepdims=True) + EPS)
            hn = x1 * rs * g_ref[...]
            h = hn * (1.0 + _mod_row(mods_ref, lat, scale)) + _mod_row(mods_ref, lat, shift)
            outs[o][...] = h.astype(BF16)

    row = pl.BlockSpec((TM, d), lambda i: (i, 0))
    ins, specs = [x, mods], [row, pl.BlockSpec(mods.shape, lambda i: (0, 0))]
    if has_y:
        ins.append(y); specs.append(row)
    if has_n:
        ins.append(g.reshape(1, d)); specs.append(pl.BlockSpec((1, d), lambda i: (0, 0)))
    out_shape, out_specs = [], []
    if has_y:
        out_shape.append(jax.ShapeDtypeStruct((t, d), F32)); out_specs.append(row)
    if has_n:
        out_shape.append(jax.ShapeDtypeStruct((t, d), BF16)); out_specs.append(row)
    res = _pcall(body, name=name, grid=(t // TM,), in_specs=specs, out_specs=out_specs,
                 out_shape=out_shape)(*ins)
    return res


def _acc_row(ref, r, val):
    ref[r:r + 1, :] += val


def _row_final(x, z, mods, target, *, gate, name):
    t, d = x.shape

    def body(x_ref, mods_ref, z_ref, t_ref, loss_ref, dx_ref, dz_ref, sums_ref):
        i = pl.program_id(0)
        lat = i > 0

        @pl.when(i == 0)
        def _():
            loss_ref[...] = jnp.zeros_like(loss_ref)
            sums_ref[...] = jnp.zeros_like(sums_ref)

        gt = _mod_row(mods_ref, lat, gate)
        zz = z_ref[...]
        yv = x_ref[...] + gt * zz
        keep = jnp.where(lat, 1.0, 0.0).astype(F32)
        diff = (yv - t_ref[...]) * keep
        part = jnp.sum(jnp.sum(diff * diff, axis=0, keepdims=True), axis=1, keepdims=True)
        loss_ref[...] += part * (0.5 / d)
        dy = diff * (1.0 / d)
        dx_ref[...] = dy
        dz_ref[...] = (gt * dy).astype(BF16)
        _acc_row(sums_ref, 6, jnp.sum(dy * zz, axis=0, keepdims=True))

    row = pl.BlockSpec((TM, d), lambda i: (i, 0))
    return _pcall(
        body, name=name, grid=(t // TM,),
        in_specs=[row, pl.BlockSpec(mods.shape, lambda i: (0, 0)), row,
                  pl.BlockSpec((TM, d), lambda i: (jnp.maximum(i - 1, 0), 0))],
        out_specs=[pl.BlockSpec((8, 128), lambda i: (0, 0)), row, row,
                   pl.BlockSpec((8, d), lambda i: (0, 0))],
        out_shape=[jax.ShapeDtypeStruct((8, 128), F32), jax.ShapeDtypeStruct((t, d), F32),
                   jax.ShapeDtypeStruct((t, d), BF16), jax.ShapeDtypeStruct((8, d), F32)],
    )(x, mods, z, target)


def _row_bwd(xn, dxo, dh, mods, g, *, shift, scale, y=None, gate=None, name):
    t, d = xn.shape
    has_y = y is not None

    def body(*refs):
        refs = list(refs)
        x_ref, dxo_ref, dh_ref, mods_ref, g_ref = refs[:5]
        pos = 5
        if has_y:
            y_ref = refs[pos]; pos += 1
        dx_ref = refs[pos]; pos += 1
        if has_y:
            dy_ref = refs[pos]; pos += 1
        sums_ref = refs[pos]
        i = pl.program_id(0)
        lat = i > 0

        @pl.when(i == 0)
        def _():
            sums_ref[...] = jnp.zeros_like(sums_ref)

        x1 = x_ref[...]
        gv = g_ref[...]
        rs = lax.rsqrt(jnp.mean(x1 * x1, axis=-1, keepdims=True) + EPS)
        xh = x1 * rs
        dhv = dh_ref[...]
        dn = dhv * (1.0 + _mod_row(mods_ref, lat, scale))
        dxh = dn * gv
        dx = dxo_ref[...] + rs * (dxh - xh * jnp.mean(dxh * xh, axis=-1, keepdims=True))
        dx_ref[...] = dx
        vals = [jnp.sum(dhv, axis=0, keepdims=True),
                jnp.sum(dhv * (xh * gv), axis=0, keepdims=True),
                None,
                jnp.sum(dn * xh, axis=0, keepdims=True)]
        if has_y:
            dy_ref[...] = (_mod_row(mods_ref, lat, gate) * dx).astype(BF16)
            vals[2] = jnp.sum(dx * y_ref[...], axis=0, keepdims=True)

        @pl.when(i == 0)
        def _():
            for r, v in enumerate(vals):
                if v is not None:
                    _acc_row(sums_ref, r, v)

        @pl.when(i > 0)
        def _():
            for r, v in enumerate(vals):
                if v is not None:
                    _acc_row(sums_ref, 4 + r, v)

    row = pl.BlockSpec((TM, d), lambda i: (i, 0))
    ins = [xn, dxo, dh, mods, g.reshape(1, d)]
    specs = [row, row, row, pl.BlockSpec(mods.shape, lambda i: (0, 0)), pl.BlockSpec((1, d), lambda i: (0, 0))]
    out_shape, out_specs = [jax.ShapeDtypeStruct((t, d), F32)], [row]
    if has_y:
        ins.append(y); specs.append(row)
        out_shape.append(jax.ShapeDtypeStruct((t, d), BF16)); out_specs.append(row)
    out_shape.append(jax.ShapeDtypeStruct((8, d), F32))
    out_specs.append(pl.BlockSpec((8, d), lambda i: (0, 0)))
    return _pcall(body, name=name, grid=(t // TM,), in_specs=specs, out_specs=out_specs,
                  out_shape=out_shape)(*ins)


FFN_BK = 256


def _ffn_interleave(w):
    *lead, k, n2 = w.shape
    nb = n2 // (2 * FFN_BK)
    return jnp.swapaxes(w.reshape(*lead, k, 2, nb, FFN_BK), -3, -2).reshape(*lead, k, n2)


def _ffn_deinterleave(w):
    *lead, k, n2 = w.shape
    nb = n2 // (2 * FFN_BK)
    return jnp.swapaxes(w.reshape(*lead, k, nb, 2, FFN_BK), -3, -2).reshape(*lead, k, n2)


def _big_tile(t):
    for bm in (2816, 768):
        if t % bm == 0:
            return bm
    return TM


def _ffn_in(h, w, *, lead, name):
    t, d = h.shape
    n2 = w.shape[-1]
    bm, bk = _big_tile(t), FFN_BK

    def body(h_ref, w_ref, u_ref, a_ref):
        ub = _dot(h_ref[...], w_ref[...]).astype(BF16)
        u_ref[...] = ub
        uf = ub.astype(F32)
        gv, up = uf[:, 0:bk], uf[:, bk:2 * bk]
        a_ref[...] = (gv * _sigmoid(gv) * up).astype(BF16)

    return _pcall(
        body, name=name, grid=(t // bm, n2 // (2 * bk)),
        in_specs=[pl.BlockSpec((bm, d), lambda i, j: (i, 0)),
                  pl.BlockSpec((None, d, 2 * bk), lambda i, j: (lead, 0, j))],
        out_specs=[pl.BlockSpec((bm, 2 * bk), lambda i, j: (i, j)), pl.BlockSpec((bm, bk), lambda i, j: (i, j))],
        out_shape=[jax.ShapeDtypeStruct((t, n2), BF16), jax.ShapeDtypeStruct((t, n2 // 2), BF16)],
    )(h, w)


def _ffn_dx(dz, w_out, u, *, lead, name):
    t, d = dz.shape
    n2 = u.shape[1]
    bm, bk = _big_tile(t), FFN_BK

    def body(dz_ref, w_ref, u_ref, du_ref):
        da = _dot_nt(dz_ref[...], w_ref[...])
        uf = u_ref[...].astype(F32)
        gv, up = uf[:, 0:bk], uf[:, bk:2 * bk]
        s = _sigmoid(gv)
        du_ref[:, 0:bk] = (da * up * (s * (1.0 + gv * (1.0 - s)))).astype(BF16)
        du_ref[:, bk:2 * bk] = (da * gv * s).astype(BF16)

    ublk = pl.BlockSpec((bm, 2 * bk), lambda i, j: (i, j))
    return _pcall(
        body, name=name, grid=(t // bm, n2 // (2 * bk)),
        in_specs=[pl.BlockSpec((bm, d), lambda i, j: (i, 0)),
                  pl.BlockSpec((None, bk, d), lambda i, j: (lead, j, 0)), ublk],
        out_specs=ublk, out_shape=jax.ShapeDtypeStruct((t, n2), BF16),
    )(dz, w_out, u)


def _lane(shape):
    return _iota(shape, len(shape) - 1)


def _pair_norm(x, g):
    lo = _lane(x.shape) < 64
    x2 = x * x
    s_lo = jnp.sum(jnp.where(lo, x2, 0.0), axis=-1, keepdims=True)
    s_hi = jnp.sum(jnp.where(lo, 0.0, x2), axis=-1, keepdims=True)
    rs = lax.rsqrt(jnp.where(lo, s_lo, s_hi) * (1.0 / 64) + EPS)
    return x * rs, rs


def _pair_mean(v):
    lo = _lane(v.shape) < 64
    s_lo = jnp.sum(jnp.where(lo, v, 0.0), axis=-1, keepdims=True)
    s_hi = jnp.sum(jnp.where(lo, 0.0, v), axis=-1, keepdims=True)
    return jnp.where(lo, s_lo, s_hi) * (1.0 / 64)


def _rot64(x):
    r1 = pltpu.roll(x, 32, 1)
    r2 = pltpu.roll(x, 96, 1)
    even = ((_lane(x.shape) >> 5) & 1) == 0
    return jnp.where(even, -r2, r1)


def _rope64(x, cos, sin):
    return x * cos + _rot64(x) * sin


def _rope64_t(d, cos, sin):
    return d * cos - _rot64(d * sin)


def _kprep_fwd(p, gk, cos, sin, *, name):
    t = p.shape[0]

    def body(k_ref, g_ref, c_ref, s_ref, o_ref):
        xh, _ = _pair_norm(k_ref[...], None)
        o_ref[...] = _rope64(xh * g_ref[...], c_ref[...], s_ref[...])

    blk = pl.BlockSpec((TM, 128), lambda i: (i, 0))
    return _pcall(
        body, name=name, grid=(t // TM,),
        in_specs=[pl.BlockSpec((TM, 128), lambda i: (i, 4)), pl.BlockSpec((1, 128), lambda i: (0, 0)), blk, blk],
        out_specs=blk, out_shape=jax.ShapeDtypeStruct((t, 128), F32),
    )(p, gk, cos, sin)


def _kprep_bwd(p, gk, cos, sin, dkp, dv, *, name):
    t = p.shape[0]

    def body(k_ref, g_ref, c_ref, s_ref, dkp_ref, dv_ref, o_ref, dg_ref):
        @pl.when(pl.program_id(0) == 0)
        def _():
            dg_ref[...] = jnp.zeros_like(dg_ref)
        xh, rs = _pair_norm(k_ref[...], None)
        dn = _rope64_t(dkp_ref[...], c_ref[...], s_ref[...])
        _acc_row(dg_ref, 0, jnp.sum(dn * xh, axis=0, keepdims=True))
        dxh = dn * g_ref[...]
        o_ref[:, 0:128] = (rs * (dxh - xh * _pair_mean(dxh * xh))).astype(BF16)
        o_ref[:, 128:256] = dv_ref[...].astype(BF16)

    blk = pl.BlockSpec((TM, 128), lambda i: (i, 0))
    return _pcall(
        body, name=name, grid=(t // TM,),
        in_specs=[pl.BlockSpec((TM, 128), lambda i: (i, 4)), pl.BlockSpec((1, 128), lambda i: (0, 0)), blk, blk, blk, blk],
        out_specs=[pl.BlockSpec((TM, 256), lambda i: (i, 0)), pl.BlockSpec((8, 128), lambda i: (0, 0))],
        out_shape=[jax.ShapeDtypeStruct((t, 256), BF16), jax.ShapeDtypeStruct((8, 128), F32)],
    )(p, gk, cos, sin, dkp, dv)


def _attn_common(i, t, lc, kp_ref, v_ref):
    span = QB + 2 * WINDOW
    start = pl.multiple_of(jnp.clip((i - 1) * QB, lc, t - span), QB)
    kall = jnp.concatenate([kp_ref[0:lc, :], kp_ref[pl.ds(start, span), :]], axis=0)
    vall = jnp.concatenate([v_ref[0:lc, :], v_ref[pl.ds(start, span), :]], axis=0)
    nk = lc + span
    col = _iota((QB, nk), 1)
    krow = jnp.where(col < lc, col, start + col - lc)
    qrow = i * QB + _iota((QB, nk), 0)
    valid = (col < lc) | ((qrow >= lc) & (krow >= lc) & (jnp.abs(krow - qrow) <= WINDOW))
    lo = _lane(kall.shape) < 64
    kroll, vroll = pltpu.roll(kall, 64, 1), pltpu.roll(vall, 64, 1)
    zero = jnp.zeros_like(kall)
    kvar = [[_bf(jnp.where(lo, kall, zero)), _bf(jnp.where(lo, zero, kroll))],
            [_bf(jnp.where(lo, kroll, zero)), _bf(jnp.where(lo, zero, kall))]]
    vvar = [[_bf(jnp.where(lo, vall, zero)), _bf(jnp.where(lo, zero, vroll))],
            [_bf(jnp.where(lo, vroll, zero)), _bf(jnp.where(lo, zero, vall))]]
    return start, valid, kvar, vvar


def _softmax_sink(s, valid, snk):
    s = jnp.where(valid, s, NEG)
    m = jnp.maximum(jnp.max(s, axis=-1, keepdims=True), snk)
    e = jnp.exp(s - m)
    es = jnp.exp(snk - m)
    inv = 1.0 / (jnp.sum(e, axis=-1, keepdims=True) + es)
    return e * inv, es * inv


def _attn_fwd(p, kp, gq, sink, cos, sin, *, lc, name):
    t = p.shape[0]
    scale = 64 ** -0.5

    def body(q_ref, kp_ref, v_ref, g_ref, sink_ref, c_ref, s_ref, o_ref):
        i = pl.program_id(0)
        _, valid, kvar, vvar = _attn_common(i, t, lc, kp_ref, v_ref)
        cosv, sinv, gv = c_ref[...], s_ref[...], g_ref[...]
        for j in range(4):
            xh, _ = _pair_norm(q_ref[:, 128 * j:128 * j + 128], None)
            q2 = _bf(_rope64(xh * gv, cosv, sinv))
            acc = jnp.zeros((QB, 128), F32)
            for half in range(2):
                s = _dot_nt(q2, kvar[j // 2][half]) * scale
                pr, _ = _softmax_sink(s, valid, sink_ref[2 * j + half])
                acc = acc + _dot(pr, vvar[j // 2][half])
            o_ref[:, 128 * j:128 * j + 128] = acc.astype(BF16)

    qblk = pl.BlockSpec((QB, 128), lambda i: (i, 0))
    return _pcall(
        body, name=name, grid=(t // QB,),
        in_specs=[pl.BlockSpec((QB, 512), lambda i: (i, 0)),
                  pl.BlockSpec((t, 128), lambda i: (0, 0)),
                  pl.BlockSpec((t, 128), lambda i: (0, 5)),
                  pl.BlockSpec((1, 128), lambda i: (0, 0)),
                  pl.BlockSpec(memory_space=pltpu.SMEM), qblk, qblk],
        out_specs=pl.BlockSpec((QB, 512), lambda i: (i, 0)),
        out_shape=jax.ShapeDtypeStruct((t, 512), BF16),
    )(p, kp, p, gq, sink, cos, sin)


def _attn_bwd(p, kp, gq, sink, cos, sin, dmix, *, lc, name):
    t = p.shape[0]
    scale = 64 ** -0.5
    span = QB + 2 * WINDOW

    def body(q_ref, kp_ref, v_ref, g_ref, sink_ref, c_ref, s_ref, do_ref,
             dq_ref, dk_ref, dv_ref, dg_ref, dsink_ref):
        i = pl.program_id(0)

        @pl.when(i == 0)
        def _():
            dk_ref[...] = jnp.zeros_like(dk_ref)
            dv_ref[...] = jnp.zeros_like(dv_ref)
            dg_ref[...] = jnp.zeros_like(dg_ref)
            dsink_ref[...] = jnp.zeros_like(dsink_ref)

        start, valid, kvar, vvar = _attn_common(i, t, lc, kp_ref, v_ref)
        cosv, sinv, gv = c_ref[...], s_ref[...], g_ref[...]
        nk = lc + span
        lo = _lane((nk, 128)) < 64
        dk_all = jnp.zeros((nk, 128), F32)
        dv_all = jnp.zeros((nk, 128), F32)
        for j in range(4):
            kvh = j // 2
            xh, rs = _pair_norm(q_ref[:, 128 * j:128 * j + 128], None)
            q2 = _bf(_rope64(xh * gv, cosv, sinv))
            do2 = _bf(do_ref[:, 128 * j:128 * j + 128])
            dq2 = jnp.zeros((QB, 128), F32)
            for half in range(2):
                s = _dot_nt(q2, kvar[kvh][half]) * scale
                pr, ps = _softmax_sink(s, valid, sink_ref[2 * j + half])
                dp = _dot_nt(do2, vvar[kvh][half])
                delta = jnp.sum(pr * dp, axis=-1, keepdims=True)
                ds = pr * (dp - delta) * scale
                dsk = jnp.sum(jnp.sum(-ps * delta, axis=0, keepdims=True), axis=1, keepdims=True)
                _acc_row(dsink_ref, 2 * j + half, jnp.broadcast_to(dsk, (1, 128)))
                dq2 = dq2 + _dot(ds, kvar[kvh][half])
                gk_ = _dot_tn(ds, q2)
                gv_ = _dot_tn(pr, do2)
                if half == 0:
                    gk_, gv_ = jnp.where(lo, gk_, 0.0), jnp.where(lo, gv_, 0.0)
                else:
                    gk_, gv_ = jnp.where(lo, 0.0, gk_), jnp.where(lo, 0.0, gv_)
                if half != kvh:
                    gk_, gv_ = pltpu.roll(gk_, 64, 1), pltpu.roll(gv_, 64, 1)
                dk_all = dk_all + gk_
                dv_all = dv_all + gv_
            dn = _rope64_t(dq2, cosv, sinv)
            _acc_row(dg_ref, 0, jnp.sum(dn * xh, axis=0, keepdims=True))
            dxh = dn * gv
            dq_ref[:, 128 * j:128 * j + 128] = (rs * (dxh - xh * _pair_mean(dxh * xh))).astype(BF16)
        dk_ref[0:lc, :] += dk_all[0:lc]
        dv_ref[0:lc, :] += dv_all[0:lc]
        dk_ref[pl.ds(start, span), :] += dk_all[lc:nk]
        dv_ref[pl.ds(start, span), :] += dv_all[lc:nk]

    qblk = pl.BlockSpec((QB, 128), lambda i: (i, 0))
    full = pl.BlockSpec((t, 128), lambda i: (0, 0))
    small = pl.BlockSpec((8, 128), lambda i: (0, 0))
    return _pcall(
        body, name=name, grid=(t // QB,),
        in_specs=[pl.BlockSpec((QB, 512), lambda i: (i, 0)), full,
                  pl.BlockSpec((t, 128), lambda i: (0, 5)),
                  pl.BlockSpec((1, 128), lambda i: (0, 0)),
                  pl.BlockSpec(memory_space=pltpu.SMEM), qblk, qblk,
                  pl.BlockSpec((QB, 512), lambda i: (i, 0))],
        out_specs=[pl.BlockSpec((QB, 512), lambda i: (i, 0)), full, full, small, small],
        out_shape=[jax.ShapeDtypeStruct((t, 512), BF16), jax.ShapeDtypeStruct((t, 128), F32),
                   jax.ShapeDtypeStruct((t, 128), F32), jax.ShapeDtypeStruct((8, 128), F32),
                   jax.ShapeDtypeStruct((8, 128), F32)],
    )(p, kp, p, gq, sink, cos, sin, dmix)


def _tri(rev):
    r, c = _iota((CHUNK, CHUNK), 0), _iota((CHUNK, CHUNK), 1)
    return (c >= r) if rev else (c <= r)


def _blk_map(nb, rev, backward):
    if not rev:
        return (lambda n: nb - 1 - n) if backward else (lambda n: n)
    if backward:
        return lambda n: jnp.where(n < nb - 1, n + 1, 0)
    return lambda n: jnp.where(n == 0, 0, nb - n)


def _chunk_order(rev, backward, nc=TM // CHUNK):
    order = list(range(nc))
    return order[::-1] if (rev != backward) else order


def _hgrn_gates(qraw, fraw, lb):
    sq = _sigmoid(qraw)
    sf = _sigmoid(fraw)
    f = lb + (1.0 - lb) * sf
    return qraw * sq, 1.0 - f, jnp.log(f), sq, sf, f


def _gla_terms(q, k, lf, rev):
    tri = _tri(rev)
    b = _dot_exact(tri.astype(F32), lf)
    mid, last = (CHUNK // 2 - 1, 0) if rev else (CHUNK // 2, CHUNK - 1)
    r, bl = b[mid:mid + 1, :], b[last:last + 1, :]
    eq, ek, ei, eki = jnp.exp(b - r), jnp.exp(r - b), jnp.exp(b), jnp.exp(bl - b)
    return tri, last, eq, ek, ei, eki, jnp.exp(bl)


def _hgrn_fwd(p, lb, *, rev, name):
    t = p.shape[0]
    nb, nc = t // TM, TM // CHUNK
    bmap = _blk_map(nb, rev, False)
    fcol = 14 if rev else 10

    def body(q_ref, f_ref, v_ref, lb_ref, o_ref, sh_ref, st):
        @pl.when(pl.program_id(1) == 0)
        def _():
            st[...] = jnp.zeros_like(st)
        for cc in _chunk_order(rev, False):
            rows = slice(cc * CHUNK, (cc + 1) * CHUNK)
            q, k, lf, _, _, _ = _hgrn_gates(q_ref[rows, :], f_ref[rows, :], lb_ref[...])
            v = v_ref[rows, :]
            tri, _, eq, ek, ei, eki, eb = _gla_terms(q, k, lf, rev)
            s0 = st[...]
            sh_ref[cc] = s0
            a = jnp.where(tri, _dot_nt(q * eq, k * ek), 0.0)
            o_ref[rows, :] = _dot(a, v) + _dot_nt(q * ei, s0)
            st[...] = s0 * eb + _dot_tn(v, k * eki)

    def col(c0):
        return pl.BlockSpec((TM, 128), lambda h, n: (bmap(n), c0 + h))

    return _pcall(
        body, name=name, grid=(4, nb),
        in_specs=[col(6), col(fcol), col(18), pl.BlockSpec((1, 128), lambda h, n: (0, h))],
        out_specs=[pl.BlockSpec((TM, 128), lambda h, n: (bmap(n), h)),
                   pl.BlockSpec((None, nc, 128, 128), lambda h, n: (h, bmap(n), 0, 0))],
        out_shape=[jax.ShapeDtypeStruct((t, 512), F32), jax.ShapeDtypeStruct((4, t // CHUNK, 128, 128), F32)],
        scratch_shapes=[pltpu.VMEM((128, 128), F32)],
    )(p, p, p, lb)


def _hgrn_bwd(p, lb, sh, do, prev, *, rev, name):
    t = p.shape[0]
    nb, nc = t // TM, TM // CHUNK
    bmap = _blk_map(nb, rev, True)
    fcol = 14 if rev else 10
    has_prev = prev is not None
    odt = BF16 if has_prev else F32

    def body(*refs):
        refs = list(refs)
        q_ref, f_ref, v_ref, lb_ref, sh_ref, do_ref = refs[:6]
        pos = 6
        if has_prev:
            pq_ref, pv_ref = refs[6], refs[7]
            pos = 8
        dq_ref, df_ref, dv_ref, dlb_ref, dst = refs[pos:pos + 5]

        @pl.when(pl.program_id(1) == 0)
        def _():
            dst[...] = jnp.zeros_like(dst)
            dlb_ref[...] = jnp.zeros_like(dlb_ref)

        lbv = lb_ref[...]
        for cc in _chunk_order(rev, True):
            rows = slice(cc * CHUNK, (cc + 1) * CHUNK)
            qraw, fraw = q_ref[rows, :], f_ref[rows, :]
            q, k, lf, sq, sf, f = _hgrn_gates(qraw, fraw, lbv)
            v = v_ref[rows, :]
            dov = do_ref[rows, :]
            tri, last, eq, ek, ei, eki, eb = _gla_terms(q, k, lf, rev)
            s0 = sh_ref[cc]
            dsc = dst[...]
            qe, ke, qi, ki = q * eq, k * ek, q * ei, k * eki
            a = jnp.where(tri, _dot_nt(qe, ke), 0.0)
            da = jnp.where(tri, _dot_nt(dov, v), 0.0)
            dv = _dot_tn(a, dov) + _dot_nt(ki, dsc)
            dqe, dke = _dot(da, ke), _dot_tn(da, qe)
            dqi, dki = _dot(dov, s0), _dot(v, dsc)
            dst[...] = dsc * eb + _dot_tn(dov, qi)
            dq = dqe * eq + dqi * ei
            dk = dke * ek + dki * eki
            db = dqe * qe - dke * ke + dqi * qi - dki * ki
            dbl = jnp.sum(dki * ki, axis=0, keepdims=True) + jnp.sum(dsc * s0, axis=0, keepdims=True) * eb
            db = db + jnp.where(_iota(db.shape, 0) == last, dbl, 0.0)
            dlf = _dot_exact(_tri(not rev).astype(F32), db)
            dqr = dq * (sq * (1.0 + qraw * (1.0 - sq)))
            dfv = dlf / f - dk
            dfr = dfv * (1.0 - lbv) * (sf * (1.0 - sf))
            dlb_ref[...] += jnp.sum(dfv * (1.0 - sf), axis=0, keepdims=True)
            if has_prev:
                dqr = dqr + pq_ref[rows, :]
                dv = dv + pv_ref[rows, :]
            dq_ref[rows, :] = dqr.astype(odt)
            df_ref[rows, :] = dfr.astype(odt)
            dv_ref[rows, :] = dv.astype(odt)

    def col(c0):
        return pl.BlockSpec((TM, 128), lambda h, n: (bmap(n), c0 + h))

    oblk = pl.BlockSpec((TM, 128), lambda h, n: (bmap(n), h))
    ins = [p, p, p, lb, sh, do]
    specs = [col(6), col(fcol), col(18), pl.BlockSpec((1, 128), lambda h, n: (0, h)),
             pl.BlockSpec((None, nc, 128, 128), lambda h, n: (h, bmap(n), 0, 0)), oblk]
    if has_prev:
        ins += list(prev); specs += [oblk, oblk]
    return _pcall(
        body, name=name, grid=(4, nb), in_specs=specs,
        out_specs=[oblk, oblk, oblk, pl.BlockSpec((1, 128), lambda h, n: (0, h))],
        out_shape=[jax.ShapeDtypeStruct((t, 512), odt)] * 3 + [jax.ShapeDtypeStruct((1, 512), F32)],
        scratch_shapes=[pltpu.VMEM((128, 128), F32)],
    )(*ins)


def _rope256(x, cos, sin):
    x1, x2 = x[:, 0:128], x[:, 128:256]
    return jnp.concatenate([x1 * cos - x2 * sin, x2 * cos + x1 * sin], axis=-1)


def _rope256_t(d, cos, sin):
    d1, d2 = d[:, 0:128], d[:, 128:256]
    return jnp.concatenate([d1 * cos + d2 * sin, d2 * cos - d1 * sin], axis=-1)


RET_DK, RET_DV, RET_H = 256, 512, 4
RET_KSCALE = RET_DK ** -0.5
RCH = TM


def _ret_terms(lg, rev):
    r, c = _iota((RCH, RCH), 0), _iota((RCH, RCH), 1)
    rel = ((c - r) if rev else (r - c)).astype(F32)
    dmat = jnp.where(rel >= 0, jnp.exp(lg[:, 0:1] * jnp.maximum(rel, 0.0)), 0.0)
    pos = _iota((RCH, 1), 0).astype(F32)
    cnt = (RCH - pos) if rev else (pos + 1.0)
    ei = jnp.exp(lg * cnt)
    eki = jnp.exp(lg * (RCH - cnt))
    eb = jnp.exp(lg * float(RCH))
    return dmat, ei, eki, eb


def _ret_fwd(p, lgt, cos, sin, *, rev, name):
    t = p.shape[0]
    nb, nc = t // TM, TM // RCH
    bmap = _blk_map(nb, rev, False)

    def body(q_ref, k_ref, v_ref, lg_ref, c_ref, s_ref, o_ref, sh_ref, st):
        @pl.when(pl.program_id(1) == 0)
        def _():
            st[...] = jnp.zeros_like(st)
        dmat, ei, eki, eb = _ret_terms(lg_ref[...], rev)
        for cc in _chunk_order(rev, False, nc):
            rows = slice(cc * RCH, (cc + 1) * RCH)
            cosv, sinv = c_ref[rows, :], s_ref[rows, :]
            q = _rope256(q_ref[rows, :], cosv, sinv)
            k = _rope256(k_ref[rows, :], cosv, sinv) * RET_KSCALE
            v = v_ref[rows, :]
            s0 = st[...]
            sh_ref[cc] = s0.astype(BF16)
            a = _dot_nt(q, k) * dmat
            o_ref[rows, :] = _dot(a, v) + _dot_nt(q * ei, s0)
            st[...] = s0 * eb + _dot_tn(v, k * eki)

    tab = pl.BlockSpec((TM, 128), lambda h, n: (bmap(n), 0))
    return _pcall(
        body, name=name, grid=(RET_H, nb),
        in_specs=[pl.BlockSpec((TM, RET_DK), lambda h, n: (bmap(n), h)),
                  pl.BlockSpec((TM, RET_DK), lambda h, n: (bmap(n), 4 + h)),
                  pl.BlockSpec((TM, RET_DV), lambda h, n: (bmap(n), 4 + h)),
                  pl.BlockSpec((None, 1, RET_DK), lambda h, n: (h, 0, 0)), tab, tab],
        out_specs=[pl.BlockSpec((TM, RET_DV), lambda h, n: (bmap(n), h)),
                   pl.BlockSpec((None, nc, RET_DV, RET_DK), lambda h, n: (h, bmap(n), 0, 0))],
        out_shape=[jax.ShapeDtypeStruct((t, RET_H * RET_DV), F32),
                   jax.ShapeDtypeStruct((RET_H, t // RCH, RET_DV, RET_DK), BF16)],
        scratch_shapes=[pltpu.VMEM((RET_DV, RET_DK), F32)],
    )(p, p, p, lgt, cos, sin)


def _ret_bwd(p, lgt, cos, sin, sh, do, prev, *, rev, name):
    t = p.shape[0]
    nb, nc = t // TM, TM // RCH
    bmap = _blk_map(nb, rev, True)
    has_prev = prev is not None
    odt = BF16 if has_prev else F32

    def body(*refs):
        refs = list(refs)
        q_ref, k_ref, v_ref, lg_ref, c_ref, s_ref, sh_ref, do_ref = refs[:8]
        pos = 8
        if has_prev:
            pq_ref, pk_ref, pv_ref = refs[8:11]
            pos = 11
        dq_ref, dk_ref, dv_ref, dst = refs[pos:pos + 4]

        @pl.when(pl.program_id(1) == 0)
        def _():
            dst[...] = jnp.zeros_like(dst)

        dmat, ei, eki, eb = _ret_terms(lg_ref[...], rev)
        for cc in _chunk_order(rev, True, nc):
            rows = slice(cc * RCH, (cc + 1) * RCH)
            cosv, sinv = c_ref[rows, :], s_ref[rows, :]
            q = _rope256(q_ref[rows, :], cosv, sinv)
            k = _rope256(k_ref[rows, :], cosv, sinv) * RET_KSCALE
            v = v_ref[rows, :]
            dov = do_ref[rows, :]
            s0 = sh_ref[cc]
            dsc = dst[...]
            qi, ki = q * ei, k * eki
            a = _dot_nt(q, k) * dmat
            da = _dot_nt(dov, v) * dmat
            dv = _dot_tn(a, dov) + _dot_nt(ki, dsc)
            dqs = _dot(da, k) + _dot(dov, s0) * ei
            dks = _dot_tn(da, q) + _dot(v, dsc) * eki
            dst[...] = dsc * eb + _dot_tn(dov, qi)
            dq = _rope256_t(dqs, cosv, sinv)
            dk = _rope256_t(dks * RET_KSCALE, cosv, sinv)
            if has_prev:
                dq = dq + pq_ref[rows, :]
                dk = dk + pk_ref[rows, :]
                dv = dv + pv_ref[rows, :]
            dq_ref[rows, :] = dq.astype(odt)
            dk_ref[rows, :] = dk.astype(odt)
            dv_ref[rows, :] = dv.astype(odt)

    tab = pl.BlockSpec((TM, 128), lambda h, n: (bmap(n), 0))
    qblk = pl.BlockSpec((TM, RET_DK), lambda h, n: (bmap(n), h))
    vblk = pl.BlockSpec((TM, RET_DV), lambda h, n: (bmap(n), h))
    ins = [p, p, p, lgt, cos, sin, sh, do]
    specs = [qblk, pl.BlockSpec((TM, RET_DK), lambda h, n: (bmap(n), 4 + h)),
             pl.BlockSpec((TM, RET_DV), lambda h, n: (bmap(n), 4 + h)),
             pl.BlockSpec((None, 1, RET_DK), lambda h, n: (h, 0, 0)), tab, tab,
             pl.BlockSpec((None, nc, RET_DV, RET_DK), lambda h, n: (h, bmap(n), 0, 0)), vblk]
    if has_prev:
        ins += list(prev); specs += [qblk, qblk, vblk]
    return _pcall(
        body, name=name, grid=(RET_H, nb), in_specs=specs,
        out_specs=[qblk, qblk, vblk],
        out_shape=[jax.ShapeDtypeStruct((t, RET_H * RET_DK), odt), jax.ShapeDtypeStruct((t, RET_H * RET_DK), odt),
                   jax.ShapeDtypeStruct((t, RET_H * RET_DV), odt)],
        scratch_shapes=[pltpu.VMEM((RET_DV, RET_DK), F32)],
    )(*ins)


def _headnorm_fwd(ofw, obw, p, gain, *, dv, gcol, name):
    t, w = ofw.shape
    nh = w // dv
    has_gain = gain is not None

    def body(*refs):
        a_ref, b_ref, g_ref = refs[:3]
        gain_ref = refs[3] if has_gain else None
        o_ref = refs[-1]
        o = a_ref[...] + b_ref[...]
        n = o * lax.rsqrt(jnp.mean(o * o, axis=-1, keepdims=True) + EPS)
        if has_gain:
            n = n * gain_ref[...]
        gv = g_ref[...]
        o_ref[...] = (n * (gv * _sigmoid(gv))).astype(BF16)

    blk = pl.BlockSpec((TM, dv), lambda i, h: (i, h))
    ins, specs = [ofw, obw, p], [blk, blk, pl.BlockSpec((TM, dv), lambda i, h: (i, gcol + h))]
    if has_gain:
        ins.append(gain); specs.append(pl.BlockSpec((1, dv), lambda i, h: (0, 0)))
    return _pcall(body, name=name, grid=(t // TM, nh), in_specs=specs, out_specs=blk,
                  out_shape=jax.ShapeDtypeStruct((t, w), BF16))(*ins)


def _headnorm_bwd(ofw, obw, p, gain, dmix, *, dv, gcol, mcol, name):
    t, w = ofw.shape
    nh = w // dv
    has_gain = gain is not None

    def body(*refs):
        a_ref, b_ref, g_ref, dm_ref = refs[:4]
        gain_ref = refs[4] if has_gain else None
        do_ref, dg_ref, dgain_ref = refs[-3:]

        @pl.when((pl.program_id(0) == 0) & (pl.program_id(1) == 0))
        def _():
            dgain_ref[...] = jnp.zeros_like(dgain_ref)

        o = a_ref[...] + b_ref[...]
        rs = lax.rsqrt(jnp.mean(o * o, axis=-1, keepdims=True) + EPS)
        xh = o * rs
        n = xh * gain_ref[...] if has_gain else xh
        gv = g_ref[...]
        sg = _sigmoid(gv)
        dy = dm_ref[...]
        dn = dy * (gv * sg)
        dg_ref[...] = (dy * n * (sg * (1.0 + gv * (1.0 - sg)))).astype(BF16)
        _acc_row(dgain_ref, 0, jnp.sum(dn * xh, axis=0, keepdims=True))
        dxh = dn * gain_ref[...] if has_gain else dn
        do_ref[...] = rs * (dxh - xh * jnp.mean(dxh * xh, axis=-1, keepdims=True))

    blk = pl.BlockSpec((TM, dv), lambda i, h: (i, h))
    ins = [ofw, obw, p, dmix]
    specs = [blk, blk, pl.BlockSpec((TM, dv), lambda i, h: (i, gcol + h)),
             pl.BlockSpec((TM, dv), lambda i, h: (i, mcol + h))]
    if has_gain:
        ins.append(gain); specs.append(pl.BlockSpec((1, dv), lambda i, h: (0, 0)))
    return _pcall(
        body, name=name, grid=(t // TM, nh), in_specs=specs,
        out_specs=[blk, blk, pl.BlockSpec((8, dv), lambda i, h: (0, 0))],
        out_shape=[jax.ShapeDtypeStruct((t, w), F32), jax.ShapeDtypeStruct((t, w), BF16),
                   jax.ShapeDtypeStruct((8, dv), F32)],
    )(*ins)


def _rope_tables(lc, l):
    tt = jnp.arange(l)
    row, colp = (tt // 64).astype(F32), (tt % 64).astype(F32)
    inv = 10000.0 ** (-jnp.arange(16, dtype=F32) / 16)
    ang = jnp.concatenate([row[:, None] * inv, colp[:, None] * inv], axis=-1)
    ang = jnp.concatenate([jnp.zeros((lc, 32), F32), ang], axis=0)
    acos, asin = jnp.tile(jnp.cos(ang), (1, 4)), jnp.tile(jnp.sin(ang), (1, 4))
    theta = 1.0 / (10000.0 ** jnp.linspace(0.0, 1.0, 128, dtype=F32))
    rang = jnp.arange(l, dtype=F32)[:, None] * theta
    rang = jnp.concatenate([jnp.zeros((lc, 128), F32), rang], axis=0)
    return acos, asin, jnp.cos(rang), jnp.sin(rang)


def _local_step(x0, target, mods, ng, w, small):
    t, d = x0.shape
    l = target.shape[0]
    lc = t - l
    acos, asin, rcos, rsin = _rope_tables(lc, l)
    lg_fw = jnp.log(1.0 - 2.0 ** (-5.0 - jnp.arange(RET_H, dtype=F32)))
    lgt_fw = jnp.broadcast_to(lg_fw[:, None, None], (RET_H, 1, RET_DK))
    lgt_bw = jnp.broadcast_to(lg_fw[::-1][:, None, None], (RET_H, 1, RET_DK))
    gq, gk, sink, gain, lb = small['gq'], small['gk'], small['sink'], small['gain'], small['lb']

    (h1,) = _row_fwd(x0, mods, g=ng[0], shift=0, scale=1, name='l0_norm1')
    p0 = _mm_nn(h1, w['even_in'], name='l0_in')
    kp = _kprep_fwd(p0, gk, acos, asin, name='l0_kprep')
    att = _attn_fwd(p0, kp, gq, sink, acos, asin, lc=lc, name='l0_attn')
    hof, hsf = _hgrn_fwd(p0, lb, rev=False, name='l0_hgrn_f')
    hob, hsb = _hgrn_fwd(p0, lb, rev=True, name='l0_hgrn_b')
    bmix = _headnorm_fwd(hof, hob, p0, gain, dv=128, gcol=22, name='l0_headnorm')
    mix0 = jnp.concatenate([att, bmix], axis=1)
    y0 = _mm_nn(mix0, w['even_out'], name='l0_out')
    x1, h2 = _row_fwd(x0, mods, y=y0, gate=2, g=ng[1], shift=3, scale=4, name='l0_norm2')
    u0, a0 = _ffn_in(h2, w['ffn_in'], lead=0, name='ffn_in')
    z0 = _mm_nn(a0, w['ffn_out'], lead=0, name='ffn_out')
    x2, h3 = _row_fwd(x1, mods, y=z0, gate=5, g=ng[2], shift=12, scale=13, name='l1_norm1')
    p1 = _mm_nn(h3, w['odd_in'], name='l1_in')
    rof, rsf = _ret_fwd(p1, lgt_fw, rcos, rsin, rev=False, name='l1_ret_f')
    rob, rsb = _ret_fwd(p1, lgt_bw, rcos, rsin, rev=True, name='l1_ret_b')
    mix1 = _headnorm_fwd(rof, rob, p1, None, dv=RET_DV, gcol=8, name='l1_headnorm')
    y1 = _mm_nn(mix1, w['odd_out'], name='l1_out')
    x3, h4 = _row_fwd(x2, mods, y=y1, gate=14, g=ng[3], shift=15, scale=16, name='l1_norm2')
    u1, a1 = _ffn_in(h4, w['ffn_in'], lead=1, name='ffn_in')
    z1 = _mm_nn(a1, w['ffn_out'], lead=1, name='ffn_out')
    loss, dx4, dz1, s_fin = _row_final(x3, z1, mods, target, gate=17, name='loss')

    du1 = _ffn_dx(dz1, w['ffn_out'], u1, lead=1, name='ffn_out_dx')
    g_ffn_out1 = _mm_tn(a1, dz1, name='ffn_out_dw')
    dh4 = _mm_nt(du1, w['ffn_in'], lead=1, name='ffn_in_dx')
    g_ffn_in1 = _mm_tn(h4, du1, name='ffn_in_dw')
    dx3, dy1, s_l1n2 = _row_bwd(x3, dx4, dh4, mods, ng[3], shift=15, scale=16, y=y1, gate=14, name='l1_norm2_bwd')
    dmix1 = _mm_nt(dy1, w['odd_out'], name='l1_out_dx')
    g_odd_out = _mm_tn(mix1, dy1, name='l1_out_dw')
    rdo, rdg, _ = _headnorm_bwd(rof, rob, p1, None, dmix1, dv=RET_DV, gcol=8, mcol=0, name='l1_headnorm_bwd')
    part = _ret_bwd(p1, lgt_fw, rcos, rsin, rsf, rdo, None, rev=False, name='l1_ret_f_bwd')
    rdq, rdk, rdv = _ret_bwd(p1, lgt_bw, rcos, rsin, rsb, rdo, part, rev=True, name='l1_ret_b_bwd')
    dp1 = jnp.concatenate([rdq, rdk, rdv, rdg], axis=1)
    dh3 = _mm_nt(dp1, w['odd_in'], name='l1_in_dx')
    g_odd_in = _mm_tn(h3, dp1, name='l1_in_dw')
    dx2, dz0, s_l1n1 = _row_bwd(x2, dx3, dh3, mods, ng[2], shift=12, scale=13, y=z0, gate=5, name='l1_norm1_bwd')
    du0 = _ffn_dx(dz0, w['ffn_out'], u0, lead=0, name='ffn_out_dx')
    g_ffn_out0 = _mm_tn(a0, dz0, name='ffn_out_dw')
    dh2 = _mm_nt(du0, w['ffn_in'], lead=0, name='ffn_in_dx')
    g_ffn_in0 = _mm_tn(h2, du0, name='ffn_in_dw')
    dx1, dy0, s_l0n2 = _row_bwd(x1, dx2, dh2, mods, ng[1], shift=3, scale=4, y=y0, gate=2, name='l0_norm2_bwd')
    dmix0 = _mm_nt(dy0, w['even_out'], name='l0_out_dx')
    g_even_out = _mm_tn(mix0, dy0, name='l0_out_dw')
    hdo, hdg, s_gain = _headnorm_bwd(hof, hob, p0, gain, dmix0, dv=128, gcol=22, mcol=4, name='l0_headnorm_bwd')
    hq, hff, hv, dlb_f = _hgrn_bwd(p0, lb, hsf, hdo, None, rev=False, name='l0_hgrn_f_bwd')
    hq, hfb, hv, dlb_b = _hgrn_bwd(p0, lb, hsb, hdo, (hq, hv), rev=True, name='l0_hgrn_b_bwd')
    adq, dkp, adv, s_gq, s_sink = _attn_bwd(p0, kp, gq, sink, acos, asin, dmix0, lc=lc, name='l0_attn_bwd')
    dkv, s_gk = _kprep_bwd(p0, gk, acos, asin, dkp, adv, name='l0_kprep_bwd')
    dp0 = jnp.concatenate([adq, dkv, hq, _bf(hff), hfb, hv, hdg], axis=1)
    dh1 = _mm_nt(dp0, w['even_in'], name='l0_in_dx')
    g_even_in = _mm_tn(h1, dp0, name='l0_in_dw')
    dx0, s_l0n1 = _row_bwd(x0, dx1, dh1, mods, ng[0], shift=0, scale=1, name='l0_norm1_bwd')

    grads = dict(ffn_in=jnp.stack([g_ffn_in0, g_ffn_in1]), ffn_out=jnp.stack([g_ffn_out0, g_ffn_out1]),
                 even_in=g_even_in, even_out=g_even_out, odd_in=g_odd_in, odd_out=g_odd_out)
    sums = dict(fin=s_fin, l1n2=s_l1n2, l1n1=s_l1n1, l0n2=s_l0n2, l0n1=s_l0n1, gain=s_gain, gq=s_gq, gk=s_gk,
                sink=s_sink, dlb=dlb_f + dlb_b)
    return loss, dx0, grads, sums


def _place():
    return lax.axis_index("x"), lax.axis_index("y"), lax.axis_index("c")


def _ag8(blk, *, name):
    r, c = blk.shape
    flips = [(dx, dy, dc) for dx in (0, 1) for dy in (0, 1) for dc in (0, 1) if (dx, dy, dc) != (0, 0, 0)]

    def body(x_ref, out_ref, send_sems, recv_sems, local_sem):
        ax, ay, ac = _place()
        me = 4 * ax + 2 * ay + ac
        mine = pltpu.make_async_copy(x_ref, out_ref.at[me], local_sem)
        mine.start()
        sent = []
        for k, (dx, dy, dc) in enumerate(flips):
            peer = (lax.rem(ax + dx, 2), lax.rem(ay + dy, 2), lax.rem(ac + dc, 2))
            cp = pltpu.make_async_remote_copy(src_ref=x_ref, dst_ref=out_ref.at[me], send_sem=send_sems.at[k],
                                              recv_sem=recv_sems.at[k], device_id=peer, device_id_type=MESH)
            cp.start()
            sent.append((cp, 4 * peer[0] + 2 * peer[1] + peer[2]))
        for k, (cp, pidx) in enumerate(sent):
            pltpu.make_async_remote_copy(src_ref=x_ref, dst_ref=out_ref.at[pidx], send_sem=send_sems.at[k],
                                         recv_sem=recv_sems.at[k], device_id=(ax, ay, ac),
                                         device_id_type=MESH).wait_recv()
        for cp, _ in sent:
            cp.wait_send()
        mine.wait()

    return _pcall(
        body, name=name,
        in_specs=[pl.BlockSpec(memory_space=pltpu.VMEM)],
        out_specs=pl.BlockSpec(memory_space=pltpu.VMEM),
        out_shape=jax.ShapeDtypeStruct((8, r, c), blk.dtype),
        scratch_shapes=[pltpu.SemaphoreType.DMA((7,)), pltpu.SemaphoreType.DMA((7,)), pltpu.SemaphoreType.DMA],
    )(blk)


def _chip_exchange(arrs, *, scatter, name):
    n = len(arrs)
    rel = [(1, 0), (0, 1), (1, 1)]

    def body(*refs):
        ins, outs = refs[:n], refs[n:2 * n]
        send_sems, recv_sems, local_sems = refs[2 * n:]
        ax, ay, ac = _place()
        s = 2 * ax + ay
        started, local = [], []
        for a in range(n):
            lcp = pltpu.make_async_copy(ins[a].at[s] if scatter else ins[a], outs[a].at[s], local_sems.at[a])
            lcp.start()
            local.append(lcp)
            for r, (dx, dy) in enumerate(rel):
                px, py = lax.rem(ax + dx, 2), lax.rem(ay + dy, 2)
                sp = 2 * px + py
                cp = pltpu.make_async_remote_copy(
                    src_ref=ins[a].at[sp] if scatter else ins[a], dst_ref=outs[a].at[s],
                    send_sem=send_sems.at[3 * a + r], recv_sem=recv_sems.at[3 * a + r],
                    device_id=(px, py, ac), device_id_type=MESH)
                cp.start()
                started.append((cp, a, r, sp))
        for cp, a, r, sp in started:
            pltpu.make_async_remote_copy(
                src_ref=ins[a].at[sp] if scatter else ins[a], dst_ref=outs[a].at[sp],
                send_sem=send_sems.at[3 * a + r], recv_sem=recv_sems.at[3 * a + r],
                device_id=(ax, ay, ac), device_id_type=MESH).wait_recv()
        for cp, _, _, _ in started:
            cp.wait_send()
        for lcp in local:
            lcp.wait()

    hbm = pl.BlockSpec(memory_space=pl.ANY)
    shapes = [jax.ShapeDtypeStruct(a.shape if scatter else (4,) + a.shape, a.dtype) for a in arrs]
    return _pcall(
        body, name=name, in_specs=[hbm] * n, out_specs=[hbm] * n, out_shape=shapes,
        scratch_shapes=[pltpu.SemaphoreType.DMA((3 * n,)), pltpu.SemaphoreType.DMA((3 * n,)),
                        pltpu.SemaphoreType.DMA((n,))],
    )(*arrs)


def _gather_weights(arrs, *, name):
    n = len(arrs)
    rel = [(1, 0), (0, 1), (1, 1)]

    def body(*refs):
        ins, outs = refs[:n], refs[n:2 * n]
        ici_send, ici_recv, d2d_send, d2d_recv = refs[2 * n:]
        ax, ay, ac = _place()
        s = 2 * ax + ay
        sib = (ax, ay, 1 - ac)
        peers = [(lax.rem(ax + dx, 2), lax.rem(ay + dy, 2)) for dx, dy in rel]

        def half(a, slot, c):
            return outs[a].at[slot, c]

        def ici(a, r, src, slot, to):
            return pltpu.make_async_remote_copy(src_ref=src, dst_ref=half(a, slot, ac), send_sem=ici_send.at[3 * a + r],
                                                recv_sem=ici_recv.at[3 * a + r], device_id=to, device_id_type=MESH)

        def d2d(a, r, slot, c):
            return pltpu.make_async_remote_copy(src_ref=half(a, slot, c), dst_ref=half(a, slot, c),
                                                send_sem=d2d_send.at[3 * a + r], recv_sem=d2d_recv.at[3 * a + r],
                                                device_id=sib, device_id_type=MESH)

        sent = []
        for a in range(n):
            for r, (px, py) in enumerate(peers):
                cp = ici(a, r, ins[a].at[ac], s, (px, py, ac))
                cp.start()
                sent.append(cp)
        for a in range(n):
            for r, (px, py) in enumerate(peers):
                sp = 2 * px + py
                ici(a, r, half(a, sp, ac), sp, (ax, ay, ac)).wait_recv()
                fw = d2d(a, r, sp, ac)
                fw.start()
                sent.append(fw)
        for a in range(n):
            for r, (px, py) in enumerate(peers):
                d2d(a, r, 2 * px + py, 1 - ac).wait_recv()
        for cp in sent:
            cp.wait_send()

    hbm = pl.BlockSpec(memory_space=pl.ANY)
    return _pcall(
        body, name=name, in_specs=[hbm] * n, out_specs=[hbm] * n,
        out_shape=[jax.ShapeDtypeStruct((4,) + a.shape, a.dtype) for a in arrs],
        scratch_shapes=[pltpu.SemaphoreType.DMA((3 * n,))] * 4,
    )(*arrs)


def _to_sibling(arrs, *, name):
    n = len(arrs)

    def body(*refs):
        ins, outs = refs[:n], refs[n:2 * n]
        send_sems, recv_sems = refs[2 * n:]
        ax, ay, ac = _place()
        cps = [pltpu.make_async_remote_copy(src_ref=ins[a], dst_ref=outs[a], send_sem=send_sems.at[a],
                                            recv_sem=recv_sems.at[a], device_id=(ax, ay, 1 - ac),
                                            device_id_type=MESH) for a in range(n)]
        for cp in cps:
            cp.start()
        for cp in cps:
            cp.wait_recv()
        for cp in cps:
            cp.wait_send()

    hbm = pl.BlockSpec(memory_space=pl.ANY)
    return _pcall(
        body, name=name, in_specs=[hbm] * n, out_specs=[hbm] * n,
        out_shape=[jax.ShapeDtypeStruct(a.shape, a.dtype) for a in arrs],
        scratch_shapes=[pltpu.SemaphoreType.DMA((n,))] * 2,
    )(*arrs)


def _mod_fwd(cond_raw, mw, mb, *, name):
    _, d, n = mw.shape

    def body(c_ref, w_ref, b_ref, o_ref):
        cv = c_ref[...]
        o_ref[...] = _dot(cv * _sigmoid(cv), w_ref[...]) + b_ref[...]

    return _pcall(
        body, name=name, grid=(2,),
        in_specs=[pl.BlockSpec((16, d), lambda l: (0, 0)), pl.BlockSpec((None, d, n), lambda l: (l, 0, 0)),
                  pl.BlockSpec((None, 1, n), lambda l: (l, 0, 0))],
        out_specs=pl.BlockSpec((None, 16, n), lambda l: (l, 0, 0)),
        out_shape=jax.ShapeDtypeStruct((2, 16, n), F32),
    )(cond_raw, mw, mb)


def _mod_bwd(cond_raw, dms, mw, *, name):
    _, d, n = mw.shape

    def body(c_ref, dm_ref, w_ref, gw_ref, dc_ref):
        @pl.when(pl.program_id(0) == 0)
        def _():
            dc_ref[...] = jnp.zeros_like(dc_ref)
        cv = c_ref[...]
        gw_ref[...] = _dot_tn(cv * _sigmoid(cv), dm_ref[...])
        dc_ref[...] += _dot_nt(dm_ref[...], w_ref[...])

    return _pcall(
        body, name=name, grid=(2,),
        in_specs=[pl.BlockSpec((16, d), lambda l: (0, 0)), pl.BlockSpec((None, 16, n), lambda l: (l, 0, 0)),
                  pl.BlockSpec((None, d, n), lambda l: (l, 0, 0))],
        out_specs=[pl.BlockSpec((None, d, n), lambda l: (l, 0, 0)), pl.BlockSpec((16, d), lambda l: (0, 0))],
        out_shape=[jax.ShapeDtypeStruct((2, d, n), F32), jax.ShapeDtypeStruct((16, d), F32)],
    )(cond_raw, dms, mw)


def _lb_fwd(hgrn_lb, *, name):
    def body(a_ref, o_ref):
        a0, a1 = a_ref[0:1, :], a_ref[1:2, :]
        m = jnp.maximum(a0, a1)
        e0, e1 = jnp.exp(a0 - m), jnp.exp(a1 - m)
        o_ref[...] = e0 / (e0 + e1)

    return _pcall(body, name=name, out_shape=jax.ShapeDtypeStruct((1, hgrn_lb.shape[1]), F32))(hgrn_lb)


PACK_ROWS = 40


def _small_finalize(gath, lb_pad, *, name):
    d = gath.shape[2]

    def body(g_ref, lb_ref, small_ref, glb_ref, gmb_ref, dm_ref):
        tot = g_ref[0]
        for e in range(1, 8):
            tot = tot + g_ref[e]
        for k in range(4):
            small_ref[k:k + 1, :] = tot[24 + 2 * k:25 + 2 * k, :] + tot[25 + 2 * k:26 + 2 * k, :]
        for k, r in ((4, 32), (5, 33)):
            v = tot[r:r + 1, :]
            small_ref[k:k + 1, :] = v + pltpu.roll(v, d - 64, 1)
        small_ref[6:7, :] = tot[34:35, :]
        small_ref[7:8, :] = tot[36:37, :]
        lbv = lb_ref[...]
        g0 = (tot[35:36, :] + tot[37:38, :]) * lbv * (1.0 - lbv)
        glb_ref[...] = jnp.zeros_like(glb_ref)
        glb_ref[0:1, :] = g0
        glb_ref[1:2, :] = -g0
        dm_ref[...] = jnp.zeros_like(dm_ref)
        for l in range(2):
            for part in range(6):
                rc, rl = l * 12 + part, l * 12 + 6 + part
                gmb_ref[l * 6 + part:l * 6 + part + 1, :] = tot[rc:rc + 1, :] + tot[rl:rl + 1, :]
                for e in range(8):
                    dm_ref[l, part, e:e + 1, :] = g_ref[e, rl:rl + 1, :]
                dm_ref[l, part, 8:9, :] = tot[rc:rc + 1, :]

    return _pcall(
        body, name=name,
        out_shape=[jax.ShapeDtypeStruct((8, d), F32), jax.ShapeDtypeStruct((8, d), F32),
                   jax.ShapeDtypeStruct((12, d), F32), jax.ShapeDtypeStruct((2, 6, 16, d), F32)],
    )(gath, lb_pad)


def _cctx_grad(gath, c_ctx2, *, name):
    def body(g_ref, c_ref, o_ref):
        tot = ((g_ref[0, 0:1, :] + g_ref[2, 0:1, :]) + g_ref[4, 0:1, :]) + g_ref[6, 0:1, :]
        cv = c_ref[...]
        s = _sigmoid(cv)
        o_ref[...] = tot * (s * (1.0 + cv * (1.0 - s)))

    return _pcall(body, name=name, out_shape=jax.ShapeDtypeStruct(c_ctx2.shape, F32))(gath, c_ctx2)


def _row_block(r, c, limit=256 * 1024):
    best = None
    for br in range(16, r + 1, 16):
        if r % br == 0 and br * c <= limit:
            best = br
    return best if best is not None else r


def _sum4(parts, *, name):
    _, r, c = parts.shape
    br = _row_block(r, c)

    def body(p_ref, o_ref):
        p = [p_ref[k].astype(F32) for k in range(4)]
        o_ref[...] = ((p[0] + p[1]) + p[2]) + p[3]

    return _pcall(body, name=name, grid=(r // br,),
                  in_specs=[pl.BlockSpec((4, br, c), lambda i: (0, i, 0))],
                  out_specs=pl.BlockSpec((br, c), lambda i: (i, 0)),
                  out_shape=jax.ShapeDtypeStruct((r, c), F32))(parts)


def _add2(a, b, *, name):
    r, c = a.shape
    br = _row_block(r, c)

    def body(a_ref, b_ref, o_ref):
        o_ref[...] = (a_ref[...].astype(F32) + b_ref[...].astype(F32)).astype(BF16)

    blk = pl.BlockSpec((br, c), lambda i: (i, 0))
    return _pcall(body, name=name, grid=(r // br,), in_specs=[blk, blk], out_specs=blk,
                  out_shape=jax.ShapeDtypeStruct((r, c), BF16))(a, b)


def _adam(w, gs, m, v, *, name):
    r, c = w.shape
    br = _row_block(r, c)
    ng = len(gs)
    c1 = 1.0 - ADAM_B1 ** ADAM_STEP
    c2 = 1.0 - ADAM_B2 ** ADAM_STEP

    def body(*refs):
        w_ref, m_ref, v_ref = refs[0], refs[1 + ng], refs[2 + ng]
        outs = refs[3 + ng:]
        g = refs[1][...]
        for k in range(1, ng):
            g = g + refs[1 + k][...]
        mn = ADAM_B1 * m_ref[...] + (1.0 - ADAM_B1) * g
        vn = ADAM_B2 * v_ref[...] + (1.0 - ADAM_B2) * (g * g)
        if ng > 1:
            outs[0][...] = g
        d_out, m_out, v_out = outs[-3:]
        m_out[...] = mn
        v_out[...] = vn
        d_out[...] = -ADAM_LR * ((mn / c1) / (jnp.sqrt(vn / c2) + ADAM_EPS) + ADAM_WD * w_ref[...])

    blk = pl.BlockSpec((br, c), lambda i: (i, 0))
    nout = 4 if ng > 1 else 3
    res = _pcall(body, name=name, grid=(r // br,), in_specs=[blk] * (3 + ng), out_specs=[blk] * nout,
                 out_shape=[jax.ShapeDtypeStruct((r, c), F32)] * nout)(w, *gs, m, v)
    return list(res) if ng > 1 else [gs[0]] + list(res)


def _to_shards(name, g):
    if name == 'ffn_in':
        l, k, n4 = g.shape
        return _ffn_deinterleave(g).reshape(l, k, 4, n4 // 4).transpose(2, 0, 1, 3).reshape(4, l * k, n4 // 4)
    if name == 'ffn_out':
        l, k4, n = g.shape
        return g.reshape(l, 4, k4 // 4, n).transpose(1, 0, 2, 3).reshape(4, l * k4 // 4, n)
    if name in ('even_in', 'odd_in'):
        k, n4 = g.shape
        return g.reshape(k, 4, n4 // 4).transpose(1, 0, 2)
    k4, n = g.shape
    return g.reshape(4, k4 // 4, n)


def _from_shards(name, g):
    _, r, n = g.shape
    if name == 'ffn_in':
        return _ffn_interleave(g.reshape(4, 2, r // 2, n).transpose(1, 2, 0, 3).reshape(2, r // 2, 4 * n))
    if name == 'ffn_out':
        return g.reshape(4, 2, r // 2, n).transpose(1, 0, 2, 3).reshape(2, 2 * r, n)
    if name in ('even_in', 'odd_in'):
        return g.transpose(1, 0, 2).reshape(r, 4 * n)
    return g.reshape(4 * r, n)


def kernel(x, c, ctx, c_ctx, mod_w, mod_b, norm_g, ffn_w_in, ffn_w_out, even_w_in, even_w_out, attn_qk_norm_g, attn_sink, hgrn_out_norm_g, hgrn_lb, odd_w_in, odd_w_out, loss_target, m_c_ctx, m_mod_w, m_mod_b, m_norm_g, m_ffn_w_in, m_ffn_w_out, m_even_w_in, m_even_w_out, m_attn_qk_norm_g, m_attn_sink, m_hgrn_out_norm_g, m_hgrn_lb, m_odd_w_in, m_odd_w_out, v_c_ctx, v_mod_w, v_mod_b, v_norm_g, v_ffn_w_in, v_ffn_w_out, v_even_w_in, v_even_w_out, v_attn_qk_norm_g, v_attn_sink, v_hgrn_out_norm_g, v_hgrn_lb, v_odd_w_in, v_odd_w_out):
    d = x.shape[-1]
    lc = ctx.shape[1]
    assert lc == TM and d == 1024
    ax, ay, ac = _place()
    s = 2 * ax + ay
    me = 4 * ax + 2 * ay + ac
    nmod = mod_w.shape[2]

    def pad8(v):
        return jnp.pad(v, ((0, 8 - v.shape[0]), (0, 0)))

    pack = jnp.concatenate([pad8(c), pad8(norm_g.reshape(1, d))], axis=0)
    g1 = _ag8(pack, name='gather_cond')
    c_all = g1[:, 0, :]
    ng = g1[0::2, 8, :].reshape(4, 2, 2, d // 4).transpose(1, 2, 0, 3).reshape(4, d)

    names = ['ffn_in', 'ffn_out', 'even_in', 'even_out', 'odd_in', 'odd_out']
    shards = [_bf(v.reshape(-1, v.shape[-1])) for v in (ffn_w_in, ffn_w_out, even_w_in, even_w_out, odd_w_in, odd_w_out)]
    gathered = _gather_weights([a.reshape(2, a.shape[0] // 2, a.shape[1]) for a in shards], name='gather_weights')
    slot = lax.broadcasted_iota(jnp.int32, (4, 1, 1), 0)
    w = {nm: _from_shards(nm, jnp.where(slot == s, a[None], g.reshape((4,) + a.shape)))
         for nm, a, g in zip(names, shards, gathered)}

    cond_raw = jnp.concatenate([c_all, pad8(c_ctx.reshape(1, d))], axis=0)
    mb_sh = lax.dynamic_slice_in_dim(mod_b, s * nmod, nmod, axis=1).reshape(2, 1, nmod)
    mpart = _mod_fwd(cond_raw, mod_w, mb_sh, name='mod_fwd')
    g3 = _ag8(mpart.reshape(32, nmod), name='gather_mods')
    mods_full = g3[0::2].reshape(4, 2, 16, nmod).transpose(1, 2, 0, 3).reshape(2, 16, 4 * nmod)
    m_lat = lax.dynamic_index_in_dim(mods_full, me, axis=1, keepdims=False)
    mods = jnp.stack([mods_full[:, 8], m_lat], axis=1).reshape(24, d)

    lb = _lb_fwd(hgrn_lb, name='hgrn_lower_bound')
    small = dict(gq=jnp.tile(attn_qk_norm_g[0, 0], 2).reshape(1, 128), gk=jnp.tile(attn_qk_norm_g[0, 1], 2).reshape(1, 128),
                 sink=attn_sink[0], gain=hgrn_out_norm_g, lb=lb)
    x0 = jnp.concatenate([ctx[0], x[0]], axis=0)
    loss_t, dx0, grads, sums = _local_step(x0, loss_target[0], mods, ng, w, small)
    loss = lax.psum(loss_t[0, 0], ("x", "y", "c"))
    grad_x = dx0[lc:][None]

    def pad(v):
        return jnp.pad(v, ((0, 0), (0, d - v.shape[1])))

    sm = sums
    dm_rows = [
        [sm['l0n1'][0], sm['l0n1'][1], sm['l0n2'][2], sm['l0n2'][0], sm['l0n2'][1], sm['l1n1'][2]],
        [sm['l0n1'][4], sm['l0n1'][5], sm['l0n2'][6], sm['l0n2'][4], sm['l0n2'][5], sm['l1n1'][6]],
        [sm['l1n1'][0], sm['l1n1'][1], sm['l1n2'][2], sm['l1n2'][0], sm['l1n2'][1], sm['fin'][2]],
        [sm['l1n1'][4], sm['l1n1'][5], sm['l1n2'][6], sm['l1n2'][4], sm['l1n2'][5], sm['fin'][6]],
    ]
    rows = [r for grp in dm_rows for r in grp]
    for key in ('l0n1', 'l0n2', 'l1n1', 'l1n2'):
        rows += [sm[key][3], sm[key][7]]
    rows = jnp.stack(rows)
    extra = jnp.concatenate([pad(sm['gq'][0:1]), pad(sm['gk'][0:1]), pad(sm['gain'][0:1]), pad(sm['dlb']),
                             pad(sm['sink'][:, 0].reshape(1, 8)), jnp.zeros((3, d), F32)], axis=0)
    g4 = _ag8(jnp.concatenate([rows, extra], axis=0), name='gather_row_sums')
    small_g, glb, gmb, dmat = _small_finalize(g4, pad(lb), name='small_grads')
    dms = lax.dynamic_slice_in_dim(dmat.transpose(0, 2, 1, 3).reshape(2, 16, 6 * d), s * nmod, nmod, axis=2)
    g_mod_w, dcond = _mod_bwd(cond_raw, dms, mod_w, name='mod_bwd')
    g5 = _ag8(dcond[8:16], name='gather_dcond')
    g_c_ctx = _cctx_grad(g5, c_ctx.reshape(8, d // 8).reshape(1, d), name='c_ctx_grad')

    def row_half(g, which):
        v = g.reshape(4, 2, g.shape[1] // 2, g.shape[2])
        return _bf(lax.dynamic_index_in_dim(v, which, axis=1, keepdims=False))

    shard_major = [_to_shards(nm, grads[nm]) for nm in names]
    mine = [row_half(g, ac) for g in shard_major]
    theirs = _to_sibling([row_half(g, 1 - ac) for g in shard_major], name='swap_core_halves')
    pair = [_add2(a.reshape(-1, a.shape[-1]), b.reshape(-1, b.shape[-1]), name='add_cores').reshape(a.shape)
            for a, b in zip(mine, theirs)]
    parts = _chip_exchange(pair, scatter=True, name='scatter_grads')
    half_sums = [_sum4(p, name='sum_chips') for p in parts]
    other = _to_sibling(half_sums, name='gather_core_halves')
    full = [jnp.concatenate([jnp.where(ac == 0, f, o), jnp.where(ac == 0, o, f)], axis=0)
            for f, o in zip(half_sums, other)]

    def upd(wv, gs, mv, vv, name):
        shp = wv.shape
        c2 = shp[-1]
        out = _adam(wv.reshape(-1, c2), [g.reshape(-1, c2) for g in gs], mv.reshape(-1, c2), vv.reshape(-1, c2), name=name)
        return [o.reshape(shp) for o in out]

    res = {}
    res['c_ctx'] = upd(c_ctx.reshape(8, d // 8), [g_c_ctx.reshape(8, d // 8)], m_c_ctx.reshape(8, d // 8), v_c_ctx.reshape(8, d // 8), 'adam_c_ctx')
    res['c_ctx'] = [o.reshape(d) for o in res['c_ctx']]
    res['mod_w'] = upd(mod_w, [g_mod_w], m_mod_w, v_mod_w, 'adam_mod_w')
    res['mod_b'] = upd(mod_b, [gmb.reshape(2, 6 * d)], m_mod_b, v_mod_b, 'adam_mod_b')
    g_ng = lax.dynamic_slice_in_dim(small_g[0:4].reshape(2, 2, d), s * (d // 4), d // 4, axis=2)
    res['norm_g'] = upd(norm_g, [g_ng], m_norm_g, v_norm_g, 'adam_norm_g')
    big = {nm: [g] for nm, g in zip(names, full)}
    res['ffn_w_in'] = upd(ffn_w_in, big['ffn_in'], m_ffn_w_in, v_ffn_w_in, 'adam_ffn_in')
    res['ffn_w_out'] = upd(ffn_w_out, big['ffn_out'], m_ffn_w_out, v_ffn_w_out, 'adam_ffn_out')
    res['even_w_in'] = upd(even_w_in, big['even_in'], m_even_w_in, v_even_w_in, 'adam_even_in')
    res['even_w_out'] = upd(even_w_out, big['even_out'], m_even_w_out, v_even_w_out, 'adam_even_out')
    g_qk = jnp.stack([small_g[4, 0:64], small_g[5, 0:64]]).reshape(1, 2, 64)
    res['attn_qk_norm_g'] = upd(attn_qk_norm_g, [g_qk], m_attn_qk_norm_g, v_attn_qk_norm_g, 'adam_qk_gain')
    res['attn_sink'] = upd(attn_sink, [small_g[7, 0:8].reshape(1, 8)], m_attn_sink, v_attn_sink, 'adam_sink')
    res['hgrn_out_norm_g'] = upd(hgrn_out_norm_g, [small_g[6, 0:128].reshape(1, 128)], m_hgrn_out_norm_g, v_hgrn_out_norm_g, 'adam_head_gain')
    res['hgrn_lb'] = upd(hgrn_lb, [glb[0:2, 0:hgrn_lb.shape[1]]], m_hgrn_lb, v_hgrn_lb, 'adam_hgrn_lb')
    res['odd_w_in'] = upd(odd_w_in, big['odd_in'], m_odd_w_in, v_odd_w_in, 'adam_odd_in')
    res['odd_w_out'] = upd(odd_w_out, big['odd_out'], m_odd_w_out, v_odd_w_out, 'adam_odd_out')

    order = ['c_ctx', 'mod_w', 'mod_b', 'norm_g', 'ffn_w_in', 'ffn_w_out', 'even_w_in', 'even_w_out',
             'attn_qk_norm_g', 'attn_sink', 'hgrn_out_norm_g', 'hgrn_lb', 'odd_w_in', 'odd_w_out']
    outs = [loss, grad_x]
    for k in range(4):
        outs += [res[nm][k] for nm in order]
    return tuple(outs)
```

```python
import functools
import math

import numpy as np
import jax
import jax.numpy as jnp
from jax import lax
from jax.experimental import pallas as pl
from jax.experimental.pallas import tpu as pltpu

F32 = jnp.float32
BF16 = jnp.bfloat16
EPS = 1e-6
TM = 256
CHUNK = 64
QB = 128
WINDOW = 128
NEG = -1e30
MESH = pl.DeviceIdType.MESH

ADAM_LR, ADAM_B1, ADAM_B2, ADAM_EPS, ADAM_WD, ADAM_STEP = 0.001, 0.9, 0.999, 1e-08, 0.01, 10


def _pcall(body, **kw):
    return pl.pallas_call(body, **kw)


def _pick(n, cap):
    best = None
    for m in range(128, min(n, cap) + 1, 128):
        if n % m == 0:
            best = m
    assert best is not None, (n, cap)
    return best


def _bf(x):
    return x.astype(BF16)


def _dot(a, b):
    return jnp.dot(_bf(a), _bf(b), preferred_element_type=F32)


def _dot_nt(a, b):
    return lax.dot_general(_bf(a), _bf(b), (((1,), (1,)), ((), ())), preferred_element_type=F32)


def _dot_tn(a, b):
    return lax.dot_general(_bf(a), _bf(b), (((0,), (0,)), ((), ())), preferred_element_type=F32)


def _dot_exact(a, b):
    return jnp.dot(a, b, preferred_element_type=F32, precision=lax.Precision.HIGHEST)


def _sigmoid(x):
    return 1.0 / (1.0 + jnp.exp(-x))


def _iota(shape, dim):
    return lax.broadcasted_iota(jnp.int32, shape, dim)


def _mm_nn(a, b, *, lead=None, out_dtype=F32, name):
    m, k = a.shape
    n = b.shape[-1]
    bm = 768 if m % 768 == 0 else TM
    bn = _pick(n, 1024)

    def body(a_ref, b_ref, o_ref):
        o_ref[...] = _dot(a_ref[...], b_ref[...]).astype(o_ref.dtype)

    if lead is None:
        b_spec = pl.BlockSpec((k, bn), lambda i, j: (0, j))
    else:
        b_spec = pl.BlockSpec((None, k, bn), lambda i, j: (lead, 0, j))
    return _pcall(
        body, name=name, grid=(m // bm, n // bn),
        in_specs=[pl.BlockSpec((bm, k), lambda i, j: (i, 0)), b_spec],
        out_specs=pl.BlockSpec((bm, bn), lambda i, j: (i, j)),
        out_shape=jax.ShapeDtypeStruct((m, n), out_dtype),
    )(a, b)


def _mm_nt(a, b, *, lead=None, name):
    m, n = a.shape
    k = b.shape[-2]
    bm = 768 if m % 768 == 0 else TM
    bk = _pick(k, 512)

    def body(a_ref, b_ref, o_ref):
        o_ref[...] = _dot_nt(a_ref[...], b_ref[...])

    if lead is None:
        b_spec = pl.BlockSpec((bk, n), lambda i, j: (j, 0))
    else:
        b_spec = pl.BlockSpec((None, bk, n), lambda i, j: (lead, j, 0))
    return _pcall(
        body, name=name, grid=(m // bm, k // bk),
        in_specs=[pl.BlockSpec((bm, n), lambda i, j: (i, 0)), b_spec],
        out_specs=pl.BlockSpec((bm, bk), lambda i, j: (i, j)),
        out_shape=jax.ShapeDtypeStruct((m, k), F32),
    )(a, b)


def _mm_tn(a, b, *, name):
    t, k = a.shape
    n = b.shape[1]
    bt = 768 if t % 768 == 0 else TM
    bk = _pick(k, 1536)
    bn = _pick(n, 1024) if n % 1024 == 0 or n < 1664 else _pick(n, 1664)

    def body(a_ref, b_ref, o_ref):
        @pl.when(pl.program_id(2) == 0)
        def _():
            o_ref[...] = jnp.zeros_like(o_ref)
        o_ref[...] += _dot_tn(a_ref[...], b_ref[...])

    return _pcall(
        body, name=name, grid=(k // bk, n // bn, t // bt),
        in_specs=[pl.BlockSpec((bt, bk), lambda i, j, s: (s, i)),
                  pl.BlockSpec((bt, bn), lambda i, j, s: (s, j))],
        out_specs=pl.BlockSpec((bk, bn), lambda i, j, s: (i, j)),
        out_shape=jax.ShapeDtypeStruct((k, n), F32),
    )(a, b)


def _mod_row(mods_ref, lat, idx):
    return jnp.where(lat, mods_ref[idx + 6:idx + 7, :], mods_ref[idx:idx + 1, :])


def _row_fwd(x, mods, *, y=None, gate=None, g=None, shift=None, scale=None, name):
    t, d = x.shape
    has_y, has_n = y is not None, g is not None

    def body(*refs):
        refs = list(refs)
        x_ref, mods_ref = refs[0], refs[1]
        pos = 2
        if has_y:
            y_ref = refs[pos]; pos += 1
        if has_n:
            g_ref = refs[pos]; pos += 1
        outs = refs[pos:]
        lat = pl.program_id(0) > 0
        x1 = x_ref[...]
        o = 0
        if has_y:
            x1 = x1 + _mod_row(mods_ref, lat, gate) * y_ref[...]
            outs[o][...] = x1; o += 1
        if has_n:
            rs = lax.rsqrt(jnp.mean(x1 * x1, axis=-1, keepdims=True) + EPS)
            hn = x1 * rs * g_ref[...]
            h = hn * (1.0 + _mod_row(mods_ref, lat, scale)) + _mod_row(mods_ref, lat, shift)
            outs[o][...] = h.astype(BF16)

    row = pl.BlockSpec((TM, d), lambda i: (i, 0))
    ins, specs = [x, mods], [row, pl.BlockSpec(mods.shape, lambda i: (0, 0))]
    if has_y:
        ins.append(y); specs.append(row)
    if has_n:
        ins.append(g.reshape(1, d)); specs.append(pl.BlockSpec((1, d), lambda i: (0, 0)))
    out_shape, out_specs = [], []
    if has_y:
        out_shape.append(jax.ShapeDtypeStruct((t, d), F32)); out_specs.append(row)
    if has_n:
        out_shape.append(jax.ShapeDtypeStruct((t, d), BF16)); out_specs.append(row)
    res = _pcall(body, name=name, grid=(t // TM,), in_specs=specs, out_specs=out_specs,
                 out_shape=out_shape)(*ins)
    return res


def _acc_row(ref, r, val):
    ref[r:r + 1, :] += val


def _row_final(x, z, mods, target, *, gate, name):
    t, d = x.shape

    def body(x_ref, mods_ref, z_ref, t_ref, loss_ref, dx_ref, dz_ref, sums_ref):
        i = pl.program_id(0)
        lat = i > 0

        @pl.when(i == 0)
        def _():
            loss_ref[...] = jnp.zeros_like(loss_ref)
            sums_ref[...] = jnp.zeros_like(sums_ref)

        gt = _mod_row(mods_ref, lat, gate)
        zz = z_ref[...]
        yv = x_ref[...] + gt * zz
        keep = jnp.where(lat, 1.0, 0.0).astype(F32)
        diff = (yv - t_ref[...]) * keep
        part = jnp.sum(jnp.sum(diff * diff, axis=0, keepdims=True), axis=1, keepdims=True)
        loss_ref[...] += part * (0.5 / d)
        dy = diff * (1.0 / d)
        dx_ref[...] = dy
        dz_ref[...] = (gt * dy).astype(BF16)
        _acc_row(sums_ref, 6, jnp.sum(dy * zz, axis=0, keepdims=True))

    row = pl.BlockSpec((TM, d), lambda i: (i, 0))
    return _pcall(
        body, name=name, grid=(t // TM,),
        in_specs=[row, pl.BlockSpec(mods.shape, lambda i: (0, 0)), row,
                  pl.BlockSpec((TM, d), lambda i: (jnp.maximum(i - 1, 0), 0))],
        out_specs=[pl.BlockSpec((8, 128), lambda i: (0, 0)), row, row,
                   pl.BlockSpec((8, d), lambda i: (0, 0))],
        out_shape=[jax.ShapeDtypeStruct((8, 128), F32), jax.ShapeDtypeStruct((t, d), F32),
                   jax.ShapeDtypeStruct((t, d), BF16), jax.ShapeDtypeStruct((8, d), F32)],
    )(x, mods, z, target)


def _row_bwd(xn, dxo, dh, mods, g, *, shift, scale, y=None, gate=None, name):
    t, d = xn.shape
    has_y = y is not None

    def body(*refs):
        refs = list(refs)
        x_ref, dxo_ref, dh_ref, mods_ref, g_ref = refs[:5]
        pos = 5
        if has_y:
            y_ref = refs[pos]; pos += 1
        dx_ref = refs[pos]; pos += 1
        if has_y:
            dy_ref = refs[pos]; pos += 1
        sums_ref = refs[pos]
        i = pl.program_id(0)
        lat = i > 0

        @pl.when(i == 0)
        def _():
            sums_ref[...] = jnp.zeros_like(sums_ref)

        x1 = x_ref[...]
        gv = g_ref[...]
        rs = lax.rsqrt(jnp.mean(x1 * x1, axis=-1, keepdims=True) + EPS)
        xh = x1 * rs
        dhv = dh_ref[...]
        dn = dhv * (1.0 + _mod_row(mods_ref, lat, scale))
        dxh = dn * gv
        dx = dxo_ref[...] + rs * (dxh - xh * jnp.mean(dxh * xh, axis=-1, keepdims=True))
        dx_ref[...] = dx
        vals = [jnp.sum(dhv, axis=0, keepdims=True),
                jnp.sum(dhv * (xh * gv), axis=0, keepdims=True),
                None,
                jnp.sum(dn * xh, axis=0, keepdims=True)]
        if has_y:
            dy_ref[...] = (_mod_row(mods_ref, lat, gate) * dx).astype(BF16)
            vals[2] = jnp.sum(dx * y_ref[...], axis=0, keepdims=True)

        @pl.when(i == 0)
        def _():
            for r, v in enumerate(vals):
                if v is not None:
                    _acc_row(sums_ref, r, v)

        @pl.when(i > 0)
        def _():
            for r, v in enumerate(vals):
                if v is not None:
                    _acc_row(sums_ref, 4 + r, v)

    row = pl.BlockSpec((TM, d), lambda i: (i, 0))
    ins = [xn, dxo, dh, mods, g.reshape(1, d)]
    specs = [row, row, row, pl.BlockSpec(mods.shape, lambda i: (0, 0)), pl.BlockSpec((1, d), lambda i: (0, 0))]
    out_shape, out_specs = [jax.ShapeDtypeStruct((t, d), F32)], [row]
    if has_y:
        ins.append(y); specs.append(row)
        out_shape.append(jax.ShapeDtypeStruct((t, d), BF16)); out_specs.append(row)
    out_shape.append(jax.ShapeDtypeStruct((8, d), F32))
    out_specs.append(pl.BlockSpec((8, d), lambda i: (0, 0)))
    return _pcall(body, name=name, grid=(t // TM,), in_specs=specs, out_specs=out_specs,
                  out_shape=out_shape)(*ins)


FFN_BK = 1408


def _ffn_interleave(w):
    *lead, k, n2 = w.shape
    nb = n2 // (2 * FFN_BK)
    return jnp.swapaxes(w.reshape(*lead, k, 2, nb, FFN_BK), -3, -2).reshape(*lead, k, n2)


def _ffn_deinterleave(w):
    *lead, k, n2 = w.shape
    nb = n2 // (2 * FFN_BK)
    return jnp.swapaxes(w.reshape(*lead, k, nb, 2, FFN_BK), -3, -2).reshape(*lead, k, n2)


def _big_tile(t):
    return 384 if t % 384 == 0 else TM


def _ffn_in(h, w, *, lead, name):
    t, d = h.shape
    n2 = w.shape[-1]
    bm, bk = _big_tile(t), FFN_BK

    def body(h_ref, w_ref, u_ref, a_ref):
        ub = _dot(h_ref[...], w_ref[...]).astype(BF16)
        u_ref[...] = ub
        uf = ub.astype(F32)
        gv, up = uf[:, 0:bk], uf[:, bk:2 * bk]
        a_ref[...] = (gv * _sigmoid(gv) * up).astype(BF16)

    return _pcall(
        body, name=name, grid=(t // bm, n2 // (2 * bk)),
        in_specs=[pl.BlockSpec((bm, d), lambda i, j: (i, 0)),
                  pl.BlockSpec((None, d, 2 * bk), lambda i, j: (lead, 0, j))],
        out_specs=[pl.BlockSpec((bm, 2 * bk), lambda i, j: (i, j)), pl.BlockSpec((bm, bk), lambda i, j: (i, j))],
        out_shape=[jax.ShapeDtypeStruct((t, n2), BF16), jax.ShapeDtypeStruct((t, n2 // 2), BF16)],
    )(h, w)


def _ffn_dx(dz, w_out, u, *, lead, name):
    t, d = dz.shape
    n2 = u.shape[1]
    bm, bk = _big_tile(t), FFN_BK

    def body(dz_ref, w_ref, u_ref, du_ref):
        da = _dot_nt(dz_ref[...], w_ref[...])
        uf = u_ref[...].astype(F32)
        gv, up = uf[:, 0:bk], uf[:, bk:2 * bk]
        s = _sigmoid(gv)
        du_ref[:, 0:bk] = (da * up * (s * (1.0 + gv * (1.0 - s)))).astype(BF16)
        du_ref[:, bk:2 * bk] = (da * gv * s).astype(BF16)

    ublk = pl.BlockSpec((bm, 2 * bk), lambda i, j: (i, j))
    return _pcall(
        body, name=name, grid=(t // bm, n2 // (2 * bk)),
        in_specs=[pl.BlockSpec((bm, d), lambda i, j: (i, 0)),
                  pl.BlockSpec((None, bk, d), lambda i, j: (lead, j, 0)), ublk],
        out_specs=ublk, out_shape=jax.ShapeDtypeStruct((t, n2), BF16),
    )(dz, w_out, u)


def _lane(shape):
    return _iota(shape, len(shape) - 1)


def _pair_norm(x, g):
    lo = _lane(x.shape) < 64
    x2 = x * x
    s_lo = jnp.sum(jnp.where(lo, x2, 0.0), axis=-1, keepdims=True)
    s_hi = jnp.sum(jnp.where(lo, 0.0, x2), axis=-1, keepdims=True)
    rs = lax.rsqrt(jnp.where(lo, s_lo, s_hi) * (1.0 / 64) + EPS)
    return x * rs, rs


def _pair_mean(v):
    lo = _lane(v.shape) < 64
    s_lo = jnp.sum(jnp.where(lo, v, 0.0), axis=-1, keepdims=True)
    s_hi = jnp.sum(jnp.where(lo, 0.0, v), axis=-1, keepdims=True)
    return jnp.where(lo, s_lo, s_hi) * (1.0 / 64)


def _rot64(x):
    r1 = pltpu.roll(x, 32, 1)
    r2 = pltpu.roll(x, 96, 1)
    even = ((_lane(x.shape) >> 5) & 1) == 0
    return jnp.where(even, -r2, r1)


def _rope64(x, cos, sin):
    return x * cos + _rot64(x) * sin


def _rope64_t(d, cos, sin):
    return d * cos - _rot64(d * sin)


def _kprep_fwd(p, gk, cos, sin, *, name):
    t = p.shape[0]

    def body(k_ref, g_ref, c_ref, s_ref, o_ref):
        xh, _ = _pair_norm(k_ref[...], None)
        o_ref[...] = _rope64(xh * g_ref[...], c_ref[...], s_ref[...])

    blk = pl.BlockSpec((TM, 128), lambda i: (i, 0))
    return _pcall(
        body, name=name, grid=(t // TM,),
        in_specs=[pl.BlockSpec((TM, 128), lambda i: (i, 4)), pl.BlockSpec((1, 128), lambda i: (0, 0)), blk, blk],
        out_specs=blk, out_shape=jax.ShapeDtypeStruct((t, 128), F32),
    )(p, gk, cos, sin)


def _kprep_bwd(p, gk, cos, sin, dkp, dv, *, name):
    t = p.shape[0]

    def body(k_ref, g_ref, c_ref, s_ref, dkp_ref, dv_ref, o_ref, dg_ref):
        @pl.when(pl.program_id(0) == 0)
        def _():
            dg_ref[...] = jnp.zeros_like(dg_ref)
        xh, rs = _pair_norm(k_ref[...], None)
        dn = _rope64_t(dkp_ref[...], c_ref[...], s_ref[...])
        _acc_row(dg_ref, 0, jnp.sum(dn * xh, axis=0, keepdims=True))
        dxh = dn * g_ref[...]
        o_ref[:, 0:128] = (rs * (dxh - xh * _pair_mean(dxh * xh))).astype(BF16)
        o_ref[:, 128:256] = dv_ref[...].astype(BF16)

    blk = pl.BlockSpec((TM, 128), lambda i: (i, 0))
    return _pcall(
        body, name=name, grid=(t // TM,),
        in_specs=[pl.BlockSpec((TM, 128), lambda i: (i, 4)), pl.BlockSpec((1, 128), lambda i: (0, 0)), blk, blk, blk, blk],
        out_specs=[pl.BlockSpec((TM, 256), lambda i: (i, 0)), pl.BlockSpec((8, 128), lambda i: (0, 0))],
        out_shape=[jax.ShapeDtypeStruct((t, 256), BF16), jax.ShapeDtypeStruct((8, 128), F32)],
    )(p, gk, cos, sin, dkp, dv)


def _attn_common(i, t, lc, kp_ref, v_ref):
    span = QB + 2 * WINDOW
    start = pl.multiple_of(jnp.clip((i - 1) * QB, lc, t - span), QB)
    kall = jnp.concatenate([kp_ref[0:lc, :], kp_ref[pl.ds(start, span), :]], axis=0)
    vall = jnp.concatenate([v_ref[0:lc, :], v_ref[pl.ds(start, span), :]], axis=0)
    nk = lc + span
    col = _iota((QB, nk), 1)
    krow = jnp.where(col < lc, col, start + col - lc)
    qrow = i * QB + _iota((QB, nk), 0)
    valid = (col < lc) | ((qrow >= lc) & (krow >= lc) & (jnp.abs(krow - qrow) <= WINDOW))
    lo = _lane(kall.shape) < 64
    kroll, vroll = pltpu.roll(kall, 64, 1), pltpu.roll(vall, 64, 1)
    zero = jnp.zeros_like(kall)
    kvar = [[_bf(jnp.where(lo, kall, zero)), _bf(jnp.where(lo, zero, kroll))],
            [_bf(jnp.where(lo, kroll, zero)), _bf(jnp.where(lo, zero, kall))]]
    vvar = [[_bf(jnp.where(lo, vall, zero)), _bf(jnp.where(lo, zero, vroll))],
            [_bf(jnp.where(lo, vroll, zero)), _bf(jnp.where(lo, zero, vall))]]
    return start, valid, kvar, vvar


def _softmax_sink(s, valid, snk):
    s = jnp.where(valid, s, NEG)
    m = jnp.maximum(jnp.max(s, axis=-1, keepdims=True), snk)
    e = jnp.exp(s - m)
    es = jnp.exp(snk - m)
    inv = 1.0 / (jnp.sum(e, axis=-1, keepdims=True) + es)
    return e * inv, es * inv


def _attn_fwd(p, kp, gq, sink, cos, sin, *, lc, name):
    t = p.shape[0]
    scale = 64 ** -0.5

    def body(q_ref, kp_ref, v_ref, g_ref, sink_ref, c_ref, s_ref, o_ref):
        i = pl.program_id(0)
        _, valid, kvar, vvar = _attn_common(i, t, lc, kp_ref, v_ref)
        cosv, sinv, gv = c_ref[...], s_ref[...], g_ref[...]
        for j in range(4):
            xh, _ = _pair_norm(q_ref[:, 128 * j:128 * j + 128], None)
            q2 = _bf(_rope64(xh * gv, cosv, sinv))
            acc = jnp.zeros((QB, 128), F32)
            for half in range(2):
                s = _dot_nt(q2, kvar[j // 2][half]) * scale
                pr, _ = _softmax_sink(s, valid, sink_ref[2 * j + half])
                acc = acc + _dot(pr, vvar[j // 2][half])
            o_ref[:, 128 * j:128 * j + 128] = acc.astype(BF16)

    qblk = pl.BlockSpec((QB, 128), lambda i: (i, 0))
    return _pcall(
        body, name=name, grid=(t // QB,),
        in_specs=[pl.BlockSpec((QB, 512), lambda i: (i, 0)),
                  pl.BlockSpec((t, 128), lambda i: (0, 0)),
                  pl.BlockSpec((t, 128), lambda i: (0, 5)),
                  pl.BlockSpec((1, 128), lambda i: (0, 0)),
                  pl.BlockSpec(memory_space=pltpu.SMEM), qblk, qblk],
        out_specs=pl.BlockSpec((QB, 512), lambda i: (i, 0)),
        out_shape=jax.ShapeDtypeStruct((t, 512), BF16),
    )(p, kp, p, gq, sink, cos, sin)


def _attn_bwd(p, kp, gq, sink, cos, sin, dmix, *, lc, name):
    t = p.shape[0]
    scale = 64 ** -0.5
    span = QB + 2 * WINDOW

    def body(q_ref, kp_ref, v_ref, g_ref, sink_ref, c_ref, s_ref, do_ref,
             dq_ref, dk_ref, dv_ref, dg_ref, dsink_ref):
        i = pl.program_id(0)

        @pl.when(i == 0)
        def _():
            dk_ref[...] = jnp.zeros_like(dk_ref)
            dv_ref[...] = jnp.zeros_like(dv_ref)
            dg_ref[...] = jnp.zeros_like(dg_ref)
            dsink_ref[...] = jnp.zeros_like(dsink_ref)

        start, valid, kvar, vvar = _attn_common(i, t, lc, kp_ref, v_ref)
        cosv, sinv, gv = c_ref[...], s_ref[...], g_ref[...]
        nk = lc + span
        lo = _lane((nk, 128)) < 64
        dk_all = jnp.zeros((nk, 128), F32)
        dv_all = jnp.zeros((nk, 128), F32)
        for j in range(4):
            kvh = j // 2
            xh, rs = _pair_norm(q_ref[:, 128 * j:128 * j + 128], None)
            q2 = _bf(_rope64(xh * gv, cosv, sinv))
            do2 = _bf(do_ref[:, 128 * j:128 * j + 128])
            dq2 = jnp.zeros((QB, 128), F32)
            for half in range(2):
                s = _dot_nt(q2, kvar[kvh][half]) * scale
                pr, ps = _softmax_sink(s, valid, sink_ref[2 * j + half])
                dp = _dot_nt(do2, vvar[kvh][half])
                delta = jnp.sum(pr * dp, axis=-1, keepdims=True)
                ds = pr * (dp - delta) * scale
                dsk = jnp.sum(jnp.sum(-ps * delta, axis=0, keepdims=True), axis=1, keepdims=True)
                _acc_row(dsink_ref, 2 * j + half, jnp.broadcast_to(dsk, (1, 128)))
                dq2 = dq2 + _dot(ds, kvar[kvh][half])
                gk_ = _dot_tn(ds, q2)
                gv_ = _dot_tn(pr, do2)
                if half == 0:
                    gk_, gv_ = jnp.where(lo, gk_, 0.0), jnp.where(lo, gv_, 0.0)
                else:
                    gk_, gv_ = jnp.where(lo, 0.0, gk_), jnp.where(lo, 0.0, gv_)
                if half != kvh:
                    gk_, gv_ = pltpu.roll(gk_, 64, 1), pltpu.roll(gv_, 64, 1)
                dk_all = dk_all + gk_
                dv_all = dv_all + gv_
            dn = _rope64_t(dq2, cosv, sinv)
            _acc_row(dg_ref, 0, jnp.sum(dn * xh, axis=0, keepdims=True))
            dxh = dn * gv
            dq_ref[:, 128 * j:128 * j + 128] = (rs * (dxh - xh * _pair_mean(dxh * xh))).astype(BF16)
        dk_ref[0:lc, :] += dk_all[0:lc]
        dv_ref[0:lc, :] += dv_all[0:lc]
        dk_ref[pl.ds(start, span), :] += dk_all[lc:nk]
        dv_ref[pl.ds(start, span), :] += dv_all[lc:nk]

    qblk = pl.BlockSpec((QB, 128), lambda i: (i, 0))
    full = pl.BlockSpec((t, 128), lambda i: (0, 0))
    small = pl.BlockSpec((8, 128), lambda i: (0, 0))
    return _pcall(
        body, name=name, grid=(t // QB,),
        in_specs=[pl.BlockSpec((QB, 512), lambda i: (i, 0)), full,
                  pl.BlockSpec((t, 128), lambda i: (0, 5)),
                  pl.BlockSpec((1, 128), lambda i: (0, 0)),
                  pl.BlockSpec(memory_space=pltpu.SMEM), qblk, qblk,
                  pl.BlockSpec((QB, 512), lambda i: (i, 0))],
        out_specs=[pl.BlockSpec((QB, 512), lambda i: (i, 0)), full, full, small, small],
        out_shape=[jax.ShapeDtypeStruct((t, 512), BF16), jax.ShapeDtypeStruct((t, 128), F32),
                   jax.ShapeDtypeStruct((t, 128), F32), jax.ShapeDtypeStruct((8, 128), F32),
                   jax.ShapeDtypeStruct((8, 128), F32)],
    )(p, kp, p, gq, sink, cos, sin, dmix)


def _tri(rev):
    r, c = _iota((CHUNK, CHUNK), 0), _iota((CHUNK, CHUNK), 1)
    return (c >= r) if rev else (c <= r)


def _blk_map(nb, rev, backward):
    if not rev:
        return (lambda n: nb - 1 - n) if backward else (lambda n: n)
    if backward:
        return lambda n: jnp.where(n < nb - 1, n + 1, 0)
    return lambda n: jnp.where(n == 0, 0, nb - n)


def _chunk_order(rev, backward, nc=TM // CHUNK):
    order = list(range(nc))
    return order[::-1] if (rev != backward) else order


def _hgrn_gates(qraw, fraw, lb):
    sq = _sigmoid(qraw)
    sf = _sigmoid(fraw)
    f = lb + (1.0 - lb) * sf
    return qraw * sq, 1.0 - f, jnp.log(f), sq, sf, f


HGRN_HP = 2


def _chunk_cumsum(x, rev):
    n = x.shape[0]
    pos = _iota(x.shape, 0) & (CHUNK - 1)
    s = 1
    while s < CHUNK:
        if rev:
            x = x + jnp.where(pos < CHUNK - s, pltpu.roll(x, n - s, 0), 0.0)
        else:
            x = x + jnp.where(pos >= s, pltpu.roll(x, s, 0), 0.0)
        s *= 2
    return x


def _block_terms(lf, rev):
    b = _chunk_cumsum(lf, rev)
    mid, last = (CHUNK // 2 - 1, 0) if rev else (CHUNK // 2, CHUNK - 1)

    def chunk_row(off):
        return jnp.concatenate([jnp.broadcast_to(b[c * CHUNK + off:c * CHUNK + off + 1, :], (CHUNK, b.shape[1]))
                                for c in range(TM // CHUNK)], axis=0)

    r, bl = chunk_row(mid), chunk_row(last)
    return _tri(rev), jnp.exp(b - r), jnp.exp(r - b), jnp.exp(b), jnp.exp(bl - b), jnp.exp(bl)


def _hgrn_fwd(p, lb, *, rev, name):
    t = p.shape[0]
    nb, nc = t // TM, TM // CHUNK
    bmap = _blk_map(nb, rev, False)
    fcol = 14 if rev else 10

    def body(q_ref, f_ref, v_ref, lb_ref, o_ref, sh_ref, st):
        @pl.when(pl.program_id(1) == 0)
        def _():
            st[...] = jnp.zeros_like(st)
        for hh in range(HGRN_HP):
            ln = slice(128 * hh, 128 * hh + 128)
            q, k, lf, _, _, _ = _hgrn_gates(q_ref[:, ln], f_ref[:, ln], lb_ref[:, ln])
            tri, eq, ek, ei, eki, eb = _block_terms(lf, rev)
            qe, ke, qi, ki, vb = _bf(q * eq), _bf(k * ek), _bf(q * ei), _bf(k * eki), _bf(v_ref[:, ln])
            intra = []
            for cc in range(nc):
                rows = slice(cc * CHUNK, (cc + 1) * CHUNK)
                a = jnp.where(tri, _dot_nt(qe[rows], ke[rows]), 0.0)
                intra.append(_dot(a, vb[rows]))
            s = st[hh]
            for cc in _chunk_order(rev, False):
                rows = slice(cc * CHUNK, (cc + 1) * CHUNK)
                sh_ref[hh, cc] = s
                o_ref[rows, ln] = intra[cc] + _dot_nt(qi[rows], s)
                s = s * eb[cc * CHUNK:cc * CHUNK + 1, :] + _dot_tn(vb[rows], ki[rows])
            st[hh] = s

    hp, wd = HGRN_HP, 128 * HGRN_HP

    def col(c0):
        return pl.BlockSpec((TM, wd), lambda h, n: (bmap(n), c0 // hp + h))

    return _pcall(
        body, name=name, grid=(4 // hp, nb),
        in_specs=[col(6), col(fcol), col(18), pl.BlockSpec((1, wd), lambda h, n: (0, h))],
        out_specs=[pl.BlockSpec((TM, wd), lambda h, n: (bmap(n), h)),
                   pl.BlockSpec((hp, nc, 128, 128), lambda h, n: (h, bmap(n), 0, 0))],
        out_shape=[jax.ShapeDtypeStruct((t, 512), F32), jax.ShapeDtypeStruct((4, t // CHUNK, 128, 128), F32)],
        scratch_shapes=[pltpu.VMEM((hp, 128, 128), F32)],
    )(p, p, p, lb)


def _hgrn_bwd(p, lb, sh, do, prev, *, rev, name):
    t = p.shape[0]
    nb, nc = t // TM, TM // CHUNK
    bmap = _blk_map(nb, rev, True)
    fcol = 14 if rev else 10
    has_prev = prev is not None
    odt = BF16 if has_prev else F32

    def body(*refs):
        refs = list(refs)
        q_ref, f_ref, v_ref, lb_ref, sh_ref, do_ref = refs[:6]
        pos = 6
        if has_prev:
            pq_ref, pv_ref = refs[6], refs[7]
            pos = 8
        dq_ref, df_ref, dv_ref, dlb_ref, dst = refs[pos:pos + 5]

        @pl.when(pl.program_id(1) == 0)
        def _():
            dst[...] = jnp.zeros_like(dst)
            dlb_ref[...] = jnp.zeros_like(dlb_ref)

        cat = functools.partial(jnp.concatenate, axis=0)
        for hh in range(HGRN_HP):
            ln = slice(128 * hh, 128 * hh + 128)
            lbv = lb_ref[:, ln]
            qraw, fraw = q_ref[:, ln], f_ref[:, ln]
            q, k, lf, sq, sf, f = _hgrn_gates(qraw, fraw, lbv)
            tri, eq, ek, ei, eki, eb = _block_terms(lf, rev)
            qe, ke, qi, ki = q * eq, k * ek, q * ei, k * eki
            qeb, keb, qib, kib, vb, dob = _bf(qe), _bf(ke), _bf(qi), _bf(ki), _bf(v_ref[:, ln]), _bf(do_ref[:, ln])
            dv, dqe, dke, dqi = [None] * nc, [None] * nc, [None] * nc, [None] * nc
            for cc in range(nc):
                rows = slice(cc * CHUNK, (cc + 1) * CHUNK)
                a = jnp.where(tri, _dot_nt(qeb[rows], keb[rows]), 0.0)
                da = jnp.where(tri, _dot_nt(dob[rows], vb[rows]), 0.0)
                dv[cc] = _dot_tn(a, dob[rows])
                dqe[cc], dke[cc] = _dot(da, keb[rows]), _dot_tn(da, qeb[rows])
                dqi[cc] = _dot(dob[rows], sh_ref[hh, cc])
            dki, dbl = [None] * nc, [None] * nc
            ds = dst[hh]
            for cc in _chunk_order(rev, True):
                rows = slice(cc * CHUNK, (cc + 1) * CHUNK)
                ebc = eb[cc * CHUNK:cc * CHUNK + 1, :]
                dv[cc] = dv[cc] + _dot_nt(kib[rows], ds)
                dki[cc] = _dot(vb[rows], ds)
                dbl[cc] = jnp.broadcast_to(jnp.sum(dki[cc] * ki[rows], axis=0, keepdims=True)
                                           + jnp.sum(ds * sh_ref[hh, cc], axis=0, keepdims=True) * ebc, (CHUNK, 128))
                ds = ds * ebc + _dot_tn(dob[rows], qib[rows])
            dst[hh] = ds
            dqe, dke, dqi, dki, dv, dbl = cat(dqe), cat(dke), cat(dqi), cat(dki), cat(dv), cat(dbl)
            dq = dqe * eq + dqi * ei
            dk = dke * ek + dki * eki
            last = 0 if rev else CHUNK - 1
            db = dqe * qe - dke * ke + dqi * qi - dki * ki
            db = db + jnp.where((_iota(db.shape, 0) & (CHUNK - 1)) == last, dbl, 0.0)
            dlf = _chunk_cumsum(db, not rev)
            dqr = dq * (sq * (1.0 + qraw * (1.0 - sq)))
            dfv = dlf / f - dk
            dfr = dfv * (1.0 - lbv) * (sf * (1.0 - sf))
            dlb_ref[:, ln] += jnp.sum(dfv * (1.0 - sf), axis=0, keepdims=True)
            if has_prev:
                dqr = dqr + pq_ref[:, ln]
                dv = dv + pv_ref[:, ln]
            dq_ref[:, ln] = dqr.astype(odt)
            df_ref[:, ln] = dfr.astype(odt)
            dv_ref[:, ln] = dv.astype(odt)

    hp, wd = HGRN_HP, 128 * HGRN_HP

    def col(c0):
        return pl.BlockSpec((TM, wd), lambda h, n: (bmap(n), c0 // hp + h))

    oblk = pl.BlockSpec((TM, wd), lambda h, n: (bmap(n), h))
    ins = [p, p, p, lb, sh, do]
    specs = [col(6), col(fcol), col(18), pl.BlockSpec((1, wd), lambda h, n: (0, h)),
             pl.BlockSpec((hp, nc, 128, 128), lambda h, n: (h, bmap(n), 0, 0)), oblk]
    if has_prev:
        ins += list(prev); specs += [oblk, oblk]
    return _pcall(
        body, name=name, grid=(4 // hp, nb), in_specs=specs,
        out_specs=[oblk, oblk, oblk, pl.BlockSpec((1, wd), lambda h, n: (0, h))],
        out_shape=[jax.ShapeDtypeStruct((t, 512), odt)] * 3 + [jax.ShapeDtypeStruct((1, 512), F32)],
        scratch_shapes=[pltpu.VMEM((hp, 128, 128), F32)],
    )(*ins)


def _rope256(x, cos, sin):
    x1, x2 = x[:, 0:128], x[:, 128:256]
    return jnp.concatenate([x1 * cos - x2 * sin, x2 * cos + x1 * sin], axis=-1)


def _rope256_t(d, cos, sin):
    d1, d2 = d[:, 0:128], d[:, 128:256]
    return jnp.concatenate([d1 * cos + d2 * sin, d2 * cos - d1 * sin], axis=-1)


RET_DK, RET_DV, RET_H = 256, 512, 4
RET_KSCALE = RET_DK ** -0.5
RCH = TM
RET_HP = 2


def _ret_terms(lg, rev):
    r, c = _iota((RCH, RCH), 0), _iota((RCH, RCH), 1)
    rel = ((c - r) if rev else (r - c)).astype(F32)
    dmat = jnp.where(rel >= 0, jnp.exp(lg[:, 0:1] * jnp.maximum(rel, 0.0)), 0.0)
    pos = _iota((RCH, 1), 0).astype(F32)
    cnt = (RCH - pos) if rev else (pos + 1.0)
    ei = jnp.exp(lg * cnt)
    eki = jnp.exp(lg * (RCH - cnt))
    eb = jnp.exp(lg * float(RCH))
    return dmat, ei, eki, eb


def _ret_fwd(p, lgt, cos, sin, *, rev, name):
    t = p.shape[0]
    nb, nc = t // TM, TM // RCH
    bmap = _blk_map(nb, rev, False)

    def body(q_ref, k_ref, v_ref, lg_ref, c_ref, s_ref, o_ref, sh_ref, st):
        @pl.when(pl.program_id(1) == 0)
        def _():
            st[...] = jnp.zeros_like(st)
        for hh in range(RET_HP):
            qc, vc = slice(RET_DK * hh, RET_DK * (hh + 1)), slice(RET_DV * hh, RET_DV * (hh + 1))
            dmat, ei, eki, eb = _ret_terms(lg_ref[hh], rev)
            for cc in _chunk_order(rev, False, nc):
                rows = slice(cc * RCH, (cc + 1) * RCH)
                cosv, sinv = c_ref[rows, :], s_ref[rows, :]
                q = _rope256(q_ref[rows, qc], cosv, sinv)
                k = _rope256(k_ref[rows, qc], cosv, sinv) * RET_KSCALE
                v = v_ref[rows, vc]
                s0 = st[hh]
                sh_ref[hh, cc] = s0.astype(BF16)
                a = _dot_nt(q, k) * dmat
                o_ref[rows, vc] = _dot(a, v) + _dot_nt(q * ei, s0)
                st[hh] = s0 * eb + _dot_tn(v, k * eki)

    hp = RET_HP
    tab = pl.BlockSpec((TM, 128), lambda h, n: (bmap(n), 0))
    return _pcall(
        body, name=name, grid=(RET_H // hp, nb),
        in_specs=[pl.BlockSpec((TM, hp * RET_DK), lambda h, n: (bmap(n), h)),
                  pl.BlockSpec((TM, hp * RET_DK), lambda h, n: (bmap(n), RET_H // hp + h)),
                  pl.BlockSpec((TM, hp * RET_DV), lambda h, n: (bmap(n), RET_H // hp + h)),
                  pl.BlockSpec((hp, 1, RET_DK), lambda h, n: (h, 0, 0)), tab, tab],
        out_specs=[pl.BlockSpec((TM, hp * RET_DV), lambda h, n: (bmap(n), h)),
                   pl.BlockSpec((hp, nc, RET_DV, RET_DK), lambda h, n: (h, bmap(n), 0, 0))],
        out_shape=[jax.ShapeDtypeStruct((t, RET_H * RET_DV), F32),
                   jax.ShapeDtypeStruct((RET_H, t // RCH, RET_DV, RET_DK), BF16)],
        scratch_shapes=[pltpu.VMEM((hp, RET_DV, RET_DK), F32)],
    )(p, p, p, lgt, cos, sin)


def _ret_bwd(p, lgt, cos, sin, sh, do, prev, *, rev, name):
    t = p.shape[0]
    nb, nc = t // TM, TM // RCH
    bmap = _blk_map(nb, rev, True)
    has_prev = prev is not None
    odt = BF16 if has_prev else F32

    def body(*refs):
        refs = list(refs)
        q_ref, k_ref, v_ref, lg_ref, c_ref, s_ref, sh_ref, do_ref = refs[:8]
        pos = 8
        if has_prev:
            pq_ref, pk_ref, pv_ref = refs[8:11]
            pos = 11
        dq_ref, dk_ref, dv_ref, dst = refs[pos:pos + 4]

        @pl.when(pl.program_id(1) == 0)
        def _():
            dst[...] = jnp.zeros_like(dst)

        for hh in range(RET_HP):
            qc, vc = slice(RET_DK * hh, RET_DK * (hh + 1)), slice(RET_DV * hh, RET_DV * (hh + 1))
            dmat, ei, eki, eb = _ret_terms(lg_ref[hh], rev)
            for cc in _chunk_order(rev, True, nc):
                rows = slice(cc * RCH, (cc + 1) * RCH)
                cosv, sinv = c_ref[rows, :], s_ref[rows, :]
                q = _rope256(q_ref[rows, qc], cosv, sinv)
                k = _rope256(k_ref[rows, qc], cosv, sinv) * RET_KSCALE
                v = v_ref[rows, vc]
                dov = do_ref[rows, vc]
                s0 = sh_ref[hh, cc]
                dsc = dst[hh]
                qi, ki = q * ei, k * eki
                a = _dot_nt(q, k) * dmat
                da = _dot_nt(dov, v) * dmat
                dv = _dot_tn(a, dov) + _dot_nt(ki, dsc)
                dqs = _dot(da, k) + _dot(dov, s0) * ei
                dks = _dot_tn(da, q) + _dot(v, dsc) * eki
                dst[hh] = dsc * eb + _dot_tn(dov, qi)
                dq = _rope256_t(dqs, cosv, sinv)
                dk = _rope256_t(dks * RET_KSCALE, cosv, sinv)
                if has_prev:
                    dq = dq + pq_ref[rows, qc]
                    dk = dk + pk_ref[rows, qc]
                    dv = dv + pv_ref[rows, vc]
                dq_ref[rows, qc] = dq.astype(odt)
                dk_ref[rows, qc] = dk.astype(odt)
                dv_ref[rows, vc] = dv.astype(odt)

    hp = RET_HP
    tab = pl.BlockSpec((TM, 128), lambda h, n: (bmap(n), 0))
    qblk = pl.BlockSpec((TM, hp * RET_DK), lambda h, n: (bmap(n), h))
    vblk = pl.BlockSpec((TM, hp * RET_DV), lambda h, n: (bmap(n), h))
    ins = [p, p, p, lgt, cos, sin, sh, do]
    specs = [qblk, pl.BlockSpec((TM, hp * RET_DK), lambda h, n: (bmap(n), RET_H // hp + h)),
             pl.BlockSpec((TM, hp * RET_DV), lambda h, n: (bmap(n), RET_H // hp + h)),
             pl.BlockSpec((hp, 1, RET_DK), lambda h, n: (h, 0, 0)), tab, tab,
             pl.BlockSpec((hp, nc, RET_DV, RET_DK), lambda h, n: (h, bmap(n), 0, 0)), vblk]
    if has_prev:
        ins += list(prev); specs += [qblk, qblk, vblk]
    return _pcall(
        body, name=name, grid=(RET_H // hp, nb), in_specs=specs,
        out_specs=[qblk, qblk, vblk],
        out_shape=[jax.ShapeDtypeStruct((t, RET_H * RET_DK), odt), jax.ShapeDtypeStruct((t, RET_H * RET_DK), odt),
                   jax.ShapeDtypeStruct((t, RET_H * RET_DV), odt)],
        scratch_shapes=[pltpu.VMEM((hp, RET_DV, RET_DK), F32)],
    )(*ins)


def _headnorm_fwd(ofw, obw, p, gain, *, dv, gcol, name):
    t, w = ofw.shape
    nh = w // dv
    has_gain = gain is not None

    def body(*refs):
        a_ref, b_ref, g_ref = refs[:3]
        gain_ref = refs[3] if has_gain else None
        o_ref = refs[-1]
        o = a_ref[...] + b_ref[...]
        n = o * lax.rsqrt(jnp.mean(o * o, axis=-1, keepdims=True) + EPS)
        if has_gain:
            n = n * gain_ref[...]
        gv = g_ref[...]
        o_ref[...] = (n * (gv * _sigmoid(gv))).astype(BF16)

    blk = pl.BlockSpec((TM, dv), lambda i, h: (i, h))
    ins, specs = [ofw, obw, p], [blk, blk, pl.BlockSpec((TM, dv), lambda i, h: (i, gcol + h))]
    if has_gain:
        ins.append(gain); specs.append(pl.BlockSpec((1, dv), lambda i, h: (0, 0)))
    return _pcall(body, name=name, grid=(t // TM, nh), in_specs=specs, out_specs=blk,
                  out_shape=jax.ShapeDtypeStruct((t, w), BF16))(*ins)


def _headnorm_bwd(ofw, obw, p, gain, dmix, *, dv, gcol, mcol, name):
    t, w = ofw.shape
    nh = w // dv
    has_gain = gain is not None

    def body(*refs):
        a_ref, b_ref, g_ref, dm_ref = refs[:4]
        gain_ref = refs[4] if has_gain else None
        do_ref, dg_ref, dgain_ref = refs[-3:]

        @pl.when((pl.program_id(0) == 0) & (pl.program_id(1) == 0))
        def _():
            dgain_ref[...] = jnp.zeros_like(dgain_ref)

        o = a_ref[...] + b_ref[...]
        rs = lax.rsqrt(jnp.mean(o * o, axis=-1, keepdims=True) + EPS)
        xh = o * rs
        n = xh * gain_ref[...] if has_gain else xh
        gv = g_ref[...]
        sg = _sigmoid(gv)
        dy = dm_ref[...]
        dn = dy * (gv * sg)
        dg_ref[...] = (dy * n * (sg * (1.0 + gv * (1.0 - sg)))).astype(BF16)
        _acc_row(dgain_ref, 0, jnp.sum(dn * xh, axis=0, keepdims=True))
        dxh = dn * gain_ref[...] if has_gain else dn
        do_ref[...] = rs * (dxh - xh * jnp.mean(dxh * xh, axis=-1, keepdims=True))

    blk = pl.BlockSpec((TM, dv), lambda i, h: (i, h))
    ins = [ofw, obw, p, dmix]
    specs = [blk, blk, pl.BlockSpec((TM, dv), lambda i, h: (i, gcol + h)),
             pl.BlockSpec((TM, dv), lambda i, h: (i, mcol + h))]
    if has_gain:
        ins.append(gain); specs.append(pl.BlockSpec((1, dv), lambda i, h: (0, 0)))
    return _pcall(
        body, name=name, grid=(t // TM, nh), in_specs=specs,
        out_specs=[blk, blk, pl.BlockSpec((8, dv), lambda i, h: (0, 0))],
        out_shape=[jax.ShapeDtypeStruct((t, w), F32), jax.ShapeDtypeStruct((t, w), BF16),
                   jax.ShapeDtypeStruct((8, dv), F32)],
    )(*ins)


def _rope_tables(lc, l):
    tt = jnp.arange(l)
    row, colp = (tt // 64).astype(F32), (tt % 64).astype(F32)
    inv = 10000.0 ** (-jnp.arange(16, dtype=F32) / 16)
    ang = jnp.concatenate([row[:, None] * inv, colp[:, None] * inv], axis=-1)
    ang = jnp.concatenate([jnp.zeros((lc, 32), F32), ang], axis=0)
    acos, asin = jnp.tile(jnp.cos(ang), (1, 4)), jnp.tile(jnp.sin(ang), (1, 4))
    theta = 1.0 / (10000.0 ** jnp.linspace(0.0, 1.0, 128, dtype=F32))
    rang = jnp.arange(l, dtype=F32)[:, None] * theta
    rang = jnp.concatenate([jnp.zeros((lc, 128), F32), rang], axis=0)
    return acos, asin, jnp.cos(rang), jnp.sin(rang)


def _local_step(x0, target, mods, ng, w, small):
    t, d = x0.shape
    l = target.shape[0]
    lc = t - l
    acos, asin, rcos, rsin = _rope_tables(lc, l)
    lg_fw = jnp.log(1.0 - 2.0 ** (-5.0 - jnp.arange(RET_H, dtype=F32)))
    lgt_fw = jnp.broadcast_to(lg_fw[:, None, None], (RET_H, 1, RET_DK))
    lgt_bw = jnp.broadcast_to(lg_fw[::-1][:, None, None], (RET_H, 1, RET_DK))
    gq, gk, sink, gain, lb = small['gq'], small['gk'], small['sink'], small['gain'], small['lb']

    (h1,) = _row_fwd(x0, mods, g=ng[0], shift=0, scale=1, name='l0_norm1')
    p0 = _mm_nn(h1, w['even_in'], name='l0_in')
    kp = _kprep_fwd(p0, gk, acos, asin, name='l0_kprep')
    att = _attn_fwd(p0, kp, gq, sink, acos, asin, lc=lc, name='l0_attn')
    hof, hsf = _hgrn_fwd(p0, lb, rev=False, name='l0_hgrn_f')
    hob, hsb = _hgrn_fwd(p0, lb, rev=True, name='l0_hgrn_b')
    bmix = _headnorm_fwd(hof, hob, p0, gain, dv=128, gcol=22, name='l0_headnorm')
    mix0 = jnp.concatenate([att, bmix], axis=1)
    y0 = _mm_nn(mix0, w['even_out'], name='l0_out')
    x1, h2 = _row_fwd(x0, mods, y=y0, gate=2, g=ng[1], shift=3, scale=4, name='l0_norm2')
    u0, a0 = _ffn_in(h2, w['ffn_in'], lead=0, name='ffn_in')
    z0 = _mm_nn(a0, w['ffn_out'], lead=0, name='ffn_out')
    x2, h3 = _row_fwd(x1, mods, y=z0, gate=5, g=ng[2], shift=12, scale=13, name='l1_norm1')
    p1 = _mm_nn(h3, w['odd_in'], name='l1_in')
    rof, rsf = _ret_fwd(p1, lgt_fw, rcos, rsin, rev=False, name='l1_ret_f')
    rob, rsb = _ret_fwd(p1, lgt_bw, rcos, rsin, rev=True, name='l1_ret_b')
    mix1 = _headnorm_fwd(rof, rob, p1, None, dv=RET_DV, gcol=8, name='l1_headnorm')
    y1 = _mm_nn(mix1, w['odd_out'], name='l1_out')
    x3, h4 = _row_fwd(x2, mods, y=y1, gate=14, g=ng[3], shift=15, scale=16, name='l1_norm2')
    u1, a1 = _ffn_in(h4, w['ffn_in'], lead=1, name='ffn_in')
    z1 = _mm_nn(a1, w['ffn_out'], lead=1, name='ffn_out')
    loss, dx4, dz1, s_fin = _row_final(x3, z1, mods, target, gate=17, name='loss')

    du1 = _ffn_dx(dz1, w['ffn_out'], u1, lead=1, name='ffn_out_dx')
    g_ffn_out1 = _mm_tn(a1, dz1, name='ffn_out_dw')
    dh4 = _mm_nt(du1, w['ffn_in'], lead=1, name='ffn_in_dx')
    g_ffn_in1 = _mm_tn(h4, du1, name='ffn_in_dw')
    dx3, dy1, s_l1n2 = _row_bwd(x3, dx4, dh4, mods, ng[3], shift=15, scale=16, y=y1, gate=14, name='l1_norm2_bwd')
    dmix1 = _mm_nt(dy1, w['odd_out'], name='l1_out_dx')
    g_odd_out = _mm_tn(mix1, dy1, name='l1_out_dw')
    rdo, rdg, _ = _headnorm_bwd(rof, rob, p1, None, dmix1, dv=RET_DV, gcol=8, mcol=0, name='l1_headnorm_bwd')
    part = _ret_bwd(p1, lgt_fw, rcos, rsin, rsf, rdo, None, rev=False, name='l1_ret_f_bwd')
    rdq, rdk, rdv = _ret_bwd(p1, lgt_bw, rcos, rsin, rsb, rdo, part, rev=True, name='l1_ret_b_bwd')
    dp1 = jnp.concatenate([rdq, rdk, rdv, rdg], axis=1)
    dh3 = _mm_nt(dp1, w['odd_in'], name='l1_in_dx')
    g_odd_in = _mm_tn(h3, dp1, name='l1_in_dw')
    dx2, dz0, s_l1n1 = _row_bwd(x2, dx3, dh3, mods, ng[2], shift=12, scale=13, y=z0, gate=5, name='l1_norm1_bwd')
    du0 = _ffn_dx(dz0, w['ffn_out'], u0, lead=0, name='ffn_out_dx')
    g_ffn_out0 = _mm_tn(a0, dz0, name='ffn_out_dw')
    dh2 = _mm_nt(du0, w['ffn_in'], lead=0, name='ffn_in_dx')
    g_ffn_in0 = _mm_tn(h2, du0, name='ffn_in_dw')
    dx1, dy0, s_l0n2 = _row_bwd(x1, dx2, dh2, mods, ng[1], shift=3, scale=4, y=y0, gate=2, name='l0_norm2_bwd')
    dmix0 = _mm_nt(dy0, w['even_out'], name='l0_out_dx')
    g_even_out = _mm_tn(mix0, dy0, name='l0_out_dw')
    hdo, hdg, s_gain = _headnorm_bwd(hof, hob, p0, gain, dmix0, dv=128, gcol=22, mcol=4, name='l0_headnorm_bwd')
    hq, hff, hv, dlb_f = _hgrn_bwd(p0, lb, hsf, hdo, None, rev=False, name='l0_hgrn_f_bwd')
    hq, hfb, hv, dlb_b = _hgrn_bwd(p0, lb, hsb, hdo, (hq, hv), rev=True, name='l0_hgrn_b_bwd')
    adq, dkp, adv, s_gq, s_sink = _attn_bwd(p0, kp, gq, sink, acos, asin, dmix0, lc=lc, name='l0_attn_bwd')
    dkv, s_gk = _kprep_bwd(p0, gk, acos, asin, dkp, adv, name='l0_kprep_bwd')
    dp0 = jnp.concatenate([adq, dkv, hq, _bf(hff), hfb, hv, hdg], axis=1)
    dh1 = _mm_nt(dp0, w['even_in'], name='l0_in_dx')
    g_even_in = _mm_tn(h1, dp0, name='l0_in_dw')
    dx0, s_l0n1 = _row_bwd(x0, dx1, dh1, mods, ng[0], shift=0, scale=1, name='l0_norm1_bwd')

    grads = dict(ffn_in=[g_ffn_in0, g_ffn_in1], ffn_out=[g_ffn_out0, g_ffn_out1],
                 even_in=g_even_in, even_out=g_even_out, odd_in=g_odd_in, odd_out=g_odd_out)
    sums = dict(fin=s_fin, l1n2=s_l1n2, l1n1=s_l1n1, l0n2=s_l0n2, l0n1=s_l0n1, gain=s_gain, gq=s_gq, gk=s_gk,
                sink=s_sink, dlb=dlb_f + dlb_b)
    return loss, dx0, grads, sums


def _place():
    return lax.axis_index("x"), lax.axis_index("y"), lax.axis_index("c")


def _ag8(blk, *, name):
    r, c = blk.shape
    flips = [(dx, dy, dc) for dx in (0, 1) for dy in (0, 1) for dc in (0, 1) if (dx, dy, dc) != (0, 0, 0)]

    def body(x_ref, out_ref, send_sems, recv_sems, local_sem):
        ax, ay, ac = _place()
        me = 4 * ax + 2 * ay + ac
        mine = pltpu.make_async_copy(x_ref, out_ref.at[me], local_sem)
        mine.start()
        sent = []
        for k, (dx, dy, dc) in enumerate(flips):
            peer = (lax.rem(ax + dx, 2), lax.rem(ay + dy, 2), lax.rem(ac + dc, 2))
            cp = pltpu.make_async_remote_copy(src_ref=x_ref, dst_ref=out_ref.at[me], send_sem=send_sems.at[k],
                                              recv_sem=recv_sems.at[k], device_id=peer, device_id_type=MESH)
            cp.start()
            sent.append((cp, 4 * peer[0] + 2 * peer[1] + peer[2]))
        for k, (cp, pidx) in enumerate(sent):
            pltpu.make_async_remote_copy(src_ref=x_ref, dst_ref=out_ref.at[pidx], send_sem=send_sems.at[k],
                                         recv_sem=recv_sems.at[k], device_id=(ax, ay, ac),
                                         device_id_type=MESH).wait_recv()
        for cp, _ in sent:
            cp.wait_send()
        mine.wait()

    return _pcall(
        body, name=name,
        in_specs=[pl.BlockSpec(memory_space=pltpu.VMEM)],
        out_specs=pl.BlockSpec(memory_space=pltpu.VMEM),
        out_shape=jax.ShapeDtypeStruct((8, r, c), blk.dtype),
        scratch_shapes=[pltpu.SemaphoreType.DMA((7,)), pltpu.SemaphoreType.DMA((7,)), pltpu.SemaphoreType.DMA],
    )(blk)


def _chip_exchange(arrs, *, scatter, name):
    n = len(arrs)
    rel = [(1, 0), (0, 1), (1, 1)]

    def body(*refs):
        ins, outs = refs[:n], refs[n:2 * n]
        send_sems, recv_sems, local_sems = refs[2 * n:]
        ax, ay, ac = _place()
        s = 2 * ax + ay
        started, local = [], []
        for a in range(n):
            lcp = pltpu.make_async_copy(ins[a].at[s] if scatter else ins[a], outs[a].at[s], local_sems.at[a])
            lcp.start()
            local.append(lcp)
            for r, (dx, dy) in enumerate(rel):
                px, py = lax.rem(ax + dx, 2), lax.rem(ay + dy, 2)
                sp = 2 * px + py
                cp = pltpu.make_async_remote_copy(
                    src_ref=ins[a].at[sp] if scatter else ins[a], dst_ref=outs[a].at[s],
                    send_sem=send_sems.at[3 * a + r], recv_sem=recv_sems.at[3 * a + r],
                    device_id=(px, py, ac), device_id_type=MESH)
                cp.start()
                started.append((cp, a, r, sp))
        for cp, a, r, sp in started:
            pltpu.make_async_remote_copy(
                src_ref=ins[a].at[sp] if scatter else ins[a], dst_ref=outs[a].at[sp],
                send_sem=send_sems.at[3 * a + r], recv_sem=recv_sems.at[3 * a + r],
                device_id=(ax, ay, ac), device_id_type=MESH).wait_recv()
        for cp, _, _, _ in started:
            cp.wait_send()
        for lcp in local:
            lcp.wait()

    hbm = pl.BlockSpec(memory_space=pl.ANY)
    shapes = [jax.ShapeDtypeStruct(a.shape if scatter else (4,) + a.shape, a.dtype) for a in arrs]
    return _pcall(
        body, name=name, in_specs=[hbm] * n, out_specs=[hbm] * n, out_shape=shapes,
        scratch_shapes=[pltpu.SemaphoreType.DMA((3 * n,)), pltpu.SemaphoreType.DMA((3 * n,)),
                        pltpu.SemaphoreType.DMA((n,))],
    )(*arrs)


def _gather_weights(arrs, *, name):
    n = len(arrs)
    rel = [(1, 0), (0, 1), (1, 1)]

    def body(*refs):
        ins, outs = refs[:n], refs[n:2 * n]
        ici_send, ici_recv, d2d_send, d2d_recv = refs[2 * n:]
        ax, ay, ac = _place()
        s = 2 * ax + ay
        sib = (ax, ay, 1 - ac)
        peers = [(lax.rem(ax + dx, 2), lax.rem(ay + dy, 2)) for dx, dy in rel]

        def half(a, slot, c):
            return outs[a].at[slot, c]

        def ici(a, r, src, slot, to):
            return pltpu.make_async_remote_copy(src_ref=src, dst_ref=half(a, slot, ac), send_sem=ici_send.at[3 * a + r],
                                                recv_sem=ici_recv.at[3 * a + r], device_id=to, device_id_type=MESH)

        def d2d(a, r, slot, c):
            return pltpu.make_async_remote_copy(src_ref=half(a, slot, c), dst_ref=half(a, slot, c),
                                                send_sem=d2d_send.at[3 * a + r], recv_sem=d2d_recv.at[3 * a + r],
                                                device_id=sib, device_id_type=MESH)

        sent = []
        for a in range(n):
            for r, (px, py) in enumerate(peers):
                cp = ici(a, r, ins[a].at[ac], s, (px, py, ac))
                cp.start()
                sent.append(cp)
        for a in range(n):
            for r, (px, py) in enumerate(peers):
                sp = 2 * px + py
                ici(a, r, half(a, sp, ac), sp, (ax, ay, ac)).wait_recv()
                fw = d2d(a, r, sp, ac)
                fw.start()
                sent.append(fw)
        for a in range(n):
            for r, (px, py) in enumerate(peers):
                d2d(a, r, 2 * px + py, 1 - ac).wait_recv()
        for cp in sent:
            cp.wait_send()

    hbm = pl.BlockSpec(memory_space=pl.ANY)
    return _pcall(
        body, name=name, in_specs=[hbm] * n, out_specs=[hbm] * n,
        out_shape=[jax.ShapeDtypeStruct((4,) + a.shape, a.dtype) for a in arrs],
        scratch_shapes=[pltpu.SemaphoreType.DMA((3 * n,))] * 4,
    )(*arrs)


def _to_sibling(arrs, *, name):
    n = len(arrs)

    def body(*refs):
        ins, outs = refs[:n], refs[n:2 * n]
        send_sems, recv_sems = refs[2 * n:]
        ax, ay, ac = _place()
        cps = [pltpu.make_async_remote_copy(src_ref=ins[a], dst_ref=outs[a], send_sem=send_sems.at[a],
                                            recv_sem=recv_sems.at[a], device_id=(ax, ay, 1 - ac),
                                            device_id_type=MESH) for a in range(n)]
        for cp in cps:
            cp.start()
        for cp in cps:
            cp.wait_recv()
        for cp in cps:
            cp.wait_send()

    hbm = pl.BlockSpec(memory_space=pl.ANY)
    return _pcall(
        body, name=name, in_specs=[hbm] * n, out_specs=[hbm] * n,
        out_shape=[jax.ShapeDtypeStruct(a.shape, a.dtype) for a in arrs],
        scratch_shapes=[pltpu.SemaphoreType.DMA((n,))] * 2,
    )(*arrs)


def _mod_fwd(cond_raw, mw, mb, *, name):
    _, d, n = mw.shape

    def body(c_ref, w_ref, b_ref, o_ref):
        cv = c_ref[...]
        o_ref[...] = _dot(cv * _sigmoid(cv), w_ref[...]) + b_ref[...]

    return _pcall(
        body, name=name, grid=(2,),
        in_specs=[pl.BlockSpec((16, d), lambda l: (0, 0)), pl.BlockSpec((None, d, n), lambda l: (l, 0, 0)),
                  pl.BlockSpec((None, 1, n), lambda l: (l, 0, 0))],
        out_specs=pl.BlockSpec((None, 16, n), lambda l: (l, 0, 0)),
        out_shape=jax.ShapeDtypeStruct((2, 16, n), F32),
    )(cond_raw, mw, mb)


def _mod_bwd(cond_raw, dms, mw, *, name):
    _, d, n = mw.shape

    def body(c_ref, dm_ref, w_ref, gw_ref, dc_ref):
        @pl.when(pl.program_id(0) == 0)
        def _():
            dc_ref[...] = jnp.zeros_like(dc_ref)
        cv = c_ref[...]
        gw_ref[...] = _dot_tn(cv * _sigmoid(cv), dm_ref[...])
        dc_ref[...] += _dot_nt(dm_ref[...], w_ref[...])

    return _pcall(
        body, name=name, grid=(2,),
        in_specs=[pl.BlockSpec((16, d), lambda l: (0, 0)), pl.BlockSpec((None, 16, n), lambda l: (l, 0, 0)),
                  pl.BlockSpec((None, d, n), lambda l: (l, 0, 0))],
        out_specs=[pl.BlockSpec((None, d, n), lambda l: (l, 0, 0)), pl.BlockSpec((16, d), lambda l: (0, 0))],
        out_shape=[jax.ShapeDtypeStruct((2, d, n), F32), jax.ShapeDtypeStruct((16, d), F32)],
    )(cond_raw, dms, mw)


def _lb_fwd(hgrn_lb, *, name):
    def body(a_ref, o_ref):
        a0, a1 = a_ref[0:1, :], a_ref[1:2, :]
        m = jnp.maximum(a0, a1)
        e0, e1 = jnp.exp(a0 - m), jnp.exp(a1 - m)
        o_ref[...] = e0 / (e0 + e1)

    return _pcall(body, name=name, out_shape=jax.ShapeDtypeStruct((1, hgrn_lb.shape[1]), F32))(hgrn_lb)


PACK_ROWS = 40


def _small_finalize(gath, lb_pad, *, name):
    d = gath.shape[2]

    def body(g_ref, lb_ref, small_ref, glb_ref, gmb_ref, dm_ref):
        tot = g_ref[0]
        for e in range(1, 8):
            tot = tot + g_ref[e]
        for k in range(4):
            small_ref[k:k + 1, :] = tot[24 + 2 * k:25 + 2 * k, :] + tot[25 + 2 * k:26 + 2 * k, :]
        for k, r in ((4, 32), (5, 33)):
            v = tot[r:r + 1, :]
            small_ref[k:k + 1, :] = v + pltpu.roll(v, d - 64, 1)
        small_ref[6:7, :] = tot[34:35, :]
        small_ref[7:8, :] = tot[36:37, :]
        lbv = lb_ref[...]
        g0 = (tot[35:36, :] + tot[37:38, :]) * lbv * (1.0 - lbv)
        glb_ref[...] = jnp.zeros_like(glb_ref)
        glb_ref[0:1, :] = g0
        glb_ref[1:2, :] = -g0
        dm_ref[...] = jnp.zeros_like(dm_ref)
        for l in range(2):
            for part in range(6):
                rc, rl = l * 12 + part, l * 12 + 6 + part
                gmb_ref[l * 6 + part:l * 6 + part + 1, :] = tot[rc:rc + 1, :] + tot[rl:rl + 1, :]
                for e in range(8):
                    dm_ref[l, part, e:e + 1, :] = g_ref[e, rl:rl + 1, :]
                dm_ref[l, part, 8:9, :] = tot[rc:rc + 1, :]

    return _pcall(
        body, name=name,
        out_shape=[jax.ShapeDtypeStruct((8, d), F32), jax.ShapeDtypeStruct((8, d), F32),
                   jax.ShapeDtypeStruct((12, d), F32), jax.ShapeDtypeStruct((2, 6, 16, d), F32)],
    )(gath, lb_pad)


def _cctx_grad(gath, c_ctx2, *, name):
    def body(g_ref, c_ref, o_ref):
        tot = ((g_ref[0, 0:1, :] + g_ref[2, 0:1, :]) + g_ref[4, 0:1, :]) + g_ref[6, 0:1, :]
        cv = c_ref[...]
        s = _sigmoid(cv)
        o_ref[...] = tot * (s * (1.0 + cv * (1.0 - s)))

    return _pcall(body, name=name, out_shape=jax.ShapeDtypeStruct(c_ctx2.shape, F32))(gath, c_ctx2)


def _row_block(r, c, limit=256 * 1024):
    best = None
    for br in range(16, r + 1, 16):
        if r % br == 0 and br * c <= limit:
            best = br
    return best if best is not None else r


def _sum4(parts, *, name):
    _, r, c = parts.shape
    br = _row_block(r, c)

    def body(p_ref, o_ref):
        p = [p_ref[k].astype(F32) for k in range(4)]
        o_ref[...] = ((p[0] + p[1]) + p[2]) + p[3]

    return _pcall(body, name=name, grid=(r // br,),
                  in_specs=[pl.BlockSpec((4, br, c), lambda i: (0, i, 0))],
                  out_specs=pl.BlockSpec((br, c), lambda i: (i, 0)),
                  out_shape=jax.ShapeDtypeStruct((r, c), F32))(parts)


def _add2(a, b, *, name):
    r, c = a.shape
    br = _row_block(r, c)

    def body(a_ref, b_ref, o_ref):
        o_ref[...] = (a_ref[...].astype(F32) + b_ref[...].astype(F32)).astype(BF16)

    blk = pl.BlockSpec((br, c), lambda i: (i, 0))
    return _pcall(body, name=name, grid=(r // br,), in_specs=[blk, blk], out_specs=blk,
                  out_shape=jax.ShapeDtypeStruct((r, c), BF16))(a, b)


def _adam(w, gs, m, v, *, name):
    r, c = w.shape
    br = _row_block(r, c)
    ng = len(gs)
    c1 = 1.0 - ADAM_B1 ** ADAM_STEP
    c2 = 1.0 - ADAM_B2 ** ADAM_STEP

    def body(*refs):
        w_ref, m_ref, v_ref = refs[0], refs[1 + ng], refs[2 + ng]
        outs = refs[3 + ng:]
        g = refs[1][...]
        for k in range(1, ng):
            g = g + refs[1 + k][...]
        mn = ADAM_B1 * m_ref[...] + (1.0 - ADAM_B1) * g
        vn = ADAM_B2 * v_ref[...] + (1.0 - ADAM_B2) * (g * g)
        if ng > 1:
            outs[0][...] = g
        d_out, m_out, v_out = outs[-3:]
        m_out[...] = mn
        v_out[...] = vn
        d_out[...] = -ADAM_LR * ((mn / c1) / (jnp.sqrt(vn / c2) + ADAM_EPS) + ADAM_WD * w_ref[...])

    blk = pl.BlockSpec((br, c), lambda i: (i, 0))
    nout = 4 if ng > 1 else 3
    res = _pcall(body, name=name, grid=(r // br,), in_specs=[blk] * (3 + ng), out_specs=[blk] * nout,
                 out_shape=[jax.ShapeDtypeStruct((r, c), F32)] * nout)(w, *gs, m, v)
    return list(res) if ng > 1 else [gs[0]] + list(res)


def _grad_halves(name, g, ac):
    if name == 'ffn_in':
        per = [_ffn_deinterleave(gl).reshape(gl.shape[0], 4, gl.shape[1] // 4).transpose(1, 0, 2) for gl in g]
    elif name == 'ffn_out':
        per = [gl.reshape(4, gl.shape[0] // 4, gl.shape[1]) for gl in g]
    elif name in ('even_in', 'odd_in'):
        k, n4 = g.shape
        v = g.reshape(2, k // 2, 4, n4 // 4).transpose(0, 2, 1, 3)
        per = [v[0], v[1]]
    else:
        k4, n = g.shape
        v = g.reshape(4, 2, k4 // 8, n).transpose(1, 0, 2, 3)
        per = [v[0], v[1]]
    first = ac == 0
    return _bf(jnp.where(first, per[0], per[1])), _bf(jnp.where(first, per[1], per[0]))


def _from_shards(name, g):
    _, r, n = g.shape
    if name == 'ffn_in':
        return _ffn_interleave(g.reshape(4, 2, r // 2, n).transpose(1, 2, 0, 3).reshape(2, r // 2, 4 * n))
    if name == 'ffn_out':
        return g.reshape(4, 2, r // 2, n).transpose(1, 0, 2, 3).reshape(2, 2 * r, n)
    if name in ('even_in', 'odd_in'):
        return g.transpose(1, 0, 2).reshape(r, 4 * n)
    return g.reshape(4 * r, n)


def kernel(x, c, ctx, c_ctx, mod_w, mod_b, norm_g, ffn_w_in, ffn_w_out, even_w_in, even_w_out, attn_qk_norm_g, attn_sink, hgrn_out_norm_g, hgrn_lb, odd_w_in, odd_w_out, loss_target, m_c_ctx, m_mod_w, m_mod_b, m_norm_g, m_ffn_w_in, m_ffn_w_out, m_even_w_in, m_even_w_out, m_attn_qk_norm_g, m_attn_sink, m_hgrn_out_norm_g, m_hgrn_lb, m_odd_w_in, m_odd_w_out, v_c_ctx, v_mod_w, v_mod_b, v_norm_g, v_ffn_w_in, v_ffn_w_out, v_even_w_in, v_even_w_out, v_attn_qk_norm_g, v_attn_sink, v_hgrn_out_norm_g, v_hgrn_lb, v_odd_w_in, v_odd_w_out):
    d = x.shape[-1]
    lc = ctx.shape[1]
    assert lc == TM and d == 1024
    ax, ay, ac = _place()
    s = 2 * ax + ay
    me = 4 * ax + 2 * ay + ac
    nmod = mod_w.shape[2]

    def pad8(v):
        return jnp.pad(v, ((0, 8 - v.shape[0]), (0, 0)))

    pack = jnp.concatenate([pad8(c), pad8(norm_g.reshape(1, d))], axis=0)
    g1 = _ag8(pack, name='gather_cond')
    c_all = g1[:, 0, :]
    ng = g1[0::2, 8, :].reshape(4, 2, 2, d // 4).transpose(1, 2, 0, 3).reshape(4, d)

    names = ['ffn_in', 'ffn_out', 'even_in', 'even_out', 'odd_in', 'odd_out']
    shards = [_bf(v.reshape(-1, v.shape[-1])) for v in (ffn_w_in, ffn_w_out, even_w_in, even_w_out, odd_w_in, odd_w_out)]
    gathered = _gather_weights([a.reshape(2, a.shape[0] // 2, a.shape[1]) for a in shards], name='gather_weights')
    slot = lax.broadcasted_iota(jnp.int32, (4, 1, 1), 0)
    w = {nm: _from_shards(nm, jnp.where(slot == s, a[None], g.reshape((4,) + a.shape)))
         for nm, a, g in zip(names, shards, gathered)}

    cond_raw = jnp.concatenate([c_all, pad8(c_ctx.reshape(1, d))], axis=0)
    mb_sh = lax.dynamic_slice_in_dim(mod_b, s * nmod, nmod, axis=1).reshape(2, 1, nmod)
    mpart = _mod_fwd(cond_raw, mod_w, mb_sh, name='mod_fwd')
    g3 = _ag8(mpart.reshape(32, nmod), name='gather_mods')
    mods_full = g3[0::2].reshape(4, 2, 16, nmod).transpose(1, 2, 0, 3).reshape(2, 16, 4 * nmod)
    m_lat = lax.dynamic_index_in_dim(mods_full, me, axis=1, keepdims=False)
    mods = jnp.stack([mods_full[:, 8], m_lat], axis=1).reshape(24, d)

    lb = _lb_fwd(hgrn_lb, name='hgrn_lower_bound')
    small = dict(gq=jnp.tile(attn_qk_norm_g[0, 0], 2).reshape(1, 128), gk=jnp.tile(attn_qk_norm_g[0, 1], 2).reshape(1, 128),
                 sink=attn_sink[0], gain=hgrn_out_norm_g, lb=lb)
    x0 = jnp.concatenate([ctx[0], x[0]], axis=0)
    loss_t, dx0, grads, sums = _local_step(x0, loss_target[0], mods, ng, w, small)
    loss = lax.psum(loss_t[0, 0], ("x", "y", "c"))
    grad_x = dx0[lc:][None]

    def pad(v):
        return jnp.pad(v, ((0, 0), (0, d - v.shape[1])))

    sm = sums
    dm_rows = [
        [sm['l0n1'][0], sm['l0n1'][1], sm['l0n2'][2], sm['l0n2'][0], sm['l0n2'][1], sm['l1n1'][2]],
        [sm['l0n1'][4], sm['l0n1'][5], sm['l0n2'][6], sm['l0n2'][4], sm['l0n2'][5], sm['l1n1'][6]],
        [sm['l1n1'][0], sm['l1n1'][1], sm['l1n2'][2], sm['l1n2'][0], sm['l1n2'][1], sm['fin'][2]],
        [sm['l1n1'][4], sm['l1n1'][5], sm['l1n2'][6], sm['l1n2'][4], sm['l1n2'][5], sm['fin'][6]],
    ]
    rows = [r for grp in dm_rows for r in grp]
    for key in ('l0n1', 'l0n2', 'l1n1', 'l1n2'):
        rows += [sm[key][3], sm[key][7]]
    rows = jnp.stack(rows)
    extra = jnp.concatenate([pad(sm['gq'][0:1]), pad(sm['gk'][0:1]), pad(sm['gain'][0:1]), pad(sm['dlb']),
                             pad(sm['sink'][:, 0].reshape(1, 8)), jnp.zeros((3, d), F32)], axis=0)
    g4 = _ag8(jnp.concatenate([rows, extra], axis=0), name='gather_row_sums')
    small_g, glb, gmb, dmat = _small_finalize(g4, pad(lb), name='small_grads')
    dms = lax.dynamic_slice_in_dim(dmat.transpose(0, 2, 1, 3).reshape(2, 16, 6 * d), s * nmod, nmod, axis=2)
    g_mod_w, dcond = _mod_bwd(cond_raw, dms, mod_w, name='mod_bwd')
    g5 = _ag8(dcond[8:16], name='gather_dcond')
    g_c_ctx = _cctx_grad(g5, c_ctx.reshape(8, d // 8).reshape(1, d), name='c_ctx_grad')

    halves = [_grad_halves(nm, grads[nm], ac) for nm in names]
    mine = [h[0] for h in halves]
    theirs = _to_sibling([h[1] for h in halves], name='swap_core_halves')
    pair = [_add2(a.reshape(-1, a.shape[-1]), b.reshape(-1, b.shape[-1]), name='add_cores').reshape(a.shape)
            for a, b in zip(mine, theirs)]
    parts = _chip_exchange(pair, scatter=True, name='scatter_grads')
    half_sums = [_sum4(p, name='sum_chips') for p in parts]
    other = _to_sibling(half_sums, name='gather_core_halves')
    full = [jnp.concatenate([jnp.where(ac == 0, f, o), jnp.where(ac == 0, o, f)], axis=0)
            for f, o in zip(half_sums, other)]

    def upd(wv, gs, mv, vv, name):
        shp = wv.shape
        c2 = shp[-1]
        out = _adam(wv.reshape(-1, c2), [g.reshape(-1, c2) for g in gs], mv.reshape(-1, c2), vv.reshape(-1, c2), name=name)
        return [o.reshape(shp) for o in out]

    res = {}
    res['c_ctx'] = upd(c_ctx.reshape(8, d // 8), [g_c_ctx.reshape(8, d // 8)], m_c_ctx.reshape(8, d // 8), v_c_ctx.reshape(8, d // 8), 'adam_c_ctx')
    res['c_ctx'] = [o.reshape(d) for o in res['c_ctx']]
    res['mod_w'] = upd(mod_w, [g_mod_w], m_mod_w, v_mod_w, 'adam_mod_w')
    res['mod_b'] = upd(mod_b, [gmb.reshape(2, 6 * d)], m_mod_b, v_mod_b, 'adam_mod_b')
    g_ng = lax.dynamic_slice_in_dim(small_g[0:4].reshape(2, 2, d), s * (d // 4), d // 4, axis=2)
    res['norm_g'] = upd(norm_g, [g_ng], m_norm_g, v_norm_g, 'adam_norm_g')
    big = {nm: [g] for nm, g in zip(names, full)}
    res['ffn_w_in'] = upd(ffn_w_in, big['ffn_in'], m_ffn_w_in, v_ffn_w_in, 'adam_ffn_in')
    res['ffn_w_out'] = upd(ffn_w_out, big['ffn_out'], m_ffn_w_out, v_ffn_w_out, 'adam_ffn_out')
    res['even_w_in'] = upd(even_w_in, big['even_in'], m_even_w_in, v_even_w_in, 'adam_even_in')
    res['even_w_out'] = upd(even_w_out, big['even_out'], m_even_w_out, v_even_w_out, 'adam_even_out')
    g_qk = jnp.stack([small_g[4, 0:64], small_g[5, 0:64]]).reshape(1, 2, 64)
    res['attn_qk_norm_g'] = upd(attn_qk_norm_g, [g_qk], m_attn_qk_norm_g, v_attn_qk_norm_g, 'adam_qk_gain')
    res['attn_sink'] = upd(attn_sink, [small_g[7, 0:8].reshape(1, 8)], m_attn_sink, v_attn_sink, 'adam_sink')
    res['hgrn_out_norm_g'] = upd(hgrn_out_norm_g, [small_g[6, 0:128].reshape(1, 128)], m_hgrn_out_norm_g, v_hgrn_out_norm_g, 'adam_head_gain')
    res['hgrn_lb'] = upd(hgrn_lb, [glb[0:2, 0:hgrn_lb.shape[1]]], m_hgrn_lb, v_hgrn_lb, 'adam_hgrn_lb')
    res['odd_w_in'] = upd(odd_w_in, big['odd_in'], m_odd_w_in, v_odd_w_in, 'adam_odd_in')
    res['odd_w_out'] = upd(odd_w_out, big['odd_out'], m_odd_w_out, v_odd_w_out, 'adam_odd_out')

    order = ['c_ctx', 'mod_w', 'mod_b', 'norm_g', 'ffn_w_in', 'ffn_w_out', 'even_w_in', 'even_w_out',
             'attn_qk_norm_g', 'attn_sink', 'hgrn_out_norm_g', 'hgrn_lb', 'odd_w_in', 'odd_w_out']
    outs = [loss, grad_x]
    for k in range(4):
        outs += [res[nm][k] for nm in order]
    return tuple(outs)
```

```python
import functools
import math

import numpy as np
import jax
import jax.numpy as jnp
from jax import lax
from jax.experimental import pallas as pl
from jax.experimental.pallas import tpu as pltpu

F32 = jnp.float32
BF16 = jnp.bfloat16
EPS = 1e-6
TM = 256
CHUNK = 64
QB = 128
WINDOW = 128
NEG = -1e30
MESH = pl.DeviceIdType.MESH

ADAM_LR, ADAM_B1, ADAM_B2, ADAM_EPS, ADAM_WD, ADAM_STEP = 0.001, 0.9, 0.999, 1e-08, 0.01, 10


def _pcall(body, **kw):
    return pl.pallas_call(body, **kw)


def _pick(n, cap):
    best = None
    for m in range(128, min(n, cap) + 1, 128):
        if n % m == 0:
            best = m
    assert best is not None, (n, cap)
    return best


def _bf(x):
    return x.astype(BF16)


def _dot(a, b):
    return jnp.dot(_bf(a), _bf(b), preferred_element_type=F32)


def _dot_nt(a, b):
    return lax.dot_general(_bf(a), _bf(b), (((1,), (1,)), ((), ())), preferred_element_type=F32)


def _dot_tn(a, b):
    return lax.dot_general(_bf(a), _bf(b), (((0,), (0,)), ((), ())), preferred_element_type=F32)


def _dot_exact(a, b):
    return jnp.dot(a, b, preferred_element_type=F32, precision=lax.Precision.HIGHEST)


def _sigmoid(x):
    return 1.0 / (1.0 + jnp.exp(-x))


def _iota(shape, dim):
    return lax.broadcasted_iota(jnp.int32, shape, dim)


def _mm_nn(a, b, *, lead=None, out_dtype=F32, name):
    m, k = a.shape
    n = b.shape[-1]
    bm = 768 if m % 768 == 0 else TM
    bn = _pick(n, 1024)

    def body(a_ref, b_ref, o_ref):
        o_ref[...] = _dot(a_ref[...], b_ref[...]).astype(o_ref.dtype)

    if lead is None:
        b_spec = pl.BlockSpec((k, bn), lambda i, j: (0, j))
    else:
        b_spec = pl.BlockSpec((None, k, bn), lambda i, j: (lead, 0, j))
    return _pcall(
        body, name=name, grid=(m // bm, n // bn),
        in_specs=[pl.BlockSpec((bm, k), lambda i, j: (i, 0)), b_spec],
        out_specs=pl.BlockSpec((bm, bn), lambda i, j: (i, j)),
        out_shape=jax.ShapeDtypeStruct((m, n), out_dtype),
    )(a, b)


def _mm_nt(a, b, *, lead=None, name):
    m, n = a.shape
    k = b.shape[-2]
    bm = 768 if m % 768 == 0 else TM
    bk = _pick(k, 512)

    def body(a_ref, b_ref, o_ref):
        o_ref[...] = _dot_nt(a_ref[...], b_ref[...])

    if lead is None:
        b_spec = pl.BlockSpec((bk, n), lambda i, j: (j, 0))
    else:
        b_spec = pl.BlockSpec((None, bk, n), lambda i, j: (lead, j, 0))
    return _pcall(
        body, name=name, grid=(m // bm, k // bk),
        in_specs=[pl.BlockSpec((bm, n), lambda i, j: (i, 0)), b_spec],
        out_specs=pl.BlockSpec((bm, bk), lambda i, j: (i, j)),
        out_shape=jax.ShapeDtypeStruct((m, k), F32),
    )(a, b)


def _mm_tn(a, b, *, name):
    t, k = a.shape
    n = b.shape[1]
    bt = 768 if t % 768 == 0 else TM
    bk = _pick(k, 1536)
    bn = _pick(n, 1024) if n % 1024 == 0 or n < 1664 else _pick(n, 1664)

    def body(a_ref, b_ref, o_ref):
        @pl.when(pl.program_id(2) == 0)
        def _():
            o_ref[...] = jnp.zeros_like(o_ref)
        o_ref[...] += _dot_tn(a_ref[...], b_ref[...])

    return _pcall(
        body, name=name, grid=(k // bk, n // bn, t // bt),
        in_specs=[pl.BlockSpec((bt, bk), lambda i, j, s: (s, i)),
                  pl.BlockSpec((bt, bn), lambda i, j, s: (s, j))],
        out_specs=pl.BlockSpec((bk, bn), lambda i, j, s: (i, j)),
        out_shape=jax.ShapeDtypeStruct((k, n), F32),
    )(a, b)


def _mod_row(mods_ref, lat, idx):
    return jnp.where(lat, mods_ref[idx + 6:idx + 7, :], mods_ref[idx:idx + 1, :])


def _row_fwd(x, mods, *, y=None, gate=None, g=None, shift=None, scale=None, name):
    t, d = x.shape
    has_y, has_n = y is not None, g is not None

    def body(*refs):
        refs = list(refs)
        x_ref, mods_ref = refs[0], refs[1]
        pos = 2
        if has_y:
            y_ref = refs[pos]; pos += 1
        if has_n:
            g_ref = refs[pos]; pos += 1
        outs = refs[pos:]
        lat = pl.program_id(0) > 0
        x1 = x_ref[...]
        o = 0
        if has_y:
            x1 = x1 + _mod_row(mods_ref, lat, gate) * y_ref[...]
            outs[o][...] = x1; o += 1
        if has_n:
            rs = lax.rsqrt(jnp.mean(x1 * x1, axis=-1, keepdims=True) + EPS)
            hn = x1 * rs * g_ref[...]
            h = hn * (1.0 + _mod_row(mods_ref, lat, scale)) + _mod_row(mods_ref, lat, shift)
            outs[o][...] = h.astype(BF16)

    row = pl.BlockSpec((TM, d), lambda i: (i, 0))
    ins, specs = [x, mods], [row, pl.BlockSpec(mods.shape, lambda i: (0, 0))]
    if has_y:
        ins.append(y); specs.append(row)
    if has_n:
        ins.append(g.reshape(1, d)); specs.append(pl.BlockSpec((1, d), lambda i: (0, 0)))
    out_shape, out_specs = [], []
    if has_y:
        out_shape.append(jax.ShapeDtypeStruct((t, d), F32)); out_specs.append(row)
    if has_n:
        out_shape.append(jax.ShapeDtypeStruct((t, d), BF16)); out_specs.append(row)
    res = _pcall(body, name=name, grid=(t // TM,), in_specs=specs, out_specs=out_specs,
                 out_shape=out_shape)(*ins)
    return res


def _acc_row(ref, r, val):
    ref[r:r + 1, :] += val


def _row_final(x, z, mods, target, *, gate, name):
    t, d = x.shape

    def body(x_ref, mods_ref, z_ref, t_ref, loss_ref, dx_ref, dz_ref, sums_ref):
        i = pl.program_id(0)
        lat = i > 0

        @pl.when(i == 0)
        def _():
            loss_ref[...] = jnp.zeros_like(loss_ref)
            sums_ref[...] = jnp.zeros_like(sums_ref)

        gt = _mod_row(mods_ref, lat, gate)
        zz = z_ref[...]
        yv = x_ref[...] + gt * zz
        keep = jnp.where(lat, 1.0, 0.0).astype(F32)
        diff = (yv - t_ref[...]) * keep
        part = jnp.sum(jnp.sum(diff * diff, axis=0, keepdims=True), axis=1, keepdims=True)
        loss_ref[...] += part * (0.5 / d)
        dy = diff * (1.0 / d)
        dx_ref[...] = dy
        dz_ref[...] = (gt * dy).astype(BF16)
        _acc_row(sums_ref, 6, jnp.sum(dy * zz, axis=0, keepdims=True))

    row = pl.BlockSpec((TM, d), lambda i: (i, 0))
    return _pcall(
        body, name=name, grid=(t // TM,),
        in_specs=[row, pl.BlockSpec(mods.shape, lambda i: (0, 0)), row,
                  pl.BlockSpec((TM, d), lambda i: (jnp.maximum(i - 1, 0), 0))],
        out_specs=[pl.BlockSpec((8, 128), lambda i: (0, 0)), row, row,
                   pl.BlockSpec((8, d), lambda i: (0, 0))],
        out_shape=[jax.ShapeDtypeStruct((8, 128), F32), jax.ShapeDtypeStruct((t, d), F32),
                   jax.ShapeDtypeStruct((t, d), BF16), jax.ShapeDtypeStruct((8, d), F32)],
    )(x, mods, z, target)


def _row_bwd(xn, dxo, dh, mods, g, *, shift, scale, y=None, gate=None, name):
    t, d = xn.shape
    has_y = y is not None

    def body(*refs):
        refs = list(refs)
        x_ref, dxo_ref, dh_ref, mods_ref, g_ref = refs[:5]
        pos = 5
        if has_y:
            y_ref = refs[pos]; pos += 1
        dx_ref = refs[pos]; pos += 1
        if has_y:
            dy_ref = refs[pos]; pos += 1
        sums_ref = refs[pos]
        i = pl.program_id(0)
        lat = i > 0

        @pl.when(i == 0)
        def _():
            sums_ref[...] = jnp.zeros_like(sums_ref)

        x1 = x_ref[...]
        gv = g_ref[...]
        rs = lax.rsqrt(jnp.mean(x1 * x1, axis=-1, keepdims=True) + EPS)
        xh = x1 * rs
        dhv = dh_ref[...]
        dn = dhv * (1.0 + _mod_row(mods_ref, lat, scale))
        dxh = dn * gv
        dx = dxo_ref[...] + rs * (dxh - xh * jnp.mean(dxh * xh, axis=-1, keepdims=True))
        dx_ref[...] = dx
        vals = [jnp.sum(dhv, axis=0, keepdims=True),
                jnp.sum(dhv * (xh * gv), axis=0, keepdims=True),
                None,
                jnp.sum(dn * xh, axis=0, keepdims=True)]
        if has_y:
            dy_ref[...] = (_mod_row(mods_ref, lat, gate) * dx).astype(BF16)
            vals[2] = jnp.sum(dx * y_ref[...], axis=0, keepdims=True)

        @pl.when(i == 0)
        def _():
            for r, v in enumerate(vals):
                if v is not None:
                    _acc_row(sums_ref, r, v)

        @pl.when(i > 0)
        def _():
            for r, v in enumerate(vals):
                if v is not None:
                    _acc_row(sums_ref, 4 + r, v)

    row = pl.BlockSpec((TM, d), lambda i: (i, 0))
    ins = [xn, dxo, dh, mods, g.reshape(1, d)]
    specs = [row, row, row, pl.BlockSpec(mods.shape, lambda i: (0, 0)), pl.BlockSpec((1, d), lambda i: (0, 0))]
    out_shape, out_specs = [jax.ShapeDtypeStruct((t, d), F32)], [row]
    if has_y:
        ins.append(y); specs.append(row)
        out_shape.append(jax.ShapeDtypeStruct((t, d), BF16)); out_specs.append(row)
    out_shape.append(jax.ShapeDtypeStruct((8, d), F32))
    out_specs.append(pl.BlockSpec((8, d), lambda i: (0, 0)))
    return _pcall(body, name=name, grid=(t // TM,), in_specs=specs, out_specs=out_specs,
                  out_shape=out_shape)(*ins)


FFN_BK = 1408


def _ffn_order(n2):
    nb = n2 // (2 * FFN_BK)
    return [h * nb + j for j in range(nb) for h in (0, 1)]


def _ffn_interleave(w):
    return jnp.concatenate([w[..., b * FFN_BK:(b + 1) * FFN_BK] for b in _ffn_order(w.shape[-1])], axis=-1)


def _ffn_deinterleave(w):
    order = _ffn_order(w.shape[-1])
    return jnp.concatenate([w[..., order.index(b) * FFN_BK:(order.index(b) + 1) * FFN_BK]
                            for b in range(len(order))], axis=-1)


def _big_tile(t):
    return 384 if t % 384 == 0 else TM


def _ffn_in(h, w, *, lead, name):
    t, d = h.shape
    n2 = w.shape[-1]
    bm, bk = _big_tile(t), FFN_BK

    def body(h_ref, w_ref, u_ref, a_ref):
        ub = _dot(h_ref[...], w_ref[...]).astype(BF16)
        u_ref[...] = ub
        uf = ub.astype(F32)
        gv, up = uf[:, 0:bk], uf[:, bk:2 * bk]
        a_ref[...] = (gv * _sigmoid(gv) * up).astype(BF16)

    return _pcall(
        body, name=name, grid=(t // bm, n2 // (2 * bk)),
        in_specs=[pl.BlockSpec((bm, d), lambda i, j: (i, 0)),
                  pl.BlockSpec((None, d, 2 * bk), lambda i, j: (lead, 0, j))],
        out_specs=[pl.BlockSpec((bm, 2 * bk), lambda i, j: (i, j)), pl.BlockSpec((bm, bk), lambda i, j: (i, j))],
        out_shape=[jax.ShapeDtypeStruct((t, n2), BF16), jax.ShapeDtypeStruct((t, n2 // 2), BF16)],
    )(h, w)


def _ffn_dx(dz, w_out, u, *, lead, name):
    t, d = dz.shape
    n2 = u.shape[1]
    bm, bk = _big_tile(t), FFN_BK

    def body(dz_ref, w_ref, u_ref, du_ref):
        da = _dot_nt(dz_ref[...], w_ref[...])
        uf = u_ref[...].astype(F32)
        gv, up = uf[:, 0:bk], uf[:, bk:2 * bk]
        s = _sigmoid(gv)
        du_ref[:, 0:bk] = (da * up * (s * (1.0 + gv * (1.0 - s)))).astype(BF16)
        du_ref[:, bk:2 * bk] = (da * gv * s).astype(BF16)

    ublk = pl.BlockSpec((bm, 2 * bk), lambda i, j: (i, j))
    return _pcall(
        body, name=name, grid=(t // bm, n2 // (2 * bk)),
        in_specs=[pl.BlockSpec((bm, d), lambda i, j: (i, 0)),
                  pl.BlockSpec((None, bk, d), lambda i, j: (lead, j, 0)), ublk],
        out_specs=ublk, out_shape=jax.ShapeDtypeStruct((t, n2), BF16),
    )(dz, w_out, u)


def _lane(shape):
    return _iota(shape, len(shape) - 1)


def _pair_norm(x, g):
    lo = _lane(x.shape) < 64
    x2 = x * x
    s_lo = jnp.sum(jnp.where(lo, x2, 0.0), axis=-1, keepdims=True)
    s_hi = jnp.sum(jnp.where(lo, 0.0, x2), axis=-1, keepdims=True)
    rs = lax.rsqrt(jnp.where(lo, s_lo, s_hi) * (1.0 / 64) + EPS)
    return x * rs, rs


def _pair_mean(v):
    lo = _lane(v.shape) < 64
    s_lo = jnp.sum(jnp.where(lo, v, 0.0), axis=-1, keepdims=True)
    s_hi = jnp.sum(jnp.where(lo, 0.0, v), axis=-1, keepdims=True)
    return jnp.where(lo, s_lo, s_hi) * (1.0 / 64)


def _rot64(x):
    r1 = pltpu.roll(x, 32, 1)
    r2 = pltpu.roll(x, 96, 1)
    even = ((_lane(x.shape) >> 5) & 1) == 0
    return jnp.where(even, -r2, r1)


def _rope64(x, cos, sin):
    return x * cos + _rot64(x) * sin


def _rope64_t(d, cos, sin):
    return d * cos - _rot64(d * sin)


def _kprep_fwd(p, gk, cos, sin, *, name):
    t = p.shape[0]

    def body(k_ref, g_ref, c_ref, s_ref, o_ref):
        xh, _ = _pair_norm(k_ref[...], None)
        o_ref[...] = _rope64(xh * g_ref[...], c_ref[...], s_ref[...])

    blk = pl.BlockSpec((TM, 128), lambda i: (i, 0))
    return _pcall(
        body, name=name, grid=(t // TM,),
        in_specs=[pl.BlockSpec((TM, 128), lambda i: (i, 4)), pl.BlockSpec((1, 128), lambda i: (0, 0)), blk, blk],
        out_specs=blk, out_shape=jax.ShapeDtypeStruct((t, 128), F32),
    )(p, gk, cos, sin)


def _kprep_bwd(p, gk, cos, sin, dkp, dv, *, name):
    t = p.shape[0]

    def body(k_ref, g_ref, c_ref, s_ref, dkp_ref, dv_ref, o_ref, dg_ref):
        @pl.when(pl.program_id(0) == 0)
        def _():
            dg_ref[...] = jnp.zeros_like(dg_ref)
        xh, rs = _pair_norm(k_ref[...], None)
        dn = _rope64_t(dkp_ref[...], c_ref[...], s_ref[...])
        _acc_row(dg_ref, 0, jnp.sum(dn * xh, axis=0, keepdims=True))
        dxh = dn * g_ref[...]
        o_ref[:, 0:128] = (rs * (dxh - xh * _pair_mean(dxh * xh))).astype(BF16)
        o_ref[:, 128:256] = dv_ref[...].astype(BF16)

    blk = pl.BlockSpec((TM, 128), lambda i: (i, 0))
    return _pcall(
        body, name=name, grid=(t // TM,),
        in_specs=[pl.BlockSpec((TM, 128), lambda i: (i, 4)), pl.BlockSpec((1, 128), lambda i: (0, 0)), blk, blk, blk, blk],
        out_specs=[pl.BlockSpec((TM, 256), lambda i: (i, 0)), pl.BlockSpec((8, 128), lambda i: (0, 0))],
        out_shape=[jax.ShapeDtypeStruct((t, 256), BF16), jax.ShapeDtypeStruct((8, 128), F32)],
    )(p, gk, cos, sin, dkp, dv)


def _attn_common(i, t, lc, kp_ref, v_ref):
    span = QB + 2 * WINDOW
    start = pl.multiple_of(jnp.clip((i - 1) * QB, lc, t - span), QB)
    kall = jnp.concatenate([kp_ref[0:lc, :], kp_ref[pl.ds(start, span), :]], axis=0)
    vall = jnp.concatenate([v_ref[0:lc, :], v_ref[pl.ds(start, span), :]], axis=0)
    nk = lc + span
    col = _iota((QB, nk), 1)
    krow = jnp.where(col < lc, col, start + col - lc)
    qrow = i * QB + _iota((QB, nk), 0)
    valid = (col < lc) | ((qrow >= lc) & (krow >= lc) & (jnp.abs(krow - qrow) <= WINDOW))
    lo = _lane(kall.shape) < 64
    kroll, vroll = pltpu.roll(kall, 64, 1), pltpu.roll(vall, 64, 1)
    zero = jnp.zeros_like(kall)
    kvar = [[_bf(jnp.where(lo, kall, zero)), _bf(jnp.where(lo, zero, kroll))],
            [_bf(jnp.where(lo, kroll, zero)), _bf(jnp.where(lo, zero, kall))]]
    vvar = [[_bf(jnp.where(lo, vall, zero)), _bf(jnp.where(lo, zero, vroll))],
            [_bf(jnp.where(lo, vroll, zero)), _bf(jnp.where(lo, zero, vall))]]
    return start, valid, kvar, vvar


def _softmax_sink(s, valid, snk):
    s = jnp.where(valid, s, NEG)
    m = jnp.maximum(jnp.max(s, axis=-1, keepdims=True), snk)
    e = jnp.exp(s - m)
    es = jnp.exp(snk - m)
    inv = 1.0 / (jnp.sum(e, axis=-1, keepdims=True) + es)
    return e * inv, es * inv


def _attn_fwd(p, kp, gq, sink, cos, sin, *, lc, name):
    t = p.shape[0]
    scale = 64 ** -0.5

    def body(q_ref, kp_ref, v_ref, g_ref, sink_ref, c_ref, s_ref, o_ref):
        i = pl.program_id(0)
        _, valid, kvar, vvar = _attn_common(i, t, lc, kp_ref, v_ref)
        cosv, sinv, gv = c_ref[...], s_ref[...], g_ref[...]
        for j in range(4):
            xh, _ = _pair_norm(q_ref[:, 128 * j:128 * j + 128], None)
            q2 = _bf(_rope64(xh * gv, cosv, sinv))
            acc = jnp.zeros((QB, 128), F32)
            for half in range(2):
                s = _dot_nt(q2, kvar[j // 2][half]) * scale
                pr, _ = _softmax_sink(s, valid, sink_ref[2 * j + half])
                acc = acc + _dot(pr, vvar[j // 2][half])
            o_ref[:, 128 * j:128 * j + 128] = acc.astype(BF16)

    qblk = pl.BlockSpec((QB, 128), lambda i: (i, 0))
    return _pcall(
        body, name=name, grid=(t // QB,),
        in_specs=[pl.BlockSpec((QB, 512), lambda i: (i, 0)),
                  pl.BlockSpec((t, 128), lambda i: (0, 0)),
                  pl.BlockSpec((t, 128), lambda i: (0, 5)),
                  pl.BlockSpec((1, 128), lambda i: (0, 0)),
                  pl.BlockSpec(memory_space=pltpu.SMEM), qblk, qblk],
        out_specs=pl.BlockSpec((QB, 512), lambda i: (i, 0)),
        out_shape=jax.ShapeDtypeStruct((t, 512), BF16),
    )(p, kp, p, gq, sink, cos, sin)


def _attn_bwd(p, kp, gq, sink, cos, sin, dmix, *, lc, name):
    t = p.shape[0]
    scale = 64 ** -0.5
    span = QB + 2 * WINDOW

    def body(q_ref, kp_ref, v_ref, g_ref, sink_ref, c_ref, s_ref, do_ref,
             dq_ref, dk_ref, dv_ref, dg_ref, dsink_ref):
        i = pl.program_id(0)

        @pl.when(i == 0)
        def _():
            dk_ref[...] = jnp.zeros_like(dk_ref)
            dv_ref[...] = jnp.zeros_like(dv_ref)
            dg_ref[...] = jnp.zeros_like(dg_ref)
            dsink_ref[...] = jnp.zeros_like(dsink_ref)

        start, valid, kvar, vvar = _attn_common(i, t, lc, kp_ref, v_ref)
        cosv, sinv, gv = c_ref[...], s_ref[...], g_ref[...]
        nk = lc + span
        lo = _lane((nk, 128)) < 64
        dk_all = jnp.zeros((nk, 128), F32)
        dv_all = jnp.zeros((nk, 128), F32)
        for j in range(4):
            kvh = j // 2
            xh, rs = _pair_norm(q_ref[:, 128 * j:128 * j + 128], None)
            q2 = _bf(_rope64(xh * gv, cosv, sinv))
            do2 = _bf(do_ref[:, 128 * j:128 * j + 128])
            dq2 = jnp.zeros((QB, 128), F32)
            for half in range(2):
                s = _dot_nt(q2, kvar[kvh][half]) * scale
                pr, ps = _softmax_sink(s, valid, sink_ref[2 * j + half])
                dp = _dot_nt(do2, vvar[kvh][half])
                delta = jnp.sum(pr * dp, axis=-1, keepdims=True)
                ds = pr * (dp - delta) * scale
                dsk = jnp.sum(jnp.sum(-ps * delta, axis=0, keepdims=True), axis=1, keepdims=True)
                _acc_row(dsink_ref, 2 * j + half, jnp.broadcast_to(dsk, (1, 128)))
                dq2 = dq2 + _dot(ds, kvar[kvh][half])
                gk_ = _dot_tn(ds, q2)
                gv_ = _dot_tn(pr, do2)
                if half == 0:
                    gk_, gv_ = jnp.where(lo, gk_, 0.0), jnp.where(lo, gv_, 0.0)
                else:
                    gk_, gv_ = jnp.where(lo, 0.0, gk_), jnp.where(lo, 0.0, gv_)
                if half != kvh:
                    gk_, gv_ = pltpu.roll(gk_, 64, 1), pltpu.roll(gv_, 64, 1)
                dk_all = dk_all + gk_
                dv_all = dv_all + gv_
            dn = _rope64_t(dq2, cosv, sinv)
            _acc_row(dg_ref, 0, jnp.sum(dn * xh, axis=0, keepdims=True))
            dxh = dn * gv
            dq_ref[:, 128 * j:128 * j + 128] = (rs * (dxh - xh * _pair_mean(dxh * xh))).astype(BF16)
        dk_ref[0:lc, :] += dk_all[0:lc]
        dv_ref[0:lc, :] += dv_all[0:lc]
        dk_ref[pl.ds(start, span), :] += dk_all[lc:nk]
        dv_ref[pl.ds(start, span), :] += dv_all[lc:nk]

    qblk = pl.BlockSpec((QB, 128), lambda i: (i, 0))
    full = pl.BlockSpec((t, 128), lambda i: (0, 0))
    small = pl.BlockSpec((8, 128), lambda i: (0, 0))
    return _pcall(
        body, name=name, grid=(t // QB,),
        in_specs=[pl.BlockSpec((QB, 512), lambda i: (i, 0)), full,
                  pl.BlockSpec((t, 128), lambda i: (0, 5)),
                  pl.BlockSpec((1, 128), lambda i: (0, 0)),
                  pl.BlockSpec(memory_space=pltpu.SMEM), qblk, qblk,
                  pl.BlockSpec((QB, 512), lambda i: (i, 0))],
        out_specs=[pl.BlockSpec((QB, 512), lambda i: (i, 0)), full, full, small, small],
        out_shape=[jax.ShapeDtypeStruct((t, 512), BF16), jax.ShapeDtypeStruct((t, 128), F32),
                   jax.ShapeDtypeStruct((t, 128), F32), jax.ShapeDtypeStruct((8, 128), F32),
                   jax.ShapeDtypeStruct((8, 128), F32)],
    )(p, kp, p, gq, sink, cos, sin, dmix)


def _tri(rev):
    r, c = _iota((CHUNK, CHUNK), 0), _iota((CHUNK, CHUNK), 1)
    return (c >= r) if rev else (c <= r)


def _blk_map(nb, rev, backward):
    if not rev:
        return (lambda n: nb - 1 - n) if backward else (lambda n: n)
    if backward:
        return lambda n: jnp.where(n < nb - 1, n + 1, 0)
    return lambda n: jnp.where(n == 0, 0, nb - n)


def _chunk_order(rev, backward, nc=TM // CHUNK):
    order = list(range(nc))
    return order[::-1] if (rev != backward) else order


def _hgrn_gates(qraw, fraw, lb):
    sq = _sigmoid(qraw)
    sf = _sigmoid(fraw)
    f = lb + (1.0 - lb) * sf
    return qraw * sq, 1.0 - f, jnp.log(f), sq, sf, f


HGRN_HP = 2


def _chunk_cumsum(x, rev):
    n = x.shape[0]
    pos = _iota(x.shape, 0) & (CHUNK - 1)
    s = 1
    while s < CHUNK:
        if rev:
            x = x + jnp.where(pos < CHUNK - s, pltpu.roll(x, n - s, 0), 0.0)
        else:
            x = x + jnp.where(pos >= s, pltpu.roll(x, s, 0), 0.0)
        s *= 2
    return x


def _block_terms(lf, rev):
    b = _chunk_cumsum(lf, rev)
    mid, last = (CHUNK // 2 - 1, 0) if rev else (CHUNK // 2, CHUNK - 1)

    def chunk_row(off):
        return jnp.concatenate([jnp.broadcast_to(b[c * CHUNK + off:c * CHUNK + off + 1, :], (CHUNK, b.shape[1]))
                                for c in range(TM // CHUNK)], axis=0)

    r, bl = chunk_row(mid), chunk_row(last)
    return _tri(rev), jnp.exp(b - r), jnp.exp(r - b), jnp.exp(b), jnp.exp(bl - b), jnp.exp(bl)


def _headnorm_apply(o, gv, gain):
    n = o * lax.rsqrt(jnp.mean(o * o, axis=-1, keepdims=True) + EPS)
    if gain is not None:
        n = n * gain
    return (n * (gv * _sigmoid(gv))).astype(BF16)


def _headnorm_grad(o, gv, dy, gain):
    rs = lax.rsqrt(jnp.mean(o * o, axis=-1, keepdims=True) + EPS)
    xh = o * rs
    n = xh * gain if gain is not None else xh
    sg = _sigmoid(gv)
    dn = dy * (gv * sg)
    dg = (dy * n * (sg * (1.0 + gv * (1.0 - sg)))).astype(BF16)
    dgain = jnp.sum(dn * xh, axis=0, keepdims=True)
    dxh = dn * gain if gain is not None else dn
    return rs * (dxh - xh * jnp.mean(dxh * xh, axis=-1, keepdims=True)), dg, dgain


def _hgrn_fwd(p, lb, *, rev, name, ofw=None, gain=None):
    t = p.shape[0]
    nb, nc = t // TM, TM // CHUNK
    bmap = _blk_map(nb, rev, False)
    fcol = 14 if rev else 10
    fused = ofw is not None

    def body(*refs):
        q_ref, f_ref, v_ref, lb_ref = refs[:4]
        if fused:
            ofw_ref, g_ref, gain_ref, o_ref, sh_ref, mix_ref, st = refs[4:]
        else:
            o_ref, sh_ref, st = refs[4:]

        @pl.when(pl.program_id(1) == 0)
        def _():
            st[...] = jnp.zeros_like(st)
        for hh in range(HGRN_HP):
            ln = slice(128 * hh, 128 * hh + 128)
            q, k, lf, _, _, _ = _hgrn_gates(q_ref[:, ln], f_ref[:, ln], lb_ref[:, ln])
            tri, eq, ek, ei, eki, eb = _block_terms(lf, rev)
            qe, ke, qi, ki, vb = _bf(q * eq), _bf(k * ek), _bf(q * ei), _bf(k * eki), _bf(v_ref[:, ln])
            intra = []
            for cc in range(nc):
                rows = slice(cc * CHUNK, (cc + 1) * CHUNK)
                a = jnp.where(tri, _dot_nt(qe[rows], ke[rows]), 0.0)
                intra.append(_dot(a, vb[rows]))
            s = st[hh]
            for cc in _chunk_order(rev, False):
                rows = slice(cc * CHUNK, (cc + 1) * CHUNK)
                sh_ref[hh, cc] = s
                o_ref[rows, ln] = intra[cc] + _dot_nt(qi[rows], s)
                s = s * eb[cc * CHUNK:cc * CHUNK + 1, :] + _dot_tn(vb[rows], ki[rows])
            st[hh] = s
            if fused:
                osum = o_ref[:, ln] + ofw_ref[:, ln]
                o_ref[:, ln] = osum
                mix_ref[:, ln] = _headnorm_apply(osum, g_ref[:, ln], gain_ref[...])

    hp, wd = HGRN_HP, 128 * HGRN_HP

    def col(c0):
        return pl.BlockSpec((TM, wd), lambda h, n: (bmap(n), c0 // hp + h))

    oblk = pl.BlockSpec((TM, wd), lambda h, n: (bmap(n), h))
    ins, specs = [p, p, p, lb], [col(6), col(fcol), col(18), pl.BlockSpec((1, wd), lambda h, n: (0, h))]
    out_specs = [oblk, pl.BlockSpec((hp, nc, 128, 128), lambda h, n: (h, bmap(n), 0, 0))]
    out_shape = [jax.ShapeDtypeStruct((t, 512), F32), jax.ShapeDtypeStruct((4, t // CHUNK, 128, 128), F32)]
    if fused:
        ins += [ofw, p, gain]
        specs += [oblk, col(22), pl.BlockSpec((1, 128), lambda h, n: (0, 0))]
        out_specs.append(oblk)
        out_shape.append(jax.ShapeDtypeStruct((t, 512), BF16))
    return _pcall(body, name=name, grid=(4 // hp, nb), in_specs=specs, out_specs=out_specs, out_shape=out_shape,
                  scratch_shapes=[pltpu.VMEM((hp, 128, 128), F32)])(*ins)


def _hgrn_bwd(p, lb, sh, do, prev, *, rev, name, head=None):
    t = p.shape[0]
    nb, nc = t // TM, TM // CHUNK
    bmap = _blk_map(nb, rev, True)
    fcol = 14 if rev else 10
    has_prev = prev is not None
    odt = BF16 if has_prev else F32
    fused = head is not None

    def body(*refs):
        refs = list(refs)
        q_ref, f_ref, v_ref, lb_ref, sh_ref = refs[:5]
        pos = 5
        if fused:
            osum_ref, g_ref, dmix_ref, gain_ref = refs[5:9]
            pos = 9
        else:
            do_ref = refs[5]
            pos = 6
        if has_prev:
            pq_ref, pv_ref = refs[pos], refs[pos + 1]
            pos += 2
        dq_ref, df_ref, dv_ref, dlb_ref = refs[pos:pos + 4]
        pos += 4
        if fused:
            do_out, dg_ref, dgain_ref = refs[pos:pos + 3]
            pos += 3
        dst = refs[pos]

        @pl.when(pl.program_id(1) == 0)
        def _():
            dst[...] = jnp.zeros_like(dst)
            dlb_ref[...] = jnp.zeros_like(dlb_ref)

        if fused:
            @pl.when((pl.program_id(0) == 0) & (pl.program_id(1) == 0))
            def _():
                dgain_ref[...] = jnp.zeros_like(dgain_ref)

        cat = functools.partial(jnp.concatenate, axis=0)
        for hh in range(HGRN_HP):
            ln = slice(128 * hh, 128 * hh + 128)
            lbv = lb_ref[:, ln]
            qraw, fraw = q_ref[:, ln], f_ref[:, ln]
            q, k, lf, sq, sf, f = _hgrn_gates(qraw, fraw, lbv)
            tri, eq, ek, ei, eki, eb = _block_terms(lf, rev)
            qe, ke, qi, ki = q * eq, k * ek, q * ei, k * eki
            if fused:
                dov, dg, dgain = _headnorm_grad(osum_ref[:, ln], g_ref[:, ln], dmix_ref[:, ln], gain_ref[...])
                do_out[:, ln] = dov
                dg_ref[:, ln] = dg
                _acc_row(dgain_ref, 0, dgain)
            else:
                dov = do_ref[:, ln]
            qeb, keb, qib, kib, vb, dob = _bf(qe), _bf(ke), _bf(qi), _bf(ki), _bf(v_ref[:, ln]), _bf(dov)
            dv, dqe, dke, dqi = [None] * nc, [None] * nc, [None] * nc, [None] * nc
            for cc in range(nc):
                rows = slice(cc * CHUNK, (cc + 1) * CHUNK)
                a = jnp.where(tri, _dot_nt(qeb[rows], keb[rows]), 0.0)
                da = jnp.where(tri, _dot_nt(dob[rows], vb[rows]), 0.0)
                dv[cc] = _dot_tn(a, dob[rows])
                dqe[cc], dke[cc] = _dot(da, keb[rows]), _dot_tn(da, qeb[rows])
                dqi[cc] = _dot(dob[rows], sh_ref[hh, cc])
            dki, dbl = [None] * nc, [None] * nc
            ds = dst[hh]
            for cc in _chunk_order(rev, True):
                rows = slice(cc * CHUNK, (cc + 1) * CHUNK)
                ebc = eb[cc * CHUNK:cc * CHUNK + 1, :]
                dv[cc] = dv[cc] + _dot_nt(kib[rows], ds)
                dki[cc] = _dot(vb[rows], ds)
                dbl[cc] = jnp.broadcast_to(jnp.sum(dki[cc] * ki[rows], axis=0, keepdims=True)
                                           + jnp.sum(ds * sh_ref[hh, cc], axis=0, keepdims=True) * ebc, (CHUNK, 128))
                ds = ds * ebc + _dot_tn(dob[rows], qib[rows])
            dst[hh] = ds
            dqe, dke, dqi, dki, dv, dbl = cat(dqe), cat(dke), cat(dqi), cat(dki), cat(dv), cat(dbl)
            dq = dqe * eq + dqi * ei
            dk = dke * ek + dki * eki
            last = 0 if rev else CHUNK - 1
            db = dqe * qe - dke * ke + dqi * qi - dki * ki
            db = db + jnp.where((_iota(db.shape, 0) & (CHUNK - 1)) == last, dbl, 0.0)
            dlf = _chunk_cumsum(db, not rev)
            dqr = dq * (sq * (1.0 + qraw * (1.0 - sq)))
            dfv = dlf / f - dk
            dfr = dfv * (1.0 - lbv) * (sf * (1.0 - sf))
            dlb_ref[:, ln] += jnp.sum(dfv * (1.0 - sf), axis=0, keepdims=True)
            if has_prev:
                dqr = dqr + pq_ref[:, ln]
                dv = dv + pv_ref[:, ln]
            dq_ref[:, ln] = dqr.astype(odt)
            df_ref[:, ln] = dfr.astype(odt)
            dv_ref[:, ln] = dv.astype(odt)

    hp, wd = HGRN_HP, 128 * HGRN_HP

    def col(c0):
        return pl.BlockSpec((TM, wd), lambda h, n: (bmap(n), c0 // hp + h))

    oblk = pl.BlockSpec((TM, wd), lambda h, n: (bmap(n), h))
    ins = [p, p, p, lb, sh]
    specs = [col(6), col(fcol), col(18), pl.BlockSpec((1, wd), lambda h, n: (0, h)),
             pl.BlockSpec((hp, nc, 128, 128), lambda h, n: (h, bmap(n), 0, 0))]
    if fused:
        osum, dmix, gain = head
        ins += [osum, p, dmix, gain]
        specs += [oblk, col(22), pl.BlockSpec((TM, wd), lambda h, n: (bmap(n), 4 // hp + h)),
                  pl.BlockSpec((1, 128), lambda h, n: (0, 0))]
    else:
        ins.append(do); specs.append(oblk)
    if has_prev:
        ins += list(prev); specs += [oblk, oblk]
    out_specs = [oblk, oblk, oblk, pl.BlockSpec((1, wd), lambda h, n: (0, h))]
    out_shape = [jax.ShapeDtypeStruct((t, 512), odt)] * 3 + [jax.ShapeDtypeStruct((1, 512), F32)]
    if fused:
        out_specs += [oblk, oblk, pl.BlockSpec((8, 128), lambda h, n: (0, 0))]
        out_shape += [jax.ShapeDtypeStruct((t, 512), F32), jax.ShapeDtypeStruct((t, 512), BF16),
                      jax.ShapeDtypeStruct((8, 128), F32)]
    return _pcall(body, name=name, grid=(4 // hp, nb), in_specs=specs, out_specs=out_specs, out_shape=out_shape,
                  scratch_shapes=[pltpu.VMEM((hp, 128, 128), F32)])(*ins)


def _rope256(x, cos, sin):
    x1, x2 = x[:, 0:128], x[:, 128:256]
    return jnp.concatenate([x1 * cos - x2 * sin, x2 * cos + x1 * sin], axis=-1)


def _rope256_t(d, cos, sin):
    d1, d2 = d[:, 0:128], d[:, 128:256]
    return jnp.concatenate([d1 * cos + d2 * sin, d2 * cos - d1 * sin], axis=-1)


RET_DK, RET_DV, RET_H = 256, 512, 4
RET_KSCALE = RET_DK ** -0.5
RCH = TM
RET_HP = 2


def _ret_terms(lg, rev):
    r, c = _iota((RCH, RCH), 0), _iota((RCH, RCH), 1)
    rel = ((c - r) if rev else (r - c)).astype(F32)
    dmat = jnp.where(rel >= 0, jnp.exp(lg[:, 0:1] * jnp.maximum(rel, 0.0)), 0.0)
    pos = _iota((RCH, 1), 0).astype(F32)
    cnt = (RCH - pos) if rev else (pos + 1.0)
    ei = jnp.exp(lg * cnt)
    eki = jnp.exp(lg * (RCH - cnt))
    eb = jnp.exp(lg * float(RCH))
    return dmat, ei, eki, eb


def _ret_fwd(p, lgt, cos, sin, *, rev, name, ofw=None):
    t = p.shape[0]
    nb, nc = t // TM, TM // RCH
    bmap = _blk_map(nb, rev, False)
    fused = ofw is not None

    def body(*refs):
        q_ref, k_ref, v_ref, lg_ref, c_ref, s_ref = refs[:6]
        if fused:
            ofw_ref, g_ref, o_ref, sh_ref, mix_ref, st = refs[6:]
        else:
            o_ref, sh_ref, st = refs[6:]

        @pl.when(pl.program_id(1) == 0)
        def _():
            st[...] = jnp.zeros_like(st)
        for hh in range(RET_HP):
            qc, vc = slice(RET_DK * hh, RET_DK * (hh + 1)), slice(RET_DV * hh, RET_DV * (hh + 1))
            dmat, ei, eki, eb = _ret_terms(lg_ref[hh], rev)
            for cc in _chunk_order(rev, False, nc):
                rows = slice(cc * RCH, (cc + 1) * RCH)
                cosv, sinv = c_ref[rows, :], s_ref[rows, :]
                q = _rope256(q_ref[rows, qc].astype(F32), cosv, sinv)
                k = _rope256(k_ref[rows, qc].astype(F32), cosv, sinv) * RET_KSCALE
                v = v_ref[rows, vc]
                s0 = st[hh]
                sh_ref[hh, cc] = s0.astype(BF16)
                a = _dot_nt(q, k) * dmat
                o = _dot(a, v) + _dot_nt(q * ei, s0)
                st[hh] = s0 * eb + _dot_tn(v, k * eki)
                if fused:
                    o = o + ofw_ref[rows, vc]
                    mix_ref[rows, vc] = _headnorm_apply(o, g_ref[rows, vc].astype(F32), None)
                o_ref[rows, vc] = o

    hp = RET_HP
    tab = pl.BlockSpec((TM, 128), lambda h, n: (bmap(n), 0))
    oblk = pl.BlockSpec((TM, hp * RET_DV), lambda h, n: (bmap(n), h))
    ins = [p, p, p, lgt, cos, sin]
    specs = [pl.BlockSpec((TM, hp * RET_DK), lambda h, n: (bmap(n), h)),
             pl.BlockSpec((TM, hp * RET_DK), lambda h, n: (bmap(n), RET_H // hp + h)),
             pl.BlockSpec((TM, hp * RET_DV), lambda h, n: (bmap(n), RET_H // hp + h)),
             pl.BlockSpec((hp, 1, RET_DK), lambda h, n: (h, 0, 0)), tab, tab]
    out_specs = [oblk, pl.BlockSpec((hp, nc, RET_DV, RET_DK), lambda h, n: (h, bmap(n), 0, 0))]
    out_shape = [jax.ShapeDtypeStruct((t, RET_H * RET_DV), F32),
                 jax.ShapeDtypeStruct((RET_H, t // RCH, RET_DV, RET_DK), BF16)]
    if fused:
        ins += [ofw, p]
        specs += [oblk, pl.BlockSpec((TM, hp * RET_DV), lambda h, n: (bmap(n), 2 * RET_H // hp + h))]
        out_specs.append(oblk)
        out_shape.append(jax.ShapeDtypeStruct((t, RET_H * RET_DV), BF16))
    return _pcall(body, name=name, grid=(RET_H // hp, nb), in_specs=specs, out_specs=out_specs, out_shape=out_shape,
                  scratch_shapes=[pltpu.VMEM((hp, RET_DV, RET_DK), F32)])(*ins)


def _ret_bwd(p, lgt, cos, sin, sh, do, prev, *, rev, name, head=None):
    t = p.shape[0]
    nb, nc = t // TM, TM // RCH
    bmap = _blk_map(nb, rev, True)
    has_prev = prev is not None
    odt = BF16 if has_prev else F32
    fused = head is not None

    def body(*refs):
        refs = list(refs)
        q_ref, k_ref, v_ref, lg_ref, c_ref, s_ref, sh_ref = refs[:7]
        if fused:
            osum_ref, g_ref, dmix_ref = refs[7:10]
            pos = 10
        else:
            do_ref = refs[7]
            pos = 8
        if has_prev:
            pq_ref, pk_ref, pv_ref = refs[pos:pos + 3]
            pos += 3
        dq_ref, dk_ref, dv_ref = refs[pos:pos + 3]
        pos += 3
        if fused:
            do_out, dg_ref = refs[pos:pos + 2]
            pos += 2
        dst = refs[pos]

        @pl.when(pl.program_id(1) == 0)
        def _():
            dst[...] = jnp.zeros_like(dst)

        for hh in range(RET_HP):
            qc, vc = slice(RET_DK * hh, RET_DK * (hh + 1)), slice(RET_DV * hh, RET_DV * (hh + 1))
            dmat, ei, eki, eb = _ret_terms(lg_ref[hh], rev)
            for cc in _chunk_order(rev, True, nc):
                rows = slice(cc * RCH, (cc + 1) * RCH)
                cosv, sinv = c_ref[rows, :], s_ref[rows, :]
                q = _rope256(q_ref[rows, qc].astype(F32), cosv, sinv)
                k = _rope256(k_ref[rows, qc].astype(F32), cosv, sinv) * RET_KSCALE
                v = v_ref[rows, vc]
                if fused:
                    dov, dg, _ = _headnorm_grad(osum_ref[rows, vc], g_ref[rows, vc].astype(F32), dmix_ref[rows, vc], None)
                    do_out[rows, vc] = dov
                    dg_ref[rows, vc] = dg
                else:
                    dov = do_ref[rows, vc]
                s0 = sh_ref[hh, cc]
                dsc = dst[hh]
                qi, ki = q * ei, k * eki
                a = _dot_nt(q, k) * dmat
                da = _dot_nt(dov, v) * dmat
                dv = _dot_tn(a, dov) + _dot_nt(ki, dsc)
                dqs = _dot(da, k) + _dot(dov, s0) * ei
                dks = _dot_tn(da, q) + _dot(v, dsc) * eki
                dst[hh] = dsc * eb + _dot_tn(dov, qi)
                dq = _rope256_t(dqs, cosv, sinv)
                dk = _rope256_t(dks * RET_KSCALE, cosv, sinv)
                if has_prev:
                    dq = dq + pq_ref[rows, qc]
                    dk = dk + pk_ref[rows, qc]
                    dv = dv + pv_ref[rows, vc]
                dq_ref[rows, qc] = dq.astype(odt)
                dk_ref[rows, qc] = dk.astype(odt)
                dv_ref[rows, vc] = dv.astype(odt)

    hp = RET_HP
    tab = pl.BlockSpec((TM, 128), lambda h, n: (bmap(n), 0))
    qblk = pl.BlockSpec((TM, hp * RET_DK), lambda h, n: (bmap(n), h))
    vblk = pl.BlockSpec((TM, hp * RET_DV), lambda h, n: (bmap(n), h))
    ins = [p, p, p, lgt, cos, sin, sh]
    specs = [qblk, pl.BlockSpec((TM, hp * RET_DK), lambda h, n: (bmap(n), RET_H // hp + h)),
             pl.BlockSpec((TM, hp * RET_DV), lambda h, n: (bmap(n), RET_H // hp + h)),
             pl.BlockSpec((hp, 1, RET_DK), lambda h, n: (h, 0, 0)), tab, tab,
             pl.BlockSpec((hp, nc, RET_DV, RET_DK), lambda h, n: (h, bmap(n), 0, 0))]
    if fused:
        osum, dmix = head
        ins += [osum, p, dmix]
        specs += [vblk, pl.BlockSpec((TM, hp * RET_DV), lambda h, n: (bmap(n), 2 * RET_H // hp + h)), vblk]
    else:
        ins.append(do); specs.append(vblk)
    if has_prev:
        ins += list(prev); specs += [qblk, qblk, vblk]
    out_specs = [qblk, qblk, vblk]
    out_shape = [jax.ShapeDtypeStruct((t, RET_H * RET_DK), odt), jax.ShapeDtypeStruct((t, RET_H * RET_DK), odt),
                 jax.ShapeDtypeStruct((t, RET_H * RET_DV), odt)]
    if fused:
        out_specs += [vblk, vblk]
        out_shape += [jax.ShapeDtypeStruct((t, RET_H * RET_DV), F32), jax.ShapeDtypeStruct((t, RET_H * RET_DV), BF16)]
    return _pcall(body, name=name, grid=(RET_H // hp, nb), in_specs=specs, out_specs=out_specs, out_shape=out_shape,
                  scratch_shapes=[pltpu.VMEM((hp, RET_DV, RET_DK), F32)])(*ins)


def _rope_tables(lc, l):
    tt = jnp.arange(l)
    row, colp = (tt // 64).astype(F32), (tt % 64).astype(F32)
    inv = 10000.0 ** (-jnp.arange(16, dtype=F32) / 16)
    ang = jnp.concatenate([row[:, None] * inv, colp[:, None] * inv], axis=-1)
    ang = jnp.concatenate([jnp.zeros((lc, 32), F32), ang], axis=0)
    acos, asin = jnp.tile(jnp.cos(ang), (1, 4)), jnp.tile(jnp.sin(ang), (1, 4))
    theta = 1.0 / (10000.0 ** jnp.linspace(0.0, 1.0, 128, dtype=F32))
    rang = jnp.arange(l, dtype=F32)[:, None] * theta
    rang = jnp.concatenate([jnp.zeros((lc, 128), F32), rang], axis=0)
    return acos, asin, jnp.cos(rang), jnp.sin(rang)


def _local_step(x0, target, mods, ng, w, small):
    t, d = x0.shape
    l = target.shape[0]
    lc = t - l
    acos, asin, rcos, rsin = _rope_tables(lc, l)
    lg_fw = jnp.log(1.0 - 2.0 ** (-5.0 - jnp.arange(RET_H, dtype=F32)))
    lgt_fw = jnp.broadcast_to(lg_fw[:, None, None], (RET_H, 1, RET_DK))
    lgt_bw = jnp.broadcast_to(lg_fw[::-1][:, None, None], (RET_H, 1, RET_DK))
    gq, gk, sink, gain, lb = small['gq'], small['gk'], small['sink'], small['gain'], small['lb']

    (h1,) = _row_fwd(x0, mods, g=ng[0], shift=0, scale=1, name='l0_norm1')
    p0 = _mm_nn(h1, w['even_in'], name='l0_in')
    kp = _kprep_fwd(p0, gk, acos, asin, name='l0_kprep')
    att = _attn_fwd(p0, kp, gq, sink, acos, asin, lc=lc, name='l0_attn')
    hof, hsf = _hgrn_fwd(p0, lb, rev=False, name='l0_hgrn_f')
    hos, hsb, bmix = _hgrn_fwd(p0, lb, rev=True, name='l0_hgrn_b', ofw=hof, gain=gain)
    mix0 = jnp.concatenate([att, bmix], axis=1)
    y0 = _mm_nn(mix0, w['even_out'], name='l0_out')
    x1, h2 = _row_fwd(x0, mods, y=y0, gate=2, g=ng[1], shift=3, scale=4, name='l0_norm2')
    u0, a0 = _ffn_in(h2, w['ffn_in'], lead=0, name='ffn_in')
    z0 = _mm_nn(a0, w['ffn_out'], lead=0, name='ffn_out')
    x2, h3 = _row_fwd(x1, mods, y=z0, gate=5, g=ng[2], shift=12, scale=13, name='l1_norm1')
    p1 = _mm_nn(h3, w['odd_in'], out_dtype=BF16, name='l1_in')
    rof, rsf = _ret_fwd(p1, lgt_fw, rcos, rsin, rev=False, name='l1_ret_f')
    ros, rsb, mix1 = _ret_fwd(p1, lgt_bw, rcos, rsin, rev=True, name='l1_ret_b', ofw=rof)
    y1 = _mm_nn(mix1, w['odd_out'], name='l1_out')
    x3, h4 = _row_fwd(x2, mods, y=y1, gate=14, g=ng[3], shift=15, scale=16, name='l1_norm2')
    u1, a1 = _ffn_in(h4, w['ffn_in'], lead=1, name='ffn_in')
    z1 = _mm_nn(a1, w['ffn_out'], lead=1, name='ffn_out')
    loss, dx4, dz1, s_fin = _row_final(x3, z1, mods, target, gate=17, name='loss')

    du1 = _ffn_dx(dz1, w['ffn_out'], u1, lead=1, name='ffn_out_dx')
    g_ffn_out1 = _mm_tn(a1, dz1, name='ffn_out_dw')
    dh4 = _mm_nt(du1, w['ffn_in'], lead=1, name='ffn_in_dx')
    g_ffn_in1 = _mm_tn(h4, du1, name='ffn_in_dw')
    dx3, dy1, s_l1n2 = _row_bwd(x3, dx4, dh4, mods, ng[3], shift=15, scale=16, y=y1, gate=14, name='l1_norm2_bwd')
    dmix1 = _mm_nt(dy1, w['odd_out'], name='l1_out_dx')
    g_odd_out = _mm_tn(mix1, dy1, name='l1_out_dw')
    rdq, rdk, rdv, rdo, rdg = _ret_bwd(p1, lgt_fw, rcos, rsin, rsf, None, None, rev=False, name='l1_ret_f_bwd',
                                       head=(ros, dmix1))
    rdq, rdk, rdv = _ret_bwd(p1, lgt_bw, rcos, rsin, rsb, rdo, (rdq, rdk, rdv), rev=True, name='l1_ret_b_bwd')
    dp1 = jnp.concatenate([rdq, rdk, rdv, rdg], axis=1)
    dh3 = _mm_nt(dp1, w['odd_in'], name='l1_in_dx')
    g_odd_in = _mm_tn(h3, dp1, name='l1_in_dw')
    dx2, dz0, s_l1n1 = _row_bwd(x2, dx3, dh3, mods, ng[2], shift=12, scale=13, y=z0, gate=5, name='l1_norm1_bwd')
    du0 = _ffn_dx(dz0, w['ffn_out'], u0, lead=0, name='ffn_out_dx')
    g_ffn_out0 = _mm_tn(a0, dz0, name='ffn_out_dw')
    dh2 = _mm_nt(du0, w['ffn_in'], lead=0, name='ffn_in_dx')
    g_ffn_in0 = _mm_tn(h2, du0, name='ffn_in_dw')
    dx1, dy0, s_l0n2 = _row_bwd(x1, dx2, dh2, mods, ng[1], shift=3, scale=4, y=y0, gate=2, name='l0_norm2_bwd')
    dmix0 = _mm_nt(dy0, w['even_out'], name='l0_out_dx')
    g_even_out = _mm_tn(mix0, dy0, name='l0_out_dw')
    hq, hff, hv, dlb_f, hdo, hdg, s_gain = _hgrn_bwd(p0, lb, hsf, None, None, rev=False, name='l0_hgrn_f_bwd',
                                                     head=(hos, dmix0, gain))
    hq, hfb, hv, dlb_b = _hgrn_bwd(p0, lb, hsb, hdo, (hq, hv), rev=True, name='l0_hgrn_b_bwd')
    adq, dkp, adv, s_gq, s_sink = _attn_bwd(p0, kp, gq, sink, acos, asin, dmix0, lc=lc, name='l0_attn_bwd')
    dkv, s_gk = _kprep_bwd(p0, gk, acos, asin, dkp, adv, name='l0_kprep_bwd')
    dp0 = jnp.concatenate([adq, dkv, hq, _bf(hff), hfb, hv, hdg], axis=1)
    dh1 = _mm_nt(dp0, w['even_in'], name='l0_in_dx')
    g_even_in = _mm_tn(h1, dp0, name='l0_in_dw')
    dx0, s_l0n1 = _row_bwd(x0, dx1, dh1, mods, ng[0], shift=0, scale=1, name='l0_norm1_bwd')

    grads = dict(ffn_in=[g_ffn_in0, g_ffn_in1], ffn_out=[g_ffn_out0, g_ffn_out1],
                 even_in=g_even_in, even_out=g_even_out, odd_in=g_odd_in, odd_out=g_odd_out)
    sums = dict(fin=s_fin, l1n2=s_l1n2, l1n1=s_l1n1, l0n2=s_l0n2, l0n1=s_l0n1, gain=s_gain, gq=s_gq, gk=s_gk,
                sink=s_sink, dlb=dlb_f + dlb_b)
    return loss, dx0, grads, sums


def _place():
    return lax.axis_index("x"), lax.axis_index("y"), lax.axis_index("c")


def _ag8(blk, *, name):
    r, c = blk.shape
    flips = [(dx, dy, dc) for dx in (0, 1) for dy in (0, 1) for dc in (0, 1) if (dx, dy, dc) != (0, 0, 0)]

    def body(x_ref, out_ref, send_sems, recv_sems, local_sem):
        ax, ay, ac = _place()
        me = 4 * ax + 2 * ay + ac
        mine = pltpu.make_async_copy(x_ref, out_ref.at[me], local_sem)
        mine.start()
        sent = []
        for k, (dx, dy, dc) in enumerate(flips):
            peer = (lax.rem(ax + dx, 2), lax.rem(ay + dy, 2), lax.rem(ac + dc, 2))
            cp = pltpu.make_async_remote_copy(src_ref=x_ref, dst_ref=out_ref.at[me], send_sem=send_sems.at[k],
                                              recv_sem=recv_sems.at[k], device_id=peer, device_id_type=MESH)
            cp.start()
            sent.append((cp, 4 * peer[0] + 2 * peer[1] + peer[2]))
        for k, (cp, pidx) in enumerate(sent):
            pltpu.make_async_remote_copy(src_ref=x_ref, dst_ref=out_ref.at[pidx], send_sem=send_sems.at[k],
                                         recv_sem=recv_sems.at[k], device_id=(ax, ay, ac),
                                         device_id_type=MESH).wait_recv()
        for cp, _ in sent:
            cp.wait_send()
        mine.wait()

    return _pcall(
        body, name=name,
        in_specs=[pl.BlockSpec(memory_space=pltpu.VMEM)],
        out_specs=pl.BlockSpec(memory_space=pltpu.VMEM),
        out_shape=jax.ShapeDtypeStruct((8, r, c), blk.dtype),
        scratch_shapes=[pltpu.SemaphoreType.DMA((7,)), pltpu.SemaphoreType.DMA((7,)), pltpu.SemaphoreType.DMA],
    )(blk)


def _chip_exchange(arrs, *, scatter, name):
    n = len(arrs)
    rel = [(1, 0), (0, 1), (1, 1)]

    def body(*refs):
        ins, outs = refs[:n], refs[n:2 * n]
        send_sems, recv_sems, local_sems = refs[2 * n:]
        ax, ay, ac = _place()
        s = 2 * ax + ay
        started, local = [], []
        for a in range(n):
            lcp = pltpu.make_async_copy(ins[a].at[s] if scatter else ins[a], outs[a].at[s], local_sems.at[a])
            lcp.start()
            local.append(lcp)
            for r, (dx, dy) in enumerate(rel):
                px, py = lax.rem(ax + dx, 2), lax.rem(ay + dy, 2)
                sp = 2 * px + py
                cp = pltpu.make_async_remote_copy(
                    src_ref=ins[a].at[sp] if scatter else ins[a], dst_ref=outs[a].at[s],
                    send_sem=send_sems.at[3 * a + r], recv_sem=recv_sems.at[3 * a + r],
                    device_id=(px, py, ac), device_id_type=MESH)
                cp.start()
                started.append((cp, a, r, sp))
        for cp, a, r, sp in started:
            pltpu.make_async_remote_copy(
                src_ref=ins[a].at[sp] if scatter else ins[a], dst_ref=outs[a].at[sp],
                send_sem=send_sems.at[3 * a + r], recv_sem=recv_sems.at[3 * a + r],
                device_id=(ax, ay, ac), device_id_type=MESH).wait_recv()
        for cp, _, _, _ in started:
            cp.wait_send()
        for lcp in local:
            lcp.wait()

    hbm = pl.BlockSpec(memory_space=pl.ANY)
    shapes = [jax.ShapeDtypeStruct(a.shape if scatter else (4,) + a.shape, a.dtype) for a in arrs]
    return _pcall(
        body, name=name, in_specs=[hbm] * n, out_specs=[hbm] * n, out_shape=shapes,
        scratch_shapes=[pltpu.SemaphoreType.DMA((3 * n,)), pltpu.SemaphoreType.DMA((3 * n,)),
                        pltpu.SemaphoreType.DMA((n,))],
    )(*arrs)


def _gather_weights(arrs, *, name):
    n = len(arrs)
    rel = [(1, 0), (0, 1), (1, 1)]

    def body(*refs):
        ins, outs = refs[:n], refs[n:2 * n]
        ici_send, ici_recv, d2d_send, d2d_recv = refs[2 * n:]
        ax, ay, ac = _place()
        s = 2 * ax + ay
        sib = (ax, ay, 1 - ac)
        peers = [(lax.rem(ax + dx, 2), lax.rem(ay + dy, 2)) for dx, dy in rel]

        def half(a, slot, c):
            return outs[a].at[slot, c]

        def ici(a, r, src, slot, to):
            return pltpu.make_async_remote_copy(src_ref=src, dst_ref=half(a, slot, ac), send_sem=ici_send.at[3 * a + r],
                                                recv_sem=ici_recv.at[3 * a + r], device_id=to, device_id_type=MESH)

        def d2d(a, r, slot, c):
            return pltpu.make_async_remote_copy(src_ref=half(a, slot, c), dst_ref=half(a, slot, c),
                                                send_sem=d2d_send.at[3 * a + r], recv_sem=d2d_recv.at[3 * a + r],
                                                device_id=sib, device_id_type=MESH)

        sent = []
        for a in range(n):
            for r, (px, py) in enumerate(peers):
                cp = ici(a, r, ins[a].at[ac], s, (px, py, ac))
                cp.start()
                sent.append(cp)
        for a in range(n):
            for r, (px, py) in enumerate(peers):
                sp = 2 * px + py
                ici(a, r, half(a, sp, ac), sp, (ax, ay, ac)).wait_recv()
                fw = d2d(a, r, sp, ac)
                fw.start()
                sent.append(fw)
        for a in range(n):
            for r, (px, py) in enumerate(peers):
                d2d(a, r, 2 * px + py, 1 - ac).wait_recv()
        for cp in sent:
            cp.wait_send()

    hbm = pl.BlockSpec(memory_space=pl.ANY)
    return _pcall(
        body, name=name, in_specs=[hbm] * n, out_specs=[hbm] * n,
        out_shape=[jax.ShapeDtypeStruct((4,) + a.shape, a.dtype) for a in arrs],
        scratch_shapes=[pltpu.SemaphoreType.DMA((3 * n,))] * 4,
    )(*arrs)


def _to_sibling(arrs, *, name):
    n = len(arrs)

    def body(*refs):
        ins, outs = refs[:n], refs[n:2 * n]
        send_sems, recv_sems = refs[2 * n:]
        ax, ay, ac = _place()
        cps = [pltpu.make_async_remote_copy(src_ref=ins[a], dst_ref=outs[a], send_sem=send_sems.at[a],
                                            recv_sem=recv_sems.at[a], device_id=(ax, ay, 1 - ac),
                                            device_id_type=MESH) for a in range(n)]
        for cp in cps:
            cp.start()
        for cp in cps:
            cp.wait_recv()
        for cp in cps:
            cp.wait_send()

    hbm = pl.BlockSpec(memory_space=pl.ANY)
    return _pcall(
        body, name=name, in_specs=[hbm] * n, out_specs=[hbm] * n,
        out_shape=[jax.ShapeDtypeStruct(a.shape, a.dtype) for a in arrs],
        scratch_shapes=[pltpu.SemaphoreType.DMA((n,))] * 2,
    )(*arrs)


def _mod_fwd(cond_raw, mw, mb, *, name):
    _, d, n = mw.shape

    def body(c_ref, w_ref, b_ref, o_ref):
        cv = c_ref[...]
        o_ref[...] = _dot(cv * _sigmoid(cv), w_ref[...]) + b_ref[...]

    return _pcall(
        body, name=name, grid=(2,),
        in_specs=[pl.BlockSpec((16, d), lambda l: (0, 0)), pl.BlockSpec((None, d, n), lambda l: (l, 0, 0)),
                  pl.BlockSpec((None, 1, n), lambda l: (l, 0, 0))],
        out_specs=pl.BlockSpec((None, 16, n), lambda l: (l, 0, 0)),
        out_shape=jax.ShapeDtypeStruct((2, 16, n), F32),
    )(cond_raw, mw, mb)


def _mod_bwd(cond_raw, dms, mw, *, name):
    _, d, n = mw.shape

    def body(c_ref, dm_ref, w_ref, gw_ref, dc_ref):
        @pl.when(pl.program_id(0) == 0)
        def _():
            dc_ref[...] = jnp.zeros_like(dc_ref)
        cv = c_ref[...]
        gw_ref[...] = _dot_tn(cv * _sigmoid(cv), dm_ref[...])
        dc_ref[...] += _dot_nt(dm_ref[...], w_ref[...])

    return _pcall(
        body, name=name, grid=(2,),
        in_specs=[pl.BlockSpec((16, d), lambda l: (0, 0)), pl.BlockSpec((None, 16, n), lambda l: (l, 0, 0)),
                  pl.BlockSpec((None, d, n), lambda l: (l, 0, 0))],
        out_specs=[pl.BlockSpec((None, d, n), lambda l: (l, 0, 0)), pl.BlockSpec((16, d), lambda l: (0, 0))],
        out_shape=[jax.ShapeDtypeStruct((2, d, n), F32), jax.ShapeDtypeStruct((16, d), F32)],
    )(cond_raw, dms, mw)


def _lb_fwd(hgrn_lb, *, name):
    def body(a_ref, o_ref):
        a0, a1 = a_ref[0:1, :], a_ref[1:2, :]
        m = jnp.maximum(a0, a1)
        e0, e1 = jnp.exp(a0 - m), jnp.exp(a1 - m)
        o_ref[...] = e0 / (e0 + e1)

    return _pcall(body, name=name, out_shape=jax.ShapeDtypeStruct((1, hgrn_lb.shape[1]), F32))(hgrn_lb)


PACK_ROWS = 40


def _small_finalize(gath, lb_pad, *, name):
    d = gath.shape[2]

    def body(g_ref, lb_ref, small_ref, glb_ref, gmb_ref, dm_ref):
        tot = g_ref[0]
        for e in range(1, 8):
            tot = tot + g_ref[e]
        for k in range(4):
            small_ref[k:k + 1, :] = tot[24 + 2 * k:25 + 2 * k, :] + tot[25 + 2 * k:26 + 2 * k, :]
        for k, r in ((4, 32), (5, 33)):
            v = tot[r:r + 1, :]
            small_ref[k:k + 1, :] = v + pltpu.roll(v, d - 64, 1)
        small_ref[6:7, :] = tot[34:35, :]
        small_ref[7:8, :] = tot[36:37, :]
        lbv = lb_ref[...]
        g0 = (tot[35:36, :] + tot[37:38, :]) * lbv * (1.0 - lbv)
        glb_ref[...] = jnp.zeros_like(glb_ref)
        glb_ref[0:1, :] = g0
        glb_ref[1:2, :] = -g0
        dm_ref[...] = jnp.zeros_like(dm_ref)
        for l in range(2):
            for part in range(6):
                rc, rl = l * 12 + part, l * 12 + 6 + part
                gmb_ref[l * 6 + part:l * 6 + part + 1, :] = tot[rc:rc + 1, :] + tot[rl:rl + 1, :]
                for e in range(8):
                    dm_ref[l, part, e:e + 1, :] = g_ref[e, rl:rl + 1, :]
                dm_ref[l, part, 8:9, :] = tot[rc:rc + 1, :]

    return _pcall(
        body, name=name,
        out_shape=[jax.ShapeDtypeStruct((8, d), F32), jax.ShapeDtypeStruct((8, d), F32),
                   jax.ShapeDtypeStruct((12, d), F32), jax.ShapeDtypeStruct((2, 6, 16, d), F32)],
    )(gath, lb_pad)


def _cctx_grad(gath, c_ctx2, *, name):
    def body(g_ref, c_ref, o_ref):
        tot = ((g_ref[0, 0:1, :] + g_ref[2, 0:1, :]) + g_ref[4, 0:1, :]) + g_ref[6, 0:1, :]
        cv = c_ref[...]
        s = _sigmoid(cv)
        o_ref[...] = tot * (s * (1.0 + cv * (1.0 - s)))

    return _pcall(body, name=name, out_shape=jax.ShapeDtypeStruct(c_ctx2.shape, F32))(gath, c_ctx2)


def _row_block(r, c, limit=256 * 1024):
    best = None
    for br in range(16, r + 1, 16):
        if r % br == 0 and br * c <= limit:
            best = br
    return best if best is not None else r


def _sum4(parts, *, name):
    _, r, c = parts.shape
    br = _row_block(r, c)

    def body(p_ref, o_ref):
        p = [p_ref[k].astype(F32) for k in range(4)]
        o_ref[...] = ((p[0] + p[1]) + p[2]) + p[3]

    return _pcall(body, name=name, grid=(r // br,),
                  in_specs=[pl.BlockSpec((4, br, c), lambda i: (0, i, 0))],
                  out_specs=pl.BlockSpec((br, c), lambda i: (i, 0)),
                  out_shape=jax.ShapeDtypeStruct((r, c), F32))(parts)


def _add2(a, b, *, name):
    r, c = a.shape
    br = _row_block(r, c)

    def body(a_ref, b_ref, o_ref):
        o_ref[...] = (a_ref[...].astype(F32) + b_ref[...].astype(F32)).astype(BF16)

    blk = pl.BlockSpec((br, c), lambda i: (i, 0))
    return _pcall(body, name=name, grid=(r // br,), in_specs=[blk, blk], out_specs=blk,
                  out_shape=jax.ShapeDtypeStruct((r, c), BF16))(a, b)


def _adam(w, gs, m, v, *, name):
    r, c = w.shape
    br = _row_block(r, c)
    ng = len(gs)
    c1 = 1.0 - ADAM_B1 ** ADAM_STEP
    c2 = 1.0 - ADAM_B2 ** ADAM_STEP

    def body(*refs):
        w_ref, m_ref, v_ref = refs[0], refs[1 + ng], refs[2 + ng]
        outs = refs[3 + ng:]
        g = refs[1][...]
        for k in range(1, ng):
            g = g + refs[1 + k][...]
        mn = ADAM_B1 * m_ref[...] + (1.0 - ADAM_B1) * g
        vn = ADAM_B2 * v_ref[...] + (1.0 - ADAM_B2) * (g * g)
        if ng > 1:
            outs[0][...] = g
        d_out, m_out, v_out = outs[-3:]
        m_out[...] = mn
        v_out[...] = vn
        d_out[...] = -ADAM_LR * ((mn / c1) / (jnp.sqrt(vn / c2) + ADAM_EPS) + ADAM_WD * w_ref[...])

    blk = pl.BlockSpec((br, c), lambda i: (i, 0))
    nout = 4 if ng > 1 else 3
    res = _pcall(body, name=name, grid=(r // br,), in_specs=[blk] * (3 + ng), out_specs=[blk] * nout,
                 out_shape=[jax.ShapeDtypeStruct((r, c), F32)] * nout)(w, *gs, m, v)
    return list(res) if ng > 1 else [gs[0]] + list(res)


def _grad_halves(name, g, ac):
    def chips(gl, order):
        n = gl.shape[1] // 4
        return jnp.stack([gl[:, b * n:(b + 1) * n] for b in order])

    if name == 'ffn_in':
        assert g[0].shape[1] == 4 * FFN_BK
        per = [chips(gl, _ffn_order(gl.shape[1])) for gl in g]
    elif name == 'ffn_out':
        per = [gl.reshape(4, gl.shape[0] // 4, gl.shape[1]) for gl in g]
    elif name in ('even_in', 'odd_in'):
        v = chips(g, range(4))
        per = [v[:, :g.shape[0] // 2], v[:, g.shape[0] // 2:]]
    else:
        k4, n = g.shape
        v = g.reshape(4, 2, k4 // 8, n).transpose(1, 0, 2, 3)
        per = [v[0], v[1]]
    first = ac == 0
    return _bf(jnp.where(first, per[0], per[1])), _bf(jnp.where(first, per[1], per[0]))


def _from_shards(name, g):
    _, r, n = g.shape
    if name == 'ffn_in':
        assert n == FFN_BK
        v = g.reshape(4, 2, r // 2, n)
        return jnp.concatenate([v[b] for b in _ffn_order(4 * n)], axis=-1)
    if name == 'ffn_out':
        return g.reshape(4, 2, r // 2, n).transpose(1, 0, 2, 3).reshape(2, 2 * r, n)
    if name in ('even_in', 'odd_in'):
        return jnp.concatenate([g[b] for b in range(4)], axis=-1)
    return g.reshape(4 * r, n)


def kernel(x, c, ctx, c_ctx, mod_w, mod_b, norm_g, ffn_w_in, ffn_w_out, even_w_in, even_w_out, attn_qk_norm_g, attn_sink, hgrn_out_norm_g, hgrn_lb, odd_w_in, odd_w_out, loss_target, m_c_ctx, m_mod_w, m_mod_b, m_norm_g, m_ffn_w_in, m_ffn_w_out, m_even_w_in, m_even_w_out, m_attn_qk_norm_g, m_attn_sink, m_hgrn_out_norm_g, m_hgrn_lb, m_odd_w_in, m_odd_w_out, v_c_ctx, v_mod_w, v_mod_b, v_norm_g, v_ffn_w_in, v_ffn_w_out, v_even_w_in, v_even_w_out, v_attn_qk_norm_g, v_attn_sink, v_hgrn_out_norm_g, v_hgrn_lb, v_odd_w_in, v_odd_w_out):
    d = x.shape[-1]
    lc = ctx.shape[1]
    assert lc == TM and d == 1024
    ax, ay, ac = _place()
    s = 2 * ax + ay
    me = 4 * ax + 2 * ay + ac
    nmod = mod_w.shape[2]

    def pad8(v):
        return jnp.pad(v, ((0, 8 - v.shape[0]), (0, 0)))

    pack = jnp.concatenate([pad8(c), pad8(norm_g.reshape(1, d))], axis=0)
    g1 = _ag8(pack, name='gather_cond')
    c_all = g1[:, 0, :]
    ng = g1[0::2, 8, :].reshape(4, 2, 2, d // 4).transpose(1, 2, 0, 3).reshape(4, d)

    names = ['ffn_in', 'ffn_out', 'even_in', 'even_out', 'odd_in', 'odd_out']
    shards = [_bf(v.reshape(-1, v.shape[-1])) for v in (ffn_w_in, ffn_w_out, even_w_in, even_w_out, odd_w_in, odd_w_out)]
    gathered = _gather_weights([a.reshape(2, a.shape[0] // 2, a.shape[1]) for a in shards], name='gather_weights')
    slot = lax.broadcasted_iota(jnp.int32, (4, 1, 1), 0)
    w = {nm: _from_shards(nm, jnp.where(slot == s, a[None], g.reshape((4,) + a.shape)))
         for nm, a, g in zip(names, shards, gathered)}

    cond_raw = jnp.concatenate([c_all, pad8(c_ctx.reshape(1, d))], axis=0)
    mb_sh = lax.dynamic_slice_in_dim(mod_b, s * nmod, nmod, axis=1).reshape(2, 1, nmod)
    mpart = _mod_fwd(cond_raw, mod_w, mb_sh, name='mod_fwd')
    g3 = _ag8(mpart.reshape(32, nmod), name='gather_mods')
    mods_full = g3[0::2].reshape(4, 2, 16, nmod).transpose(1, 2, 0, 3).reshape(2, 16, 4 * nmod)
    m_lat = lax.dynamic_index_in_dim(mods_full, me, axis=1, keepdims=False)
    mods = jnp.stack([mods_full[:, 8], m_lat], axis=1).reshape(24, d)

    lb = _lb_fwd(hgrn_lb, name='hgrn_lower_bound')
    small = dict(gq=jnp.tile(attn_qk_norm_g[0, 0], 2).reshape(1, 128), gk=jnp.tile(attn_qk_norm_g[0, 1], 2).reshape(1, 128),
                 sink=attn_sink[0], gain=hgrn_out_norm_g, lb=lb)
    x0 = jnp.concatenate([ctx[0], x[0]], axis=0)
    loss_t, dx0, grads, sums = _local_step(x0, loss_target[0], mods, ng, w, small)
    loss = lax.psum(loss_t[0, 0], ("x", "y", "c"))
    grad_x = dx0[lc:][None]

    def pad(v):
        return jnp.pad(v, ((0, 0), (0, d - v.shape[1])))

    sm = sums
    dm_rows = [
        [sm['l0n1'][0], sm['l0n1'][1], sm['l0n2'][2], sm['l0n2'][0], sm['l0n2'][1], sm['l1n1'][2]],
        [sm['l0n1'][4], sm['l0n1'][5], sm['l0n2'][6], sm['l0n2'][4], sm['l0n2'][5], sm['l1n1'][6]],
        [sm['l1n1'][0], sm['l1n1'][1], sm['l1n2'][2], sm['l1n2'][0], sm['l1n2'][1], sm['fin'][2]],
        [sm['l1n1'][4], sm['l1n1'][5], sm['l1n2'][6], sm['l1n2'][4], sm['l1n2'][5], sm['fin'][6]],
    ]
    rows = [r for grp in dm_rows for r in grp]
    for key in ('l0n1', 'l0n2', 'l1n1', 'l1n2'):
        rows += [sm[key][3], sm[key][7]]
    rows = jnp.stack(rows)
    extra = jnp.concatenate([pad(sm['gq'][0:1]), pad(sm['gk'][0:1]), pad(sm['gain'][0:1]), pad(sm['dlb']),
                             pad(sm['sink'][:, 0].reshape(1, 8)), jnp.zeros((3, d), F32)], axis=0)
    g4 = _ag8(jnp.concatenate([rows, extra], axis=0), name='gather_row_sums')
    small_g, glb, gmb, dmat = _small_finalize(g4, pad(lb), name='small_grads')
    dms = lax.dynamic_slice_in_dim(dmat.transpose(0, 2, 1, 3).reshape(2, 16, 6 * d), s * nmod, nmod, axis=2)
    g_mod_w, dcond = _mod_bwd(cond_raw, dms, mod_w, name='mod_bwd')
    g5 = _ag8(dcond[8:16], name='gather_dcond')
    g_c_ctx = _cctx_grad(g5, c_ctx.reshape(8, d // 8).reshape(1, d), name='c_ctx_grad')

    halves = [_grad_halves(nm, grads[nm], ac) for nm in names]
    mine = [h[0] for h in halves]
    theirs = _to_sibling([h[1] for h in halves], name='swap_core_halves')
    pair = [_add2(a.reshape(-1, a.shape[-1]), b.reshape(-1, b.shape[-1]), name='add_cores').reshape(a.shape)
            for a, b in zip(mine, theirs)]
    parts = _chip_exchange(pair, scatter=True, name='scatter_grads')
    half_sums = [_sum4(p, name='sum_chips') for p in parts]
    other = _to_sibling(half_sums, name='gather_core_halves')
    full = [jnp.concatenate([jnp.where(ac == 0, f, o), jnp.where(ac == 0, o, f)], axis=0)
            for f, o in zip(half_sums, other)]

    def upd(wv, gs, mv, vv, name):
        shp = wv.shape
        c2 = shp[-1]
        out = _adam(wv.reshape(-1, c2), [g.reshape(-1, c2) for g in gs], mv.reshape(-1, c2), vv.reshape(-1, c2), name=name)
        return [o.reshape(shp) for o in out]

    res = {}
    res['c_ctx'] = upd(c_ctx.reshape(8, d // 8), [g_c_ctx.reshape(8, d // 8)], m_c_ctx.reshape(8, d // 8), v_c_ctx.reshape(8, d // 8), 'adam_c_ctx')
    res['c_ctx'] = [o.reshape(d) for o in res['c_ctx']]
    res['mod_w'] = upd(mod_w, [g_mod_w], m_mod_w, v_mod_w, 'adam_mod_w')
    res['mod_b'] = upd(mod_b, [gmb.reshape(2, 6 * d)], m_mod_b, v_mod_b, 'adam_mod_b')
    g_ng = lax.dynamic_slice_in_dim(small_g[0:4].reshape(2, 2, d), s * (d // 4), d // 4, axis=2)
    res['norm_g'] = upd(norm_g, [g_ng], m_norm_g, v_norm_g, 'adam_norm_g')
    big = {nm: [g] for nm, g in zip(names, full)}
    res['ffn_w_in'] = upd(ffn_w_in, big['ffn_in'], m_ffn_w_in, v_ffn_w_in, 'adam_ffn_in')
    res['ffn_w_out'] = upd(ffn_w_out, big['ffn_out'], m_ffn_w_out, v_ffn_w_out, 'adam_ffn_out')
    res['even_w_in'] = upd(even_w_in, big['even_in'], m_even_w_in, v_even_w_in, 'adam_even_in')
    res['even_w_out'] = upd(even_w_out, big['even_out'], m_even_w_out, v_even_w_out, 'adam_even_out')
    g_qk = jnp.stack([small_g[4, 0:64], small_g[5, 0:64]]).reshape(1, 2, 64)
    res['attn_qk_norm_g'] = upd(attn_qk_norm_g, [g_qk], m_attn_qk_norm_g, v_attn_qk_norm_g, 'adam_qk_gain')
    res['attn_sink'] = upd(attn_sink, [small_g[7, 0:8].reshape(1, 8)], m_attn_sink, v_attn_sink, 'adam_sink')
    res['hgrn_out_norm_g'] = upd(hgrn_out_norm_g, [small_g[6, 0:128].reshape(1, 128)], m_hgrn_out_norm_g, v_hgrn_out_norm_g, 'adam_head_gain')
    res['hgrn_lb'] = upd(hgrn_lb, [glb[0:2, 0:hgrn_lb.shape[1]]], m_hgrn_lb, v_hgrn_lb, 'adam_hgrn_lb')
    res['odd_w_in'] = upd(odd_w_in, big['odd_in'], m_odd_w_in, v_odd_w_in, 'adam_odd_in')
    res['odd_w_out'] = upd(odd_w_out, big['odd_out'], m_odd_w_out, v_odd_w_out, 'adam_odd_out')

    order = ['c_ctx', 'mod_w', 'mod_b', 'norm_g', 'ffn_w_in', 'ffn_w_out', 'even_w_in', 'even_w_out',
             'attn_qk_norm_g', 'attn_sink', 'hgrn_out_norm_g', 'hgrn_lb', 'odd_w_in', 'odd_w_out']
    outs = [loss, grad_x]
    for k in range(4):
        outs += [res[nm][k] for nm in order]
    return tuple(outs)
```

```python
import functools
import math

import numpy as np
import jax
import jax.numpy as jnp
from jax import lax
from jax.experimental import pallas as pl
from jax.experimental.pallas import tpu as pltpu

F32 = jnp.float32
BF16 = jnp.bfloat16
EPS = 1e-6
TM = 256
CHUNK = 64
QB = 128
WINDOW = 128
NEG = -1e30
MESH = pl.DeviceIdType.MESH

ADAM_LR, ADAM_B1, ADAM_B2, ADAM_EPS, ADAM_WD, ADAM_STEP = 0.001, 0.9, 0.999, 1e-08, 0.01, 10


def _pcall(body, **kw):
    return pl.pallas_call(body, **kw)


def _pick(n, cap):
    best = None
    for m in range(128, min(n, cap) + 1, 128):
        if n % m == 0:
            best = m
    assert best is not None, (n, cap)
    return best


def _bf(x):
    return x.astype(BF16)


def _dot(a, b):
    return jnp.dot(_bf(a), _bf(b), preferred_element_type=F32)


def _dot_nt(a, b):
    return lax.dot_general(_bf(a), _bf(b), (((1,), (1,)), ((), ())), preferred_element_type=F32)


def _dot_tn(a, b):
    return lax.dot_general(_bf(a), _bf(b), (((0,), (0,)), ((), ())), preferred_element_type=F32)


def _dot_exact(a, b):
    return jnp.dot(a, b, preferred_element_type=F32, precision=lax.Precision.HIGHEST)


def _sigmoid(x):
    return 1.0 / (1.0 + jnp.exp(-x))


def _iota(shape, dim):
    return lax.broadcasted_iota(jnp.int32, shape, dim)


def _mm_nn(a, b, *, lead=None, out_dtype=F32, name):
    m, k = a.shape
    n = b.shape[-1]
    bm = 768 if m % 768 == 0 else TM
    bn = _pick(n, 1024)

    def body(a_ref, b_ref, o_ref):
        o_ref[...] = _dot(a_ref[...], b_ref[...]).astype(o_ref.dtype)

    if lead is None:
        b_spec = pl.BlockSpec((k, bn), lambda i, j: (0, j))
    else:
        b_spec = pl.BlockSpec((None, k, bn), lambda i, j: (lead, 0, j))
    return _pcall(
        body, name=name, grid=(m // bm, n // bn),
        in_specs=[pl.BlockSpec((bm, k), lambda i, j: (i, 0)), b_spec],
        out_specs=pl.BlockSpec((bm, bn), lambda i, j: (i, j)),
        out_shape=jax.ShapeDtypeStruct((m, n), out_dtype),
    )(a, b)


def _mm_nt(a, b, *, lead=None, name):
    m, n = a.shape
    k = b.shape[-2]
    bm = 768 if m % 768 == 0 else TM
    bk = _pick(k, 512)

    def body(a_ref, b_ref, o_ref):
        o_ref[...] = _dot_nt(a_ref[...], b_ref[...])

    if lead is None:
        b_spec = pl.BlockSpec((bk, n), lambda i, j: (j, 0))
    else:
        b_spec = pl.BlockSpec((None, bk, n), lambda i, j: (lead, j, 0))
    return _pcall(
        body, name=name, grid=(m // bm, k // bk),
        in_specs=[pl.BlockSpec((bm, n), lambda i, j: (i, 0)), b_spec],
        out_specs=pl.BlockSpec((bm, bk), lambda i, j: (i, j)),
        out_shape=jax.ShapeDtypeStruct((m, k), F32),
    )(a, b)


def _mm_tn(a, b, *, name):
    t, k = a.shape
    n = b.shape[1]
    bt = 768 if t % 768 == 0 else TM
    bk = _pick(k, 1536)
    bn = _pick(n, 1024) if n % 1024 == 0 or n < 1664 else _pick(n, 1664)

    def body(a_ref, b_ref, o_ref):
        @pl.when(pl.program_id(2) == 0)
        def _():
            o_ref[...] = jnp.zeros_like(o_ref)
        o_ref[...] += _dot_tn(a_ref[...], b_ref[...])

    return _pcall(
        body, name=name, grid=(k // bk, n // bn, t // bt),
        in_specs=[pl.BlockSpec((bt, bk), lambda i, j, s: (s, i)),
                  pl.BlockSpec((bt, bn), lambda i, j, s: (s, j))],
        out_specs=pl.BlockSpec((bk, bn), lambda i, j, s: (i, j)),
        out_shape=jax.ShapeDtypeStruct((k, n), F32),
    )(a, b)


def _mod_row(mods_ref, lat, idx):
    return jnp.where(lat, mods_ref[idx + 6:idx + 7, :], mods_ref[idx:idx + 1, :])


def _row_fwd(x, mods, *, y=None, gate=None, g=None, shift=None, scale=None, name):
    t, d = x.shape
    has_y, has_n = y is not None, g is not None

    def body(*refs):
        refs = list(refs)
        x_ref, mods_ref = refs[0], refs[1]
        pos = 2
        if has_y:
            y_ref = refs[pos]; pos += 1
        if has_n:
            g_ref = refs[pos]; pos += 1
        outs = refs[pos:]
        lat = pl.program_id(0) > 0
        x1 = x_ref[...]
        o = 0
        if has_y:
            x1 = x1 + _mod_row(mods_ref, lat, gate) * y_ref[...]
            outs[o][...] = x1; o += 1
        if has_n:
            rs = lax.rsqrt(jnp.mean(x1 * x1, axis=-1, keepdims=True) + EPS)
            hn = x1 * rs * g_ref[...]
            h = hn * (1.0 + _mod_row(mods_ref, lat, scale)) + _mod_row(mods_ref, lat, shift)
            outs[o][...] = h.astype(BF16)

    row = pl.BlockSpec((TM, d), lambda i: (i, 0))
    ins, specs = [x, mods], [row, pl.BlockSpec(mods.shape, lambda i: (0, 0))]
    if has_y:
        ins.append(y); specs.append(row)
    if has_n:
        ins.append(g.reshape(1, d)); specs.append(pl.BlockSpec((1, d), lambda i: (0, 0)))
    out_shape, out_specs = [], []
    if has_y:
        out_shape.append(jax.ShapeDtypeStruct((t, d), F32)); out_specs.append(row)
    if has_n:
        out_shape.append(jax.ShapeDtypeStruct((t, d), BF16)); out_specs.append(row)
    res = _pcall(body, name=name, grid=(t // TM,), in_specs=specs, out_specs=out_specs,
                 out_shape=out_shape)(*ins)
    return res


def _acc_row(ref, r, val):
    ref[r:r + 1, :] += val


def _row_final(x, z, mods, target, *, gate, name):
    t, d = x.shape

    def body(x_ref, mods_ref, z_ref, t_ref, loss_ref, dx_ref, dz_ref, sums_ref):
        i = pl.program_id(0)
        lat = i > 0

        @pl.when(i == 0)
        def _():
            loss_ref[...] = jnp.zeros_like(loss_ref)
            sums_ref[...] = jnp.zeros_like(sums_ref)

        gt = _mod_row(mods_ref, lat, gate)
        zz = z_ref[...]
        yv = x_ref[...] + gt * zz
        keep = jnp.where(lat, 1.0, 0.0).astype(F32)
        diff = (yv - t_ref[...]) * keep
        part = jnp.sum(jnp.sum(diff * diff, axis=0, keepdims=True), axis=1, keepdims=True)
        loss_ref[...] += part * (0.5 / d)
        dy = diff * (1.0 / d)
        dx_ref[...] = dy
        dz_ref[...] = (gt * dy).astype(BF16)
        _acc_row(sums_ref, 6, jnp.sum(dy * zz, axis=0, keepdims=True))

    row = pl.BlockSpec((TM, d), lambda i: (i, 0))
    return _pcall(
        body, name=name, grid=(t // TM,),
        in_specs=[row, pl.BlockSpec(mods.shape, lambda i: (0, 0)), row,
                  pl.BlockSpec((TM, d), lambda i: (jnp.maximum(i - 1, 0), 0))],
        out_specs=[pl.BlockSpec((8, 128), lambda i: (0, 0)), row, row,
                   pl.BlockSpec((8, d), lambda i: (0, 0))],
        out_shape=[jax.ShapeDtypeStruct((8, 128), F32), jax.ShapeDtypeStruct((t, d), F32),
                   jax.ShapeDtypeStruct((t, d), BF16), jax.ShapeDtypeStruct((8, d), F32)],
    )(x, mods, z, target)


def _row_bwd(xn, dxo, dh, mods, g, *, shift, scale, y=None, gate=None, name):
    t, d = xn.shape
    has_y = y is not None

    def body(*refs):
        refs = list(refs)
        x_ref, dxo_ref, dh_ref, mods_ref, g_ref = refs[:5]
        pos = 5
        if has_y:
            y_ref = refs[pos]; pos += 1
        dx_ref = refs[pos]; pos += 1
        if has_y:
            dy_ref = refs[pos]; pos += 1
        sums_ref = refs[pos]
        i = pl.program_id(0)
        lat = i > 0

        @pl.when(i == 0)
        def _():
            sums_ref[...] = jnp.zeros_like(sums_ref)

        x1 = x_ref[...]
        gv = g_ref[...]
        rs = lax.rsqrt(jnp.mean(x1 * x1, axis=-1, keepdims=True) + EPS)
        xh = x1 * rs
        dhv = dh_ref[...]
        dn = dhv * (1.0 + _mod_row(mods_ref, lat, scale))
        dxh = dn * gv
        dx = dxo_ref[...] + rs * (dxh - xh * jnp.mean(dxh * xh, axis=-1, keepdims=True))
        dx_ref[...] = dx
        vals = [jnp.sum(dhv, axis=0, keepdims=True),
                jnp.sum(dhv * (xh * gv), axis=0, keepdims=True),
                None,
                jnp.sum(dn * xh, axis=0, keepdims=True)]
        if has_y:
            dy_ref[...] = (_mod_row(mods_ref, lat, gate) * dx).astype(BF16)
            vals[2] = jnp.sum(dx * y_ref[...], axis=0, keepdims=True)

        @pl.when(i == 0)
        def _():
            for r, v in enumerate(vals):
                if v is not None:
                    _acc_row(sums_ref, r, v)

        @pl.when(i > 0)
        def _():
            for r, v in enumerate(vals):
                if v is not None:
                    _acc_row(sums_ref, 4 + r, v)

    row = pl.BlockSpec((TM, d), lambda i: (i, 0))
    ins = [xn, dxo, dh, mods, g.reshape(1, d)]
    specs = [row, row, row, pl.BlockSpec(mods.shape, lambda i: (0, 0)), pl.BlockSpec((1, d), lambda i: (0, 0))]
    out_shape, out_specs = [jax.ShapeDtypeStruct((t, d), F32)], [row]
    if has_y:
        ins.append(y); specs.append(row)
        out_shape.append(jax.ShapeDtypeStruct((t, d), BF16)); out_specs.append(row)
    out_shape.append(jax.ShapeDtypeStruct((8, d), F32))
    out_specs.append(pl.BlockSpec((8, d), lambda i: (0, 0)))
    return _pcall(body, name=name, grid=(t // TM,), in_specs=specs, out_specs=out_specs,
                  out_shape=out_shape)(*ins)


FFN_BK = 1408


def _ffn_order(n2):
    nb = n2 // (2 * FFN_BK)
    return [h * nb + j for j in range(nb) for h in (0, 1)]


def _ffn_interleave(w):
    return jnp.concatenate([w[..., b * FFN_BK:(b + 1) * FFN_BK] for b in _ffn_order(w.shape[-1])], axis=-1)


def _ffn_deinterleave(w):
    order = _ffn_order(w.shape[-1])
    return jnp.concatenate([w[..., order.index(b) * FFN_BK:(order.index(b) + 1) * FFN_BK]
                            for b in range(len(order))], axis=-1)


def _big_tile(t):
    return 384 if t % 384 == 0 else TM


def _ffn_in(h, w, *, lead, name):
    t, d = h.shape
    n2 = w.shape[-1]
    bm, bk = _big_tile(t), FFN_BK

    def body(h_ref, w_ref, u_ref, a_ref):
        ub = _dot(h_ref[...], w_ref[...]).astype(BF16)
        u_ref[...] = ub
        uf = ub.astype(F32)
        gv, up = uf[:, 0:bk], uf[:, bk:2 * bk]
        a_ref[...] = (gv * _sigmoid(gv) * up).astype(BF16)

    return _pcall(
        body, name=name, grid=(t // bm, n2 // (2 * bk)),
        in_specs=[pl.BlockSpec((bm, d), lambda i, j: (i, 0)),
                  pl.BlockSpec((None, d, 2 * bk), lambda i, j: (lead, 0, j))],
        out_specs=[pl.BlockSpec((bm, 2 * bk), lambda i, j: (i, j)), pl.BlockSpec((bm, bk), lambda i, j: (i, j))],
        out_shape=[jax.ShapeDtypeStruct((t, n2), BF16), jax.ShapeDtypeStruct((t, n2 // 2), BF16)],
    )(h, w)


def _ffn_dx(dz, w_out, u, *, lead, name):
    t, d = dz.shape
    n2 = u.shape[1]
    bm, bk = _big_tile(t), FFN_BK

    def body(dz_ref, w_ref, u_ref, du_ref):
        da = _dot_nt(dz_ref[...], w_ref[...])
        uf = u_ref[...].astype(F32)
        gv, up = uf[:, 0:bk], uf[:, bk:2 * bk]
        s = _sigmoid(gv)
        du_ref[:, 0:bk] = (da * up * (s * (1.0 + gv * (1.0 - s)))).astype(BF16)
        du_ref[:, bk:2 * bk] = (da * gv * s).astype(BF16)

    ublk = pl.BlockSpec((bm, 2 * bk), lambda i, j: (i, j))
    return _pcall(
        body, name=name, grid=(t // bm, n2 // (2 * bk)),
        in_specs=[pl.BlockSpec((bm, d), lambda i, j: (i, 0)),
                  pl.BlockSpec((None, bk, d), lambda i, j: (lead, j, 0)), ublk],
        out_specs=ublk, out_shape=jax.ShapeDtypeStruct((t, n2), BF16),
    )(dz, w_out, u)


def _lane(shape):
    return _iota(shape, len(shape) - 1)


def _pair_norm(x, g):
    lo = _lane(x.shape) < 64
    x2 = x * x
    s_lo = jnp.sum(jnp.where(lo, x2, 0.0), axis=-1, keepdims=True)
    s_hi = jnp.sum(jnp.where(lo, 0.0, x2), axis=-1, keepdims=True)
    rs = lax.rsqrt(jnp.where(lo, s_lo, s_hi) * (1.0 / 64) + EPS)
    return x * rs, rs


def _pair_mean(v):
    lo = _lane(v.shape) < 64
    s_lo = jnp.sum(jnp.where(lo, v, 0.0), axis=-1, keepdims=True)
    s_hi = jnp.sum(jnp.where(lo, 0.0, v), axis=-1, keepdims=True)
    return jnp.where(lo, s_lo, s_hi) * (1.0 / 64)


def _rot64(x):
    r1 = pltpu.roll(x, 32, 1)
    r2 = pltpu.roll(x, 96, 1)
    even = ((_lane(x.shape) >> 5) & 1) == 0
    return jnp.where(even, -r2, r1)


def _rope64(x, cos, sin):
    return x * cos + _rot64(x) * sin


def _rope64_t(d, cos, sin):
    return d * cos - _rot64(d * sin)


def _kprep_fwd(p, gk, cos, sin, *, name):
    t = p.shape[0]

    def body(k_ref, g_ref, c_ref, s_ref, o_ref):
        xh, _ = _pair_norm(k_ref[...], None)
        o_ref[...] = _rope64(xh * g_ref[...], c_ref[...], s_ref[...])

    blk = pl.BlockSpec((TM, 128), lambda i: (i, 0))
    return _pcall(
        body, name=name, grid=(t // TM,),
        in_specs=[pl.BlockSpec((TM, 128), lambda i: (i, 4)), pl.BlockSpec((1, 128), lambda i: (0, 0)), blk, blk],
        out_specs=blk, out_shape=jax.ShapeDtypeStruct((t, 128), F32),
    )(p, gk, cos, sin)


def _kprep_bwd(p, gk, cos, sin, dkp, dv, *, name):
    t = p.shape[0]

    def body(k_ref, g_ref, c_ref, s_ref, dkp_ref, dv_ref, o_ref, dg_ref):
        @pl.when(pl.program_id(0) == 0)
        def _():
            dg_ref[...] = jnp.zeros_like(dg_ref)
        xh, rs = _pair_norm(k_ref[...], None)
        dn = _rope64_t(dkp_ref[...], c_ref[...], s_ref[...])
        _acc_row(dg_ref, 0, jnp.sum(dn * xh, axis=0, keepdims=True))
        dxh = dn * g_ref[...]
        o_ref[:, 0:128] = (rs * (dxh - xh * _pair_mean(dxh * xh))).astype(BF16)
        o_ref[:, 128:256] = dv_ref[...].astype(BF16)

    blk = pl.BlockSpec((TM, 128), lambda i: (i, 0))
    return _pcall(
        body, name=name, grid=(t // TM,),
        in_specs=[pl.BlockSpec((TM, 128), lambda i: (i, 4)), pl.BlockSpec((1, 128), lambda i: (0, 0)), blk, blk, blk, blk],
        out_specs=[pl.BlockSpec((TM, 256), lambda i: (i, 0)), pl.BlockSpec((8, 128), lambda i: (0, 0))],
        out_shape=[jax.ShapeDtypeStruct((t, 256), BF16), jax.ShapeDtypeStruct((8, 128), F32)],
    )(p, gk, cos, sin, dkp, dv)


def _attn_common(i, t, lc, kp_ref, v_ref):
    span = QB + 2 * WINDOW
    start = pl.multiple_of(jnp.clip((i - 1) * QB, lc, t - span), QB)
    kall = jnp.concatenate([kp_ref[0:lc, :], kp_ref[pl.ds(start, span), :]], axis=0)
    vall = jnp.concatenate([v_ref[0:lc, :], v_ref[pl.ds(start, span), :]], axis=0)
    nk = lc + span
    col = _iota((QB, nk), 1)
    krow = jnp.where(col < lc, col, start + col - lc)
    qrow = i * QB + _iota((QB, nk), 0)
    valid = (col < lc) | ((qrow >= lc) & (krow >= lc) & (jnp.abs(krow - qrow) <= WINDOW))
    lo = _lane(kall.shape) < 64
    kroll, vroll = pltpu.roll(kall, 64, 1), pltpu.roll(vall, 64, 1)
    zero = jnp.zeros_like(kall)
    kvar = [[_bf(jnp.where(lo, kall, zero)), _bf(jnp.where(lo, zero, kroll))],
            [_bf(jnp.where(lo, kroll, zero)), _bf(jnp.where(lo, zero, kall))]]
    vvar = [[_bf(jnp.where(lo, vall, zero)), _bf(jnp.where(lo, zero, vroll))],
            [_bf(jnp.where(lo, vroll, zero)), _bf(jnp.where(lo, zero, vall))]]
    return start, valid, kvar, vvar


def _softmax_sink(s, valid, snk):
    s = jnp.where(valid, s, NEG)
    m = jnp.maximum(jnp.max(s, axis=-1, keepdims=True), snk)
    e = jnp.exp(s - m)
    es = jnp.exp(snk - m)
    inv = 1.0 / (jnp.sum(e, axis=-1, keepdims=True) + es)
    return e * inv, es * inv


def _attn_fwd(p, kp, gq, sink, cos, sin, *, lc, name):
    t = p.shape[0]
    scale = 64 ** -0.5

    def body(q_ref, kp_ref, v_ref, g_ref, sink_ref, c_ref, s_ref, o_ref):
        i = pl.program_id(0)
        _, valid, kvar, vvar = _attn_common(i, t, lc, kp_ref, v_ref)
        cosv, sinv, gv = c_ref[...], s_ref[...], g_ref[...]
        for j in range(4):
            xh, _ = _pair_norm(q_ref[:, 128 * j:128 * j + 128], None)
            q2 = _bf(_rope64(xh * gv, cosv, sinv))
            acc = jnp.zeros((QB, 128), F32)
            for half in range(2):
                s = _dot_nt(q2, kvar[j // 2][half]) * scale
                pr, _ = _softmax_sink(s, valid, sink_ref[2 * j + half])
                acc = acc + _dot(pr, vvar[j // 2][half])
            o_ref[:, 128 * j:128 * j + 128] = acc.astype(BF16)

    qblk = pl.BlockSpec((QB, 128), lambda i: (i, 0))
    return _pcall(
        body, name=name, grid=(t // QB,),
        in_specs=[pl.BlockSpec((QB, 512), lambda i: (i, 0)),
                  pl.BlockSpec((t, 128), lambda i: (0, 0)),
                  pl.BlockSpec((t, 128), lambda i: (0, 5)),
                  pl.BlockSpec((1, 128), lambda i: (0, 0)),
                  pl.BlockSpec(memory_space=pltpu.SMEM), qblk, qblk],
        out_specs=pl.BlockSpec((QB, 512), lambda i: (i, 0)),
        out_shape=jax.ShapeDtypeStruct((t, 512), BF16),
    )(p, kp, p, gq, sink, cos, sin)


def _attn_bwd(p, kp, gq, sink, cos, sin, dmix, *, lc, name):
    t = p.shape[0]
    scale = 64 ** -0.5
    span = QB + 2 * WINDOW

    def body(q_ref, kp_ref, v_ref, g_ref, sink_ref, c_ref, s_ref, do_ref,
             dq_ref, dk_ref, dv_ref, dg_ref, dsink_ref):
        i = pl.program_id(0)

        @pl.when(i == 0)
        def _():
            dk_ref[...] = jnp.zeros_like(dk_ref)
            dv_ref[...] = jnp.zeros_like(dv_ref)
            dg_ref[...] = jnp.zeros_like(dg_ref)
            dsink_ref[...] = jnp.zeros_like(dsink_ref)

        start, valid, kvar, vvar = _attn_common(i, t, lc, kp_ref, v_ref)
        cosv, sinv, gv = c_ref[...], s_ref[...], g_ref[...]
        nk = lc + span
        lo = _lane((nk, 128)) < 64
        dk_all = jnp.zeros((nk, 128), F32)
        dv_all = jnp.zeros((nk, 128), F32)
        for j in range(4):
            kvh = j // 2
            xh, rs = _pair_norm(q_ref[:, 128 * j:128 * j + 128], None)
            q2 = _bf(_rope64(xh * gv, cosv, sinv))
            do2 = _bf(do_ref[:, 128 * j:128 * j + 128])
            dq2 = jnp.zeros((QB, 128), F32)
            for half in range(2):
                s = _dot_nt(q2, kvar[kvh][half]) * scale
                pr, ps = _softmax_sink(s, valid, sink_ref[2 * j + half])
                dp = _dot_nt(do2, vvar[kvh][half])
                delta = jnp.sum(pr * dp, axis=-1, keepdims=True)
                ds = pr * (dp - delta) * scale
                dsk = jnp.sum(jnp.sum(-ps * delta, axis=0, keepdims=True), axis=1, keepdims=True)
                _acc_row(dsink_ref, 2 * j + half, jnp.broadcast_to(dsk, (1, 128)))
                dq2 = dq2 + _dot(ds, kvar[kvh][half])
                gk_ = _dot_tn(ds, q2)
                gv_ = _dot_tn(pr, do2)
                if half == 0:
                    gk_, gv_ = jnp.where(lo, gk_, 0.0), jnp.where(lo, gv_, 0.0)
                else:
                    gk_, gv_ = jnp.where(lo, 0.0, gk_), jnp.where(lo, 0.0, gv_)
                if half != kvh:
                    gk_, gv_ = pltpu.roll(gk_, 64, 1), pltpu.roll(gv_, 64, 1)
                dk_all = dk_all + gk_
                dv_all = dv_all + gv_
            dn = _rope64_t(dq2, cosv, sinv)
            _acc_row(dg_ref, 0, jnp.sum(dn * xh, axis=0, keepdims=True))
            dxh = dn * gv
            dq_ref[:, 128 * j:128 * j + 128] = (rs * (dxh - xh * _pair_mean(dxh * xh))).astype(BF16)
        dk_ref[0:lc, :] += dk_all[0:lc]
        dv_ref[0:lc, :] += dv_all[0:lc]
        dk_ref[pl.ds(start, span), :] += dk_all[lc:nk]
        dv_ref[pl.ds(start, span), :] += dv_all[lc:nk]

    qblk = pl.BlockSpec((QB, 128), lambda i: (i, 0))
    full = pl.BlockSpec((t, 128), lambda i: (0, 0))
    small = pl.BlockSpec((8, 128), lambda i: (0, 0))
    return _pcall(
        body, name=name, grid=(t // QB,),
        in_specs=[pl.BlockSpec((QB, 512), lambda i: (i, 0)), full,
                  pl.BlockSpec((t, 128), lambda i: (0, 5)),
                  pl.BlockSpec((1, 128), lambda i: (0, 0)),
                  pl.BlockSpec(memory_space=pltpu.SMEM), qblk, qblk,
                  pl.BlockSpec((QB, 512), lambda i: (i, 0))],
        out_specs=[pl.BlockSpec((QB, 512), lambda i: (i, 0)), full, full, small, small],
        out_shape=[jax.ShapeDtypeStruct((t, 512), BF16), jax.ShapeDtypeStruct((t, 128), F32),
                   jax.ShapeDtypeStruct((t, 128), F32), jax.ShapeDtypeStruct((8, 128), F32),
                   jax.ShapeDtypeStruct((8, 128), F32)],
    )(p, kp, p, gq, sink, cos, sin, dmix)


def _tri(rev):
    r, c = _iota((CHUNK, CHUNK), 0), _iota((CHUNK, CHUNK), 1)
    return (c >= r) if rev else (c <= r)


def _blk_map(nb, rev, backward):
    if not rev:
        return (lambda n: nb - 1 - n) if backward else (lambda n: n)
    if backward:
        return lambda n: jnp.where(n < nb - 1, n + 1, 0)
    return lambda n: jnp.where(n == 0, 0, nb - n)


def _chunk_order(rev, backward, nc=TM // CHUNK):
    order = list(range(nc))
    return order[::-1] if (rev != backward) else order


def _hgrn_gates(qraw, fraw, lb):
    sq = _sigmoid(qraw)
    sf = _sigmoid(fraw)
    f = lb + (1.0 - lb) * sf
    return qraw * sq, 1.0 - f, jnp.log(f), sq, sf, f


HGRN_HP = 2


def _chunk_cumsum(x, rev):
    n = x.shape[0]
    pos = _iota(x.shape, 0) & (CHUNK - 1)
    s = 1
    while s < CHUNK:
        if rev:
            x = x + jnp.where(pos < CHUNK - s, pltpu.roll(x, n - s, 0), 0.0)
        else:
            x = x + jnp.where(pos >= s, pltpu.roll(x, s, 0), 0.0)
        s *= 2
    return x


def _block_terms(lf, rev):
    b = _chunk_cumsum(lf, rev)
    mid, last = (CHUNK // 2 - 1, 0) if rev else (CHUNK // 2, CHUNK - 1)

    def chunk_row(off):
        return jnp.concatenate([jnp.broadcast_to(b[c * CHUNK + off:c * CHUNK + off + 1, :], (CHUNK, b.shape[1]))
                                for c in range(TM // CHUNK)], axis=0)

    r, bl = chunk_row(mid), chunk_row(last)
    return _tri(rev), jnp.exp(b - r), jnp.exp(r - b), jnp.exp(b), jnp.exp(bl - b), jnp.exp(bl)


def _headnorm_apply(o, gv, gain):
    n = o * lax.rsqrt(jnp.mean(o * o, axis=-1, keepdims=True) + EPS)
    if gain is not None:
        n = n * gain
    return (n * (gv * _sigmoid(gv))).astype(BF16)


def _headnorm_grad(o, gv, dy, gain):
    rs = lax.rsqrt(jnp.mean(o * o, axis=-1, keepdims=True) + EPS)
    xh = o * rs
    n = xh * gain if gain is not None else xh
    sg = _sigmoid(gv)
    dn = dy * (gv * sg)
    dg = (dy * n * (sg * (1.0 + gv * (1.0 - sg)))).astype(BF16)
    dgain = jnp.sum(dn * xh, axis=0, keepdims=True)
    dxh = dn * gain if gain is not None else dn
    return rs * (dxh - xh * jnp.mean(dxh * xh, axis=-1, keepdims=True)), dg, dgain


def _hgrn_fwd(p, lb, *, rev, name, ofw=None, gain=None):
    t = p.shape[0]
    nb, nc = t // TM, TM // CHUNK
    bmap = _blk_map(nb, rev, False)
    fcol = 14 if rev else 10
    fused = ofw is not None

    def body(*refs):
        q_ref, f_ref, v_ref, lb_ref = refs[:4]
        if fused:
            ofw_ref, g_ref, gain_ref, o_ref, sh_ref, mix_ref, st = refs[4:]
        else:
            o_ref, sh_ref, st = refs[4:]

        @pl.when(pl.program_id(1) == 0)
        def _():
            st[...] = jnp.zeros_like(st)
        for hh in range(HGRN_HP):
            ln = slice(128 * hh, 128 * hh + 128)
            q, k, lf, _, _, _ = _hgrn_gates(q_ref[:, ln], f_ref[:, ln], lb_ref[:, ln])
            tri, eq, ek, ei, eki, eb = _block_terms(lf, rev)
            qe, ke, qi, ki, vb = _bf(q * eq), _bf(k * ek), _bf(q * ei), _bf(k * eki), _bf(v_ref[:, ln])
            intra = []
            for cc in range(nc):
                rows = slice(cc * CHUNK, (cc + 1) * CHUNK)
                a = jnp.where(tri, _dot_nt(qe[rows], ke[rows]), 0.0)
                intra.append(_dot(a, vb[rows]))
            s = st[hh]
            for cc in _chunk_order(rev, False):
                rows = slice(cc * CHUNK, (cc + 1) * CHUNK)
                sh_ref[hh, cc] = s
                o_ref[rows, ln] = intra[cc] + _dot_nt(qi[rows], s)
                s = s * eb[cc * CHUNK:cc * CHUNK + 1, :] + _dot_tn(vb[rows], ki[rows])
            st[hh] = s
            if fused:
                osum = o_ref[:, ln] + ofw_ref[:, ln]
                o_ref[:, ln] = osum
                mix_ref[:, ln] = _headnorm_apply(osum, g_ref[:, ln], gain_ref[...])

    hp, wd = HGRN_HP, 128 * HGRN_HP

    def col(c0):
        return pl.BlockSpec((TM, wd), lambda h, n: (bmap(n), c0 // hp + h))

    oblk = pl.BlockSpec((TM, wd), lambda h, n: (bmap(n), h))
    ins, specs = [p, p, p, lb], [col(6), col(fcol), col(18), pl.BlockSpec((1, wd), lambda h, n: (0, h))]
    out_specs = [oblk, pl.BlockSpec((hp, nc, 128, 128), lambda h, n: (h, bmap(n), 0, 0))]
    out_shape = [jax.ShapeDtypeStruct((t, 512), F32), jax.ShapeDtypeStruct((4, t // CHUNK, 128, 128), F32)]
    if fused:
        ins += [ofw, p, gain]
        specs += [oblk, col(22), pl.BlockSpec((1, 128), lambda h, n: (0, 0))]
        out_specs.append(oblk)
        out_shape.append(jax.ShapeDtypeStruct((t, 512), BF16))
    return _pcall(body, name=name, grid=(4 // hp, nb), in_specs=specs, out_specs=out_specs, out_shape=out_shape,
                  scratch_shapes=[pltpu.VMEM((hp, 128, 128), F32)])(*ins)


def _hgrn_bwd(p, lb, sh, do, prev, *, rev, name, head=None):
    t = p.shape[0]
    nb, nc = t // TM, TM // CHUNK
    bmap = _blk_map(nb, rev, True)
    fcol = 14 if rev else 10
    has_prev = prev is not None
    odt = BF16 if has_prev else F32
    fused = head is not None

    def body(*refs):
        refs = list(refs)
        q_ref, f_ref, v_ref, lb_ref, sh_ref = refs[:5]
        pos = 5
        if fused:
            osum_ref, g_ref, dmix_ref, gain_ref = refs[5:9]
            pos = 9
        else:
            do_ref = refs[5]
            pos = 6
        if has_prev:
            pq_ref, pv_ref = refs[pos], refs[pos + 1]
            pos += 2
        dq_ref, df_ref, dv_ref, dlb_ref = refs[pos:pos + 4]
        pos += 4
        if fused:
            do_out, dg_ref, dgain_ref = refs[pos:pos + 3]
            pos += 3
        dst = refs[pos]

        @pl.when(pl.program_id(1) == 0)
        def _():
            dst[...] = jnp.zeros_like(dst)
            dlb_ref[...] = jnp.zeros_like(dlb_ref)

        if fused:
            @pl.when((pl.program_id(0) == 0) & (pl.program_id(1) == 0))
            def _():
                dgain_ref[...] = jnp.zeros_like(dgain_ref)

        cat = functools.partial(jnp.concatenate, axis=0)
        for hh in range(HGRN_HP):
            ln = slice(128 * hh, 128 * hh + 128)
            lbv = lb_ref[:, ln]
            qraw, fraw = q_ref[:, ln], f_ref[:, ln]
            q, k, lf, sq, sf, f = _hgrn_gates(qraw, fraw, lbv)
            tri, eq, ek, ei, eki, eb = _block_terms(lf, rev)
            qe, ke, qi, ki = q * eq, k * ek, q * ei, k * eki
            if fused:
                dov, dg, dgain = _headnorm_grad(osum_ref[:, ln], g_ref[:, ln], dmix_ref[:, ln], gain_ref[...])
                do_out[:, ln] = dov
                dg_ref[:, ln] = dg
                _acc_row(dgain_ref, 0, dgain)
            else:
                dov = do_ref[:, ln]
            qeb, keb, qib, kib, vb, dob = _bf(qe), _bf(ke), _bf(qi), _bf(ki), _bf(v_ref[:, ln]), _bf(dov)
            dv, dqe, dke, dqi = [None] * nc, [None] * nc, [None] * nc, [None] * nc
            for cc in range(nc):
                rows = slice(cc * CHUNK, (cc + 1) * CHUNK)
                a = jnp.where(tri, _dot_nt(qeb[rows], keb[rows]), 0.0)
                da = jnp.where(tri, _dot_nt(dob[rows], vb[rows]), 0.0)
                dv[cc] = _dot_tn(a, dob[rows])
                dqe[cc], dke[cc] = _dot(da, keb[rows]), _dot_tn(da, qeb[rows])
                dqi[cc] = _dot(dob[rows], sh_ref[hh, cc])
            dki, dbl = [None] * nc, [None] * nc
            ds = dst[hh]
            for cc in _chunk_order(rev, True):
                rows = slice(cc * CHUNK, (cc + 1) * CHUNK)
                ebc = eb[cc * CHUNK:cc * CHUNK + 1, :]
                dv[cc] = dv[cc] + _dot_nt(kib[rows], ds)
                dki[cc] = _dot(vb[rows], ds)
                dbl[cc] = jnp.broadcast_to(jnp.sum(dki[cc] * ki[rows], axis=0, keepdims=True)
                                           + jnp.sum(ds * sh_ref[hh, cc], axis=0, keepdims=True) * ebc, (CHUNK, 128))
                ds = ds * ebc + _dot_tn(dob[rows], qib[rows])
            dst[hh] = ds
            dqe, dke, dqi, dki, dv, dbl = cat(dqe), cat(dke), cat(dqi), cat(dki), cat(dv), cat(dbl)
            dq = dqe * eq + dqi * ei
            dk = dke * ek + dki * eki
            last = 0 if rev else CHUNK - 1
            db = dqe * qe - dke * ke + dqi * qi - dki * ki
            db = db + jnp.where((_iota(db.shape, 0) & (CHUNK - 1)) == last, dbl, 0.0)
            dlf = _chunk_cumsum(db, not rev)
            dqr = dq * (sq * (1.0 + qraw * (1.0 - sq)))
            dfv = dlf / f - dk
            dfr = dfv * (1.0 - lbv) * (sf * (1.0 - sf))
            dlb_ref[:, ln] += jnp.sum(dfv * (1.0 - sf), axis=0, keepdims=True)
            if has_prev:
                dqr = dqr + pq_ref[:, ln]
                dv = dv + pv_ref[:, ln]
            dq_ref[:, ln] = dqr.astype(odt)
            df_ref[:, ln] = dfr.astype(odt)
            dv_ref[:, ln] = dv.astype(odt)

    hp, wd = HGRN_HP, 128 * HGRN_HP

    def col(c0):
        return pl.BlockSpec((TM, wd), lambda h, n: (bmap(n), c0 // hp + h))

    oblk = pl.BlockSpec((TM, wd), lambda h, n: (bmap(n), h))
    ins = [p, p, p, lb, sh]
    specs = [col(6), col(fcol), col(18), pl.BlockSpec((1, wd), lambda h, n: (0, h)),
             pl.BlockSpec((hp, nc, 128, 128), lambda h, n: (h, bmap(n), 0, 0))]
    if fused:
        osum, dmix, gain = head
        ins += [osum, p, dmix, gain]
        specs += [oblk, col(22), pl.BlockSpec((TM, wd), lambda h, n: (bmap(n), 4 // hp + h)),
                  pl.BlockSpec((1, 128), lambda h, n: (0, 0))]
    else:
        ins.append(do); specs.append(oblk)
    if has_prev:
        ins += list(prev); specs += [oblk, oblk]
    out_specs = [oblk, oblk, oblk, pl.BlockSpec((1, wd), lambda h, n: (0, h))]
    out_shape = [jax.ShapeDtypeStruct((t, 512), odt)] * 3 + [jax.ShapeDtypeStruct((1, 512), F32)]
    if fused:
        out_specs += [oblk, oblk, pl.BlockSpec((8, 128), lambda h, n: (0, 0))]
        out_shape += [jax.ShapeDtypeStruct((t, 512), F32), jax.ShapeDtypeStruct((t, 512), BF16),
                      jax.ShapeDtypeStruct((8, 128), F32)]
    return _pcall(body, name=name, grid=(4 // hp, nb), in_specs=specs, out_specs=out_specs, out_shape=out_shape,
                  scratch_shapes=[pltpu.VMEM((hp, 128, 128), F32)])(*ins)


def _rope256(x, cos, sin):
    x1, x2 = x[:, 0:128], x[:, 128:256]
    return jnp.concatenate([x1 * cos - x2 * sin, x2 * cos + x1 * sin], axis=-1)


def _rope256_t(d, cos, sin):
    d1, d2 = d[:, 0:128], d[:, 128:256]
    return jnp.concatenate([d1 * cos + d2 * sin, d2 * cos - d1 * sin], axis=-1)


RET_DK, RET_DV, RET_H = 256, 512, 4
RET_KSCALE = RET_DK ** -0.5
RCH = TM
RET_HP = 2


def _ret_terms(lg, rev):
    r, c = _iota((RCH, RCH), 0), _iota((RCH, RCH), 1)
    rel = ((c - r) if rev else (r - c)).astype(F32)
    dmat = jnp.where(rel >= 0, jnp.exp(lg[:, 0:1] * jnp.maximum(rel, 0.0)), 0.0)
    pos = _iota((RCH, 1), 0).astype(F32)
    cnt = (RCH - pos) if rev else (pos + 1.0)
    ei = jnp.exp(lg * cnt)
    eki = jnp.exp(lg * (RCH - cnt))
    eb = jnp.exp(lg * float(RCH))
    return dmat, ei, eki, eb


def _ret_fwd(p, lgt, cos, sin, *, rev, name, ofw=None):
    t = p.shape[0]
    nb, nc = t // TM, TM // RCH
    bmap = _blk_map(nb, rev, False)
    fused = ofw is not None

    def body(*refs):
        q_ref, k_ref, v_ref, lg_ref, c_ref, s_ref = refs[:6]
        if fused:
            ofw_ref, g_ref, o_ref, sh_ref, mix_ref, st = refs[6:]
        else:
            o_ref, sh_ref, st = refs[6:]

        @pl.when(pl.program_id(1) == 0)
        def _():
            st[...] = jnp.zeros_like(st)
        for hh in range(RET_HP):
            qc, vc = slice(RET_DK * hh, RET_DK * (hh + 1)), slice(RET_DV * hh, RET_DV * (hh + 1))
            dmat, ei, eki, eb = _ret_terms(lg_ref[hh], rev)
            for cc in _chunk_order(rev, False, nc):
                rows = slice(cc * RCH, (cc + 1) * RCH)
                cosv, sinv = c_ref[rows, :], s_ref[rows, :]
                q = _rope256(q_ref[rows, qc].astype(F32), cosv, sinv)
                k = _rope256(k_ref[rows, qc].astype(F32), cosv, sinv) * RET_KSCALE
                v = v_ref[rows, vc]
                s0 = st[hh]
                sh_ref[hh, cc] = s0.astype(BF16)
                a = _dot_nt(q, k) * dmat
                o = _dot(a, v) + _dot_nt(q * ei, s0)
                st[hh] = s0 * eb + _dot_tn(v, k * eki)
                if fused:
                    o = o + ofw_ref[rows, vc]
                    mix_ref[rows, vc] = _headnorm_apply(o, g_ref[rows, vc].astype(F32), None)
                o_ref[rows, vc] = o

    hp = RET_HP
    tab = pl.BlockSpec((TM, 128), lambda h, n: (bmap(n), 0))
    oblk = pl.BlockSpec((TM, hp * RET_DV), lambda h, n: (bmap(n), h))
    ins = [p, p, p, lgt, cos, sin]
    specs = [pl.BlockSpec((TM, hp * RET_DK), lambda h, n: (bmap(n), h)),
             pl.BlockSpec((TM, hp * RET_DK), lambda h, n: (bmap(n), RET_H // hp + h)),
             pl.BlockSpec((TM, hp * RET_DV), lambda h, n: (bmap(n), RET_H // hp + h)),
             pl.BlockSpec((hp, 1, RET_DK), lambda h, n: (h, 0, 0)), tab, tab]
    out_specs = [oblk, pl.BlockSpec((hp, nc, RET_DV, RET_DK), lambda h, n: (h, bmap(n), 0, 0))]
    out_shape = [jax.ShapeDtypeStruct((t, RET_H * RET_DV), F32),
                 jax.ShapeDtypeStruct((RET_H, t // RCH, RET_DV, RET_DK), BF16)]
    if fused:
        ins += [ofw, p]
        specs += [oblk, pl.BlockSpec((TM, hp * RET_DV), lambda h, n: (bmap(n), 2 * RET_H // hp + h))]
        out_specs.append(oblk)
        out_shape.append(jax.ShapeDtypeStruct((t, RET_H * RET_DV), BF16))
    return _pcall(body, name=name, grid=(RET_H // hp, nb), in_specs=specs, out_specs=out_specs, out_shape=out_shape,
                  scratch_shapes=[pltpu.VMEM((hp, RET_DV, RET_DK), F32)])(*ins)


def _ret_bwd(p, lgt, cos, sin, sh, do, prev, *, rev, name, head=None):
    t = p.shape[0]
    nb, nc = t // TM, TM // RCH
    bmap = _blk_map(nb, rev, True)
    has_prev = prev is not None
    odt = BF16 if has_prev else F32
    fused = head is not None

    def body(*refs):
        refs = list(refs)
        q_ref, k_ref, v_ref, lg_ref, c_ref, s_ref, sh_ref = refs[:7]
        if fused:
            osum_ref, g_ref, dmix_ref = refs[7:10]
            pos = 10
        else:
            do_ref = refs[7]
            pos = 8
        if has_prev:
            pq_ref, pk_ref, pv_ref = refs[pos:pos + 3]
            pos += 3
        dq_ref, dk_ref, dv_ref = refs[pos:pos + 3]
        pos += 3
        if fused:
            do_out, dg_ref = refs[pos:pos + 2]
            pos += 2
        dst = refs[pos]

        @pl.when(pl.program_id(1) == 0)
        def _():
            dst[...] = jnp.zeros_like(dst)

        for hh in range(RET_HP):
            qc, vc = slice(RET_DK * hh, RET_DK * (hh + 1)), slice(RET_DV * hh, RET_DV * (hh + 1))
            dmat, ei, eki, eb = _ret_terms(lg_ref[hh], rev)
            for cc in _chunk_order(rev, True, nc):
                rows = slice(cc * RCH, (cc + 1) * RCH)
                cosv, sinv = c_ref[rows, :], s_ref[rows, :]
                q = _rope256(q_ref[rows, qc].astype(F32), cosv, sinv)
                k = _rope256(k_ref[rows, qc].astype(F32), cosv, sinv) * RET_KSCALE
                v = v_ref[rows, vc]
                if fused:
                    dov, dg, _ = _headnorm_grad(osum_ref[rows, vc], g_ref[rows, vc].astype(F32), dmix_ref[rows, vc], None)
                    do_out[rows, vc] = dov
                    dg_ref[rows, vc] = dg
                else:
                    dov = do_ref[rows, vc]
                s0 = sh_ref[hh, cc]
                dsc = dst[hh]
                qi, ki = q * ei, k * eki
                a = _dot_nt(q, k) * dmat
                da = _dot_nt(dov, v) * dmat
                dv = _dot_tn(a, dov) + _dot_nt(ki, dsc)
                dqs = _dot(da, k) + _dot(dov, s0) * ei
                dks = _dot_tn(da, q) + _dot(v, dsc) * eki
                dst[hh] = dsc * eb + _dot_tn(dov, qi)
                dq = _rope256_t(dqs, cosv, sinv)
                dk = _rope256_t(dks * RET_KSCALE, cosv, sinv)
                if has_prev:
                    dq = dq + pq_ref[rows, qc]
                    dk = dk + pk_ref[rows, qc]
                    dv = dv + pv_ref[rows, vc]
                dq_ref[rows, qc] = dq.astype(odt)
                dk_ref[rows, qc] = dk.astype(odt)
                dv_ref[rows, vc] = dv.astype(odt)

    hp = RET_HP
    tab = pl.BlockSpec((TM, 128), lambda h, n: (bmap(n), 0))
    qblk = pl.BlockSpec((TM, hp * RET_DK), lambda h, n: (bmap(n), h))
    vblk = pl.BlockSpec((TM, hp * RET_DV), lambda h, n: (bmap(n), h))
    ins = [p, p, p, lgt, cos, sin, sh]
    specs = [qblk, pl.BlockSpec((TM, hp * RET_DK), lambda h, n: (bmap(n), RET_H // hp + h)),
             pl.BlockSpec((TM, hp * RET_DV), lambda h, n: (bmap(n), RET_H // hp + h)),
             pl.BlockSpec((hp, 1, RET_DK), lambda h, n: (h, 0, 0)), tab, tab,
             pl.BlockSpec((hp, nc, RET_DV, RET_DK), lambda h, n: (h, bmap(n), 0, 0))]
    if fused:
        osum, dmix = head
        ins += [osum, p, dmix]
        specs += [vblk, pl.BlockSpec((TM, hp * RET_DV), lambda h, n: (bmap(n), 2 * RET_H // hp + h)), vblk]
    else:
        ins.append(do); specs.append(vblk)
    if has_prev:
        ins += list(prev); specs += [qblk, qblk, vblk]
    out_specs = [qblk, qblk, vblk]
    out_shape = [jax.ShapeDtypeStruct((t, RET_H * RET_DK), odt), jax.ShapeDtypeStruct((t, RET_H * RET_DK), odt),
                 jax.ShapeDtypeStruct((t, RET_H * RET_DV), odt)]
    if fused:
        out_specs += [vblk, vblk]
        out_shape += [jax.ShapeDtypeStruct((t, RET_H * RET_DV), F32), jax.ShapeDtypeStruct((t, RET_H * RET_DV), BF16)]
    return _pcall(body, name=name, grid=(RET_H // hp, nb), in_specs=specs, out_specs=out_specs, out_shape=out_shape,
                  scratch_shapes=[pltpu.VMEM((hp, RET_DV, RET_DK), F32)])(*ins)


def _rope_tables(lc, l):
    tt = jnp.arange(l)
    row, colp = (tt // 64).astype(F32), (tt % 64).astype(F32)
    inv = 10000.0 ** (-jnp.arange(16, dtype=F32) / 16)
    ang = jnp.concatenate([row[:, None] * inv, colp[:, None] * inv], axis=-1)
    ang = jnp.concatenate([jnp.zeros((lc, 32), F32), ang], axis=0)
    acos, asin = jnp.tile(jnp.cos(ang), (1, 4)), jnp.tile(jnp.sin(ang), (1, 4))
    theta = 1.0 / (10000.0 ** jnp.linspace(0.0, 1.0, 128, dtype=F32))
    rang = jnp.arange(l, dtype=F32)[:, None] * theta
    rang = jnp.concatenate([jnp.zeros((lc, 128), F32), rang], axis=0)
    return acos, asin, jnp.cos(rang), jnp.sin(rang)


class _Weights:
    def __init__(self, w):
        self.w = w

    def first(self, after):
        return self.w

    def rest_landed(self, after):
        pass

    def rest(self, after):
        return self.w


def _local_step(x0, target, mods, ng, wsrc, small):
    t, d = x0.shape
    l = target.shape[0]
    lc = t - l
    acos, asin, rcos, rsin = _rope_tables(lc, l)
    lg_fw = jnp.log(1.0 - 2.0 ** (-5.0 - jnp.arange(RET_H, dtype=F32)))
    lgt_fw = jnp.broadcast_to(lg_fw[:, None, None], (RET_H, 1, RET_DK))
    lgt_bw = jnp.broadcast_to(lg_fw[::-1][:, None, None], (RET_H, 1, RET_DK))
    gq, gk, sink, gain, lb = small['gq'], small['gk'], small['sink'], small['gain'], small['lb']

    (h1,) = _row_fwd(x0, mods, g=ng[0], shift=0, scale=1, name='l0_norm1')
    w = wsrc.first(h1)
    p0 = _mm_nn(h1, w['even_in'], name='l0_in')
    kp = _kprep_fwd(p0, gk, acos, asin, name='l0_kprep')
    att = _attn_fwd(p0, kp, gq, sink, acos, asin, lc=lc, name='l0_attn')
    hof, hsf = _hgrn_fwd(p0, lb, rev=False, name='l0_hgrn_f')
    wsrc.rest_landed(hof)
    hos, hsb, bmix = _hgrn_fwd(p0, lb, rev=True, name='l0_hgrn_b', ofw=hof, gain=gain)
    mix0 = jnp.concatenate([att, bmix], axis=1)
    y0 = _mm_nn(mix0, w['even_out'], name='l0_out')
    x1, h2 = _row_fwd(x0, mods, y=y0, gate=2, g=ng[1], shift=3, scale=4, name='l0_norm2')
    w = dict(w, **wsrc.rest(h2))
    u0, a0 = _ffn_in(h2, w['ffn_in'], lead=0, name='ffn_in')
    z0 = _mm_nn(a0, w['ffn_out'], lead=0, name='ffn_out')
    x2, h3 = _row_fwd(x1, mods, y=z0, gate=5, g=ng[2], shift=12, scale=13, name='l1_norm1')
    p1 = _mm_nn(h3, w['odd_in'], out_dtype=BF16, name='l1_in')
    rof, rsf = _ret_fwd(p1, lgt_fw, rcos, rsin, rev=False, name='l1_ret_f')
    ros, rsb, mix1 = _ret_fwd(p1, lgt_bw, rcos, rsin, rev=True, name='l1_ret_b', ofw=rof)
    y1 = _mm_nn(mix1, w['odd_out'], name='l1_out')
    x3, h4 = _row_fwd(x2, mods, y=y1, gate=14, g=ng[3], shift=15, scale=16, name='l1_norm2')
    u1, a1 = _ffn_in(h4, w['ffn_in'], lead=1, name='ffn_in')
    z1 = _mm_nn(a1, w['ffn_out'], lead=1, name='ffn_out')
    loss, dx4, dz1, s_fin = _row_final(x3, z1, mods, target, gate=17, name='loss')

    du1 = _ffn_dx(dz1, w['ffn_out'], u1, lead=1, name='ffn_out_dx')
    g_ffn_out1 = _mm_tn(a1, dz1, name='ffn_out_dw')
    dh4 = _mm_nt(du1, w['ffn_in'], lead=1, name='ffn_in_dx')
    g_ffn_in1 = _mm_tn(h4, du1, name='ffn_in_dw')
    dx3, dy1, s_l1n2 = _row_bwd(x3, dx4, dh4, mods, ng[3], shift=15, scale=16, y=y1, gate=14, name='l1_norm2_bwd')
    dmix1 = _mm_nt(dy1, w['odd_out'], name='l1_out_dx')
    g_odd_out = _mm_tn(mix1, dy1, name='l1_out_dw')
    rdq, rdk, rdv, rdo, rdg = _ret_bwd(p1, lgt_fw, rcos, rsin, rsf, None, None, rev=False, name='l1_ret_f_bwd',
                                       head=(ros, dmix1))
    rdq, rdk, rdv = _ret_bwd(p1, lgt_bw, rcos, rsin, rsb, rdo, (rdq, rdk, rdv), rev=True, name='l1_ret_b_bwd')
    dp1 = jnp.concatenate([rdq, rdk, rdv, rdg], axis=1)
    dh3 = _mm_nt(dp1, w['odd_in'], name='l1_in_dx')
    g_odd_in = _mm_tn(h3, dp1, name='l1_in_dw')
    dx2, dz0, s_l1n1 = _row_bwd(x2, dx3, dh3, mods, ng[2], shift=12, scale=13, y=z0, gate=5, name='l1_norm1_bwd')
    du0 = _ffn_dx(dz0, w['ffn_out'], u0, lead=0, name='ffn_out_dx')
    g_ffn_out0 = _mm_tn(a0, dz0, name='ffn_out_dw')
    dh2 = _mm_nt(du0, w['ffn_in'], lead=0, name='ffn_in_dx')
    g_ffn_in0 = _mm_tn(h2, du0, name='ffn_in_dw')
    dx1, dy0, s_l0n2 = _row_bwd(x1, dx2, dh2, mods, ng[1], shift=3, scale=4, y=y0, gate=2, name='l0_norm2_bwd')
    dmix0 = _mm_nt(dy0, w['even_out'], name='l0_out_dx')
    g_even_out = _mm_tn(mix0, dy0, name='l0_out_dw')
    hq, hff, hv, dlb_f, hdo, hdg, s_gain = _hgrn_bwd(p0, lb, hsf, None, None, rev=False, name='l0_hgrn_f_bwd',
                                                     head=(hos, dmix0, gain))
    hq, hfb, hv, dlb_b = _hgrn_bwd(p0, lb, hsb, hdo, (hq, hv), rev=True, name='l0_hgrn_b_bwd')
    adq, dkp, adv, s_gq, s_sink = _attn_bwd(p0, kp, gq, sink, acos, asin, dmix0, lc=lc, name='l0_attn_bwd')
    dkv, s_gk = _kprep_bwd(p0, gk, acos, asin, dkp, adv, name='l0_kprep_bwd')
    dp0 = jnp.concatenate([adq, dkv, hq, _bf(hff), hfb, hv, hdg], axis=1)
    dh1 = _mm_nt(dp0, w['even_in'], name='l0_in_dx')
    g_even_in = _mm_tn(h1, dp0, name='l0_in_dw')
    dx0, s_l0n1 = _row_bwd(x0, dx1, dh1, mods, ng[0], shift=0, scale=1, name='l0_norm1_bwd')

    grads = dict(ffn_in=[g_ffn_in0, g_ffn_in1], ffn_out=[g_ffn_out0, g_ffn_out1],
                 even_in=g_even_in, even_out=g_even_out, odd_in=g_odd_in, odd_out=g_odd_out)
    sums = dict(fin=s_fin, l1n2=s_l1n2, l1n1=s_l1n1, l0n2=s_l0n2, l0n1=s_l0n1, gain=s_gain, gq=s_gq, gk=s_gk,
                sink=s_sink, dlb_f=dlb_f, dlb_b=dlb_b)
    return loss, dx0, grads, sums


def _place():
    return lax.axis_index("x"), lax.axis_index("y"), lax.axis_index("c")


def _ag8(blk, *, name):
    r, c = blk.shape
    flips = [(dx, dy, dc) for dx in (0, 1) for dy in (0, 1) for dc in (0, 1) if (dx, dy, dc) != (0, 0, 0)]

    def body(x_ref, out_ref, send_sems, recv_sems, local_sem):
        ax, ay, ac = _place()
        me = 4 * ax + 2 * ay + ac
        mine = pltpu.make_async_copy(x_ref, out_ref.at[me], local_sem)
        mine.start()
        sent = []
        for k, (dx, dy, dc) in enumerate(flips):
            peer = (lax.rem(ax + dx, 2), lax.rem(ay + dy, 2), lax.rem(ac + dc, 2))
            cp = pltpu.make_async_remote_copy(src_ref=x_ref, dst_ref=out_ref.at[me], send_sem=send_sems.at[k],
                                              recv_sem=recv_sems.at[k], device_id=peer, device_id_type=MESH)
            cp.start()
            sent.append((cp, 4 * peer[0] + 2 * peer[1] + peer[2]))
        for k, (cp, pidx) in enumerate(sent):
            pltpu.make_async_remote_copy(src_ref=x_ref, dst_ref=out_ref.at[pidx], send_sem=send_sems.at[k],
                                         recv_sem=recv_sems.at[k], device_id=(ax, ay, ac),
                                         device_id_type=MESH).wait_recv()
        for cp, _ in sent:
            cp.wait_send()
        mine.wait()

    return _pcall(
        body, name=name,
        in_specs=[pl.BlockSpec(memory_space=pltpu.VMEM)],
        out_specs=pl.BlockSpec(memory_space=pltpu.VMEM),
        out_shape=jax.ShapeDtypeStruct((8, r, c), blk.dtype),
        scratch_shapes=[pltpu.SemaphoreType.DMA((7,)), pltpu.SemaphoreType.DMA((7,)), pltpu.SemaphoreType.DMA],
    )(blk)


def _chip_exchange(arrs, *, scatter, name):
    n = len(arrs)
    rel = [(1, 0), (0, 1), (1, 1)]

    def body(*refs):
        ins, outs = refs[:n], refs[n:2 * n]
        send_sems, recv_sems, local_sems = refs[2 * n:]
        ax, ay, ac = _place()
        s = 2 * ax + ay
        started, local = [], []
        for a in range(n):
            lcp = pltpu.make_async_copy(ins[a].at[s] if scatter else ins[a], outs[a].at[s], local_sems.at[a])
            lcp.start()
            local.append(lcp)
            for r, (dx, dy) in enumerate(rel):
                px, py = lax.rem(ax + dx, 2), lax.rem(ay + dy, 2)
                sp = 2 * px + py
                cp = pltpu.make_async_remote_copy(
                    src_ref=ins[a].at[sp] if scatter else ins[a], dst_ref=outs[a].at[s],
                    send_sem=send_sems.at[3 * a + r], recv_sem=recv_sems.at[3 * a + r],
                    device_id=(px, py, ac), device_id_type=MESH)
                cp.start()
                started.append((cp, a, r, sp))
        for cp, a, r, sp in started:
            pltpu.make_async_remote_copy(
                src_ref=ins[a].at[sp] if scatter else ins[a], dst_ref=outs[a].at[sp],
                send_sem=send_sems.at[3 * a + r], recv_sem=recv_sems.at[3 * a + r],
                device_id=(ax, ay, ac), device_id_type=MESH).wait_recv()
        for cp, _, _, _ in started:
            cp.wait_send()
        for lcp in local:
            lcp.wait()

    hbm = pl.BlockSpec(memory_space=pl.ANY)
    shapes = [jax.ShapeDtypeStruct(a.shape if scatter else (4,) + a.shape, a.dtype) for a in arrs]
    return _pcall(
        body, name=name, in_specs=[hbm] * n, out_specs=[hbm] * n, out_shape=shapes,
        scratch_shapes=[pltpu.SemaphoreType.DMA((3 * n,)), pltpu.SemaphoreType.DMA((3 * n,)),
                        pltpu.SemaphoreType.DMA((n,))],
    )(*arrs)


_HBM = pl.BlockSpec(memory_space=pltpu.HBM)
_SEM = pl.BlockSpec(memory_space=pltpu.SEMAPHORE)
_DATAFLOW = pltpu.SideEffectType.DATAFLOW_SIDE_EFFECTING


def _split_start(bufs, plan, k, *, name):
    n = len(bufs)

    def body(*refs):
        ins, send_sems, recv_sems, token = refs[:n], refs[n], refs[n + 1], refs[2 * n + 2]
        for i, (src, dst, dev) in enumerate(plan(ins)):
            pltpu.make_async_remote_copy(src_ref=src, dst_ref=dst, send_sem=send_sems.at[i], recv_sem=recv_sems.at[i],
                                         device_id=dev, device_id_type=MESH).start()
        token[...] = jnp.zeros_like(token)

    res = _pcall(
        body, name=name,
        out_shape=(pltpu.SemaphoreType.DMA((k,)), pltpu.SemaphoreType.DMA((k,)),
                   *[pltpu.HBM(b.shape, b.dtype) for b in bufs], jax.ShapeDtypeStruct((8, 128), F32)),
        in_specs=[_HBM] * n, out_specs=(_SEM, _SEM, *[_HBM] * n, pl.BlockSpec(memory_space=pltpu.VMEM)),
        input_output_aliases={i: 2 + i for i in range(n)},
        compiler_params=pltpu.CompilerParams(has_side_effects=_DATAFLOW),
    )(*[pltpu.with_memory_space_constraint(b, pltpu.HBM) for b in bufs])
    return res[0], res[1], list(res[2:2 + n]), res[2 + n]


def _split_wait(bufs, send_sems, recv_sems, plan, after, *, name):
    n = len(bufs)

    def body(*refs):
        ins, ssem, rsem = refs[:n], refs[n], refs[n + 1]
        for i, (src, dst, dev) in enumerate(plan(ins)):
            cp = pltpu.make_async_remote_copy(src_ref=src, dst_ref=dst, send_sem=ssem.at[i], recv_sem=rsem.at[i],
                                              device_id=dev, device_id_type=MESH)
            cp.wait_send()
            cp.wait_recv()

    res = _pcall(
        body, name=name, out_shape=tuple(pltpu.HBM(b.shape, b.dtype) for b in bufs),
        in_specs=[_HBM] * n + [_SEM, _SEM, pl.BlockSpec(memory_space=pl.ANY)], out_specs=tuple([_HBM] * n),
        input_output_aliases={i: i for i in range(n)},
        compiler_params=pltpu.CompilerParams(has_side_effects=_DATAFLOW),
    )(*bufs, send_sems, recv_sems, after)
    return list(res)


_CHIP_FLIPS = [(1, 0), (0, 1), (1, 1)]


class _GatheredWeights:
    FIRST = ('even_in', 'even_out')
    REST = ('ffn_in', 'ffn_out', 'odd_in', 'odd_out')

    def __init__(self, shards):
        self.shards = shards
        self.ici = {}
        for grp, names in (('first', self.FIRST), ('rest', self.REST)):
            src = [shards[nm].reshape(2, shards[nm].shape[0] // 2, shards[nm].shape[1]) for nm in names]
            land = [lax.empty((4,) + a.shape, a.dtype) for a in src]
            m = len(names)
            sends, recvs, bufs, token = _split_start(src + land, functools.partial(self._ici_plan, m, True), 3 * m,
                                                     name='gather_' + grp + '_ici_start')
            self.ici[grp] = (sends, recvs, bufs, m)
            self.token = token if grp == 'first' else self.token + token
        self.rest_d2d = None

    @staticmethod
    def _ici_plan(m, sending, refs):
        ax, ay, ac = _place()
        s = 2 * ax + ay
        out = []
        for a in range(m):
            for dx, dy in _CHIP_FLIPS:
                px, py = lax.rem(ax + dx, 2), lax.rem(ay + dy, 2)
                slot = s if sending else 2 * px + py
                out.append((refs[a].at[ac], refs[m + a].at[slot, ac], (px, py, ac)))
        return out

    @staticmethod
    def _d2d_plan(m, sending, refs):
        ax, ay, ac = _place()
        out = []
        for a in range(m):
            for dx, dy in _CHIP_FLIPS:
                sp = 2 * lax.rem(ax + dx, 2) + lax.rem(ay + dy, 2)
                out.append((refs[a].at[sp, ac], refs[a].at[sp, ac if sending else 1 - ac], (ax, ay, 1 - ac)))
        return out

    def _landed(self, grp, after):
        sends, recvs, bufs, m = self.ici[grp]
        bufs = _split_wait(bufs, sends, recvs, functools.partial(self._ici_plan, m, False), after,
                           name='gather_' + grp + '_ici_wait')
        sends, recvs, land, _ = _split_start(bufs[m:], functools.partial(self._d2d_plan, m, True), 3 * m,
                                             name='gather_' + grp + '_d2d_start')
        return sends, recvs, land, m

    def _full(self, grp, names, d2d, after):
        sends, recvs, land, m = d2d
        land = _split_wait(land, sends, recvs, functools.partial(self._d2d_plan, m, False), after,
                           name='gather_' + grp + '_d2d_wait')
        s = 2 * lax.axis_index("x") + lax.axis_index("y")
        slot = lax.broadcasted_iota(jnp.int32, (4, 1, 1), 0)
        return {nm: _from_shards(nm, jnp.where(slot == s, self.shards[nm][None], g.reshape((4,) + self.shards[nm].shape)))
                for nm, g in zip(names, land)}

    def first(self, after):
        return self._full('first', self.FIRST, self._landed('first', after), after)

    def rest_landed(self, after):
        self.rest_d2d = self._landed('rest', after)

    def rest(self, after):
        return self._full('rest', self.REST, self.rest_d2d, after)


def _to_sibling(arrs, *, name):
    n = len(arrs)

    def body(*refs):
        ins, outs = refs[:n], refs[n:2 * n]
        send_sems, recv_sems = refs[2 * n:]
        ax, ay, ac = _place()
        cps = [pltpu.make_async_remote_copy(src_ref=ins[a], dst_ref=outs[a], send_sem=send_sems.at[a],
                                            recv_sem=recv_sems.at[a], device_id=(ax, ay, 1 - ac),
                                            device_id_type=MESH) for a in range(n)]
        for cp in cps:
            cp.start()
        for cp in cps:
            cp.wait_recv()
        for cp in cps:
            cp.wait_send()

    hbm = pl.BlockSpec(memory_space=pl.ANY)
    return _pcall(
        body, name=name, in_specs=[hbm] * n, out_specs=[hbm] * n,
        out_shape=[jax.ShapeDtypeStruct(a.shape, a.dtype) for a in arrs],
        scratch_shapes=[pltpu.SemaphoreType.DMA((n,))] * 2,
    )(*arrs)


def _mod_fwd(cond_raw, mw, mb, *, name):
    _, d, n = mw.shape

    def body(c_ref, w_ref, b_ref, o_ref):
        cv = c_ref[...]
        o_ref[...] = _dot(cv * _sigmoid(cv), w_ref[...]) + b_ref[...]

    return _pcall(
        body, name=name, grid=(2,),
        in_specs=[pl.BlockSpec((16, d), lambda l: (0, 0)), pl.BlockSpec((None, d, n), lambda l: (l, 0, 0)),
                  pl.BlockSpec((None, 1, n), lambda l: (l, 0, 0))],
        out_specs=pl.BlockSpec((None, 16, n), lambda l: (l, 0, 0)),
        out_shape=jax.ShapeDtypeStruct((2, 16, n), F32),
    )(cond_raw, mw, mb)


def _mod_bwd(cond_raw, dms, mw, *, name):
    _, d, n = mw.shape

    def body(c_ref, dm_ref, w_ref, gw_ref, dc_ref):
        @pl.when(pl.program_id(0) == 0)
        def _():
            dc_ref[...] = jnp.zeros_like(dc_ref)
        cv = c_ref[...]
        gw_ref[...] = _dot_tn(cv * _sigmoid(cv), dm_ref[...])
        dc_ref[...] += _dot_nt(dm_ref[...], w_ref[...])

    return _pcall(
        body, name=name, grid=(2,),
        in_specs=[pl.BlockSpec((16, d), lambda l: (0, 0)), pl.BlockSpec((None, 16, n), lambda l: (l, 0, 0)),
                  pl.BlockSpec((None, d, n), lambda l: (l, 0, 0))],
        out_specs=[pl.BlockSpec((None, d, n), lambda l: (l, 0, 0)), pl.BlockSpec((16, d), lambda l: (0, 0))],
        out_shape=[jax.ShapeDtypeStruct((2, d, n), F32), jax.ShapeDtypeStruct((16, d), F32)],
    )(cond_raw, dms, mw)


def _lb_fwd(hgrn_lb, *, name):
    def body(a_ref, o_ref):
        a0, a1 = a_ref[0:1, :], a_ref[1:2, :]
        m = jnp.maximum(a0, a1)
        e0, e1 = jnp.exp(a0 - m), jnp.exp(a1 - m)
        o_ref[...] = e0 / (e0 + e1)

    return _pcall(body, name=name, out_shape=jax.ShapeDtypeStruct((1, hgrn_lb.shape[1]), F32))(hgrn_lb)


PACK_TILES = ('l0n1', 'l0n2', 'l1n1', 'l1n2', 'fin', 'gq', 'gk', 'gain', 'dlb_f', 'dlb_b', 'sink')
PACK_ROW = {nm: 8 * i for i, nm in enumerate(PACK_TILES)}
MOD_SOURCE = ((('l0n1', 0), ('l0n1', 1), ('l0n2', 2), ('l0n2', 0), ('l0n2', 1), ('l1n1', 2)),
              (('l1n1', 0), ('l1n1', 1), ('l1n2', 2), ('l1n2', 0), ('l1n2', 1), ('fin', 2)))


def _small_finalize(gath, lb_pad, *, name):
    d = gath.shape[2]

    def body(g_ref, lb_ref, small_ref, glb_ref, gmb_ref, dm_ref):
        tot = g_ref[0]
        for e in range(1, 8):
            tot = tot + g_ref[e]

        def row(nm, r=0):
            return tot[PACK_ROW[nm] + r:PACK_ROW[nm] + r + 1, :]

        for k, nm in enumerate(('l0n1', 'l0n2', 'l1n1', 'l1n2')):
            small_ref[k:k + 1, :] = row(nm, 3) + row(nm, 7)
        for k, nm in ((4, 'gq'), (5, 'gk')):
            small_ref[k:k + 1, :] = row(nm) + pltpu.roll(row(nm), d - 64, 1)
        small_ref[6:7, :] = row('gain')
        small_ref[7:8, :] = row('sink')
        lbv = lb_ref[...]
        g0 = (row('dlb_f') + row('dlb_b')) * lbv * (1.0 - lbv)
        glb_ref[...] = jnp.zeros_like(glb_ref)
        glb_ref[0:1, :] = g0
        glb_ref[1:2, :] = -g0
        dm_ref[...] = jnp.zeros_like(dm_ref)
        for l in range(2):
            for part in range(6):
                nm, r = MOD_SOURCE[l][part]
                gmb_ref[l * 6 + part:l * 6 + part + 1, :] = row(nm, r) + row(nm, r + 4)
                rl = PACK_ROW[nm] + r + 4
                for e in range(8):
                    dm_ref[l, part, e:e + 1, :] = g_ref[e, rl:rl + 1, :]
                dm_ref[l, part, 8:9, :] = row(nm, r)

    return _pcall(
        body, name=name,
        out_shape=[jax.ShapeDtypeStruct((8, d), F32), jax.ShapeDtypeStruct((8, d), F32),
                   jax.ShapeDtypeStruct((12, d), F32), jax.ShapeDtypeStruct((2, 6, 16, d), F32)],
    )(gath, lb_pad)


def _cctx_grad(gath, c_ctx2, *, name):
    def body(g_ref, c_ref, o_ref):
        tot = ((g_ref[0, 0:1, :] + g_ref[2, 0:1, :]) + g_ref[4, 0:1, :]) + g_ref[6, 0:1, :]
        cv = c_ref[...]
        s = _sigmoid(cv)
        o_ref[...] = tot * (s * (1.0 + cv * (1.0 - s)))

    return _pcall(body, name=name, out_shape=jax.ShapeDtypeStruct(c_ctx2.shape, F32))(gath, c_ctx2)


def _row_block(r, c, limit=256 * 1024):
    best = None
    for br in range(16, r + 1, 16):
        if r % br == 0 and br * c <= limit:
            best = br
    return best if best is not None else r


def _sum4(parts, *, name):
    _, r, c = parts.shape
    br = _row_block(r, c)

    def body(p_ref, o_ref):
        p = [p_ref[k].astype(F32) for k in range(4)]
        o_ref[...] = ((p[0] + p[1]) + p[2]) + p[3]

    return _pcall(body, name=name, grid=(r // br,),
                  in_specs=[pl.BlockSpec((4, br, c), lambda i: (0, i, 0))],
                  out_specs=pl.BlockSpec((br, c), lambda i: (i, 0)),
                  out_shape=jax.ShapeDtypeStruct((r, c), F32))(parts)


def _add2(a, b, *, name):
    r, c = a.shape
    br = _row_block(r, c)

    def body(a_ref, b_ref, o_ref):
        o_ref[...] = (a_ref[...].astype(F32) + b_ref[...].astype(F32)).astype(BF16)

    blk = pl.BlockSpec((br, c), lambda i: (i, 0))
    return _pcall(body, name=name, grid=(r // br,), in_specs=[blk, blk], out_specs=blk,
                  out_shape=jax.ShapeDtypeStruct((r, c), BF16))(a, b)


def _adam(w, gs, m, v, *, name):
    r, c = w.shape
    br = _row_block(r, c)
    ng = len(gs)
    c1 = 1.0 - ADAM_B1 ** ADAM_STEP
    c2 = 1.0 - ADAM_B2 ** ADAM_STEP

    def body(*refs):
        w_ref, m_ref, v_ref = refs[0], refs[1 + ng], refs[2 + ng]
        outs = refs[3 + ng:]
        g = refs[1][...]
        for k in range(1, ng):
            g = g + refs[1 + k][...]
        mn = ADAM_B1 * m_ref[...] + (1.0 - ADAM_B1) * g
        vn = ADAM_B2 * v_ref[...] + (1.0 - ADAM_B2) * (g * g)
        if ng > 1:
            outs[0][...] = g
        d_out, m_out, v_out = outs[-3:]
        m_out[...] = mn
        v_out[...] = vn
        d_out[...] = -ADAM_LR * ((mn / c1) / (jnp.sqrt(vn / c2) + ADAM_EPS) + ADAM_WD * w_ref[...])

    blk = pl.BlockSpec((br, c), lambda i: (i, 0))
    nout = 4 if ng > 1 else 3
    res = _pcall(body, name=name, grid=(r // br,), in_specs=[blk] * (3 + ng), out_specs=[blk] * nout,
                 out_shape=[jax.ShapeDtypeStruct((r, c), F32)] * nout)(w, *gs, m, v)
    return list(res) if ng > 1 else [gs[0]] + list(res)


def _grad_halves(name, g, ac):
    def chips(gl, order):
        n = gl.shape[1] // 4
        return jnp.stack([gl[:, b * n:(b + 1) * n] for b in order])

    if name == 'ffn_in':
        assert g[0].shape[1] == 4 * FFN_BK
        per = [chips(gl, _ffn_order(gl.shape[1])) for gl in g]
    elif name == 'ffn_out':
        per = [gl.reshape(4, gl.shape[0] // 4, gl.shape[1]) for gl in g]
    elif name in ('even_in', 'odd_in'):
        v = chips(g, range(4))
        per = [v[:, :g.shape[0] // 2], v[:, g.shape[0] // 2:]]
    else:
        k4, n = g.shape
        v = g.reshape(4, 2, k4 // 8, n).transpose(1, 0, 2, 3)
        per = [v[0], v[1]]
    first = ac == 0
    return _bf(jnp.where(first, per[0], per[1])), _bf(jnp.where(first, per[1], per[0]))


def _from_shards(name, g):
    _, r, n = g.shape
    if name == 'ffn_in':
        assert n == FFN_BK
        v = g.reshape(4, 2, r // 2, n)
        return jnp.concatenate([v[b] for b in _ffn_order(4 * n)], axis=-1)
    if name == 'ffn_out':
        return g.reshape(4, 2, r // 2, n).transpose(1, 0, 2, 3).reshape(2, 2 * r, n)
    if name in ('even_in', 'odd_in'):
        return jnp.concatenate([g[b] for b in range(4)], axis=-1)
    return g.reshape(4 * r, n)


def kernel(x, c, ctx, c_ctx, mod_w, mod_b, norm_g, ffn_w_in, ffn_w_out, even_w_in, even_w_out, attn_qk_norm_g, attn_sink, hgrn_out_norm_g, hgrn_lb, odd_w_in, odd_w_out, loss_target, m_c_ctx, m_mod_w, m_mod_b, m_norm_g, m_ffn_w_in, m_ffn_w_out, m_even_w_in, m_even_w_out, m_attn_qk_norm_g, m_attn_sink, m_hgrn_out_norm_g, m_hgrn_lb, m_odd_w_in, m_odd_w_out, v_c_ctx, v_mod_w, v_mod_b, v_norm_g, v_ffn_w_in, v_ffn_w_out, v_even_w_in, v_even_w_out, v_attn_qk_norm_g, v_attn_sink, v_hgrn_out_norm_g, v_hgrn_lb, v_odd_w_in, v_odd_w_out):
    d = x.shape[-1]
    lc = ctx.shape[1]
    assert lc == TM and d == 1024
    ax, ay, ac = _place()
    s = 2 * ax + ay
    me = 4 * ax + 2 * ay + ac
    nmod = mod_w.shape[2]

    names = ['ffn_in', 'ffn_out', 'even_in', 'even_out', 'odd_in', 'odd_out']
    shards = [_bf(v.reshape(-1, v.shape[-1])) for v in (ffn_w_in, ffn_w_out, even_w_in, even_w_out, odd_w_in, odd_w_out)]
    wsrc = _GatheredWeights(dict(zip(names, shards)))

    def pad8(v):
        return jnp.pad(v, ((0, 8 - v.shape[0]), (0, 0)))

    pack = jnp.concatenate([pad8(c), pad8(norm_g.reshape(1, d))], axis=0) + wsrc.token[0, 0]
    g1 = _ag8(pack, name='gather_cond')
    c_all = g1[:, 0, :]
    ng = g1[0::2, 8, :].reshape(4, 2, 2, d // 4).transpose(1, 2, 0, 3).reshape(4, d)

    cond_raw = jnp.concatenate([c_all, pad8(c_ctx.reshape(1, d))], axis=0)
    mb_sh = lax.dynamic_slice_in_dim(mod_b, s * nmod, nmod, axis=1).reshape(2, 1, nmod)
    mpart = _mod_fwd(cond_raw, mod_w, mb_sh, name='mod_fwd')
    g3 = _ag8(mpart.reshape(32, nmod), name='gather_mods')
    mods_full = g3[0::2].reshape(4, 2, 16, nmod).transpose(1, 2, 0, 3).reshape(2, 16, 4 * nmod)
    m_lat = lax.dynamic_index_in_dim(mods_full, me, axis=1, keepdims=False)
    mods = jnp.stack([mods_full[:, 8], m_lat], axis=1).reshape(24, d)

    lb = _lb_fwd(hgrn_lb, name='hgrn_lower_bound')
    small = dict(gq=jnp.tile(attn_qk_norm_g[0, 0], 2).reshape(1, 128), gk=jnp.tile(attn_qk_norm_g[0, 1], 2).reshape(1, 128),
                 sink=attn_sink[0], gain=hgrn_out_norm_g, lb=lb)
    x0 = jnp.concatenate([ctx[0], x[0]], axis=0)
    loss_t, dx0, grads, sums = _local_step(x0, loss_target[0], mods, ng, wsrc, small)
    loss = lax.psum(loss_t[0, 0], ("x", "y", "c"))
    grad_x = dx0[lc:][None]

    def tile(v):
        return jnp.pad(v, ((0, 8 - v.shape[0]), (0, d - v.shape[1])))

    sums = dict(sums, sink=sums['sink'][:, 0].reshape(1, 8))
    g4 = _ag8(jnp.concatenate([tile(sums[nm]) for nm in PACK_TILES], axis=0), name='gather_row_sums')
    small_g, glb, gmb, dmat = _small_finalize(g4, tile(lb)[0:1], name='small_grads')
    dms = lax.dynamic_slice_in_dim(dmat.transpose(0, 2, 1, 3).reshape(2, 16, 6 * d), s * nmod, nmod, axis=2)
    g_mod_w, dcond = _mod_bwd(cond_raw, dms, mod_w, name='mod_bwd')
    g5 = _ag8(dcond[8:16], name='gather_dcond')
    g_c_ctx = _cctx_grad(g5, c_ctx.reshape(8, d // 8).reshape(1, d), name='c_ctx_grad')

    halves = [_grad_halves(nm, grads[nm], ac) for nm in names]
    mine = [h[0] for h in halves]
    theirs = _to_sibling([h[1] for h in halves], name='swap_core_halves')
    pair = [_add2(a.reshape(-1, a.shape[-1]), b.reshape(-1, b.shape[-1]), name='add_cores').reshape(a.shape)
            for a, b in zip(mine, theirs)]
    parts = _chip_exchange(pair, scatter=True, name='scatter_grads')
    half_sums = [_sum4(p, name='sum_chips') for p in parts]
    other = _to_sibling(half_sums, name='gather_core_halves')
    full = [jnp.concatenate([jnp.where(ac == 0, f, o), jnp.where(ac == 0, o, f)], axis=0)
            for f, o in zip(half_sums, other)]

    def upd(wv, gs, mv, vv, name):
        shp = wv.shape
        c2 = shp[-1]
        out = _adam(wv.reshape(-1, c2), [g.reshape(-1, c2) for g in gs], mv.reshape(-1, c2), vv.reshape(-1, c2), name=name)
        return [o.reshape(shp) for o in out]

    res = {}
    res['c_ctx'] = upd(c_ctx.reshape(8, d // 8), [g_c_ctx.reshape(8, d // 8)], m_c_ctx.reshape(8, d // 8), v_c_ctx.reshape(8, d // 8), 'adam_c_ctx')
    res['c_ctx'] = [o.reshape(d) for o in res['c_ctx']]
    res['mod_w'] = upd(mod_w, [g_mod_w], m_mod_w, v_mod_w, 'adam_mod_w')
    res['mod_b'] = upd(mod_b, [gmb.reshape(2, 6 * d)], m_mod_b, v_mod_b, 'adam_mod_b')
    g_ng = lax.dynamic_slice_in_dim(small_g[0:4].reshape(2, 2, d), s * (d // 4), d // 4, axis=2)
    res['norm_g'] = upd(norm_g, [g_ng], m_norm_g, v_norm_g, 'adam_norm_g')
    big = {nm: [g] for nm, g in zip(names, full)}
    res['ffn_w_in'] = upd(ffn_w_in, big['ffn_in'], m_ffn_w_in, v_ffn_w_in, 'adam_ffn_in')
    res['ffn_w_out'] = upd(ffn_w_out, big['ffn_out'], m_ffn_w_out, v_ffn_w_out, 'adam_ffn_out')
    res['even_w_in'] = upd(even_w_in, big['even_in'], m_even_w_in, v_even_w_in, 'adam_even_in')
    res['even_w_out'] = upd(even_w_out, big['even_out'], m_even_w_out, v_even_w_out, 'adam_even_out')
    g_qk = jnp.stack([small_g[4, 0:64], small_g[5, 0:64]]).reshape(1, 2, 64)
    res['attn_qk_norm_g'] = upd(attn_qk_norm_g, [g_qk], m_attn_qk_norm_g, v_attn_qk_norm_g, 'adam_qk_gain')
    res['attn_sink'] = upd(attn_sink, [small_g[7, 0:8].reshape(1, 8)], m_attn_sink, v_attn_sink, 'adam_sink')
    res['hgrn_out_norm_g'] = upd(hgrn_out_norm_g, [small_g[6, 0:128].reshape(1, 128)], m_hgrn_out_norm_g, v_hgrn_out_norm_g, 'adam_head_gain')
    res['hgrn_lb'] = upd(hgrn_lb, [glb[0:2, 0:hgrn_lb.shape[1]]], m_hgrn_lb, v_hgrn_lb, 'adam_hgrn_lb')
    res['odd_w_in'] = upd(odd_w_in, big['odd_in'], m_odd_w_in, v_odd_w_in, 'adam_odd_in')
    res['odd_w_out'] = upd(odd_w_out, big['odd_out'], m_odd_w_out, v_odd_w_out, 'adam_odd_out')

    order = ['c_ctx', 'mod_w', 'mod_b', 'norm_g', 'ffn_w_in', 'ffn_w_out', 'even_w_in', 'even_w_out',
             'attn_qk_norm_g', 'attn_sink', 'hgrn_out_norm_g', 'hgrn_lb', 'odd_w_in', 'odd_w_out']
    outs = [loss, grad_x]
    for k in range(4):
        outs += [res[nm][k] for nm in order]
    return tuple(outs)
```

```python
import functools
import math

import numpy as np
import jax
import jax.numpy as jnp
from jax import lax
from jax.experimental import pallas as pl
from jax.experimental.pallas import tpu as pltpu

F32 = jnp.float32
BF16 = jnp.bfloat16
EPS = 1e-6
TM = 256
CHUNK = 64
QB = 128
WINDOW = 128
NEG = -1e30
MESH = pl.DeviceIdType.MESH

ADAM_LR, ADAM_B1, ADAM_B2, ADAM_EPS, ADAM_WD, ADAM_STEP = 0.001, 0.9, 0.999, 1e-08, 0.01, 10


def _pcall(body, **kw):
    return pl.pallas_call(body, **kw)


def _pick(n, cap):
    best = None
    for m in range(128, min(n, cap) + 1, 128):
        if n % m == 0:
            best = m
    assert best is not None, (n, cap)
    return best


def _bf(x):
    return x.astype(BF16)


def _dot(a, b):
    return jnp.dot(_bf(a), _bf(b), preferred_element_type=F32)


def _dot_nt(a, b):
    return lax.dot_general(_bf(a), _bf(b), (((1,), (1,)), ((), ())), preferred_element_type=F32)


def _dot_tn(a, b):
    return lax.dot_general(_bf(a), _bf(b), (((0,), (0,)), ((), ())), preferred_element_type=F32)


def _dot_exact(a, b):
    return jnp.dot(a, b, preferred_element_type=F32, precision=lax.Precision.HIGHEST)


def _sigmoid(x):
    return 1.0 / (1.0 + jnp.exp(-x))


def _iota(shape, dim):
    return lax.broadcasted_iota(jnp.int32, shape, dim)


def _mm_nn(a, b, *, lead=None, out_dtype=F32, name):
    m, k = a.shape
    n = b.shape[-1]
    bm = 768 if m % 768 == 0 else TM
    bn = _pick(n, 1024)

    def body(a_ref, b_ref, o_ref):
        o_ref[...] = _dot(a_ref[...], b_ref[...]).astype(o_ref.dtype)

    if lead is None:
        b_spec = pl.BlockSpec((k, bn), lambda i, j: (0, j))
    else:
        b_spec = pl.BlockSpec((None, k, bn), lambda i, j: (lead, 0, j))
    return _pcall(
        body, name=name, grid=(m // bm, n // bn),
        in_specs=[pl.BlockSpec((bm, k), lambda i, j: (i, 0)), b_spec],
        out_specs=pl.BlockSpec((bm, bn), lambda i, j: (i, j)),
        out_shape=jax.ShapeDtypeStruct((m, n), out_dtype),
    )(a, b)


def _mm_nt(a, b, *, lead=None, name):
    m, n = a.shape
    k = b.shape[-2]
    bm = 768 if m % 768 == 0 else TM
    bk = _pick(k, 512)

    def body(a_ref, b_ref, o_ref):
        o_ref[...] = _dot_nt(a_ref[...], b_ref[...])

    if lead is None:
        b_spec = pl.BlockSpec((bk, n), lambda i, j: (j, 0))
    else:
        b_spec = pl.BlockSpec((None, bk, n), lambda i, j: (lead, j, 0))
    return _pcall(
        body, name=name, grid=(m // bm, k // bk),
        in_specs=[pl.BlockSpec((bm, n), lambda i, j: (i, 0)), b_spec],
        out_specs=pl.BlockSpec((bm, bk), lambda i, j: (i, j)),
        out_shape=jax.ShapeDtypeStruct((m, k), F32),
    )(a, b)


def _mm_tn(a, b, *, name):
    t, k = a.shape
    n = b.shape[1]
    bt = 768 if t % 768 == 0 else TM
    bk = _pick(k, 1536)
    bn = _pick(n, 1024) if n % 1024 == 0 or n < 1664 else _pick(n, 1664)

    def body(a_ref, b_ref, o_ref):
        @pl.when(pl.program_id(2) == 0)
        def _():
            o_ref[...] = jnp.zeros_like(o_ref)
        o_ref[...] += _dot_tn(a_ref[...], b_ref[...])

    return _pcall(
        body, name=name, grid=(k // bk, n // bn, t // bt),
        in_specs=[pl.BlockSpec((bt, bk), lambda i, j, s: (s, i)),
                  pl.BlockSpec((bt, bn), lambda i, j, s: (s, j))],
        out_specs=pl.BlockSpec((bk, bn), lambda i, j, s: (i, j)),
        out_shape=jax.ShapeDtypeStruct((k, n), F32),
    )(a, b)


def _mod_row(mods_ref, lat, idx):
    return jnp.where(lat, mods_ref[idx + 6:idx + 7, :], mods_ref[idx:idx + 1, :])


def _row_fwd(x, mods, *, y=None, gate=None, g=None, shift=None, scale=None, name):
    t, d = x.shape
    has_y, has_n = y is not None, g is not None

    def body(*refs):
        refs = list(refs)
        x_ref, mods_ref = refs[0], refs[1]
        pos = 2
        if has_y:
            y_ref = refs[pos]; pos += 1
        if has_n:
            g_ref = refs[pos]; pos += 1
        outs = refs[pos:]
        lat = pl.program_id(0) > 0
        x1 = x_ref[...]
        o = 0
        if has_y:
            x1 = x1 + _mod_row(mods_ref, lat, gate) * y_ref[...]
            outs[o][...] = x1; o += 1
        if has_n:
            rs = lax.rsqrt(jnp.mean(x1 * x1, axis=-1, keepdims=True) + EPS)
            hn = x1 * rs * g_ref[...]
            h = hn * (1.0 + _mod_row(mods_ref, lat, scale)) + _mod_row(mods_ref, lat, shift)
            outs[o][...] = h.astype(BF16)

    row = pl.BlockSpec((TM, d), lambda i: (i, 0))
    ins, specs = [x, mods], [row, pl.BlockSpec(mods.shape, lambda i: (0, 0))]
    if has_y:
        ins.append(y); specs.append(row)
    if has_n:
        ins.append(g.reshape(1, d)); specs.append(pl.BlockSpec((1, d), lambda i: (0, 0)))
    out_shape, out_specs = [], []
    if has_y:
        out_shape.append(jax.ShapeDtypeStruct((t, d), F32)); out_specs.append(row)
    if has_n:
        out_shape.append(jax.ShapeDtypeStruct((t, d), BF16)); out_specs.append(row)
    res = _pcall(body, name=name, grid=(t // TM,), in_specs=specs, out_specs=out_specs,
                 out_shape=out_shape)(*ins)
    return res


def _acc_row(ref, r, val):
    ref[r:r + 1, :] += val


def _row_final(x, z, mods, target, *, gate, name):
    t, d = x.shape

    def body(x_ref, mods_ref, z_ref, t_ref, loss_ref, dx_ref, dz_ref, sums_ref):
        i = pl.program_id(0)
        lat = i > 0

        @pl.when(i == 0)
        def _():
            loss_ref[...] = jnp.zeros_like(loss_ref)
            sums_ref[...] = jnp.zeros_like(sums_ref)

        gt = _mod_row(mods_ref, lat, gate)
        zz = z_ref[...]
        yv = x_ref[...] + gt * zz
        keep = jnp.where(lat, 1.0, 0.0).astype(F32)
        diff = (yv - t_ref[...]) * keep
        part = jnp.sum(jnp.sum(diff * diff, axis=0, keepdims=True), axis=1, keepdims=True)
        loss_ref[...] += part * (0.5 / d)
        dy = diff * (1.0 / d)
        dx_ref[...] = dy
        dz_ref[...] = (gt * dy).astype(BF16)
        _acc_row(sums_ref, 6, jnp.sum(dy * zz, axis=0, keepdims=True))

    row = pl.BlockSpec((TM, d), lambda i: (i, 0))
    return _pcall(
        body, name=name, grid=(t // TM,),
        in_specs=[row, pl.BlockSpec(mods.shape, lambda i: (0, 0)), row,
                  pl.BlockSpec((TM, d), lambda i: (jnp.maximum(i - 1, 0), 0))],
        out_specs=[pl.BlockSpec((8, 128), lambda i: (0, 0)), row, row,
                   pl.BlockSpec((8, d), lambda i: (0, 0))],
        out_shape=[jax.ShapeDtypeStruct((8, 128), F32), jax.ShapeDtypeStruct((t, d), F32),
                   jax.ShapeDtypeStruct((t, d), BF16), jax.ShapeDtypeStruct((8, d), F32)],
    )(x, mods, z, target)


def _row_bwd(xn, dxo, dh, mods, g, *, shift, scale, y=None, gate=None, name):
    t, d = xn.shape
    has_y = y is not None

    def body(*refs):
        refs = list(refs)
        x_ref, dxo_ref, dh_ref, mods_ref, g_ref = refs[:5]
        pos = 5
        if has_y:
            y_ref = refs[pos]; pos += 1
        dx_ref = refs[pos]; pos += 1
        if has_y:
            dy_ref = refs[pos]; pos += 1
        sums_ref = refs[pos]
        i = pl.program_id(0)
        lat = i > 0

        @pl.when(i == 0)
        def _():
            sums_ref[...] = jnp.zeros_like(sums_ref)

        x1 = x_ref[...]
        gv = g_ref[...]
        rs = lax.rsqrt(jnp.mean(x1 * x1, axis=-1, keepdims=True) + EPS)
        xh = x1 * rs
        dhv = dh_ref[...]
        dn = dhv * (1.0 + _mod_row(mods_ref, lat, scale))
        dxh = dn * gv
        dx = dxo_ref[...] + rs * (dxh - xh * jnp.mean(dxh * xh, axis=-1, keepdims=True))
        dx_ref[...] = dx
        vals = [jnp.sum(dhv, axis=0, keepdims=True),
                jnp.sum(dhv * (xh * gv), axis=0, keepdims=True),
                None,
                jnp.sum(dn * xh, axis=0, keepdims=True)]
        if has_y:
            dy_ref[...] = (_mod_row(mods_ref, lat, gate) * dx).astype(BF16)
            vals[2] = jnp.sum(dx * y_ref[...], axis=0, keepdims=True)

        @pl.when(i == 0)
        def _():
            for r, v in enumerate(vals):
                if v is not None:
                    _acc_row(sums_ref, r, v)

        @pl.when(i > 0)
        def _():
            for r, v in enumerate(vals):
                if v is not None:
                    _acc_row(sums_ref, 4 + r, v)

    row = pl.BlockSpec((TM, d), lambda i: (i, 0))
    ins = [xn, dxo, dh, mods, g.reshape(1, d)]
    specs = [row, row, row, pl.BlockSpec(mods.shape, lambda i: (0, 0)), pl.BlockSpec((1, d), lambda i: (0, 0))]
    out_shape, out_specs = [jax.ShapeDtypeStruct((t, d), F32)], [row]
    if has_y:
        ins.append(y); specs.append(row)
        out_shape.append(jax.ShapeDtypeStruct((t, d), BF16)); out_specs.append(row)
    out_shape.append(jax.ShapeDtypeStruct((8, d), F32))
    out_specs.append(pl.BlockSpec((8, d), lambda i: (0, 0)))
    return _pcall(body, name=name, grid=(t // TM,), in_specs=specs, out_specs=out_specs,
                  out_shape=out_shape)(*ins)


FFN_BK = 1408


def _ffn_order(n2):
    nb = n2 // (2 * FFN_BK)
    return [h * nb + j for j in range(nb) for h in (0, 1)]


def _ffn_interleave(w):
    return jnp.concatenate([w[..., b * FFN_BK:(b + 1) * FFN_BK] for b in _ffn_order(w.shape[-1])], axis=-1)


def _ffn_deinterleave(w):
    order = _ffn_order(w.shape[-1])
    return jnp.concatenate([w[..., order.index(b) * FFN_BK:(order.index(b) + 1) * FFN_BK]
                            for b in range(len(order))], axis=-1)


def _big_tile(t):
    return 384 if t % 384 == 0 else TM


def _ffn_in(h, w, *, lead, name):
    t, d = h.shape
    n2 = w.shape[-1]
    bm, bk = _big_tile(t), FFN_BK

    def body(h_ref, w_ref, u_ref, a_ref):
        ub = _dot(h_ref[...], w_ref[...]).astype(BF16)
        u_ref[...] = ub
        uf = ub.astype(F32)
        gv, up = uf[:, 0:bk], uf[:, bk:2 * bk]
        a_ref[...] = (gv * _sigmoid(gv) * up).astype(BF16)

    return _pcall(
        body, name=name, grid=(t // bm, n2 // (2 * bk)),
        in_specs=[pl.BlockSpec((bm, d), lambda i, j: (i, 0)),
                  pl.BlockSpec((None, d, 2 * bk), lambda i, j: (lead, 0, j))],
        out_specs=[pl.BlockSpec((bm, 2 * bk), lambda i, j: (i, j)), pl.BlockSpec((bm, bk), lambda i, j: (i, j))],
        out_shape=[jax.ShapeDtypeStruct((t, n2), BF16), jax.ShapeDtypeStruct((t, n2 // 2), BF16)],
    )(h, w)


def _ffn_dx(dz, w_out, u, *, lead, name):
    t, d = dz.shape
    n2 = u.shape[1]
    bm, bk = _big_tile(t), FFN_BK

    def body(dz_ref, w_ref, u_ref, du_ref):
        da = _dot_nt(dz_ref[...], w_ref[...])
        uf = u_ref[...].astype(F32)
        gv, up = uf[:, 0:bk], uf[:, bk:2 * bk]
        s = _sigmoid(gv)
        du_ref[:, 0:bk] = (da * up * (s * (1.0 + gv * (1.0 - s)))).astype(BF16)
        du_ref[:, bk:2 * bk] = (da * gv * s).astype(BF16)

    ublk = pl.BlockSpec((bm, 2 * bk), lambda i, j: (i, j))
    return _pcall(
        body, name=name, grid=(t // bm, n2 // (2 * bk)),
        in_specs=[pl.BlockSpec((bm, d), lambda i, j: (i, 0)),
                  pl.BlockSpec((None, bk, d), lambda i, j: (lead, j, 0)), ublk],
        out_specs=ublk, out_shape=jax.ShapeDtypeStruct((t, n2), BF16),
    )(dz, w_out, u)


def _lane(shape):
    return _iota(shape, len(shape) - 1)


def _pair_norm(x, g):
    lo = _lane(x.shape) < 64
    x2 = x * x
    s_lo = jnp.sum(jnp.where(lo, x2, 0.0), axis=-1, keepdims=True)
    s_hi = jnp.sum(jnp.where(lo, 0.0, x2), axis=-1, keepdims=True)
    rs = lax.rsqrt(jnp.where(lo, s_lo, s_hi) * (1.0 / 64) + EPS)
    return x * rs, rs


def _pair_mean(v):
    lo = _lane(v.shape) < 64
    s_lo = jnp.sum(jnp.where(lo, v, 0.0), axis=-1, keepdims=True)
    s_hi = jnp.sum(jnp.where(lo, 0.0, v), axis=-1, keepdims=True)
    return jnp.where(lo, s_lo, s_hi) * (1.0 / 64)


def _rot64(x):
    r1 = pltpu.roll(x, 32, 1)
    r2 = pltpu.roll(x, 96, 1)
    even = ((_lane(x.shape) >> 5) & 1) == 0
    return jnp.where(even, -r2, r1)


def _rope64(x, cos, sin):
    return x * cos + _rot64(x) * sin


def _rope64_t(d, cos, sin):
    return d * cos - _rot64(d * sin)


def _kprep_fwd(p, gk, cos, sin, *, name):
    t = p.shape[0]

    def body(k_ref, g_ref, c_ref, s_ref, o_ref):
        xh, _ = _pair_norm(k_ref[...], None)
        o_ref[...] = _rope64(xh * g_ref[...], c_ref[...], s_ref[...])

    blk = pl.BlockSpec((TM, 128), lambda i: (i, 0))
    return _pcall(
        body, name=name, grid=(t // TM,),
        in_specs=[pl.BlockSpec((TM, 128), lambda i: (i, 4)), pl.BlockSpec((1, 128), lambda i: (0, 0)), blk, blk],
        out_specs=blk, out_shape=jax.ShapeDtypeStruct((t, 128), F32),
    )(p, gk, cos, sin)


def _kprep_bwd(p, gk, cos, sin, dkp, dv, *, name):
    t = p.shape[0]

    def body(k_ref, g_ref, c_ref, s_ref, dkp_ref, dv_ref, o_ref, dg_ref):
        @pl.when(pl.program_id(0) == 0)
        def _():
            dg_ref[...] = jnp.zeros_like(dg_ref)
        xh, rs = _pair_norm(k_ref[...], None)
        dn = _rope64_t(dkp_ref[...], c_ref[...], s_ref[...])
        _acc_row(dg_ref, 0, jnp.sum(dn * xh, axis=0, keepdims=True))
        dxh = dn * g_ref[...]
        o_ref[:, 0:128] = (rs * (dxh - xh * _pair_mean(dxh * xh))).astype(BF16)
        o_ref[:, 128:256] = dv_ref[...].astype(BF16)

    blk = pl.BlockSpec((TM, 128), lambda i: (i, 0))
    return _pcall(
        body, name=name, grid=(t // TM,),
        in_specs=[pl.BlockSpec((TM, 128), lambda i: (i, 4)), pl.BlockSpec((1, 128), lambda i: (0, 0)), blk, blk, blk, blk],
        out_specs=[pl.BlockSpec((TM, 256), lambda i: (i, 0)), pl.BlockSpec((8, 128), lambda i: (0, 0))],
        out_shape=[jax.ShapeDtypeStruct((t, 256), BF16), jax.ShapeDtypeStruct((8, 128), F32)],
    )(p, gk, cos, sin, dkp, dv)


def _attn_common(i, t, lc, kp_ref, v_ref):
    span = QB + 2 * WINDOW
    start = pl.multiple_of(jnp.clip((i - 1) * QB, lc, t - span), QB)
    kall = jnp.concatenate([kp_ref[0:lc, :], kp_ref[pl.ds(start, span), :]], axis=0)
    vall = jnp.concatenate([v_ref[0:lc, :], v_ref[pl.ds(start, span), :]], axis=0)
    nk = lc + span
    col = _iota((QB, nk), 1)
    krow = jnp.where(col < lc, col, start + col - lc)
    qrow = i * QB + _iota((QB, nk), 0)
    valid = (col < lc) | ((qrow >= lc) & (krow >= lc) & (jnp.abs(krow - qrow) <= WINDOW))
    lo = _lane(kall.shape) < 64
    kroll, vroll = pltpu.roll(kall, 64, 1), pltpu.roll(vall, 64, 1)
    zero = jnp.zeros_like(kall)
    kvar = [[_bf(jnp.where(lo, kall, zero)), _bf(jnp.where(lo, zero, kroll))],
            [_bf(jnp.where(lo, kroll, zero)), _bf(jnp.where(lo, zero, kall))]]
    vvar = [[_bf(jnp.where(lo, vall, zero)), _bf(jnp.where(lo, zero, vroll))],
            [_bf(jnp.where(lo, vroll, zero)), _bf(jnp.where(lo, zero, vall))]]
    return start, valid, kvar, vvar


def _softmax_sink(s, valid, snk):
    s = jnp.where(valid, s, NEG)
    m = jnp.maximum(jnp.max(s, axis=-1, keepdims=True), snk)
    e = jnp.exp(s - m)
    es = jnp.exp(snk - m)
    inv = 1.0 / (jnp.sum(e, axis=-1, keepdims=True) + es)
    return e * inv, es * inv


def _attn_fwd(p, kp, gq, sink, cos, sin, *, lc, name):
    t = p.shape[0]
    scale = 64 ** -0.5

    def body(q_ref, kp_ref, v_ref, g_ref, sink_ref, c_ref, s_ref, o_ref):
        i = pl.program_id(0)
        _, valid, kvar, vvar = _attn_common(i, t, lc, kp_ref, v_ref)
        cosv, sinv, gv = c_ref[...], s_ref[...], g_ref[...]
        for j in range(4):
            xh, _ = _pair_norm(q_ref[:, 128 * j:128 * j + 128], None)
            q2 = _bf(_rope64(xh * gv, cosv, sinv))
            acc = jnp.zeros((QB, 128), F32)
            for half in range(2):
                s = _dot_nt(q2, kvar[j // 2][half]) * scale
                pr, _ = _softmax_sink(s, valid, sink_ref[2 * j + half])
                acc = acc + _dot(pr, vvar[j // 2][half])
            o_ref[:, 128 * j:128 * j + 128] = acc.astype(BF16)

    qblk = pl.BlockSpec((QB, 128), lambda i: (i, 0))
    return _pcall(
        body, name=name, grid=(t // QB,),
        in_specs=[pl.BlockSpec((QB, 512), lambda i: (i, 0)),
                  pl.BlockSpec((t, 128), lambda i: (0, 0)),
                  pl.BlockSpec((t, 128), lambda i: (0, 5)),
                  pl.BlockSpec((1, 128), lambda i: (0, 0)),
                  pl.BlockSpec(memory_space=pltpu.SMEM), qblk, qblk],
        out_specs=pl.BlockSpec((QB, 512), lambda i: (i, 0)),
        out_shape=jax.ShapeDtypeStruct((t, 512), BF16),
    )(p, kp, p, gq, sink, cos, sin)


def _attn_bwd(p, kp, gq, sink, cos, sin, dmix, *, lc, name):
    t = p.shape[0]
    scale = 64 ** -0.5
    span = QB + 2 * WINDOW

    def body(q_ref, kp_ref, v_ref, g_ref, sink_ref, c_ref, s_ref, do_ref,
             dq_ref, dk_ref, dv_ref, dg_ref, dsink_ref):
        i = pl.program_id(0)

        @pl.when(i == 0)
        def _():
            dk_ref[...] = jnp.zeros_like(dk_ref)
            dv_ref[...] = jnp.zeros_like(dv_ref)
            dg_ref[...] = jnp.zeros_like(dg_ref)
            dsink_ref[...] = jnp.zeros_like(dsink_ref)

        start, valid, kvar, vvar = _attn_common(i, t, lc, kp_ref, v_ref)
        cosv, sinv, gv = c_ref[...], s_ref[...], g_ref[...]
        nk = lc + span
        lo = _lane((nk, 128)) < 64
        dk_all = jnp.zeros((nk, 128), F32)
        dv_all = jnp.zeros((nk, 128), F32)
        for j in range(4):
            kvh = j // 2
            xh, rs = _pair_norm(q_ref[:, 128 * j:128 * j + 128], None)
            q2 = _bf(_rope64(xh * gv, cosv, sinv))
            do2 = _bf(do_ref[:, 128 * j:128 * j + 128])
            dq2 = jnp.zeros((QB, 128), F32)
            for half in range(2):
                s = _dot_nt(q2, kvar[kvh][half]) * scale
                pr, ps = _softmax_sink(s, valid, sink_ref[2 * j + half])
                dp = _dot_nt(do2, vvar[kvh][half])
                delta = jnp.sum(pr * dp, axis=-1, keepdims=True)
                ds = pr * (dp - delta) * scale
                dsk = jnp.sum(jnp.sum(-ps * delta, axis=0, keepdims=True), axis=1, keepdims=True)
                _acc_row(dsink_ref, 2 * j + half, jnp.broadcast_to(dsk, (1, 128)))
                dq2 = dq2 + _dot(ds, kvar[kvh][half])
                gk_ = _dot_tn(ds, q2)
                gv_ = _dot_tn(pr, do2)
                if half == 0:
                    gk_, gv_ = jnp.where(lo, gk_, 0.0), jnp.where(lo, gv_, 0.0)
                else:
                    gk_, gv_ = jnp.where(lo, 0.0, gk_), jnp.where(lo, 0.0, gv_)
                if half != kvh:
                    gk_, gv_ = pltpu.roll(gk_, 64, 1), pltpu.roll(gv_, 64, 1)
                dk_all = dk_all + gk_
                dv_all = dv_all + gv_
            dn = _rope64_t(dq2, cosv, sinv)
            _acc_row(dg_ref, 0, jnp.sum(dn * xh, axis=0, keepdims=True))
            dxh = dn * gv
            dq_ref[:, 128 * j:128 * j + 128] = (rs * (dxh - xh * _pair_mean(dxh * xh))).astype(BF16)
        dk_ref[0:lc, :] += dk_all[0:lc]
        dv_ref[0:lc, :] += dv_all[0:lc]
        dk_ref[pl.ds(start, span), :] += dk_all[lc:nk]
        dv_ref[pl.ds(start, span), :] += dv_all[lc:nk]

    qblk = pl.BlockSpec((QB, 128), lambda i: (i, 0))
    full = pl.BlockSpec((t, 128), lambda i: (0, 0))
    small = pl.BlockSpec((8, 128), lambda i: (0, 0))
    return _pcall(
        body, name=name, grid=(t // QB,),
        in_specs=[pl.BlockSpec((QB, 512), lambda i: (i, 0)), full,
                  pl.BlockSpec((t, 128), lambda i: (0, 5)),
                  pl.BlockSpec((1, 128), lambda i: (0, 0)),
                  pl.BlockSpec(memory_space=pltpu.SMEM), qblk, qblk,
                  pl.BlockSpec((QB, 512), lambda i: (i, 0))],
        out_specs=[pl.BlockSpec((QB, 512), lambda i: (i, 0)), full, full, small, small],
        out_shape=[jax.ShapeDtypeStruct((t, 512), BF16), jax.ShapeDtypeStruct((t, 128), F32),
                   jax.ShapeDtypeStruct((t, 128), F32), jax.ShapeDtypeStruct((8, 128), F32),
                   jax.ShapeDtypeStruct((8, 128), F32)],
    )(p, kp, p, gq, sink, cos, sin, dmix)


def _tri(rev):
    r, c = _iota((CHUNK, CHUNK), 0), _iota((CHUNK, CHUNK), 1)
    return (c >= r) if rev else (c <= r)


def _blk_map(nb, rev, backward):
    if not rev:
        return (lambda n: nb - 1 - n) if backward else (lambda n: n)
    if backward:
        return lambda n: jnp.where(n < nb - 1, n + 1, 0)
    return lambda n: jnp.where(n == 0, 0, nb - n)


def _chunk_order(rev, backward, nc=TM // CHUNK):
    order = list(range(nc))
    return order[::-1] if (rev != backward) else order


def _hgrn_gates(qraw, fraw, lb):
    sq = _sigmoid(qraw)
    sf = _sigmoid(fraw)
    f = lb + (1.0 - lb) * sf
    return qraw * sq, 1.0 - f, jnp.log(f), sq, sf, f


HGRN_HP = 2


def _chunk_cumsum(x, rev):
    n = x.shape[0]
    pos = _iota(x.shape, 0) & (CHUNK - 1)
    s = 1
    while s < CHUNK:
        if rev:
            x = x + jnp.where(pos < CHUNK - s, pltpu.roll(x, n - s, 0), 0.0)
        else:
            x = x + jnp.where(pos >= s, pltpu.roll(x, s, 0), 0.0)
        s *= 2
    return x


def _block_terms(lf, rev):
    b = _chunk_cumsum(lf, rev)
    mid, last = (CHUNK // 2 - 1, 0) if rev else (CHUNK // 2, CHUNK - 1)

    def chunk_row(off):
        return jnp.concatenate([jnp.broadcast_to(b[c * CHUNK + off:c * CHUNK + off + 1, :], (CHUNK, b.shape[1]))
                                for c in range(TM // CHUNK)], axis=0)

    r, bl = chunk_row(mid), chunk_row(last)
    return _tri(rev), jnp.exp(b - r), jnp.exp(r - b), jnp.exp(b), jnp.exp(bl - b), jnp.exp(bl)


def _headnorm_apply(o, gv, gain):
    n = o * lax.rsqrt(jnp.mean(o * o, axis=-1, keepdims=True) + EPS)
    if gain is not None:
        n = n * gain
    return (n * (gv * _sigmoid(gv))).astype(BF16)


def _headnorm_grad(o, gv, dy, gain):
    rs = lax.rsqrt(jnp.mean(o * o, axis=-1, keepdims=True) + EPS)
    xh = o * rs
    n = xh * gain if gain is not None else xh
    sg = _sigmoid(gv)
    dn = dy * (gv * sg)
    dg = (dy * n * (sg * (1.0 + gv * (1.0 - sg)))).astype(BF16)
    dgain = jnp.sum(dn * xh, axis=0, keepdims=True)
    dxh = dn * gain if gain is not None else dn
    return rs * (dxh - xh * jnp.mean(dxh * xh, axis=-1, keepdims=True)), dg, dgain


def _hgrn_fwd(p, lb, *, rev, name, ofw=None, gain=None):
    t = p.shape[0]
    nb, nc = t // TM, TM // CHUNK
    bmap = _blk_map(nb, rev, False)
    fcol = 14 if rev else 10
    fused = ofw is not None

    def body(*refs):
        q_ref, f_ref, v_ref, lb_ref = refs[:4]
        if fused:
            ofw_ref, g_ref, gain_ref, o_ref, sh_ref, mix_ref, st = refs[4:]
        else:
            o_ref, sh_ref, st = refs[4:]

        @pl.when(pl.program_id(1) == 0)
        def _():
            st[...] = jnp.zeros_like(st)
        for hh in range(HGRN_HP):
            ln = slice(128 * hh, 128 * hh + 128)
            q, k, lf, _, _, _ = _hgrn_gates(q_ref[:, ln], f_ref[:, ln], lb_ref[:, ln])
            tri, eq, ek, ei, eki, eb = _block_terms(lf, rev)
            qe, ke, qi, ki, vb = _bf(q * eq), _bf(k * ek), _bf(q * ei), _bf(k * eki), _bf(v_ref[:, ln])
            intra = []
            for cc in range(nc):
                rows = slice(cc * CHUNK, (cc + 1) * CHUNK)
                a = jnp.where(tri, _dot_nt(qe[rows], ke[rows]), 0.0)
                intra.append(_dot(a, vb[rows]))
            s = st[hh]
            for cc in _chunk_order(rev, False):
                rows = slice(cc * CHUNK, (cc + 1) * CHUNK)
                sh_ref[hh, cc] = s
                o_ref[rows, ln] = intra[cc] + _dot_nt(qi[rows], s)
                s = s * eb[cc * CHUNK:cc * CHUNK + 1, :] + _dot_tn(vb[rows], ki[rows])
            st[hh] = s
            if fused:
                osum = o_ref[:, ln] + ofw_ref[:, ln]
                o_ref[:, ln] = osum
                mix_ref[:, ln] = _headnorm_apply(osum, g_ref[:, ln], gain_ref[...])

    hp, wd = HGRN_HP, 128 * HGRN_HP

    def col(c0):
        return pl.BlockSpec((TM, wd), lambda h, n: (bmap(n), c0 // hp + h))

    oblk = pl.BlockSpec((TM, wd), lambda h, n: (bmap(n), h))
    ins, specs = [p, p, p, lb], [col(6), col(fcol), col(18), pl.BlockSpec((1, wd), lambda h, n: (0, h))]
    out_specs = [oblk, pl.BlockSpec((hp, nc, 128, 128), lambda h, n: (h, bmap(n), 0, 0))]
    out_shape = [jax.ShapeDtypeStruct((t, 512), F32), jax.ShapeDtypeStruct((4, t // CHUNK, 128, 128), F32)]
    if fused:
        ins += [ofw, p, gain]
        specs += [oblk, col(22), pl.BlockSpec((1, 128), lambda h, n: (0, 0))]
        out_specs.append(oblk)
        out_shape.append(jax.ShapeDtypeStruct((t, 512), BF16))
    return _pcall(body, name=name, grid=(4 // hp, nb), in_specs=specs, out_specs=out_specs, out_shape=out_shape,
                  scratch_shapes=[pltpu.VMEM((hp, 128, 128), F32)])(*ins)


def _hgrn_bwd(p, lb, sh, do, prev, *, rev, name, head=None):
    t = p.shape[0]
    nb, nc = t // TM, TM // CHUNK
    bmap = _blk_map(nb, rev, True)
    fcol = 14 if rev else 10
    has_prev = prev is not None
    odt = BF16 if has_prev else F32
    fused = head is not None

    def body(*refs):
        refs = list(refs)
        q_ref, f_ref, v_ref, lb_ref, sh_ref = refs[:5]
        pos = 5
        if fused:
            osum_ref, g_ref, dmix_ref, gain_ref = refs[5:9]
            pos = 9
        else:
            do_ref = refs[5]
            pos = 6
        if has_prev:
            pq_ref, pv_ref = refs[pos], refs[pos + 1]
            pos += 2
        dq_ref, df_ref, dv_ref, dlb_ref = refs[pos:pos + 4]
        pos += 4
        if fused:
            do_out, dg_ref, dgain_ref = refs[pos:pos + 3]
            pos += 3
        dst = refs[pos]

        @pl.when(pl.program_id(1) == 0)
        def _():
            dst[...] = jnp.zeros_like(dst)
            dlb_ref[...] = jnp.zeros_like(dlb_ref)

        if fused:
            @pl.when((pl.program_id(0) == 0) & (pl.program_id(1) == 0))
            def _():
                dgain_ref[...] = jnp.zeros_like(dgain_ref)

        cat = functools.partial(jnp.concatenate, axis=0)
        for hh in range(HGRN_HP):
            ln = slice(128 * hh, 128 * hh + 128)
            lbv = lb_ref[:, ln]
            qraw, fraw = q_ref[:, ln], f_ref[:, ln]
            q, k, lf, sq, sf, f = _hgrn_gates(qraw, fraw, lbv)
            tri, eq, ek, ei, eki, eb = _block_terms(lf, rev)
            qe, ke, qi, ki = q * eq, k * ek, q * ei, k * eki
            if fused:
                dov, dg, dgain = _headnorm_grad(osum_ref[:, ln], g_ref[:, ln], dmix_ref[:, ln], gain_ref[...])
                do_out[:, ln] = dov
                dg_ref[:, ln] = dg
                _acc_row(dgain_ref, 0, dgain)
            else:
                dov = do_ref[:, ln]
            qeb, keb, qib, kib, vb, dob = _bf(qe), _bf(ke), _bf(qi), _bf(ki), _bf(v_ref[:, ln]), _bf(dov)
            dv, dqe, dke, dqi = [None] * nc, [None] * nc, [None] * nc, [None] * nc
            for cc in range(nc):
                rows = slice(cc * CHUNK, (cc + 1) * CHUNK)
                a = jnp.where(tri, _dot_nt(qeb[rows], keb[rows]), 0.0)
                da = jnp.where(tri, _dot_nt(dob[rows], vb[rows]), 0.0)
                dv[cc] = _dot_tn(a, dob[rows])
                dqe[cc], dke[cc] = _dot(da, keb[rows]), _dot_tn(da, qeb[rows])
                dqi[cc] = _dot(dob[rows], sh_ref[hh, cc])
            dki, dbl = [None] * nc, [None] * nc
            ds = dst[hh]
            for cc in _chunk_order(rev, True):
                rows = slice(cc * CHUNK, (cc + 1) * CHUNK)
                ebc = eb[cc * CHUNK:cc * CHUNK + 1, :]
                dv[cc] = dv[cc] + _dot_nt(kib[rows], ds)
                dki[cc] = _dot(vb[rows], ds)
                dbl[cc] = jnp.broadcast_to(jnp.sum(dki[cc] * ki[rows], axis=0, keepdims=True)
                                           + jnp.sum(ds * sh_ref[hh, cc], axis=0, keepdims=True) * ebc, (CHUNK, 128))
                ds = ds * ebc + _dot_tn(dob[rows], qib[rows])
            dst[hh] = ds
            dqe, dke, dqi, dki, dv, dbl = cat(dqe), cat(dke), cat(dqi), cat(dki), cat(dv), cat(dbl)
            dq = dqe * eq + dqi * ei
            dk = dke * ek + dki * eki
            last = 0 if rev else CHUNK - 1
            db = dqe * qe - dke * ke + dqi * qi - dki * ki
            db = db + jnp.where((_iota(db.shape, 0) & (CHUNK - 1)) == last, dbl, 0.0)
            dlf = _chunk_cumsum(db, not rev)
            dqr = dq * (sq * (1.0 + qraw * (1.0 - sq)))
            dfv = dlf / f - dk
            dfr = dfv * (1.0 - lbv) * (sf * (1.0 - sf))
            dlb_ref[:, ln] += jnp.sum(dfv * (1.0 - sf), axis=0, keepdims=True)
            if has_prev:
                dqr = dqr + pq_ref[:, ln]
                dv = dv + pv_ref[:, ln]
            dq_ref[:, ln] = dqr.astype(odt)
            df_ref[:, ln] = dfr.astype(odt)
            dv_ref[:, ln] = dv.astype(odt)

    hp, wd = HGRN_HP, 128 * HGRN_HP

    def col(c0):
        return pl.BlockSpec((TM, wd), lambda h, n: (bmap(n), c0 // hp + h))

    oblk = pl.BlockSpec((TM, wd), lambda h, n: (bmap(n), h))
    ins = [p, p, p, lb, sh]
    specs = [col(6), col(fcol), col(18), pl.BlockSpec((1, wd), lambda h, n: (0, h)),
             pl.BlockSpec((hp, nc, 128, 128), lambda h, n: (h, bmap(n), 0, 0))]
    if fused:
        osum, dmix, gain = head
        ins += [osum, p, dmix, gain]
        specs += [oblk, col(22), pl.BlockSpec((TM, wd), lambda h, n: (bmap(n), 4 // hp + h)),
                  pl.BlockSpec((1, 128), lambda h, n: (0, 0))]
    else:
        ins.append(do); specs.append(oblk)
    if has_prev:
        ins += list(prev); specs += [oblk, oblk]
    out_specs = [oblk, oblk, oblk, pl.BlockSpec((1, wd), lambda h, n: (0, h))]
    out_shape = [jax.ShapeDtypeStruct((t, 512), odt)] * 3 + [jax.ShapeDtypeStruct((1, 512), F32)]
    if fused:
        out_specs += [oblk, oblk, pl.BlockSpec((8, 128), lambda h, n: (0, 0))]
        out_shape += [jax.ShapeDtypeStruct((t, 512), F32), jax.ShapeDtypeStruct((t, 512), BF16),
                      jax.ShapeDtypeStruct((8, 128), F32)]
    return _pcall(body, name=name, grid=(4 // hp, nb), in_specs=specs, out_specs=out_specs, out_shape=out_shape,
                  scratch_shapes=[pltpu.VMEM((hp, 128, 128), F32)])(*ins)


def _rope256(x, cos, sin):
    x1, x2 = x[:, 0:128], x[:, 128:256]
    return jnp.concatenate([x1 * cos - x2 * sin, x2 * cos + x1 * sin], axis=-1)


def _rope256_t(d, cos, sin):
    d1, d2 = d[:, 0:128], d[:, 128:256]
    return jnp.concatenate([d1 * cos + d2 * sin, d2 * cos - d1 * sin], axis=-1)


RET_DK, RET_DV, RET_H = 256, 512, 4
RET_KSCALE = RET_DK ** -0.5
RCH = TM
RET_HP = 2


def _ret_terms(lg, rev):
    r, c = _iota((RCH, RCH), 0), _iota((RCH, RCH), 1)
    rel = ((c - r) if rev else (r - c)).astype(F32)
    dmat = jnp.where(rel >= 0, jnp.exp(lg[:, 0:1] * jnp.maximum(rel, 0.0)), 0.0)
    pos = _iota((RCH, 1), 0).astype(F32)
    cnt = (RCH - pos) if rev else (pos + 1.0)
    ei = jnp.exp(lg * cnt)
    eki = jnp.exp(lg * (RCH - cnt))
    eb = jnp.exp(lg * float(RCH))
    return dmat, ei, eki, eb


def _ret_fwd(p, lgt, cos, sin, *, rev, name, ofw=None):
    t = p.shape[0]
    nb, nc = t // TM, TM // RCH
    bmap = _blk_map(nb, rev, False)
    fused = ofw is not None

    def body(*refs):
        q_ref, k_ref, v_ref, lg_ref, c_ref, s_ref = refs[:6]
        if fused:
            ofw_ref, g_ref, o_ref, sh_ref, mix_ref, st = refs[6:]
        else:
            o_ref, sh_ref, st = refs[6:]

        @pl.when(pl.program_id(1) == 0)
        def _():
            st[...] = jnp.zeros_like(st)
        for hh in range(RET_HP):
            qc, vc = slice(RET_DK * hh, RET_DK * (hh + 1)), slice(RET_DV * hh, RET_DV * (hh + 1))
            dmat, ei, eki, eb = _ret_terms(lg_ref[hh], rev)
            for cc in _chunk_order(rev, False, nc):
                rows = slice(cc * RCH, (cc + 1) * RCH)
                cosv, sinv = c_ref[rows, :], s_ref[rows, :]
                q = _rope256(q_ref[rows, qc].astype(F32), cosv, sinv)
                k = _rope256(k_ref[rows, qc].astype(F32), cosv, sinv) * RET_KSCALE
                v = v_ref[rows, vc]
                s0 = st[hh]
                sh_ref[hh, cc] = s0.astype(BF16)
                a = _dot_nt(q, k) * dmat
                o = _dot(a, v) + _dot_nt(q * ei, s0)
                st[hh] = s0 * eb + _dot_tn(v, k * eki)
                if fused:
                    o = o + ofw_ref[rows, vc]
                    mix_ref[rows, vc] = _headnorm_apply(o, g_ref[rows, vc].astype(F32), None)
                o_ref[rows, vc] = o

    hp = RET_HP
    tab = pl.BlockSpec((TM, 128), lambda h, n: (bmap(n), 0))
    oblk = pl.BlockSpec((TM, hp * RET_DV), lambda h, n: (bmap(n), h))
    ins = [p, p, p, lgt, cos, sin]
    specs = [pl.BlockSpec((TM, hp * RET_DK), lambda h, n: (bmap(n), h)),
             pl.BlockSpec((TM, hp * RET_DK), lambda h, n: (bmap(n), RET_H // hp + h)),
             pl.BlockSpec((TM, hp * RET_DV), lambda h, n: (bmap(n), RET_H // hp + h)),
             pl.BlockSpec((hp, 1, RET_DK), lambda h, n: (h, 0, 0)), tab, tab]
    out_specs = [oblk, pl.BlockSpec((hp, nc, RET_DV, RET_DK), lambda h, n: (h, bmap(n), 0, 0))]
    out_shape = [jax.ShapeDtypeStruct((t, RET_H * RET_DV), F32),
                 jax.ShapeDtypeStruct((RET_H, t // RCH, RET_DV, RET_DK), BF16)]
    if fused:
        ins += [ofw, p]
        specs += [oblk, pl.BlockSpec((TM, hp * RET_DV), lambda h, n: (bmap(n), 2 * RET_H // hp + h))]
        out_specs.append(oblk)
        out_shape.append(jax.ShapeDtypeStruct((t, RET_H * RET_DV), BF16))
    return _pcall(body, name=name, grid=(RET_H // hp, nb), in_specs=specs, out_specs=out_specs, out_shape=out_shape,
                  scratch_shapes=[pltpu.VMEM((hp, RET_DV, RET_DK), F32)])(*ins)


def _ret_bwd(p, lgt, cos, sin, sh, do, prev, *, rev, name, head=None):
    t = p.shape[0]
    nb, nc = t // TM, TM // RCH
    bmap = _blk_map(nb, rev, True)
    has_prev = prev is not None
    odt = BF16 if has_prev else F32
    fused = head is not None

    def body(*refs):
        refs = list(refs)
        q_ref, k_ref, v_ref, lg_ref, c_ref, s_ref, sh_ref = refs[:7]
        if fused:
            osum_ref, g_ref, dmix_ref = refs[7:10]
            pos = 10
        else:
            do_ref = refs[7]
            pos = 8
        if has_prev:
            pq_ref, pk_ref, pv_ref = refs[pos:pos + 3]
            pos += 3
        dq_ref, dk_ref, dv_ref = refs[pos:pos + 3]
        pos += 3
        if fused:
            do_out, dg_ref = refs[pos:pos + 2]
            pos += 2
        dst = refs[pos]

        @pl.when(pl.program_id(1) == 0)
        def _():
            dst[...] = jnp.zeros_like(dst)

        for hh in range(RET_HP):
            qc, vc = slice(RET_DK * hh, RET_DK * (hh + 1)), slice(RET_DV * hh, RET_DV * (hh + 1))
            dmat, ei, eki, eb = _ret_terms(lg_ref[hh], rev)
            for cc in _chunk_order(rev, True, nc):
                rows = slice(cc * RCH, (cc + 1) * RCH)
                cosv, sinv = c_ref[rows, :], s_ref[rows, :]
                q = _rope256(q_ref[rows, qc].astype(F32), cosv, sinv)
                k = _rope256(k_ref[rows, qc].astype(F32), cosv, sinv) * RET_KSCALE
                v = v_ref[rows, vc]
                if fused:
                    dov, dg, _ = _headnorm_grad(osum_ref[rows, vc], g_ref[rows, vc].astype(F32), dmix_ref[rows, vc], None)
                    do_out[rows, vc] = dov
                    dg_ref[rows, vc] = dg
                else:
                    dov = do_ref[rows, vc]
                s0 = sh_ref[hh, cc]
                dsc = dst[hh]
                qi, ki = q * ei, k * eki
                a = _dot_nt(q, k) * dmat
                da = _dot_nt(dov, v) * dmat
                dv = _dot_tn(a, dov) + _dot_nt(ki, dsc)
                dqs = _dot(da, k) + _dot(dov, s0) * ei
                dks = _dot_tn(da, q) + _dot(v, dsc) * eki
                dst[hh] = dsc * eb + _dot_tn(dov, qi)
                dq = _rope256_t(dqs, cosv, sinv)
                dk = _rope256_t(dks * RET_KSCALE, cosv, sinv)
                if has_prev:
                    dq = dq + pq_ref[rows, qc]
                    dk = dk + pk_ref[rows, qc]
                    dv = dv + pv_ref[rows, vc]
                dq_ref[rows, qc] = dq.astype(odt)
                dk_ref[rows, qc] = dk.astype(odt)
                dv_ref[rows, vc] = dv.astype(odt)

    hp = RET_HP
    tab = pl.BlockSpec((TM, 128), lambda h, n: (bmap(n), 0))
    qblk = pl.BlockSpec((TM, hp * RET_DK), lambda h, n: (bmap(n), h))
    vblk = pl.BlockSpec((TM, hp * RET_DV), lambda h, n: (bmap(n), h))
    ins = [p, p, p, lgt, cos, sin, sh]
    specs = [qblk, pl.BlockSpec((TM, hp * RET_DK), lambda h, n: (bmap(n), RET_H // hp + h)),
             pl.BlockSpec((TM, hp * RET_DV), lambda h, n: (bmap(n), RET_H // hp + h)),
             pl.BlockSpec((hp, 1, RET_DK), lambda h, n: (h, 0, 0)), tab, tab,
             pl.BlockSpec((hp, nc, RET_DV, RET_DK), lambda h, n: (h, bmap(n), 0, 0))]
    if fused:
        osum, dmix = head
        ins += [osum, p, dmix]
        specs += [vblk, pl.BlockSpec((TM, hp * RET_DV), lambda h, n: (bmap(n), 2 * RET_H // hp + h)), vblk]
    else:
        ins.append(do); specs.append(vblk)
    if has_prev:
        ins += list(prev); specs += [qblk, qblk, vblk]
    out_specs = [qblk, qblk, vblk]
    out_shape = [jax.ShapeDtypeStruct((t, RET_H * RET_DK), odt), jax.ShapeDtypeStruct((t, RET_H * RET_DK), odt),
                 jax.ShapeDtypeStruct((t, RET_H * RET_DV), odt)]
    if fused:
        out_specs += [vblk, vblk]
        out_shape += [jax.ShapeDtypeStruct((t, RET_H * RET_DV), F32), jax.ShapeDtypeStruct((t, RET_H * RET_DV), BF16)]
    return _pcall(body, name=name, grid=(RET_H // hp, nb), in_specs=specs, out_specs=out_specs, out_shape=out_shape,
                  scratch_shapes=[pltpu.VMEM((hp, RET_DV, RET_DK), F32)])(*ins)


def _rope_tables(lc, l):
    tt = jnp.arange(l)
    row, colp = (tt // 64).astype(F32), (tt % 64).astype(F32)
    inv = 10000.0 ** (-jnp.arange(16, dtype=F32) / 16)
    ang = jnp.concatenate([row[:, None] * inv, colp[:, None] * inv], axis=-1)
    ang = jnp.concatenate([jnp.zeros((lc, 32), F32), ang], axis=0)
    acos, asin = jnp.tile(jnp.cos(ang), (1, 4)), jnp.tile(jnp.sin(ang), (1, 4))
    theta = 1.0 / (10000.0 ** jnp.linspace(0.0, 1.0, 128, dtype=F32))
    rang = jnp.arange(l, dtype=F32)[:, None] * theta
    rang = jnp.concatenate([jnp.zeros((lc, 128), F32), rang], axis=0)
    return acos, asin, jnp.cos(rang), jnp.sin(rang)


class _Weights:
    def __init__(self, w):
        self.w = w

    def first(self, after):
        return self.w

    def rest_landed(self, after):
        pass

    def rest(self, after):
        return self.w

    def early_grads(self, grads):
        return jnp.zeros((8, 128), F32)


def _local_step(x0, target, mods, ng, wsrc, small):
    t, d = x0.shape
    l = target.shape[0]
    lc = t - l
    acos, asin, rcos, rsin = _rope_tables(lc, l)
    lg_fw = jnp.log(1.0 - 2.0 ** (-5.0 - jnp.arange(RET_H, dtype=F32)))
    lgt_fw = jnp.broadcast_to(lg_fw[:, None, None], (RET_H, 1, RET_DK))
    lgt_bw = jnp.broadcast_to(lg_fw[::-1][:, None, None], (RET_H, 1, RET_DK))
    gq, gk, sink, gain, lb = small['gq'], small['gk'], small['sink'], small['gain'], small['lb']

    (h1,) = _row_fwd(x0, mods, g=ng[0], shift=0, scale=1, name='l0_norm1')
    w = wsrc.first(h1)
    p0 = _mm_nn(h1, w['even_in'], name='l0_in')
    kp = _kprep_fwd(p0, gk, acos, asin, name='l0_kprep')
    att = _attn_fwd(p0, kp, gq, sink, acos, asin, lc=lc, name='l0_attn')
    hof, hsf = _hgrn_fwd(p0, lb, rev=False, name='l0_hgrn_f')
    wsrc.rest_landed(hof)
    hos, hsb, bmix = _hgrn_fwd(p0, lb, rev=True, name='l0_hgrn_b', ofw=hof, gain=gain)
    mix0 = jnp.concatenate([att, bmix], axis=1)
    y0 = _mm_nn(mix0, w['even_out'], name='l0_out')
    x1, h2 = _row_fwd(x0, mods, y=y0, gate=2, g=ng[1], shift=3, scale=4, name='l0_norm2')
    w = dict(w, **wsrc.rest(h2))
    u0, a0 = _ffn_in(h2, w['ffn_in'], lead=0, name='ffn_in')
    z0 = _mm_nn(a0, w['ffn_out'], lead=0, name='ffn_out')
    x2, h3 = _row_fwd(x1, mods, y=z0, gate=5, g=ng[2], shift=12, scale=13, name='l1_norm1')
    p1 = _mm_nn(h3, w['odd_in'], out_dtype=BF16, name='l1_in')
    rof, rsf = _ret_fwd(p1, lgt_fw, rcos, rsin, rev=False, name='l1_ret_f')
    ros, rsb, mix1 = _ret_fwd(p1, lgt_bw, rcos, rsin, rev=True, name='l1_ret_b', ofw=rof)
    y1 = _mm_nn(mix1, w['odd_out'], name='l1_out')
    x3, h4 = _row_fwd(x2, mods, y=y1, gate=14, g=ng[3], shift=15, scale=16, name='l1_norm2')
    u1, a1 = _ffn_in(h4, w['ffn_in'], lead=1, name='ffn_in')
    z1 = _mm_nn(a1, w['ffn_out'], lead=1, name='ffn_out')
    loss, dx4, dz1, s_fin = _row_final(x3, z1, mods, target, gate=17, name='loss')

    du1 = _ffn_dx(dz1, w['ffn_out'], u1, lead=1, name='ffn_out_dx')
    g_ffn_out1 = _mm_tn(a1, dz1, name='ffn_out_dw')
    dh4 = _mm_nt(du1, w['ffn_in'], lead=1, name='ffn_in_dx')
    g_ffn_in1 = _mm_tn(h4, du1, name='ffn_in_dw')
    dx3, dy1, s_l1n2 = _row_bwd(x3, dx4, dh4, mods, ng[3], shift=15, scale=16, y=y1, gate=14, name='l1_norm2_bwd')
    dmix1 = _mm_nt(dy1, w['odd_out'], name='l1_out_dx')
    g_odd_out = _mm_tn(mix1, dy1, name='l1_out_dw')
    rdq, rdk, rdv, rdo, rdg = _ret_bwd(p1, lgt_fw, rcos, rsin, rsf, None, None, rev=False, name='l1_ret_f_bwd',
                                       head=(ros, dmix1))
    rdq, rdk, rdv = _ret_bwd(p1, lgt_bw, rcos, rsin, rsb, rdo, (rdq, rdk, rdv), rev=True, name='l1_ret_b_bwd')
    dp1 = jnp.concatenate([rdq, rdk, rdv, rdg], axis=1)
    dh3 = _mm_nt(dp1, w['odd_in'], name='l1_in_dx')
    g_odd_in = _mm_tn(h3, dp1, name='l1_in_dw')
    mods = mods + wsrc.early_grads(dict(ffn_in1=g_ffn_in1, ffn_out1=g_ffn_out1, odd_in=g_odd_in, odd_out=g_odd_out))[0, 0]
    dx2, dz0, s_l1n1 = _row_bwd(x2, dx3, dh3, mods, ng[2], shift=12, scale=13, y=z0, gate=5, name='l1_norm1_bwd')
    du0 = _ffn_dx(dz0, w['ffn_out'], u0, lead=0, name='ffn_out_dx')
    g_ffn_out0 = _mm_tn(a0, dz0, name='ffn_out_dw')
    dh2 = _mm_nt(du0, w['ffn_in'], lead=0, name='ffn_in_dx')
    g_ffn_in0 = _mm_tn(h2, du0, name='ffn_in_dw')
    dx1, dy0, s_l0n2 = _row_bwd(x1, dx2, dh2, mods, ng[1], shift=3, scale=4, y=y0, gate=2, name='l0_norm2_bwd')
    dmix0 = _mm_nt(dy0, w['even_out'], name='l0_out_dx')
    g_even_out = _mm_tn(mix0, dy0, name='l0_out_dw')
    hq, hff, hv, dlb_f, hdo, hdg, s_gain = _hgrn_bwd(p0, lb, hsf, None, None, rev=False, name='l0_hgrn_f_bwd',
                                                     head=(hos, dmix0, gain))
    hq, hfb, hv, dlb_b = _hgrn_bwd(p0, lb, hsb, hdo, (hq, hv), rev=True, name='l0_hgrn_b_bwd')
    adq, dkp, adv, s_gq, s_sink = _attn_bwd(p0, kp, gq, sink, acos, asin, dmix0, lc=lc, name='l0_attn_bwd')
    dkv, s_gk = _kprep_bwd(p0, gk, acos, asin, dkp, adv, name='l0_kprep_bwd')
    dp0 = jnp.concatenate([adq, dkv, hq, _bf(hff), hfb, hv, hdg], axis=1)
    dh1 = _mm_nt(dp0, w['even_in'], name='l0_in_dx')
    g_even_in = _mm_tn(h1, dp0, name='l0_in_dw')
    dx0, s_l0n1 = _row_bwd(x0, dx1, dh1, mods, ng[0], shift=0, scale=1, name='l0_norm1_bwd')

    grads = dict(ffn_in0=g_ffn_in0, ffn_in1=g_ffn_in1, ffn_out0=g_ffn_out0, ffn_out1=g_ffn_out1,
                 even_in=g_even_in, even_out=g_even_out, odd_in=g_odd_in, odd_out=g_odd_out)
    sums = dict(fin=s_fin, l1n2=s_l1n2, l1n1=s_l1n1, l0n2=s_l0n2, l0n1=s_l0n1, gain=s_gain, gq=s_gq, gk=s_gk,
                sink=s_sink, dlb_f=dlb_f, dlb_b=dlb_b)
    return loss, dx0, grads, sums


def _place():
    return lax.axis_index("x"), lax.axis_index("y"), lax.axis_index("c")


def _ag8(blk, *, name):
    r, c = blk.shape
    flips = [(dx, dy, dc) for dx in (0, 1) for dy in (0, 1) for dc in (0, 1) if (dx, dy, dc) != (0, 0, 0)]

    def body(x_ref, out_ref, send_sems, recv_sems, local_sem):
        ax, ay, ac = _place()
        me = 4 * ax + 2 * ay + ac
        mine = pltpu.make_async_copy(x_ref, out_ref.at[me], local_sem)
        mine.start()
        sent = []
        for k, (dx, dy, dc) in enumerate(flips):
            peer = (lax.rem(ax + dx, 2), lax.rem(ay + dy, 2), lax.rem(ac + dc, 2))
            cp = pltpu.make_async_remote_copy(src_ref=x_ref, dst_ref=out_ref.at[me], send_sem=send_sems.at[k],
                                              recv_sem=recv_sems.at[k], device_id=peer, device_id_type=MESH)
            cp.start()
            sent.append((cp, 4 * peer[0] + 2 * peer[1] + peer[2]))
        for k, (cp, pidx) in enumerate(sent):
            pltpu.make_async_remote_copy(src_ref=x_ref, dst_ref=out_ref.at[pidx], send_sem=send_sems.at[k],
                                         recv_sem=recv_sems.at[k], device_id=(ax, ay, ac),
                                         device_id_type=MESH).wait_recv()
        for cp, _ in sent:
            cp.wait_send()
        mine.wait()

    return _pcall(
        body, name=name,
        in_specs=[pl.BlockSpec(memory_space=pltpu.VMEM)],
        out_specs=pl.BlockSpec(memory_space=pltpu.VMEM),
        out_shape=jax.ShapeDtypeStruct((8, r, c), blk.dtype),
        scratch_shapes=[pltpu.SemaphoreType.DMA((7,)), pltpu.SemaphoreType.DMA((7,)), pltpu.SemaphoreType.DMA],
    )(blk)


_HBM = pl.BlockSpec(memory_space=pltpu.HBM)
_SEM = pl.BlockSpec(memory_space=pltpu.SEMAPHORE)
_DATAFLOW = pltpu.SideEffectType.DATAFLOW_SIDE_EFFECTING


def _split_start(bufs, plan, k, *, name):
    n = len(bufs)

    def body(*refs):
        ins, send_sems, recv_sems, token = refs[:n], refs[n], refs[n + 1], refs[2 * n + 2]
        for i, (src, dst, dev) in enumerate(plan(ins)):
            pltpu.make_async_remote_copy(src_ref=src, dst_ref=dst, send_sem=send_sems.at[i], recv_sem=recv_sems.at[i],
                                         device_id=dev, device_id_type=MESH).start()
        token[...] = jnp.zeros_like(token)

    res = _pcall(
        body, name=name,
        out_shape=(pltpu.SemaphoreType.DMA((k,)), pltpu.SemaphoreType.DMA((k,)),
                   *[pltpu.HBM(b.shape, b.dtype) for b in bufs], jax.ShapeDtypeStruct((8, 128), F32)),
        in_specs=[_HBM] * n, out_specs=(_SEM, _SEM, *[_HBM] * n, pl.BlockSpec(memory_space=pltpu.VMEM)),
        input_output_aliases={i: 2 + i for i in range(n)},
        compiler_params=pltpu.CompilerParams(has_side_effects=_DATAFLOW),
    )(*[pltpu.with_memory_space_constraint(b, pltpu.HBM) for b in bufs])
    return res[0], res[1], list(res[2:2 + n]), res[2 + n]


def _split_wait(bufs, send_sems, recv_sems, plan, after, *, name):
    n = len(bufs)

    def body(*refs):
        ins, ssem, rsem = refs[:n], refs[n], refs[n + 1]
        for i, (src, dst, dev) in enumerate(plan(ins)):
            cp = pltpu.make_async_remote_copy(src_ref=src, dst_ref=dst, send_sem=ssem.at[i], recv_sem=rsem.at[i],
                                              device_id=dev, device_id_type=MESH)
            cp.wait_send()
            cp.wait_recv()

    res = _pcall(
        body, name=name, out_shape=tuple(pltpu.HBM(b.shape, b.dtype) for b in bufs),
        in_specs=[_HBM] * n + [_SEM, _SEM, pl.BlockSpec(memory_space=pl.ANY)], out_specs=tuple([_HBM] * n),
        input_output_aliases={i: i for i in range(n)},
        compiler_params=pltpu.CompilerParams(has_side_effects=_DATAFLOW),
    )(*bufs, send_sems, recv_sems, after)
    return list(res)


_CHIP_FLIPS = [(1, 0), (0, 1), (1, 1)]


class _GatheredWeights:
    FIRST = ('even_in', 'even_out')
    REST = ('ffn_in', 'ffn_out', 'odd_in', 'odd_out')

    def __init__(self, shards, reducer):
        self.shards = shards
        self.early_grads = functools.partial(reducer.start, 'early')
        self.ici = {}
        for grp, names in (('first', self.FIRST), ('rest', self.REST)):
            src = [shards[nm].reshape(2, shards[nm].shape[0] // 2, shards[nm].shape[1]) for nm in names]
            land = [lax.empty((4,) + a.shape, a.dtype) for a in src]
            m = len(names)
            sends, recvs, bufs, token = _split_start(src + land, functools.partial(self._ici_plan, m, True), 3 * m,
                                                     name='gather_' + grp + '_ici_start')
            self.ici[grp] = (sends, recvs, bufs, m)
            self.token = token if grp == 'first' else self.token + token
        self.rest_d2d = None

    @staticmethod
    def _ici_plan(m, sending, refs):
        ax, ay, ac = _place()
        s = 2 * ax + ay
        out = []
        for a in range(m):
            for dx, dy in _CHIP_FLIPS:
                px, py = lax.rem(ax + dx, 2), lax.rem(ay + dy, 2)
                slot = s if sending else 2 * px + py
                out.append((refs[a].at[ac], refs[m + a].at[slot, ac], (px, py, ac)))
        return out

    @staticmethod
    def _d2d_plan(m, sending, refs):
        ax, ay, ac = _place()
        out = []
        for a in range(m):
            for dx, dy in _CHIP_FLIPS:
                sp = 2 * lax.rem(ax + dx, 2) + lax.rem(ay + dy, 2)
                out.append((refs[a].at[sp, ac], refs[a].at[sp, ac if sending else 1 - ac], (ax, ay, 1 - ac)))
        return out

    def _landed(self, grp, after):
        sends, recvs, bufs, m = self.ici[grp]
        bufs = _split_wait(bufs, sends, recvs, functools.partial(self._ici_plan, m, False), after,
                           name='gather_' + grp + '_ici_wait')
        sends, recvs, land, _ = _split_start(bufs[m:], functools.partial(self._d2d_plan, m, True), 3 * m,
                                             name='gather_' + grp + '_d2d_start')
        return sends, recvs, land, m

    def _full(self, grp, names, d2d, after):
        sends, recvs, land, m = d2d
        land = _split_wait(land, sends, recvs, functools.partial(self._d2d_plan, m, False), after,
                           name='gather_' + grp + '_d2d_wait')
        s = 2 * lax.axis_index("x") + lax.axis_index("y")
        slot = lax.broadcasted_iota(jnp.int32, (4, 1, 1), 0)
        return {nm: _from_shards(nm, jnp.where(slot == s, self.shards[nm][None], g.reshape((4,) + self.shards[nm].shape)))
                for nm, g in zip(names, land)}

    def first(self, after):
        return self._full('first', self.FIRST, self._landed('first', after), after)

    def rest_landed(self, after):
        self.rest_d2d = self._landed('rest', after)

    def rest(self, after):
        return self._full('rest', self.REST, self.rest_d2d, after)


def _to_sibling(arrs, *, name):
    n = len(arrs)

    def body(*refs):
        ins, outs = refs[:n], refs[n:2 * n]
        send_sems, recv_sems = refs[2 * n:]
        ax, ay, ac = _place()
        cps = [pltpu.make_async_remote_copy(src_ref=ins[a], dst_ref=outs[a], send_sem=send_sems.at[a],
                                            recv_sem=recv_sems.at[a], device_id=(ax, ay, 1 - ac),
                                            device_id_type=MESH) for a in range(n)]
        for cp in cps:
            cp.start()
        for cp in cps:
            cp.wait_recv()
        for cp in cps:
            cp.wait_send()

    hbm = pl.BlockSpec(memory_space=pl.ANY)
    return _pcall(
        body, name=name, in_specs=[hbm] * n, out_specs=[hbm] * n,
        out_shape=[jax.ShapeDtypeStruct(a.shape, a.dtype) for a in arrs],
        scratch_shapes=[pltpu.SemaphoreType.DMA((n,))] * 2,
    )(*arrs)


def _mod_fwd(cond_raw, mw, mb, *, name):
    _, d, n = mw.shape

    def body(c_ref, w_ref, b_ref, o_ref):
        cv = c_ref[...]
        o_ref[...] = _dot(cv * _sigmoid(cv), w_ref[...]) + b_ref[...]

    return _pcall(
        body, name=name, grid=(2,),
        in_specs=[pl.BlockSpec((16, d), lambda l: (0, 0)), pl.BlockSpec((None, d, n), lambda l: (l, 0, 0)),
                  pl.BlockSpec((None, 1, n), lambda l: (l, 0, 0))],
        out_specs=pl.BlockSpec((None, 16, n), lambda l: (l, 0, 0)),
        out_shape=jax.ShapeDtypeStruct((2, 16, n), F32),
    )(cond_raw, mw, mb)


def _mod_bwd(cond_raw, dms, mw, *, name):
    _, d, n = mw.shape

    def body(c_ref, dm_ref, w_ref, gw_ref, dc_ref):
        @pl.when(pl.program_id(0) == 0)
        def _():
            dc_ref[...] = jnp.zeros_like(dc_ref)
        cv = c_ref[...]
        gw_ref[...] = _dot_tn(cv * _sigmoid(cv), dm_ref[...])
        dc_ref[...] += _dot_nt(dm_ref[...], w_ref[...])

    return _pcall(
        body, name=name, grid=(2,),
        in_specs=[pl.BlockSpec((16, d), lambda l: (0, 0)), pl.BlockSpec((None, 16, n), lambda l: (l, 0, 0)),
                  pl.BlockSpec((None, d, n), lambda l: (l, 0, 0))],
        out_specs=[pl.BlockSpec((None, d, n), lambda l: (l, 0, 0)), pl.BlockSpec((16, d), lambda l: (0, 0))],
        out_shape=[jax.ShapeDtypeStruct((2, d, n), F32), jax.ShapeDtypeStruct((16, d), F32)],
    )(cond_raw, dms, mw)


def _lb_fwd(hgrn_lb, *, name):
    def body(a_ref, o_ref):
        a0, a1 = a_ref[0:1, :], a_ref[1:2, :]
        m = jnp.maximum(a0, a1)
        e0, e1 = jnp.exp(a0 - m), jnp.exp(a1 - m)
        o_ref[...] = e0 / (e0 + e1)

    return _pcall(body, name=name, out_shape=jax.ShapeDtypeStruct((1, hgrn_lb.shape[1]), F32))(hgrn_lb)


PACK_TILES = ('l0n1', 'l0n2', 'l1n1', 'l1n2', 'fin', 'gq', 'gk', 'gain', 'dlb_f', 'dlb_b', 'sink')
PACK_ROW = {nm: 8 * i for i, nm in enumerate(PACK_TILES)}
MOD_SOURCE = ((('l0n1', 0), ('l0n1', 1), ('l0n2', 2), ('l0n2', 0), ('l0n2', 1), ('l1n1', 2)),
              (('l1n1', 0), ('l1n1', 1), ('l1n2', 2), ('l1n2', 0), ('l1n2', 1), ('fin', 2)))


def _small_finalize(gath, lb_pad, *, name):
    d = gath.shape[2]

    def body(g_ref, lb_ref, small_ref, glb_ref, gmb_ref, dm_ref):
        tot = g_ref[0]
        for e in range(1, 8):
            tot = tot + g_ref[e]

        def row(nm, r=0):
            return tot[PACK_ROW[nm] + r:PACK_ROW[nm] + r + 1, :]

        for k, nm in enumerate(('l0n1', 'l0n2', 'l1n1', 'l1n2')):
            small_ref[k:k + 1, :] = row(nm, 3) + row(nm, 7)
        for k, nm in ((4, 'gq'), (5, 'gk')):
            small_ref[k:k + 1, :] = row(nm) + pltpu.roll(row(nm), d - 64, 1)
        small_ref[6:7, :] = row('gain')
        small_ref[7:8, :] = row('sink')
        lbv = lb_ref[...]
        g0 = (row('dlb_f') + row('dlb_b')) * lbv * (1.0 - lbv)
        glb_ref[...] = jnp.zeros_like(glb_ref)
        glb_ref[0:1, :] = g0
        glb_ref[1:2, :] = -g0
        dm_ref[...] = jnp.zeros_like(dm_ref)
        for l in range(2):
            for part in range(6):
                nm, r = MOD_SOURCE[l][part]
                gmb_ref[l * 6 + part:l * 6 + part + 1, :] = row(nm, r) + row(nm, r + 4)
                rl = PACK_ROW[nm] + r + 4
                for e in range(8):
                    dm_ref[l, part, e:e + 1, :] = g_ref[e, rl:rl + 1, :]
                dm_ref[l, part, 8:9, :] = row(nm, r)

    return _pcall(
        body, name=name,
        out_shape=[jax.ShapeDtypeStruct((8, d), F32), jax.ShapeDtypeStruct((8, d), F32),
                   jax.ShapeDtypeStruct((12, d), F32), jax.ShapeDtypeStruct((2, 6, 16, d), F32)],
    )(gath, lb_pad)


def _cctx_grad(gath, c_ctx2, *, name):
    def body(g_ref, c_ref, o_ref):
        tot = ((g_ref[0, 0:1, :] + g_ref[2, 0:1, :]) + g_ref[4, 0:1, :]) + g_ref[6, 0:1, :]
        cv = c_ref[...]
        s = _sigmoid(cv)
        o_ref[...] = tot * (s * (1.0 + cv * (1.0 - s)))

    return _pcall(body, name=name, out_shape=jax.ShapeDtypeStruct(c_ctx2.shape, F32))(gath, c_ctx2)


def _row_block(r, c, limit=256 * 1024):
    best = None
    for br in range(16, r + 1, 16):
        if r % br == 0 and br * c <= limit:
            best = br
    return best if best is not None else r


def _sum4(parts, *, name):
    _, r, c = parts.shape
    br = _row_block(r, c)

    def body(p_ref, o_ref):
        p = [p_ref[k].astype(F32) for k in range(4)]
        o_ref[...] = ((p[0] + p[1]) + p[2]) + p[3]

    return _pcall(body, name=name, grid=(r // br,),
                  in_specs=[pl.BlockSpec((4, br, c), lambda i: (0, i, 0))],
                  out_specs=pl.BlockSpec((br, c), lambda i: (i, 0)),
                  out_shape=jax.ShapeDtypeStruct((r, c), F32))(parts)


def _add2(a, b, *, name):
    r, c = a.shape
    br = _row_block(r, c)

    def body(a_ref, b_ref, o_ref):
        o_ref[...] = (a_ref[...].astype(F32) + b_ref[...].astype(F32)).astype(BF16)

    blk = pl.BlockSpec((br, c), lambda i: (i, 0))
    return _pcall(body, name=name, grid=(r // br,), in_specs=[blk, blk], out_specs=blk,
                  out_shape=jax.ShapeDtypeStruct((r, c), BF16))(a, b)


def _adam(w, gs, m, v, *, name):
    r, c = w.shape
    br = _row_block(r, c)
    ng = len(gs)
    c1 = 1.0 - ADAM_B1 ** ADAM_STEP
    c2 = 1.0 - ADAM_B2 ** ADAM_STEP

    def body(*refs):
        w_ref, m_ref, v_ref = refs[0], refs[1 + ng], refs[2 + ng]
        outs = refs[3 + ng:]
        g = refs[1][...]
        for k in range(1, ng):
            g = g + refs[1 + k][...]
        mn = ADAM_B1 * m_ref[...] + (1.0 - ADAM_B1) * g
        vn = ADAM_B2 * v_ref[...] + (1.0 - ADAM_B2) * (g * g)
        if ng > 1:
            outs[0][...] = g
        d_out, m_out, v_out = outs[-3:]
        m_out[...] = mn
        v_out[...] = vn
        d_out[...] = -ADAM_LR * ((mn / c1) / (jnp.sqrt(vn / c2) + ADAM_EPS) + ADAM_WD * w_ref[...])

    blk = pl.BlockSpec((br, c), lambda i: (i, 0))
    nout = 4 if ng > 1 else 3
    res = _pcall(body, name=name, grid=(r // br,), in_specs=[blk] * (3 + ng), out_specs=[blk] * nout,
                 out_shape=[jax.ShapeDtypeStruct((r, c), F32)] * nout)(w, *gs, m, v)
    return list(res) if ng > 1 else [gs[0]] + list(res)


def _grad_halves(name, g, ac):
    if name.endswith('_in'):
        n = g.shape[1] // 4
        if name == 'ffn_in':
            assert n == FFN_BK
        order = _ffn_order(g.shape[1]) if name == 'ffn_in' else range(4)
        v = jnp.stack([g[:, b * n:(b + 1) * n] for b in order])
        per = [v[:, :g.shape[0] // 2], v[:, g.shape[0] // 2:]]
    else:
        k4, n = g.shape
        v = g.reshape(4, 2, k4 // 8, n)
        per = [v[:, 0], v[:, 1]]
    first = ac == 0
    return _bf(jnp.where(first, per[0], per[1])), _bf(jnp.where(first, per[1], per[0]))


class _GradReducer:
    def __init__(self):
        self.flight = {}

    @staticmethod
    def _plan(m, sending, refs):
        ax, ay, ac = _place()
        s = 2 * ax + ay
        out = []
        for a in range(m):
            for dx, dy in _CHIP_FLIPS:
                px, py = lax.rem(ax + dx, 2), lax.rem(ay + dy, 2)
                sp = 2 * px + py
                out.append((refs[a].at[sp], refs[m + a].at[s if sending else sp], (px, py, ac)))
        return out

    def start(self, grp, grads):
        ac = lax.axis_index("c")
        names = list(grads)
        halves = [_grad_halves(nm.rstrip('01'), grads[nm], ac) for nm in names]
        theirs = _to_sibling([h[1] for h in halves], name='swap_core_halves_' + grp)
        pair = [_add2(h[0].reshape(-1, b.shape[-1]), b.reshape(-1, b.shape[-1]), name='add_cores').reshape(b.shape)
                for h, b in zip(halves, theirs)]
        m = len(names)
        land = [lax.empty(a.shape, a.dtype) for a in pair]
        sends, recvs, bufs, token = _split_start(pair + land, functools.partial(self._plan, m, True), 3 * m,
                                                 name='scatter_' + grp + '_start')
        self.flight[grp] = (names, sends, recvs, bufs)
        return token

    def finish(self, grp, after):
        names, sends, recvs, bufs = self.flight.pop(grp)
        m = len(names)
        bufs = _split_wait(bufs, sends, recvs, functools.partial(self._plan, m, False), after,
                           name='scatter_' + grp + '_wait')
        ac = lax.axis_index("c")
        s = 2 * lax.axis_index("x") + lax.axis_index("y")
        slot = lax.broadcasted_iota(jnp.int32, (4, 1, 1), 0)
        half_sums = [_sum4(jnp.where(slot == s, p, l), name='sum_chips') for p, l in zip(bufs[:m], bufs[m:])]
        other = _to_sibling(half_sums, name='gather_core_halves_' + grp)
        return {nm: jnp.concatenate([jnp.where(ac == 0, f, o), jnp.where(ac == 0, o, f)], axis=0)
                for nm, f, o in zip(names, half_sums, other)}


def _from_shards(name, g):
    _, r, n = g.shape
    if name == 'ffn_in':
        assert n == FFN_BK
        v = g.reshape(4, 2, r // 2, n)
        return jnp.concatenate([v[b] for b in _ffn_order(4 * n)], axis=-1)
    if name == 'ffn_out':
        return g.reshape(4, 2, r // 2, n).transpose(1, 0, 2, 3).reshape(2, 2 * r, n)
    if name in ('even_in', 'odd_in'):
        return jnp.concatenate([g[b] for b in range(4)], axis=-1)
    return g.reshape(4 * r, n)


def kernel(x, c, ctx, c_ctx, mod_w, mod_b, norm_g, ffn_w_in, ffn_w_out, even_w_in, even_w_out, attn_qk_norm_g, attn_sink, hgrn_out_norm_g, hgrn_lb, odd_w_in, odd_w_out, loss_target, m_c_ctx, m_mod_w, m_mod_b, m_norm_g, m_ffn_w_in, m_ffn_w_out, m_even_w_in, m_even_w_out, m_attn_qk_norm_g, m_attn_sink, m_hgrn_out_norm_g, m_hgrn_lb, m_odd_w_in, m_odd_w_out, v_c_ctx, v_mod_w, v_mod_b, v_norm_g, v_ffn_w_in, v_ffn_w_out, v_even_w_in, v_even_w_out, v_attn_qk_norm_g, v_attn_sink, v_hgrn_out_norm_g, v_hgrn_lb, v_odd_w_in, v_odd_w_out):
    d = x.shape[-1]
    lc = ctx.shape[1]
    assert lc == TM and d == 1024
    ax, ay, ac = _place()
    s = 2 * ax + ay
    me = 4 * ax + 2 * ay + ac
    nmod = mod_w.shape[2]

    def pad8(v):
        return jnp.pad(v, ((0, 8 - v.shape[0]), (0, 0)))

    pack = jnp.concatenate([pad8(c), pad8(norm_g.reshape(1, d))], axis=0)
    g1 = _ag8(pack, name='gather_cond')
    c_all = g1[:, 0, :]
    ng = g1[0::2, 8, :].reshape(4, 2, 2, d // 4).transpose(1, 2, 0, 3).reshape(4, d)

    cond_raw = jnp.concatenate([c_all, pad8(c_ctx.reshape(1, d))], axis=0)
    mb_sh = lax.dynamic_slice_in_dim(mod_b, s * nmod, nmod, axis=1).reshape(2, 1, nmod)
    mpart = _mod_fwd(cond_raw, mod_w, mb_sh, name='mod_fwd')
    g3 = _ag8(mpart.reshape(32, nmod), name='gather_mods')
    mods_full = g3[0::2].reshape(4, 2, 16, nmod).transpose(1, 2, 0, 3).reshape(2, 16, 4 * nmod)
    m_lat = lax.dynamic_index_in_dim(mods_full, me, axis=1, keepdims=False)
    mods = jnp.stack([mods_full[:, 8], m_lat], axis=1).reshape(24, d)

    names = ['ffn_in', 'ffn_out', 'even_in', 'even_out', 'odd_in', 'odd_out']
    shards = [_bf(v.reshape(-1, v.shape[-1])) for v in (ffn_w_in, ffn_w_out, even_w_in, even_w_out, odd_w_in, odd_w_out)]
    shards, mods = lax.optimization_barrier((shards, mods))
    reducer = _GradReducer()
    wsrc = _GatheredWeights(dict(zip(names, shards)), reducer)

    lb = _lb_fwd(hgrn_lb, name='hgrn_lower_bound')
    small = dict(gq=jnp.tile(attn_qk_norm_g[0, 0], 2).reshape(1, 128), gk=jnp.tile(attn_qk_norm_g[0, 1], 2).reshape(1, 128),
                 sink=attn_sink[0], gain=hgrn_out_norm_g, lb=lb)
    x0 = jnp.concatenate([ctx[0], x[0]], axis=0) + wsrc.token[0, 0]
    loss_t, dx0, grads, sums = _local_step(x0, loss_target[0], mods, ng, wsrc, small)
    loss = lax.psum(loss_t[0, 0], ("x", "y", "c"))
    grad_x = dx0[lc:][None]

    def tile(v):
        return jnp.pad(v, ((0, 8 - v.shape[0]), (0, d - v.shape[1])))

    sums = dict(sums, sink=sums['sink'][:, 0].reshape(1, 8))
    g4 = _ag8(jnp.concatenate([tile(sums[nm]) for nm in PACK_TILES], axis=0), name='gather_row_sums')
    small_g, glb, gmb, dmat = _small_finalize(g4, tile(lb)[0:1], name='small_grads')
    dms = lax.dynamic_slice_in_dim(dmat.transpose(0, 2, 1, 3).reshape(2, 16, 6 * d), s * nmod, nmod, axis=2)
    g_mod_w, dcond = _mod_bwd(cond_raw, dms, mod_w, name='mod_bwd')
    g5 = _ag8(dcond[8:16], name='gather_dcond')
    g_c_ctx = _cctx_grad(g5, c_ctx.reshape(8, d // 8).reshape(1, d), name='c_ctx_grad')

    late = {nm: grads[nm] for nm in ('ffn_in0', 'ffn_out0', 'even_in', 'even_out')}
    late, g_c_ctx = lax.optimization_barrier((late, g_c_ctx))
    token = reducer.start('late', late)
    full = reducer.finish('early', token)

    def upd(wv, gs, mv, vv, name):
        shp = wv.shape
        c2 = shp[-1]
        out = _adam(wv.reshape(-1, c2), [g.reshape(-1, c2) for g in gs], mv.reshape(-1, c2), vv.reshape(-1, c2), name=name)
        return [o.reshape(shp) for o in out]

    res = {}
    res['c_ctx'] = upd(c_ctx.reshape(8, d // 8), [g_c_ctx.reshape(8, d // 8)], m_c_ctx.reshape(8, d // 8), v_c_ctx.reshape(8, d // 8), 'adam_c_ctx')
    res['c_ctx'] = [o.reshape(d) for o in res['c_ctx']]
    res['mod_w'] = upd(mod_w, [g_mod_w], m_mod_w, v_mod_w, 'adam_mod_w')
    res['mod_b'] = upd(mod_b, [gmb.reshape(2, 6 * d)], m_mod_b, v_mod_b, 'adam_mod_b')
    g_ng = lax.dynamic_slice_in_dim(small_g[0:4].reshape(2, 2, d), s * (d // 4), d // 4, axis=2)
    res['norm_g'] = upd(norm_g, [g_ng], m_norm_g, v_norm_g, 'adam_norm_g')
    g_qk = jnp.stack([small_g[4, 0:64], small_g[5, 0:64]]).reshape(1, 2, 64)
    res['attn_qk_norm_g'] = upd(attn_qk_norm_g, [g_qk], m_attn_qk_norm_g, v_attn_qk_norm_g, 'adam_qk_gain')
    res['attn_sink'] = upd(attn_sink, [small_g[7, 0:8].reshape(1, 8)], m_attn_sink, v_attn_sink, 'adam_sink')
    res['hgrn_out_norm_g'] = upd(hgrn_out_norm_g, [small_g[6, 0:128].reshape(1, 128)], m_hgrn_out_norm_g, v_hgrn_out_norm_g, 'adam_head_gain')
    res['hgrn_lb'] = upd(hgrn_lb, [glb[0:2, 0:hgrn_lb.shape[1]]], m_hgrn_lb, v_hgrn_lb, 'adam_hgrn_lb')
    res['odd_w_in'] = upd(odd_w_in, [full['odd_in']], m_odd_w_in, v_odd_w_in, 'adam_odd_in')
    res['odd_w_out'] = upd(odd_w_out, [full['odd_out']], m_odd_w_out, v_odd_w_out, 'adam_odd_out')
    full.update(reducer.finish('late', res['odd_w_in'][1]))
    g_ffn_in = jnp.concatenate([full['ffn_in0'], full['ffn_in1']], axis=0)
    g_ffn_out = jnp.concatenate([full['ffn_out0'], full['ffn_out1']], axis=0)
    res['ffn_w_in'] = upd(ffn_w_in, [g_ffn_in], m_ffn_w_in, v_ffn_w_in, 'adam_ffn_in')
    res['ffn_w_out'] = upd(ffn_w_out, [g_ffn_out], m_ffn_w_out, v_ffn_w_out, 'adam_ffn_out')
    res['even_w_in'] = upd(even_w_in, [full['even_in']], m_even_w_in, v_even_w_in, 'adam_even_in')
    res['even_w_out'] = upd(even_w_out, [full['even_out']], m_even_w_out, v_even_w_out, 'adam_even_out')

    order = ['c_ctx', 'mod_w', 'mod_b', 'norm_g', 'ffn_w_in', 'ffn_w_out', 'even_w_in', 'even_w_out',
             'attn_qk_norm_g', 'attn_sink', 'hgrn_out_norm_g', 'hgrn_lb', 'odd_w_in', 'odd_w_out']
    outs = [loss, grad_x]
    for k in range(4):
        outs += [res[nm][k] for nm in order]
    return tuple(outs)
```

```python
import functools
import math

import numpy as np
import jax
import jax.numpy as jnp
from jax import lax
from jax.experimental import pallas as pl
from jax.experimental.pallas import tpu as pltpu

F32 = jnp.float32
BF16 = jnp.bfloat16
EPS = 1e-6
TM = 256
CHUNK = 64
QB = 128
WINDOW = 128
NEG = -1e30
MESH = pl.DeviceIdType.MESH

ADAM_LR, ADAM_B1, ADAM_B2, ADAM_EPS, ADAM_WD, ADAM_STEP = 0.001, 0.9, 0.999, 1e-08, 0.01, 10


def _pcall(body, **kw):
    return pl.pallas_call(body, **kw)


def _pick(n, cap):
    best = None
    for m in range(128, min(n, cap) + 1, 128):
        if n % m == 0:
            best = m
    assert best is not None, (n, cap)
    return best


def _bf(x):
    return x.astype(BF16)


def _dot(a, b):
    return jnp.dot(_bf(a), _bf(b), preferred_element_type=F32)


def _dot_nt(a, b):
    return lax.dot_general(_bf(a), _bf(b), (((1,), (1,)), ((), ())), preferred_element_type=F32)


def _dot_tn(a, b):
    return lax.dot_general(_bf(a), _bf(b), (((0,), (0,)), ((), ())), preferred_element_type=F32)


def _dot_exact(a, b):
    return jnp.dot(a, b, preferred_element_type=F32, precision=lax.Precision.HIGHEST)


def _sigmoid(x):
    return 1.0 / (1.0 + jnp.exp(-x))


def _iota(shape, dim):
    return lax.broadcasted_iota(jnp.int32, shape, dim)


def _mm_nn(a, b, *, lead=None, out_dtype=F32, name):
    m, k = a.shape
    n = b.shape[-1]
    bm = 768 if m % 768 == 0 else TM
    bn = _pick(n, 1024) if n % 512 == 0 else _pick(n, 1664)

    def body(a_ref, b_ref, o_ref):
        o_ref[...] = _dot(a_ref[...], b_ref[...]).astype(o_ref.dtype)

    if lead is None:
        b_spec = pl.BlockSpec((k, bn), lambda i, j: (0, j))
    else:
        b_spec = pl.BlockSpec((None, k, bn), lambda i, j: (lead, 0, j))
    return _pcall(
        body, name=name, grid=(m // bm, n // bn),
        in_specs=[pl.BlockSpec((bm, k), lambda i, j: (i, 0)), b_spec],
        out_specs=pl.BlockSpec((bm, bn), lambda i, j: (i, j)),
        out_shape=jax.ShapeDtypeStruct((m, n), out_dtype),
    )(a, b)


def _mm_nt(a, b, *, lead=None, name):
    m, n = a.shape
    k = b.shape[-2]
    bm = 768 if m % 768 == 0 else TM
    bk = _pick(k, 512)

    def body(a_ref, b_ref, o_ref):
        o_ref[...] = _dot_nt(a_ref[...], b_ref[...])

    if lead is None:
        b_spec = pl.BlockSpec((bk, n), lambda i, j: (j, 0))
    else:
        b_spec = pl.BlockSpec((None, bk, n), lambda i, j: (lead, j, 0))
    return _pcall(
        body, name=name, grid=(m // bm, k // bk),
        in_specs=[pl.BlockSpec((bm, n), lambda i, j: (i, 0)), b_spec],
        out_specs=pl.BlockSpec((bm, bk), lambda i, j: (i, j)),
        out_shape=jax.ShapeDtypeStruct((m, k), F32),
    )(a, b)


def _mm_tn(a, b, *, name):
    t, k = a.shape
    n = b.shape[1]
    bt = 768 if t % 768 == 0 else TM
    bk = _pick(k, 1536)
    bn = _pick(n, 1024) if n % 1024 == 0 or n < 1664 else _pick(n, 1664)

    def body(a_ref, b_ref, o_ref):
        @pl.when(pl.program_id(2) == 0)
        def _():
            o_ref[...] = jnp.zeros_like(o_ref)
        o_ref[...] += _dot_tn(a_ref[...], b_ref[...])

    return _pcall(
        body, name=name, grid=(k // bk, n // bn, t // bt),
        in_specs=[pl.BlockSpec((bt, bk), lambda i, j, s: (s, i)),
                  pl.BlockSpec((bt, bn), lambda i, j, s: (s, j))],
        out_specs=pl.BlockSpec((bk, bn), lambda i, j, s: (i, j)),
        out_shape=jax.ShapeDtypeStruct((k, n), F32),
    )(a, b)


def _mod_row(mods_ref, lat, idx):
    return jnp.where(lat, mods_ref[idx + 6:idx + 7, :], mods_ref[idx:idx + 1, :])


def _row_fwd(x, mods, *, y=None, gate=None, g=None, shift=None, scale=None, name):
    t, d = x.shape
    has_y, has_n = y is not None, g is not None

    def body(*refs):
        refs = list(refs)
        x_ref, mods_ref = refs[0], refs[1]
        pos = 2
        if has_y:
            y_ref = refs[pos]; pos += 1
        if has_n:
            g_ref = refs[pos]; pos += 1
        outs = refs[pos:]
        lat = pl.program_id(0) > 0
        x1 = x_ref[...]
        o = 0
        if has_y:
            x1 = x1 + _mod_row(mods_ref, lat, gate) * y_ref[...]
            outs[o][...] = x1; o += 1
        if has_n:
            rs = lax.rsqrt(jnp.mean(x1 * x1, axis=-1, keepdims=True) + EPS)
            hn = x1 * rs * g_ref[...]
            h = hn * (1.0 + _mod_row(mods_ref, lat, scale)) + _mod_row(mods_ref, lat, shift)
            outs[o][...] = h.astype(BF16)

    row = pl.BlockSpec((TM, d), lambda i: (i, 0))
    ins, specs = [x, mods], [row, pl.BlockSpec(mods.shape, lambda i: (0, 0))]
    if has_y:
        ins.append(y); specs.append(row)
    if has_n:
        ins.append(g.reshape(1, d)); specs.append(pl.BlockSpec((1, d), lambda i: (0, 0)))
    out_shape, out_specs = [], []
    if has_y:
        out_shape.append(jax.ShapeDtypeStruct((t, d), F32)); out_specs.append(row)
    if has_n:
        out_shape.append(jax.ShapeDtypeStruct((t, d), BF16)); out_specs.append(row)
    res = _pcall(body, name=name, grid=(t // TM,), in_specs=specs, out_specs=out_specs,
                 out_shape=out_shape)(*ins)
    return res


def _acc_row(ref, r, val):
    ref[r:r + 1, :] += val


def _row_final(x, z, mods, target, *, gate, name):
    t, d = x.shape

    def body(x_ref, mods_ref, z_ref, t_ref, loss_ref, dx_ref, dz_ref, sums_ref):
        i = pl.program_id(0)
        lat = i > 0

        @pl.when(i == 0)
        def _():
            loss_ref[...] = jnp.zeros_like(loss_ref)
            sums_ref[...] = jnp.zeros_like(sums_ref)

        gt = _mod_row(mods_ref, lat, gate)
        zz = z_ref[...]
        yv = x_ref[...] + gt * zz
        keep = jnp.where(lat, 1.0, 0.0).astype(F32)
        diff = (yv - t_ref[...]) * keep
        part = jnp.sum(jnp.sum(diff * diff, axis=0, keepdims=True), axis=1, keepdims=True)
        loss_ref[...] += part * (0.5 / d)
        dy = diff * (1.0 / d)
        dx_ref[...] = dy
        dz_ref[...] = (gt * dy).astype(BF16)
        _acc_row(sums_ref, 6, jnp.sum(dy * zz, axis=0, keepdims=True))

    row = pl.BlockSpec((TM, d), lambda i: (i, 0))
    return _pcall(
        body, name=name, grid=(t // TM,),
        in_specs=[row, pl.BlockSpec(mods.shape, lambda i: (0, 0)), row,
                  pl.BlockSpec((TM, d), lambda i: (jnp.maximum(i - 1, 0), 0))],
        out_specs=[pl.BlockSpec((8, 128), lambda i: (0, 0)), row, row,
                   pl.BlockSpec((8, d), lambda i: (0, 0))],
        out_shape=[jax.ShapeDtypeStruct((8, 128), F32), jax.ShapeDtypeStruct((t, d), F32),
                   jax.ShapeDtypeStruct((t, d), BF16), jax.ShapeDtypeStruct((8, d), F32)],
    )(x, mods, z, target)


def _row_bwd(xn, dxo, dh, mods, g, *, shift, scale, y=None, gate=None, latent_only=False, name):
    t, d = xn.shape
    has_y = y is not None

    def body(*refs):
        refs = list(refs)
        x_ref, dxo_ref, dh_ref, mods_ref, g_ref = refs[:5]
        pos = 5
        if has_y:
            y_ref = refs[pos]; pos += 1
        dx_ref = refs[pos]; pos += 1
        if has_y:
            dy_ref = refs[pos]; pos += 1
        sums_ref = refs[pos]
        i = pl.program_id(0)
        lat = i > 0

        @pl.when(i == 0)
        def _():
            sums_ref[...] = jnp.zeros_like(sums_ref)

        x1 = x_ref[...]
        gv = g_ref[...]
        rs = lax.rsqrt(jnp.mean(x1 * x1, axis=-1, keepdims=True) + EPS)
        xh = x1 * rs
        dhv = dh_ref[...]
        dn = dhv * (1.0 + _mod_row(mods_ref, lat, scale))
        dxh = dn * gv
        dx = dxo_ref[...] + rs * (dxh - xh * jnp.mean(dxh * xh, axis=-1, keepdims=True))
        dx_ref[...] = dx
        vals = [jnp.sum(dhv, axis=0, keepdims=True),
                jnp.sum(dhv * (xh * gv), axis=0, keepdims=True),
                None,
                jnp.sum(dn * xh, axis=0, keepdims=True)]
        if has_y:
            dy_ref[...] = (_mod_row(mods_ref, lat, gate) * dx).astype(BF16)
            vals[2] = jnp.sum(dx * y_ref[...], axis=0, keepdims=True)

        @pl.when(i == 0)
        def _():
            for r, v in enumerate(vals):
                if v is not None:
                    _acc_row(sums_ref, r, v)

        @pl.when(i > 0)
        def _():
            for r, v in enumerate(vals):
                if v is not None:
                    _acc_row(sums_ref, 4 + r, v)

    row = pl.BlockSpec((TM, d), lambda i: (i, 0))
    ins = [xn, dxo, dh, mods, g.reshape(1, d)]
    specs = [row, row, row, pl.BlockSpec(mods.shape, lambda i: (0, 0)), pl.BlockSpec((1, d), lambda i: (0, 0))]
    if latent_only:
        out_shape = [jax.ShapeDtypeStruct((t - TM, d), F32)]
        out_specs = [pl.BlockSpec((TM, d), lambda i: (jnp.maximum(i - 1, 0), 0))]
    else:
        out_shape, out_specs = [jax.ShapeDtypeStruct((t, d), F32)], [row]
    if has_y:
        ins.append(y); specs.append(row)
        out_shape.append(jax.ShapeDtypeStruct((t, d), BF16)); out_specs.append(row)
    out_shape.append(jax.ShapeDtypeStruct((8, d), F32))
    out_specs.append(pl.BlockSpec((8, d), lambda i: (0, 0)))
    return _pcall(body, name=name, grid=(t // TM,), in_specs=specs, out_specs=out_specs,
                  out_shape=out_shape)(*ins)


FFN_BK = 1408


FFN_SUB = 256


def _ffn_order(n2):
    nb = n2 // (2 * FFN_BK)
    return [h * nb + j for j in range(nb) for h in (0, 1)]


def _ffn_interleave(w):
    return jnp.concatenate([w[..., b * FFN_BK:(b + 1) * FFN_BK] for b in _ffn_order(w.shape[-1])], axis=-1)


def _ffn_deinterleave(w):
    order = _ffn_order(w.shape[-1])
    return jnp.concatenate([w[..., order.index(b) * FFN_BK:(order.index(b) + 1) * FFN_BK]
                            for b in range(len(order))], axis=-1)


def _big_tile(t):
    return 768 if t % 768 == 0 else TM


def _ffn_in(h, w, *, lead, name):
    t, d = h.shape
    n2 = w.shape[-1]
    bm, bk = _big_tile(t), FFN_BK

    def body(h_ref, w_ref, u_ref, a_ref):
        hb = h_ref[...]
        for c0 in range(0, bk, FFN_SUB):
            c1 = min(c0 + FFN_SUB, bk)
            ug = _dot(hb, w_ref[:, c0:c1]).astype(BF16)
            uu = _dot(hb, w_ref[:, bk + c0:bk + c1]).astype(BF16)
            u_ref[:, c0:c1] = ug
            u_ref[:, bk + c0:bk + c1] = uu
            gv, up = ug.astype(F32), uu.astype(F32)
            a_ref[:, c0:c1] = (gv * _sigmoid(gv) * up).astype(BF16)

    return _pcall(
        body, name=name, grid=(t // bm, n2 // (2 * bk)),
        in_specs=[pl.BlockSpec((bm, d), lambda i, j: (i, 0)),
                  pl.BlockSpec((None, d, 2 * bk), lambda i, j: (lead, 0, j))],
        out_specs=[pl.BlockSpec((bm, 2 * bk), lambda i, j: (i, j)), pl.BlockSpec((bm, bk), lambda i, j: (i, j))],
        out_shape=[jax.ShapeDtypeStruct((t, n2), BF16), jax.ShapeDtypeStruct((t, n2 // 2), BF16)],
    )(h, w)


def _ffn_dx(dz, w_out, u, *, lead, name):
    t, d = dz.shape
    n2 = u.shape[1]
    bm, bk = _big_tile(t), FFN_BK

    def body(dz_ref, w_ref, u_ref, du_ref):
        dzb = dz_ref[...]
        for c0 in range(0, bk, FFN_SUB):
            c1 = min(c0 + FFN_SUB, bk)
            da = _dot_nt(dzb, w_ref[c0:c1, :])
            gv, up = u_ref[:, c0:c1].astype(F32), u_ref[:, bk + c0:bk + c1].astype(F32)
            s = _sigmoid(gv)
            du_ref[:, c0:c1] = (da * up * (s * (1.0 + gv * (1.0 - s)))).astype(BF16)
            du_ref[:, bk + c0:bk + c1] = (da * gv * s).astype(BF16)

    ublk = pl.BlockSpec((bm, 2 * bk), lambda i, j: (i, j))
    return _pcall(
        body, name=name, grid=(t // bm, n2 // (2 * bk)),
        in_specs=[pl.BlockSpec((bm, d), lambda i, j: (i, 0)),
                  pl.BlockSpec((None, bk, d), lambda i, j: (lead, j, 0)), ublk],
        out_specs=ublk, out_shape=jax.ShapeDtypeStruct((t, n2), BF16),
    )(dz, w_out, u)


def _lane(shape):
    return _iota(shape, len(shape) - 1)


def _pair_norm(x, g):
    lo = _lane(x.shape) < 64
    x2 = x * x
    s_lo = jnp.sum(jnp.where(lo, x2, 0.0), axis=-1, keepdims=True)
    s_hi = jnp.sum(jnp.where(lo, 0.0, x2), axis=-1, keepdims=True)
    rs = lax.rsqrt(jnp.where(lo, s_lo, s_hi) * (1.0 / 64) + EPS)
    return x * rs, rs


def _pair_mean(v):
    lo = _lane(v.shape) < 64
    s_lo = jnp.sum(jnp.where(lo, v, 0.0), axis=-1, keepdims=True)
    s_hi = jnp.sum(jnp.where(lo, 0.0, v), axis=-1, keepdims=True)
    return jnp.where(lo, s_lo, s_hi) * (1.0 / 64)


def _rot64(x):
    r1 = pltpu.roll(x, 32, 1)
    r2 = pltpu.roll(x, 96, 1)
    even = ((_lane(x.shape) >> 5) & 1) == 0
    return jnp.where(even, -r2, r1)


def _rope64(x, cos, sin):
    return x * cos + _rot64(x) * sin


def _rope64_t(d, cos, sin):
    return d * cos - _rot64(d * sin)


def _kprep_fwd(p, gk, cos, sin, *, name):
    t = p.shape[0]

    def body(k_ref, g_ref, c_ref, s_ref, o_ref):
        xh, _ = _pair_norm(k_ref[...], None)
        o_ref[...] = _rope64(xh * g_ref[...], c_ref[...], s_ref[...])

    blk = pl.BlockSpec((TM, 128), lambda i: (i, 0))
    return _pcall(
        body, name=name, grid=(t // TM,),
        in_specs=[pl.BlockSpec((TM, 128), lambda i: (i, 4)), pl.BlockSpec((1, 128), lambda i: (0, 0)), blk, blk],
        out_specs=blk, out_shape=jax.ShapeDtypeStruct((t, 128), F32),
    )(p, gk, cos, sin)


def _kprep_bwd(p, gk, cos, sin, dkp, dv, *, name):
    t = p.shape[0]

    def body(k_ref, g_ref, c_ref, s_ref, dkp_ref, dv_ref, o_ref, dg_ref):
        @pl.when(pl.program_id(0) == 0)
        def _():
            dg_ref[...] = jnp.zeros_like(dg_ref)
        xh, rs = _pair_norm(k_ref[...], None)
        dn = _rope64_t(dkp_ref[...], c_ref[...], s_ref[...])
        _acc_row(dg_ref, 0, jnp.sum(dn * xh, axis=0, keepdims=True))
        dxh = dn * g_ref[...]
        o_ref[:, 0:128] = (rs * (dxh - xh * _pair_mean(dxh * xh))).astype(BF16)
        o_ref[:, 128:256] = dv_ref[...].astype(BF16)

    blk = pl.BlockSpec((TM, 128), lambda i: (i, 0))
    return _pcall(
        body, name=name, grid=(t // TM,),
        in_specs=[pl.BlockSpec((TM, 128), lambda i: (i, 4)), pl.BlockSpec((1, 128), lambda i: (0, 0)), blk, blk, blk, blk],
        out_specs=[pl.BlockSpec((TM, 256), lambda i: (i, 0)), pl.BlockSpec((8, 128), lambda i: (0, 0))],
        out_shape=[jax.ShapeDtypeStruct((t, 256), BF16), jax.ShapeDtypeStruct((8, 128), F32)],
    )(p, gk, cos, sin, dkp, dv)


def _attn_common(i, t, lc, kp_ref, v_ref):
    span = QB + 2 * WINDOW
    start = pl.multiple_of(jnp.clip((i - 1) * QB, lc, t - span), QB)
    kall = jnp.concatenate([kp_ref[0:lc, :], kp_ref[pl.ds(start, span), :]], axis=0)
    vall = jnp.concatenate([v_ref[0:lc, :], v_ref[pl.ds(start, span), :]], axis=0)
    nk = lc + span
    col = _iota((QB, nk), 1)
    krow = jnp.where(col < lc, col, start + col - lc)
    qrow = i * QB + _iota((QB, nk), 0)
    valid = (col < lc) | ((qrow >= lc) & (krow >= lc) & (jnp.abs(krow - qrow) <= WINDOW))
    lo = _lane(kall.shape) < 64
    kroll, vroll = pltpu.roll(kall, 64, 1), pltpu.roll(vall, 64, 1)
    zero = jnp.zeros_like(kall)
    kvar = [[_bf(jnp.where(lo, kall, zero)), _bf(jnp.where(lo, zero, kroll))],
            [_bf(jnp.where(lo, kroll, zero)), _bf(jnp.where(lo, zero, kall))]]
    vvar = [[_bf(jnp.where(lo, vall, zero)), _bf(jnp.where(lo, zero, vroll))],
            [_bf(jnp.where(lo, vroll, zero)), _bf(jnp.where(lo, zero, vall))]]
    return start, valid, kvar, vvar


def _softmax_sink(s, valid, snk):
    s = jnp.where(valid, s, NEG)
    m = jnp.maximum(jnp.max(s, axis=-1, keepdims=True), snk)
    e = jnp.exp(s - m)
    es = jnp.exp(snk - m)
    inv = 1.0 / (jnp.sum(e, axis=-1, keepdims=True) + es)
    return e * inv, es * inv


def _attn_fwd(p, kp, gq, sink, cos, sin, *, lc, name):
    t = p.shape[0]
    scale = 64 ** -0.5

    def body(q_ref, kp_ref, v_ref, g_ref, sink_ref, c_ref, s_ref, o_ref):
        i = pl.program_id(0)
        _, valid, kvar, vvar = _attn_common(i, t, lc, kp_ref, v_ref)
        cosv, sinv, gv = c_ref[...], s_ref[...], g_ref[...]
        for j in range(4):
            xh, _ = _pair_norm(q_ref[:, 128 * j:128 * j + 128], None)
            q2 = _bf(_rope64(xh * gv, cosv, sinv))
            acc = jnp.zeros((QB, 128), F32)
            for half in range(2):
                s = _dot_nt(q2, kvar[j // 2][half]) * scale
                pr, _ = _softmax_sink(s, valid, sink_ref[2 * j + half])
                acc = acc + _dot(pr, vvar[j // 2][half])
            o_ref[:, 128 * j:128 * j + 128] = acc.astype(BF16)

    qblk = pl.BlockSpec((QB, 128), lambda i: (i, 0))
    return _pcall(
        body, name=name, grid=(t // QB,),
        in_specs=[pl.BlockSpec((QB, 512), lambda i: (i, 0)),
                  pl.BlockSpec((t, 128), lambda i: (0, 0)),
                  pl.BlockSpec((t, 128), lambda i: (0, 5)),
                  pl.BlockSpec((1, 128), lambda i: (0, 0)),
                  pl.BlockSpec(memory_space=pltpu.SMEM), qblk, qblk],
        out_specs=pl.BlockSpec((QB, 512), lambda i: (i, 0)),
        out_shape=jax.ShapeDtypeStruct((t, 512), BF16),
    )(p, kp, p, gq, sink, cos, sin)


def _attn_bwd(p, kp, gq, sink, cos, sin, dmix, *, lc, name):
    t = p.shape[0]
    scale = 64 ** -0.5
    span = QB + 2 * WINDOW

    def body(q_ref, kp_ref, v_ref, g_ref, sink_ref, c_ref, s_ref, do_ref,
             dq_ref, dk_ref, dv_ref, dg_ref, dsink_ref):
        i = pl.program_id(0)

        @pl.when(i == 0)
        def _():
            dk_ref[...] = jnp.zeros_like(dk_ref)
            dv_ref[...] = jnp.zeros_like(dv_ref)
            dg_ref[...] = jnp.zeros_like(dg_ref)
            dsink_ref[...] = jnp.zeros_like(dsink_ref)

        start, valid, kvar, vvar = _attn_common(i, t, lc, kp_ref, v_ref)
        cosv, sinv, gv = c_ref[...], s_ref[...], g_ref[...]
        nk = lc + span
        lo = _lane((nk, 128)) < 64
        dkt = [jnp.zeros((64, nk), F32), jnp.zeros((64, nk), F32)]
        dvt = [jnp.zeros((64, nk), F32), jnp.zeros((64, nk), F32)]
        for j in range(4):
            kvh = j // 2
            xh, rs = _pair_norm(q_ref[:, 128 * j:128 * j + 128], None)
            q2 = _bf(_rope64(xh * gv, cosv, sinv))
            do2 = _bf(do_ref[:, 128 * j:128 * j + 128])
            dq2 = jnp.zeros((QB, 128), F32)
            for half in range(2):
                s = _dot_nt(q2, kvar[kvh][half]) * scale
                pr, ps = _softmax_sink(s, valid, sink_ref[2 * j + half])
                dp = _dot_nt(do2, vvar[kvh][half])
                delta = jnp.sum(pr * dp, axis=-1, keepdims=True)
                ds = pr * (dp - delta) * scale
                dsk = jnp.sum(jnp.sum(-ps * delta, axis=0, keepdims=True), axis=1, keepdims=True)
                _acc_row(dsink_ref, 2 * j + half, jnp.broadcast_to(dsk, (1, 128)))
                dq2 = dq2 + _dot(ds, kvar[kvh][half])
                hrows = slice(64 * half, 64 * half + 64)
                dkt[kvh] = dkt[kvh] + _dot_tn(q2, ds)[hrows]
                dvt[kvh] = dvt[kvh] + _dot_tn(do2, pr)[hrows]
            dn = _rope64_t(dq2, cosv, sinv)
            _acc_row(dg_ref, 0, jnp.sum(dn * xh, axis=0, keepdims=True))
            dxh = dn * gv
            dq_ref[:, 128 * j:128 * j + 128] = (rs * (dxh - xh * _pair_mean(dxh * xh))).astype(BF16)
        dk_all = jnp.concatenate(dkt, axis=0).T
        dv_all = jnp.concatenate(dvt, axis=0).T
        dk_ref[0:lc, :] += dk_all[0:lc]
        dv_ref[0:lc, :] += dv_all[0:lc]
        dk_ref[pl.ds(start, span), :] += dk_all[lc:nk]
        dv_ref[pl.ds(start, span), :] += dv_all[lc:nk]

    qblk = pl.BlockSpec((QB, 128), lambda i: (i, 0))
    full = pl.BlockSpec((t, 128), lambda i: (0, 0))
    small = pl.BlockSpec((8, 128), lambda i: (0, 0))
    return _pcall(
        body, name=name, grid=(t // QB,),
        in_specs=[pl.BlockSpec((QB, 512), lambda i: (i, 0)), full,
                  pl.BlockSpec((t, 128), lambda i: (0, 5)),
                  pl.BlockSpec((1, 128), lambda i: (0, 0)),
                  pl.BlockSpec(memory_space=pltpu.SMEM), qblk, qblk,
                  pl.BlockSpec((QB, 512), lambda i: (i, 0))],
        out_specs=[pl.BlockSpec((QB, 512), lambda i: (i, 0)), full, full, small, small],
        out_shape=[jax.ShapeDtypeStruct((t, 512), BF16), jax.ShapeDtypeStruct((t, 128), F32),
                   jax.ShapeDtypeStruct((t, 128), F32), jax.ShapeDtypeStruct((8, 128), F32),
                   jax.ShapeDtypeStruct((8, 128), F32)],
    )(p, kp, p, gq, sink, cos, sin, dmix)


def _tri(rev):
    r, c = _iota((CHUNK, CHUNK), 0), _iota((CHUNK, CHUNK), 1)
    return (c >= r) if rev else (c <= r)


def _blk_map(nb, rev, backward):
    if not rev:
        return (lambda n: nb - 1 - n) if backward else (lambda n: n)
    if backward:
        return lambda n: jnp.where(n < nb - 1, n + 1, 0)
    return lambda n: jnp.where(n == 0, 0, nb - n)


def _chunk_order(rev, backward, nc=TM // CHUNK):
    order = list(range(nc))
    return order[::-1] if (rev != backward) else order


def _hgrn_gates(qraw, fraw, lb):
    sq = _sigmoid(qraw)
    sf = _sigmoid(fraw)
    f = lb + (1.0 - lb) * sf
    return qraw * sq, 1.0 - f, jnp.log(f), sq, sf, f


HGRN_HP = 2


def _chunk_cumsum(x, rev):
    n = x.shape[0]
    pos = _iota(x.shape, 0) & (CHUNK - 1)
    s = 1
    while s < CHUNK:
        if rev:
            x = x + jnp.where(pos < CHUNK - s, pltpu.roll(x, n - s, 0), 0.0)
        else:
            x = x + jnp.where(pos >= s, pltpu.roll(x, s, 0), 0.0)
        s *= 2
    return x


def _block_terms(lf, rev):
    b = _chunk_cumsum(lf, rev)
    mid, last = (CHUNK // 2 - 1, 0) if rev else (CHUNK // 2, CHUNK - 1)

    def chunk_row(off):
        return jnp.concatenate([jnp.broadcast_to(b[c * CHUNK + off:c * CHUNK + off + 1, :], (CHUNK, b.shape[1]))
                                for c in range(TM // CHUNK)], axis=0)

    r, bl = chunk_row(mid), chunk_row(last)
    return _tri(rev), jnp.exp(b - r), jnp.exp(r - b), jnp.exp(b), jnp.exp(bl - b), jnp.exp(bl)


def _headnorm_apply(o, gv, gain):
    n = o * lax.rsqrt(jnp.mean(o * o, axis=-1, keepdims=True) + EPS)
    if gain is not None:
        n = n * gain
    return (n * (gv * _sigmoid(gv))).astype(BF16)


def _headnorm_grad(o, gv, dy, gain):
    rs = lax.rsqrt(jnp.mean(o * o, axis=-1, keepdims=True) + EPS)
    xh = o * rs
    n = xh * gain if gain is not None else xh
    sg = _sigmoid(gv)
    dn = dy * (gv * sg)
    dg = (dy * n * (sg * (1.0 + gv * (1.0 - sg)))).astype(BF16)
    dgain = jnp.sum(dn * xh, axis=0, keepdims=True)
    dxh = dn * gain if gain is not None else dn
    return rs * (dxh - xh * jnp.mean(dxh * xh, axis=-1, keepdims=True)), dg, dgain


def _hgrn_fwd(p, lb, *, rev, name, ofw=None, gain=None):
    t = p.shape[0]
    nb, nc = t // TM, TM // CHUNK
    bmap = _blk_map(nb, rev, False)
    fcol = 14 if rev else 10
    fused = ofw is not None

    def body(*refs):
        q_ref, f_ref, v_ref, lb_ref = refs[:4]
        if fused:
            ofw_ref, g_ref, gain_ref, o_ref, sh_ref, mix_ref, st = refs[4:]
        else:
            o_ref, sh_ref, st = refs[4:]

        @pl.when(pl.program_id(1) == 0)
        def _():
            st[...] = jnp.zeros_like(st)
        for hh in range(HGRN_HP):
            ln = slice(128 * hh, 128 * hh + 128)
            q, k, lf, _, _, _ = _hgrn_gates(q_ref[:, ln], f_ref[:, ln], lb_ref[:, ln])
            tri, eq, ek, ei, eki, eb = _block_terms(lf, rev)
            qe, ke, qi, ki, vb = _bf(q * eq), _bf(k * ek), _bf(q * ei), _bf(k * eki), _bf(v_ref[:, ln])
            intra = []
            for cc in range(nc):
                rows = slice(cc * CHUNK, (cc + 1) * CHUNK)
                a = jnp.where(tri, _dot_nt(qe[rows], ke[rows]), 0.0)
                intra.append(_dot(a, vb[rows]))
            s = st[hh]
            for cc in _chunk_order(rev, False):
                rows = slice(cc * CHUNK, (cc + 1) * CHUNK)
                sh_ref[hh, cc] = s
                o_ref[rows, ln] = intra[cc] + _dot_nt(qi[rows], s)
                s = s * eb[cc * CHUNK:cc * CHUNK + 1, :] + _dot_tn(vb[rows], ki[rows])
            st[hh] = s
            if fused:
                osum = o_ref[:, ln] + ofw_ref[:, ln]
                o_ref[:, ln] = osum
                mix_ref[:, ln] = _headnorm_apply(osum, g_ref[:, ln], gain_ref[...])

    hp, wd = HGRN_HP, 128 * HGRN_HP

    def col(c0):
        return pl.BlockSpec((TM, wd), lambda h, n: (bmap(n), c0 // hp + h))

    oblk = pl.BlockSpec((TM, wd), lambda h, n: (bmap(n), h))
    ins, specs = [p, p, p, lb], [col(6), col(fcol), col(18), pl.BlockSpec((1, wd), lambda h, n: (0, h))]
    out_specs = [oblk, pl.BlockSpec((hp, nc, 128, 128), lambda h, n: (h, bmap(n), 0, 0))]
    out_shape = [jax.ShapeDtypeStruct((t, 512), F32), jax.ShapeDtypeStruct((4, t // CHUNK, 128, 128), F32)]
    if fused:
        ins += [ofw, p, gain]
        specs += [oblk, col(22), pl.BlockSpec((1, 128), lambda h, n: (0, 0))]
        out_specs.append(oblk)
        out_shape.append(jax.ShapeDtypeStruct((t, 512), BF16))
    return _pcall(body, name=name, grid=(4 // hp, nb), in_specs=specs, out_specs=out_specs, out_shape=out_shape,
                  scratch_shapes=[pltpu.VMEM((hp, 128, 128), F32)])(*ins)


def _hgrn_bwd(p, lb, sh, do, prev, *, rev, name, head=None):
    t = p.shape[0]
    nb, nc = t // TM, TM // CHUNK
    bmap = _blk_map(nb, rev, True)
    fcol = 14 if rev else 10
    has_prev = prev is not None
    odt = BF16 if has_prev else F32
    fused = head is not None

    def body(*refs):
        refs = list(refs)
        q_ref, f_ref, v_ref, lb_ref, sh_ref = refs[:5]
        pos = 5
        if fused:
            osum_ref, g_ref, dmix_ref, gain_ref = refs[5:9]
            pos = 9
        else:
            do_ref = refs[5]
            pos = 6
        if has_prev:
            pq_ref, pv_ref = refs[pos], refs[pos + 1]
            pos += 2
        dq_ref, df_ref, dv_ref, dlb_ref = refs[pos:pos + 4]
        pos += 4
        if fused:
            do_out, dg_ref, dgain_ref = refs[pos:pos + 3]
            pos += 3
        dst = refs[pos]

        @pl.when(pl.program_id(1) == 0)
        def _():
            dst[...] = jnp.zeros_like(dst)
            dlb_ref[...] = jnp.zeros_like(dlb_ref)

        if fused:
            @pl.when((pl.program_id(0) == 0) & (pl.program_id(1) == 0))
            def _():
                dgain_ref[...] = jnp.zeros_like(dgain_ref)

        cat = functools.partial(jnp.concatenate, axis=0)
        for hh in range(HGRN_HP):
            ln = slice(128 * hh, 128 * hh + 128)
            lbv = lb_ref[:, ln]
            qraw, fraw = q_ref[:, ln], f_ref[:, ln]
            q, k, lf, sq, sf, f = _hgrn_gates(qraw, fraw, lbv)
            tri, eq, ek, ei, eki, eb = _block_terms(lf, rev)
            qe, ke, qi, ki = q * eq, k * ek, q * ei, k * eki
            if fused:
                dov, dg, dgain = _headnorm_grad(osum_ref[:, ln], g_ref[:, ln], dmix_ref[:, ln], gain_ref[...])
                do_out[:, ln] = dov
                dg_ref[:, ln] = dg
                _acc_row(dgain_ref, 0, dgain)
            else:
                dov = do_ref[:, ln]
            qeb, keb, qib, kib, vb, dob = _bf(qe), _bf(ke), _bf(qi), _bf(ki), _bf(v_ref[:, ln]), _bf(dov)
            dv, dqe, dke, dqi = [None] * nc, [None] * nc, [None] * nc, [None] * nc
            for cc in range(nc):
                rows = slice(cc * CHUNK, (cc + 1) * CHUNK)
                a = jnp.where(tri, _dot_nt(qeb[rows], keb[rows]), 0.0)
                da = jnp.where(tri, _dot_nt(dob[rows], vb[rows]), 0.0)
                dv[cc] = _dot_tn(a, dob[rows])
                dqe[cc], dke[cc] = _dot(da, keb[rows]), _dot_tn(da, qeb[rows])
                dqi[cc] = _dot(dob[rows], sh_ref[hh, cc])
            dki, dbl = [None] * nc, [None] * nc
            ds = dst[hh]
            for cc in _chunk_order(rev, True):
                rows = slice(cc * CHUNK, (cc + 1) * CHUNK)
                ebc = eb[cc * CHUNK:cc * CHUNK + 1, :]
                dv[cc] = dv[cc] + _dot_nt(kib[rows], ds)
                dki[cc] = _dot(vb[rows], ds)
                dbl[cc] = jnp.broadcast_to(jnp.sum(dki[cc] * ki[rows], axis=0, keepdims=True)
                                           + jnp.sum(ds * sh_ref[hh, cc], axis=0, keepdims=True) * ebc, (CHUNK, 128))
                ds = ds * ebc + _dot_tn(dob[rows], qib[rows])
            dst[hh] = ds
            dqe, dke, dqi, dki, dv, dbl = cat(dqe), cat(dke), cat(dqi), cat(dki), cat(dv), cat(dbl)
            dq = dqe * eq + dqi * ei
            dk = dke * ek + dki * eki
            last = 0 if rev else CHUNK - 1
            db = dqe * qe - dke * ke + dqi * qi - dki * ki
            db = db + jnp.where((_iota(db.shape, 0) & (CHUNK - 1)) == last, dbl, 0.0)
            dlf = _chunk_cumsum(db, not rev)
            dqr = dq * (sq * (1.0 + qraw * (1.0 - sq)))
            dfv = dlf / f - dk
            dfr = dfv * (1.0 - lbv) * (sf * (1.0 - sf))
            dlb_ref[:, ln] += jnp.sum(dfv * (1.0 - sf), axis=0, keepdims=True)
            if has_prev:
                dqr = dqr + pq_ref[:, ln]
                dv = dv + pv_ref[:, ln]
            dq_ref[:, ln] = dqr.astype(odt)
            df_ref[:, ln] = dfr.astype(odt)
            dv_ref[:, ln] = dv.astype(odt)

    hp, wd = HGRN_HP, 128 * HGRN_HP

    def col(c0):
        return pl.BlockSpec((TM, wd), lambda h, n: (bmap(n), c0 // hp + h))

    oblk = pl.BlockSpec((TM, wd), lambda h, n: (bmap(n), h))
    ins = [p, p, p, lb, sh]
    specs = [col(6), col(fcol), col(18), pl.BlockSpec((1, wd), lambda h, n: (0, h)),
             pl.BlockSpec((hp, nc, 128, 128), lambda h, n: (h, bmap(n), 0, 0))]
    if fused:
        osum, dmix, gain = head
        ins += [osum, p, dmix, gain]
        specs += [oblk, col(22), pl.BlockSpec((TM, wd), lambda h, n: (bmap(n), 4 // hp + h)),
                  pl.BlockSpec((1, 128), lambda h, n: (0, 0))]
    else:
        ins.append(do); specs.append(oblk)
    if has_prev:
        ins += list(prev); specs += [oblk, oblk]
    out_specs = [oblk, oblk, oblk, pl.BlockSpec((1, wd), lambda h, n: (0, h))]
    out_shape = [jax.ShapeDtypeStruct((t, 512), odt)] * 3 + [jax.ShapeDtypeStruct((1, 512), F32)]
    if fused:
        out_specs += [oblk, oblk, pl.BlockSpec((8, 128), lambda h, n: (0, 0))]
        out_shape += [jax.ShapeDtypeStruct((t, 512), F32), jax.ShapeDtypeStruct((t, 512), BF16),
                      jax.ShapeDtypeStruct((8, 128), F32)]
    return _pcall(body, name=name, grid=(4 // hp, nb), in_specs=specs, out_specs=out_specs, out_shape=out_shape,
                  scratch_shapes=[pltpu.VMEM((hp, 128, 128), F32)])(*ins)


def _rope256(x, cos, sin):
    x1, x2 = x[:, 0:128], x[:, 128:256]
    return jnp.concatenate([x1 * cos - x2 * sin, x2 * cos + x1 * sin], axis=-1)


def _rope256_t(d, cos, sin):
    d1, d2 = d[:, 0:128], d[:, 128:256]
    return jnp.concatenate([d1 * cos + d2 * sin, d2 * cos - d1 * sin], axis=-1)


RET_DK, RET_DV, RET_H = 256, 512, 4
RET_KSCALE = RET_DK ** -0.5
RCH = TM
RET_HP = 2


def _ret_terms(lg, rev):
    r, c = _iota((RCH, RCH), 0), _iota((RCH, RCH), 1)
    rel = ((c - r) if rev else (r - c)).astype(F32)
    dmat = jnp.where(rel >= 0, jnp.exp(lg[:, 0:1] * jnp.maximum(rel, 0.0)), 0.0)
    pos = _iota((RCH, 1), 0).astype(F32)
    cnt = (RCH - pos) if rev else (pos + 1.0)
    ei = jnp.exp(lg * cnt)
    eki = jnp.exp(lg * (RCH - cnt))
    eb = jnp.exp(lg * float(RCH))
    return dmat, ei, eki, eb


def _ret_fwd(p, lgt, cos, sin, *, rev, name, ofw=None):
    t = p.shape[0]
    nb, nc = t // TM, TM // RCH
    bmap = _blk_map(nb, rev, False)
    fused = ofw is not None

    def body(*refs):
        q_ref, k_ref, v_ref, lg_ref, c_ref, s_ref = refs[:6]
        if fused:
            ofw_ref, g_ref, o_ref, sh_ref, mix_ref, st = refs[6:]
        else:
            o_ref, sh_ref, st = refs[6:]

        @pl.when(pl.program_id(1) == 0)
        def _():
            st[...] = jnp.zeros_like(st)
        for hh in range(RET_HP):
            qc, vc = slice(RET_DK * hh, RET_DK * (hh + 1)), slice(RET_DV * hh, RET_DV * (hh + 1))
            dmat, ei, eki, eb = _ret_terms(lg_ref[hh], rev)
            for cc in _chunk_order(rev, False, nc):
                rows = slice(cc * RCH, (cc + 1) * RCH)
                cosv, sinv = c_ref[rows, :], s_ref[rows, :]
                q = _rope256(q_ref[rows, qc].astype(F32), cosv, sinv)
                k = _rope256(k_ref[rows, qc].astype(F32), cosv, sinv) * RET_KSCALE
                v = v_ref[rows, vc]
                s0 = st[hh]
                sh_ref[hh, cc] = s0.astype(BF16)
                a = _dot_nt(q, k) * dmat
                o = _dot(a, v) + _dot_nt(q * ei, s0)
                st[hh] = s0 * eb + _dot_tn(v, k * eki)
                if fused:
                    o = o + ofw_ref[rows, vc]
                    mix_ref[rows, vc] = _headnorm_apply(o, g_ref[rows, vc].astype(F32), None)
                o_ref[rows, vc] = o

    hp = RET_HP
    tab = pl.BlockSpec((TM, 128), lambda h, n: (bmap(n), 0))
    oblk = pl.BlockSpec((TM, hp * RET_DV), lambda h, n: (bmap(n), h))
    ins = [p, p, p, lgt, cos, sin]
    specs = [pl.BlockSpec((TM, hp * RET_DK), lambda h, n: (bmap(n), h)),
             pl.BlockSpec((TM, hp * RET_DK), lambda h, n: (bmap(n), RET_H // hp + h)),
             pl.BlockSpec((TM, hp * RET_DV), lambda h, n: (bmap(n), RET_H // hp + h)),
             pl.BlockSpec((hp, 1, RET_DK), lambda h, n: (h, 0, 0)), tab, tab]
    out_specs = [oblk, pl.BlockSpec((hp, nc, RET_DV, RET_DK), lambda h, n: (h, bmap(n), 0, 0))]
    out_shape = [jax.ShapeDtypeStruct((t, RET_H * RET_DV), F32),
                 jax.ShapeDtypeStruct((RET_H, t // RCH, RET_DV, RET_DK), BF16)]
    if fused:
        ins += [ofw, p]
        specs += [oblk, pl.BlockSpec((TM, hp * RET_DV), lambda h, n: (bmap(n), 2 * RET_H // hp + h))]
        out_specs.append(oblk)
        out_shape.append(jax.ShapeDtypeStruct((t, RET_H * RET_DV), BF16))
    return _pcall(body, name=name, grid=(RET_H // hp, nb), in_specs=specs, out_specs=out_specs, out_shape=out_shape,
                  scratch_shapes=[pltpu.VMEM((hp, RET_DV, RET_DK), F32)])(*ins)


def _ret_bwd(p, lgt, cos, sin, sh, do, prev, *, rev, name, head=None):
    t = p.shape[0]
    nb, nc = t // TM, TM // RCH
    bmap = _blk_map(nb, rev, True)
    has_prev = prev is not None
    odt = BF16 if has_prev else F32
    fused = head is not None

    def body(*refs):
        refs = list(refs)
        q_ref, k_ref, v_ref, lg_ref, c_ref, s_ref, sh_ref = refs[:7]
        if fused:
            osum_ref, g_ref, dmix_ref = refs[7:10]
            pos = 10
        else:
            do_ref = refs[7]
            pos = 8
        if has_prev:
            pq_ref, pk_ref, pv_ref = refs[pos:pos + 3]
            pos += 3
        dq_ref, dk_ref, dv_ref = refs[pos:pos + 3]
        pos += 3
        if fused:
            do_out, dg_ref = refs[pos:pos + 2]
            pos += 2
        dst = refs[pos]

        @pl.when(pl.program_id(1) == 0)
        def _():
            dst[...] = jnp.zeros_like(dst)

        for hh in range(RET_HP):
            qc, vc = slice(RET_DK * hh, RET_DK * (hh + 1)), slice(RET_DV * hh, RET_DV * (hh + 1))
            dmat, ei, eki, eb = _ret_terms(lg_ref[hh], rev)
            for cc in _chunk_order(rev, True, nc):
                rows = slice(cc * RCH, (cc + 1) * RCH)
                cosv, sinv = c_ref[rows, :], s_ref[rows, :]
                q = _rope256(q_ref[rows, qc].astype(F32), cosv, sinv)
                k = _rope256(k_ref[rows, qc].astype(F32), cosv, sinv) * RET_KSCALE
                v = v_ref[rows, vc]
                if fused:
                    dov, dg, _ = _headnorm_grad(osum_ref[rows, vc], g_ref[rows, vc].astype(F32), dmix_ref[rows, vc], None)
                    do_out[rows, vc] = dov
                    dg_ref[rows, vc] = dg
                else:
                    dov = do_ref[rows, vc]
                s0 = sh_ref[hh, cc]
                dsc = dst[hh]
                qi, ki = q * ei, k * eki
                a = _dot_nt(q, k) * dmat
                da = _dot_nt(dov, v) * dmat
                dv = _dot_tn(a, dov) + _dot_nt(ki, dsc)
                dqs = _dot(da, k) + _dot(dov, s0) * ei
                dks = _dot_tn(da, q) + _dot(v, dsc) * eki
                dst[hh] = dsc * eb + _dot_tn(dov, qi)
                dq = _rope256_t(dqs, cosv, sinv)
                dk = _rope256_t(dks * RET_KSCALE, cosv, sinv)
                if has_prev:
                    dq = dq + pq_ref[rows, qc]
                    dk = dk + pk_ref[rows, qc]
                    dv = dv + pv_ref[rows, vc]
                dq_ref[rows, qc] = dq.astype(odt)
                dk_ref[rows, qc] = dk.astype(odt)
                dv_ref[rows, vc] = dv.astype(odt)

    hp = RET_HP
    tab = pl.BlockSpec((TM, 128), lambda h, n: (bmap(n), 0))
    qblk = pl.BlockSpec((TM, hp * RET_DK), lambda h, n: (bmap(n), h))
    vblk = pl.BlockSpec((TM, hp * RET_DV), lambda h, n: (bmap(n), h))
    ins = [p, p, p, lgt, cos, sin, sh]
    specs = [qblk, pl.BlockSpec((TM, hp * RET_DK), lambda h, n: (bmap(n), RET_H // hp + h)),
             pl.BlockSpec((TM, hp * RET_DV), lambda h, n: (bmap(n), RET_H // hp + h)),
             pl.BlockSpec((hp, 1, RET_DK), lambda h, n: (h, 0, 0)), tab, tab,
             pl.BlockSpec((hp, nc, RET_DV, RET_DK), lambda h, n: (h, bmap(n), 0, 0))]
    if fused:
        osum, dmix = head
        ins += [osum, p, dmix]
        specs += [vblk, pl.BlockSpec((TM, hp * RET_DV), lambda h, n: (bmap(n), 2 * RET_H // hp + h)), vblk]
    else:
        ins.append(do); specs.append(vblk)
    if has_prev:
        ins += list(prev); specs += [qblk, qblk, vblk]
    out_specs = [qblk, qblk, vblk]
    out_shape = [jax.ShapeDtypeStruct((t, RET_H * RET_DK), odt), jax.ShapeDtypeStruct((t, RET_H * RET_DK), odt),
                 jax.ShapeDtypeStruct((t, RET_H * RET_DV), odt)]
    if fused:
        out_specs += [vblk, vblk]
        out_shape += [jax.ShapeDtypeStruct((t, RET_H * RET_DV), F32), jax.ShapeDtypeStruct((t, RET_H * RET_DV), BF16)]
    return _pcall(body, name=name, grid=(RET_H // hp, nb), in_specs=specs, out_specs=out_specs, out_shape=out_shape,
                  scratch_shapes=[pltpu.VMEM((hp, RET_DV, RET_DK), F32)])(*ins)


def _rope_tables(lc, l):
    tt = jnp.arange(l)
    row, colp = (tt // 64).astype(F32), (tt % 64).astype(F32)
    inv = 10000.0 ** (-jnp.arange(16, dtype=F32) / 16)
    ang = jnp.concatenate([row[:, None] * inv, colp[:, None] * inv], axis=-1)
    ang = jnp.concatenate([jnp.zeros((lc, 32), F32), ang], axis=0)
    acos, asin = jnp.tile(jnp.cos(ang), (1, 4)), jnp.tile(jnp.sin(ang), (1, 4))
    theta = 1.0 / (10000.0 ** jnp.linspace(0.0, 1.0, 128, dtype=F32))
    rang = jnp.arange(l, dtype=F32)[:, None] * theta
    rang = jnp.concatenate([jnp.zeros((lc, 128), F32), rang], axis=0)
    return acos, asin, jnp.cos(rang), jnp.sin(rang)


class _Weights:
    def __init__(self, w):
        self.w = w

    def first(self, after):
        return self.w

    def rest_landed(self, after):
        pass

    def rest(self, after):
        return self.w

    def early_grads(self, grads):
        return jnp.zeros((8, 128), F32)


def _local_step(x0, target, mods, ng, wsrc, small):
    t, d = x0.shape
    l = target.shape[0]
    lc = t - l
    acos, asin, rcos, rsin = _rope_tables(lc, l)
    lg_fw = jnp.log(1.0 - 2.0 ** (-5.0 - jnp.arange(RET_H, dtype=F32)))
    lgt_fw = jnp.broadcast_to(lg_fw[:, None, None], (RET_H, 1, RET_DK))
    lgt_bw = jnp.broadcast_to(lg_fw[::-1][:, None, None], (RET_H, 1, RET_DK))
    gq, gk, sink, gain, lb = small['gq'], small['gk'], small['sink'], small['gain'], small['lb']

    (h1,) = _row_fwd(x0, mods, g=ng[0], shift=0, scale=1, name='l0_norm1')
    w = wsrc.first(h1)
    p0 = _mm_nn(h1, w['even_in'], name='l0_in')
    kp = _kprep_fwd(p0, gk, acos, asin, name='l0_kprep')
    att = _attn_fwd(p0, kp, gq, sink, acos, asin, lc=lc, name='l0_attn')
    hof, hsf = _hgrn_fwd(p0, lb, rev=False, name='l0_hgrn_f')
    wsrc.rest_landed(hof)
    hos, hsb, bmix = _hgrn_fwd(p0, lb, rev=True, name='l0_hgrn_b', ofw=hof, gain=gain)
    mix0 = jnp.concatenate([att, bmix], axis=1)
    y0 = _mm_nn(mix0, w['even_out'], name='l0_out')
    x1, h2 = _row_fwd(x0, mods, y=y0, gate=2, g=ng[1], shift=3, scale=4, name='l0_norm2')
    w = dict(w, **wsrc.rest(h2))
    u0, a0 = _ffn_in(h2, w['ffn_in'], lead=0, name='ffn_in')
    z0 = _mm_nn(a0, w['ffn_out'], lead=0, name='ffn_out')
    x2, h3 = _row_fwd(x1, mods, y=z0, gate=5, g=ng[2], shift=12, scale=13, name='l1_norm1')
    p1 = _mm_nn(h3, w['odd_in'], out_dtype=BF16, name='l1_in')
    rof, rsf = _ret_fwd(p1, lgt_fw, rcos, rsin, rev=False, name='l1_ret_f')
    ros, rsb, mix1 = _ret_fwd(p1, lgt_bw, rcos, rsin, rev=True, name='l1_ret_b', ofw=rof)
    y1 = _mm_nn(mix1, w['odd_out'], name='l1_out')
    x3, h4 = _row_fwd(x2, mods, y=y1, gate=14, g=ng[3], shift=15, scale=16, name='l1_norm2')
    u1, a1 = _ffn_in(h4, w['ffn_in'], lead=1, name='ffn_in')
    z1 = _mm_nn(a1, w['ffn_out'], lead=1, name='ffn_out')
    loss, dx4, dz1, s_fin = _row_final(x3, z1, mods, target, gate=17, name='loss')

    du1 = _ffn_dx(dz1, w['ffn_out'], u1, lead=1, name='ffn_out_dx')
    g_ffn_out1 = _mm_tn(a1, dz1, name='ffn_out_dw')
    dh4 = _mm_nt(du1, w['ffn_in'], lead=1, name='ffn_in_dx')
    g_ffn_in1 = _mm_tn(h4, du1, name='ffn_in_dw')
    dx3, dy1, s_l1n2 = _row_bwd(x3, dx4, dh4, mods, ng[3], shift=15, scale=16, y=y1, gate=14, name='l1_norm2_bwd')
    dmix1 = _mm_nt(dy1, w['odd_out'], name='l1_out_dx')
    g_odd_out = _mm_tn(mix1, dy1, name='l1_out_dw')
    rdq, rdk, rdv, rdo, rdg = _ret_bwd(p1, lgt_fw, rcos, rsin, rsf, None, None, rev=False, name='l1_ret_f_bwd',
                                       head=(ros, dmix1))
    rdq, rdk, rdv = _ret_bwd(p1, lgt_bw, rcos, rsin, rsb, rdo, (rdq, rdk, rdv), rev=True, name='l1_ret_b_bwd')
    dp1 = jnp.concatenate([rdq, rdk, rdv, rdg], axis=1)
    dh3 = _mm_nt(dp1, w['odd_in'], name='l1_in_dx')
    g_odd_in = _mm_tn(h3, dp1, name='l1_in_dw')
    mods = mods + wsrc.early_grads(dict(ffn_in1=g_ffn_in1, ffn_out1=g_ffn_out1, odd_in=g_odd_in, odd_out=g_odd_out))[0, 0]
    dx2, dz0, s_l1n1 = _row_bwd(x2, dx3, dh3, mods, ng[2], shift=12, scale=13, y=z0, gate=5, name='l1_norm1_bwd')
    du0 = _ffn_dx(dz0, w['ffn_out'], u0, lead=0, name='ffn_out_dx')
    g_ffn_out0 = _mm_tn(a0, dz0, name='ffn_out_dw')
    dh2 = _mm_nt(du0, w['ffn_in'], lead=0, name='ffn_in_dx')
    g_ffn_in0 = _mm_tn(h2, du0, name='ffn_in_dw')
    dx1, dy0, s_l0n2 = _row_bwd(x1, dx2, dh2, mods, ng[1], shift=3, scale=4, y=y0, gate=2, name='l0_norm2_bwd')
    dmix0 = _mm_nt(dy0, w['even_out'], name='l0_out_dx')
    g_even_out = _mm_tn(mix0, dy0, name='l0_out_dw')
    hq, hff, hv, dlb_f, hdo, hdg, s_gain = _hgrn_bwd(p0, lb, hsf, None, None, rev=False, name='l0_hgrn_f_bwd',
                                                     head=(hos, dmix0, gain))
    hq, hfb, hv, dlb_b = _hgrn_bwd(p0, lb, hsb, hdo, (hq, hv), rev=True, name='l0_hgrn_b_bwd')
    adq, dkp, adv, s_gq, s_sink = _attn_bwd(p0, kp, gq, sink, acos, asin, dmix0, lc=lc, name='l0_attn_bwd')
    dkv, s_gk = _kprep_bwd(p0, gk, acos, asin, dkp, adv, name='l0_kprep_bwd')
    dp0 = jnp.concatenate([adq, dkv, hq, _bf(hff), hfb, hv, hdg], axis=1)
    dh1 = _mm_nt(dp0, w['even_in'], name='l0_in_dx')
    g_even_in = _mm_tn(h1, dp0, name='l0_in_dw')
    dx0, s_l0n1 = _row_bwd(x0, dx1, dh1, mods, ng[0], shift=0, scale=1, latent_only=True, name='l0_norm1_bwd')

    grads = dict(ffn_in0=g_ffn_in0, ffn_in1=g_ffn_in1, ffn_out0=g_ffn_out0, ffn_out1=g_ffn_out1,
                 even_in=g_even_in, even_out=g_even_out, odd_in=g_odd_in, odd_out=g_odd_out)
    sums = dict(fin=s_fin, l1n2=s_l1n2, l1n1=s_l1n1, l0n2=s_l0n2, l0n1=s_l0n1, gain=s_gain, gq=s_gq, gk=s_gk,
                sink=s_sink, dlb_f=dlb_f, dlb_b=dlb_b)
    return loss, dx0, grads, sums


def _place():
    return lax.axis_index("x"), lax.axis_index("y"), lax.axis_index("c")


def _ag8(blk, *, name):
    r, c = blk.shape
    flips = [(dx, dy, dc) for dx in (0, 1) for dy in (0, 1) for dc in (0, 1) if (dx, dy, dc) != (0, 0, 0)]

    def body(x_ref, out_ref, send_sems, recv_sems, local_sem):
        ax, ay, ac = _place()
        me = 4 * ax + 2 * ay + ac
        mine = pltpu.make_async_copy(x_ref, out_ref.at[me], local_sem)
        mine.start()
        sent = []
        for k, (dx, dy, dc) in enumerate(flips):
            peer = (lax.rem(ax + dx, 2), lax.rem(ay + dy, 2), lax.rem(ac + dc, 2))
            cp = pltpu.make_async_remote_copy(src_ref=x_ref, dst_ref=out_ref.at[me], send_sem=send_sems.at[k],
                                              recv_sem=recv_sems.at[k], device_id=peer, device_id_type=MESH)
            cp.start()
            sent.append((cp, 4 * peer[0] + 2 * peer[1] + peer[2]))
        for k, (cp, pidx) in enumerate(sent):
            pltpu.make_async_remote_copy(src_ref=x_ref, dst_ref=out_ref.at[pidx], send_sem=send_sems.at[k],
                                         recv_sem=recv_sems.at[k], device_id=(ax, ay, ac),
                                         device_id_type=MESH).wait_recv()
        for cp, _ in sent:
            cp.wait_send()
        mine.wait()

    return _pcall(
        body, name=name,
        in_specs=[pl.BlockSpec(memory_space=pltpu.VMEM)],
        out_specs=pl.BlockSpec(memory_space=pltpu.VMEM),
        out_shape=jax.ShapeDtypeStruct((8, r, c), blk.dtype),
        scratch_shapes=[pltpu.SemaphoreType.DMA((7,)), pltpu.SemaphoreType.DMA((7,)), pltpu.SemaphoreType.DMA],
    )(blk)


_HBM = pl.BlockSpec(memory_space=pltpu.HBM)
_SEM = pl.BlockSpec(memory_space=pltpu.SEMAPHORE)
_DATAFLOW = pltpu.SideEffectType.DATAFLOW_SIDE_EFFECTING


def _split_start(bufs, plan, k, *, name):
    n = len(bufs)

    def body(*refs):
        ins, send_sems, recv_sems, token = refs[:n], refs[n], refs[n + 1], refs[2 * n + 2]
        for i, (src, dst, dev) in enumerate(plan(ins)):
            pltpu.make_async_remote_copy(src_ref=src, dst_ref=dst, send_sem=send_sems.at[i], recv_sem=recv_sems.at[i],
                                         device_id=dev, device_id_type=MESH).start()
        token[...] = jnp.zeros_like(token)

    res = _pcall(
        body, name=name,
        out_shape=(pltpu.SemaphoreType.DMA((k,)), pltpu.SemaphoreType.DMA((k,)),
                   *[pltpu.HBM(b.shape, b.dtype) for b in bufs], jax.ShapeDtypeStruct((8, 128), F32)),
        in_specs=[_HBM] * n, out_specs=(_SEM, _SEM, *[_HBM] * n, pl.BlockSpec(memory_space=pltpu.VMEM)),
        input_output_aliases={i: 2 + i for i in range(n)},
        compiler_params=pltpu.CompilerParams(has_side_effects=_DATAFLOW),
    )(*[pltpu.with_memory_space_constraint(b, pltpu.HBM) for b in bufs])
    return res[0], res[1], list(res[2:2 + n]), res[2 + n]


def _split_wait(bufs, send_sems, recv_sems, plan, after, *, name):
    n = len(bufs)

    def body(*refs):
        ins, ssem, rsem = refs[:n], refs[n], refs[n + 1]
        for i, (src, dst, dev) in enumerate(plan(ins)):
            cp = pltpu.make_async_remote_copy(src_ref=src, dst_ref=dst, send_sem=ssem.at[i], recv_sem=rsem.at[i],
                                              device_id=dev, device_id_type=MESH)
            cp.wait_send()
            cp.wait_recv()

    res = _pcall(
        body, name=name, out_shape=tuple(pltpu.HBM(b.shape, b.dtype) for b in bufs),
        in_specs=[_HBM] * n + [_SEM, _SEM, pl.BlockSpec(memory_space=pl.ANY)], out_specs=tuple([_HBM] * n),
        input_output_aliases={i: i for i in range(n)},
        compiler_params=pltpu.CompilerParams(has_side_effects=_DATAFLOW),
    )(*bufs, send_sems, recv_sems, after)
    return list(res)


_CHIP_FLIPS = [(1, 0), (0, 1), (1, 1)]


class _GatheredWeights:
    FIRST = ('even_in', 'even_out')
    REST = ('ffn_in', 'ffn_out', 'odd_in', 'odd_out')

    def __init__(self, shards, reducer):
        self.shards = shards
        self.early_grads = functools.partial(reducer.start, 'early')
        self.ici = {}
        for grp, names in (('first', self.FIRST), ('rest', self.REST)):
            src = [shards[nm].reshape(2, shards[nm].shape[0] // 2, shards[nm].shape[1]) for nm in names]
            land = [lax.empty((4,) + a.shape, a.dtype) for a in src]
            m = len(names)
            sends, recvs, bufs, token = _split_start(src + land, functools.partial(self._ici_plan, m, True), 3 * m,
                                                     name='gather_' + grp + '_ici_start')
            self.ici[grp] = (sends, recvs, bufs, m)
            self.token = token if grp == 'first' else self.token + token
        self.rest_d2d = None

    @staticmethod
    def _ici_plan(m, sending, refs):
        ax, ay, ac = _place()
        s = 2 * ax + ay
        out = []
        for a in range(m):
            for dx, dy in _CHIP_FLIPS:
                px, py = lax.rem(ax + dx, 2), lax.rem(ay + dy, 2)
                slot = s if sending else 2 * px + py
                out.append((refs[a].at[ac], refs[m + a].at[slot, ac], (px, py, ac)))
        return out

    @staticmethod
    def _d2d_plan(m, sending, refs):
        ax, ay, ac = _place()
        out = []
        for a in range(m):
            for dx, dy in _CHIP_FLIPS:
                sp = 2 * lax.rem(ax + dx, 2) + lax.rem(ay + dy, 2)
                out.append((refs[a].at[sp, ac], refs[a].at[sp, ac if sending else 1 - ac], (ax, ay, 1 - ac)))
        return out

    def _landed(self, grp, after):
        sends, recvs, bufs, m = self.ici[grp]
        bufs = _split_wait(bufs, sends, recvs, functools.partial(self._ici_plan, m, False), after,
                           name='gather_' + grp + '_ici_wait')
        sends, recvs, land, _ = _split_start(bufs[m:], functools.partial(self._d2d_plan, m, True), 3 * m,
                                             name='gather_' + grp + '_d2d_start')
        return sends, recvs, land, m

    def _full(self, grp, names, d2d, after):
        sends, recvs, land, m = d2d
        land = _split_wait(land, sends, recvs, functools.partial(self._d2d_plan, m, False), after,
                           name='gather_' + grp + '_d2d_wait')
        s = 2 * lax.axis_index("x") + lax.axis_index("y")
        slot = lax.broadcasted_iota(jnp.int32, (4, 1, 1), 0)
        return {nm: _from_shards(nm, jnp.where(slot == s, self.shards[nm][None], g.reshape((4,) + self.shards[nm].shape)))
                for nm, g in zip(names, land)}

    def first(self, after):
        return self._full('first', self.FIRST, self._landed('first', after), after)

    def rest_landed(self, after):
        self.rest_d2d = self._landed('rest', after)

    def rest(self, after):
        return self._full('rest', self.REST, self.rest_d2d, after)


def _to_sibling(arrs, *, name):
    n = len(arrs)

    def body(*refs):
        ins, outs = refs[:n], refs[n:2 * n]
        send_sems, recv_sems = refs[2 * n:]
        ax, ay, ac = _place()
        cps = [pltpu.make_async_remote_copy(src_ref=ins[a], dst_ref=outs[a], send_sem=send_sems.at[a],
                                            recv_sem=recv_sems.at[a], device_id=(ax, ay, 1 - ac),
                                            device_id_type=MESH) for a in range(n)]
        for cp in cps:
            cp.start()
        for cp in cps:
            cp.wait_recv()
        for cp in cps:
            cp.wait_send()

    hbm = pl.BlockSpec(memory_space=pl.ANY)
    return _pcall(
        body, name=name, in_specs=[hbm] * n, out_specs=[hbm] * n,
        out_shape=[jax.ShapeDtypeStruct(a.shape, a.dtype) for a in arrs],
        scratch_shapes=[pltpu.SemaphoreType.DMA((n,))] * 2,
    )(*arrs)


def _mod_fwd(cond_raw, mw, mb, *, name):
    _, d, n = mw.shape

    def body(c_ref, w_ref, b_ref, o_ref):
        cv = c_ref[...]
        o_ref[...] = _dot(cv * _sigmoid(cv), w_ref[...]) + b_ref[...]

    return _pcall(
        body, name=name, grid=(2,),
        in_specs=[pl.BlockSpec((16, d), lambda l: (0, 0)), pl.BlockSpec((None, d, n), lambda l: (l, 0, 0)),
                  pl.BlockSpec((None, 1, n), lambda l: (l, 0, 0))],
        out_specs=pl.BlockSpec((None, 16, n), lambda l: (l, 0, 0)),
        out_shape=jax.ShapeDtypeStruct((2, 16, n), F32),
    )(cond_raw, mw, mb)


def _mod_bwd(cond_raw, dms, mw, *, name):
    _, d, n = mw.shape

    def body(c_ref, dm_ref, w_ref, gw_ref, dc_ref):
        @pl.when(pl.program_id(0) == 0)
        def _():
            dc_ref[...] = jnp.zeros_like(dc_ref)
        cv = c_ref[...]
        gw_ref[...] = _dot_tn(cv * _sigmoid(cv), dm_ref[...])
        dc_ref[...] += _dot_nt(dm_ref[...], w_ref[...])

    return _pcall(
        body, name=name, grid=(2,),
        in_specs=[pl.BlockSpec((16, d), lambda l: (0, 0)), pl.BlockSpec((None, 16, n), lambda l: (l, 0, 0)),
                  pl.BlockSpec((None, d, n), lambda l: (l, 0, 0))],
        out_specs=[pl.BlockSpec((None, d, n), lambda l: (l, 0, 0)), pl.BlockSpec((16, d), lambda l: (0, 0))],
        out_shape=[jax.ShapeDtypeStruct((2, d, n), F32), jax.ShapeDtypeStruct((16, d), F32)],
    )(cond_raw, dms, mw)


def _lb_fwd(hgrn_lb, *, name):
    def body(a_ref, o_ref):
        a0, a1 = a_ref[0:1, :], a_ref[1:2, :]
        m = jnp.maximum(a0, a1)
        e0, e1 = jnp.exp(a0 - m), jnp.exp(a1 - m)
        o_ref[...] = e0 / (e0 + e1)

    return _pcall(body, name=name, out_shape=jax.ShapeDtypeStruct((1, hgrn_lb.shape[1]), F32))(hgrn_lb)


PACK_TILES = ('l0n1', 'l0n2', 'l1n1', 'l1n2', 'fin', 'gq', 'gk', 'gain', 'dlb_f', 'dlb_b', 'sink')
PACK_ROW = {nm: 8 * i for i, nm in enumerate(PACK_TILES)}
MOD_SOURCE = ((('l0n1', 0), ('l0n1', 1), ('l0n2', 2), ('l0n2', 0), ('l0n2', 1), ('l1n1', 2)),
              (('l1n1', 0), ('l1n1', 1), ('l1n2', 2), ('l1n2', 0), ('l1n2', 1), ('fin', 2)))


def _small_finalize(gath, lb_pad, *, name):
    d = gath.shape[2]

    def body(g_ref, lb_ref, small_ref, glb_ref, gmb_ref, dm_ref):
        tot = g_ref[0]
        for e in range(1, 8):
            tot = tot + g_ref[e]

        def row(nm, r=0):
            return tot[PACK_ROW[nm] + r:PACK_ROW[nm] + r + 1, :]

        for k, nm in enumerate(('l0n1', 'l0n2', 'l1n1', 'l1n2')):
            small_ref[k:k + 1, :] = row(nm, 3) + row(nm, 7)
        for k, nm in ((4, 'gq'), (5, 'gk')):
            small_ref[k:k + 1, :] = row(nm) + pltpu.roll(row(nm), d - 64, 1)
        small_ref[6:7, :] = row('gain')
        small_ref[7:8, :] = row('sink')
        lbv = lb_ref[...]
        g0 = (row('dlb_f') + row('dlb_b')) * lbv * (1.0 - lbv)
        glb_ref[...] = jnp.zeros_like(glb_ref)
        glb_ref[0:1, :] = g0
        glb_ref[1:2, :] = -g0
        dm_ref[...] = jnp.zeros_like(dm_ref)
        for l in range(2):
            for part in range(6):
                nm, r = MOD_SOURCE[l][part]
                gmb_ref[l * 6 + part:l * 6 + part + 1, :] = row(nm, r) + row(nm, r + 4)
                rl = PACK_ROW[nm] + r + 4
                for e in range(8):
                    dm_ref[l, part, e:e + 1, :] = g_ref[e, rl:rl + 1, :]
                dm_ref[l, part, 8:9, :] = row(nm, r)

    return _pcall(
        body, name=name,
        out_shape=[jax.ShapeDtypeStruct((8, d), F32), jax.ShapeDtypeStruct((8, d), F32),
                   jax.ShapeDtypeStruct((12, d), F32), jax.ShapeDtypeStruct((2, 6, 16, d), F32)],
    )(gath, lb_pad)


def _cctx_grad(gath, c_ctx2, *, name):
    def body(g_ref, c_ref, o_ref):
        tot = ((g_ref[0, 0:1, :] + g_ref[2, 0:1, :]) + g_ref[4, 0:1, :]) + g_ref[6, 0:1, :]
        cv = c_ref[...]
        s = _sigmoid(cv)
        o_ref[...] = tot * (s * (1.0 + cv * (1.0 - s)))

    return _pcall(body, name=name, out_shape=jax.ShapeDtypeStruct(c_ctx2.shape, F32))(gath, c_ctx2)


def _row_block(r, c, limit=256 * 1024):
    best = None
    for br in range(16, r + 1, 16):
        if r % br == 0 and br * c <= limit:
            best = br
    return best if best is not None else r


def _sum4(parts, *, name):
    _, r, c = parts.shape
    br = _row_block(r, c)

    def body(p_ref, o_ref):
        p = [p_ref[k].astype(F32) for k in range(4)]
        o_ref[...] = ((p[0] + p[1]) + p[2]) + p[3]

    return _pcall(body, name=name, grid=(r // br,),
                  in_specs=[pl.BlockSpec((4, br, c), lambda i: (0, i, 0))],
                  out_specs=pl.BlockSpec((br, c), lambda i: (i, 0)),
                  out_shape=jax.ShapeDtypeStruct((r, c), F32))(parts)


def _add2(a, b, *, name):
    r, c = a.shape
    br = _row_block(r, c)

    def body(a_ref, b_ref, o_ref):
        o_ref[...] = (a_ref[...].astype(F32) + b_ref[...].astype(F32)).astype(BF16)

    blk = pl.BlockSpec((br, c), lambda i: (i, 0))
    return _pcall(body, name=name, grid=(r // br,), in_specs=[blk, blk], out_specs=blk,
                  out_shape=jax.ShapeDtypeStruct((r, c), BF16))(a, b)


def _adam(w, gs, m, v, *, name):
    r, c = w.shape
    br = _row_block(r, c)
    ng = len(gs)
    c1 = 1.0 - ADAM_B1 ** ADAM_STEP
    c2 = 1.0 - ADAM_B2 ** ADAM_STEP

    def body(*refs):
        w_ref, m_ref, v_ref = refs[0], refs[1 + ng], refs[2 + ng]
        outs = refs[3 + ng:]
        g = refs[1][...]
        for k in range(1, ng):
            g = g + refs[1 + k][...]
        mn = ADAM_B1 * m_ref[...] + (1.0 - ADAM_B1) * g
        vn = ADAM_B2 * v_ref[...] + (1.0 - ADAM_B2) * (g * g)
        if ng > 1:
            outs[0][...] = g
        d_out, m_out, v_out = outs[-3:]
        m_out[...] = mn
        v_out[...] = vn
        d_out[...] = -ADAM_LR * ((mn / c1) / (jnp.sqrt(vn / c2) + ADAM_EPS) + ADAM_WD * w_ref[...])

    blk = pl.BlockSpec((br, c), lambda i: (i, 0))
    nout = 4 if ng > 1 else 3
    res = _pcall(body, name=name, grid=(r // br,), in_specs=[blk] * (3 + ng), out_specs=[blk] * nout,
                 out_shape=[jax.ShapeDtypeStruct((r, c), F32)] * nout)(w, *gs, m, v)
    return list(res) if ng > 1 else [gs[0]] + list(res)


def _grad_halves(name, g, ac):
    if name.endswith('_in'):
        n = g.shape[1] // 4
        if name == 'ffn_in':
            assert n == FFN_BK
        order = _ffn_order(g.shape[1]) if name == 'ffn_in' else range(4)
        v = jnp.stack([g[:, b * n:(b + 1) * n] for b in order])
        per = [v[:, :g.shape[0] // 2], v[:, g.shape[0] // 2:]]
    else:
        k4, n = g.shape
        v = g.reshape(4, 2, k4 // 8, n)
        per = [v[:, 0], v[:, 1]]
    first = ac == 0
    return _bf(jnp.where(first, per[0], per[1])), _bf(jnp.where(first, per[1], per[0]))


class _GradReducer:
    def __init__(self):
        self.flight = {}

    @staticmethod
    def _plan(m, sending, refs):
        ax, ay, ac = _place()
        s = 2 * ax + ay
        out = []
        for a in range(m):
            for dx, dy in _CHIP_FLIPS:
                px, py = lax.rem(ax + dx, 2), lax.rem(ay + dy, 2)
                sp = 2 * px + py
                out.append((refs[a].at[sp], refs[m + a].at[s if sending else sp], (px, py, ac)))
        return out

    def start(self, grp, grads):
        ac = lax.axis_index("c")
        names = list(grads)
        halves = [_grad_halves(nm.rstrip('01'), grads[nm], ac) for nm in names]
        theirs = _to_sibling([h[1] for h in halves], name='swap_core_halves_' + grp)
        pair = [_add2(h[0].reshape(-1, b.shape[-1]), b.reshape(-1, b.shape[-1]), name='add_cores').reshape(b.shape)
                for h, b in zip(halves, theirs)]
        m = len(names)
        land = [lax.empty(a.shape, a.dtype) for a in pair]
        sends, recvs, bufs, token = _split_start(pair + land, functools.partial(self._plan, m, True), 3 * m,
                                                 name='scatter_' + grp + '_start')
        self.flight[grp] = (names, sends, recvs, bufs)
        return token

    def finish(self, grp, after):
        names, sends, recvs, bufs = self.flight.pop(grp)
        m = len(names)
        bufs = _split_wait(bufs, sends, recvs, functools.partial(self._plan, m, False), after,
                           name='scatter_' + grp + '_wait')
        ac = lax.axis_index("c")
        s = 2 * lax.axis_index("x") + lax.axis_index("y")
        slot = lax.broadcasted_iota(jnp.int32, (4, 1, 1), 0)
        half_sums = [_sum4(jnp.where(slot == s, p, l), name='sum_chips') for p, l in zip(bufs[:m], bufs[m:])]
        other = _to_sibling(half_sums, name='gather_core_halves_' + grp)
        return {nm: jnp.concatenate([jnp.where(ac == 0, f, o), jnp.where(ac == 0, o, f)], axis=0)
                for nm, f, o in zip(names, half_sums, other)}


def _from_shards(name, g):
    _, r, n = g.shape
    if name == 'ffn_in':
        assert n == FFN_BK
        v = g.reshape(4, 2, r // 2, n)
        return jnp.concatenate([v[b] for b in _ffn_order(4 * n)], axis=-1)
    if name == 'ffn_out':
        return g.reshape(4, 2, r // 2, n).transpose(1, 0, 2, 3).reshape(2, 2 * r, n)
    if name in ('even_in', 'odd_in'):
        return jnp.concatenate([g[b] for b in range(4)], axis=-1)
    return g.reshape(4 * r, n)


def kernel(x, c, ctx, c_ctx, mod_w, mod_b, norm_g, ffn_w_in, ffn_w_out, even_w_in, even_w_out, attn_qk_norm_g, attn_sink, hgrn_out_norm_g, hgrn_lb, odd_w_in, odd_w_out, loss_target, m_c_ctx, m_mod_w, m_mod_b, m_norm_g, m_ffn_w_in, m_ffn_w_out, m_even_w_in, m_even_w_out, m_attn_qk_norm_g, m_attn_sink, m_hgrn_out_norm_g, m_hgrn_lb, m_odd_w_in, m_odd_w_out, v_c_ctx, v_mod_w, v_mod_b, v_norm_g, v_ffn_w_in, v_ffn_w_out, v_even_w_in, v_even_w_out, v_attn_qk_norm_g, v_attn_sink, v_hgrn_out_norm_g, v_hgrn_lb, v_odd_w_in, v_odd_w_out):
    d = x.shape[-1]
    lc = ctx.shape[1]
    assert lc == TM and d == 1024
    ax, ay, ac = _place()
    s = 2 * ax + ay
    me = 4 * ax + 2 * ay + ac
    nmod = mod_w.shape[2]

    def pad8(v):
        return jnp.pad(v, ((0, 8 - v.shape[0]), (0, 0)))

    pack = jnp.concatenate([pad8(c), pad8(norm_g.reshape(1, d))], axis=0)
    g1 = _ag8(pack, name='gather_cond')
    c_all = g1[:, 0, :]
    ng = g1[0::2, 8, :].reshape(4, 2, 2, d // 4).transpose(1, 2, 0, 3).reshape(4, d)

    cond_raw = jnp.concatenate([c_all, pad8(c_ctx.reshape(1, d))], axis=0)
    mb_sh = lax.dynamic_slice_in_dim(mod_b, s * nmod, nmod, axis=1).reshape(2, 1, nmod)
    mpart = _mod_fwd(cond_raw, mod_w, mb_sh, name='mod_fwd')
    g3 = _ag8(mpart.reshape(32, nmod), name='gather_mods')
    mods_full = g3[0::2].reshape(4, 2, 16, nmod).transpose(1, 2, 0, 3).reshape(2, 16, 4 * nmod)
    m_lat = lax.dynamic_index_in_dim(mods_full, me, axis=1, keepdims=False)
    mods = jnp.stack([mods_full[:, 8], m_lat], axis=1).reshape(24, d)

    names = ['ffn_in', 'ffn_out', 'even_in', 'even_out', 'odd_in', 'odd_out']
    shards = [_bf(v.reshape(-1, v.shape[-1])) for v in (ffn_w_in, ffn_w_out, even_w_in, even_w_out, odd_w_in, odd_w_out)]
    shards, mods = lax.optimization_barrier((shards, mods))
    reducer = _GradReducer()
    wsrc = _GatheredWeights(dict(zip(names, shards)), reducer)

    lb = _lb_fwd(hgrn_lb, name='hgrn_lower_bound')
    small = dict(gq=jnp.tile(attn_qk_norm_g[0, 0], 2).reshape(1, 128), gk=jnp.tile(attn_qk_norm_g[0, 1], 2).reshape(1, 128),
                 sink=attn_sink[0], gain=hgrn_out_norm_g, lb=lb)
    x0 = jnp.concatenate([ctx[0], x[0]], axis=0) + wsrc.token[0, 0]
    loss_t, dx0, grads, sums = _local_step(x0, loss_target[0], mods, ng, wsrc, small)
    loss = lax.psum(loss_t[0, 0], ("x", "y", "c"))
    grad_x = dx0[None]

    def tile(v):
        return jnp.pad(v, ((0, 8 - v.shape[0]), (0, d - v.shape[1])))

    sums = dict(sums, sink=sums['sink'][:, 0].reshape(1, 8))
    g4 = _ag8(jnp.concatenate([tile(sums[nm]) for nm in PACK_TILES], axis=0), name='gather_row_sums')
    small_g, glb, gmb, dmat = _small_finalize(g4, tile(lb)[0:1], name='small_grads')
    dms = lax.dynamic_slice_in_dim(dmat.transpose(0, 2, 1, 3).reshape(2, 16, 6 * d), s * nmod, nmod, axis=2)
    g_mod_w, dcond = _mod_bwd(cond_raw, dms, mod_w, name='mod_bwd')
    g5 = _ag8(dcond[8:16], name='gather_dcond')
    g_c_ctx = _cctx_grad(g5, c_ctx.reshape(8, d // 8).reshape(1, d), name='c_ctx_grad')

    late = {nm: grads[nm] for nm in ('ffn_in0', 'ffn_out0', 'even_in', 'even_out')}
    late, g_c_ctx = lax.optimization_barrier((late, g_c_ctx))
    token = reducer.start('late', late)
    full = reducer.finish('early', token)

    def upd(wv, gs, mv, vv, name):
        shp = wv.shape
        c2 = shp[-1]
        out = _adam(wv.reshape(-1, c2), [g.reshape(-1, c2) for g in gs], mv.reshape(-1, c2), vv.reshape(-1, c2), name=name)
        return [o.reshape(shp) for o in out]

    res = {}
    res['c_ctx'] = upd(c_ctx.reshape(8, d // 8), [g_c_ctx.reshape(8, d // 8)], m_c_ctx.reshape(8, d // 8), v_c_ctx.reshape(8, d // 8), 'adam_c_ctx')
    res['c_ctx'] = [o.reshape(d) for o in res['c_ctx']]
    res['mod_w'] = upd(mod_w, [g_mod_w], m_mod_w, v_mod_w, 'adam_mod_w')
    res['mod_b'] = upd(mod_b, [gmb.reshape(2, 6 * d)], m_mod_b, v_mod_b, 'adam_mod_b')
    g_ng = lax.dynamic_slice_in_dim(small_g[0:4].reshape(2, 2, d), s * (d // 4), d // 4, axis=2)
    res['norm_g'] = upd(norm_g, [g_ng], m_norm_g, v_norm_g, 'adam_norm_g')
    g_qk = jnp.stack([small_g[4, 0:64], small_g[5, 0:64]]).reshape(1, 2, 64)
    res['attn_qk_norm_g'] = upd(attn_qk_norm_g, [g_qk], m_attn_qk_norm_g, v_attn_qk_norm_g, 'adam_qk_gain')
    res['attn_sink'] = upd(attn_sink, [small_g[7, 0:8].reshape(1, 8)], m_attn_sink, v_attn_sink, 'adam_sink')
    res['hgrn_out_norm_g'] = upd(hgrn_out_norm_g, [small_g[6, 0:128].reshape(1, 128)], m_hgrn_out_norm_g, v_hgrn_out_norm_g, 'adam_head_gain')
    res['hgrn_lb'] = upd(hgrn_lb, [glb[0:2, 0:hgrn_lb.shape[1]]], m_hgrn_lb, v_hgrn_lb, 'adam_hgrn_lb')
    res['odd_w_in'] = upd(odd_w_in, [full['odd_in']], m_odd_w_in, v_odd_w_in, 'adam_odd_in')
    res['odd_w_out'] = upd(odd_w_out, [full['odd_out']], m_odd_w_out, v_odd_w_out, 'adam_odd_out')
    full.update(reducer.finish('late', res['odd_w_in'][1]))
    g_ffn_in = jnp.concatenate([full['ffn_in0'], full['ffn_in1']], axis=0)
    g_ffn_out = jnp.concatenate([full['ffn_out0'], full['ffn_out1']], axis=0)
    res['ffn_w_in'] = upd(ffn_w_in, [g_ffn_in], m_ffn_w_in, v_ffn_w_in, 'adam_ffn_in')
    res['ffn_w_out'] = upd(ffn_w_out, [g_ffn_out], m_ffn_w_out, v_ffn_w_out, 'adam_ffn_out')
    res['even_w_in'] = upd(even_w_in, [full['even_in']], m_even_w_in, v_even_w_in, 'adam_even_in')
    res['even_w_out'] = upd(even_w_out, [full['even_out']], m_even_w_out, v_even_w_out, 'adam_even_out')

    order = ['c_ctx', 'mod_w', 'mod_b', 'norm_g', 'ffn_w_in', 'ffn_w_out', 'even_w_in', 'even_w_out',
             'attn_qk_norm_g', 'attn_sink', 'hgrn_out_norm_g', 'hgrn_lb', 'odd_w_in', 'odd_w_out']
    outs = [loss, grad_x]
    for k in range(4):
        outs += [res[nm][k] for nm in order]
    return tuple(outs)
```

```python
import functools
import math

import numpy as np
import jax
import jax.numpy as jnp
from jax import lax
from jax.experimental import pallas as pl
from jax.experimental.pallas import tpu as pltpu

F32 = jnp.float32
BF16 = jnp.bfloat16
EPS = 1e-6
TM = 256
CHUNK = 64
QB = 128
WINDOW = 128
NEG = -1e30
MESH = pl.DeviceIdType.MESH

ADAM_LR, ADAM_B1, ADAM_B2, ADAM_EPS, ADAM_WD, ADAM_STEP = 0.001, 0.9, 0.999, 1e-08, 0.01, 10


def _pcall(body, **kw):
    return pl.pallas_call(body, **kw)


def _pick(n, cap):
    best = None
    for m in range(128, min(n, cap) + 1, 128):
        if n % m == 0:
            best = m
    assert best is not None, (n, cap)
    return best


def _bf(x):
    return x.astype(BF16)


def _dot(a, b):
    return jnp.dot(_bf(a), _bf(b), preferred_element_type=F32)


def _dot_nt(a, b):
    return lax.dot_general(_bf(a), _bf(b), (((1,), (1,)), ((), ())), preferred_element_type=F32)


def _dot_tn(a, b):
    return lax.dot_general(_bf(a), _bf(b), (((0,), (0,)), ((), ())), preferred_element_type=F32)


def _dot_exact(a, b):
    return jnp.dot(a, b, preferred_element_type=F32, precision=lax.Precision.HIGHEST)


def _sigmoid(x):
    return 1.0 / (1.0 + jnp.exp(-x))


def _iota(shape, dim):
    return lax.broadcasted_iota(jnp.int32, shape, dim)


def _mm_nn(a, b, *, lead=None, out_dtype=F32, name):
    m, k = a.shape
    n = b.shape[-1]
    bm = 1408 if (m % 1408 == 0 and k <= 1024) else (768 if m % 768 == 0 else TM)
    bn = _pick(n, 1024) if n % 512 == 0 else _pick(n, 1664)

    def body(a_ref, b_ref, o_ref):
        o_ref[...] = _dot(a_ref[...], b_ref[...]).astype(o_ref.dtype)

    if lead is None:
        b_spec = pl.BlockSpec((k, bn), lambda i, j: (0, j))
    else:
        b_spec = pl.BlockSpec((None, k, bn), lambda i, j: (lead, 0, j))
    return _pcall(
        body, name=name, grid=(m // bm, n // bn),
        in_specs=[pl.BlockSpec((bm, k), lambda i, j: (i, 0)), b_spec],
        out_specs=pl.BlockSpec((bm, bn), lambda i, j: (i, j)),
        out_shape=jax.ShapeDtypeStruct((m, n), out_dtype),
    )(a, b)


def _mm_nt(a, b, *, lead=None, name):
    m, n = a.shape
    k = b.shape[-2]
    bm = 768 if m % 768 == 0 else TM
    bk = _pick(k, 512)

    def body(a_ref, b_ref, o_ref):
        o_ref[...] = _dot_nt(a_ref[...], b_ref[...])

    if lead is None:
        b_spec = pl.BlockSpec((bk, n), lambda i, j: (j, 0))
    else:
        b_spec = pl.BlockSpec((None, bk, n), lambda i, j: (lead, j, 0))
    return _pcall(
        body, name=name, grid=(m // bm, k // bk),
        in_specs=[pl.BlockSpec((bm, n), lambda i, j: (i, 0)), b_spec],
        out_specs=pl.BlockSpec((bm, bk), lambda i, j: (i, j)),
        out_shape=jax.ShapeDtypeStruct((m, k), F32),
    )(a, b)


def _mm_tn(a, b, *, name):
    t, k = a.shape
    n = b.shape[1]
    bt = 768 if t % 768 == 0 else TM
    bk = _pick(k, 1536)
    bn = _pick(n, 1024) if n % 1024 == 0 or n < 1664 else _pick(n, 1664)

    def body(a_ref, b_ref, o_ref):
        @pl.when(pl.program_id(2) == 0)
        def _():
            o_ref[...] = jnp.zeros_like(o_ref)
        o_ref[...] += _dot_tn(a_ref[...], b_ref[...])

    return _pcall(
        body, name=name, grid=(k // bk, n // bn, t // bt),
        in_specs=[pl.BlockSpec((bt, bk), lambda i, j, s: (s, i)),
                  pl.BlockSpec((bt, bn), lambda i, j, s: (s, j))],
        out_specs=pl.BlockSpec((bk, bn), lambda i, j, s: (i, j)),
        out_shape=jax.ShapeDtypeStruct((k, n), F32),
    )(a, b)


def _mod_row(mods_ref, lat, idx):
    return jnp.where(lat, mods_ref[idx + 6:idx + 7, :], mods_ref[idx:idx + 1, :])


def _row_fwd(x, mods, *, y=None, gate=None, g=None, shift=None, scale=None, name):
    t, d = x.shape
    has_y, has_n = y is not None, g is not None

    def body(*refs):
        refs = list(refs)
        x_ref, mods_ref = refs[0], refs[1]
        pos = 2
        if has_y:
            y_ref = refs[pos]; pos += 1
        if has_n:
            g_ref = refs[pos]; pos += 1
        outs = refs[pos:]
        lat = pl.program_id(0) > 0
        x1 = x_ref[...]
        o = 0
        if has_y:
            x1 = x1 + _mod_row(mods_ref, lat, gate) * y_ref[...]
            outs[o][...] = x1; o += 1
        if has_n:
            rs = lax.rsqrt(jnp.mean(x1 * x1, axis=-1, keepdims=True) + EPS)
            hn = x1 * rs * g_ref[...]
            h = hn * (1.0 + _mod_row(mods_ref, lat, scale)) + _mod_row(mods_ref, lat, shift)
            outs[o][...] = h.astype(BF16)

    row = pl.BlockSpec((TM, d), lambda i: (i, 0))
    ins, specs = [x, mods], [row, pl.BlockSpec(mods.shape, lambda i: (0, 0))]
    if has_y:
        ins.append(y); specs.append(row)
    if has_n:
        ins.append(g.reshape(1, d)); specs.append(pl.BlockSpec((1, d), lambda i: (0, 0)))
    out_shape, out_specs = [], []
    if has_y:
        out_shape.append(jax.ShapeDtypeStruct((t, d), F32)); out_specs.append(row)
    if has_n:
        out_shape.append(jax.ShapeDtypeStruct((t, d), BF16)); out_specs.append(row)
    res = _pcall(body, name=name, grid=(t // TM,), in_specs=specs, out_specs=out_specs,
                 out_shape=out_shape)(*ins)
    return res


def _acc_row(ref, r, val):
    ref[r:r + 1, :] += val


def _row_final(x, z, mods, target, *, gate, name):
    t, d = x.shape

    def body(x_ref, mods_ref, z_ref, t_ref, loss_ref, dx_ref, dz_ref, sums_ref):
        i = pl.program_id(0)
        lat = i > 0

        @pl.when(i == 0)
        def _():
            loss_ref[...] = jnp.zeros_like(loss_ref)
            sums_ref[...] = jnp.zeros_like(sums_ref)

        gt = _mod_row(mods_ref, lat, gate)
        zz = z_ref[...]
        yv = x_ref[...] + gt * zz
        keep = jnp.where(lat, 1.0, 0.0).astype(F32)
        diff = (yv - t_ref[...]) * keep
        part = jnp.sum(jnp.sum(diff * diff, axis=0, keepdims=True), axis=1, keepdims=True)
        loss_ref[...] += part * (0.5 / d)
        dy = diff * (1.0 / d)
        dx_ref[...] = dy
        dz_ref[...] = (gt * dy).astype(BF16)
        _acc_row(sums_ref, 6, jnp.sum(dy * zz, axis=0, keepdims=True))

    row = pl.BlockSpec((TM, d), lambda i: (i, 0))
    return _pcall(
        body, name=name, grid=(t // TM,),
        in_specs=[row, pl.BlockSpec(mods.shape, lambda i: (0, 0)), row,
                  pl.BlockSpec((TM, d), lambda i: (jnp.maximum(i - 1, 0), 0))],
        out_specs=[pl.BlockSpec((8, 128), lambda i: (0, 0)), row, row,
                   pl.BlockSpec((8, d), lambda i: (0, 0))],
        out_shape=[jax.ShapeDtypeStruct((8, 128), F32), jax.ShapeDtypeStruct((t, d), F32),
                   jax.ShapeDtypeStruct((t, d), BF16), jax.ShapeDtypeStruct((8, d), F32)],
    )(x, mods, z, target)


def _row_bwd(xn, dxo, dh, mods, g, *, shift, scale, y=None, gate=None, latent_only=False, name):
    t, d = xn.shape
    has_y = y is not None

    def body(*refs):
        refs = list(refs)
        x_ref, dxo_ref, dh_ref, mods_ref, g_ref = refs[:5]
        pos = 5
        if has_y:
            y_ref = refs[pos]; pos += 1
        dx_ref = refs[pos]; pos += 1
        if has_y:
            dy_ref = refs[pos]; pos += 1
        sums_ref = refs[pos]
        i = pl.program_id(0)
        lat = i > 0

        @pl.when(i == 0)
        def _():
            sums_ref[...] = jnp.zeros_like(sums_ref)

        x1 = x_ref[...]
        gv = g_ref[...]
        rs = lax.rsqrt(jnp.mean(x1 * x1, axis=-1, keepdims=True) + EPS)
        xh = x1 * rs
        dhv = dh_ref[...]
        dn = dhv * (1.0 + _mod_row(mods_ref, lat, scale))
        dxh = dn * gv
        dx = dxo_ref[...] + rs * (dxh - xh * jnp.mean(dxh * xh, axis=-1, keepdims=True))
        dx_ref[...] = dx
        vals = [jnp.sum(dhv, axis=0, keepdims=True),
                jnp.sum(dhv * (xh * gv), axis=0, keepdims=True),
                None,
                jnp.sum(dn * xh, axis=0, keepdims=True)]
        if has_y:
            dy_ref[...] = (_mod_row(mods_ref, lat, gate) * dx).astype(BF16)
            vals[2] = jnp.sum(dx * y_ref[...], axis=0, keepdims=True)

        @pl.when(i == 0)
        def _():
            for r, v in enumerate(vals):
                if v is not None:
                    _acc_row(sums_ref, r, v)

        @pl.when(i > 0)
        def _():
            for r, v in enumerate(vals):
                if v is not None:
                    _acc_row(sums_ref, 4 + r, v)

    row = pl.BlockSpec((TM, d), lambda i: (i, 0))
    ins = [xn, dxo, dh, mods, g.reshape(1, d)]
    specs = [row, row, row, pl.BlockSpec(mods.shape, lambda i: (0, 0)), pl.BlockSpec((1, d), lambda i: (0, 0))]
    if latent_only:
        out_shape = [jax.ShapeDtypeStruct((t - TM, d), F32)]
        out_specs = [pl.BlockSpec((TM, d), lambda i: (jnp.maximum(i - 1, 0), 0))]
    else:
        out_shape, out_specs = [jax.ShapeDtypeStruct((t, d), F32)], [row]
    if has_y:
        ins.append(y); specs.append(row)
        out_shape.append(jax.ShapeDtypeStruct((t, d), BF16)); out_specs.append(row)
    out_shape.append(jax.ShapeDtypeStruct((8, d), F32))
    out_specs.append(pl.BlockSpec((8, d), lambda i: (0, 0)))
    return _pcall(body, name=name, grid=(t // TM,), in_specs=specs, out_specs=out_specs,
                  out_shape=out_shape)(*ins)


FFN_BK = 1408


FFN_SUB = 256


def _ffn_order(n2):
    nb = n2 // (2 * FFN_BK)
    return [h * nb + j for j in range(nb) for h in (0, 1)]


def _ffn_interleave(w):
    return jnp.concatenate([w[..., b * FFN_BK:(b + 1) * FFN_BK] for b in _ffn_order(w.shape[-1])], axis=-1)


def _ffn_deinterleave(w):
    order = _ffn_order(w.shape[-1])
    return jnp.concatenate([w[..., order.index(b) * FFN_BK:(order.index(b) + 1) * FFN_BK]
                            for b in range(len(order))], axis=-1)


def _big_tile(t):
    return 768 if t % 768 == 0 else TM


def _ffn_in(h, w, *, lead, name):
    t, d = h.shape
    n2 = w.shape[-1]
    bm, bk = _big_tile(t), FFN_BK

    def body(h_ref, w_ref, u_ref, a_ref):
        hb = h_ref[...]
        for c0 in range(0, bk, FFN_SUB):
            c1 = min(c0 + FFN_SUB, bk)
            ug = _dot(hb, w_ref[:, c0:c1]).astype(BF16)
            uu = _dot(hb, w_ref[:, bk + c0:bk + c1]).astype(BF16)
            u_ref[:, c0:c1] = ug
            u_ref[:, bk + c0:bk + c1] = uu
            gv, up = ug.astype(F32), uu.astype(F32)
            a_ref[:, c0:c1] = (gv * _sigmoid(gv) * up).astype(BF16)

    return _pcall(
        body, name=name, grid=(t // bm, n2 // (2 * bk)),
        in_specs=[pl.BlockSpec((bm, d), lambda i, j: (i, 0)),
                  pl.BlockSpec((None, d, 2 * bk), lambda i, j: (lead, 0, j))],
        out_specs=[pl.BlockSpec((bm, 2 * bk), lambda i, j: (i, j)), pl.BlockSpec((bm, bk), lambda i, j: (i, j))],
        out_shape=[jax.ShapeDtypeStruct((t, n2), BF16), jax.ShapeDtypeStruct((t, n2 // 2), BF16)],
    )(h, w)


def _ffn_dx(dz, w_out, u, *, lead, name):
    t, d = dz.shape
    n2 = u.shape[1]
    bm, bk = _big_tile(t), FFN_BK

    def body(dz_ref, w_ref, u_ref, du_ref):
        dzb = dz_ref[...]
        for c0 in range(0, bk, FFN_SUB):
            c1 = min(c0 + FFN_SUB, bk)
            da = _dot_nt(dzb, w_ref[c0:c1, :])
            gv, up = u_ref[:, c0:c1].astype(F32), u_ref[:, bk + c0:bk + c1].astype(F32)
            s = _sigmoid(gv)
            du_ref[:, c0:c1] = (da * up * (s * (1.0 + gv * (1.0 - s)))).astype(BF16)
            du_ref[:, bk + c0:bk + c1] = (da * gv * s).astype(BF16)

    ublk = pl.BlockSpec((bm, 2 * bk), lambda i, j: (i, j))
    return _pcall(
        body, name=name, grid=(t // bm, n2 // (2 * bk)),
        in_specs=[pl.BlockSpec((bm, d), lambda i, j: (i, 0)),
                  pl.BlockSpec((None, bk, d), lambda i, j: (lead, j, 0)), ublk],
        out_specs=ublk, out_shape=jax.ShapeDtypeStruct((t, n2), BF16),
    )(dz, w_out, u)


def _lane(shape):
    return _iota(shape, len(shape) - 1)


def _pair_norm(x, g):
    lo = _lane(x.shape) < 64
    x2 = x * x
    s_lo = jnp.sum(jnp.where(lo, x2, 0.0), axis=-1, keepdims=True)
    s_hi = jnp.sum(jnp.where(lo, 0.0, x2), axis=-1, keepdims=True)
    rs = lax.rsqrt(jnp.where(lo, s_lo, s_hi) * (1.0 / 64) + EPS)
    return x * rs, rs


def _pair_mean(v):
    lo = _lane(v.shape) < 64
    s_lo = jnp.sum(jnp.where(lo, v, 0.0), axis=-1, keepdims=True)
    s_hi = jnp.sum(jnp.where(lo, 0.0, v), axis=-1, keepdims=True)
    return jnp.where(lo, s_lo, s_hi) * (1.0 / 64)


def _rot64(x):
    r1 = pltpu.roll(x, 32, 1)
    r2 = pltpu.roll(x, 96, 1)
    even = ((_lane(x.shape) >> 5) & 1) == 0
    return jnp.where(even, -r2, r1)


def _rope64(x, cos, sin):
    return x * cos + _rot64(x) * sin


def _rope64_t(d, cos, sin):
    return d * cos - _rot64(d * sin)


def _kprep_fwd(p, gk, cos, sin, *, name):
    t = p.shape[0]

    def body(k_ref, g_ref, c_ref, s_ref, o_ref):
        xh, _ = _pair_norm(k_ref[...], None)
        o_ref[...] = _rope64(xh * g_ref[...], c_ref[...], s_ref[...])

    blk = pl.BlockSpec((TM, 128), lambda i: (i, 0))
    return _pcall(
        body, name=name, grid=(t // TM,),
        in_specs=[pl.BlockSpec((TM, 128), lambda i: (i, 4)), pl.BlockSpec((1, 128), lambda i: (0, 0)), blk, blk],
        out_specs=blk, out_shape=jax.ShapeDtypeStruct((t, 128), F32),
    )(p, gk, cos, sin)


def _kprep_bwd(p, gk, cos, sin, dkp, dv, *, name):
    t = p.shape[0]

    def body(k_ref, g_ref, c_ref, s_ref, dkp_ref, dv_ref, o_ref, dg_ref):
        @pl.when(pl.program_id(0) == 0)
        def _():
            dg_ref[...] = jnp.zeros_like(dg_ref)
        xh, rs = _pair_norm(k_ref[...], None)
        dn = _rope64_t(dkp_ref[...], c_ref[...], s_ref[...])
        _acc_row(dg_ref, 0, jnp.sum(dn * xh, axis=0, keepdims=True))
        dxh = dn * g_ref[...]
        o_ref[:, 0:128] = (rs * (dxh - xh * _pair_mean(dxh * xh))).astype(BF16)
        o_ref[:, 128:256] = dv_ref[...].astype(BF16)

    blk = pl.BlockSpec((TM, 128), lambda i: (i, 0))
    return _pcall(
        body, name=name, grid=(t // TM,),
        in_specs=[pl.BlockSpec((TM, 128), lambda i: (i, 4)), pl.BlockSpec((1, 128), lambda i: (0, 0)), blk, blk, blk, blk],
        out_specs=[pl.BlockSpec((TM, 256), lambda i: (i, 0)), pl.BlockSpec((8, 128), lambda i: (0, 0))],
        out_shape=[jax.ShapeDtypeStruct((t, 256), BF16), jax.ShapeDtypeStruct((8, 128), F32)],
    )(p, gk, cos, sin, dkp, dv)


def _attn_common(i, t, lc, kp_ref, v_ref):
    span = QB + 2 * WINDOW
    start = pl.multiple_of(jnp.clip((i - 1) * QB, lc, t - span), QB)
    kall = jnp.concatenate([kp_ref[0:lc, :], kp_ref[pl.ds(start, span), :]], axis=0)
    vall = jnp.concatenate([v_ref[0:lc, :], v_ref[pl.ds(start, span), :]], axis=0)
    nk = lc + span
    col = _iota((QB, nk), 1)
    krow = jnp.where(col < lc, col, start + col - lc)
    qrow = i * QB + _iota((QB, nk), 0)
    valid = (col < lc) | ((qrow >= lc) & (krow >= lc) & (jnp.abs(krow - qrow) <= WINDOW))
    lo = _lane(kall.shape) < 64
    kroll, vroll = pltpu.roll(kall, 64, 1), pltpu.roll(vall, 64, 1)
    zero = jnp.zeros_like(kall)
    kvar = [[_bf(jnp.where(lo, kall, zero)), _bf(jnp.where(lo, zero, kroll))],
            [_bf(jnp.where(lo, kroll, zero)), _bf(jnp.where(lo, zero, kall))]]
    vvar = [[_bf(jnp.where(lo, vall, zero)), _bf(jnp.where(lo, zero, vroll))],
            [_bf(jnp.where(lo, vroll, zero)), _bf(jnp.where(lo, zero, vall))]]
    return start, valid, kvar, vvar


def _softmax_sink(s, valid, snk):
    s = jnp.where(valid, s, NEG)
    m = jnp.maximum(jnp.max(s, axis=-1, keepdims=True), snk)
    e = jnp.exp(s - m)
    es = jnp.exp(snk - m)
    inv = 1.0 / (jnp.sum(e, axis=-1, keepdims=True) + es)
    return e * inv, es * inv


def _attn_fwd(p, kp, gq, sink, cos, sin, *, lc, name):
    t = p.shape[0]
    scale = 64 ** -0.5

    def body(q_ref, kp_ref, v_ref, g_ref, sink_ref, c_ref, s_ref, o_ref):
        i = pl.program_id(0)
        _, valid, kvar, vvar = _attn_common(i, t, lc, kp_ref, v_ref)
        cosv, sinv, gv = c_ref[...], s_ref[...], g_ref[...]
        for j in range(4):
            xh, _ = _pair_norm(q_ref[:, 128 * j:128 * j + 128], None)
            q2 = _bf(_rope64(xh * gv, cosv, sinv))
            acc = jnp.zeros((QB, 128), F32)
            for half in range(2):
                s = _dot_nt(q2, kvar[j // 2][half]) * scale
                pr, _ = _softmax_sink(s, valid, sink_ref[2 * j + half])
                acc = acc + _dot(pr, vvar[j // 2][half])
            o_ref[:, 128 * j:128 * j + 128] = acc.astype(BF16)

    qblk = pl.BlockSpec((QB, 128), lambda i: (i, 0))
    return _pcall(
        body, name=name, grid=(t // QB,),
        in_specs=[pl.BlockSpec((QB, 512), lambda i: (i, 0)),
                  pl.BlockSpec((t, 128), lambda i: (0, 0)),
                  pl.BlockSpec((t, 128), lambda i: (0, 5)),
                  pl.BlockSpec((1, 128), lambda i: (0, 0)),
                  pl.BlockSpec(memory_space=pltpu.SMEM), qblk, qblk],
        out_specs=pl.BlockSpec((QB, 512), lambda i: (i, 0)),
        out_shape=jax.ShapeDtypeStruct((t, 512), BF16),
    )(p, kp, p, gq, sink, cos, sin)


def _attn_bwd(p, kp, gq, sink, cos, sin, dmix, *, lc, name):
    t = p.shape[0]
    scale = 64 ** -0.5
    span = QB + 2 * WINDOW

    def body(q_ref, kp_ref, v_ref, g_ref, sink_ref, c_ref, s_ref, do_ref,
             dq_ref, dk_ref, dv_ref, dg_ref, dsink_ref):
        i = pl.program_id(0)

        @pl.when(i == 0)
        def _():
            dk_ref[...] = jnp.zeros_like(dk_ref)
            dv_ref[...] = jnp.zeros_like(dv_ref)
            dg_ref[...] = jnp.zeros_like(dg_ref)
            dsink_ref[...] = jnp.zeros_like(dsink_ref)

        start, valid, kvar, vvar = _attn_common(i, t, lc, kp_ref, v_ref)
        cosv, sinv, gv = c_ref[...], s_ref[...], g_ref[...]
        nk = lc + span
        lo = _lane((nk, 128)) < 64
        dkt = [jnp.zeros((64, nk), F32), jnp.zeros((64, nk), F32)]
        dvt = [jnp.zeros((64, nk), F32), jnp.zeros((64, nk), F32)]
        for j in range(4):
            kvh = j // 2
            xh, rs = _pair_norm(q_ref[:, 128 * j:128 * j + 128], None)
            q2 = _bf(_rope64(xh * gv, cosv, sinv))
            do2 = _bf(do_ref[:, 128 * j:128 * j + 128])
            dq2 = jnp.zeros((QB, 128), F32)
            for half in range(2):
                s = _dot_nt(q2, kvar[kvh][half]) * scale
                pr, ps = _softmax_sink(s, valid, sink_ref[2 * j + half])
                dp = _dot_nt(do2, vvar[kvh][half])
                delta = jnp.sum(pr * dp, axis=-1, keepdims=True)
                ds = pr * (dp - delta) * scale
                dsk = jnp.sum(jnp.sum(-ps * delta, axis=0, keepdims=True), axis=1, keepdims=True)
                _acc_row(dsink_ref, 2 * j + half, jnp.broadcast_to(dsk, (1, 128)))
                dq2 = dq2 + _dot(ds, kvar[kvh][half])
                hrows = slice(64 * half, 64 * half + 64)
                dkt[kvh] = dkt[kvh] + _dot_tn(q2, ds)[hrows]
                dvt[kvh] = dvt[kvh] + _dot_tn(do2, pr)[hrows]
            dn = _rope64_t(dq2, cosv, sinv)
            _acc_row(dg_ref, 0, jnp.sum(dn * xh, axis=0, keepdims=True))
            dxh = dn * gv
            dq_ref[:, 128 * j:128 * j + 128] = (rs * (dxh - xh * _pair_mean(dxh * xh))).astype(BF16)
        dk_all = jnp.concatenate(dkt, axis=0).T
        dv_all = jnp.concatenate(dvt, axis=0).T
        dk_ref[0:lc, :] += dk_all[0:lc]
        dv_ref[0:lc, :] += dv_all[0:lc]
        dk_ref[pl.ds(start, span), :] += dk_all[lc:nk]
        dv_ref[pl.ds(start, span), :] += dv_all[lc:nk]

    qblk = pl.BlockSpec((QB, 128), lambda i: (i, 0))
    full = pl.BlockSpec((t, 128), lambda i: (0, 0))
    small = pl.BlockSpec((8, 128), lambda i: (0, 0))
    return _pcall(
        body, name=name, grid=(t // QB,),
        in_specs=[pl.BlockSpec((QB, 512), lambda i: (i, 0)), full,
                  pl.BlockSpec((t, 128), lambda i: (0, 5)),
                  pl.BlockSpec((1, 128), lambda i: (0, 0)),
                  pl.BlockSpec(memory_space=pltpu.SMEM), qblk, qblk,
                  pl.BlockSpec((QB, 512), lambda i: (i, 0))],
        out_specs=[pl.BlockSpec((QB, 512), lambda i: (i, 0)), full, full, small, small],
        out_shape=[jax.ShapeDtypeStruct((t, 512), BF16), jax.ShapeDtypeStruct((t, 128), F32),
                   jax.ShapeDtypeStruct((t, 128), F32), jax.ShapeDtypeStruct((8, 128), F32),
                   jax.ShapeDtypeStruct((8, 128), F32)],
    )(p, kp, p, gq, sink, cos, sin, dmix)


def _tri(rev):
    r, c = _iota((CHUNK, CHUNK), 0), _iota((CHUNK, CHUNK), 1)
    return (c >= r) if rev else (c <= r)


def _blk_map(nb, rev, backward):
    if not rev:
        return (lambda n: nb - 1 - n) if backward else (lambda n: n)
    if backward:
        return lambda n: jnp.where(n < nb - 1, n + 1, 0)
    return lambda n: jnp.where(n == 0, 0, nb - n)


def _chunk_order(rev, backward, nc=TM // CHUNK):
    order = list(range(nc))
    return order[::-1] if (rev != backward) else order


def _hgrn_gates(qraw, fraw, lb):
    sq = _sigmoid(qraw)
    sf = _sigmoid(fraw)
    f = lb + (1.0 - lb) * sf
    return qraw * sq, 1.0 - f, jnp.log(f), sq, sf, f


HGRN_HP = 2


def _chunk_cumsum(x, rev):
    n = x.shape[0]
    pos = _iota(x.shape, 0) & (CHUNK - 1)
    s = 1
    while s < CHUNK:
        if rev:
            x = x + jnp.where(pos < CHUNK - s, pltpu.roll(x, n - s, 0), 0.0)
        else:
            x = x + jnp.where(pos >= s, pltpu.roll(x, s, 0), 0.0)
        s *= 2
    return x


def _block_terms(lf, rev):
    b = _chunk_cumsum(lf, rev)
    mid, last = (CHUNK // 2 - 1, 0) if rev else (CHUNK // 2, CHUNK - 1)

    def chunk_row(off):
        return jnp.concatenate([jnp.broadcast_to(b[c * CHUNK + off:c * CHUNK + off + 1, :], (CHUNK, b.shape[1]))
                                for c in range(TM // CHUNK)], axis=0)

    r, bl = chunk_row(mid), chunk_row(last)
    return _tri(rev), jnp.exp(b - r), jnp.exp(r - b), jnp.exp(b), jnp.exp(bl - b), jnp.exp(bl)


def _headnorm_apply(o, gv, gain):
    n = o * lax.rsqrt(jnp.mean(o * o, axis=-1, keepdims=True) + EPS)
    if gain is not None:
        n = n * gain
    return (n * (gv * _sigmoid(gv))).astype(BF16)


def _headnorm_grad(o, gv, dy, gain):
    rs = lax.rsqrt(jnp.mean(o * o, axis=-1, keepdims=True) + EPS)
    xh = o * rs
    n = xh * gain if gain is not None else xh
    sg = _sigmoid(gv)
    dn = dy * (gv * sg)
    dg = (dy * n * (sg * (1.0 + gv * (1.0 - sg)))).astype(BF16)
    dgain = jnp.sum(dn * xh, axis=0, keepdims=True)
    dxh = dn * gain if gain is not None else dn
    return rs * (dxh - xh * jnp.mean(dxh * xh, axis=-1, keepdims=True)), dg, dgain


def _hgrn_fwd(p, lb, *, rev, name, ofw=None, gain=None):
    t = p.shape[0]
    nb, nc = t // TM, TM // CHUNK
    bmap = _blk_map(nb, rev, False)
    fcol = 14 if rev else 10
    fused = ofw is not None

    def body(*refs):
        q_ref, f_ref, v_ref, lb_ref = refs[:4]
        if fused:
            ofw_ref, g_ref, gain_ref, o_ref, sh_ref, mix_ref, st = refs[4:]
        else:
            o_ref, sh_ref, st = refs[4:]

        @pl.when(pl.program_id(1) == 0)
        def _():
            st[...] = jnp.zeros_like(st)
        for hh in range(HGRN_HP):
            ln = slice(128 * hh, 128 * hh + 128)
            q, k, lf, _, _, _ = _hgrn_gates(q_ref[:, ln], f_ref[:, ln], lb_ref[:, ln])
            tri, eq, ek, ei, eki, eb = _block_terms(lf, rev)
            qe, ke, qi, ki, vb = _bf(q * eq), _bf(k * ek), _bf(q * ei), _bf(k * eki), _bf(v_ref[:, ln])
            intra = []
            for cc in range(nc):
                rows = slice(cc * CHUNK, (cc + 1) * CHUNK)
                a = jnp.where(tri, _dot_nt(qe[rows], ke[rows]), 0.0)
                intra.append(_dot(a, vb[rows]))
            s = st[hh]
            for cc in _chunk_order(rev, False):
                rows = slice(cc * CHUNK, (cc + 1) * CHUNK)
                sh_ref[hh, cc] = s
                o_ref[rows, ln] = intra[cc] + _dot_nt(qi[rows], s)
                s = s * eb[cc * CHUNK:cc * CHUNK + 1, :] + _dot_tn(vb[rows], ki[rows])
            st[hh] = s
            if fused:
                osum = o_ref[:, ln] + ofw_ref[:, ln]
                o_ref[:, ln] = osum
                mix_ref[:, ln] = _headnorm_apply(osum, g_ref[:, ln], gain_ref[...])

    hp, wd = HGRN_HP, 128 * HGRN_HP

    def col(c0):
        return pl.BlockSpec((TM, wd), lambda h, n: (bmap(n), c0 // hp + h))

    oblk = pl.BlockSpec((TM, wd), lambda h, n: (bmap(n), h))
    ins, specs = [p, p, p, lb], [col(6), col(fcol), col(18), pl.BlockSpec((1, wd), lambda h, n: (0, h))]
    out_specs = [oblk, pl.BlockSpec((hp, nc, 128, 128), lambda h, n: (h, bmap(n), 0, 0))]
    out_shape = [jax.ShapeDtypeStruct((t, 512), F32), jax.ShapeDtypeStruct((4, t // CHUNK, 128, 128), F32)]
    if fused:
        ins += [ofw, p, gain]
        specs += [oblk, col(22), pl.BlockSpec((1, 128), lambda h, n: (0, 0))]
        out_specs.append(oblk)
        out_shape.append(jax.ShapeDtypeStruct((t, 512), BF16))
    return _pcall(body, name=name, grid=(4 // hp, nb), in_specs=specs, out_specs=out_specs, out_shape=out_shape,
                  scratch_shapes=[pltpu.VMEM((hp, 128, 128), F32)])(*ins)


def _hgrn_bwd(p, lb, sh, do, prev, *, rev, name, head=None):
    t = p.shape[0]
    nb, nc = t // TM, TM // CHUNK
    bmap = _blk_map(nb, rev, True)
    fcol = 14 if rev else 10
    has_prev = prev is not None
    odt = BF16 if has_prev else F32
    fused = head is not None

    def body(*refs):
        refs = list(refs)
        q_ref, f_ref, v_ref, lb_ref, sh_ref = refs[:5]
        pos = 5
        if fused:
            osum_ref, g_ref, dmix_ref, gain_ref = refs[5:9]
            pos = 9
        else:
            do_ref = refs[5]
            pos = 6
        if has_prev:
            pq_ref, pv_ref = refs[pos], refs[pos + 1]
            pos += 2
        dq_ref, df_ref, dv_ref, dlb_ref = refs[pos:pos + 4]
        pos += 4
        if fused:
            do_out, dg_ref, dgain_ref = refs[pos:pos + 3]
            pos += 3
        dst = refs[pos]

        @pl.when(pl.program_id(1) == 0)
        def _():
            dst[...] = jnp.zeros_like(dst)
            dlb_ref[...] = jnp.zeros_like(dlb_ref)

        if fused:
            @pl.when((pl.program_id(0) == 0) & (pl.program_id(1) == 0))
            def _():
                dgain_ref[...] = jnp.zeros_like(dgain_ref)

        cat = functools.partial(jnp.concatenate, axis=0)
        for hh in range(HGRN_HP):
            ln = slice(128 * hh, 128 * hh + 128)
            lbv = lb_ref[:, ln]
            qraw, fraw = q_ref[:, ln], f_ref[:, ln]
            q, k, lf, sq, sf, f = _hgrn_gates(qraw, fraw, lbv)
            tri, eq, ek, ei, eki, eb = _block_terms(lf, rev)
            qe, ke, qi, ki = q * eq, k * ek, q * ei, k * eki
            if fused:
                dov, dg, dgain = _headnorm_grad(osum_ref[:, ln], g_ref[:, ln], dmix_ref[:, ln], gain_ref[...])
                do_out[:, ln] = dov
                dg_ref[:, ln] = dg
                _acc_row(dgain_ref, 0, dgain)
            else:
                dov = do_ref[:, ln]
            qeb, keb, qib, kib, vb, dob = _bf(qe), _bf(ke), _bf(qi), _bf(ki), _bf(v_ref[:, ln]), _bf(dov)
            dv, dqe, dke, dqi = [None] * nc, [None] * nc, [None] * nc, [None] * nc
            for cc in range(nc):
                rows = slice(cc * CHUNK, (cc + 1) * CHUNK)
                a = jnp.where(tri, _dot_nt(qeb[rows], keb[rows]), 0.0)
                da = jnp.where(tri, _dot_nt(dob[rows], vb[rows]), 0.0)
                dv[cc] = _dot_tn(a, dob[rows])
                dqe[cc], dke[cc] = _dot(da, keb[rows]), _dot_tn(da, qeb[rows])
                dqi[cc] = _dot(dob[rows], sh_ref[hh, cc])
            dki, dbl = [None] * nc, [None] * nc
            ds = dst[hh]
            for cc in _chunk_order(rev, True):
                rows = slice(cc * CHUNK, (cc + 1) * CHUNK)
                ebc = eb[cc * CHUNK:cc * CHUNK + 1, :]
                dv[cc] = dv[cc] + _dot_nt(kib[rows], ds)
                dki[cc] = _dot(vb[rows], ds)
                dbl[cc] = jnp.broadcast_to(jnp.sum(dki[cc] * ki[rows], axis=0, keepdims=True)
                                           + jnp.sum(ds * sh_ref[hh, cc], axis=0, keepdims=True) * ebc, (CHUNK, 128))
                ds = ds * ebc + _dot_tn(dob[rows], qib[rows])
            dst[hh] = ds
            dqe, dke, dqi, dki, dv, dbl = cat(dqe), cat(dke), cat(dqi), cat(dki), cat(dv), cat(dbl)
            dq = dqe * eq + dqi * ei
            dk = dke * ek + dki * eki
            last = 0 if rev else CHUNK - 1
            db = dqe * qe - dke * ke + dqi * qi - dki * ki
            db = db + jnp.where((_iota(db.shape, 0) & (CHUNK - 1)) == last, dbl, 0.0)
            dlf = _chunk_cumsum(db, not rev)
            dqr = dq * (sq * (1.0 + qraw * (1.0 - sq)))
            dfv = dlf / f - dk
            dfr = dfv * (1.0 - lbv) * (sf * (1.0 - sf))
            dlb_ref[:, ln] += jnp.sum(dfv * (1.0 - sf), axis=0, keepdims=True)
            if has_prev:
                dqr = dqr + pq_ref[:, ln]
                dv = dv + pv_ref[:, ln]
            dq_ref[:, ln] = dqr.astype(odt)
            df_ref[:, ln] = dfr.astype(odt)
            dv_ref[:, ln] = dv.astype(odt)

    hp, wd = HGRN_HP, 128 * HGRN_HP

    def col(c0):
        return pl.BlockSpec((TM, wd), lambda h, n: (bmap(n), c0 // hp + h))

    oblk = pl.BlockSpec((TM, wd), lambda h, n: (bmap(n), h))
    ins = [p, p, p, lb, sh]
    specs = [col(6), col(fcol), col(18), pl.BlockSpec((1, wd), lambda h, n: (0, h)),
             pl.BlockSpec((hp, nc, 128, 128), lambda h, n: (h, bmap(n), 0, 0))]
    if fused:
        osum, dmix, gain = head
        ins += [osum, p, dmix, gain]
        specs += [oblk, col(22), pl.BlockSpec((TM, wd), lambda h, n: (bmap(n), 4 // hp + h)),
                  pl.BlockSpec((1, 128), lambda h, n: (0, 0))]
    else:
        ins.append(do); specs.append(oblk)
    if has_prev:
        ins += list(prev); specs += [oblk, oblk]
    out_specs = [oblk, oblk, oblk, pl.BlockSpec((1, wd), lambda h, n: (0, h))]
    out_shape = [jax.ShapeDtypeStruct((t, 512), odt)] * 3 + [jax.ShapeDtypeStruct((1, 512), F32)]
    if fused:
        out_specs += [oblk, oblk, pl.BlockSpec((8, 128), lambda h, n: (0, 0))]
        out_shape += [jax.ShapeDtypeStruct((t, 512), F32), jax.ShapeDtypeStruct((t, 512), BF16),
                      jax.ShapeDtypeStruct((8, 128), F32)]
    return _pcall(body, name=name, grid=(4 // hp, nb), in_specs=specs, out_specs=out_specs, out_shape=out_shape,
                  scratch_shapes=[pltpu.VMEM((hp, 128, 128), F32)])(*ins)


def _rope256(x, cos, sin):
    x1, x2 = x[:, 0:128], x[:, 128:256]
    return jnp.concatenate([x1 * cos - x2 * sin, x2 * cos + x1 * sin], axis=-1)


def _rope256_t(d, cos, sin):
    d1, d2 = d[:, 0:128], d[:, 128:256]
    return jnp.concatenate([d1 * cos + d2 * sin, d2 * cos - d1 * sin], axis=-1)


RET_DK, RET_DV, RET_H = 256, 512, 4
RET_KSCALE = RET_DK ** -0.5
RCH = TM
RET_HP = 2


def _ret_terms(lg, rev):
    r, c = _iota((RCH, RCH), 0), _iota((RCH, RCH), 1)
    rel = ((c - r) if rev else (r - c)).astype(F32)
    dmat = jnp.where(rel >= 0, jnp.exp(lg[:, 0:1] * jnp.maximum(rel, 0.0)), 0.0)
    pos = _iota((RCH, 1), 0).astype(F32)
    cnt = (RCH - pos) if rev else (pos + 1.0)
    ei = jnp.exp(lg * cnt)
    eki = jnp.exp(lg * (RCH - cnt))
    eb = jnp.exp(lg * float(RCH))
    return dmat, ei, eki, eb


def _ret_fwd(p, lgt, cos, sin, *, rev, name, ofw=None):
    t = p.shape[0]
    nb, nc = t // TM, TM // RCH
    bmap = _blk_map(nb, rev, False)
    fused = ofw is not None

    def body(*refs):
        q_ref, k_ref, v_ref, lg_ref, c_ref, s_ref = refs[:6]
        if fused:
            ofw_ref, g_ref, o_ref, sh_ref, mix_ref, st = refs[6:]
        else:
            o_ref, sh_ref, st = refs[6:]

        @pl.when(pl.program_id(1) == 0)
        def _():
            st[...] = jnp.zeros_like(st)
        for hh in range(RET_HP):
            qc, vc = slice(RET_DK * hh, RET_DK * (hh + 1)), slice(RET_DV * hh, RET_DV * (hh + 1))
            dmat, ei, eki, eb = _ret_terms(lg_ref[hh], rev)
            for cc in _chunk_order(rev, False, nc):
                rows = slice(cc * RCH, (cc + 1) * RCH)
                cosv, sinv = c_ref[rows, :], s_ref[rows, :]
                q = _rope256(q_ref[rows, qc].astype(F32), cosv, sinv)
                k = _rope256(k_ref[rows, qc].astype(F32), cosv, sinv) * RET_KSCALE
                v = v_ref[rows, vc]
                s0 = st[hh]
                sh_ref[hh, cc] = s0.astype(BF16)
                a = _dot_nt(q, k) * dmat
                o = _dot(a, v) + _dot_nt(q * ei, s0)
                st[hh] = s0 * eb + _dot_tn(v, k * eki)
                if fused:
                    o = o + ofw_ref[rows, vc]
                    mix_ref[rows, vc] = _headnorm_apply(o, g_ref[rows, vc].astype(F32), None)
                o_ref[rows, vc] = o

    hp = RET_HP
    tab = pl.BlockSpec((TM, 128), lambda h, n: (bmap(n), 0))
    oblk = pl.BlockSpec((TM, hp * RET_DV), lambda h, n: (bmap(n), h))
    ins = [p, p, p, lgt, cos, sin]
    specs = [pl.BlockSpec((TM, hp * RET_DK), lambda h, n: (bmap(n), h)),
             pl.BlockSpec((TM, hp * RET_DK), lambda h, n: (bmap(n), RET_H // hp + h)),
             pl.BlockSpec((TM, hp * RET_DV), lambda h, n: (bmap(n), RET_H // hp + h)),
             pl.BlockSpec((hp, 1, RET_DK), lambda h, n: (h, 0, 0)), tab, tab]
    out_specs = [oblk, pl.BlockSpec((hp, nc, RET_DV, RET_DK), lambda h, n: (h, bmap(n), 0, 0))]
    out_shape = [jax.ShapeDtypeStruct((t, RET_H * RET_DV), F32),
                 jax.ShapeDtypeStruct((RET_H, t // RCH, RET_DV, RET_DK), BF16)]
    if fused:
        ins += [ofw, p]
        specs += [oblk, pl.BlockSpec((TM, hp * RET_DV), lambda h, n: (bmap(n), 2 * RET_H // hp + h))]
        out_specs.append(oblk)
        out_shape.append(jax.ShapeDtypeStruct((t, RET_H * RET_DV), BF16))
    return _pcall(body, name=name, grid=(RET_H // hp, nb), in_specs=specs, out_specs=out_specs, out_shape=out_shape,
                  scratch_shapes=[pltpu.VMEM((hp, RET_DV, RET_DK), F32)])(*ins)


def _ret_bwd(p, lgt, cos, sin, sh, do, prev, *, rev, name, head=None):
    t = p.shape[0]
    nb, nc = t // TM, TM // RCH
    bmap = _blk_map(nb, rev, True)
    has_prev = prev is not None
    odt = BF16 if has_prev else F32
    fused = head is not None

    def body(*refs):
        refs = list(refs)
        q_ref, k_ref, v_ref, lg_ref, c_ref, s_ref, sh_ref = refs[:7]
        if fused:
            osum_ref, g_ref, dmix_ref = refs[7:10]
            pos = 10
        else:
            do_ref = refs[7]
            pos = 8
        if has_prev:
            pq_ref, pk_ref, pv_ref = refs[pos:pos + 3]
            pos += 3
        dq_ref, dk_ref, dv_ref = refs[pos:pos + 3]
        pos += 3
        if fused:
            do_out, dg_ref = refs[pos:pos + 2]
            pos += 2
        dst = refs[pos]

        @pl.when(pl.program_id(1) == 0)
        def _():
            dst[...] = jnp.zeros_like(dst)

        for hh in range(RET_HP):
            qc, vc = slice(RET_DK * hh, RET_DK * (hh + 1)), slice(RET_DV * hh, RET_DV * (hh + 1))
            dmat, ei, eki, eb = _ret_terms(lg_ref[hh], rev)
            for cc in _chunk_order(rev, True, nc):
                rows = slice(cc * RCH, (cc + 1) * RCH)
                cosv, sinv = c_ref[rows, :], s_ref[rows, :]
                q = _rope256(q_ref[rows, qc].astype(F32), cosv, sinv)
                k = _rope256(k_ref[rows, qc].astype(F32), cosv, sinv) * RET_KSCALE
                v = v_ref[rows, vc]
                if fused:
                    dov, dg, _ = _headnorm_grad(osum_ref[rows, vc], g_ref[rows, vc].astype(F32), dmix_ref[rows, vc], None)
                    do_out[rows, vc] = dov
                    dg_ref[rows, vc] = dg
                else:
                    dov = do_ref[rows, vc]
                s0 = sh_ref[hh, cc]
                dsc = dst[hh]
                qi, ki = q * ei, k * eki
                a = _dot_nt(q, k) * dmat
                da = _dot_nt(dov, v) * dmat
                dv = _dot_tn(a, dov) + _dot_nt(ki, dsc)
                dqs = _dot(da, k) + _dot(dov, s0) * ei
                dks = _dot_tn(da, q) + _dot(v, dsc) * eki
                dst[hh] = dsc * eb + _dot_tn(dov, qi)
                dq = _rope256_t(dqs, cosv, sinv)
                dk = _rope256_t(dks * RET_KSCALE, cosv, sinv)
                if has_prev:
                    dq = dq + pq_ref[rows, qc]
                    dk = dk + pk_ref[rows, qc]
                    dv = dv + pv_ref[rows, vc]
                dq_ref[rows, qc] = dq.astype(odt)
                dk_ref[rows, qc] = dk.astype(odt)
                dv_ref[rows, vc] = dv.astype(odt)

    hp = RET_HP
    tab = pl.BlockSpec((TM, 128), lambda h, n: (bmap(n), 0))
    qblk = pl.BlockSpec((TM, hp * RET_DK), lambda h, n: (bmap(n), h))
    vblk = pl.BlockSpec((TM, hp * RET_DV), lambda h, n: (bmap(n), h))
    ins = [p, p, p, lgt, cos, sin, sh]
    specs = [qblk, pl.BlockSpec((TM, hp * RET_DK), lambda h, n: (bmap(n), RET_H // hp + h)),
             pl.BlockSpec((TM, hp * RET_DV), lambda h, n: (bmap(n), RET_H // hp + h)),
             pl.BlockSpec((hp, 1, RET_DK), lambda h, n: (h, 0, 0)), tab, tab,
             pl.BlockSpec((hp, nc, RET_DV, RET_DK), lambda h, n: (h, bmap(n), 0, 0))]
    if fused:
        osum, dmix = head
        ins += [osum, p, dmix]
        specs += [vblk, pl.BlockSpec((TM, hp * RET_DV), lambda h, n: (bmap(n), 2 * RET_H // hp + h)), vblk]
    else:
        ins.append(do); specs.append(vblk)
    if has_prev:
        ins += list(prev); specs += [qblk, qblk, vblk]
    out_specs = [qblk, qblk, vblk]
    out_shape = [jax.ShapeDtypeStruct((t, RET_H * RET_DK), odt), jax.ShapeDtypeStruct((t, RET_H * RET_DK), odt),
                 jax.ShapeDtypeStruct((t, RET_H * RET_DV), odt)]
    if fused:
        out_specs += [vblk, vblk]
        out_shape += [jax.ShapeDtypeStruct((t, RET_H * RET_DV), F32), jax.ShapeDtypeStruct((t, RET_H * RET_DV), BF16)]
    return _pcall(body, name=name, grid=(RET_H // hp, nb), in_specs=specs, out_specs=out_specs, out_shape=out_shape,
                  scratch_shapes=[pltpu.VMEM((hp, RET_DV, RET_DK), F32)])(*ins)


def _rope_tables(lc, l):
    tt = jnp.arange(l)
    row, colp = (tt // 64).astype(F32), (tt % 64).astype(F32)
    inv = 10000.0 ** (-jnp.arange(16, dtype=F32) / 16)
    ang = jnp.concatenate([row[:, None] * inv, colp[:, None] * inv], axis=-1)
    ang = jnp.concatenate([jnp.zeros((lc, 32), F32), ang], axis=0)
    acos, asin = jnp.tile(jnp.cos(ang), (1, 4)), jnp.tile(jnp.sin(ang), (1, 4))
    theta = 1.0 / (10000.0 ** jnp.linspace(0.0, 1.0, 128, dtype=F32))
    rang = jnp.arange(l, dtype=F32)[:, None] * theta
    rang = jnp.concatenate([jnp.zeros((lc, 128), F32), rang], axis=0)
    return acos, asin, jnp.cos(rang), jnp.sin(rang)


class _Weights:
    def __init__(self, w):
        self.w = w

    def first(self, after):
        return self.w

    def rest_landed(self, after):
        pass

    def rest(self, after):
        return self.w

    def early_grads(self, grads):
        return jnp.zeros((8, 128), F32)


def _local_step(x0, target, mods, ng, wsrc, small):
    t, d = x0.shape
    l = target.shape[0]
    lc = t - l
    acos, asin, rcos, rsin = _rope_tables(lc, l)
    lg_fw = jnp.log(1.0 - 2.0 ** (-5.0 - jnp.arange(RET_H, dtype=F32)))
    lgt_fw = jnp.broadcast_to(lg_fw[:, None, None], (RET_H, 1, RET_DK))
    lgt_bw = jnp.broadcast_to(lg_fw[::-1][:, None, None], (RET_H, 1, RET_DK))
    gq, gk, sink, gain, lb = small['gq'], small['gk'], small['sink'], small['gain'], small['lb']

    (h1,) = _row_fwd(x0, mods, g=ng[0], shift=0, scale=1, name='l0_norm1')
    w = wsrc.first(h1)
    p0 = _mm_nn(h1, w['even_in'], name='l0_in')
    kp = _kprep_fwd(p0, gk, acos, asin, name='l0_kprep')
    att = _attn_fwd(p0, kp, gq, sink, acos, asin, lc=lc, name='l0_attn')
    hof, hsf = _hgrn_fwd(p0, lb, rev=False, name='l0_hgrn_f')
    wsrc.rest_landed(hof)
    hos, hsb, bmix = _hgrn_fwd(p0, lb, rev=True, name='l0_hgrn_b', ofw=hof, gain=gain)
    mix0 = jnp.concatenate([att, bmix], axis=1)
    y0 = _mm_nn(mix0, w['even_out'], name='l0_out')
    x1, h2 = _row_fwd(x0, mods, y=y0, gate=2, g=ng[1], shift=3, scale=4, name='l0_norm2')
    w = dict(w, **wsrc.rest(h2))
    u0, a0 = _ffn_in(h2, w['ffn_in'], lead=0, name='ffn_in')
    z0 = _mm_nn(a0, w['ffn_out'], lead=0, name='ffn_out')
    x2, h3 = _row_fwd(x1, mods, y=z0, gate=5, g=ng[2], shift=12, scale=13, name='l1_norm1')
    p1 = _mm_nn(h3, w['odd_in'], out_dtype=BF16, name='l1_in')
    rof, rsf = _ret_fwd(p1, lgt_fw, rcos, rsin, rev=False, name='l1_ret_f')
    ros, rsb, mix1 = _ret_fwd(p1, lgt_bw, rcos, rsin, rev=True, name='l1_ret_b', ofw=rof)
    y1 = _mm_nn(mix1, w['odd_out'], name='l1_out')
    x3, h4 = _row_fwd(x2, mods, y=y1, gate=14, g=ng[3], shift=15, scale=16, name='l1_norm2')
    u1, a1 = _ffn_in(h4, w['ffn_in'], lead=1, name='ffn_in')
    z1 = _mm_nn(a1, w['ffn_out'], lead=1, name='ffn_out')
    loss, dx4, dz1, s_fin = _row_final(x3, z1, mods, target, gate=17, name='loss')

    du1 = _ffn_dx(dz1, w['ffn_out'], u1, lead=1, name='ffn_out_dx')
    g_ffn_out1 = _mm_tn(a1, dz1, name='ffn_out_dw')
    dh4 = _mm_nt(du1, w['ffn_in'], lead=1, name='ffn_in_dx')
    g_ffn_in1 = _mm_tn(h4, du1, name='ffn_in_dw')
    dx3, dy1, s_l1n2 = _row_bwd(x3, dx4, dh4, mods, ng[3], shift=15, scale=16, y=y1, gate=14, name='l1_norm2_bwd')
    dmix1 = _mm_nt(dy1, w['odd_out'], name='l1_out_dx')
    g_odd_out = _mm_tn(mix1, dy1, name='l1_out_dw')
    rdq, rdk, rdv, rdo, rdg = _ret_bwd(p1, lgt_fw, rcos, rsin, rsf, None, None, rev=False, name='l1_ret_f_bwd',
                                       head=(ros, dmix1))
    rdq, rdk, rdv = _ret_bwd(p1, lgt_bw, rcos, rsin, rsb, rdo, (rdq, rdk, rdv), rev=True, name='l1_ret_b_bwd')
    dp1 = jnp.concatenate([rdq, rdk, rdv, rdg], axis=1)
    dh3 = _mm_nt(dp1, w['odd_in'], name='l1_in_dx')
    g_odd_in = _mm_tn(h3, dp1, name='l1_in_dw')
    mods = mods + wsrc.early_grads(dict(ffn_in1=g_ffn_in1, ffn_out1=g_ffn_out1, odd_in=g_odd_in, odd_out=g_odd_out))[0, 0]
    dx2, dz0, s_l1n1 = _row_bwd(x2, dx3, dh3, mods, ng[2], shift=12, scale=13, y=z0, gate=5, name='l1_norm1_bwd')
    du0 = _ffn_dx(dz0, w['ffn_out'], u0, lead=0, name='ffn_out_dx')
    g_ffn_out0 = _mm_tn(a0, dz0, name='ffn_out_dw')
    dh2 = _mm_nt(du0, w['ffn_in'], lead=0, name='ffn_in_dx')
    g_ffn_in0 = _mm_tn(h2, du0, name='ffn_in_dw')
    dx1, dy0, s_l0n2 = _row_bwd(x1, dx2, dh2, mods, ng[1], shift=3, scale=4, y=y0, gate=2, name='l0_norm2_bwd')
    dmix0 = _mm_nt(dy0, w['even_out'], name='l0_out_dx')
    g_even_out = _mm_tn(mix0, dy0, name='l0_out_dw')
    hq, hff, hv, dlb_f, hdo, hdg, s_gain = _hgrn_bwd(p0, lb, hsf, None, None, rev=False, name='l0_hgrn_f_bwd',
                                                     head=(hos, dmix0, gain))
    hq, hfb, hv, dlb_b = _hgrn_bwd(p0, lb, hsb, hdo, (hq, hv), rev=True, name='l0_hgrn_b_bwd')
    adq, dkp, adv, s_gq, s_sink = _attn_bwd(p0, kp, gq, sink, acos, asin, dmix0, lc=lc, name='l0_attn_bwd')
    dkv, s_gk = _kprep_bwd(p0, gk, acos, asin, dkp, adv, name='l0_kprep_bwd')
    dp0 = jnp.concatenate([adq, dkv, hq, _bf(hff), hfb, hv, hdg], axis=1)
    dh1 = _mm_nt(dp0, w['even_in'], name='l0_in_dx')
    g_even_in = _mm_tn(h1, dp0, name='l0_in_dw')
    dx0, s_l0n1 = _row_bwd(x0, dx1, dh1, mods, ng[0], shift=0, scale=1, latent_only=True, name='l0_norm1_bwd')

    grads = dict(ffn_in0=g_ffn_in0, ffn_in1=g_ffn_in1, ffn_out0=g_ffn_out0, ffn_out1=g_ffn_out1,
                 even_in=g_even_in, even_out=g_even_out, odd_in=g_odd_in, odd_out=g_odd_out)
    sums = dict(fin=s_fin, l1n2=s_l1n2, l1n1=s_l1n1, l0n2=s_l0n2, l0n1=s_l0n1, gain=s_gain, gq=s_gq, gk=s_gk,
                sink=s_sink, dlb_f=dlb_f, dlb_b=dlb_b)
    return loss, dx0, grads, sums


def _place():
    return lax.axis_index("x"), lax.axis_index("y"), lax.axis_index("c")


def _ag8(blk, *, name):
    r, c = blk.shape
    flips = [(dx, dy, dc) for dx in (0, 1) for dy in (0, 1) for dc in (0, 1) if (dx, dy, dc) != (0, 0, 0)]

    def body(x_ref, out_ref, send_sems, recv_sems, local_sem):
        ax, ay, ac = _place()
        me = 4 * ax + 2 * ay + ac
        mine = pltpu.make_async_copy(x_ref, out_ref.at[me], local_sem)
        mine.start()
        sent = []
        for k, (dx, dy, dc) in enumerate(flips):
            peer = (lax.rem(ax + dx, 2), lax.rem(ay + dy, 2), lax.rem(ac + dc, 2))
            cp = pltpu.make_async_remote_copy(src_ref=x_ref, dst_ref=out_ref.at[me], send_sem=send_sems.at[k],
                                              recv_sem=recv_sems.at[k], device_id=peer, device_id_type=MESH)
            cp.start()
            sent.append((cp, 4 * peer[0] + 2 * peer[1] + peer[2]))
        for k, (cp, pidx) in enumerate(sent):
            pltpu.make_async_remote_copy(src_ref=x_ref, dst_ref=out_ref.at[pidx], send_sem=send_sems.at[k],
                                         recv_sem=recv_sems.at[k], device_id=(ax, ay, ac),
                                         device_id_type=MESH).wait_recv()
        for cp, _ in sent:
            cp.wait_send()
        mine.wait()

    return _pcall(
        body, name=name,
        in_specs=[pl.BlockSpec(memory_space=pltpu.VMEM)],
        out_specs=pl.BlockSpec(memory_space=pltpu.VMEM),
        out_shape=jax.ShapeDtypeStruct((8, r, c), blk.dtype),
        scratch_shapes=[pltpu.SemaphoreType.DMA((7,)), pltpu.SemaphoreType.DMA((7,)), pltpu.SemaphoreType.DMA],
    )(blk)


_HBM = pl.BlockSpec(memory_space=pltpu.HBM)
_SEM = pl.BlockSpec(memory_space=pltpu.SEMAPHORE)
_DATAFLOW = pltpu.SideEffectType.DATAFLOW_SIDE_EFFECTING


def _split_start(bufs, plan, k, *, name):
    n = len(bufs)

    def body(*refs):
        ins, send_sems, recv_sems, token = refs[:n], refs[n], refs[n + 1], refs[2 * n + 2]
        for i, (src, dst, dev) in enumerate(plan(ins)):
            pltpu.make_async_remote_copy(src_ref=src, dst_ref=dst, send_sem=send_sems.at[i], recv_sem=recv_sems.at[i],
                                         device_id=dev, device_id_type=MESH).start()
        token[...] = jnp.zeros_like(token)

    res = _pcall(
        body, name=name,
        out_shape=(pltpu.SemaphoreType.DMA((k,)), pltpu.SemaphoreType.DMA((k,)),
                   *[pltpu.HBM(b.shape, b.dtype) for b in bufs], jax.ShapeDtypeStruct((8, 128), F32)),
        in_specs=[_HBM] * n, out_specs=(_SEM, _SEM, *[_HBM] * n, pl.BlockSpec(memory_space=pltpu.VMEM)),
        input_output_aliases={i: 2 + i for i in range(n)},
        compiler_params=pltpu.CompilerParams(has_side_effects=_DATAFLOW),
    )(*[pltpu.with_memory_space_constraint(b, pltpu.HBM) for b in bufs])
    return res[0], res[1], list(res[2:2 + n]), res[2 + n]


def _split_wait(bufs, send_sems, recv_sems, plan, after, *, name):
    n = len(bufs)

    def body(*refs):
        ins, ssem, rsem = refs[:n], refs[n], refs[n + 1]
        for i, (src, dst, dev) in enumerate(plan(ins)):
            cp = pltpu.make_async_remote_copy(src_ref=src, dst_ref=dst, send_sem=ssem.at[i], recv_sem=rsem.at[i],
                                              device_id=dev, device_id_type=MESH)
            cp.wait_send()
            cp.wait_recv()

    res = _pcall(
        body, name=name, out_shape=tuple(pltpu.HBM(b.shape, b.dtype) for b in bufs),
        in_specs=[_HBM] * n + [_SEM, _SEM, pl.BlockSpec(memory_space=pl.ANY)], out_specs=tuple([_HBM] * n),
        input_output_aliases={i: i for i in range(n)},
        compiler_params=pltpu.CompilerParams(has_side_effects=_DATAFLOW),
    )(*bufs, send_sems, recv_sems, after)
    return list(res)


_CHIP_FLIPS = [(1, 0), (0, 1), (1, 1)]


class _GatheredWeights:
    FIRST = ('even_in', 'even_out')
    REST = ('ffn_in', 'ffn_out', 'odd_in', 'odd_out')

    def __init__(self, shards, reducer):
        self.shards = shards
        self.early_grads = functools.partial(reducer.start, 'early')
        self.ici = {}
        for grp, names in (('first', self.FIRST), ('rest', self.REST)):
            src = [shards[nm].reshape(2, shards[nm].shape[0] // 2, shards[nm].shape[1]) for nm in names]
            land = [lax.empty((4,) + a.shape, a.dtype) for a in src]
            m = len(names)
            sends, recvs, bufs, token = _split_start(src + land, functools.partial(self._ici_plan, m, True), 3 * m,
                                                     name='gather_' + grp + '_ici_start')
            self.ici[grp] = (sends, recvs, bufs, m)
            self.token = token if grp == 'first' else self.token + token
        self.rest_d2d = None

    @staticmethod
    def _ici_plan(m, sending, refs):
        ax, ay, ac = _place()
        s = 2 * ax + ay
        out = []
        for a in range(m):
            for dx, dy in _CHIP_FLIPS:
                px, py = lax.rem(ax + dx, 2), lax.rem(ay + dy, 2)
                slot = s if sending else 2 * px + py
                out.append((refs[a].at[ac], refs[m + a].at[slot, ac], (px, py, ac)))
        return out

    @staticmethod
    def _d2d_plan(m, sending, refs):
        ax, ay, ac = _place()
        out = []
        for a in range(m):
            for dx, dy in _CHIP_FLIPS:
                sp = 2 * lax.rem(ax + dx, 2) + lax.rem(ay + dy, 2)
                out.append((refs[a].at[sp, ac], refs[a].at[sp, ac if sending else 1 - ac], (ax, ay, 1 - ac)))
        return out

    def _landed(self, grp, after):
        sends, recvs, bufs, m = self.ici[grp]
        bufs = _split_wait(bufs, sends, recvs, functools.partial(self._ici_plan, m, False), after,
                           name='gather_' + grp + '_ici_wait')
        sends, recvs, land, _ = _split_start(bufs[m:], functools.partial(self._d2d_plan, m, True), 3 * m,
                                             name='gather_' + grp + '_d2d_start')
        return sends, recvs, land, m

    def _full(self, grp, names, d2d, after):
        sends, recvs, land, m = d2d
        land = _split_wait(land, sends, recvs, functools.partial(self._d2d_plan, m, False), after,
                           name='gather_' + grp + '_d2d_wait')
        s = 2 * lax.axis_index("x") + lax.axis_index("y")
        slot = lax.broadcasted_iota(jnp.int32, (4, 1, 1), 0)
        return {nm: _from_shards(nm, jnp.where(slot == s, self.shards[nm][None], g.reshape((4,) + self.shards[nm].shape)))
                for nm, g in zip(names, land)}

    def first(self, after):
        return self._full('first', self.FIRST, self._landed('first', after), after)

    def rest_landed(self, after):
        self.rest_d2d = self._landed('rest', after)

    def rest(self, after):
        return self._full('rest', self.REST, self.rest_d2d, after)


def _to_sibling(arrs, *, name):
    n = len(arrs)

    def body(*refs):
        ins, outs = refs[:n], refs[n:2 * n]
        send_sems, recv_sems = refs[2 * n:]
        ax, ay, ac = _place()
        cps = [pltpu.make_async_remote_copy(src_ref=ins[a], dst_ref=outs[a], send_sem=send_sems.at[a],
                                            recv_sem=recv_sems.at[a], device_id=(ax, ay, 1 - ac),
                                            device_id_type=MESH) for a in range(n)]
        for cp in cps:
            cp.start()
        for cp in cps:
            cp.wait_recv()
        for cp in cps:
            cp.wait_send()

    hbm = pl.BlockSpec(memory_space=pl.ANY)
    return _pcall(
        body, name=name, in_specs=[hbm] * n, out_specs=[hbm] * n,
        out_shape=[jax.ShapeDtypeStruct(a.shape, a.dtype) for a in arrs],
        scratch_shapes=[pltpu.SemaphoreType.DMA((n,))] * 2,
    )(*arrs)


def _mod_fwd(cond_raw, mw, mb, *, name):
    _, d, n = mw.shape

    def body(c_ref, w_ref, b_ref, o_ref):
        cv = c_ref[...]
        o_ref[...] = _dot(cv * _sigmoid(cv), w_ref[...]) + b_ref[...]

    return _pcall(
        body, name=name, grid=(2,),
        in_specs=[pl.BlockSpec((16, d), lambda l: (0, 0)), pl.BlockSpec((None, d, n), lambda l: (l, 0, 0)),
                  pl.BlockSpec((None, 1, n), lambda l: (l, 0, 0))],
        out_specs=pl.BlockSpec((None, 16, n), lambda l: (l, 0, 0)),
        out_shape=jax.ShapeDtypeStruct((2, 16, n), F32),
    )(cond_raw, mw, mb)


def _mod_bwd(cond_raw, dms, mw, *, name):
    _, d, n = mw.shape

    def body(c_ref, dm_ref, w_ref, gw_ref, dc_ref):
        @pl.when(pl.program_id(0) == 0)
        def _():
            dc_ref[...] = jnp.zeros_like(dc_ref)
        cv = c_ref[...]
        gw_ref[...] = _dot_tn(cv * _sigmoid(cv), dm_ref[...])
        dc_ref[...] += _dot_nt(dm_ref[...], w_ref[...])

    return _pcall(
        body, name=name, grid=(2,),
        in_specs=[pl.BlockSpec((16, d), lambda l: (0, 0)), pl.BlockSpec((None, 16, n), lambda l: (l, 0, 0)),
                  pl.BlockSpec((None, d, n), lambda l: (l, 0, 0))],
        out_specs=[pl.BlockSpec((None, d, n), lambda l: (l, 0, 0)), pl.BlockSpec((16, d), lambda l: (0, 0))],
        out_shape=[jax.ShapeDtypeStruct((2, d, n), F32), jax.ShapeDtypeStruct((16, d), F32)],
    )(cond_raw, dms, mw)


def _lb_fwd(hgrn_lb, *, name):
    def body(a_ref, o_ref):
        a0, a1 = a_ref[0:1, :], a_ref[1:2, :]
        m = jnp.maximum(a0, a1)
        e0, e1 = jnp.exp(a0 - m), jnp.exp(a1 - m)
        o_ref[...] = e0 / (e0 + e1)

    return _pcall(body, name=name, out_shape=jax.ShapeDtypeStruct((1, hgrn_lb.shape[1]), F32))(hgrn_lb)


PACK_TILES = ('l0n1', 'l0n2', 'l1n1', 'l1n2', 'fin', 'gq', 'gk', 'gain', 'dlb_f', 'dlb_b', 'sink')
PACK_ROW = {nm: 8 * i for i, nm in enumerate(PACK_TILES)}
MOD_SOURCE = ((('l0n1', 0), ('l0n1', 1), ('l0n2', 2), ('l0n2', 0), ('l0n2', 1), ('l1n1', 2)),
              (('l1n1', 0), ('l1n1', 1), ('l1n2', 2), ('l1n2', 0), ('l1n2', 1), ('fin', 2)))


def _small_finalize(gath, lb_pad, *, name):
    d = gath.shape[2]

    def body(g_ref, lb_ref, small_ref, glb_ref, gmb_ref, dm_ref):
        tot = g_ref[0]
        for e in range(1, 8):
            tot = tot + g_ref[e]

        def row(nm, r=0):
            return tot[PACK_ROW[nm] + r:PACK_ROW[nm] + r + 1, :]

        for k, nm in enumerate(('l0n1', 'l0n2', 'l1n1', 'l1n2')):
            small_ref[k:k + 1, :] = row(nm, 3) + row(nm, 7)
        for k, nm in ((4, 'gq'), (5, 'gk')):
            small_ref[k:k + 1, :] = row(nm) + pltpu.roll(row(nm), d - 64, 1)
        small_ref[6:7, :] = row('gain')
        small_ref[7:8, :] = row('sink')
        lbv = lb_ref[...]
        g0 = (row('dlb_f') + row('dlb_b')) * lbv * (1.0 - lbv)
        glb_ref[...] = jnp.zeros_like(glb_ref)
        glb_ref[0:1, :] = g0
        glb_ref[1:2, :] = -g0
        dm_ref[...] = jnp.zeros_like(dm_ref)
        for l in range(2):
            for part in range(6):
                nm, r = MOD_SOURCE[l][part]
                gmb_ref[l * 6 + part:l * 6 + part + 1, :] = row(nm, r) + row(nm, r + 4)
                rl = PACK_ROW[nm] + r + 4
                for e in range(8):
                    dm_ref[l, part, e:e + 1, :] = g_ref[e, rl:rl + 1, :]
                dm_ref[l, part, 8:9, :] = row(nm, r)

    return _pcall(
        body, name=name,
        out_shape=[jax.ShapeDtypeStruct((8, d), F32), jax.ShapeDtypeStruct((8, d), F32),
                   jax.ShapeDtypeStruct((12, d), F32), jax.ShapeDtypeStruct((2, 6, 16, d), F32)],
    )(gath, lb_pad)


def _cctx_grad(gath, c_ctx2, *, name):
    def body(g_ref, c_ref, o_ref):
        tot = ((g_ref[0, 0:1, :] + g_ref[2, 0:1, :]) + g_ref[4, 0:1, :]) + g_ref[6, 0:1, :]
        cv = c_ref[...]
        s = _sigmoid(cv)
        o_ref[...] = tot * (s * (1.0 + cv * (1.0 - s)))

    return _pcall(body, name=name, out_shape=jax.ShapeDtypeStruct(c_ctx2.shape, F32))(gath, c_ctx2)


def _row_block(r, c, limit=256 * 1024):
    best = None
    for br in range(16, r + 1, 16):
        if r % br == 0 and br * c <= limit:
            best = br
    return best if best is not None else r


def _sum4(own, landed, core, *, name):
    _, r, c = own.shape
    br = _row_block(r, c, 512 * 1024)

    def body(core_ref, own_ref, land_ref, o_ref):
        s = 2 * lax.axis_index("x") + lax.axis_index("y")
        p = [jnp.where(s == k, own_ref[k], land_ref[k]).astype(F32) for k in range(4)]
        o_ref[...] = ((p[0] + p[1]) + p[2]) + p[3]

    blk = pl.BlockSpec((4, br, c), lambda i, core_ref: (0, i, 0))
    spec = pltpu.PrefetchScalarGridSpec(
        num_scalar_prefetch=1, grid=(r // br,), in_specs=[blk, blk],
        out_specs=pl.BlockSpec((None, br, c), lambda i, core_ref: (core_ref[0], i, 0)))
    return _pcall(body, name=name, grid_spec=spec, out_shape=jax.ShapeDtypeStruct((2, r, c), F32))(core, own, landed)


def _exchange_halves(arrs, *, name):
    n = len(arrs)

    def body(*refs):
        ins, outs = refs[:n], refs[n:2 * n]
        send_sems, recv_sems = refs[2 * n:]
        ax, ay, ac = _place()
        cps = [pltpu.make_async_remote_copy(src_ref=ins[a].at[ac], dst_ref=outs[a].at[ac], send_sem=send_sems.at[a],
                                            recv_sem=recv_sems.at[a], device_id=(ax, ay, 1 - ac),
                                            device_id_type=MESH) for a in range(n)]
        for cp in cps:
            cp.start()
        for a in range(n):
            pltpu.make_async_remote_copy(src_ref=ins[a].at[ac], dst_ref=outs[a].at[1 - ac], send_sem=send_sems.at[a],
                                         recv_sem=recv_sems.at[a], device_id=(ax, ay, ac),
                                         device_id_type=MESH).wait_recv()
        for cp in cps:
            cp.wait_send()

    hbm = pl.BlockSpec(memory_space=pl.ANY)
    return _pcall(
        body, name=name, in_specs=[hbm] * n, out_specs=[hbm] * n,
        out_shape=[jax.ShapeDtypeStruct(a.shape, a.dtype) for a in arrs],
        input_output_aliases={a: a for a in range(n)},
        scratch_shapes=[pltpu.SemaphoreType.DMA((n,))] * 2,
    )(*arrs)


def _add2(a, b, *, name):
    r, c = a.shape
    br = _row_block(r, c, 1024 * 1024)

    def body(a_ref, b_ref, o_ref):
        o_ref[...] = (a_ref[...].astype(F32) + b_ref[...].astype(F32)).astype(BF16)

    blk = pl.BlockSpec((br, c), lambda i: (i, 0))
    return _pcall(body, name=name, grid=(r // br,), in_specs=[blk, blk], out_specs=blk,
                  out_shape=jax.ShapeDtypeStruct((r, c), BF16))(a, b)


def _adam(w, gs, m, v, *, name):
    r, c = w.shape
    br = _row_block(r, c)
    ng = len(gs)
    c1 = 1.0 - ADAM_B1 ** ADAM_STEP
    c2 = 1.0 - ADAM_B2 ** ADAM_STEP

    def body(*refs):
        w_ref, m_ref, v_ref = refs[0], refs[1 + ng], refs[2 + ng]
        outs = refs[3 + ng:]
        g = refs[1][...]
        for k in range(1, ng):
            g = g + refs[1 + k][...]
        mn = ADAM_B1 * m_ref[...] + (1.0 - ADAM_B1) * g
        vn = ADAM_B2 * v_ref[...] + (1.0 - ADAM_B2) * (g * g)
        if ng > 1:
            outs[0][...] = g
        d_out, m_out, v_out = outs[-3:]
        m_out[...] = mn
        v_out[...] = vn
        d_out[...] = -ADAM_LR * ((mn / c1) / (jnp.sqrt(vn / c2) + ADAM_EPS) + ADAM_WD * w_ref[...])

    blk = pl.BlockSpec((br, c), lambda i: (i, 0))
    nout = 4 if ng > 1 else 3
    res = _pcall(body, name=name, grid=(r // br,), in_specs=[blk] * (3 + ng), out_specs=[blk] * nout,
                 out_shape=[jax.ShapeDtypeStruct((r, c), F32)] * nout)(w, *gs, m, v)
    return list(res) if ng > 1 else [gs[0]] + list(res)


def _grad_halves(name, g, ac):
    if name.endswith('_in'):
        n = g.shape[1] // 4
        if name == 'ffn_in':
            assert n == FFN_BK
        order = _ffn_order(g.shape[1]) if name == 'ffn_in' else range(4)
        v = jnp.stack([g[:, b * n:(b + 1) * n] for b in order])
        per = [v[:, :g.shape[0] // 2], v[:, g.shape[0] // 2:]]
    else:
        k4, n = g.shape
        v = g.reshape(4, 2, k4 // 8, n)
        per = [v[:, 0], v[:, 1]]
    first = ac == 0
    return _bf(jnp.where(first, per[0], per[1])), _bf(jnp.where(first, per[1], per[0]))


class _GradReducer:
    def __init__(self):
        self.flight = {}

    @staticmethod
    def _plan(m, sending, refs):
        ax, ay, ac = _place()
        s = 2 * ax + ay
        out = []
        for a in range(m):
            for dx, dy in _CHIP_FLIPS:
                px, py = lax.rem(ax + dx, 2), lax.rem(ay + dy, 2)
                sp = 2 * px + py
                out.append((refs[a].at[sp], refs[m + a].at[s if sending else sp], (px, py, ac)))
        return out

    def start(self, grp, grads):
        ac = lax.axis_index("c")
        names = list(grads)
        halves = [_grad_halves(nm.rstrip('01'), grads[nm], ac) for nm in names]
        theirs = _to_sibling([h[1] for h in halves], name='swap_core_halves_' + grp)
        pair = [_add2(h[0].reshape(-1, b.shape[-1]), b.reshape(-1, b.shape[-1]), name='add_cores').reshape(b.shape)
                for h, b in zip(halves, theirs)]
        m = len(names)
        land = [lax.empty(a.shape, a.dtype) for a in pair]
        sends, recvs, bufs, token = _split_start(pair + land, functools.partial(self._plan, m, True), 3 * m,
                                                 name='scatter_' + grp + '_start')
        self.flight[grp] = (names, sends, recvs, bufs)
        return token

    def finish(self, grp, after):
        names, sends, recvs, bufs = self.flight.pop(grp)
        m = len(names)
        bufs = _split_wait(bufs, sends, recvs, functools.partial(self._plan, m, False), after,
                           name='scatter_' + grp + '_wait')
        core = lax.axis_index("c").astype(jnp.int32).reshape(1)
        sums = [_sum4(p, l, core, name='sum_chips') for p, l in zip(bufs[:m], bufs[m:])]
        both = _exchange_halves(sums, name='gather_core_halves_' + grp)
        return {nm: g.reshape(-1, g.shape[-1]) for nm, g in zip(names, both)}


def _from_shards(name, g):
    _, r, n = g.shape
    if name == 'ffn_in':
        assert n == FFN_BK
        v = g.reshape(4, 2, r // 2, n)
        return jnp.concatenate([v[b] for b in _ffn_order(4 * n)], axis=-1)
    if name == 'ffn_out':
        return g.reshape(4, 2, r // 2, n).transpose(1, 0, 2, 3).reshape(2, 2 * r, n)
    if name in ('even_in', 'odd_in'):
        return jnp.concatenate([g[b] for b in range(4)], axis=-1)
    return g.reshape(4 * r, n)


def kernel(x, c, ctx, c_ctx, mod_w, mod_b, norm_g, ffn_w_in, ffn_w_out, even_w_in, even_w_out, attn_qk_norm_g, attn_sink, hgrn_out_norm_g, hgrn_lb, odd_w_in, odd_w_out, loss_target, m_c_ctx, m_mod_w, m_mod_b, m_norm_g, m_ffn_w_in, m_ffn_w_out, m_even_w_in, m_even_w_out, m_attn_qk_norm_g, m_attn_sink, m_hgrn_out_norm_g, m_hgrn_lb, m_odd_w_in, m_odd_w_out, v_c_ctx, v_mod_w, v_mod_b, v_norm_g, v_ffn_w_in, v_ffn_w_out, v_even_w_in, v_even_w_out, v_attn_qk_norm_g, v_attn_sink, v_hgrn_out_norm_g, v_hgrn_lb, v_odd_w_in, v_odd_w_out):
    d = x.shape[-1]
    lc = ctx.shape[1]
    assert lc == TM and d == 1024
    ax, ay, ac = _place()
    s = 2 * ax + ay
    me = 4 * ax + 2 * ay + ac
    nmod = mod_w.shape[2]

    def pad8(v):
        return jnp.pad(v, ((0, 8 - v.shape[0]), (0, 0)))

    pack = jnp.concatenate([pad8(c), pad8(norm_g.reshape(1, d))], axis=0)
    g1 = _ag8(pack, name='gather_cond')
    c_all = g1[:, 0, :]
    ng = g1[0::2, 8, :].reshape(4, 2, 2, d // 4).transpose(1, 2, 0, 3).reshape(4, d)

    cond_raw = jnp.concatenate([c_all, pad8(c_ctx.reshape(1, d))], axis=0)
    mb_sh = lax.dynamic_slice_in_dim(mod_b, s * nmod, nmod, axis=1).reshape(2, 1, nmod)
    mpart = _mod_fwd(cond_raw, mod_w, mb_sh, name='mod_fwd')
    g3 = _ag8(mpart.reshape(32, nmod), name='gather_mods')
    mods_full = g3[0::2].reshape(4, 2, 16, nmod).transpose(1, 2, 0, 3).reshape(2, 16, 4 * nmod)
    m_lat = lax.dynamic_index_in_dim(mods_full, me, axis=1, keepdims=False)
    mods = jnp.stack([mods_full[:, 8], m_lat], axis=1).reshape(24, d)

    names = ['ffn_in', 'ffn_out', 'even_in', 'even_out', 'odd_in', 'odd_out']
    shards = [_bf(v.reshape(-1, v.shape[-1])) for v in (ffn_w_in, ffn_w_out, even_w_in, even_w_out, odd_w_in, odd_w_out)]
    shards, mods = lax.optimization_barrier((shards, mods))
    reducer = _GradReducer()
    wsrc = _GatheredWeights(dict(zip(names, shards)), reducer)

    lb = _lb_fwd(hgrn_lb, name='hgrn_lower_bound')
    small = dict(gq=jnp.tile(attn_qk_norm_g[0, 0], 2).reshape(1, 128), gk=jnp.tile(attn_qk_norm_g[0, 1], 2).reshape(1, 128),
                 sink=attn_sink[0], gain=hgrn_out_norm_g, lb=lb)
    x0 = jnp.concatenate([ctx[0], x[0]], axis=0) + wsrc.token[0, 0]
    loss_t, dx0, grads, sums = _local_step(x0, loss_target[0], mods, ng, wsrc, small)
    loss = lax.psum(loss_t[0, 0], ("x", "y", "c"))
    grad_x = dx0[None]

    def tile(v):
        return jnp.pad(v, ((0, 8 - v.shape[0]), (0, d - v.shape[1])))

    sums = dict(sums, sink=sums['sink'][:, 0].reshape(1, 8))
    g4 = _ag8(jnp.concatenate([tile(sums[nm]) for nm in PACK_TILES], axis=0), name='gather_row_sums')
    small_g, glb, gmb, dmat = _small_finalize(g4, tile(lb)[0:1], name='small_grads')
    dms = lax.dynamic_slice_in_dim(dmat.transpose(0, 2, 1, 3).reshape(2, 16, 6 * d), s * nmod, nmod, axis=2)
    g_mod_w, dcond = _mod_bwd(cond_raw, dms, mod_w, name='mod_bwd')
    g5 = _ag8(dcond[8:16], name='gather_dcond')
    g_c_ctx = _cctx_grad(g5, c_ctx.reshape(8, d // 8).reshape(1, d), name='c_ctx_grad')

    late = {nm: grads[nm] for nm in ('ffn_in0', 'ffn_out0', 'even_in', 'even_out')}
    late, g_c_ctx = lax.optimization_barrier((late, g_c_ctx))
    token = reducer.start('late', late)
    full = reducer.finish('early', token)

    def upd(wv, gs, mv, vv, name):
        shp = wv.shape
        c2 = shp[-1]
        out = _adam(wv.reshape(-1, c2), [g.reshape(-1, c2) for g in gs], mv.reshape(-1, c2), vv.reshape(-1, c2), name=name)
        return [o.reshape(shp) for o in out]

    res = {}
    res['c_ctx'] = upd(c_ctx.reshape(8, d // 8), [g_c_ctx.reshape(8, d // 8)], m_c_ctx.reshape(8, d // 8), v_c_ctx.reshape(8, d // 8), 'adam_c_ctx')
    res['c_ctx'] = [o.reshape(d) for o in res['c_ctx']]
    res['mod_w'] = upd(mod_w, [g_mod_w], m_mod_w, v_mod_w, 'adam_mod_w')
    res['mod_b'] = upd(mod_b, [gmb.reshape(2, 6 * d)], m_mod_b, v_mod_b, 'adam_mod_b')
    g_ng = lax.dynamic_slice_in_dim(small_g[0:4].reshape(2, 2, d), s * (d // 4), d // 4, axis=2)
    res['norm_g'] = upd(norm_g, [g_ng], m_norm_g, v_norm_g, 'adam_norm_g')
    g_qk = jnp.stack([small_g[4, 0:64], small_g[5, 0:64]]).reshape(1, 2, 64)
    res['attn_qk_norm_g'] = upd(attn_qk_norm_g, [g_qk], m_attn_qk_norm_g, v_attn_qk_norm_g, 'adam_qk_gain')
    res['attn_sink'] = upd(attn_sink, [small_g[7, 0:8].reshape(1, 8)], m_attn_sink, v_attn_sink, 'adam_sink')
    res['hgrn_out_norm_g'] = upd(hgrn_out_norm_g, [small_g[6, 0:128].reshape(1, 128)], m_hgrn_out_norm_g, v_hgrn_out_norm_g, 'adam_head_gain')
    res['hgrn_lb'] = upd(hgrn_lb, [glb[0:2, 0:hgrn_lb.shape[1]]], m_hgrn_lb, v_hgrn_lb, 'adam_hgrn_lb')
    res['odd_w_in'] = upd(odd_w_in, [full['odd_in']], m_odd_w_in, v_odd_w_in, 'adam_odd_in')
    res['odd_w_out'] = upd(odd_w_out, [full['odd_out']], m_odd_w_out, v_odd_w_out, 'adam_odd_out')
    full.update(reducer.finish('late', res['odd_w_in'][1]))
    g_ffn_in = jnp.concatenate([full['ffn_in0'], full['ffn_in1']], axis=0)
    g_ffn_out = jnp.concatenate([full['ffn_out0'], full['ffn_out1']], axis=0)
    res['ffn_w_in'] = upd(ffn_w_in, [g_ffn_in], m_ffn_w_in, v_ffn_w_in, 'adam_ffn_in')
    res['ffn_w_out'] = upd(ffn_w_out, [g_ffn_out], m_ffn_w_out, v_ffn_w_out, 'adam_ffn_out')
    res['even_w_in'] = upd(even_w_in, [full['even_in']], m_even_w_in, v_even_w_in, 'adam_even_in')
    res['even_w_out'] = upd(even_w_out, [full['even_out']], m_even_w_out, v_even_w_out, 'adam_even_out')

    order = ['c_ctx', 'mod_w', 'mod_b', 'norm_g', 'ffn_w_in', 'ffn_w_out', 'even_w_in', 'even_w_out',
             'attn_qk_norm_g', 'attn_sink', 'hgrn_out_norm_g', 'hgrn_lb', 'odd_w_in', 'odd_w_out']
    outs = [loss, grad_x]
    for k in range(4):
        outs += [res[nm][k] for nm in order]
    return tuple(outs)
```

```python
import functools
import math

import numpy as np
import jax
import jax.numpy as jnp
from jax import lax
from jax.experimental import pallas as pl
from jax.experimental.pallas import tpu as pltpu

F32 = jnp.float32
BF16 = jnp.bfloat16
EPS = 1e-6
TM = 256
CHUNK = 64
QB = 128
WINDOW = 128
NEG = -1e30
MESH = pl.DeviceIdType.MESH

ADAM_LR, ADAM_B1, ADAM_B2, ADAM_EPS, ADAM_WD, ADAM_STEP = 0.001, 0.9, 0.999, 1e-08, 0.01, 10


def _pcall(body, **kw):
    return pl.pallas_call(body, **kw)


def _pick(n, cap):
    best = None
    for m in range(128, min(n, cap) + 1, 128):
        if n % m == 0:
            best = m
    assert best is not None, (n, cap)
    return best


def _bf(x):
    return x.astype(BF16)


def _dot(a, b):
    return jnp.dot(_bf(a), _bf(b), preferred_element_type=F32)


def _dot_nt(a, b):
    return lax.dot_general(_bf(a), _bf(b), (((1,), (1,)), ((), ())), preferred_element_type=F32)


def _dot_tn(a, b):
    return lax.dot_general(_bf(a), _bf(b), (((0,), (0,)), ((), ())), preferred_element_type=F32)


def _dot_exact(a, b):
    return jnp.dot(a, b, preferred_element_type=F32, precision=lax.Precision.HIGHEST)


def _sigmoid(x):
    return 1.0 / (1.0 + jnp.exp(-x))


def _iota(shape, dim):
    return lax.broadcasted_iota(jnp.int32, shape, dim)


def _mm_nn(a, b, *, lead=None, out_dtype=F32, name):
    m, k = a.shape
    n = b.shape[-1]
    bm = 1408 if (m % 1408 == 0 and k <= 1024) else (768 if m % 768 == 0 else TM)
    bn = _pick(n, 1024) if n % 512 == 0 else _pick(n, 1664)

    def body(a_ref, b_ref, o_ref):
        o_ref[...] = _dot(a_ref[...], b_ref[...]).astype(o_ref.dtype)

    if lead is None:
        b_spec = pl.BlockSpec((k, bn), lambda i, j: (0, j))
    else:
        b_spec = pl.BlockSpec((None, k, bn), lambda i, j: (lead, 0, j))
    return _pcall(
        body, name=name, grid=(m // bm, n // bn),
        in_specs=[pl.BlockSpec((bm, k), lambda i, j: (i, 0)), b_spec],
        out_specs=pl.BlockSpec((bm, bn), lambda i, j: (i, j)),
        out_shape=jax.ShapeDtypeStruct((m, n), out_dtype),
    )(a, b)


def _mm_nt(a, b, *, lead=None, name):
    m, n = a.shape
    k = b.shape[-2]
    bm = 768 if m % 768 == 0 else TM
    bk = _pick(k, 512)

    def body(a_ref, b_ref, o_ref):
        o_ref[...] = _dot_nt(a_ref[...], b_ref[...])

    if lead is None:
        b_spec = pl.BlockSpec((bk, n), lambda i, j: (j, 0))
    else:
        b_spec = pl.BlockSpec((None, bk, n), lambda i, j: (lead, j, 0))
    return _pcall(
        body, name=name, grid=(m // bm, k // bk),
        in_specs=[pl.BlockSpec((bm, n), lambda i, j: (i, 0)), b_spec],
        out_specs=pl.BlockSpec((bm, bk), lambda i, j: (i, j)),
        out_shape=jax.ShapeDtypeStruct((m, k), F32),
    )(a, b)


def _mm_tn(a, b, *, name):
    t, k = a.shape
    n = b.shape[1]
    bt = 768 if t % 768 == 0 else TM
    bk = _pick(k, 1536)
    bn = _pick(n, 1024) if n % 1024 == 0 or n < 1664 else _pick(n, 1664)

    def body(a_ref, b_ref, o_ref):
        @pl.when(pl.program_id(2) == 0)
        def _():
            o_ref[...] = jnp.zeros_like(o_ref)
        o_ref[...] += _dot_tn(a_ref[...], b_ref[...])

    return _pcall(
        body, name=name, grid=(k // bk, n // bn, t // bt),
        in_specs=[pl.BlockSpec((bt, bk), lambda i, j, s: (s, i)),
                  pl.BlockSpec((bt, bn), lambda i, j, s: (s, j))],
        out_specs=pl.BlockSpec((bk, bn), lambda i, j, s: (i, j)),
        out_shape=jax.ShapeDtypeStruct((k, n), F32),
    )(a, b)


def _mod_row(mods_ref, lat, idx):
    return jnp.where(lat, mods_ref[idx + 6:idx + 7, :], mods_ref[idx:idx + 1, :])


def _row_fwd(x, mods, *, y=None, gate=None, g=None, shift=None, scale=None, name):
    t, d = x.shape
    has_y, has_n = y is not None, g is not None

    def body(*refs):
        refs = list(refs)
        x_ref, mods_ref = refs[0], refs[1]
        pos = 2
        if has_y:
            y_ref = refs[pos]; pos += 1
        if has_n:
            g_ref = refs[pos]; pos += 1
        outs = refs[pos:]
        lat = pl.program_id(0) > 0
        x1 = x_ref[...]
        o = 0
        if has_y:
            x1 = x1 + _mod_row(mods_ref, lat, gate) * y_ref[...]
            outs[o][...] = x1; o += 1
        if has_n:
            rs = lax.rsqrt(jnp.mean(x1 * x1, axis=-1, keepdims=True) + EPS)
            hn = x1 * rs * g_ref[...]
            h = hn * (1.0 + _mod_row(mods_ref, lat, scale)) + _mod_row(mods_ref, lat, shift)
            outs[o][...] = h.astype(BF16)

    row = pl.BlockSpec((TM, d), lambda i: (i, 0))
    ins, specs = [x, mods], [row, pl.BlockSpec(mods.shape, lambda i: (0, 0))]
    if has_y:
        ins.append(y); specs.append(row)
    if has_n:
        ins.append(g.reshape(1, d)); specs.append(pl.BlockSpec((1, d), lambda i: (0, 0)))
    out_shape, out_specs = [], []
    if has_y:
        out_shape.append(jax.ShapeDtypeStruct((t, d), F32)); out_specs.append(row)
    if has_n:
        out_shape.append(jax.ShapeDtypeStruct((t, d), BF16)); out_specs.append(row)
    res = _pcall(body, name=name, grid=(t // TM,), in_specs=specs, out_specs=out_specs,
                 out_shape=out_shape)(*ins)
    return res


def _acc_row(ref, r, val):
    ref[r:r + 1, :] += val


def _row_final(x, z, mods, target, *, gate, name):
    t, d = x.shape

    def body(x_ref, mods_ref, z_ref, t_ref, loss_ref, dx_ref, dz_ref, sums_ref):
        i = pl.program_id(0)
        lat = i > 0

        @pl.when(i == 0)
        def _():
            loss_ref[...] = jnp.zeros_like(loss_ref)
            sums_ref[...] = jnp.zeros_like(sums_ref)

        gt = _mod_row(mods_ref, lat, gate)
        zz = z_ref[...]
        yv = x_ref[...] + gt * zz
        keep = jnp.where(lat, 1.0, 0.0).astype(F32)
        diff = (yv - t_ref[...]) * keep
        part = jnp.sum(jnp.sum(diff * diff, axis=0, keepdims=True), axis=1, keepdims=True)
        loss_ref[...] += part * (0.5 / d)
        dy = diff * (1.0 / d)
        dx_ref[...] = dy
        dz_ref[...] = (gt * dy).astype(BF16)
        _acc_row(sums_ref, 6, jnp.sum(dy * zz, axis=0, keepdims=True))

    row = pl.BlockSpec((TM, d), lambda i: (i, 0))
    return _pcall(
        body, name=name, grid=(t // TM,),
        in_specs=[row, pl.BlockSpec(mods.shape, lambda i: (0, 0)), row,
                  pl.BlockSpec((TM, d), lambda i: (jnp.maximum(i - 1, 0), 0))],
        out_specs=[pl.BlockSpec((8, 128), lambda i: (0, 0)), row, row,
                   pl.BlockSpec((8, d), lambda i: (0, 0))],
        out_shape=[jax.ShapeDtypeStruct((8, 128), F32), jax.ShapeDtypeStruct((t, d), F32),
                   jax.ShapeDtypeStruct((t, d), BF16), jax.ShapeDtypeStruct((8, d), F32)],
    )(x, mods, z, target)


def _row_bwd(xn, dxo, dh, mods, g, *, shift, scale, y=None, gate=None, latent_only=False, name):
    t, d = xn.shape
    has_y = y is not None

    def body(*refs):
        refs = list(refs)
        x_ref, dxo_ref, dh_ref, mods_ref, g_ref = refs[:5]
        pos = 5
        if has_y:
            y_ref = refs[pos]; pos += 1
        dx_ref = refs[pos]; pos += 1
        if has_y:
            dy_ref = refs[pos]; pos += 1
        sums_ref = refs[pos]
        i = pl.program_id(0)
        lat = i > 0

        @pl.when(i == 0)
        def _():
            sums_ref[...] = jnp.zeros_like(sums_ref)

        x1 = x_ref[...]
        gv = g_ref[...]
        rs = lax.rsqrt(jnp.mean(x1 * x1, axis=-1, keepdims=True) + EPS)
        xh = x1 * rs
        dhv = dh_ref[...]
        dn = dhv * (1.0 + _mod_row(mods_ref, lat, scale))
        dxh = dn * gv
        dx = dxo_ref[...] + rs * (dxh - xh * jnp.mean(dxh * xh, axis=-1, keepdims=True))
        dx_ref[...] = dx
        vals = [jnp.sum(dhv, axis=0, keepdims=True),
                jnp.sum(dhv * (xh * gv), axis=0, keepdims=True),
                None,
                jnp.sum(dn * xh, axis=0, keepdims=True)]
        if has_y:
            dy_ref[...] = (_mod_row(mods_ref, lat, gate) * dx).astype(BF16)
            vals[2] = jnp.sum(dx * y_ref[...], axis=0, keepdims=True)

        @pl.when(i == 0)
        def _():
            for r, v in enumerate(vals):
                if v is not None:
                    _acc_row(sums_ref, r, v)

        @pl.when(i > 0)
        def _():
            for r, v in enumerate(vals):
                if v is not None:
                    _acc_row(sums_ref, 4 + r, v)

    row = pl.BlockSpec((TM, d), lambda i: (i, 0))
    ins = [xn, dxo, dh, mods, g.reshape(1, d)]
    specs = [row, row, row, pl.BlockSpec(mods.shape, lambda i: (0, 0)), pl.BlockSpec((1, d), lambda i: (0, 0))]
    if latent_only:
        out_shape = [jax.ShapeDtypeStruct((t - TM, d), F32)]
        out_specs = [pl.BlockSpec((TM, d), lambda i: (jnp.maximum(i - 1, 0), 0))]
    else:
        out_shape, out_specs = [jax.ShapeDtypeStruct((t, d), F32)], [row]
    if has_y:
        ins.append(y); specs.append(row)
        out_shape.append(jax.ShapeDtypeStruct((t, d), BF16)); out_specs.append(row)
    out_shape.append(jax.ShapeDtypeStruct((8, d), F32))
    out_specs.append(pl.BlockSpec((8, d), lambda i: (0, 0)))
    return _pcall(body, name=name, grid=(t // TM,), in_specs=specs, out_specs=out_specs,
                  out_shape=out_shape)(*ins)


FFN_BK = 1408


FFN_SUB = 256


def _ffn_order(n2):
    nb = n2 // (2 * FFN_BK)
    return [h * nb + j for j in range(nb) for h in (0, 1)]


def _ffn_interleave(w):
    return jnp.concatenate([w[..., b * FFN_BK:(b + 1) * FFN_BK] for b in _ffn_order(w.shape[-1])], axis=-1)


def _ffn_deinterleave(w):
    order = _ffn_order(w.shape[-1])
    return jnp.concatenate([w[..., order.index(b) * FFN_BK:(order.index(b) + 1) * FFN_BK]
                            for b in range(len(order))], axis=-1)


def _big_tile(t):
    return 768 if t % 768 == 0 else TM


def _ffn_in(h, w, *, lead, name):
    t, d = h.shape
    n2 = w.shape[-1]
    bm, bk = _big_tile(t), FFN_BK

    def body(h_ref, w_ref, u_ref, a_ref):
        hb = h_ref[...]
        for c0 in range(0, bk, FFN_SUB):
            c1 = min(c0 + FFN_SUB, bk)
            ug = _dot(hb, w_ref[:, c0:c1]).astype(BF16)
            uu = _dot(hb, w_ref[:, bk + c0:bk + c1]).astype(BF16)
            u_ref[:, c0:c1] = ug
            u_ref[:, bk + c0:bk + c1] = uu
            gv, up = ug.astype(F32), uu.astype(F32)
            a_ref[:, c0:c1] = (gv * _sigmoid(gv) * up).astype(BF16)

    return _pcall(
        body, name=name, grid=(t // bm, n2 // (2 * bk)),
        in_specs=[pl.BlockSpec((bm, d), lambda i, j: (i, 0)),
                  pl.BlockSpec((None, d, 2 * bk), lambda i, j: (lead, 0, j))],
        out_specs=[pl.BlockSpec((bm, 2 * bk), lambda i, j: (i, j)), pl.BlockSpec((bm, bk), lambda i, j: (i, j))],
        out_shape=[jax.ShapeDtypeStruct((t, n2), BF16), jax.ShapeDtypeStruct((t, n2 // 2), BF16)],
    )(h, w)


def _ffn_dx(dz, w_out, u, *, lead, name):
    t, d = dz.shape
    n2 = u.shape[1]
    bm, bk = _big_tile(t), FFN_BK

    def body(dz_ref, w_ref, u_ref, du_ref):
        dzb = dz_ref[...]
        for c0 in range(0, bk, FFN_SUB):
            c1 = min(c0 + FFN_SUB, bk)
            da = _dot_nt(dzb, w_ref[c0:c1, :])
            gv, up = u_ref[:, c0:c1].astype(F32), u_ref[:, bk + c0:bk + c1].astype(F32)
            s = _sigmoid(gv)
            du_ref[:, c0:c1] = (da * up * (s * (1.0 + gv * (1.0 - s)))).astype(BF16)
            du_ref[:, bk + c0:bk + c1] = (da * gv * s).astype(BF16)

    ublk = pl.BlockSpec((bm, 2 * bk), lambda i, j: (i, j))
    return _pcall(
        body, name=name, grid=(t // bm, n2 // (2 * bk)),
        in_specs=[pl.BlockSpec((bm, d), lambda i, j: (i, 0)),
                  pl.BlockSpec((None, bk, d), lambda i, j: (lead, j, 0)), ublk],
        out_specs=ublk, out_shape=jax.ShapeDtypeStruct((t, n2), BF16),
    )(dz, w_out, u)


def _lane(shape):
    return _iota(shape, len(shape) - 1)


def _pair_norm(x, g):
    lo = _lane(x.shape) < 64
    x2 = x * x
    s_lo = jnp.sum(jnp.where(lo, x2, 0.0), axis=-1, keepdims=True)
    s_hi = jnp.sum(jnp.where(lo, 0.0, x2), axis=-1, keepdims=True)
    rs = lax.rsqrt(jnp.where(lo, s_lo, s_hi) * (1.0 / 64) + EPS)
    return x * rs, rs


def _pair_mean(v):
    lo = _lane(v.shape) < 64
    s_lo = jnp.sum(jnp.where(lo, v, 0.0), axis=-1, keepdims=True)
    s_hi = jnp.sum(jnp.where(lo, 0.0, v), axis=-1, keepdims=True)
    return jnp.where(lo, s_lo, s_hi) * (1.0 / 64)


def _rot64(x):
    r1 = pltpu.roll(x, 32, 1)
    r2 = pltpu.roll(x, 96, 1)
    even = ((_lane(x.shape) >> 5) & 1) == 0
    return jnp.where(even, -r2, r1)


def _rope64(x, cos, sin):
    return x * cos + _rot64(x) * sin


def _rope64_t(d, cos, sin):
    return d * cos - _rot64(d * sin)


def _kprep_fwd(p, gk, cos, sin, *, name):
    t = p.shape[0]

    def body(k_ref, g_ref, c_ref, s_ref, o_ref):
        xh, _ = _pair_norm(k_ref[...], None)
        o_ref[...] = _rope64(xh * g_ref[...], c_ref[...], s_ref[...])

    blk = pl.BlockSpec((TM, 128), lambda i: (i, 0))
    return _pcall(
        body, name=name, grid=(t // TM,),
        in_specs=[pl.BlockSpec((TM, 128), lambda i: (i, 4)), pl.BlockSpec((1, 128), lambda i: (0, 0)), blk, blk],
        out_specs=blk, out_shape=jax.ShapeDtypeStruct((t, 128), F32),
    )(p, gk, cos, sin)


def _kprep_bwd(p, gk, cos, sin, dkp, dv, *, name):
    t = p.shape[0]

    def body(k_ref, g_ref, c_ref, s_ref, dkp_ref, dv_ref, o_ref, dg_ref):
        @pl.when(pl.program_id(0) == 0)
        def _():
            dg_ref[...] = jnp.zeros_like(dg_ref)
        xh, rs = _pair_norm(k_ref[...], None)
        dn = _rope64_t(dkp_ref[...], c_ref[...], s_ref[...])
        _acc_row(dg_ref, 0, jnp.sum(dn * xh, axis=0, keepdims=True))
        dxh = dn * g_ref[...]
        o_ref[:, 0:128] = (rs * (dxh - xh * _pair_mean(dxh * xh))).astype(BF16)
        o_ref[:, 128:256] = dv_ref[...].astype(BF16)

    blk = pl.BlockSpec((TM, 128), lambda i: (i, 0))
    return _pcall(
        body, name=name, grid=(t // TM,),
        in_specs=[pl.BlockSpec((TM, 128), lambda i: (i, 4)), pl.BlockSpec((1, 128), lambda i: (0, 0)), blk, blk, blk, blk],
        out_specs=[pl.BlockSpec((TM, 256), lambda i: (i, 0)), pl.BlockSpec((8, 128), lambda i: (0, 0))],
        out_shape=[jax.ShapeDtypeStruct((t, 256), BF16), jax.ShapeDtypeStruct((8, 128), F32)],
    )(p, gk, cos, sin, dkp, dv)


def _attn_common(i, t, lc, kp_ref, v_ref):
    span = QB + 2 * WINDOW
    start = pl.multiple_of(jnp.clip((i - 1) * QB, lc, t - span), QB)
    kall = jnp.concatenate([kp_ref[0:lc, :], kp_ref[pl.ds(start, span), :]], axis=0)
    vall = jnp.concatenate([v_ref[0:lc, :], v_ref[pl.ds(start, span), :]], axis=0)
    nk = lc + span
    col = _iota((QB, nk), 1)
    krow = jnp.where(col < lc, col, start + col - lc)
    qrow = i * QB + _iota((QB, nk), 0)
    valid = (col < lc) | ((qrow >= lc) & (krow >= lc) & (jnp.abs(krow - qrow) <= WINDOW))
    lo = _lane(kall.shape) < 64
    kroll, vroll = pltpu.roll(kall, 64, 1), pltpu.roll(vall, 64, 1)
    zero = jnp.zeros_like(kall)
    kvar = [[_bf(jnp.where(lo, kall, zero)), _bf(jnp.where(lo, zero, kroll))],
            [_bf(jnp.where(lo, kroll, zero)), _bf(jnp.where(lo, zero, kall))]]
    vvar = [[_bf(jnp.where(lo, vall, zero)), _bf(jnp.where(lo, zero, vroll))],
            [_bf(jnp.where(lo, vroll, zero)), _bf(jnp.where(lo, zero, vall))]]
    return start, valid, kvar, vvar


def _softmax_sink(s, valid, snk):
    s = jnp.where(valid, s, NEG)
    m = jnp.maximum(jnp.max(s, axis=-1, keepdims=True), snk)
    e = jnp.exp(s - m)
    es = jnp.exp(snk - m)
    inv = 1.0 / (jnp.sum(e, axis=-1, keepdims=True) + es)
    return e * inv, es * inv


def _attn_fwd(p, kp, gq, sink, cos, sin, *, lc, name):
    t = p.shape[0]
    scale = 64 ** -0.5

    def body(q_ref, kp_ref, v_ref, g_ref, sink_ref, c_ref, s_ref, o_ref):
        i = pl.program_id(0)
        _, valid, kvar, vvar = _attn_common(i, t, lc, kp_ref, v_ref)
        cosv, sinv, gv = c_ref[...], s_ref[...], g_ref[...]
        for j in range(4):
            xh, _ = _pair_norm(q_ref[:, 128 * j:128 * j + 128], None)
            q2 = _bf(_rope64(xh * gv, cosv, sinv) * scale)
            acc = jnp.zeros((QB, 128), F32)
            for half in range(2):
                s = _dot_nt(q2, kvar[j // 2][half])
                pr, _ = _softmax_sink(s, valid, sink_ref[2 * j + half])
                acc = acc + _dot(pr, vvar[j // 2][half])
            o_ref[:, 128 * j:128 * j + 128] = acc.astype(BF16)

    qblk = pl.BlockSpec((QB, 128), lambda i: (i, 0))
    return _pcall(
        body, name=name, grid=(t // QB,),
        in_specs=[pl.BlockSpec((QB, 512), lambda i: (i, 0)),
                  pl.BlockSpec((t, 128), lambda i: (0, 0)),
                  pl.BlockSpec((t, 128), lambda i: (0, 5)),
                  pl.BlockSpec((1, 128), lambda i: (0, 0)),
                  pl.BlockSpec(memory_space=pltpu.SMEM), qblk, qblk],
        out_specs=pl.BlockSpec((QB, 512), lambda i: (i, 0)),
        out_shape=jax.ShapeDtypeStruct((t, 512), BF16),
    )(p, kp, p, gq, sink, cos, sin)


def _attn_bwd(p, kp, gq, sink, cos, sin, dmix, *, lc, name):
    t = p.shape[0]
    scale = 64 ** -0.5
    span = QB + 2 * WINDOW

    def body(q_ref, kp_ref, v_ref, g_ref, sink_ref, c_ref, s_ref, do_ref,
             dq_ref, dk_ref, dv_ref, dg_ref, dsink_ref):
        i = pl.program_id(0)

        @pl.when(i == 0)
        def _():
            dk_ref[...] = jnp.zeros_like(dk_ref)
            dv_ref[...] = jnp.zeros_like(dv_ref)
            dg_ref[...] = jnp.zeros_like(dg_ref)
            dsink_ref[...] = jnp.zeros_like(dsink_ref)

        start, valid, kvar, vvar = _attn_common(i, t, lc, kp_ref, v_ref)
        cosv, sinv, gv = c_ref[...], s_ref[...], g_ref[...]
        nk = lc + span
        dkt = [jnp.zeros((64, nk), F32), jnp.zeros((64, nk), F32)]
        dvt = [jnp.zeros((64, nk), F32), jnp.zeros((64, nk), F32)]
        for j in range(4):
            kvh = j // 2
            xh, rs = _pair_norm(q_ref[:, 128 * j:128 * j + 128], None)
            q2 = _bf(_rope64(xh * gv, cosv, sinv) * scale)
            do2 = _bf(do_ref[:, 128 * j:128 * j + 128])
            dq2 = jnp.zeros((QB, 128), F32)
            for half in range(2):
                s = _dot_nt(q2, kvar[kvh][half])
                pr, ps = _softmax_sink(s, valid, sink_ref[2 * j + half])
                dp = _dot_nt(do2, vvar[kvh][half])
                delta = jnp.sum(pr * dp, axis=-1, keepdims=True)
                ds = pr * (dp - delta)
                dsk = jnp.sum(jnp.sum(-ps * delta, axis=0, keepdims=True), axis=1, keepdims=True)
                _acc_row(dsink_ref, 2 * j + half, jnp.broadcast_to(dsk, (1, 128)))
                dq2 = dq2 + _dot(ds, kvar[kvh][half])
                hrows = slice(64 * half, 64 * half + 64)
                dkt[kvh] = dkt[kvh] + _dot_tn(q2, ds)[hrows]
                dvt[kvh] = dvt[kvh] + _dot_tn(do2, pr)[hrows]
            dn = _rope64_t(dq2 * scale, cosv, sinv)
            _acc_row(dg_ref, 0, jnp.sum(dn * xh, axis=0, keepdims=True))
            dxh = dn * gv
            dq_ref[:, 128 * j:128 * j + 128] = (rs * (dxh - xh * _pair_mean(dxh * xh))).astype(BF16)
        dk_all = jnp.concatenate(dkt, axis=0).T
        dv_all = jnp.concatenate(dvt, axis=0).T
        dk_ref[0:lc, :] += dk_all[0:lc]
        dv_ref[0:lc, :] += dv_all[0:lc]
        dk_ref[pl.ds(start, span), :] += dk_all[lc:nk]
        dv_ref[pl.ds(start, span), :] += dv_all[lc:nk]

    qblk = pl.BlockSpec((QB, 128), lambda i: (i, 0))
    full = pl.BlockSpec((t, 128), lambda i: (0, 0))
    small = pl.BlockSpec((8, 128), lambda i: (0, 0))
    return _pcall(
        body, name=name, grid=(t // QB,),
        in_specs=[pl.BlockSpec((QB, 512), lambda i: (i, 0)), full,
                  pl.BlockSpec((t, 128), lambda i: (0, 5)),
                  pl.BlockSpec((1, 128), lambda i: (0, 0)),
                  pl.BlockSpec(memory_space=pltpu.SMEM), qblk, qblk,
                  pl.BlockSpec((QB, 512), lambda i: (i, 0))],
        out_specs=[pl.BlockSpec((QB, 512), lambda i: (i, 0)), full, full, small, small],
        out_shape=[jax.ShapeDtypeStruct((t, 512), BF16), jax.ShapeDtypeStruct((t, 128), F32),
                   jax.ShapeDtypeStruct((t, 128), F32), jax.ShapeDtypeStruct((8, 128), F32),
                   jax.ShapeDtypeStruct((8, 128), F32)],
    )(p, kp, p, gq, sink, cos, sin, dmix)


def _tri(rev):
    r, c = _iota((CHUNK, CHUNK), 0), _iota((CHUNK, CHUNK), 1)
    return (c >= r) if rev else (c <= r)


def _blk_map(nb, rev, backward):
    if not rev:
        return (lambda n: nb - 1 - n) if backward else (lambda n: n)
    if backward:
        return lambda n: jnp.where(n < nb - 1, n + 1, 0)
    return lambda n: jnp.where(n == 0, 0, nb - n)


def _chunk_order(rev, backward, nc=TM // CHUNK):
    order = list(range(nc))
    return order[::-1] if (rev != backward) else order


def _hgrn_gates(qraw, fraw, lb):
    sq = _sigmoid(qraw)
    sf = _sigmoid(fraw)
    f = lb + (1.0 - lb) * sf
    return qraw * sq, 1.0 - f, jnp.log(f), sq, sf, f


HGRN_HP = 2


def _chunk_cumsum(x, rev):
    n = x.shape[0]
    pos = _iota(x.shape, 0) & (CHUNK - 1)
    s = 1
    while s < CHUNK:
        if rev:
            x = x + jnp.where(pos < CHUNK - s, pltpu.roll(x, n - s, 0), 0.0)
        else:
            x = x + jnp.where(pos >= s, pltpu.roll(x, s, 0), 0.0)
        s *= 2
    return x


def _block_terms(lf, rev):
    b = _chunk_cumsum(lf, rev)
    mid, last = (CHUNK // 2 - 1, 0) if rev else (CHUNK // 2, CHUNK - 1)

    def chunk_row(off):
        return jnp.concatenate([jnp.broadcast_to(b[c * CHUNK + off:c * CHUNK + off + 1, :], (CHUNK, b.shape[1]))
                                for c in range(TM // CHUNK)], axis=0)

    r, bl = chunk_row(mid), chunk_row(last)
    return _tri(rev), jnp.exp(b - r), jnp.exp(r - b), jnp.exp(b), jnp.exp(bl - b), jnp.exp(bl)


def _headnorm_apply(o, gv, gain):
    n = o * lax.rsqrt(jnp.mean(o * o, axis=-1, keepdims=True) + EPS)
    if gain is not None:
        n = n * gain
    return (n * (gv * _sigmoid(gv))).astype(BF16)


def _headnorm_grad(o, gv, dy, gain):
    rs = lax.rsqrt(jnp.mean(o * o, axis=-1, keepdims=True) + EPS)
    xh = o * rs
    n = xh * gain if gain is not None else xh
    sg = _sigmoid(gv)
    dn = dy * (gv * sg)
    dg = (dy * n * (sg * (1.0 + gv * (1.0 - sg)))).astype(BF16)
    dgain = jnp.sum(dn * xh, axis=0, keepdims=True)
    dxh = dn * gain if gain is not None else dn
    return rs * (dxh - xh * jnp.mean(dxh * xh, axis=-1, keepdims=True)), dg, dgain


def _hgrn_fwd(p, lb, *, rev, name, ofw=None, gain=None):
    t = p.shape[0]
    nb, nc = t // TM, TM // CHUNK
    bmap = _blk_map(nb, rev, False)
    fcol = 14 if rev else 10
    fused = ofw is not None

    def body(*refs):
        q_ref, f_ref, v_ref, lb_ref = refs[:4]
        if fused:
            ofw_ref, g_ref, gain_ref, o_ref, sh_ref, mix_ref, st = refs[4:]
        else:
            o_ref, sh_ref, st = refs[4:]

        @pl.when(pl.program_id(1) == 0)
        def _():
            st[...] = jnp.zeros_like(st)
        for hh in range(HGRN_HP):
            ln = slice(128 * hh, 128 * hh + 128)
            q, k, lf, _, _, _ = _hgrn_gates(q_ref[:, ln], f_ref[:, ln], lb_ref[:, ln])
            tri, eq, ek, ei, eki, eb = _block_terms(lf, rev)
            qe, ke, qi, ki, vb = _bf(q * eq), _bf(k * ek), _bf(q * ei), _bf(k * eki), _bf(v_ref[:, ln])
            intra = []
            for cc in range(nc):
                rows = slice(cc * CHUNK, (cc + 1) * CHUNK)
                a = jnp.where(tri, _dot_nt(qe[rows], ke[rows]), 0.0)
                intra.append(_dot(a, vb[rows]))
            s = st[hh]
            for cc in _chunk_order(rev, False):
                rows = slice(cc * CHUNK, (cc + 1) * CHUNK)
                sh_ref[hh, cc] = s
                o_ref[rows, ln] = intra[cc] + _dot_nt(qi[rows], s)
                s = s * eb[cc * CHUNK:cc * CHUNK + 1, :] + _dot_tn(vb[rows], ki[rows])
            st[hh] = s
            if fused:
                osum = o_ref[:, ln] + ofw_ref[:, ln]
                o_ref[:, ln] = osum
                mix_ref[:, ln] = _headnorm_apply(osum, g_ref[:, ln], gain_ref[...])

    hp, wd = HGRN_HP, 128 * HGRN_HP

    def col(c0):
        return pl.BlockSpec((TM, wd), lambda h, n: (bmap(n), c0 // hp + h))

    oblk = pl.BlockSpec((TM, wd), lambda h, n: (bmap(n), h))
    ins, specs = [p, p, p, lb], [col(6), col(fcol), col(18), pl.BlockSpec((1, wd), lambda h, n: (0, h))]
    out_specs = [oblk, pl.BlockSpec((hp, nc, 128, 128), lambda h, n: (h, bmap(n), 0, 0))]
    out_shape = [jax.ShapeDtypeStruct((t, 512), F32), jax.ShapeDtypeStruct((4, t // CHUNK, 128, 128), F32)]
    if fused:
        ins += [ofw, p, gain]
        specs += [oblk, col(22), pl.BlockSpec((1, 128), lambda h, n: (0, 0))]
        out_specs.append(oblk)
        out_shape.append(jax.ShapeDtypeStruct((t, 512), BF16))
    return _pcall(body, name=name, grid=(4 // hp, nb), in_specs=specs, out_specs=out_specs, out_shape=out_shape,
                  scratch_shapes=[pltpu.VMEM((hp, 128, 128), F32)])(*ins)


def _hgrn_bwd(p, lb, sh, do, prev, *, rev, name, head=None):
    t = p.shape[0]
    nb, nc = t // TM, TM // CHUNK
    bmap = _blk_map(nb, rev, True)
    fcol = 14 if rev else 10
    has_prev = prev is not None
    odt = BF16 if has_prev else F32
    fused = head is not None

    def body(*refs):
        refs = list(refs)
        q_ref, f_ref, v_ref, lb_ref, sh_ref = refs[:5]
        pos = 5
        if fused:
            osum_ref, g_ref, dmix_ref, gain_ref = refs[5:9]
            pos = 9
        else:
            do_ref = refs[5]
            pos = 6
        if has_prev:
            pq_ref, pv_ref = refs[pos], refs[pos + 1]
            pos += 2
        dq_ref, df_ref, dv_ref, dlb_ref = refs[pos:pos + 4]
        pos += 4
        if fused:
            do_out, dg_ref, dgain_ref = refs[pos:pos + 3]
            pos += 3
        dst = refs[pos]

        @pl.when(pl.program_id(1) == 0)
        def _():
            dst[...] = jnp.zeros_like(dst)
            dlb_ref[...] = jnp.zeros_like(dlb_ref)

        if fused:
            @pl.when((pl.program_id(0) == 0) & (pl.program_id(1) == 0))
            def _():
                dgain_ref[...] = jnp.zeros_like(dgain_ref)

        cat = functools.partial(jnp.concatenate, axis=0)
        for hh in range(HGRN_HP):
            ln = slice(128 * hh, 128 * hh + 128)
            lbv = lb_ref[:, ln]
            qraw, fraw = q_ref[:, ln], f_ref[:, ln]
            q, k, lf, sq, sf, f = _hgrn_gates(qraw, fraw, lbv)
            tri, eq, ek, ei, eki, eb = _block_terms(lf, rev)
            qe, ke, qi, ki = q * eq, k * ek, q * ei, k * eki
            if fused:
                dov, dg, dgain = _headnorm_grad(osum_ref[:, ln], g_ref[:, ln], dmix_ref[:, ln], gain_ref[...])
                do_out[:, ln] = dov
                dg_ref[:, ln] = dg
                _acc_row(dgain_ref, 0, dgain)
            else:
                dov = do_ref[:, ln]
            qeb, keb, qib, kib, vb, dob = _bf(qe), _bf(ke), _bf(qi), _bf(ki), _bf(v_ref[:, ln]), _bf(dov)
            dv, dqe, dke, dqi = [None] * nc, [None] * nc, [None] * nc, [None] * nc
            for cc in range(nc):
                rows = slice(cc * CHUNK, (cc + 1) * CHUNK)
                a = jnp.where(tri, _dot_nt(qeb[rows], keb[rows]), 0.0)
                da = jnp.where(tri, _dot_nt(dob[rows], vb[rows]), 0.0)
                dv[cc] = _dot_tn(a, dob[rows])
                dqe[cc], dke[cc] = _dot(da, keb[rows]), _dot_tn(da, qeb[rows])
                dqi[cc] = _dot(dob[rows], sh_ref[hh, cc])
            dki, dbl = [None] * nc, [None] * nc
            ds = dst[hh]
            for cc in _chunk_order(rev, True):
                rows = slice(cc * CHUNK, (cc + 1) * CHUNK)
                ebc = eb[cc * CHUNK:cc * CHUNK + 1, :]
                dv[cc] = dv[cc] + _dot_nt(kib[rows], ds)
                dki[cc] = _dot(vb[rows], ds)
                dbl[cc] = jnp.broadcast_to(jnp.sum(dki[cc] * ki[rows], axis=0, keepdims=True)
                                           + jnp.sum(ds * sh_ref[hh, cc], axis=0, keepdims=True) * ebc, (CHUNK, 128))
                ds = ds * ebc + _dot_tn(dob[rows], qib[rows])
            dst[hh] = ds
            dqe, dke, dqi, dki, dv, dbl = cat(dqe), cat(dke), cat(dqi), cat(dki), cat(dv), cat(dbl)
            dq = dqe * eq + dqi * ei
            dk = dke * ek + dki * eki
            last = 0 if rev else CHUNK - 1
            db = dqe * qe - dke * ke + dqi * qi - dki * ki
            db = db + jnp.where((_iota(db.shape, 0) & (CHUNK - 1)) == last, dbl, 0.0)
            dlf = _chunk_cumsum(db, not rev)
            dqr = dq * (sq * (1.0 + qraw * (1.0 - sq)))
            dfv = dlf / f - dk
            dfr = dfv * (1.0 - lbv) * (sf * (1.0 - sf))
            dlb_ref[:, ln] += jnp.sum(dfv * (1.0 - sf), axis=0, keepdims=True)
            if has_prev:
                dqr = dqr + pq_ref[:, ln]
                dv = dv + pv_ref[:, ln]
            dq_ref[:, ln] = dqr.astype(odt)
            df_ref[:, ln] = dfr.astype(odt)
            dv_ref[:, ln] = dv.astype(odt)

    hp, wd = HGRN_HP, 128 * HGRN_HP

    def col(c0):
        return pl.BlockSpec((TM, wd), lambda h, n: (bmap(n), c0 // hp + h))

    oblk = pl.BlockSpec((TM, wd), lambda h, n: (bmap(n), h))
    ins = [p, p, p, lb, sh]
    specs = [col(6), col(fcol), col(18), pl.BlockSpec((1, wd), lambda h, n: (0, h)),
             pl.BlockSpec((hp, nc, 128, 128), lambda h, n: (h, bmap(n), 0, 0))]
    if fused:
        osum, dmix, gain = head
        ins += [osum, p, dmix, gain]
        specs += [oblk, col(22), pl.BlockSpec((TM, wd), lambda h, n: (bmap(n), 4 // hp + h)),
                  pl.BlockSpec((1, 128), lambda h, n: (0, 0))]
    else:
        ins.append(do); specs.append(oblk)
    if has_prev:
        ins += list(prev); specs += [oblk, oblk]
    out_specs = [oblk, oblk, oblk, pl.BlockSpec((1, wd), lambda h, n: (0, h))]
    out_shape = [jax.ShapeDtypeStruct((t, 512), odt)] * 3 + [jax.ShapeDtypeStruct((1, 512), F32)]
    if fused:
        out_specs += [oblk, oblk, pl.BlockSpec((8, 128), lambda h, n: (0, 0))]
        out_shape += [jax.ShapeDtypeStruct((t, 512), F32), jax.ShapeDtypeStruct((t, 512), BF16),
                      jax.ShapeDtypeStruct((8, 128), F32)]
    return _pcall(body, name=name, grid=(4 // hp, nb), in_specs=specs, out_specs=out_specs, out_shape=out_shape,
                  scratch_shapes=[pltpu.VMEM((hp, 128, 128), F32)])(*ins)


def _rope256(x, cos, sin):
    x1, x2 = x[:, 0:128], x[:, 128:256]
    return jnp.concatenate([x1 * cos - x2 * sin, x2 * cos + x1 * sin], axis=-1)


def _rope256_t(d, cos, sin):
    d1, d2 = d[:, 0:128], d[:, 128:256]
    return jnp.concatenate([d1 * cos + d2 * sin, d2 * cos - d1 * sin], axis=-1)


RET_DK, RET_DV, RET_H = 256, 512, 4
RET_KSCALE = RET_DK ** -0.5
RCH = TM
RET_HP = 2


def _ret_terms(lg, rev):
    r, c = _iota((RCH, RCH), 0), _iota((RCH, RCH), 1)
    rel = ((c - r) if rev else (r - c)).astype(F32)
    dmat = jnp.where(rel >= 0, jnp.exp(lg[:, 0:1] * jnp.maximum(rel, 0.0)), 0.0)
    pos = _iota((RCH, 1), 0).astype(F32)
    cnt = (RCH - pos) if rev else (pos + 1.0)
    ei = jnp.exp(lg * cnt)
    eki = jnp.exp(lg * (RCH - cnt))
    eb = jnp.exp(lg * float(RCH))
    return dmat, ei, eki, eb


def _ret_fwd(p, lgt, cos, sin, *, rev, name, ofw=None):
    t = p.shape[0]
    nb, nc = t // TM, TM // RCH
    bmap = _blk_map(nb, rev, False)
    fused = ofw is not None

    def body(*refs):
        q_ref, k_ref, v_ref, lg_ref, c_ref, s_ref = refs[:6]
        if fused:
            ofw_ref, g_ref, o_ref, sh_ref, mix_ref, st = refs[6:]
        else:
            o_ref, sh_ref, st = refs[6:]

        @pl.when(pl.program_id(1) == 0)
        def _():
            st[...] = jnp.zeros_like(st)
        for hh in range(RET_HP):
            qc, vc = slice(RET_DK * hh, RET_DK * (hh + 1)), slice(RET_DV * hh, RET_DV * (hh + 1))
            dmat, ei, eki, eb = _ret_terms(lg_ref[hh], rev)
            for cc in _chunk_order(rev, False, nc):
                rows = slice(cc * RCH, (cc + 1) * RCH)
                cosv, sinv = c_ref[rows, :], s_ref[rows, :]
                q = _rope256(q_ref[rows, qc].astype(F32), cosv, sinv)
                k = _rope256(k_ref[rows, qc].astype(F32), cosv, sinv) * RET_KSCALE
                v = v_ref[rows, vc]
                s0 = st[hh]
                sh_ref[hh, cc] = s0.astype(BF16)
                a = _dot_nt(q, k) * dmat
                o = _dot(a, v) + _dot_nt(q * ei, s0)
                st[hh] = s0 * eb + _dot_tn(v, k * eki)
                if fused:
                    o = o + ofw_ref[rows, vc]
                    mix_ref[rows, vc] = _headnorm_apply(o, g_ref[rows, vc].astype(F32), None)
                o_ref[rows, vc] = o

    hp = RET_HP
    tab = pl.BlockSpec((TM, 128), lambda h, n: (bmap(n), 0))
    oblk = pl.BlockSpec((TM, hp * RET_DV), lambda h, n: (bmap(n), h))
    ins = [p, p, p, lgt, cos, sin]
    specs = [pl.BlockSpec((TM, hp * RET_DK), lambda h, n: (bmap(n), h)),
             pl.BlockSpec((TM, hp * RET_DK), lambda h, n: (bmap(n), RET_H // hp + h)),
             pl.BlockSpec((TM, hp * RET_DV), lambda h, n: (bmap(n), RET_H // hp + h)),
             pl.BlockSpec((hp, 1, RET_DK), lambda h, n: (h, 0, 0)), tab, tab]
    out_specs = [oblk, pl.BlockSpec((hp, nc, RET_DV, RET_DK), lambda h, n: (h, bmap(n), 0, 0))]
    out_shape = [jax.ShapeDtypeStruct((t, RET_H * RET_DV), F32),
                 jax.ShapeDtypeStruct((RET_H, t // RCH, RET_DV, RET_DK), BF16)]
    if fused:
        ins += [ofw, p]
        specs += [oblk, pl.BlockSpec((TM, hp * RET_DV), lambda h, n: (bmap(n), 2 * RET_H // hp + h))]
        out_specs.append(oblk)
        out_shape.append(jax.ShapeDtypeStruct((t, RET_H * RET_DV), BF16))
    return _pcall(body, name=name, grid=(RET_H // hp, nb), in_specs=specs, out_specs=out_specs, out_shape=out_shape,
                  scratch_shapes=[pltpu.VMEM((hp, RET_DV, RET_DK), F32)])(*ins)


def _ret_bwd(p, lgt, cos, sin, sh, do, prev, *, rev, name, head=None):
    t = p.shape[0]
    nb, nc = t // TM, TM // RCH
    bmap = _blk_map(nb, rev, True)
    has_prev = prev is not None
    odt = BF16 if has_prev else F32
    fused = head is not None

    def body(*refs):
        refs = list(refs)
        q_ref, k_ref, v_ref, lg_ref, c_ref, s_ref, sh_ref = refs[:7]
        if fused:
            osum_ref, g_ref, dmix_ref = refs[7:10]
            pos = 10
        else:
            do_ref = refs[7]
            pos = 8
        if has_prev:
            pq_ref, pk_ref, pv_ref = refs[pos:pos + 3]
            pos += 3
        dq_ref, dk_ref, dv_ref = refs[pos:pos + 3]
        pos += 3
        if fused:
            do_out, dg_ref = refs[pos:pos + 2]
            pos += 2
        dst = refs[pos]

        @pl.when(pl.program_id(1) == 0)
        def _():
            dst[...] = jnp.zeros_like(dst)

        for hh in range(RET_HP):
            qc, vc = slice(RET_DK * hh, RET_DK * (hh + 1)), slice(RET_DV * hh, RET_DV * (hh + 1))
            dmat, ei, eki, eb = _ret_terms(lg_ref[hh], rev)
            for cc in _chunk_order(rev, True, nc):
                rows = slice(cc * RCH, (cc + 1) * RCH)
                cosv, sinv = c_ref[rows, :], s_ref[rows, :]
                q = _rope256(q_ref[rows, qc].astype(F32), cosv, sinv)
                k = _rope256(k_ref[rows, qc].astype(F32), cosv, sinv) * RET_KSCALE
                v = v_ref[rows, vc]
                if fused:
                    dov, dg, _ = _headnorm_grad(osum_ref[rows, vc], g_ref[rows, vc].astype(F32), dmix_ref[rows, vc], None)
                    do_out[rows, vc] = dov
                    dg_ref[rows, vc] = dg
                else:
                    dov = do_ref[rows, vc]
                s0 = sh_ref[hh, cc]
                dsc = dst[hh]
                qi, ki = q * ei, k * eki
                a = _dot_nt(q, k) * dmat
                da = _dot_nt(dov, v) * dmat
                dv = _dot_tn(a, dov) + _dot_nt(ki, dsc)
                dqs = _dot(da, k) + _dot(dov, s0) * ei
                dks = _dot_tn(da, q) + _dot(v, dsc) * eki
                dst[hh] = dsc * eb + _dot_tn(dov, qi)
                dq = _rope256_t(dqs, cosv, sinv)
                dk = _rope256_t(dks * RET_KSCALE, cosv, sinv)
                if has_prev:
                    dq = dq + pq_ref[rows, qc]
                    dk = dk + pk_ref[rows, qc]
                    dv = dv + pv_ref[rows, vc]
                dq_ref[rows, qc] = dq.astype(odt)
                dk_ref[rows, qc] = dk.astype(odt)
                dv_ref[rows, vc] = dv.astype(odt)

    hp = RET_HP
    tab = pl.BlockSpec((TM, 128), lambda h, n: (bmap(n), 0))
    qblk = pl.BlockSpec((TM, hp * RET_DK), lambda h, n: (bmap(n), h))
    vblk = pl.BlockSpec((TM, hp * RET_DV), lambda h, n: (bmap(n), h))
    ins = [p, p, p, lgt, cos, sin, sh]
    specs = [qblk, pl.BlockSpec((TM, hp * RET_DK), lambda h, n: (bmap(n), RET_H // hp + h)),
             pl.BlockSpec((TM, hp * RET_DV), lambda h, n: (bmap(n), RET_H // hp + h)),
             pl.BlockSpec((hp, 1, RET_DK), lambda h, n: (h, 0, 0)), tab, tab,
             pl.BlockSpec((hp, nc, RET_DV, RET_DK), lambda h, n: (h, bmap(n), 0, 0))]
    if fused:
        osum, dmix = head
        ins += [osum, p, dmix]
        specs += [vblk, pl.BlockSpec((TM, hp * RET_DV), lambda h, n: (bmap(n), 2 * RET_H // hp + h)), vblk]
    else:
        ins.append(do); specs.append(vblk)
    if has_prev:
        ins += list(prev); specs += [qblk, qblk, vblk]
    out_specs = [qblk, qblk, vblk]
    out_shape = [jax.ShapeDtypeStruct((t, RET_H * RET_DK), odt), jax.ShapeDtypeStruct((t, RET_H * RET_DK), odt),
                 jax.ShapeDtypeStruct((t, RET_H * RET_DV), odt)]
    if fused:
        out_specs += [vblk, vblk]
        out_shape += [jax.ShapeDtypeStruct((t, RET_H * RET_DV), F32), jax.ShapeDtypeStruct((t, RET_H * RET_DV), BF16)]
    return _pcall(body, name=name, grid=(RET_H // hp, nb), in_specs=specs, out_specs=out_specs, out_shape=out_shape,
                  scratch_shapes=[pltpu.VMEM((hp, RET_DV, RET_DK), F32)])(*ins)


def _rope_tables(lc, l):
    tt = jnp.arange(l)
    row, colp = (tt // 64).astype(F32), (tt % 64).astype(F32)
    inv = 10000.0 ** (-jnp.arange(16, dtype=F32) / 16)
    ang = jnp.concatenate([row[:, None] * inv, colp[:, None] * inv], axis=-1)
    ang = jnp.concatenate([jnp.zeros((lc, 32), F32), ang], axis=0)
    acos, asin = jnp.tile(jnp.cos(ang), (1, 4)), jnp.tile(jnp.sin(ang), (1, 4))
    theta = 1.0 / (10000.0 ** jnp.linspace(0.0, 1.0, 128, dtype=F32))
    rang = jnp.arange(l, dtype=F32)[:, None] * theta
    rang = jnp.concatenate([jnp.zeros((lc, 128), F32), rang], axis=0)
    return acos, asin, jnp.cos(rang), jnp.sin(rang)


class _Weights:
    def __init__(self, w):
        self.w = w

    def first(self, after):
        return self.w

    def rest_landed(self, after):
        pass

    def rest(self, after):
        return self.w

    def send_grads(self, grp, grads):
        return jnp.zeros((8, 128), F32)


def _local_step(x0, target, mods, ng, wsrc, small):
    t, d = x0.shape
    l = target.shape[0]
    lc = t - l
    acos, asin, rcos, rsin = _rope_tables(lc, l)
    lg_fw = jnp.log(1.0 - 2.0 ** (-5.0 - jnp.arange(RET_H, dtype=F32)))
    lgt_fw = jnp.broadcast_to(lg_fw[:, None, None], (RET_H, 1, RET_DK))
    lgt_bw = jnp.broadcast_to(lg_fw[::-1][:, None, None], (RET_H, 1, RET_DK))
    gq, gk, sink, gain, lb = small['gq'], small['gk'], small['sink'], small['gain'], small['lb']

    (h1,) = _row_fwd(x0, mods, g=ng[0], shift=0, scale=1, name='l0_norm1')
    w = wsrc.first(h1)
    p0 = _mm_nn(h1, w['even_in'], name='l0_in')
    kp = _kprep_fwd(p0, gk, acos, asin, name='l0_kprep')
    att = _attn_fwd(p0, kp, gq, sink, acos, asin, lc=lc, name='l0_attn')
    hof, hsf = _hgrn_fwd(p0, lb, rev=False, name='l0_hgrn_f')
    wsrc.rest_landed(hof)
    hos, hsb, bmix = _hgrn_fwd(p0, lb, rev=True, name='l0_hgrn_b', ofw=hof, gain=gain)
    mix0 = jnp.concatenate([att, bmix], axis=1)
    y0 = _mm_nn(mix0, w['even_out'], name='l0_out')
    x1, h2 = _row_fwd(x0, mods, y=y0, gate=2, g=ng[1], shift=3, scale=4, name='l0_norm2')
    w = dict(w, **wsrc.rest(h2))
    u0, a0 = _ffn_in(h2, w['ffn_in'], lead=0, name='ffn_in')
    z0 = _mm_nn(a0, w['ffn_out'], lead=0, name='ffn_out')
    x2, h3 = _row_fwd(x1, mods, y=z0, gate=5, g=ng[2], shift=12, scale=13, name='l1_norm1')
    p1 = _mm_nn(h3, w['odd_in'], out_dtype=BF16, name='l1_in')
    rof, rsf = _ret_fwd(p1, lgt_fw, rcos, rsin, rev=False, name='l1_ret_f')
    ros, rsb, mix1 = _ret_fwd(p1, lgt_bw, rcos, rsin, rev=True, name='l1_ret_b', ofw=rof)
    y1 = _mm_nn(mix1, w['odd_out'], name='l1_out')
    x3, h4 = _row_fwd(x2, mods, y=y1, gate=14, g=ng[3], shift=15, scale=16, name='l1_norm2')
    u1, a1 = _ffn_in(h4, w['ffn_in'], lead=1, name='ffn_in')
    z1 = _mm_nn(a1, w['ffn_out'], lead=1, name='ffn_out')
    loss, dx4, dz1, s_fin = _row_final(x3, z1, mods, target, gate=17, name='loss')

    du1 = _ffn_dx(dz1, w['ffn_out'], u1, lead=1, name='ffn_out_dx')
    g_ffn_out1 = _mm_tn(a1, dz1, name='ffn_out_dw')
    dh4 = _mm_nt(du1, w['ffn_in'], lead=1, name='ffn_in_dx')
    g_ffn_in1 = _mm_tn(h4, du1, name='ffn_in_dw')
    dx3, dy1, s_l1n2 = _row_bwd(x3, dx4, dh4, mods, ng[3], shift=15, scale=16, y=y1, gate=14, name='l1_norm2_bwd')
    dmix1 = _mm_nt(dy1, w['odd_out'], name='l1_out_dx')
    g_odd_out = _mm_tn(mix1, dy1, name='l1_out_dw')
    rdq, rdk, rdv, rdo, rdg = _ret_bwd(p1, lgt_fw, rcos, rsin, rsf, None, None, rev=False, name='l1_ret_f_bwd',
                                       head=(ros, dmix1))
    rdq, rdk, rdv = _ret_bwd(p1, lgt_bw, rcos, rsin, rsb, rdo, (rdq, rdk, rdv), rev=True, name='l1_ret_b_bwd')
    dp1 = jnp.concatenate([rdq, rdk, rdv, rdg], axis=1)
    dh3 = _mm_nt(dp1, w['odd_in'], name='l1_in_dx')
    g_odd_in = _mm_tn(h3, dp1, name='l1_in_dw')
    mods = mods + wsrc.send_grads('early', dict(ffn_in1=g_ffn_in1, ffn_out1=g_ffn_out1, odd_in=g_odd_in,
                                                odd_out=g_odd_out))[0, 0]
    dx2, dz0, s_l1n1 = _row_bwd(x2, dx3, dh3, mods, ng[2], shift=12, scale=13, y=z0, gate=5, name='l1_norm1_bwd')
    du0 = _ffn_dx(dz0, w['ffn_out'], u0, lead=0, name='ffn_out_dx')
    g_ffn_out0 = _mm_tn(a0, dz0, name='ffn_out_dw')
    dh2 = _mm_nt(du0, w['ffn_in'], lead=0, name='ffn_in_dx')
    g_ffn_in0 = _mm_tn(h2, du0, name='ffn_in_dw')
    mods = mods + wsrc.send_grads('mid', dict(ffn_in0=g_ffn_in0, ffn_out0=g_ffn_out0))[0, 0]
    dx1, dy0, s_l0n2 = _row_bwd(x1, dx2, dh2, mods, ng[1], shift=3, scale=4, y=y0, gate=2, name='l0_norm2_bwd')
    dmix0 = _mm_nt(dy0, w['even_out'], name='l0_out_dx')
    g_even_out = _mm_tn(mix0, dy0, name='l0_out_dw')
    hq, hff, hv, dlb_f, hdo, hdg, s_gain = _hgrn_bwd(p0, lb, hsf, None, None, rev=False, name='l0_hgrn_f_bwd',
                                                     head=(hos, dmix0, gain))
    hq, hfb, hv, dlb_b = _hgrn_bwd(p0, lb, hsb, hdo, (hq, hv), rev=True, name='l0_hgrn_b_bwd')
    adq, dkp, adv, s_gq, s_sink = _attn_bwd(p0, kp, gq, sink, acos, asin, dmix0, lc=lc, name='l0_attn_bwd')
    dkv, s_gk = _kprep_bwd(p0, gk, acos, asin, dkp, adv, name='l0_kprep_bwd')
    dp0 = jnp.concatenate([adq, dkv, hq, _bf(hff), hfb, hv, hdg], axis=1)
    dh1 = _mm_nt(dp0, w['even_in'], name='l0_in_dx')
    g_even_in = _mm_tn(h1, dp0, name='l0_in_dw')
    dx0, s_l0n1 = _row_bwd(x0, dx1, dh1, mods, ng[0], shift=0, scale=1, latent_only=True, name='l0_norm1_bwd')

    grads = dict(ffn_in0=g_ffn_in0, ffn_in1=g_ffn_in1, ffn_out0=g_ffn_out0, ffn_out1=g_ffn_out1,
                 even_in=g_even_in, even_out=g_even_out, odd_in=g_odd_in, odd_out=g_odd_out)
    sums = dict(fin=s_fin, l1n2=s_l1n2, l1n1=s_l1n1, l0n2=s_l0n2, l0n1=s_l0n1, gain=s_gain, gq=s_gq, gk=s_gk,
                sink=s_sink, dlb_f=dlb_f, dlb_b=dlb_b)
    return loss, dx0, grads, sums


def _place():
    return lax.axis_index("x"), lax.axis_index("y"), lax.axis_index("c")


def _ag8(blk, *, name):
    r, c = blk.shape
    flips = [(dx, dy, dc) for dx in (0, 1) for dy in (0, 1) for dc in (0, 1) if (dx, dy, dc) != (0, 0, 0)]

    def body(x_ref, out_ref, send_sems, recv_sems, local_sem):
        ax, ay, ac = _place()
        me = 4 * ax + 2 * ay + ac
        mine = pltpu.make_async_copy(x_ref, out_ref.at[me], local_sem)
        mine.start()
        sent = []
        for k, (dx, dy, dc) in enumerate(flips):
            peer = (lax.rem(ax + dx, 2), lax.rem(ay + dy, 2), lax.rem(ac + dc, 2))
            cp = pltpu.make_async_remote_copy(src_ref=x_ref, dst_ref=out_ref.at[me], send_sem=send_sems.at[k],
                                              recv_sem=recv_sems.at[k], device_id=peer, device_id_type=MESH)
            cp.start()
            sent.append((cp, 4 * peer[0] + 2 * peer[1] + peer[2]))
        for k, (cp, pidx) in enumerate(sent):
            pltpu.make_async_remote_copy(src_ref=x_ref, dst_ref=out_ref.at[pidx], send_sem=send_sems.at[k],
                                         recv_sem=recv_sems.at[k], device_id=(ax, ay, ac),
                                         device_id_type=MESH).wait_recv()
        for cp, _ in sent:
            cp.wait_send()
        mine.wait()

    return _pcall(
        body, name=name,
        in_specs=[pl.BlockSpec(memory_space=pltpu.VMEM)],
        out_specs=pl.BlockSpec(memory_space=pltpu.VMEM),
        out_shape=jax.ShapeDtypeStruct((8, r, c), blk.dtype),
        scratch_shapes=[pltpu.SemaphoreType.DMA((7,)), pltpu.SemaphoreType.DMA((7,)), pltpu.SemaphoreType.DMA],
    )(blk)


_HBM = pl.BlockSpec(memory_space=pltpu.HBM)
_SEM = pl.BlockSpec(memory_space=pltpu.SEMAPHORE)
_DATAFLOW = pltpu.SideEffectType.DATAFLOW_SIDE_EFFECTING


def _split_start(bufs, plan, k, *, name):
    n = len(bufs)

    def body(*refs):
        ins, send_sems, recv_sems, token = refs[:n], refs[n], refs[n + 1], refs[2 * n + 2]
        for i, (src, dst, dev) in enumerate(plan(ins)):
            pltpu.make_async_remote_copy(src_ref=src, dst_ref=dst, send_sem=send_sems.at[i], recv_sem=recv_sems.at[i],
                                         device_id=dev, device_id_type=MESH).start()
        token[...] = jnp.zeros_like(token)

    res = _pcall(
        body, name=name,
        out_shape=(pltpu.SemaphoreType.DMA((k,)), pltpu.SemaphoreType.DMA((k,)),
                   *[pltpu.HBM(b.shape, b.dtype) for b in bufs], jax.ShapeDtypeStruct((8, 128), F32)),
        in_specs=[_HBM] * n, out_specs=(_SEM, _SEM, *[_HBM] * n, pl.BlockSpec(memory_space=pltpu.VMEM)),
        input_output_aliases={i: 2 + i for i in range(n)},
        compiler_params=pltpu.CompilerParams(has_side_effects=_DATAFLOW),
    )(*[pltpu.with_memory_space_constraint(b, pltpu.HBM) for b in bufs])
    return res[0], res[1], list(res[2:2 + n]), res[2 + n]


def _split_wait(bufs, send_sems, recv_sems, plan, after, *, name):
    n = len(bufs)

    def body(*refs):
        ins, ssem, rsem = refs[:n], refs[n], refs[n + 1]
        for i, (src, dst, dev) in enumerate(plan(ins)):
            cp = pltpu.make_async_remote_copy(src_ref=src, dst_ref=dst, send_sem=ssem.at[i], recv_sem=rsem.at[i],
                                              device_id=dev, device_id_type=MESH)
            cp.wait_send()
            cp.wait_recv()

    res = _pcall(
        body, name=name, out_shape=tuple(pltpu.HBM(b.shape, b.dtype) for b in bufs),
        in_specs=[_HBM] * n + [_SEM, _SEM, pl.BlockSpec(memory_space=pl.ANY)], out_specs=tuple([_HBM] * n),
        input_output_aliases={i: i for i in range(n)},
        compiler_params=pltpu.CompilerParams(has_side_effects=_DATAFLOW),
    )(*bufs, send_sems, recv_sems, after)
    return list(res)


_CHIP_FLIPS = [(1, 0), (0, 1), (1, 1)]


class _GatheredWeights:
    FIRST = ('even_in', 'even_out')
    REST = ('ffn_in', 'ffn_out', 'odd_in', 'odd_out')

    def __init__(self, shards, reducer):
        self.shards = shards
        self.send_grads = reducer.start
        self.ici = {}
        for grp, names in (('first', self.FIRST), ('rest', self.REST)):
            src = [shards[nm].reshape(2, shards[nm].shape[0] // 2, shards[nm].shape[1]) for nm in names]
            land = [lax.empty((4,) + a.shape, a.dtype) for a in src]
            m = len(names)
            sends, recvs, bufs, token = _split_start(src + land, functools.partial(self._ici_plan, m, True), 3 * m,
                                                     name='gather_' + grp + '_ici_start')
            self.ici[grp] = (sends, recvs, bufs, m)
            self.token = token if grp == 'first' else self.token + token
        self.rest_d2d = None

    @staticmethod
    def _ici_plan(m, sending, refs):
        ax, ay, ac = _place()
        s = 2 * ax + ay
        out = []
        for a in range(m):
            for dx, dy in _CHIP_FLIPS:
                px, py = lax.rem(ax + dx, 2), lax.rem(ay + dy, 2)
                slot = s if sending else 2 * px + py
                out.append((refs[a].at[ac], refs[m + a].at[slot, ac], (px, py, ac)))
        return out

    @staticmethod
    def _d2d_plan(m, sending, refs):
        ax, ay, ac = _place()
        out = []
        for a in range(m):
            for dx, dy in _CHIP_FLIPS:
                sp = 2 * lax.rem(ax + dx, 2) + lax.rem(ay + dy, 2)
                out.append((refs[a].at[sp, ac], refs[a].at[sp, ac if sending else 1 - ac], (ax, ay, 1 - ac)))
        return out

    def _landed(self, grp, after):
        sends, recvs, bufs, m = self.ici[grp]
        bufs = _split_wait(bufs, sends, recvs, functools.partial(self._ici_plan, m, False), after,
                           name='gather_' + grp + '_ici_wait')
        sends, recvs, land, _ = _split_start(bufs[m:], functools.partial(self._d2d_plan, m, True), 3 * m,
                                             name='gather_' + grp + '_d2d_start')
        return sends, recvs, land, m

    def _full(self, grp, names, d2d, after):
        sends, recvs, land, m = d2d
        land = _split_wait(land, sends, recvs, functools.partial(self._d2d_plan, m, False), after,
                           name='gather_' + grp + '_d2d_wait')
        s = 2 * lax.axis_index("x") + lax.axis_index("y")
        slot = lax.broadcasted_iota(jnp.int32, (4, 1, 1), 0)
        return {nm: _from_shards(nm, jnp.where(slot == s, self.shards[nm][None], g.reshape((4,) + self.shards[nm].shape)))
                for nm, g in zip(names, land)}

    def first(self, after):
        return self._full('first', self.FIRST, self._landed('first', after), after)

    def rest_landed(self, after):
        self.rest_d2d = self._landed('rest', after)

    def rest(self, after):
        return self._full('rest', self.REST, self.rest_d2d, after)


def _to_sibling(arrs, *, name):
    n = len(arrs)

    def body(*refs):
        ins, outs = refs[:n], refs[n:2 * n]
        send_sems, recv_sems = refs[2 * n:]
        ax, ay, ac = _place()
        cps = [pltpu.make_async_remote_copy(src_ref=ins[a], dst_ref=outs[a], send_sem=send_sems.at[a],
                                            recv_sem=recv_sems.at[a], device_id=(ax, ay, 1 - ac),
                                            device_id_type=MESH) for a in range(n)]
        for cp in cps:
            cp.start()
        for cp in cps:
            cp.wait_recv()
        for cp in cps:
            cp.wait_send()

    hbm = pl.BlockSpec(memory_space=pl.ANY)
    return _pcall(
        body, name=name, in_specs=[hbm] * n, out_specs=[hbm] * n,
        out_shape=[jax.ShapeDtypeStruct(a.shape, a.dtype) for a in arrs],
        scratch_shapes=[pltpu.SemaphoreType.DMA((n,))] * 2,
    )(*arrs)


def _mod_fwd(cond_raw, mw, mb, *, name):
    _, d, n = mw.shape

    def body(c_ref, w_ref, b_ref, o_ref):
        cv = c_ref[...]
        o_ref[...] = _dot(cv * _sigmoid(cv), w_ref[...]) + b_ref[...]

    return _pcall(
        body, name=name, grid=(2,),
        in_specs=[pl.BlockSpec((16, d), lambda l: (0, 0)), pl.BlockSpec((None, d, n), lambda l: (l, 0, 0)),
                  pl.BlockSpec((None, 1, n), lambda l: (l, 0, 0))],
        out_specs=pl.BlockSpec((None, 16, n), lambda l: (l, 0, 0)),
        out_shape=jax.ShapeDtypeStruct((2, 16, n), F32),
    )(cond_raw, mw, mb)


def _mod_bwd(cond_raw, dms, mw, *, name):
    _, d, n = mw.shape

    def body(c_ref, dm_ref, w_ref, gw_ref, dc_ref):
        @pl.when(pl.program_id(0) == 0)
        def _():
            dc_ref[...] = jnp.zeros_like(dc_ref)
        cv = c_ref[...]
        gw_ref[...] = _dot_tn(cv * _sigmoid(cv), dm_ref[...])
        dc_ref[...] += _dot_nt(dm_ref[...], w_ref[...])

    return _pcall(
        body, name=name, grid=(2,),
        in_specs=[pl.BlockSpec((16, d), lambda l: (0, 0)), pl.BlockSpec((None, 16, n), lambda l: (l, 0, 0)),
                  pl.BlockSpec((None, d, n), lambda l: (l, 0, 0))],
        out_specs=[pl.BlockSpec((None, d, n), lambda l: (l, 0, 0)), pl.BlockSpec((16, d), lambda l: (0, 0))],
        out_shape=[jax.ShapeDtypeStruct((2, d, n), F32), jax.ShapeDtypeStruct((16, d), F32)],
    )(cond_raw, dms, mw)


def _lb_fwd(hgrn_lb, *, name):
    def body(a_ref, o_ref):
        a0, a1 = a_ref[0:1, :], a_ref[1:2, :]
        m = jnp.maximum(a0, a1)
        e0, e1 = jnp.exp(a0 - m), jnp.exp(a1 - m)
        o_ref[...] = e0 / (e0 + e1)

    return _pcall(body, name=name, out_shape=jax.ShapeDtypeStruct((1, hgrn_lb.shape[1]), F32))(hgrn_lb)


PACK_TILES = ('l0n1', 'l0n2', 'l1n1', 'l1n2', 'fin', 'gq', 'gk', 'gain', 'dlb_f', 'dlb_b', 'sink')
PACK_ROW = {nm: 8 * i for i, nm in enumerate(PACK_TILES)}
MOD_SOURCE = ((('l0n1', 0), ('l0n1', 1), ('l0n2', 2), ('l0n2', 0), ('l0n2', 1), ('l1n1', 2)),
              (('l1n1', 0), ('l1n1', 1), ('l1n2', 2), ('l1n2', 0), ('l1n2', 1), ('fin', 2)))


def _small_finalize(gath, lb_pad, *, name):
    d = gath.shape[2]

    def body(g_ref, lb_ref, small_ref, glb_ref, gmb_ref, dm_ref):
        tot = g_ref[0]
        for e in range(1, 8):
            tot = tot + g_ref[e]

        def row(nm, r=0):
            return tot[PACK_ROW[nm] + r:PACK_ROW[nm] + r + 1, :]

        for k, nm in enumerate(('l0n1', 'l0n2', 'l1n1', 'l1n2')):
            small_ref[k:k + 1, :] = row(nm, 3) + row(nm, 7)
        for k, nm in ((4, 'gq'), (5, 'gk')):
            small_ref[k:k + 1, :] = row(nm) + pltpu.roll(row(nm), d - 64, 1)
        small_ref[6:7, :] = row('gain')
        small_ref[7:8, :] = row('sink')
        lbv = lb_ref[...]
        g0 = (row('dlb_f') + row('dlb_b')) * lbv * (1.0 - lbv)
        glb_ref[...] = jnp.zeros_like(glb_ref)
        glb_ref[0:1, :] = g0
        glb_ref[1:2, :] = -g0
        dm_ref[...] = jnp.zeros_like(dm_ref)
        for l in range(2):
            for part in range(6):
                nm, r = MOD_SOURCE[l][part]
                gmb_ref[l * 6 + part:l * 6 + part + 1, :] = row(nm, r) + row(nm, r + 4)
                rl = PACK_ROW[nm] + r + 4
                for e in range(8):
                    dm_ref[l, part, e:e + 1, :] = g_ref[e, rl:rl + 1, :]
                dm_ref[l, part, 8:9, :] = row(nm, r)

    return _pcall(
        body, name=name,
        out_shape=[jax.ShapeDtypeStruct((8, d), F32), jax.ShapeDtypeStruct((8, d), F32),
                   jax.ShapeDtypeStruct((12, d), F32), jax.ShapeDtypeStruct((2, 6, 16, d), F32)],
    )(gath, lb_pad)


def _cctx_grad(gath, c_ctx2, *, name):
    def body(g_ref, c_ref, o_ref):
        tot = ((g_ref[0, 0:1, :] + g_ref[2, 0:1, :]) + g_ref[4, 0:1, :]) + g_ref[6, 0:1, :]
        cv = c_ref[...]
        s = _sigmoid(cv)
        o_ref[...] = tot * (s * (1.0 + cv * (1.0 - s)))

    return _pcall(body, name=name, out_shape=jax.ShapeDtypeStruct(c_ctx2.shape, F32))(gath, c_ctx2)


def _row_block(r, c, limit=256 * 1024):
    best = None
    for br in range(16, r + 1, 16):
        if r % br == 0 and br * c <= limit:
            best = br
    return best if best is not None else r


def _sum4(own, landed, core, *, name):
    _, r, c = own.shape
    br = _row_block(r, c, 512 * 1024)

    def body(core_ref, own_ref, land_ref, o_ref):
        s = 2 * lax.axis_index("x") + lax.axis_index("y")
        p = [jnp.where(s == k, own_ref[k], land_ref[k]).astype(F32) for k in range(4)]
        o_ref[...] = ((p[0] + p[1]) + p[2]) + p[3]

    blk = pl.BlockSpec((4, br, c), lambda i, core_ref: (0, i, 0))
    spec = pltpu.PrefetchScalarGridSpec(
        num_scalar_prefetch=1, grid=(r // br,), in_specs=[blk, blk],
        out_specs=pl.BlockSpec((None, br, c), lambda i, core_ref: (core_ref[0], i, 0)))
    return _pcall(body, name=name, grid_spec=spec, out_shape=jax.ShapeDtypeStruct((2, r, c), F32))(core, own, landed)


def _exchange_halves(arrs, *, name):
    n = len(arrs)

    def body(*refs):
        ins, outs = refs[:n], refs[n:2 * n]
        send_sems, recv_sems = refs[2 * n:]
        ax, ay, ac = _place()
        cps = [pltpu.make_async_remote_copy(src_ref=ins[a].at[ac], dst_ref=outs[a].at[ac], send_sem=send_sems.at[a],
                                            recv_sem=recv_sems.at[a], device_id=(ax, ay, 1 - ac),
                                            device_id_type=MESH) for a in range(n)]
        for cp in cps:
            cp.start()
        for a in range(n):
            pltpu.make_async_remote_copy(src_ref=ins[a].at[ac], dst_ref=outs[a].at[1 - ac], send_sem=send_sems.at[a],
                                         recv_sem=recv_sems.at[a], device_id=(ax, ay, ac),
                                         device_id_type=MESH).wait_recv()
        for cp in cps:
            cp.wait_send()

    hbm = pl.BlockSpec(memory_space=pl.ANY)
    return _pcall(
        body, name=name, in_specs=[hbm] * n, out_specs=[hbm] * n,
        out_shape=[jax.ShapeDtypeStruct(a.shape, a.dtype) for a in arrs],
        input_output_aliases={a: a for a in range(n)},
        scratch_shapes=[pltpu.SemaphoreType.DMA((n,))] * 2,
    )(*arrs)


def _add2(a, b, *, name):
    r, c = a.shape
    br = _row_block(r, c, 1024 * 1024)

    def body(a_ref, b_ref, o_ref):
        o_ref[...] = (a_ref[...].astype(F32) + b_ref[...].astype(F32)).astype(BF16)

    blk = pl.BlockSpec((br, c), lambda i: (i, 0))
    return _pcall(body, name=name, grid=(r // br,), in_specs=[blk, blk], out_specs=blk,
                  out_shape=jax.ShapeDtypeStruct((r, c), BF16))(a, b)


def _adam(w, gs, m, v, *, name):
    r, c = w.shape
    br = _row_block(r, c)
    ng = len(gs)
    c1 = 1.0 - ADAM_B1 ** ADAM_STEP
    c2 = 1.0 - ADAM_B2 ** ADAM_STEP

    def body(*refs):
        w_ref, m_ref, v_ref = refs[0], refs[1 + ng], refs[2 + ng]
        outs = refs[3 + ng:]
        g = refs[1][...]
        for k in range(1, ng):
            g = g + refs[1 + k][...]
        mn = ADAM_B1 * m_ref[...] + (1.0 - ADAM_B1) * g
        vn = ADAM_B2 * v_ref[...] + (1.0 - ADAM_B2) * (g * g)
        if ng > 1:
            outs[0][...] = g
        d_out, m_out, v_out = outs[-3:]
        m_out[...] = mn
        v_out[...] = vn
        d_out[...] = -ADAM_LR * ((mn / c1) / (jnp.sqrt(vn / c2) + ADAM_EPS) + ADAM_WD * w_ref[...])

    blk = pl.BlockSpec((br, c), lambda i: (i, 0))
    nout = 4 if ng > 1 else 3
    res = _pcall(body, name=name, grid=(r // br,), in_specs=[blk] * (3 + ng), out_specs=[blk] * nout,
                 out_shape=[jax.ShapeDtypeStruct((r, c), F32)] * nout)(w, *gs, m, v)
    return list(res) if ng > 1 else [gs[0]] + list(res)


def _grad_halves(name, g, ac):
    if name.endswith('_in'):
        n = g.shape[1] // 4
        if name == 'ffn_in':
            assert n == FFN_BK
        order = _ffn_order(g.shape[1]) if name == 'ffn_in' else range(4)
        v = jnp.stack([g[:, b * n:(b + 1) * n] for b in order])
        per = [v[:, :g.shape[0] // 2], v[:, g.shape[0] // 2:]]
    else:
        k4, n = g.shape
        v = g.reshape(4, 2, k4 // 8, n)
        per = [v[:, 0], v[:, 1]]
    first = ac == 0
    return _bf(jnp.where(first, per[0], per[1])), _bf(jnp.where(first, per[1], per[0]))


class _GradReducer:
    def __init__(self):
        self.flight = {}

    @staticmethod
    def _plan(m, sending, refs):
        ax, ay, ac = _place()
        s = 2 * ax + ay
        out = []
        for a in range(m):
            for dx, dy in _CHIP_FLIPS:
                px, py = lax.rem(ax + dx, 2), lax.rem(ay + dy, 2)
                sp = 2 * px + py
                out.append((refs[a].at[sp], refs[m + a].at[s if sending else sp], (px, py, ac)))
        return out

    def start(self, grp, grads):
        ac = lax.axis_index("c")
        names = list(grads)
        halves = [_grad_halves(nm.rstrip('01'), grads[nm], ac) for nm in names]
        theirs = _to_sibling([h[1] for h in halves], name='swap_core_halves_' + grp)
        pair = [_add2(h[0].reshape(-1, b.shape[-1]), b.reshape(-1, b.shape[-1]), name='add_cores').reshape(b.shape)
                for h, b in zip(halves, theirs)]
        m = len(names)
        land = [lax.empty(a.shape, a.dtype) for a in pair]
        sends, recvs, bufs, token = _split_start(pair + land, functools.partial(self._plan, m, True), 3 * m,
                                                 name='scatter_' + grp + '_start')
        self.flight[grp] = (names, sends, recvs, bufs)
        return token

    def finish(self, grp, after):
        names, sends, recvs, bufs = self.flight.pop(grp)
        m = len(names)
        bufs = _split_wait(bufs, sends, recvs, functools.partial(self._plan, m, False), after,
                           name='scatter_' + grp + '_wait')
        core = lax.axis_index("c").astype(jnp.int32).reshape(1)
        sums = [_sum4(p, l, core, name='sum_chips') for p, l in zip(bufs[:m], bufs[m:])]
        both = _exchange_halves(sums, name='gather_core_halves_' + grp)
        return {nm: g.reshape(-1, g.shape[-1]) for nm, g in zip(names, both)}


def _from_shards(name, g):
    _, r, n = g.shape
    if name == 'ffn_in':
        assert n == FFN_BK
        v = g.reshape(4, 2, r // 2, n)
        return jnp.concatenate([v[b] for b in _ffn_order(4 * n)], axis=-1)
    if name == 'ffn_out':
        return g.reshape(4, 2, r // 2, n).transpose(1, 0, 2, 3).reshape(2, 2 * r, n)
    if name in ('even_in', 'odd_in'):
        return jnp.concatenate([g[b] for b in range(4)], axis=-1)
    return g.reshape(4 * r, n)


def kernel(x, c, ctx, c_ctx, mod_w, mod_b, norm_g, ffn_w_in, ffn_w_out, even_w_in, even_w_out, attn_qk_norm_g, attn_sink, hgrn_out_norm_g, hgrn_lb, odd_w_in, odd_w_out, loss_target, m_c_ctx, m_mod_w, m_mod_b, m_norm_g, m_ffn_w_in, m_ffn_w_out, m_even_w_in, m_even_w_out, m_attn_qk_norm_g, m_attn_sink, m_hgrn_out_norm_g, m_hgrn_lb, m_odd_w_in, m_odd_w_out, v_c_ctx, v_mod_w, v_mod_b, v_norm_g, v_ffn_w_in, v_ffn_w_out, v_even_w_in, v_even_w_out, v_attn_qk_norm_g, v_attn_sink, v_hgrn_out_norm_g, v_hgrn_lb, v_odd_w_in, v_odd_w_out):
    d = x.shape[-1]
    lc = ctx.shape[1]
    assert lc == TM and d == 1024
    ax, ay, ac = _place()
    s = 2 * ax + ay
    me = 4 * ax + 2 * ay + ac
    nmod = mod_w.shape[2]

    def pad8(v):
        return jnp.pad(v, ((0, 8 - v.shape[0]), (0, 0)))

    pack = jnp.concatenate([pad8(c), pad8(norm_g.reshape(1, d))], axis=0)
    g1 = _ag8(pack, name='gather_cond')
    c_all = g1[:, 0, :]
    ng = g1[0::2, 8, :].reshape(4, 2, 2, d // 4).transpose(1, 2, 0, 3).reshape(4, d)

    cond_raw = jnp.concatenate([c_all, pad8(c_ctx.reshape(1, d))], axis=0)
    mb_sh = lax.dynamic_slice_in_dim(mod_b, s * nmod, nmod, axis=1).reshape(2, 1, nmod)
    mpart = _mod_fwd(cond_raw, mod_w, mb_sh, name='mod_fwd')
    g3 = _ag8(mpart.reshape(32, nmod), name='gather_mods')
    mods_full = g3[0::2].reshape(4, 2, 16, nmod).transpose(1, 2, 0, 3).reshape(2, 16, 4 * nmod)
    m_lat = lax.dynamic_index_in_dim(mods_full, me, axis=1, keepdims=False)
    mods = jnp.stack([mods_full[:, 8], m_lat], axis=1).reshape(24, d)

    names = ['ffn_in', 'ffn_out', 'even_in', 'even_out', 'odd_in', 'odd_out']
    shards = [_bf(v.reshape(-1, v.shape[-1])) for v in (ffn_w_in, ffn_w_out, even_w_in, even_w_out, odd_w_in, odd_w_out)]
    shards, mods = lax.optimization_barrier((shards, mods))
    reducer = _GradReducer()
    wsrc = _GatheredWeights(dict(zip(names, shards)), reducer)

    lb = _lb_fwd(hgrn_lb, name='hgrn_lower_bound')
    small = dict(gq=jnp.tile(attn_qk_norm_g[0, 0], 2).reshape(1, 128), gk=jnp.tile(attn_qk_norm_g[0, 1], 2).reshape(1, 128),
                 sink=attn_sink[0], gain=hgrn_out_norm_g, lb=lb)
    x0 = jnp.concatenate([ctx[0], x[0]], axis=0) + wsrc.token[0, 0]
    loss_t, dx0, grads, sums = _local_step(x0, loss_target[0], mods, ng, wsrc, small)
    loss = lax.psum(loss_t[0, 0], ("x", "y", "c"))
    grad_x = dx0[None]

    def tile(v):
        return jnp.pad(v, ((0, 8 - v.shape[0]), (0, d - v.shape[1])))

    sums = dict(sums, sink=sums['sink'][:, 0].reshape(1, 8))
    g4 = _ag8(jnp.concatenate([tile(sums[nm]) for nm in PACK_TILES], axis=0), name='gather_row_sums')
    small_g, glb, gmb, dmat = _small_finalize(g4, tile(lb)[0:1], name='small_grads')
    dms = lax.dynamic_slice_in_dim(dmat.transpose(0, 2, 1, 3).reshape(2, 16, 6 * d), s * nmod, nmod, axis=2)
    g_mod_w, dcond = _mod_bwd(cond_raw, dms, mod_w, name='mod_bwd')
    g5 = _ag8(dcond[8:16], name='gather_dcond')
    g_c_ctx = _cctx_grad(g5, c_ctx.reshape(8, d // 8).reshape(1, d), name='c_ctx_grad')

    late = {nm: grads[nm] for nm in ('even_in', 'even_out')}
    late, g_c_ctx = lax.optimization_barrier((late, g_c_ctx))
    token = reducer.start('late', late)
    full = reducer.finish('early', token)

    def upd(wv, gs, mv, vv, name):
        shp = wv.shape
        c2 = shp[-1]
        out = _adam(wv.reshape(-1, c2), [g.reshape(-1, c2) for g in gs], mv.reshape(-1, c2), vv.reshape(-1, c2), name=name)
        return [o.reshape(shp) for o in out]

    res = {}
    res['c_ctx'] = upd(c_ctx.reshape(8, d // 8), [g_c_ctx.reshape(8, d // 8)], m_c_ctx.reshape(8, d // 8), v_c_ctx.reshape(8, d // 8), 'adam_c_ctx')
    res['c_ctx'] = [o.reshape(d) for o in res['c_ctx']]
    res['mod_w'] = upd(mod_w, [g_mod_w], m_mod_w, v_mod_w, 'adam_mod_w')
    res['mod_b'] = upd(mod_b, [gmb.reshape(2, 6 * d)], m_mod_b, v_mod_b, 'adam_mod_b')
    g_ng = lax.dynamic_slice_in_dim(small_g[0:4].reshape(2, 2, d), s * (d // 4), d // 4, axis=2)
    res['norm_g'] = upd(norm_g, [g_ng], m_norm_g, v_norm_g, 'adam_norm_g')
    g_qk = jnp.stack([small_g[4, 0:64], small_g[5, 0:64]]).reshape(1, 2, 64)
    res['attn_qk_norm_g'] = upd(attn_qk_norm_g, [g_qk], m_attn_qk_norm_g, v_attn_qk_norm_g, 'adam_qk_gain')
    res['attn_sink'] = upd(attn_sink, [small_g[7, 0:8].reshape(1, 8)], m_attn_sink, v_attn_sink, 'adam_sink')
    res['hgrn_out_norm_g'] = upd(hgrn_out_norm_g, [small_g[6, 0:128].reshape(1, 128)], m_hgrn_out_norm_g, v_hgrn_out_norm_g, 'adam_head_gain')
    res['hgrn_lb'] = upd(hgrn_lb, [glb[0:2, 0:hgrn_lb.shape[1]]], m_hgrn_lb, v_hgrn_lb, 'adam_hgrn_lb')
    res['odd_w_in'] = upd(odd_w_in, [full['odd_in']], m_odd_w_in, v_odd_w_in, 'adam_odd_in')
    res['odd_w_out'] = upd(odd_w_out, [full['odd_out']], m_odd_w_out, v_odd_w_out, 'adam_odd_out')
    full.update(reducer.finish('mid', res['odd_w_in'][1]))
    g_ffn_in = jnp.concatenate([full['ffn_in0'], full['ffn_in1']], axis=0)
    g_ffn_out = jnp.concatenate([full['ffn_out0'], full['ffn_out1']], axis=0)
    res['ffn_w_in'] = upd(ffn_w_in, [g_ffn_in], m_ffn_w_in, v_ffn_w_in, 'adam_ffn_in')
    res['ffn_w_out'] = upd(ffn_w_out, [g_ffn_out], m_ffn_w_out, v_ffn_w_out, 'adam_ffn_out')
    full.update(reducer.finish('late', res['ffn_w_in'][1]))
    res['even_w_in'] = upd(even_w_in, [full['even_in']], m_even_w_in, v_even_w_in, 'adam_even_in')
    res['even_w_out'] = upd(even_w_out, [full['even_out']], m_even_w_out, v_even_w_out, 'adam_even_out')

    order = ['c_ctx', 'mod_w', 'mod_b', 'norm_g', 'ffn_w_in', 'ffn_w_out', 'even_w_in', 'even_w_out',
             'attn_qk_norm_g', 'attn_sink', 'hgrn_out_norm_g', 'hgrn_lb', 'odd_w_in', 'odd_w_out']
    outs = [loss, grad_x]
    for k in range(4):
        outs += [res[nm][k] for nm in order]
    return tuple(outs)
```

```python
import functools
import math

import numpy as np
import jax
import jax.numpy as jnp
from jax import lax
from jax.experimental import pallas as pl
from jax.experimental.pallas import tpu as pltpu

F32 = jnp.float32
BF16 = jnp.bfloat16
EPS = 1e-6
TM = 256
CHUNK = 64
QB = 256
WINDOW = 128
NEG = -1e30
MESH = pl.DeviceIdType.MESH

ADAM_LR, ADAM_B1, ADAM_B2, ADAM_EPS, ADAM_WD, ADAM_STEP = 0.001, 0.9, 0.999, 1e-08, 0.01, 10


def _pcall(body, **kw):
    return pl.pallas_call(body, **kw)


def _pick(n, cap):
    best = None
    for m in range(128, min(n, cap) + 1, 128):
        if n % m == 0:
            best = m
    assert best is not None, (n, cap)
    return best


def _bf(x):
    return x.astype(BF16)


def _dot(a, b):
    return jnp.dot(_bf(a), _bf(b), preferred_element_type=F32)


def _dot_nt(a, b):
    return lax.dot_general(_bf(a), _bf(b), (((1,), (1,)), ((), ())), preferred_element_type=F32)


def _dot_tn(a, b):
    return lax.dot_general(_bf(a), _bf(b), (((0,), (0,)), ((), ())), preferred_element_type=F32)


def _dot_exact(a, b):
    return jnp.dot(a, b, preferred_element_type=F32, precision=lax.Precision.HIGHEST)


def _sigmoid(x):
    return 1.0 / (1.0 + jnp.exp(-x))


def _iota(shape, dim):
    return lax.broadcasted_iota(jnp.int32, shape, dim)


def _mm_nn(a, b, *, lead=None, out_dtype=F32, name):
    m, k = a.shape
    n = b.shape[-1]
    bm = 1408 if (m % 1408 == 0 and k <= 1024) else (768 if m % 768 == 0 else TM)
    bn = _pick(n, 1024) if n % 512 == 0 else _pick(n, 1664)

    def body(a_ref, b_ref, o_ref):
        o_ref[...] = _dot(a_ref[...], b_ref[...]).astype(o_ref.dtype)

    if lead is None:
        b_spec = pl.BlockSpec((k, bn), lambda i, j: (0, j))
    else:
        b_spec = pl.BlockSpec((None, k, bn), lambda i, j: (lead, 0, j))
    return _pcall(
        body, name=name, grid=(m // bm, n // bn),
        in_specs=[pl.BlockSpec((bm, k), lambda i, j: (i, 0)), b_spec],
        out_specs=pl.BlockSpec((bm, bn), lambda i, j: (i, j)),
        out_shape=jax.ShapeDtypeStruct((m, n), out_dtype),
    )(a, b)


def _mm_nt(a, b, *, lead=None, name):
    m, n = a.shape
    k = b.shape[-2]
    bm = 768 if m % 768 == 0 else TM
    bk = _pick(k, 512)

    def body(a_ref, b_ref, o_ref):
        o_ref[...] = _dot_nt(a_ref[...], b_ref[...])

    if lead is None:
        b_spec = pl.BlockSpec((bk, n), lambda i, j: (j, 0))
    else:
        b_spec = pl.BlockSpec((None, bk, n), lambda i, j: (lead, j, 0))
    return _pcall(
        body, name=name, grid=(m // bm, k // bk),
        in_specs=[pl.BlockSpec((bm, n), lambda i, j: (i, 0)), b_spec],
        out_specs=pl.BlockSpec((bm, bk), lambda i, j: (i, j)),
        out_shape=jax.ShapeDtypeStruct((m, k), F32),
    )(a, b)


def _mm_tn(a, b, *, name):
    t, k = a.shape
    n = b.shape[1]
    bt = 768 if t % 768 == 0 else TM
    bk = _pick(k, 1536)
    bn = _pick(n, 1024) if n % 1024 == 0 or n < 1664 else _pick(n, 1664)

    def body(a_ref, b_ref, o_ref):
        @pl.when(pl.program_id(2) == 0)
        def _():
            o_ref[...] = jnp.zeros_like(o_ref)
        o_ref[...] += _dot_tn(a_ref[...], b_ref[...])

    return _pcall(
        body, name=name, grid=(k // bk, n // bn, t // bt),
        in_specs=[pl.BlockSpec((bt, bk), lambda i, j, s: (s, i)),
                  pl.BlockSpec((bt, bn), lambda i, j, s: (s, j))],
        out_specs=pl.BlockSpec((bk, bn), lambda i, j, s: (i, j)),
        out_shape=jax.ShapeDtypeStruct((k, n), F32),
    )(a, b)


def _mod_row(mods_ref, lat, idx):
    return jnp.where(lat, mods_ref[idx + 6:idx + 7, :], mods_ref[idx:idx + 1, :])


def _row_fwd(x, mods, *, y=None, gate=None, g=None, shift=None, scale=None, name):
    t, d = x.shape
    has_y, has_n = y is not None, g is not None

    def body(*refs):
        refs = list(refs)
        x_ref, mods_ref = refs[0], refs[1]
        pos = 2
        if has_y:
            y_ref = refs[pos]; pos += 1
        if has_n:
            g_ref = refs[pos]; pos += 1
        outs = refs[pos:]
        lat = pl.program_id(0) > 0
        x1 = x_ref[...]
        o = 0
        if has_y:
            x1 = x1 + _mod_row(mods_ref, lat, gate) * y_ref[...]
            outs[o][...] = x1; o += 1
        if has_n:
            rs = lax.rsqrt(jnp.mean(x1 * x1, axis=-1, keepdims=True) + EPS)
            hn = x1 * rs * g_ref[...]
            h = hn * (1.0 + _mod_row(mods_ref, lat, scale)) + _mod_row(mods_ref, lat, shift)
            outs[o][...] = h.astype(BF16)

    row = pl.BlockSpec((TM, d), lambda i: (i, 0))
    ins, specs = [x, mods], [row, pl.BlockSpec(mods.shape, lambda i: (0, 0))]
    if has_y:
        ins.append(y); specs.append(row)
    if has_n:
        ins.append(g.reshape(1, d)); specs.append(pl.BlockSpec((1, d), lambda i: (0, 0)))
    out_shape, out_specs = [], []
    if has_y:
        out_shape.append(jax.ShapeDtypeStruct((t, d), F32)); out_specs.append(row)
    if has_n:
        out_shape.append(jax.ShapeDtypeStruct((t, d), BF16)); out_specs.append(row)
    res = _pcall(body, name=name, grid=(t // TM,), in_specs=specs, out_specs=out_specs,
                 out_shape=out_shape)(*ins)
    return res


def _acc_row(ref, r, val):
    ref[r:r + 1, :] += val


def _row_final(x, z, mods, target, *, gate, name):
    t, d = x.shape

    def body(x_ref, mods_ref, z_ref, t_ref, loss_ref, dx_ref, dz_ref, sums_ref):
        i = pl.program_id(0)
        lat = i > 0

        @pl.when(i == 0)
        def _():
            loss_ref[...] = jnp.zeros_like(loss_ref)
            sums_ref[...] = jnp.zeros_like(sums_ref)

        gt = _mod_row(mods_ref, lat, gate)
        zz = z_ref[...]
        yv = x_ref[...] + gt * zz
        keep = jnp.where(lat, 1.0, 0.0).astype(F32)
        diff = (yv - t_ref[...]) * keep
        part = jnp.sum(jnp.sum(diff * diff, axis=0, keepdims=True), axis=1, keepdims=True)
        loss_ref[...] += part * (0.5 / d)
        dy = diff * (1.0 / d)
        dx_ref[...] = dy
        dz_ref[...] = (gt * dy).astype(BF16)
        _acc_row(sums_ref, 6, jnp.sum(dy * zz, axis=0, keepdims=True))

    row = pl.BlockSpec((TM, d), lambda i: (i, 0))
    return _pcall(
        body, name=name, grid=(t // TM,),
        in_specs=[row, pl.BlockSpec(mods.shape, lambda i: (0, 0)), row,
                  pl.BlockSpec((TM, d), lambda i: (jnp.maximum(i - 1, 0), 0))],
        out_specs=[pl.BlockSpec((8, 128), lambda i: (0, 0)), row, row,
                   pl.BlockSpec((8, d), lambda i: (0, 0))],
        out_shape=[jax.ShapeDtypeStruct((8, 128), F32), jax.ShapeDtypeStruct((t, d), F32),
                   jax.ShapeDtypeStruct((t, d), BF16), jax.ShapeDtypeStruct((8, d), F32)],
    )(x, mods, z, target)


def _row_bwd(xn, dxo, dh, mods, g, *, shift, scale, y=None, gate=None, latent_only=False, name):
    t, d = xn.shape
    has_y = y is not None

    def body(*refs):
        refs = list(refs)
        x_ref, dxo_ref, dh_ref, mods_ref, g_ref = refs[:5]
        pos = 5
        if has_y:
            y_ref = refs[pos]; pos += 1
        dx_ref = refs[pos]; pos += 1
        if has_y:
            dy_ref = refs[pos]; pos += 1
        sums_ref = refs[pos]
        i = pl.program_id(0)
        lat = i > 0

        @pl.when(i == 0)
        def _():
            sums_ref[...] = jnp.zeros_like(sums_ref)

        x1 = x_ref[...]
        gv = g_ref[...]
        rs = lax.rsqrt(jnp.mean(x1 * x1, axis=-1, keepdims=True) + EPS)
        xh = x1 * rs
        dhv = dh_ref[...]
        dn = dhv * (1.0 + _mod_row(mods_ref, lat, scale))
        dxh = dn * gv
        dx = dxo_ref[...] + rs * (dxh - xh * jnp.mean(dxh * xh, axis=-1, keepdims=True))
        dx_ref[...] = dx
        vals = [jnp.sum(dhv, axis=0, keepdims=True),
                jnp.sum(dhv * (xh * gv), axis=0, keepdims=True),
                None,
                jnp.sum(dn * xh, axis=0, keepdims=True)]
        if has_y:
            dy_ref[...] = (_mod_row(mods_ref, lat, gate) * dx).astype(BF16)
            vals[2] = jnp.sum(dx * y_ref[...], axis=0, keepdims=True)

        @pl.when(i == 0)
        def _():
            for r, v in enumerate(vals):
                if v is not None:
                    _acc_row(sums_ref, r, v)

        @pl.when(i > 0)
        def _():
            for r, v in enumerate(vals):
                if v is not None:
                    _acc_row(sums_ref, 4 + r, v)

    row = pl.BlockSpec((TM, d), lambda i: (i, 0))
    ins = [xn, dxo, dh, mods, g.reshape(1, d)]
    specs = [row, row, row, pl.BlockSpec(mods.shape, lambda i: (0, 0)), pl.BlockSpec((1, d), lambda i: (0, 0))]
    if latent_only:
        out_shape = [jax.ShapeDtypeStruct((t - TM, d), F32)]
        out_specs = [pl.BlockSpec((TM, d), lambda i: (jnp.maximum(i - 1, 0), 0))]
    else:
        out_shape, out_specs = [jax.ShapeDtypeStruct((t, d), F32)], [row]
    if has_y:
        ins.append(y); specs.append(row)
        out_shape.append(jax.ShapeDtypeStruct((t, d), BF16)); out_specs.append(row)
    out_shape.append(jax.ShapeDtypeStruct((8, d), F32))
    out_specs.append(pl.BlockSpec((8, d), lambda i: (0, 0)))
    return _pcall(body, name=name, grid=(t // TM,), in_specs=specs, out_specs=out_specs,
                  out_shape=out_shape)(*ins)


FFN_BK = 1408


FFN_SUB = 256


def _ffn_order(n2):
    nb = n2 // (2 * FFN_BK)
    return [h * nb + j for j in range(nb) for h in (0, 1)]


def _ffn_interleave(w):
    return jnp.concatenate([w[..., b * FFN_BK:(b + 1) * FFN_BK] for b in _ffn_order(w.shape[-1])], axis=-1)


def _ffn_deinterleave(w):
    order = _ffn_order(w.shape[-1])
    return jnp.concatenate([w[..., order.index(b) * FFN_BK:(order.index(b) + 1) * FFN_BK]
                            for b in range(len(order))], axis=-1)


def _big_tile(t):
    return 768 if t % 768 == 0 else TM


def _ffn_in(h, w, *, lead, name):
    t, d = h.shape
    n2 = w.shape[-1]
    bm, bk = _big_tile(t), FFN_BK

    def body(h_ref, w_ref, u_ref, a_ref):
        hb = h_ref[...]
        for c0 in range(0, bk, FFN_SUB):
            c1 = min(c0 + FFN_SUB, bk)
            ug = _dot(hb, w_ref[:, c0:c1]).astype(BF16)
            uu = _dot(hb, w_ref[:, bk + c0:bk + c1]).astype(BF16)
            u_ref[:, c0:c1] = ug
            u_ref[:, bk + c0:bk + c1] = uu
            gv, up = ug.astype(F32), uu.astype(F32)
            a_ref[:, c0:c1] = (gv * _sigmoid(gv) * up).astype(BF16)

    return _pcall(
        body, name=name, grid=(t // bm, n2 // (2 * bk)),
        in_specs=[pl.BlockSpec((bm, d), lambda i, j: (i, 0)),
                  pl.BlockSpec((None, d, 2 * bk), lambda i, j: (lead, 0, j))],
        out_specs=[pl.BlockSpec((bm, 2 * bk), lambda i, j: (i, j)), pl.BlockSpec((bm, bk), lambda i, j: (i, j))],
        out_shape=[jax.ShapeDtypeStruct((t, n2), BF16), jax.ShapeDtypeStruct((t, n2 // 2), BF16)],
    )(h, w)


def _ffn_dx(dz, w_out, u, *, lead, name):
    t, d = dz.shape
    n2 = u.shape[1]
    bm, bk = _big_tile(t), FFN_BK

    def body(dz_ref, w_ref, u_ref, du_ref):
        dzb = dz_ref[...]
        for c0 in range(0, bk, FFN_SUB):
            c1 = min(c0 + FFN_SUB, bk)
            da = _dot_nt(dzb, w_ref[c0:c1, :])
            gv, up = u_ref[:, c0:c1].astype(F32), u_ref[:, bk + c0:bk + c1].astype(F32)
            s = _sigmoid(gv)
            du_ref[:, c0:c1] = (da * up * (s * (1.0 + gv * (1.0 - s)))).astype(BF16)
            du_ref[:, bk + c0:bk + c1] = (da * gv * s).astype(BF16)

    ublk = pl.BlockSpec((bm, 2 * bk), lambda i, j: (i, j))
    return _pcall(
        body, name=name, grid=(t // bm, n2 // (2 * bk)),
        in_specs=[pl.BlockSpec((bm, d), lambda i, j: (i, 0)),
                  pl.BlockSpec((None, bk, d), lambda i, j: (lead, j, 0)), ublk],
        out_specs=ublk, out_shape=jax.ShapeDtypeStruct((t, n2), BF16),
    )(dz, w_out, u)


def _lane(shape):
    return _iota(shape, len(shape) - 1)


def _pair_norm(x, g):
    lo = _lane(x.shape) < 64
    x2 = x * x
    s_lo = jnp.sum(jnp.where(lo, x2, 0.0), axis=-1, keepdims=True)
    s_hi = jnp.sum(jnp.where(lo, 0.0, x2), axis=-1, keepdims=True)
    rs = lax.rsqrt(jnp.where(lo, s_lo, s_hi) * (1.0 / 64) + EPS)
    return x * rs, rs


def _pair_mean(v):
    lo = _lane(v.shape) < 64
    s_lo = jnp.sum(jnp.where(lo, v, 0.0), axis=-1, keepdims=True)
    s_hi = jnp.sum(jnp.where(lo, 0.0, v), axis=-1, keepdims=True)
    return jnp.where(lo, s_lo, s_hi) * (1.0 / 64)


def _rot64(x):
    r1 = pltpu.roll(x, 32, 1)
    r2 = pltpu.roll(x, 96, 1)
    even = ((_lane(x.shape) >> 5) & 1) == 0
    return jnp.where(even, -r2, r1)


def _rope64(x, cos, sin):
    return x * cos + _rot64(x) * sin


def _rope64_t(d, cos, sin):
    return d * cos - _rot64(d * sin)


def _kprep_fwd(p, gk, cos, sin, *, name):
    t = p.shape[0]

    def body(k_ref, g_ref, c_ref, s_ref, o_ref):
        xh, _ = _pair_norm(k_ref[...], None)
        o_ref[...] = _rope64(xh * g_ref[...], c_ref[...], s_ref[...])

    blk = pl.BlockSpec((TM, 128), lambda i: (i, 0))
    return _pcall(
        body, name=name, grid=(t // TM,),
        in_specs=[pl.BlockSpec((TM, 128), lambda i: (i, 4)), pl.BlockSpec((1, 128), lambda i: (0, 0)), blk, blk],
        out_specs=blk, out_shape=jax.ShapeDtypeStruct((t, 128), F32),
    )(p, gk, cos, sin)


def _kprep_bwd(p, gk, cos, sin, dkp, dv, *, name):
    t = p.shape[0]

    def body(k_ref, g_ref, c_ref, s_ref, dkp_ref, dv_ref, o_ref, dg_ref):
        @pl.when(pl.program_id(0) == 0)
        def _():
            dg_ref[...] = jnp.zeros_like(dg_ref)
        xh, rs = _pair_norm(k_ref[...], None)
        dn = _rope64_t(dkp_ref[...], c_ref[...], s_ref[...])
        _acc_row(dg_ref, 0, jnp.sum(dn * xh, axis=0, keepdims=True))
        dxh = dn * g_ref[...]
        o_ref[:, 0:128] = (rs * (dxh - xh * _pair_mean(dxh * xh))).astype(BF16)
        o_ref[:, 128:256] = dv_ref[...].astype(BF16)

    blk = pl.BlockSpec((TM, 128), lambda i: (i, 0))
    return _pcall(
        body, name=name, grid=(t // TM,),
        in_specs=[pl.BlockSpec((TM, 128), lambda i: (i, 4)), pl.BlockSpec((1, 128), lambda i: (0, 0)), blk, blk, blk, blk],
        out_specs=[pl.BlockSpec((TM, 256), lambda i: (i, 0)), pl.BlockSpec((8, 128), lambda i: (0, 0))],
        out_shape=[jax.ShapeDtypeStruct((t, 256), BF16), jax.ShapeDtypeStruct((8, 128), F32)],
    )(p, gk, cos, sin, dkp, dv)


def _attn_common(i, t, lc, kp_ref, v_ref):
    span = QB + 2 * WINDOW
    start = pl.multiple_of(jnp.clip(i * QB - WINDOW, lc, t - span), WINDOW)
    kall = jnp.concatenate([kp_ref[0:lc, :], kp_ref[pl.ds(start, span), :]], axis=0)
    vall = jnp.concatenate([v_ref[0:lc, :], v_ref[pl.ds(start, span), :]], axis=0)
    nk = lc + span
    col = _iota((QB, nk), 1)
    krow = jnp.where(col < lc, col, start + col - lc)
    qrow = i * QB + _iota((QB, nk), 0)
    valid = (col < lc) | ((qrow >= lc) & (krow >= lc) & (jnp.abs(krow - qrow) <= WINDOW))
    lo = _lane(kall.shape) < 64
    kroll, vroll = pltpu.roll(kall, 64, 1), pltpu.roll(vall, 64, 1)
    zero = jnp.zeros_like(kall)
    kvar = [[_bf(jnp.where(lo, kall, zero)), _bf(jnp.where(lo, zero, kroll))],
            [_bf(jnp.where(lo, kroll, zero)), _bf(jnp.where(lo, zero, kall))]]
    vvar = [[_bf(jnp.where(lo, vall, zero)), _bf(jnp.where(lo, zero, vroll))],
            [_bf(jnp.where(lo, vroll, zero)), _bf(jnp.where(lo, zero, vall))]]
    return start, valid, kvar, vvar


def _softmax_sink(s, valid, snk):
    s = jnp.where(valid, s, NEG)
    m = jnp.maximum(jnp.max(s, axis=-1, keepdims=True), snk)
    e = jnp.exp(s - m)
    es = jnp.exp(snk - m)
    inv = 1.0 / (jnp.sum(e, axis=-1, keepdims=True) + es)
    return e * inv, es * inv


def _attn_fwd(p, kp, gq, sink, cos, sin, *, lc, name):
    t = p.shape[0]
    scale = 64 ** -0.5

    def body(q_ref, kp_ref, v_ref, g_ref, sink_ref, c_ref, s_ref, o_ref):
        i = pl.program_id(0)
        _, valid, kvar, vvar = _attn_common(i, t, lc, kp_ref, v_ref)
        cosv, sinv, gv = c_ref[...], s_ref[...], g_ref[...]
        for j in range(4):
            xh, _ = _pair_norm(q_ref[:, 128 * j:128 * j + 128], None)
            q2 = _bf(_rope64(xh * gv, cosv, sinv) * scale)
            acc = jnp.zeros((QB, 128), F32)
            for half in range(2):
                s = _dot_nt(q2, kvar[j // 2][half])
                pr, _ = _softmax_sink(s, valid, sink_ref[2 * j + half])
                acc = acc + _dot(pr, vvar[j // 2][half])
            o_ref[:, 128 * j:128 * j + 128] = acc.astype(BF16)

    qblk = pl.BlockSpec((QB, 128), lambda i: (i, 0))
    return _pcall(
        body, name=name, grid=(t // QB,),
        in_specs=[pl.BlockSpec((QB, 512), lambda i: (i, 0)),
                  pl.BlockSpec((t, 128), lambda i: (0, 0)),
                  pl.BlockSpec((t, 128), lambda i: (0, 5)),
                  pl.BlockSpec((1, 128), lambda i: (0, 0)),
                  pl.BlockSpec(memory_space=pltpu.SMEM), qblk, qblk],
        out_specs=pl.BlockSpec((QB, 512), lambda i: (i, 0)),
        out_shape=jax.ShapeDtypeStruct((t, 512), BF16),
    )(p, kp, p, gq, sink, cos, sin)


def _attn_bwd(p, kp, gq, sink, cos, sin, dmix, *, lc, name):
    t = p.shape[0]
    scale = 64 ** -0.5
    span = QB + 2 * WINDOW

    def body(q_ref, kp_ref, v_ref, g_ref, sink_ref, c_ref, s_ref, do_ref,
             dq_ref, dk_ref, dv_ref, dg_ref, dsink_ref):
        i = pl.program_id(0)

        @pl.when(i == 0)
        def _():
            dk_ref[...] = jnp.zeros_like(dk_ref)
            dv_ref[...] = jnp.zeros_like(dv_ref)
            dg_ref[...] = jnp.zeros_like(dg_ref)
            dsink_ref[...] = jnp.zeros_like(dsink_ref)

        start, valid, kvar, vvar = _attn_common(i, t, lc, kp_ref, v_ref)
        cosv, sinv, gv = c_ref[...], s_ref[...], g_ref[...]
        nk = lc + span
        dkt = [jnp.zeros((64, nk), F32), jnp.zeros((64, nk), F32)]
        dvt = [jnp.zeros((64, nk), F32), jnp.zeros((64, nk), F32)]
        for j in range(4):
            kvh = j // 2
            xh, rs = _pair_norm(q_ref[:, 128 * j:128 * j + 128], None)
            q2 = _bf(_rope64(xh * gv, cosv, sinv) * scale)
            do2 = _bf(do_ref[:, 128 * j:128 * j + 128])
            dq2 = jnp.zeros((QB, 128), F32)
            for half in range(2):
                s = _dot_nt(q2, kvar[kvh][half])
                pr, ps = _softmax_sink(s, valid, sink_ref[2 * j + half])
                dp = _dot_nt(do2, vvar[kvh][half])
                delta = jnp.sum(pr * dp, axis=-1, keepdims=True)
                ds = pr * (dp - delta)
                dsk = jnp.sum(jnp.sum(-ps * delta, axis=0, keepdims=True), axis=1, keepdims=True)
                _acc_row(dsink_ref, 2 * j + half, jnp.broadcast_to(dsk, (1, 128)))
                dq2 = dq2 + _dot(ds, kvar[kvh][half])
                hrows = slice(64 * half, 64 * half + 64)
                dkt[kvh] = dkt[kvh] + _dot_tn(q2, ds)[hrows]
                dvt[kvh] = dvt[kvh] + _dot_tn(do2, pr)[hrows]
            dn = _rope64_t(dq2 * scale, cosv, sinv)
            _acc_row(dg_ref, 0, jnp.sum(dn * xh, axis=0, keepdims=True))
            dxh = dn * gv
            dq_ref[:, 128 * j:128 * j + 128] = (rs * (dxh - xh * _pair_mean(dxh * xh))).astype(BF16)
        dk_all = jnp.concatenate(dkt, axis=0).T
        dv_all = jnp.concatenate(dvt, axis=0).T
        dk_ref[0:lc, :] += dk_all[0:lc]
        dv_ref[0:lc, :] += dv_all[0:lc]
        dk_ref[pl.ds(start, span), :] += dk_all[lc:nk]
        dv_ref[pl.ds(start, span), :] += dv_all[lc:nk]

    qblk = pl.BlockSpec((QB, 128), lambda i: (i, 0))
    full = pl.BlockSpec((t, 128), lambda i: (0, 0))
    small = pl.BlockSpec((8, 128), lambda i: (0, 0))
    return _pcall(
        body, name=name, grid=(t // QB,),
        in_specs=[pl.BlockSpec((QB, 512), lambda i: (i, 0)), full,
                  pl.BlockSpec((t, 128), lambda i: (0, 5)),
                  pl.BlockSpec((1, 128), lambda i: (0, 0)),
                  pl.BlockSpec(memory_space=pltpu.SMEM), qblk, qblk,
                  pl.BlockSpec((QB, 512), lambda i: (i, 0))],
        out_specs=[pl.BlockSpec((QB, 512), lambda i: (i, 0)), full, full, small, small],
        out_shape=[jax.ShapeDtypeStruct((t, 512), BF16), jax.ShapeDtypeStruct((t, 128), F32),
                   jax.ShapeDtypeStruct((t, 128), F32), jax.ShapeDtypeStruct((8, 128), F32),
                   jax.ShapeDtypeStruct((8, 128), F32)],
    )(p, kp, p, gq, sink, cos, sin, dmix)


def _tri(rev):
    r, c = _iota((CHUNK, CHUNK), 0), _iota((CHUNK, CHUNK), 1)
    return (c >= r) if rev else (c <= r)


def _blk_map(nb, rev, backward):
    if not rev:
        return (lambda n: nb - 1 - n) if backward else (lambda n: n)
    if backward:
        return lambda n: jnp.where(n < nb - 1, n + 1, 0)
    return lambda n: jnp.where(n == 0, 0, nb - n)


def _chunk_order(rev, backward, nc=TM // CHUNK):
    order = list(range(nc))
    return order[::-1] if (rev != backward) else order


def _hgrn_gates(qraw, fraw, lb):
    sq = _sigmoid(qraw)
    sf = _sigmoid(fraw)
    f = lb + (1.0 - lb) * sf
    return qraw * sq, 1.0 - f, jnp.log(f), sq, sf, f


HGRN_HP = 2


def _chunk_cumsum(x, rev):
    n = x.shape[0]
    pos = _iota(x.shape, 0) & (CHUNK - 1)
    s = 1
    while s < CHUNK:
        if rev:
            x = x + jnp.where(pos < CHUNK - s, pltpu.roll(x, n - s, 0), 0.0)
        else:
            x = x + jnp.where(pos >= s, pltpu.roll(x, s, 0), 0.0)
        s *= 2
    return x


def _block_terms(lf, rev):
    b = _chunk_cumsum(lf, rev)
    mid, last = (CHUNK // 2 - 1, 0) if rev else (CHUNK // 2, CHUNK - 1)

    def chunk_row(off):
        return jnp.concatenate([jnp.broadcast_to(b[c * CHUNK + off:c * CHUNK + off + 1, :], (CHUNK, b.shape[1]))
                                for c in range(TM // CHUNK)], axis=0)

    r, bl = chunk_row(mid), chunk_row(last)
    return _tri(rev), jnp.exp(b - r), jnp.exp(r - b), jnp.exp(b), jnp.exp(bl - b), jnp.exp(bl)


def _headnorm_apply(o, gv, gain):
    n = o * lax.rsqrt(jnp.mean(o * o, axis=-1, keepdims=True) + EPS)
    if gain is not None:
        n = n * gain
    return (n * (gv * _sigmoid(gv))).astype(BF16)


def _headnorm_grad(o, gv, dy, gain):
    rs = lax.rsqrt(jnp.mean(o * o, axis=-1, keepdims=True) + EPS)
    xh = o * rs
    n = xh * gain if gain is not None else xh
    sg = _sigmoid(gv)
    dn = dy * (gv * sg)
    dg = (dy * n * (sg * (1.0 + gv * (1.0 - sg)))).astype(BF16)
    dgain = jnp.sum(dn * xh, axis=0, keepdims=True)
    dxh = dn * gain if gain is not None else dn
    return rs * (dxh - xh * jnp.mean(dxh * xh, axis=-1, keepdims=True)), dg, dgain


def _hgrn_fwd(p, lb, *, rev, name, ofw=None, gain=None):
    t = p.shape[0]
    nb, nc = t // TM, TM // CHUNK
    bmap = _blk_map(nb, rev, False)
    fcol = 14 if rev else 10
    fused = ofw is not None

    def body(*refs):
        q_ref, f_ref, v_ref, lb_ref = refs[:4]
        if fused:
            ofw_ref, g_ref, gain_ref, o_ref, sh_ref, mix_ref, st = refs[4:]
        else:
            o_ref, sh_ref, st = refs[4:]

        @pl.when(pl.program_id(1) == 0)
        def _():
            st[...] = jnp.zeros_like(st)
        for hh in range(HGRN_HP):
            ln = slice(128 * hh, 128 * hh + 128)
            q, k, lf, _, _, _ = _hgrn_gates(q_ref[:, ln], f_ref[:, ln], lb_ref[:, ln])
            tri, eq, ek, ei, eki, eb = _block_terms(lf, rev)
            qe, ke, qi, ki, vb = _bf(q * eq), _bf(k * ek), _bf(q * ei), _bf(k * eki), _bf(v_ref[:, ln])
            intra = []
            for cc in range(nc):
                rows = slice(cc * CHUNK, (cc + 1) * CHUNK)
                a = jnp.where(tri, _dot_nt(qe[rows], ke[rows]), 0.0)
                intra.append(_dot(a, vb[rows]))
            s = st[hh]
            for cc in _chunk_order(rev, False):
                rows = slice(cc * CHUNK, (cc + 1) * CHUNK)
                sh_ref[hh, cc] = s
                o_ref[rows, ln] = intra[cc] + _dot_nt(qi[rows], s)
                s = s * eb[cc * CHUNK:cc * CHUNK + 1, :] + _dot_tn(vb[rows], ki[rows])
            st[hh] = s
            if fused:
                osum = o_ref[:, ln] + ofw_ref[:, ln]
                o_ref[:, ln] = osum
                mix_ref[:, ln] = _headnorm_apply(osum, g_ref[:, ln], gain_ref[...])

    hp, wd = HGRN_HP, 128 * HGRN_HP

    def col(c0):
        return pl.BlockSpec((TM, wd), lambda h, n: (bmap(n), c0 // hp + h))

    oblk = pl.BlockSpec((TM, wd), lambda h, n: (bmap(n), h))
    ins, specs = [p, p, p, lb], [col(6), col(fcol), col(18), pl.BlockSpec((1, wd), lambda h, n: (0, h))]
    out_specs = [oblk, pl.BlockSpec((hp, nc, 128, 128), lambda h, n: (h, bmap(n), 0, 0))]
    out_shape = [jax.ShapeDtypeStruct((t, 512), F32), jax.ShapeDtypeStruct((4, t // CHUNK, 128, 128), F32)]
    if fused:
        ins += [ofw, p, gain]
        specs += [oblk, col(22), pl.BlockSpec((1, 128), lambda h, n: (0, 0))]
        out_specs.append(oblk)
        out_shape.append(jax.ShapeDtypeStruct((t, 512), BF16))
    return _pcall(body, name=name, grid=(4 // hp, nb), in_specs=specs, out_specs=out_specs, out_shape=out_shape,
                  scratch_shapes=[pltpu.VMEM((hp, 128, 128), F32)])(*ins)


def _hgrn_bwd(p, lb, sh, do, prev, *, rev, name, head=None):
    t = p.shape[0]
    nb, nc = t // TM, TM // CHUNK
    bmap = _blk_map(nb, rev, True)
    fcol = 14 if rev else 10
    has_prev = prev is not None
    odt = BF16 if has_prev else F32
    fused = head is not None

    def body(*refs):
        refs = list(refs)
        q_ref, f_ref, v_ref, lb_ref, sh_ref = refs[:5]
        pos = 5
        if fused:
            osum_ref, g_ref, dmix_ref, gain_ref = refs[5:9]
            pos = 9
        else:
            do_ref = refs[5]
            pos = 6
        if has_prev:
            pq_ref, pv_ref = refs[pos], refs[pos + 1]
            pos += 2
        dq_ref, df_ref, dv_ref, dlb_ref = refs[pos:pos + 4]
        pos += 4
        if fused:
            do_out, dg_ref, dgain_ref = refs[pos:pos + 3]
            pos += 3
        dst = refs[pos]

        @pl.when(pl.program_id(1) == 0)
        def _():
            dst[...] = jnp.zeros_like(dst)
            dlb_ref[...] = jnp.zeros_like(dlb_ref)

        if fused:
            @pl.when((pl.program_id(0) == 0) & (pl.program_id(1) == 0))
            def _():
                dgain_ref[...] = jnp.zeros_like(dgain_ref)

        cat = functools.partial(jnp.concatenate, axis=0)
        for hh in range(HGRN_HP):
            ln = slice(128 * hh, 128 * hh + 128)
            lbv = lb_ref[:, ln]
            qraw, fraw = q_ref[:, ln], f_ref[:, ln]
            q, k, lf, sq, sf, f = _hgrn_gates(qraw, fraw, lbv)
            tri, eq, ek, ei, eki, eb = _block_terms(lf, rev)
            qe, ke, qi, ki = q * eq, k * ek, q * ei, k * eki
            if fused:
                dov, dg, dgain = _headnorm_grad(osum_ref[:, ln], g_ref[:, ln], dmix_ref[:, ln], gain_ref[...])
                do_out[:, ln] = dov
                dg_ref[:, ln] = dg
                _acc_row(dgain_ref, 0, dgain)
            else:
                dov = do_ref[:, ln]
            qeb, keb, qib, kib, vb, dob = _bf(qe), _bf(ke), _bf(qi), _bf(ki), _bf(v_ref[:, ln]), _bf(dov)
            dv, dqe, dke, dqi = [None] * nc, [None] * nc, [None] * nc, [None] * nc
            for cc in range(nc):
                rows = slice(cc * CHUNK, (cc + 1) * CHUNK)
                a = jnp.where(tri, _dot_nt(qeb[rows], keb[rows]), 0.0)
                da = jnp.where(tri, _dot_nt(dob[rows], vb[rows]), 0.0)
                dv[cc] = _dot_tn(a, dob[rows])
                dqe[cc], dke[cc] = _dot(da, keb[rows]), _dot_tn(da, qeb[rows])
                dqi[cc] = _dot(dob[rows], sh_ref[hh, cc])
            dki, dbl = [None] * nc, [None] * nc
            ds = dst[hh]
            for cc in _chunk_order(rev, True):
                rows = slice(cc * CHUNK, (cc + 1) * CHUNK)
                ebc = eb[cc * CHUNK:cc * CHUNK + 1, :]
                dv[cc] = dv[cc] + _dot_nt(kib[rows], ds)
                dki[cc] = _dot(vb[rows], ds)
                dbl[cc] = jnp.broadcast_to(jnp.sum(dki[cc] * ki[rows], axis=0, keepdims=True)
                                           + jnp.sum(ds * sh_ref[hh, cc], axis=0, keepdims=True) * ebc, (CHUNK, 128))
                ds = ds * ebc + _dot_tn(dob[rows], qib[rows])
            dst[hh] = ds
            dqe, dke, dqi, dki, dv, dbl = cat(dqe), cat(dke), cat(dqi), cat(dki), cat(dv), cat(dbl)
            dq = dqe * eq + dqi * ei
            dk = dke * ek + dki * eki
            last = 0 if rev else CHUNK - 1
            db = dqe * qe - dke * ke + dqi * qi - dki * ki
            db = db + jnp.where((_iota(db.shape, 0) & (CHUNK - 1)) == last, dbl, 0.0)
            dlf = _chunk_cumsum(db, not rev)
            dqr = dq * (sq * (1.0 + qraw * (1.0 - sq)))
            dfv = dlf / f - dk
            dfr = dfv * (1.0 - lbv) * (sf * (1.0 - sf))
            dlb_ref[:, ln] += jnp.sum(dfv * (1.0 - sf), axis=0, keepdims=True)
            if has_prev:
                dqr = dqr + pq_ref[:, ln]
                dv = dv + pv_ref[:, ln]
            dq_ref[:, ln] = dqr.astype(odt)
            df_ref[:, ln] = dfr.astype(odt)
            dv_ref[:, ln] = dv.astype(odt)

    hp, wd = HGRN_HP, 128 * HGRN_HP

    def col(c0):
        return pl.BlockSpec((TM, wd), lambda h, n: (bmap(n), c0 // hp + h))

    oblk = pl.BlockSpec((TM, wd), lambda h, n: (bmap(n), h))
    ins = [p, p, p, lb, sh]
    specs = [col(6), col(fcol), col(18), pl.BlockSpec((1, wd), lambda h, n: (0, h)),
             pl.BlockSpec((hp, nc, 128, 128), lambda h, n: (h, bmap(n), 0, 0))]
    if fused:
        osum, dmix, gain = head
        ins += [osum, p, dmix, gain]
        specs += [oblk, col(22), pl.BlockSpec((TM, wd), lambda h, n: (bmap(n), 4 // hp + h)),
                  pl.BlockSpec((1, 128), lambda h, n: (0, 0))]
    else:
        ins.append(do); specs.append(oblk)
    if has_prev:
        ins += list(prev); specs += [oblk, oblk]
    out_specs = [oblk, oblk, oblk, pl.BlockSpec((1, wd), lambda h, n: (0, h))]
    out_shape = [jax.ShapeDtypeStruct((t, 512), odt)] * 3 + [jax.ShapeDtypeStruct((1, 512), F32)]
    if fused:
        out_specs += [oblk, oblk, pl.BlockSpec((8, 128), lambda h, n: (0, 0))]
        out_shape += [jax.ShapeDtypeStruct((t, 512), F32), jax.ShapeDtypeStruct((t, 512), BF16),
                      jax.ShapeDtypeStruct((8, 128), F32)]
    return _pcall(body, name=name, grid=(4 // hp, nb), in_specs=specs, out_specs=out_specs, out_shape=out_shape,
                  scratch_shapes=[pltpu.VMEM((hp, 128, 128), F32)])(*ins)


def _rope256(x, cos, sin):
    x1, x2 = x[:, 0:128], x[:, 128:256]
    return jnp.concatenate([x1 * cos - x2 * sin, x2 * cos + x1 * sin], axis=-1)


def _rope256_t(d, cos, sin):
    d1, d2 = d[:, 0:128], d[:, 128:256]
    return jnp.concatenate([d1 * cos + d2 * sin, d2 * cos - d1 * sin], axis=-1)


RET_DK, RET_DV, RET_H = 256, 512, 4
RET_KSCALE = RET_DK ** -0.5
RCH = TM
RET_HP = 4


def _ret_terms(lg, rev):
    r, c = _iota((RCH, RCH), 0), _iota((RCH, RCH), 1)
    rel = ((c - r) if rev else (r - c)).astype(F32)
    dmat = jnp.where(rel >= 0, jnp.exp(lg[:, 0:1] * jnp.maximum(rel, 0.0)), 0.0)
    pos = _iota((RCH, 1), 0).astype(F32)
    cnt = (RCH - pos) if rev else (pos + 1.0)
    ei = jnp.exp(lg * cnt)
    eki = jnp.exp(lg * (RCH - cnt))
    eb = jnp.exp(lg * float(RCH))
    return dmat, ei, eki, eb


def _ret_fwd(p, lgt, cos, sin, *, rev, name, ofw=None):
    t = p.shape[0]
    nb, nc = t // TM, TM // RCH
    bmap = _blk_map(nb, rev, False)
    fused = ofw is not None

    def body(*refs):
        q_ref, k_ref, v_ref, lg_ref, c_ref, s_ref = refs[:6]
        if fused:
            ofw_ref, g_ref, o_ref, sh_ref, mix_ref, st = refs[6:]
        else:
            o_ref, sh_ref, st = refs[6:]

        @pl.when(pl.program_id(1) == 0)
        def _():
            st[...] = jnp.zeros_like(st)
        for hh in range(RET_HP):
            qc, vc = slice(RET_DK * hh, RET_DK * (hh + 1)), slice(RET_DV * hh, RET_DV * (hh + 1))
            dmat, ei, eki, eb = _ret_terms(lg_ref[hh], rev)
            for cc in _chunk_order(rev, False, nc):
                rows = slice(cc * RCH, (cc + 1) * RCH)
                cosv, sinv = c_ref[rows, :], s_ref[rows, :]
                q = _rope256(q_ref[rows, qc].astype(F32), cosv, sinv)
                k = _rope256(k_ref[rows, qc].astype(F32), cosv, sinv) * RET_KSCALE
                v = v_ref[rows, vc]
                s0 = st[hh]
                sh_ref[hh, cc] = s0.astype(BF16)
                a = _dot_nt(q, k) * dmat
                o = _dot(a, v) + _dot_nt(q * ei, s0)
                st[hh] = s0 * eb + _dot_tn(v, k * eki)
                if fused:
                    o = o + ofw_ref[rows, vc]
                    mix_ref[rows, vc] = _headnorm_apply(o, g_ref[rows, vc].astype(F32), None)
                o_ref[rows, vc] = o

    hp = RET_HP
    tab = pl.BlockSpec((TM, 128), lambda h, n: (bmap(n), 0))
    oblk = pl.BlockSpec((TM, hp * RET_DV), lambda h, n: (bmap(n), h))
    ins = [p, p, p, lgt, cos, sin]
    specs = [pl.BlockSpec((TM, hp * RET_DK), lambda h, n: (bmap(n), h)),
             pl.BlockSpec((TM, hp * RET_DK), lambda h, n: (bmap(n), RET_H // hp + h)),
             pl.BlockSpec((TM, hp * RET_DV), lambda h, n: (bmap(n), RET_H // hp + h)),
             pl.BlockSpec((hp, 1, RET_DK), lambda h, n: (h, 0, 0)), tab, tab]
    out_specs = [oblk, pl.BlockSpec((hp, nc, RET_DV, RET_DK), lambda h, n: (h, bmap(n), 0, 0))]
    out_shape = [jax.ShapeDtypeStruct((t, RET_H * RET_DV), F32),
                 jax.ShapeDtypeStruct((RET_H, t // RCH, RET_DV, RET_DK), BF16)]
    if fused:
        ins += [ofw, p]
        specs += [oblk, pl.BlockSpec((TM, hp * RET_DV), lambda h, n: (bmap(n), 2 * RET_H // hp + h))]
        out_specs.append(oblk)
        out_shape.append(jax.ShapeDtypeStruct((t, RET_H * RET_DV), BF16))
    return _pcall(body, name=name, grid=(RET_H // hp, nb), in_specs=specs, out_specs=out_specs, out_shape=out_shape,
                  scratch_shapes=[pltpu.VMEM((hp, RET_DV, RET_DK), F32)])(*ins)


def _ret_bwd(p, lgt, cos, sin, sh, do, prev, *, rev, name, head=None):
    t = p.shape[0]
    nb, nc = t // TM, TM // RCH
    bmap = _blk_map(nb, rev, True)
    has_prev = prev is not None
    odt = BF16 if has_prev else F32
    fused = head is not None

    def body(*refs):
        refs = list(refs)
        q_ref, k_ref, v_ref, lg_ref, c_ref, s_ref, sh_ref = refs[:7]
        if fused:
            osum_ref, g_ref, dmix_ref = refs[7:10]
            pos = 10
        else:
            do_ref = refs[7]
            pos = 8
        if has_prev:
            pq_ref, pk_ref, pv_ref = refs[pos:pos + 3]
            pos += 3
        dq_ref, dk_ref, dv_ref = refs[pos:pos + 3]
        pos += 3
        if fused:
            do_out, dg_ref = refs[pos:pos + 2]
            pos += 2
        dst = refs[pos]

        @pl.when(pl.program_id(1) == 0)
        def _():
            dst[...] = jnp.zeros_like(dst)

        for hh in range(RET_HP):
            qc, vc = slice(RET_DK * hh, RET_DK * (hh + 1)), slice(RET_DV * hh, RET_DV * (hh + 1))
            dmat, ei, eki, eb = _ret_terms(lg_ref[hh], rev)
            for cc in _chunk_order(rev, True, nc):
                rows = slice(cc * RCH, (cc + 1) * RCH)
                cosv, sinv = c_ref[rows, :], s_ref[rows, :]
                q = _rope256(q_ref[rows, qc].astype(F32), cosv, sinv)
                k = _rope256(k_ref[rows, qc].astype(F32), cosv, sinv) * RET_KSCALE
                v = v_ref[rows, vc]
                if fused:
                    dov, dg, _ = _headnorm_grad(osum_ref[rows, vc], g_ref[rows, vc].astype(F32), dmix_ref[rows, vc], None)
                    do_out[rows, vc] = dov
                    dg_ref[rows, vc] = dg
                else:
                    dov = do_ref[rows, vc]
                s0 = sh_ref[hh, cc]
                dsc = dst[hh]
                qi, ki = q * ei, k * eki
                a = _dot_nt(q, k) * dmat
                da = _dot_nt(dov, v) * dmat
                dv = _dot_tn(a, dov) + _dot_nt(ki, dsc)
                dqs = _dot(da, k) + _dot(dov, s0) * ei
                dks = _dot_tn(da, q) + _dot(v, dsc) * eki
                dst[hh] = dsc * eb + _dot_tn(dov, qi)
                dq = _rope256_t(dqs, cosv, sinv)
                dk = _rope256_t(dks * RET_KSCALE, cosv, sinv)
                if has_prev:
                    dq = dq + pq_ref[rows, qc]
                    dk = dk + pk_ref[rows, qc]
                    dv = dv + pv_ref[rows, vc]
                dq_ref[rows, qc] = dq.astype(odt)
                dk_ref[rows, qc] = dk.astype(odt)
                dv_ref[rows, vc] = dv.astype(odt)

    hp = RET_HP
    tab = pl.BlockSpec((TM, 128), lambda h, n: (bmap(n), 0))
    qblk = pl.BlockSpec((TM, hp * RET_DK), lambda h, n: (bmap(n), h))
    vblk = pl.BlockSpec((TM, hp * RET_DV), lambda h, n: (bmap(n), h))
    ins = [p, p, p, lgt, cos, sin, sh]
    specs = [qblk, pl.BlockSpec((TM, hp * RET_DK), lambda h, n: (bmap(n), RET_H // hp + h)),
             pl.BlockSpec((TM, hp * RET_DV), lambda h, n: (bmap(n), RET_H // hp + h)),
             pl.BlockSpec((hp, 1, RET_DK), lambda h, n: (h, 0, 0)), tab, tab,
             pl.BlockSpec((hp, nc, RET_DV, RET_DK), lambda h, n: (h, bmap(n), 0, 0))]
    if fused:
        osum, dmix = head
        ins += [osum, p, dmix]
        specs += [vblk, pl.BlockSpec((TM, hp * RET_DV), lambda h, n: (bmap(n), 2 * RET_H // hp + h)), vblk]
    else:
        ins.append(do); specs.append(vblk)
    if has_prev:
        ins += list(prev); specs += [qblk, qblk, vblk]
    out_specs = [qblk, qblk, vblk]
    out_shape = [jax.ShapeDtypeStruct((t, RET_H * RET_DK), odt), jax.ShapeDtypeStruct((t, RET_H * RET_DK), odt),
                 jax.ShapeDtypeStruct((t, RET_H * RET_DV), odt)]
    if fused:
        out_specs += [vblk, vblk]
        out_shape += [jax.ShapeDtypeStruct((t, RET_H * RET_DV), F32), jax.ShapeDtypeStruct((t, RET_H * RET_DV), BF16)]
    return _pcall(body, name=name, grid=(RET_H // hp, nb), in_specs=specs, out_specs=out_specs, out_shape=out_shape,
                  scratch_shapes=[pltpu.VMEM((hp, RET_DV, RET_DK), F32)])(*ins)


def _rope_tables(lc, l):
    tt = jnp.arange(l)
    row, colp = (tt // 64).astype(F32), (tt % 64).astype(F32)
    inv = 10000.0 ** (-jnp.arange(16, dtype=F32) / 16)
    ang = jnp.concatenate([row[:, None] * inv, colp[:, None] * inv], axis=-1)
    ang = jnp.concatenate([jnp.zeros((lc, 32), F32), ang], axis=0)
    acos, asin = jnp.tile(jnp.cos(ang), (1, 4)), jnp.tile(jnp.sin(ang), (1, 4))
    theta = 1.0 / (10000.0 ** jnp.linspace(0.0, 1.0, 128, dtype=F32))
    rang = jnp.arange(l, dtype=F32)[:, None] * theta
    rang = jnp.concatenate([jnp.zeros((lc, 128), F32), rang], axis=0)
    return acos, asin, jnp.cos(rang), jnp.sin(rang)


class _Weights:
    def __init__(self, w):
        self.w = w

    def first(self, after):
        return self.w

    def rest_landed(self, after):
        pass

    def rest(self, after):
        return self.w

    def send_grads(self, grp, grads):
        return jnp.zeros((8, 128), F32)


def _local_step(x0, target, mods, ng, wsrc, small):
    t, d = x0.shape
    l = target.shape[0]
    lc = t - l
    acos, asin, rcos, rsin = _rope_tables(lc, l)
    lg_fw = jnp.log(1.0 - 2.0 ** (-5.0 - jnp.arange(RET_H, dtype=F32)))
    lgt_fw = jnp.broadcast_to(lg_fw[:, None, None], (RET_H, 1, RET_DK))
    lgt_bw = jnp.broadcast_to(lg_fw[::-1][:, None, None], (RET_H, 1, RET_DK))
    gq, gk, sink, gain, lb = small['gq'], small['gk'], small['sink'], small['gain'], small['lb']

    (h1,) = _row_fwd(x0, mods, g=ng[0], shift=0, scale=1, name='l0_norm1')
    w = wsrc.first(h1)
    p0 = _mm_nn(h1, w['even_in'], name='l0_in')
    kp = _kprep_fwd(p0, gk, acos, asin, name='l0_kprep')
    att = _attn_fwd(p0, kp, gq, sink, acos, asin, lc=lc, name='l0_attn')
    hof, hsf = _hgrn_fwd(p0, lb, rev=False, name='l0_hgrn_f')
    wsrc.rest_landed(hof)
    hos, hsb, bmix = _hgrn_fwd(p0, lb, rev=True, name='l0_hgrn_b', ofw=hof, gain=gain)
    mix0 = jnp.concatenate([att, bmix], axis=1)
    y0 = _mm_nn(mix0, w['even_out'], name='l0_out')
    x1, h2 = _row_fwd(x0, mods, y=y0, gate=2, g=ng[1], shift=3, scale=4, name='l0_norm2')
    w = dict(w, **wsrc.rest(h2))
    u0, a0 = _ffn_in(h2, w['ffn_in'], lead=0, name='ffn_in')
    z0 = _mm_nn(a0, w['ffn_out'], lead=0, name='ffn_out')
    x2, h3 = _row_fwd(x1, mods, y=z0, gate=5, g=ng[2], shift=12, scale=13, name='l1_norm1')
    p1 = _mm_nn(h3, w['odd_in'], out_dtype=BF16, name='l1_in')
    rof, rsf = _ret_fwd(p1, lgt_fw, rcos, rsin, rev=False, name='l1_ret_f')
    ros, rsb, mix1 = _ret_fwd(p1, lgt_bw, rcos, rsin, rev=True, name='l1_ret_b', ofw=rof)
    y1 = _mm_nn(mix1, w['odd_out'], name='l1_out')
    x3, h4 = _row_fwd(x2, mods, y=y1, gate=14, g=ng[3], shift=15, scale=16, name='l1_norm2')
    u1, a1 = _ffn_in(h4, w['ffn_in'], lead=1, name='ffn_in')
    z1 = _mm_nn(a1, w['ffn_out'], lead=1, name='ffn_out')
    loss, dx4, dz1, s_fin = _row_final(x3, z1, mods, target, gate=17, name='loss')

    du1 = _ffn_dx(dz1, w['ffn_out'], u1, lead=1, name='ffn_out_dx')
    g_ffn_out1 = _mm_tn(a1, dz1, name='ffn_out_dw')
    dh4 = _mm_nt(du1, w['ffn_in'], lead=1, name='ffn_in_dx')
    g_ffn_in1 = _mm_tn(h4, du1, name='ffn_in_dw')
    dx3, dy1, s_l1n2 = _row_bwd(x3, dx4, dh4, mods, ng[3], shift=15, scale=16, y=y1, gate=14, name='l1_norm2_bwd')
    dmix1 = _mm_nt(dy1, w['odd_out'], name='l1_out_dx')
    g_odd_out = _mm_tn(mix1, dy1, name='l1_out_dw')
    rdq, rdk, rdv, rdo, rdg = _ret_bwd(p1, lgt_fw, rcos, rsin, rsf, None, None, rev=False, name='l1_ret_f_bwd',
                                       head=(ros, dmix1))
    rdq, rdk, rdv = _ret_bwd(p1, lgt_bw, rcos, rsin, rsb, rdo, (rdq, rdk, rdv), rev=True, name='l1_ret_b_bwd')
    dp1 = jnp.concatenate([rdq, rdk, rdv, rdg], axis=1)
    dh3 = _mm_nt(dp1, w['odd_in'], name='l1_in_dx')
    g_odd_in = _mm_tn(h3, dp1, name='l1_in_dw')
    mods = mods + wsrc.send_grads('early', dict(ffn_in1=g_ffn_in1, ffn_out1=g_ffn_out1, odd_in=g_odd_in,
                                                odd_out=g_odd_out))[0, 0]
    dx2, dz0, s_l1n1 = _row_bwd(x2, dx3, dh3, mods, ng[2], shift=12, scale=13, y=z0, gate=5, name='l1_norm1_bwd')
    du0 = _ffn_dx(dz0, w['ffn_out'], u0, lead=0, name='ffn_out_dx')
    g_ffn_out0 = _mm_tn(a0, dz0, name='ffn_out_dw')
    dh2 = _mm_nt(du0, w['ffn_in'], lead=0, name='ffn_in_dx')
    g_ffn_in0 = _mm_tn(h2, du0, name='ffn_in_dw')
    mods = mods + wsrc.send_grads('mid', dict(ffn_in0=g_ffn_in0, ffn_out0=g_ffn_out0))[0, 0]
    dx1, dy0, s_l0n2 = _row_bwd(x1, dx2, dh2, mods, ng[1], shift=3, scale=4, y=y0, gate=2, name='l0_norm2_bwd')
    dmix0 = _mm_nt(dy0, w['even_out'], name='l0_out_dx')
    g_even_out = _mm_tn(mix0, dy0, name='l0_out_dw')
    hq, hff, hv, dlb_f, hdo, hdg, s_gain = _hgrn_bwd(p0, lb, hsf, None, None, rev=False, name='l0_hgrn_f_bwd',
                                                     head=(hos, dmix0, gain))
    hq, hfb, hv, dlb_b = _hgrn_bwd(p0, lb, hsb, hdo, (hq, hv), rev=True, name='l0_hgrn_b_bwd')
    adq, dkp, adv, s_gq, s_sink = _attn_bwd(p0, kp, gq, sink, acos, asin, dmix0, lc=lc, name='l0_attn_bwd')
    dkv, s_gk = _kprep_bwd(p0, gk, acos, asin, dkp, adv, name='l0_kprep_bwd')
    dp0 = jnp.concatenate([adq, dkv, hq, _bf(hff), hfb, hv, hdg], axis=1)
    dh1 = _mm_nt(dp0, w['even_in'], name='l0_in_dx')
    g_even_in = _mm_tn(h1, dp0, name='l0_in_dw')
    dx0, s_l0n1 = _row_bwd(x0, dx1, dh1, mods, ng[0], shift=0, scale=1, latent_only=True, name='l0_norm1_bwd')

    grads = dict(ffn_in0=g_ffn_in0, ffn_in1=g_ffn_in1, ffn_out0=g_ffn_out0, ffn_out1=g_ffn_out1,
                 even_in=g_even_in, even_out=g_even_out, odd_in=g_odd_in, odd_out=g_odd_out)
    sums = dict(fin=s_fin, l1n2=s_l1n2, l1n1=s_l1n1, l0n2=s_l0n2, l0n1=s_l0n1, gain=s_gain, gq=s_gq, gk=s_gk,
                sink=s_sink, dlb_f=dlb_f, dlb_b=dlb_b)
    return loss, dx0, grads, sums


def _place():
    return lax.axis_index("x"), lax.axis_index("y"), lax.axis_index("c")


def _ag8(blk, *, name):
    r, c = blk.shape
    flips = [(dx, dy, dc) for dx in (0, 1) for dy in (0, 1) for dc in (0, 1) if (dx, dy, dc) != (0, 0, 0)]

    def body(x_ref, out_ref, send_sems, recv_sems, local_sem):
        ax, ay, ac = _place()
        me = 4 * ax + 2 * ay + ac
        mine = pltpu.make_async_copy(x_ref, out_ref.at[me], local_sem)
        mine.start()
        sent = []
        for k, (dx, dy, dc) in enumerate(flips):
            peer = (lax.rem(ax + dx, 2), lax.rem(ay + dy, 2), lax.rem(ac + dc, 2))
            cp = pltpu.make_async_remote_copy(src_ref=x_ref, dst_ref=out_ref.at[me], send_sem=send_sems.at[k],
                                              recv_sem=recv_sems.at[k], device_id=peer, device_id_type=MESH)
            cp.start()
            sent.append((cp, 4 * peer[0] + 2 * peer[1] + peer[2]))
        for k, (cp, pidx) in enumerate(sent):
            pltpu.make_async_remote_copy(src_ref=x_ref, dst_ref=out_ref.at[pidx], send_sem=send_sems.at[k],
                                         recv_sem=recv_sems.at[k], device_id=(ax, ay, ac),
                                         device_id_type=MESH).wait_recv()
        for cp, _ in sent:
            cp.wait_send()
        mine.wait()

    return _pcall(
        body, name=name,
        in_specs=[pl.BlockSpec(memory_space=pltpu.VMEM)],
        out_specs=pl.BlockSpec(memory_space=pltpu.VMEM),
        out_shape=jax.ShapeDtypeStruct((8, r, c), blk.dtype),
        scratch_shapes=[pltpu.SemaphoreType.DMA((7,)), pltpu.SemaphoreType.DMA((7,)), pltpu.SemaphoreType.DMA],
    )(blk)


_HBM = pl.BlockSpec(memory_space=pltpu.HBM)
_SEM = pl.BlockSpec(memory_space=pltpu.SEMAPHORE)
_DATAFLOW = pltpu.SideEffectType.DATAFLOW_SIDE_EFFECTING


def _split_start(bufs, plan, k, *, name):
    n = len(bufs)

    def body(*refs):
        ins, send_sems, recv_sems, token = refs[:n], refs[n], refs[n + 1], refs[2 * n + 2]
        for i, (src, dst, dev) in enumerate(plan(ins)):
            pltpu.make_async_remote_copy(src_ref=src, dst_ref=dst, send_sem=send_sems.at[i], recv_sem=recv_sems.at[i],
                                         device_id=dev, device_id_type=MESH).start()
        token[...] = jnp.zeros_like(token)

    res = _pcall(
        body, name=name,
        out_shape=(pltpu.SemaphoreType.DMA((k,)), pltpu.SemaphoreType.DMA((k,)),
                   *[pltpu.HBM(b.shape, b.dtype) for b in bufs], jax.ShapeDtypeStruct((8, 128), F32)),
        in_specs=[_HBM] * n, out_specs=(_SEM, _SEM, *[_HBM] * n, pl.BlockSpec(memory_space=pltpu.VMEM)),
        input_output_aliases={i: 2 + i for i in range(n)},
        compiler_params=pltpu.CompilerParams(has_side_effects=_DATAFLOW),
    )(*[pltpu.with_memory_space_constraint(b, pltpu.HBM) for b in bufs])
    return res[0], res[1], list(res[2:2 + n]), res[2 + n]


def _split_wait(bufs, send_sems, recv_sems, plan, after, *, name):
    n = len(bufs)

    def body(*refs):
        ins, ssem, rsem = refs[:n], refs[n], refs[n + 1]
        for i, (src, dst, dev) in enumerate(plan(ins)):
            cp = pltpu.make_async_remote_copy(src_ref=src, dst_ref=dst, send_sem=ssem.at[i], recv_sem=rsem.at[i],
                                              device_id=dev, device_id_type=MESH)
            cp.wait_send()
            cp.wait_recv()

    res = _pcall(
        body, name=name, out_shape=tuple(pltpu.HBM(b.shape, b.dtype) for b in bufs),
        in_specs=[_HBM] * n + [_SEM, _SEM, pl.BlockSpec(memory_space=pl.ANY)], out_specs=tuple([_HBM] * n),
        input_output_aliases={i: i for i in range(n)},
        compiler_params=pltpu.CompilerParams(has_side_effects=_DATAFLOW),
    )(*bufs, send_sems, recv_sems, after)
    return list(res)


_CHIP_FLIPS = [(1, 0), (0, 1), (1, 1)]


class _GatheredWeights:
    FIRST = ('even_in', 'even_out')
    REST = ('ffn_in', 'ffn_out', 'odd_in', 'odd_out')

    def __init__(self, shards, reducer):
        self.shards = shards
        self.send_grads = reducer.start
        self.ici = {}
        for grp, names in (('first', self.FIRST), ('rest', self.REST)):
            src = [shards[nm].reshape(2, shards[nm].shape[0] // 2, shards[nm].shape[1]) for nm in names]
            land = [lax.empty((4,) + a.shape, a.dtype) for a in src]
            m = len(names)
            sends, recvs, bufs, token = _split_start(src + land, functools.partial(self._ici_plan, m, True), 3 * m,
                                                     name='gather_' + grp + '_ici_start')
            self.ici[grp] = (sends, recvs, bufs, m)
            self.token = token if grp == 'first' else self.token + token
        self.rest_d2d = None

    @staticmethod
    def _ici_plan(m, sending, refs):
        ax, ay, ac = _place()
        s = 2 * ax + ay
        out = []
        for a in range(m):
            for dx, dy in _CHIP_FLIPS:
                px, py = lax.rem(ax + dx, 2), lax.rem(ay + dy, 2)
                slot = s if sending else 2 * px + py
                out.append((refs[a].at[ac], refs[m + a].at[slot, ac], (px, py, ac)))
        return out

    @staticmethod
    def _d2d_plan(m, sending, refs):
        ax, ay, ac = _place()
        out = []
        for a in range(m):
            for dx, dy in _CHIP_FLIPS:
                sp = 2 * lax.rem(ax + dx, 2) + lax.rem(ay + dy, 2)
                out.append((refs[a].at[sp, ac], refs[a].at[sp, ac if sending else 1 - ac], (ax, ay, 1 - ac)))
        return out

    def _landed(self, grp, after):
        sends, recvs, bufs, m = self.ici[grp]
        bufs = _split_wait(bufs, sends, recvs, functools.partial(self._ici_plan, m, False), after,
                           name='gather_' + grp + '_ici_wait')
        sends, recvs, land, _ = _split_start(bufs[m:], functools.partial(self._d2d_plan, m, True), 3 * m,
                                             name='gather_' + grp + '_d2d_start')
        return sends, recvs, land, m

    def _full(self, grp, names, d2d, after):
        sends, recvs, land, m = d2d
        land = _split_wait(land, sends, recvs, functools.partial(self._d2d_plan, m, False), after,
                           name='gather_' + grp + '_d2d_wait')
        s = 2 * lax.axis_index("x") + lax.axis_index("y")
        slot = lax.broadcasted_iota(jnp.int32, (4, 1, 1), 0)
        return {nm: _from_shards(nm, jnp.where(slot == s, self.shards[nm][None], g.reshape((4,) + self.shards[nm].shape)))
                for nm, g in zip(names, land)}

    def first(self, after):
        return self._full('first', self.FIRST, self._landed('first', after), after)

    def rest_landed(self, after):
        self.rest_d2d = self._landed('rest', after)

    def rest(self, after):
        return self._full('rest', self.REST, self.rest_d2d, after)


def _to_sibling(arrs, *, name):
    n = len(arrs)

    def body(*refs):
        ins, outs = refs[:n], refs[n:2 * n]
        send_sems, recv_sems = refs[2 * n:]
        ax, ay, ac = _place()
        cps = [pltpu.make_async_remote_copy(src_ref=ins[a], dst_ref=outs[a], send_sem=send_sems.at[a],
                                            recv_sem=recv_sems.at[a], device_id=(ax, ay, 1 - ac),
                                            device_id_type=MESH) for a in range(n)]
        for cp in cps:
            cp.start()
        for cp in cps:
            cp.wait_recv()
        for cp in cps:
            cp.wait_send()

    hbm = pl.BlockSpec(memory_space=pl.ANY)
    return _pcall(
        body, name=name, in_specs=[hbm] * n, out_specs=[hbm] * n,
        out_shape=[jax.ShapeDtypeStruct(a.shape, a.dtype) for a in arrs],
        scratch_shapes=[pltpu.SemaphoreType.DMA((n,))] * 2,
    )(*arrs)


def _mod_fwd(cond_raw, mw, mb, *, name):
    _, d, n = mw.shape

    def body(c_ref, w_ref, b_ref, o_ref):
        cv = c_ref[...]
        o_ref[...] = _dot(cv * _sigmoid(cv), w_ref[...]) + b_ref[...]

    return _pcall(
        body, name=name, grid=(2,),
        in_specs=[pl.BlockSpec((16, d), lambda l: (0, 0)), pl.BlockSpec((None, d, n), lambda l: (l, 0, 0)),
                  pl.BlockSpec((None, 1, n), lambda l: (l, 0, 0))],
        out_specs=pl.BlockSpec((None, 16, n), lambda l: (l, 0, 0)),
        out_shape=jax.ShapeDtypeStruct((2, 16, n), F32),
    )(cond_raw, mw, mb)


def _mod_bwd(cond_raw, dms, mw, *, name):
    _, d, n = mw.shape

    def body(c_ref, dm_ref, w_ref, gw_ref, dc_ref):
        @pl.when(pl.program_id(0) == 0)
        def _():
            dc_ref[...] = jnp.zeros_like(dc_ref)
        cv = c_ref[...]
        gw_ref[...] = _dot_tn(cv * _sigmoid(cv), dm_ref[...])
        dc_ref[...] += _dot_nt(dm_ref[...], w_ref[...])

    return _pcall(
        body, name=name, grid=(2,),
        in_specs=[pl.BlockSpec((16, d), lambda l: (0, 0)), pl.BlockSpec((None, 16, n), lambda l: (l, 0, 0)),
                  pl.BlockSpec((None, d, n), lambda l: (l, 0, 0))],
        out_specs=[pl.BlockSpec((None, d, n), lambda l: (l, 0, 0)), pl.BlockSpec((16, d), lambda l: (0, 0))],
        out_shape=[jax.ShapeDtypeStruct((2, d, n), F32), jax.ShapeDtypeStruct((16, d), F32)],
    )(cond_raw, dms, mw)


def _lb_fwd(hgrn_lb, *, name):
    def body(a_ref, o_ref):
        a0, a1 = a_ref[0:1, :], a_ref[1:2, :]
        m = jnp.maximum(a0, a1)
        e0, e1 = jnp.exp(a0 - m), jnp.exp(a1 - m)
        o_ref[...] = e0 / (e0 + e1)

    return _pcall(body, name=name, out_shape=jax.ShapeDtypeStruct((1, hgrn_lb.shape[1]), F32))(hgrn_lb)


PACK_TILES = ('l0n1', 'l0n2', 'l1n1', 'l1n2', 'fin', 'gq', 'gk', 'gain', 'dlb_f', 'dlb_b', 'sink')
PACK_ROW = {nm: 8 * i for i, nm in enumerate(PACK_TILES)}
MOD_SOURCE = ((('l0n1', 0), ('l0n1', 1), ('l0n2', 2), ('l0n2', 0), ('l0n2', 1), ('l1n1', 2)),
              (('l1n1', 0), ('l1n1', 1), ('l1n2', 2), ('l1n2', 0), ('l1n2', 1), ('fin', 2)))


def _small_finalize(gath, lb_pad, *, name):
    d = gath.shape[2]

    def body(g_ref, lb_ref, small_ref, glb_ref, gmb_ref, dm_ref):
        tot = g_ref[0]
        for e in range(1, 8):
            tot = tot + g_ref[e]

        def row(nm, r=0):
            return tot[PACK_ROW[nm] + r:PACK_ROW[nm] + r + 1, :]

        for k, nm in enumerate(('l0n1', 'l0n2', 'l1n1', 'l1n2')):
            small_ref[k:k + 1, :] = row(nm, 3) + row(nm, 7)
        for k, nm in ((4, 'gq'), (5, 'gk')):
            small_ref[k:k + 1, :] = row(nm) + pltpu.roll(row(nm), d - 64, 1)
        small_ref[6:7, :] = row('gain')
        small_ref[7:8, :] = row('sink')
        lbv = lb_ref[...]
        g0 = (row('dlb_f') + row('dlb_b')) * lbv * (1.0 - lbv)
        glb_ref[...] = jnp.zeros_like(glb_ref)
        glb_ref[0:1, :] = g0
        glb_ref[1:2, :] = -g0
        dm_ref[...] = jnp.zeros_like(dm_ref)
        for l in range(2):
            for part in range(6):
                nm, r = MOD_SOURCE[l][part]
                gmb_ref[l * 6 + part:l * 6 + part + 1, :] = row(nm, r) + row(nm, r + 4)
                rl = PACK_ROW[nm] + r + 4
                for e in range(8):
                    dm_ref[l, part, e:e + 1, :] = g_ref[e, rl:rl + 1, :]
                dm_ref[l, part, 8:9, :] = row(nm, r)

    return _pcall(
        body, name=name,
        out_shape=[jax.ShapeDtypeStruct((8, d), F32), jax.ShapeDtypeStruct((8, d), F32),
                   jax.ShapeDtypeStruct((12, d), F32), jax.ShapeDtypeStruct((2, 6, 16, d), F32)],
    )(gath, lb_pad)


def _cctx_grad(gath, c_ctx2, *, name):
    def body(g_ref, c_ref, o_ref):
        tot = ((g_ref[0, 0:1, :] + g_ref[2, 0:1, :]) + g_ref[4, 0:1, :]) + g_ref[6, 0:1, :]
        cv = c_ref[...]
        s = _sigmoid(cv)
        o_ref[...] = tot * (s * (1.0 + cv * (1.0 - s)))

    return _pcall(body, name=name, out_shape=jax.ShapeDtypeStruct(c_ctx2.shape, F32))(gath, c_ctx2)


def _row_block(r, c, limit=256 * 1024):
    best = None
    for br in range(16, r + 1, 16):
        if r % br == 0 and br * c <= limit:
            best = br
    return best if best is not None else r


def _sum4(own, landed, core, *, name):
    _, r, c = own.shape
    br = _row_block(r, c, 512 * 1024)

    def body(core_ref, own_ref, land_ref, o_ref):
        s = 2 * lax.axis_index("x") + lax.axis_index("y")
        p = [jnp.where(s == k, own_ref[k], land_ref[k]).astype(F32) for k in range(4)]
        o_ref[...] = ((p[0] + p[1]) + p[2]) + p[3]

    blk = pl.BlockSpec((4, br, c), lambda i, core_ref: (0, i, 0))
    spec = pltpu.PrefetchScalarGridSpec(
        num_scalar_prefetch=1, grid=(r // br,), in_specs=[blk, blk],
        out_specs=pl.BlockSpec((None, br, c), lambda i, core_ref: (core_ref[0], i, 0)))
    return _pcall(body, name=name, grid_spec=spec, out_shape=jax.ShapeDtypeStruct((2, r, c), F32))(core, own, landed)


def _exchange_halves(arrs, *, name):
    n = len(arrs)

    def body(*refs):
        ins, outs = refs[:n], refs[n:2 * n]
        send_sems, recv_sems = refs[2 * n:]
        ax, ay, ac = _place()
        cps = [pltpu.make_async_remote_copy(src_ref=ins[a].at[ac], dst_ref=outs[a].at[ac], send_sem=send_sems.at[a],
                                            recv_sem=recv_sems.at[a], device_id=(ax, ay, 1 - ac),
                                            device_id_type=MESH) for a in range(n)]
        for cp in cps:
            cp.start()
        for a in range(n):
            pltpu.make_async_remote_copy(src_ref=ins[a].at[ac], dst_ref=outs[a].at[1 - ac], send_sem=send_sems.at[a],
                                         recv_sem=recv_sems.at[a], device_id=(ax, ay, ac),
                                         device_id_type=MESH).wait_recv()
        for cp in cps:
            cp.wait_send()

    hbm = pl.BlockSpec(memory_space=pl.ANY)
    return _pcall(
        body, name=name, in_specs=[hbm] * n, out_specs=[hbm] * n,
        out_shape=[jax.ShapeDtypeStruct(a.shape, a.dtype) for a in arrs],
        input_output_aliases={a: a for a in range(n)},
        scratch_shapes=[pltpu.SemaphoreType.DMA((n,))] * 2,
    )(*arrs)


def _add2(a, b, *, name):
    r, c = a.shape
    br = _row_block(r, c, 1024 * 1024)

    def body(a_ref, b_ref, o_ref):
        o_ref[...] = (a_ref[...].astype(F32) + b_ref[...].astype(F32)).astype(BF16)

    blk = pl.BlockSpec((br, c), lambda i: (i, 0))
    return _pcall(body, name=name, grid=(r // br,), in_specs=[blk, blk], out_specs=blk,
                  out_shape=jax.ShapeDtypeStruct((r, c), BF16))(a, b)


def _adam(w, gs, m, v, *, name):
    r, c = w.shape
    br = _row_block(r, c)
    ng = len(gs)
    c1 = 1.0 - ADAM_B1 ** ADAM_STEP
    c2 = 1.0 - ADAM_B2 ** ADAM_STEP

    def body(*refs):
        w_ref, m_ref, v_ref = refs[0], refs[1 + ng], refs[2 + ng]
        outs = refs[3 + ng:]
        g = refs[1][...]
        for k in range(1, ng):
            g = g + refs[1 + k][...]
        mn = ADAM_B1 * m_ref[...] + (1.0 - ADAM_B1) * g
        vn = ADAM_B2 * v_ref[...] + (1.0 - ADAM_B2) * (g * g)
        if ng > 1:
            outs[0][...] = g
        d_out, m_out, v_out = outs[-3:]
        m_out[...] = mn
        v_out[...] = vn
        d_out[...] = -ADAM_LR * ((mn / c1) / (jnp.sqrt(vn / c2) + ADAM_EPS) + ADAM_WD * w_ref[...])

    blk = pl.BlockSpec((br, c), lambda i: (i, 0))
    nout = 4 if ng > 1 else 3
    res = _pcall(body, name=name, grid=(r // br,), in_specs=[blk] * (3 + ng), out_specs=[blk] * nout,
                 out_shape=[jax.ShapeDtypeStruct((r, c), F32)] * nout)(w, *gs, m, v)
    return list(res) if ng > 1 else [gs[0]] + list(res)


def _grad_halves(name, g, ac):
    if name.endswith('_in'):
        n = g.shape[1] // 4
        if name == 'ffn_in':
            assert n == FFN_BK
        order = _ffn_order(g.shape[1]) if name == 'ffn_in' else range(4)
        v = jnp.stack([g[:, b * n:(b + 1) * n] for b in order])
        per = [v[:, :g.shape[0] // 2], v[:, g.shape[0] // 2:]]
    else:
        k4, n = g.shape
        v = g.reshape(4, 2, k4 // 8, n)
        per = [v[:, 0], v[:, 1]]
    first = ac == 0
    return _bf(jnp.where(first, per[0], per[1])), _bf(jnp.where(first, per[1], per[0]))


class _GradReducer:
    def __init__(self):
        self.flight = {}

    @staticmethod
    def _plan(m, sending, refs):
        ax, ay, ac = _place()
        s = 2 * ax + ay
        out = []
        for a in range(m):
            for dx, dy in _CHIP_FLIPS:
                px, py = lax.rem(ax + dx, 2), lax.rem(ay + dy, 2)
                sp = 2 * px + py
                out.append((refs[a].at[sp], refs[m + a].at[s if sending else sp], (px, py, ac)))
        return out

    def start(self, grp, grads):
        ac = lax.axis_index("c")
        names = list(grads)
        halves = [_grad_halves(nm.rstrip('01'), grads[nm], ac) for nm in names]
        theirs = _to_sibling([h[1] for h in halves], name='swap_core_halves_' + grp)
        pair = [_add2(h[0].reshape(-1, b.shape[-1]), b.reshape(-1, b.shape[-1]), name='add_cores').reshape(b.shape)
                for h, b in zip(halves, theirs)]
        m = len(names)
        land = [lax.empty(a.shape, a.dtype) for a in pair]
        sends, recvs, bufs, token = _split_start(pair + land, functools.partial(self._plan, m, True), 3 * m,
                                                 name='scatter_' + grp + '_start')
        self.flight[grp] = (names, sends, recvs, bufs)
        return token

    def finish(self, grp, after):
        names, sends, recvs, bufs = self.flight.pop(grp)
        m = len(names)
        bufs = _split_wait(bufs, sends, recvs, functools.partial(self._plan, m, False), after,
                           name='scatter_' + grp + '_wait')
        core = lax.axis_index("c").astype(jnp.int32).reshape(1)
        sums = [_sum4(p, l, core, name='sum_chips') for p, l in zip(bufs[:m], bufs[m:])]
        both = _exchange_halves(sums, name='gather_core_halves_' + grp)
        return {nm: g.reshape(-1, g.shape[-1]) for nm, g in zip(names, both)}


def _from_shards(name, g):
    _, r, n = g.shape
    if name == 'ffn_in':
        assert n == FFN_BK
        v = g.reshape(4, 2, r // 2, n)
        return jnp.concatenate([v[b] for b in _ffn_order(4 * n)], axis=-1)
    if name == 'ffn_out':
        return g.reshape(4, 2, r // 2, n).transpose(1, 0, 2, 3).reshape(2, 2 * r, n)
    if name in ('even_in', 'odd_in'):
        return jnp.concatenate([g[b] for b in range(4)], axis=-1)
    return g.reshape(4 * r, n)


def kernel(x, c, ctx, c_ctx, mod_w, mod_b, norm_g, ffn_w_in, ffn_w_out, even_w_in, even_w_out, attn_qk_norm_g, attn_sink, hgrn_out_norm_g, hgrn_lb, odd_w_in, odd_w_out, loss_target, m_c_ctx, m_mod_w, m_mod_b, m_norm_g, m_ffn_w_in, m_ffn_w_out, m_even_w_in, m_even_w_out, m_attn_qk_norm_g, m_attn_sink, m_hgrn_out_norm_g, m_hgrn_lb, m_odd_w_in, m_odd_w_out, v_c_ctx, v_mod_w, v_mod_b, v_norm_g, v_ffn_w_in, v_ffn_w_out, v_even_w_in, v_even_w_out, v_attn_qk_norm_g, v_attn_sink, v_hgrn_out_norm_g, v_hgrn_lb, v_odd_w_in, v_odd_w_out):
    d = x.shape[-1]
    lc = ctx.shape[1]
    assert lc == TM and d == 1024
    ax, ay, ac = _place()
    s = 2 * ax + ay
    me = 4 * ax + 2 * ay + ac
    nmod = mod_w.shape[2]

    def pad8(v):
        return jnp.pad(v, ((0, 8 - v.shape[0]), (0, 0)))

    pack = jnp.concatenate([pad8(c), pad8(norm_g.reshape(1, d))], axis=0)
    g1 = _ag8(pack, name='gather_cond')
    c_all = g1[:, 0, :]
    ng = g1[0::2, 8, :].reshape(4, 2, 2, d // 4).transpose(1, 2, 0, 3).reshape(4, d)

    cond_raw = jnp.concatenate([c_all, pad8(c_ctx.reshape(1, d))], axis=0)
    mb_sh = lax.dynamic_slice_in_dim(mod_b, s * nmod, nmod, axis=1).reshape(2, 1, nmod)
    mpart = _mod_fwd(cond_raw, mod_w, mb_sh, name='mod_fwd')
    g3 = _ag8(mpart.reshape(32, nmod), name='gather_mods')
    mods_full = g3[0::2].reshape(4, 2, 16, nmod).transpose(1, 2, 0, 3).reshape(2, 16, 4 * nmod)
    m_lat = lax.dynamic_index_in_dim(mods_full, me, axis=1, keepdims=False)
    mods = jnp.stack([mods_full[:, 8], m_lat], axis=1).reshape(24, d)

    names = ['ffn_in', 'ffn_out', 'even_in', 'even_out', 'odd_in', 'odd_out']
    shards = [_bf(v.reshape(-1, v.shape[-1])) for v in (ffn_w_in, ffn_w_out, even_w_in, even_w_out, odd_w_in, odd_w_out)]
    shards, mods = lax.optimization_barrier((shards, mods))
    reducer = _GradReducer()
    wsrc = _GatheredWeights(dict(zip(names, shards)), reducer)

    lb = _lb_fwd(hgrn_lb, name='hgrn_lower_bound')
    small = dict(gq=jnp.tile(attn_qk_norm_g[0, 0], 2).reshape(1, 128), gk=jnp.tile(attn_qk_norm_g[0, 1], 2).reshape(1, 128),
                 sink=attn_sink[0], gain=hgrn_out_norm_g, lb=lb)
    x0 = jnp.concatenate([ctx[0], x[0]], axis=0) + wsrc.token[0, 0]
    loss_t, dx0, grads, sums = _local_step(x0, loss_target[0], mods, ng, wsrc, small)
    loss = lax.psum(loss_t[0, 0], ("x", "y", "c"))
    grad_x = dx0[None]

    def tile(v):
        return jnp.pad(v, ((0, 8 - v.shape[0]), (0, d - v.shape[1])))

    sums = dict(sums, sink=sums['sink'][:, 0].reshape(1, 8))
    g4 = _ag8(jnp.concatenate([tile(sums[nm]) for nm in PACK_TILES], axis=0), name='gather_row_sums')
    small_g, glb, gmb, dmat = _small_finalize(g4, tile(lb)[0:1], name='small_grads')
    dms = lax.dynamic_slice_in_dim(dmat.transpose(0, 2, 1, 3).reshape(2, 16, 6 * d), s * nmod, nmod, axis=2)
    g_mod_w, dcond = _mod_bwd(cond_raw, dms, mod_w, name='mod_bwd')
    g5 = _ag8(dcond[8:16], name='gather_dcond')
    g_c_ctx = _cctx_grad(g5, c_ctx.reshape(8, d // 8).reshape(1, d), name='c_ctx_grad')

    late = {nm: grads[nm] for nm in ('even_in', 'even_out')}
    late, g_c_ctx = lax.optimization_barrier((late, g_c_ctx))
    token = reducer.start('late', late)
    full = reducer.finish('early', token)

    def upd(wv, gs, mv, vv, name):
        shp = wv.shape
        c2 = shp[-1]
        out = _adam(wv.reshape(-1, c2), [g.reshape(-1, c2) for g in gs], mv.reshape(-1, c2), vv.reshape(-1, c2), name=name)
        return [o.reshape(shp) for o in out]

    res = {}
    res['c_ctx'] = upd(c_ctx.reshape(8, d // 8), [g_c_ctx.reshape(8, d // 8)], m_c_ctx.reshape(8, d // 8), v_c_ctx.reshape(8, d // 8), 'adam_c_ctx')
    res['c_ctx'] = [o.reshape(d) for o in res['c_ctx']]
    res['mod_w'] = upd(mod_w, [g_mod_w], m_mod_w, v_mod_w, 'adam_mod_w')
    res['mod_b'] = upd(mod_b, [gmb.reshape(2, 6 * d)], m_mod_b, v_mod_b, 'adam_mod_b')
    g_ng = lax.dynamic_slice_in_dim(small_g[0:4].reshape(2, 2, d), s * (d // 4), d // 4, axis=2)
    res['norm_g'] = upd(norm_g, [g_ng], m_norm_g, v_norm_g, 'adam_norm_g')
    g_qk = jnp.stack([small_g[4, 0:64], small_g[5, 0:64]]).reshape(1, 2, 64)
    res['attn_qk_norm_g'] = upd(attn_qk_norm_g, [g_qk], m_attn_qk_norm_g, v_attn_qk_norm_g, 'adam_qk_gain')
    res['attn_sink'] = upd(attn_sink, [small_g[7, 0:8].reshape(1, 8)], m_attn_sink, v_attn_sink, 'adam_sink')
    res['hgrn_out_norm_g'] = upd(hgrn_out_norm_g, [small_g[6, 0:128].reshape(1, 128)], m_hgrn_out_norm_g, v_hgrn_out_norm_g, 'adam_head_gain')
    res['hgrn_lb'] = upd(hgrn_lb, [glb[0:2, 0:hgrn_lb.shape[1]]], m_hgrn_lb, v_hgrn_lb, 'adam_hgrn_lb')
    res['odd_w_in'] = upd(odd_w_in, [full['odd_in']], m_odd_w_in, v_odd_w_in, 'adam_odd_in')
    res['odd_w_out'] = upd(odd_w_out, [full['odd_out']], m_odd_w_out, v_odd_w_out, 'adam_odd_out')
    full.update(reducer.finish('mid', res['odd_w_in'][1]))
    g_ffn_in = jnp.concatenate([full['ffn_in0'], full['ffn_in1']], axis=0)
    g_ffn_out = jnp.concatenate([full['ffn_out0'], full['ffn_out1']], axis=0)
    res['ffn_w_in'] = upd(ffn_w_in, [g_ffn_in], m_ffn_w_in, v_ffn_w_in, 'adam_ffn_in')
    res['ffn_w_out'] = upd(ffn_w_out, [g_ffn_out], m_ffn_w_out, v_ffn_w_out, 'adam_ffn_out')
    full.update(reducer.finish('late', res['ffn_w_in'][1]))
    res['even_w_in'] = upd(even_w_in, [full['even_in']], m_even_w_in, v_even_w_in, 'adam_even_in')
    res['even_w_out'] = upd(even_w_out, [full['even_out']], m_even_w_out, v_even_w_out, 'adam_even_out')

    order = ['c_ctx', 'mod_w', 'mod_b', 'norm_g', 'ffn_w_in', 'ffn_w_out', 'even_w_in', 'even_w_out',
             'attn_qk_norm_g', 'attn_sink', 'hgrn_out_norm_g', 'hgrn_lb', 'odd_w_in', 'odd_w_out']
    outs = [loss, grad_x]
    for k in range(4):
        outs += [res[nm][k] for nm in order]
    return tuple(outs)
```

```python
import functools
import math

import numpy as np
import jax
import jax.numpy as jnp
from jax import lax
from jax.experimental import pallas as pl
from jax.experimental.pallas import tpu as pltpu

F32 = jnp.float32
BF16 = jnp.bfloat16
EPS = 1e-6
TM = 256
CHUNK = 64
QB = 256
WINDOW = 128
NEG = -1e30
MESH = pl.DeviceIdType.MESH

ADAM_LR, ADAM_B1, ADAM_B2, ADAM_EPS, ADAM_WD, ADAM_STEP = 0.001, 0.9, 0.999, 1e-08, 0.01, 10


def _pcall(body, **kw):
    return pl.pallas_call(body, **kw)


def _pick(n, cap):
    best = None
    for m in range(128, min(n, cap) + 1, 128):
        if n % m == 0:
            best = m
    assert best is not None, (n, cap)
    return best


def _bf(x):
    return x.astype(BF16)


def _dot(a, b):
    return jnp.dot(_bf(a), _bf(b), preferred_element_type=F32)


def _dot_nt(a, b):
    return lax.dot_general(_bf(a), _bf(b), (((1,), (1,)), ((), ())), preferred_element_type=F32)


def _dot_tn(a, b):
    return lax.dot_general(_bf(a), _bf(b), (((0,), (0,)), ((), ())), preferred_element_type=F32)


def _dot_exact(a, b):
    return jnp.dot(a, b, preferred_element_type=F32, precision=lax.Precision.HIGHEST)


def _sigmoid(x):
    return 1.0 / (1.0 + jnp.exp(-x))


def _iota(shape, dim):
    return lax.broadcasted_iota(jnp.int32, shape, dim)


def _mm_nn(a, b, *, lead=None, out_dtype=F32, name):
    m, k = a.shape
    n = b.shape[-1]
    bm = 1408 if (m % 1408 == 0 and k <= 1024) else (768 if m % 768 == 0 else TM)
    bn = _pick(n, 1024) if n % 512 == 0 else _pick(n, 1664)

    def body(a_ref, b_ref, o_ref):
        o_ref[...] = _dot(a_ref[...], b_ref[...]).astype(o_ref.dtype)

    if lead is None:
        b_spec = pl.BlockSpec((k, bn), lambda i, j: (0, j))
    else:
        b_spec = pl.BlockSpec((None, k, bn), lambda i, j: (lead, 0, j))
    return _pcall(
        body, name=name, grid=(m // bm, n // bn),
        in_specs=[pl.BlockSpec((bm, k), lambda i, j: (i, 0)), b_spec],
        out_specs=pl.BlockSpec((bm, bn), lambda i, j: (i, j)),
        out_shape=jax.ShapeDtypeStruct((m, n), out_dtype),
    )(a, b)


def _mm_nt(a, b, *, lead=None, name):
    m, n = a.shape
    k = b.shape[-2]
    bm = 768 if m % 768 == 0 else TM
    bk = _pick(k, 512)

    def body(a_ref, b_ref, o_ref):
        o_ref[...] = _dot_nt(a_ref[...], b_ref[...])

    if lead is None:
        b_spec = pl.BlockSpec((bk, n), lambda i, j: (j, 0))
    else:
        b_spec = pl.BlockSpec((None, bk, n), lambda i, j: (lead, j, 0))
    return _pcall(
        body, name=name, grid=(m // bm, k // bk),
        in_specs=[pl.BlockSpec((bm, n), lambda i, j: (i, 0)), b_spec],
        out_specs=pl.BlockSpec((bm, bk), lambda i, j: (i, j)),
        out_shape=jax.ShapeDtypeStruct((m, k), F32),
    )(a, b)


def _mm_tn(a, b, *, name):
    t, k = a.shape
    n = b.shape[1]
    bt = 1408 if t % 1408 == 0 else (768 if t % 768 == 0 else TM)
    bk = _pick(k, 1536)
    bn = _pick(n, 1024) if n % 1024 == 0 or n < 1664 else _pick(n, 1664)

    def body(a_ref, b_ref, o_ref):
        @pl.when(pl.program_id(2) == 0)
        def _():
            o_ref[...] = jnp.zeros_like(o_ref)
        o_ref[...] += _dot_tn(a_ref[...], b_ref[...])

    return _pcall(
        body, name=name, grid=(k // bk, n // bn, t // bt),
        in_specs=[pl.BlockSpec((bt, bk), lambda i, j, s: (s, i)),
                  pl.BlockSpec((bt, bn), lambda i, j, s: (s, j))],
        out_specs=pl.BlockSpec((bk, bn), lambda i, j, s: (i, j)),
        out_shape=jax.ShapeDtypeStruct((k, n), F32),
    )(a, b)


def _mod_row(mods_ref, lat, idx):
    return jnp.where(lat, mods_ref[idx + 6:idx + 7, :], mods_ref[idx:idx + 1, :])


def _row_step(t):
    return 768 if t % 768 == 0 else TM


def _row_fwd(x, mods, *, y=None, gate=None, g=None, shift=None, scale=None, name):
    t, d = x.shape
    has_y, has_n = y is not None, g is not None
    rt = _row_step(t)

    def body(*refs):
        refs = list(refs)
        x_ref, mods_ref = refs[0], refs[1]
        pos = 2
        if has_y:
            y_ref = refs[pos]; pos += 1
        if has_n:
            g_ref = refs[pos]; pos += 1
        outs = refs[pos:]
        for sub in range(rt // TM):
            rows = slice(sub * TM, (sub + 1) * TM)
            lat = pl.program_id(0) * (rt // TM) + sub > 0
            x1 = x_ref[rows, :]
            o = 0
            if has_y:
                x1 = x1 + _mod_row(mods_ref, lat, gate) * y_ref[rows, :]
                outs[o][rows, :] = x1; o += 1
            if has_n:
                rs = lax.rsqrt(jnp.mean(x1 * x1, axis=-1, keepdims=True) + EPS)
                hn = x1 * rs * g_ref[...]
                h = hn * (1.0 + _mod_row(mods_ref, lat, scale)) + _mod_row(mods_ref, lat, shift)
                outs[o][rows, :] = h.astype(BF16)

    row = pl.BlockSpec((rt, d), lambda i: (i, 0))
    ins, specs = [x, mods], [row, pl.BlockSpec(mods.shape, lambda i: (0, 0))]
    if has_y:
        ins.append(y); specs.append(row)
    if has_n:
        ins.append(g.reshape(1, d)); specs.append(pl.BlockSpec((1, d), lambda i: (0, 0)))
    out_shape, out_specs = [], []
    if has_y:
        out_shape.append(jax.ShapeDtypeStruct((t, d), F32)); out_specs.append(row)
    if has_n:
        out_shape.append(jax.ShapeDtypeStruct((t, d), BF16)); out_specs.append(row)
    res = _pcall(body, name=name, grid=(t // rt,), in_specs=specs, out_specs=out_specs,
                 out_shape=out_shape)(*ins)
    return res


def _acc_row(ref, r, val):
    ref[r:r + 1, :] += val


def _row_final(x, z, mods, target, *, gate, name):
    t, d = x.shape

    def body(x_ref, mods_ref, z_ref, t_ref, loss_ref, dx_ref, dz_ref, sums_ref):
        i = pl.program_id(0)
        lat = i > 0

        @pl.when(i == 0)
        def _():
            loss_ref[...] = jnp.zeros_like(loss_ref)
            sums_ref[...] = jnp.zeros_like(sums_ref)

        gt = _mod_row(mods_ref, lat, gate)
        zz = z_ref[...]
        yv = x_ref[...] + gt * zz
        keep = jnp.where(lat, 1.0, 0.0).astype(F32)
        diff = (yv - t_ref[...]) * keep
        part = jnp.sum(jnp.sum(diff * diff, axis=0, keepdims=True), axis=1, keepdims=True)
        loss_ref[...] += part * (0.5 / d)
        dy = diff * (1.0 / d)
        dx_ref[...] = dy
        dz_ref[...] = (gt * dy).astype(BF16)
        _acc_row(sums_ref, 6, jnp.sum(dy * zz, axis=0, keepdims=True))

    row = pl.BlockSpec((TM, d), lambda i: (i, 0))
    return _pcall(
        body, name=name, grid=(t // TM,),
        in_specs=[row, pl.BlockSpec(mods.shape, lambda i: (0, 0)), row,
                  pl.BlockSpec((TM, d), lambda i: (jnp.maximum(i - 1, 0), 0))],
        out_specs=[pl.BlockSpec((8, 128), lambda i: (0, 0)), row, row,
                   pl.BlockSpec((8, d), lambda i: (0, 0))],
        out_shape=[jax.ShapeDtypeStruct((8, 128), F32), jax.ShapeDtypeStruct((t, d), F32),
                   jax.ShapeDtypeStruct((t, d), BF16), jax.ShapeDtypeStruct((8, d), F32)],
    )(x, mods, z, target)


def _row_bwd(xn, dxo, dh, mods, g, *, shift, scale, y=None, gate=None, latent_only=False, name):
    t, d = xn.shape
    has_y = y is not None

    def body(*refs):
        refs = list(refs)
        x_ref, dxo_ref, dh_ref, mods_ref, g_ref = refs[:5]
        pos = 5
        if has_y:
            y_ref = refs[pos]; pos += 1
        dx_ref = refs[pos]; pos += 1
        if has_y:
            dy_ref = refs[pos]; pos += 1
        sums_ref = refs[pos]
        i = pl.program_id(0)

        @pl.when(i == 0)
        def _():
            sums_ref[...] = jnp.zeros_like(sums_ref)

        def add_sums(vals, base):
            for r, v in enumerate(vals):
                if v is not None:
                    _acc_row(sums_ref, base + r, v)

        gv = g_ref[...]
        for sub in range(rt // TM):
            rows = slice(sub * TM, (sub + 1) * TM)
            lat = i * (rt // TM) + sub > 0
            x1 = x_ref[rows, :]
            rs = lax.rsqrt(jnp.mean(x1 * x1, axis=-1, keepdims=True) + EPS)
            xh = x1 * rs
            dhv = dh_ref[rows, :]
            dn = dhv * (1.0 + _mod_row(mods_ref, lat, scale))
            dxh = dn * gv
            dx = dxo_ref[rows, :] + rs * (dxh - xh * jnp.mean(dxh * xh, axis=-1, keepdims=True))
            dx_ref[rows, :] = dx
            vals = [jnp.sum(dhv, axis=0, keepdims=True),
                    jnp.sum(dhv * (xh * gv), axis=0, keepdims=True),
                    None,
                    jnp.sum(dn * xh, axis=0, keepdims=True)]
            if has_y:
                dy_ref[rows, :] = (_mod_row(mods_ref, lat, gate) * dx).astype(BF16)
                vals[2] = jnp.sum(dx * y_ref[rows, :], axis=0, keepdims=True)
            if sub == 0:
                pl.when(i == 0)(functools.partial(add_sums, vals, 0))
                pl.when(i > 0)(functools.partial(add_sums, vals, 4))
            else:
                add_sums(vals, 4)

    rt = TM if latent_only else _row_step(t)
    row = pl.BlockSpec((rt, d), lambda i: (i, 0))
    ins = [xn, dxo, dh, mods, g.reshape(1, d)]
    specs = [row, row, row, pl.BlockSpec(mods.shape, lambda i: (0, 0)), pl.BlockSpec((1, d), lambda i: (0, 0))]
    if latent_only:
        out_shape = [jax.ShapeDtypeStruct((t - TM, d), F32)]
        out_specs = [pl.BlockSpec((TM, d), lambda i: (jnp.maximum(i - 1, 0), 0))]
    else:
        out_shape, out_specs = [jax.ShapeDtypeStruct((t, d), F32)], [row]
    if has_y:
        ins.append(y); specs.append(row)
        out_shape.append(jax.ShapeDtypeStruct((t, d), BF16)); out_specs.append(row)
    out_shape.append(jax.ShapeDtypeStruct((8, d), F32))
    out_specs.append(pl.BlockSpec((8, d), lambda i: (0, 0)))
    return _pcall(body, name=name, grid=(t // rt,), in_specs=specs, out_specs=out_specs,
                  out_shape=out_shape)(*ins)


FFN_BK = 1408


FFN_SUB = 256


def _ffn_order(n2):
    nb = n2 // (2 * FFN_BK)
    return [h * nb + j for j in range(nb) for h in (0, 1)]


def _ffn_interleave(w):
    return jnp.concatenate([w[..., b * FFN_BK:(b + 1) * FFN_BK] for b in _ffn_order(w.shape[-1])], axis=-1)


def _ffn_deinterleave(w):
    order = _ffn_order(w.shape[-1])
    return jnp.concatenate([w[..., order.index(b) * FFN_BK:(order.index(b) + 1) * FFN_BK]
                            for b in range(len(order))], axis=-1)


def _big_tile(t):
    return 768 if t % 768 == 0 else TM


def _ffn_in(h, w, *, lead, name):
    t, d = h.shape
    n2 = w.shape[-1]
    bm, bk = _big_tile(t), FFN_BK

    def body(h_ref, w_ref, u_ref, a_ref):
        hb = h_ref[...]
        for c0 in range(0, bk, FFN_SUB):
            c1 = min(c0 + FFN_SUB, bk)
            ug = _dot(hb, w_ref[:, c0:c1]).astype(BF16)
            uu = _dot(hb, w_ref[:, bk + c0:bk + c1]).astype(BF16)
            u_ref[:, c0:c1] = ug
            u_ref[:, bk + c0:bk + c1] = uu
            gv, up = ug.astype(F32), uu.astype(F32)
            a_ref[:, c0:c1] = (gv * _sigmoid(gv) * up).astype(BF16)

    return _pcall(
        body, name=name, grid=(t // bm, n2 // (2 * bk)),
        in_specs=[pl.BlockSpec((bm, d), lambda i, j: (i, 0)),
                  pl.BlockSpec((None, d, 2 * bk), lambda i, j: (lead, 0, j))],
        out_specs=[pl.BlockSpec((bm, 2 * bk), lambda i, j: (i, j)), pl.BlockSpec((bm, bk), lambda i, j: (i, j))],
        out_shape=[jax.ShapeDtypeStruct((t, n2), BF16), jax.ShapeDtypeStruct((t, n2 // 2), BF16)],
    )(h, w)


def _ffn_dx(dz, w_out, u, *, lead, name):
    t, d = dz.shape
    n2 = u.shape[1]
    bm, bk = _big_tile(t), FFN_BK

    def body(dz_ref, w_ref, u_ref, du_ref):
        dzb = dz_ref[...]
        for c0 in range(0, bk, FFN_SUB):
            c1 = min(c0 + FFN_SUB, bk)
            da = _dot_nt(dzb, w_ref[c0:c1, :])
            gv, up = u_ref[:, c0:c1].astype(F32), u_ref[:, bk + c0:bk + c1].astype(F32)
            s = _sigmoid(gv)
            du_ref[:, c0:c1] = (da * up * (s * (1.0 + gv * (1.0 - s)))).astype(BF16)
            du_ref[:, bk + c0:bk + c1] = (da * gv * s).astype(BF16)

    ublk = pl.BlockSpec((bm, 2 * bk), lambda i, j: (i, j))
    return _pcall(
        body, name=name, grid=(t // bm, n2 // (2 * bk)),
        in_specs=[pl.BlockSpec((bm, d), lambda i, j: (i, 0)),
                  pl.BlockSpec((None, bk, d), lambda i, j: (lead, j, 0)), ublk],
        out_specs=ublk, out_shape=jax.ShapeDtypeStruct((t, n2), BF16),
    )(dz, w_out, u)


def _lane(shape):
    return _iota(shape, len(shape) - 1)


def _pair_norm(x, g):
    lo = _lane(x.shape) < 64
    x2 = x * x
    s_lo = jnp.sum(jnp.where(lo, x2, 0.0), axis=-1, keepdims=True)
    s_hi = jnp.sum(jnp.where(lo, 0.0, x2), axis=-1, keepdims=True)
    rs = lax.rsqrt(jnp.where(lo, s_lo, s_hi) * (1.0 / 64) + EPS)
    return x * rs, rs


def _pair_mean(v):
    lo = _lane(v.shape) < 64
    s_lo = jnp.sum(jnp.where(lo, v, 0.0), axis=-1, keepdims=True)
    s_hi = jnp.sum(jnp.where(lo, 0.0, v), axis=-1, keepdims=True)
    return jnp.where(lo, s_lo, s_hi) * (1.0 / 64)


def _rot64(x):
    r1 = pltpu.roll(x, 32, 1)
    r2 = pltpu.roll(x, 96, 1)
    even = ((_lane(x.shape) >> 5) & 1) == 0
    return jnp.where(even, -r2, r1)


def _rope64(x, cos, sin):
    return x * cos + _rot64(x) * sin


def _rope64_t(d, cos, sin):
    return d * cos - _rot64(d * sin)


def _kprep_fwd(p, gk, cos, sin, *, name):
    t = p.shape[0]

    def body(k_ref, g_ref, c_ref, s_ref, o_ref):
        xh, _ = _pair_norm(k_ref[...], None)
        o_ref[...] = _rope64(xh * g_ref[...], c_ref[...], s_ref[...])

    blk = pl.BlockSpec((TM, 128), lambda i: (i, 0))
    return _pcall(
        body, name=name, grid=(t // TM,),
        in_specs=[pl.BlockSpec((TM, 128), lambda i: (i, 4)), pl.BlockSpec((1, 128), lambda i: (0, 0)), blk, blk],
        out_specs=blk, out_shape=jax.ShapeDtypeStruct((t, 128), F32),
    )(p, gk, cos, sin)


def _kprep_bwd(p, gk, cos, sin, dkp, dv, *, name):
    t = p.shape[0]

    def body(k_ref, g_ref, c_ref, s_ref, dkp_ref, dv_ref, o_ref, dg_ref):
        @pl.when(pl.program_id(0) == 0)
        def _():
            dg_ref[...] = jnp.zeros_like(dg_ref)
        xh, rs = _pair_norm(k_ref[...], None)
        dn = _rope64_t(dkp_ref[...], c_ref[...], s_ref[...])
        _acc_row(dg_ref, 0, jnp.sum(dn * xh, axis=0, keepdims=True))
        dxh = dn * g_ref[...]
        o_ref[:, 0:128] = (rs * (dxh - xh * _pair_mean(dxh * xh))).astype(BF16)
        o_ref[:, 128:256] = dv_ref[...].astype(BF16)

    blk = pl.BlockSpec((TM, 128), lambda i: (i, 0))
    return _pcall(
        body, name=name, grid=(t // TM,),
        in_specs=[pl.BlockSpec((TM, 128), lambda i: (i, 4)), pl.BlockSpec((1, 128), lambda i: (0, 0)), blk, blk, blk, blk],
        out_specs=[pl.BlockSpec((TM, 256), lambda i: (i, 0)), pl.BlockSpec((8, 128), lambda i: (0, 0))],
        out_shape=[jax.ShapeDtypeStruct((t, 256), BF16), jax.ShapeDtypeStruct((8, 128), F32)],
    )(p, gk, cos, sin, dkp, dv)


def _attn_common(i, t, lc, kp_ref, v_ref):
    span = QB + 2 * WINDOW
    start = pl.multiple_of(jnp.clip(i * QB - WINDOW, lc, t - span), WINDOW)
    kall = jnp.concatenate([kp_ref[0:lc, :], kp_ref[pl.ds(start, span), :]], axis=0)
    vall = jnp.concatenate([v_ref[0:lc, :], v_ref[pl.ds(start, span), :]], axis=0)
    nk = lc + span
    col = _iota((QB, nk), 1)
    krow = jnp.where(col < lc, col, start + col - lc)
    qrow = i * QB + _iota((QB, nk), 0)
    valid = (col < lc) | ((qrow >= lc) & (krow >= lc) & (jnp.abs(krow - qrow) <= WINDOW))
    lo = _lane(kall.shape) < 64
    kroll, vroll = pltpu.roll(kall, 64, 1), pltpu.roll(vall, 64, 1)
    zero = jnp.zeros_like(kall)
    kvar = [[_bf(jnp.where(lo, kall, zero)), _bf(jnp.where(lo, zero, kroll))],
            [_bf(jnp.where(lo, kroll, zero)), _bf(jnp.where(lo, zero, kall))]]
    vvar = [[_bf(jnp.where(lo, vall, zero)), _bf(jnp.where(lo, zero, vroll))],
            [_bf(jnp.where(lo, vroll, zero)), _bf(jnp.where(lo, zero, vall))]]
    return start, valid, kvar, vvar


def _softmax_sink(s, valid, snk):
    s = jnp.where(valid, s, NEG)
    m = jnp.maximum(jnp.max(s, axis=-1, keepdims=True), snk)
    e = jnp.exp(s - m)
    es = jnp.exp(snk - m)
    inv = 1.0 / (jnp.sum(e, axis=-1, keepdims=True) + es)
    return e * inv, es * inv


def _attn_fwd(p, kp, gq, sink, cos, sin, *, lc, name):
    t = p.shape[0]
    scale = 64 ** -0.5

    def body(q_ref, kp_ref, v_ref, g_ref, sink_ref, c_ref, s_ref, o_ref):
        i = pl.program_id(0)
        _, valid, kvar, vvar = _attn_common(i, t, lc, kp_ref, v_ref)
        cosv, sinv, gv = c_ref[...], s_ref[...], g_ref[...]
        for j in range(4):
            xh, _ = _pair_norm(q_ref[:, 128 * j:128 * j + 128], None)
            q2 = _bf(_rope64(xh * gv, cosv, sinv) * scale)
            acc = jnp.zeros((QB, 128), F32)
            for half in range(2):
                s = _dot_nt(q2, kvar[j // 2][half])
                pr, _ = _softmax_sink(s, valid, sink_ref[2 * j + half])
                acc = acc + _dot(pr, vvar[j // 2][half])
            o_ref[:, 128 * j:128 * j + 128] = acc.astype(BF16)

    qblk = pl.BlockSpec((QB, 128), lambda i: (i, 0))
    return _pcall(
        body, name=name, grid=(t // QB,),
        in_specs=[pl.BlockSpec((QB, 512), lambda i: (i, 0)),
                  pl.BlockSpec((t, 128), lambda i: (0, 0)),
                  pl.BlockSpec((t, 128), lambda i: (0, 5)),
                  pl.BlockSpec((1, 128), lambda i: (0, 0)),
                  pl.BlockSpec(memory_space=pltpu.SMEM), qblk, qblk],
        out_specs=pl.BlockSpec((QB, 512), lambda i: (i, 0)),
        out_shape=jax.ShapeDtypeStruct((t, 512), BF16),
    )(p, kp, p, gq, sink, cos, sin)


def _attn_bwd(p, kp, gq, sink, cos, sin, dmix, *, lc, name):
    t = p.shape[0]
    scale = 64 ** -0.5
    span = QB + 2 * WINDOW

    def body(q_ref, kp_ref, v_ref, g_ref, sink_ref, c_ref, s_ref, do_ref,
             dq_ref, dk_ref, dv_ref, dg_ref, dsink_ref):
        i = pl.program_id(0)

        @pl.when(i == 0)
        def _():
            dk_ref[...] = jnp.zeros_like(dk_ref)
            dv_ref[...] = jnp.zeros_like(dv_ref)
            dg_ref[...] = jnp.zeros_like(dg_ref)
            dsink_ref[...] = jnp.zeros_like(dsink_ref)

        start, valid, kvar, vvar = _attn_common(i, t, lc, kp_ref, v_ref)
        cosv, sinv, gv = c_ref[...], s_ref[...], g_ref[...]
        nk = lc + span
        dkt = [jnp.zeros((64, nk), F32), jnp.zeros((64, nk), F32)]
        dvt = [jnp.zeros((64, nk), F32), jnp.zeros((64, nk), F32)]
        for j in range(4):
            kvh = j // 2
            xh, rs = _pair_norm(q_ref[:, 128 * j:128 * j + 128], None)
            q2 = _bf(_rope64(xh * gv, cosv, sinv) * scale)
            do2 = _bf(do_ref[:, 128 * j:128 * j + 128])
            dq2 = jnp.zeros((QB, 128), F32)
            for half in range(2):
                s = _dot_nt(q2, kvar[kvh][half])
                pr, ps = _softmax_sink(s, valid, sink_ref[2 * j + half])
                dp = _dot_nt(do2, vvar[kvh][half])
                delta = jnp.sum(pr * dp, axis=-1, keepdims=True)
                ds = pr * (dp - delta)
                dsk = jnp.sum(jnp.sum(-ps * delta, axis=0, keepdims=True), axis=1, keepdims=True)
                _acc_row(dsink_ref, 2 * j + half, jnp.broadcast_to(dsk, (1, 128)))
                dq2 = dq2 + _dot(ds, kvar[kvh][half])
                hrows = slice(64 * half, 64 * half + 64)
                dkt[kvh] = dkt[kvh] + _dot_tn(q2, ds)[hrows]
                dvt[kvh] = dvt[kvh] + _dot_tn(do2, pr)[hrows]
            dn = _rope64_t(dq2 * scale, cosv, sinv)
            _acc_row(dg_ref, 0, jnp.sum(dn * xh, axis=0, keepdims=True))
            dxh = dn * gv
            dq_ref[:, 128 * j:128 * j + 128] = (rs * (dxh - xh * _pair_mean(dxh * xh))).astype(BF16)
        dk_all = jnp.concatenate(dkt, axis=0).T
        dv_all = jnp.concatenate(dvt, axis=0).T
        dk_ref[0:lc, :] += dk_all[0:lc]
        dv_ref[0:lc, :] += dv_all[0:lc]
        dk_ref[pl.ds(start, span), :] += dk_all[lc:nk]
        dv_ref[pl.ds(start, span), :] += dv_all[lc:nk]

    qblk = pl.BlockSpec((QB, 128), lambda i: (i, 0))
    full = pl.BlockSpec((t, 128), lambda i: (0, 0))
    small = pl.BlockSpec((8, 128), lambda i: (0, 0))
    return _pcall(
        body, name=name, grid=(t // QB,),
        in_specs=[pl.BlockSpec((QB, 512), lambda i: (i, 0)), full,
                  pl.BlockSpec((t, 128), lambda i: (0, 5)),
                  pl.BlockSpec((1, 128), lambda i: (0, 0)),
                  pl.BlockSpec(memory_space=pltpu.SMEM), qblk, qblk,
                  pl.BlockSpec((QB, 512), lambda i: (i, 0))],
        out_specs=[pl.BlockSpec((QB, 512), lambda i: (i, 0)), full, full, small, small],
        out_shape=[jax.ShapeDtypeStruct((t, 512), BF16), jax.ShapeDtypeStruct((t, 128), F32),
                   jax.ShapeDtypeStruct((t, 128), F32), jax.ShapeDtypeStruct((8, 128), F32),
                   jax.ShapeDtypeStruct((8, 128), F32)],
    )(p, kp, p, gq, sink, cos, sin, dmix)


def _tri(rev):
    r, c = _iota((CHUNK, CHUNK), 0), _iota((CHUNK, CHUNK), 1)
    return (c >= r) if rev else (c <= r)


def _blk_map(nb, rev, backward):
    if not rev:
        return (lambda n: nb - 1 - n) if backward else (lambda n: n)
    if backward:
        return lambda n: jnp.where(n < nb - 1, n + 1, 0)
    return lambda n: jnp.where(n == 0, 0, nb - n)


def _chunk_order(rev, backward, nc=TM // CHUNK):
    order = list(range(nc))
    return order[::-1] if (rev != backward) else order


def _hgrn_gates(qraw, fraw, lb):
    sq = _sigmoid(qraw)
    sf = _sigmoid(fraw)
    f = lb + (1.0 - lb) * sf
    return qraw * sq, 1.0 - f, jnp.log(f), sq, sf, f


HGRN_HP = 2


def _chunk_cumsum(x, rev):
    n = x.shape[0]
    pos = _iota(x.shape, 0) & (CHUNK - 1)
    s = 1
    while s < CHUNK:
        if rev:
            x = x + jnp.where(pos < CHUNK - s, pltpu.roll(x, n - s, 0), 0.0)
        else:
            x = x + jnp.where(pos >= s, pltpu.roll(x, s, 0), 0.0)
        s *= 2
    return x


def _block_terms(lf, rev):
    b = _chunk_cumsum(lf, rev)
    mid, last = (CHUNK // 2 - 1, 0) if rev else (CHUNK // 2, CHUNK - 1)

    def chunk_row(off):
        return jnp.concatenate([jnp.broadcast_to(b[c * CHUNK + off:c * CHUNK + off + 1, :], (CHUNK, b.shape[1]))
                                for c in range(TM // CHUNK)], axis=0)

    r, bl = chunk_row(mid), chunk_row(last)
    return _tri(rev), jnp.exp(b - r), jnp.exp(r - b), jnp.exp(b), jnp.exp(bl - b), jnp.exp(bl)


def _headnorm_apply(o, gv, gain):
    n = o * lax.rsqrt(jnp.mean(o * o, axis=-1, keepdims=True) + EPS)
    if gain is not None:
        n = n * gain
    return (n * (gv * _sigmoid(gv))).astype(BF16)


def _headnorm_grad(o, gv, dy, gain):
    rs = lax.rsqrt(jnp.mean(o * o, axis=-1, keepdims=True) + EPS)
    xh = o * rs
    n = xh * gain if gain is not None else xh
    sg = _sigmoid(gv)
    dn = dy * (gv * sg)
    dg = (dy * n * (sg * (1.0 + gv * (1.0 - sg)))).astype(BF16)
    dgain = jnp.sum(dn * xh, axis=0, keepdims=True)
    dxh = dn * gain if gain is not None else dn
    return rs * (dxh - xh * jnp.mean(dxh * xh, axis=-1, keepdims=True)), dg, dgain


def _hgrn_fwd(p, lb, *, rev, name, ofw=None, gain=None):
    t = p.shape[0]
    nb, nc = t // TM, TM // CHUNK
    bmap = _blk_map(nb, rev, False)
    fcol = 14 if rev else 10
    fused = ofw is not None

    def body(*refs):
        q_ref, f_ref, v_ref, lb_ref = refs[:4]
        if fused:
            ofw_ref, g_ref, gain_ref, o_ref, sh_ref, mix_ref, st = refs[4:]
        else:
            o_ref, sh_ref, st = refs[4:]

        @pl.when(pl.program_id(1) == 0)
        def _():
            st[...] = jnp.zeros_like(st)
        for hh in range(HGRN_HP):
            ln = slice(128 * hh, 128 * hh + 128)
            q, k, lf, _, _, _ = _hgrn_gates(q_ref[:, ln], f_ref[:, ln], lb_ref[:, ln])
            tri, eq, ek, ei, eki, eb = _block_terms(lf, rev)
            qe, ke, qi, ki, vb = _bf(q * eq), _bf(k * ek), _bf(q * ei), _bf(k * eki), _bf(v_ref[:, ln])
            intra = []
            for cc in range(nc):
                rows = slice(cc * CHUNK, (cc + 1) * CHUNK)
                a = jnp.where(tri, _dot_nt(qe[rows], ke[rows]), 0.0)
                intra.append(_dot(a, vb[rows]))
            s = st[hh]
            for cc in _chunk_order(rev, False):
                rows = slice(cc * CHUNK, (cc + 1) * CHUNK)
                sh_ref[hh, cc] = s
                o_ref[rows, ln] = intra[cc] + _dot_nt(qi[rows], s)
                s = s * eb[cc * CHUNK:cc * CHUNK + 1, :] + _dot_tn(vb[rows], ki[rows])
            st[hh] = s
            if fused:
                osum = o_ref[:, ln] + ofw_ref[:, ln]
                o_ref[:, ln] = osum
                mix_ref[:, ln] = _headnorm_apply(osum, g_ref[:, ln], gain_ref[...])

    hp, wd = HGRN_HP, 128 * HGRN_HP

    def col(c0):
        return pl.BlockSpec((TM, wd), lambda h, n: (bmap(n), c0 // hp + h))

    oblk = pl.BlockSpec((TM, wd), lambda h, n: (bmap(n), h))
    ins, specs = [p, p, p, lb], [col(6), col(fcol), col(18), pl.BlockSpec((1, wd), lambda h, n: (0, h))]
    out_specs = [oblk, pl.BlockSpec((hp, nc, 128, 128), lambda h, n: (h, bmap(n), 0, 0))]
    out_shape = [jax.ShapeDtypeStruct((t, 512), F32), jax.ShapeDtypeStruct((4, t // CHUNK, 128, 128), F32)]
    if fused:
        ins += [ofw, p, gain]
        specs += [oblk, col(22), pl.BlockSpec((1, 128), lambda h, n: (0, 0))]
        out_specs.append(oblk)
        out_shape.append(jax.ShapeDtypeStruct((t, 512), BF16))
    return _pcall(body, name=name, grid=(4 // hp, nb), in_specs=specs, out_specs=out_specs, out_shape=out_shape,
                  scratch_shapes=[pltpu.VMEM((hp, 128, 128), F32)])(*ins)


def _hgrn_bwd(p, lb, sh, do, prev, *, rev, name, head=None):
    t = p.shape[0]
    nb, nc = t // TM, TM // CHUNK
    bmap = _blk_map(nb, rev, True)
    fcol = 14 if rev else 10
    has_prev = prev is not None
    odt = BF16 if has_prev else F32
    fused = head is not None

    def body(*refs):
        refs = list(refs)
        q_ref, f_ref, v_ref, lb_ref, sh_ref = refs[:5]
        pos = 5
        if fused:
            osum_ref, g_ref, dmix_ref, gain_ref = refs[5:9]
            pos = 9
        else:
            do_ref = refs[5]
            pos = 6
        if has_prev:
            pq_ref, pv_ref = refs[pos], refs[pos + 1]
            pos += 2
        dq_ref, df_ref, dv_ref, dlb_ref = refs[pos:pos + 4]
        pos += 4
        if fused:
            do_out, dg_ref, dgain_ref = refs[pos:pos + 3]
            pos += 3
        dst = refs[pos]

        @pl.when(pl.program_id(1) == 0)
        def _():
            dst[...] = jnp.zeros_like(dst)
            dlb_ref[...] = jnp.zeros_like(dlb_ref)

        if fused:
            @pl.when((pl.program_id(0) == 0) & (pl.program_id(1) == 0))
            def _():
                dgain_ref[...] = jnp.zeros_like(dgain_ref)

        cat = functools.partial(jnp.concatenate, axis=0)
        for hh in range(HGRN_HP):
            ln = slice(128 * hh, 128 * hh + 128)
            lbv = lb_ref[:, ln]
            qraw, fraw = q_ref[:, ln], f_ref[:, ln]
            q, k, lf, sq, sf, f = _hgrn_gates(qraw, fraw, lbv)
            tri, eq, ek, ei, eki, eb = _block_terms(lf, rev)
            qe, ke, qi, ki = q * eq, k * ek, q * ei, k * eki
            if fused:
                dov, dg, dgain = _headnorm_grad(osum_ref[:, ln], g_ref[:, ln], dmix_ref[:, ln], gain_ref[...])
                do_out[:, ln] = dov
                dg_ref[:, ln] = dg
                _acc_row(dgain_ref, 0, dgain)
            else:
                dov = do_ref[:, ln]
            qeb, keb, qib, kib, vb, dob = _bf(qe), _bf(ke), _bf(qi), _bf(ki), _bf(v_ref[:, ln]), _bf(dov)
            dv, dqe, dke, dqi = [None] * nc, [None] * nc, [None] * nc, [None] * nc
            for cc in range(nc):
                rows = slice(cc * CHUNK, (cc + 1) * CHUNK)
                a = jnp.where(tri, _dot_nt(qeb[rows], keb[rows]), 0.0)
                da = jnp.where(tri, _dot_nt(dob[rows], vb[rows]), 0.0)
                dv[cc] = _dot_tn(a, dob[rows])
                dqe[cc], dke[cc] = _dot(da, keb[rows]), _dot_tn(da, qeb[rows])
                dqi[cc] = _dot(dob[rows], sh_ref[hh, cc])
            dki, dbl = [None] * nc, [None] * nc
            ds = dst[hh]
            for cc in _chunk_order(rev, True):
                rows = slice(cc * CHUNK, (cc + 1) * CHUNK)
                ebc = eb[cc * CHUNK:cc * CHUNK + 1, :]
                dv[cc] = dv[cc] + _dot_nt(kib[rows], ds)
                dki[cc] = _dot(vb[rows], ds)
                dbl[cc] = jnp.broadcast_to(jnp.sum(dki[cc] * ki[rows], axis=0, keepdims=True)
                                           + jnp.sum(ds * sh_ref[hh, cc], axis=0, keepdims=True) * ebc, (CHUNK, 128))
                ds = ds * ebc + _dot_tn(dob[rows], qib[rows])
            dst[hh] = ds
            dqe, dke, dqi, dki, dv, dbl = cat(dqe), cat(dke), cat(dqi), cat(dki), cat(dv), cat(dbl)
            dq = dqe * eq + dqi * ei
            dk = dke * ek + dki * eki
            last = 0 if rev else CHUNK - 1
            db = dqe * qe - dke * ke + dqi * qi - dki * ki
            db = db + jnp.where((_iota(db.shape, 0) & (CHUNK - 1)) == last, dbl, 0.0)
            dlf = _chunk_cumsum(db, not rev)
            dqr = dq * (sq * (1.0 + qraw * (1.0 - sq)))
            dfv = dlf / f - dk
            dfr = dfv * (1.0 - lbv) * (sf * (1.0 - sf))
            dlb_ref[:, ln] += jnp.sum(dfv * (1.0 - sf), axis=0, keepdims=True)
            if has_prev:
                dqr = dqr + pq_ref[:, ln]
                dv = dv + pv_ref[:, ln]
            dq_ref[:, ln] = dqr.astype(odt)
            df_ref[:, ln] = dfr.astype(odt)
            dv_ref[:, ln] = dv.astype(odt)

    hp, wd = HGRN_HP, 128 * HGRN_HP

    def col(c0):
        return pl.BlockSpec((TM, wd), lambda h, n: (bmap(n), c0 // hp + h))

    oblk = pl.BlockSpec((TM, wd), lambda h, n: (bmap(n), h))
    ins = [p, p, p, lb, sh]
    specs = [col(6), col(fcol), col(18), pl.BlockSpec((1, wd), lambda h, n: (0, h)),
             pl.BlockSpec((hp, nc, 128, 128), lambda h, n: (h, bmap(n), 0, 0))]
    if fused:
        osum, dmix, gain = head
        ins += [osum, p, dmix, gain]
        specs += [oblk, col(22), pl.BlockSpec((TM, wd), lambda h, n: (bmap(n), 4 // hp + h)),
                  pl.BlockSpec((1, 128), lambda h, n: (0, 0))]
    else:
        ins.append(do); specs.append(oblk)
    if has_prev:
        ins += list(prev); specs += [oblk, oblk]
    out_specs = [oblk, oblk, oblk, pl.BlockSpec((1, wd), lambda h, n: (0, h))]
    out_shape = [jax.ShapeDtypeStruct((t, 512), odt)] * 3 + [jax.ShapeDtypeStruct((1, 512), F32)]
    if fused:
        out_specs += [oblk, oblk, pl.BlockSpec((8, 128), lambda h, n: (0, 0))]
        out_shape += [jax.ShapeDtypeStruct((t, 512), F32), jax.ShapeDtypeStruct((t, 512), BF16),
                      jax.ShapeDtypeStruct((8, 128), F32)]
    return _pcall(body, name=name, grid=(4 // hp, nb), in_specs=specs, out_specs=out_specs, out_shape=out_shape,
                  scratch_shapes=[pltpu.VMEM((hp, 128, 128), F32)])(*ins)


def _rope256(x, cos, sin):
    x1, x2 = x[:, 0:128], x[:, 128:256]
    return jnp.concatenate([x1 * cos - x2 * sin, x2 * cos + x1 * sin], axis=-1)


def _rope256_t(d, cos, sin):
    d1, d2 = d[:, 0:128], d[:, 128:256]
    return jnp.concatenate([d1 * cos + d2 * sin, d2 * cos - d1 * sin], axis=-1)


RET_DK, RET_DV, RET_H = 256, 512, 4
RET_KSCALE = RET_DK ** -0.5
RCH = TM
RET_HP = 4


def _ret_terms(lg, rev):
    r, c = _iota((RCH, RCH), 0), _iota((RCH, RCH), 1)
    rel = ((c - r) if rev else (r - c)).astype(F32)
    dmat = jnp.where(rel >= 0, jnp.exp(lg[:, 0:1] * jnp.maximum(rel, 0.0)), 0.0)
    pos = _iota((RCH, 1), 0).astype(F32)
    cnt = (RCH - pos) if rev else (pos + 1.0)
    ei = jnp.exp(lg * cnt)
    eki = jnp.exp(lg * (RCH - cnt))
    eb = jnp.exp(lg * float(RCH))
    return dmat, ei, eki, eb


def _ret_fwd(p, lgt, cos, sin, *, rev, name, ofw=None):
    t = p.shape[0]
    nb, nc = t // TM, TM // RCH
    bmap = _blk_map(nb, rev, False)
    fused = ofw is not None

    def body(*refs):
        q_ref, k_ref, v_ref, lg_ref, c_ref, s_ref = refs[:6]
        if fused:
            ofw_ref, g_ref, o_ref, sh_ref, mix_ref, st = refs[6:]
        else:
            o_ref, sh_ref, st = refs[6:]

        @pl.when(pl.program_id(1) == 0)
        def _():
            st[...] = jnp.zeros_like(st)
        for hh in range(RET_HP):
            qc, vc = slice(RET_DK * hh, RET_DK * (hh + 1)), slice(RET_DV * hh, RET_DV * (hh + 1))
            dmat, ei, eki, eb = _ret_terms(lg_ref[hh], rev)
            for cc in _chunk_order(rev, False, nc):
                rows = slice(cc * RCH, (cc + 1) * RCH)
                cosv, sinv = c_ref[rows, :], s_ref[rows, :]
                q = _rope256(q_ref[rows, qc].astype(F32), cosv, sinv)
                k = _rope256(k_ref[rows, qc].astype(F32), cosv, sinv) * RET_KSCALE
                v = v_ref[rows, vc]
                s0 = st[hh]
                sh_ref[hh, cc] = s0.astype(BF16)
                a = _dot_nt(q, k) * dmat
                o = _dot(a, v) + _dot_nt(q * ei, s0)
                st[hh] = s0 * eb + _dot_tn(v, k * eki)
                if fused:
                    o = o + ofw_ref[rows, vc]
                    mix_ref[rows, vc] = _headnorm_apply(o, g_ref[rows, vc].astype(F32), None)
                o_ref[rows, vc] = o

    hp = RET_HP
    tab = pl.BlockSpec((TM, 128), lambda h, n: (bmap(n), 0))
    oblk = pl.BlockSpec((TM, hp * RET_DV), lambda h, n: (bmap(n), h))
    ins = [p, p, p, lgt, cos, sin]
    specs = [pl.BlockSpec((TM, hp * RET_DK), lambda h, n: (bmap(n), h)),
             pl.BlockSpec((TM, hp * RET_DK), lambda h, n: (bmap(n), RET_H // hp + h)),
             pl.BlockSpec((TM, hp * RET_DV), lambda h, n: (bmap(n), RET_H // hp + h)),
             pl.BlockSpec((hp, 1, RET_DK), lambda h, n: (h, 0, 0)), tab, tab]
    out_specs = [oblk, pl.BlockSpec((hp, nc, RET_DV, RET_DK), lambda h, n: (h, bmap(n), 0, 0))]
    out_shape = [jax.ShapeDtypeStruct((t, RET_H * RET_DV), F32),
                 jax.ShapeDtypeStruct((RET_H, t // RCH, RET_DV, RET_DK), BF16)]
    if fused:
        ins += [ofw, p]
        specs += [oblk, pl.BlockSpec((TM, hp * RET_DV), lambda h, n: (bmap(n), 2 * RET_H // hp + h))]
        out_specs.append(oblk)
        out_shape.append(jax.ShapeDtypeStruct((t, RET_H * RET_DV), BF16))
    return _pcall(body, name=name, grid=(RET_H // hp, nb), in_specs=specs, out_specs=out_specs, out_shape=out_shape,
                  scratch_shapes=[pltpu.VMEM((hp, RET_DV, RET_DK), F32)])(*ins)


def _ret_bwd(p, lgt, cos, sin, sh, do, prev, *, rev, name, head=None):
    t = p.shape[0]
    nb, nc = t // TM, TM // RCH
    bmap = _blk_map(nb, rev, True)
    has_prev = prev is not None
    odt = BF16 if has_prev else F32
    fused = head is not None

    def body(*refs):
        refs = list(refs)
        q_ref, k_ref, v_ref, lg_ref, c_ref, s_ref, sh_ref = refs[:7]
        if fused:
            osum_ref, g_ref, dmix_ref = refs[7:10]
            pos = 10
        else:
            do_ref = refs[7]
            pos = 8
        if has_prev:
            pq_ref, pk_ref, pv_ref = refs[pos:pos + 3]
            pos += 3
        dq_ref, dk_ref, dv_ref = refs[pos:pos + 3]
        pos += 3
        if fused:
            do_out, dg_ref = refs[pos:pos + 2]
            pos += 2
        dst = refs[pos]

        @pl.when(pl.program_id(1) == 0)
        def _():
            dst[...] = jnp.zeros_like(dst)

        for hh in range(RET_HP):
            qc, vc = slice(RET_DK * hh, RET_DK * (hh + 1)), slice(RET_DV * hh, RET_DV * (hh + 1))
            dmat, ei, eki, eb = _ret_terms(lg_ref[hh], rev)
            for cc in _chunk_order(rev, True, nc):
                rows = slice(cc * RCH, (cc + 1) * RCH)
                cosv, sinv = c_ref[rows, :], s_ref[rows, :]
                q = _rope256(q_ref[rows, qc].astype(F32), cosv, sinv)
                k = _rope256(k_ref[rows, qc].astype(F32), cosv, sinv) * RET_KSCALE
                v = v_ref[rows, vc]
                if fused:
                    dov, dg, _ = _headnorm_grad(osum_ref[rows, vc], g_ref[rows, vc].astype(F32), dmix_ref[rows, vc], None)
                    do_out[rows, vc] = dov
                    dg_ref[rows, vc] = dg
                else:
                    dov = do_ref[rows, vc]
                s0 = sh_ref[hh, cc]
                dsc = dst[hh]
                qi, ki = q * ei, k * eki
                a = _dot_nt(q, k) * dmat
                da = _dot_nt(dov, v) * dmat
                dv = _dot_tn(a, dov) + _dot_nt(ki, dsc)
                dqs = _dot(da, k) + _dot(dov, s0) * ei
                dks = _dot_tn(da, q) + _dot(v, dsc) * eki
                dst[hh] = dsc * eb + _dot_tn(dov, qi)
                dq = _rope256_t(dqs, cosv, sinv)
                dk = _rope256_t(dks * RET_KSCALE, cosv, sinv)
                if has_prev:
                    dq = dq + pq_ref[rows, qc]
                    dk = dk + pk_ref[rows, qc]
                    dv = dv + pv_ref[rows, vc]
                dq_ref[rows, qc] = dq.astype(odt)
                dk_ref[rows, qc] = dk.astype(odt)
                dv_ref[rows, vc] = dv.astype(odt)

    hp = RET_HP
    tab = pl.BlockSpec((TM, 128), lambda h, n: (bmap(n), 0))
    qblk = pl.BlockSpec((TM, hp * RET_DK), lambda h, n: (bmap(n), h))
    vblk = pl.BlockSpec((TM, hp * RET_DV), lambda h, n: (bmap(n), h))
    ins = [p, p, p, lgt, cos, sin, sh]
    specs = [qblk, pl.BlockSpec((TM, hp * RET_DK), lambda h, n: (bmap(n), RET_H // hp + h)),
             pl.BlockSpec((TM, hp * RET_DV), lambda h, n: (bmap(n), RET_H // hp + h)),
             pl.BlockSpec((hp, 1, RET_DK), lambda h, n: (h, 0, 0)), tab, tab,
             pl.BlockSpec((hp, nc, RET_DV, RET_DK), lambda h, n: (h, bmap(n), 0, 0))]
    if fused:
        osum, dmix = head
        ins += [osum, p, dmix]
        specs += [vblk, pl.BlockSpec((TM, hp * RET_DV), lambda h, n: (bmap(n), 2 * RET_H // hp + h)), vblk]
    else:
        ins.append(do); specs.append(vblk)
    if has_prev:
        ins += list(prev); specs += [qblk, qblk, vblk]
    out_specs = [qblk, qblk, vblk]
    out_shape = [jax.ShapeDtypeStruct((t, RET_H * RET_DK), odt), jax.ShapeDtypeStruct((t, RET_H * RET_DK), odt),
                 jax.ShapeDtypeStruct((t, RET_H * RET_DV), odt)]
    if fused:
        out_specs += [vblk, vblk]
        out_shape += [jax.ShapeDtypeStruct((t, RET_H * RET_DV), F32), jax.ShapeDtypeStruct((t, RET_H * RET_DV), BF16)]
    return _pcall(body, name=name, grid=(RET_H // hp, nb), in_specs=specs, out_specs=out_specs, out_shape=out_shape,
                  scratch_shapes=[pltpu.VMEM((hp, RET_DV, RET_DK), F32)])(*ins)


def _rope_tables(lc, l):
    tt = jnp.arange(l)
    row, colp = (tt // 64).astype(F32), (tt % 64).astype(F32)
    inv = 10000.0 ** (-jnp.arange(16, dtype=F32) / 16)
    ang = jnp.concatenate([row[:, None] * inv, colp[:, None] * inv], axis=-1)
    ang = jnp.concatenate([jnp.zeros((lc, 32), F32), ang], axis=0)
    acos, asin = jnp.tile(jnp.cos(ang), (1, 4)), jnp.tile(jnp.sin(ang), (1, 4))
    theta = 1.0 / (10000.0 ** jnp.linspace(0.0, 1.0, 128, dtype=F32))
    rang = jnp.arange(l, dtype=F32)[:, None] * theta
    rang = jnp.concatenate([jnp.zeros((lc, 128), F32), rang], axis=0)
    return acos, asin, jnp.cos(rang), jnp.sin(rang)


class _Weights:
    def __init__(self, w):
        self.w = w

    def first(self, after):
        return self.w

    def rest_landed(self, after):
        pass

    def rest(self, after):
        return self.w

    def send_grads(self, grp, grads):
        return jnp.zeros((8, 128), F32)


def _local_step(x0, target, mods, ng, wsrc, small):
    t, d = x0.shape
    l = target.shape[0]
    lc = t - l
    acos, asin, rcos, rsin = _rope_tables(lc, l)
    lg_fw = jnp.log(1.0 - 2.0 ** (-5.0 - jnp.arange(RET_H, dtype=F32)))
    lgt_fw = jnp.broadcast_to(lg_fw[:, None, None], (RET_H, 1, RET_DK))
    lgt_bw = jnp.broadcast_to(lg_fw[::-1][:, None, None], (RET_H, 1, RET_DK))
    gq, gk, sink, gain, lb = small['gq'], small['gk'], small['sink'], small['gain'], small['lb']

    (h1,) = _row_fwd(x0, mods, g=ng[0], shift=0, scale=1, name='l0_norm1')
    w = wsrc.first(h1)
    p0 = _mm_nn(h1, w['even_in'], name='l0_in')
    kp = _kprep_fwd(p0, gk, acos, asin, name='l0_kprep')
    att = _attn_fwd(p0, kp, gq, sink, acos, asin, lc=lc, name='l0_attn')
    hof, hsf = _hgrn_fwd(p0, lb, rev=False, name='l0_hgrn_f')
    wsrc.rest_landed(hof)
    hos, hsb, bmix = _hgrn_fwd(p0, lb, rev=True, name='l0_hgrn_b', ofw=hof, gain=gain)
    mix0 = jnp.concatenate([att, bmix], axis=1)
    y0 = _mm_nn(mix0, w['even_out'], name='l0_out')
    x1, h2 = _row_fwd(x0, mods, y=y0, gate=2, g=ng[1], shift=3, scale=4, name='l0_norm2')
    w = dict(w, **wsrc.rest(h2))
    u0, a0 = _ffn_in(h2, w['ffn_in'], lead=0, name='ffn_in')
    z0 = _mm_nn(a0, w['ffn_out'], lead=0, name='ffn_out')
    x2, h3 = _row_fwd(x1, mods, y=z0, gate=5, g=ng[2], shift=12, scale=13, name='l1_norm1')
    p1 = _mm_nn(h3, w['odd_in'], out_dtype=BF16, name='l1_in')
    rof, rsf = _ret_fwd(p1, lgt_fw, rcos, rsin, rev=False, name='l1_ret_f')
    ros, rsb, mix1 = _ret_fwd(p1, lgt_bw, rcos, rsin, rev=True, name='l1_ret_b', ofw=rof)
    y1 = _mm_nn(mix1, w['odd_out'], name='l1_out')
    x3, h4 = _row_fwd(x2, mods, y=y1, gate=14, g=ng[3], shift=15, scale=16, name='l1_norm2')
    u1, a1 = _ffn_in(h4, w['ffn_in'], lead=1, name='ffn_in')
    z1 = _mm_nn(a1, w['ffn_out'], lead=1, name='ffn_out')
    loss, dx4, dz1, s_fin = _row_final(x3, z1, mods, target, gate=17, name='loss')

    du1 = _ffn_dx(dz1, w['ffn_out'], u1, lead=1, name='ffn_out_dx')
    g_ffn_out1 = _mm_tn(a1, dz1, name='ffn_out_dw')
    dh4 = _mm_nt(du1, w['ffn_in'], lead=1, name='ffn_in_dx')
    g_ffn_in1 = _mm_tn(h4, du1, name='ffn_in_dw')
    dx3, dy1, s_l1n2 = _row_bwd(x3, dx4, dh4, mods, ng[3], shift=15, scale=16, y=y1, gate=14, name='l1_norm2_bwd')
    dmix1 = _mm_nt(dy1, w['odd_out'], name='l1_out_dx')
    g_odd_out = _mm_tn(mix1, dy1, name='l1_out_dw')
    rdq, rdk, rdv, rdo, rdg = _ret_bwd(p1, lgt_fw, rcos, rsin, rsf, None, None, rev=False, name='l1_ret_f_bwd',
                                       head=(ros, dmix1))
    rdq, rdk, rdv = _ret_bwd(p1, lgt_bw, rcos, rsin, rsb, rdo, (rdq, rdk, rdv), rev=True, name='l1_ret_b_bwd')
    dp1 = jnp.concatenate([rdq, rdk, rdv, rdg], axis=1)
    dh3 = _mm_nt(dp1, w['odd_in'], name='l1_in_dx')
    g_odd_in = _mm_tn(h3, dp1, name='l1_in_dw')
    mods = mods + wsrc.send_grads('early', dict(ffn_in1=g_ffn_in1, ffn_out1=g_ffn_out1, odd_in=g_odd_in,
                                                odd_out=g_odd_out))[0, 0]
    dx2, dz0, s_l1n1 = _row_bwd(x2, dx3, dh3, mods, ng[2], shift=12, scale=13, y=z0, gate=5, name='l1_norm1_bwd')
    du0 = _ffn_dx(dz0, w['ffn_out'], u0, lead=0, name='ffn_out_dx')
    g_ffn_out0 = _mm_tn(a0, dz0, name='ffn_out_dw')
    dh2 = _mm_nt(du0, w['ffn_in'], lead=0, name='ffn_in_dx')
    g_ffn_in0 = _mm_tn(h2, du0, name='ffn_in_dw')
    mods = mods + wsrc.send_grads('mid', dict(ffn_in0=g_ffn_in0, ffn_out0=g_ffn_out0))[0, 0]
    dx1, dy0, s_l0n2 = _row_bwd(x1, dx2, dh2, mods, ng[1], shift=3, scale=4, y=y0, gate=2, name='l0_norm2_bwd')
    dmix0 = _mm_nt(dy0, w['even_out'], name='l0_out_dx')
    g_even_out = _mm_tn(mix0, dy0, name='l0_out_dw')
    hq, hff, hv, dlb_f, hdo, hdg, s_gain = _hgrn_bwd(p0, lb, hsf, None, None, rev=False, name='l0_hgrn_f_bwd',
                                                     head=(hos, dmix0, gain))
    hq, hfb, hv, dlb_b = _hgrn_bwd(p0, lb, hsb, hdo, (hq, hv), rev=True, name='l0_hgrn_b_bwd')
    adq, dkp, adv, s_gq, s_sink = _attn_bwd(p0, kp, gq, sink, acos, asin, dmix0, lc=lc, name='l0_attn_bwd')
    dkv, s_gk = _kprep_bwd(p0, gk, acos, asin, dkp, adv, name='l0_kprep_bwd')
    dp0 = jnp.concatenate([adq, dkv, hq, _bf(hff), hfb, hv, hdg], axis=1)
    dh1 = _mm_nt(dp0, w['even_in'], name='l0_in_dx')
    g_even_in = _mm_tn(h1, dp0, name='l0_in_dw')
    dx0, s_l0n1 = _row_bwd(x0, dx1, dh1, mods, ng[0], shift=0, scale=1, latent_only=True, name='l0_norm1_bwd')

    grads = dict(ffn_in0=g_ffn_in0, ffn_in1=g_ffn_in1, ffn_out0=g_ffn_out0, ffn_out1=g_ffn_out1,
                 even_in=g_even_in, even_out=g_even_out, odd_in=g_odd_in, odd_out=g_odd_out)
    sums = dict(fin=s_fin, l1n2=s_l1n2, l1n1=s_l1n1, l0n2=s_l0n2, l0n1=s_l0n1, gain=s_gain, gq=s_gq, gk=s_gk,
                sink=s_sink, dlb_f=dlb_f, dlb_b=dlb_b)
    return loss, dx0, grads, sums


def _place():
    return lax.axis_index("x"), lax.axis_index("y"), lax.axis_index("c")


def _ag8(blk, *, name):
    r, c = blk.shape
    flips = [(dx, dy, dc) for dx in (0, 1) for dy in (0, 1) for dc in (0, 1) if (dx, dy, dc) != (0, 0, 0)]

    def body(x_ref, out_ref, send_sems, recv_sems, local_sem):
        ax, ay, ac = _place()
        me = 4 * ax + 2 * ay + ac
        mine = pltpu.make_async_copy(x_ref, out_ref.at[me], local_sem)
        mine.start()
        sent = []
        for k, (dx, dy, dc) in enumerate(flips):
            peer = (lax.rem(ax + dx, 2), lax.rem(ay + dy, 2), lax.rem(ac + dc, 2))
            cp = pltpu.make_async_remote_copy(src_ref=x_ref, dst_ref=out_ref.at[me], send_sem=send_sems.at[k],
                                              recv_sem=recv_sems.at[k], device_id=peer, device_id_type=MESH)
            cp.start()
            sent.append((cp, 4 * peer[0] + 2 * peer[1] + peer[2]))
        for k, (cp, pidx) in enumerate(sent):
            pltpu.make_async_remote_copy(src_ref=x_ref, dst_ref=out_ref.at[pidx], send_sem=send_sems.at[k],
                                         recv_sem=recv_sems.at[k], device_id=(ax, ay, ac),
                                         device_id_type=MESH).wait_recv()
        for cp, _ in sent:
            cp.wait_send()
        mine.wait()

    return _pcall(
        body, name=name,
        in_specs=[pl.BlockSpec(memory_space=pltpu.VMEM)],
        out_specs=pl.BlockSpec(memory_space=pltpu.VMEM),
        out_shape=jax.ShapeDtypeStruct((8, r, c), blk.dtype),
        scratch_shapes=[pltpu.SemaphoreType.DMA((7,)), pltpu.SemaphoreType.DMA((7,)), pltpu.SemaphoreType.DMA],
    )(blk)


_HBM = pl.BlockSpec(memory_space=pltpu.HBM)
_SEM = pl.BlockSpec(memory_space=pltpu.SEMAPHORE)
_DATAFLOW = pltpu.SideEffectType.DATAFLOW_SIDE_EFFECTING


def _split_start(bufs, plan, k, *, name):
    n = len(bufs)

    def body(*refs):
        ins, send_sems, recv_sems, token = refs[:n], refs[n], refs[n + 1], refs[2 * n + 2]
        for i, (src, dst, dev) in enumerate(plan(ins)):
            pltpu.make_async_remote_copy(src_ref=src, dst_ref=dst, send_sem=send_sems.at[i], recv_sem=recv_sems.at[i],
                                         device_id=dev, device_id_type=MESH).start()
        token[...] = jnp.zeros_like(token)

    res = _pcall(
        body, name=name,
        out_shape=(pltpu.SemaphoreType.DMA((k,)), pltpu.SemaphoreType.DMA((k,)),
                   *[pltpu.HBM(b.shape, b.dtype) for b in bufs], jax.ShapeDtypeStruct((8, 128), F32)),
        in_specs=[_HBM] * n, out_specs=(_SEM, _SEM, *[_HBM] * n, pl.BlockSpec(memory_space=pltpu.VMEM)),
        input_output_aliases={i: 2 + i for i in range(n)},
        compiler_params=pltpu.CompilerParams(has_side_effects=_DATAFLOW),
    )(*[pltpu.with_memory_space_constraint(b, pltpu.HBM) for b in bufs])
    return res[0], res[1], list(res[2:2 + n]), res[2 + n]


def _split_wait(bufs, send_sems, recv_sems, plan, after, *, name):
    n = len(bufs)

    def body(*refs):
        ins, ssem, rsem = refs[:n], refs[n], refs[n + 1]
        for i, (src, dst, dev) in enumerate(plan(ins)):
            cp = pltpu.make_async_remote_copy(src_ref=src, dst_ref=dst, send_sem=ssem.at[i], recv_sem=rsem.at[i],
                                              device_id=dev, device_id_type=MESH)
            cp.wait_send()
            cp.wait_recv()

    res = _pcall(
        body, name=name, out_shape=tuple(pltpu.HBM(b.shape, b.dtype) for b in bufs),
        in_specs=[_HBM] * n + [_SEM, _SEM, pl.BlockSpec(memory_space=pl.ANY)], out_specs=tuple([_HBM] * n),
        input_output_aliases={i: i for i in range(n)},
        compiler_params=pltpu.CompilerParams(has_side_effects=_DATAFLOW),
    )(*bufs, send_sems, recv_sems, after)
    return list(res)


_CHIP_FLIPS = [(1, 0), (0, 1), (1, 1)]


class _GatheredWeights:
    FIRST = ('even_in', 'even_out')
    REST = ('ffn_in', 'ffn_out', 'odd_in', 'odd_out')

    def __init__(self, shards, reducer):
        self.shards = shards
        self.send_grads = reducer.start
        self.ici = {}
        for grp, names in (('first', self.FIRST), ('rest', self.REST)):
            src = [shards[nm].reshape(2, shards[nm].shape[0] // 2, shards[nm].shape[1]) for nm in names]
            land = [lax.empty((4,) + a.shape, a.dtype) for a in src]
            m = len(names)
            sends, recvs, bufs, token = _split_start(src + land, functools.partial(self._ici_plan, m, True), 3 * m,
                                                     name='gather_' + grp + '_ici_start')
            self.ici[grp] = (sends, recvs, bufs, m)
            self.token = token if grp == 'first' else self.token + token
        self.rest_d2d = None

    @staticmethod
    def _ici_plan(m, sending, refs):
        ax, ay, ac = _place()
        s = 2 * ax + ay
        out = []
        for a in range(m):
            for dx, dy in _CHIP_FLIPS:
                px, py = lax.rem(ax + dx, 2), lax.rem(ay + dy, 2)
                slot = s if sending else 2 * px + py
                out.append((refs[a].at[ac], refs[m + a].at[slot, ac], (px, py, ac)))
        return out

    @staticmethod
    def _d2d_plan(m, sending, refs):
        ax, ay, ac = _place()
        out = []
        for a in range(m):
            for dx, dy in _CHIP_FLIPS:
                sp = 2 * lax.rem(ax + dx, 2) + lax.rem(ay + dy, 2)
                out.append((refs[a].at[sp, ac], refs[a].at[sp, ac if sending else 1 - ac], (ax, ay, 1 - ac)))
        return out

    def _landed(self, grp, after):
        sends, recvs, bufs, m = self.ici[grp]
        bufs = _split_wait(bufs, sends, recvs, functools.partial(self._ici_plan, m, False), after,
                           name='gather_' + grp + '_ici_wait')
        sends, recvs, land, _ = _split_start(bufs[m:], functools.partial(self._d2d_plan, m, True), 3 * m,
                                             name='gather_' + grp + '_d2d_start')
        return sends, recvs, land, m

    def _full(self, grp, names, d2d, after):
        sends, recvs, land, m = d2d
        land = _split_wait(land, sends, recvs, functools.partial(self._d2d_plan, m, False), after,
                           name='gather_' + grp + '_d2d_wait')
        s = 2 * lax.axis_index("x") + lax.axis_index("y")
        slot = lax.broadcasted_iota(jnp.int32, (4, 1, 1), 0)
        return {nm: _from_shards(nm, jnp.where(slot == s, self.shards[nm][None], g.reshape((4,) + self.shards[nm].shape)))
                for nm, g in zip(names, land)}

    def first(self, after):
        return self._full('first', self.FIRST, self._landed('first', after), after)

    def rest_landed(self, after):
        self.rest_d2d = self._landed('rest', after)

    def rest(self, after):
        return self._full('rest', self.REST, self.rest_d2d, after)


def _to_sibling(arrs, *, name):
    n = len(arrs)

    def body(*refs):
        ins, outs = refs[:n], refs[n:2 * n]
        send_sems, recv_sems = refs[2 * n:]
        ax, ay, ac = _place()
        cps = [pltpu.make_async_remote_copy(src_ref=ins[a], dst_ref=outs[a], send_sem=send_sems.at[a],
                                            recv_sem=recv_sems.at[a], device_id=(ax, ay, 1 - ac),
                                            device_id_type=MESH) for a in range(n)]
        for cp in cps:
            cp.start()
        for cp in cps:
            cp.wait_recv()
        for cp in cps:
            cp.wait_send()

    hbm = pl.BlockSpec(memory_space=pl.ANY)
    return _pcall(
        body, name=name, in_specs=[hbm] * n, out_specs=[hbm] * n,
        out_shape=[jax.ShapeDtypeStruct(a.shape, a.dtype) for a in arrs],
        scratch_shapes=[pltpu.SemaphoreType.DMA((n,))] * 2,
    )(*arrs)


def _mod_fwd(cond_raw, mw, mb, *, name):
    _, d, n = mw.shape

    def body(c_ref, w_ref, b_ref, o_ref):
        cv = c_ref[...]
        o_ref[...] = _dot(cv * _sigmoid(cv), w_ref[...]) + b_ref[...]

    return _pcall(
        body, name=name, grid=(2,),
        in_specs=[pl.BlockSpec((16, d), lambda l: (0, 0)), pl.BlockSpec((None, d, n), lambda l: (l, 0, 0)),
                  pl.BlockSpec((None, 1, n), lambda l: (l, 0, 0))],
        out_specs=pl.BlockSpec((None, 16, n), lambda l: (l, 0, 0)),
        out_shape=jax.ShapeDtypeStruct((2, 16, n), F32),
    )(cond_raw, mw, mb)


def _mod_bwd(cond_raw, dms, mw, *, name):
    _, d, n = mw.shape

    def body(c_ref, dm_ref, w_ref, gw_ref, dc_ref):
        @pl.when(pl.program_id(0) == 0)
        def _():
            dc_ref[...] = jnp.zeros_like(dc_ref)
        cv = c_ref[...]
        gw_ref[...] = _dot_tn(cv * _sigmoid(cv), dm_ref[...])
        dc_ref[...] += _dot_nt(dm_ref[...], w_ref[...])

    return _pcall(
        body, name=name, grid=(2,),
        in_specs=[pl.BlockSpec((16, d), lambda l: (0, 0)), pl.BlockSpec((None, 16, n), lambda l: (l, 0, 0)),
                  pl.BlockSpec((None, d, n), lambda l: (l, 0, 0))],
        out_specs=[pl.BlockSpec((None, d, n), lambda l: (l, 0, 0)), pl.BlockSpec((16, d), lambda l: (0, 0))],
        out_shape=[jax.ShapeDtypeStruct((2, d, n), F32), jax.ShapeDtypeStruct((16, d), F32)],
    )(cond_raw, dms, mw)


def _lb_fwd(hgrn_lb, *, name):
    def body(a_ref, o_ref):
        a0, a1 = a_ref[0:1, :], a_ref[1:2, :]
        m = jnp.maximum(a0, a1)
        e0, e1 = jnp.exp(a0 - m), jnp.exp(a1 - m)
        o_ref[...] = e0 / (e0 + e1)

    return _pcall(body, name=name, out_shape=jax.ShapeDtypeStruct((1, hgrn_lb.shape[1]), F32))(hgrn_lb)


PACK_TILES = ('l0n1', 'l0n2', 'l1n1', 'l1n2', 'fin', 'gq', 'gk', 'gain', 'dlb_f', 'dlb_b', 'sink')
PACK_ROW = {nm: 8 * i for i, nm in enumerate(PACK_TILES)}
MOD_SOURCE = ((('l0n1', 0), ('l0n1', 1), ('l0n2', 2), ('l0n2', 0), ('l0n2', 1), ('l1n1', 2)),
              (('l1n1', 0), ('l1n1', 1), ('l1n2', 2), ('l1n2', 0), ('l1n2', 1), ('fin', 2)))


def _small_finalize(gath, lb_pad, *, name):
    d = gath.shape[2]

    def body(g_ref, lb_ref, small_ref, glb_ref, gmb_ref, dm_ref):
        tot = g_ref[0]
        for e in range(1, 8):
            tot = tot + g_ref[e]

        def row(nm, r=0):
            return tot[PACK_ROW[nm] + r:PACK_ROW[nm] + r + 1, :]

        for k, nm in enumerate(('l0n1', 'l0n2', 'l1n1', 'l1n2')):
            small_ref[k:k + 1, :] = row(nm, 3) + row(nm, 7)
        for k, nm in ((4, 'gq'), (5, 'gk')):
            small_ref[k:k + 1, :] = row(nm) + pltpu.roll(row(nm), d - 64, 1)
        small_ref[6:7, :] = row('gain')
        small_ref[7:8, :] = row('sink')
        lbv = lb_ref[...]
        g0 = (row('dlb_f') + row('dlb_b')) * lbv * (1.0 - lbv)
        glb_ref[...] = jnp.zeros_like(glb_ref)
        glb_ref[0:1, :] = g0
        glb_ref[1:2, :] = -g0
        dm_ref[...] = jnp.zeros_like(dm_ref)
        for l in range(2):
            for part in range(6):
                nm, r = MOD_SOURCE[l][part]
                gmb_ref[l * 6 + part:l * 6 + part + 1, :] = row(nm, r) + row(nm, r + 4)
                rl = PACK_ROW[nm] + r + 4
                for e in range(8):
                    dm_ref[l, part, e:e + 1, :] = g_ref[e, rl:rl + 1, :]
                dm_ref[l, part, 8:9, :] = row(nm, r)

    return _pcall(
        body, name=name,
        out_shape=[jax.ShapeDtypeStruct((8, d), F32), jax.ShapeDtypeStruct((8, d), F32),
                   jax.ShapeDtypeStruct((12, d), F32), jax.ShapeDtypeStruct((2, 6, 16, d), F32)],
    )(gath, lb_pad)


def _cctx_grad(gath, c_ctx2, *, name):
    def body(g_ref, c_ref, o_ref):
        tot = ((g_ref[0, 0:1, :] + g_ref[2, 0:1, :]) + g_ref[4, 0:1, :]) + g_ref[6, 0:1, :]
        cv = c_ref[...]
        s = _sigmoid(cv)
        o_ref[...] = tot * (s * (1.0 + cv * (1.0 - s)))

    return _pcall(body, name=name, out_shape=jax.ShapeDtypeStruct(c_ctx2.shape, F32))(gath, c_ctx2)


def _row_block(r, c, limit=256 * 1024):
    best = None
    for br in range(16, r + 1, 16):
        if r % br == 0 and br * c <= limit:
            best = br
    return best if best is not None else r


def _sum4(own, landed, core, *, name):
    _, r, c = own.shape
    br = _row_block(r, c, 512 * 1024)

    def body(core_ref, own_ref, land_ref, o_ref):
        s = 2 * lax.axis_index("x") + lax.axis_index("y")
        p = [jnp.where(s == k, own_ref[k], land_ref[k]).astype(F32) for k in range(4)]
        o_ref[...] = ((p[0] + p[1]) + p[2]) + p[3]

    blk = pl.BlockSpec((4, br, c), lambda i, core_ref: (0, i, 0))
    spec = pltpu.PrefetchScalarGridSpec(
        num_scalar_prefetch=1, grid=(r // br,), in_specs=[blk, blk],
        out_specs=pl.BlockSpec((None, br, c), lambda i, core_ref: (core_ref[0], i, 0)))
    return _pcall(body, name=name, grid_spec=spec, out_shape=jax.ShapeDtypeStruct((2, r, c), F32))(core, own, landed)


def _exchange_halves(arrs, *, name):
    n = len(arrs)

    def body(*refs):
        ins, outs = refs[:n], refs[n:2 * n]
        send_sems, recv_sems = refs[2 * n:]
        ax, ay, ac = _place()
        cps = [pltpu.make_async_remote_copy(src_ref=ins[a].at[ac], dst_ref=outs[a].at[ac], send_sem=send_sems.at[a],
                                            recv_sem=recv_sems.at[a], device_id=(ax, ay, 1 - ac),
                                            device_id_type=MESH) for a in range(n)]
        for cp in cps:
            cp.start()
        for a in range(n):
            pltpu.make_async_remote_copy(src_ref=ins[a].at[ac], dst_ref=outs[a].at[1 - ac], send_sem=send_sems.at[a],
                                         recv_sem=recv_sems.at[a], device_id=(ax, ay, ac),
                                         device_id_type=MESH).wait_recv()
        for cp in cps:
            cp.wait_send()

    hbm = pl.BlockSpec(memory_space=pl.ANY)
    return _pcall(
        body, name=name, in_specs=[hbm] * n, out_specs=[hbm] * n,
        out_shape=[jax.ShapeDtypeStruct(a.shape, a.dtype) for a in arrs],
        input_output_aliases={a: a for a in range(n)},
        scratch_shapes=[pltpu.SemaphoreType.DMA((n,))] * 2,
    )(*arrs)


def _add2(a, b, *, name):
    r, c = a.shape
    br = _row_block(r, c, 1024 * 1024)

    def body(a_ref, b_ref, o_ref):
        o_ref[...] = (a_ref[...].astype(F32) + b_ref[...].astype(F32)).astype(BF16)

    blk = pl.BlockSpec((br, c), lambda i: (i, 0))
    return _pcall(body, name=name, grid=(r // br,), in_specs=[blk, blk], out_specs=blk,
                  out_shape=jax.ShapeDtypeStruct((r, c), BF16))(a, b)


def _adam(w, gs, m, v, *, name):
    r, c = w.shape
    br = _row_block(r, c)
    ng = len(gs)
    c1 = 1.0 - ADAM_B1 ** ADAM_STEP
    c2 = 1.0 - ADAM_B2 ** ADAM_STEP

    def body(*refs):
        w_ref, m_ref, v_ref = refs[0], refs[1 + ng], refs[2 + ng]
        outs = refs[3 + ng:]
        g = refs[1][...]
        for k in range(1, ng):
            g = g + refs[1 + k][...]
        mn = ADAM_B1 * m_ref[...] + (1.0 - ADAM_B1) * g
        vn = ADAM_B2 * v_ref[...] + (1.0 - ADAM_B2) * (g * g)
        if ng > 1:
            outs[0][...] = g
        d_out, m_out, v_out = outs[-3:]
        m_out[...] = mn
        v_out[...] = vn
        d_out[...] = -ADAM_LR * ((mn / c1) / (jnp.sqrt(vn / c2) + ADAM_EPS) + ADAM_WD * w_ref[...])

    blk = pl.BlockSpec((br, c), lambda i: (i, 0))
    nout = 4 if ng > 1 else 3
    res = _pcall(body, name=name, grid=(r // br,), in_specs=[blk] * (3 + ng), out_specs=[blk] * nout,
                 out_shape=[jax.ShapeDtypeStruct((r, c), F32)] * nout)(w, *gs, m, v)
    return list(res) if ng > 1 else [gs[0]] + list(res)


def _grad_halves(name, g, ac):
    if name.endswith('_in'):
        n = g.shape[1] // 4
        if name == 'ffn_in':
            assert n == FFN_BK
        order = _ffn_order(g.shape[1]) if name == 'ffn_in' else range(4)
        v = jnp.stack([g[:, b * n:(b + 1) * n] for b in order])
        per = [v[:, :g.shape[0] // 2], v[:, g.shape[0] // 2:]]
    else:
        k4, n = g.shape
        v = g.reshape(4, 2, k4 // 8, n)
        per = [v[:, 0], v[:, 1]]
    first = ac == 0
    return _bf(jnp.where(first, per[0], per[1])), _bf(jnp.where(first, per[1], per[0]))


class _GradReducer:
    def __init__(self):
        self.flight = {}

    @staticmethod
    def _plan(m, sending, refs):
        ax, ay, ac = _place()
        s = 2 * ax + ay
        out = []
        for a in range(m):
            for dx, dy in _CHIP_FLIPS:
                px, py = lax.rem(ax + dx, 2), lax.rem(ay + dy, 2)
                sp = 2 * px + py
                out.append((refs[a].at[sp], refs[m + a].at[s if sending else sp], (px, py, ac)))
        return out

    def start(self, grp, grads):
        ac = lax.axis_index("c")
        names = list(grads)
        halves = [_grad_halves(nm.rstrip('01'), grads[nm], ac) for nm in names]
        theirs = _to_sibling([h[1] for h in halves], name='swap_core_halves_' + grp)
        pair = [_add2(h[0].reshape(-1, b.shape[-1]), b.reshape(-1, b.shape[-1]), name='add_cores').reshape(b.shape)
                for h, b in zip(halves, theirs)]
        m = len(names)
        land = [lax.empty(a.shape, a.dtype) for a in pair]
        sends, recvs, bufs, token = _split_start(pair + land, functools.partial(self._plan, m, True), 3 * m,
                                                 name='scatter_' + grp + '_start')
        self.flight[grp] = (names, sends, recvs, bufs)
        return token

    def finish(self, grp, after):
        names, sends, recvs, bufs = self.flight.pop(grp)
        m = len(names)
        bufs = _split_wait(bufs, sends, recvs, functools.partial(self._plan, m, False), after,
                           name='scatter_' + grp + '_wait')
        core = lax.axis_index("c").astype(jnp.int32).reshape(1)
        sums = [_sum4(p, l, core, name='sum_chips') for p, l in zip(bufs[:m], bufs[m:])]
        both = _exchange_halves(sums, name='gather_core_halves_' + grp)
        return {nm: g.reshape(-1, g.shape[-1]) for nm, g in zip(names, both)}


def _from_shards(name, g):
    _, r, n = g.shape
    if name == 'ffn_in':
        assert n == FFN_BK
        v = g.reshape(4, 2, r // 2, n)
        return jnp.concatenate([v[b] for b in _ffn_order(4 * n)], axis=-1)
    if name == 'ffn_out':
        return g.reshape(4, 2, r // 2, n).transpose(1, 0, 2, 3).reshape(2, 2 * r, n)
    if name in ('even_in', 'odd_in'):
        return jnp.concatenate([g[b] for b in range(4)], axis=-1)
    return g.reshape(4 * r, n)


def kernel(x, c, ctx, c_ctx, mod_w, mod_b, norm_g, ffn_w_in, ffn_w_out, even_w_in, even_w_out, attn_qk_norm_g, attn_sink, hgrn_out_norm_g, hgrn_lb, odd_w_in, odd_w_out, loss_target, m_c_ctx, m_mod_w, m_mod_b, m_norm_g, m_ffn_w_in, m_ffn_w_out, m_even_w_in, m_even_w_out, m_attn_qk_norm_g, m_attn_sink, m_hgrn_out_norm_g, m_hgrn_lb, m_odd_w_in, m_odd_w_out, v_c_ctx, v_mod_w, v_mod_b, v_norm_g, v_ffn_w_in, v_ffn_w_out, v_even_w_in, v_even_w_out, v_attn_qk_norm_g, v_attn_sink, v_hgrn_out_norm_g, v_hgrn_lb, v_odd_w_in, v_odd_w_out):
    d = x.shape[-1]
    lc = ctx.shape[1]
    assert lc == TM and d == 1024
    ax, ay, ac = _place()
    s = 2 * ax + ay
    me = 4 * ax + 2 * ay + ac
    nmod = mod_w.shape[2]

    def pad8(v):
        return jnp.pad(v, ((0, 8 - v.shape[0]), (0, 0)))

    pack = jnp.concatenate([pad8(c), pad8(norm_g.reshape(1, d))], axis=0)
    g1 = _ag8(pack, name='gather_cond')
    c_all = g1[:, 0, :]
    ng = g1[0::2, 8, :].reshape(4, 2, 2, d // 4).transpose(1, 2, 0, 3).reshape(4, d)

    cond_raw = jnp.concatenate([c_all, pad8(c_ctx.reshape(1, d))], axis=0)
    mb_sh = lax.dynamic_slice_in_dim(mod_b, s * nmod, nmod, axis=1).reshape(2, 1, nmod)
    mpart = _mod_fwd(cond_raw, mod_w, mb_sh, name='mod_fwd')
    g3 = _ag8(mpart.reshape(32, nmod), name='gather_mods')
    mods_full = g3[0::2].reshape(4, 2, 16, nmod).transpose(1, 2, 0, 3).reshape(2, 16, 4 * nmod)
    m_lat = lax.dynamic_index_in_dim(mods_full, me, axis=1, keepdims=False)
    mods = jnp.stack([mods_full[:, 8], m_lat], axis=1).reshape(24, d)

    names = ['ffn_in', 'ffn_out', 'even_in', 'even_out', 'odd_in', 'odd_out']
    shards = [_bf(v.reshape(-1, v.shape[-1])) for v in (ffn_w_in, ffn_w_out, even_w_in, even_w_out, odd_w_in, odd_w_out)]
    shards, mods = lax.optimization_barrier((shards, mods))
    reducer = _GradReducer()
    wsrc = _GatheredWeights(dict(zip(names, shards)), reducer)

    lb = _lb_fwd(hgrn_lb, name='hgrn_lower_bound')
    small = dict(gq=jnp.tile(attn_qk_norm_g[0, 0], 2).reshape(1, 128), gk=jnp.tile(attn_qk_norm_g[0, 1], 2).reshape(1, 128),
                 sink=attn_sink[0], gain=hgrn_out_norm_g, lb=lb)
    x0 = jnp.concatenate([ctx[0], x[0]], axis=0) + wsrc.token[0, 0]
    loss_t, dx0, grads, sums = _local_step(x0, loss_target[0], mods, ng, wsrc, small)
    loss = lax.psum(loss_t[0, 0], ("x", "y", "c"))
    grad_x = dx0[None]

    def tile(v):
        return jnp.pad(v, ((0, 8 - v.shape[0]), (0, d - v.shape[1])))

    sums = dict(sums, sink=sums['sink'][:, 0].reshape(1, 8))
    g4 = _ag8(jnp.concatenate([tile(sums[nm]) for nm in PACK_TILES], axis=0), name='gather_row_sums')
    small_g, glb, gmb, dmat = _small_finalize(g4, tile(lb)[0:1], name='small_grads')
    dms = lax.dynamic_slice_in_dim(dmat.transpose(0, 2, 1, 3).reshape(2, 16, 6 * d), s * nmod, nmod, axis=2)
    g_mod_w, dcond = _mod_bwd(cond_raw, dms, mod_w, name='mod_bwd')
    g5 = _ag8(dcond[8:16], name='gather_dcond')
    g_c_ctx = _cctx_grad(g5, c_ctx.reshape(8, d // 8).reshape(1, d), name='c_ctx_grad')

    late = {nm: grads[nm] for nm in ('even_in', 'even_out')}
    late, g_c_ctx = lax.optimization_barrier((late, g_c_ctx))
    token = reducer.start('late', late)
    full = reducer.finish('early', token)

    def upd(wv, gs, mv, vv, name):
        shp = wv.shape
        c2 = shp[-1]
        out = _adam(wv.reshape(-1, c2), [g.reshape(-1, c2) for g in gs], mv.reshape(-1, c2), vv.reshape(-1, c2), name=name)
        return [o.reshape(shp) for o in out]

    res = {}
    res['c_ctx'] = upd(c_ctx.reshape(8, d // 8), [g_c_ctx.reshape(8, d // 8)], m_c_ctx.reshape(8, d // 8), v_c_ctx.reshape(8, d // 8), 'adam_c_ctx')
    res['c_ctx'] = [o.reshape(d) for o in res['c_ctx']]
    res['mod_w'] = upd(mod_w, [g_mod_w], m_mod_w, v_mod_w, 'adam_mod_w')
    res['mod_b'] = upd(mod_b, [gmb.reshape(2, 6 * d)], m_mod_b, v_mod_b, 'adam_mod_b')
    g_ng = lax.dynamic_slice_in_dim(small_g[0:4].reshape(2, 2, d), s * (d // 4), d // 4, axis=2)
    res['norm_g'] = upd(norm_g, [g_ng], m_norm_g, v_norm_g, 'adam_norm_g')
    g_qk = jnp.stack([small_g[4, 0:64], small_g[5, 0:64]]).reshape(1, 2, 64)
    res['attn_qk_norm_g'] = upd(attn_qk_norm_g, [g_qk], m_attn_qk_norm_g, v_attn_qk_norm_g, 'adam_qk_gain')
    res['attn_sink'] = upd(attn_sink, [small_g[7, 0:8].reshape(1, 8)], m_attn_sink, v_attn_sink, 'adam_sink')
    res['hgrn_out_norm_g'] = upd(hgrn_out_norm_g, [small_g[6, 0:128].reshape(1, 128)], m_hgrn_out_norm_g, v_hgrn_out_norm_g, 'adam_head_gain')
    res['hgrn_lb'] = upd(hgrn_lb, [glb[0:2, 0:hgrn_lb.shape[1]]], m_hgrn_lb, v_hgrn_lb, 'adam_hgrn_lb')
    res['odd_w_in'] = upd(odd_w_in, [full['odd_in']], m_odd_w_in, v_odd_w_in, 'adam_odd_in')
    res['odd_w_out'] = upd(odd_w_out, [full['odd_out']], m_odd_w_out, v_odd_w_out, 'adam_odd_out')
    full.update(reducer.finish('mid', res['odd_w_in'][1]))
    g_ffn_in = jnp.concatenate([full['ffn_in0'], full['ffn_in1']], axis=0)
    g_ffn_out = jnp.concatenate([full['ffn_out0'], full['ffn_out1']], axis=0)
    res['ffn_w_in'] = upd(ffn_w_in, [g_ffn_in], m_ffn_w_in, v_ffn_w_in, 'adam_ffn_in')
    res['ffn_w_out'] = upd(ffn_w_out, [g_ffn_out], m_ffn_w_out, v_ffn_w_out, 'adam_ffn_out')
    full.update(reducer.finish('late', res['ffn_w_in'][1]))
    res['even_w_in'] = upd(even_w_in, [full['even_in']], m_even_w_in, v_even_w_in, 'adam_even_in')
    res['even_w_out'] = upd(even_w_out, [full['even_out']], m_even_w_out, v_even_w_out, 'adam_even_out')

    order = ['c_ctx', 'mod_w', 'mod_b', 'norm_g', 'ffn_w_in', 'ffn_w_out', 'even_w_in', 'even_w_out',
             'attn_qk_norm_g', 'attn_sink', 'hgrn_out_norm_g', 'hgrn_lb', 'odd_w_in', 'odd_w_out']
    outs = [loss, grad_x]
    for k in range(4):
        outs += [res[nm][k] for nm in order]
    return tuple(outs)
```

```python
import functools
import math

import numpy as np
import jax
import jax.numpy as jnp
from jax import lax
from jax.experimental import pallas as pl
from jax.experimental.pallas import tpu as pltpu

F32 = jnp.float32
BF16 = jnp.bfloat16
EPS = 1e-6
TM = 256
CHUNK = 64
QB = 256
WINDOW = 128
NEG = -1e30
MESH = pl.DeviceIdType.MESH

ADAM_LR, ADAM_B1, ADAM_B2, ADAM_EPS, ADAM_WD, ADAM_STEP = 0.001, 0.9, 0.999, 1e-08, 0.01, 10


def _pcall(body, **kw):
    return pl.pallas_call(body, **kw)


def _pick(n, cap):
    best = None
    for m in range(128, min(n, cap) + 1, 128):
        if n % m == 0:
            best = m
    assert best is not None, (n, cap)
    return best


def _bf(x):
    return x.astype(BF16)


def _dot(a, b):
    return jnp.dot(_bf(a), _bf(b), preferred_element_type=F32)


def _dot_nt(a, b):
    return lax.dot_general(_bf(a), _bf(b), (((1,), (1,)), ((), ())), preferred_element_type=F32)


def _dot_tn(a, b):
    return lax.dot_general(_bf(a), _bf(b), (((0,), (0,)), ((), ())), preferred_element_type=F32)


def _dot_exact(a, b):
    return jnp.dot(a, b, preferred_element_type=F32, precision=lax.Precision.HIGHEST)


def _sigmoid(x):
    return 1.0 / (1.0 + jnp.exp(-x))


def _iota(shape, dim):
    return lax.broadcasted_iota(jnp.int32, shape, dim)


def _mm_nn(a, b, *, lead=None, out_dtype=F32, name):
    m, k = a.shape
    n = b.shape[-1]
    bm = 1408 if (m % 1408 == 0 and k <= 1024) else (768 if m % 768 == 0 else TM)
    bn = _pick(n, 1024) if n % 512 == 0 else _pick(n, 1664)

    def body(a_ref, b_ref, o_ref):
        o_ref[...] = _dot(a_ref[...], b_ref[...]).astype(o_ref.dtype)

    if lead is None:
        b_spec = pl.BlockSpec((k, bn), lambda i, j: (0, j))
    else:
        b_spec = pl.BlockSpec((None, k, bn), lambda i, j: (lead, 0, j))
    return _pcall(
        body, name=name, grid=(m // bm, n // bn),
        in_specs=[pl.BlockSpec((bm, k), lambda i, j: (i, 0)), b_spec],
        out_specs=pl.BlockSpec((bm, bn), lambda i, j: (i, j)),
        out_shape=jax.ShapeDtypeStruct((m, n), out_dtype),
    )(a, b)


def _mm_nt(a, b, *, lead=None, name):
    m, n = a.shape
    k = b.shape[-2]
    bm = 768 if m % 768 == 0 else TM
    bk = _pick(k, 512)

    def body(a_ref, b_ref, o_ref):
        o_ref[...] = _dot_nt(a_ref[...], b_ref[...])

    if lead is None:
        b_spec = pl.BlockSpec((bk, n), lambda i, j: (j, 0))
    else:
        b_spec = pl.BlockSpec((None, bk, n), lambda i, j: (lead, j, 0))
    return _pcall(
        body, name=name, grid=(m // bm, k // bk),
        in_specs=[pl.BlockSpec((bm, n), lambda i, j: (i, 0)), b_spec],
        out_specs=pl.BlockSpec((bm, bk), lambda i, j: (i, j)),
        out_shape=jax.ShapeDtypeStruct((m, k), F32),
    )(a, b)


def _mm_tn(a, b, *, name):
    t, k = a.shape
    n = b.shape[1]
    bt = 1408 if t % 1408 == 0 else (768 if t % 768 == 0 else TM)
    bk = _pick(k, 1536)
    bn = _pick(n, 1024) if n % 1024 == 0 or n < 1664 else _pick(n, 1664)

    def body(a_ref, b_ref, o_ref):
        @pl.when(pl.program_id(2) == 0)
        def _():
            o_ref[...] = jnp.zeros_like(o_ref)
        o_ref[...] += _dot_tn(a_ref[...], b_ref[...])

    return _pcall(
        body, name=name, grid=(k // bk, n // bn, t // bt),
        in_specs=[pl.BlockSpec((bt, bk), lambda i, j, s: (s, i)),
                  pl.BlockSpec((bt, bn), lambda i, j, s: (s, j))],
        out_specs=pl.BlockSpec((bk, bn), lambda i, j, s: (i, j)),
        out_shape=jax.ShapeDtypeStruct((k, n), F32),
    )(a, b)


def _mod_row(mods_ref, lat, idx):
    return jnp.where(lat, mods_ref[idx + 6:idx + 7, :], mods_ref[idx:idx + 1, :])


def _row_step(t):
    return 768 if t % 768 == 0 else TM


def _row_fwd(x, mods, *, y=None, gate=None, g=None, shift=None, scale=None, name):
    t, d = x.shape
    has_y, has_n = y is not None, g is not None
    rt = _row_step(t)

    def body(*refs):
        refs = list(refs)
        x_ref, mods_ref = refs[0], refs[1]
        pos = 2
        if has_y:
            y_ref = refs[pos]; pos += 1
        if has_n:
            g_ref = refs[pos]; pos += 1
        outs = refs[pos:]
        for sub in range(rt // TM):
            rows = slice(sub * TM, (sub + 1) * TM)
            lat = pl.program_id(0) * (rt // TM) + sub > 0
            x1 = x_ref[rows, :]
            o = 0
            if has_y:
                x1 = x1 + _mod_row(mods_ref, lat, gate) * y_ref[rows, :]
                outs[o][rows, :] = x1; o += 1
            if has_n:
                rs = lax.rsqrt(jnp.mean(x1 * x1, axis=-1, keepdims=True) + EPS)
                hn = x1 * rs * g_ref[...]
                h = hn * (1.0 + _mod_row(mods_ref, lat, scale)) + _mod_row(mods_ref, lat, shift)
                outs[o][rows, :] = h.astype(BF16)

    row = pl.BlockSpec((rt, d), lambda i: (i, 0))
    ins, specs = [x, mods], [row, pl.BlockSpec(mods.shape, lambda i: (0, 0))]
    if has_y:
        ins.append(y); specs.append(row)
    if has_n:
        ins.append(g.reshape(1, d)); specs.append(pl.BlockSpec((1, d), lambda i: (0, 0)))
    out_shape, out_specs = [], []
    if has_y:
        out_shape.append(jax.ShapeDtypeStruct((t, d), F32)); out_specs.append(row)
    if has_n:
        out_shape.append(jax.ShapeDtypeStruct((t, d), BF16)); out_specs.append(row)
    res = _pcall(body, name=name, grid=(t // rt,), in_specs=specs, out_specs=out_specs,
                 out_shape=out_shape)(*ins)
    return res


def _acc_row(ref, r, val):
    ref[r:r + 1, :] += val


def _row_final(x, z, mods, target, *, gate, name):
    t, d = x.shape

    def body(x_ref, mods_ref, z_ref, t_ref, loss_ref, dx_ref, dz_ref, sums_ref):
        i = pl.program_id(0)
        lat = i > 0

        @pl.when(i == 0)
        def _():
            loss_ref[...] = jnp.zeros_like(loss_ref)
            sums_ref[...] = jnp.zeros_like(sums_ref)

        gt = _mod_row(mods_ref, lat, gate)
        zz = z_ref[...]
        yv = x_ref[...] + gt * zz
        keep = jnp.where(lat, 1.0, 0.0).astype(F32)
        diff = (yv - t_ref[...]) * keep
        part = jnp.sum(jnp.sum(diff * diff, axis=0, keepdims=True), axis=1, keepdims=True)
        loss_ref[...] += part * (0.5 / d)
        dy = diff * (1.0 / d)
        dx_ref[...] = dy
        dz_ref[...] = (gt * dy).astype(BF16)
        _acc_row(sums_ref, 6, jnp.sum(dy * zz, axis=0, keepdims=True))

    row = pl.BlockSpec((TM, d), lambda i: (i, 0))
    return _pcall(
        body, name=name, grid=(t // TM,),
        in_specs=[row, pl.BlockSpec(mods.shape, lambda i: (0, 0)), row,
                  pl.BlockSpec((TM, d), lambda i: (jnp.maximum(i - 1, 0), 0))],
        out_specs=[pl.BlockSpec((8, 128), lambda i: (0, 0)), row, row,
                   pl.BlockSpec((8, d), lambda i: (0, 0))],
        out_shape=[jax.ShapeDtypeStruct((8, 128), F32), jax.ShapeDtypeStruct((t, d), F32),
                   jax.ShapeDtypeStruct((t, d), BF16), jax.ShapeDtypeStruct((8, d), F32)],
    )(x, mods, z, target)


def _row_bwd(xn, dxo, dh, mods, g, *, shift, scale, y=None, gate=None, latent_only=False, name):
    t, d = xn.shape
    has_y = y is not None

    def body(*refs):
        refs = list(refs)
        x_ref, dxo_ref, dh_ref, mods_ref, g_ref = refs[:5]
        pos = 5
        if has_y:
            y_ref = refs[pos]; pos += 1
        dx_ref = refs[pos]; pos += 1
        if has_y:
            dy_ref = refs[pos]; pos += 1
        sums_ref = refs[pos]
        i = pl.program_id(0)

        @pl.when(i == 0)
        def _():
            sums_ref[...] = jnp.zeros_like(sums_ref)

        def add_sums(vals, base):
            for r, v in enumerate(vals):
                if v is not None:
                    _acc_row(sums_ref, base + r, v)

        gv = g_ref[...]
        for sub in range(rt // TM):
            rows = slice(sub * TM, (sub + 1) * TM)
            lat = i * (rt // TM) + sub > 0
            x1 = x_ref[rows, :]
            rs = lax.rsqrt(jnp.mean(x1 * x1, axis=-1, keepdims=True) + EPS)
            xh = x1 * rs
            dhv = dh_ref[rows, :]
            dn = dhv * (1.0 + _mod_row(mods_ref, lat, scale))
            dxh = dn * gv
            dx = dxo_ref[rows, :] + rs * (dxh - xh * jnp.mean(dxh * xh, axis=-1, keepdims=True))
            dx_ref[rows, :] = dx
            vals = [jnp.sum(dhv, axis=0, keepdims=True),
                    jnp.sum(dhv * (xh * gv), axis=0, keepdims=True),
                    None,
                    jnp.sum(dn * xh, axis=0, keepdims=True)]
            if has_y:
                dy_ref[rows, :] = (_mod_row(mods_ref, lat, gate) * dx).astype(BF16)
                vals[2] = jnp.sum(dx * y_ref[rows, :], axis=0, keepdims=True)
            if sub == 0:
                pl.when(i == 0)(functools.partial(add_sums, vals, 0))
                pl.when(i > 0)(functools.partial(add_sums, vals, 4))
            else:
                add_sums(vals, 4)

    rt = TM if latent_only else _row_step(t)
    row = pl.BlockSpec((rt, d), lambda i: (i, 0))
    ins = [xn, dxo, dh, mods, g.reshape(1, d)]
    specs = [row, row, row, pl.BlockSpec(mods.shape, lambda i: (0, 0)), pl.BlockSpec((1, d), lambda i: (0, 0))]
    if latent_only:
        out_shape = [jax.ShapeDtypeStruct((t - TM, d), F32)]
        out_specs = [pl.BlockSpec((TM, d), lambda i: (jnp.maximum(i - 1, 0), 0))]
    else:
        out_shape, out_specs = [jax.ShapeDtypeStruct((t, d), F32)], [row]
    if has_y:
        ins.append(y); specs.append(row)
        out_shape.append(jax.ShapeDtypeStruct((t, d), BF16)); out_specs.append(row)
    out_shape.append(jax.ShapeDtypeStruct((8, d), F32))
    out_specs.append(pl.BlockSpec((8, d), lambda i: (0, 0)))
    return _pcall(body, name=name, grid=(t // rt,), in_specs=specs, out_specs=out_specs,
                  out_shape=out_shape)(*ins)


FFN_BK = 1408


FFN_SUB = 256


def _ffn_order(n2):
    nb = n2 // (2 * FFN_BK)
    return [h * nb + j for j in range(nb) for h in (0, 1)]


def _ffn_interleave(w):
    return jnp.concatenate([w[..., b * FFN_BK:(b + 1) * FFN_BK] for b in _ffn_order(w.shape[-1])], axis=-1)


def _ffn_deinterleave(w):
    order = _ffn_order(w.shape[-1])
    return jnp.concatenate([w[..., order.index(b) * FFN_BK:(order.index(b) + 1) * FFN_BK]
                            for b in range(len(order))], axis=-1)


def _big_tile(t):
    return 768 if t % 768 == 0 else TM


def _ffn_in(h, w, *, lead, name):
    t, d = h.shape
    n2 = w.shape[-1]
    bm, bk = _big_tile(t), FFN_BK

    def body(h_ref, w_ref, u_ref, a_ref):
        hb = h_ref[...]
        for c0 in range(0, bk, FFN_SUB):
            c1 = min(c0 + FFN_SUB, bk)
            ug = _dot(hb, w_ref[:, c0:c1]).astype(BF16)
            uu = _dot(hb, w_ref[:, bk + c0:bk + c1]).astype(BF16)
            u_ref[:, c0:c1] = ug
            u_ref[:, bk + c0:bk + c1] = uu
            gv, up = ug.astype(F32), uu.astype(F32)
            a_ref[:, c0:c1] = (gv * _sigmoid(gv) * up).astype(BF16)

    return _pcall(
        body, name=name, grid=(t // bm, n2 // (2 * bk)),
        in_specs=[pl.BlockSpec((bm, d), lambda i, j: (i, 0)),
                  pl.BlockSpec((None, d, 2 * bk), lambda i, j: (lead, 0, j))],
        out_specs=[pl.BlockSpec((bm, 2 * bk), lambda i, j: (i, j)), pl.BlockSpec((bm, bk), lambda i, j: (i, j))],
        out_shape=[jax.ShapeDtypeStruct((t, n2), BF16), jax.ShapeDtypeStruct((t, n2 // 2), BF16)],
    )(h, w)


def _ffn_dx(dz, w_out, u, *, lead, name):
    t, d = dz.shape
    n2 = u.shape[1]
    bm, bk = _big_tile(t), FFN_BK

    def body(dz_ref, w_ref, u_ref, du_ref):
        dzb = dz_ref[...]
        for c0 in range(0, bk, FFN_SUB):
            c1 = min(c0 + FFN_SUB, bk)
            da = _dot_nt(dzb, w_ref[c0:c1, :])
            gv, up = u_ref[:, c0:c1].astype(F32), u_ref[:, bk + c0:bk + c1].astype(F32)
            s = _sigmoid(gv)
            du_ref[:, c0:c1] = (da * up * (s * (1.0 + gv * (1.0 - s)))).astype(BF16)
            du_ref[:, bk + c0:bk + c1] = (da * gv * s).astype(BF16)

    ublk = pl.BlockSpec((bm, 2 * bk), lambda i, j: (i, j))
    return _pcall(
        body, name=name, grid=(t // bm, n2 // (2 * bk)),
        in_specs=[pl.BlockSpec((bm, d), lambda i, j: (i, 0)),
                  pl.BlockSpec((None, bk, d), lambda i, j: (lead, j, 0)), ublk],
        out_specs=ublk, out_shape=jax.ShapeDtypeStruct((t, n2), BF16),
    )(dz, w_out, u)


def _lane(shape):
    return _iota(shape, len(shape) - 1)


def _pair_norm(x, g):
    lo = _lane(x.shape) < 64
    x2 = x * x
    s_lo = jnp.sum(jnp.where(lo, x2, 0.0), axis=-1, keepdims=True)
    s_hi = jnp.sum(jnp.where(lo, 0.0, x2), axis=-1, keepdims=True)
    rs = lax.rsqrt(jnp.where(lo, s_lo, s_hi) * (1.0 / 64) + EPS)
    return x * rs, rs


def _pair_mean(v):
    lo = _lane(v.shape) < 64
    s_lo = jnp.sum(jnp.where(lo, v, 0.0), axis=-1, keepdims=True)
    s_hi = jnp.sum(jnp.where(lo, 0.0, v), axis=-1, keepdims=True)
    return jnp.where(lo, s_lo, s_hi) * (1.0 / 64)


def _rot64(x):
    r1 = pltpu.roll(x, 32, 1)
    r2 = pltpu.roll(x, 96, 1)
    even = ((_lane(x.shape) >> 5) & 1) == 0
    return jnp.where(even, -r2, r1)


def _rope64(x, cos, sin):
    return x * cos + _rot64(x) * sin


def _rope64_t(d, cos, sin):
    return d * cos - _rot64(d * sin)


def _kprep_fwd(p, gk, cos, sin, *, name):
    t = p.shape[0]

    def body(k_ref, g_ref, c_ref, s_ref, o_ref):
        xh, _ = _pair_norm(k_ref[...], None)
        o_ref[...] = _rope64(xh * g_ref[...], c_ref[...], s_ref[...])

    blk = pl.BlockSpec((TM, 128), lambda i: (i, 0))
    return _pcall(
        body, name=name, grid=(t // TM,),
        in_specs=[pl.BlockSpec((TM, 128), lambda i: (i, 4)), pl.BlockSpec((1, 128), lambda i: (0, 0)), blk, blk],
        out_specs=blk, out_shape=jax.ShapeDtypeStruct((t, 128), F32),
    )(p, gk, cos, sin)


def _kprep_bwd(p, gk, cos, sin, dkp, dv, *, name):
    t = p.shape[0]

    def body(k_ref, g_ref, c_ref, s_ref, dkp_ref, dv_ref, o_ref, dg_ref):
        @pl.when(pl.program_id(0) == 0)
        def _():
            dg_ref[...] = jnp.zeros_like(dg_ref)
        xh, rs = _pair_norm(k_ref[...], None)
        dn = _rope64_t(dkp_ref[...], c_ref[...], s_ref[...])
        _acc_row(dg_ref, 0, jnp.sum(dn * xh, axis=0, keepdims=True))
        dxh = dn * g_ref[...]
        o_ref[:, 0:128] = (rs * (dxh - xh * _pair_mean(dxh * xh))).astype(BF16)
        o_ref[:, 128:256] = dv_ref[...].astype(BF16)

    blk = pl.BlockSpec((TM, 128), lambda i: (i, 0))
    return _pcall(
        body, name=name, grid=(t // TM,),
        in_specs=[pl.BlockSpec((TM, 128), lambda i: (i, 4)), pl.BlockSpec((1, 128), lambda i: (0, 0)), blk, blk, blk, blk],
        out_specs=[pl.BlockSpec((TM, 256), lambda i: (i, 0)), pl.BlockSpec((8, 128), lambda i: (0, 0))],
        out_shape=[jax.ShapeDtypeStruct((t, 256), BF16), jax.ShapeDtypeStruct((8, 128), F32)],
    )(p, gk, cos, sin, dkp, dv)


def _attn_common(i, t, lc, kp_ref, v_ref):
    span = QB + 2 * WINDOW
    start = pl.multiple_of(jnp.clip(i * QB - WINDOW, lc, t - span), WINDOW)
    kall = jnp.concatenate([kp_ref[0:lc, :], kp_ref[pl.ds(start, span), :]], axis=0)
    vall = jnp.concatenate([v_ref[0:lc, :], v_ref[pl.ds(start, span), :]], axis=0)
    nk = lc + span
    col = _iota((QB, nk), 1)
    krow = jnp.where(col < lc, col, start + col - lc)
    qrow = i * QB + _iota((QB, nk), 0)
    valid = (col < lc) | ((qrow >= lc) & (krow >= lc) & (jnp.abs(krow - qrow) <= WINDOW))
    lo = _lane(kall.shape) < 64
    kroll, vroll = pltpu.roll(kall, 64, 1), pltpu.roll(vall, 64, 1)
    zero = jnp.zeros_like(kall)
    kvar = [[_bf(jnp.where(lo, kall, zero)), _bf(jnp.where(lo, zero, kroll))],
            [_bf(jnp.where(lo, kroll, zero)), _bf(jnp.where(lo, zero, kall))]]
    vvar = [[_bf(jnp.where(lo, vall, zero)), _bf(jnp.where(lo, zero, vroll))],
            [_bf(jnp.where(lo, vroll, zero)), _bf(jnp.where(lo, zero, vall))]]
    return start, valid, kvar, vvar


def _softmax_sink(s, valid, snk):
    s = jnp.where(valid, s, NEG)
    m = jnp.maximum(jnp.max(s, axis=-1, keepdims=True), snk)
    e = jnp.exp(s - m)
    es = jnp.exp(snk - m)
    inv = 1.0 / (jnp.sum(e, axis=-1, keepdims=True) + es)
    return e * inv, es * inv


def _attn_fwd(p, kp, gq, sink, cos, sin, *, lc, name):
    t = p.shape[0]
    scale = 64 ** -0.5

    def body(q_ref, kp_ref, v_ref, g_ref, sink_ref, c_ref, s_ref, o_ref):
        i = pl.program_id(0)
        _, valid, kvar, vvar = _attn_common(i, t, lc, kp_ref, v_ref)
        cosv, sinv, gv = c_ref[...], s_ref[...], g_ref[...]
        for j in range(4):
            xh, _ = _pair_norm(q_ref[:, 128 * j:128 * j + 128], None)
            q2 = _bf(_rope64(xh * gv, cosv, sinv) * scale)
            acc = jnp.zeros((QB, 128), F32)
            for half in range(2):
                s = _dot_nt(q2, kvar[j // 2][half])
                pr, _ = _softmax_sink(s, valid, sink_ref[2 * j + half])
                acc = acc + _dot(pr, vvar[j // 2][half])
            o_ref[:, 128 * j:128 * j + 128] = acc.astype(BF16)

    qblk = pl.BlockSpec((QB, 128), lambda i: (i, 0))
    return _pcall(
        body, name=name, grid=(t // QB,),
        in_specs=[pl.BlockSpec((QB, 512), lambda i: (i, 0)),
                  pl.BlockSpec((t, 128), lambda i: (0, 0)),
                  pl.BlockSpec((t, 128), lambda i: (0, 5)),
                  pl.BlockSpec((1, 128), lambda i: (0, 0)),
                  pl.BlockSpec(memory_space=pltpu.SMEM), qblk, qblk],
        out_specs=pl.BlockSpec((QB, 512), lambda i: (i, 0)),
        out_shape=jax.ShapeDtypeStruct((t, 512), BF16),
    )(p, kp, p, gq, sink, cos, sin)


def _attn_bwd(p, kp, gq, sink, cos, sin, dmix, *, lc, name):
    t = p.shape[0]
    scale = 64 ** -0.5
    span = QB + 2 * WINDOW

    def body(q_ref, kp_ref, v_ref, g_ref, sink_ref, c_ref, s_ref, do_ref,
             dq_ref, dk_ref, dv_ref, dg_ref, dsink_ref):
        i = pl.program_id(0)

        @pl.when(i == 0)
        def _():
            dk_ref[...] = jnp.zeros_like(dk_ref)
            dv_ref[...] = jnp.zeros_like(dv_ref)
            dg_ref[...] = jnp.zeros_like(dg_ref)
            dsink_ref[...] = jnp.zeros_like(dsink_ref)

        start, valid, kvar, vvar = _attn_common(i, t, lc, kp_ref, v_ref)
        cosv, sinv, gv = c_ref[...], s_ref[...], g_ref[...]
        nk = lc + span
        dkt = [jnp.zeros((64, nk), F32), jnp.zeros((64, nk), F32)]
        dvt = [jnp.zeros((64, nk), F32), jnp.zeros((64, nk), F32)]
        for j in range(4):
            kvh = j // 2
            xh, rs = _pair_norm(q_ref[:, 128 * j:128 * j + 128], None)
            q2 = _bf(_rope64(xh * gv, cosv, sinv) * scale)
            do2 = _bf(do_ref[:, 128 * j:128 * j + 128])
            dq2 = jnp.zeros((QB, 128), F32)
            for half in range(2):
                s = _dot_nt(q2, kvar[kvh][half])
                pr, ps = _softmax_sink(s, valid, sink_ref[2 * j + half])
                dp = _dot_nt(do2, vvar[kvh][half])
                delta = jnp.sum(pr * dp, axis=-1, keepdims=True)
                ds = pr * (dp - delta)
                dsk = jnp.sum(jnp.sum(-ps * delta, axis=0, keepdims=True), axis=1, keepdims=True)
                _acc_row(dsink_ref, 2 * j + half, jnp.broadcast_to(dsk, (1, 128)))
                dq2 = dq2 + _dot(ds, kvar[kvh][half])
                hrows = slice(64 * half, 64 * half + 64)
                dkt[kvh] = dkt[kvh] + _dot_tn(q2, ds)[hrows]
                dvt[kvh] = dvt[kvh] + _dot_tn(do2, pr)[hrows]
            dn = _rope64_t(dq2 * scale, cosv, sinv)
            _acc_row(dg_ref, 0, jnp.sum(dn * xh, axis=0, keepdims=True))
            dxh = dn * gv
            dq_ref[:, 128 * j:128 * j + 128] = (rs * (dxh - xh * _pair_mean(dxh * xh))).astype(BF16)
        dk_all = jnp.concatenate(dkt, axis=0).T
        dv_all = jnp.concatenate(dvt, axis=0).T
        dk_ref[0:lc, :] += dk_all[0:lc]
        dv_ref[0:lc, :] += dv_all[0:lc]
        dk_ref[pl.ds(start, span), :] += dk_all[lc:nk]
        dv_ref[pl.ds(start, span), :] += dv_all[lc:nk]

    qblk = pl.BlockSpec((QB, 128), lambda i: (i, 0))
    full = pl.BlockSpec((t, 128), lambda i: (0, 0))
    small = pl.BlockSpec((8, 128), lambda i: (0, 0))
    return _pcall(
        body, name=name, grid=(t // QB,),
        in_specs=[pl.BlockSpec((QB, 512), lambda i: (i, 0)), full,
                  pl.BlockSpec((t, 128), lambda i: (0, 5)),
                  pl.BlockSpec((1, 128), lambda i: (0, 0)),
                  pl.BlockSpec(memory_space=pltpu.SMEM), qblk, qblk,
                  pl.BlockSpec((QB, 512), lambda i: (i, 0))],
        out_specs=[pl.BlockSpec((QB, 512), lambda i: (i, 0)), full, full, small, small],
        out_shape=[jax.ShapeDtypeStruct((t, 512), BF16), jax.ShapeDtypeStruct((t, 128), F32),
                   jax.ShapeDtypeStruct((t, 128), F32), jax.ShapeDtypeStruct((8, 128), F32),
                   jax.ShapeDtypeStruct((8, 128), F32)],
    )(p, kp, p, gq, sink, cos, sin, dmix)


def _tri(rev):
    r, c = _iota((CHUNK, CHUNK), 0), _iota((CHUNK, CHUNK), 1)
    return (c >= r) if rev else (c <= r)


def _blk_map(nb, rev, backward):
    if not rev:
        return (lambda n: nb - 1 - n) if backward else (lambda n: n)
    if backward:
        return lambda n: jnp.where(n < nb - 1, n + 1, 0)
    return lambda n: jnp.where(n == 0, 0, nb - n)


def _chunk_order(rev, backward, nc=TM // CHUNK):
    order = list(range(nc))
    return order[::-1] if (rev != backward) else order


def _hgrn_gates(qraw, fraw, lb):
    sq = _sigmoid(qraw)
    sf = _sigmoid(fraw)
    f = lb + (1.0 - lb) * sf
    return qraw * sq, 1.0 - f, jnp.log(f), sq, sf, f


HGRN_HP = 4


def _chunk_cumsum(x, rev):
    n = x.shape[0]
    pos = _iota(x.shape, 0) & (CHUNK - 1)
    s = 1
    while s < CHUNK:
        if rev:
            x = x + jnp.where(pos < CHUNK - s, pltpu.roll(x, n - s, 0), 0.0)
        else:
            x = x + jnp.where(pos >= s, pltpu.roll(x, s, 0), 0.0)
        s *= 2
    return x


def _block_terms(lf, rev):
    b = _chunk_cumsum(lf, rev)
    mid, last = (CHUNK // 2 - 1, 0) if rev else (CHUNK // 2, CHUNK - 1)

    def chunk_row(off):
        return jnp.concatenate([jnp.broadcast_to(b[c * CHUNK + off:c * CHUNK + off + 1, :], (CHUNK, b.shape[1]))
                                for c in range(TM // CHUNK)], axis=0)

    r, bl = chunk_row(mid), chunk_row(last)
    return _tri(rev), jnp.exp(b - r), jnp.exp(r - b), jnp.exp(b), jnp.exp(bl - b), jnp.exp(bl)


def _headnorm_apply(o, gv, gain):
    n = o * lax.rsqrt(jnp.mean(o * o, axis=-1, keepdims=True) + EPS)
    if gain is not None:
        n = n * gain
    return (n * (gv * _sigmoid(gv))).astype(BF16)


def _headnorm_grad(o, gv, dy, gain):
    rs = lax.rsqrt(jnp.mean(o * o, axis=-1, keepdims=True) + EPS)
    xh = o * rs
    n = xh * gain if gain is not None else xh
    sg = _sigmoid(gv)
    dn = dy * (gv * sg)
    dg = (dy * n * (sg * (1.0 + gv * (1.0 - sg)))).astype(BF16)
    dgain = jnp.sum(dn * xh, axis=0, keepdims=True)
    dxh = dn * gain if gain is not None else dn
    return rs * (dxh - xh * jnp.mean(dxh * xh, axis=-1, keepdims=True)), dg, dgain


def _hgrn_cols(bmap, n2, c0):
    return [pl.BlockSpec((TM, 256), lambda h, n, b=b: (bmap(n), c0 // 2 + h * n2 + b)) for b in range(n2)]


def _head_cols(refs, hh):
    return refs[hh // 2][:, 128 * (hh % 2):128 * (hh % 2) + 128]


def _hgrn_fwd(p, lb, *, rev, name, ofw=None, gain=None):
    t = p.shape[0]
    nb, nc = t // TM, TM // CHUNK
    bmap = _blk_map(nb, rev, False)
    fcol = 14 if rev else 10
    fused = ofw is not None

    n2 = HGRN_HP // 2

    def body(*refs):
        q_refs, f_refs, v_refs, lb_ref = refs[:n2], refs[n2:2 * n2], refs[2 * n2:3 * n2], refs[3 * n2]
        rest = refs[3 * n2 + 1:]
        if fused:
            ofw_ref, g_refs, gain_ref = rest[0], rest[1:1 + n2], rest[1 + n2]
            o_ref, sh_ref, mix_ref, st = rest[2 + n2:]
        else:
            o_ref, sh_ref, st = rest

        @pl.when(pl.program_id(1) == 0)
        def _():
            st[...] = jnp.zeros_like(st)
        for hh in range(HGRN_HP):
            ln = slice(128 * hh, 128 * hh + 128)
            q, k, lf, _, _, _ = _hgrn_gates(_head_cols(q_refs, hh), _head_cols(f_refs, hh), lb_ref[:, ln])
            tri, eq, ek, ei, eki, eb = _block_terms(lf, rev)
            qe, ke, qi, ki, vb = _bf(q * eq), _bf(k * ek), _bf(q * ei), _bf(k * eki), _bf(_head_cols(v_refs, hh))
            intra = []
            for cc in range(nc):
                rows = slice(cc * CHUNK, (cc + 1) * CHUNK)
                a = jnp.where(tri, _dot_nt(qe[rows], ke[rows]), 0.0)
                intra.append(_dot(a, vb[rows]))
            s = st[hh]
            for cc in _chunk_order(rev, False):
                rows = slice(cc * CHUNK, (cc + 1) * CHUNK)
                sh_ref[hh, cc] = s
                o_ref[rows, ln] = intra[cc] + _dot_nt(qi[rows], s)
                s = s * eb[cc * CHUNK:cc * CHUNK + 1, :] + _dot_tn(vb[rows], ki[rows])
            st[hh] = s
            if fused:
                osum = o_ref[:, ln] + ofw_ref[:, ln]
                o_ref[:, ln] = osum
                mix_ref[:, ln] = _headnorm_apply(osum, _head_cols(g_refs, hh), gain_ref[...])

    hp, wd = HGRN_HP, 128 * HGRN_HP
    col = functools.partial(_hgrn_cols, bmap, n2)
    oblk = pl.BlockSpec((TM, wd), lambda h, n: (bmap(n), h))
    ins = [p] * (3 * n2) + [lb]
    specs = col(6) + col(fcol) + col(18) + [pl.BlockSpec((1, wd), lambda h, n: (0, h))]
    out_specs = [oblk, pl.BlockSpec((hp, nc, 128, 128), lambda h, n: (h, bmap(n), 0, 0))]
    out_shape = [jax.ShapeDtypeStruct((t, 512), F32), jax.ShapeDtypeStruct((4, t // CHUNK, 128, 128), F32)]
    if fused:
        ins += [ofw] + [p] * n2 + [gain]
        specs += [oblk] + col(22) + [pl.BlockSpec((1, 128), lambda h, n: (0, 0))]
        out_specs.append(oblk)
        out_shape.append(jax.ShapeDtypeStruct((t, 512), BF16))
    return _pcall(body, name=name, grid=(4 // hp, nb), in_specs=specs, out_specs=out_specs, out_shape=out_shape,
                  scratch_shapes=[pltpu.VMEM((hp, 128, 128), F32)])(*ins)


def _hgrn_bwd(p, lb, sh, do, prev, *, rev, name, head=None):
    t = p.shape[0]
    nb, nc = t // TM, TM // CHUNK
    bmap = _blk_map(nb, rev, True)
    fcol = 14 if rev else 10
    has_prev = prev is not None
    odt = BF16 if has_prev else F32
    fused = head is not None

    n2 = HGRN_HP // 2

    def body(*refs):
        refs = list(refs)
        q_refs, f_refs, v_refs = refs[:n2], refs[n2:2 * n2], refs[2 * n2:3 * n2]
        lb_ref, sh_ref = refs[3 * n2], refs[3 * n2 + 1]
        pos = 3 * n2 + 2
        if fused:
            osum_ref, g_refs, dmix_ref, gain_ref = refs[pos], refs[pos + 1:pos + 1 + n2], refs[pos + 1 + n2], refs[pos + 2 + n2]
            pos += 3 + n2
        else:
            do_ref = refs[pos]
            pos += 1
        if has_prev:
            pq_ref, pv_ref = refs[pos], refs[pos + 1]
            pos += 2
        dq_ref, df_ref, dv_ref, dlb_ref = refs[pos:pos + 4]
        pos += 4
        if fused:
            do_out, dg_ref, dgain_ref = refs[pos:pos + 3]
            pos += 3
        dst = refs[pos]

        @pl.when(pl.program_id(1) == 0)
        def _():
            dst[...] = jnp.zeros_like(dst)
            dlb_ref[...] = jnp.zeros_like(dlb_ref)

        if fused:
            @pl.when((pl.program_id(0) == 0) & (pl.program_id(1) == 0))
            def _():
                dgain_ref[...] = jnp.zeros_like(dgain_ref)

        cat = functools.partial(jnp.concatenate, axis=0)
        for hh in range(HGRN_HP):
            ln = slice(128 * hh, 128 * hh + 128)
            lbv = lb_ref[:, ln]
            qraw, fraw = _head_cols(q_refs, hh), _head_cols(f_refs, hh)
            q, k, lf, sq, sf, f = _hgrn_gates(qraw, fraw, lbv)
            tri, eq, ek, ei, eki, eb = _block_terms(lf, rev)
            qe, ke, qi, ki = q * eq, k * ek, q * ei, k * eki
            if fused:
                dov, dg, dgain = _headnorm_grad(osum_ref[:, ln], _head_cols(g_refs, hh), dmix_ref[:, ln], gain_ref[...])
                do_out[:, ln] = dov
                dg_ref[:, ln] = dg
                _acc_row(dgain_ref, 0, dgain)
            else:
                dov = do_ref[:, ln]
            qeb, keb, qib, kib, vb, dob = _bf(qe), _bf(ke), _bf(qi), _bf(ki), _bf(_head_cols(v_refs, hh)), _bf(dov)
            dv, dqe, dke, dqi = [None] * nc, [None] * nc, [None] * nc, [None] * nc
            for cc in range(nc):
                rows = slice(cc * CHUNK, (cc + 1) * CHUNK)
                a = jnp.where(tri, _dot_nt(qeb[rows], keb[rows]), 0.0)
                da = jnp.where(tri, _dot_nt(dob[rows], vb[rows]), 0.0)
                dv[cc] = _dot_tn(a, dob[rows])
                dqe[cc], dke[cc] = _dot(da, keb[rows]), _dot_tn(da, qeb[rows])
                dqi[cc] = _dot(dob[rows], sh_ref[hh, cc])
            dki, dbl = [None] * nc, [None] * nc
            ds = dst[hh]
            for cc in _chunk_order(rev, True):
                rows = slice(cc * CHUNK, (cc + 1) * CHUNK)
                ebc = eb[cc * CHUNK:cc * CHUNK + 1, :]
                dv[cc] = dv[cc] + _dot_nt(kib[rows], ds)
                dki[cc] = _dot(vb[rows], ds)
                dbl[cc] = jnp.broadcast_to(jnp.sum(dki[cc] * ki[rows], axis=0, keepdims=True)
                                           + jnp.sum(ds * sh_ref[hh, cc], axis=0, keepdims=True) * ebc, (CHUNK, 128))
                ds = ds * ebc + _dot_tn(dob[rows], qib[rows])
            dst[hh] = ds
            dqe, dke, dqi, dki, dv, dbl = cat(dqe), cat(dke), cat(dqi), cat(dki), cat(dv), cat(dbl)
            dq = dqe * eq + dqi * ei
            dk = dke * ek + dki * eki
            last = 0 if rev else CHUNK - 1
            db = dqe * qe - dke * ke + dqi * qi - dki * ki
            db = db + jnp.where((_iota(db.shape, 0) & (CHUNK - 1)) == last, dbl, 0.0)
            dlf = _chunk_cumsum(db, not rev)
            dqr = dq * (sq * (1.0 + qraw * (1.0 - sq)))
            dfv = dlf / f - dk
            dfr = dfv * (1.0 - lbv) * (sf * (1.0 - sf))
            dlb_ref[:, ln] += jnp.sum(dfv * (1.0 - sf), axis=0, keepdims=True)
            if has_prev:
                dqr = dqr + pq_ref[:, ln]
                dv = dv + pv_ref[:, ln]
            dq_ref[:, ln] = dqr.astype(odt)
            df_ref[:, ln] = dfr.astype(odt)
            dv_ref[:, ln] = dv.astype(odt)

    hp, wd = HGRN_HP, 128 * HGRN_HP
    col = functools.partial(_hgrn_cols, bmap, n2)
    oblk = pl.BlockSpec((TM, wd), lambda h, n: (bmap(n), h))
    ins = [p] * (3 * n2) + [lb, sh]
    specs = col(6) + col(fcol) + col(18) + [pl.BlockSpec((1, wd), lambda h, n: (0, h)),
                                            pl.BlockSpec((hp, nc, 128, 128), lambda h, n: (h, bmap(n), 0, 0))]
    if fused:
        osum, dmix, gain = head
        ins += [osum] + [p] * n2 + [dmix, gain]
        specs += [oblk] + col(22) + [pl.BlockSpec((TM, wd), lambda h, n: (bmap(n), 4 // hp + h)),
                                     pl.BlockSpec((1, 128), lambda h, n: (0, 0))]
    else:
        ins.append(do); specs.append(oblk)
    if has_prev:
        ins += list(prev); specs += [oblk, oblk]
    out_specs = [oblk, oblk, oblk, pl.BlockSpec((1, wd), lambda h, n: (0, h))]
    out_shape = [jax.ShapeDtypeStruct((t, 512), odt)] * 3 + [jax.ShapeDtypeStruct((1, 512), F32)]
    if fused:
        out_specs += [oblk, oblk, pl.BlockSpec((8, 128), lambda h, n: (0, 0))]
        out_shape += [jax.ShapeDtypeStruct((t, 512), F32), jax.ShapeDtypeStruct((t, 512), BF16),
                      jax.ShapeDtypeStruct((8, 128), F32)]
    return _pcall(body, name=name, grid=(4 // hp, nb), in_specs=specs, out_specs=out_specs, out_shape=out_shape,
                  scratch_shapes=[pltpu.VMEM((hp, 128, 128), F32)])(*ins)


def _rope256(x, cos, sin):
    x1, x2 = x[:, 0:128], x[:, 128:256]
    return jnp.concatenate([x1 * cos - x2 * sin, x2 * cos + x1 * sin], axis=-1)


def _rope256_t(d, cos, sin):
    d1, d2 = d[:, 0:128], d[:, 128:256]
    return jnp.concatenate([d1 * cos + d2 * sin, d2 * cos - d1 * sin], axis=-1)


RET_DK, RET_DV, RET_H = 256, 512, 4
RET_KSCALE = RET_DK ** -0.5
RCH = TM
RET_HP = 4


def _ret_terms(lg, rev):
    r, c = _iota((RCH, RCH), 0), _iota((RCH, RCH), 1)
    rel = ((c - r) if rev else (r - c)).astype(F32)
    dmat = jnp.where(rel >= 0, jnp.exp(lg[:, 0:1] * jnp.maximum(rel, 0.0)), 0.0)
    pos = _iota((RCH, 1), 0).astype(F32)
    cnt = (RCH - pos) if rev else (pos + 1.0)
    ei = jnp.exp(lg * cnt)
    eki = jnp.exp(lg * (RCH - cnt))
    eb = jnp.exp(lg * float(RCH))
    return dmat, ei, eki, eb


def _ret_fwd(p, lgt, cos, sin, *, rev, name, ofw=None):
    t = p.shape[0]
    nb, nc = t // TM, TM // RCH
    bmap = _blk_map(nb, rev, False)
    fused = ofw is not None

    def body(*refs):
        q_ref, k_ref, v_ref, lg_ref, c_ref, s_ref = refs[:6]
        if fused:
            ofw_ref, g_ref, o_ref, sh_ref, mix_ref, st = refs[6:]
        else:
            o_ref, sh_ref, st = refs[6:]

        @pl.when(pl.program_id(1) == 0)
        def _():
            st[...] = jnp.zeros_like(st)
        for hh in range(RET_HP):
            qc, vc = slice(RET_DK * hh, RET_DK * (hh + 1)), slice(RET_DV * hh, RET_DV * (hh + 1))
            dmat, ei, eki, eb = _ret_terms(lg_ref[hh], rev)
            for cc in _chunk_order(rev, False, nc):
                rows = slice(cc * RCH, (cc + 1) * RCH)
                cosv, sinv = c_ref[rows, :], s_ref[rows, :]
                q = _rope256(q_ref[rows, qc].astype(F32), cosv, sinv)
                k = _rope256(k_ref[rows, qc].astype(F32), cosv, sinv) * RET_KSCALE
                v = v_ref[rows, vc]
                s0 = st[hh]
                sh_ref[hh, cc] = s0.astype(BF16)
                a = _dot_nt(q, k) * dmat
                o = _dot(a, v) + _dot_nt(q * ei, s0)
                st[hh] = s0 * eb + _dot_tn(v, k * eki)
                if fused:
                    o = o + ofw_ref[rows, vc]
                    mix_ref[rows, vc] = _headnorm_apply(o, g_ref[rows, vc].astype(F32), None)
                o_ref[rows, vc] = o

    hp = RET_HP
    tab = pl.BlockSpec((TM, 128), lambda h, n: (bmap(n), 0))
    oblk = pl.BlockSpec((TM, hp * RET_DV), lambda h, n: (bmap(n), h))
    ins = [p, p, p, lgt, cos, sin]
    specs = [pl.BlockSpec((TM, hp * RET_DK), lambda h, n: (bmap(n), h)),
             pl.BlockSpec((TM, hp * RET_DK), lambda h, n: (bmap(n), RET_H // hp + h)),
             pl.BlockSpec((TM, hp * RET_DV), lambda h, n: (bmap(n), RET_H // hp + h)),
             pl.BlockSpec((hp, 1, RET_DK), lambda h, n: (h, 0, 0)), tab, tab]
    out_specs = [oblk, pl.BlockSpec((hp, nc, RET_DV, RET_DK), lambda h, n: (h, bmap(n), 0, 0))]
    out_shape = [jax.ShapeDtypeStruct((t, RET_H * RET_DV), F32),
                 jax.ShapeDtypeStruct((RET_H, t // RCH, RET_DV, RET_DK), BF16)]
    if fused:
        ins += [ofw, p]
        specs += [oblk, pl.BlockSpec((TM, hp * RET_DV), lambda h, n: (bmap(n), 2 * RET_H // hp + h))]
        out_specs.append(oblk)
        out_shape.append(jax.ShapeDtypeStruct((t, RET_H * RET_DV), BF16))
    return _pcall(body, name=name, grid=(RET_H // hp, nb), in_specs=specs, out_specs=out_specs, out_shape=out_shape,
                  scratch_shapes=[pltpu.VMEM((hp, RET_DV, RET_DK), F32)])(*ins)


def _ret_bwd(p, lgt, cos, sin, sh, do, prev, *, rev, name, head=None):
    t = p.shape[0]
    nb, nc = t // TM, TM // RCH
    bmap = _blk_map(nb, rev, True)
    has_prev = prev is not None
    odt = BF16 if has_prev else F32
    fused = head is not None

    def body(*refs):
        refs = list(refs)
        q_ref, k_ref, v_ref, lg_ref, c_ref, s_ref, sh_ref = refs[:7]
        if fused:
            osum_ref, g_ref, dmix_ref = refs[7:10]
            pos = 10
        else:
            do_ref = refs[7]
            pos = 8
        if has_prev:
            pq_ref, pk_ref, pv_ref = refs[pos:pos + 3]
            pos += 3
        dq_ref, dk_ref, dv_ref = refs[pos:pos + 3]
        pos += 3
        if fused:
            do_out, dg_ref = refs[pos:pos + 2]
            pos += 2
        dst = refs[pos]

        @pl.when(pl.program_id(1) == 0)
        def _():
            dst[...] = jnp.zeros_like(dst)

        for hh in range(RET_HP):
            qc, vc = slice(RET_DK * hh, RET_DK * (hh + 1)), slice(RET_DV * hh, RET_DV * (hh + 1))
            dmat, ei, eki, eb = _ret_terms(lg_ref[hh], rev)
            for cc in _chunk_order(rev, True, nc):
                rows = slice(cc * RCH, (cc + 1) * RCH)
                cosv, sinv = c_ref[rows, :], s_ref[rows, :]
                q = _rope256(q_ref[rows, qc].astype(F32), cosv, sinv)
                k = _rope256(k_ref[rows, qc].astype(F32), cosv, sinv) * RET_KSCALE
                v = v_ref[rows, vc]
                if fused:
                    dov, dg, _ = _headnorm_grad(osum_ref[rows, vc], g_ref[rows, vc].astype(F32), dmix_ref[rows, vc], None)
                    do_out[rows, vc] = dov
                    dg_ref[rows, vc] = dg
                else:
                    dov = do_ref[rows, vc]
                s0 = sh_ref[hh, cc]
                dsc = dst[hh]
                qi, ki = q * ei, k * eki
                a = _dot_nt(q, k) * dmat
                da = _dot_nt(dov, v) * dmat
                dv = _dot_tn(a, dov) + _dot_nt(ki, dsc)
                dqs = _dot(da, k) + _dot(dov, s0) * ei
                dks = _dot_tn(da, q) + _dot(v, dsc) * eki
                dst[hh] = dsc * eb + _dot_tn(dov, qi)
                dq = _rope256_t(dqs, cosv, sinv)
                dk = _rope256_t(dks * RET_KSCALE, cosv, sinv)
                if has_prev:
                    dq = dq + pq_ref[rows, qc]
                    dk = dk + pk_ref[rows, qc]
                    dv = dv + pv_ref[rows, vc]
                dq_ref[rows, qc] = dq.astype(odt)
                dk_ref[rows, qc] = dk.astype(odt)
                dv_ref[rows, vc] = dv.astype(odt)

    hp = RET_HP
    tab = pl.BlockSpec((TM, 128), lambda h, n: (bmap(n), 0))
    qblk = pl.BlockSpec((TM, hp * RET_DK), lambda h, n: (bmap(n), h))
    vblk = pl.BlockSpec((TM, hp * RET_DV), lambda h, n: (bmap(n), h))
    ins = [p, p, p, lgt, cos, sin, sh]
    specs = [qblk, pl.BlockSpec((TM, hp * RET_DK), lambda h, n: (bmap(n), RET_H // hp + h)),
             pl.BlockSpec((TM, hp * RET_DV), lambda h, n: (bmap(n), RET_H // hp + h)),
             pl.BlockSpec((hp, 1, RET_DK), lambda h, n: (h, 0, 0)), tab, tab,
             pl.BlockSpec((hp, nc, RET_DV, RET_DK), lambda h, n: (h, bmap(n), 0, 0))]
    if fused:
        osum, dmix = head
        ins += [osum, p, dmix]
        specs += [vblk, pl.BlockSpec((TM, hp * RET_DV), lambda h, n: (bmap(n), 2 * RET_H // hp + h)), vblk]
    else:
        ins.append(do); specs.append(vblk)
    if has_prev:
        ins += list(prev); specs += [qblk, qblk, vblk]
    out_specs = [qblk, qblk, vblk]
    out_shape = [jax.ShapeDtypeStruct((t, RET_H * RET_DK), odt), jax.ShapeDtypeStruct((t, RET_H * RET_DK), odt),
                 jax.ShapeDtypeStruct((t, RET_H * RET_DV), odt)]
    if fused:
        out_specs += [vblk, vblk]
        out_shape += [jax.ShapeDtypeStruct((t, RET_H * RET_DV), F32), jax.ShapeDtypeStruct((t, RET_H * RET_DV), BF16)]
    return _pcall(body, name=name, grid=(RET_H // hp, nb), in_specs=specs, out_specs=out_specs, out_shape=out_shape,
                  scratch_shapes=[pltpu.VMEM((hp, RET_DV, RET_DK), F32)])(*ins)


def _rope_tables(lc, l):
    tt = jnp.arange(l)
    row, colp = (tt // 64).astype(F32), (tt % 64).astype(F32)
    inv = 10000.0 ** (-jnp.arange(16, dtype=F32) / 16)
    ang = jnp.concatenate([row[:, None] * inv, colp[:, None] * inv], axis=-1)
    ang = jnp.concatenate([jnp.zeros((lc, 32), F32), ang], axis=0)
    acos, asin = jnp.tile(jnp.cos(ang), (1, 4)), jnp.tile(jnp.sin(ang), (1, 4))
    theta = 1.0 / (10000.0 ** jnp.linspace(0.0, 1.0, 128, dtype=F32))
    rang = jnp.arange(l, dtype=F32)[:, None] * theta
    rang = jnp.concatenate([jnp.zeros((lc, 128), F32), rang], axis=0)
    return acos, asin, jnp.cos(rang), jnp.sin(rang)


class _Weights:
    def __init__(self, w):
        self.w = w

    def first(self, after):
        return self.w

    def rest_landed(self, after):
        pass

    def rest(self, after):
        return self.w

    def send_grads(self, grp, grads):
        return jnp.zeros((8, 128), F32)


def _local_step(x0, target, mods, ng, wsrc, small):
    t, d = x0.shape
    l = target.shape[0]
    lc = t - l
    acos, asin, rcos, rsin = _rope_tables(lc, l)
    lg_fw = jnp.log(1.0 - 2.0 ** (-5.0 - jnp.arange(RET_H, dtype=F32)))
    lgt_fw = jnp.broadcast_to(lg_fw[:, None, None], (RET_H, 1, RET_DK))
    lgt_bw = jnp.broadcast_to(lg_fw[::-1][:, None, None], (RET_H, 1, RET_DK))
    gq, gk, sink, gain, lb = small['gq'], small['gk'], small['sink'], small['gain'], small['lb']

    (h1,) = _row_fwd(x0, mods, g=ng[0], shift=0, scale=1, name='l0_norm1')
    w = wsrc.first(h1)
    p0 = _mm_nn(h1, w['even_in'], name='l0_in')
    kp = _kprep_fwd(p0, gk, acos, asin, name='l0_kprep')
    att = _attn_fwd(p0, kp, gq, sink, acos, asin, lc=lc, name='l0_attn')
    hof, hsf = _hgrn_fwd(p0, lb, rev=False, name='l0_hgrn_f')
    wsrc.rest_landed(hof)
    hos, hsb, bmix = _hgrn_fwd(p0, lb, rev=True, name='l0_hgrn_b', ofw=hof, gain=gain)
    mix0 = jnp.concatenate([att, bmix], axis=1)
    y0 = _mm_nn(mix0, w['even_out'], name='l0_out')
    x1, h2 = _row_fwd(x0, mods, y=y0, gate=2, g=ng[1], shift=3, scale=4, name='l0_norm2')
    w = dict(w, **wsrc.rest(h2))
    u0, a0 = _ffn_in(h2, w['ffn_in'], lead=0, name='ffn_in')
    z0 = _mm_nn(a0, w['ffn_out'], lead=0, name='ffn_out')
    x2, h3 = _row_fwd(x1, mods, y=z0, gate=5, g=ng[2], shift=12, scale=13, name='l1_norm1')
    p1 = _mm_nn(h3, w['odd_in'], out_dtype=BF16, name='l1_in')
    rof, rsf = _ret_fwd(p1, lgt_fw, rcos, rsin, rev=False, name='l1_ret_f')
    ros, rsb, mix1 = _ret_fwd(p1, lgt_bw, rcos, rsin, rev=True, name='l1_ret_b', ofw=rof)
    y1 = _mm_nn(mix1, w['odd_out'], name='l1_out')
    x3, h4 = _row_fwd(x2, mods, y=y1, gate=14, g=ng[3], shift=15, scale=16, name='l1_norm2')
    u1, a1 = _ffn_in(h4, w['ffn_in'], lead=1, name='ffn_in')
    z1 = _mm_nn(a1, w['ffn_out'], lead=1, name='ffn_out')
    loss, dx4, dz1, s_fin = _row_final(x3, z1, mods, target, gate=17, name='loss')

    du1 = _ffn_dx(dz1, w['ffn_out'], u1, lead=1, name='ffn_out_dx')
    g_ffn_out1 = _mm_tn(a1, dz1, name='ffn_out_dw')
    dh4 = _mm_nt(du1, w['ffn_in'], lead=1, name='ffn_in_dx')
    g_ffn_in1 = _mm_tn(h4, du1, name='ffn_in_dw')
    dx3, dy1, s_l1n2 = _row_bwd(x3, dx4, dh4, mods, ng[3], shift=15, scale=16, y=y1, gate=14, name='l1_norm2_bwd')
    dmix1 = _mm_nt(dy1, w['odd_out'], name='l1_out_dx')
    g_odd_out = _mm_tn(mix1, dy1, name='l1_out_dw')
    rdq, rdk, rdv, rdo, rdg = _ret_bwd(p1, lgt_fw, rcos, rsin, rsf, None, None, rev=False, name='l1_ret_f_bwd',
                                       head=(ros, dmix1))
    rdq, rdk, rdv = _ret_bwd(p1, lgt_bw, rcos, rsin, rsb, rdo, (rdq, rdk, rdv), rev=True, name='l1_ret_b_bwd')
    dp1 = jnp.concatenate([rdq, rdk, rdv, rdg], axis=1)
    dh3 = _mm_nt(dp1, w['odd_in'], name='l1_in_dx')
    g_odd_in = _mm_tn(h3, dp1, name='l1_in_dw')
    mods = mods + wsrc.send_grads('early', dict(ffn_in1=g_ffn_in1, ffn_out1=g_ffn_out1, odd_in=g_odd_in,
                                                odd_out=g_odd_out))[0, 0]
    dx2, dz0, s_l1n1 = _row_bwd(x2, dx3, dh3, mods, ng[2], shift=12, scale=13, y=z0, gate=5, name='l1_norm1_bwd')
    du0 = _ffn_dx(dz0, w['ffn_out'], u0, lead=0, name='ffn_out_dx')
    g_ffn_out0 = _mm_tn(a0, dz0, name='ffn_out_dw')
    dh2 = _mm_nt(du0, w['ffn_in'], lead=0, name='ffn_in_dx')
    g_ffn_in0 = _mm_tn(h2, du0, name='ffn_in_dw')
    mods = mods + wsrc.send_grads('mid', dict(ffn_in0=g_ffn_in0, ffn_out0=g_ffn_out0))[0, 0]
    dx1, dy0, s_l0n2 = _row_bwd(x1, dx2, dh2, mods, ng[1], shift=3, scale=4, y=y0, gate=2, name='l0_norm2_bwd')
    dmix0 = _mm_nt(dy0, w['even_out'], name='l0_out_dx')
    g_even_out = _mm_tn(mix0, dy0, name='l0_out_dw')
    hq, hff, hv, dlb_f, hdo, hdg, s_gain = _hgrn_bwd(p0, lb, hsf, None, None, rev=False, name='l0_hgrn_f_bwd',
                                                     head=(hos, dmix0, gain))
    hq, hfb, hv, dlb_b = _hgrn_bwd(p0, lb, hsb, hdo, (hq, hv), rev=True, name='l0_hgrn_b_bwd')
    adq, dkp, adv, s_gq, s_sink = _attn_bwd(p0, kp, gq, sink, acos, asin, dmix0, lc=lc, name='l0_attn_bwd')
    dkv, s_gk = _kprep_bwd(p0, gk, acos, asin, dkp, adv, name='l0_kprep_bwd')
    dp0 = jnp.concatenate([adq, dkv, hq, _bf(hff), hfb, hv, hdg], axis=1)
    dh1 = _mm_nt(dp0, w['even_in'], name='l0_in_dx')
    g_even_in = _mm_tn(h1, dp0, name='l0_in_dw')
    dx0, s_l0n1 = _row_bwd(x0, dx1, dh1, mods, ng[0], shift=0, scale=1, latent_only=True, name='l0_norm1_bwd')

    grads = dict(ffn_in0=g_ffn_in0, ffn_in1=g_ffn_in1, ffn_out0=g_ffn_out0, ffn_out1=g_ffn_out1,
                 even_in=g_even_in, even_out=g_even_out, odd_in=g_odd_in, odd_out=g_odd_out)
    sums = dict(fin=s_fin, l1n2=s_l1n2, l1n1=s_l1n1, l0n2=s_l0n2, l0n1=s_l0n1, gain=s_gain, gq=s_gq, gk=s_gk,
                sink=s_sink, dlb_f=dlb_f, dlb_b=dlb_b)
    return loss, dx0, grads, sums


def _place():
    return lax.axis_index("x"), lax.axis_index("y"), lax.axis_index("c")


def _ag8(blk, *, name):
    r, c = blk.shape
    flips = [(dx, dy, dc) for dx in (0, 1) for dy in (0, 1) for dc in (0, 1) if (dx, dy, dc) != (0, 0, 0)]

    def body(x_ref, out_ref, send_sems, recv_sems, local_sem):
        ax, ay, ac = _place()
        me = 4 * ax + 2 * ay + ac
        mine = pltpu.make_async_copy(x_ref, out_ref.at[me], local_sem)
        mine.start()
        sent = []
        for k, (dx, dy, dc) in enumerate(flips):
            peer = (lax.rem(ax + dx, 2), lax.rem(ay + dy, 2), lax.rem(ac + dc, 2))
            cp = pltpu.make_async_remote_copy(src_ref=x_ref, dst_ref=out_ref.at[me], send_sem=send_sems.at[k],
                                              recv_sem=recv_sems.at[k], device_id=peer, device_id_type=MESH)
            cp.start()
            sent.append((cp, 4 * peer[0] + 2 * peer[1] + peer[2]))
        for k, (cp, pidx) in enumerate(sent):
            pltpu.make_async_remote_copy(src_ref=x_ref, dst_ref=out_ref.at[pidx], send_sem=send_sems.at[k],
                                         recv_sem=recv_sems.at[k], device_id=(ax, ay, ac),
                                         device_id_type=MESH).wait_recv()
        for cp, _ in sent:
            cp.wait_send()
        mine.wait()

    return _pcall(
        body, name=name,
        in_specs=[pl.BlockSpec(memory_space=pltpu.VMEM)],
        out_specs=pl.BlockSpec(memory_space=pltpu.VMEM),
        out_shape=jax.ShapeDtypeStruct((8, r, c), blk.dtype),
        scratch_shapes=[pltpu.SemaphoreType.DMA((7,)), pltpu.SemaphoreType.DMA((7,)), pltpu.SemaphoreType.DMA],
    )(blk)


_HBM = pl.BlockSpec(memory_space=pltpu.HBM)
_SEM = pl.BlockSpec(memory_space=pltpu.SEMAPHORE)
_DATAFLOW = pltpu.SideEffectType.DATAFLOW_SIDE_EFFECTING


def _split_start(bufs, plan, k, *, name):
    n = len(bufs)

    def body(*refs):
        ins, send_sems, recv_sems, token = refs[:n], refs[n], refs[n + 1], refs[2 * n + 2]
        for i, (src, dst, dev) in enumerate(plan(ins)):
            pltpu.make_async_remote_copy(src_ref=src, dst_ref=dst, send_sem=send_sems.at[i], recv_sem=recv_sems.at[i],
                                         device_id=dev, device_id_type=MESH).start()
        token[...] = jnp.zeros_like(token)

    res = _pcall(
        body, name=name,
        out_shape=(pltpu.SemaphoreType.DMA((k,)), pltpu.SemaphoreType.DMA((k,)),
                   *[pltpu.HBM(b.shape, b.dtype) for b in bufs], jax.ShapeDtypeStruct((8, 128), F32)),
        in_specs=[_HBM] * n, out_specs=(_SEM, _SEM, *[_HBM] * n, pl.BlockSpec(memory_space=pltpu.VMEM)),
        input_output_aliases={i: 2 + i for i in range(n)},
        compiler_params=pltpu.CompilerParams(has_side_effects=_DATAFLOW),
    )(*[pltpu.with_memory_space_constraint(b, pltpu.HBM) for b in bufs])
    return res[0], res[1], list(res[2:2 + n]), res[2 + n]


def _split_wait(bufs, send_sems, recv_sems, plan, after, *, name):
    n = len(bufs)

    def body(*refs):
        ins, ssem, rsem = refs[:n], refs[n], refs[n + 1]
        for i, (src, dst, dev) in enumerate(plan(ins)):
            cp = pltpu.make_async_remote_copy(src_ref=src, dst_ref=dst, send_sem=ssem.at[i], recv_sem=rsem.at[i],
                                              device_id=dev, device_id_type=MESH)
            cp.wait_send()
            cp.wait_recv()

    res = _pcall(
        body, name=name, out_shape=tuple(pltpu.HBM(b.shape, b.dtype) for b in bufs),
        in_specs=[_HBM] * n + [_SEM, _SEM, pl.BlockSpec(memory_space=pl.ANY)], out_specs=tuple([_HBM] * n),
        input_output_aliases={i: i for i in range(n)},
        compiler_params=pltpu.CompilerParams(has_side_effects=_DATAFLOW),
    )(*bufs, send_sems, recv_sems, after)
    return list(res)


_CHIP_FLIPS = [(1, 0), (0, 1), (1, 1)]


class _GatheredWeights:
    FIRST = ('even_in', 'even_out')
    REST = ('ffn_in', 'ffn_out', 'odd_in', 'odd_out')

    def __init__(self, shards, reducer):
        self.shards = shards
        self.send_grads = reducer.start
        self.ici = {}
        for grp, names in (('first', self.FIRST), ('rest', self.REST)):
            src = [shards[nm].reshape(2, shards[nm].shape[0] // 2, shards[nm].shape[1]) for nm in names]
            land = [lax.empty((4,) + a.shape, a.dtype) for a in src]
            m = len(names)
            sends, recvs, bufs, token = _split_start(src + land, functools.partial(self._ici_plan, m, True), 3 * m,
                                                     name='gather_' + grp + '_ici_start')
            self.ici[grp] = (sends, recvs, bufs, m)
            self.token = token if grp == 'first' else self.token + token
        self.rest_d2d = None

    @staticmethod
    def _ici_plan(m, sending, refs):
        ax, ay, ac = _place()
        s = 2 * ax + ay
        out = []
        for a in range(m):
            for dx, dy in _CHIP_FLIPS:
                px, py = lax.rem(ax + dx, 2), lax.rem(ay + dy, 2)
                slot = s if sending else 2 * px + py
                out.append((refs[a].at[ac], refs[m + a].at[slot, ac], (px, py, ac)))
        return out

    @staticmethod
    def _d2d_plan(m, sending, refs):
        ax, ay, ac = _place()
        out = []
        for a in range(m):
            for dx, dy in _CHIP_FLIPS:
                sp = 2 * lax.rem(ax + dx, 2) + lax.rem(ay + dy, 2)
                out.append((refs[a].at[sp, ac], refs[a].at[sp, ac if sending else 1 - ac], (ax, ay, 1 - ac)))
        return out

    def _landed(self, grp, after):
        sends, recvs, bufs, m = self.ici[grp]
        bufs = _split_wait(bufs, sends, recvs, functools.partial(self._ici_plan, m, False), after,
                           name='gather_' + grp + '_ici_wait')
        sends, recvs, land, _ = _split_start(bufs[m:], functools.partial(self._d2d_plan, m, True), 3 * m,
                                             name='gather_' + grp + '_d2d_start')
        return sends, recvs, land, m

    def _full(self, grp, names, d2d, after):
        sends, recvs, land, m = d2d
        land = _split_wait(land, sends, recvs, functools.partial(self._d2d_plan, m, False), after,
                           name='gather_' + grp + '_d2d_wait')
        s = 2 * lax.axis_index("x") + lax.axis_index("y")
        slot = lax.broadcasted_iota(jnp.int32, (4, 1, 1), 0)
        return {nm: _from_shards(nm, jnp.where(slot == s, self.shards[nm][None], g.reshape((4,) + self.shards[nm].shape)))
                for nm, g in zip(names, land)}

    def first(self, after):
        return self._full('first', self.FIRST, self._landed('first', after), after)

    def rest_landed(self, after):
        self.rest_d2d = self._landed('rest', after)

    def rest(self, after):
        return self._full('rest', self.REST, self.rest_d2d, after)


def _to_sibling(arrs, *, name):
    n = len(arrs)

    def body(*refs):
        ins, outs = refs[:n], refs[n:2 * n]
        send_sems, recv_sems = refs[2 * n:]
        ax, ay, ac = _place()
        cps = [pltpu.make_async_remote_copy(src_ref=ins[a], dst_ref=outs[a], send_sem=send_sems.at[a],
                                            recv_sem=recv_sems.at[a], device_id=(ax, ay, 1 - ac),
                                            device_id_type=MESH) for a in range(n)]
        for cp in cps:
            cp.start()
        for cp in cps:
            cp.wait_recv()
        for cp in cps:
            cp.wait_send()

    hbm = pl.BlockSpec(memory_space=pl.ANY)
    return _pcall(
        body, name=name, in_specs=[hbm] * n, out_specs=[hbm] * n,
        out_shape=[jax.ShapeDtypeStruct(a.shape, a.dtype) for a in arrs],
        scratch_shapes=[pltpu.SemaphoreType.DMA((n,))] * 2,
    )(*arrs)


def _mod_fwd(cond_raw, mw, mb, *, name):
    _, d, n = mw.shape

    def body(c_ref, w_ref, b_ref, o_ref):
        cv = c_ref[...]
        o_ref[...] = _dot(cv * _sigmoid(cv), w_ref[...]) + b_ref[...]

    return _pcall(
        body, name=name, grid=(2,),
        in_specs=[pl.BlockSpec((16, d), lambda l: (0, 0)), pl.BlockSpec((None, d, n), lambda l: (l, 0, 0)),
                  pl.BlockSpec((None, 1, n), lambda l: (l, 0, 0))],
        out_specs=pl.BlockSpec((None, 16, n), lambda l: (l, 0, 0)),
        out_shape=jax.ShapeDtypeStruct((2, 16, n), F32),
    )(cond_raw, mw, mb)


def _mod_bwd(cond_raw, dms, mw, *, name):
    _, d, n = mw.shape

    def body(c_ref, dm_ref, w_ref, gw_ref, dc_ref):
        @pl.when(pl.program_id(0) == 0)
        def _():
            dc_ref[...] = jnp.zeros_like(dc_ref)
        cv = c_ref[...]
        gw_ref[...] = _dot_tn(cv * _sigmoid(cv), dm_ref[...])
        dc_ref[...] += _dot_nt(dm_ref[...], w_ref[...])

    return _pcall(
        body, name=name, grid=(2,),
        in_specs=[pl.BlockSpec((16, d), lambda l: (0, 0)), pl.BlockSpec((None, 16, n), lambda l: (l, 0, 0)),
                  pl.BlockSpec((None, d, n), lambda l: (l, 0, 0))],
        out_specs=[pl.BlockSpec((None, d, n), lambda l: (l, 0, 0)), pl.BlockSpec((16, d), lambda l: (0, 0))],
        out_shape=[jax.ShapeDtypeStruct((2, d, n), F32), jax.ShapeDtypeStruct((16, d), F32)],
    )(cond_raw, dms, mw)


def _lb_fwd(hgrn_lb, *, name):
    def body(a_ref, o_ref):
        a0, a1 = a_ref[0:1, :], a_ref[1:2, :]
        m = jnp.maximum(a0, a1)
        e0, e1 = jnp.exp(a0 - m), jnp.exp(a1 - m)
        o_ref[...] = e0 / (e0 + e1)

    return _pcall(body, name=name, out_shape=jax.ShapeDtypeStruct((1, hgrn_lb.shape[1]), F32))(hgrn_lb)


PACK_TILES = ('l0n1', 'l0n2', 'l1n1', 'l1n2', 'fin', 'gq', 'gk', 'gain', 'dlb_f', 'dlb_b', 'sink')
PACK_ROW = {nm: 8 * i for i, nm in enumerate(PACK_TILES)}
MOD_SOURCE = ((('l0n1', 0), ('l0n1', 1), ('l0n2', 2), ('l0n2', 0), ('l0n2', 1), ('l1n1', 2)),
              (('l1n1', 0), ('l1n1', 1), ('l1n2', 2), ('l1n2', 0), ('l1n2', 1), ('fin', 2)))


def _small_finalize(gath, lb_pad, *, name):
    d = gath.shape[2]

    def body(g_ref, lb_ref, small_ref, glb_ref, gmb_ref, dm_ref):
        tot = g_ref[0]
        for e in range(1, 8):
            tot = tot + g_ref[e]

        def row(nm, r=0):
            return tot[PACK_ROW[nm] + r:PACK_ROW[nm] + r + 1, :]

        for k, nm in enumerate(('l0n1', 'l0n2', 'l1n1', 'l1n2')):
            small_ref[k:k + 1, :] = row(nm, 3) + row(nm, 7)
        for k, nm in ((4, 'gq'), (5, 'gk')):
            small_ref[k:k + 1, :] = row(nm) + pltpu.roll(row(nm), d - 64, 1)
        small_ref[6:7, :] = row('gain')
        small_ref[7:8, :] = row('sink')
        lbv = lb_ref[...]
        g0 = (row('dlb_f') + row('dlb_b')) * lbv * (1.0 - lbv)
        glb_ref[...] = jnp.zeros_like(glb_ref)
        glb_ref[0:1, :] = g0
        glb_ref[1:2, :] = -g0
        dm_ref[...] = jnp.zeros_like(dm_ref)
        for l in range(2):
            for part in range(6):
                nm, r = MOD_SOURCE[l][part]
                gmb_ref[l * 6 + part:l * 6 + part + 1, :] = row(nm, r) + row(nm, r + 4)
                rl = PACK_ROW[nm] + r + 4
                for e in range(8):
                    dm_ref[l, part, e:e + 1, :] = g_ref[e, rl:rl + 1, :]
                dm_ref[l, part, 8:9, :] = row(nm, r)

    return _pcall(
        body, name=name,
        out_shape=[jax.ShapeDtypeStruct((8, d), F32), jax.ShapeDtypeStruct((8, d), F32),
                   jax.ShapeDtypeStruct((12, d), F32), jax.ShapeDtypeStruct((2, 6, 16, d), F32)],
    )(gath, lb_pad)


def _cctx_grad(gath, c_ctx2, *, name):
    def body(g_ref, c_ref, o_ref):
        tot = ((g_ref[0, 0:1, :] + g_ref[2, 0:1, :]) + g_ref[4, 0:1, :]) + g_ref[6, 0:1, :]
        cv = c_ref[...]
        s = _sigmoid(cv)
        o_ref[...] = tot * (s * (1.0 + cv * (1.0 - s)))

    return _pcall(body, name=name, out_shape=jax.ShapeDtypeStruct(c_ctx2.shape, F32))(gath, c_ctx2)


def _row_block(r, c, limit=256 * 1024):
    best = None
    for br in range(16, r + 1, 16):
        if r % br == 0 and br * c <= limit:
            best = br
    return best if best is not None else r


def _sum4(own, landed, core, *, name):
    _, r, c = own.shape
    br = _row_block(r, c, 512 * 1024)

    def body(core_ref, own_ref, land_ref, o_ref):
        s = 2 * lax.axis_index("x") + lax.axis_index("y")
        p = [jnp.where(s == k, own_ref[k], land_ref[k]).astype(F32) for k in range(4)]
        o_ref[...] = ((p[0] + p[1]) + p[2]) + p[3]

    blk = pl.BlockSpec((4, br, c), lambda i, core_ref: (0, i, 0))
    spec = pltpu.PrefetchScalarGridSpec(
        num_scalar_prefetch=1, grid=(r // br,), in_specs=[blk, blk],
        out_specs=pl.BlockSpec((None, br, c), lambda i, core_ref: (core_ref[0], i, 0)))
    return _pcall(body, name=name, grid_spec=spec, out_shape=jax.ShapeDtypeStruct((2, r, c), F32))(core, own, landed)


def _exchange_halves(arrs, *, name):
    n = len(arrs)

    def body(*refs):
        ins, outs = refs[:n], refs[n:2 * n]
        send_sems, recv_sems = refs[2 * n:]
        ax, ay, ac = _place()
        cps = [pltpu.make_async_remote_copy(src_ref=ins[a].at[ac], dst_ref=outs[a].at[ac], send_sem=send_sems.at[a],
                                            recv_sem=recv_sems.at[a], device_id=(ax, ay, 1 - ac),
                                            device_id_type=MESH) for a in range(n)]
        for cp in cps:
            cp.start()
        for a in range(n):
            pltpu.make_async_remote_copy(src_ref=ins[a].at[ac], dst_ref=outs[a].at[1 - ac], send_sem=send_sems.at[a],
                                         recv_sem=recv_sems.at[a], device_id=(ax, ay, ac),
                                         device_id_type=MESH).wait_recv()
        for cp in cps:
            cp.wait_send()

    hbm = pl.BlockSpec(memory_space=pl.ANY)
    return _pcall(
        body, name=name, in_specs=[hbm] * n, out_specs=[hbm] * n,
        out_shape=[jax.ShapeDtypeStruct(a.shape, a.dtype) for a in arrs],
        input_output_aliases={a: a for a in range(n)},
        scratch_shapes=[pltpu.SemaphoreType.DMA((n,))] * 2,
    )(*arrs)


def _add2(a, b, *, name):
    r, c = a.shape
    br = _row_block(r, c, 1024 * 1024)

    def body(a_ref, b_ref, o_ref):
        o_ref[...] = (a_ref[...].astype(F32) + b_ref[...].astype(F32)).astype(BF16)

    blk = pl.BlockSpec((br, c), lambda i: (i, 0))
    return _pcall(body, name=name, grid=(r // br,), in_specs=[blk, blk], out_specs=blk,
                  out_shape=jax.ShapeDtypeStruct((r, c), BF16))(a, b)


def _adam(w, gs, m, v, *, name):
    r, c = w.shape
    br = _row_block(r, c)
    ng = len(gs)
    c1 = 1.0 - ADAM_B1 ** ADAM_STEP
    c2 = 1.0 - ADAM_B2 ** ADAM_STEP

    def body(*refs):
        w_ref, m_ref, v_ref = refs[0], refs[1 + ng], refs[2 + ng]
        outs = refs[3 + ng:]
        g = refs[1][...]
        for k in range(1, ng):
            g = g + refs[1 + k][...]
        mn = ADAM_B1 * m_ref[...] + (1.0 - ADAM_B1) * g
        vn = ADAM_B2 * v_ref[...] + (1.0 - ADAM_B2) * (g * g)
        if ng > 1:
            outs[0][...] = g
        d_out, m_out, v_out = outs[-3:]
        m_out[...] = mn
        v_out[...] = vn
        d_out[...] = -ADAM_LR * ((mn / c1) / (jnp.sqrt(vn / c2) + ADAM_EPS) + ADAM_WD * w_ref[...])

    blk = pl.BlockSpec((br, c), lambda i: (i, 0))
    nout = 4 if ng > 1 else 3
    res = _pcall(body, name=name, grid=(r // br,), in_specs=[blk] * (3 + ng), out_specs=[blk] * nout,
                 out_shape=[jax.ShapeDtypeStruct((r, c), F32)] * nout)(w, *gs, m, v)
    return list(res) if ng > 1 else [gs[0]] + list(res)


def _grad_halves(name, g, ac):
    if name.endswith('_in'):
        n = g.shape[1] // 4
        if name == 'ffn_in':
            assert n == FFN_BK
        order = _ffn_order(g.shape[1]) if name == 'ffn_in' else range(4)
        v = jnp.stack([g[:, b * n:(b + 1) * n] for b in order])
        per = [v[:, :g.shape[0] // 2], v[:, g.shape[0] // 2:]]
    else:
        k4, n = g.shape
        v = g.reshape(4, 2, k4 // 8, n)
        per = [v[:, 0], v[:, 1]]
    first = ac == 0
    return _bf(jnp.where(first, per[0], per[1])), _bf(jnp.where(first, per[1], per[0]))


class _GradReducer:
    def __init__(self):
        self.flight = {}

    @staticmethod
    def _plan(m, sending, refs):
        ax, ay, ac = _place()
        s = 2 * ax + ay
        out = []
        for a in range(m):
            for dx, dy in _CHIP_FLIPS:
                px, py = lax.rem(ax + dx, 2), lax.rem(ay + dy, 2)
                sp = 2 * px + py
                out.append((refs[a].at[sp], refs[m + a].at[s if sending else sp], (px, py, ac)))
        return out

    def start(self, grp, grads):
        ac = lax.axis_index("c")
        names = list(grads)
        halves = [_grad_halves(nm.rstrip('01'), grads[nm], ac) for nm in names]
        theirs = _to_sibling([h[1] for h in halves], name='swap_core_halves_' + grp)
        pair = [_add2(h[0].reshape(-1, b.shape[-1]), b.reshape(-1, b.shape[-1]), name='add_cores').reshape(b.shape)
                for h, b in zip(halves, theirs)]
        m = len(names)
        land = [lax.empty(a.shape, a.dtype) for a in pair]
        sends, recvs, bufs, token = _split_start(pair + land, functools.partial(self._plan, m, True), 3 * m,
                                                 name='scatter_' + grp + '_start')
        self.flight[grp] = (names, sends, recvs, bufs)
        return token

    def finish(self, grp, after):
        names, sends, recvs, bufs = self.flight.pop(grp)
        m = len(names)
        bufs = _split_wait(bufs, sends, recvs, functools.partial(self._plan, m, False), after,
                           name='scatter_' + grp + '_wait')
        core = lax.axis_index("c").astype(jnp.int32).reshape(1)
        sums = [_sum4(p, l, core, name='sum_chips') for p, l in zip(bufs[:m], bufs[m:])]
        both = _exchange_halves(sums, name='gather_core_halves_' + grp)
        return {nm: g.reshape(-1, g.shape[-1]) for nm, g in zip(names, both)}


def _from_shards(name, g):
    _, r, n = g.shape
    if name == 'ffn_in':
        assert n == FFN_BK
        v = g.reshape(4, 2, r // 2, n)
        return jnp.concatenate([v[b] for b in _ffn_order(4 * n)], axis=-1)
    if name == 'ffn_out':
        return g.reshape(4, 2, r // 2, n).transpose(1, 0, 2, 3).reshape(2, 2 * r, n)
    if name in ('even_in', 'odd_in'):
        return jnp.concatenate([g[b] for b in range(4)], axis=-1)
    return g.reshape(4 * r, n)


def kernel(x, c, ctx, c_ctx, mod_w, mod_b, norm_g, ffn_w_in, ffn_w_out, even_w_in, even_w_out, attn_qk_norm_g, attn_sink, hgrn_out_norm_g, hgrn_lb, odd_w_in, odd_w_out, loss_target, m_c_ctx, m_mod_w, m_mod_b, m_norm_g, m_ffn_w_in, m_ffn_w_out, m_even_w_in, m_even_w_out, m_attn_qk_norm_g, m_attn_sink, m_hgrn_out_norm_g, m_hgrn_lb, m_odd_w_in, m_odd_w_out, v_c_ctx, v_mod_w, v_mod_b, v_norm_g, v_ffn_w_in, v_ffn_w_out, v_even_w_in, v_even_w_out, v_attn_qk_norm_g, v_attn_sink, v_hgrn_out_norm_g, v_hgrn_lb, v_odd_w_in, v_odd_w_out):
    d = x.shape[-1]
    lc = ctx.shape[1]
    assert lc == TM and d == 1024
    ax, ay, ac = _place()
    s = 2 * ax + ay
    me = 4 * ax + 2 * ay + ac
    nmod = mod_w.shape[2]

    def pad8(v):
        return jnp.pad(v, ((0, 8 - v.shape[0]), (0, 0)))

    pack = jnp.concatenate([pad8(c), pad8(norm_g.reshape(1, d))], axis=0)
    g1 = _ag8(pack, name='gather_cond')
    c_all = g1[:, 0, :]
    ng = g1[0::2, 8, :].reshape(4, 2, 2, d // 4).transpose(1, 2, 0, 3).reshape(4, d)

    cond_raw = jnp.concatenate([c_all, pad8(c_ctx.reshape(1, d))], axis=0)
    mb_sh = lax.dynamic_slice_in_dim(mod_b, s * nmod, nmod, axis=1).reshape(2, 1, nmod)
    mpart = _mod_fwd(cond_raw, mod_w, mb_sh, name='mod_fwd')
    g3 = _ag8(mpart.reshape(32, nmod), name='gather_mods')
    mods_full = g3[0::2].reshape(4, 2, 16, nmod).transpose(1, 2, 0, 3).reshape(2, 16, 4 * nmod)
    m_lat = lax.dynamic_index_in_dim(mods_full, me, axis=1, keepdims=False)
    mods = jnp.stack([mods_full[:, 8], m_lat], axis=1).reshape(24, d)

    names = ['ffn_in', 'ffn_out', 'even_in', 'even_out', 'odd_in', 'odd_out']
    shards = [_bf(v.reshape(-1, v.shape[-1])) for v in (ffn_w_in, ffn_w_out, even_w_in, even_w_out, odd_w_in, odd_w_out)]
    shards, mods = lax.optimization_barrier((shards, mods))
    reducer = _GradReducer()
    wsrc = _GatheredWeights(dict(zip(names, shards)), reducer)

    lb = _lb_fwd(hgrn_lb, name='hgrn_lower_bound')
    small = dict(gq=jnp.tile(attn_qk_norm_g[0, 0], 2).reshape(1, 128), gk=jnp.tile(attn_qk_norm_g[0, 1], 2).reshape(1, 128),
                 sink=attn_sink[0], gain=hgrn_out_norm_g, lb=lb)
    x0 = jnp.concatenate([ctx[0], x[0]], axis=0) + wsrc.token[0, 0]
    loss_t, dx0, grads, sums = _local_step(x0, loss_target[0], mods, ng, wsrc, small)
    loss = lax.psum(loss_t[0, 0], ("x", "y", "c"))
    grad_x = dx0[None]

    def tile(v):
        return jnp.pad(v, ((0, 8 - v.shape[0]), (0, d - v.shape[1])))

    sums = dict(sums, sink=sums['sink'][:, 0].reshape(1, 8))
    g4 = _ag8(jnp.concatenate([tile(sums[nm]) for nm in PACK_TILES], axis=0), name='gather_row_sums')
    small_g, glb, gmb, dmat = _small_finalize(g4, tile(lb)[0:1], name='small_grads')
    dms = lax.dynamic_slice_in_dim(dmat.transpose(0, 2, 1, 3).reshape(2, 16, 6 * d), s * nmod, nmod, axis=2)
    g_mod_w, dcond = _mod_bwd(cond_raw, dms, mod_w, name='mod_bwd')
    g5 = _ag8(dcond[8:16], name='gather_dcond')
    g_c_ctx = _cctx_grad(g5, c_ctx.reshape(8, d // 8).reshape(1, d), name='c_ctx_grad')

    late = {nm: grads[nm] for nm in ('even_in', 'even_out')}
    late, g_c_ctx = lax.optimization_barrier((late, g_c_ctx))
    token = reducer.start('late', late)
    full = reducer.finish('early', token)

    def upd(wv, gs, mv, vv, name):
        shp = wv.shape
        c2 = shp[-1]
        out = _adam(wv.reshape(-1, c2), [g.reshape(-1, c2) for g in gs], mv.reshape(-1, c2), vv.reshape(-1, c2), name=name)
        return [o.reshape(shp) for o in out]

    res = {}
    res['c_ctx'] = upd(c_ctx.reshape(8, d // 8), [g_c_ctx.reshape(8, d // 8)], m_c_ctx.reshape(8, d // 8), v_c_ctx.reshape(8, d // 8), 'adam_c_ctx')
    res['c_ctx'] = [o.reshape(d) for o in res['c_ctx']]
    res['mod_w'] = upd(mod_w, [g_mod_w], m_mod_w, v_mod_w, 'adam_mod_w')
    res['mod_b'] = upd(mod_b, [gmb.reshape(2, 6 * d)], m_mod_b, v_mod_b, 'adam_mod_b')
    g_ng = lax.dynamic_slice_in_dim(small_g[0:4].reshape(2, 2, d), s * (d // 4), d // 4, axis=2)
    res['norm_g'] = upd(norm_g, [g_ng], m_norm_g, v_norm_g, 'adam_norm_g')
    g_qk = jnp.stack([small_g[4, 0:64], small_g[5, 0:64]]).reshape(1, 2, 64)
    res['attn_qk_norm_g'] = upd(attn_qk_norm_g, [g_qk], m_attn_qk_norm_g, v_attn_qk_norm_g, 'adam_qk_gain')
    res['attn_sink'] = upd(attn_sink, [small_g[7, 0:8].reshape(1, 8)], m_attn_sink, v_attn_sink, 'adam_sink')
    res['hgrn_out_norm_g'] = upd(hgrn_out_norm_g, [small_g[6, 0:128].reshape(1, 128)], m_hgrn_out_norm_g, v_hgrn_out_norm_g, 'adam_head_gain')
    res['hgrn_lb'] = upd(hgrn_lb, [glb[0:2, 0:hgrn_lb.shape[1]]], m_hgrn_lb, v_hgrn_lb, 'adam_hgrn_lb')
    res['odd_w_in'] = upd(odd_w_in, [full['odd_in']], m_odd_w_in, v_odd_w_in, 'adam_odd_in')
    res['odd_w_out'] = upd(odd_w_out, [full['odd_out']], m_odd_w_out, v_odd_w_out, 'adam_odd_out')
    full.update(reducer.finish('mid', res['odd_w_in'][1]))
    g_ffn_in = jnp.concatenate([full['ffn_in0'], full['ffn_in1']], axis=0)
    g_ffn_out = jnp.concatenate([full['ffn_out0'], full['ffn_out1']], axis=0)
    res['ffn_w_in'] = upd(ffn_w_in, [g_ffn_in], m_ffn_w_in, v_ffn_w_in, 'adam_ffn_in')
    res['ffn_w_out'] = upd(ffn_w_out, [g_ffn_out], m_ffn_w_out, v_ffn_w_out, 'adam_ffn_out')
    full.update(reducer.finish('late', res['ffn_w_in'][1]))
    res['even_w_in'] = upd(even_w_in, [full['even_in']], m_even_w_in, v_even_w_in, 'adam_even_in')
    res['even_w_out'] = upd(even_w_out, [full['even_out']], m_even_w_out, v_even_w_out, 'adam_even_out')

    order = ['c_ctx', 'mod_w', 'mod_b', 'norm_g', 'ffn_w_in', 'ffn_w_out', 'even_w_in', 'even_w_out',
             'attn_qk_norm_g', 'attn_sink', 'hgrn_out_norm_g', 'hgrn_lb', 'odd_w_in', 'odd_w_out']
    outs = [loss, grad_x]
    for k in range(4):
        outs += [res[nm][k] for nm in order]
    return tuple(outs)
```

```python
import functools
import math

import numpy as np
import jax
import jax.numpy as jnp
from jax import lax
from jax.experimental import pallas as pl
from jax.experimental.pallas import tpu as pltpu

F32 = jnp.float32
BF16 = jnp.bfloat16
EPS = 1e-6
TM = 256
CHUNK = 64
QB = 256
WINDOW = 128
NEG = -1e30
MESH = pl.DeviceIdType.MESH

ADAM_LR, ADAM_B1, ADAM_B2, ADAM_EPS, ADAM_WD, ADAM_STEP = 0.001, 0.9, 0.999, 1e-08, 0.01, 10


def _pcall(body, **kw):
    return pl.pallas_call(body, **kw)


def _pick(n, cap):
    best = None
    for m in range(128, min(n, cap) + 1, 128):
        if n % m == 0:
            best = m
    assert best is not None, (n, cap)
    return best


def _bf(x):
    return x.astype(BF16)


def _dot(a, b):
    return jnp.dot(_bf(a), _bf(b), preferred_element_type=F32)


def _dot_nt(a, b):
    return lax.dot_general(_bf(a), _bf(b), (((1,), (1,)), ((), ())), preferred_element_type=F32)


def _dot_tn(a, b):
    return lax.dot_general(_bf(a), _bf(b), (((0,), (0,)), ((), ())), preferred_element_type=F32)


def _dot_exact(a, b):
    return jnp.dot(a, b, preferred_element_type=F32, precision=lax.Precision.HIGHEST)


def _sigmoid(x):
    return 1.0 / (1.0 + jnp.exp(-x))


def _iota(shape, dim):
    return lax.broadcasted_iota(jnp.int32, shape, dim)


def _mm_nn(a, b, *, lead=None, out_dtype=F32, name):
    m, k = a.shape
    n = b.shape[-1]
    bm = 1408 if (m % 1408 == 0 and k <= 1024) else (768 if m % 768 == 0 else TM)
    bn = _pick(n, 1024) if n % 512 == 0 else _pick(n, 1664)

    def body(a_ref, b_ref, o_ref):
        o_ref[...] = _dot(a_ref[...], b_ref[...]).astype(o_ref.dtype)

    if lead is None:
        b_spec = pl.BlockSpec((k, bn), lambda i, j: (0, j))
    else:
        b_spec = pl.BlockSpec((None, k, bn), lambda i, j: (lead, 0, j))
    return _pcall(
        body, name=name, grid=(m // bm, n // bn),
        in_specs=[pl.BlockSpec((bm, k), lambda i, j: (i, 0)), b_spec],
        out_specs=pl.BlockSpec((bm, bn), lambda i, j: (i, j)),
        out_shape=jax.ShapeDtypeStruct((m, n), out_dtype),
    )(a, b)


def _mm_nt(a, b, *, lead=None, name):
    m, n = a.shape
    k = b.shape[-2]
    bm = 768 if m % 768 == 0 else TM
    bk = _pick(k, 512)

    def body(a_ref, b_ref, o_ref):
        o_ref[...] = _dot_nt(a_ref[...], b_ref[...])

    if lead is None:
        b_spec = pl.BlockSpec((bk, n), lambda i, j: (j, 0))
    else:
        b_spec = pl.BlockSpec((None, bk, n), lambda i, j: (lead, j, 0))
    return _pcall(
        body, name=name, grid=(m // bm, k // bk),
        in_specs=[pl.BlockSpec((bm, n), lambda i, j: (i, 0)), b_spec],
        out_specs=pl.BlockSpec((bm, bk), lambda i, j: (i, j)),
        out_shape=jax.ShapeDtypeStruct((m, k), F32),
    )(a, b)


def _mm_tn(a, b, *, name):
    t, k = a.shape
    n = b.shape[1]
    bt = 1408 if t % 1408 == 0 else (768 if t % 768 == 0 else TM)
    bk = _pick(k, 1536)
    bn = _pick(n, 1024) if n % 1024 == 0 or n < 1664 else _pick(n, 1664)

    def body(a_ref, b_ref, o_ref):
        @pl.when(pl.program_id(2) == 0)
        def _():
            o_ref[...] = jnp.zeros_like(o_ref)
        o_ref[...] += _dot_tn(a_ref[...], b_ref[...])

    return _pcall(
        body, name=name, grid=(k // bk, n // bn, t // bt),
        in_specs=[pl.BlockSpec((bt, bk), lambda i, j, s: (s, i)),
                  pl.BlockSpec((bt, bn), lambda i, j, s: (s, j))],
        out_specs=pl.BlockSpec((bk, bn), lambda i, j, s: (i, j)),
        out_shape=jax.ShapeDtypeStruct((k, n), F32),
    )(a, b)


def _mod_row(mods_ref, lat, idx):
    return jnp.where(lat, mods_ref[idx + 6:idx + 7, :], mods_ref[idx:idx + 1, :])


def _row_step(t):
    return 768 if t % 768 == 0 else TM


def _row_fwd(x, mods, *, y=None, gate=None, g=None, shift=None, scale=None, name):
    t, d = x.shape
    has_y, has_n = y is not None, g is not None
    rt = _row_step(t)

    def body(*refs):
        refs = list(refs)
        x_ref, mods_ref = refs[0], refs[1]
        pos = 2
        if has_y:
            y_ref = refs[pos]; pos += 1
        if has_n:
            g_ref = refs[pos]; pos += 1
        outs = refs[pos:]
        for sub in range(rt // TM):
            rows = slice(sub * TM, (sub + 1) * TM)
            lat = pl.program_id(0) * (rt // TM) + sub > 0
            x1 = x_ref[rows, :]
            o = 0
            if has_y:
                x1 = x1 + _mod_row(mods_ref, lat, gate) * y_ref[rows, :]
                outs[o][rows, :] = x1; o += 1
            if has_n:
                rs = lax.rsqrt(jnp.mean(x1 * x1, axis=-1, keepdims=True) + EPS)
                hn = x1 * rs * g_ref[...]
                h = hn * (1.0 + _mod_row(mods_ref, lat, scale)) + _mod_row(mods_ref, lat, shift)
                outs[o][rows, :] = h.astype(BF16)

    row = pl.BlockSpec((rt, d), lambda i: (i, 0))
    ins, specs = [x, mods], [row, pl.BlockSpec(mods.shape, lambda i: (0, 0))]
    if has_y:
        ins.append(y); specs.append(row)
    if has_n:
        ins.append(g.reshape(1, d)); specs.append(pl.BlockSpec((1, d), lambda i: (0, 0)))
    out_shape, out_specs = [], []
    if has_y:
        out_shape.append(jax.ShapeDtypeStruct((t, d), F32)); out_specs.append(row)
    if has_n:
        out_shape.append(jax.ShapeDtypeStruct((t, d), BF16)); out_specs.append(row)
    res = _pcall(body, name=name, grid=(t // rt,), in_specs=specs, out_specs=out_specs,
                 out_shape=out_shape)(*ins)
    return res


def _acc_row(ref, r, val):
    ref[r:r + 1, :] += val


def _row_final(x, z, mods, target, *, gate, name):
    t, d = x.shape

    def body(x_ref, mods_ref, z_ref, t_ref, loss_ref, dx_ref, dz_ref, sums_ref):
        i = pl.program_id(0)
        lat = i > 0

        @pl.when(i == 0)
        def _():
            loss_ref[...] = jnp.zeros_like(loss_ref)
            sums_ref[...] = jnp.zeros_like(sums_ref)

        gt = _mod_row(mods_ref, lat, gate)
        zz = z_ref[...]
        yv = x_ref[...] + gt * zz
        keep = jnp.where(lat, 1.0, 0.0).astype(F32)
        diff = (yv - t_ref[...]) * keep
        part = jnp.sum(jnp.sum(diff * diff, axis=0, keepdims=True), axis=1, keepdims=True)
        loss_ref[...] += part * (0.5 / d)
        dy = diff * (1.0 / d)
        dx_ref[...] = dy
        dz_ref[...] = (gt * dy).astype(BF16)
        _acc_row(sums_ref, 6, jnp.sum(dy * zz, axis=0, keepdims=True))

    row = pl.BlockSpec((TM, d), lambda i: (i, 0))
    return _pcall(
        body, name=name, grid=(t // TM,),
        in_specs=[row, pl.BlockSpec(mods.shape, lambda i: (0, 0)), row,
                  pl.BlockSpec((TM, d), lambda i: (jnp.maximum(i - 1, 0), 0))],
        out_specs=[pl.BlockSpec((8, 128), lambda i: (0, 0)), row, row,
                   pl.BlockSpec((8, d), lambda i: (0, 0))],
        out_shape=[jax.ShapeDtypeStruct((8, 128), F32), jax.ShapeDtypeStruct((t, d), F32),
                   jax.ShapeDtypeStruct((t, d), BF16), jax.ShapeDtypeStruct((8, d), F32)],
    )(x, mods, z, target)


def _row_bwd(xn, dxo, dh, mods, g, *, shift, scale, y=None, gate=None, latent_only=False, name):
    t, d = xn.shape
    has_y = y is not None

    def body(*refs):
        refs = list(refs)
        x_ref, dxo_ref, dh_ref, mods_ref, g_ref = refs[:5]
        pos = 5
        if has_y:
            y_ref = refs[pos]; pos += 1
        dx_ref = refs[pos]; pos += 1
        if has_y:
            dy_ref = refs[pos]; pos += 1
        sums_ref = refs[pos]
        i = pl.program_id(0)

        @pl.when(i == 0)
        def _():
            sums_ref[...] = jnp.zeros_like(sums_ref)

        def add_sums(vals, base):
            for r, v in enumerate(vals):
                if v is not None:
                    _acc_row(sums_ref, base + r, v)

        gv = g_ref[...]
        for sub in range(rt // TM):
            rows = slice(sub * TM, (sub + 1) * TM)
            lat = i * (rt // TM) + sub > 0
            x1 = x_ref[rows, :]
            rs = lax.rsqrt(jnp.mean(x1 * x1, axis=-1, keepdims=True) + EPS)
            xh = x1 * rs
            dhv = dh_ref[rows, :]
            dn = dhv * (1.0 + _mod_row(mods_ref, lat, scale))
            dxh = dn * gv
            dx = dxo_ref[rows, :] + rs * (dxh - xh * jnp.mean(dxh * xh, axis=-1, keepdims=True))
            dx_ref[rows, :] = dx
            vals = [jnp.sum(dhv, axis=0, keepdims=True),
                    jnp.sum(dhv * (xh * gv), axis=0, keepdims=True),
                    None,
                    jnp.sum(dn * xh, axis=0, keepdims=True)]
            if has_y:
                dy_ref[rows, :] = (_mod_row(mods_ref, lat, gate) * dx).astype(BF16)
                vals[2] = jnp.sum(dx * y_ref[rows, :], axis=0, keepdims=True)
            if sub == 0:
                pl.when(i == 0)(functools.partial(add_sums, vals, 0))
                pl.when(i > 0)(functools.partial(add_sums, vals, 4))
            else:
                add_sums(vals, 4)

    rt = TM if latent_only else _row_step(t)
    row = pl.BlockSpec((rt, d), lambda i: (i, 0))
    ins = [xn, dxo, dh, mods, g.reshape(1, d)]
    specs = [row, row, row, pl.BlockSpec(mods.shape, lambda i: (0, 0)), pl.BlockSpec((1, d), lambda i: (0, 0))]
    if latent_only:
        out_shape = [jax.ShapeDtypeStruct((t - TM, d), F32)]
        out_specs = [pl.BlockSpec((TM, d), lambda i: (jnp.maximum(i - 1, 0), 0))]
    else:
        out_shape, out_specs = [jax.ShapeDtypeStruct((t, d), F32)], [row]
    if has_y:
        ins.append(y); specs.append(row)
        out_shape.append(jax.ShapeDtypeStruct((t, d), BF16)); out_specs.append(row)
    out_shape.append(jax.ShapeDtypeStruct((8, d), F32))
    out_specs.append(pl.BlockSpec((8, d), lambda i: (0, 0)))
    return _pcall(body, name=name, grid=(t // rt,), in_specs=specs, out_specs=out_specs,
                  out_shape=out_shape)(*ins)


FFN_BK = 1408


FFN_SUB = 256


def _ffn_order(n2):
    nb = n2 // (2 * FFN_BK)
    return [h * nb + j for j in range(nb) for h in (0, 1)]


def _ffn_interleave(w):
    return jnp.concatenate([w[..., b * FFN_BK:(b + 1) * FFN_BK] for b in _ffn_order(w.shape[-1])], axis=-1)


def _ffn_deinterleave(w):
    order = _ffn_order(w.shape[-1])
    return jnp.concatenate([w[..., order.index(b) * FFN_BK:(order.index(b) + 1) * FFN_BK]
                            for b in range(len(order))], axis=-1)


def _big_tile(t):
    return 768 if t % 768 == 0 else TM


def _ffn_in(h, w, *, lead, name):
    t, d = h.shape
    n2 = w.shape[-1]
    bm, bk = _big_tile(t), FFN_BK

    def body(h_ref, w_ref, u_ref, a_ref):
        hb = h_ref[...]
        for c0 in range(0, bk, FFN_SUB):
            c1 = min(c0 + FFN_SUB, bk)
            ug = _dot(hb, w_ref[:, c0:c1]).astype(BF16)
            uu = _dot(hb, w_ref[:, bk + c0:bk + c1]).astype(BF16)
            u_ref[:, c0:c1] = ug
            u_ref[:, bk + c0:bk + c1] = uu
            gv, up = ug.astype(F32), uu.astype(F32)
            a_ref[:, c0:c1] = (gv * _sigmoid(gv) * up).astype(BF16)

    return _pcall(
        body, name=name, grid=(t // bm, n2 // (2 * bk)),
        in_specs=[pl.BlockSpec((bm, d), lambda i, j: (i, 0)),
                  pl.BlockSpec((None, d, 2 * bk), lambda i, j: (lead, 0, j))],
        out_specs=[pl.BlockSpec((bm, 2 * bk), lambda i, j: (i, j)), pl.BlockSpec((bm, bk), lambda i, j: (i, j))],
        out_shape=[jax.ShapeDtypeStruct((t, n2), BF16), jax.ShapeDtypeStruct((t, n2 // 2), BF16)],
    )(h, w)


def _ffn_dx(dz, w_out, u, *, lead, name):
    t, d = dz.shape
    n2 = u.shape[1]
    bm, bk = _big_tile(t), FFN_BK

    def body(dz_ref, w_ref, u_ref, du_ref):
        dzb = dz_ref[...]
        for c0 in range(0, bk, FFN_SUB):
            c1 = min(c0 + FFN_SUB, bk)
            da = _dot_nt(dzb, w_ref[c0:c1, :])
            gv, up = u_ref[:, c0:c1].astype(F32), u_ref[:, bk + c0:bk + c1].astype(F32)
            s = _sigmoid(gv)
            du_ref[:, c0:c1] = (da * up * (s * (1.0 + gv * (1.0 - s)))).astype(BF16)
            du_ref[:, bk + c0:bk + c1] = (da * gv * s).astype(BF16)

    ublk = pl.BlockSpec((bm, 2 * bk), lambda i, j: (i, j))
    return _pcall(
        body, name=name, grid=(t // bm, n2 // (2 * bk)),
        in_specs=[pl.BlockSpec((bm, d), lambda i, j: (i, 0)),
                  pl.BlockSpec((None, bk, d), lambda i, j: (lead, j, 0)), ublk],
        out_specs=ublk, out_shape=jax.ShapeDtypeStruct((t, n2), BF16),
    )(dz, w_out, u)


def _lane(shape):
    return _iota(shape, len(shape) - 1)


def _pair_norm(x, g):
    lo = _lane(x.shape) < 64
    x2 = x * x
    s_lo = jnp.sum(jnp.where(lo, x2, 0.0), axis=-1, keepdims=True)
    s_hi = jnp.sum(jnp.where(lo, 0.0, x2), axis=-1, keepdims=True)
    rs = lax.rsqrt(jnp.where(lo, s_lo, s_hi) * (1.0 / 64) + EPS)
    return x * rs, rs


def _pair_mean(v):
    lo = _lane(v.shape) < 64
    s_lo = jnp.sum(jnp.where(lo, v, 0.0), axis=-1, keepdims=True)
    s_hi = jnp.sum(jnp.where(lo, 0.0, v), axis=-1, keepdims=True)
    return jnp.where(lo, s_lo, s_hi) * (1.0 / 64)


def _rot64(x):
    r1 = pltpu.roll(x, 32, 1)
    r2 = pltpu.roll(x, 96, 1)
    even = ((_lane(x.shape) >> 5) & 1) == 0
    return jnp.where(even, -r2, r1)


def _rope64(x, cos, sin):
    return x * cos + _rot64(x) * sin


def _rope64_t(d, cos, sin):
    return d * cos - _rot64(d * sin)


def _kprep_fwd(p, gk, cos, sin, *, name):
    t = p.shape[0]

    def body(k_ref, g_ref, c_ref, s_ref, o_ref):
        xh, _ = _pair_norm(k_ref[...], None)
        o_ref[...] = _rope64(xh * g_ref[...], c_ref[...], s_ref[...])

    blk = pl.BlockSpec((TM, 128), lambda i: (i, 0))
    return _pcall(
        body, name=name, grid=(t // TM,),
        in_specs=[pl.BlockSpec((TM, 128), lambda i: (i, 4)), pl.BlockSpec((1, 128), lambda i: (0, 0)), blk, blk],
        out_specs=blk, out_shape=jax.ShapeDtypeStruct((t, 128), F32),
    )(p, gk, cos, sin)


def _kprep_bwd(p, gk, cos, sin, dkp, dv, *, name):
    t = p.shape[0]

    def body(k_ref, g_ref, c_ref, s_ref, dkp_ref, dv_ref, o_ref, dg_ref):
        @pl.when(pl.program_id(0) == 0)
        def _():
            dg_ref[...] = jnp.zeros_like(dg_ref)
        xh, rs = _pair_norm(k_ref[...], None)
        dn = _rope64_t(dkp_ref[...], c_ref[...], s_ref[...])
        _acc_row(dg_ref, 0, jnp.sum(dn * xh, axis=0, keepdims=True))
        dxh = dn * g_ref[...]
        o_ref[:, 0:128] = (rs * (dxh - xh * _pair_mean(dxh * xh))).astype(BF16)
        o_ref[:, 128:256] = dv_ref[...].astype(BF16)

    blk = pl.BlockSpec((TM, 128), lambda i: (i, 0))
    return _pcall(
        body, name=name, grid=(t // TM,),
        in_specs=[pl.BlockSpec((TM, 128), lambda i: (i, 4)), pl.BlockSpec((1, 128), lambda i: (0, 0)), blk, blk, blk, blk],
        out_specs=[pl.BlockSpec((TM, 256), lambda i: (i, 0)), pl.BlockSpec((8, 128), lambda i: (0, 0))],
        out_shape=[jax.ShapeDtypeStruct((t, 256), BF16), jax.ShapeDtypeStruct((8, 128), F32)],
    )(p, gk, cos, sin, dkp, dv)


def _attn_common(i, t, lc, kp_ref, v_ref):
    span = QB + 2 * WINDOW
    start = pl.multiple_of(jnp.clip(i * QB - WINDOW, lc, t - span), WINDOW)
    kall = jnp.concatenate([kp_ref[0:lc, :], kp_ref[pl.ds(start, span), :]], axis=0)
    vall = jnp.concatenate([v_ref[0:lc, :], v_ref[pl.ds(start, span), :]], axis=0)
    nk = lc + span
    col = _iota((QB, nk), 1)
    krow = jnp.where(col < lc, col, start + col - lc)
    qrow = i * QB + _iota((QB, nk), 0)
    valid = (col < lc) | ((qrow >= lc) & (krow >= lc) & (jnp.abs(krow - qrow) <= WINDOW))
    lo = _lane(kall.shape) < 64
    kroll, vroll = pltpu.roll(kall, 64, 1), pltpu.roll(vall, 64, 1)
    zero = jnp.zeros_like(kall)
    kvar = [[_bf(jnp.where(lo, kall, zero)), _bf(jnp.where(lo, zero, kroll))],
            [_bf(jnp.where(lo, kroll, zero)), _bf(jnp.where(lo, zero, kall))]]
    vvar = [[_bf(jnp.where(lo, vall, zero)), _bf(jnp.where(lo, zero, vroll))],
            [_bf(jnp.where(lo, vroll, zero)), _bf(jnp.where(lo, zero, vall))]]
    return start, valid, kvar, vvar


def _softmax_sink(s, valid, snk):
    s = jnp.where(valid, s, NEG)
    m = jnp.maximum(jnp.max(s, axis=-1, keepdims=True), snk)
    e = jnp.exp(s - m)
    es = jnp.exp(snk - m)
    inv = 1.0 / (jnp.sum(e, axis=-1, keepdims=True) + es)
    return e * inv, es * inv


def _attn_fwd(p, kp, gq, sink, cos, sin, *, lc, name):
    t = p.shape[0]
    scale = 64 ** -0.5

    def body(q_ref, kp_ref, v_ref, g_ref, sink_ref, c_ref, s_ref, o_ref):
        i = pl.program_id(0)
        _, valid, kvar, vvar = _attn_common(i, t, lc, kp_ref, v_ref)
        cosv, sinv, gv = c_ref[...], s_ref[...], g_ref[...]
        for j in range(4):
            xh, _ = _pair_norm(q_ref[:, 128 * j:128 * j + 128], None)
            q2 = _bf(_rope64(xh * gv, cosv, sinv) * scale)
            acc = jnp.zeros((QB, 128), F32)
            for half in range(2):
                s = _dot_nt(q2, kvar[j // 2][half])
                pr, _ = _softmax_sink(s, valid, sink_ref[2 * j + half])
                acc = acc + _dot(pr, vvar[j // 2][half])
            o_ref[:, 128 * j:128 * j + 128] = acc.astype(BF16)

    qblk = pl.BlockSpec((QB, 128), lambda i: (i, 0))
    return _pcall(
        body, name=name, grid=(t // QB,),
        in_specs=[pl.BlockSpec((QB, 512), lambda i: (i, 0)),
                  pl.BlockSpec((t, 128), lambda i: (0, 0)),
                  pl.BlockSpec((t, 128), lambda i: (0, 5)),
                  pl.BlockSpec((1, 128), lambda i: (0, 0)),
                  pl.BlockSpec(memory_space=pltpu.SMEM), qblk, qblk],
        out_specs=pl.BlockSpec((QB, 512), lambda i: (i, 0)),
        out_shape=jax.ShapeDtypeStruct((t, 512), BF16),
    )(p, kp, p, gq, sink, cos, sin)


def _attn_bwd(p, kp, gq, sink, cos, sin, dmix, *, lc, name):
    t = p.shape[0]
    scale = 64 ** -0.5
    span = QB + 2 * WINDOW

    def body(q_ref, kp_ref, v_ref, g_ref, sink_ref, c_ref, s_ref, do_ref,
             dq_ref, dk_ref, dv_ref, dg_ref, dsink_ref):
        i = pl.program_id(0)

        @pl.when(i == 0)
        def _():
            dk_ref[...] = jnp.zeros_like(dk_ref)
            dv_ref[...] = jnp.zeros_like(dv_ref)
            dg_ref[...] = jnp.zeros_like(dg_ref)
            dsink_ref[...] = jnp.zeros_like(dsink_ref)

        start, valid, kvar, vvar = _attn_common(i, t, lc, kp_ref, v_ref)
        cosv, sinv, gv = c_ref[...], s_ref[...], g_ref[...]
        nk = lc + span
        dkt = [jnp.zeros((64, nk), F32), jnp.zeros((64, nk), F32)]
        dvt = [jnp.zeros((64, nk), F32), jnp.zeros((64, nk), F32)]
        for j in range(4):
            kvh = j // 2
            xh, rs = _pair_norm(q_ref[:, 128 * j:128 * j + 128], None)
            q2 = _bf(_rope64(xh * gv, cosv, sinv) * scale)
            do2 = _bf(do_ref[:, 128 * j:128 * j + 128])
            dq2 = jnp.zeros((QB, 128), F32)
            for half in range(2):
                s = _dot_nt(q2, kvar[kvh][half])
                pr, ps = _softmax_sink(s, valid, sink_ref[2 * j + half])
                dp = _dot_nt(do2, vvar[kvh][half])
                delta = jnp.sum(pr * dp, axis=-1, keepdims=True)
                ds = pr * (dp - delta)
                dsk = jnp.sum(jnp.sum(-ps * delta, axis=0, keepdims=True), axis=1, keepdims=True)
                _acc_row(dsink_ref, 2 * j + half, jnp.broadcast_to(dsk, (1, 128)))
                dq2 = dq2 + _dot(ds, kvar[kvh][half])
                hrows = slice(64 * half, 64 * half + 64)
                dkt[kvh] = dkt[kvh] + _dot_tn(q2, ds)[hrows]
                dvt[kvh] = dvt[kvh] + _dot_tn(do2, pr)[hrows]
            dn = _rope64_t(dq2 * scale, cosv, sinv)
            _acc_row(dg_ref, 0, jnp.sum(dn * xh, axis=0, keepdims=True))
            dxh = dn * gv
            dq_ref[:, 128 * j:128 * j + 128] = (rs * (dxh - xh * _pair_mean(dxh * xh))).astype(BF16)
        dk_all = jnp.concatenate(dkt, axis=0).T
        dv_all = jnp.concatenate(dvt, axis=0).T
        dk_ref[0:lc, :] += dk_all[0:lc]
        dv_ref[0:lc, :] += dv_all[0:lc]
        dk_ref[pl.ds(start, span), :] += dk_all[lc:nk]
        dv_ref[pl.ds(start, span), :] += dv_all[lc:nk]

    qblk = pl.BlockSpec((QB, 128), lambda i: (i, 0))
    full = pl.BlockSpec((t, 128), lambda i: (0, 0))
    small = pl.BlockSpec((8, 128), lambda i: (0, 0))
    return _pcall(
        body, name=name, grid=(t // QB,),
        in_specs=[pl.BlockSpec((QB, 512), lambda i: (i, 0)), full,
                  pl.BlockSpec((t, 128), lambda i: (0, 5)),
                  pl.BlockSpec((1, 128), lambda i: (0, 0)),
                  pl.BlockSpec(memory_space=pltpu.SMEM), qblk, qblk,
                  pl.BlockSpec((QB, 512), lambda i: (i, 0))],
        out_specs=[pl.BlockSpec((QB, 512), lambda i: (i, 0)), full, full, small, small],
        out_shape=[jax.ShapeDtypeStruct((t, 512), BF16), jax.ShapeDtypeStruct((t, 128), F32),
                   jax.ShapeDtypeStruct((t, 128), F32), jax.ShapeDtypeStruct((8, 128), F32),
                   jax.ShapeDtypeStruct((8, 128), F32)],
    )(p, kp, p, gq, sink, cos, sin, dmix)


def _tri(rev):
    r, c = _iota((CHUNK, CHUNK), 0), _iota((CHUNK, CHUNK), 1)
    return (c >= r) if rev else (c <= r)


def _blk_map(nb, rev, backward):
    if not rev:
        return (lambda n: nb - 1 - n) if backward else (lambda n: n)
    if backward:
        return lambda n: jnp.where(n < nb - 1, n + 1, 0)
    return lambda n: jnp.where(n == 0, 0, nb - n)


def _chunk_order(rev, backward, nc=TM // CHUNK):
    order = list(range(nc))
    return order[::-1] if (rev != backward) else order


def _hgrn_gates(qraw, fraw, lb):
    sq = _sigmoid(qraw)
    sf = _sigmoid(fraw)
    f = lb + (1.0 - lb) * sf
    return qraw * sq, 1.0 - f, jnp.log(f), sq, sf, f


HGRN_HP = 4


def _chunk_cumsum(x, rev):
    n = x.shape[0]
    pos = _iota(x.shape, 0) & (CHUNK - 1)
    s = 1
    while s < CHUNK:
        if rev:
            x = x + jnp.where(pos < CHUNK - s, pltpu.roll(x, n - s, 0), 0.0)
        else:
            x = x + jnp.where(pos >= s, pltpu.roll(x, s, 0), 0.0)
        s *= 2
    return x


def _block_terms(lf, rev):
    b = _chunk_cumsum(lf, rev)
    mid, last = (CHUNK // 2 - 1, 0) if rev else (CHUNK // 2, CHUNK - 1)

    def chunk_row(off):
        return jnp.concatenate([jnp.broadcast_to(b[c * CHUNK + off:c * CHUNK + off + 1, :], (CHUNK, b.shape[1]))
                                for c in range(TM // CHUNK)], axis=0)

    r, bl = chunk_row(mid), chunk_row(last)
    return _tri(rev), jnp.exp(b - r), jnp.exp(r - b), jnp.exp(b), jnp.exp(bl - b), jnp.exp(bl)


def _headnorm_apply(o, gv, gain):
    n = o * lax.rsqrt(jnp.mean(o * o, axis=-1, keepdims=True) + EPS)
    if gain is not None:
        n = n * gain
    return (n * (gv * _sigmoid(gv))).astype(BF16)


def _headnorm_grad(o, gv, dy, gain):
    rs = lax.rsqrt(jnp.mean(o * o, axis=-1, keepdims=True) + EPS)
    xh = o * rs
    n = xh * gain if gain is not None else xh
    sg = _sigmoid(gv)
    dn = dy * (gv * sg)
    dg = (dy * n * (sg * (1.0 + gv * (1.0 - sg)))).astype(BF16)
    dgain = jnp.sum(dn * xh, axis=0, keepdims=True)
    dxh = dn * gain if gain is not None else dn
    return rs * (dxh - xh * jnp.mean(dxh * xh, axis=-1, keepdims=True)), dg, dgain


def _hgrn_cols(bmap, n2, c0):
    return [pl.BlockSpec((TM, 256), lambda h, n, b=b: (bmap(n), c0 // 2 + h * n2 + b)) for b in range(n2)]


def _head_cols(refs, hh):
    return refs[hh // 2][:, 128 * (hh % 2):128 * (hh % 2) + 128]


def _hgrn_fwd(p, lb, *, rev, name, ofw=None, gain=None):
    t = p.shape[0]
    nb, nc = t // TM, TM // CHUNK
    bmap = _blk_map(nb, rev, False)
    fcol = 14 if rev else 10
    fused = ofw is not None

    n2 = HGRN_HP // 2

    def body(*refs):
        q_refs, f_refs, v_refs, lb_ref = refs[:n2], refs[n2:2 * n2], refs[2 * n2:3 * n2], refs[3 * n2]
        rest = refs[3 * n2 + 1:]
        if fused:
            ofw_ref, g_refs, gain_ref = rest[0], rest[1:1 + n2], rest[1 + n2]
            o_ref, sh_ref, mix_ref, st = rest[2 + n2:]
        else:
            o_ref, sh_ref, st = rest

        @pl.when(pl.program_id(1) == 0)
        def _():
            st[...] = jnp.zeros_like(st)
        for hh in range(HGRN_HP):
            ln = slice(128 * hh, 128 * hh + 128)
            q, k, lf, _, _, _ = _hgrn_gates(_head_cols(q_refs, hh), _head_cols(f_refs, hh), lb_ref[:, ln])
            tri, eq, ek, ei, eki, eb = _block_terms(lf, rev)
            qe, ke, qi, ki, vb = _bf(q * eq), _bf(k * ek), _bf(q * ei), _bf(k * eki), _bf(_head_cols(v_refs, hh))
            intra = []
            for cc in range(nc):
                rows = slice(cc * CHUNK, (cc + 1) * CHUNK)
                a = jnp.where(tri, _dot_nt(qe[rows], ke[rows]), 0.0)
                intra.append(_dot(a, vb[rows]))
            s = st[hh]
            for cc in _chunk_order(rev, False):
                rows = slice(cc * CHUNK, (cc + 1) * CHUNK)
                sh_ref[hh, cc] = s
                o_ref[rows, ln] = intra[cc] + _dot_nt(qi[rows], s)
                s = s * eb[cc * CHUNK:cc * CHUNK + 1, :] + _dot_tn(vb[rows], ki[rows])
            st[hh] = s
            if fused:
                osum = o_ref[:, ln] + ofw_ref[:, ln]
                o_ref[:, ln] = osum
                mix_ref[:, ln] = _headnorm_apply(osum, _head_cols(g_refs, hh), gain_ref[...])

    hp, wd = HGRN_HP, 128 * HGRN_HP
    col = functools.partial(_hgrn_cols, bmap, n2)
    oblk = pl.BlockSpec((TM, wd), lambda h, n: (bmap(n), h))
    ins = [p] * (3 * n2) + [lb]
    specs = col(6) + col(fcol) + col(18) + [pl.BlockSpec((1, wd), lambda h, n: (0, h))]
    out_specs = [oblk, pl.BlockSpec((hp, nc, 128, 128), lambda h, n: (h, bmap(n), 0, 0))]
    out_shape = [jax.ShapeDtypeStruct((t, 512), F32), jax.ShapeDtypeStruct((4, t // CHUNK, 128, 128), F32)]
    if fused:
        ins += [ofw] + [p] * n2 + [gain]
        specs += [oblk] + col(22) + [pl.BlockSpec((1, 128), lambda h, n: (0, 0))]
        out_specs.append(oblk)
        out_shape.append(jax.ShapeDtypeStruct((t, 512), BF16))
    return _pcall(body, name=name, grid=(4 // hp, nb), in_specs=specs, out_specs=out_specs, out_shape=out_shape,
                  scratch_shapes=[pltpu.VMEM((hp, 128, 128), F32)])(*ins)


def _hgrn_bwd(p, lb, sh, do, prev, *, rev, name, head=None):
    t = p.shape[0]
    nb, nc = t // TM, TM // CHUNK
    bmap = _blk_map(nb, rev, True)
    fcol = 14 if rev else 10
    has_prev = prev is not None
    odt = BF16 if has_prev else F32
    fused = head is not None

    n2 = HGRN_HP // 2

    def body(*refs):
        refs = list(refs)
        q_refs, f_refs, v_refs = refs[:n2], refs[n2:2 * n2], refs[2 * n2:3 * n2]
        lb_ref, sh_ref = refs[3 * n2], refs[3 * n2 + 1]
        pos = 3 * n2 + 2
        if fused:
            osum_ref, g_refs, dmix_ref, gain_ref = refs[pos], refs[pos + 1:pos + 1 + n2], refs[pos + 1 + n2], refs[pos + 2 + n2]
            pos += 3 + n2
        else:
            do_ref = refs[pos]
            pos += 1
        if has_prev:
            pq_ref, pv_ref = refs[pos], refs[pos + 1]
            pos += 2
        dq_ref, df_ref, dv_ref, dlb_ref = refs[pos:pos + 4]
        pos += 4
        if fused:
            do_out, dg_ref, dgain_ref = refs[pos:pos + 3]
            pos += 3
        dst = refs[pos]

        @pl.when(pl.program_id(1) == 0)
        def _():
            dst[...] = jnp.zeros_like(dst)
            dlb_ref[...] = jnp.zeros_like(dlb_ref)

        if fused:
            @pl.when((pl.program_id(0) == 0) & (pl.program_id(1) == 0))
            def _():
                dgain_ref[...] = jnp.zeros_like(dgain_ref)

        cat = functools.partial(jnp.concatenate, axis=0)
        for hh in range(HGRN_HP):
            ln = slice(128 * hh, 128 * hh + 128)
            lbv = lb_ref[:, ln]
            qraw, fraw = _head_cols(q_refs, hh), _head_cols(f_refs, hh)
            q, k, lf, sq, sf, f = _hgrn_gates(qraw, fraw, lbv)
            tri, eq, ek, ei, eki, eb = _block_terms(lf, rev)
            qe, ke, qi, ki = q * eq, k * ek, q * ei, k * eki
            if fused:
                dov, dg, dgain = _headnorm_grad(osum_ref[:, ln], _head_cols(g_refs, hh), dmix_ref[:, ln], gain_ref[...])
                do_out[:, ln] = dov
                dg_ref[:, ln] = dg
                _acc_row(dgain_ref, 0, dgain)
            else:
                dov = do_ref[:, ln]
            qeb, keb, qib, kib, vb, dob = _bf(qe), _bf(ke), _bf(qi), _bf(ki), _bf(_head_cols(v_refs, hh)), _bf(dov)
            dv, dqe, dke, dqi = [None] * nc, [None] * nc, [None] * nc, [None] * nc
            for cc in range(nc):
                rows = slice(cc * CHUNK, (cc + 1) * CHUNK)
                a = jnp.where(tri, _dot_nt(qeb[rows], keb[rows]), 0.0)
                da = jnp.where(tri, _dot_nt(dob[rows], vb[rows]), 0.0)
                dv[cc] = _dot_tn(a, dob[rows])
                dqe[cc], dke[cc] = _dot(da, keb[rows]), _dot_tn(da, qeb[rows])
                dqi[cc] = _dot(dob[rows], sh_ref[hh, cc])
            dki, dbl = [None] * nc, [None] * nc
            ds = dst[hh]
            for cc in _chunk_order(rev, True):
                rows = slice(cc * CHUNK, (cc + 1) * CHUNK)
                ebc = eb[cc * CHUNK:cc * CHUNK + 1, :]
                dv[cc] = dv[cc] + _dot_nt(kib[rows], ds)
                dki[cc] = _dot(vb[rows], ds)
                dbl[cc] = jnp.broadcast_to(jnp.sum(dki[cc] * ki[rows], axis=0, keepdims=True)
                                           + jnp.sum(ds * sh_ref[hh, cc], axis=0, keepdims=True) * ebc, (CHUNK, 128))
                ds = ds * ebc + _dot_tn(dob[rows], qib[rows])
            dst[hh] = ds
            dqe, dke, dqi, dki, dv, dbl = cat(dqe), cat(dke), cat(dqi), cat(dki), cat(dv), cat(dbl)
            dq = dqe * eq + dqi * ei
            dk = dke * ek + dki * eki
            last = 0 if rev else CHUNK - 1
            db = dqe * qe - dke * ke + dqi * qi - dki * ki
            db = db + jnp.where((_iota(db.shape, 0) & (CHUNK - 1)) == last, dbl, 0.0)
            dlf = _chunk_cumsum(db, not rev)
            dqr = dq * (sq * (1.0 + qraw * (1.0 - sq)))
            dfv = dlf / f - dk
            dfr = dfv * (1.0 - lbv) * (sf * (1.0 - sf))
            dlb_ref[:, ln] += jnp.sum(dfv * (1.0 - sf), axis=0, keepdims=True)
            if has_prev:
                dqr = dqr + pq_ref[:, ln]
                dv = dv + pv_ref[:, ln]
            dq_ref[:, ln] = dqr.astype(odt)
            df_ref[:, ln] = dfr.astype(odt)
            dv_ref[:, ln] = dv.astype(odt)

    hp, wd = HGRN_HP, 128 * HGRN_HP
    col = functools.partial(_hgrn_cols, bmap, n2)
    oblk = pl.BlockSpec((TM, wd), lambda h, n: (bmap(n), h))
    ins = [p] * (3 * n2) + [lb, sh]
    specs = col(6) + col(fcol) + col(18) + [pl.BlockSpec((1, wd), lambda h, n: (0, h)),
                                            pl.BlockSpec((hp, nc, 128, 128), lambda h, n: (h, bmap(n), 0, 0))]
    if fused:
        osum, dmix, gain = head
        ins += [osum] + [p] * n2 + [dmix, gain]
        specs += [oblk] + col(22) + [pl.BlockSpec((TM, wd), lambda h, n: (bmap(n), 4 // hp + h)),
                                     pl.BlockSpec((1, 128), lambda h, n: (0, 0))]
    else:
        ins.append(do); specs.append(oblk)
    if has_prev:
        ins += list(prev); specs += [oblk, oblk]
    out_specs = [oblk, oblk, oblk, pl.BlockSpec((1, wd), lambda h, n: (0, h))]
    out_shape = [jax.ShapeDtypeStruct((t, 512), odt)] * 3 + [jax.ShapeDtypeStruct((1, 512), F32)]
    if fused:
        out_specs += [oblk, oblk, pl.BlockSpec((8, 128), lambda h, n: (0, 0))]
        out_shape += [jax.ShapeDtypeStruct((t, 512), F32), jax.ShapeDtypeStruct((t, 512), BF16),
                      jax.ShapeDtypeStruct((8, 128), F32)]
    return _pcall(body, name=name, grid=(4 // hp, nb), in_specs=specs, out_specs=out_specs, out_shape=out_shape,
                  scratch_shapes=[pltpu.VMEM((hp, 128, 128), F32)])(*ins)


def _rope256(x, cos, sin):
    x1, x2 = x[:, 0:128], x[:, 128:256]
    return jnp.concatenate([x1 * cos - x2 * sin, x2 * cos + x1 * sin], axis=-1)


def _rope256_t(d, cos, sin):
    d1, d2 = d[:, 0:128], d[:, 128:256]
    return jnp.concatenate([d1 * cos + d2 * sin, d2 * cos - d1 * sin], axis=-1)


RET_DK, RET_DV, RET_H = 256, 512, 4
RET_KSCALE = RET_DK ** -0.5
RCH = TM
RET_HP = 4


def _ret_terms(lg, rev):
    r, c = _iota((RCH, RCH), 0), _iota((RCH, RCH), 1)
    rel = ((c - r) if rev else (r - c)).astype(F32)
    dmat = jnp.where(rel >= 0, jnp.exp(lg[:, 0:1] * jnp.maximum(rel, 0.0)), 0.0)
    pos = _iota((RCH, 1), 0).astype(F32)
    cnt = (RCH - pos) if rev else (pos + 1.0)
    ei = jnp.exp(lg * cnt)
    eki = jnp.exp(lg * (RCH - cnt))
    eb = jnp.exp(lg * float(RCH))
    return dmat, ei, eki, eb


def _ret_fwd(p, lgt, cos, sin, *, rev, name, ofw=None):
    t = p.shape[0]
    nb, nc = t // TM, TM // RCH
    bmap = _blk_map(nb, rev, False)
    fused = ofw is not None

    def body(*refs):
        q_ref, k_ref, v_ref, lg_ref, c_ref, s_ref = refs[:6]
        if fused:
            ofw_ref, g_ref, o_ref, sh_ref, mix_ref, st = refs[6:]
        else:
            o_ref, sh_ref, st = refs[6:]

        @pl.when(pl.program_id(1) == 0)
        def _():
            st[...] = jnp.zeros_like(st)
        for hh in range(RET_HP):
            qc, vc = slice(RET_DK * hh, RET_DK * (hh + 1)), slice(RET_DV * hh, RET_DV * (hh + 1))
            dmat, ei, eki, eb = _ret_terms(lg_ref[hh], rev)
            for cc in _chunk_order(rev, False, nc):
                rows = slice(cc * RCH, (cc + 1) * RCH)
                cosv, sinv = c_ref[rows, :], s_ref[rows, :]
                q = _rope256(q_ref[rows, qc].astype(F32), cosv, sinv)
                k = _rope256(k_ref[rows, qc].astype(F32), cosv, sinv) * RET_KSCALE
                v = v_ref[rows, vc]
                s0 = st[hh]
                sh_ref[hh, cc] = s0.astype(BF16)
                a = _dot_nt(q, k) * dmat
                o = _dot(a, v) + _dot_nt(q * ei, s0)
                st[hh] = s0 * eb + _dot_tn(v, k * eki)
                if fused:
                    o = o + ofw_ref[rows, vc]
                    mix_ref[rows, vc] = _headnorm_apply(o, g_ref[rows, vc].astype(F32), None)
                o_ref[rows, vc] = o

    hp = RET_HP
    tab = pl.BlockSpec((TM, 128), lambda h, n: (bmap(n), 0))
    oblk = pl.BlockSpec((TM, hp * RET_DV), lambda h, n: (bmap(n), h))
    ins = [p, p, p, lgt, cos, sin]
    specs = [pl.BlockSpec((TM, hp * RET_DK), lambda h, n: (bmap(n), h)),
             pl.BlockSpec((TM, hp * RET_DK), lambda h, n: (bmap(n), RET_H // hp + h)),
             pl.BlockSpec((TM, hp * RET_DV), lambda h, n: (bmap(n), RET_H // hp + h)),
             pl.BlockSpec((hp, 1, RET_DK), lambda h, n: (h, 0, 0)), tab, tab]
    out_specs = [oblk, pl.BlockSpec((hp, nc, RET_DV, RET_DK), lambda h, n: (h, bmap(n), 0, 0))]
    out_shape = [jax.ShapeDtypeStruct((t, RET_H * RET_DV), F32),
                 jax.ShapeDtypeStruct((RET_H, t // RCH, RET_DV, RET_DK), BF16)]
    if fused:
        ins += [ofw, p]
        specs += [oblk, pl.BlockSpec((TM, hp * RET_DV), lambda h, n: (bmap(n), 2 * RET_H // hp + h))]
        out_specs.append(oblk)
        out_shape.append(jax.ShapeDtypeStruct((t, RET_H * RET_DV), BF16))
    return _pcall(body, name=name, grid=(RET_H // hp, nb), in_specs=specs, out_specs=out_specs, out_shape=out_shape,
                  scratch_shapes=[pltpu.VMEM((hp, RET_DV, RET_DK), F32)])(*ins)


def _ret_bwd(p, lgt, cos, sin, sh, do, prev, *, rev, name, head=None):
    t = p.shape[0]
    nb, nc = t // TM, TM // RCH
    bmap = _blk_map(nb, rev, True)
    has_prev = prev is not None
    odt = BF16 if has_prev else F32
    fused = head is not None

    def body(*refs):
        refs = list(refs)
        q_ref, k_ref, v_ref, lg_ref, c_ref, s_ref, sh_ref = refs[:7]
        if fused:
            osum_ref, g_ref, dmix_ref = refs[7:10]
            pos = 10
        else:
            do_ref = refs[7]
            pos = 8
        if has_prev:
            pq_ref, pk_ref, pv_ref = refs[pos:pos + 3]
            pos += 3
        dq_ref, dk_ref, dv_ref = refs[pos:pos + 3]
        pos += 3
        if fused:
            do_out, dg_ref = refs[pos:pos + 2]
            pos += 2
        dst = refs[pos]

        @pl.when(pl.program_id(1) == 0)
        def _():
            dst[...] = jnp.zeros_like(dst)

        for hh in range(RET_HP):
            qc, vc = slice(RET_DK * hh, RET_DK * (hh + 1)), slice(RET_DV * hh, RET_DV * (hh + 1))
            dmat, ei, eki, eb = _ret_terms(lg_ref[hh], rev)
            for cc in _chunk_order(rev, True, nc):
                rows = slice(cc * RCH, (cc + 1) * RCH)
                cosv, sinv = c_ref[rows, :], s_ref[rows, :]
                q = _rope256(q_ref[rows, qc].astype(F32), cosv, sinv)
                k = _rope256(k_ref[rows, qc].astype(F32), cosv, sinv) * RET_KSCALE
                v = v_ref[rows, vc]
                if fused:
                    dov, dg, _ = _headnorm_grad(osum_ref[rows, vc], g_ref[rows, vc].astype(F32), dmix_ref[rows, vc], None)
                    do_out[rows, vc] = dov
                    dg_ref[rows, vc] = dg
                else:
                    dov = do_ref[rows, vc]
                s0 = sh_ref[hh, cc]
                dsc = dst[hh]
                qi, ki = q * ei, k * eki
                a = _dot_nt(q, k) * dmat
                da = _dot_nt(dov, v) * dmat
                dv = _dot_tn(a, dov) + _dot_nt(ki, dsc)
                dqs = _dot(da, k) + _dot(dov, s0) * ei
                dks = _dot_tn(da, q) + _dot(v, dsc) * eki
                dst[hh] = dsc * eb + _dot_tn(dov, qi)
                dq = _rope256_t(dqs, cosv, sinv)
                dk = _rope256_t(dks * RET_KSCALE, cosv, sinv)
                if has_prev:
                    dq = dq + pq_ref[rows, qc]
                    dk = dk + pk_ref[rows, qc]
                    dv = dv + pv_ref[rows, vc]
                dq_ref[rows, qc] = dq.astype(odt)
                dk_ref[rows, qc] = dk.astype(odt)
                dv_ref[rows, vc] = dv.astype(odt)

    hp = RET_HP
    tab = pl.BlockSpec((TM, 128), lambda h, n: (bmap(n), 0))
    qblk = pl.BlockSpec((TM, hp * RET_DK), lambda h, n: (bmap(n), h))
    vblk = pl.BlockSpec((TM, hp * RET_DV), lambda h, n: (bmap(n), h))
    ins = [p, p, p, lgt, cos, sin, sh]
    specs = [qblk, pl.BlockSpec((TM, hp * RET_DK), lambda h, n: (bmap(n), RET_H // hp + h)),
             pl.BlockSpec((TM, hp * RET_DV), lambda h, n: (bmap(n), RET_H // hp + h)),
             pl.BlockSpec((hp, 1, RET_DK), lambda h, n: (h, 0, 0)), tab, tab,
             pl.BlockSpec((hp, nc, RET_DV, RET_DK), lambda h, n: (h, bmap(n), 0, 0))]
    if fused:
        osum, dmix = head
        ins += [osum, p, dmix]
        specs += [vblk, pl.BlockSpec((TM, hp * RET_DV), lambda h, n: (bmap(n), 2 * RET_H // hp + h)), vblk]
    else:
        ins.append(do); specs.append(vblk)
    if has_prev:
        ins += list(prev); specs += [qblk, qblk, vblk]
    out_specs = [qblk, qblk, vblk]
    out_shape = [jax.ShapeDtypeStruct((t, RET_H * RET_DK), odt), jax.ShapeDtypeStruct((t, RET_H * RET_DK), odt),
                 jax.ShapeDtypeStruct((t, RET_H * RET_DV), odt)]
    if fused:
        out_specs += [vblk, vblk]
        out_shape += [jax.ShapeDtypeStruct((t, RET_H * RET_DV), F32), jax.ShapeDtypeStruct((t, RET_H * RET_DV), BF16)]
    return _pcall(body, name=name, grid=(RET_H // hp, nb), in_specs=specs, out_specs=out_specs, out_shape=out_shape,
                  scratch_shapes=[pltpu.VMEM((hp, RET_DV, RET_DK), F32)])(*ins)


def _rope_tables(lc, l):
    tt = jnp.arange(l)
    row, colp = (tt // 64).astype(F32), (tt % 64).astype(F32)
    inv = 10000.0 ** (-jnp.arange(16, dtype=F32) / 16)
    ang = jnp.concatenate([row[:, None] * inv, colp[:, None] * inv], axis=-1)
    ang = jnp.concatenate([jnp.zeros((lc, 32), F32), ang], axis=0)
    acos, asin = jnp.tile(jnp.cos(ang), (1, 4)), jnp.tile(jnp.sin(ang), (1, 4))
    theta = 1.0 / (10000.0 ** jnp.linspace(0.0, 1.0, 128, dtype=F32))
    rang = jnp.arange(l, dtype=F32)[:, None] * theta
    rang = jnp.concatenate([jnp.zeros((lc, 128), F32), rang], axis=0)
    return acos, asin, jnp.cos(rang), jnp.sin(rang)


class _Weights:
    def __init__(self, w):
        self.w = w

    def first(self, after):
        return self.w

    def rest_landed(self, after):
        pass

    def rest(self, after):
        return self.w

    def send_grads(self, grp, grads):
        return jnp.zeros((8, 128), F32)


def _local_step(x0, target, mods, ng, wsrc, small):
    t, d = x0.shape
    l = target.shape[0]
    lc = t - l
    acos, asin, rcos, rsin = _rope_tables(lc, l)
    lg_fw = jnp.log(1.0 - 2.0 ** (-5.0 - jnp.arange(RET_H, dtype=F32)))
    lgt_fw = jnp.broadcast_to(lg_fw[:, None, None], (RET_H, 1, RET_DK))
    lgt_bw = jnp.broadcast_to(lg_fw[::-1][:, None, None], (RET_H, 1, RET_DK))
    gq, gk, sink, gain, lb = small['gq'], small['gk'], small['sink'], small['gain'], small['lb']

    (h1,) = _row_fwd(x0, mods, g=ng[0], shift=0, scale=1, name='l0_norm1')
    w = wsrc.first(h1)
    p0 = _mm_nn(h1, w['even_in'], name='l0_in')
    kp = _kprep_fwd(p0, gk, acos, asin, name='l0_kprep')
    att = _attn_fwd(p0, kp, gq, sink, acos, asin, lc=lc, name='l0_attn')
    hof, hsf = _hgrn_fwd(p0, lb, rev=False, name='l0_hgrn_f')
    wsrc.rest_landed(hof)
    hos, hsb, bmix = _hgrn_fwd(p0, lb, rev=True, name='l0_hgrn_b', ofw=hof, gain=gain)
    mix0 = jnp.concatenate([att, bmix], axis=1)
    y0 = _mm_nn(mix0, w['even_out'], name='l0_out')
    x1, h2 = _row_fwd(x0, mods, y=y0, gate=2, g=ng[1], shift=3, scale=4, name='l0_norm2')
    w = dict(w, **wsrc.rest(h2))
    u0, a0 = _ffn_in(h2, w['ffn_in'], lead=0, name='ffn_in')
    z0 = _mm_nn(a0, w['ffn_out'], lead=0, name='ffn_out')
    x2, h3 = _row_fwd(x1, mods, y=z0, gate=5, g=ng[2], shift=12, scale=13, name='l1_norm1')
    p1 = _mm_nn(h3, w['odd_in'], out_dtype=BF16, name='l1_in')
    rof, rsf = _ret_fwd(p1, lgt_fw, rcos, rsin, rev=False, name='l1_ret_f')
    ros, rsb, mix1 = _ret_fwd(p1, lgt_bw, rcos, rsin, rev=True, name='l1_ret_b', ofw=rof)
    y1 = _mm_nn(mix1, w['odd_out'], name='l1_out')
    x3, h4 = _row_fwd(x2, mods, y=y1, gate=14, g=ng[3], shift=15, scale=16, name='l1_norm2')
    u1, a1 = _ffn_in(h4, w['ffn_in'], lead=1, name='ffn_in')
    z1 = _mm_nn(a1, w['ffn_out'], lead=1, name='ffn_out')
    loss, dx4, dz1, s_fin = _row_final(x3, z1, mods, target, gate=17, name='loss')

    du1 = _ffn_dx(dz1, w['ffn_out'], u1, lead=1, name='ffn_out_dx')
    g_ffn_out1 = _mm_tn(a1, dz1, name='ffn_out_dw')
    dh4 = _mm_nt(du1, w['ffn_in'], lead=1, name='ffn_in_dx')
    g_ffn_in1 = _mm_tn(h4, du1, name='ffn_in_dw')
    dx3, dy1, s_l1n2 = _row_bwd(x3, dx4, dh4, mods, ng[3], shift=15, scale=16, y=y1, gate=14, name='l1_norm2_bwd')
    dmix1 = _mm_nt(dy1, w['odd_out'], name='l1_out_dx')
    g_odd_out = _mm_tn(mix1, dy1, name='l1_out_dw')
    rdq, rdk, rdv, rdo, rdg = _ret_bwd(p1, lgt_fw, rcos, rsin, rsf, None, None, rev=False, name='l1_ret_f_bwd',
                                       head=(ros, dmix1))
    rdq, rdk, rdv = _ret_bwd(p1, lgt_bw, rcos, rsin, rsb, rdo, (rdq, rdk, rdv), rev=True, name='l1_ret_b_bwd')
    dp1 = jnp.concatenate([rdq, rdk, rdv, rdg], axis=1)
    dh3 = _mm_nt(dp1, w['odd_in'], name='l1_in_dx')
    g_odd_in = _mm_tn(h3, dp1, name='l1_in_dw')
    mods = mods + wsrc.send_grads('early', dict(ffn_in1=g_ffn_in1, ffn_out1=g_ffn_out1, odd_in=g_odd_in,
                                                odd_out=g_odd_out))[0, 0]
    dx2, dz0, s_l1n1 = _row_bwd(x2, dx3, dh3, mods, ng[2], shift=12, scale=13, y=z0, gate=5, name='l1_norm1_bwd')
    du0 = _ffn_dx(dz0, w['ffn_out'], u0, lead=0, name='ffn_out_dx')
    g_ffn_out0 = _mm_tn(a0, dz0, name='ffn_out_dw')
    dh2 = _mm_nt(du0, w['ffn_in'], lead=0, name='ffn_in_dx')
    g_ffn_in0 = _mm_tn(h2, du0, name='ffn_in_dw')
    mods = mods + wsrc.send_grads('mid', dict(ffn_in0=g_ffn_in0, ffn_out0=g_ffn_out0))[0, 0]
    dx1, dy0, s_l0n2 = _row_bwd(x1, dx2, dh2, mods, ng[1], shift=3, scale=4, y=y0, gate=2, name='l0_norm2_bwd')
    dmix0 = _mm_nt(dy0, w['even_out'], name='l0_out_dx')
    g_even_out = _mm_tn(mix0, dy0, name='l0_out_dw')
    hq, hff, hv, dlb_f, hdo, hdg, s_gain = _hgrn_bwd(p0, lb, hsf, None, None, rev=False, name='l0_hgrn_f_bwd',
                                                     head=(hos, dmix0, gain))
    hq, hfb, hv, dlb_b = _hgrn_bwd(p0, lb, hsb, hdo, (hq, hv), rev=True, name='l0_hgrn_b_bwd')
    adq, dkp, adv, s_gq, s_sink = _attn_bwd(p0, kp, gq, sink, acos, asin, dmix0, lc=lc, name='l0_attn_bwd')
    dkv, s_gk = _kprep_bwd(p0, gk, acos, asin, dkp, adv, name='l0_kprep_bwd')
    dp0 = jnp.concatenate([adq, dkv, hq, _bf(hff), hfb, hv, hdg], axis=1)
    dh1 = _mm_nt(dp0, w['even_in'], name='l0_in_dx')
    g_even_in = _mm_tn(h1, dp0, name='l0_in_dw')
    dx0, s_l0n1 = _row_bwd(x0, dx1, dh1, mods, ng[0], shift=0, scale=1, latent_only=True, name='l0_norm1_bwd')

    grads = dict(ffn_in0=g_ffn_in0, ffn_in1=g_ffn_in1, ffn_out0=g_ffn_out0, ffn_out1=g_ffn_out1,
                 even_in=g_even_in, even_out=g_even_out, odd_in=g_odd_in, odd_out=g_odd_out)
    sums = dict(fin=s_fin, l1n2=s_l1n2, l1n1=s_l1n1, l0n2=s_l0n2, l0n1=s_l0n1, gain=s_gain, gq=s_gq, gk=s_gk,
                sink=s_sink, dlb_f=dlb_f, dlb_b=dlb_b)
    return loss, dx0, grads, sums


def _place():
    return lax.axis_index("x"), lax.axis_index("y"), lax.axis_index("c")


def _ag8(blk, *, name):
    r, c = blk.shape
    flips = [(dx, dy, dc) for dx in (0, 1) for dy in (0, 1) for dc in (0, 1) if (dx, dy, dc) != (0, 0, 0)]

    def body(x_ref, out_ref, send_sems, recv_sems, local_sem):
        ax, ay, ac = _place()
        me = 4 * ax + 2 * ay + ac
        mine = pltpu.make_async_copy(x_ref, out_ref.at[me], local_sem)
        mine.start()
        sent = []
        for k, (dx, dy, dc) in enumerate(flips):
            peer = (lax.rem(ax + dx, 2), lax.rem(ay + dy, 2), lax.rem(ac + dc, 2))
            cp = pltpu.make_async_remote_copy(src_ref=x_ref, dst_ref=out_ref.at[me], send_sem=send_sems.at[k],
                                              recv_sem=recv_sems.at[k], device_id=peer, device_id_type=MESH)
            cp.start()
            sent.append((cp, 4 * peer[0] + 2 * peer[1] + peer[2]))
        for k, (cp, pidx) in enumerate(sent):
            pltpu.make_async_remote_copy(src_ref=x_ref, dst_ref=out_ref.at[pidx], send_sem=send_sems.at[k],
                                         recv_sem=recv_sems.at[k], device_id=(ax, ay, ac),
                                         device_id_type=MESH).wait_recv()
        for cp, _ in sent:
            cp.wait_send()
        mine.wait()

    return _pcall(
        body, name=name,
        in_specs=[pl.BlockSpec(memory_space=pltpu.VMEM)],
        out_specs=pl.BlockSpec(memory_space=pltpu.VMEM),
        out_shape=jax.ShapeDtypeStruct((8, r, c), blk.dtype),
        scratch_shapes=[pltpu.SemaphoreType.DMA((7,)), pltpu.SemaphoreType.DMA((7,)), pltpu.SemaphoreType.DMA],
    )(blk)


_HBM = pl.BlockSpec(memory_space=pltpu.HBM)
_SEM = pl.BlockSpec(memory_space=pltpu.SEMAPHORE)
_DATAFLOW = pltpu.SideEffectType.DATAFLOW_SIDE_EFFECTING


def _split_start(bufs, plan, k, *, name):
    n = len(bufs)

    def body(*refs):
        ins, send_sems, recv_sems, token = refs[:n], refs[n], refs[n + 1], refs[2 * n + 2]
        for i, (src, dst, dev) in enumerate(plan(ins)):
            pltpu.make_async_remote_copy(src_ref=src, dst_ref=dst, send_sem=send_sems.at[i], recv_sem=recv_sems.at[i],
                                         device_id=dev, device_id_type=MESH).start()
        token[...] = jnp.zeros_like(token)

    res = _pcall(
        body, name=name,
        out_shape=(pltpu.SemaphoreType.DMA((k,)), pltpu.SemaphoreType.DMA((k,)),
                   *[pltpu.HBM(b.shape, b.dtype) for b in bufs], jax.ShapeDtypeStruct((8, 128), F32)),
        in_specs=[_HBM] * n, out_specs=(_SEM, _SEM, *[_HBM] * n, pl.BlockSpec(memory_space=pltpu.VMEM)),
        input_output_aliases={i: 2 + i for i in range(n)},
        compiler_params=pltpu.CompilerParams(has_side_effects=_DATAFLOW),
    )(*[pltpu.with_memory_space_constraint(b, pltpu.HBM) for b in bufs])
    return res[0], res[1], list(res[2:2 + n]), res[2 + n]


def _split_wait(bufs, send_sems, recv_sems, plan, after, *, name):
    n = len(bufs)

    def body(*refs):
        ins, ssem, rsem = refs[:n], refs[n], refs[n + 1]
        for i, (src, dst, dev) in enumerate(plan(ins)):
            cp = pltpu.make_async_remote_copy(src_ref=src, dst_ref=dst, send_sem=ssem.at[i], recv_sem=rsem.at[i],
                                              device_id=dev, device_id_type=MESH)
            cp.wait_send()
            cp.wait_recv()

    res = _pcall(
        body, name=name, out_shape=tuple(pltpu.HBM(b.shape, b.dtype) for b in bufs),
        in_specs=[_HBM] * n + [_SEM, _SEM, pl.BlockSpec(memory_space=pl.ANY)], out_specs=tuple([_HBM] * n),
        input_output_aliases={i: i for i in range(n)},
        compiler_params=pltpu.CompilerParams(has_side_effects=_DATAFLOW),
    )(*bufs, send_sems, recv_sems, after)
    return list(res)


_CHIP_FLIPS = [(1, 0), (0, 1), (1, 1)]


class _GatheredWeights:
    FIRST = ('even_in', 'even_out')
    REST = ('ffn_in', 'ffn_out', 'odd_in', 'odd_out')

    def __init__(self, shards, reducer):
        self.shards = shards
        self.send_grads = reducer.start
        self.ici = {}
        for grp, names in (('first', self.FIRST), ('rest', self.REST)):
            src = [shards[nm].reshape(2, shards[nm].shape[0] // 2, shards[nm].shape[1]) for nm in names]
            land = [lax.empty((4,) + a.shape, a.dtype) for a in src]
            m = len(names)
            sends, recvs, bufs, token = _split_start(src + land, functools.partial(self._ici_plan, m, True), 4 * m,
                                                     name='gather_' + grp + '_ici_start')
            self.ici[grp] = (sends, recvs, bufs, m)
            self.token = token if grp == 'first' else self.token + token
        self.rest_d2d = None

    @staticmethod
    def _ici_plan(m, sending, refs):
        ax, ay, ac = _place()
        s = 2 * ax + ay
        out = []
        for a in range(m):
            for dx, dy in _CHIP_FLIPS:
                px, py = lax.rem(ax + dx, 2), lax.rem(ay + dy, 2)
                slot = s if sending else 2 * px + py
                out.append((refs[a].at[ac], refs[m + a].at[slot, ac], (px, py, ac)))
        for a in range(m):
            out.append((refs[a], refs[m + a].at[s], (ax, ay, 1 - ac)))
        return out

    @staticmethod
    def _d2d_plan(m, sending, refs):
        ax, ay, ac = _place()
        out = []
        for a in range(m):
            for dx, dy in _CHIP_FLIPS:
                sp = 2 * lax.rem(ax + dx, 2) + lax.rem(ay + dy, 2)
                out.append((refs[a].at[sp, ac], refs[a].at[sp, ac if sending else 1 - ac], (ax, ay, 1 - ac)))
        return out

    def _landed(self, grp, after):
        sends, recvs, bufs, m = self.ici[grp]
        bufs = _split_wait(bufs, sends, recvs, functools.partial(self._ici_plan, m, False), after,
                           name='gather_' + grp + '_ici_wait')
        sends, recvs, land, _ = _split_start(bufs[m:], functools.partial(self._d2d_plan, m, True), 3 * m,
                                             name='gather_' + grp + '_d2d_start')
        return sends, recvs, land, m

    def _full(self, grp, names, d2d, after):
        sends, recvs, land, m = d2d
        land = _split_wait(land, sends, recvs, functools.partial(self._d2d_plan, m, False), after,
                           name='gather_' + grp + '_d2d_wait')
        return {nm: _from_shards(nm, g.reshape((4,) + self.shards[nm].shape)) for nm, g in zip(names, land)}

    def first(self, after):
        return self._full('first', self.FIRST, self._landed('first', after), after)

    def rest_landed(self, after):
        self.rest_d2d = self._landed('rest', after)

    def rest(self, after):
        return self._full('rest', self.REST, self.rest_d2d, after)


def _to_sibling(arrs, *, name):
    n = len(arrs)

    def body(*refs):
        ins, outs = refs[:n], refs[n:2 * n]
        send_sems, recv_sems = refs[2 * n:]
        ax, ay, ac = _place()
        cps = [pltpu.make_async_remote_copy(src_ref=ins[a], dst_ref=outs[a], send_sem=send_sems.at[a],
                                            recv_sem=recv_sems.at[a], device_id=(ax, ay, 1 - ac),
                                            device_id_type=MESH) for a in range(n)]
        for cp in cps:
            cp.start()
        for cp in cps:
            cp.wait_recv()
        for cp in cps:
            cp.wait_send()

    hbm = pl.BlockSpec(memory_space=pl.ANY)
    return _pcall(
        body, name=name, in_specs=[hbm] * n, out_specs=[hbm] * n,
        out_shape=[jax.ShapeDtypeStruct(a.shape, a.dtype) for a in arrs],
        scratch_shapes=[pltpu.SemaphoreType.DMA((n,))] * 2,
    )(*arrs)


def _mod_fwd(cond_raw, mw, mb, *, name):
    _, d, n = mw.shape

    def body(c_ref, w_ref, b_ref, o_ref):
        cv = c_ref[...]
        o_ref[...] = _dot(cv * _sigmoid(cv), w_ref[...]) + b_ref[...]

    return _pcall(
        body, name=name, grid=(2,),
        in_specs=[pl.BlockSpec((16, d), lambda l: (0, 0)), pl.BlockSpec((None, d, n), lambda l: (l, 0, 0)),
                  pl.BlockSpec((None, 1, n), lambda l: (l, 0, 0))],
        out_specs=pl.BlockSpec((None, 16, n), lambda l: (l, 0, 0)),
        out_shape=jax.ShapeDtypeStruct((2, 16, n), F32),
    )(cond_raw, mw, mb)


def _mod_bwd(cond_raw, dms, mw, *, name):
    _, d, n = mw.shape

    def body(c_ref, dm_ref, w_ref, gw_ref, dc_ref):
        @pl.when(pl.program_id(0) == 0)
        def _():
            dc_ref[...] = jnp.zeros_like(dc_ref)
        cv = c_ref[...]
        gw_ref[...] = _dot_tn(cv * _sigmoid(cv), dm_ref[...])
        dc_ref[...] += _dot_nt(dm_ref[...], w_ref[...])

    return _pcall(
        body, name=name, grid=(2,),
        in_specs=[pl.BlockSpec((16, d), lambda l: (0, 0)), pl.BlockSpec((None, 16, n), lambda l: (l, 0, 0)),
                  pl.BlockSpec((None, d, n), lambda l: (l, 0, 0))],
        out_specs=[pl.BlockSpec((None, d, n), lambda l: (l, 0, 0)), pl.BlockSpec((16, d), lambda l: (0, 0))],
        out_shape=[jax.ShapeDtypeStruct((2, d, n), F32), jax.ShapeDtypeStruct((16, d), F32)],
    )(cond_raw, dms, mw)


def _lb_fwd(hgrn_lb, *, name):
    def body(a_ref, o_ref):
        a0, a1 = a_ref[0:1, :], a_ref[1:2, :]
        m = jnp.maximum(a0, a1)
        e0, e1 = jnp.exp(a0 - m), jnp.exp(a1 - m)
        o_ref[...] = e0 / (e0 + e1)

    return _pcall(body, name=name, out_shape=jax.ShapeDtypeStruct((1, hgrn_lb.shape[1]), F32))(hgrn_lb)


PACK_TILES = ('l0n1', 'l0n2', 'l1n1', 'l1n2', 'fin', 'gq', 'gk', 'gain', 'dlb_f', 'dlb_b', 'sink')
PACK_ROW = {nm: 8 * i for i, nm in enumerate(PACK_TILES)}
MOD_SOURCE = ((('l0n1', 0), ('l0n1', 1), ('l0n2', 2), ('l0n2', 0), ('l0n2', 1), ('l1n1', 2)),
              (('l1n1', 0), ('l1n1', 1), ('l1n2', 2), ('l1n2', 0), ('l1n2', 1), ('fin', 2)))


def _small_finalize(gath, lb_pad, *, name):
    d = gath.shape[2]

    def body(g_ref, lb_ref, small_ref, glb_ref, gmb_ref, dm_ref):
        tot = g_ref[0]
        for e in range(1, 8):
            tot = tot + g_ref[e]

        def row(nm, r=0):
            return tot[PACK_ROW[nm] + r:PACK_ROW[nm] + r + 1, :]

        for k, nm in enumerate(('l0n1', 'l0n2', 'l1n1', 'l1n2')):
            small_ref[k:k + 1, :] = row(nm, 3) + row(nm, 7)
        for k, nm in ((4, 'gq'), (5, 'gk')):
            small_ref[k:k + 1, :] = row(nm) + pltpu.roll(row(nm), d - 64, 1)
        small_ref[6:7, :] = row('gain')
        small_ref[7:8, :] = row('sink')
        lbv = lb_ref[...]
        g0 = (row('dlb_f') + row('dlb_b')) * lbv * (1.0 - lbv)
        glb_ref[...] = jnp.zeros_like(glb_ref)
        glb_ref[0:1, :] = g0
        glb_ref[1:2, :] = -g0
        dm_ref[...] = jnp.zeros_like(dm_ref)
        for l in range(2):
            for part in range(6):
                nm, r = MOD_SOURCE[l][part]
                gmb_ref[l * 6 + part:l * 6 + part + 1, :] = row(nm, r) + row(nm, r + 4)
                rl = PACK_ROW[nm] + r + 4
                for e in range(8):
                    dm_ref[l, part, e:e + 1, :] = g_ref[e, rl:rl + 1, :]
                dm_ref[l, part, 8:9, :] = row(nm, r)

    return _pcall(
        body, name=name,
        out_shape=[jax.ShapeDtypeStruct((8, d), F32), jax.ShapeDtypeStruct((8, d), F32),
                   jax.ShapeDtypeStruct((12, d), F32), jax.ShapeDtypeStruct((2, 6, 16, d), F32)],
    )(gath, lb_pad)


def _cctx_grad(gath, c_ctx2, *, name):
    def body(g_ref, c_ref, o_ref):
        tot = ((g_ref[0, 0:1, :] + g_ref[2, 0:1, :]) + g_ref[4, 0:1, :]) + g_ref[6, 0:1, :]
        cv = c_ref[...]
        s = _sigmoid(cv)
        o_ref[...] = tot * (s * (1.0 + cv * (1.0 - s)))

    return _pcall(body, name=name, out_shape=jax.ShapeDtypeStruct(c_ctx2.shape, F32))(gath, c_ctx2)


def _row_block(r, c, limit=256 * 1024):
    best = None
    for br in range(16, r + 1, 16):
        if r % br == 0 and br * c <= limit:
            best = br
    return best if best is not None else r


def _sum4(own, landed, core, *, name):
    _, r, c = own.shape
    br = _row_block(r, c, 512 * 1024)

    def body(core_ref, own_ref, land_ref, o_ref):
        s = 2 * lax.axis_index("x") + lax.axis_index("y")
        p = [jnp.where(s == k, own_ref[k], land_ref[k]).astype(F32) for k in range(4)]
        o_ref[...] = ((p[0] + p[1]) + p[2]) + p[3]

    blk = pl.BlockSpec((4, br, c), lambda i, core_ref: (0, i, 0))
    spec = pltpu.PrefetchScalarGridSpec(
        num_scalar_prefetch=1, grid=(r // br,), in_specs=[blk, blk],
        out_specs=pl.BlockSpec((None, br, c), lambda i, core_ref: (core_ref[0], i, 0)))
    return _pcall(body, name=name, grid_spec=spec, out_shape=jax.ShapeDtypeStruct((2, r, c), F32))(core, own, landed)


def _exchange_halves(arrs, *, name):
    n = len(arrs)

    def body(*refs):
        ins, outs = refs[:n], refs[n:2 * n]
        send_sems, recv_sems = refs[2 * n:]
        ax, ay, ac = _place()
        cps = [pltpu.make_async_remote_copy(src_ref=ins[a].at[ac], dst_ref=outs[a].at[ac], send_sem=send_sems.at[a],
                                            recv_sem=recv_sems.at[a], device_id=(ax, ay, 1 - ac),
                                            device_id_type=MESH) for a in range(n)]
        for cp in cps:
            cp.start()
        for a in range(n):
            pltpu.make_async_remote_copy(src_ref=ins[a].at[ac], dst_ref=outs[a].at[1 - ac], send_sem=send_sems.at[a],
                                         recv_sem=recv_sems.at[a], device_id=(ax, ay, ac),
                                         device_id_type=MESH).wait_recv()
        for cp in cps:
            cp.wait_send()

    hbm = pl.BlockSpec(memory_space=pl.ANY)
    return _pcall(
        body, name=name, in_specs=[hbm] * n, out_specs=[hbm] * n,
        out_shape=[jax.ShapeDtypeStruct(a.shape, a.dtype) for a in arrs],
        input_output_aliases={a: a for a in range(n)},
        scratch_shapes=[pltpu.SemaphoreType.DMA((n,))] * 2,
    )(*arrs)


def _add2(a, b, *, name):
    r, c = a.shape
    br = _row_block(r, c, 1024 * 1024)

    def body(a_ref, b_ref, o_ref):
        o_ref[...] = (a_ref[...].astype(F32) + b_ref[...].astype(F32)).astype(BF16)

    blk = pl.BlockSpec((br, c), lambda i: (i, 0))
    return _pcall(body, name=name, grid=(r // br,), in_specs=[blk, blk], out_specs=blk,
                  out_shape=jax.ShapeDtypeStruct((r, c), BF16))(a, b)


def _adam(w, gs, m, v, *, name):
    r, c = w.shape
    br = _row_block(r, c)
    ng = len(gs)
    c1 = 1.0 - ADAM_B1 ** ADAM_STEP
    c2 = 1.0 - ADAM_B2 ** ADAM_STEP

    def body(*refs):
        w_ref, m_ref, v_ref = refs[0], refs[1 + ng], refs[2 + ng]
        outs = refs[3 + ng:]
        g = refs[1][...]
        for k in range(1, ng):
            g = g + refs[1 + k][...]
        mn = ADAM_B1 * m_ref[...] + (1.0 - ADAM_B1) * g
        vn = ADAM_B2 * v_ref[...] + (1.0 - ADAM_B2) * (g * g)
        if ng > 1:
            outs[0][...] = g
        d_out, m_out, v_out = outs[-3:]
        m_out[...] = mn
        v_out[...] = vn
        d_out[...] = -ADAM_LR * ((mn / c1) / (jnp.sqrt(vn / c2) + ADAM_EPS) + ADAM_WD * w_ref[...])

    blk = pl.BlockSpec((br, c), lambda i: (i, 0))
    nout = 4 if ng > 1 else 3
    res = _pcall(body, name=name, grid=(r // br,), in_specs=[blk] * (3 + ng), out_specs=[blk] * nout,
                 out_shape=[jax.ShapeDtypeStruct((r, c), F32)] * nout)(w, *gs, m, v)
    return list(res) if ng > 1 else [gs[0]] + list(res)


def _grad_halves(name, g, ac):
    if name.endswith('_in'):
        n = g.shape[1] // 4
        if name == 'ffn_in':
            assert n == FFN_BK
        order = _ffn_order(g.shape[1]) if name == 'ffn_in' else range(4)
        v = jnp.stack([g[:, b * n:(b + 1) * n] for b in order])
        per = [v[:, :g.shape[0] // 2], v[:, g.shape[0] // 2:]]
    else:
        k4, n = g.shape
        v = g.reshape(4, 2, k4 // 8, n)
        per = [v[:, 0], v[:, 1]]
    first = ac == 0
    return _bf(jnp.where(first, per[0], per[1])), _bf(jnp.where(first, per[1], per[0]))


class _GradReducer:
    def __init__(self):
        self.flight = {}

    @staticmethod
    def _plan(m, sending, refs):
        ax, ay, ac = _place()
        s = 2 * ax + ay
        out = []
        for a in range(m):
            for dx, dy in _CHIP_FLIPS:
                px, py = lax.rem(ax + dx, 2), lax.rem(ay + dy, 2)
                sp = 2 * px + py
                out.append((refs[a].at[sp], refs[m + a].at[s if sending else sp], (px, py, ac)))
        return out

    def start(self, grp, grads):
        ac = lax.axis_index("c")
        names = list(grads)
        halves = [_grad_halves(nm.rstrip('01'), grads[nm], ac) for nm in names]
        theirs = _to_sibling([h[1] for h in halves], name='swap_core_halves_' + grp)
        pair = [_add2(h[0].reshape(-1, b.shape[-1]), b.reshape(-1, b.shape[-1]), name='add_cores').reshape(b.shape)
                for h, b in zip(halves, theirs)]
        m = len(names)
        land = [lax.empty(a.shape, a.dtype) for a in pair]
        sends, recvs, bufs, token = _split_start(pair + land, functools.partial(self._plan, m, True), 3 * m,
                                                 name='scatter_' + grp + '_start')
        self.flight[grp] = (names, sends, recvs, bufs)
        return token

    def finish(self, grp, after):
        names, sends, recvs, bufs = self.flight.pop(grp)
        m = len(names)
        bufs = _split_wait(bufs, sends, recvs, functools.partial(self._plan, m, False), after,
                           name='scatter_' + grp + '_wait')
        core = lax.axis_index("c").astype(jnp.int32).reshape(1)
        sums = [_sum4(p, l, core, name='sum_chips') for p, l in zip(bufs[:m], bufs[m:])]
        both = _exchange_halves(sums, name='gather_core_halves_' + grp)
        return {nm: g.reshape(-1, g.shape[-1]) for nm, g in zip(names, both)}


def _from_shards(name, g):
    _, r, n = g.shape
    if name == 'ffn_in':
        assert n == FFN_BK
        v = g.reshape(4, 2, r // 2, n)
        return jnp.concatenate([v[b] for b in _ffn_order(4 * n)], axis=-1)
    if name == 'ffn_out':
        return g.reshape(4, 2, r // 2, n).transpose(1, 0, 2, 3).reshape(2, 2 * r, n)
    if name in ('even_in', 'odd_in'):
        return jnp.concatenate([g[b] for b in range(4)], axis=-1)
    return g.reshape(4 * r, n)


def kernel(x, c, ctx, c_ctx, mod_w, mod_b, norm_g, ffn_w_in, ffn_w_out, even_w_in, even_w_out, attn_qk_norm_g, attn_sink, hgrn_out_norm_g, hgrn_lb, odd_w_in, odd_w_out, loss_target, m_c_ctx, m_mod_w, m_mod_b, m_norm_g, m_ffn_w_in, m_ffn_w_out, m_even_w_in, m_even_w_out, m_attn_qk_norm_g, m_attn_sink, m_hgrn_out_norm_g, m_hgrn_lb, m_odd_w_in, m_odd_w_out, v_c_ctx, v_mod_w, v_mod_b, v_norm_g, v_ffn_w_in, v_ffn_w_out, v_even_w_in, v_even_w_out, v_attn_qk_norm_g, v_attn_sink, v_hgrn_out_norm_g, v_hgrn_lb, v_odd_w_in, v_odd_w_out):
    d = x.shape[-1]
    lc = ctx.shape[1]
    assert lc == TM and d == 1024
    ax, ay, ac = _place()
    s = 2 * ax + ay
    me = 4 * ax + 2 * ay + ac
    nmod = mod_w.shape[2]

    def pad8(v):
        return jnp.pad(v, ((0, 8 - v.shape[0]), (0, 0)))

    pack = jnp.concatenate([pad8(c), pad8(norm_g.reshape(1, d))], axis=0)
    g1 = _ag8(pack, name='gather_cond')
    c_all = g1[:, 0, :]
    ng = g1[0::2, 8, :].reshape(4, 2, 2, d // 4).transpose(1, 2, 0, 3).reshape(4, d)

    cond_raw = jnp.concatenate([c_all, pad8(c_ctx.reshape(1, d))], axis=0)
    mb_sh = lax.dynamic_slice_in_dim(mod_b, s * nmod, nmod, axis=1).reshape(2, 1, nmod)
    mpart = _mod_fwd(cond_raw, mod_w, mb_sh, name='mod_fwd')
    g3 = _ag8(mpart.reshape(32, nmod), name='gather_mods')
    mods_full = g3[0::2].reshape(4, 2, 16, nmod).transpose(1, 2, 0, 3).reshape(2, 16, 4 * nmod)
    m_lat = lax.dynamic_index_in_dim(mods_full, me, axis=1, keepdims=False)
    mods = jnp.stack([mods_full[:, 8], m_lat], axis=1).reshape(24, d)

    names = ['ffn_in', 'ffn_out', 'even_in', 'even_out', 'odd_in', 'odd_out']
    shards = [_bf(v.reshape(-1, v.shape[-1])) for v in (ffn_w_in, ffn_w_out, even_w_in, even_w_out, odd_w_in, odd_w_out)]
    shards, mods = lax.optimization_barrier((shards, mods))
    reducer = _GradReducer()
    wsrc = _GatheredWeights(dict(zip(names, shards)), reducer)

    lb = _lb_fwd(hgrn_lb, name='hgrn_lower_bound')
    small = dict(gq=jnp.tile(attn_qk_norm_g[0, 0], 2).reshape(1, 128), gk=jnp.tile(attn_qk_norm_g[0, 1], 2).reshape(1, 128),
                 sink=attn_sink[0], gain=hgrn_out_norm_g, lb=lb)
    x0 = jnp.concatenate([ctx[0], x[0]], axis=0) + wsrc.token[0, 0]
    loss_t, dx0, grads, sums = _local_step(x0, loss_target[0], mods, ng, wsrc, small)
    loss = lax.psum(loss_t[0, 0], ("x", "y", "c"))
    grad_x = dx0[None]

    def tile(v):
        return jnp.pad(v, ((0, 8 - v.shape[0]), (0, d - v.shape[1])))

    sums = dict(sums, sink=sums['sink'][:, 0].reshape(1, 8))
    g4 = _ag8(jnp.concatenate([tile(sums[nm]) for nm in PACK_TILES], axis=0), name='gather_row_sums')
    small_g, glb, gmb, dmat = _small_finalize(g4, tile(lb)[0:1], name='small_grads')
    dms = lax.dynamic_slice_in_dim(dmat.transpose(0, 2, 1, 3).reshape(2, 16, 6 * d), s * nmod, nmod, axis=2)
    g_mod_w, dcond = _mod_bwd(cond_raw, dms, mod_w, name='mod_bwd')
    g5 = _ag8(dcond[8:16], name='gather_dcond')
    g_c_ctx = _cctx_grad(g5, c_ctx.reshape(8, d // 8).reshape(1, d), name='c_ctx_grad')

    late = {nm: grads[nm] for nm in ('even_in', 'even_out')}
    late, g_c_ctx = lax.optimization_barrier((late, g_c_ctx))
    token = reducer.start('late', late)
    full = reducer.finish('early', token)

    def upd(wv, gs, mv, vv, name):
        shp = wv.shape
        c2 = shp[-1]
        out = _adam(wv.reshape(-1, c2), [g.reshape(-1, c2) for g in gs], mv.reshape(-1, c2), vv.reshape(-1, c2), name=name)
        return [o.reshape(shp) for o in out]

    res = {}
    res['c_ctx'] = upd(c_ctx.reshape(8, d // 8), [g_c_ctx.reshape(8, d // 8)], m_c_ctx.reshape(8, d // 8), v_c_ctx.reshape(8, d // 8), 'adam_c_ctx')
    res['c_ctx'] = [o.reshape(d) for o in res['c_ctx']]
    res['mod_w'] = upd(mod_w, [g_mod_w], m_mod_w, v_mod_w, 'adam_mod_w')
    res['mod_b'] = upd(mod_b, [gmb.reshape(2, 6 * d)], m_mod_b, v_mod_b, 'adam_mod_b')
    g_ng = lax.dynamic_slice_in_dim(small_g[0:4].reshape(2, 2, d), s * (d // 4), d // 4, axis=2)
    res['norm_g'] = upd(norm_g, [g_ng], m_norm_g, v_norm_g, 'adam_norm_g')
    g_qk = jnp.stack([small_g[4, 0:64], small_g[5, 0:64]]).reshape(1, 2, 64)
    res['attn_qk_norm_g'] = upd(attn_qk_norm_g, [g_qk], m_attn_qk_norm_g, v_attn_qk_norm_g, 'adam_qk_gain')
    res['attn_sink'] = upd(attn_sink, [small_g[7, 0:8].reshape(1, 8)], m_attn_sink, v_attn_sink, 'adam_sink')
    res['hgrn_out_norm_g'] = upd(hgrn_out_norm_g, [small_g[6, 0:128].reshape(1, 128)], m_hgrn_out_norm_g, v_hgrn_out_norm_g, 'adam_head_gain')
    res['hgrn_lb'] = upd(hgrn_lb, [glb[0:2, 0:hgrn_lb.shape[1]]], m_hgrn_lb, v_hgrn_lb, 'adam_hgrn_lb')
    res['odd_w_in'] = upd(odd_w_in, [full['odd_in']], m_odd_w_in, v_odd_w_in, 'adam_odd_in')
    res['odd_w_out'] = upd(odd_w_out, [full['odd_out']], m_odd_w_out, v_odd_w_out, 'adam_odd_out')
    full.update(reducer.finish('mid', res['odd_w_in'][1]))
    g_ffn_in = jnp.concatenate([full['ffn_in0'], full['ffn_in1']], axis=0)
    g_ffn_out = jnp.concatenate([full['ffn_out0'], full['ffn_out1']], axis=0)
    res['ffn_w_in'] = upd(ffn_w_in, [g_ffn_in], m_ffn_w_in, v_ffn_w_in, 'adam_ffn_in')
    res['ffn_w_out'] = upd(ffn_w_out, [g_ffn_out], m_ffn_w_out, v_ffn_w_out, 'adam_ffn_out')
    full.update(reducer.finish('late', res['ffn_w_in'][1]))
    res['even_w_in'] = upd(even_w_in, [full['even_in']], m_even_w_in, v_even_w_in, 'adam_even_in')
    res['even_w_out'] = upd(even_w_out, [full['even_out']], m_even_w_out, v_even_w_out, 'adam_even_out')

    order = ['c_ctx', 'mod_w', 'mod_b', 'norm_g', 'ffn_w_in', 'ffn_w_out', 'even_w_in', 'even_w_out',
             'attn_qk_norm_g', 'attn_sink', 'hgrn_out_norm_g', 'hgrn_lb', 'odd_w_in', 'odd_w_out']
    outs = [loss, grad_x]
    for k in range(4):
        outs += [res[nm][k] for nm in order]
    return tuple(outs)
```

```python
import functools
import math

import numpy as np
import jax
import jax.numpy as jnp
from jax import lax
from jax.experimental import pallas as pl
from jax.experimental.pallas import tpu as pltpu

F32 = jnp.float32
BF16 = jnp.bfloat16
EPS = 1e-6
TM = 256
CHUNK = 64
QB = 256
WINDOW = 128
NEG = -1e30
MESH = pl.DeviceIdType.MESH

ADAM_LR, ADAM_B1, ADAM_B2, ADAM_EPS, ADAM_WD, ADAM_STEP = 0.001, 0.9, 0.999, 1e-08, 0.01, 10


def _pcall(body, **kw):
    return pl.pallas_call(body, **kw)


def _pick(n, cap):
    best = None
    for m in range(128, min(n, cap) + 1, 128):
        if n % m == 0:
            best = m
    assert best is not None, (n, cap)
    return best


def _bf(x):
    return x.astype(BF16)


def _dot(a, b):
    return jnp.dot(_bf(a), _bf(b), preferred_element_type=F32)


def _dot_nt(a, b):
    return lax.dot_general(_bf(a), _bf(b), (((1,), (1,)), ((), ())), preferred_element_type=F32)


def _dot_tn(a, b):
    return lax.dot_general(_bf(a), _bf(b), (((0,), (0,)), ((), ())), preferred_element_type=F32)


def _dot_exact(a, b):
    return jnp.dot(a, b, preferred_element_type=F32, precision=lax.Precision.HIGHEST)


def _sigmoid(x):
    return 1.0 / (1.0 + jnp.exp(-x))


def _iota(shape, dim):
    return lax.broadcasted_iota(jnp.int32, shape, dim)


def _parts(a):
    parts = list(a) if isinstance(a, (list, tuple)) else [a]
    widths = [p.shape[1] for p in parts]
    return parts, widths, [sum(widths[:i]) for i in range(len(parts))]


def _mm_nn(a, b, *, lead=None, out_dtype=F32, name):
    parts, widths, offs = _parts(a)
    m, k = parts[0].shape[0], sum(widths)
    n = b.shape[-1]
    bm = 1408 if (m % 1408 == 0 and k <= 1024) else (768 if m % 768 == 0 else TM)
    bn = _pick(n, 1024) if n % 512 == 0 else _pick(n, 1664)

    def body(*refs):
        b_ref, o_ref = refs[-2], refs[-1]
        acc = None
        for p_ref, w, off in zip(refs, widths, offs):
            term = _dot(p_ref[...], b_ref[off:off + w, :])
            acc = term if acc is None else acc + term
        o_ref[...] = acc.astype(o_ref.dtype)

    if lead is None:
        b_spec = pl.BlockSpec((k, bn), lambda i, j: (0, j))
    else:
        b_spec = pl.BlockSpec((None, k, bn), lambda i, j: (lead, 0, j))
    return _pcall(
        body, name=name, grid=(m // bm, n // bn),
        in_specs=[pl.BlockSpec((bm, w), lambda i, j: (i, 0)) for w in widths] + [b_spec],
        out_specs=pl.BlockSpec((bm, bn), lambda i, j: (i, j)),
        out_shape=jax.ShapeDtypeStruct((m, n), out_dtype),
    )(*parts, b)


def _mm_nt(a, b, *, lead=None, name):
    parts, widths, offs = _parts(a)
    m, n = parts[0].shape[0], sum(widths)
    k = b.shape[-2]
    bm = 768 if m % 768 == 0 else TM
    bk = _pick(k, 512)

    def body(*refs):
        b_ref, o_ref = refs[-2], refs[-1]
        acc = None
        for p_ref, w, off in zip(refs, widths, offs):
            term = _dot_nt(p_ref[...], b_ref[:, off:off + w])
            acc = term if acc is None else acc + term
        o_ref[...] = acc

    if lead is None:
        b_spec = pl.BlockSpec((bk, n), lambda i, j: (j, 0))
    else:
        b_spec = pl.BlockSpec((None, bk, n), lambda i, j: (lead, j, 0))
    return _pcall(
        body, name=name, grid=(m // bm, k // bk),
        in_specs=[pl.BlockSpec((bm, w), lambda i, j: (i, 0)) for w in widths] + [b_spec],
        out_specs=pl.BlockSpec((bm, bk), lambda i, j: (i, j)),
        out_shape=jax.ShapeDtypeStruct((m, k), F32),
    )(*parts, b)


def _mm_tn(a, b, *, name):
    a_parts, a_w, a_off = _parts(a)
    b_parts, b_w, b_off = _parts(b)
    t, k, n = a_parts[0].shape[0], sum(a_w), sum(b_w)
    bt = 1408 if t % 1408 == 0 else (768 if t % 768 == 0 else TM)
    bk = _pick(k, 1536) if len(a_parts) == 1 else math.gcd(*a_w)
    if len(b_parts) == 1:
        bn = _pick(n, 1024) if n % 1024 == 0 or n < 1664 else _pick(n, 1664)
    else:
        bn = math.gcd(*b_w)
    na, nbp = len(a_parts), len(b_parts)

    def block_range(off, w, blk):
        return off // blk, w // blk

    def body(*refs):
        a_refs, b_refs, o_ref = refs[:na], refs[na:na + nbp], refs[-1]
        i, j = pl.program_id(0), pl.program_id(1)

        @pl.when(pl.program_id(2) == 0)
        def _():
            o_ref[...] = jnp.zeros_like(o_ref)

        def add(a_ref, b_ref):
            o_ref[...] += _dot_tn(a_ref[...], b_ref[...])

        for pa in range(na):
            sa, ca = block_range(a_off[pa], a_w[pa], bk)
            for pb in range(nbp):
                sb, cb = block_range(b_off[pb], b_w[pb], bn)
                if na == 1 and nbp == 1:
                    add(a_refs[0], b_refs[0])
                else:
                    pl.when((i >= sa) & (i < sa + ca) & (j >= sb) & (j < sb + cb))(
                        functools.partial(add, a_refs[pa], b_refs[pb]))

    def spec(off, w, blk, axis):
        s0, cnt = block_range(off, w, blk)

        def index(i, j, s):
            g = i if axis == 0 else j
            inside = (g >= s0) & (g < s0 + cnt)
            return (jnp.where(inside, s, 0), jnp.clip(g - s0, 0, cnt - 1))

        return pl.BlockSpec((bt, blk), index)

    return _pcall(
        body, name=name, grid=(k // bk, n // bn, t // bt),
        in_specs=[spec(o, w, bk, 0) for o, w in zip(a_off, a_w)] + [spec(o, w, bn, 1) for o, w in zip(b_off, b_w)],
        out_specs=pl.BlockSpec((bk, bn), lambda i, j, s: (i, j)),
        out_shape=jax.ShapeDtypeStruct((k, n), F32),
    )(*a_parts, *b_parts)


def _mod_row(mods_ref, lat, idx):
    return jnp.where(lat, mods_ref[idx + 6:idx + 7, :], mods_ref[idx:idx + 1, :])


def _row_step(t):
    return 768 if t % 768 == 0 else TM


def _row_fwd(x, mods, *, y=None, gate=None, g=None, shift=None, scale=None, name):
    t, d = x.shape
    has_y, has_n = y is not None, g is not None
    rt = _row_step(t)

    def body(*refs):
        refs = list(refs)
        x_ref, mods_ref = refs[0], refs[1]
        pos = 2
        if has_y:
            y_ref = refs[pos]; pos += 1
        if has_n:
            g_ref = refs[pos]; pos += 1
        outs = refs[pos:]
        for sub in range(rt // TM):
            rows = slice(sub * TM, (sub + 1) * TM)
            lat = pl.program_id(0) * (rt // TM) + sub > 0
            x1 = x_ref[rows, :]
            o = 0
            if has_y:
                x1 = x1 + _mod_row(mods_ref, lat, gate) * y_ref[rows, :]
                outs[o][rows, :] = x1; o += 1
            if has_n:
                rs = lax.rsqrt(jnp.mean(x1 * x1, axis=-1, keepdims=True) + EPS)
                hn = x1 * rs * g_ref[...]
                h = hn * (1.0 + _mod_row(mods_ref, lat, scale)) + _mod_row(mods_ref, lat, shift)
                outs[o][rows, :] = h.astype(BF16)

    row = pl.BlockSpec((rt, d), lambda i: (i, 0))
    ins, specs = [x, mods], [row, pl.BlockSpec(mods.shape, lambda i: (0, 0))]
    if has_y:
        ins.append(y); specs.append(row)
    if has_n:
        ins.append(g.reshape(1, d)); specs.append(pl.BlockSpec((1, d), lambda i: (0, 0)))
    out_shape, out_specs = [], []
    if has_y:
        out_shape.append(jax.ShapeDtypeStruct((t, d), F32)); out_specs.append(row)
    if has_n:
        out_shape.append(jax.ShapeDtypeStruct((t, d), BF16)); out_specs.append(row)
    res = _pcall(body, name=name, grid=(t // rt,), in_specs=specs, out_specs=out_specs,
                 out_shape=out_shape)(*ins)
    return res


def _acc_row(ref, r, val):
    ref[r:r + 1, :] += val


def _row_final(x, z, mods, target, *, gate, name):
    t, d = x.shape

    def body(x_ref, mods_ref, z_ref, t_ref, loss_ref, dx_ref, dz_ref, sums_ref):
        i = pl.program_id(0)
        lat = i > 0

        @pl.when(i == 0)
        def _():
            loss_ref[...] = jnp.zeros_like(loss_ref)
            sums_ref[...] = jnp.zeros_like(sums_ref)

        gt = _mod_row(mods_ref, lat, gate)
        zz = z_ref[...]
        yv = x_ref[...] + gt * zz
        keep = jnp.where(lat, 1.0, 0.0).astype(F32)
        diff = (yv - t_ref[...]) * keep
        part = jnp.sum(jnp.sum(diff * diff, axis=0, keepdims=True), axis=1, keepdims=True)
        loss_ref[...] += part * (0.5 / d)
        dy = diff * (1.0 / d)
        dx_ref[...] = dy
        dz_ref[...] = (gt * dy).astype(BF16)
        _acc_row(sums_ref, 6, jnp.sum(dy * zz, axis=0, keepdims=True))

    row = pl.BlockSpec((TM, d), lambda i: (i, 0))
    return _pcall(
        body, name=name, grid=(t // TM,),
        in_specs=[row, pl.BlockSpec(mods.shape, lambda i: (0, 0)), row,
                  pl.BlockSpec((TM, d), lambda i: (jnp.maximum(i - 1, 0), 0))],
        out_specs=[pl.BlockSpec((8, 128), lambda i: (0, 0)), row, row,
                   pl.BlockSpec((8, d), lambda i: (0, 0))],
        out_shape=[jax.ShapeDtypeStruct((8, 128), F32), jax.ShapeDtypeStruct((t, d), F32),
                   jax.ShapeDtypeStruct((t, d), BF16), jax.ShapeDtypeStruct((8, d), F32)],
    )(x, mods, z, target)


def _row_bwd(xn, dxo, dh, mods, g, *, shift, scale, y=None, gate=None, latent_only=False, name):
    t, d = xn.shape
    has_y = y is not None

    def body(*refs):
        refs = list(refs)
        x_ref, dxo_ref, dh_ref, mods_ref, g_ref = refs[:5]
        pos = 5
        if has_y:
            y_ref = refs[pos]; pos += 1
        dx_ref = refs[pos]; pos += 1
        if has_y:
            dy_ref = refs[pos]; pos += 1
        sums_ref = refs[pos]
        i = pl.program_id(0)

        @pl.when(i == 0)
        def _():
            sums_ref[...] = jnp.zeros_like(sums_ref)

        def add_sums(vals, base):
            for r, v in enumerate(vals):
                if v is not None:
                    _acc_row(sums_ref, base + r, v)

        gv = g_ref[...]
        for sub in range(rt // TM):
            rows = slice(sub * TM, (sub + 1) * TM)
            lat = i * (rt // TM) + sub > 0
            x1 = x_ref[rows, :]
            rs = lax.rsqrt(jnp.mean(x1 * x1, axis=-1, keepdims=True) + EPS)
            xh = x1 * rs
            dhv = dh_ref[rows, :]
            dn = dhv * (1.0 + _mod_row(mods_ref, lat, scale))
            dxh = dn * gv
            dx = dxo_ref[rows, :] + rs * (dxh - xh * jnp.mean(dxh * xh, axis=-1, keepdims=True))
            dx_ref[rows, :] = dx
            vals = [jnp.sum(dhv, axis=0, keepdims=True),
                    jnp.sum(dhv * (xh * gv), axis=0, keepdims=True),
                    None,
                    jnp.sum(dn * xh, axis=0, keepdims=True)]
            if has_y:
                dy_ref[rows, :] = (_mod_row(mods_ref, lat, gate) * dx).astype(BF16)
                vals[2] = jnp.sum(dx * y_ref[rows, :], axis=0, keepdims=True)
            if sub == 0:
                pl.when(i == 0)(functools.partial(add_sums, vals, 0))
                pl.when(i > 0)(functools.partial(add_sums, vals, 4))
            else:
                add_sums(vals, 4)

    rt = TM if latent_only else _row_step(t)
    row = pl.BlockSpec((rt, d), lambda i: (i, 0))
    ins = [xn, dxo, dh, mods, g.reshape(1, d)]
    specs = [row, row, row, pl.BlockSpec(mods.shape, lambda i: (0, 0)), pl.BlockSpec((1, d), lambda i: (0, 0))]
    if latent_only:
        out_shape = [jax.ShapeDtypeStruct((t - TM, d), F32)]
        out_specs = [pl.BlockSpec((TM, d), lambda i: (jnp.maximum(i - 1, 0), 0))]
    else:
        out_shape, out_specs = [jax.ShapeDtypeStruct((t, d), F32)], [row]
    if has_y:
        ins.append(y); specs.append(row)
        out_shape.append(jax.ShapeDtypeStruct((t, d), BF16)); out_specs.append(row)
    out_shape.append(jax.ShapeDtypeStruct((8, d), F32))
    out_specs.append(pl.BlockSpec((8, d), lambda i: (0, 0)))
    return _pcall(body, name=name, grid=(t // rt,), in_specs=specs, out_specs=out_specs,
                  out_shape=out_shape)(*ins)


FFN_BK = 1408


FFN_SUB = 256


def _ffn_order(n2):
    nb = n2 // (2 * FFN_BK)
    return [h * nb + j for j in range(nb) for h in (0, 1)]


def _ffn_interleave(w):
    return jnp.concatenate([w[..., b * FFN_BK:(b + 1) * FFN_BK] for b in _ffn_order(w.shape[-1])], axis=-1)


def _ffn_deinterleave(w):
    order = _ffn_order(w.shape[-1])
    return jnp.concatenate([w[..., order.index(b) * FFN_BK:(order.index(b) + 1) * FFN_BK]
                            for b in range(len(order))], axis=-1)


def _big_tile(t):
    return 768 if t % 768 == 0 else TM


def _ffn_in(h, w, *, lead, name):
    t, d = h.shape
    n2 = w.shape[-1]
    bm, bk = _big_tile(t), FFN_BK

    def body(h_ref, w_ref, u_ref, a_ref):
        hb = h_ref[...]
        for c0 in range(0, bk, FFN_SUB):
            c1 = min(c0 + FFN_SUB, bk)
            ug = _dot(hb, w_ref[:, c0:c1]).astype(BF16)
            uu = _dot(hb, w_ref[:, bk + c0:bk + c1]).astype(BF16)
            u_ref[:, c0:c1] = ug
            u_ref[:, bk + c0:bk + c1] = uu
            gv, up = ug.astype(F32), uu.astype(F32)
            a_ref[:, c0:c1] = (gv * _sigmoid(gv) * up).astype(BF16)

    return _pcall(
        body, name=name, grid=(t // bm, n2 // (2 * bk)),
        in_specs=[pl.BlockSpec((bm, d), lambda i, j: (i, 0)),
                  pl.BlockSpec((None, d, 2 * bk), lambda i, j: (lead, 0, j))],
        out_specs=[pl.BlockSpec((bm, 2 * bk), lambda i, j: (i, j)), pl.BlockSpec((bm, bk), lambda i, j: (i, j))],
        out_shape=[jax.ShapeDtypeStruct((t, n2), BF16), jax.ShapeDtypeStruct((t, n2 // 2), BF16)],
    )(h, w)


def _ffn_dx(dz, w_out, u, *, lead, name):
    t, d = dz.shape
    n2 = u.shape[1]
    bm, bk = _big_tile(t), FFN_BK

    def body(dz_ref, w_ref, u_ref, du_ref):
        dzb = dz_ref[...]
        for c0 in range(0, bk, FFN_SUB):
            c1 = min(c0 + FFN_SUB, bk)
            da = _dot_nt(dzb, w_ref[c0:c1, :])
            gv, up = u_ref[:, c0:c1].astype(F32), u_ref[:, bk + c0:bk + c1].astype(F32)
            s = _sigmoid(gv)
            du_ref[:, c0:c1] = (da * up * (s * (1.0 + gv * (1.0 - s)))).astype(BF16)
            du_ref[:, bk + c0:bk + c1] = (da * gv * s).astype(BF16)

    ublk = pl.BlockSpec((bm, 2 * bk), lambda i, j: (i, j))
    return _pcall(
        body, name=name, grid=(t // bm, n2 // (2 * bk)),
        in_specs=[pl.BlockSpec((bm, d), lambda i, j: (i, 0)),
                  pl.BlockSpec((None, bk, d), lambda i, j: (lead, j, 0)), ublk],
        out_specs=ublk, out_shape=jax.ShapeDtypeStruct((t, n2), BF16),
    )(dz, w_out, u)


def _lane(shape):
    return _iota(shape, len(shape) - 1)


def _pair_norm(x, g):
    lo = _lane(x.shape) < 64
    x2 = x * x
    s_lo = jnp.sum(jnp.where(lo, x2, 0.0), axis=-1, keepdims=True)
    s_hi = jnp.sum(jnp.where(lo, 0.0, x2), axis=-1, keepdims=True)
    rs = lax.rsqrt(jnp.where(lo, s_lo, s_hi) * (1.0 / 64) + EPS)
    return x * rs, rs


def _pair_mean(v):
    lo = _lane(v.shape) < 64
    s_lo = jnp.sum(jnp.where(lo, v, 0.0), axis=-1, keepdims=True)
    s_hi = jnp.sum(jnp.where(lo, 0.0, v), axis=-1, keepdims=True)
    return jnp.where(lo, s_lo, s_hi) * (1.0 / 64)


def _rot64(x):
    r1 = pltpu.roll(x, 32, 1)
    r2 = pltpu.roll(x, 96, 1)
    even = ((_lane(x.shape) >> 5) & 1) == 0
    return jnp.where(even, -r2, r1)


def _rope64(x, cos, sin):
    return x * cos + _rot64(x) * sin


def _rope64_t(d, cos, sin):
    return d * cos - _rot64(d * sin)


def _kprep_fwd(p, gk, cos, sin, *, name):
    t = p.shape[0]

    def body(k_ref, g_ref, c_ref, s_ref, o_ref):
        xh, _ = _pair_norm(k_ref[...], None)
        o_ref[...] = _rope64(xh * g_ref[...], c_ref[...], s_ref[...])

    blk = pl.BlockSpec((TM, 128), lambda i: (i, 0))
    return _pcall(
        body, name=name, grid=(t // TM,),
        in_specs=[pl.BlockSpec((TM, 128), lambda i: (i, 4)), pl.BlockSpec((1, 128), lambda i: (0, 0)), blk, blk],
        out_specs=blk, out_shape=jax.ShapeDtypeStruct((t, 128), F32),
    )(p, gk, cos, sin)


def _kprep_bwd(p, gk, cos, sin, dkp, dv, *, name):
    t = p.shape[0]

    def body(k_ref, g_ref, c_ref, s_ref, dkp_ref, dv_ref, o_ref, dg_ref):
        @pl.when(pl.program_id(0) == 0)
        def _():
            dg_ref[...] = jnp.zeros_like(dg_ref)
        xh, rs = _pair_norm(k_ref[...], None)
        dn = _rope64_t(dkp_ref[...], c_ref[...], s_ref[...])
        _acc_row(dg_ref, 0, jnp.sum(dn * xh, axis=0, keepdims=True))
        dxh = dn * g_ref[...]
        o_ref[:, 0:128] = (rs * (dxh - xh * _pair_mean(dxh * xh))).astype(BF16)
        o_ref[:, 128:256] = dv_ref[...].astype(BF16)

    blk = pl.BlockSpec((TM, 128), lambda i: (i, 0))
    return _pcall(
        body, name=name, grid=(t // TM,),
        in_specs=[pl.BlockSpec((TM, 128), lambda i: (i, 4)), pl.BlockSpec((1, 128), lambda i: (0, 0)), blk, blk, blk, blk],
        out_specs=[pl.BlockSpec((TM, 256), lambda i: (i, 0)), pl.BlockSpec((8, 128), lambda i: (0, 0))],
        out_shape=[jax.ShapeDtypeStruct((t, 256), BF16), jax.ShapeDtypeStruct((8, 128), F32)],
    )(p, gk, cos, sin, dkp, dv)


def _attn_common(i, t, lc, kp_ref, v_ref):
    span = QB + 2 * WINDOW
    start = pl.multiple_of(jnp.clip(i * QB - WINDOW, lc, t - span), WINDOW)
    kall = jnp.concatenate([kp_ref[0:lc, :], kp_ref[pl.ds(start, span), :]], axis=0)
    vall = jnp.concatenate([v_ref[0:lc, :], v_ref[pl.ds(start, span), :]], axis=0)
    nk = lc + span
    col = _iota((QB, nk), 1)
    krow = jnp.where(col < lc, col, start + col - lc)
    qrow = i * QB + _iota((QB, nk), 0)
    valid = (col < lc) | ((qrow >= lc) & (krow >= lc) & (jnp.abs(krow - qrow) <= WINDOW))
    lo = _lane(kall.shape) < 64
    kroll, vroll = pltpu.roll(kall, 64, 1), pltpu.roll(vall, 64, 1)
    zero = jnp.zeros_like(kall)
    kvar = [[_bf(jnp.where(lo, kall, zero)), _bf(jnp.where(lo, zero, kroll))],
            [_bf(jnp.where(lo, kroll, zero)), _bf(jnp.where(lo, zero, kall))]]
    vvar = [[_bf(jnp.where(lo, vall, zero)), _bf(jnp.where(lo, zero, vroll))],
            [_bf(jnp.where(lo, vroll, zero)), _bf(jnp.where(lo, zero, vall))]]
    return start, valid, kvar, vvar


def _softmax_sink(s, valid, snk):
    s = jnp.where(valid, s, NEG)
    m = jnp.maximum(jnp.max(s, axis=-1, keepdims=True), snk)
    e = jnp.exp(s - m)
    es = jnp.exp(snk - m)
    inv = 1.0 / (jnp.sum(e, axis=-1, keepdims=True) + es)
    return e * inv, es * inv


def _attn_fwd(p, kp, gq, sink, cos, sin, *, lc, name):
    t = p.shape[0]
    scale = 64 ** -0.5

    def body(q_ref, kp_ref, v_ref, g_ref, sink_ref, c_ref, s_ref, o_ref):
        i = pl.program_id(0)
        _, valid, kvar, vvar = _attn_common(i, t, lc, kp_ref, v_ref)
        cosv, sinv, gv = c_ref[...], s_ref[...], g_ref[...]
        for j in range(4):
            xh, _ = _pair_norm(q_ref[:, 128 * j:128 * j + 128], None)
            q2 = _bf(_rope64(xh * gv, cosv, sinv) * scale)
            acc = jnp.zeros((QB, 128), F32)
            for half in range(2):
                s = _dot_nt(q2, kvar[j // 2][half])
                pr, _ = _softmax_sink(s, valid, sink_ref[2 * j + half])
                acc = acc + _dot(pr, vvar[j // 2][half])
            o_ref[:, 128 * j:128 * j + 128] = acc.astype(BF16)

    qblk = pl.BlockSpec((QB, 128), lambda i: (i, 0))
    return _pcall(
        body, name=name, grid=(t // QB,),
        in_specs=[pl.BlockSpec((QB, 512), lambda i: (i, 0)),
                  pl.BlockSpec((t, 128), lambda i: (0, 0)),
                  pl.BlockSpec((t, 128), lambda i: (0, 5)),
                  pl.BlockSpec((1, 128), lambda i: (0, 0)),
                  pl.BlockSpec(memory_space=pltpu.SMEM), qblk, qblk],
        out_specs=pl.BlockSpec((QB, 512), lambda i: (i, 0)),
        out_shape=jax.ShapeDtypeStruct((t, 512), BF16),
    )(p, kp, p, gq, sink, cos, sin)


def _attn_bwd(p, kp, gq, sink, cos, sin, dmix, *, lc, name):
    t = p.shape[0]
    scale = 64 ** -0.5
    span = QB + 2 * WINDOW

    def body(q_ref, kp_ref, v_ref, g_ref, sink_ref, c_ref, s_ref, do_ref,
             dq_ref, dk_ref, dv_ref, dg_ref, dsink_ref):
        i = pl.program_id(0)

        @pl.when(i == 0)
        def _():
            dk_ref[...] = jnp.zeros_like(dk_ref)
            dv_ref[...] = jnp.zeros_like(dv_ref)
            dg_ref[...] = jnp.zeros_like(dg_ref)
            dsink_ref[...] = jnp.zeros_like(dsink_ref)

        start, valid, kvar, vvar = _attn_common(i, t, lc, kp_ref, v_ref)
        cosv, sinv, gv = c_ref[...], s_ref[...], g_ref[...]
        nk = lc + span
        dkt = [jnp.zeros((64, nk), F32), jnp.zeros((64, nk), F32)]
        dvt = [jnp.zeros((64, nk), F32), jnp.zeros((64, nk), F32)]
        for j in range(4):
            kvh = j // 2
            xh, rs = _pair_norm(q_ref[:, 128 * j:128 * j + 128], None)
            q2 = _bf(_rope64(xh * gv, cosv, sinv) * scale)
            do2 = _bf(do_ref[:, 128 * j:128 * j + 128])
            dq2 = jnp.zeros((QB, 128), F32)
            for half in range(2):
                s = _dot_nt(q2, kvar[kvh][half])
                pr, ps = _softmax_sink(s, valid, sink_ref[2 * j + half])
                dp = _dot_nt(do2, vvar[kvh][half])
                delta = jnp.sum(pr * dp, axis=-1, keepdims=True)
                ds = pr * (dp - delta)
                dsk = jnp.sum(jnp.sum(-ps * delta, axis=0, keepdims=True), axis=1, keepdims=True)
                _acc_row(dsink_ref, 2 * j + half, jnp.broadcast_to(dsk, (1, 128)))
                dq2 = dq2 + _dot(ds, kvar[kvh][half])
                hrows = slice(64 * half, 64 * half + 64)
                dkt[kvh] = dkt[kvh] + _dot_tn(q2, ds)[hrows]
                dvt[kvh] = dvt[kvh] + _dot_tn(do2, pr)[hrows]
            dn = _rope64_t(dq2 * scale, cosv, sinv)
            _acc_row(dg_ref, 0, jnp.sum(dn * xh, axis=0, keepdims=True))
            dxh = dn * gv
            dq_ref[:, 128 * j:128 * j + 128] = (rs * (dxh - xh * _pair_mean(dxh * xh))).astype(BF16)
        dk_all = jnp.concatenate(dkt, axis=0).T
        dv_all = jnp.concatenate(dvt, axis=0).T
        dk_ref[0:lc, :] += dk_all[0:lc]
        dv_ref[0:lc, :] += dv_all[0:lc]
        dk_ref[pl.ds(start, span), :] += dk_all[lc:nk]
        dv_ref[pl.ds(start, span), :] += dv_all[lc:nk]

    qblk = pl.BlockSpec((QB, 128), lambda i: (i, 0))
    full = pl.BlockSpec((t, 128), lambda i: (0, 0))
    small = pl.BlockSpec((8, 128), lambda i: (0, 0))
    return _pcall(
        body, name=name, grid=(t // QB,),
        in_specs=[pl.BlockSpec((QB, 512), lambda i: (i, 0)), full,
                  pl.BlockSpec((t, 128), lambda i: (0, 5)),
                  pl.BlockSpec((1, 128), lambda i: (0, 0)),
                  pl.BlockSpec(memory_space=pltpu.SMEM), qblk, qblk,
                  pl.BlockSpec((QB, 512), lambda i: (i, 0))],
        out_specs=[pl.BlockSpec((QB, 512), lambda i: (i, 0)), full, full, small, small],
        out_shape=[jax.ShapeDtypeStruct((t, 512), BF16), jax.ShapeDtypeStruct((t, 128), F32),
                   jax.ShapeDtypeStruct((t, 128), F32), jax.ShapeDtypeStruct((8, 128), F32),
                   jax.ShapeDtypeStruct((8, 128), F32)],
    )(p, kp, p, gq, sink, cos, sin, dmix)


def _tri(rev):
    r, c = _iota((CHUNK, CHUNK), 0), _iota((CHUNK, CHUNK), 1)
    return (c >= r) if rev else (c <= r)


def _blk_map(nb, rev, backward):
    if not rev:
        return (lambda n: nb - 1 - n) if backward else (lambda n: n)
    if backward:
        return lambda n: jnp.where(n < nb - 1, n + 1, 0)
    return lambda n: jnp.where(n == 0, 0, nb - n)


def _chunk_order(rev, backward, nc=TM // CHUNK):
    order = list(range(nc))
    return order[::-1] if (rev != backward) else order


def _hgrn_gates(qraw, fraw, lb):
    sq = _sigmoid(qraw)
    sf = _sigmoid(fraw)
    f = lb + (1.0 - lb) * sf
    return qraw * sq, 1.0 - f, jnp.log(f), sq, sf, f


HGRN_HP = 4


def _chunk_cumsum(x, rev):
    n = x.shape[0]
    pos = _iota(x.shape, 0) & (CHUNK - 1)
    s = 1
    while s < CHUNK:
        if rev:
            x = x + jnp.where(pos < CHUNK - s, pltpu.roll(x, n - s, 0), 0.0)
        else:
            x = x + jnp.where(pos >= s, pltpu.roll(x, s, 0), 0.0)
        s *= 2
    return x


def _block_terms(lf, rev):
    b = _chunk_cumsum(lf, rev)
    mid, last = (CHUNK // 2 - 1, 0) if rev else (CHUNK // 2, CHUNK - 1)

    def chunk_row(off):
        return jnp.concatenate([jnp.broadcast_to(b[c * CHUNK + off:c * CHUNK + off + 1, :], (CHUNK, b.shape[1]))
                                for c in range(TM // CHUNK)], axis=0)

    r, bl = chunk_row(mid), chunk_row(last)
    return _tri(rev), jnp.exp(b - r), jnp.exp(r - b), jnp.exp(b), jnp.exp(bl - b), jnp.exp(bl)


def _headnorm_apply(o, gv, gain):
    n = o * lax.rsqrt(jnp.mean(o * o, axis=-1, keepdims=True) + EPS)
    if gain is not None:
        n = n * gain
    return (n * (gv * _sigmoid(gv))).astype(BF16)


def _headnorm_grad(o, gv, dy, gain):
    rs = lax.rsqrt(jnp.mean(o * o, axis=-1, keepdims=True) + EPS)
    xh = o * rs
    n = xh * gain if gain is not None else xh
    sg = _sigmoid(gv)
    dn = dy * (gv * sg)
    dg = (dy * n * (sg * (1.0 + gv * (1.0 - sg)))).astype(BF16)
    dgain = jnp.sum(dn * xh, axis=0, keepdims=True)
    dxh = dn * gain if gain is not None else dn
    return rs * (dxh - xh * jnp.mean(dxh * xh, axis=-1, keepdims=True)), dg, dgain


def _hgrn_cols(bmap, n2, c0):
    return [pl.BlockSpec((TM, 256), lambda h, n, b=b: (bmap(n), c0 // 2 + h * n2 + b)) for b in range(n2)]


def _head_cols(refs, hh):
    return refs[hh // 2][:, 128 * (hh % 2):128 * (hh % 2) + 128]


def _hgrn_fwd(p, lb, *, rev, name, ofw=None, gain=None):
    t = p.shape[0]
    nb, nc = t // TM, TM // CHUNK
    bmap = _blk_map(nb, rev, False)
    fcol = 14 if rev else 10
    fused = ofw is not None

    n2 = HGRN_HP // 2

    def body(*refs):
        q_refs, f_refs, v_refs, lb_ref = refs[:n2], refs[n2:2 * n2], refs[2 * n2:3 * n2], refs[3 * n2]
        rest = refs[3 * n2 + 1:]
        if fused:
            ofw_ref, g_refs, gain_ref = rest[0], rest[1:1 + n2], rest[1 + n2]
            o_ref, sh_ref, mix_ref, st = rest[2 + n2:]
        else:
            o_ref, sh_ref, st = rest

        @pl.when(pl.program_id(1) == 0)
        def _():
            st[...] = jnp.zeros_like(st)
        for hh in range(HGRN_HP):
            ln = slice(128 * hh, 128 * hh + 128)
            q, k, lf, _, _, _ = _hgrn_gates(_head_cols(q_refs, hh), _head_cols(f_refs, hh), lb_ref[:, ln])
            tri, eq, ek, ei, eki, eb = _block_terms(lf, rev)
            qe, ke, qi, ki, vb = _bf(q * eq), _bf(k * ek), _bf(q * ei), _bf(k * eki), _bf(_head_cols(v_refs, hh))
            intra = []
            for cc in range(nc):
                rows = slice(cc * CHUNK, (cc + 1) * CHUNK)
                a = jnp.where(tri, _dot_nt(qe[rows], ke[rows]), 0.0)
                intra.append(_dot(a, vb[rows]))
            s = st[hh]
            for cc in _chunk_order(rev, False):
                rows = slice(cc * CHUNK, (cc + 1) * CHUNK)
                sh_ref[hh, cc] = s
                o_ref[rows, ln] = intra[cc] + _dot_nt(qi[rows], s)
                s = s * eb[cc * CHUNK:cc * CHUNK + 1, :] + _dot_tn(vb[rows], ki[rows])
            st[hh] = s
            if fused:
                osum = o_ref[:, ln] + ofw_ref[:, ln]
                o_ref[:, ln] = osum
                mix_ref[:, ln] = _headnorm_apply(osum, _head_cols(g_refs, hh), gain_ref[...])

    hp, wd = HGRN_HP, 128 * HGRN_HP
    col = functools.partial(_hgrn_cols, bmap, n2)
    oblk = pl.BlockSpec((TM, wd), lambda h, n: (bmap(n), h))
    ins = [p] * (3 * n2) + [lb]
    specs = col(6) + col(fcol) + col(18) + [pl.BlockSpec((1, wd), lambda h, n: (0, h))]
    out_specs = [oblk, pl.BlockSpec((hp, nc, 128, 128), lambda h, n: (h, bmap(n), 0, 0))]
    out_shape = [jax.ShapeDtypeStruct((t, 512), F32), jax.ShapeDtypeStruct((4, t // CHUNK, 128, 128), F32)]
    if fused:
        ins += [ofw] + [p] * n2 + [gain]
        specs += [oblk] + col(22) + [pl.BlockSpec((1, 128), lambda h, n: (0, 0))]
        out_specs.append(oblk)
        out_shape.append(jax.ShapeDtypeStruct((t, 512), BF16))
    return _pcall(body, name=name, grid=(4 // hp, nb), in_specs=specs, out_specs=out_specs, out_shape=out_shape,
                  scratch_shapes=[pltpu.VMEM((hp, 128, 128), F32)])(*ins)


def _hgrn_bwd(p, lb, sh, do, prev, *, rev, name, head=None):
    t = p.shape[0]
    nb, nc = t // TM, TM // CHUNK
    bmap = _blk_map(nb, rev, True)
    fcol = 14 if rev else 10
    has_prev = prev is not None
    odt = BF16 if has_prev else F32
    fused = head is not None

    n2 = HGRN_HP // 2

    def body(*refs):
        refs = list(refs)
        q_refs, f_refs, v_refs = refs[:n2], refs[n2:2 * n2], refs[2 * n2:3 * n2]
        lb_ref, sh_ref = refs[3 * n2], refs[3 * n2 + 1]
        pos = 3 * n2 + 2
        if fused:
            osum_ref, g_refs, dmix_ref, gain_ref = refs[pos], refs[pos + 1:pos + 1 + n2], refs[pos + 1 + n2], refs[pos + 2 + n2]
            pos += 3 + n2
        else:
            do_ref = refs[pos]
            pos += 1
        if has_prev:
            pq_ref, pv_ref = refs[pos], refs[pos + 1]
            pos += 2
        dq_ref, df_ref, dv_ref, dlb_ref = refs[pos:pos + 4]
        pos += 4
        if fused:
            do_out, dg_ref, dgain_ref = refs[pos:pos + 3]
            pos += 3
        dst = refs[pos]

        @pl.when(pl.program_id(1) == 0)
        def _():
            dst[...] = jnp.zeros_like(dst)
            dlb_ref[...] = jnp.zeros_like(dlb_ref)

        if fused:
            @pl.when((pl.program_id(0) == 0) & (pl.program_id(1) == 0))
            def _():
                dgain_ref[...] = jnp.zeros_like(dgain_ref)

        cat = functools.partial(jnp.concatenate, axis=0)
        for hh in range(HGRN_HP):
            ln = slice(128 * hh, 128 * hh + 128)
            lbv = lb_ref[:, ln]
            qraw, fraw = _head_cols(q_refs, hh), _head_cols(f_refs, hh)
            q, k, lf, sq, sf, f = _hgrn_gates(qraw, fraw, lbv)
            tri, eq, ek, ei, eki, eb = _block_terms(lf, rev)
            qe, ke, qi, ki = q * eq, k * ek, q * ei, k * eki
            if fused:
                dov, dg, dgain = _headnorm_grad(osum_ref[:, ln], _head_cols(g_refs, hh), dmix_ref[:, ln], gain_ref[...])
                do_out[:, ln] = dov
                dg_ref[:, ln] = dg
                _acc_row(dgain_ref, 0, dgain)
            else:
                dov = do_ref[:, ln]
            qeb, keb, qib, kib, vb, dob = _bf(qe), _bf(ke), _bf(qi), _bf(ki), _bf(_head_cols(v_refs, hh)), _bf(dov)
            dv, dqe, dke, dqi = [None] * nc, [None] * nc, [None] * nc, [None] * nc
            for cc in range(nc):
                rows = slice(cc * CHUNK, (cc + 1) * CHUNK)
                a = jnp.where(tri, _dot_nt(qeb[rows], keb[rows]), 0.0)
                da = jnp.where(tri, _dot_nt(dob[rows], vb[rows]), 0.0)
                dv[cc] = _dot_tn(a, dob[rows])
                dqe[cc], dke[cc] = _dot(da, keb[rows]), _dot_tn(da, qeb[rows])
                dqi[cc] = _dot(dob[rows], sh_ref[hh, cc])
            dki, dbl = [None] * nc, [None] * nc
            ds = dst[hh]
            for cc in _chunk_order(rev, True):
                rows = slice(cc * CHUNK, (cc + 1) * CHUNK)
                ebc = eb[cc * CHUNK:cc * CHUNK + 1, :]
                dv[cc] = dv[cc] + _dot_nt(kib[rows], ds)
                dki[cc] = _dot(vb[rows], ds)
                dbl[cc] = jnp.broadcast_to(jnp.sum(dki[cc] * ki[rows], axis=0, keepdims=True)
                                           + jnp.sum(ds * sh_ref[hh, cc], axis=0, keepdims=True) * ebc, (CHUNK, 128))
                ds = ds * ebc + _dot_tn(dob[rows], qib[rows])
            dst[hh] = ds
            dqe, dke, dqi, dki, dv, dbl = cat(dqe), cat(dke), cat(dqi), cat(dki), cat(dv), cat(dbl)
            dq = dqe * eq + dqi * ei
            dk = dke * ek + dki * eki
            last = 0 if rev else CHUNK - 1
            db = dqe * qe - dke * ke + dqi * qi - dki * ki
            db = db + jnp.where((_iota(db.shape, 0) & (CHUNK - 1)) == last, dbl, 0.0)
            dlf = _chunk_cumsum(db, not rev)
            dqr = dq * (sq * (1.0 + qraw * (1.0 - sq)))
            dfv = dlf / f - dk
            dfr = dfv * (1.0 - lbv) * (sf * (1.0 - sf))
            dlb_ref[:, ln] += jnp.sum(dfv * (1.0 - sf), axis=0, keepdims=True)
            if has_prev:
                dqr = dqr + pq_ref[:, ln]
                dv = dv + pv_ref[:, ln]
            dq_ref[:, ln] = dqr.astype(odt)
            df_ref[:, ln] = dfr.astype(odt)
            dv_ref[:, ln] = dv.astype(odt)

    hp, wd = HGRN_HP, 128 * HGRN_HP
    col = functools.partial(_hgrn_cols, bmap, n2)
    oblk = pl.BlockSpec((TM, wd), lambda h, n: (bmap(n), h))
    ins = [p] * (3 * n2) + [lb, sh]
    specs = col(6) + col(fcol) + col(18) + [pl.BlockSpec((1, wd), lambda h, n: (0, h)),
                                            pl.BlockSpec((hp, nc, 128, 128), lambda h, n: (h, bmap(n), 0, 0))]
    if fused:
        osum, dmix, gain = head
        ins += [osum] + [p] * n2 + [dmix, gain]
        specs += [oblk] + col(22) + [pl.BlockSpec((TM, wd), lambda h, n: (bmap(n), 4 // hp + h)),
                                     pl.BlockSpec((1, 128), lambda h, n: (0, 0))]
    else:
        ins.append(do); specs.append(oblk)
    if has_prev:
        ins += list(prev); specs += [oblk, oblk]
    out_specs = [oblk, oblk, oblk, pl.BlockSpec((1, wd), lambda h, n: (0, h))]
    out_shape = [jax.ShapeDtypeStruct((t, 512), odt)] * 3 + [jax.ShapeDtypeStruct((1, 512), F32)]
    if fused:
        out_specs += [oblk, oblk, pl.BlockSpec((8, 128), lambda h, n: (0, 0))]
        out_shape += [jax.ShapeDtypeStruct((t, 512), F32), jax.ShapeDtypeStruct((t, 512), BF16),
                      jax.ShapeDtypeStruct((8, 128), F32)]
    return _pcall(body, name=name, grid=(4 // hp, nb), in_specs=specs, out_specs=out_specs, out_shape=out_shape,
                  scratch_shapes=[pltpu.VMEM((hp, 128, 128), F32)])(*ins)


def _rope256(x, cos, sin):
    x1, x2 = x[:, 0:128], x[:, 128:256]
    return jnp.concatenate([x1 * cos - x2 * sin, x2 * cos + x1 * sin], axis=-1)


def _rope256_t(d, cos, sin):
    d1, d2 = d[:, 0:128], d[:, 128:256]
    return jnp.concatenate([d1 * cos + d2 * sin, d2 * cos - d1 * sin], axis=-1)


RET_DK, RET_DV, RET_H = 256, 512, 4
RET_KSCALE = RET_DK ** -0.5
RCH = TM
RET_HP = 4


def _ret_terms(lg, rev):
    r, c = _iota((RCH, RCH), 0), _iota((RCH, RCH), 1)
    rel = ((c - r) if rev else (r - c)).astype(F32)
    dmat = jnp.where(rel >= 0, jnp.exp(lg[:, 0:1] * jnp.maximum(rel, 0.0)), 0.0)
    pos = _iota((RCH, 1), 0).astype(F32)
    cnt = (RCH - pos) if rev else (pos + 1.0)
    ei = jnp.exp(lg * cnt)
    eki = jnp.exp(lg * (RCH - cnt))
    eb = jnp.exp(lg * float(RCH))
    return dmat, ei, eki, eb


def _ret_fwd(p, lgt, cos, sin, *, rev, name, ofw=None):
    t = p.shape[0]
    nb, nc = t // TM, TM // RCH
    bmap = _blk_map(nb, rev, False)
    fused = ofw is not None

    def body(*refs):
        q_ref, k_ref, v_ref, lg_ref, c_ref, s_ref = refs[:6]
        if fused:
            ofw_ref, g_ref, o_ref, sh_ref, mix_ref, st = refs[6:]
        else:
            o_ref, sh_ref, st = refs[6:]

        @pl.when(pl.program_id(1) == 0)
        def _():
            st[...] = jnp.zeros_like(st)
        for hh in range(RET_HP):
            qc, vc = slice(RET_DK * hh, RET_DK * (hh + 1)), slice(RET_DV * hh, RET_DV * (hh + 1))
            dmat, ei, eki, eb = _ret_terms(lg_ref[hh], rev)
            for cc in _chunk_order(rev, False, nc):
                rows = slice(cc * RCH, (cc + 1) * RCH)
                cosv, sinv = c_ref[rows, :], s_ref[rows, :]
                q = _rope256(q_ref[rows, qc].astype(F32), cosv, sinv)
                k = _rope256(k_ref[rows, qc].astype(F32), cosv, sinv) * RET_KSCALE
                v = v_ref[rows, vc]
                s0 = st[hh]
                sh_ref[hh, cc] = s0.astype(BF16)
                a = _dot_nt(q, k) * dmat
                o = _dot(a, v) + _dot_nt(q * ei, s0)
                st[hh] = s0 * eb + _dot_tn(v, k * eki)
                if fused:
                    o = o + ofw_ref[rows, vc]
                    mix_ref[rows, vc] = _headnorm_apply(o, g_ref[rows, vc].astype(F32), None)
                o_ref[rows, vc] = o

    hp = RET_HP
    tab = pl.BlockSpec((TM, 128), lambda h, n: (bmap(n), 0))
    oblk = pl.BlockSpec((TM, hp * RET_DV), lambda h, n: (bmap(n), h))
    ins = [p, p, p, lgt, cos, sin]
    specs = [pl.BlockSpec((TM, hp * RET_DK), lambda h, n: (bmap(n), h)),
             pl.BlockSpec((TM, hp * RET_DK), lambda h, n: (bmap(n), RET_H // hp + h)),
             pl.BlockSpec((TM, hp * RET_DV), lambda h, n: (bmap(n), RET_H // hp + h)),
             pl.BlockSpec((hp, 1, RET_DK), lambda h, n: (h, 0, 0)), tab, tab]
    out_specs = [oblk, pl.BlockSpec((hp, nc, RET_DV, RET_DK), lambda h, n: (h, bmap(n), 0, 0))]
    out_shape = [jax.ShapeDtypeStruct((t, RET_H * RET_DV), F32),
                 jax.ShapeDtypeStruct((RET_H, t // RCH, RET_DV, RET_DK), BF16)]
    if fused:
        ins += [ofw, p]
        specs += [oblk, pl.BlockSpec((TM, hp * RET_DV), lambda h, n: (bmap(n), 2 * RET_H // hp + h))]
        out_specs.append(oblk)
        out_shape.append(jax.ShapeDtypeStruct((t, RET_H * RET_DV), BF16))
    return _pcall(body, name=name, grid=(RET_H // hp, nb), in_specs=specs, out_specs=out_specs, out_shape=out_shape,
                  scratch_shapes=[pltpu.VMEM((hp, RET_DV, RET_DK), F32)])(*ins)


def _ret_bwd(p, lgt, cos, sin, sh, do, prev, *, rev, name, head=None):
    t = p.shape[0]
    nb, nc = t // TM, TM // RCH
    bmap = _blk_map(nb, rev, True)
    has_prev = prev is not None
    odt = BF16 if has_prev else F32
    fused = head is not None

    def body(*refs):
        refs = list(refs)
        q_ref, k_ref, v_ref, lg_ref, c_ref, s_ref, sh_ref = refs[:7]
        if fused:
            osum_ref, g_ref, dmix_ref = refs[7:10]
            pos = 10
        else:
            do_ref = refs[7]
            pos = 8
        if has_prev:
            pq_ref, pk_ref, pv_ref = refs[pos:pos + 3]
            pos += 3
        dq_ref, dk_ref, dv_ref = refs[pos:pos + 3]
        pos += 3
        if fused:
            do_out, dg_ref = refs[pos:pos + 2]
            pos += 2
        dst = refs[pos]

        @pl.when(pl.program_id(1) == 0)
        def _():
            dst[...] = jnp.zeros_like(dst)

        for hh in range(RET_HP):
            qc, vc = slice(RET_DK * hh, RET_DK * (hh + 1)), slice(RET_DV * hh, RET_DV * (hh + 1))
            dmat, ei, eki, eb = _ret_terms(lg_ref[hh], rev)
            for cc in _chunk_order(rev, True, nc):
                rows = slice(cc * RCH, (cc + 1) * RCH)
                cosv, sinv = c_ref[rows, :], s_ref[rows, :]
                q = _rope256(q_ref[rows, qc].astype(F32), cosv, sinv)
                k = _rope256(k_ref[rows, qc].astype(F32), cosv, sinv) * RET_KSCALE
                v = v_ref[rows, vc]
                if fused:
                    dov, dg, _ = _headnorm_grad(osum_ref[rows, vc], g_ref[rows, vc].astype(F32), dmix_ref[rows, vc], None)
                    do_out[rows, vc] = dov
                    dg_ref[rows, vc] = dg
                else:
                    dov = do_ref[rows, vc]
                s0 = sh_ref[hh, cc]
                dsc = dst[hh]
                qi, ki = q * ei, k * eki
                a = _dot_nt(q, k) * dmat
                da = _dot_nt(dov, v) * dmat
                dv = _dot_tn(a, dov) + _dot_nt(ki, dsc)
                dqs = _dot(da, k) + _dot(dov, s0) * ei
                dks = _dot_tn(da, q) + _dot(v, dsc) * eki
                dst[hh] = dsc * eb + _dot_tn(dov, qi)
                dq = _rope256_t(dqs, cosv, sinv)
                dk = _rope256_t(dks * RET_KSCALE, cosv, sinv)
                if has_prev:
                    dq = dq + pq_ref[rows, qc]
                    dk = dk + pk_ref[rows, qc]
                    dv = dv + pv_ref[rows, vc]
                dq_ref[rows, qc] = dq.astype(odt)
                dk_ref[rows, qc] = dk.astype(odt)
                dv_ref[rows, vc] = dv.astype(odt)

    hp = RET_HP
    tab = pl.BlockSpec((TM, 128), lambda h, n: (bmap(n), 0))
    qblk = pl.BlockSpec((TM, hp * RET_DK), lambda h, n: (bmap(n), h))
    vblk = pl.BlockSpec((TM, hp * RET_DV), lambda h, n: (bmap(n), h))
    ins = [p, p, p, lgt, cos, sin, sh]
    specs = [qblk, pl.BlockSpec((TM, hp * RET_DK), lambda h, n: (bmap(n), RET_H // hp + h)),
             pl.BlockSpec((TM, hp * RET_DV), lambda h, n: (bmap(n), RET_H // hp + h)),
             pl.BlockSpec((hp, 1, RET_DK), lambda h, n: (h, 0, 0)), tab, tab,
             pl.BlockSpec((hp, nc, RET_DV, RET_DK), lambda h, n: (h, bmap(n), 0, 0))]
    if fused:
        osum, dmix = head
        ins += [osum, p, dmix]
        specs += [vblk, pl.BlockSpec((TM, hp * RET_DV), lambda h, n: (bmap(n), 2 * RET_H // hp + h)), vblk]
    else:
        ins.append(do); specs.append(vblk)
    if has_prev:
        ins += list(prev); specs += [qblk, qblk, vblk]
    out_specs = [qblk, qblk, vblk]
    out_shape = [jax.ShapeDtypeStruct((t, RET_H * RET_DK), odt), jax.ShapeDtypeStruct((t, RET_H * RET_DK), odt),
                 jax.ShapeDtypeStruct((t, RET_H * RET_DV), odt)]
    if fused:
        out_specs += [vblk, vblk]
        out_shape += [jax.ShapeDtypeStruct((t, RET_H * RET_DV), F32), jax.ShapeDtypeStruct((t, RET_H * RET_DV), BF16)]
    return _pcall(body, name=name, grid=(RET_H // hp, nb), in_specs=specs, out_specs=out_specs, out_shape=out_shape,
                  scratch_shapes=[pltpu.VMEM((hp, RET_DV, RET_DK), F32)])(*ins)


def _rope_tables(lc, l):
    tt = jnp.arange(l)
    row, colp = (tt // 64).astype(F32), (tt % 64).astype(F32)
    inv = 10000.0 ** (-jnp.arange(16, dtype=F32) / 16)
    ang = jnp.concatenate([row[:, None] * inv, colp[:, None] * inv], axis=-1)
    ang = jnp.concatenate([jnp.zeros((lc, 32), F32), ang], axis=0)
    acos, asin = jnp.tile(jnp.cos(ang), (1, 4)), jnp.tile(jnp.sin(ang), (1, 4))
    theta = 1.0 / (10000.0 ** jnp.linspace(0.0, 1.0, 128, dtype=F32))
    rang = jnp.arange(l, dtype=F32)[:, None] * theta
    rang = jnp.concatenate([jnp.zeros((lc, 128), F32), rang], axis=0)
    return acos, asin, jnp.cos(rang), jnp.sin(rang)


class _Weights:
    def __init__(self, w):
        self.w = w

    def first(self, after):
        return self.w

    def rest_landed(self, after):
        pass

    def rest(self, after):
        return self.w

    def send_grads(self, grp, grads):
        return jnp.zeros((8, 128), F32)


def _local_step(x0, target, mods, ng, wsrc, small):
    t, d = x0.shape
    l = target.shape[0]
    lc = t - l
    acos, asin, rcos, rsin = _rope_tables(lc, l)
    lg_fw = jnp.log(1.0 - 2.0 ** (-5.0 - jnp.arange(RET_H, dtype=F32)))
    lgt_fw = jnp.broadcast_to(lg_fw[:, None, None], (RET_H, 1, RET_DK))
    lgt_bw = jnp.broadcast_to(lg_fw[::-1][:, None, None], (RET_H, 1, RET_DK))
    gq, gk, sink, gain, lb = small['gq'], small['gk'], small['sink'], small['gain'], small['lb']

    (h1,) = _row_fwd(x0, mods, g=ng[0], shift=0, scale=1, name='l0_norm1')
    w = wsrc.first(h1)
    p0 = _mm_nn(h1, w['even_in'], name='l0_in')
    kp = _kprep_fwd(p0, gk, acos, asin, name='l0_kprep')
    att = _attn_fwd(p0, kp, gq, sink, acos, asin, lc=lc, name='l0_attn')
    hof, hsf = _hgrn_fwd(p0, lb, rev=False, name='l0_hgrn_f')
    wsrc.rest_landed(hof)
    hos, hsb, bmix = _hgrn_fwd(p0, lb, rev=True, name='l0_hgrn_b', ofw=hof, gain=gain)
    mix0 = [att, bmix]
    y0 = _mm_nn(mix0, w['even_out'], name='l0_out')
    x1, h2 = _row_fwd(x0, mods, y=y0, gate=2, g=ng[1], shift=3, scale=4, name='l0_norm2')
    w = dict(w, **wsrc.rest(h2))
    u0, a0 = _ffn_in(h2, w['ffn_in'], lead=0, name='ffn_in')
    z0 = _mm_nn(a0, w['ffn_out'], lead=0, name='ffn_out')
    x2, h3 = _row_fwd(x1, mods, y=z0, gate=5, g=ng[2], shift=12, scale=13, name='l1_norm1')
    p1 = _mm_nn(h3, w['odd_in'], out_dtype=BF16, name='l1_in')
    rof, rsf = _ret_fwd(p1, lgt_fw, rcos, rsin, rev=False, name='l1_ret_f')
    ros, rsb, mix1 = _ret_fwd(p1, lgt_bw, rcos, rsin, rev=True, name='l1_ret_b', ofw=rof)
    y1 = _mm_nn(mix1, w['odd_out'], name='l1_out')
    x3, h4 = _row_fwd(x2, mods, y=y1, gate=14, g=ng[3], shift=15, scale=16, name='l1_norm2')
    u1, a1 = _ffn_in(h4, w['ffn_in'], lead=1, name='ffn_in')
    z1 = _mm_nn(a1, w['ffn_out'], lead=1, name='ffn_out')
    loss, dx4, dz1, s_fin = _row_final(x3, z1, mods, target, gate=17, name='loss')

    du1 = _ffn_dx(dz1, w['ffn_out'], u1, lead=1, name='ffn_out_dx')
    g_ffn_out1 = _mm_tn(a1, dz1, name='ffn_out_dw')
    dh4 = _mm_nt(du1, w['ffn_in'], lead=1, name='ffn_in_dx')
    g_ffn_in1 = _mm_tn(h4, du1, name='ffn_in_dw')
    dx3, dy1, s_l1n2 = _row_bwd(x3, dx4, dh4, mods, ng[3], shift=15, scale=16, y=y1, gate=14, name='l1_norm2_bwd')
    dmix1 = _mm_nt(dy1, w['odd_out'], name='l1_out_dx')
    g_odd_out = _mm_tn(mix1, dy1, name='l1_out_dw')
    rdq, rdk, rdv, rdo, rdg = _ret_bwd(p1, lgt_fw, rcos, rsin, rsf, None, None, rev=False, name='l1_ret_f_bwd',
                                       head=(ros, dmix1))
    rdq, rdk, rdv = _ret_bwd(p1, lgt_bw, rcos, rsin, rsb, rdo, (rdq, rdk, rdv), rev=True, name='l1_ret_b_bwd')
    dp1 = [rdq, rdk, rdv, rdg]
    dh3 = _mm_nt(dp1, w['odd_in'], name='l1_in_dx')
    g_odd_in = _mm_tn(h3, dp1, name='l1_in_dw')
    mods = mods + wsrc.send_grads('early', dict(ffn_in1=g_ffn_in1, ffn_out1=g_ffn_out1, odd_in=g_odd_in,
                                                odd_out=g_odd_out))[0, 0]
    dx2, dz0, s_l1n1 = _row_bwd(x2, dx3, dh3, mods, ng[2], shift=12, scale=13, y=z0, gate=5, name='l1_norm1_bwd')
    du0 = _ffn_dx(dz0, w['ffn_out'], u0, lead=0, name='ffn_out_dx')
    g_ffn_out0 = _mm_tn(a0, dz0, name='ffn_out_dw')
    dh2 = _mm_nt(du0, w['ffn_in'], lead=0, name='ffn_in_dx')
    g_ffn_in0 = _mm_tn(h2, du0, name='ffn_in_dw')
    mods = mods + wsrc.send_grads('mid', dict(ffn_in0=g_ffn_in0, ffn_out0=g_ffn_out0))[0, 0]
    dx1, dy0, s_l0n2 = _row_bwd(x1, dx2, dh2, mods, ng[1], shift=3, scale=4, y=y0, gate=2, name='l0_norm2_bwd')
    dmix0 = _mm_nt(dy0, w['even_out'], name='l0_out_dx')
    g_even_out = _mm_tn(mix0, dy0, name='l0_out_dw')
    hq, hff, hv, dlb_f, hdo, hdg, s_gain = _hgrn_bwd(p0, lb, hsf, None, None, rev=False, name='l0_hgrn_f_bwd',
                                                     head=(hos, dmix0, gain))
    hq, hfb, hv, dlb_b = _hgrn_bwd(p0, lb, hsb, hdo, (hq, hv), rev=True, name='l0_hgrn_b_bwd')
    adq, dkp, adv, s_gq, s_sink = _attn_bwd(p0, kp, gq, sink, acos, asin, dmix0, lc=lc, name='l0_attn_bwd')
    dkv, s_gk = _kprep_bwd(p0, gk, acos, asin, dkp, adv, name='l0_kprep_bwd')
    dp0 = jnp.concatenate([adq, dkv, hq, _bf(hff), hfb, hv, hdg], axis=1)
    dh1 = _mm_nt(dp0, w['even_in'], name='l0_in_dx')
    g_even_in = _mm_tn(h1, dp0, name='l0_in_dw')
    dx0, s_l0n1 = _row_bwd(x0, dx1, dh1, mods, ng[0], shift=0, scale=1, latent_only=True, name='l0_norm1_bwd')

    grads = dict(ffn_in0=g_ffn_in0, ffn_in1=g_ffn_in1, ffn_out0=g_ffn_out0, ffn_out1=g_ffn_out1,
                 even_in=g_even_in, even_out=g_even_out, odd_in=g_odd_in, odd_out=g_odd_out)
    sums = dict(fin=s_fin, l1n2=s_l1n2, l1n1=s_l1n1, l0n2=s_l0n2, l0n1=s_l0n1, gain=s_gain, gq=s_gq, gk=s_gk,
                sink=s_sink, dlb_f=dlb_f, dlb_b=dlb_b)
    return loss, dx0, grads, sums


def _place():
    return lax.axis_index("x"), lax.axis_index("y"), lax.axis_index("c")


def _ag8(blk, *, name):
    r, c = blk.shape
    flips = [(dx, dy, dc) for dx in (0, 1) for dy in (0, 1) for dc in (0, 1) if (dx, dy, dc) != (0, 0, 0)]

    def body(x_ref, out_ref, send_sems, recv_sems, local_sem):
        ax, ay, ac = _place()
        me = 4 * ax + 2 * ay + ac
        mine = pltpu.make_async_copy(x_ref, out_ref.at[me], local_sem)
        mine.start()
        sent = []
        for k, (dx, dy, dc) in enumerate(flips):
            peer = (lax.rem(ax + dx, 2), lax.rem(ay + dy, 2), lax.rem(ac + dc, 2))
            cp = pltpu.make_async_remote_copy(src_ref=x_ref, dst_ref=out_ref.at[me], send_sem=send_sems.at[k],
                                              recv_sem=recv_sems.at[k], device_id=peer, device_id_type=MESH)
            cp.start()
            sent.append((cp, 4 * peer[0] + 2 * peer[1] + peer[2]))
        for k, (cp, pidx) in enumerate(sent):
            pltpu.make_async_remote_copy(src_ref=x_ref, dst_ref=out_ref.at[pidx], send_sem=send_sems.at[k],
                                         recv_sem=recv_sems.at[k], device_id=(ax, ay, ac),
                                         device_id_type=MESH).wait_recv()
        for cp, _ in sent:
            cp.wait_send()
        mine.wait()

    return _pcall(
        body, name=name,
        in_specs=[pl.BlockSpec(memory_space=pltpu.VMEM)],
        out_specs=pl.BlockSpec(memory_space=pltpu.VMEM),
        out_shape=jax.ShapeDtypeStruct((8, r, c), blk.dtype),
        scratch_shapes=[pltpu.SemaphoreType.DMA((7,)), pltpu.SemaphoreType.DMA((7,)), pltpu.SemaphoreType.DMA],
    )(blk)


_HBM = pl.BlockSpec(memory_space=pltpu.HBM)
_SEM = pl.BlockSpec(memory_space=pltpu.SEMAPHORE)
_DATAFLOW = pltpu.SideEffectType.DATAFLOW_SIDE_EFFECTING


def _split_start(bufs, plan, k, *, name):
    n = len(bufs)

    def body(*refs):
        ins, send_sems, recv_sems, token = refs[:n], refs[n], refs[n + 1], refs[2 * n + 2]
        for i, (src, dst, dev) in enumerate(plan(ins)):
            pltpu.make_async_remote_copy(src_ref=src, dst_ref=dst, send_sem=send_sems.at[i], recv_sem=recv_sems.at[i],
                                         device_id=dev, device_id_type=MESH).start()
        token[...] = jnp.zeros_like(token)

    res = _pcall(
        body, name=name,
        out_shape=(pltpu.SemaphoreType.DMA((k,)), pltpu.SemaphoreType.DMA((k,)),
                   *[pltpu.HBM(b.shape, b.dtype) for b in bufs], jax.ShapeDtypeStruct((8, 128), F32)),
        in_specs=[_HBM] * n, out_specs=(_SEM, _SEM, *[_HBM] * n, pl.BlockSpec(memory_space=pltpu.VMEM)),
        input_output_aliases={i: 2 + i for i in range(n)},
        compiler_params=pltpu.CompilerParams(has_side_effects=_DATAFLOW),
    )(*[pltpu.with_memory_space_constraint(b, pltpu.HBM) for b in bufs])
    return res[0], res[1], list(res[2:2 + n]), res[2 + n]


def _split_wait(bufs, send_sems, recv_sems, plan, after, *, name):
    n = len(bufs)

    def body(*refs):
        ins, ssem, rsem = refs[:n], refs[n], refs[n + 1]
        for i, (src, dst, dev) in enumerate(plan(ins)):
            cp = pltpu.make_async_remote_copy(src_ref=src, dst_ref=dst, send_sem=ssem.at[i], recv_sem=rsem.at[i],
                                              device_id=dev, device_id_type=MESH)
            cp.wait_send()
            cp.wait_recv()

    res = _pcall(
        body, name=name, out_shape=tuple(pltpu.HBM(b.shape, b.dtype) for b in bufs),
        in_specs=[_HBM] * n + [_SEM, _SEM, pl.BlockSpec(memory_space=pl.ANY)], out_specs=tuple([_HBM] * n),
        input_output_aliases={i: i for i in range(n)},
        compiler_params=pltpu.CompilerParams(has_side_effects=_DATAFLOW),
    )(*bufs, send_sems, recv_sems, after)
    return list(res)


_CHIP_FLIPS = [(1, 0), (0, 1), (1, 1)]


class _GatheredWeights:
    FIRST = ('even_in', 'even_out')
    REST = ('ffn_in', 'ffn_out', 'odd_in', 'odd_out')

    def __init__(self, shards, reducer):
        self.shards = shards
        self.send_grads = reducer.start
        self.ici = {}
        for grp, names in (('first', self.FIRST), ('rest', self.REST)):
            src = [shards[nm].reshape(2, shards[nm].shape[0] // 2, shards[nm].shape[1]) for nm in names]
            land = [lax.empty((4,) + a.shape, a.dtype) for a in src]
            m = len(names)
            sends, recvs, bufs, token = _split_start(src + land, functools.partial(self._ici_plan, m, True), 4 * m,
                                                     name='gather_' + grp + '_ici_start')
            self.ici[grp] = (sends, recvs, bufs, m)
            self.token = token if grp == 'first' else self.token + token
        self.rest_d2d = None

    @staticmethod
    def _ici_plan(m, sending, refs):
        ax, ay, ac = _place()
        s = 2 * ax + ay
        out = []
        for a in range(m):
            for dx, dy in _CHIP_FLIPS:
                px, py = lax.rem(ax + dx, 2), lax.rem(ay + dy, 2)
                slot = s if sending else 2 * px + py
                out.append((refs[a].at[ac], refs[m + a].at[slot, ac], (px, py, ac)))
        for a in range(m):
            out.append((refs[a], refs[m + a].at[s], (ax, ay, 1 - ac)))
        return out

    @staticmethod
    def _d2d_plan(m, sending, refs):
        ax, ay, ac = _place()
        out = []
        for a in range(m):
            for dx, dy in _CHIP_FLIPS:
                sp = 2 * lax.rem(ax + dx, 2) + lax.rem(ay + dy, 2)
                out.append((refs[a].at[sp, ac], refs[a].at[sp, ac if sending else 1 - ac], (ax, ay, 1 - ac)))
        return out

    def _landed(self, grp, after):
        sends, recvs, bufs, m = self.ici[grp]
        bufs = _split_wait(bufs, sends, recvs, functools.partial(self._ici_plan, m, False), after,
                           name='gather_' + grp + '_ici_wait')
        sends, recvs, land, _ = _split_start(bufs[m:], functools.partial(self._d2d_plan, m, True), 3 * m,
                                             name='gather_' + grp + '_d2d_start')
        return sends, recvs, land, m

    def _full(self, grp, names, d2d, after):
        sends, recvs, land, m = d2d
        land = _split_wait(land, sends, recvs, functools.partial(self._d2d_plan, m, False), after,
                           name='gather_' + grp + '_d2d_wait')
        return {nm: _from_shards(nm, g.reshape((4,) + self.shards[nm].shape)) for nm, g in zip(names, land)}

    def first(self, after):
        return self._full('first', self.FIRST, self._landed('first', after), after)

    def rest_landed(self, after):
        self.rest_d2d = self._landed('rest', after)

    def rest(self, after):
        return self._full('rest', self.REST, self.rest_d2d, after)


def _to_sibling(arrs, *, name):
    n = len(arrs)

    def body(*refs):
        ins, outs = refs[:n], refs[n:2 * n]
        send_sems, recv_sems = refs[2 * n:]
        ax, ay, ac = _place()
        cps = [pltpu.make_async_remote_copy(src_ref=ins[a], dst_ref=outs[a], send_sem=send_sems.at[a],
                                            recv_sem=recv_sems.at[a], device_id=(ax, ay, 1 - ac),
                                            device_id_type=MESH) for a in range(n)]
        for cp in cps:
            cp.start()
        for cp in cps:
            cp.wait_recv()
        for cp in cps:
            cp.wait_send()

    hbm = pl.BlockSpec(memory_space=pl.ANY)
    return _pcall(
        body, name=name, in_specs=[hbm] * n, out_specs=[hbm] * n,
        out_shape=[jax.ShapeDtypeStruct(a.shape, a.dtype) for a in arrs],
        scratch_shapes=[pltpu.SemaphoreType.DMA((n,))] * 2,
    )(*arrs)


def _mod_fwd(cond_raw, mw, mb, *, name):
    _, d, n = mw.shape

    def body(c_ref, w_ref, b_ref, o_ref):
        cv = c_ref[...]
        o_ref[...] = _dot(cv * _sigmoid(cv), w_ref[...]) + b_ref[...]

    return _pcall(
        body, name=name, grid=(2,),
        in_specs=[pl.BlockSpec((16, d), lambda l: (0, 0)), pl.BlockSpec((None, d, n), lambda l: (l, 0, 0)),
                  pl.BlockSpec((None, 1, n), lambda l: (l, 0, 0))],
        out_specs=pl.BlockSpec((None, 16, n), lambda l: (l, 0, 0)),
        out_shape=jax.ShapeDtypeStruct((2, 16, n), F32),
    )(cond_raw, mw, mb)


def _mod_bwd(cond_raw, dms, mw, *, name):
    _, d, n = mw.shape

    def body(c_ref, dm_ref, w_ref, gw_ref, dc_ref):
        @pl.when(pl.program_id(0) == 0)
        def _():
            dc_ref[...] = jnp.zeros_like(dc_ref)
        cv = c_ref[...]
        gw_ref[...] = _dot_tn(cv * _sigmoid(cv), dm_ref[...])
        dc_ref[...] += _dot_nt(dm_ref[...], w_ref[...])

    return _pcall(
        body, name=name, grid=(2,),
        in_specs=[pl.BlockSpec((16, d), lambda l: (0, 0)), pl.BlockSpec((None, 16, n), lambda l: (l, 0, 0)),
                  pl.BlockSpec((None, d, n), lambda l: (l, 0, 0))],
        out_specs=[pl.BlockSpec((None, d, n), lambda l: (l, 0, 0)), pl.BlockSpec((16, d), lambda l: (0, 0))],
        out_shape=[jax.ShapeDtypeStruct((2, d, n), F32), jax.ShapeDtypeStruct((16, d), F32)],
    )(cond_raw, dms, mw)


def _lb_fwd(hgrn_lb, *, name):
    def body(a_ref, o_ref):
        a0, a1 = a_ref[0:1, :], a_ref[1:2, :]
        m = jnp.maximum(a0, a1)
        e0, e1 = jnp.exp(a0 - m), jnp.exp(a1 - m)
        o_ref[...] = e0 / (e0 + e1)

    return _pcall(body, name=name, out_shape=jax.ShapeDtypeStruct((1, hgrn_lb.shape[1]), F32))(hgrn_lb)


PACK_TILES = ('l0n1', 'l0n2', 'l1n1', 'l1n2', 'fin', 'gq', 'gk', 'gain', 'dlb_f', 'dlb_b', 'sink')
PACK_ROW = {nm: 8 * i for i, nm in enumerate(PACK_TILES)}
MOD_SOURCE = ((('l0n1', 0), ('l0n1', 1), ('l0n2', 2), ('l0n2', 0), ('l0n2', 1), ('l1n1', 2)),
              (('l1n1', 0), ('l1n1', 1), ('l1n2', 2), ('l1n2', 0), ('l1n2', 1), ('fin', 2)))


def _small_finalize(gath, lb_pad, *, name):
    d = gath.shape[2]

    def body(g_ref, lb_ref, small_ref, glb_ref, gmb_ref, dm_ref):
        tot = g_ref[0]
        for e in range(1, 8):
            tot = tot + g_ref[e]

        def row(nm, r=0):
            return tot[PACK_ROW[nm] + r:PACK_ROW[nm] + r + 1, :]

        for k, nm in enumerate(('l0n1', 'l0n2', 'l1n1', 'l1n2')):
            small_ref[k:k + 1, :] = row(nm, 3) + row(nm, 7)
        for k, nm in ((4, 'gq'), (5, 'gk')):
            small_ref[k:k + 1, :] = row(nm) + pltpu.roll(row(nm), d - 64, 1)
        small_ref[6:7, :] = row('gain')
        small_ref[7:8, :] = row('sink')
        lbv = lb_ref[...]
        g0 = (row('dlb_f') + row('dlb_b')) * lbv * (1.0 - lbv)
        glb_ref[...] = jnp.zeros_like(glb_ref)
        glb_ref[0:1, :] = g0
        glb_ref[1:2, :] = -g0
        dm_ref[...] = jnp.zeros_like(dm_ref)
        for l in range(2):
            for part in range(6):
                nm, r = MOD_SOURCE[l][part]
                gmb_ref[l * 6 + part:l * 6 + part + 1, :] = row(nm, r) + row(nm, r + 4)
                rl = PACK_ROW[nm] + r + 4
                for e in range(8):
                    dm_ref[l, part, e:e + 1, :] = g_ref[e, rl:rl + 1, :]
                dm_ref[l, part, 8:9, :] = row(nm, r)

    return _pcall(
        body, name=name,
        out_shape=[jax.ShapeDtypeStruct((8, d), F32), jax.ShapeDtypeStruct((8, d), F32),
                   jax.ShapeDtypeStruct((12, d), F32), jax.ShapeDtypeStruct((2, 6, 16, d), F32)],
    )(gath, lb_pad)


def _cctx_grad(gath, c_ctx2, *, name):
    def body(g_ref, c_ref, o_ref):
        tot = ((g_ref[0, 0:1, :] + g_ref[2, 0:1, :]) + g_ref[4, 0:1, :]) + g_ref[6, 0:1, :]
        cv = c_ref[...]
        s = _sigmoid(cv)
        o_ref[...] = tot * (s * (1.0 + cv * (1.0 - s)))

    return _pcall(body, name=name, out_shape=jax.ShapeDtypeStruct(c_ctx2.shape, F32))(gath, c_ctx2)


def _row_block(r, c, limit=256 * 1024):
    best = None
    for br in range(16, r + 1, 16):
        if r % br == 0 and br * c <= limit:
            best = br
    return best if best is not None else r


def _sum4(own, landed, core, *, name):
    _, r, c = own.shape
    br = _row_block(r, c, 512 * 1024)

    def body(core_ref, own_ref, land_ref, o_ref):
        s = 2 * lax.axis_index("x") + lax.axis_index("y")
        p = [jnp.where(s == k, own_ref[k], land_ref[k]).astype(F32) for k in range(4)]
        o_ref[...] = ((p[0] + p[1]) + p[2]) + p[3]

    blk = pl.BlockSpec((4, br, c), lambda i, core_ref: (0, i, 0))
    spec = pltpu.PrefetchScalarGridSpec(
        num_scalar_prefetch=1, grid=(r // br,), in_specs=[blk, blk],
        out_specs=pl.BlockSpec((None, br, c), lambda i, core_ref: (core_ref[0], i, 0)))
    return _pcall(body, name=name, grid_spec=spec, out_shape=jax.ShapeDtypeStruct((2, r, c), F32))(core, own, landed)


def _exchange_halves(arrs, *, name):
    n = len(arrs)

    def body(*refs):
        ins, outs = refs[:n], refs[n:2 * n]
        send_sems, recv_sems = refs[2 * n:]
        ax, ay, ac = _place()
        cps = [pltpu.make_async_remote_copy(src_ref=ins[a].at[ac], dst_ref=outs[a].at[ac], send_sem=send_sems.at[a],
                                            recv_sem=recv_sems.at[a], device_id=(ax, ay, 1 - ac),
                                            device_id_type=MESH) for a in range(n)]
        for cp in cps:
            cp.start()
        for a in range(n):
            pltpu.make_async_remote_copy(src_ref=ins[a].at[ac], dst_ref=outs[a].at[1 - ac], send_sem=send_sems.at[a],
                                         recv_sem=recv_sems.at[a], device_id=(ax, ay, ac),
                                         device_id_type=MESH).wait_recv()
        for cp in cps:
            cp.wait_send()

    hbm = pl.BlockSpec(memory_space=pl.ANY)
    return _pcall(
        body, name=name, in_specs=[hbm] * n, out_specs=[hbm] * n,
        out_shape=[jax.ShapeDtypeStruct(a.shape, a.dtype) for a in arrs],
        input_output_aliases={a: a for a in range(n)},
        scratch_shapes=[pltpu.SemaphoreType.DMA((n,))] * 2,
    )(*arrs)


def _add2(a, b, *, name):
    r, c = a.shape
    br = _row_block(r, c, 1024 * 1024)

    def body(a_ref, b_ref, o_ref):
        o_ref[...] = (a_ref[...].astype(F32) + b_ref[...].astype(F32)).astype(BF16)

    blk = pl.BlockSpec((br, c), lambda i: (i, 0))
    return _pcall(body, name=name, grid=(r // br,), in_specs=[blk, blk], out_specs=blk,
                  out_shape=jax.ShapeDtypeStruct((r, c), BF16))(a, b)


def _adam(w, gs, m, v, *, name):
    r, c = w.shape
    br = _row_block(r, c)
    ng = len(gs)
    c1 = 1.0 - ADAM_B1 ** ADAM_STEP
    c2 = 1.0 - ADAM_B2 ** ADAM_STEP

    def body(*refs):
        w_ref, m_ref, v_ref = refs[0], refs[1 + ng], refs[2 + ng]
        outs = refs[3 + ng:]
        g = refs[1][...]
        for k in range(1, ng):
            g = g + refs[1 + k][...]
        mn = ADAM_B1 * m_ref[...] + (1.0 - ADAM_B1) * g
        vn = ADAM_B2 * v_ref[...] + (1.0 - ADAM_B2) * (g * g)
        if ng > 1:
            outs[0][...] = g
        d_out, m_out, v_out = outs[-3:]
        m_out[...] = mn
        v_out[...] = vn
        d_out[...] = -ADAM_LR * ((mn / c1) / (jnp.sqrt(vn / c2) + ADAM_EPS) + ADAM_WD * w_ref[...])

    blk = pl.BlockSpec((br, c), lambda i: (i, 0))
    nout = 4 if ng > 1 else 3
    res = _pcall(body, name=name, grid=(r // br,), in_specs=[blk] * (3 + ng), out_specs=[blk] * nout,
                 out_shape=[jax.ShapeDtypeStruct((r, c), F32)] * nout)(w, *gs, m, v)
    return list(res) if ng > 1 else [gs[0]] + list(res)


def _grad_halves(name, g, ac):
    if name.endswith('_in'):
        n = g.shape[1] // 4
        if name == 'ffn_in':
            assert n == FFN_BK
        order = _ffn_order(g.shape[1]) if name == 'ffn_in' else range(4)
        v = jnp.stack([g[:, b * n:(b + 1) * n] for b in order])
        per = [v[:, :g.shape[0] // 2], v[:, g.shape[0] // 2:]]
    else:
        k4, n = g.shape
        v = g.reshape(4, 2, k4 // 8, n)
        per = [v[:, 0], v[:, 1]]
    first = ac == 0
    return _bf(jnp.where(first, per[0], per[1])), _bf(jnp.where(first, per[1], per[0]))


class _GradReducer:
    def __init__(self):
        self.flight = {}

    @staticmethod
    def _plan(m, sending, refs):
        ax, ay, ac = _place()
        s = 2 * ax + ay
        out = []
        for a in range(m):
            for dx, dy in _CHIP_FLIPS:
                px, py = lax.rem(ax + dx, 2), lax.rem(ay + dy, 2)
                sp = 2 * px + py
                out.append((refs[a].at[sp], refs[m + a].at[s if sending else sp], (px, py, ac)))
        return out

    def start(self, grp, grads):
        ac = lax.axis_index("c")
        names = list(grads)
        halves = [_grad_halves(nm.rstrip('01'), grads[nm], ac) for nm in names]
        theirs = _to_sibling([h[1] for h in halves], name='swap_core_halves_' + grp)
        pair = [_add2(h[0].reshape(-1, b.shape[-1]), b.reshape(-1, b.shape[-1]), name='add_cores').reshape(b.shape)
                for h, b in zip(halves, theirs)]
        m = len(names)
        land = [lax.empty(a.shape, a.dtype) for a in pair]
        sends, recvs, bufs, token = _split_start(pair + land, functools.partial(self._plan, m, True), 3 * m,
                                                 name='scatter_' + grp + '_start')
        self.flight[grp] = (names, sends, recvs, bufs)
        return token

    def finish(self, grp, after):
        names, sends, recvs, bufs = self.flight.pop(grp)
        m = len(names)
        bufs = _split_wait(bufs, sends, recvs, functools.partial(self._plan, m, False), after,
                           name='scatter_' + grp + '_wait')
        core = lax.axis_index("c").astype(jnp.int32).reshape(1)
        sums = [_sum4(p, l, core, name='sum_chips') for p, l in zip(bufs[:m], bufs[m:])]
        both = _exchange_halves(sums, name='gather_core_halves_' + grp)
        return {nm: g.reshape(-1, g.shape[-1]) for nm, g in zip(names, both)}


def _from_shards(name, g):
    _, r, n = g.shape
    if name == 'ffn_in':
        assert n == FFN_BK
        v = g.reshape(4, 2, r // 2, n)
        return jnp.concatenate([v[b] for b in _ffn_order(4 * n)], axis=-1)
    if name == 'ffn_out':
        return g.reshape(4, 2, r // 2, n).transpose(1, 0, 2, 3).reshape(2, 2 * r, n)
    if name in ('even_in', 'odd_in'):
        return jnp.concatenate([g[b] for b in range(4)], axis=-1)
    return g.reshape(4 * r, n)


def kernel(x, c, ctx, c_ctx, mod_w, mod_b, norm_g, ffn_w_in, ffn_w_out, even_w_in, even_w_out, attn_qk_norm_g, attn_sink, hgrn_out_norm_g, hgrn_lb, odd_w_in, odd_w_out, loss_target, m_c_ctx, m_mod_w, m_mod_b, m_norm_g, m_ffn_w_in, m_ffn_w_out, m_even_w_in, m_even_w_out, m_attn_qk_norm_g, m_attn_sink, m_hgrn_out_norm_g, m_hgrn_lb, m_odd_w_in, m_odd_w_out, v_c_ctx, v_mod_w, v_mod_b, v_norm_g, v_ffn_w_in, v_ffn_w_out, v_even_w_in, v_even_w_out, v_attn_qk_norm_g, v_attn_sink, v_hgrn_out_norm_g, v_hgrn_lb, v_odd_w_in, v_odd_w_out):
    d = x.shape[-1]
    lc = ctx.shape[1]
    assert lc == TM and d == 1024
    ax, ay, ac = _place()
    s = 2 * ax + ay
    me = 4 * ax + 2 * ay + ac
    nmod = mod_w.shape[2]

    def pad8(v):
        return jnp.pad(v, ((0, 8 - v.shape[0]), (0, 0)))

    pack = jnp.concatenate([pad8(c), pad8(norm_g.reshape(1, d))], axis=0)
    g1 = _ag8(pack, name='gather_cond')
    c_all = g1[:, 0, :]
    ng = g1[0::2, 8, :].reshape(4, 2, 2, d // 4).transpose(1, 2, 0, 3).reshape(4, d)

    cond_raw = jnp.concatenate([c_all, pad8(c_ctx.reshape(1, d))], axis=0)
    mb_sh = lax.dynamic_slice_in_dim(mod_b, s * nmod, nmod, axis=1).reshape(2, 1, nmod)
    mpart = _mod_fwd(cond_raw, mod_w, mb_sh, name='mod_fwd')
    g3 = _ag8(mpart.reshape(32, nmod), name='gather_mods')
    mods_full = g3[0::2].reshape(4, 2, 16, nmod).transpose(1, 2, 0, 3).reshape(2, 16, 4 * nmod)
    m_lat = lax.dynamic_index_in_dim(mods_full, me, axis=1, keepdims=False)
    mods = jnp.stack([mods_full[:, 8], m_lat], axis=1).reshape(24, d)

    names = ['ffn_in', 'ffn_out', 'even_in', 'even_out', 'odd_in', 'odd_out']
    shards = [_bf(v.reshape(-1, v.shape[-1])) for v in (ffn_w_in, ffn_w_out, even_w_in, even_w_out, odd_w_in, odd_w_out)]
    shards, mods = lax.optimization_barrier((shards, mods))
    reducer = _GradReducer()
    wsrc = _GatheredWeights(dict(zip(names, shards)), reducer)

    lb = _lb_fwd(hgrn_lb, name='hgrn_lower_bound')
    small = dict(gq=jnp.tile(attn_qk_norm_g[0, 0], 2).reshape(1, 128), gk=jnp.tile(attn_qk_norm_g[0, 1], 2).reshape(1, 128),
                 sink=attn_sink[0], gain=hgrn_out_norm_g, lb=lb)
    x0 = jnp.concatenate([ctx[0], x[0]], axis=0) + wsrc.token[0, 0]
    loss_t, dx0, grads, sums = _local_step(x0, loss_target[0], mods, ng, wsrc, small)
    loss = lax.psum(loss_t[0, 0], ("x", "y", "c"))
    grad_x = dx0[None]

    def tile(v):
        return jnp.pad(v, ((0, 8 - v.shape[0]), (0, d - v.shape[1])))

    sums = dict(sums, sink=sums['sink'][:, 0].reshape(1, 8))
    g4 = _ag8(jnp.concatenate([tile(sums[nm]) for nm in PACK_TILES], axis=0), name='gather_row_sums')
    small_g, glb, gmb, dmat = _small_finalize(g4, tile(lb)[0:1], name='small_grads')
    dms = lax.dynamic_slice_in_dim(dmat.transpose(0, 2, 1, 3).reshape(2, 16, 6 * d), s * nmod, nmod, axis=2)
    g_mod_w, dcond = _mod_bwd(cond_raw, dms, mod_w, name='mod_bwd')
    g5 = _ag8(dcond[8:16], name='gather_dcond')
    g_c_ctx = _cctx_grad(g5, c_ctx.reshape(8, d // 8).reshape(1, d), name='c_ctx_grad')

    late = {nm: grads[nm] for nm in ('even_in', 'even_out')}
    late, g_c_ctx = lax.optimization_barrier((late, g_c_ctx))
    token = reducer.start('late', late)
    full = reducer.finish('early', token)

    def upd(wv, gs, mv, vv, name):
        shp = wv.shape
        c2 = shp[-1]
        out = _adam(wv.reshape(-1, c2), [g.reshape(-1, c2) for g in gs], mv.reshape(-1, c2), vv.reshape(-1, c2), name=name)
        return [o.reshape(shp) for o in out]

    res = {}
    res['c_ctx'] = upd(c_ctx.reshape(8, d // 8), [g_c_ctx.reshape(8, d // 8)], m_c_ctx.reshape(8, d // 8), v_c_ctx.reshape(8, d // 8), 'adam_c_ctx')
    res['c_ctx'] = [o.reshape(d) for o in res['c_ctx']]
    res['mod_w'] = upd(mod_w, [g_mod_w], m_mod_w, v_mod_w, 'adam_mod_w')
    res['mod_b'] = upd(mod_b, [gmb.reshape(2, 6 * d)], m_mod_b, v_mod_b, 'adam_mod_b')
    g_ng = lax.dynamic_slice_in_dim(small_g[0:4].reshape(2, 2, d), s * (d // 4), d // 4, axis=2)
    res['norm_g'] = upd(norm_g, [g_ng], m_norm_g, v_norm_g, 'adam_norm_g')
    g_qk = jnp.stack([small_g[4, 0:64], small_g[5, 0:64]]).reshape(1, 2, 64)
    res['attn_qk_norm_g'] = upd(attn_qk_norm_g, [g_qk], m_attn_qk_norm_g, v_attn_qk_norm_g, 'adam_qk_gain')
    res['attn_sink'] = upd(attn_sink, [small_g[7, 0:8].reshape(1, 8)], m_attn_sink, v_attn_sink, 'adam_sink')
    res['hgrn_out_norm_g'] = upd(hgrn_out_norm_g, [small_g[6, 0:128].reshape(1, 128)], m_hgrn_out_norm_g, v_hgrn_out_norm_g, 'adam_head_gain')
    res['hgrn_lb'] = upd(hgrn_lb, [glb[0:2, 0:hgrn_lb.shape[1]]], m_hgrn_lb, v_hgrn_lb, 'adam_hgrn_lb')
    res['odd_w_in'] = upd(odd_w_in, [full['odd_in']], m_odd_w_in, v_odd_w_in, 'adam_odd_in')
    res['odd_w_out'] = upd(odd_w_out, [full['odd_out']], m_odd_w_out, v_odd_w_out, 'adam_odd_out')
    full.update(reducer.finish('mid', res['odd_w_in'][1]))
    g_ffn_in = jnp.concatenate([full['ffn_in0'], full['ffn_in1']], axis=0)
    g_ffn_out = jnp.concatenate([full['ffn_out0'], full['ffn_out1']], axis=0)
    res['ffn_w_in'] = upd(ffn_w_in, [g_ffn_in], m_ffn_w_in, v_ffn_w_in, 'adam_ffn_in')
    res['ffn_w_out'] = upd(ffn_w_out, [g_ffn_out], m_ffn_w_out, v_ffn_w_out, 'adam_ffn_out')
    full.update(reducer.finish('late', res['ffn_w_in'][1]))
    res['even_w_in'] = upd(even_w_in, [full['even_in']], m_even_w_in, v_even_w_in, 'adam_even_in')
    res['even_w_out'] = upd(even_w_out, [full['even_out']], m_even_w_out, v_even_w_out, 'adam_even_out')

    order = ['c_ctx', 'mod_w', 'mod_b', 'norm_g', 'ffn_w_in', 'ffn_w_out', 'even_w_in', 'even_w_out',
             'attn_qk_norm_g', 'attn_sink', 'hgrn_out_norm_g', 'hgrn_lb', 'odd_w_in', 'odd_w_out']
    outs = [loss, grad_x]
    for k in range(4):
        outs += [res[nm][k] for nm in order]
    return tuple(outs)
```

```python
import functools
import math

import numpy as np
import jax
import jax.numpy as jnp
from jax import lax
from jax.experimental import pallas as pl
from jax.experimental.pallas import tpu as pltpu

F32 = jnp.float32
BF16 = jnp.bfloat16
EPS = 1e-6
TM = 256
CHUNK = 64
QB = 256
WINDOW = 128
NEG = -1e30
MESH = pl.DeviceIdType.MESH

ADAM_LR, ADAM_B1, ADAM_B2, ADAM_EPS, ADAM_WD, ADAM_STEP = 0.001, 0.9, 0.999, 1e-08, 0.01, 10


def _pcall(body, **kw):
    return pl.pallas_call(body, **kw)


def _pick(n, cap):
    best = None
    for m in range(128, min(n, cap) + 1, 128):
        if n % m == 0:
            best = m
    assert best is not None, (n, cap)
    return best


def _bf(x):
    return x.astype(BF16)


def _dot(a, b):
    return jnp.dot(_bf(a), _bf(b), preferred_element_type=F32)


def _dot_nt(a, b):
    return lax.dot_general(_bf(a), _bf(b), (((1,), (1,)), ((), ())), preferred_element_type=F32)


def _dot_tn(a, b):
    return lax.dot_general(_bf(a), _bf(b), (((0,), (0,)), ((), ())), preferred_element_type=F32)


def _dot_exact(a, b):
    return jnp.dot(a, b, preferred_element_type=F32, precision=lax.Precision.HIGHEST)


def _sigmoid(x):
    return 1.0 / (1.0 + jnp.exp(-x))


def _iota(shape, dim):
    return lax.broadcasted_iota(jnp.int32, shape, dim)


def _parts(a):
    parts = list(a) if isinstance(a, (list, tuple)) else [a]
    widths = [p.shape[1] for p in parts]
    return parts, widths, [sum(widths[:i]) for i in range(len(parts))]


def _mm_nn(a, b, *, lead=None, out_dtype=F32, name):
    parts, widths, offs = _parts(a)
    m, k = parts[0].shape[0], sum(widths)
    n = b.shape[-1]
    bm = 1408 if (m % 1408 == 0 and k <= 1024) else (768 if m % 768 == 0 else TM)
    bn = _pick(n, 1024) if n % 512 == 0 else _pick(n, 1664)

    def body(*refs):
        b_ref, o_ref = refs[-2], refs[-1]
        acc = None
        for p_ref, w, off in zip(refs, widths, offs):
            term = _dot(p_ref[...], b_ref[off:off + w, :])
            acc = term if acc is None else acc + term
        o_ref[...] = acc.astype(o_ref.dtype)

    if lead is None:
        b_spec = pl.BlockSpec((k, bn), lambda i, j: (0, j))
    else:
        b_spec = pl.BlockSpec((None, k, bn), lambda i, j: (lead, 0, j))
    return _pcall(
        body, name=name, grid=(m // bm, n // bn),
        in_specs=[pl.BlockSpec((bm, w), lambda i, j: (i, 0)) for w in widths] + [b_spec],
        out_specs=pl.BlockSpec((bm, bn), lambda i, j: (i, j)),
        out_shape=jax.ShapeDtypeStruct((m, n), out_dtype),
    )(*parts, b)


def _mm_nt(a, b, *, lead=None, name):
    parts, widths, offs = _parts(a)
    m, n = parts[0].shape[0], sum(widths)
    k = b.shape[-2]
    bm = 1408 if (m % 1408 == 0 and n <= 1024) else (768 if m % 768 == 0 else TM)
    bk = _pick(k, 1024 if n <= 2048 else 512)

    def body(*refs):
        b_ref, o_ref = refs[-2], refs[-1]
        acc = None
        for p_ref, w, off in zip(refs, widths, offs):
            term = _dot_nt(p_ref[...], b_ref[:, off:off + w])
            acc = term if acc is None else acc + term
        o_ref[...] = acc

    if lead is None:
        b_spec = pl.BlockSpec((bk, n), lambda i, j: (j, 0))
    else:
        b_spec = pl.BlockSpec((None, bk, n), lambda i, j: (lead, j, 0))
    return _pcall(
        body, name=name, grid=(m // bm, k // bk),
        in_specs=[pl.BlockSpec((bm, w), lambda i, j: (i, 0)) for w in widths] + [b_spec],
        out_specs=pl.BlockSpec((bm, bk), lambda i, j: (i, j)),
        out_shape=jax.ShapeDtypeStruct((m, k), F32),
    )(*parts, b)


def _mm_tn(a, b, *, name):
    a_parts, a_w, a_off = _parts(a)
    b_parts, b_w, b_off = _parts(b)
    t, k, n = a_parts[0].shape[0], sum(a_w), sum(b_w)
    bt = 1408 if t % 1408 == 0 else (768 if t % 768 == 0 else TM)
    bk = _pick(k, 1536) if len(a_parts) == 1 else math.gcd(*a_w)
    if len(b_parts) == 1:
        bn = _pick(n, 1024) if n % 1024 == 0 or n < 1664 else _pick(n, 1664)
    else:
        bn = math.gcd(*b_w)
    na, nbp = len(a_parts), len(b_parts)

    def block_range(off, w, blk):
        return off // blk, w // blk

    def body(*refs):
        a_refs, b_refs, o_ref = refs[:na], refs[na:na + nbp], refs[-1]
        i, j = pl.program_id(0), pl.program_id(1)

        @pl.when(pl.program_id(2) == 0)
        def _():
            o_ref[...] = jnp.zeros_like(o_ref)

        def add(a_ref, b_ref):
            o_ref[...] += _dot_tn(a_ref[...], b_ref[...])

        for pa in range(na):
            sa, ca = block_range(a_off[pa], a_w[pa], bk)
            for pb in range(nbp):
                sb, cb = block_range(b_off[pb], b_w[pb], bn)
                if na == 1 and nbp == 1:
                    add(a_refs[0], b_refs[0])
                else:
                    pl.when((i >= sa) & (i < sa + ca) & (j >= sb) & (j < sb + cb))(
                        functools.partial(add, a_refs[pa], b_refs[pb]))

    def spec(off, w, blk, axis):
        s0, cnt = block_range(off, w, blk)

        def index(i, j, s):
            g = i if axis == 0 else j
            inside = (g >= s0) & (g < s0 + cnt)
            return (jnp.where(inside, s, 0), jnp.clip(g - s0, 0, cnt - 1))

        return pl.BlockSpec((bt, blk), index)

    return _pcall(
        body, name=name, grid=(k // bk, n // bn, t // bt),
        in_specs=[spec(o, w, bk, 0) for o, w in zip(a_off, a_w)] + [spec(o, w, bn, 1) for o, w in zip(b_off, b_w)],
        out_specs=pl.BlockSpec((bk, bn), lambda i, j, s: (i, j)),
        out_shape=jax.ShapeDtypeStruct((k, n), F32),
    )(*a_parts, *b_parts)


def _mod_row(mods_ref, lat, idx):
    return jnp.where(lat, mods_ref[idx + 6:idx + 7, :], mods_ref[idx:idx + 1, :])


def _row_step(t):
    return 768 if t % 768 == 0 else TM


def _row_fwd(x, mods, *, y=None, gate=None, g=None, shift=None, scale=None, name):
    t, d = x.shape
    has_y, has_n = y is not None, g is not None
    rt = _row_step(t)

    def body(*refs):
        refs = list(refs)
        x_ref, mods_ref = refs[0], refs[1]
        pos = 2
        if has_y:
            y_ref = refs[pos]; pos += 1
        if has_n:
            g_ref = refs[pos]; pos += 1
        outs = refs[pos:]
        for sub in range(rt // TM):
            rows = slice(sub * TM, (sub + 1) * TM)
            lat = pl.program_id(0) * (rt // TM) + sub > 0
            x1 = x_ref[rows, :]
            o = 0
            if has_y:
                x1 = x1 + _mod_row(mods_ref, lat, gate) * y_ref[rows, :]
                outs[o][rows, :] = x1; o += 1
            if has_n:
                rs = lax.rsqrt(jnp.mean(x1 * x1, axis=-1, keepdims=True) + EPS)
                hn = x1 * rs * g_ref[...]
                h = hn * (1.0 + _mod_row(mods_ref, lat, scale)) + _mod_row(mods_ref, lat, shift)
                outs[o][rows, :] = h.astype(BF16)

    row = pl.BlockSpec((rt, d), lambda i: (i, 0))
    ins, specs = [x, mods], [row, pl.BlockSpec(mods.shape, lambda i: (0, 0))]
    if has_y:
        ins.append(y); specs.append(row)
    if has_n:
        ins.append(g.reshape(1, d)); specs.append(pl.BlockSpec((1, d), lambda i: (0, 0)))
    out_shape, out_specs = [], []
    if has_y:
        out_shape.append(jax.ShapeDtypeStruct((t, d), F32)); out_specs.append(row)
    if has_n:
        out_shape.append(jax.ShapeDtypeStruct((t, d), BF16)); out_specs.append(row)
    res = _pcall(body, name=name, grid=(t // rt,), in_specs=specs, out_specs=out_specs,
                 out_shape=out_shape)(*ins)
    return res


def _acc_row(ref, r, val):
    ref[r:r + 1, :] += val


def _row_final(x, z, mods, target, *, gate, name):
    t, d = x.shape

    def body(x_ref, mods_ref, z_ref, t_ref, loss_ref, dx_ref, dz_ref, sums_ref):
        i = pl.program_id(0)
        lat = i > 0

        @pl.when(i == 0)
        def _():
            loss_ref[...] = jnp.zeros_like(loss_ref)
            sums_ref[...] = jnp.zeros_like(sums_ref)

        gt = _mod_row(mods_ref, lat, gate)
        zz = z_ref[...]
        yv = x_ref[...] + gt * zz
        keep = jnp.where(lat, 1.0, 0.0).astype(F32)
        diff = (yv - t_ref[...]) * keep
        part = jnp.sum(jnp.sum(diff * diff, axis=0, keepdims=True), axis=1, keepdims=True)
        loss_ref[...] += part * (0.5 / d)
        dy = diff * (1.0 / d)
        dx_ref[...] = dy
        dz_ref[...] = (gt * dy).astype(BF16)
        _acc_row(sums_ref, 6, jnp.sum(dy * zz, axis=0, keepdims=True))

    row = pl.BlockSpec((TM, d), lambda i: (i, 0))
    return _pcall(
        body, name=name, grid=(t // TM,),
        in_specs=[row, pl.BlockSpec(mods.shape, lambda i: (0, 0)), row,
                  pl.BlockSpec((TM, d), lambda i: (jnp.maximum(i - 1, 0), 0))],
        out_specs=[pl.BlockSpec((8, 128), lambda i: (0, 0)), row, row,
                   pl.BlockSpec((8, d), lambda i: (0, 0))],
        out_shape=[jax.ShapeDtypeStruct((8, 128), F32), jax.ShapeDtypeStruct((t, d), F32),
                   jax.ShapeDtypeStruct((t, d), BF16), jax.ShapeDtypeStruct((8, d), F32)],
    )(x, mods, z, target)


def _row_bwd(xn, dxo, dh, mods, g, *, shift, scale, y=None, gate=None, latent_only=False, name):
    t, d = xn.shape
    has_y = y is not None

    def body(*refs):
        refs = list(refs)
        x_ref, dxo_ref, dh_ref, mods_ref, g_ref = refs[:5]
        pos = 5
        if has_y:
            y_ref = refs[pos]; pos += 1
        dx_ref = refs[pos]; pos += 1
        if has_y:
            dy_ref = refs[pos]; pos += 1
        sums_ref = refs[pos]
        i = pl.program_id(0)

        @pl.when(i == 0)
        def _():
            sums_ref[...] = jnp.zeros_like(sums_ref)

        def add_sums(vals, base):
            for r, v in enumerate(vals):
                if v is not None:
                    _acc_row(sums_ref, base + r, v)

        gv = g_ref[...]
        for sub in range(rt // TM):
            rows = slice(sub * TM, (sub + 1) * TM)
            lat = i * (rt // TM) + sub > 0
            x1 = x_ref[rows, :]
            rs = lax.rsqrt(jnp.mean(x1 * x1, axis=-1, keepdims=True) + EPS)
            xh = x1 * rs
            dhv = dh_ref[rows, :]
            dn = dhv * (1.0 + _mod_row(mods_ref, lat, scale))
            dxh = dn * gv
            dx = dxo_ref[rows, :] + rs * (dxh - xh * jnp.mean(dxh * xh, axis=-1, keepdims=True))
            dx_ref[rows, :] = dx
            vals = [jnp.sum(dhv, axis=0, keepdims=True),
                    jnp.sum(dhv * (xh * gv), axis=0, keepdims=True),
                    None,
                    jnp.sum(dn * xh, axis=0, keepdims=True)]
            if has_y:
                dy_ref[rows, :] = (_mod_row(mods_ref, lat, gate) * dx).astype(BF16)
                vals[2] = jnp.sum(dx * y_ref[rows, :], axis=0, keepdims=True)
            if sub == 0:
                pl.when(i == 0)(functools.partial(add_sums, vals, 0))
                pl.when(i > 0)(functools.partial(add_sums, vals, 4))
            else:
                add_sums(vals, 4)

    rt = TM if latent_only else _row_step(t)
    row = pl.BlockSpec((rt, d), lambda i: (i, 0))
    ins = [xn, dxo, dh, mods, g.reshape(1, d)]
    specs = [row, row, row, pl.BlockSpec(mods.shape, lambda i: (0, 0)), pl.BlockSpec((1, d), lambda i: (0, 0))]
    if latent_only:
        out_shape = [jax.ShapeDtypeStruct((t - TM, d), F32)]
        out_specs = [pl.BlockSpec((TM, d), lambda i: (jnp.maximum(i - 1, 0), 0))]
    else:
        out_shape, out_specs = [jax.ShapeDtypeStruct((t, d), F32)], [row]
    if has_y:
        ins.append(y); specs.append(row)
        out_shape.append(jax.ShapeDtypeStruct((t, d), BF16)); out_specs.append(row)
    out_shape.append(jax.ShapeDtypeStruct((8, d), F32))
    out_specs.append(pl.BlockSpec((8, d), lambda i: (0, 0)))
    return _pcall(body, name=name, grid=(t // rt,), in_specs=specs, out_specs=out_specs,
                  out_shape=out_shape)(*ins)


FFN_BK = 1408


FFN_SUB = 256


def _ffn_order(n2):
    nb = n2 // (2 * FFN_BK)
    return [h * nb + j for j in range(nb) for h in (0, 1)]


def _ffn_interleave(w):
    return jnp.concatenate([w[..., b * FFN_BK:(b + 1) * FFN_BK] for b in _ffn_order(w.shape[-1])], axis=-1)


def _ffn_deinterleave(w):
    order = _ffn_order(w.shape[-1])
    return jnp.concatenate([w[..., order.index(b) * FFN_BK:(order.index(b) + 1) * FFN_BK]
                            for b in range(len(order))], axis=-1)


def _big_tile(t):
    return 768 if t % 768 == 0 else TM


def _ffn_in(h, w, *, lead, name):
    t, d = h.shape
    n2 = w.shape[-1]
    bm, bk = _big_tile(t), FFN_BK

    def body(h_ref, w_ref, u_ref, a_ref):
        hb = h_ref[...]
        for c0 in range(0, bk, FFN_SUB):
            c1 = min(c0 + FFN_SUB, bk)
            ug = _dot(hb, w_ref[:, c0:c1]).astype(BF16)
            uu = _dot(hb, w_ref[:, bk + c0:bk + c1]).astype(BF16)
            u_ref[:, c0:c1] = ug
            u_ref[:, bk + c0:bk + c1] = uu
            gv, up = ug.astype(F32), uu.astype(F32)
            a_ref[:, c0:c1] = (gv * _sigmoid(gv) * up).astype(BF16)

    return _pcall(
        body, name=name, grid=(t // bm, n2 // (2 * bk)),
        in_specs=[pl.BlockSpec((bm, d), lambda i, j: (i, 0)),
                  pl.BlockSpec((None, d, 2 * bk), lambda i, j: (lead, 0, j))],
        out_specs=[pl.BlockSpec((bm, 2 * bk), lambda i, j: (i, j)), pl.BlockSpec((bm, bk), lambda i, j: (i, j))],
        out_shape=[jax.ShapeDtypeStruct((t, n2), BF16), jax.ShapeDtypeStruct((t, n2 // 2), BF16)],
    )(h, w)


def _ffn_dx(dz, w_out, u, *, lead, name):
    t, d = dz.shape
    n2 = u.shape[1]
    bm, bk = _big_tile(t), FFN_BK

    def body(dz_ref, w_ref, u_ref, du_ref):
        dzb = dz_ref[...]
        for c0 in range(0, bk, FFN_SUB):
            c1 = min(c0 + FFN_SUB, bk)
            da = _dot_nt(dzb, w_ref[c0:c1, :])
            gv, up = u_ref[:, c0:c1].astype(F32), u_ref[:, bk + c0:bk + c1].astype(F32)
            s = _sigmoid(gv)
            du_ref[:, c0:c1] = (da * up * (s * (1.0 + gv * (1.0 - s)))).astype(BF16)
            du_ref[:, bk + c0:bk + c1] = (da * gv * s).astype(BF16)

    ublk = pl.BlockSpec((bm, 2 * bk), lambda i, j: (i, j))
    return _pcall(
        body, name=name, grid=(t // bm, n2 // (2 * bk)),
        in_specs=[pl.BlockSpec((bm, d), lambda i, j: (i, 0)),
                  pl.BlockSpec((None, bk, d), lambda i, j: (lead, j, 0)), ublk],
        out_specs=ublk, out_shape=jax.ShapeDtypeStruct((t, n2), BF16),
    )(dz, w_out, u)


def _lane(shape):
    return _iota(shape, len(shape) - 1)


def _pair_norm(x, g):
    lo = _lane(x.shape) < 64
    x2 = x * x
    s_lo = jnp.sum(jnp.where(lo, x2, 0.0), axis=-1, keepdims=True)
    s_hi = jnp.sum(jnp.where(lo, 0.0, x2), axis=-1, keepdims=True)
    rs = lax.rsqrt(jnp.where(lo, s_lo, s_hi) * (1.0 / 64) + EPS)
    return x * rs, rs


def _pair_mean(v):
    lo = _lane(v.shape) < 64
    s_lo = jnp.sum(jnp.where(lo, v, 0.0), axis=-1, keepdims=True)
    s_hi = jnp.sum(jnp.where(lo, 0.0, v), axis=-1, keepdims=True)
    return jnp.where(lo, s_lo, s_hi) * (1.0 / 64)


def _rot64(x):
    r1 = pltpu.roll(x, 32, 1)
    r2 = pltpu.roll(x, 96, 1)
    even = ((_lane(x.shape) >> 5) & 1) == 0
    return jnp.where(even, -r2, r1)


def _rope64(x, cos, sin):
    return x * cos + _rot64(x) * sin


def _rope64_t(d, cos, sin):
    return d * cos - _rot64(d * sin)


def _kprep_fwd(p, gk, cos, sin, *, name):
    t = p.shape[0]

    def body(k_ref, g_ref, c_ref, s_ref, o_ref):
        xh, _ = _pair_norm(k_ref[...], None)
        o_ref[...] = _rope64(xh * g_ref[...], c_ref[...], s_ref[...])

    blk = pl.BlockSpec((TM, 128), lambda i: (i, 0))
    return _pcall(
        body, name=name, grid=(t // TM,),
        in_specs=[pl.BlockSpec((TM, 128), lambda i: (i, 4)), pl.BlockSpec((1, 128), lambda i: (0, 0)), blk, blk],
        out_specs=blk, out_shape=jax.ShapeDtypeStruct((t, 128), F32),
    )(p, gk, cos, sin)


def _kprep_bwd(p, gk, cos, sin, dkp, dv, *, name):
    t = p.shape[0]

    def body(k_ref, g_ref, c_ref, s_ref, dkp_ref, dv_ref, o_ref, dg_ref):
        @pl.when(pl.program_id(0) == 0)
        def _():
            dg_ref[...] = jnp.zeros_like(dg_ref)
        xh, rs = _pair_norm(k_ref[...], None)
        dn = _rope64_t(dkp_ref[...], c_ref[...], s_ref[...])
        _acc_row(dg_ref, 0, jnp.sum(dn * xh, axis=0, keepdims=True))
        dxh = dn * g_ref[...]
        o_ref[:, 0:128] = (rs * (dxh - xh * _pair_mean(dxh * xh))).astype(BF16)
        o_ref[:, 128:256] = dv_ref[...].astype(BF16)

    blk = pl.BlockSpec((TM, 128), lambda i: (i, 0))
    return _pcall(
        body, name=name, grid=(t // TM,),
        in_specs=[pl.BlockSpec((TM, 128), lambda i: (i, 4)), pl.BlockSpec((1, 128), lambda i: (0, 0)), blk, blk, blk, blk],
        out_specs=[pl.BlockSpec((TM, 256), lambda i: (i, 0)), pl.BlockSpec((8, 128), lambda i: (0, 0))],
        out_shape=[jax.ShapeDtypeStruct((t, 256), BF16), jax.ShapeDtypeStruct((8, 128), F32)],
    )(p, gk, cos, sin, dkp, dv)


def _attn_common(i, t, lc, kp_ref, v_ref):
    span = QB + 2 * WINDOW
    start = pl.multiple_of(jnp.clip(i * QB - WINDOW, lc, t - span), WINDOW)
    kall = jnp.concatenate([kp_ref[0:lc, :], kp_ref[pl.ds(start, span), :]], axis=0)
    vall = jnp.concatenate([v_ref[0:lc, :], v_ref[pl.ds(start, span), :]], axis=0)
    nk = lc + span
    col = _iota((QB, nk), 1)
    krow = jnp.where(col < lc, col, start + col - lc)
    qrow = i * QB + _iota((QB, nk), 0)
    valid = (col < lc) | ((qrow >= lc) & (krow >= lc) & (jnp.abs(krow - qrow) <= WINDOW))
    lo = _lane(kall.shape) < 64
    kroll, vroll = pltpu.roll(kall, 64, 1), pltpu.roll(vall, 64, 1)
    zero = jnp.zeros_like(kall)
    kvar = [[_bf(jnp.where(lo, kall, zero)), _bf(jnp.where(lo, zero, kroll))],
            [_bf(jnp.where(lo, kroll, zero)), _bf(jnp.where(lo, zero, kall))]]
    vvar = [[_bf(jnp.where(lo, vall, zero)), _bf(jnp.where(lo, zero, vroll))],
            [_bf(jnp.where(lo, vroll, zero)), _bf(jnp.where(lo, zero, vall))]]
    return start, valid, kvar, vvar


def _softmax_sink(s, valid, snk):
    s = jnp.where(valid, s, NEG)
    m = jnp.maximum(jnp.max(s, axis=-1, keepdims=True), snk)
    e = jnp.exp(s - m)
    es = jnp.exp(snk - m)
    inv = 1.0 / (jnp.sum(e, axis=-1, keepdims=True) + es)
    return e * inv, es * inv


def _attn_fwd(p, kp, gq, sink, cos, sin, *, lc, name):
    t = p.shape[0]
    scale = 64 ** -0.5

    def body(q_ref, kp_ref, v_ref, g_ref, sink_ref, c_ref, s_ref, o_ref):
        i = pl.program_id(0)
        _, valid, kvar, vvar = _attn_common(i, t, lc, kp_ref, v_ref)
        cosv, sinv, gv = c_ref[...], s_ref[...], g_ref[...]
        for j in range(4):
            xh, _ = _pair_norm(q_ref[:, 128 * j:128 * j + 128], None)
            q2 = _bf(_rope64(xh * gv, cosv, sinv) * scale)
            acc = jnp.zeros((QB, 128), F32)
            for half in range(2):
                s = _dot_nt(q2, kvar[j // 2][half])
                pr, _ = _softmax_sink(s, valid, sink_ref[2 * j + half])
                acc = acc + _dot(pr, vvar[j // 2][half])
            o_ref[:, 128 * j:128 * j + 128] = acc.astype(BF16)

    qblk = pl.BlockSpec((QB, 128), lambda i: (i, 0))
    return _pcall(
        body, name=name, grid=(t // QB,),
        in_specs=[pl.BlockSpec((QB, 512), lambda i: (i, 0)),
                  pl.BlockSpec((t, 128), lambda i: (0, 0)),
                  pl.BlockSpec((t, 128), lambda i: (0, 5)),
                  pl.BlockSpec((1, 128), lambda i: (0, 0)),
                  pl.BlockSpec(memory_space=pltpu.SMEM), qblk, qblk],
        out_specs=pl.BlockSpec((QB, 512), lambda i: (i, 0)),
        out_shape=jax.ShapeDtypeStruct((t, 512), BF16),
    )(p, kp, p, gq, sink, cos, sin)


def _attn_bwd(p, kp, gq, sink, cos, sin, dmix, *, lc, name):
    t = p.shape[0]
    scale = 64 ** -0.5
    span = QB + 2 * WINDOW

    def body(q_ref, kp_ref, v_ref, g_ref, sink_ref, c_ref, s_ref, do_ref,
             dq_ref, dk_ref, dv_ref, dg_ref, dsink_ref):
        i = pl.program_id(0)

        @pl.when(i == 0)
        def _():
            dk_ref[...] = jnp.zeros_like(dk_ref)
            dv_ref[...] = jnp.zeros_like(dv_ref)
            dg_ref[...] = jnp.zeros_like(dg_ref)
            dsink_ref[...] = jnp.zeros_like(dsink_ref)

        start, valid, kvar, vvar = _attn_common(i, t, lc, kp_ref, v_ref)
        cosv, sinv, gv = c_ref[...], s_ref[...], g_ref[...]
        nk = lc + span
        dkt = [jnp.zeros((64, nk), F32), jnp.zeros((64, nk), F32)]
        dvt = [jnp.zeros((64, nk), F32), jnp.zeros((64, nk), F32)]
        for j in range(4):
            kvh = j // 2
            xh, rs = _pair_norm(q_ref[:, 128 * j:128 * j + 128], None)
            q2 = _bf(_rope64(xh * gv, cosv, sinv) * scale)
            do2 = _bf(do_ref[:, 128 * j:128 * j + 128])
            dq2 = jnp.zeros((QB, 128), F32)
            for half in range(2):
                s = _dot_nt(q2, kvar[kvh][half])
                pr, ps = _softmax_sink(s, valid, sink_ref[2 * j + half])
                dp = _dot_nt(do2, vvar[kvh][half])
                delta = jnp.sum(pr * dp, axis=-1, keepdims=True)
                ds = pr * (dp - delta)
                dsk = jnp.sum(jnp.sum(-ps * delta, axis=0, keepdims=True), axis=1, keepdims=True)
                _acc_row(dsink_ref, 2 * j + half, jnp.broadcast_to(dsk, (1, 128)))
                dq2 = dq2 + _dot(ds, kvar[kvh][half])
                hrows = slice(64 * half, 64 * half + 64)
                dkt[kvh] = dkt[kvh] + _dot_tn(q2, ds)[hrows]
                dvt[kvh] = dvt[kvh] + _dot_tn(do2, pr)[hrows]
            dn = _rope64_t(dq2 * scale, cosv, sinv)
            _acc_row(dg_ref, 0, jnp.sum(dn * xh, axis=0, keepdims=True))
            dxh = dn * gv
            dq_ref[:, 128 * j:128 * j + 128] = (rs * (dxh - xh * _pair_mean(dxh * xh))).astype(BF16)
        dk_all = jnp.concatenate(dkt, axis=0).T
        dv_all = jnp.concatenate(dvt, axis=0).T
        dk_ref[0:lc, :] += dk_all[0:lc]
        dv_ref[0:lc, :] += dv_all[0:lc]
        dk_ref[pl.ds(start, span), :] += dk_all[lc:nk]
        dv_ref[pl.ds(start, span), :] += dv_all[lc:nk]

    qblk = pl.BlockSpec((QB, 128), lambda i: (i, 0))
    full = pl.BlockSpec((t, 128), lambda i: (0, 0))
    small = pl.BlockSpec((8, 128), lambda i: (0, 0))
    return _pcall(
        body, name=name, grid=(t // QB,),
        in_specs=[pl.BlockSpec((QB, 512), lambda i: (i, 0)), full,
                  pl.BlockSpec((t, 128), lambda i: (0, 5)),
                  pl.BlockSpec((1, 128), lambda i: (0, 0)),
                  pl.BlockSpec(memory_space=pltpu.SMEM), qblk, qblk,
                  pl.BlockSpec((QB, 512), lambda i: (i, 0))],
        out_specs=[pl.BlockSpec((QB, 512), lambda i: (i, 0)), full, full, small, small],
        out_shape=[jax.ShapeDtypeStruct((t, 512), BF16), jax.ShapeDtypeStruct((t, 128), F32),
                   jax.ShapeDtypeStruct((t, 128), F32), jax.ShapeDtypeStruct((8, 128), F32),
                   jax.ShapeDtypeStruct((8, 128), F32)],
    )(p, kp, p, gq, sink, cos, sin, dmix)


def _tri(rev):
    r, c = _iota((CHUNK, CHUNK), 0), _iota((CHUNK, CHUNK), 1)
    return (c >= r) if rev else (c <= r)


def _blk_map(nb, rev, backward):
    if not rev:
        return (lambda n: nb - 1 - n) if backward else (lambda n: n)
    if backward:
        return lambda n: jnp.where(n < nb - 1, n + 1, 0)
    return lambda n: jnp.where(n == 0, 0, nb - n)


def _chunk_order(rev, backward, nc=TM // CHUNK):
    order = list(range(nc))
    return order[::-1] if (rev != backward) else order


def _hgrn_gates(qraw, fraw, lb):
    sq = _sigmoid(qraw)
    sf = _sigmoid(fraw)
    f = lb + (1.0 - lb) * sf
    return qraw * sq, 1.0 - f, jnp.log(f), sq, sf, f


HGRN_HP = 4


def _chunk_cumsum(x, rev):
    n = x.shape[0]
    pos = _iota(x.shape, 0) & (CHUNK - 1)
    s = 1
    while s < CHUNK:
        if rev:
            x = x + jnp.where(pos < CHUNK - s, pltpu.roll(x, n - s, 0), 0.0)
        else:
            x = x + jnp.where(pos >= s, pltpu.roll(x, s, 0), 0.0)
        s *= 2
    return x


def _block_terms(lf, rev):
    b = _chunk_cumsum(lf, rev)
    mid, last = (CHUNK // 2 - 1, 0) if rev else (CHUNK // 2, CHUNK - 1)

    def chunk_row(off):
        return jnp.concatenate([jnp.broadcast_to(b[c * CHUNK + off:c * CHUNK + off + 1, :], (CHUNK, b.shape[1]))
                                for c in range(TM // CHUNK)], axis=0)

    r, bl = chunk_row(mid), chunk_row(last)
    return _tri(rev), jnp.exp(b - r), jnp.exp(r - b), jnp.exp(b), jnp.exp(bl - b), jnp.exp(bl)


def _headnorm_apply(o, gv, gain):
    n = o * lax.rsqrt(jnp.mean(o * o, axis=-1, keepdims=True) + EPS)
    if gain is not None:
        n = n * gain
    return (n * (gv * _sigmoid(gv))).astype(BF16)


def _headnorm_grad(o, gv, dy, gain):
    rs = lax.rsqrt(jnp.mean(o * o, axis=-1, keepdims=True) + EPS)
    xh = o * rs
    n = xh * gain if gain is not None else xh
    sg = _sigmoid(gv)
    dn = dy * (gv * sg)
    dg = (dy * n * (sg * (1.0 + gv * (1.0 - sg)))).astype(BF16)
    dgain = jnp.sum(dn * xh, axis=0, keepdims=True)
    dxh = dn * gain if gain is not None else dn
    return rs * (dxh - xh * jnp.mean(dxh * xh, axis=-1, keepdims=True)), dg, dgain


def _hgrn_cols(bmap, n2, c0):
    return [pl.BlockSpec((TM, 256), lambda h, n, b=b: (bmap(n), c0 // 2 + h * n2 + b)) for b in range(n2)]


def _head_cols(refs, hh):
    return refs[hh // 2][:, 128 * (hh % 2):128 * (hh % 2) + 128]


def _hgrn_fwd(p, lb, *, rev, name, ofw=None, gain=None):
    t = p.shape[0]
    nb, nc = t // TM, TM // CHUNK
    bmap = _blk_map(nb, rev, False)
    fcol = 14 if rev else 10
    fused = ofw is not None

    n2 = HGRN_HP // 2

    def body(*refs):
        q_refs, f_refs, v_refs, lb_ref = refs[:n2], refs[n2:2 * n2], refs[2 * n2:3 * n2], refs[3 * n2]
        rest = refs[3 * n2 + 1:]
        if fused:
            ofw_ref, g_refs, gain_ref = rest[0], rest[1:1 + n2], rest[1 + n2]
            o_ref, sh_ref, mix_ref, st = rest[2 + n2:]
        else:
            o_ref, sh_ref, st = rest

        @pl.when(pl.program_id(1) == 0)
        def _():
            st[...] = jnp.zeros_like(st)
        for hh in range(HGRN_HP):
            ln = slice(128 * hh, 128 * hh + 128)
            q, k, lf, _, _, _ = _hgrn_gates(_head_cols(q_refs, hh), _head_cols(f_refs, hh), lb_ref[:, ln])
            tri, eq, ek, ei, eki, eb = _block_terms(lf, rev)
            qe, ke, qi, ki, vb = _bf(q * eq), _bf(k * ek), _bf(q * ei), _bf(k * eki), _bf(_head_cols(v_refs, hh))
            intra = []
            for cc in range(nc):
                rows = slice(cc * CHUNK, (cc + 1) * CHUNK)
                a = jnp.where(tri, _dot_nt(qe[rows], ke[rows]), 0.0)
                intra.append(_dot(a, vb[rows]))
            s = st[hh]
            for cc in _chunk_order(rev, False):
                rows = slice(cc * CHUNK, (cc + 1) * CHUNK)
                sh_ref[hh, cc] = s
                o_ref[rows, ln] = intra[cc] + _dot_nt(qi[rows], s)
                s = s * eb[cc * CHUNK:cc * CHUNK + 1, :] + _dot_tn(vb[rows], ki[rows])
            st[hh] = s
            if fused:
                osum = o_ref[:, ln] + ofw_ref[:, ln]
                o_ref[:, ln] = osum
                mix_ref[:, ln] = _headnorm_apply(osum, _head_cols(g_refs, hh), gain_ref[...])

    hp, wd = HGRN_HP, 128 * HGRN_HP
    col = functools.partial(_hgrn_cols, bmap, n2)
    oblk = pl.BlockSpec((TM, wd), lambda h, n: (bmap(n), h))
    ins = [p] * (3 * n2) + [lb]
    specs = col(6) + col(fcol) + col(18) + [pl.BlockSpec((1, wd), lambda h, n: (0, h))]
    out_specs = [oblk, pl.BlockSpec((hp, nc, 128, 128), lambda h, n: (h, bmap(n), 0, 0))]
    out_shape = [jax.ShapeDtypeStruct((t, 512), F32), jax.ShapeDtypeStruct((4, t // CHUNK, 128, 128), F32)]
    if fused:
        ins += [ofw] + [p] * n2 + [gain]
        specs += [oblk] + col(22) + [pl.BlockSpec((1, 128), lambda h, n: (0, 0))]
        out_specs.append(oblk)
        out_shape.append(jax.ShapeDtypeStruct((t, 512), BF16))
    return _pcall(body, name=name, grid=(4 // hp, nb), in_specs=specs, out_specs=out_specs, out_shape=out_shape,
                  scratch_shapes=[pltpu.VMEM((hp, 128, 128), F32)])(*ins)


def _hgrn_bwd(p, lb, sh, do, prev, *, rev, name, head=None):
    t = p.shape[0]
    nb, nc = t // TM, TM // CHUNK
    bmap = _blk_map(nb, rev, True)
    fcol = 14 if rev else 10
    has_prev = prev is not None
    odt = BF16 if has_prev else F32
    fused = head is not None

    n2 = HGRN_HP // 2

    def body(*refs):
        refs = list(refs)
        q_refs, f_refs, v_refs = refs[:n2], refs[n2:2 * n2], refs[2 * n2:3 * n2]
        lb_ref, sh_ref = refs[3 * n2], refs[3 * n2 + 1]
        pos = 3 * n2 + 2
        if fused:
            osum_ref, g_refs, dmix_ref, gain_ref = refs[pos], refs[pos + 1:pos + 1 + n2], refs[pos + 1 + n2], refs[pos + 2 + n2]
            pos += 3 + n2
        else:
            do_ref = refs[pos]
            pos += 1
        if has_prev:
            pq_ref, pv_ref = refs[pos], refs[pos + 1]
            pos += 2
        dq_ref, df_ref, dv_ref, dlb_ref = refs[pos:pos + 4]
        pos += 4
        if fused:
            do_out, dg_ref, dgain_ref = refs[pos:pos + 3]
            pos += 3
        dst = refs[pos]

        @pl.when(pl.program_id(1) == 0)
        def _():
            dst[...] = jnp.zeros_like(dst)
            dlb_ref[...] = jnp.zeros_like(dlb_ref)

        if fused:
            @pl.when((pl.program_id(0) == 0) & (pl.program_id(1) == 0))
            def _():
                dgain_ref[...] = jnp.zeros_like(dgain_ref)

        cat = functools.partial(jnp.concatenate, axis=0)
        for hh in range(HGRN_HP):
            ln = slice(128 * hh, 128 * hh + 128)
            lbv = lb_ref[:, ln]
            qraw, fraw = _head_cols(q_refs, hh), _head_cols(f_refs, hh)
            q, k, lf, sq, sf, f = _hgrn_gates(qraw, fraw, lbv)
            tri, eq, ek, ei, eki, eb = _block_terms(lf, rev)
            qe, ke, qi, ki = q * eq, k * ek, q * ei, k * eki
            if fused:
                dov, dg, dgain = _headnorm_grad(osum_ref[:, ln], _head_cols(g_refs, hh), dmix_ref[:, ln], gain_ref[...])
                do_out[:, ln] = dov
                dg_ref[:, ln] = dg
                _acc_row(dgain_ref, 0, dgain)
            else:
                dov = do_ref[:, ln]
            qeb, keb, qib, kib, vb, dob = _bf(qe), _bf(ke), _bf(qi), _bf(ki), _bf(_head_cols(v_refs, hh)), _bf(dov)
            dv, dqe, dke, dqi = [None] * nc, [None] * nc, [None] * nc, [None] * nc
            for cc in range(nc):
                rows = slice(cc * CHUNK, (cc + 1) * CHUNK)
                a = jnp.where(tri, _dot_nt(qeb[rows], keb[rows]), 0.0)
                da = jnp.where(tri, _dot_nt(dob[rows], vb[rows]), 0.0)
                dv[cc] = _dot_tn(a, dob[rows])
                dqe[cc], dke[cc] = _dot(da, keb[rows]), _dot_tn(da, qeb[rows])
                dqi[cc] = _dot(dob[rows], sh_ref[hh, cc])
            dki, dbl = [None] * nc, [None] * nc
            ds = dst[hh]
            for cc in _chunk_order(rev, True):
                rows = slice(cc * CHUNK, (cc + 1) * CHUNK)
                ebc = eb[cc * CHUNK:cc * CHUNK + 1, :]
                dv[cc] = dv[cc] + _dot_nt(kib[rows], ds)
                dki[cc] = _dot(vb[rows], ds)
                dbl[cc] = jnp.broadcast_to(jnp.sum(dki[cc] * ki[rows], axis=0, keepdims=True)
                                           + jnp.sum(ds * sh_ref[hh, cc], axis=0, keepdims=True) * ebc, (CHUNK, 128))
                ds = ds * ebc + _dot_tn(dob[rows], qib[rows])
            dst[hh] = ds
            dqe, dke, dqi, dki, dv, dbl = cat(dqe), cat(dke), cat(dqi), cat(dki), cat(dv), cat(dbl)
            dq = dqe * eq + dqi * ei
            dk = dke * ek + dki * eki
            last = 0 if rev else CHUNK - 1
            db = dqe * qe - dke * ke + dqi * qi - dki * ki
            db = db + jnp.where((_iota(db.shape, 0) & (CHUNK - 1)) == last, dbl, 0.0)
            dlf = _chunk_cumsum(db, not rev)
            dqr = dq * (sq * (1.0 + qraw * (1.0 - sq)))
            dfv = dlf / f - dk
            dfr = dfv * (1.0 - lbv) * (sf * (1.0 - sf))
            dlb_ref[:, ln] += jnp.sum(dfv * (1.0 - sf), axis=0, keepdims=True)
            if has_prev:
                dqr = dqr + pq_ref[:, ln]
                dv = dv + pv_ref[:, ln]
            dq_ref[:, ln] = dqr.astype(odt)
            df_ref[:, ln] = dfr.astype(odt)
            dv_ref[:, ln] = dv.astype(odt)

    hp, wd = HGRN_HP, 128 * HGRN_HP
    col = functools.partial(_hgrn_cols, bmap, n2)
    oblk = pl.BlockSpec((TM, wd), lambda h, n: (bmap(n), h))
    ins = [p] * (3 * n2) + [lb, sh]
    specs = col(6) + col(fcol) + col(18) + [pl.BlockSpec((1, wd), lambda h, n: (0, h)),
                                            pl.BlockSpec((hp, nc, 128, 128), lambda h, n: (h, bmap(n), 0, 0))]
    if fused:
        osum, dmix, gain = head
        ins += [osum] + [p] * n2 + [dmix, gain]
        specs += [oblk] + col(22) + [pl.BlockSpec((TM, wd), lambda h, n: (bmap(n), 4 // hp + h)),
                                     pl.BlockSpec((1, 128), lambda h, n: (0, 0))]
    else:
        ins.append(do); specs.append(oblk)
    if has_prev:
        ins += list(prev); specs += [oblk, oblk]
    out_specs = [oblk, oblk, oblk, pl.BlockSpec((1, wd), lambda h, n: (0, h))]
    out_shape = [jax.ShapeDtypeStruct((t, 512), odt)] * 3 + [jax.ShapeDtypeStruct((1, 512), F32)]
    if fused:
        out_specs += [oblk, oblk, pl.BlockSpec((8, 128), lambda h, n: (0, 0))]
        out_shape += [jax.ShapeDtypeStruct((t, 512), F32), jax.ShapeDtypeStruct((t, 512), BF16),
                      jax.ShapeDtypeStruct((8, 128), F32)]
    return _pcall(body, name=name, grid=(4 // hp, nb), in_specs=specs, out_specs=out_specs, out_shape=out_shape,
                  scratch_shapes=[pltpu.VMEM((hp, 128, 128), F32)])(*ins)


def _rope256(x, cos, sin):
    x1, x2 = x[:, 0:128], x[:, 128:256]
    return jnp.concatenate([x1 * cos - x2 * sin, x2 * cos + x1 * sin], axis=-1)


def _rope256_t(d, cos, sin):
    d1, d2 = d[:, 0:128], d[:, 128:256]
    return jnp.concatenate([d1 * cos + d2 * sin, d2 * cos - d1 * sin], axis=-1)


RET_DK, RET_DV, RET_H = 256, 512, 4
RET_KSCALE = RET_DK ** -0.5
RCH = TM
RET_HP = 4


def _ret_terms(lg, rev):
    r, c = _iota((RCH, RCH), 0), _iota((RCH, RCH), 1)
    rel = ((c - r) if rev else (r - c)).astype(F32)
    dmat = jnp.where(rel >= 0, jnp.exp(lg[:, 0:1] * jnp.maximum(rel, 0.0)), 0.0)
    pos = _iota((RCH, 1), 0).astype(F32)
    cnt = (RCH - pos) if rev else (pos + 1.0)
    ei = jnp.exp(lg * cnt)
    eki = jnp.exp(lg * (RCH - cnt))
    eb = jnp.exp(lg * float(RCH))
    return dmat, ei, eki, eb


def _ret_fwd(p, lgt, cos, sin, *, rev, name, ofw=None):
    t = p.shape[0]
    nb, nc = t // TM, TM // RCH
    bmap = _blk_map(nb, rev, False)
    fused = ofw is not None

    def body(*refs):
        q_ref, k_ref, v_ref, lg_ref, c_ref, s_ref = refs[:6]
        if fused:
            ofw_ref, g_ref, o_ref, sh_ref, mix_ref, st = refs[6:]
        else:
            o_ref, sh_ref, st = refs[6:]

        @pl.when(pl.program_id(1) == 0)
        def _():
            st[...] = jnp.zeros_like(st)
        for hh in range(RET_HP):
            qc, vc = slice(RET_DK * hh, RET_DK * (hh + 1)), slice(RET_DV * hh, RET_DV * (hh + 1))
            dmat, ei, eki, eb = _ret_terms(lg_ref[hh], rev)
            for cc in _chunk_order(rev, False, nc):
                rows = slice(cc * RCH, (cc + 1) * RCH)
                cosv, sinv = c_ref[rows, :], s_ref[rows, :]
                q = _rope256(q_ref[rows, qc].astype(F32), cosv, sinv)
                k = _rope256(k_ref[rows, qc].astype(F32), cosv, sinv) * RET_KSCALE
                v = v_ref[rows, vc]
                s0 = st[hh]
                sh_ref[hh, cc] = s0.astype(BF16)
                a = _dot_nt(q, k) * dmat
                o = _dot(a, v) + _dot_nt(q * ei, s0)
                st[hh] = s0 * eb + _dot_tn(v, k * eki)
                if fused:
                    o = o + ofw_ref[rows, vc]
                    mix_ref[rows, vc] = _headnorm_apply(o, g_ref[rows, vc].astype(F32), None)
                o_ref[rows, vc] = o

    hp = RET_HP
    tab = pl.BlockSpec((TM, 128), lambda h, n: (bmap(n), 0))
    oblk = pl.BlockSpec((TM, hp * RET_DV), lambda h, n: (bmap(n), h))
    ins = [p, p, p, lgt, cos, sin]
    specs = [pl.BlockSpec((TM, hp * RET_DK), lambda h, n: (bmap(n), h)),
             pl.BlockSpec((TM, hp * RET_DK), lambda h, n: (bmap(n), RET_H // hp + h)),
             pl.BlockSpec((TM, hp * RET_DV), lambda h, n: (bmap(n), RET_H // hp + h)),
             pl.BlockSpec((hp, 1, RET_DK), lambda h, n: (h, 0, 0)), tab, tab]
    out_specs = [oblk, pl.BlockSpec((hp, nc, RET_DV, RET_DK), lambda h, n: (h, bmap(n), 0, 0))]
    out_shape = [jax.ShapeDtypeStruct((t, RET_H * RET_DV), F32),
                 jax.ShapeDtypeStruct((RET_H, t // RCH, RET_DV, RET_DK), BF16)]
    if fused:
        ins += [ofw, p]
        specs += [oblk, pl.BlockSpec((TM, hp * RET_DV), lambda h, n: (bmap(n), 2 * RET_H // hp + h))]
        out_specs.append(oblk)
        out_shape.append(jax.ShapeDtypeStruct((t, RET_H * RET_DV), BF16))
    return _pcall(body, name=name, grid=(RET_H // hp, nb), in_specs=specs, out_specs=out_specs, out_shape=out_shape,
                  scratch_shapes=[pltpu.VMEM((hp, RET_DV, RET_DK), F32)])(*ins)


def _ret_bwd(p, lgt, cos, sin, sh, do, prev, *, rev, name, head=None):
    t = p.shape[0]
    nb, nc = t // TM, TM // RCH
    bmap = _blk_map(nb, rev, True)
    has_prev = prev is not None
    odt = BF16 if has_prev else F32
    fused = head is not None

    def body(*refs):
        refs = list(refs)
        q_ref, k_ref, v_ref, lg_ref, c_ref, s_ref, sh_ref = refs[:7]
        if fused:
            osum_ref, g_ref, dmix_ref = refs[7:10]
            pos = 10
        else:
            do_ref = refs[7]
            pos = 8
        if has_prev:
            pq_ref, pk_ref, pv_ref = refs[pos:pos + 3]
            pos += 3
        dq_ref, dk_ref, dv_ref = refs[pos:pos + 3]
        pos += 3
        if fused:
            do_out, dg_ref = refs[pos:pos + 2]
            pos += 2
        dst = refs[pos]

        @pl.when(pl.program_id(1) == 0)
        def _():
            dst[...] = jnp.zeros_like(dst)

        for hh in range(RET_HP):
            qc, vc = slice(RET_DK * hh, RET_DK * (hh + 1)), slice(RET_DV * hh, RET_DV * (hh + 1))
            dmat, ei, eki, eb = _ret_terms(lg_ref[hh], rev)
            for cc in _chunk_order(rev, True, nc):
                rows = slice(cc * RCH, (cc + 1) * RCH)
                cosv, sinv = c_ref[rows, :], s_ref[rows, :]
                q = _rope256(q_ref[rows, qc].astype(F32), cosv, sinv)
                k = _rope256(k_ref[rows, qc].astype(F32), cosv, sinv) * RET_KSCALE
                v = v_ref[rows, vc]
                if fused:
                    dov, dg, _ = _headnorm_grad(osum_ref[rows, vc], g_ref[rows, vc].astype(F32), dmix_ref[rows, vc], None)
                    do_out[rows, vc] = dov
                    dg_ref[rows, vc] = dg
                else:
                    dov = do_ref[rows, vc]
                s0 = sh_ref[hh, cc]
                dsc = dst[hh]
                qi, ki = q * ei, k * eki
                a = _dot_nt(q, k) * dmat
                da = _dot_nt(dov, v) * dmat
                dv = _dot_tn(a, dov) + _dot_nt(ki, dsc)
                dqs = _dot(da, k) + _dot(dov, s0) * ei
                dks = _dot_tn(da, q) + _dot(v, dsc) * eki
                dst[hh] = dsc * eb + _dot_tn(dov, qi)
                dq = _rope256_t(dqs, cosv, sinv)
                dk = _rope256_t(dks * RET_KSCALE, cosv, sinv)
                if has_prev:
                    dq = dq + pq_ref[rows, qc]
                    dk = dk + pk_ref[rows, qc]
                    dv = dv + pv_ref[rows, vc]
                dq_ref[rows, qc] = dq.astype(odt)
                dk_ref[rows, qc] = dk.astype(odt)
                dv_ref[rows, vc] = dv.astype(odt)

    hp = RET_HP
    tab = pl.BlockSpec((TM, 128), lambda h, n: (bmap(n), 0))
    qblk = pl.BlockSpec((TM, hp * RET_DK), lambda h, n: (bmap(n), h))
    vblk = pl.BlockSpec((TM, hp * RET_DV), lambda h, n: (bmap(n), h))
    ins = [p, p, p, lgt, cos, sin, sh]
    specs = [qblk, pl.BlockSpec((TM, hp * RET_DK), lambda h, n: (bmap(n), RET_H // hp + h)),
             pl.BlockSpec((TM, hp * RET_DV), lambda h, n: (bmap(n), RET_H // hp + h)),
             pl.BlockSpec((hp, 1, RET_DK), lambda h, n: (h, 0, 0)), tab, tab,
             pl.BlockSpec((hp, nc, RET_DV, RET_DK), lambda h, n: (h, bmap(n), 0, 0))]
    if fused:
        osum, dmix = head
        ins += [osum, p, dmix]
        specs += [vblk, pl.BlockSpec((TM, hp * RET_DV), lambda h, n: (bmap(n), 2 * RET_H // hp + h)), vblk]
    else:
        ins.append(do); specs.append(vblk)
    if has_prev:
        ins += list(prev); specs += [qblk, qblk, vblk]
    out_specs = [qblk, qblk, vblk]
    out_shape = [jax.ShapeDtypeStruct((t, RET_H * RET_DK), odt), jax.ShapeDtypeStruct((t, RET_H * RET_DK), odt),
                 jax.ShapeDtypeStruct((t, RET_H * RET_DV), odt)]
    if fused:
        out_specs += [vblk, vblk]
        out_shape += [jax.ShapeDtypeStruct((t, RET_H * RET_DV), F32), jax.ShapeDtypeStruct((t, RET_H * RET_DV), BF16)]
    return _pcall(body, name=name, grid=(RET_H // hp, nb), in_specs=specs, out_specs=out_specs, out_shape=out_shape,
                  scratch_shapes=[pltpu.VMEM((hp, RET_DV, RET_DK), F32)])(*ins)


def _rope_tables(lc, l):
    tt = jnp.arange(l)
    row, colp = (tt // 64).astype(F32), (tt % 64).astype(F32)
    inv = 10000.0 ** (-jnp.arange(16, dtype=F32) / 16)
    ang = jnp.concatenate([row[:, None] * inv, colp[:, None] * inv], axis=-1)
    ang = jnp.concatenate([jnp.zeros((lc, 32), F32), ang], axis=0)
    acos, asin = jnp.tile(jnp.cos(ang), (1, 4)), jnp.tile(jnp.sin(ang), (1, 4))
    theta = 1.0 / (10000.0 ** jnp.linspace(0.0, 1.0, 128, dtype=F32))
    rang = jnp.arange(l, dtype=F32)[:, None] * theta
    rang = jnp.concatenate([jnp.zeros((lc, 128), F32), rang], axis=0)
    return acos, asin, jnp.cos(rang), jnp.sin(rang)


class _Weights:
    def __init__(self, w):
        self.w = w

    def first(self, after):
        return self.w

    def rest_landed(self, after):
        pass

    def rest(self, after):
        return self.w

    def send_grads(self, grp, grads):
        return jnp.zeros((8, 128), F32)


def _local_step(x0, target, mods, ng, wsrc, small):
    t, d = x0.shape
    l = target.shape[0]
    lc = t - l
    acos, asin, rcos, rsin = _rope_tables(lc, l)
    lg_fw = jnp.log(1.0 - 2.0 ** (-5.0 - jnp.arange(RET_H, dtype=F32)))
    lgt_fw = jnp.broadcast_to(lg_fw[:, None, None], (RET_H, 1, RET_DK))
    lgt_bw = jnp.broadcast_to(lg_fw[::-1][:, None, None], (RET_H, 1, RET_DK))
    gq, gk, sink, gain, lb = small['gq'], small['gk'], small['sink'], small['gain'], small['lb']

    (h1,) = _row_fwd(x0, mods, g=ng[0], shift=0, scale=1, name='l0_norm1')
    w = wsrc.first(h1)
    p0 = _mm_nn(h1, w['even_in'], name='l0_in')
    kp = _kprep_fwd(p0, gk, acos, asin, name='l0_kprep')
    att = _attn_fwd(p0, kp, gq, sink, acos, asin, lc=lc, name='l0_attn')
    hof, hsf = _hgrn_fwd(p0, lb, rev=False, name='l0_hgrn_f')
    wsrc.rest_landed(hof)
    hos, hsb, bmix = _hgrn_fwd(p0, lb, rev=True, name='l0_hgrn_b', ofw=hof, gain=gain)
    mix0 = [att, bmix]
    y0 = _mm_nn(mix0, w['even_out'], name='l0_out')
    x1, h2 = _row_fwd(x0, mods, y=y0, gate=2, g=ng[1], shift=3, scale=4, name='l0_norm2')
    w = dict(w, **wsrc.rest(h2))
    u0, a0 = _ffn_in(h2, w['ffn_in'], lead=0, name='ffn_in')
    z0 = _mm_nn(a0, w['ffn_out'], lead=0, name='ffn_out')
    x2, h3 = _row_fwd(x1, mods, y=z0, gate=5, g=ng[2], shift=12, scale=13, name='l1_norm1')
    p1 = _mm_nn(h3, w['odd_in'], out_dtype=BF16, name='l1_in')
    rof, rsf = _ret_fwd(p1, lgt_fw, rcos, rsin, rev=False, name='l1_ret_f')
    ros, rsb, mix1 = _ret_fwd(p1, lgt_bw, rcos, rsin, rev=True, name='l1_ret_b', ofw=rof)
    y1 = _mm_nn(mix1, w['odd_out'], name='l1_out')
    x3, h4 = _row_fwd(x2, mods, y=y1, gate=14, g=ng[3], shift=15, scale=16, name='l1_norm2')
    u1, a1 = _ffn_in(h4, w['ffn_in'], lead=1, name='ffn_in')
    z1 = _mm_nn(a1, w['ffn_out'], lead=1, name='ffn_out')
    loss, dx4, dz1, s_fin = _row_final(x3, z1, mods, target, gate=17, name='loss')

    du1 = _ffn_dx(dz1, w['ffn_out'], u1, lead=1, name='ffn_out_dx')
    g_ffn_out1 = _mm_tn(a1, dz1, name='ffn_out_dw')
    dh4 = _mm_nt(du1, w['ffn_in'], lead=1, name='ffn_in_dx')
    g_ffn_in1 = _mm_tn(h4, du1, name='ffn_in_dw')
    dx3, dy1, s_l1n2 = _row_bwd(x3, dx4, dh4, mods, ng[3], shift=15, scale=16, y=y1, gate=14, name='l1_norm2_bwd')
    dmix1 = _mm_nt(dy1, w['odd_out'], name='l1_out_dx')
    g_odd_out = _mm_tn(mix1, dy1, name='l1_out_dw')
    rdq, rdk, rdv, rdo, rdg = _ret_bwd(p1, lgt_fw, rcos, rsin, rsf, None, None, rev=False, name='l1_ret_f_bwd',
                                       head=(ros, dmix1))
    rdq, rdk, rdv = _ret_bwd(p1, lgt_bw, rcos, rsin, rsb, rdo, (rdq, rdk, rdv), rev=True, name='l1_ret_b_bwd')
    dp1 = [rdq, rdk, rdv, rdg]
    dh3 = _mm_nt(dp1, w['odd_in'], name='l1_in_dx')
    g_odd_in = _mm_tn(h3, dp1, name='l1_in_dw')
    mods = mods + wsrc.send_grads('early', dict(ffn_in1=g_ffn_in1, ffn_out1=g_ffn_out1, odd_in=g_odd_in,
                                                odd_out=g_odd_out))[0, 0]
    dx2, dz0, s_l1n1 = _row_bwd(x2, dx3, dh3, mods, ng[2], shift=12, scale=13, y=z0, gate=5, name='l1_norm1_bwd')
    du0 = _ffn_dx(dz0, w['ffn_out'], u0, lead=0, name='ffn_out_dx')
    g_ffn_out0 = _mm_tn(a0, dz0, name='ffn_out_dw')
    dh2 = _mm_nt(du0, w['ffn_in'], lead=0, name='ffn_in_dx')
    g_ffn_in0 = _mm_tn(h2, du0, name='ffn_in_dw')
    mods = mods + wsrc.send_grads('mid', dict(ffn_in0=g_ffn_in0, ffn_out0=g_ffn_out0))[0, 0]
    dx1, dy0, s_l0n2 = _row_bwd(x1, dx2, dh2, mods, ng[1], shift=3, scale=4, y=y0, gate=2, name='l0_norm2_bwd')
    dmix0 = _mm_nt(dy0, w['even_out'], name='l0_out_dx')
    g_even_out = _mm_tn(mix0, dy0, name='l0_out_dw')
    hq, hff, hv, dlb_f, hdo, hdg, s_gain = _hgrn_bwd(p0, lb, hsf, None, None, rev=False, name='l0_hgrn_f_bwd',
                                                     head=(hos, dmix0, gain))
    hq, hfb, hv, dlb_b = _hgrn_bwd(p0, lb, hsb, hdo, (hq, hv), rev=True, name='l0_hgrn_b_bwd')
    adq, dkp, adv, s_gq, s_sink = _attn_bwd(p0, kp, gq, sink, acos, asin, dmix0, lc=lc, name='l0_attn_bwd')
    dkv, s_gk = _kprep_bwd(p0, gk, acos, asin, dkp, adv, name='l0_kprep_bwd')
    dp0 = jnp.concatenate([adq, dkv, hq, _bf(hff), hfb, hv, hdg], axis=1)
    dh1 = _mm_nt(dp0, w['even_in'], name='l0_in_dx')
    g_even_in = _mm_tn(h1, dp0, name='l0_in_dw')
    dx0, s_l0n1 = _row_bwd(x0, dx1, dh1, mods, ng[0], shift=0, scale=1, latent_only=True, name='l0_norm1_bwd')

    grads = dict(ffn_in0=g_ffn_in0, ffn_in1=g_ffn_in1, ffn_out0=g_ffn_out0, ffn_out1=g_ffn_out1,
                 even_in=g_even_in, even_out=g_even_out, odd_in=g_odd_in, odd_out=g_odd_out)
    sums = dict(fin=s_fin, l1n2=s_l1n2, l1n1=s_l1n1, l0n2=s_l0n2, l0n1=s_l0n1, gain=s_gain, gq=s_gq, gk=s_gk,
                sink=s_sink, dlb_f=dlb_f, dlb_b=dlb_b)
    return loss, dx0, grads, sums


def _place():
    return lax.axis_index("x"), lax.axis_index("y"), lax.axis_index("c")


def _ag8(blk, *, name):
    r, c = blk.shape
    flips = [(dx, dy, dc) for dx in (0, 1) for dy in (0, 1) for dc in (0, 1) if (dx, dy, dc) != (0, 0, 0)]

    def body(x_ref, out_ref, send_sems, recv_sems, local_sem):
        ax, ay, ac = _place()
        me = 4 * ax + 2 * ay + ac
        mine = pltpu.make_async_copy(x_ref, out_ref.at[me], local_sem)
        mine.start()
        sent = []
        for k, (dx, dy, dc) in enumerate(flips):
            peer = (lax.rem(ax + dx, 2), lax.rem(ay + dy, 2), lax.rem(ac + dc, 2))
            cp = pltpu.make_async_remote_copy(src_ref=x_ref, dst_ref=out_ref.at[me], send_sem=send_sems.at[k],
                                              recv_sem=recv_sems.at[k], device_id=peer, device_id_type=MESH)
            cp.start()
            sent.append((cp, 4 * peer[0] + 2 * peer[1] + peer[2]))
        for k, (cp, pidx) in enumerate(sent):
            pltpu.make_async_remote_copy(src_ref=x_ref, dst_ref=out_ref.at[pidx], send_sem=send_sems.at[k],
                                         recv_sem=recv_sems.at[k], device_id=(ax, ay, ac),
                                         device_id_type=MESH).wait_recv()
        for cp, _ in sent:
            cp.wait_send()
        mine.wait()

    return _pcall(
        body, name=name,
        in_specs=[pl.BlockSpec(memory_space=pltpu.VMEM)],
        out_specs=pl.BlockSpec(memory_space=pltpu.VMEM),
        out_shape=jax.ShapeDtypeStruct((8, r, c), blk.dtype),
        scratch_shapes=[pltpu.SemaphoreType.DMA((7,)), pltpu.SemaphoreType.DMA((7,)), pltpu.SemaphoreType.DMA],
    )(blk)


_HBM = pl.BlockSpec(memory_space=pltpu.HBM)
_SEM = pl.BlockSpec(memory_space=pltpu.SEMAPHORE)
_DATAFLOW = pltpu.SideEffectType.DATAFLOW_SIDE_EFFECTING


def _split_start(bufs, plan, k, *, name):
    n = len(bufs)

    def body(*refs):
        ins, send_sems, recv_sems, token = refs[:n], refs[n], refs[n + 1], refs[2 * n + 2]
        for i, (src, dst, dev) in enumerate(plan(ins)):
            pltpu.make_async_remote_copy(src_ref=src, dst_ref=dst, send_sem=send_sems.at[i], recv_sem=recv_sems.at[i],
                                         device_id=dev, device_id_type=MESH).start()
        token[...] = jnp.zeros_like(token)

    res = _pcall(
        body, name=name,
        out_shape=(pltpu.SemaphoreType.DMA((k,)), pltpu.SemaphoreType.DMA((k,)),
                   *[pltpu.HBM(b.shape, b.dtype) for b in bufs], jax.ShapeDtypeStruct((8, 128), F32)),
        in_specs=[_HBM] * n, out_specs=(_SEM, _SEM, *[_HBM] * n, pl.BlockSpec(memory_space=pltpu.VMEM)),
        input_output_aliases={i: 2 + i for i in range(n)},
        compiler_params=pltpu.CompilerParams(has_side_effects=_DATAFLOW),
    )(*[pltpu.with_memory_space_constraint(b, pltpu.HBM) for b in bufs])
    return res[0], res[1], list(res[2:2 + n]), res[2 + n]


def _split_wait(bufs, send_sems, recv_sems, plan, after, *, name):
    n = len(bufs)

    def body(*refs):
        ins, ssem, rsem = refs[:n], refs[n], refs[n + 1]
        for i, (src, dst, dev) in enumerate(plan(ins)):
            cp = pltpu.make_async_remote_copy(src_ref=src, dst_ref=dst, send_sem=ssem.at[i], recv_sem=rsem.at[i],
                                              device_id=dev, device_id_type=MESH)
            cp.wait_send()
            cp.wait_recv()

    res = _pcall(
        body, name=name, out_shape=tuple(pltpu.HBM(b.shape, b.dtype) for b in bufs),
        in_specs=[_HBM] * n + [_SEM, _SEM, pl.BlockSpec(memory_space=pl.ANY)], out_specs=tuple([_HBM] * n),
        input_output_aliases={i: i for i in range(n)},
        compiler_params=pltpu.CompilerParams(has_side_effects=_DATAFLOW),
    )(*bufs, send_sems, recv_sems, after)
    return list(res)


_CHIP_FLIPS = [(1, 0), (0, 1), (1, 1)]


class _GatheredWeights:
    FIRST = ('even_in', 'even_out')
    REST = ('ffn_in', 'ffn_out', 'odd_in', 'odd_out')

    def __init__(self, shards, reducer):
        self.shards = shards
        self.send_grads = reducer.start
        self.ici = {}
        for grp, names in (('first', self.FIRST), ('rest', self.REST)):
            src = [shards[nm].reshape(2, shards[nm].shape[0] // 2, shards[nm].shape[1]) for nm in names]
            land = [lax.empty((4,) + a.shape, a.dtype) for a in src]
            m = len(names)
            sends, recvs, bufs, token = _split_start(src + land, functools.partial(self._ici_plan, m, True), 4 * m,
                                                     name='gather_' + grp + '_ici_start')
            self.ici[grp] = (sends, recvs, bufs, m)
            self.token = token if grp == 'first' else self.token + token
        self.rest_d2d = None

    @staticmethod
    def _ici_plan(m, sending, refs):
        ax, ay, ac = _place()
        s = 2 * ax + ay
        out = []
        for a in range(m):
            for dx, dy in _CHIP_FLIPS:
                px, py = lax.rem(ax + dx, 2), lax.rem(ay + dy, 2)
                slot = s if sending else 2 * px + py
                out.append((refs[a].at[ac], refs[m + a].at[slot, ac], (px, py, ac)))
        for a in range(m):
            out.append((refs[a], refs[m + a].at[s], (ax, ay, 1 - ac)))
        return out

    @staticmethod
    def _d2d_plan(m, sending, refs):
        ax, ay, ac = _place()
        out = []
        for a in range(m):
            for dx, dy in _CHIP_FLIPS:
                sp = 2 * lax.rem(ax + dx, 2) + lax.rem(ay + dy, 2)
                out.append((refs[a].at[sp, ac], refs[a].at[sp, ac if sending else 1 - ac], (ax, ay, 1 - ac)))
        return out

    def _landed(self, grp, after):
        sends, recvs, bufs, m = self.ici[grp]
        bufs = _split_wait(bufs, sends, recvs, functools.partial(self._ici_plan, m, False), after,
                           name='gather_' + grp + '_ici_wait')
        sends, recvs, land, _ = _split_start(bufs[m:], functools.partial(self._d2d_plan, m, True), 3 * m,
                                             name='gather_' + grp + '_d2d_start')
        return sends, recvs, land, m

    def _full(self, grp, names, d2d, after):
        sends, recvs, land, m = d2d
        land = _split_wait(land, sends, recvs, functools.partial(self._d2d_plan, m, False), after,
                           name='gather_' + grp + '_d2d_wait')
        return {nm: _from_shards(nm, g.reshape((4,) + self.shards[nm].shape)) for nm, g in zip(names, land)}

    def first(self, after):
        return self._full('first', self.FIRST, self._landed('first', after), after)

    def rest_landed(self, after):
        self.rest_d2d = self._landed('rest', after)

    def rest(self, after):
        return self._full('rest', self.REST, self.rest_d2d, after)


def _to_sibling(arrs, *, name):
    n = len(arrs)

    def body(*refs):
        ins, outs = refs[:n], refs[n:2 * n]
        send_sems, recv_sems = refs[2 * n:]
        ax, ay, ac = _place()
        cps = [pltpu.make_async_remote_copy(src_ref=ins[a], dst_ref=outs[a], send_sem=send_sems.at[a],
                                            recv_sem=recv_sems.at[a], device_id=(ax, ay, 1 - ac),
                                            device_id_type=MESH) for a in range(n)]
        for cp in cps:
            cp.start()
        for cp in cps:
            cp.wait_recv()
        for cp in cps:
            cp.wait_send()

    hbm = pl.BlockSpec(memory_space=pl.ANY)
    return _pcall(
        body, name=name, in_specs=[hbm] * n, out_specs=[hbm] * n,
        out_shape=[jax.ShapeDtypeStruct(a.shape, a.dtype) for a in arrs],
        scratch_shapes=[pltpu.SemaphoreType.DMA((n,))] * 2,
    )(*arrs)


def _mod_fwd(cond_raw, mw, mb, *, name):
    _, d, n = mw.shape

    def body(c_ref, w_ref, b_ref, o_ref):
        cv = c_ref[...]
        o_ref[...] = _dot(cv * _sigmoid(cv), w_ref[...]) + b_ref[...]

    return _pcall(
        body, name=name, grid=(2,),
        in_specs=[pl.BlockSpec((16, d), lambda l: (0, 0)), pl.BlockSpec((None, d, n), lambda l: (l, 0, 0)),
                  pl.BlockSpec((None, 1, n), lambda l: (l, 0, 0))],
        out_specs=pl.BlockSpec((None, 16, n), lambda l: (l, 0, 0)),
        out_shape=jax.ShapeDtypeStruct((2, 16, n), F32),
    )(cond_raw, mw, mb)


def _mod_bwd(cond_raw, dms, mw, *, name):
    _, d, n = mw.shape

    def body(c_ref, dm_ref, w_ref, gw_ref, dc_ref):
        @pl.when(pl.program_id(0) == 0)
        def _():
            dc_ref[...] = jnp.zeros_like(dc_ref)
        cv = c_ref[...]
        gw_ref[...] = _dot_tn(cv * _sigmoid(cv), dm_ref[...])
        dc_ref[...] += _dot_nt(dm_ref[...], w_ref[...])

    return _pcall(
        body, name=name, grid=(2,),
        in_specs=[pl.BlockSpec((16, d), lambda l: (0, 0)), pl.BlockSpec((None, 16, n), lambda l: (l, 0, 0)),
                  pl.BlockSpec((None, d, n), lambda l: (l, 0, 0))],
        out_specs=[pl.BlockSpec((None, d, n), lambda l: (l, 0, 0)), pl.BlockSpec((16, d), lambda l: (0, 0))],
        out_shape=[jax.ShapeDtypeStruct((2, d, n), F32), jax.ShapeDtypeStruct((16, d), F32)],
    )(cond_raw, dms, mw)


def _lb_fwd(hgrn_lb, *, name):
    def body(a_ref, o_ref):
        a0, a1 = a_ref[0:1, :], a_ref[1:2, :]
        m = jnp.maximum(a0, a1)
        e0, e1 = jnp.exp(a0 - m), jnp.exp(a1 - m)
        o_ref[...] = e0 / (e0 + e1)

    return _pcall(body, name=name, out_shape=jax.ShapeDtypeStruct((1, hgrn_lb.shape[1]), F32))(hgrn_lb)


PACK_TILES = ('l0n1', 'l0n2', 'l1n1', 'l1n2', 'fin', 'gq', 'gk', 'gain', 'dlb_f', 'dlb_b', 'sink')
PACK_ROW = {nm: 8 * i for i, nm in enumerate(PACK_TILES)}
MOD_SOURCE = ((('l0n1', 0), ('l0n1', 1), ('l0n2', 2), ('l0n2', 0), ('l0n2', 1), ('l1n1', 2)),
              (('l1n1', 0), ('l1n1', 1), ('l1n2', 2), ('l1n2', 0), ('l1n2', 1), ('fin', 2)))


def _small_finalize(gath, lb_pad, *, name):
    d = gath.shape[2]

    def body(g_ref, lb_ref, small_ref, glb_ref, gmb_ref, dm_ref):
        tot = g_ref[0]
        for e in range(1, 8):
            tot = tot + g_ref[e]

        def row(nm, r=0):
            return tot[PACK_ROW[nm] + r:PACK_ROW[nm] + r + 1, :]

        for k, nm in enumerate(('l0n1', 'l0n2', 'l1n1', 'l1n2')):
            small_ref[k:k + 1, :] = row(nm, 3) + row(nm, 7)
        for k, nm in ((4, 'gq'), (5, 'gk')):
            small_ref[k:k + 1, :] = row(nm) + pltpu.roll(row(nm), d - 64, 1)
        small_ref[6:7, :] = row('gain')
        small_ref[7:8, :] = row('sink')
        lbv = lb_ref[...]
        g0 = (row('dlb_f') + row('dlb_b')) * lbv * (1.0 - lbv)
        glb_ref[...] = jnp.zeros_like(glb_ref)
        glb_ref[0:1, :] = g0
        glb_ref[1:2, :] = -g0
        dm_ref[...] = jnp.zeros_like(dm_ref)
        for l in range(2):
            for part in range(6):
                nm, r = MOD_SOURCE[l][part]
                gmb_ref[l * 6 + part:l * 6 + part + 1, :] = row(nm, r) + row(nm, r + 4)
                rl = PACK_ROW[nm] + r + 4
                for e in range(8):
                    dm_ref[l, part, e:e + 1, :] = g_ref[e, rl:rl + 1, :]
                dm_ref[l, part, 8:9, :] = row(nm, r)

    return _pcall(
        body, name=name,
        out_shape=[jax.ShapeDtypeStruct((8, d), F32), jax.ShapeDtypeStruct((8, d), F32),
                   jax.ShapeDtypeStruct((12, d), F32), jax.ShapeDtypeStruct((2, 6, 16, d), F32)],
    )(gath, lb_pad)


def _cctx_grad(gath, c_ctx2, *, name):
    def body(g_ref, c_ref, o_ref):
        tot = ((g_ref[0, 0:1, :] + g_ref[2, 0:1, :]) + g_ref[4, 0:1, :]) + g_ref[6, 0:1, :]
        cv = c_ref[...]
        s = _sigmoid(cv)
        o_ref[...] = tot * (s * (1.0 + cv * (1.0 - s)))

    return _pcall(body, name=name, out_shape=jax.ShapeDtypeStruct(c_ctx2.shape, F32))(gath, c_ctx2)


def _row_block(r, c, limit=256 * 1024):
    best = None
    for br in range(16, r + 1, 16):
        if r % br == 0 and br * c <= limit:
            best = br
    return best if best is not None else r


def _sum4(own, landed, core, *, name):
    _, r, c = own.shape
    br = _row_block(r, c, 512 * 1024)

    def body(core_ref, own_ref, land_ref, o_ref):
        s = 2 * lax.axis_index("x") + lax.axis_index("y")
        p = [jnp.where(s == k, own_ref[k], land_ref[k]).astype(F32) for k in range(4)]
        o_ref[...] = ((p[0] + p[1]) + p[2]) + p[3]

    blk = pl.BlockSpec((4, br, c), lambda i, core_ref: (0, i, 0))
    spec = pltpu.PrefetchScalarGridSpec(
        num_scalar_prefetch=1, grid=(r // br,), in_specs=[blk, blk],
        out_specs=pl.BlockSpec((None, br, c), lambda i, core_ref: (core_ref[0], i, 0)))
    return _pcall(body, name=name, grid_spec=spec, out_shape=jax.ShapeDtypeStruct((2, r, c), F32))(core, own, landed)


def _exchange_halves(arrs, *, name):
    n = len(arrs)

    def body(*refs):
        ins, outs = refs[:n], refs[n:2 * n]
        send_sems, recv_sems = refs[2 * n:]
        ax, ay, ac = _place()
        cps = [pltpu.make_async_remote_copy(src_ref=ins[a].at[ac], dst_ref=outs[a].at[ac], send_sem=send_sems.at[a],
                                            recv_sem=recv_sems.at[a], device_id=(ax, ay, 1 - ac),
                                            device_id_type=MESH) for a in range(n)]
        for cp in cps:
            cp.start()
        for a in range(n):
            pltpu.make_async_remote_copy(src_ref=ins[a].at[ac], dst_ref=outs[a].at[1 - ac], send_sem=send_sems.at[a],
                                         recv_sem=recv_sems.at[a], device_id=(ax, ay, ac),
                                         device_id_type=MESH).wait_recv()
        for cp in cps:
            cp.wait_send()

    hbm = pl.BlockSpec(memory_space=pl.ANY)
    return _pcall(
        body, name=name, in_specs=[hbm] * n, out_specs=[hbm] * n,
        out_shape=[jax.ShapeDtypeStruct(a.shape, a.dtype) for a in arrs],
        input_output_aliases={a: a for a in range(n)},
        scratch_shapes=[pltpu.SemaphoreType.DMA((n,))] * 2,
    )(*arrs)


def _add2(a, b, *, name):
    r, c = a.shape
    br = _row_block(r, c, 1024 * 1024)

    def body(a_ref, b_ref, o_ref):
        o_ref[...] = (a_ref[...].astype(F32) + b_ref[...].astype(F32)).astype(BF16)

    blk = pl.BlockSpec((br, c), lambda i: (i, 0))
    return _pcall(body, name=name, grid=(r // br,), in_specs=[blk, blk], out_specs=blk,
                  out_shape=jax.ShapeDtypeStruct((r, c), BF16))(a, b)


def _adam(w, gs, m, v, *, name):
    r, c = w.shape
    br = _row_block(r, c)
    ng = len(gs)
    c1 = 1.0 - ADAM_B1 ** ADAM_STEP
    c2 = 1.0 - ADAM_B2 ** ADAM_STEP

    def body(*refs):
        w_ref, m_ref, v_ref = refs[0], refs[1 + ng], refs[2 + ng]
        outs = refs[3 + ng:]
        g = refs[1][...]
        for k in range(1, ng):
            g = g + refs[1 + k][...]
        mn = ADAM_B1 * m_ref[...] + (1.0 - ADAM_B1) * g
        vn = ADAM_B2 * v_ref[...] + (1.0 - ADAM_B2) * (g * g)
        if ng > 1:
            outs[0][...] = g
        d_out, m_out, v_out = outs[-3:]
        m_out[...] = mn
        v_out[...] = vn
        d_out[...] = -ADAM_LR * ((mn / c1) / (jnp.sqrt(vn / c2) + ADAM_EPS) + ADAM_WD * w_ref[...])

    blk = pl.BlockSpec((br, c), lambda i: (i, 0))
    nout = 4 if ng > 1 else 3
    res = _pcall(body, name=name, grid=(r // br,), in_specs=[blk] * (3 + ng), out_specs=[blk] * nout,
                 out_shape=[jax.ShapeDtypeStruct((r, c), F32)] * nout)(w, *gs, m, v)
    return list(res) if ng > 1 else [gs[0]] + list(res)


def _grad_halves(name, g, ac):
    if name.endswith('_in'):
        n = g.shape[1] // 4
        if name == 'ffn_in':
            assert n == FFN_BK
        order = _ffn_order(g.shape[1]) if name == 'ffn_in' else range(4)
        v = jnp.stack([g[:, b * n:(b + 1) * n] for b in order])
        per = [v[:, :g.shape[0] // 2], v[:, g.shape[0] // 2:]]
    else:
        k4, n = g.shape
        v = g.reshape(4, 2, k4 // 8, n)
        per = [v[:, 0], v[:, 1]]
    first = ac == 0
    return _bf(jnp.where(first, per[0], per[1])), _bf(jnp.where(first, per[1], per[0]))


class _GradReducer:
    def __init__(self):
        self.flight = {}

    @staticmethod
    def _plan(m, sending, refs):
        ax, ay, ac = _place()
        s = 2 * ax + ay
        out = []
        for a in range(m):
            for dx, dy in _CHIP_FLIPS:
                px, py = lax.rem(ax + dx, 2), lax.rem(ay + dy, 2)
                sp = 2 * px + py
                out.append((refs[a].at[sp], refs[m + a].at[s if sending else sp], (px, py, ac)))
        return out

    def start(self, grp, grads):
        ac = lax.axis_index("c")
        names = list(grads)
        halves = [_grad_halves(nm.rstrip('01'), grads[nm], ac) for nm in names]
        theirs = _to_sibling([h[1] for h in halves], name='swap_core_halves_' + grp)
        pair = [_add2(h[0].reshape(-1, b.shape[-1]), b.reshape(-1, b.shape[-1]), name='add_cores').reshape(b.shape)
                for h, b in zip(halves, theirs)]
        m = len(names)
        land = [lax.empty(a.shape, a.dtype) for a in pair]
        sends, recvs, bufs, token = _split_start(pair + land, functools.partial(self._plan, m, True), 3 * m,
                                                 name='scatter_' + grp + '_start')
        self.flight[grp] = (names, sends, recvs, bufs)
        return token

    def finish(self, grp, after):
        names, sends, recvs, bufs = self.flight.pop(grp)
        m = len(names)
        bufs = _split_wait(bufs, sends, recvs, functools.partial(self._plan, m, False), after,
                           name='scatter_' + grp + '_wait')
        core = lax.axis_index("c").astype(jnp.int32).reshape(1)
        sums = [_sum4(p, l, core, name='sum_chips') for p, l in zip(bufs[:m], bufs[m:])]
        both = _exchange_halves(sums, name='gather_core_halves_' + grp)
        return {nm: g.reshape(-1, g.shape[-1]) for nm, g in zip(names, both)}


def _from_shards(name, g):
    _, r, n = g.shape
    if name == 'ffn_in':
        assert n == FFN_BK
        v = g.reshape(4, 2, r // 2, n)
        return jnp.concatenate([v[b] for b in _ffn_order(4 * n)], axis=-1)
    if name == 'ffn_out':
        return g.reshape(4, 2, r // 2, n).transpose(1, 0, 2, 3).reshape(2, 2 * r, n)
    if name in ('even_in', 'odd_in'):
        return jnp.concatenate([g[b] for b in range(4)], axis=-1)
    return g.reshape(4 * r, n)


def kernel(x, c, ctx, c_ctx, mod_w, mod_b, norm_g, ffn_w_in, ffn_w_out, even_w_in, even_w_out, attn_qk_norm_g, attn_sink, hgrn_out_norm_g, hgrn_lb, odd_w_in, odd_w_out, loss_target, m_c_ctx, m_mod_w, m_mod_b, m_norm_g, m_ffn_w_in, m_ffn_w_out, m_even_w_in, m_even_w_out, m_attn_qk_norm_g, m_attn_sink, m_hgrn_out_norm_g, m_hgrn_lb, m_odd_w_in, m_odd_w_out, v_c_ctx, v_mod_w, v_mod_b, v_norm_g, v_ffn_w_in, v_ffn_w_out, v_even_w_in, v_even_w_out, v_attn_qk_norm_g, v_attn_sink, v_hgrn_out_norm_g, v_hgrn_lb, v_odd_w_in, v_odd_w_out):
    d = x.shape[-1]
    lc = ctx.shape[1]
    assert lc == TM and d == 1024
    ax, ay, ac = _place()
    s = 2 * ax + ay
    me = 4 * ax + 2 * ay + ac
    nmod = mod_w.shape[2]

    def pad8(v):
        return jnp.pad(v, ((0, 8 - v.shape[0]), (0, 0)))

    pack = jnp.concatenate([pad8(c), pad8(norm_g.reshape(1, d))], axis=0)
    g1 = _ag8(pack, name='gather_cond')
    c_all = g1[:, 0, :]
    ng = g1[0::2, 8, :].reshape(4, 2, 2, d // 4).transpose(1, 2, 0, 3).reshape(4, d)

    cond_raw = jnp.concatenate([c_all, pad8(c_ctx.reshape(1, d))], axis=0)
    mb_sh = lax.dynamic_slice_in_dim(mod_b, s * nmod, nmod, axis=1).reshape(2, 1, nmod)
    mpart = _mod_fwd(cond_raw, mod_w, mb_sh, name='mod_fwd')
    g3 = _ag8(mpart.reshape(32, nmod), name='gather_mods')
    mods_full = g3[0::2].reshape(4, 2, 16, nmod).transpose(1, 2, 0, 3).reshape(2, 16, 4 * nmod)
    m_lat = lax.dynamic_index_in_dim(mods_full, me, axis=1, keepdims=False)
    mods = jnp.stack([mods_full[:, 8], m_lat], axis=1).reshape(24, d)

    names = ['ffn_in', 'ffn_out', 'even_in', 'even_out', 'odd_in', 'odd_out']
    shards = [_bf(v.reshape(-1, v.shape[-1])) for v in (ffn_w_in, ffn_w_out, even_w_in, even_w_out, odd_w_in, odd_w_out)]
    shards, mods = lax.optimization_barrier((shards, mods))
    reducer = _GradReducer()
    wsrc = _GatheredWeights(dict(zip(names, shards)), reducer)

    lb = _lb_fwd(hgrn_lb, name='hgrn_lower_bound')
    small = dict(gq=jnp.tile(attn_qk_norm_g[0, 0], 2).reshape(1, 128), gk=jnp.tile(attn_qk_norm_g[0, 1], 2).reshape(1, 128),
                 sink=attn_sink[0], gain=hgrn_out_norm_g, lb=lb)
    x0, _ = lax.optimization_barrier((jnp.concatenate([ctx[0], x[0]], axis=0), wsrc.token))
    loss_t, dx0, grads, sums = _local_step(x0, loss_target[0], mods, ng, wsrc, small)
    loss = lax.psum(loss_t[0, 0], ("x", "y", "c"))
    grad_x = dx0[None]

    def tile(v):
        return jnp.pad(v, ((0, 8 - v.shape[0]), (0, d - v.shape[1])))

    sums = dict(sums, sink=sums['sink'][:, 0].reshape(1, 8))
    g4 = _ag8(jnp.concatenate([tile(sums[nm]) for nm in PACK_TILES], axis=0), name='gather_row_sums')
    small_g, glb, gmb, dmat = _small_finalize(g4, tile(lb)[0:1], name='small_grads')
    dms = lax.dynamic_slice_in_dim(dmat.transpose(0, 2, 1, 3).reshape(2, 16, 6 * d), s * nmod, nmod, axis=2)
    g_mod_w, dcond = _mod_bwd(cond_raw, dms, mod_w, name='mod_bwd')
    g5 = _ag8(dcond[8:16], name='gather_dcond')
    g_c_ctx = _cctx_grad(g5, c_ctx.reshape(8, d // 8).reshape(1, d), name='c_ctx_grad')

    late = {nm: grads[nm] for nm in ('even_in', 'even_out')}
    late, g_c_ctx = lax.optimization_barrier((late, g_c_ctx))
    token = reducer.start('late', late)
    full = reducer.finish('early', token)

    def upd(wv, gs, mv, vv, name):
        shp = wv.shape
        c2 = shp[-1]
        out = _adam(wv.reshape(-1, c2), [g.reshape(-1, c2) for g in gs], mv.reshape(-1, c2), vv.reshape(-1, c2), name=name)
        return [o.reshape(shp) for o in out]

    res = {}
    res['c_ctx'] = upd(c_ctx.reshape(8, d // 8), [g_c_ctx.reshape(8, d // 8)], m_c_ctx.reshape(8, d // 8), v_c_ctx.reshape(8, d // 8), 'adam_c_ctx')
    res['c_ctx'] = [o.reshape(d) for o in res['c_ctx']]
    res['mod_w'] = upd(mod_w, [g_mod_w], m_mod_w, v_mod_w, 'adam_mod_w')
    res['mod_b'] = upd(mod_b, [gmb.reshape(2, 6 * d)], m_mod_b, v_mod_b, 'adam_mod_b')
    g_ng = lax.dynamic_slice_in_dim(small_g[0:4].reshape(2, 2, d), s * (d // 4), d // 4, axis=2)
    res['norm_g'] = upd(norm_g, [g_ng], m_norm_g, v_norm_g, 'adam_norm_g')
    g_qk = jnp.stack([small_g[4, 0:64], small_g[5, 0:64]]).reshape(1, 2, 64)
    res['attn_qk_norm_g'] = upd(attn_qk_norm_g, [g_qk], m_attn_qk_norm_g, v_attn_qk_norm_g, 'adam_qk_gain')
    res['attn_sink'] = upd(attn_sink, [small_g[7, 0:8].reshape(1, 8)], m_attn_sink, v_attn_sink, 'adam_sink')
    res['hgrn_out_norm_g'] = upd(hgrn_out_norm_g, [small_g[6, 0:128].reshape(1, 128)], m_hgrn_out_norm_g, v_hgrn_out_norm_g, 'adam_head_gain')
    res['hgrn_lb'] = upd(hgrn_lb, [glb[0:2, 0:hgrn_lb.shape[1]]], m_hgrn_lb, v_hgrn_lb, 'adam_hgrn_lb')
    res['odd_w_in'] = upd(odd_w_in, [full['odd_in']], m_odd_w_in, v_odd_w_in, 'adam_odd_in')
    res['odd_w_out'] = upd(odd_w_out, [full['odd_out']], m_odd_w_out, v_odd_w_out, 'adam_odd_out')
    full.update(reducer.finish('mid', res['odd_w_in'][1]))
    g_ffn_in = jnp.concatenate([full['ffn_in0'], full['ffn_in1']], axis=0)
    g_ffn_out = jnp.concatenate([full['ffn_out0'], full['ffn_out1']], axis=0)
    res['ffn_w_in'] = upd(ffn_w_in, [g_ffn_in], m_ffn_w_in, v_ffn_w_in, 'adam_ffn_in')
    res['ffn_w_out'] = upd(ffn_w_out, [g_ffn_out], m_ffn_w_out, v_ffn_w_out, 'adam_ffn_out')
    full.update(reducer.finish('late', res['ffn_w_in'][1]))
    res['even_w_in'] = upd(even_w_in, [full['even_in']], m_even_w_in, v_even_w_in, 'adam_even_in')
    res['even_w_out'] = upd(even_w_out, [full['even_out']], m_even_w_out, v_even_w_out, 'adam_even_out')

    order = ['c_ctx', 'mod_w', 'mod_b', 'norm_g', 'ffn_w_in', 'ffn_w_out', 'even_w_in', 'even_w_out',
             'attn_qk_norm_g', 'attn_sink', 'hgrn_out_norm_g', 'hgrn_lb', 'odd_w_in', 'odd_w_out']
    outs = [loss, grad_x]
    for k in range(4):
        outs += [res[nm][k] for nm in order]
    return tuple(outs)
```

```python
import functools
import math

import numpy as np
import jax
import jax.numpy as jnp
from jax import lax
from jax.experimental import pallas as pl
from jax.experimental.pallas import tpu as pltpu

F32 = jnp.float32
BF16 = jnp.bfloat16
EPS = 1e-6
TM = 256
CHUNK = 64
QB = 256
WINDOW = 128
NEG = -1e30
MESH = pl.DeviceIdType.MESH

ADAM_LR, ADAM_B1, ADAM_B2, ADAM_EPS, ADAM_WD, ADAM_STEP = 0.001, 0.9, 0.999, 1e-08, 0.01, 10


def _pcall(body, **kw):
    return pl.pallas_call(body, **kw)


def _pick(n, cap):
    best = None
    for m in range(128, min(n, cap) + 1, 128):
        if n % m == 0:
            best = m
    assert best is not None, (n, cap)
    return best


def _bf(x):
    return x.astype(BF16)


def _dot(a, b):
    return jnp.dot(_bf(a), _bf(b), preferred_element_type=F32)


def _dot_nt(a, b):
    return lax.dot_general(_bf(a), _bf(b), (((1,), (1,)), ((), ())), preferred_element_type=F32)


def _dot_tn(a, b):
    return lax.dot_general(_bf(a), _bf(b), (((0,), (0,)), ((), ())), preferred_element_type=F32)


def _dot_exact(a, b):
    return jnp.dot(a, b, preferred_element_type=F32, precision=lax.Precision.HIGHEST)


def _sigmoid(x):
    return 1.0 / (1.0 + jnp.exp(-x))


def _iota(shape, dim):
    return lax.broadcasted_iota(jnp.int32, shape, dim)


def _parts(a):
    parts = list(a) if isinstance(a, (list, tuple)) else [a]
    widths = [p.shape[1] for p in parts]
    return parts, widths, [sum(widths[:i]) for i in range(len(parts))]


def _mm_nn(a, b, *, lead=None, out_dtype=F32, name):
    parts, widths, offs = _parts(a)
    m, k = parts[0].shape[0], sum(widths)
    n = b.shape[-1]
    bm = 1408 if (m % 1408 == 0 and k <= 1024) else (768 if m % 768 == 0 else TM)
    bn = _pick(n, 1024) if n % 512 == 0 else _pick(n, 1664)

    def body(*refs):
        b_ref, o_ref = refs[-2], refs[-1]
        acc = None
        for p_ref, w, off in zip(refs, widths, offs):
            term = _dot(p_ref[...], b_ref[off:off + w, :])
            acc = term if acc is None else acc + term
        o_ref[...] = acc.astype(o_ref.dtype)

    if lead is None:
        b_spec = pl.BlockSpec((k, bn), lambda i, j: (0, j))
    else:
        b_spec = pl.BlockSpec((None, k, bn), lambda i, j: (lead, 0, j))
    return _pcall(
        body, name=name, grid=(m // bm, n // bn),
        in_specs=[pl.BlockSpec((bm, w), lambda i, j: (i, 0)) for w in widths] + [b_spec],
        out_specs=pl.BlockSpec((bm, bn), lambda i, j: (i, j)),
        out_shape=jax.ShapeDtypeStruct((m, n), out_dtype),
    )(*parts, b)


def _mm_nt(a, b, *, lead=None, name):
    parts, widths, offs = _parts(a)
    m, n = parts[0].shape[0], sum(widths)
    k = b.shape[-2]
    bm = 1408 if (m % 1408 == 0 and n <= 1024) else (768 if m % 768 == 0 else TM)
    bk = _pick(k, 1024 if n <= 2048 else 512)

    def body(*refs):
        b_ref, o_ref = refs[-2], refs[-1]
        acc = None
        for p_ref, w, off in zip(refs, widths, offs):
            term = _dot_nt(p_ref[...], b_ref[:, off:off + w])
            acc = term if acc is None else acc + term
        o_ref[...] = acc

    if lead is None:
        b_spec = pl.BlockSpec((bk, n), lambda i, j: (j, 0))
    else:
        b_spec = pl.BlockSpec((None, bk, n), lambda i, j: (lead, j, 0))
    return _pcall(
        body, name=name, grid=(m // bm, k // bk),
        in_specs=[pl.BlockSpec((bm, w), lambda i, j: (i, 0)) for w in widths] + [b_spec],
        out_specs=pl.BlockSpec((bm, bk), lambda i, j: (i, j)),
        out_shape=jax.ShapeDtypeStruct((m, k), F32),
    )(*parts, b)


def _mm_tn(a, b, *, name):
    a_parts, a_w, a_off = _parts(a)
    b_parts, b_w, b_off = _parts(b)
    t, k, n = a_parts[0].shape[0], sum(a_w), sum(b_w)
    bt = 1408 if t % 1408 == 0 else (768 if t % 768 == 0 else TM)
    bk = _pick(k, 1536) if len(a_parts) == 1 else math.gcd(*a_w)
    if len(b_parts) == 1:
        bn = _pick(n, 1024) if n % 1024 == 0 or n < 1664 else _pick(n, 1664)
    else:
        bn = math.gcd(*b_w)
    na, nbp = len(a_parts), len(b_parts)

    def block_range(off, w, blk):
        return off // blk, w // blk

    def body(*refs):
        a_refs, b_refs, o_ref = refs[:na], refs[na:na + nbp], refs[-1]
        i, j = pl.program_id(0), pl.program_id(1)

        @pl.when(pl.program_id(2) == 0)
        def _():
            o_ref[...] = jnp.zeros_like(o_ref)

        def add(a_ref, b_ref):
            o_ref[...] += _dot_tn(a_ref[...], b_ref[...])

        for pa in range(na):
            sa, ca = block_range(a_off[pa], a_w[pa], bk)
            for pb in range(nbp):
                sb, cb = block_range(b_off[pb], b_w[pb], bn)
                if na == 1 and nbp == 1:
                    add(a_refs[0], b_refs[0])
                else:
                    pl.when((i >= sa) & (i < sa + ca) & (j >= sb) & (j < sb + cb))(
                        functools.partial(add, a_refs[pa], b_refs[pb]))

    def spec(off, w, blk, axis):
        s0, cnt = block_range(off, w, blk)

        def index(i, j, s):
            g = i if axis == 0 else j
            inside = (g >= s0) & (g < s0 + cnt)
            return (jnp.where(inside, s, 0), jnp.clip(g - s0, 0, cnt - 1))

        return pl.BlockSpec((bt, blk), index)

    return _pcall(
        body, name=name, grid=(k // bk, n // bn, t // bt),
        in_specs=[spec(o, w, bk, 0) for o, w in zip(a_off, a_w)] + [spec(o, w, bn, 1) for o, w in zip(b_off, b_w)],
        out_specs=pl.BlockSpec((bk, bn), lambda i, j, s: (i, j)),
        out_shape=jax.ShapeDtypeStruct((k, n), F32),
    )(*a_parts, *b_parts)


def _mod_row(mods_ref, lat, idx):
    return jnp.where(lat, mods_ref[idx + 6:idx + 7, :], mods_ref[idx:idx + 1, :])


def _row_step(t):
    return 768 if t % 768 == 0 else TM


def _row_fwd(x, mods, *, y=None, gate=None, g=None, shift=None, scale=None, name):
    t, d = x.shape
    has_y, has_n = y is not None, g is not None
    rt = _row_step(t)

    def body(*refs):
        refs = list(refs)
        x_ref, mods_ref = refs[0], refs[1]
        pos = 2
        if has_y:
            y_ref = refs[pos]; pos += 1
        if has_n:
            g_ref = refs[pos]; pos += 1
        outs = refs[pos:]
        for sub in range(rt // TM):
            rows = slice(sub * TM, (sub + 1) * TM)
            lat = pl.program_id(0) * (rt // TM) + sub > 0
            x1 = x_ref[rows, :]
            o = 0
            if has_y:
                x1 = x1 + _mod_row(mods_ref, lat, gate) * y_ref[rows, :]
                outs[o][rows, :] = x1; o += 1
            if has_n:
                rs = lax.rsqrt(jnp.mean(x1 * x1, axis=-1, keepdims=True) + EPS)
                hn = x1 * rs * g_ref[...]
                h = hn * (1.0 + _mod_row(mods_ref, lat, scale)) + _mod_row(mods_ref, lat, shift)
                outs[o][rows, :] = h.astype(BF16)

    row = pl.BlockSpec((rt, d), lambda i: (i, 0))
    ins, specs = [x, mods], [row, pl.BlockSpec(mods.shape, lambda i: (0, 0))]
    if has_y:
        ins.append(y); specs.append(row)
    if has_n:
        ins.append(g.reshape(1, d)); specs.append(pl.BlockSpec((1, d), lambda i: (0, 0)))
    out_shape, out_specs = [], []
    if has_y:
        out_shape.append(jax.ShapeDtypeStruct((t, d), F32)); out_specs.append(row)
    if has_n:
        out_shape.append(jax.ShapeDtypeStruct((t, d), BF16)); out_specs.append(row)
    res = _pcall(body, name=name, grid=(t // rt,), in_specs=specs, out_specs=out_specs,
                 out_shape=out_shape)(*ins)
    return res


def _acc_row(ref, r, val):
    ref[r:r + 1, :] += val


def _row_final(x, z, mods, target, *, gate, name):
    t, d = x.shape

    def body(x_ref, mods_ref, z_ref, t_ref, loss_ref, dx_ref, dz_ref, sums_ref):
        i = pl.program_id(0)
        lat = i > 0

        @pl.when(i == 0)
        def _():
            loss_ref[...] = jnp.zeros_like(loss_ref)
            sums_ref[...] = jnp.zeros_like(sums_ref)

        gt = _mod_row(mods_ref, lat, gate)
        zz = z_ref[...]
        yv = x_ref[...] + gt * zz
        keep = jnp.where(lat, 1.0, 0.0).astype(F32)
        diff = (yv - t_ref[...]) * keep
        part = jnp.sum(jnp.sum(diff * diff, axis=0, keepdims=True), axis=1, keepdims=True)
        loss_ref[...] += part * (0.5 / d)
        dy = diff * (1.0 / d)
        dx_ref[...] = dy
        dz_ref[...] = (gt * dy).astype(BF16)
        _acc_row(sums_ref, 6, jnp.sum(dy * zz, axis=0, keepdims=True))

    row = pl.BlockSpec((TM, d), lambda i: (i, 0))
    return _pcall(
        body, name=name, grid=(t // TM,),
        in_specs=[row, pl.BlockSpec(mods.shape, lambda i: (0, 0)), row,
                  pl.BlockSpec((TM, d), lambda i: (jnp.maximum(i - 1, 0), 0))],
        out_specs=[pl.BlockSpec((8, 128), lambda i: (0, 0)), row, row,
                   pl.BlockSpec((8, d), lambda i: (0, 0))],
        out_shape=[jax.ShapeDtypeStruct((8, 128), F32), jax.ShapeDtypeStruct((t, d), F32),
                   jax.ShapeDtypeStruct((t, d), BF16), jax.ShapeDtypeStruct((8, d), F32)],
    )(x, mods, z, target)


def _row_bwd(xn, dxo, dh, mods, g, *, shift, scale, y=None, gate=None, latent_only=False, name):
    t, d = xn.shape
    has_y = y is not None

    def body(*refs):
        refs = list(refs)
        x_ref, dxo_ref, dh_ref, mods_ref, g_ref = refs[:5]
        pos = 5
        if has_y:
            y_ref = refs[pos]; pos += 1
        dx_ref = refs[pos]; pos += 1
        if has_y:
            dy_ref = refs[pos]; pos += 1
        sums_ref = refs[pos]
        i = pl.program_id(0)

        @pl.when(i == 0)
        def _():
            sums_ref[...] = jnp.zeros_like(sums_ref)

        def add_sums(vals, base):
            for r, v in enumerate(vals):
                if v is not None:
                    _acc_row(sums_ref, base + r, v)

        gv = g_ref[...]
        for sub in range(rt // TM):
            rows = slice(sub * TM, (sub + 1) * TM)
            lat = i * (rt // TM) + sub > 0
            x1 = x_ref[rows, :]
            rs = lax.rsqrt(jnp.mean(x1 * x1, axis=-1, keepdims=True) + EPS)
            xh = x1 * rs
            dhv = dh_ref[rows, :]
            dn = dhv * (1.0 + _mod_row(mods_ref, lat, scale))
            dxh = dn * gv
            dx = dxo_ref[rows, :] + rs * (dxh - xh * jnp.mean(dxh * xh, axis=-1, keepdims=True))
            dx_ref[rows, :] = dx
            vals = [jnp.sum(dhv, axis=0, keepdims=True),
                    jnp.sum(dhv * (xh * gv), axis=0, keepdims=True),
                    None,
                    jnp.sum(dn * xh, axis=0, keepdims=True)]
            if has_y:
                dy_ref[rows, :] = (_mod_row(mods_ref, lat, gate) * dx).astype(BF16)
                vals[2] = jnp.sum(dx * y_ref[rows, :], axis=0, keepdims=True)
            if sub == 0:
                pl.when(i == 0)(functools.partial(add_sums, vals, 0))
                pl.when(i > 0)(functools.partial(add_sums, vals, 4))
            else:
                add_sums(vals, 4)

    rt = TM if latent_only else _row_step(t)
    row = pl.BlockSpec((rt, d), lambda i: (i, 0))
    ins = [xn, dxo, dh, mods, g.reshape(1, d)]
    specs = [row, row, row, pl.BlockSpec(mods.shape, lambda i: (0, 0)), pl.BlockSpec((1, d), lambda i: (0, 0))]
    if latent_only:
        out_shape = [jax.ShapeDtypeStruct((t - TM, d), F32)]
        out_specs = [pl.BlockSpec((TM, d), lambda i: (jnp.maximum(i - 1, 0), 0))]
    else:
        out_shape, out_specs = [jax.ShapeDtypeStruct((t, d), F32)], [row]
    if has_y:
        ins.append(y); specs.append(row)
        out_shape.append(jax.ShapeDtypeStruct((t, d), BF16)); out_specs.append(row)
    out_shape.append(jax.ShapeDtypeStruct((8, d), F32))
    out_specs.append(pl.BlockSpec((8, d), lambda i: (0, 0)))
    return _pcall(body, name=name, grid=(t // rt,), in_specs=specs, out_specs=out_specs,
                  out_shape=out_shape)(*ins)


FFN_BK = 1408


FFN_SUB = 256


def _ffn_order(n2):
    nb = n2 // (2 * FFN_BK)
    return [h * nb + j for j in range(nb) for h in (0, 1)]


def _ffn_interleave(w):
    return jnp.concatenate([w[..., b * FFN_BK:(b + 1) * FFN_BK] for b in _ffn_order(w.shape[-1])], axis=-1)


def _ffn_deinterleave(w):
    order = _ffn_order(w.shape[-1])
    return jnp.concatenate([w[..., order.index(b) * FFN_BK:(order.index(b) + 1) * FFN_BK]
                            for b in range(len(order))], axis=-1)


def _big_tile(t):
    return 768 if t % 768 == 0 else TM


def _ffn_in(h, w, *, lead, name):
    t, d = h.shape
    n2 = w.shape[-1]
    bm, bk = _big_tile(t), FFN_BK

    def body(h_ref, w_ref, u_ref, a_ref):
        hb = h_ref[...]
        for c0 in range(0, bk, FFN_SUB):
            c1 = min(c0 + FFN_SUB, bk)
            ug = _dot(hb, w_ref[:, c0:c1]).astype(BF16)
            uu = _dot(hb, w_ref[:, bk + c0:bk + c1]).astype(BF16)
            u_ref[:, c0:c1] = ug
            u_ref[:, bk + c0:bk + c1] = uu
            gv, up = ug.astype(F32), uu.astype(F32)
            a_ref[:, c0:c1] = (gv * _sigmoid(gv) * up).astype(BF16)

    return _pcall(
        body, name=name, grid=(t // bm, n2 // (2 * bk)),
        in_specs=[pl.BlockSpec((bm, d), lambda i, j: (i, 0)),
                  pl.BlockSpec((None, d, 2 * bk), lambda i, j: (lead, 0, j))],
        out_specs=[pl.BlockSpec((bm, 2 * bk), lambda i, j: (i, j)), pl.BlockSpec((bm, bk), lambda i, j: (i, j))],
        out_shape=[jax.ShapeDtypeStruct((t, n2), BF16), jax.ShapeDtypeStruct((t, n2 // 2), BF16)],
    )(h, w)


def _ffn_dx(dz, w_out, u, *, lead, name):
    t, d = dz.shape
    n2 = u.shape[1]
    bm, bk = _big_tile(t), FFN_BK

    def body(dz_ref, w_ref, u_ref, du_ref):
        dzb = dz_ref[...]
        for c0 in range(0, bk, FFN_SUB):
            c1 = min(c0 + FFN_SUB, bk)
            da = _dot_nt(dzb, w_ref[c0:c1, :])
            gv, up = u_ref[:, c0:c1].astype(F32), u_ref[:, bk + c0:bk + c1].astype(F32)
            s = _sigmoid(gv)
            du_ref[:, c0:c1] = (da * up * (s * (1.0 + gv * (1.0 - s)))).astype(BF16)
            du_ref[:, bk + c0:bk + c1] = (da * gv * s).astype(BF16)

    ublk = pl.BlockSpec((bm, 2 * bk), lambda i, j: (i, j))
    return _pcall(
        body, name=name, grid=(t // bm, n2 // (2 * bk)),
        in_specs=[pl.BlockSpec((bm, d), lambda i, j: (i, 0)),
                  pl.BlockSpec((None, bk, d), lambda i, j: (lead, j, 0)), ublk],
        out_specs=ublk, out_shape=jax.ShapeDtypeStruct((t, n2), BF16),
    )(dz, w_out, u)


def _lane(shape):
    return _iota(shape, len(shape) - 1)


def _pair_norm(x, g):
    lo = _lane(x.shape) < 64
    x2 = x * x
    s_lo = jnp.sum(jnp.where(lo, x2, 0.0), axis=-1, keepdims=True)
    s_hi = jnp.sum(jnp.where(lo, 0.0, x2), axis=-1, keepdims=True)
    rs = lax.rsqrt(jnp.where(lo, s_lo, s_hi) * (1.0 / 64) + EPS)
    return x * rs, rs


def _pair_mean(v):
    lo = _lane(v.shape) < 64
    s_lo = jnp.sum(jnp.where(lo, v, 0.0), axis=-1, keepdims=True)
    s_hi = jnp.sum(jnp.where(lo, 0.0, v), axis=-1, keepdims=True)
    return jnp.where(lo, s_lo, s_hi) * (1.0 / 64)


def _rot64(x):
    r1 = pltpu.roll(x, 32, 1)
    r2 = pltpu.roll(x, 96, 1)
    even = ((_lane(x.shape) >> 5) & 1) == 0
    return jnp.where(even, -r2, r1)


def _rope64(x, cos, sin):
    return x * cos + _rot64(x) * sin


def _rope64_t(d, cos, sin):
    return d * cos - _rot64(d * sin)


def _kprep_fwd(p, gk, cos, sin, *, name):
    t = p.shape[0]

    def body(k_ref, g_ref, c_ref, s_ref, o_ref):
        xh, _ = _pair_norm(k_ref[...], None)
        o_ref[...] = _rope64(xh * g_ref[...], c_ref[...], s_ref[...])

    blk = pl.BlockSpec((TM, 128), lambda i: (i, 0))
    return _pcall(
        body, name=name, grid=(t // TM,),
        in_specs=[pl.BlockSpec((TM, 128), lambda i: (i, 4)), pl.BlockSpec((1, 128), lambda i: (0, 0)), blk, blk],
        out_specs=blk, out_shape=jax.ShapeDtypeStruct((t, 128), F32),
    )(p, gk, cos, sin)


def _kprep_bwd(p, gk, cos, sin, dkp, dv, *, name):
    t = p.shape[0]

    def body(k_ref, g_ref, c_ref, s_ref, dkp_ref, dv_ref, o_ref, dg_ref):
        @pl.when(pl.program_id(0) == 0)
        def _():
            dg_ref[...] = jnp.zeros_like(dg_ref)
        xh, rs = _pair_norm(k_ref[...], None)
        dn = _rope64_t(dkp_ref[...], c_ref[...], s_ref[...])
        _acc_row(dg_ref, 0, jnp.sum(dn * xh, axis=0, keepdims=True))
        dxh = dn * g_ref[...]
        o_ref[:, 0:128] = (rs * (dxh - xh * _pair_mean(dxh * xh))).astype(BF16)
        o_ref[:, 128:256] = dv_ref[...].astype(BF16)

    blk = pl.BlockSpec((TM, 128), lambda i: (i, 0))
    return _pcall(
        body, name=name, grid=(t // TM,),
        in_specs=[pl.BlockSpec((TM, 128), lambda i: (i, 4)), pl.BlockSpec((1, 128), lambda i: (0, 0)), blk, blk, blk, blk],
        out_specs=[pl.BlockSpec((TM, 256), lambda i: (i, 0)), pl.BlockSpec((8, 128), lambda i: (0, 0))],
        out_shape=[jax.ShapeDtypeStruct((t, 256), BF16), jax.ShapeDtypeStruct((8, 128), F32)],
    )(p, gk, cos, sin, dkp, dv)


def _attn_common(i, t, lc, kp_ref, v_ref):
    span = QB + 2 * WINDOW
    start = pl.multiple_of(jnp.clip(i * QB - WINDOW, lc, t - span), WINDOW)
    kall = jnp.concatenate([kp_ref[0:lc, :], kp_ref[pl.ds(start, span), :]], axis=0)
    vall = jnp.concatenate([v_ref[0:lc, :], v_ref[pl.ds(start, span), :]], axis=0)
    nk = lc + span
    col = _iota((QB, nk), 1)
    krow = jnp.where(col < lc, col, start + col - lc)
    qrow = i * QB + _iota((QB, nk), 0)
    valid = (col < lc) | ((qrow >= lc) & (krow >= lc) & (jnp.abs(krow - qrow) <= WINDOW))
    lo = _lane(kall.shape) < 64
    kroll, vroll = pltpu.roll(kall, 64, 1), pltpu.roll(vall, 64, 1)
    zero = jnp.zeros_like(kall)
    kvar = [[_bf(jnp.where(lo, kall, zero)), _bf(jnp.where(lo, zero, kroll))],
            [_bf(jnp.where(lo, kroll, zero)), _bf(jnp.where(lo, zero, kall))]]
    vvar = [[_bf(jnp.where(lo, vall, zero)), _bf(jnp.where(lo, zero, vroll))],
            [_bf(jnp.where(lo, vroll, zero)), _bf(jnp.where(lo, zero, vall))]]
    return start, valid, kvar, vvar


def _softmax_sink(s, valid, snk):
    s = jnp.where(valid, s, NEG)
    m = jnp.maximum(jnp.max(s, axis=-1, keepdims=True), snk)
    e = jnp.exp(s - m)
    es = jnp.exp(snk - m)
    inv = 1.0 / (jnp.sum(e, axis=-1, keepdims=True) + es)
    return e * inv, es * inv


def _attn_fwd(p, kp, gq, sink, cos, sin, *, lc, name):
    t = p.shape[0]
    scale = 64 ** -0.5

    def body(q_ref, kp_ref, v_ref, g_ref, sink_ref, c_ref, s_ref, o_ref):
        i = pl.program_id(0)
        _, valid, kvar, vvar = _attn_common(i, t, lc, kp_ref, v_ref)
        cosv, sinv, gv = c_ref[...], s_ref[...], g_ref[...]
        for j in range(4):
            xh, _ = _pair_norm(q_ref[:, 128 * j:128 * j + 128], None)
            q2 = _bf(_rope64(xh * gv, cosv, sinv) * scale)
            acc = jnp.zeros((QB, 128), F32)
            for half in range(2):
                s = _dot_nt(q2, kvar[j // 2][half])
                pr, _ = _softmax_sink(s, valid, sink_ref[2 * j + half])
                acc = acc + _dot(pr, vvar[j // 2][half])
            o_ref[:, 128 * j:128 * j + 128] = acc.astype(BF16)

    qblk = pl.BlockSpec((QB, 128), lambda i: (i, 0))
    return _pcall(
        body, name=name, grid=(t // QB,),
        in_specs=[pl.BlockSpec((QB, 512), lambda i: (i, 0)),
                  pl.BlockSpec((t, 128), lambda i: (0, 0)),
                  pl.BlockSpec((t, 128), lambda i: (0, 5)),
                  pl.BlockSpec((1, 128), lambda i: (0, 0)),
                  pl.BlockSpec(memory_space=pltpu.SMEM), qblk, qblk],
        out_specs=pl.BlockSpec((QB, 512), lambda i: (i, 0)),
        out_shape=jax.ShapeDtypeStruct((t, 512), BF16),
    )(p, kp, p, gq, sink, cos, sin)


def _attn_bwd(p, kp, gq, sink, cos, sin, dmix, *, lc, name):
    t = p.shape[0]
    scale = 64 ** -0.5
    span = QB + 2 * WINDOW

    def body(q_ref, kp_ref, v_ref, g_ref, sink_ref, c_ref, s_ref, do_ref,
             dq_ref, dk_ref, dv_ref, dg_ref, dsink_ref):
        i = pl.program_id(0)

        @pl.when(i == 0)
        def _():
            dk_ref[...] = jnp.zeros_like(dk_ref)
            dv_ref[...] = jnp.zeros_like(dv_ref)
            dg_ref[...] = jnp.zeros_like(dg_ref)
            dsink_ref[...] = jnp.zeros_like(dsink_ref)

        start, valid, kvar, vvar = _attn_common(i, t, lc, kp_ref, v_ref)
        cosv, sinv, gv = c_ref[...], s_ref[...], g_ref[...]
        nk = lc + span
        dkt = [jnp.zeros((64, nk), F32), jnp.zeros((64, nk), F32)]
        dvt = [jnp.zeros((64, nk), F32), jnp.zeros((64, nk), F32)]
        for j in range(4):
            kvh = j // 2
            xh, rs = _pair_norm(q_ref[:, 128 * j:128 * j + 128], None)
            q2 = _bf(_rope64(xh * gv, cosv, sinv) * scale)
            do2 = _bf(do_ref[:, 128 * j:128 * j + 128])
            dq2 = jnp.zeros((QB, 128), F32)
            for half in range(2):
                s = _dot_nt(q2, kvar[kvh][half])
                pr, ps = _softmax_sink(s, valid, sink_ref[2 * j + half])
                dp = _dot_nt(do2, vvar[kvh][half])
                delta = jnp.sum(pr * dp, axis=-1, keepdims=True)
                ds = pr * (dp - delta)
                dsk = jnp.sum(jnp.sum(-ps * delta, axis=0, keepdims=True), axis=1, keepdims=True)
                _acc_row(dsink_ref, 2 * j + half, jnp.broadcast_to(dsk, (1, 128)))
                dq2 = dq2 + _dot(ds, kvar[kvh][half])
                hrows = slice(64 * half, 64 * half + 64)
                dkt[kvh] = dkt[kvh] + _dot_tn(q2, ds)[hrows]
                dvt[kvh] = dvt[kvh] + _dot_tn(do2, pr)[hrows]
            dn = _rope64_t(dq2 * scale, cosv, sinv)
            _acc_row(dg_ref, 0, jnp.sum(dn * xh, axis=0, keepdims=True))
            dxh = dn * gv
            dq_ref[:, 128 * j:128 * j + 128] = (rs * (dxh - xh * _pair_mean(dxh * xh))).astype(BF16)
        dk_all = jnp.concatenate(dkt, axis=0).T
        dv_all = jnp.concatenate(dvt, axis=0).T
        dk_ref[0:lc, :] += dk_all[0:lc]
        dv_ref[0:lc, :] += dv_all[0:lc]
        dk_ref[pl.ds(start, span), :] += dk_all[lc:nk]
        dv_ref[pl.ds(start, span), :] += dv_all[lc:nk]

    qblk = pl.BlockSpec((QB, 128), lambda i: (i, 0))
    full = pl.BlockSpec((t, 128), lambda i: (0, 0))
    small = pl.BlockSpec((8, 128), lambda i: (0, 0))
    return _pcall(
        body, name=name, grid=(t // QB,),
        in_specs=[pl.BlockSpec((QB, 512), lambda i: (i, 0)), full,
                  pl.BlockSpec((t, 128), lambda i: (0, 5)),
                  pl.BlockSpec((1, 128), lambda i: (0, 0)),
                  pl.BlockSpec(memory_space=pltpu.SMEM), qblk, qblk,
                  pl.BlockSpec((QB, 512), lambda i: (i, 0))],
        out_specs=[pl.BlockSpec((QB, 512), lambda i: (i, 0)), full, full, small, small],
        out_shape=[jax.ShapeDtypeStruct((t, 512), BF16), jax.ShapeDtypeStruct((t, 128), F32),
                   jax.ShapeDtypeStruct((t, 128), F32), jax.ShapeDtypeStruct((8, 128), F32),
                   jax.ShapeDtypeStruct((8, 128), F32)],
    )(p, kp, p, gq, sink, cos, sin, dmix)


def _tri(rev):
    r, c = _iota((CHUNK, CHUNK), 0), _iota((CHUNK, CHUNK), 1)
    return (c >= r) if rev else (c <= r)


def _blk_map(nb, rev, backward):
    if not rev:
        return (lambda n: nb - 1 - n) if backward else (lambda n: n)
    if backward:
        return lambda n: jnp.where(n < nb - 1, n + 1, 0)
    return lambda n: jnp.where(n == 0, 0, nb - n)


def _chunk_order(rev, backward, nc=TM // CHUNK):
    order = list(range(nc))
    return order[::-1] if (rev != backward) else order


def _hgrn_gates(qraw, fraw, lb):
    sq = _sigmoid(qraw)
    sf = _sigmoid(fraw)
    f = lb + (1.0 - lb) * sf
    return qraw * sq, 1.0 - f, jnp.log(f), sq, sf, f


HGRN_HP = 4


def _chunk_cumsum(x, rev):
    n = x.shape[0]
    pos = _iota(x.shape, 0) & (CHUNK - 1)
    s = 1
    while s < CHUNK:
        if rev:
            x = x + jnp.where(pos < CHUNK - s, pltpu.roll(x, n - s, 0), 0.0)
        else:
            x = x + jnp.where(pos >= s, pltpu.roll(x, s, 0), 0.0)
        s *= 2
    return x


def _block_terms(lf, rev):
    b = _chunk_cumsum(lf, rev)
    mid, last = (CHUNK // 2 - 1, 0) if rev else (CHUNK // 2, CHUNK - 1)

    def chunk_row(off):
        return jnp.concatenate([jnp.broadcast_to(b[c * CHUNK + off:c * CHUNK + off + 1, :], (CHUNK, b.shape[1]))
                                for c in range(TM // CHUNK)], axis=0)

    r, bl = chunk_row(mid), chunk_row(last)
    return _tri(rev), jnp.exp(b - r), jnp.exp(r - b), jnp.exp(b), jnp.exp(bl - b), jnp.exp(bl)


def _headnorm_apply(o, gv, gain):
    n = o * lax.rsqrt(jnp.mean(o * o, axis=-1, keepdims=True) + EPS)
    if gain is not None:
        n = n * gain
    return (n * (gv * _sigmoid(gv))).astype(BF16)


def _headnorm_grad(o, gv, dy, gain):
    rs = lax.rsqrt(jnp.mean(o * o, axis=-1, keepdims=True) + EPS)
    xh = o * rs
    n = xh * gain if gain is not None else xh
    sg = _sigmoid(gv)
    dn = dy * (gv * sg)
    dg = (dy * n * (sg * (1.0 + gv * (1.0 - sg)))).astype(BF16)
    dgain = jnp.sum(dn * xh, axis=0, keepdims=True)
    dxh = dn * gain if gain is not None else dn
    return rs * (dxh - xh * jnp.mean(dxh * xh, axis=-1, keepdims=True)), dg, dgain


def _hgrn_cols(bmap, n2, c0):
    return [pl.BlockSpec((TM, 256), lambda h, n, b=b: (bmap(n), c0 // 2 + h * n2 + b)) for b in range(n2)]


def _head_cols(refs, hh):
    return refs[hh // 2][:, 128 * (hh % 2):128 * (hh % 2) + 128]


def _hgrn_fwd(p, lb, *, rev, name, ofw=None, gain=None):
    t = p.shape[0]
    nb, nc = t // TM, TM // CHUNK
    bmap = _blk_map(nb, rev, False)
    fcol = 14 if rev else 10
    fused = ofw is not None

    n2 = HGRN_HP // 2

    def body(*refs):
        q_refs, f_refs, v_refs, lb_ref = refs[:n2], refs[n2:2 * n2], refs[2 * n2:3 * n2], refs[3 * n2]
        rest = refs[3 * n2 + 1:]
        if fused:
            ofw_ref, g_refs, gain_ref = rest[0], rest[1:1 + n2], rest[1 + n2]
            o_ref, sh_ref, mix_ref, st = rest[2 + n2:]
        else:
            o_ref, sh_ref, st = rest

        @pl.when(pl.program_id(1) == 0)
        def _():
            st[...] = jnp.zeros_like(st)
        for hh in range(HGRN_HP):
            ln = slice(128 * hh, 128 * hh + 128)
            q, k, lf, _, _, _ = _hgrn_gates(_head_cols(q_refs, hh), _head_cols(f_refs, hh), lb_ref[:, ln])
            tri, eq, ek, ei, eki, eb = _block_terms(lf, rev)
            qe, ke, qi, ki, vb = _bf(q * eq), _bf(k * ek), _bf(q * ei), _bf(k * eki), _bf(_head_cols(v_refs, hh))
            intra = []
            for cc in range(nc):
                rows = slice(cc * CHUNK, (cc + 1) * CHUNK)
                a = jnp.where(tri, _dot_nt(qe[rows], ke[rows]), 0.0)
                intra.append(_dot(a, vb[rows]))
            s = st[hh]
            for cc in _chunk_order(rev, False):
                rows = slice(cc * CHUNK, (cc + 1) * CHUNK)
                sh_ref[hh, cc] = s
                o_ref[rows, ln] = intra[cc] + _dot_nt(qi[rows], s)
                s = s * eb[cc * CHUNK:cc * CHUNK + 1, :] + _dot_tn(vb[rows], ki[rows])
            st[hh] = s
            if fused:
                osum = o_ref[:, ln] + ofw_ref[:, ln]
                o_ref[:, ln] = osum
                mix_ref[:, ln] = _headnorm_apply(osum, _head_cols(g_refs, hh), gain_ref[...])

    hp, wd = HGRN_HP, 128 * HGRN_HP
    col = functools.partial(_hgrn_cols, bmap, n2)
    oblk = pl.BlockSpec((TM, wd), lambda h, n: (bmap(n), h))
    ins = [p] * (3 * n2) + [lb]
    specs = col(6) + col(fcol) + col(18) + [pl.BlockSpec((1, wd), lambda h, n: (0, h))]
    out_specs = [oblk, pl.BlockSpec((hp, nc, 128, 128), lambda h, n: (h, bmap(n), 0, 0))]
    out_shape = [jax.ShapeDtypeStruct((t, 512), F32), jax.ShapeDtypeStruct((4, t // CHUNK, 128, 128), F32)]
    if fused:
        ins += [ofw] + [p] * n2 + [gain]
        specs += [oblk] + col(22) + [pl.BlockSpec((1, 128), lambda h, n: (0, 0))]
        out_specs.append(oblk)
        out_shape.append(jax.ShapeDtypeStruct((t, 512), BF16))
    return _pcall(body, name=name, grid=(4 // hp, nb), in_specs=specs, out_specs=out_specs, out_shape=out_shape,
                  scratch_shapes=[pltpu.VMEM((hp, 128, 128), F32)])(*ins)


def _hgrn_bwd(p, lb, sh, do, prev, *, rev, name, head=None):
    t = p.shape[0]
    nb, nc = t // TM, TM // CHUNK
    bmap = _blk_map(nb, rev, True)
    fcol = 14 if rev else 10
    has_prev = prev is not None
    odt = BF16 if has_prev else F32
    fused = head is not None

    n2 = HGRN_HP // 2

    def body(*refs):
        refs = list(refs)
        q_refs, f_refs, v_refs = refs[:n2], refs[n2:2 * n2], refs[2 * n2:3 * n2]
        lb_ref, sh_ref = refs[3 * n2], refs[3 * n2 + 1]
        pos = 3 * n2 + 2
        if fused:
            osum_ref, g_refs, dmix_ref, gain_ref = refs[pos], refs[pos + 1:pos + 1 + n2], refs[pos + 1 + n2], refs[pos + 2 + n2]
            pos += 3 + n2
        else:
            do_ref = refs[pos]
            pos += 1
        if has_prev:
            pq_ref, pv_ref = refs[pos], refs[pos + 1]
            pos += 2
        dq_ref, df_ref, dv_ref, dlb_ref = refs[pos:pos + 4]
        pos += 4
        if fused:
            do_out, dg_ref, dgain_ref = refs[pos:pos + 3]
            pos += 3
        dst = refs[pos]

        @pl.when(pl.program_id(1) == 0)
        def _():
            dst[...] = jnp.zeros_like(dst)
            dlb_ref[...] = jnp.zeros_like(dlb_ref)

        if fused:
            @pl.when((pl.program_id(0) == 0) & (pl.program_id(1) == 0))
            def _():
                dgain_ref[...] = jnp.zeros_like(dgain_ref)

        cat = functools.partial(jnp.concatenate, axis=0)
        for hh in range(HGRN_HP):
            ln = slice(128 * hh, 128 * hh + 128)
            lbv = lb_ref[:, ln]
            qraw, fraw = _head_cols(q_refs, hh), _head_cols(f_refs, hh)
            q, k, lf, sq, sf, f = _hgrn_gates(qraw, fraw, lbv)
            tri, eq, ek, ei, eki, eb = _block_terms(lf, rev)
            qe, ke, qi, ki = q * eq, k * ek, q * ei, k * eki
            if fused:
                dov, dg, dgain = _headnorm_grad(osum_ref[:, ln], _head_cols(g_refs, hh), dmix_ref[:, ln], gain_ref[...])
                do_out[:, ln] = dov
                dg_ref[:, ln] = dg
                _acc_row(dgain_ref, 0, dgain)
            else:
                dov = do_ref[:, ln]
            qeb, keb, qib, kib, vb, dob = _bf(qe), _bf(ke), _bf(qi), _bf(ki), _bf(_head_cols(v_refs, hh)), _bf(dov)
            dv, dqe, dke, dqi = [None] * nc, [None] * nc, [None] * nc, [None] * nc
            for cc in range(nc):
                rows = slice(cc * CHUNK, (cc + 1) * CHUNK)
                a = jnp.where(tri, _dot_nt(qeb[rows], keb[rows]), 0.0)
                da = jnp.where(tri, _dot_nt(dob[rows], vb[rows]), 0.0)
                dv[cc] = _dot_tn(a, dob[rows])
                dqe[cc], dke[cc] = _dot(da, keb[rows]), _dot_tn(da, qeb[rows])
                dqi[cc] = _dot(dob[rows], sh_ref[hh, cc])
            dki, dbl = [None] * nc, [None] * nc
            ds = dst[hh]
            for cc in _chunk_order(rev, True):
                rows = slice(cc * CHUNK, (cc + 1) * CHUNK)
                ebc = eb[cc * CHUNK:cc * CHUNK + 1, :]
                dv[cc] = dv[cc] + _dot_nt(kib[rows], ds)
                dki[cc] = _dot(vb[rows], ds)
                dbl[cc] = jnp.broadcast_to(jnp.sum(dki[cc] * ki[rows], axis=0, keepdims=True)
                                           + jnp.sum(ds * sh_ref[hh, cc], axis=0, keepdims=True) * ebc, (CHUNK, 128))
                ds = ds * ebc + _dot_tn(dob[rows], qib[rows])
            dst[hh] = ds
            dqe, dke, dqi, dki, dv, dbl = cat(dqe), cat(dke), cat(dqi), cat(dki), cat(dv), cat(dbl)
            dq = dqe * eq + dqi * ei
            dk = dke * ek + dki * eki
            last = 0 if rev else CHUNK - 1
            db = dqe * qe - dke * ke + dqi * qi - dki * ki
            db = db + jnp.where((_iota(db.shape, 0) & (CHUNK - 1)) == last, dbl, 0.0)
            dlf = _chunk_cumsum(db, not rev)
            dqr = dq * (sq * (1.0 + qraw * (1.0 - sq)))
            dfv = dlf / f - dk
            dfr = dfv * (1.0 - lbv) * (sf * (1.0 - sf))
            dlb_ref[:, ln] += jnp.sum(dfv * (1.0 - sf), axis=0, keepdims=True)
            if has_prev:
                dqr = dqr + pq_ref[:, ln]
                dv = dv + pv_ref[:, ln]
            dq_ref[:, ln] = dqr.astype(odt)
            df_ref[:, ln] = dfr.astype(odt)
            dv_ref[:, ln] = dv.astype(odt)

    hp, wd = HGRN_HP, 128 * HGRN_HP
    col = functools.partial(_hgrn_cols, bmap, n2)
    oblk = pl.BlockSpec((TM, wd), lambda h, n: (bmap(n), h))
    ins = [p] * (3 * n2) + [lb, sh]
    specs = col(6) + col(fcol) + col(18) + [pl.BlockSpec((1, wd), lambda h, n: (0, h)),
                                            pl.BlockSpec((hp, nc, 128, 128), lambda h, n: (h, bmap(n), 0, 0))]
    if fused:
        osum, dmix, gain = head
        ins += [osum] + [p] * n2 + [dmix, gain]
        specs += [oblk] + col(22) + [pl.BlockSpec((TM, wd), lambda h, n: (bmap(n), 4 // hp + h)),
                                     pl.BlockSpec((1, 128), lambda h, n: (0, 0))]
    else:
        ins.append(do); specs.append(oblk)
    if has_prev:
        ins += list(prev); specs += [oblk, oblk]
    out_specs = [oblk, oblk, oblk, pl.BlockSpec((1, wd), lambda h, n: (0, h))]
    out_shape = [jax.ShapeDtypeStruct((t, 512), odt)] * 3 + [jax.ShapeDtypeStruct((1, 512), F32)]
    if fused:
        out_specs += [oblk, oblk, pl.BlockSpec((8, 128), lambda h, n: (0, 0))]
        out_shape += [jax.ShapeDtypeStruct((t, 512), F32), jax.ShapeDtypeStruct((t, 512), BF16),
                      jax.ShapeDtypeStruct((8, 128), F32)]
    return _pcall(body, name=name, grid=(4 // hp, nb), in_specs=specs, out_specs=out_specs, out_shape=out_shape,
                  scratch_shapes=[pltpu.VMEM((hp, 128, 128), F32)])(*ins)


def _rope256(x, cos, sin):
    x1, x2 = x[:, 0:128], x[:, 128:256]
    return jnp.concatenate([x1 * cos - x2 * sin, x2 * cos + x1 * sin], axis=-1)


def _rope256_t(d, cos, sin):
    d1, d2 = d[:, 0:128], d[:, 128:256]
    return jnp.concatenate([d1 * cos + d2 * sin, d2 * cos - d1 * sin], axis=-1)


RET_DK, RET_DV, RET_H = 256, 512, 4
RET_KSCALE = RET_DK ** -0.5
RCH = TM
RET_HP = 4


def _ret_terms(lg, rev):
    r, c = _iota((RCH, RCH), 0), _iota((RCH, RCH), 1)
    rel = ((c - r) if rev else (r - c)).astype(F32)
    dmat = jnp.where(rel >= 0, jnp.exp(lg[:, 0:1] * jnp.maximum(rel, 0.0)), 0.0)
    pos = _iota((RCH, 1), 0).astype(F32)
    cnt = (RCH - pos) if rev else (pos + 1.0)
    ei = jnp.exp(lg * cnt)
    eki = jnp.exp(lg * (RCH - cnt))
    eb = jnp.exp(lg * float(RCH))
    return dmat, ei, eki, eb


def _ret_fwd(p, lgt, cos, sin, *, rev, name, ofw=None):
    t = p.shape[0]
    nb, nc = t // TM, TM // RCH
    bmap = _blk_map(nb, rev, False)
    fused = ofw is not None

    def body(*refs):
        q_ref, k_ref, v_ref, lg_ref, c_ref, s_ref = refs[:6]
        if fused:
            ofw_ref, g_ref, o_ref, sh_ref, mix_ref, st = refs[6:]
        else:
            o_ref, sh_ref, st = refs[6:]

        @pl.when(pl.program_id(1) == 0)
        def _():
            st[...] = jnp.zeros_like(st)
        for hh in range(RET_HP):
            qc, vc = slice(RET_DK * hh, RET_DK * (hh + 1)), slice(RET_DV * hh, RET_DV * (hh + 1))
            dmat, ei, eki, eb = _ret_terms(lg_ref[hh], rev)
            for cc in _chunk_order(rev, False, nc):
                rows = slice(cc * RCH, (cc + 1) * RCH)
                cosv, sinv = c_ref[rows, :], s_ref[rows, :]
                q = _rope256(q_ref[rows, qc].astype(F32), cosv, sinv)
                k = _rope256(k_ref[rows, qc].astype(F32), cosv, sinv) * RET_KSCALE
                v = v_ref[rows, vc]
                s0 = st[hh]
                sh_ref[hh, cc] = s0.astype(BF16)
                a = _dot_nt(q, k) * dmat
                o = _dot(a, v) + _dot_nt(q * ei, s0)
                st[hh] = s0 * eb + _dot_tn(v, k * eki)
                if fused:
                    o = o + ofw_ref[rows, vc]
                    mix_ref[rows, vc] = _headnorm_apply(o, g_ref[rows, vc].astype(F32), None)
                o_ref[rows, vc] = o

    hp = RET_HP
    tab = pl.BlockSpec((TM, 128), lambda h, n: (bmap(n), 0))
    oblk = pl.BlockSpec((TM, hp * RET_DV), lambda h, n: (bmap(n), h))
    ins = [p, p, p, lgt, cos, sin]
    specs = [pl.BlockSpec((TM, hp * RET_DK), lambda h, n: (bmap(n), h)),
             pl.BlockSpec((TM, hp * RET_DK), lambda h, n: (bmap(n), RET_H // hp + h)),
             pl.BlockSpec((TM, hp * RET_DV), lambda h, n: (bmap(n), RET_H // hp + h)),
             pl.BlockSpec((hp, 1, RET_DK), lambda h, n: (h, 0, 0)), tab, tab]
    out_specs = [oblk, pl.BlockSpec((hp, nc, RET_DV, RET_DK), lambda h, n: (h, bmap(n), 0, 0))]
    out_shape = [jax.ShapeDtypeStruct((t, RET_H * RET_DV), F32),
                 jax.ShapeDtypeStruct((RET_H, t // RCH, RET_DV, RET_DK), BF16)]
    if fused:
        ins += [ofw, p]
        specs += [oblk, pl.BlockSpec((TM, hp * RET_DV), lambda h, n: (bmap(n), 2 * RET_H // hp + h))]
        out_specs.append(oblk)
        out_shape.append(jax.ShapeDtypeStruct((t, RET_H * RET_DV), BF16))
    return _pcall(body, name=name, grid=(RET_H // hp, nb), in_specs=specs, out_specs=out_specs, out_shape=out_shape,
                  scratch_shapes=[pltpu.VMEM((hp, RET_DV, RET_DK), F32)])(*ins)


def _ret_bwd(p, lgt, cos, sin, sh, do, prev, *, rev, name, head=None):
    t = p.shape[0]
    nb, nc = t // TM, TM // RCH
    bmap = _blk_map(nb, rev, True)
    has_prev = prev is not None
    odt = BF16 if has_prev else F32
    fused = head is not None

    def body(*refs):
        refs = list(refs)
        q_ref, k_ref, v_ref, lg_ref, c_ref, s_ref, sh_ref = refs[:7]
        if fused:
            osum_ref, g_ref, dmix_ref = refs[7:10]
            pos = 10
        else:
            do_ref = refs[7]
            pos = 8
        if has_prev:
            pq_ref, pk_ref, pv_ref = refs[pos:pos + 3]
            pos += 3
        dq_ref, dk_ref, dv_ref = refs[pos:pos + 3]
        pos += 3
        if fused:
            do_out, dg_ref = refs[pos:pos + 2]
            pos += 2
        dst = refs[pos]

        @pl.when(pl.program_id(1) == 0)
        def _():
            dst[...] = jnp.zeros_like(dst)

        for hh in range(RET_HP):
            qc, vc = slice(RET_DK * hh, RET_DK * (hh + 1)), slice(RET_DV * hh, RET_DV * (hh + 1))
            dmat, ei, eki, eb = _ret_terms(lg_ref[hh], rev)
            for cc in _chunk_order(rev, True, nc):
                rows = slice(cc * RCH, (cc + 1) * RCH)
                cosv, sinv = c_ref[rows, :], s_ref[rows, :]
                q = _rope256(q_ref[rows, qc].astype(F32), cosv, sinv)
                k = _rope256(k_ref[rows, qc].astype(F32), cosv, sinv) * RET_KSCALE
                v = v_ref[rows, vc]
                if fused:
                    dov, dg, _ = _headnorm_grad(osum_ref[rows, vc], g_ref[rows, vc].astype(F32), dmix_ref[rows, vc], None)
                    do_out[rows, vc] = dov
                    dg_ref[rows, vc] = dg
                else:
                    dov = do_ref[rows, vc]
                s0 = sh_ref[hh, cc]
                dsc = dst[hh]
                qi, ki = q * ei, k * eki
                a = _dot_nt(q, k) * dmat
                da = _dot_nt(dov, v) * dmat
                dv = _dot_tn(a, dov) + _dot_nt(ki, dsc)
                dqs = _dot(da, k) + _dot(dov, s0) * ei
                dks = _dot_tn(da, q) + _dot(v, dsc) * eki
                dst[hh] = dsc * eb + _dot_tn(dov, qi)
                dq = _rope256_t(dqs, cosv, sinv)
                dk = _rope256_t(dks * RET_KSCALE, cosv, sinv)
                if has_prev:
                    dq = dq + pq_ref[rows, qc]
                    dk = dk + pk_ref[rows, qc]
                    dv = dv + pv_ref[rows, vc]
                dq_ref[rows, qc] = dq.astype(odt)
                dk_ref[rows, qc] = dk.astype(odt)
                dv_ref[rows, vc] = dv.astype(odt)

    hp = RET_HP
    tab = pl.BlockSpec((TM, 128), lambda h, n: (bmap(n), 0))
    qblk = pl.BlockSpec((TM, hp * RET_DK), lambda h, n: (bmap(n), h))
    vblk = pl.BlockSpec((TM, hp * RET_DV), lambda h, n: (bmap(n), h))
    ins = [p, p, p, lgt, cos, sin, sh]
    specs = [qblk, pl.BlockSpec((TM, hp * RET_DK), lambda h, n: (bmap(n), RET_H // hp + h)),
             pl.BlockSpec((TM, hp * RET_DV), lambda h, n: (bmap(n), RET_H // hp + h)),
             pl.BlockSpec((hp, 1, RET_DK), lambda h, n: (h, 0, 0)), tab, tab,
             pl.BlockSpec((hp, nc, RET_DV, RET_DK), lambda h, n: (h, bmap(n), 0, 0))]
    if fused:
        osum, dmix = head
        ins += [osum, p, dmix]
        specs += [vblk, pl.BlockSpec((TM, hp * RET_DV), lambda h, n: (bmap(n), 2 * RET_H // hp + h)), vblk]
    else:
        ins.append(do); specs.append(vblk)
    if has_prev:
        ins += list(prev); specs += [qblk, qblk, vblk]
    out_specs = [qblk, qblk, vblk]
    out_shape = [jax.ShapeDtypeStruct((t, RET_H * RET_DK), odt), jax.ShapeDtypeStruct((t, RET_H * RET_DK), odt),
                 jax.ShapeDtypeStruct((t, RET_H * RET_DV), odt)]
    if fused:
        out_specs += [vblk, vblk]
        out_shape += [jax.ShapeDtypeStruct((t, RET_H * RET_DV), F32), jax.ShapeDtypeStruct((t, RET_H * RET_DV), BF16)]
    return _pcall(body, name=name, grid=(RET_H // hp, nb), in_specs=specs, out_specs=out_specs, out_shape=out_shape,
                  scratch_shapes=[pltpu.VMEM((hp, RET_DV, RET_DK), F32)])(*ins)


def _rope_tables(lc, l):
    tt = jnp.arange(l)
    row, colp = (tt // 64).astype(F32), (tt % 64).astype(F32)
    inv = 10000.0 ** (-jnp.arange(16, dtype=F32) / 16)
    ang = jnp.concatenate([row[:, None] * inv, colp[:, None] * inv], axis=-1)
    ang = jnp.concatenate([jnp.zeros((lc, 32), F32), ang], axis=0)
    acos, asin = jnp.tile(jnp.cos(ang), (1, 4)), jnp.tile(jnp.sin(ang), (1, 4))
    theta = 1.0 / (10000.0 ** jnp.linspace(0.0, 1.0, 128, dtype=F32))
    rang = jnp.arange(l, dtype=F32)[:, None] * theta
    rang = jnp.concatenate([jnp.zeros((lc, 128), F32), rang], axis=0)
    return acos, asin, jnp.cos(rang), jnp.sin(rang)


class _Weights:
    def __init__(self, w):
        self.w = w

    def first(self, after):
        return self.w

    def rest_landed(self, after):
        pass

    def rest(self, after):
        return self.w

    def send_grads(self, grp, grads):
        return jnp.zeros((8, 128), F32)


def _local_step(x0, target, mods, ng, wsrc, small):
    t, d = x0.shape
    l = target.shape[0]
    lc = t - l
    acos, asin, rcos, rsin = _rope_tables(lc, l)
    lg_fw = jnp.log(1.0 - 2.0 ** (-5.0 - jnp.arange(RET_H, dtype=F32)))
    lgt_fw = jnp.broadcast_to(lg_fw[:, None, None], (RET_H, 1, RET_DK))
    lgt_bw = jnp.broadcast_to(lg_fw[::-1][:, None, None], (RET_H, 1, RET_DK))
    gq, gk, sink, gain, lb = small['gq'], small['gk'], small['sink'], small['gain'], small['lb']

    (h1,) = _row_fwd(x0, mods, g=ng[0], shift=0, scale=1, name='l0_norm1')
    w = wsrc.first(h1)
    p0 = _mm_nn(h1, w['even_in'], name='l0_in')
    kp = _kprep_fwd(p0, gk, acos, asin, name='l0_kprep')
    att = _attn_fwd(p0, kp, gq, sink, acos, asin, lc=lc, name='l0_attn')
    hof, hsf = _hgrn_fwd(p0, lb, rev=False, name='l0_hgrn_f')
    wsrc.rest_landed(hof)
    hos, hsb, bmix = _hgrn_fwd(p0, lb, rev=True, name='l0_hgrn_b', ofw=hof, gain=gain)
    mix0 = [att, bmix]
    y0 = _mm_nn(mix0, w['even_out'], name='l0_out')
    x1, h2 = _row_fwd(x0, mods, y=y0, gate=2, g=ng[1], shift=3, scale=4, name='l0_norm2')
    w = dict(w, **wsrc.rest(h2))
    u0, a0 = _ffn_in(h2, w['ffn_in'], lead=0, name='ffn_in')
    z0 = _mm_nn(a0, w['ffn_out'], lead=0, name='ffn_out')
    x2, h3 = _row_fwd(x1, mods, y=z0, gate=5, g=ng[2], shift=12, scale=13, name='l1_norm1')
    p1 = _mm_nn(h3, w['odd_in'], out_dtype=BF16, name='l1_in')
    rof, rsf = _ret_fwd(p1, lgt_fw, rcos, rsin, rev=False, name='l1_ret_f')
    ros, rsb, mix1 = _ret_fwd(p1, lgt_bw, rcos, rsin, rev=True, name='l1_ret_b', ofw=rof)
    y1 = _mm_nn(mix1, w['odd_out'], name='l1_out')
    x3, h4 = _row_fwd(x2, mods, y=y1, gate=14, g=ng[3], shift=15, scale=16, name='l1_norm2')
    u1, a1 = _ffn_in(h4, w['ffn_in'], lead=1, name='ffn_in')
    z1 = _mm_nn(a1, w['ffn_out'], lead=1, name='ffn_out')
    loss, dx4, dz1, s_fin = _row_final(x3, z1, mods, target, gate=17, name='loss')

    du1 = _ffn_dx(dz1, w['ffn_out'], u1, lead=1, name='ffn_out_dx')
    g_ffn_out1 = _mm_tn(a1, dz1, name='ffn_out_dw')
    dh4 = _mm_nt(du1, w['ffn_in'], lead=1, name='ffn_in_dx')
    g_ffn_in1 = _mm_tn(h4, du1, name='ffn_in_dw')
    dx3, dy1, s_l1n2 = _row_bwd(x3, dx4, dh4, mods, ng[3], shift=15, scale=16, y=y1, gate=14, name='l1_norm2_bwd')
    dmix1 = _mm_nt(dy1, w['odd_out'], name='l1_out_dx')
    g_odd_out = _mm_tn(mix1, dy1, name='l1_out_dw')
    rdq, rdk, rdv, rdo, rdg = _ret_bwd(p1, lgt_fw, rcos, rsin, rsf, None, None, rev=False, name='l1_ret_f_bwd',
                                       head=(ros, dmix1))
    rdq, rdk, rdv = _ret_bwd(p1, lgt_bw, rcos, rsin, rsb, rdo, (rdq, rdk, rdv), rev=True, name='l1_ret_b_bwd')
    dp1 = [rdq, rdk, rdv, rdg]
    dh3 = _mm_nt(dp1, w['odd_in'], name='l1_in_dx')
    g_odd_in = _mm_tn(h3, dp1, name='l1_in_dw')
    mods = mods + wsrc.send_grads('early', dict(ffn_in1=g_ffn_in1, ffn_out1=g_ffn_out1, odd_in=g_odd_in,
                                                odd_out=g_odd_out))[0, 0]
    dx2, dz0, s_l1n1 = _row_bwd(x2, dx3, dh3, mods, ng[2], shift=12, scale=13, y=z0, gate=5, name='l1_norm1_bwd')
    du0 = _ffn_dx(dz0, w['ffn_out'], u0, lead=0, name='ffn_out_dx')
    g_ffn_out0 = _mm_tn(a0, dz0, name='ffn_out_dw')
    dh2 = _mm_nt(du0, w['ffn_in'], lead=0, name='ffn_in_dx')
    g_ffn_in0 = _mm_tn(h2, du0, name='ffn_in_dw')
    mods = mods + wsrc.send_grads('mid', dict(ffn_in0=g_ffn_in0, ffn_out0=g_ffn_out0))[0, 0]
    dx1, dy0, s_l0n2 = _row_bwd(x1, dx2, dh2, mods, ng[1], shift=3, scale=4, y=y0, gate=2, name='l0_norm2_bwd')
    dmix0 = _mm_nt(dy0, w['even_out'], name='l0_out_dx')
    g_even_out = _mm_tn(mix0, dy0, name='l0_out_dw')
    hq, hff, hv, dlb_f, hdo, hdg, s_gain = _hgrn_bwd(p0, lb, hsf, None, None, rev=False, name='l0_hgrn_f_bwd',
                                                     head=(hos, dmix0, gain))
    hq, hfb, hv, dlb_b = _hgrn_bwd(p0, lb, hsb, hdo, (hq, hv), rev=True, name='l0_hgrn_b_bwd')
    adq, dkp, adv, s_gq, s_sink = _attn_bwd(p0, kp, gq, sink, acos, asin, dmix0, lc=lc, name='l0_attn_bwd')
    dkv, s_gk = _kprep_bwd(p0, gk, acos, asin, dkp, adv, name='l0_kprep_bwd')
    dp0 = jnp.concatenate([adq, dkv, hq, _bf(hff), hfb, hv, hdg], axis=1)
    dh1 = _mm_nt(dp0, w['even_in'], name='l0_in_dx')
    g_even_in = _mm_tn(h1, dp0, name='l0_in_dw')
    dx0, s_l0n1 = _row_bwd(x0, dx1, dh1, mods, ng[0], shift=0, scale=1, latent_only=True, name='l0_norm1_bwd')

    grads = dict(ffn_in0=g_ffn_in0, ffn_in1=g_ffn_in1, ffn_out0=g_ffn_out0, ffn_out1=g_ffn_out1,
                 even_in=g_even_in, even_out=g_even_out, odd_in=g_odd_in, odd_out=g_odd_out)
    sums = dict(fin=s_fin, l1n2=s_l1n2, l1n1=s_l1n1, l0n2=s_l0n2, l0n1=s_l0n1, gain=s_gain, gq=s_gq, gk=s_gk,
                sink=s_sink, dlb_f=dlb_f, dlb_b=dlb_b)
    return loss, dx0, grads, sums


def _place():
    return lax.axis_index("x"), lax.axis_index("y"), lax.axis_index("c")


def _ag8(blk, *, name):
    r, c = blk.shape
    flips = [(dx, dy, dc) for dx in (0, 1) for dy in (0, 1) for dc in (0, 1) if (dx, dy, dc) != (0, 0, 0)]

    def body(x_ref, out_ref, send_sems, recv_sems, local_sem):
        ax, ay, ac = _place()
        me = 4 * ax + 2 * ay + ac
        mine = pltpu.make_async_copy(x_ref, out_ref.at[me], local_sem)
        mine.start()
        sent = []
        for k, (dx, dy, dc) in enumerate(flips):
            peer = (lax.rem(ax + dx, 2), lax.rem(ay + dy, 2), lax.rem(ac + dc, 2))
            cp = pltpu.make_async_remote_copy(src_ref=x_ref, dst_ref=out_ref.at[me], send_sem=send_sems.at[k],
                                              recv_sem=recv_sems.at[k], device_id=peer, device_id_type=MESH)
            cp.start()
            sent.append((cp, 4 * peer[0] + 2 * peer[1] + peer[2]))
        for k, (cp, pidx) in enumerate(sent):
            pltpu.make_async_remote_copy(src_ref=x_ref, dst_ref=out_ref.at[pidx], send_sem=send_sems.at[k],
                                         recv_sem=recv_sems.at[k], device_id=(ax, ay, ac),
                                         device_id_type=MESH).wait_recv()
        for cp, _ in sent:
            cp.wait_send()
        mine.wait()

    return _pcall(
        body, name=name,
        in_specs=[pl.BlockSpec(memory_space=pltpu.VMEM)],
        out_specs=pl.BlockSpec(memory_space=pltpu.VMEM),
        out_shape=jax.ShapeDtypeStruct((8, r, c), blk.dtype),
        scratch_shapes=[pltpu.SemaphoreType.DMA((7,)), pltpu.SemaphoreType.DMA((7,)), pltpu.SemaphoreType.DMA],
    )(blk)


_HBM = pl.BlockSpec(memory_space=pltpu.HBM)
_SEM = pl.BlockSpec(memory_space=pltpu.SEMAPHORE)
_DATAFLOW = pltpu.SideEffectType.DATAFLOW_SIDE_EFFECTING


def _split_start(bufs, plan, k, *, name):
    n = len(bufs)

    def body(*refs):
        ins, send_sems, recv_sems, token = refs[:n], refs[n], refs[n + 1], refs[2 * n + 2]
        for i, (src, dst, dev) in enumerate(plan(ins)):
            pltpu.make_async_remote_copy(src_ref=src, dst_ref=dst, send_sem=send_sems.at[i], recv_sem=recv_sems.at[i],
                                         device_id=dev, device_id_type=MESH).start()
        token[...] = jnp.zeros_like(token)

    res = _pcall(
        body, name=name,
        out_shape=(pltpu.SemaphoreType.DMA((k,)), pltpu.SemaphoreType.DMA((k,)),
                   *[pltpu.HBM(b.shape, b.dtype) for b in bufs], jax.ShapeDtypeStruct((8, 128), F32)),
        in_specs=[_HBM] * n, out_specs=(_SEM, _SEM, *[_HBM] * n, pl.BlockSpec(memory_space=pltpu.VMEM)),
        input_output_aliases={i: 2 + i for i in range(n)},
        compiler_params=pltpu.CompilerParams(has_side_effects=_DATAFLOW),
    )(*[pltpu.with_memory_space_constraint(b, pltpu.HBM) for b in bufs])
    return res[0], res[1], list(res[2:2 + n]), res[2 + n]


def _split_wait(bufs, send_sems, recv_sems, plan, after, *, name):
    n = len(bufs)

    def body(*refs):
        ins, ssem, rsem = refs[:n], refs[n], refs[n + 1]
        for i, (src, dst, dev) in enumerate(plan(ins)):
            cp = pltpu.make_async_remote_copy(src_ref=src, dst_ref=dst, send_sem=ssem.at[i], recv_sem=rsem.at[i],
                                              device_id=dev, device_id_type=MESH)
            cp.wait_send()
            cp.wait_recv()

    res = _pcall(
        body, name=name, out_shape=tuple(pltpu.HBM(b.shape, b.dtype) for b in bufs),
        in_specs=[_HBM] * n + [_SEM, _SEM, pl.BlockSpec(memory_space=pl.ANY)], out_specs=tuple([_HBM] * n),
        input_output_aliases={i: i for i in range(n)},
        compiler_params=pltpu.CompilerParams(has_side_effects=_DATAFLOW),
    )(*bufs, send_sems, recv_sems, after)
    return list(res)


_CHIP_FLIPS = [(1, 0), (0, 1), (1, 1)]


class _GatheredWeights:
    FIRST = ('even_in', 'even_out')
    REST = ('ffn_in', 'ffn_out', 'odd_in', 'odd_out')

    def __init__(self, shards, reducer):
        self.shards = shards
        self.send_grads = reducer.start
        self.ici = {}
        for grp, names in (('first', self.FIRST), ('rest', self.REST)):
            src = [shards[nm].reshape(2, shards[nm].shape[0] // 2, shards[nm].shape[1]) for nm in names]
            land = [lax.empty((4,) + a.shape, a.dtype) for a in src]
            m = len(names)
            sends, recvs, bufs, token = _split_start(src + land, functools.partial(self._ici_plan, m, True), 4 * m,
                                                     name='gather_' + grp + '_ici_start')
            self.ici[grp] = (sends, recvs, bufs, m)
            self.token = token if grp == 'first' else self.token + token
        self.rest_d2d = None

    @staticmethod
    def _ici_plan(m, sending, refs):
        ax, ay, ac = _place()
        s = 2 * ax + ay
        out = []
        for a in range(m):
            for dx, dy in _CHIP_FLIPS:
                px, py = lax.rem(ax + dx, 2), lax.rem(ay + dy, 2)
                slot = s if sending else 2 * px + py
                out.append((refs[a].at[ac], refs[m + a].at[slot, ac], (px, py, ac)))
        for a in range(m):
            out.append((refs[a], refs[m + a].at[s], (ax, ay, 1 - ac)))
        return out

    @staticmethod
    def _d2d_plan(m, sending, refs):
        ax, ay, ac = _place()
        out = []
        for a in range(m):
            for dx, dy in _CHIP_FLIPS:
                sp = 2 * lax.rem(ax + dx, 2) + lax.rem(ay + dy, 2)
                out.append((refs[a].at[sp, ac], refs[a].at[sp, ac if sending else 1 - ac], (ax, ay, 1 - ac)))
        return out

    def _landed(self, grp, after):
        sends, recvs, bufs, m = self.ici[grp]
        bufs = _split_wait(bufs, sends, recvs, functools.partial(self._ici_plan, m, False), after,
                           name='gather_' + grp + '_ici_wait')
        sends, recvs, land, _ = _split_start(bufs[m:], functools.partial(self._d2d_plan, m, True), 3 * m,
                                             name='gather_' + grp + '_d2d_start')
        return sends, recvs, land, m

    def _full(self, grp, names, d2d, after):
        sends, recvs, land, m = d2d
        land = _split_wait(land, sends, recvs, functools.partial(self._d2d_plan, m, False), after,
                           name='gather_' + grp + '_d2d_wait')
        return {nm: _from_shards(nm, g.reshape((4,) + self.shards[nm].shape)) for nm, g in zip(names, land)}

    def first(self, after):
        return self._full('first', self.FIRST, self._landed('first', after), after)

    def rest_landed(self, after):
        self.rest_d2d = self._landed('rest', after)

    def rest(self, after):
        return self._full('rest', self.REST, self.rest_d2d, after)


def _to_sibling(arrs, *, name):
    n = len(arrs)

    def body(*refs):
        ins, outs = refs[:n], refs[n:2 * n]
        send_sems, recv_sems = refs[2 * n:]
        ax, ay, ac = _place()
        cps = [pltpu.make_async_remote_copy(src_ref=ins[a], dst_ref=outs[a], send_sem=send_sems.at[a],
                                            recv_sem=recv_sems.at[a], device_id=(ax, ay, 1 - ac),
                                            device_id_type=MESH) for a in range(n)]
        for cp in cps:
            cp.start()
        for cp in cps:
            cp.wait_recv()
        for cp in cps:
            cp.wait_send()

    hbm = pl.BlockSpec(memory_space=pl.ANY)
    return _pcall(
        body, name=name, in_specs=[hbm] * n, out_specs=[hbm] * n,
        out_shape=[jax.ShapeDtypeStruct(a.shape, a.dtype) for a in arrs],
        scratch_shapes=[pltpu.SemaphoreType.DMA((n,))] * 2,
    )(*arrs)


def _mod_fwd(cond_raw, mw, mb, *, name):
    _, d, n = mw.shape

    def body(c_ref, w_ref, b_ref, o_ref):
        cv = c_ref[...]
        o_ref[...] = _dot(cv * _sigmoid(cv), w_ref[...]) + b_ref[...]

    return _pcall(
        body, name=name, grid=(2,),
        in_specs=[pl.BlockSpec((16, d), lambda l: (0, 0)), pl.BlockSpec((None, d, n), lambda l: (l, 0, 0)),
                  pl.BlockSpec((None, 1, n), lambda l: (l, 0, 0))],
        out_specs=pl.BlockSpec((None, 16, n), lambda l: (l, 0, 0)),
        out_shape=jax.ShapeDtypeStruct((2, 16, n), F32),
    )(cond_raw, mw, mb)


def _mod_bwd(cond_raw, dms, mw, *, name):
    _, d, n = mw.shape

    def body(c_ref, dm_ref, w_ref, gw_ref, dc_ref):
        @pl.when(pl.program_id(0) == 0)
        def _():
            dc_ref[...] = jnp.zeros_like(dc_ref)
        cv = c_ref[...]
        gw_ref[...] = _dot_tn(cv * _sigmoid(cv), dm_ref[...])
        dc_ref[...] += _dot_nt(dm_ref[...], w_ref[...])

    return _pcall(
        body, name=name, grid=(2,),
        in_specs=[pl.BlockSpec((16, d), lambda l: (0, 0)), pl.BlockSpec((None, 16, n), lambda l: (l, 0, 0)),
                  pl.BlockSpec((None, d, n), lambda l: (l, 0, 0))],
        out_specs=[pl.BlockSpec((None, d, n), lambda l: (l, 0, 0)), pl.BlockSpec((16, d), lambda l: (0, 0))],
        out_shape=[jax.ShapeDtypeStruct((2, d, n), F32), jax.ShapeDtypeStruct((16, d), F32)],
    )(cond_raw, dms, mw)


def _lb_fwd(hgrn_lb, *, name):
    def body(a_ref, o_ref):
        a0, a1 = a_ref[0:1, :], a_ref[1:2, :]
        m = jnp.maximum(a0, a1)
        e0, e1 = jnp.exp(a0 - m), jnp.exp(a1 - m)
        o_ref[...] = e0 / (e0 + e1)

    return _pcall(body, name=name, out_shape=jax.ShapeDtypeStruct((1, hgrn_lb.shape[1]), F32))(hgrn_lb)


PACK_TILES = ('l0n1', 'l0n2', 'l1n1', 'l1n2', 'fin', 'gq', 'gk', 'gain', 'dlb_f', 'dlb_b', 'sink')
PACK_ROW = {nm: 8 * i for i, nm in enumerate(PACK_TILES)}
MOD_SOURCE = ((('l0n1', 0), ('l0n1', 1), ('l0n2', 2), ('l0n2', 0), ('l0n2', 1), ('l1n1', 2)),
              (('l1n1', 0), ('l1n1', 1), ('l1n2', 2), ('l1n2', 0), ('l1n2', 1), ('fin', 2)))


def _small_finalize(gath, lb_pad, *, name):
    d = gath.shape[2]

    def body(g_ref, lb_ref, small_ref, glb_ref, gmb_ref, dm_ref):
        tot = g_ref[0]
        for e in range(1, 8):
            tot = tot + g_ref[e]

        def row(nm, r=0):
            return tot[PACK_ROW[nm] + r:PACK_ROW[nm] + r + 1, :]

        for k, nm in enumerate(('l0n1', 'l0n2', 'l1n1', 'l1n2')):
            small_ref[k:k + 1, :] = row(nm, 3) + row(nm, 7)
        for k, nm in ((4, 'gq'), (5, 'gk')):
            small_ref[k:k + 1, :] = row(nm) + pltpu.roll(row(nm), d - 64, 1)
        small_ref[6:7, :] = row('gain')
        small_ref[7:8, :] = row('sink')
        lbv = lb_ref[...]
        g0 = (row('dlb_f') + row('dlb_b')) * lbv * (1.0 - lbv)
        glb_ref[...] = jnp.zeros_like(glb_ref)
        glb_ref[0:1, :] = g0
        glb_ref[1:2, :] = -g0
        dm_ref[...] = jnp.zeros_like(dm_ref)
        for l in range(2):
            for part in range(6):
                nm, r = MOD_SOURCE[l][part]
                gmb_ref[l * 6 + part:l * 6 + part + 1, :] = row(nm, r) + row(nm, r + 4)
                rl = PACK_ROW[nm] + r + 4
                for e in range(8):
                    dm_ref[l, part, e:e + 1, :] = g_ref[e, rl:rl + 1, :]
                dm_ref[l, part, 8:9, :] = row(nm, r)

    return _pcall(
        body, name=name,
        out_shape=[jax.ShapeDtypeStruct((8, d), F32), jax.ShapeDtypeStruct((8, d), F32),
                   jax.ShapeDtypeStruct((12, d), F32), jax.ShapeDtypeStruct((2, 6, 16, d), F32)],
    )(gath, lb_pad)


def _cctx_grad(gath, c_ctx2, *, name):
    def body(g_ref, c_ref, o_ref):
        tot = ((g_ref[0, 0:1, :] + g_ref[2, 0:1, :]) + g_ref[4, 0:1, :]) + g_ref[6, 0:1, :]
        cv = c_ref[...]
        s = _sigmoid(cv)
        o_ref[...] = tot * (s * (1.0 + cv * (1.0 - s)))

    return _pcall(body, name=name, out_shape=jax.ShapeDtypeStruct(c_ctx2.shape, F32))(gath, c_ctx2)


def _row_block(r, c, limit=256 * 1024):
    best = None
    for br in range(16, r + 1, 16):
        if r % br == 0 and br * c <= limit:
            best = br
    return best if best is not None else r


def _sum4(own, landed, core, *, name):
    _, r, c = own.shape
    br = _row_block(r, c, 512 * 1024)

    def body(core_ref, own_ref, land_ref, o_ref):
        s = 2 * lax.axis_index("x") + lax.axis_index("y")
        p = [jnp.where(s == k, own_ref[k], land_ref[k]).astype(F32) for k in range(4)]
        o_ref[...] = ((p[0] + p[1]) + p[2]) + p[3]

    blk = pl.BlockSpec((4, br, c), lambda i, core_ref: (0, i, 0))
    spec = pltpu.PrefetchScalarGridSpec(
        num_scalar_prefetch=1, grid=(r // br,), in_specs=[blk, blk],
        out_specs=pl.BlockSpec((None, br, c), lambda i, core_ref: (core_ref[0], i, 0)))
    return _pcall(body, name=name, grid_spec=spec, out_shape=jax.ShapeDtypeStruct((2, r, c), F32))(core, own, landed)


def _exchange_halves(arrs, *, name):
    n = len(arrs)

    def body(*refs):
        ins, outs = refs[:n], refs[n:2 * n]
        send_sems, recv_sems = refs[2 * n:]
        ax, ay, ac = _place()
        cps = [pltpu.make_async_remote_copy(src_ref=ins[a].at[ac], dst_ref=outs[a].at[ac], send_sem=send_sems.at[a],
                                            recv_sem=recv_sems.at[a], device_id=(ax, ay, 1 - ac),
                                            device_id_type=MESH) for a in range(n)]
        for cp in cps:
            cp.start()
        for a in range(n):
            pltpu.make_async_remote_copy(src_ref=ins[a].at[ac], dst_ref=outs[a].at[1 - ac], send_sem=send_sems.at[a],
                                         recv_sem=recv_sems.at[a], device_id=(ax, ay, ac),
                                         device_id_type=MESH).wait_recv()
        for cp in cps:
            cp.wait_send()

    hbm = pl.BlockSpec(memory_space=pl.ANY)
    return _pcall(
        body, name=name, in_specs=[hbm] * n, out_specs=[hbm] * n,
        out_shape=[jax.ShapeDtypeStruct(a.shape, a.dtype) for a in arrs],
        input_output_aliases={a: a for a in range(n)},
        scratch_shapes=[pltpu.SemaphoreType.DMA((n,))] * 2,
    )(*arrs)


def _add2(a, b, *, name):
    r, c = a.shape
    br = _row_block(r, c, 1024 * 1024)

    def body(a_ref, b_ref, o_ref):
        o_ref[...] = (a_ref[...].astype(F32) + b_ref[...].astype(F32)).astype(BF16)

    blk = pl.BlockSpec((br, c), lambda i: (i, 0))
    return _pcall(body, name=name, grid=(r // br,), in_specs=[blk, blk], out_specs=blk,
                  out_shape=jax.ShapeDtypeStruct((r, c), BF16))(a, b)


def _adam(w, gs, m, v, *, name):
    r, c = w.shape
    br = _row_block(r, c)
    ng = len(gs)
    c1 = 1.0 - ADAM_B1 ** ADAM_STEP
    c2 = 1.0 - ADAM_B2 ** ADAM_STEP

    def body(*refs):
        w_ref, m_ref, v_ref = refs[0], refs[1 + ng], refs[2 + ng]
        outs = refs[3 + ng:]
        g = refs[1][...]
        for k in range(1, ng):
            g = g + refs[1 + k][...]
        mn = ADAM_B1 * m_ref[...] + (1.0 - ADAM_B1) * g
        vn = ADAM_B2 * v_ref[...] + (1.0 - ADAM_B2) * (g * g)
        if ng > 1:
            outs[0][...] = g
        d_out, m_out, v_out = outs[-3:]
        m_out[...] = mn
        v_out[...] = vn
        d_out[...] = -ADAM_LR * ((mn / c1) / (jnp.sqrt(vn / c2) + ADAM_EPS) + ADAM_WD * w_ref[...])

    blk = pl.BlockSpec((br, c), lambda i: (i, 0))
    nout = 4 if ng > 1 else 3
    res = _pcall(body, name=name, grid=(r // br,), in_specs=[blk] * (3 + ng), out_specs=[blk] * nout,
                 out_shape=[jax.ShapeDtypeStruct((r, c), F32)] * nout)(w, *gs, m, v)
    return list(res) if ng > 1 else [gs[0]] + list(res)


def _grad_halves(name, g, ac):
    if name.endswith('_in'):
        n = g.shape[1] // 4
        if name == 'ffn_in':
            assert n == FFN_BK
        order = _ffn_order(g.shape[1]) if name == 'ffn_in' else range(4)
        v = jnp.stack([g[:, b * n:(b + 1) * n] for b in order])
        per = [v[:, :g.shape[0] // 2], v[:, g.shape[0] // 2:]]
    else:
        k4, n = g.shape
        v = g.reshape(4, 2, k4 // 8, n)
        per = [v[:, 0], v[:, 1]]
    first = ac == 0
    return _bf(jnp.where(first, per[0], per[1])), _bf(jnp.where(first, per[1], per[0]))


class _GradReducer:
    def __init__(self):
        self.flight = {}

    @staticmethod
    def _plan(m, sending, refs):
        ax, ay, ac = _place()
        s = 2 * ax + ay
        out = []
        for a in range(m):
            for dx, dy in _CHIP_FLIPS:
                px, py = lax.rem(ax + dx, 2), lax.rem(ay + dy, 2)
                sp = 2 * px + py
                out.append((refs[a].at[sp], refs[m + a].at[s if sending else sp], (px, py, ac)))
        return out

    def start(self, grp, grads):
        ac = lax.axis_index("c")
        names = list(grads)
        halves = [_grad_halves(nm.rstrip('01'), grads[nm], ac) for nm in names]
        theirs = _to_sibling([h[1] for h in halves], name='swap_core_halves_' + grp)
        pair = [_add2(h[0].reshape(-1, b.shape[-1]), b.reshape(-1, b.shape[-1]), name='add_cores').reshape(b.shape)
                for h, b in zip(halves, theirs)]
        m = len(names)
        land = [lax.empty(a.shape, a.dtype) for a in pair]
        sends, recvs, bufs, token = _split_start(pair + land, functools.partial(self._plan, m, True), 3 * m,
                                                 name='scatter_' + grp + '_start')
        self.flight[grp] = (names, sends, recvs, bufs)
        return token

    def finish(self, grp, after):
        names, sends, recvs, bufs = self.flight.pop(grp)
        m = len(names)
        bufs = _split_wait(bufs, sends, recvs, functools.partial(self._plan, m, False), after,
                           name='scatter_' + grp + '_wait')
        core = lax.axis_index("c").astype(jnp.int32).reshape(1)
        sums = [_sum4(p, l, core, name='sum_chips') for p, l in zip(bufs[:m], bufs[m:])]
        both = _exchange_halves(sums, name='gather_core_halves_' + grp)
        return {nm: g.reshape(-1, g.shape[-1]) for nm, g in zip(names, both)}


def _from_shards(name, g):
    _, r, n = g.shape
    if name == 'ffn_in':
        assert n == FFN_BK
        v = g.reshape(4, 2, r // 2, n)
        return jnp.concatenate([v[b] for b in _ffn_order(4 * n)], axis=-1)
    if name == 'ffn_out':
        return g.reshape(4, 2, r // 2, n).transpose(1, 0, 2, 3).reshape(2, 2 * r, n)
    if name in ('even_in', 'odd_in'):
        return jnp.concatenate([g[b] for b in range(4)], axis=-1)
    return g.reshape(4 * r, n)


def kernel(x, c, ctx, c_ctx, mod_w, mod_b, norm_g, ffn_w_in, ffn_w_out, even_w_in, even_w_out, attn_qk_norm_g, attn_sink, hgrn_out_norm_g, hgrn_lb, odd_w_in, odd_w_out, loss_target, m_c_ctx, m_mod_w, m_mod_b, m_norm_g, m_ffn_w_in, m_ffn_w_out, m_even_w_in, m_even_w_out, m_attn_qk_norm_g, m_attn_sink, m_hgrn_out_norm_g, m_hgrn_lb, m_odd_w_in, m_odd_w_out, v_c_ctx, v_mod_w, v_mod_b, v_norm_g, v_ffn_w_in, v_ffn_w_out, v_even_w_in, v_even_w_out, v_attn_qk_norm_g, v_attn_sink, v_hgrn_out_norm_g, v_hgrn_lb, v_odd_w_in, v_odd_w_out):
    d = x.shape[-1]
    lc = ctx.shape[1]
    assert lc == TM and d == 1024
    ax, ay, ac = _place()
    s = 2 * ax + ay
    me = 4 * ax + 2 * ay + ac
    nmod = mod_w.shape[2]

    def pad8(v):
        return jnp.pad(v, ((0, 8 - v.shape[0]), (0, 0)))

    pack = jnp.concatenate([pad8(c), pad8(norm_g.reshape(1, d))], axis=0)
    g1 = _ag8(pack, name='gather_cond')
    c_all = g1[:, 0, :]
    ng = g1[0::2, 8, :].reshape(4, 2, 2, d // 4).transpose(1, 2, 0, 3).reshape(4, d)

    cond_raw = jnp.concatenate([c_all, pad8(c_ctx.reshape(1, d))], axis=0)
    mb_sh = lax.dynamic_slice_in_dim(mod_b, s * nmod, nmod, axis=1).reshape(2, 1, nmod)
    mpart = _mod_fwd(cond_raw, mod_w, mb_sh, name='mod_fwd')
    g3 = _ag8(mpart.reshape(32, nmod), name='gather_mods')
    mods_full = g3[0::2].reshape(4, 2, 16, nmod).transpose(1, 2, 0, 3).reshape(2, 16, 4 * nmod)
    m_lat = lax.dynamic_index_in_dim(mods_full, me, axis=1, keepdims=False)
    mods = jnp.stack([mods_full[:, 8], m_lat], axis=1).reshape(24, d)

    names = ['ffn_in', 'ffn_out', 'even_in', 'even_out', 'odd_in', 'odd_out']
    shards = [_bf(v.reshape(-1, v.shape[-1])) for v in (ffn_w_in, ffn_w_out, even_w_in, even_w_out, odd_w_in, odd_w_out)]
    shards, mods = lax.optimization_barrier((shards, mods))
    reducer = _GradReducer()
    wsrc = _GatheredWeights(dict(zip(names, shards)), reducer)

    lb = _lb_fwd(hgrn_lb, name='hgrn_lower_bound')
    small = dict(gq=jnp.tile(attn_qk_norm_g[0, 0], 2).reshape(1, 128), gk=jnp.tile(attn_qk_norm_g[0, 1], 2).reshape(1, 128),
                 sink=attn_sink[0], gain=hgrn_out_norm_g, lb=lb)
    x0 = jnp.concatenate([ctx[0], x[0]], axis=0)
    mods = mods + wsrc.token[0, 0]
    loss_t, dx0, grads, sums = _local_step(x0, loss_target[0], mods, ng, wsrc, small)
    loss = lax.psum(loss_t[0, 0], ("x", "y", "c"))
    grad_x = dx0[None]

    def tile(v):
        return jnp.pad(v, ((0, 8 - v.shape[0]), (0, d - v.shape[1])))

    sums = dict(sums, sink=sums['sink'][:, 0].reshape(1, 8))
    g4 = _ag8(jnp.concatenate([tile(sums[nm]) for nm in PACK_TILES], axis=0), name='gather_row_sums')
    small_g, glb, gmb, dmat = _small_finalize(g4, tile(lb)[0:1], name='small_grads')
    dms = lax.dynamic_slice_in_dim(dmat.transpose(0, 2, 1, 3).reshape(2, 16, 6 * d), s * nmod, nmod, axis=2)
    g_mod_w, dcond = _mod_bwd(cond_raw, dms, mod_w, name='mod_bwd')
    g5 = _ag8(dcond[8:16], name='gather_dcond')
    g_c_ctx = _cctx_grad(g5, c_ctx.reshape(8, d // 8).reshape(1, d), name='c_ctx_grad')

    late = {nm: grads[nm] for nm in ('even_in', 'even_out')}
    late, g_c_ctx = lax.optimization_barrier((late, g_c_ctx))
    token = reducer.start('late', late)
    full = reducer.finish('early', token)

    def upd(wv, gs, mv, vv, name):
        shp = wv.shape
        c2 = shp[-1]
        out = _adam(wv.reshape(-1, c2), [g.reshape(-1, c2) for g in gs], mv.reshape(-1, c2), vv.reshape(-1, c2), name=name)
        return [o.reshape(shp) for o in out]

    res = {}
    res['c_ctx'] = upd(c_ctx.reshape(8, d // 8), [g_c_ctx.reshape(8, d // 8)], m_c_ctx.reshape(8, d // 8), v_c_ctx.reshape(8, d // 8), 'adam_c_ctx')
    res['c_ctx'] = [o.reshape(d) for o in res['c_ctx']]
    res['mod_w'] = upd(mod_w, [g_mod_w], m_mod_w, v_mod_w, 'adam_mod_w')
    res['mod_b'] = upd(mod_b, [gmb.reshape(2, 6 * d)], m_mod_b, v_mod_b, 'adam_mod_b')
    g_ng = lax.dynamic_slice_in_dim(small_g[0:4].reshape(2, 2, d), s * (d // 4), d // 4, axis=2)
    res['norm_g'] = upd(norm_g, [g_ng], m_norm_g, v_norm_g, 'adam_norm_g')
    g_qk = jnp.stack([small_g[4, 0:64], small_g[5, 0:64]]).reshape(1, 2, 64)
    res['attn_qk_norm_g'] = upd(attn_qk_norm_g, [g_qk], m_attn_qk_norm_g, v_attn_qk_norm_g, 'adam_qk_gain')
    res['attn_sink'] = upd(attn_sink, [small_g[7, 0:8].reshape(1, 8)], m_attn_sink, v_attn_sink, 'adam_sink')
    res['hgrn_out_norm_g'] = upd(hgrn_out_norm_g, [small_g[6, 0:128].reshape(1, 128)], m_hgrn_out_norm_g, v_hgrn_out_norm_g, 'adam_head_gain')
    res['hgrn_lb'] = upd(hgrn_lb, [glb[0:2, 0:hgrn_lb.shape[1]]], m_hgrn_lb, v_hgrn_lb, 'adam_hgrn_lb')
    res['odd_w_in'] = upd(odd_w_in, [full['odd_in']], m_odd_w_in, v_odd_w_in, 'adam_odd_in')
    res['odd_w_out'] = upd(odd_w_out, [full['odd_out']], m_odd_w_out, v_odd_w_out, 'adam_odd_out')
    full.update(reducer.finish('mid', res['odd_w_in'][1]))
    g_ffn_in = jnp.concatenate([full['ffn_in0'], full['ffn_in1']], axis=0)
    g_ffn_out = jnp.concatenate([full['ffn_out0'], full['ffn_out1']], axis=0)
    res['ffn_w_in'] = upd(ffn_w_in, [g_ffn_in], m_ffn_w_in, v_ffn_w_in, 'adam_ffn_in')
    res['ffn_w_out'] = upd(ffn_w_out, [g_ffn_out], m_ffn_w_out, v_ffn_w_out, 'adam_ffn_out')
    full.update(reducer.finish('late', res['ffn_w_in'][1]))
    res['even_w_in'] = upd(even_w_in, [full['even_in']], m_even_w_in, v_even_w_in, 'adam_even_in')
    res['even_w_out'] = upd(even_w_out, [full['even_out']], m_even_w_out, v_even_w_out, 'adam_even_out')

    order = ['c_ctx', 'mod_w', 'mod_b', 'norm_g', 'ffn_w_in', 'ffn_w_out', 'even_w_in', 'even_w_out',
             'attn_qk_norm_g', 'attn_sink', 'hgrn_out_norm_g', 'hgrn_lb', 'odd_w_in', 'odd_w_out']
    outs = [loss, grad_x]
    for k in range(4):
        outs += [res[nm][k] for nm in order]
    return tuple(outs)
```

```python
import functools
import math

import numpy as np
import jax
import jax.numpy as jnp
from jax import lax
from jax.experimental import pallas as pl
from jax.experimental.pallas import tpu as pltpu

F32 = jnp.float32
BF16 = jnp.bfloat16
EPS = 1e-6
TM = 256
CHUNK = 64
QB = 256
WINDOW = 128
NEG = -1e30
MESH = pl.DeviceIdType.MESH

ADAM_LR, ADAM_B1, ADAM_B2, ADAM_EPS, ADAM_WD, ADAM_STEP = 0.001, 0.9, 0.999, 1e-08, 0.01, 10


def _pcall(body, **kw):
    return pl.pallas_call(body, **kw)


def _pick(n, cap):
    best = None
    for m in range(128, min(n, cap) + 1, 128):
        if n % m == 0:
            best = m
    assert best is not None, (n, cap)
    return best


def _bf(x):
    return x.astype(BF16)


def _dot(a, b):
    return jnp.dot(_bf(a), _bf(b), preferred_element_type=F32)


def _dot_nt(a, b):
    return lax.dot_general(_bf(a), _bf(b), (((1,), (1,)), ((), ())), preferred_element_type=F32)


def _dot_tn(a, b):
    return lax.dot_general(_bf(a), _bf(b), (((0,), (0,)), ((), ())), preferred_element_type=F32)


def _dot_exact(a, b):
    return jnp.dot(a, b, preferred_element_type=F32, precision=lax.Precision.HIGHEST)


def _sigmoid(x):
    return 1.0 / (1.0 + jnp.exp(-x))


def _iota(shape, dim):
    return lax.broadcasted_iota(jnp.int32, shape, dim)


def _parts(a):
    parts = list(a) if isinstance(a, (list, tuple)) else [a]
    widths = [p.shape[1] for p in parts]
    return parts, widths, [sum(widths[:i]) for i in range(len(parts))]


def _mm_nn(a, b, *, lead=None, out_dtype=F32, name):
    parts, widths, offs = _parts(a)
    m, k = parts[0].shape[0], sum(widths)
    n = b.shape[-1]
    bm = 1408 if (m % 1408 == 0 and k <= 1024) else (768 if m % 768 == 0 else TM)
    bn = _pick(n, 1024) if n % 512 == 0 else _pick(n, 1664)

    def body(*refs):
        b_ref, o_ref = refs[-2], refs[-1]
        acc = None
        for p_ref, w, off in zip(refs, widths, offs):
            term = _dot(p_ref[...], b_ref[off:off + w, :])
            acc = term if acc is None else acc + term
        o_ref[...] = acc.astype(o_ref.dtype)

    if lead is None:
        b_spec = pl.BlockSpec((k, bn), lambda i, j: (0, j))
    else:
        b_spec = pl.BlockSpec((None, k, bn), lambda i, j: (lead, 0, j))
    return _pcall(
        body, name=name, grid=(m // bm, n // bn),
        in_specs=[pl.BlockSpec((bm, w), lambda i, j: (i, 0)) for w in widths] + [b_spec],
        out_specs=pl.BlockSpec((bm, bn), lambda i, j: (i, j)),
        out_shape=jax.ShapeDtypeStruct((m, n), out_dtype),
    )(*parts, b)


def _mm_nt(a, b, *, lead=None, name):
    parts, widths, offs = _parts(a)
    m, n = parts[0].shape[0], sum(widths)
    k = b.shape[-2]
    bm = 1408 if (m % 1408 == 0 and n <= 1024) else (768 if m % 768 == 0 else TM)
    bk = _pick(k, 1024 if n <= 2048 else 512)

    def body(*refs):
        b_ref, o_ref = refs[-2], refs[-1]
        acc = None
        for p_ref, w, off in zip(refs, widths, offs):
            term = _dot_nt(p_ref[...], b_ref[:, off:off + w])
            acc = term if acc is None else acc + term
        o_ref[...] = acc

    if lead is None:
        b_spec = pl.BlockSpec((bk, n), lambda i, j: (j, 0))
    else:
        b_spec = pl.BlockSpec((None, bk, n), lambda i, j: (lead, j, 0))
    return _pcall(
        body, name=name, grid=(m // bm, k // bk),
        in_specs=[pl.BlockSpec((bm, w), lambda i, j: (i, 0)) for w in widths] + [b_spec],
        out_specs=pl.BlockSpec((bm, bk), lambda i, j: (i, j)),
        out_shape=jax.ShapeDtypeStruct((m, k), F32),
    )(*parts, b)


def _mm_tn(a, b, *, name):
    a_parts, a_w, a_off = _parts(a)
    b_parts, b_w, b_off = _parts(b)
    t, k, n = a_parts[0].shape[0], sum(a_w), sum(b_w)
    bt = 1408 if t % 1408 == 0 else (768 if t % 768 == 0 else TM)
    bk = _pick(k, 1536) if len(a_parts) == 1 else math.gcd(*a_w)
    if len(b_parts) == 1:
        bn = _pick(n, 1024) if n % 1024 == 0 or n < 1664 else _pick(n, 1664)
    else:
        bn = math.gcd(*b_w)
    na, nbp = len(a_parts), len(b_parts)

    def block_range(off, w, blk):
        return off // blk, w // blk

    def body(*refs):
        a_refs, b_refs, o_ref = refs[:na], refs[na:na + nbp], refs[-1]
        i, j = pl.program_id(0), pl.program_id(1)

        @pl.when(pl.program_id(2) == 0)
        def _():
            o_ref[...] = jnp.zeros_like(o_ref)

        def add(a_ref, b_ref):
            o_ref[...] += _dot_tn(a_ref[...], b_ref[...])

        for pa in range(na):
            sa, ca = block_range(a_off[pa], a_w[pa], bk)
            for pb in range(nbp):
                sb, cb = block_range(b_off[pb], b_w[pb], bn)
                if na == 1 and nbp == 1:
                    add(a_refs[0], b_refs[0])
                else:
                    pl.when((i >= sa) & (i < sa + ca) & (j >= sb) & (j < sb + cb))(
                        functools.partial(add, a_refs[pa], b_refs[pb]))

    def spec(off, w, blk, axis):
        s0, cnt = block_range(off, w, blk)

        def index(i, j, s):
            g = i if axis == 0 else j
            inside = (g >= s0) & (g < s0 + cnt)
            return (jnp.where(inside, s, 0), jnp.clip(g - s0, 0, cnt - 1))

        return pl.BlockSpec((bt, blk), index)

    return _pcall(
        body, name=name, grid=(k // bk, n // bn, t // bt),
        in_specs=[spec(o, w, bk, 0) for o, w in zip(a_off, a_w)] + [spec(o, w, bn, 1) for o, w in zip(b_off, b_w)],
        out_specs=pl.BlockSpec((bk, bn), lambda i, j, s: (i, j)),
        out_shape=jax.ShapeDtypeStruct((k, n), F32),
    )(*a_parts, *b_parts)


def _mod_row(mods_ref, lat, idx):
    return jnp.where(lat, mods_ref[idx + 6:idx + 7, :], mods_ref[idx:idx + 1, :])


def _row_step(t):
    return 768 if t % 768 == 0 else TM


def _row_fwd(x, mods, *, y=None, gate=None, g=None, shift=None, scale=None, name):
    t, d = x.shape
    has_y, has_n = y is not None, g is not None
    rt = _row_step(t)

    def body(*refs):
        refs = list(refs)
        x_ref, mods_ref = refs[0], refs[1]
        pos = 2
        if has_y:
            y_ref = refs[pos]; pos += 1
        if has_n:
            g_ref = refs[pos]; pos += 1
        outs = refs[pos:]
        for sub in range(rt // TM):
            rows = slice(sub * TM, (sub + 1) * TM)
            lat = pl.program_id(0) * (rt // TM) + sub > 0
            x1 = x_ref[rows, :]
            o = 0
            if has_y:
                x1 = x1 + _mod_row(mods_ref, lat, gate) * y_ref[rows, :]
                outs[o][rows, :] = x1; o += 1
            if has_n:
                rs = lax.rsqrt(jnp.mean(x1 * x1, axis=-1, keepdims=True) + EPS)
                hn = x1 * rs * g_ref[...]
                h = hn * (1.0 + _mod_row(mods_ref, lat, scale)) + _mod_row(mods_ref, lat, shift)
                outs[o][rows, :] = h.astype(BF16)

    row = pl.BlockSpec((rt, d), lambda i: (i, 0))
    ins, specs = [x, mods], [row, pl.BlockSpec(mods.shape, lambda i: (0, 0))]
    if has_y:
        ins.append(y); specs.append(row)
    if has_n:
        ins.append(g.reshape(1, d)); specs.append(pl.BlockSpec((1, d), lambda i: (0, 0)))
    out_shape, out_specs = [], []
    if has_y:
        out_shape.append(jax.ShapeDtypeStruct((t, d), F32)); out_specs.append(row)
    if has_n:
        out_shape.append(jax.ShapeDtypeStruct((t, d), BF16)); out_specs.append(row)
    res = _pcall(body, name=name, grid=(t // rt,), in_specs=specs, out_specs=out_specs,
                 out_shape=out_shape)(*ins)
    return res


def _acc_row(ref, r, val):
    ref[r:r + 1, :] += val


def _row_final(x, z, mods, target, *, gate, name):
    t, d = x.shape

    def body(x_ref, mods_ref, z_ref, t_ref, loss_ref, dx_ref, dz_ref, sums_ref):
        i = pl.program_id(0)
        lat = i > 0

        @pl.when(i == 0)
        def _():
            loss_ref[...] = jnp.zeros_like(loss_ref)
            sums_ref[...] = jnp.zeros_like(sums_ref)

        gt = _mod_row(mods_ref, lat, gate)
        zz = z_ref[...]
        yv = x_ref[...] + gt * zz
        keep = jnp.where(lat, 1.0, 0.0).astype(F32)
        diff = (yv - t_ref[...]) * keep
        part = jnp.sum(jnp.sum(diff * diff, axis=0, keepdims=True), axis=1, keepdims=True)
        loss_ref[...] += part * (0.5 / d)
        dy = diff * (1.0 / d)
        dx_ref[...] = dy
        dz_ref[...] = (gt * dy).astype(BF16)
        _acc_row(sums_ref, 6, jnp.sum(dy * zz, axis=0, keepdims=True))

    row = pl.BlockSpec((TM, d), lambda i: (i, 0))
    return _pcall(
        body, name=name, grid=(t // TM,),
        in_specs=[row, pl.BlockSpec(mods.shape, lambda i: (0, 0)), row,
                  pl.BlockSpec((TM, d), lambda i: (jnp.maximum(i - 1, 0), 0))],
        out_specs=[pl.BlockSpec((8, 128), lambda i: (0, 0)), row, row,
                   pl.BlockSpec((8, d), lambda i: (0, 0))],
        out_shape=[jax.ShapeDtypeStruct((8, 128), F32), jax.ShapeDtypeStruct((t, d), F32),
                   jax.ShapeDtypeStruct((t, d), BF16), jax.ShapeDtypeStruct((8, d), F32)],
    )(x, mods, z, target)


def _row_bwd(xn, dxo, dh, mods, g, *, shift, scale, y=None, gate=None, latent_only=False, name):
    t, d = xn.shape
    has_y = y is not None

    def body(*refs):
        refs = list(refs)
        x_ref, dxo_ref, dh_ref, mods_ref, g_ref = refs[:5]
        pos = 5
        if has_y:
            y_ref = refs[pos]; pos += 1
        dx_ref = refs[pos]; pos += 1
        if has_y:
            dy_ref = refs[pos]; pos += 1
        sums_ref = refs[pos]
        i = pl.program_id(0)

        @pl.when(i == 0)
        def _():
            sums_ref[...] = jnp.zeros_like(sums_ref)

        def add_sums(vals, base):
            for r, v in enumerate(vals):
                if v is not None:
                    _acc_row(sums_ref, base + r, v)

        gv = g_ref[...]
        for sub in range(rt // TM):
            rows = slice(sub * TM, (sub + 1) * TM)
            lat = i * (rt // TM) + sub > 0
            x1 = x_ref[rows, :]
            rs = lax.rsqrt(jnp.mean(x1 * x1, axis=-1, keepdims=True) + EPS)
            xh = x1 * rs
            dhv = dh_ref[rows, :]
            dn = dhv * (1.0 + _mod_row(mods_ref, lat, scale))
            dxh = dn * gv
            dx = dxo_ref[rows, :] + rs * (dxh - xh * jnp.mean(dxh * xh, axis=-1, keepdims=True))
            dx_ref[rows, :] = dx
            vals = [jnp.sum(dhv, axis=0, keepdims=True),
                    jnp.sum(dhv * (xh * gv), axis=0, keepdims=True),
                    None,
                    jnp.sum(dn * xh, axis=0, keepdims=True)]
            if has_y:
                dy_ref[rows, :] = (_mod_row(mods_ref, lat, gate) * dx).astype(BF16)
                vals[2] = jnp.sum(dx * y_ref[rows, :], axis=0, keepdims=True)
            if sub == 0:
                pl.when(i == 0)(functools.partial(add_sums, vals, 0))
                pl.when(i > 0)(functools.partial(add_sums, vals, 4))
            else:
                add_sums(vals, 4)

    rt = TM if latent_only else _row_step(t)
    row = pl.BlockSpec((rt, d), lambda i: (i, 0))
    ins = [xn, dxo, dh, mods, g.reshape(1, d)]
    specs = [row, row, row, pl.BlockSpec(mods.shape, lambda i: (0, 0)), pl.BlockSpec((1, d), lambda i: (0, 0))]
    if latent_only:
        out_shape = [jax.ShapeDtypeStruct((t - TM, d), F32)]
        out_specs = [pl.BlockSpec((TM, d), lambda i: (jnp.maximum(i - 1, 0), 0))]
    else:
        out_shape, out_specs = [jax.ShapeDtypeStruct((t, d), F32)], [row]
    if has_y:
        ins.append(y); specs.append(row)
        out_shape.append(jax.ShapeDtypeStruct((t, d), BF16)); out_specs.append(row)
    out_shape.append(jax.ShapeDtypeStruct((8, d), F32))
    out_specs.append(pl.BlockSpec((8, d), lambda i: (0, 0)))
    return _pcall(body, name=name, grid=(t // rt,), in_specs=specs, out_specs=out_specs,
                  out_shape=out_shape)(*ins)


FFN_BK = 1408


FFN_SUB = 256


def _ffn_order(n2):
    nb = n2 // (2 * FFN_BK)
    return [h * nb + j for j in range(nb) for h in (0, 1)]


def _ffn_interleave(w):
    return jnp.concatenate([w[..., b * FFN_BK:(b + 1) * FFN_BK] for b in _ffn_order(w.shape[-1])], axis=-1)


def _ffn_deinterleave(w):
    order = _ffn_order(w.shape[-1])
    return jnp.concatenate([w[..., order.index(b) * FFN_BK:(order.index(b) + 1) * FFN_BK]
                            for b in range(len(order))], axis=-1)


def _big_tile(t):
    return 768 if t % 768 == 0 else TM


def _ffn_in(h, w, *, lead, name):
    t, d = h.shape
    n2 = w.shape[-1]
    bm, bk = _big_tile(t), FFN_BK

    def body(h_ref, w_ref, u_ref, a_ref):
        hb = h_ref[...]
        for c0 in range(0, bk, FFN_SUB):
            c1 = min(c0 + FFN_SUB, bk)
            ug = _dot(hb, w_ref[:, c0:c1]).astype(BF16)
            uu = _dot(hb, w_ref[:, bk + c0:bk + c1]).astype(BF16)
            u_ref[:, c0:c1] = ug
            u_ref[:, bk + c0:bk + c1] = uu
            gv, up = ug.astype(F32), uu.astype(F32)
            a_ref[:, c0:c1] = (gv * _sigmoid(gv) * up).astype(BF16)

    return _pcall(
        body, name=name, grid=(t // bm, n2 // (2 * bk)),
        in_specs=[pl.BlockSpec((bm, d), lambda i, j: (i, 0)),
                  pl.BlockSpec((None, d, 2 * bk), lambda i, j: (lead, 0, j))],
        out_specs=[pl.BlockSpec((bm, 2 * bk), lambda i, j: (i, j)), pl.BlockSpec((bm, bk), lambda i, j: (i, j))],
        out_shape=[jax.ShapeDtypeStruct((t, n2), BF16), jax.ShapeDtypeStruct((t, n2 // 2), BF16)],
    )(h, w)


def _ffn_dx(dz, w_out, u, *, lead, name):
    t, d = dz.shape
    n2 = u.shape[1]
    bm, bk = _big_tile(t), FFN_BK

    def body(dz_ref, w_ref, u_ref, du_ref):
        dzb = dz_ref[...]
        for c0 in range(0, bk, FFN_SUB):
            c1 = min(c0 + FFN_SUB, bk)
            da = _dot_nt(dzb, w_ref[c0:c1, :])
            gv, up = u_ref[:, c0:c1].astype(F32), u_ref[:, bk + c0:bk + c1].astype(F32)
            s = _sigmoid(gv)
            du_ref[:, c0:c1] = (da * up * (s * (1.0 + gv * (1.0 - s)))).astype(BF16)
            du_ref[:, bk + c0:bk + c1] = (da * gv * s).astype(BF16)

    ublk = pl.BlockSpec((bm, 2 * bk), lambda i, j: (i, j))
    return _pcall(
        body, name=name, grid=(t // bm, n2 // (2 * bk)),
        in_specs=[pl.BlockSpec((bm, d), lambda i, j: (i, 0)),
                  pl.BlockSpec((None, bk, d), lambda i, j: (lead, j, 0)), ublk],
        out_specs=ublk, out_shape=jax.ShapeDtypeStruct((t, n2), BF16),
    )(dz, w_out, u)


def _lane(shape):
    return _iota(shape, len(shape) - 1)


def _pair_norm(x, g):
    lo = _lane(x.shape) < 64
    x2 = x * x
    s_lo = jnp.sum(jnp.where(lo, x2, 0.0), axis=-1, keepdims=True)
    s_hi = jnp.sum(jnp.where(lo, 0.0, x2), axis=-1, keepdims=True)
    rs = lax.rsqrt(jnp.where(lo, s_lo, s_hi) * (1.0 / 64) + EPS)
    return x * rs, rs


def _pair_mean(v):
    lo = _lane(v.shape) < 64
    s_lo = jnp.sum(jnp.where(lo, v, 0.0), axis=-1, keepdims=True)
    s_hi = jnp.sum(jnp.where(lo, 0.0, v), axis=-1, keepdims=True)
    return jnp.where(lo, s_lo, s_hi) * (1.0 / 64)


def _rot64(x):
    r1 = pltpu.roll(x, 32, 1)
    r2 = pltpu.roll(x, 96, 1)
    even = ((_lane(x.shape) >> 5) & 1) == 0
    return jnp.where(even, -r2, r1)


def _rope64(x, cos, sin):
    return x * cos + _rot64(x) * sin


def _rope64_t(d, cos, sin):
    return d * cos - _rot64(d * sin)


def _kprep_fwd(p, gk, cos, sin, *, name):
    t = p.shape[0]

    def body(k_ref, g_ref, c_ref, s_ref, o_ref):
        xh, _ = _pair_norm(k_ref[...], None)
        o_ref[...] = _rope64(xh * g_ref[...], c_ref[...], s_ref[...])

    blk = pl.BlockSpec((TM, 128), lambda i: (i, 0))
    return _pcall(
        body, name=name, grid=(t // TM,),
        in_specs=[pl.BlockSpec((TM, 128), lambda i: (i, 4)), pl.BlockSpec((1, 128), lambda i: (0, 0)), blk, blk],
        out_specs=blk, out_shape=jax.ShapeDtypeStruct((t, 128), F32),
    )(p, gk, cos, sin)


def _kprep_bwd(p, gk, cos, sin, dkp, dv, *, name):
    t = p.shape[0]

    def body(k_ref, g_ref, c_ref, s_ref, dkp_ref, dv_ref, o_ref, dg_ref):
        @pl.when(pl.program_id(0) == 0)
        def _():
            dg_ref[...] = jnp.zeros_like(dg_ref)
        xh, rs = _pair_norm(k_ref[...], None)
        dn = _rope64_t(dkp_ref[...], c_ref[...], s_ref[...])
        _acc_row(dg_ref, 0, jnp.sum(dn * xh, axis=0, keepdims=True))
        dxh = dn * g_ref[...]
        o_ref[:, 0:128] = (rs * (dxh - xh * _pair_mean(dxh * xh))).astype(BF16)
        o_ref[:, 128:256] = dv_ref[...].astype(BF16)

    blk = pl.BlockSpec((TM, 128), lambda i: (i, 0))
    return _pcall(
        body, name=name, grid=(t // TM,),
        in_specs=[pl.BlockSpec((TM, 128), lambda i: (i, 4)), pl.BlockSpec((1, 128), lambda i: (0, 0)), blk, blk, blk, blk],
        out_specs=[pl.BlockSpec((TM, 256), lambda i: (i, 0)), pl.BlockSpec((8, 128), lambda i: (0, 0))],
        out_shape=[jax.ShapeDtypeStruct((t, 256), BF16), jax.ShapeDtypeStruct((8, 128), F32)],
    )(p, gk, cos, sin, dkp, dv)


def _attn_common(i, t, lc, kp_ref, v_ref):
    span = QB + 2 * WINDOW
    start = pl.multiple_of(jnp.clip(i * QB - WINDOW, lc, t - span), WINDOW)
    kall = jnp.concatenate([kp_ref[0:lc, :], kp_ref[pl.ds(start, span), :]], axis=0)
    vall = jnp.concatenate([v_ref[0:lc, :], v_ref[pl.ds(start, span), :]], axis=0)
    nk = lc + span
    col = _iota((QB, nk), 1)
    krow = jnp.where(col < lc, col, start + col - lc)
    qrow = i * QB + _iota((QB, nk), 0)
    valid = (col < lc) | ((qrow >= lc) & (krow >= lc) & (jnp.abs(krow - qrow) <= WINDOW))
    lo = _lane(kall.shape) < 64
    kroll, vroll = pltpu.roll(kall, 64, 1), pltpu.roll(vall, 64, 1)
    zero = jnp.zeros_like(kall)
    kvar = [[_bf(jnp.where(lo, kall, zero)), _bf(jnp.where(lo, zero, kroll))],
            [_bf(jnp.where(lo, kroll, zero)), _bf(jnp.where(lo, zero, kall))]]
    vvar = [[_bf(jnp.where(lo, vall, zero)), _bf(jnp.where(lo, zero, vroll))],
            [_bf(jnp.where(lo, vroll, zero)), _bf(jnp.where(lo, zero, vall))]]
    return start, valid, kvar, vvar


def _softmax_sink(s, valid, snk):
    s = jnp.where(valid, s, NEG)
    m = jnp.maximum(jnp.max(s, axis=-1, keepdims=True), snk)
    e = jnp.exp(s - m)
    es = jnp.exp(snk - m)
    inv = 1.0 / (jnp.sum(e, axis=-1, keepdims=True) + es)
    return e * inv, es * inv


def _attn_fwd(p, kp, gq, sink, cos, sin, *, lc, name):
    t = p.shape[0]
    scale = 64 ** -0.5

    def body(q_ref, kp_ref, v_ref, g_ref, sink_ref, c_ref, s_ref, o_ref):
        i = pl.program_id(0)
        _, valid, kvar, vvar = _attn_common(i, t, lc, kp_ref, v_ref)
        cosv, sinv, gv = c_ref[...], s_ref[...], g_ref[...]
        for j in range(4):
            xh, _ = _pair_norm(q_ref[:, 128 * j:128 * j + 128], None)
            q2 = _bf(_rope64(xh * gv, cosv, sinv) * scale)
            acc = jnp.zeros((QB, 128), F32)
            for half in range(2):
                s = _dot_nt(q2, kvar[j // 2][half])
                pr, _ = _softmax_sink(s, valid, sink_ref[2 * j + half])
                acc = acc + _dot(pr, vvar[j // 2][half])
            o_ref[:, 128 * j:128 * j + 128] = acc.astype(BF16)

    qblk = pl.BlockSpec((QB, 128), lambda i: (i, 0))
    return _pcall(
        body, name=name, grid=(t // QB,),
        in_specs=[pl.BlockSpec((QB, 512), lambda i: (i, 0)),
                  pl.BlockSpec((t, 128), lambda i: (0, 0)),
                  pl.BlockSpec((t, 128), lambda i: (0, 5)),
                  pl.BlockSpec((1, 128), lambda i: (0, 0)),
                  pl.BlockSpec(memory_space=pltpu.SMEM), qblk, qblk],
        out_specs=pl.BlockSpec((QB, 512), lambda i: (i, 0)),
        out_shape=jax.ShapeDtypeStruct((t, 512), BF16),
    )(p, kp, p, gq, sink, cos, sin)


def _attn_bwd(p, kp, gq, sink, cos, sin, dmix, *, lc, name):
    t = p.shape[0]
    scale = 64 ** -0.5
    span = QB + 2 * WINDOW

    def body(q_ref, kp_ref, v_ref, g_ref, sink_ref, c_ref, s_ref, do_ref,
             dq_ref, dk_ref, dv_ref, dg_ref, dsink_ref):
        i = pl.program_id(0)

        @pl.when(i == 0)
        def _():
            dk_ref[...] = jnp.zeros_like(dk_ref)
            dv_ref[...] = jnp.zeros_like(dv_ref)
            dg_ref[...] = jnp.zeros_like(dg_ref)
            dsink_ref[...] = jnp.zeros_like(dsink_ref)

        start, valid, kvar, vvar = _attn_common(i, t, lc, kp_ref, v_ref)
        cosv, sinv, gv = c_ref[...], s_ref[...], g_ref[...]
        nk = lc + span
        dkt = [jnp.zeros((64, nk), F32), jnp.zeros((64, nk), F32)]
        dvt = [jnp.zeros((64, nk), F32), jnp.zeros((64, nk), F32)]
        for j in range(4):
            kvh = j // 2
            xh, rs = _pair_norm(q_ref[:, 128 * j:128 * j + 128], None)
            q2 = _bf(_rope64(xh * gv, cosv, sinv) * scale)
            do2 = _bf(do_ref[:, 128 * j:128 * j + 128])
            dq2 = jnp.zeros((QB, 128), F32)
            for half in range(2):
                s = _dot_nt(q2, kvar[kvh][half])
                pr, ps = _softmax_sink(s, valid, sink_ref[2 * j + half])
                dp = _dot_nt(do2, vvar[kvh][half])
                delta = jnp.sum(pr * dp, axis=-1, keepdims=True)
                ds = pr * (dp - delta)
                dsk = jnp.sum(jnp.sum(-ps * delta, axis=0, keepdims=True), axis=1, keepdims=True)
                _acc_row(dsink_ref, 2 * j + half, jnp.broadcast_to(dsk, (1, 128)))
                dq2 = dq2 + _dot(ds, kvar[kvh][half])
                hrows = slice(64 * half, 64 * half + 64)
                dkt[kvh] = dkt[kvh] + _dot_tn(q2, ds)[hrows]
                dvt[kvh] = dvt[kvh] + _dot_tn(do2, pr)[hrows]
            dn = _rope64_t(dq2 * scale, cosv, sinv)
            _acc_row(dg_ref, 0, jnp.sum(dn * xh, axis=0, keepdims=True))
            dxh = dn * gv
            dq_ref[:, 128 * j:128 * j + 128] = (rs * (dxh - xh * _pair_mean(dxh * xh))).astype(BF16)
        dk_all = jnp.concatenate(dkt, axis=0).T
        dv_all = jnp.concatenate(dvt, axis=0).T
        dk_ref[0:lc, :] += dk_all[0:lc]
        dv_ref[0:lc, :] += dv_all[0:lc]
        dk_ref[pl.ds(start, span), :] += dk_all[lc:nk]
        dv_ref[pl.ds(start, span), :] += dv_all[lc:nk]

    qblk = pl.BlockSpec((QB, 128), lambda i: (i, 0))
    full = pl.BlockSpec((t, 128), lambda i: (0, 0))
    small = pl.BlockSpec((8, 128), lambda i: (0, 0))
    return _pcall(
        body, name=name, grid=(t // QB,),
        in_specs=[pl.BlockSpec((QB, 512), lambda i: (i, 0)), full,
                  pl.BlockSpec((t, 128), lambda i: (0, 5)),
                  pl.BlockSpec((1, 128), lambda i: (0, 0)),
                  pl.BlockSpec(memory_space=pltpu.SMEM), qblk, qblk,
                  pl.BlockSpec((QB, 512), lambda i: (i, 0))],
        out_specs=[pl.BlockSpec((QB, 512), lambda i: (i, 0)), full, full, small, small],
        out_shape=[jax.ShapeDtypeStruct((t, 512), BF16), jax.ShapeDtypeStruct((t, 128), F32),
                   jax.ShapeDtypeStruct((t, 128), F32), jax.ShapeDtypeStruct((8, 128), F32),
                   jax.ShapeDtypeStruct((8, 128), F32)],
    )(p, kp, p, gq, sink, cos, sin, dmix)


def _tri(rev):
    r, c = _iota((CHUNK, CHUNK), 0), _iota((CHUNK, CHUNK), 1)
    return (c >= r) if rev else (c <= r)


def _blk_map(nb, rev, backward):
    if not rev:
        return (lambda n: nb - 1 - n) if backward else (lambda n: n)
    if backward:
        return lambda n: jnp.where(n < nb - 1, n + 1, 0)
    return lambda n: jnp.where(n == 0, 0, nb - n)


def _chunk_order(rev, backward, nc=TM // CHUNK):
    order = list(range(nc))
    return order[::-1] if (rev != backward) else order


def _hgrn_gates(qraw, fraw, lb):
    sq = _sigmoid(qraw)
    sf = _sigmoid(fraw)
    f = lb + (1.0 - lb) * sf
    return qraw * sq, 1.0 - f, jnp.log(f), sq, sf, f


HGRN_HP = 4


def _chunk_cumsum(x, rev):
    n = x.shape[0]
    pos = _iota(x.shape, 0) & (CHUNK - 1)
    s = 1
    while s < CHUNK:
        if rev:
            x = x + jnp.where(pos < CHUNK - s, pltpu.roll(x, n - s, 0), 0.0)
        else:
            x = x + jnp.where(pos >= s, pltpu.roll(x, s, 0), 0.0)
        s *= 2
    return x


def _block_terms(lf, rev):
    b = _chunk_cumsum(lf, rev)
    mid, last = (CHUNK // 2 - 1, 0) if rev else (CHUNK // 2, CHUNK - 1)

    def chunk_row(off):
        return jnp.concatenate([jnp.broadcast_to(b[c * CHUNK + off:c * CHUNK + off + 1, :], (CHUNK, b.shape[1]))
                                for c in range(TM // CHUNK)], axis=0)

    r, bl = chunk_row(mid), chunk_row(last)
    return _tri(rev), jnp.exp(b - r), jnp.exp(r - b), jnp.exp(b), jnp.exp(bl - b), jnp.exp(bl)


def _headnorm_apply(o, gv, gain):
    n = o * lax.rsqrt(jnp.mean(o * o, axis=-1, keepdims=True) + EPS)
    if gain is not None:
        n = n * gain
    return (n * (gv * _sigmoid(gv))).astype(BF16)


def _headnorm_grad(o, gv, dy, gain):
    rs = lax.rsqrt(jnp.mean(o * o, axis=-1, keepdims=True) + EPS)
    xh = o * rs
    n = xh * gain if gain is not None else xh
    sg = _sigmoid(gv)
    dn = dy * (gv * sg)
    dg = (dy * n * (sg * (1.0 + gv * (1.0 - sg)))).astype(BF16)
    dgain = jnp.sum(dn * xh, axis=0, keepdims=True)
    dxh = dn * gain if gain is not None else dn
    return rs * (dxh - xh * jnp.mean(dxh * xh, axis=-1, keepdims=True)), dg, dgain


def _hgrn_cols(bmap, n2, c0):
    return [pl.BlockSpec((TM, 256), lambda h, n, b=b: (bmap(n), c0 // 2 + h * n2 + b)) for b in range(n2)]


def _head_cols(refs, hh):
    return refs[hh // 2][:, 128 * (hh % 2):128 * (hh % 2) + 128]


def _hgrn_fwd(p, lb, *, rev, name, ofw=None, gain=None):
    t = p.shape[0]
    nb, nc = t // TM, TM // CHUNK
    bmap = _blk_map(nb, rev, False)
    fcol = 14 if rev else 10
    fused = ofw is not None

    n2 = HGRN_HP // 2

    def body(*refs):
        q_refs, f_refs, v_refs, lb_ref = refs[:n2], refs[n2:2 * n2], refs[2 * n2:3 * n2], refs[3 * n2]
        rest = refs[3 * n2 + 1:]
        if fused:
            ofw_ref, g_refs, gain_ref = rest[0], rest[1:1 + n2], rest[1 + n2]
            o_ref, sh_ref, mix_ref, st = rest[2 + n2:]
        else:
            o_ref, sh_ref, st = rest

        @pl.when(pl.program_id(1) == 0)
        def _():
            st[...] = jnp.zeros_like(st)
        for hh in range(HGRN_HP):
            ln = slice(128 * hh, 128 * hh + 128)
            q, k, lf, _, _, _ = _hgrn_gates(_head_cols(q_refs, hh), _head_cols(f_refs, hh), lb_ref[:, ln])
            tri, eq, ek, ei, eki, eb = _block_terms(lf, rev)
            qe, ke, qi, ki, vb = _bf(q * eq), _bf(k * ek), _bf(q * ei), _bf(k * eki), _bf(_head_cols(v_refs, hh))
            intra = []
            for cc in range(nc):
                rows = slice(cc * CHUNK, (cc + 1) * CHUNK)
                a = jnp.where(tri, _dot_nt(qe[rows], ke[rows]), 0.0)
                intra.append(_dot(a, vb[rows]))
            s = st[hh]
            for cc in _chunk_order(rev, False):
                rows = slice(cc * CHUNK, (cc + 1) * CHUNK)
                sh_ref[hh, cc] = s
                o_ref[rows, ln] = intra[cc] + _dot_nt(qi[rows], s)
                s = s * eb[cc * CHUNK:cc * CHUNK + 1, :] + _dot_tn(vb[rows], ki[rows])
            st[hh] = s
            if fused:
                osum = o_ref[:, ln] + ofw_ref[:, ln]
                o_ref[:, ln] = osum
                mix_ref[:, ln] = _headnorm_apply(osum, _head_cols(g_refs, hh), gain_ref[...])

    hp, wd = HGRN_HP, 128 * HGRN_HP
    col = functools.partial(_hgrn_cols, bmap, n2)
    oblk = pl.BlockSpec((TM, wd), lambda h, n: (bmap(n), h))
    ins = [p] * (3 * n2) + [lb]
    specs = col(6) + col(fcol) + col(18) + [pl.BlockSpec((1, wd), lambda h, n: (0, h))]
    out_specs = [oblk, pl.BlockSpec((hp, nc, 128, 128), lambda h, n: (h, bmap(n), 0, 0))]
    out_shape = [jax.ShapeDtypeStruct((t, 512), F32), jax.ShapeDtypeStruct((4, t // CHUNK, 128, 128), F32)]
    if fused:
        ins += [ofw] + [p] * n2 + [gain]
        specs += [oblk] + col(22) + [pl.BlockSpec((1, 128), lambda h, n: (0, 0))]
        out_specs.append(oblk)
        out_shape.append(jax.ShapeDtypeStruct((t, 512), BF16))
    return _pcall(body, name=name, grid=(4 // hp, nb), in_specs=specs, out_specs=out_specs, out_shape=out_shape,
                  scratch_shapes=[pltpu.VMEM((hp, 128, 128), F32)])(*ins)


def _hgrn_bwd(p, lb, sh, do, prev, *, rev, name, head=None):
    t = p.shape[0]
    nb, nc = t // TM, TM // CHUNK
    bmap = _blk_map(nb, rev, True)
    fcol = 14 if rev else 10
    has_prev = prev is not None
    odt = BF16 if has_prev else F32
    fused = head is not None

    n2 = HGRN_HP // 2

    def body(*refs):
        refs = list(refs)
        q_refs, f_refs, v_refs = refs[:n2], refs[n2:2 * n2], refs[2 * n2:3 * n2]
        lb_ref, sh_ref = refs[3 * n2], refs[3 * n2 + 1]
        pos = 3 * n2 + 2
        if fused:
            osum_ref, g_refs, dmix_ref, gain_ref = refs[pos], refs[pos + 1:pos + 1 + n2], refs[pos + 1 + n2], refs[pos + 2 + n2]
            pos += 3 + n2
        else:
            do_ref = refs[pos]
            pos += 1
        if has_prev:
            pq_ref, pv_ref = refs[pos], refs[pos + 1]
            pos += 2
        dq_ref, df_ref, dv_ref, dlb_ref = refs[pos:pos + 4]
        pos += 4
        if fused:
            do_out, dg_ref, dgain_ref = refs[pos:pos + 3]
            pos += 3
        dst = refs[pos]

        @pl.when(pl.program_id(1) == 0)
        def _():
            dst[...] = jnp.zeros_like(dst)
            dlb_ref[...] = jnp.zeros_like(dlb_ref)

        if fused:
            @pl.when((pl.program_id(0) == 0) & (pl.program_id(1) == 0))
            def _():
                dgain_ref[...] = jnp.zeros_like(dgain_ref)

        cat = functools.partial(jnp.concatenate, axis=0)
        for hh in range(HGRN_HP):
            ln = slice(128 * hh, 128 * hh + 128)
            lbv = lb_ref[:, ln]
            qraw, fraw = _head_cols(q_refs, hh), _head_cols(f_refs, hh)
            q, k, lf, sq, sf, f = _hgrn_gates(qraw, fraw, lbv)
            tri, eq, ek, ei, eki, eb = _block_terms(lf, rev)
            qe, ke, qi, ki = q * eq, k * ek, q * ei, k * eki
            if fused:
                dov, dg, dgain = _headnorm_grad(osum_ref[:, ln], _head_cols(g_refs, hh), dmix_ref[:, ln], gain_ref[...])
                do_out[:, ln] = _bf(dov)
                dg_ref[:, ln] = dg
                _acc_row(dgain_ref, 0, dgain)
            else:
                dov = do_ref[:, ln]
            qeb, keb, qib, kib, vb, dob = _bf(qe), _bf(ke), _bf(qi), _bf(ki), _bf(_head_cols(v_refs, hh)), _bf(dov)
            dv, dqe, dke, dqi = [None] * nc, [None] * nc, [None] * nc, [None] * nc
            for cc in range(nc):
                rows = slice(cc * CHUNK, (cc + 1) * CHUNK)
                a = jnp.where(tri, _dot_nt(qeb[rows], keb[rows]), 0.0)
                da = jnp.where(tri, _dot_nt(dob[rows], vb[rows]), 0.0)
                dv[cc] = _dot_tn(a, dob[rows])
                dqe[cc], dke[cc] = _dot(da, keb[rows]), _dot_tn(da, qeb[rows])
                dqi[cc] = _dot(dob[rows], sh_ref[hh, cc])
            dki, dbl = [None] * nc, [None] * nc
            ds = dst[hh]
            for cc in _chunk_order(rev, True):
                rows = slice(cc * CHUNK, (cc + 1) * CHUNK)
                ebc = eb[cc * CHUNK:cc * CHUNK + 1, :]
                dv[cc] = dv[cc] + _dot_nt(kib[rows], ds)
                dki[cc] = _dot(vb[rows], ds)
                dbl[cc] = jnp.broadcast_to(jnp.sum(dki[cc] * ki[rows], axis=0, keepdims=True)
                                           + jnp.sum(ds * sh_ref[hh, cc], axis=0, keepdims=True) * ebc, (CHUNK, 128))
                ds = ds * ebc + _dot_tn(dob[rows], qib[rows])
            dst[hh] = ds
            dqe, dke, dqi, dki, dv, dbl = cat(dqe), cat(dke), cat(dqi), cat(dki), cat(dv), cat(dbl)
            dq = dqe * eq + dqi * ei
            dk = dke * ek + dki * eki
            last = 0 if rev else CHUNK - 1
            db = dqe * qe - dke * ke + dqi * qi - dki * ki
            db = db + jnp.where((_iota(db.shape, 0) & (CHUNK - 1)) == last, dbl, 0.0)
            dlf = _chunk_cumsum(db, not rev)
            dqr = dq * (sq * (1.0 + qraw * (1.0 - sq)))
            dfv = dlf / f - dk
            dfr = dfv * (1.0 - lbv) * (sf * (1.0 - sf))
            dlb_ref[:, ln] += jnp.sum(dfv * (1.0 - sf), axis=0, keepdims=True)
            if has_prev:
                dqr = dqr + pq_ref[:, ln]
                dv = dv + pv_ref[:, ln]
            dq_ref[:, ln] = dqr.astype(odt)
            df_ref[:, ln] = dfr.astype(odt)
            dv_ref[:, ln] = dv.astype(odt)

    hp, wd = HGRN_HP, 128 * HGRN_HP
    col = functools.partial(_hgrn_cols, bmap, n2)
    oblk = pl.BlockSpec((TM, wd), lambda h, n: (bmap(n), h))
    ins = [p] * (3 * n2) + [lb, sh]
    specs = col(6) + col(fcol) + col(18) + [pl.BlockSpec((1, wd), lambda h, n: (0, h)),
                                            pl.BlockSpec((hp, nc, 128, 128), lambda h, n: (h, bmap(n), 0, 0))]
    if fused:
        osum, dmix, gain = head
        ins += [osum] + [p] * n2 + [dmix, gain]
        specs += [oblk] + col(22) + [pl.BlockSpec((TM, wd), lambda h, n: (bmap(n), 4 // hp + h)),
                                     pl.BlockSpec((1, 128), lambda h, n: (0, 0))]
    else:
        ins.append(do); specs.append(oblk)
    if has_prev:
        ins += list(prev); specs += [oblk, oblk]
    out_specs = [oblk, oblk, oblk, pl.BlockSpec((1, wd), lambda h, n: (0, h))]
    out_shape = [jax.ShapeDtypeStruct((t, 512), odt)] * 3 + [jax.ShapeDtypeStruct((1, 512), F32)]
    if fused:
        out_specs += [oblk, oblk, pl.BlockSpec((8, 128), lambda h, n: (0, 0))]
        out_shape += [jax.ShapeDtypeStruct((t, 512), BF16), jax.ShapeDtypeStruct((t, 512), BF16),
                      jax.ShapeDtypeStruct((8, 128), F32)]
    return _pcall(body, name=name, grid=(4 // hp, nb), in_specs=specs, out_specs=out_specs, out_shape=out_shape,
                  scratch_shapes=[pltpu.VMEM((hp, 128, 128), F32)])(*ins)


def _rope256(x, cos, sin):
    x1, x2 = x[:, 0:128], x[:, 128:256]
    return jnp.concatenate([x1 * cos - x2 * sin, x2 * cos + x1 * sin], axis=-1)


def _rope256_t(d, cos, sin):
    d1, d2 = d[:, 0:128], d[:, 128:256]
    return jnp.concatenate([d1 * cos + d2 * sin, d2 * cos - d1 * sin], axis=-1)


RET_DK, RET_DV, RET_H = 256, 512, 4
RET_KSCALE = RET_DK ** -0.5
RCH = TM
RET_HP = 4


def _ret_terms(lg, rev):
    r, c = _iota((RCH, RCH), 0), _iota((RCH, RCH), 1)
    rel = ((c - r) if rev else (r - c)).astype(F32)
    dmat = jnp.where(rel >= 0, jnp.exp(lg[:, 0:1] * jnp.maximum(rel, 0.0)), 0.0)
    pos = _iota((RCH, 1), 0).astype(F32)
    cnt = (RCH - pos) if rev else (pos + 1.0)
    ei = jnp.exp(lg * cnt)
    eki = jnp.exp(lg * (RCH - cnt))
    eb = jnp.exp(lg * float(RCH))
    return dmat, ei, eki, eb


def _ret_fwd(p, lgt, cos, sin, *, rev, name, ofw=None):
    t = p.shape[0]
    nb, nc = t // TM, TM // RCH
    bmap = _blk_map(nb, rev, False)
    fused = ofw is not None

    def body(*refs):
        q_ref, k_ref, v_ref, lg_ref, c_ref, s_ref = refs[:6]
        if fused:
            ofw_ref, g_ref, o_ref, sh_ref, mix_ref, st = refs[6:]
        else:
            o_ref, sh_ref, st = refs[6:]

        @pl.when(pl.program_id(1) == 0)
        def _():
            st[...] = jnp.zeros_like(st)
        for hh in range(RET_HP):
            qc, vc = slice(RET_DK * hh, RET_DK * (hh + 1)), slice(RET_DV * hh, RET_DV * (hh + 1))
            dmat, ei, eki, eb = _ret_terms(lg_ref[hh], rev)
            for cc in _chunk_order(rev, False, nc):
                rows = slice(cc * RCH, (cc + 1) * RCH)
                cosv, sinv = c_ref[rows, :], s_ref[rows, :]
                q = _rope256(q_ref[rows, qc].astype(F32), cosv, sinv)
                k = _rope256(k_ref[rows, qc].astype(F32), cosv, sinv) * RET_KSCALE
                v = v_ref[rows, vc]
                s0 = st[hh]
                sh_ref[hh, cc] = s0.astype(BF16)
                a = _dot_nt(q, k) * dmat
                o = _dot(a, v) + _dot_nt(q * ei, s0)
                st[hh] = s0 * eb + _dot_tn(v, k * eki)
                if fused:
                    o = o + ofw_ref[rows, vc]
                    mix_ref[rows, vc] = _headnorm_apply(o, g_ref[rows, vc].astype(F32), None)
                o_ref[rows, vc] = o

    hp = RET_HP
    tab = pl.BlockSpec((TM, 128), lambda h, n: (bmap(n), 0))
    oblk = pl.BlockSpec((TM, hp * RET_DV), lambda h, n: (bmap(n), h))
    ins = [p, p, p, lgt, cos, sin]
    specs = [pl.BlockSpec((TM, hp * RET_DK), lambda h, n: (bmap(n), h)),
             pl.BlockSpec((TM, hp * RET_DK), lambda h, n: (bmap(n), RET_H // hp + h)),
             pl.BlockSpec((TM, hp * RET_DV), lambda h, n: (bmap(n), RET_H // hp + h)),
             pl.BlockSpec((hp, 1, RET_DK), lambda h, n: (h, 0, 0)), tab, tab]
    out_specs = [oblk, pl.BlockSpec((hp, nc, RET_DV, RET_DK), lambda h, n: (h, bmap(n), 0, 0))]
    out_shape = [jax.ShapeDtypeStruct((t, RET_H * RET_DV), F32),
                 jax.ShapeDtypeStruct((RET_H, t // RCH, RET_DV, RET_DK), BF16)]
    if fused:
        ins += [ofw, p]
        specs += [oblk, pl.BlockSpec((TM, hp * RET_DV), lambda h, n: (bmap(n), 2 * RET_H // hp + h))]
        out_specs.append(oblk)
        out_shape.append(jax.ShapeDtypeStruct((t, RET_H * RET_DV), BF16))
    return _pcall(body, name=name, grid=(RET_H // hp, nb), in_specs=specs, out_specs=out_specs, out_shape=out_shape,
                  scratch_shapes=[pltpu.VMEM((hp, RET_DV, RET_DK), F32)])(*ins)


def _ret_bwd(p, lgt, cos, sin, sh, do, prev, *, rev, name, head=None):
    t = p.shape[0]
    nb, nc = t // TM, TM // RCH
    bmap = _blk_map(nb, rev, True)
    has_prev = prev is not None
    odt = BF16 if has_prev else F32
    fused = head is not None

    def body(*refs):
        refs = list(refs)
        q_ref, k_ref, v_ref, lg_ref, c_ref, s_ref, sh_ref = refs[:7]
        if fused:
            osum_ref, g_ref, dmix_ref = refs[7:10]
            pos = 10
        else:
            do_ref = refs[7]
            pos = 8
        if has_prev:
            pq_ref, pk_ref, pv_ref = refs[pos:pos + 3]
            pos += 3
        dq_ref, dk_ref, dv_ref = refs[pos:pos + 3]
        pos += 3
        if fused:
            do_out, dg_ref = refs[pos:pos + 2]
            pos += 2
        dst = refs[pos]

        @pl.when(pl.program_id(1) == 0)
        def _():
            dst[...] = jnp.zeros_like(dst)

        for hh in range(RET_HP):
            qc, vc = slice(RET_DK * hh, RET_DK * (hh + 1)), slice(RET_DV * hh, RET_DV * (hh + 1))
            dmat, ei, eki, eb = _ret_terms(lg_ref[hh], rev)
            for cc in _chunk_order(rev, True, nc):
                rows = slice(cc * RCH, (cc + 1) * RCH)
                cosv, sinv = c_ref[rows, :], s_ref[rows, :]
                q = _rope256(q_ref[rows, qc].astype(F32), cosv, sinv)
                k = _rope256(k_ref[rows, qc].astype(F32), cosv, sinv) * RET_KSCALE
                v = v_ref[rows, vc]
                if fused:
                    dov, dg, _ = _headnorm_grad(osum_ref[rows, vc], g_ref[rows, vc].astype(F32), dmix_ref[rows, vc], None)
                    do_out[rows, vc] = _bf(dov)
                    dg_ref[rows, vc] = dg
                else:
                    dov = do_ref[rows, vc]
                s0 = sh_ref[hh, cc]
                dsc = dst[hh]
                qi, ki = q * ei, k * eki
                a = _dot_nt(q, k) * dmat
                da = _dot_nt(dov, v) * dmat
                dv = _dot_tn(a, dov) + _dot_nt(ki, dsc)
                dqs = _dot(da, k) + _dot(dov, s0) * ei
                dks = _dot_tn(da, q) + _dot(v, dsc) * eki
                dst[hh] = dsc * eb + _dot_tn(dov, qi)
                dq = _rope256_t(dqs, cosv, sinv)
                dk = _rope256_t(dks * RET_KSCALE, cosv, sinv)
                if has_prev:
                    dq = dq + pq_ref[rows, qc]
                    dk = dk + pk_ref[rows, qc]
                    dv = dv + pv_ref[rows, vc]
                dq_ref[rows, qc] = dq.astype(odt)
                dk_ref[rows, qc] = dk.astype(odt)
                dv_ref[rows, vc] = dv.astype(odt)

    hp = RET_HP
    tab = pl.BlockSpec((TM, 128), lambda h, n: (bmap(n), 0))
    qblk = pl.BlockSpec((TM, hp * RET_DK), lambda h, n: (bmap(n), h))
    vblk = pl.BlockSpec((TM, hp * RET_DV), lambda h, n: (bmap(n), h))
    ins = [p, p, p, lgt, cos, sin, sh]
    specs = [qblk, pl.BlockSpec((TM, hp * RET_DK), lambda h, n: (bmap(n), RET_H // hp + h)),
             pl.BlockSpec((TM, hp * RET_DV), lambda h, n: (bmap(n), RET_H // hp + h)),
             pl.BlockSpec((hp, 1, RET_DK), lambda h, n: (h, 0, 0)), tab, tab,
             pl.BlockSpec((hp, nc, RET_DV, RET_DK), lambda h, n: (h, bmap(n), 0, 0))]
    if fused:
        osum, dmix = head
        ins += [osum, p, dmix]
        specs += [vblk, pl.BlockSpec((TM, hp * RET_DV), lambda h, n: (bmap(n), 2 * RET_H // hp + h)), vblk]
    else:
        ins.append(do); specs.append(vblk)
    if has_prev:
        ins += list(prev); specs += [qblk, qblk, vblk]
    out_specs = [qblk, qblk, vblk]
    out_shape = [jax.ShapeDtypeStruct((t, RET_H * RET_DK), odt), jax.ShapeDtypeStruct((t, RET_H * RET_DK), odt),
                 jax.ShapeDtypeStruct((t, RET_H * RET_DV), odt)]
    if fused:
        out_specs += [vblk, vblk]
        out_shape += [jax.ShapeDtypeStruct((t, RET_H * RET_DV), BF16), jax.ShapeDtypeStruct((t, RET_H * RET_DV), BF16)]
    return _pcall(body, name=name, grid=(RET_H // hp, nb), in_specs=specs, out_specs=out_specs, out_shape=out_shape,
                  scratch_shapes=[pltpu.VMEM((hp, RET_DV, RET_DK), F32)])(*ins)


def _rope_tables(lc, l):
    tt = jnp.arange(l)
    row, colp = (tt // 64).astype(F32), (tt % 64).astype(F32)
    inv = 10000.0 ** (-jnp.arange(16, dtype=F32) / 16)
    ang = jnp.concatenate([row[:, None] * inv, colp[:, None] * inv], axis=-1)
    ang = jnp.concatenate([jnp.zeros((lc, 32), F32), ang], axis=0)
    acos, asin = jnp.tile(jnp.cos(ang), (1, 4)), jnp.tile(jnp.sin(ang), (1, 4))
    theta = 1.0 / (10000.0 ** jnp.linspace(0.0, 1.0, 128, dtype=F32))
    rang = jnp.arange(l, dtype=F32)[:, None] * theta
    rang = jnp.concatenate([jnp.zeros((lc, 128), F32), rang], axis=0)
    return acos, asin, jnp.cos(rang), jnp.sin(rang)


class _Weights:
    def __init__(self, w):
        self.w = w

    def landed(self, grp, after):
        pass

    def full(self, grp, after):
        return self.w

    def send_grads(self, grp, grads):
        return jnp.zeros((8, 128), F32)


def _local_step(x0, target, mods, ng, wsrc, small):
    t, d = x0.shape
    l = target.shape[0]
    lc = t - l
    acos, asin, rcos, rsin = _rope_tables(lc, l)
    lg_fw = jnp.log(1.0 - 2.0 ** (-5.0 - jnp.arange(RET_H, dtype=F32)))
    lgt_fw = jnp.broadcast_to(lg_fw[:, None, None], (RET_H, 1, RET_DK))
    lgt_bw = jnp.broadcast_to(lg_fw[::-1][:, None, None], (RET_H, 1, RET_DK))
    gq, gk, sink, gain, lb = small['gq'], small['gk'], small['sink'], small['gain'], small['lb']

    (h1,) = _row_fwd(x0, mods, g=ng[0], shift=0, scale=1, name='l0_norm1')
    wsrc.landed('even', h1)
    w = dict(wsrc.full('even', h1))
    p0 = _mm_nn(h1, w['even_in'], name='l0_in')
    kp = _kprep_fwd(p0, gk, acos, asin, name='l0_kprep')
    att = _attn_fwd(p0, kp, gq, sink, acos, asin, lc=lc, name='l0_attn')
    wsrc.landed('ffn', att)
    hof, hsf = _hgrn_fwd(p0, lb, rev=False, name='l0_hgrn_f')
    wsrc.landed('odd', hof)
    hos, hsb, bmix = _hgrn_fwd(p0, lb, rev=True, name='l0_hgrn_b', ofw=hof, gain=gain)
    mix0 = [att, bmix]
    y0 = _mm_nn(mix0, w['even_out'], name='l0_out')
    x1, h2 = _row_fwd(x0, mods, y=y0, gate=2, g=ng[1], shift=3, scale=4, name='l0_norm2')
    w.update(wsrc.full('ffn', h2))
    u0, a0 = _ffn_in(h2, w['ffn_in'], lead=0, name='ffn_in')
    z0 = _mm_nn(a0, w['ffn_out'], lead=0, name='ffn_out')
    x2, h3 = _row_fwd(x1, mods, y=z0, gate=5, g=ng[2], shift=12, scale=13, name='l1_norm1')
    w.update(wsrc.full('odd', h3))
    p1 = _mm_nn(h3, w['odd_in'], out_dtype=BF16, name='l1_in')
    rof, rsf = _ret_fwd(p1, lgt_fw, rcos, rsin, rev=False, name='l1_ret_f')
    ros, rsb, mix1 = _ret_fwd(p1, lgt_bw, rcos, rsin, rev=True, name='l1_ret_b', ofw=rof)
    y1 = _mm_nn(mix1, w['odd_out'], name='l1_out')
    x3, h4 = _row_fwd(x2, mods, y=y1, gate=14, g=ng[3], shift=15, scale=16, name='l1_norm2')
    u1, a1 = _ffn_in(h4, w['ffn_in'], lead=1, name='ffn_in')
    z1 = _mm_nn(a1, w['ffn_out'], lead=1, name='ffn_out')
    loss, dx4, dz1, s_fin = _row_final(x3, z1, mods, target, gate=17, name='loss')

    du1 = _ffn_dx(dz1, w['ffn_out'], u1, lead=1, name='ffn_out_dx')
    g_ffn_out1 = _mm_tn(a1, dz1, name='ffn_out_dw')
    dh4 = _mm_nt(du1, w['ffn_in'], lead=1, name='ffn_in_dx')
    g_ffn_in1 = _mm_tn(h4, du1, name='ffn_in_dw')
    dx3, dy1, s_l1n2 = _row_bwd(x3, dx4, dh4, mods, ng[3], shift=15, scale=16, y=y1, gate=14, name='l1_norm2_bwd')
    dmix1 = _mm_nt(dy1, w['odd_out'], name='l1_out_dx')
    g_odd_out = _mm_tn(mix1, dy1, name='l1_out_dw')
    rdq, rdk, rdv, rdo, rdg = _ret_bwd(p1, lgt_fw, rcos, rsin, rsf, None, None, rev=False, name='l1_ret_f_bwd',
                                       head=(ros, dmix1))
    rdq, rdk, rdv = _ret_bwd(p1, lgt_bw, rcos, rsin, rsb, rdo, (rdq, rdk, rdv), rev=True, name='l1_ret_b_bwd')
    dp1 = [rdq, rdk, rdv, rdg]
    dh3 = _mm_nt(dp1, w['odd_in'], name='l1_in_dx')
    g_odd_in = _mm_tn(h3, dp1, name='l1_in_dw')
    mods = mods + wsrc.send_grads('early', dict(ffn_in1=g_ffn_in1, ffn_out1=g_ffn_out1, odd_in=g_odd_in,
                                                odd_out=g_odd_out))[0, 0]
    dx2, dz0, s_l1n1 = _row_bwd(x2, dx3, dh3, mods, ng[2], shift=12, scale=13, y=z0, gate=5, name='l1_norm1_bwd')
    du0 = _ffn_dx(dz0, w['ffn_out'], u0, lead=0, name='ffn_out_dx')
    g_ffn_out0 = _mm_tn(a0, dz0, name='ffn_out_dw')
    dh2 = _mm_nt(du0, w['ffn_in'], lead=0, name='ffn_in_dx')
    g_ffn_in0 = _mm_tn(h2, du0, name='ffn_in_dw')
    mods = mods + wsrc.send_grads('mid', dict(ffn_in0=g_ffn_in0, ffn_out0=g_ffn_out0))[0, 0]
    dx1, dy0, s_l0n2 = _row_bwd(x1, dx2, dh2, mods, ng[1], shift=3, scale=4, y=y0, gate=2, name='l0_norm2_bwd')
    dmix0 = _mm_nt(dy0, w['even_out'], name='l0_out_dx')
    g_even_out = _mm_tn(mix0, dy0, name='l0_out_dw')
    hq, hff, hv, dlb_f, hdo, hdg, s_gain = _hgrn_bwd(p0, lb, hsf, None, None, rev=False, name='l0_hgrn_f_bwd',
                                                     head=(hos, dmix0, gain))
    hq, hfb, hv, dlb_b = _hgrn_bwd(p0, lb, hsb, hdo, (hq, hv), rev=True, name='l0_hgrn_b_bwd')
    adq, dkp, adv, s_gq, s_sink = _attn_bwd(p0, kp, gq, sink, acos, asin, dmix0, lc=lc, name='l0_attn_bwd')
    dkv, s_gk = _kprep_bwd(p0, gk, acos, asin, dkp, adv, name='l0_kprep_bwd')
    dp0 = jnp.concatenate([adq, dkv, hq, _bf(hff), hfb, hv, hdg], axis=1)
    dh1 = _mm_nt(dp0, w['even_in'], name='l0_in_dx')
    g_even_in = _mm_tn(h1, dp0, name='l0_in_dw')
    dx0, s_l0n1 = _row_bwd(x0, dx1, dh1, mods, ng[0], shift=0, scale=1, latent_only=True, name='l0_norm1_bwd')

    grads = dict(ffn_in0=g_ffn_in0, ffn_in1=g_ffn_in1, ffn_out0=g_ffn_out0, ffn_out1=g_ffn_out1,
                 even_in=g_even_in, even_out=g_even_out, odd_in=g_odd_in, odd_out=g_odd_out)
    sums = dict(fin=s_fin, l1n2=s_l1n2, l1n1=s_l1n1, l0n2=s_l0n2, l0n1=s_l0n1, gain=s_gain, gq=s_gq, gk=s_gk,
                sink=s_sink, dlb_f=dlb_f, dlb_b=dlb_b)
    return loss, dx0, grads, sums


def _place():
    return lax.axis_index("x"), lax.axis_index("y"), lax.axis_index("c")


def _ag8(blk, *, name):
    r, c = blk.shape
    flips = [(dx, dy, dc) for dx in (0, 1) for dy in (0, 1) for dc in (0, 1) if (dx, dy, dc) != (0, 0, 0)]

    def body(x_ref, out_ref, send_sems, recv_sems, local_sem):
        ax, ay, ac = _place()
        me = 4 * ax + 2 * ay + ac
        mine = pltpu.make_async_copy(x_ref, out_ref.at[me], local_sem)
        mine.start()
        sent = []
        for k, (dx, dy, dc) in enumerate(flips):
            peer = (lax.rem(ax + dx, 2), lax.rem(ay + dy, 2), lax.rem(ac + dc, 2))
            cp = pltpu.make_async_remote_copy(src_ref=x_ref, dst_ref=out_ref.at[me], send_sem=send_sems.at[k],
                                              recv_sem=recv_sems.at[k], device_id=peer, device_id_type=MESH)
            cp.start()
            sent.append((cp, 4 * peer[0] + 2 * peer[1] + peer[2]))
        for k, (cp, pidx) in enumerate(sent):
            pltpu.make_async_remote_copy(src_ref=x_ref, dst_ref=out_ref.at[pidx], send_sem=send_sems.at[k],
                                         recv_sem=recv_sems.at[k], device_id=(ax, ay, ac),
                                         device_id_type=MESH).wait_recv()
        for cp, _ in sent:
            cp.wait_send()
        mine.wait()

    return _pcall(
        body, name=name,
        in_specs=[pl.BlockSpec(memory_space=pltpu.VMEM)],
        out_specs=pl.BlockSpec(memory_space=pltpu.VMEM),
        out_shape=jax.ShapeDtypeStruct((8, r, c), blk.dtype),
        scratch_shapes=[pltpu.SemaphoreType.DMA((7,)), pltpu.SemaphoreType.DMA((7,)), pltpu.SemaphoreType.DMA],
    )(blk)


_HBM = pl.BlockSpec(memory_space=pltpu.HBM)
_SEM = pl.BlockSpec(memory_space=pltpu.SEMAPHORE)
_DATAFLOW = pltpu.SideEffectType.DATAFLOW_SIDE_EFFECTING


def _split_start(bufs, plan, k, *, name):
    n = len(bufs)

    def body(*refs):
        ins, send_sems, recv_sems, token = refs[:n], refs[n], refs[n + 1], refs[2 * n + 2]
        for i, (src, dst, dev) in enumerate(plan(ins)):
            pltpu.make_async_remote_copy(src_ref=src, dst_ref=dst, send_sem=send_sems.at[i], recv_sem=recv_sems.at[i],
                                         device_id=dev, device_id_type=MESH).start()
        token[...] = jnp.zeros_like(token)

    res = _pcall(
        body, name=name,
        out_shape=(pltpu.SemaphoreType.DMA((k,)), pltpu.SemaphoreType.DMA((k,)),
                   *[pltpu.HBM(b.shape, b.dtype) for b in bufs], jax.ShapeDtypeStruct((8, 128), F32)),
        in_specs=[_HBM] * n, out_specs=(_SEM, _SEM, *[_HBM] * n, pl.BlockSpec(memory_space=pltpu.VMEM)),
        input_output_aliases={i: 2 + i for i in range(n)},
        compiler_params=pltpu.CompilerParams(has_side_effects=_DATAFLOW),
    )(*[pltpu.with_memory_space_constraint(b, pltpu.HBM) for b in bufs])
    return res[0], res[1], list(res[2:2 + n]), res[2 + n]


def _split_wait(bufs, send_sems, recv_sems, plan, after, *, name):
    n = len(bufs)

    def body(*refs):
        ins, ssem, rsem = refs[:n], refs[n], refs[n + 1]
        for i, (src, dst, dev) in enumerate(plan(ins)):
            cp = pltpu.make_async_remote_copy(src_ref=src, dst_ref=dst, send_sem=ssem.at[i], recv_sem=rsem.at[i],
                                              device_id=dev, device_id_type=MESH)
            cp.wait_send()
            cp.wait_recv()

    res = _pcall(
        body, name=name, out_shape=tuple(pltpu.HBM(b.shape, b.dtype) for b in bufs),
        in_specs=[_HBM] * n + [_SEM, _SEM, pl.BlockSpec(memory_space=pl.ANY)], out_specs=tuple([_HBM] * n),
        input_output_aliases={i: i for i in range(n)},
        compiler_params=pltpu.CompilerParams(has_side_effects=_DATAFLOW),
    )(*bufs, send_sems, recv_sems, after)
    return list(res)


_CHIP_FLIPS = [(1, 0), (0, 1), (1, 1)]


class _GatheredWeights:
    GROUPS = (('even', ('even_in', 'even_out')), ('ffn', ('ffn_in', 'ffn_out')), ('odd', ('odd_in', 'odd_out')))

    def __init__(self, shards, reducer):
        self.shards = shards
        self.send_grads = reducer.start
        self.ici, self.d2d, self.token = {}, {}, None
        for grp, names in self.GROUPS:
            src = [shards[nm].reshape(2, shards[nm].shape[0] // 2, shards[nm].shape[1]) for nm in names]
            land = [lax.empty((4,) + a.shape, a.dtype) for a in src]
            m = len(names)
            sends, recvs, bufs, token = _split_start(src + land, functools.partial(self._ici_plan, m, True), 4 * m,
                                                     name='gather_' + grp + '_ici_start')
            self.ici[grp] = (sends, recvs, bufs, m)
            self.token = token if self.token is None else self.token + token

    @staticmethod
    def _ici_plan(m, sending, refs):
        ax, ay, ac = _place()
        s = 2 * ax + ay
        out = []
        for a in range(m):
            for dx, dy in _CHIP_FLIPS:
                px, py = lax.rem(ax + dx, 2), lax.rem(ay + dy, 2)
                slot = s if sending else 2 * px + py
                out.append((refs[a].at[ac], refs[m + a].at[slot, ac], (px, py, ac)))
        for a in range(m):
            out.append((refs[a], refs[m + a].at[s], (ax, ay, 1 - ac)))
        return out

    @staticmethod
    def _d2d_plan(m, sending, refs):
        ax, ay, ac = _place()
        out = []
        for a in range(m):
            for dx, dy in _CHIP_FLIPS:
                sp = 2 * lax.rem(ax + dx, 2) + lax.rem(ay + dy, 2)
                out.append((refs[a].at[sp, ac], refs[a].at[sp, ac if sending else 1 - ac], (ax, ay, 1 - ac)))
        return out

    def landed(self, grp, after):
        sends, recvs, bufs, m = self.ici[grp]
        bufs = _split_wait(bufs, sends, recvs, functools.partial(self._ici_plan, m, False), after,
                           name='gather_' + grp + '_ici_wait')
        sends, recvs, land, _ = _split_start(bufs[m:], functools.partial(self._d2d_plan, m, True), 3 * m,
                                             name='gather_' + grp + '_d2d_start')
        self.d2d[grp] = (sends, recvs, land, m)

    def full(self, grp, after):
        sends, recvs, land, m = self.d2d[grp]
        land = _split_wait(land, sends, recvs, functools.partial(self._d2d_plan, m, False), after,
                           name='gather_' + grp + '_d2d_wait')
        names = dict(self.GROUPS)[grp]
        return {nm: _from_shards(nm, g.reshape((4,) + self.shards[nm].shape)) for nm, g in zip(names, land)}


def _to_sibling(arrs, *, name):
    n = len(arrs)

    def body(*refs):
        ins, outs = refs[:n], refs[n:2 * n]
        send_sems, recv_sems = refs[2 * n:]
        ax, ay, ac = _place()
        cps = [pltpu.make_async_remote_copy(src_ref=ins[a], dst_ref=outs[a], send_sem=send_sems.at[a],
                                            recv_sem=recv_sems.at[a], device_id=(ax, ay, 1 - ac),
                                            device_id_type=MESH) for a in range(n)]
        for cp in cps:
            cp.start()
        for cp in cps:
            cp.wait_recv()
        for cp in cps:
            cp.wait_send()

    hbm = pl.BlockSpec(memory_space=pl.ANY)
    return _pcall(
        body, name=name, in_specs=[hbm] * n, out_specs=[hbm] * n,
        out_shape=[jax.ShapeDtypeStruct(a.shape, a.dtype) for a in arrs],
        scratch_shapes=[pltpu.SemaphoreType.DMA((n,))] * 2,
    )(*arrs)


def _mod_fwd(cond_raw, mw, mb, *, name):
    _, d, n = mw.shape

    def body(c_ref, w_ref, b_ref, o_ref):
        cv = c_ref[...]
        o_ref[...] = _dot(cv * _sigmoid(cv), w_ref[...]) + b_ref[...]

    return _pcall(
        body, name=name, grid=(2,),
        in_specs=[pl.BlockSpec((16, d), lambda l: (0, 0)), pl.BlockSpec((None, d, n), lambda l: (l, 0, 0)),
                  pl.BlockSpec((None, 1, n), lambda l: (l, 0, 0))],
        out_specs=pl.BlockSpec((None, 16, n), lambda l: (l, 0, 0)),
        out_shape=jax.ShapeDtypeStruct((2, 16, n), F32),
    )(cond_raw, mw, mb)


def _mod_bwd(cond_raw, dms, mw, *, name):
    _, d, n = mw.shape

    def body(c_ref, dm_ref, w_ref, gw_ref, dc_ref):
        @pl.when(pl.program_id(0) == 0)
        def _():
            dc_ref[...] = jnp.zeros_like(dc_ref)
        cv = c_ref[...]
        gw_ref[...] = _dot_tn(cv * _sigmoid(cv), dm_ref[...])
        dc_ref[...] += _dot_nt(dm_ref[...], w_ref[...])

    return _pcall(
        body, name=name, grid=(2,),
        in_specs=[pl.BlockSpec((16, d), lambda l: (0, 0)), pl.BlockSpec((None, 16, n), lambda l: (l, 0, 0)),
                  pl.BlockSpec((None, d, n), lambda l: (l, 0, 0))],
        out_specs=[pl.BlockSpec((None, d, n), lambda l: (l, 0, 0)), pl.BlockSpec((16, d), lambda l: (0, 0))],
        out_shape=[jax.ShapeDtypeStruct((2, d, n), F32), jax.ShapeDtypeStruct((16, d), F32)],
    )(cond_raw, dms, mw)


def _lb_fwd(hgrn_lb, *, name):
    def body(a_ref, o_ref):
        a0, a1 = a_ref[0:1, :], a_ref[1:2, :]
        m = jnp.maximum(a0, a1)
        e0, e1 = jnp.exp(a0 - m), jnp.exp(a1 - m)
        o_ref[...] = e0 / (e0 + e1)

    return _pcall(body, name=name, out_shape=jax.ShapeDtypeStruct((1, hgrn_lb.shape[1]), F32))(hgrn_lb)


PACK_TILES = ('l0n1', 'l0n2', 'l1n1', 'l1n2', 'fin', 'gq', 'gk', 'gain', 'dlb_f', 'dlb_b', 'sink')
PACK_ROW = {nm: 8 * i for i, nm in enumerate(PACK_TILES)}
MOD_SOURCE = ((('l0n1', 0), ('l0n1', 1), ('l0n2', 2), ('l0n2', 0), ('l0n2', 1), ('l1n1', 2)),
              (('l1n1', 0), ('l1n1', 1), ('l1n2', 2), ('l1n2', 0), ('l1n2', 1), ('fin', 2)))


def _small_finalize(gath, lb_pad, *, name):
    d = gath.shape[2]

    def body(g_ref, lb_ref, small_ref, glb_ref, gmb_ref, dm_ref):
        tot = g_ref[0]
        for e in range(1, 8):
            tot = tot + g_ref[e]

        def row(nm, r=0):
            return tot[PACK_ROW[nm] + r:PACK_ROW[nm] + r + 1, :]

        for k, nm in enumerate(('l0n1', 'l0n2', 'l1n1', 'l1n2')):
            small_ref[k:k + 1, :] = row(nm, 3) + row(nm, 7)
        for k, nm in ((4, 'gq'), (5, 'gk')):
            small_ref[k:k + 1, :] = row(nm) + pltpu.roll(row(nm), d - 64, 1)
        small_ref[6:7, :] = row('gain')
        small_ref[7:8, :] = row('sink')
        lbv = lb_ref[...]
        g0 = (row('dlb_f') + row('dlb_b')) * lbv * (1.0 - lbv)
        glb_ref[...] = jnp.zeros_like(glb_ref)
        glb_ref[0:1, :] = g0
        glb_ref[1:2, :] = -g0
        dm_ref[...] = jnp.zeros_like(dm_ref)
        for l in range(2):
            for part in range(6):
                nm, r = MOD_SOURCE[l][part]
                gmb_ref[l * 6 + part:l * 6 + part + 1, :] = row(nm, r) + row(nm, r + 4)
                rl = PACK_ROW[nm] + r + 4
                for e in range(8):
                    dm_ref[l, part, e:e + 1, :] = g_ref[e, rl:rl + 1, :]
                dm_ref[l, part, 8:9, :] = row(nm, r)

    return _pcall(
        body, name=name,
        out_shape=[jax.ShapeDtypeStruct((8, d), F32), jax.ShapeDtypeStruct((8, d), F32),
                   jax.ShapeDtypeStruct((12, d), F32), jax.ShapeDtypeStruct((2, 6, 16, d), F32)],
    )(gath, lb_pad)


def _cctx_grad(gath, c_ctx2, *, name):
    def body(g_ref, c_ref, o_ref):
        tot = ((g_ref[0, 0:1, :] + g_ref[2, 0:1, :]) + g_ref[4, 0:1, :]) + g_ref[6, 0:1, :]
        cv = c_ref[...]
        s = _sigmoid(cv)
        o_ref[...] = tot * (s * (1.0 + cv * (1.0 - s)))

    return _pcall(body, name=name, out_shape=jax.ShapeDtypeStruct(c_ctx2.shape, F32))(gath, c_ctx2)


def _row_block(r, c, limit=256 * 1024):
    best = None
    for br in range(16, r + 1, 16):
        if r % br == 0 and br * c <= limit:
            best = br
    return best if best is not None else r


def _sum4(own, landed, core, *, name):
    _, r, c = own.shape
    br = _row_block(r, c, 512 * 1024)

    def body(core_ref, own_ref, land_ref, o_ref):
        s = 2 * lax.axis_index("x") + lax.axis_index("y")
        p = [jnp.where(s == k, own_ref[k], land_ref[k]).astype(F32) for k in range(4)]
        o_ref[...] = ((p[0] + p[1]) + p[2]) + p[3]

    blk = pl.BlockSpec((4, br, c), lambda i, core_ref: (0, i, 0))
    spec = pltpu.PrefetchScalarGridSpec(
        num_scalar_prefetch=1, grid=(r // br,), in_specs=[blk, blk],
        out_specs=pl.BlockSpec((None, br, c), lambda i, core_ref: (core_ref[0], i, 0)))
    return _pcall(body, name=name, grid_spec=spec, out_shape=jax.ShapeDtypeStruct((2, r, c), F32))(core, own, landed)


def _exchange_halves(arrs, *, name):
    n = len(arrs)

    def body(*refs):
        ins, outs = refs[:n], refs[n:2 * n]
        send_sems, recv_sems = refs[2 * n:]
        ax, ay, ac = _place()
        cps = [pltpu.make_async_remote_copy(src_ref=ins[a].at[ac], dst_ref=outs[a].at[ac], send_sem=send_sems.at[a],
                                            recv_sem=recv_sems.at[a], device_id=(ax, ay, 1 - ac),
                                            device_id_type=MESH) for a in range(n)]
        for cp in cps:
            cp.start()
        for a in range(n):
            pltpu.make_async_remote_copy(src_ref=ins[a].at[ac], dst_ref=outs[a].at[1 - ac], send_sem=send_sems.at[a],
                                         recv_sem=recv_sems.at[a], device_id=(ax, ay, ac),
                                         device_id_type=MESH).wait_recv()
        for cp in cps:
            cp.wait_send()

    hbm = pl.BlockSpec(memory_space=pl.ANY)
    return _pcall(
        body, name=name, in_specs=[hbm] * n, out_specs=[hbm] * n,
        out_shape=[jax.ShapeDtypeStruct(a.shape, a.dtype) for a in arrs],
        input_output_aliases={a: a for a in range(n)},
        scratch_shapes=[pltpu.SemaphoreType.DMA((n,))] * 2,
    )(*arrs)


def _add2(a, b, *, name):
    r, c = a.shape
    br = _row_block(r, c, 1024 * 1024)

    def body(a_ref, b_ref, o_ref):
        o_ref[...] = (a_ref[...].astype(F32) + b_ref[...].astype(F32)).astype(BF16)

    blk = pl.BlockSpec((br, c), lambda i: (i, 0))
    return _pcall(body, name=name, grid=(r // br,), in_specs=[blk, blk], out_specs=blk,
                  out_shape=jax.ShapeDtypeStruct((r, c), BF16))(a, b)


def _adam(w, gs, m, v, *, name):
    r, c = w.shape
    br = _row_block(r, c)
    ng = len(gs)
    c1 = 1.0 - ADAM_B1 ** ADAM_STEP
    c2 = 1.0 - ADAM_B2 ** ADAM_STEP

    def body(*refs):
        w_ref, m_ref, v_ref = refs[0], refs[1 + ng], refs[2 + ng]
        outs = refs[3 + ng:]
        g = refs[1][...]
        for k in range(1, ng):
            g = g + refs[1 + k][...]
        mn = ADAM_B1 * m_ref[...] + (1.0 - ADAM_B1) * g
        vn = ADAM_B2 * v_ref[...] + (1.0 - ADAM_B2) * (g * g)
        if ng > 1:
            outs[0][...] = g
        d_out, m_out, v_out = outs[-3:]
        m_out[...] = mn
        v_out[...] = vn
        d_out[...] = -ADAM_LR * ((mn / c1) / (jnp.sqrt(vn / c2) + ADAM_EPS) + ADAM_WD * w_ref[...])

    blk = pl.BlockSpec((br, c), lambda i: (i, 0))
    nout = 4 if ng > 1 else 3
    res = _pcall(body, name=name, grid=(r // br,), in_specs=[blk] * (3 + ng), out_specs=[blk] * nout,
                 out_shape=[jax.ShapeDtypeStruct((r, c), F32)] * nout)(w, *gs, m, v)
    return list(res) if ng > 1 else [gs[0]] + list(res)


def _grad_halves(name, g, ac):
    if name.endswith('_in'):
        n = g.shape[1] // 4
        if name == 'ffn_in':
            assert n == FFN_BK
        order = _ffn_order(g.shape[1]) if name == 'ffn_in' else range(4)
        v = jnp.stack([g[:, b * n:(b + 1) * n] for b in order])
        per = [v[:, :g.shape[0] // 2], v[:, g.shape[0] // 2:]]
    else:
        k4, n = g.shape
        v = g.reshape(4, 2, k4 // 8, n)
        per = [v[:, 0], v[:, 1]]
    first = ac == 0
    return _bf(jnp.where(first, per[0], per[1])), _bf(jnp.where(first, per[1], per[0]))


class _GradReducer:
    def __init__(self):
        self.flight = {}

    @staticmethod
    def _plan(m, sending, refs):
        ax, ay, ac = _place()
        s = 2 * ax + ay
        out = []
        for a in range(m):
            for dx, dy in _CHIP_FLIPS:
                px, py = lax.rem(ax + dx, 2), lax.rem(ay + dy, 2)
                sp = 2 * px + py
                out.append((refs[a].at[sp], refs[m + a].at[s if sending else sp], (px, py, ac)))
        return out

    def start(self, grp, grads):
        ac = lax.axis_index("c")
        names = list(grads)
        halves = [_grad_halves(nm.rstrip('01'), grads[nm], ac) for nm in names]
        theirs = _to_sibling([h[1] for h in halves], name='swap_core_halves_' + grp)
        pair = [_add2(h[0].reshape(-1, b.shape[-1]), b.reshape(-1, b.shape[-1]), name='add_cores').reshape(b.shape)
                for h, b in zip(halves, theirs)]
        m = len(names)
        land = [lax.empty(a.shape, a.dtype) for a in pair]
        sends, recvs, bufs, token = _split_start(pair + land, functools.partial(self._plan, m, True), 3 * m,
                                                 name='scatter_' + grp + '_start')
        self.flight[grp] = (names, sends, recvs, bufs)
        return token

    def finish(self, grp, after):
        names, sends, recvs, bufs = self.flight.pop(grp)
        m = len(names)
        bufs = _split_wait(bufs, sends, recvs, functools.partial(self._plan, m, False), after,
                           name='scatter_' + grp + '_wait')
        core = lax.axis_index("c").astype(jnp.int32).reshape(1)
        sums = [_sum4(p, l, core, name='sum_chips') for p, l in zip(bufs[:m], bufs[m:])]
        both = _exchange_halves(sums, name='gather_core_halves_' + grp)
        return {nm: g.reshape(-1, g.shape[-1]) for nm, g in zip(names, both)}


def _from_shards(name, g):
    _, r, n = g.shape
    if name == 'ffn_in':
        assert n == FFN_BK
        v = g.reshape(4, 2, r // 2, n)
        return jnp.concatenate([v[b] for b in _ffn_order(4 * n)], axis=-1)
    if name == 'ffn_out':
        return g.reshape(4, 2, r // 2, n).transpose(1, 0, 2, 3).reshape(2, 2 * r, n)
    if name in ('even_in', 'odd_in'):
        return jnp.concatenate([g[b] for b in range(4)], axis=-1)
    return g.reshape(4 * r, n)


def kernel(x, c, ctx, c_ctx, mod_w, mod_b, norm_g, ffn_w_in, ffn_w_out, even_w_in, even_w_out, attn_qk_norm_g, attn_sink, hgrn_out_norm_g, hgrn_lb, odd_w_in, odd_w_out, loss_target, m_c_ctx, m_mod_w, m_mod_b, m_norm_g, m_ffn_w_in, m_ffn_w_out, m_even_w_in, m_even_w_out, m_attn_qk_norm_g, m_attn_sink, m_hgrn_out_norm_g, m_hgrn_lb, m_odd_w_in, m_odd_w_out, v_c_ctx, v_mod_w, v_mod_b, v_norm_g, v_ffn_w_in, v_ffn_w_out, v_even_w_in, v_even_w_out, v_attn_qk_norm_g, v_attn_sink, v_hgrn_out_norm_g, v_hgrn_lb, v_odd_w_in, v_odd_w_out):
    d = x.shape[-1]
    lc = ctx.shape[1]
    assert lc == TM and d == 1024
    ax, ay, ac = _place()
    s = 2 * ax + ay
    me = 4 * ax + 2 * ay + ac
    nmod = mod_w.shape[2]

    def pad8(v):
        return jnp.pad(v, ((0, 8 - v.shape[0]), (0, 0)))

    pack = jnp.concatenate([pad8(c), pad8(norm_g.reshape(1, d))], axis=0)
    g1 = _ag8(pack, name='gather_cond')
    c_all = g1[:, 0, :]
    ng = g1[0::2, 8, :].reshape(4, 2, 2, d // 4).transpose(1, 2, 0, 3).reshape(4, d)

    cond_raw = jnp.concatenate([c_all, pad8(c_ctx.reshape(1, d))], axis=0)
    mb_sh = lax.dynamic_slice_in_dim(mod_b, s * nmod, nmod, axis=1).reshape(2, 1, nmod)
    mpart = _mod_fwd(cond_raw, mod_w, mb_sh, name='mod_fwd')
    g3 = _ag8(mpart.reshape(32, nmod), name='gather_mods')
    mods_full = g3[0::2].reshape(4, 2, 16, nmod).transpose(1, 2, 0, 3).reshape(2, 16, 4 * nmod)
    m_lat = lax.dynamic_index_in_dim(mods_full, me, axis=1, keepdims=False)
    mods = jnp.stack([mods_full[:, 8], m_lat], axis=1).reshape(24, d)

    names = ['ffn_in', 'ffn_out', 'even_in', 'even_out', 'odd_in', 'odd_out']
    shards = [_bf(v.reshape(-1, v.shape[-1])) for v in (ffn_w_in, ffn_w_out, even_w_in, even_w_out, odd_w_in, odd_w_out)]
    shards, mods = lax.optimization_barrier((shards, mods))
    reducer = _GradReducer()
    wsrc = _GatheredWeights(dict(zip(names, shards)), reducer)

    lb = _lb_fwd(hgrn_lb, name='hgrn_lower_bound')
    small = dict(gq=jnp.tile(attn_qk_norm_g[0, 0], 2).reshape(1, 128), gk=jnp.tile(attn_qk_norm_g[0, 1], 2).reshape(1, 128),
                 sink=attn_sink[0], gain=hgrn_out_norm_g, lb=lb)
    x0 = jnp.concatenate([ctx[0], x[0]], axis=0)
    mods = mods + wsrc.token[0, 0]
    loss_t, dx0, grads, sums = _local_step(x0, loss_target[0], mods, ng, wsrc, small)
    loss = lax.psum(loss_t[0, 0], ("x", "y", "c"))
    grad_x = dx0[None]

    def tile(v):
        return jnp.pad(v, ((0, 8 - v.shape[0]), (0, d - v.shape[1])))

    sums = dict(sums, sink=sums['sink'][:, 0].reshape(1, 8))
    g4 = _ag8(jnp.concatenate([tile(sums[nm]) for nm in PACK_TILES], axis=0), name='gather_row_sums')
    small_g, glb, gmb, dmat = _small_finalize(g4, tile(lb)[0:1], name='small_grads')
    dms = lax.dynamic_slice_in_dim(dmat.transpose(0, 2, 1, 3).reshape(2, 16, 6 * d), s * nmod, nmod, axis=2)
    g_mod_w, dcond = _mod_bwd(cond_raw, dms, mod_w, name='mod_bwd')
    g5 = _ag8(dcond[8:16], name='gather_dcond')
    g_c_ctx = _cctx_grad(g5, c_ctx.reshape(8, d // 8).reshape(1, d), name='c_ctx_grad')

    late = {nm: grads[nm] for nm in ('even_in', 'even_out')}
    late, g_c_ctx = lax.optimization_barrier((late, g_c_ctx))
    token = reducer.start('late', late)
    full = reducer.finish('early', token)

    def upd(wv, gs, mv, vv, name):
        shp = wv.shape
        c2 = shp[-1]
        out = _adam(wv.reshape(-1, c2), [g.reshape(-1, c2) for g in gs], mv.reshape(-1, c2), vv.reshape(-1, c2), name=name)
        return [o.reshape(shp) for o in out]

    res = {}
    res['c_ctx'] = upd(c_ctx.reshape(8, d // 8), [g_c_ctx.reshape(8, d // 8)], m_c_ctx.reshape(8, d // 8), v_c_ctx.reshape(8, d // 8), 'adam_c_ctx')
    res['c_ctx'] = [o.reshape(d) for o in res['c_ctx']]
    res['mod_w'] = upd(mod_w, [g_mod_w], m_mod_w, v_mod_w, 'adam_mod_w')
    res['mod_b'] = upd(mod_b, [gmb.reshape(2, 6 * d)], m_mod_b, v_mod_b, 'adam_mod_b')
    g_ng = lax.dynamic_slice_in_dim(small_g[0:4].reshape(2, 2, d), s * (d // 4), d // 4, axis=2)
    res['norm_g'] = upd(norm_g, [g_ng], m_norm_g, v_norm_g, 'adam_norm_g')
    g_qk = jnp.stack([small_g[4, 0:64], small_g[5, 0:64]]).reshape(1, 2, 64)
    res['attn_qk_norm_g'] = upd(attn_qk_norm_g, [g_qk], m_attn_qk_norm_g, v_attn_qk_norm_g, 'adam_qk_gain')
    res['attn_sink'] = upd(attn_sink, [small_g[7, 0:8].reshape(1, 8)], m_attn_sink, v_attn_sink, 'adam_sink')
    res['hgrn_out_norm_g'] = upd(hgrn_out_norm_g, [small_g[6, 0:128].reshape(1, 128)], m_hgrn_out_norm_g, v_hgrn_out_norm_g, 'adam_head_gain')
    res['hgrn_lb'] = upd(hgrn_lb, [glb[0:2, 0:hgrn_lb.shape[1]]], m_hgrn_lb, v_hgrn_lb, 'adam_hgrn_lb')
    res['odd_w_in'] = upd(odd_w_in, [full['odd_in']], m_odd_w_in, v_odd_w_in, 'adam_odd_in')
    res['odd_w_out'] = upd(odd_w_out, [full['odd_out']], m_odd_w_out, v_odd_w_out, 'adam_odd_out')
    full.update(reducer.finish('mid', res['odd_w_in'][1]))
    g_ffn_in = jnp.concatenate([full['ffn_in0'], full['ffn_in1']], axis=0)
    g_ffn_out = jnp.concatenate([full['ffn_out0'], full['ffn_out1']], axis=0)
    res['ffn_w_in'] = upd(ffn_w_in, [g_ffn_in], m_ffn_w_in, v_ffn_w_in, 'adam_ffn_in')
    res['ffn_w_out'] = upd(ffn_w_out, [g_ffn_out], m_ffn_w_out, v_ffn_w_out, 'adam_ffn_out')
    full.update(reducer.finish('late', res['ffn_w_in'][1]))
    res['even_w_in'] = upd(even_w_in, [full['even_in']], m_even_w_in, v_even_w_in, 'adam_even_in')
    res['even_w_out'] = upd(even_w_out, [full['even_out']], m_even_w_out, v_even_w_out, 'adam_even_out')

    order = ['c_ctx', 'mod_w', 'mod_b', 'norm_g', 'ffn_w_in', 'ffn_w_out', 'even_w_in', 'even_w_out',
             'attn_qk_norm_g', 'attn_sink', 'hgrn_out_norm_g', 'hgrn_lb', 'odd_w_in', 'odd_w_out']
    outs = [loss, grad_x]
    for k in range(4):
        outs += [res[nm][k] for nm in order]
    return tuple(outs)
```

```python
import functools
import math

import numpy as np
import jax
import jax.numpy as jnp
from jax import lax
from jax.experimental import pallas as pl
from jax.experimental.pallas import tpu as pltpu

F32 = jnp.float32
BF16 = jnp.bfloat16
EPS = 1e-6
TM = 256
CHUNK = 64
QB = 256
WINDOW = 128
NEG = -1e30
MESH = pl.DeviceIdType.MESH

ADAM_LR, ADAM_B1, ADAM_B2, ADAM_EPS, ADAM_WD, ADAM_STEP = 0.001, 0.9, 0.999, 1e-08, 0.01, 10


def _pcall(body, **kw):
    return pl.pallas_call(body, **kw)


def _pick(n, cap):
    best = None
    for m in range(128, min(n, cap) + 1, 128):
        if n % m == 0:
            best = m
    assert best is not None, (n, cap)
    return best


def _bf(x):
    return x.astype(BF16)


def _dot(a, b):
    return jnp.dot(_bf(a), _bf(b), preferred_element_type=F32)


def _dot_nt(a, b):
    return lax.dot_general(_bf(a), _bf(b), (((1,), (1,)), ((), ())), preferred_element_type=F32)


def _dot_tn(a, b):
    return lax.dot_general(_bf(a), _bf(b), (((0,), (0,)), ((), ())), preferred_element_type=F32)


def _dot_exact(a, b):
    return jnp.dot(a, b, preferred_element_type=F32, precision=lax.Precision.HIGHEST)


def _sigmoid(x):
    return 1.0 / (1.0 + jnp.exp(-x))


def _iota(shape, dim):
    return lax.broadcasted_iota(jnp.int32, shape, dim)


def _parts(a):
    parts = list(a) if isinstance(a, (list, tuple)) else [a]
    widths = [p.shape[1] for p in parts]
    return parts, widths, [sum(widths[:i]) for i in range(len(parts))]


def _mm_nn(a, b, *, lead=None, out_dtype=F32, name):
    parts, widths, offs = _parts(a)
    m, k = parts[0].shape[0], sum(widths)
    n = b.shape[-1]
    bm = 1408 if (m % 1408 == 0 and k <= 1024) else (768 if m % 768 == 0 else TM)
    bn = _pick(n, 1024) if n % 512 == 0 else _pick(n, 1664)

    def body(*refs):
        b_ref, o_ref = refs[-2], refs[-1]
        acc = None
        for p_ref, w, off in zip(refs, widths, offs):
            term = _dot(p_ref[...], b_ref[off:off + w, :])
            acc = term if acc is None else acc + term
        o_ref[...] = acc.astype(o_ref.dtype)

    if lead is None:
        b_spec = pl.BlockSpec((k, bn), lambda i, j: (0, j))
    else:
        b_spec = pl.BlockSpec((None, k, bn), lambda i, j: (lead, 0, j))
    return _pcall(
        body, name=name, grid=(m // bm, n // bn),
        in_specs=[pl.BlockSpec((bm, w), lambda i, j: (i, 0)) for w in widths] + [b_spec],
        out_specs=pl.BlockSpec((bm, bn), lambda i, j: (i, j)),
        out_shape=jax.ShapeDtypeStruct((m, n), out_dtype),
    )(*parts, b)


def _mm_nt(a, b, *, lead=None, name):
    parts, widths, offs = _parts(a)
    m, n = parts[0].shape[0], sum(widths)
    k = b.shape[-2]
    bm = 1408 if (m % 1408 == 0 and n <= 1024) else (768 if m % 768 == 0 else TM)
    bk = _pick(k, 1024 if n <= 2048 else 512)

    def body(*refs):
        b_ref, o_ref = refs[-2], refs[-1]
        acc = None
        for p_ref, w, off in zip(refs, widths, offs):
            term = _dot_nt(p_ref[...], b_ref[:, off:off + w])
            acc = term if acc is None else acc + term
        o_ref[...] = acc

    if lead is None:
        b_spec = pl.BlockSpec((bk, n), lambda i, j: (j, 0))
    else:
        b_spec = pl.BlockSpec((None, bk, n), lambda i, j: (lead, j, 0))
    return _pcall(
        body, name=name, grid=(m // bm, k // bk),
        in_specs=[pl.BlockSpec((bm, w), lambda i, j: (i, 0)) for w in widths] + [b_spec],
        out_specs=pl.BlockSpec((bm, bk), lambda i, j: (i, j)),
        out_shape=jax.ShapeDtypeStruct((m, k), F32),
    )(*parts, b)


def _mm_tn(a, b, *, name):
    a_parts, a_w, a_off = _parts(a)
    b_parts, b_w, b_off = _parts(b)
    t, k, n = a_parts[0].shape[0], sum(a_w), sum(b_w)
    bt = 1408 if t % 1408 == 0 else (768 if t % 768 == 0 else TM)
    bk = _pick(k, 1536) if len(a_parts) == 1 else math.gcd(*a_w)
    if len(b_parts) == 1:
        bn = _pick(n, 1024) if n % 1024 == 0 or n < 1664 else _pick(n, 1664)
    else:
        bn = math.gcd(*b_w)
    na, nbp = len(a_parts), len(b_parts)

    def block_range(off, w, blk):
        return off // blk, w // blk

    def body(*refs):
        a_refs, b_refs, o_ref = refs[:na], refs[na:na + nbp], refs[-1]
        i, j = pl.program_id(0), pl.program_id(1)

        @pl.when(pl.program_id(2) == 0)
        def _():
            o_ref[...] = jnp.zeros_like(o_ref)

        def add(a_ref, b_ref):
            o_ref[...] += _dot_tn(a_ref[...], b_ref[...])

        for pa in range(na):
            sa, ca = block_range(a_off[pa], a_w[pa], bk)
            for pb in range(nbp):
                sb, cb = block_range(b_off[pb], b_w[pb], bn)
                if na == 1 and nbp == 1:
                    add(a_refs[0], b_refs[0])
                else:
                    pl.when((i >= sa) & (i < sa + ca) & (j >= sb) & (j < sb + cb))(
                        functools.partial(add, a_refs[pa], b_refs[pb]))

    def spec(off, w, blk, axis):
        s0, cnt = block_range(off, w, blk)

        def index(i, j, s):
            g = i if axis == 0 else j
            inside = (g >= s0) & (g < s0 + cnt)
            return (jnp.where(inside, s, 0), jnp.clip(g - s0, 0, cnt - 1))

        return pl.BlockSpec((bt, blk), index)

    return _pcall(
        body, name=name, grid=(k // bk, n // bn, t // bt),
        in_specs=[spec(o, w, bk, 0) for o, w in zip(a_off, a_w)] + [spec(o, w, bn, 1) for o, w in zip(b_off, b_w)],
        out_specs=pl.BlockSpec((bk, bn), lambda i, j, s: (i, j)),
        out_shape=jax.ShapeDtypeStruct((k, n), F32),
    )(*a_parts, *b_parts)


def _mod_row(mods_ref, lat, idx):
    return jnp.where(lat, mods_ref[idx + 6:idx + 7, :], mods_ref[idx:idx + 1, :])


def _row_step(t):
    return 768 if t % 768 == 0 else TM


def _row_fwd(x, mods, *, y=None, gate=None, g=None, shift=None, scale=None, name):
    t, d = x.shape
    has_y, has_n = y is not None, g is not None
    rt = _row_step(t)

    def body(*refs):
        refs = list(refs)
        x_ref, mods_ref = refs[0], refs[1]
        pos = 2
        if has_y:
            y_ref = refs[pos]; pos += 1
        if has_n:
            g_ref = refs[pos]; pos += 1
        outs = refs[pos:]
        for sub in range(rt // TM):
            rows = slice(sub * TM, (sub + 1) * TM)
            lat = pl.program_id(0) * (rt // TM) + sub > 0
            x1 = x_ref[rows, :]
            o = 0
            if has_y:
                x1 = x1 + _mod_row(mods_ref, lat, gate) * y_ref[rows, :]
                outs[o][rows, :] = x1; o += 1
            if has_n:
                rs = lax.rsqrt(jnp.mean(x1 * x1, axis=-1, keepdims=True) + EPS)
                hn = x1 * rs * g_ref[...]
                h = hn * (1.0 + _mod_row(mods_ref, lat, scale)) + _mod_row(mods_ref, lat, shift)
                outs[o][rows, :] = h.astype(BF16)

    row = pl.BlockSpec((rt, d), lambda i: (i, 0))
    ins, specs = [x, mods], [row, pl.BlockSpec(mods.shape, lambda i: (0, 0))]
    if has_y:
        ins.append(y); specs.append(row)
    if has_n:
        ins.append(g.reshape(1, d)); specs.append(pl.BlockSpec((1, d), lambda i: (0, 0)))
    out_shape, out_specs = [], []
    if has_y:
        out_shape.append(jax.ShapeDtypeStruct((t, d), F32)); out_specs.append(row)
    if has_n:
        out_shape.append(jax.ShapeDtypeStruct((t, d), BF16)); out_specs.append(row)
    res = _pcall(body, name=name, grid=(t // rt,), in_specs=specs, out_specs=out_specs,
                 out_shape=out_shape)(*ins)
    return res


def _acc_row(ref, r, val):
    ref[r:r + 1, :] += val


def _row_final(x, z, mods, target, *, gate, name):
    t, d = x.shape

    def body(x_ref, mods_ref, z_ref, t_ref, loss_ref, dx_ref, dz_ref, sums_ref):
        i = pl.program_id(0)
        lat = i > 0

        @pl.when(i == 0)
        def _():
            loss_ref[...] = jnp.zeros_like(loss_ref)
            sums_ref[...] = jnp.zeros_like(sums_ref)

        gt = _mod_row(mods_ref, lat, gate)
        zz = z_ref[...]
        yv = x_ref[...] + gt * zz
        keep = jnp.where(lat, 1.0, 0.0).astype(F32)
        diff = (yv - t_ref[...]) * keep
        part = jnp.sum(jnp.sum(diff * diff, axis=0, keepdims=True), axis=1, keepdims=True)
        loss_ref[...] += part * (0.5 / d)
        dy = diff * (1.0 / d)
        dx_ref[...] = dy
        dz_ref[...] = (gt * dy).astype(BF16)
        _acc_row(sums_ref, 6, jnp.sum(dy * zz, axis=0, keepdims=True))

    row = pl.BlockSpec((TM, d), lambda i: (i, 0))
    return _pcall(
        body, name=name, grid=(t // TM,),
        in_specs=[row, pl.BlockSpec(mods.shape, lambda i: (0, 0)), row,
                  pl.BlockSpec((TM, d), lambda i: (jnp.maximum(i - 1, 0), 0))],
        out_specs=[pl.BlockSpec((8, 128), lambda i: (0, 0)), row, row,
                   pl.BlockSpec((8, d), lambda i: (0, 0))],
        out_shape=[jax.ShapeDtypeStruct((8, 128), F32), jax.ShapeDtypeStruct((t, d), F32),
                   jax.ShapeDtypeStruct((t, d), BF16), jax.ShapeDtypeStruct((8, d), F32)],
    )(x, mods, z, target)


def _row_bwd(xn, dxo, dh, mods, g, *, shift, scale, y=None, gate=None, latent_only=False, name):
    t, d = xn.shape
    has_y = y is not None

    def body(*refs):
        refs = list(refs)
        x_ref, dxo_ref, dh_ref, mods_ref, g_ref = refs[:5]
        pos = 5
        if has_y:
            y_ref = refs[pos]; pos += 1
        dx_ref = refs[pos]; pos += 1
        if has_y:
            dy_ref = refs[pos]; pos += 1
        sums_ref = refs[pos]
        i = pl.program_id(0)

        @pl.when(i == 0)
        def _():
            sums_ref[...] = jnp.zeros_like(sums_ref)

        def add_sums(vals, base):
            for r, v in enumerate(vals):
                if v is not None:
                    _acc_row(sums_ref, base + r, v)

        gv = g_ref[...]
        for sub in range(rt // TM):
            rows = slice(sub * TM, (sub + 1) * TM)
            lat = i * (rt // TM) + sub > 0
            x1 = x_ref[rows, :]
            rs = lax.rsqrt(jnp.mean(x1 * x1, axis=-1, keepdims=True) + EPS)
            xh = x1 * rs
            dhv = dh_ref[rows, :]
            dn = dhv * (1.0 + _mod_row(mods_ref, lat, scale))
            dxh = dn * gv
            dx = dxo_ref[rows, :] + rs * (dxh - xh * jnp.mean(dxh * xh, axis=-1, keepdims=True))
            dx_ref[rows, :] = dx
            vals = [jnp.sum(dhv, axis=0, keepdims=True),
                    jnp.sum(dhv * (xh * gv), axis=0, keepdims=True),
                    None,
                    jnp.sum(dn * xh, axis=0, keepdims=True)]
            if has_y:
                dy_ref[rows, :] = (_mod_row(mods_ref, lat, gate) * dx).astype(BF16)
                vals[2] = jnp.sum(dx * y_ref[rows, :], axis=0, keepdims=True)
            if sub == 0:
                pl.when(i == 0)(functools.partial(add_sums, vals, 0))
                pl.when(i > 0)(functools.partial(add_sums, vals, 4))
            else:
                add_sums(vals, 4)

    rt = TM if latent_only else _row_step(t)
    row = pl.BlockSpec((rt, d), lambda i: (i, 0))
    ins = [xn, dxo, dh, mods, g.reshape(1, d)]
    specs = [row, row, row, pl.BlockSpec(mods.shape, lambda i: (0, 0)), pl.BlockSpec((1, d), lambda i: (0, 0))]
    if latent_only:
        out_shape = [jax.ShapeDtypeStruct((t - TM, d), F32)]
        out_specs = [pl.BlockSpec((TM, d), lambda i: (jnp.maximum(i - 1, 0), 0))]
    else:
        out_shape, out_specs = [jax.ShapeDtypeStruct((t, d), F32)], [row]
    if has_y:
        ins.append(y); specs.append(row)
        out_shape.append(jax.ShapeDtypeStruct((t, d), BF16)); out_specs.append(row)
    out_shape.append(jax.ShapeDtypeStruct((8, d), F32))
    out_specs.append(pl.BlockSpec((8, d), lambda i: (0, 0)))
    return _pcall(body, name=name, grid=(t // rt,), in_specs=specs, out_specs=out_specs,
                  out_shape=out_shape)(*ins)


FFN_BK = 1408


FFN_SUB = 256


def _ffn_order(n2):
    nb = n2 // (2 * FFN_BK)
    return [h * nb + j for j in range(nb) for h in (0, 1)]


def _ffn_interleave(w):
    return jnp.concatenate([w[..., b * FFN_BK:(b + 1) * FFN_BK] for b in _ffn_order(w.shape[-1])], axis=-1)


def _ffn_deinterleave(w):
    order = _ffn_order(w.shape[-1])
    return jnp.concatenate([w[..., order.index(b) * FFN_BK:(order.index(b) + 1) * FFN_BK]
                            for b in range(len(order))], axis=-1)


def _big_tile(t):
    return 768 if t % 768 == 0 else TM


def _ffn_in(h, w, *, lead, name):
    t, d = h.shape
    n2 = w.shape[-1]
    bm, bk = _big_tile(t), FFN_BK

    def body(h_ref, w_ref, u_ref, a_ref):
        hb = h_ref[...]
        for c0 in range(0, bk, FFN_SUB):
            c1 = min(c0 + FFN_SUB, bk)
            ug = _dot(hb, w_ref[:, c0:c1]).astype(BF16)
            uu = _dot(hb, w_ref[:, bk + c0:bk + c1]).astype(BF16)
            u_ref[:, c0:c1] = ug
            u_ref[:, bk + c0:bk + c1] = uu
            gv, up = ug.astype(F32), uu.astype(F32)
            a_ref[:, c0:c1] = (gv * _sigmoid(gv) * up).astype(BF16)

    return _pcall(
        body, name=name, grid=(t // bm, n2 // (2 * bk)),
        in_specs=[pl.BlockSpec((bm, d), lambda i, j: (i, 0)),
                  pl.BlockSpec((None, d, 2 * bk), lambda i, j: (lead, 0, j))],
        out_specs=[pl.BlockSpec((bm, 2 * bk), lambda i, j: (i, j)), pl.BlockSpec((bm, bk), lambda i, j: (i, j))],
        out_shape=[jax.ShapeDtypeStruct((t, n2), BF16), jax.ShapeDtypeStruct((t, n2 // 2), BF16)],
    )(h, w)


def _ffn_dx(dz, w_out, u, *, lead, name):
    t, d = dz.shape
    n2 = u.shape[1]
    bm, bk = _big_tile(t), FFN_BK

    def body(dz_ref, w_ref, u_ref, du_ref):
        dzb = dz_ref[...]
        for c0 in range(0, bk, FFN_SUB):
            c1 = min(c0 + FFN_SUB, bk)
            da = _dot_nt(dzb, w_ref[c0:c1, :])
            gv, up = u_ref[:, c0:c1].astype(F32), u_ref[:, bk + c0:bk + c1].astype(F32)
            s = _sigmoid(gv)
            du_ref[:, c0:c1] = (da * up * (s * (1.0 + gv * (1.0 - s)))).astype(BF16)
            du_ref[:, bk + c0:bk + c1] = (da * gv * s).astype(BF16)

    ublk = pl.BlockSpec((bm, 2 * bk), lambda i, j: (i, j))
    return _pcall(
        body, name=name, grid=(t // bm, n2 // (2 * bk)),
        in_specs=[pl.BlockSpec((bm, d), lambda i, j: (i, 0)),
                  pl.BlockSpec((None, bk, d), lambda i, j: (lead, j, 0)), ublk],
        out_specs=ublk, out_shape=jax.ShapeDtypeStruct((t, n2), BF16),
    )(dz, w_out, u)


def _lane(shape):
    return _iota(shape, len(shape) - 1)


def _pair_norm(x, g):
    lo = _lane(x.shape) < 64
    x2 = x * x
    s_lo = jnp.sum(jnp.where(lo, x2, 0.0), axis=-1, keepdims=True)
    s_hi = jnp.sum(jnp.where(lo, 0.0, x2), axis=-1, keepdims=True)
    rs = lax.rsqrt(jnp.where(lo, s_lo, s_hi) * (1.0 / 64) + EPS)
    return x * rs, rs


def _pair_mean(v):
    lo = _lane(v.shape) < 64
    s_lo = jnp.sum(jnp.where(lo, v, 0.0), axis=-1, keepdims=True)
    s_hi = jnp.sum(jnp.where(lo, 0.0, v), axis=-1, keepdims=True)
    return jnp.where(lo, s_lo, s_hi) * (1.0 / 64)


def _rot64(x):
    r1 = pltpu.roll(x, 32, 1)
    r2 = pltpu.roll(x, 96, 1)
    even = ((_lane(x.shape) >> 5) & 1) == 0
    return jnp.where(even, -r2, r1)


def _rope64(x, cos, sin):
    return x * cos + _rot64(x) * sin


def _rope64_t(d, cos, sin):
    return d * cos - _rot64(d * sin)


def _kprep_fwd(p, gk, cos, sin, *, name):
    t = p.shape[0]

    def body(k_ref, g_ref, c_ref, s_ref, o_ref):
        xh, _ = _pair_norm(k_ref[...], None)
        o_ref[...] = _rope64(xh * g_ref[...], c_ref[...], s_ref[...])

    blk = pl.BlockSpec((TM, 128), lambda i: (i, 0))
    return _pcall(
        body, name=name, grid=(t // TM,),
        in_specs=[pl.BlockSpec((TM, 128), lambda i: (i, 4)), pl.BlockSpec((1, 128), lambda i: (0, 0)), blk, blk],
        out_specs=blk, out_shape=jax.ShapeDtypeStruct((t, 128), F32),
    )(p, gk, cos, sin)


def _kprep_bwd(p, gk, cos, sin, dkp, dv, *, name):
    t = p.shape[0]

    def body(k_ref, g_ref, c_ref, s_ref, dkp_ref, dv_ref, o_ref, dg_ref):
        @pl.when(pl.program_id(0) == 0)
        def _():
            dg_ref[...] = jnp.zeros_like(dg_ref)
        xh, rs = _pair_norm(k_ref[...], None)
        dn = _rope64_t(dkp_ref[...], c_ref[...], s_ref[...])
        _acc_row(dg_ref, 0, jnp.sum(dn * xh, axis=0, keepdims=True))
        dxh = dn * g_ref[...]
        o_ref[:, 0:128] = (rs * (dxh - xh * _pair_mean(dxh * xh))).astype(BF16)
        o_ref[:, 128:256] = dv_ref[...].astype(BF16)

    blk = pl.BlockSpec((TM, 128), lambda i: (i, 0))
    return _pcall(
        body, name=name, grid=(t // TM,),
        in_specs=[pl.BlockSpec((TM, 128), lambda i: (i, 4)), pl.BlockSpec((1, 128), lambda i: (0, 0)), blk, blk, blk, blk],
        out_specs=[pl.BlockSpec((TM, 256), lambda i: (i, 0)), pl.BlockSpec((8, 128), lambda i: (0, 0))],
        out_shape=[jax.ShapeDtypeStruct((t, 256), BF16), jax.ShapeDtypeStruct((8, 128), F32)],
    )(p, gk, cos, sin, dkp, dv)


def _attn_common(i, t, lc, kp_ref, v_ref):
    span = QB + 2 * WINDOW
    start = pl.multiple_of(jnp.clip(i * QB - WINDOW, lc, t - span), WINDOW)
    kall = jnp.concatenate([kp_ref[0:lc, :], kp_ref[pl.ds(start, span), :]], axis=0)
    vall = jnp.concatenate([v_ref[0:lc, :], v_ref[pl.ds(start, span), :]], axis=0)
    nk = lc + span
    col = _iota((QB, nk), 1)
    krow = jnp.where(col < lc, col, start + col - lc)
    qrow = i * QB + _iota((QB, nk), 0)
    valid = (col < lc) | ((qrow >= lc) & (krow >= lc) & (jnp.abs(krow - qrow) <= WINDOW))
    lo = _lane(kall.shape) < 64
    kroll, vroll = pltpu.roll(kall, 64, 1), pltpu.roll(vall, 64, 1)
    zero = jnp.zeros_like(kall)
    kvar = [[_bf(jnp.where(lo, kall, zero)), _bf(jnp.where(lo, zero, kroll))],
            [_bf(jnp.where(lo, kroll, zero)), _bf(jnp.where(lo, zero, kall))]]
    vvar = [[_bf(jnp.where(lo, vall, zero)), _bf(jnp.where(lo, zero, vroll))],
            [_bf(jnp.where(lo, vroll, zero)), _bf(jnp.where(lo, zero, vall))]]
    return start, valid, kvar, vvar


def _softmax_sink(s, valid, snk):
    s = jnp.where(valid, s, NEG)
    m = jnp.maximum(jnp.max(s, axis=-1, keepdims=True), snk)
    e = jnp.exp(s - m)
    es = jnp.exp(snk - m)
    inv = 1.0 / (jnp.sum(e, axis=-1, keepdims=True) + es)
    return e * inv, es * inv


def _attn_fwd(p, kp, gq, sink, cos, sin, *, lc, name):
    t = p.shape[0]
    scale = 64 ** -0.5

    def body(q_ref, kp_ref, v_ref, g_ref, sink_ref, c_ref, s_ref, o_ref):
        i = pl.program_id(0)
        _, valid, kvar, vvar = _attn_common(i, t, lc, kp_ref, v_ref)
        cosv, sinv, gv = c_ref[...], s_ref[...], g_ref[...]
        for j in range(4):
            xh, _ = _pair_norm(q_ref[:, 128 * j:128 * j + 128], None)
            q2 = _bf(_rope64(xh * gv, cosv, sinv) * scale)
            acc = jnp.zeros((QB, 128), F32)
            for half in range(2):
                s = _dot_nt(q2, kvar[j // 2][half])
                pr, _ = _softmax_sink(s, valid, sink_ref[2 * j + half])
                acc = acc + _dot(pr, vvar[j // 2][half])
            o_ref[:, 128 * j:128 * j + 128] = acc.astype(BF16)

    qblk = pl.BlockSpec((QB, 128), lambda i: (i, 0))
    return _pcall(
        body, name=name, grid=(t // QB,),
        in_specs=[pl.BlockSpec((QB, 512), lambda i: (i, 0)),
                  pl.BlockSpec((t, 128), lambda i: (0, 0)),
                  pl.BlockSpec((t, 128), lambda i: (0, 5)),
                  pl.BlockSpec((1, 128), lambda i: (0, 0)),
                  pl.BlockSpec(memory_space=pltpu.SMEM), qblk, qblk],
        out_specs=pl.BlockSpec((QB, 512), lambda i: (i, 0)),
        out_shape=jax.ShapeDtypeStruct((t, 512), BF16),
    )(p, kp, p, gq, sink, cos, sin)


def _attn_bwd(p, kp, gq, sink, cos, sin, dmix, *, lc, name):
    t = p.shape[0]
    scale = 64 ** -0.5
    span = QB + 2 * WINDOW

    def body(q_ref, kp_ref, v_ref, g_ref, sink_ref, c_ref, s_ref, do_ref,
             dq_ref, dk_ref, dv_ref, dg_ref, dsink_ref):
        i = pl.program_id(0)

        @pl.when(i == 0)
        def _():
            dk_ref[...] = jnp.zeros_like(dk_ref)
            dv_ref[...] = jnp.zeros_like(dv_ref)
            dg_ref[...] = jnp.zeros_like(dg_ref)
            dsink_ref[...] = jnp.zeros_like(dsink_ref)

        start, valid, kvar, vvar = _attn_common(i, t, lc, kp_ref, v_ref)
        cosv, sinv, gv = c_ref[...], s_ref[...], g_ref[...]
        nk = lc + span
        dkt = [jnp.zeros((64, nk), F32), jnp.zeros((64, nk), F32)]
        dvt = [jnp.zeros((64, nk), F32), jnp.zeros((64, nk), F32)]
        for j in range(4):
            kvh = j // 2
            xh, rs = _pair_norm(q_ref[:, 128 * j:128 * j + 128], None)
            q2 = _bf(_rope64(xh * gv, cosv, sinv) * scale)
            do2 = _bf(do_ref[:, 128 * j:128 * j + 128])
            dq2 = jnp.zeros((QB, 128), F32)
            for half in range(2):
                s = _dot_nt(q2, kvar[kvh][half])
                pr, ps = _softmax_sink(s, valid, sink_ref[2 * j + half])
                dp = _dot_nt(do2, vvar[kvh][half])
                delta = jnp.sum(pr * dp, axis=-1, keepdims=True)
                ds = pr * (dp - delta)
                dsk = jnp.sum(jnp.sum(-ps * delta, axis=0, keepdims=True), axis=1, keepdims=True)
                _acc_row(dsink_ref, 2 * j + half, jnp.broadcast_to(dsk, (1, 128)))
                dq2 = dq2 + _dot(ds, kvar[kvh][half])
                hrows = slice(64 * half, 64 * half + 64)
                dkt[kvh] = dkt[kvh] + _dot_tn(q2, ds)[hrows]
                dvt[kvh] = dvt[kvh] + _dot_tn(do2, pr)[hrows]
            dn = _rope64_t(dq2 * scale, cosv, sinv)
            _acc_row(dg_ref, 0, jnp.sum(dn * xh, axis=0, keepdims=True))
            dxh = dn * gv
            dq_ref[:, 128 * j:128 * j + 128] = (rs * (dxh - xh * _pair_mean(dxh * xh))).astype(BF16)
        dk_all = jnp.concatenate(dkt, axis=0).T
        dv_all = jnp.concatenate(dvt, axis=0).T
        dk_ref[0:lc, :] += dk_all[0:lc]
        dv_ref[0:lc, :] += dv_all[0:lc]
        dk_ref[pl.ds(start, span), :] += dk_all[lc:nk]
        dv_ref[pl.ds(start, span), :] += dv_all[lc:nk]

    qblk = pl.BlockSpec((QB, 128), lambda i: (i, 0))
    full = pl.BlockSpec((t, 128), lambda i: (0, 0))
    small = pl.BlockSpec((8, 128), lambda i: (0, 0))
    return _pcall(
        body, name=name, grid=(t // QB,),
        in_specs=[pl.BlockSpec((QB, 512), lambda i: (i, 0)), full,
                  pl.BlockSpec((t, 128), lambda i: (0, 5)),
                  pl.BlockSpec((1, 128), lambda i: (0, 0)),
                  pl.BlockSpec(memory_space=pltpu.SMEM), qblk, qblk,
                  pl.BlockSpec((QB, 512), lambda i: (i, 0))],
        out_specs=[pl.BlockSpec((QB, 512), lambda i: (i, 0)), full, full, small, small],
        out_shape=[jax.ShapeDtypeStruct((t, 512), BF16), jax.ShapeDtypeStruct((t, 128), F32),
                   jax.ShapeDtypeStruct((t, 128), F32), jax.ShapeDtypeStruct((8, 128), F32),
                   jax.ShapeDtypeStruct((8, 128), F32)],
    )(p, kp, p, gq, sink, cos, sin, dmix)


def _tri(rev):
    r, c = _iota((CHUNK, CHUNK), 0), _iota((CHUNK, CHUNK), 1)
    return (c >= r) if rev else (c <= r)


def _blk_map(nb, rev, backward):
    if not rev:
        return (lambda n: nb - 1 - n) if backward else (lambda n: n)
    if backward:
        return lambda n: jnp.where(n < nb - 1, n + 1, 0)
    return lambda n: jnp.where(n == 0, 0, nb - n)


def _chunk_order(rev, backward, nc=TM // CHUNK):
    order = list(range(nc))
    return order[::-1] if (rev != backward) else order


def _hgrn_gates(qraw, fraw, lb):
    sq = _sigmoid(qraw)
    sf = _sigmoid(fraw)
    f = lb + (1.0 - lb) * sf
    return qraw * sq, 1.0 - f, jnp.log(f), sq, sf, f


HGRN_HP = 4


def _chunk_cumsum(x, rev):
    n = x.shape[0]
    pos = _iota(x.shape, 0) & (CHUNK - 1)
    s = 1
    while s < CHUNK:
        if rev:
            x = x + jnp.where(pos < CHUNK - s, pltpu.roll(x, n - s, 0), 0.0)
        else:
            x = x + jnp.where(pos >= s, pltpu.roll(x, s, 0), 0.0)
        s *= 2
    return x


def _block_terms(lf, rev):
    b = _chunk_cumsum(lf, rev)
    mid, last = (CHUNK // 2 - 1, 0) if rev else (CHUNK // 2, CHUNK - 1)

    def chunk_row(off):
        return jnp.concatenate([jnp.broadcast_to(b[c * CHUNK + off:c * CHUNK + off + 1, :], (CHUNK, b.shape[1]))
                                for c in range(TM // CHUNK)], axis=0)

    r, bl = chunk_row(mid), chunk_row(last)
    return _tri(rev), jnp.exp(b - r), jnp.exp(r - b), jnp.exp(b), jnp.exp(bl - b), jnp.exp(bl)


def _headnorm_apply(o, gv, gain):
    n = o * lax.rsqrt(jnp.mean(o * o, axis=-1, keepdims=True) + EPS)
    if gain is not None:
        n = n * gain
    return (n * (gv * _sigmoid(gv))).astype(BF16)


def _headnorm_grad(o, gv, dy, gain):
    rs = lax.rsqrt(jnp.mean(o * o, axis=-1, keepdims=True) + EPS)
    xh = o * rs
    n = xh * gain if gain is not None else xh
    sg = _sigmoid(gv)
    dn = dy * (gv * sg)
    dg = (dy * n * (sg * (1.0 + gv * (1.0 - sg)))).astype(BF16)
    dgain = jnp.sum(dn * xh, axis=0, keepdims=True)
    dxh = dn * gain if gain is not None else dn
    return rs * (dxh - xh * jnp.mean(dxh * xh, axis=-1, keepdims=True)), dg, dgain


def _hgrn_cols(bmap, n2, c0):
    return [pl.BlockSpec((TM, 256), lambda h, n, b=b: (bmap(n), c0 // 2 + h * n2 + b)) for b in range(n2)]


def _head_cols(refs, hh):
    return refs[hh // 2][:, 128 * (hh % 2):128 * (hh % 2) + 128]


def _hgrn_fwd(p, lb, *, rev, name, ofw=None, gain=None):
    t = p.shape[0]
    nb, nc = t // TM, TM // CHUNK
    bmap = _blk_map(nb, rev, False)
    fcol = 14 if rev else 10
    fused = ofw is not None

    n2 = HGRN_HP // 2

    def body(*refs):
        q_refs, f_refs, v_refs, lb_ref = refs[:n2], refs[n2:2 * n2], refs[2 * n2:3 * n2], refs[3 * n2]
        rest = refs[3 * n2 + 1:]
        if fused:
            ofw_ref, g_refs, gain_ref = rest[0], rest[1:1 + n2], rest[1 + n2]
            o_ref, sh_ref, mix_ref, st = rest[2 + n2:]
        else:
            o_ref, sh_ref, st = rest

        @pl.when(pl.program_id(1) == 0)
        def _():
            st[...] = jnp.zeros_like(st)
        for hh in range(HGRN_HP):
            ln = slice(128 * hh, 128 * hh + 128)
            q, k, lf, _, _, _ = _hgrn_gates(_head_cols(q_refs, hh), _head_cols(f_refs, hh), lb_ref[:, ln])
            tri, eq, ek, ei, eki, eb = _block_terms(lf, rev)
            qe, ke, qi, ki, vb = _bf(q * eq), _bf(k * ek), _bf(q * ei), _bf(k * eki), _bf(_head_cols(v_refs, hh))
            intra = []
            for cc in range(nc):
                rows = slice(cc * CHUNK, (cc + 1) * CHUNK)
                a = jnp.where(tri, _dot_nt(qe[rows], ke[rows]), 0.0)
                intra.append(_dot(a, vb[rows]))
            s = st[hh]
            for cc in _chunk_order(rev, False):
                rows = slice(cc * CHUNK, (cc + 1) * CHUNK)
                sh_ref[hh, cc] = s
                o_ref[rows, ln] = intra[cc] + _dot_nt(qi[rows], s)
                s = s * eb[cc * CHUNK:cc * CHUNK + 1, :] + _dot_tn(vb[rows], ki[rows])
            st[hh] = s
            if fused:
                osum = o_ref[:, ln] + ofw_ref[:, ln]
                o_ref[:, ln] = osum
                mix_ref[:, ln] = _headnorm_apply(osum, _head_cols(g_refs, hh), gain_ref[...])

    hp, wd = HGRN_HP, 128 * HGRN_HP
    col = functools.partial(_hgrn_cols, bmap, n2)
    oblk = pl.BlockSpec((TM, wd), lambda h, n: (bmap(n), h))
    ins = [p] * (3 * n2) + [lb]
    specs = col(6) + col(fcol) + col(18) + [pl.BlockSpec((1, wd), lambda h, n: (0, h))]
    out_specs = [oblk, pl.BlockSpec((hp, nc, 128, 128), lambda h, n: (h, bmap(n), 0, 0))]
    out_shape = [jax.ShapeDtypeStruct((t, 512), F32), jax.ShapeDtypeStruct((4, t // CHUNK, 128, 128), F32)]
    if fused:
        ins += [ofw] + [p] * n2 + [gain]
        specs += [oblk] + col(22) + [pl.BlockSpec((1, 128), lambda h, n: (0, 0))]
        out_specs.append(oblk)
        out_shape.append(jax.ShapeDtypeStruct((t, 512), BF16))
    return _pcall(body, name=name, grid=(4 // hp, nb), in_specs=specs, out_specs=out_specs, out_shape=out_shape,
                  scratch_shapes=[pltpu.VMEM((hp, 128, 128), F32)])(*ins)


def _hgrn_bwd(p, lb, sh, do, prev, *, rev, name, head=None):
    t = p.shape[0]
    nb, nc = t // TM, TM // CHUNK
    bmap = _blk_map(nb, rev, True)
    fcol = 14 if rev else 10
    has_prev = prev is not None
    odt = BF16
    fused = head is not None

    n2 = HGRN_HP // 2

    def body(*refs):
        refs = list(refs)
        q_refs, f_refs, v_refs = refs[:n2], refs[n2:2 * n2], refs[2 * n2:3 * n2]
        lb_ref, sh_ref = refs[3 * n2], refs[3 * n2 + 1]
        pos = 3 * n2 + 2
        if fused:
            osum_ref, g_refs, dmix_ref, gain_ref = refs[pos], refs[pos + 1:pos + 1 + n2], refs[pos + 1 + n2], refs[pos + 2 + n2]
            pos += 3 + n2
        else:
            do_ref = refs[pos]
            pos += 1
        if has_prev:
            pq_ref, pv_ref = refs[pos], refs[pos + 1]
            pos += 2
        dq_ref, df_ref, dv_ref, dlb_ref = refs[pos:pos + 4]
        pos += 4
        if fused:
            do_out, dg_ref, dgain_ref = refs[pos:pos + 3]
            pos += 3
        dst = refs[pos]

        @pl.when(pl.program_id(1) == 0)
        def _():
            dst[...] = jnp.zeros_like(dst)
            dlb_ref[...] = jnp.zeros_like(dlb_ref)

        if fused:
            @pl.when((pl.program_id(0) == 0) & (pl.program_id(1) == 0))
            def _():
                dgain_ref[...] = jnp.zeros_like(dgain_ref)

        cat = functools.partial(jnp.concatenate, axis=0)
        for hh in range(HGRN_HP):
            ln = slice(128 * hh, 128 * hh + 128)
            lbv = lb_ref[:, ln]
            qraw, fraw = _head_cols(q_refs, hh), _head_cols(f_refs, hh)
            q, k, lf, sq, sf, f = _hgrn_gates(qraw, fraw, lbv)
            tri, eq, ek, ei, eki, eb = _block_terms(lf, rev)
            qe, ke, qi, ki = q * eq, k * ek, q * ei, k * eki
            if fused:
                dov, dg, dgain = _headnorm_grad(osum_ref[:, ln], _head_cols(g_refs, hh), dmix_ref[:, ln], gain_ref[...])
                do_out[:, ln] = _bf(dov)
                dg_ref[:, ln] = dg
                _acc_row(dgain_ref, 0, dgain)
            else:
                dov = do_ref[:, ln]
            qeb, keb, qib, kib, vb, dob = _bf(qe), _bf(ke), _bf(qi), _bf(ki), _bf(_head_cols(v_refs, hh)), _bf(dov)
            dv, dqe, dke, dqi = [None] * nc, [None] * nc, [None] * nc, [None] * nc
            for cc in range(nc):
                rows = slice(cc * CHUNK, (cc + 1) * CHUNK)
                a = jnp.where(tri, _dot_nt(qeb[rows], keb[rows]), 0.0)
                da = jnp.where(tri, _dot_nt(dob[rows], vb[rows]), 0.0)
                dv[cc] = _dot_tn(a, dob[rows])
                dqe[cc], dke[cc] = _dot(da, keb[rows]), _dot_tn(da, qeb[rows])
                dqi[cc] = _dot(dob[rows], sh_ref[hh, cc])
            dki, dbl = [None] * nc, [None] * nc
            ds = dst[hh]
            for cc in _chunk_order(rev, True):
                rows = slice(cc * CHUNK, (cc + 1) * CHUNK)
                ebc = eb[cc * CHUNK:cc * CHUNK + 1, :]
                dv[cc] = dv[cc] + _dot_nt(kib[rows], ds)
                dki[cc] = _dot(vb[rows], ds)
                dbl[cc] = jnp.broadcast_to(jnp.sum(dki[cc] * ki[rows], axis=0, keepdims=True)
                                           + jnp.sum(ds * sh_ref[hh, cc], axis=0, keepdims=True) * ebc, (CHUNK, 128))
                ds = ds * ebc + _dot_tn(dob[rows], qib[rows])
            dst[hh] = ds
            dqe, dke, dqi, dki, dv, dbl = cat(dqe), cat(dke), cat(dqi), cat(dki), cat(dv), cat(dbl)
            dq = dqe * eq + dqi * ei
            dk = dke * ek + dki * eki
            last = 0 if rev else CHUNK - 1
            db = dqe * qe - dke * ke + dqi * qi - dki * ki
            db = db + jnp.where((_iota(db.shape, 0) & (CHUNK - 1)) == last, dbl, 0.0)
            dlf = _chunk_cumsum(db, not rev)
            dqr = dq * (sq * (1.0 + qraw * (1.0 - sq)))
            dfv = dlf / f - dk
            dfr = dfv * (1.0 - lbv) * (sf * (1.0 - sf))
            dlb_ref[:, ln] += jnp.sum(dfv * (1.0 - sf), axis=0, keepdims=True)
            if has_prev:
                dqr = dqr + pq_ref[:, ln]
                dv = dv + pv_ref[:, ln]
            dq_ref[:, ln] = dqr.astype(odt)
            df_ref[:, ln] = dfr.astype(odt)
            dv_ref[:, ln] = dv.astype(odt)

    hp, wd = HGRN_HP, 128 * HGRN_HP
    col = functools.partial(_hgrn_cols, bmap, n2)
    oblk = pl.BlockSpec((TM, wd), lambda h, n: (bmap(n), h))
    ins = [p] * (3 * n2) + [lb, sh]
    specs = col(6) + col(fcol) + col(18) + [pl.BlockSpec((1, wd), lambda h, n: (0, h)),
                                            pl.BlockSpec((hp, nc, 128, 128), lambda h, n: (h, bmap(n), 0, 0))]
    if fused:
        osum, dmix, gain = head
        ins += [osum] + [p] * n2 + [dmix, gain]
        specs += [oblk] + col(22) + [pl.BlockSpec((TM, wd), lambda h, n: (bmap(n), 4 // hp + h)),
                                     pl.BlockSpec((1, 128), lambda h, n: (0, 0))]
    else:
        ins.append(do); specs.append(oblk)
    if has_prev:
        ins += list(prev); specs += [oblk, oblk]
    out_specs = [oblk, oblk, oblk, pl.BlockSpec((1, wd), lambda h, n: (0, h))]
    out_shape = [jax.ShapeDtypeStruct((t, 512), odt)] * 3 + [jax.ShapeDtypeStruct((1, 512), F32)]
    if fused:
        out_specs += [oblk, oblk, pl.BlockSpec((8, 128), lambda h, n: (0, 0))]
        out_shape += [jax.ShapeDtypeStruct((t, 512), BF16), jax.ShapeDtypeStruct((t, 512), BF16),
                      jax.ShapeDtypeStruct((8, 128), F32)]
    return _pcall(body, name=name, grid=(4 // hp, nb), in_specs=specs, out_specs=out_specs, out_shape=out_shape,
                  scratch_shapes=[pltpu.VMEM((hp, 128, 128), F32)])(*ins)


def _rope256(x, cos, sin):
    x1, x2 = x[:, 0:128], x[:, 128:256]
    return jnp.concatenate([x1 * cos - x2 * sin, x2 * cos + x1 * sin], axis=-1)


def _rope256_t(d, cos, sin):
    d1, d2 = d[:, 0:128], d[:, 128:256]
    return jnp.concatenate([d1 * cos + d2 * sin, d2 * cos - d1 * sin], axis=-1)


RET_DK, RET_DV, RET_H = 256, 512, 4
RET_KSCALE = RET_DK ** -0.5
RCH = TM
RET_HP = 4


def _ret_terms(lg, rev):
    r, c = _iota((RCH, RCH), 0), _iota((RCH, RCH), 1)
    rel = ((c - r) if rev else (r - c)).astype(F32)
    dmat = jnp.where(rel >= 0, jnp.exp(lg[:, 0:1] * jnp.maximum(rel, 0.0)), 0.0)
    pos = _iota((RCH, 1), 0).astype(F32)
    cnt = (RCH - pos) if rev else (pos + 1.0)
    ei = jnp.exp(lg * cnt)
    eki = jnp.exp(lg * (RCH - cnt))
    eb = jnp.exp(lg * float(RCH))
    return dmat, ei, eki, eb


def _ret_fwd(p, lgt, cos, sin, *, rev, name, ofw=None):
    t = p.shape[0]
    nb, nc = t // TM, TM // RCH
    bmap = _blk_map(nb, rev, False)
    fused = ofw is not None

    def body(*refs):
        q_ref, k_ref, v_ref, lg_ref, c_ref, s_ref = refs[:6]
        if fused:
            ofw_ref, g_ref, o_ref, sh_ref, mix_ref, st = refs[6:]
        else:
            o_ref, sh_ref, st = refs[6:]

        @pl.when(pl.program_id(1) == 0)
        def _():
            st[...] = jnp.zeros_like(st)
        for hh in range(RET_HP):
            qc, vc = slice(RET_DK * hh, RET_DK * (hh + 1)), slice(RET_DV * hh, RET_DV * (hh + 1))
            dmat, ei, eki, eb = _ret_terms(lg_ref[hh], rev)
            for cc in _chunk_order(rev, False, nc):
                rows = slice(cc * RCH, (cc + 1) * RCH)
                cosv, sinv = c_ref[rows, :], s_ref[rows, :]
                q = _rope256(q_ref[rows, qc].astype(F32), cosv, sinv)
                k = _rope256(k_ref[rows, qc].astype(F32), cosv, sinv) * RET_KSCALE
                v = v_ref[rows, vc]
                s0 = st[hh]
                sh_ref[hh, cc] = s0.astype(BF16)
                a = _dot_nt(q, k) * dmat
                o = _dot(a, v) + _dot_nt(q * ei, s0)
                st[hh] = s0 * eb + _dot_tn(v, k * eki)
                if fused:
                    o = o + ofw_ref[rows, vc]
                    mix_ref[rows, vc] = _headnorm_apply(o, g_ref[rows, vc].astype(F32), None)
                o_ref[rows, vc] = o

    hp = RET_HP
    tab = pl.BlockSpec((TM, 128), lambda h, n: (bmap(n), 0))
    oblk = pl.BlockSpec((TM, hp * RET_DV), lambda h, n: (bmap(n), h))
    ins = [p, p, p, lgt, cos, sin]
    specs = [pl.BlockSpec((TM, hp * RET_DK), lambda h, n: (bmap(n), h)),
             pl.BlockSpec((TM, hp * RET_DK), lambda h, n: (bmap(n), RET_H // hp + h)),
             pl.BlockSpec((TM, hp * RET_DV), lambda h, n: (bmap(n), RET_H // hp + h)),
             pl.BlockSpec((hp, 1, RET_DK), lambda h, n: (h, 0, 0)), tab, tab]
    out_specs = [oblk, pl.BlockSpec((hp, nc, RET_DV, RET_DK), lambda h, n: (h, bmap(n), 0, 0))]
    out_shape = [jax.ShapeDtypeStruct((t, RET_H * RET_DV), F32),
                 jax.ShapeDtypeStruct((RET_H, t // RCH, RET_DV, RET_DK), BF16)]
    if fused:
        ins += [ofw, p]
        specs += [oblk, pl.BlockSpec((TM, hp * RET_DV), lambda h, n: (bmap(n), 2 * RET_H // hp + h))]
        out_specs.append(oblk)
        out_shape.append(jax.ShapeDtypeStruct((t, RET_H * RET_DV), BF16))
    return _pcall(body, name=name, grid=(RET_H // hp, nb), in_specs=specs, out_specs=out_specs, out_shape=out_shape,
                  scratch_shapes=[pltpu.VMEM((hp, RET_DV, RET_DK), F32)])(*ins)


def _ret_bwd(p, lgt, cos, sin, sh, do, prev, *, rev, name, head=None):
    t = p.shape[0]
    nb, nc = t // TM, TM // RCH
    bmap = _blk_map(nb, rev, True)
    has_prev = prev is not None
    odt = BF16
    fused = head is not None

    def body(*refs):
        refs = list(refs)
        q_ref, k_ref, v_ref, lg_ref, c_ref, s_ref, sh_ref = refs[:7]
        if fused:
            osum_ref, g_ref, dmix_ref = refs[7:10]
            pos = 10
        else:
            do_ref = refs[7]
            pos = 8
        if has_prev:
            pq_ref, pk_ref, pv_ref = refs[pos:pos + 3]
            pos += 3
        dq_ref, dk_ref, dv_ref = refs[pos:pos + 3]
        pos += 3
        if fused:
            do_out, dg_ref = refs[pos:pos + 2]
            pos += 2
        dst = refs[pos]

        @pl.when(pl.program_id(1) == 0)
        def _():
            dst[...] = jnp.zeros_like(dst)

        for hh in range(RET_HP):
            qc, vc = slice(RET_DK * hh, RET_DK * (hh + 1)), slice(RET_DV * hh, RET_DV * (hh + 1))
            dmat, ei, eki, eb = _ret_terms(lg_ref[hh], rev)
            for cc in _chunk_order(rev, True, nc):
                rows = slice(cc * RCH, (cc + 1) * RCH)
                cosv, sinv = c_ref[rows, :], s_ref[rows, :]
                q = _rope256(q_ref[rows, qc].astype(F32), cosv, sinv)
                k = _rope256(k_ref[rows, qc].astype(F32), cosv, sinv) * RET_KSCALE
                v = v_ref[rows, vc]
                if fused:
                    dov, dg, _ = _headnorm_grad(osum_ref[rows, vc], g_ref[rows, vc].astype(F32), dmix_ref[rows, vc], None)
                    do_out[rows, vc] = _bf(dov)
                    dg_ref[rows, vc] = dg
                else:
                    dov = do_ref[rows, vc]
                s0 = sh_ref[hh, cc]
                dsc = dst[hh]
                qi, ki = q * ei, k * eki
                a = _dot_nt(q, k) * dmat
                da = _dot_nt(dov, v) * dmat
                dv = _dot_tn(a, dov) + _dot_nt(ki, dsc)
                dqs = _dot(da, k) + _dot(dov, s0) * ei
                dks = _dot_tn(da, q) + _dot(v, dsc) * eki
                dst[hh] = dsc * eb + _dot_tn(dov, qi)
                dq = _rope256_t(dqs, cosv, sinv)
                dk = _rope256_t(dks * RET_KSCALE, cosv, sinv)
                if has_prev:
                    dq = dq + pq_ref[rows, qc]
                    dk = dk + pk_ref[rows, qc]
                    dv = dv + pv_ref[rows, vc]
                dq_ref[rows, qc] = dq.astype(odt)
                dk_ref[rows, qc] = dk.astype(odt)
                dv_ref[rows, vc] = dv.astype(odt)

    hp = RET_HP
    tab = pl.BlockSpec((TM, 128), lambda h, n: (bmap(n), 0))
    qblk = pl.BlockSpec((TM, hp * RET_DK), lambda h, n: (bmap(n), h))
    vblk = pl.BlockSpec((TM, hp * RET_DV), lambda h, n: (bmap(n), h))
    ins = [p, p, p, lgt, cos, sin, sh]
    specs = [qblk, pl.BlockSpec((TM, hp * RET_DK), lambda h, n: (bmap(n), RET_H // hp + h)),
             pl.BlockSpec((TM, hp * RET_DV), lambda h, n: (bmap(n), RET_H // hp + h)),
             pl.BlockSpec((hp, 1, RET_DK), lambda h, n: (h, 0, 0)), tab, tab,
             pl.BlockSpec((hp, nc, RET_DV, RET_DK), lambda h, n: (h, bmap(n), 0, 0))]
    if fused:
        osum, dmix = head
        ins += [osum, p, dmix]
        specs += [vblk, pl.BlockSpec((TM, hp * RET_DV), lambda h, n: (bmap(n), 2 * RET_H // hp + h)), vblk]
    else:
        ins.append(do); specs.append(vblk)
    if has_prev:
        ins += list(prev); specs += [qblk, qblk, vblk]
    out_specs = [qblk, qblk, vblk]
    out_shape = [jax.ShapeDtypeStruct((t, RET_H * RET_DK), odt), jax.ShapeDtypeStruct((t, RET_H * RET_DK), odt),
                 jax.ShapeDtypeStruct((t, RET_H * RET_DV), odt)]
    if fused:
        out_specs += [vblk, vblk]
        out_shape += [jax.ShapeDtypeStruct((t, RET_H * RET_DV), BF16), jax.ShapeDtypeStruct((t, RET_H * RET_DV), BF16)]
    return _pcall(body, name=name, grid=(RET_H // hp, nb), in_specs=specs, out_specs=out_specs, out_shape=out_shape,
                  scratch_shapes=[pltpu.VMEM((hp, RET_DV, RET_DK), F32)])(*ins)


def _rope_tables(lc, l):
    tt = jnp.arange(l)
    row, colp = (tt // 64).astype(F32), (tt % 64).astype(F32)
    inv = 10000.0 ** (-jnp.arange(16, dtype=F32) / 16)
    ang = jnp.concatenate([row[:, None] * inv, colp[:, None] * inv], axis=-1)
    ang = jnp.concatenate([jnp.zeros((lc, 32), F32), ang], axis=0)
    acos, asin = jnp.tile(jnp.cos(ang), (1, 4)), jnp.tile(jnp.sin(ang), (1, 4))
    theta = 1.0 / (10000.0 ** jnp.linspace(0.0, 1.0, 128, dtype=F32))
    rang = jnp.arange(l, dtype=F32)[:, None] * theta
    rang = jnp.concatenate([jnp.zeros((lc, 128), F32), rang], axis=0)
    return acos, asin, jnp.cos(rang), jnp.sin(rang)


class _Weights:
    def __init__(self, w):
        self.w = w

    def landed(self, grp, after):
        pass

    def full(self, grp, after):
        return self.w

    def send_grads(self, grp, grads):
        return jnp.zeros((8, 128), F32)


def _local_step(x0, target, mods, ng, wsrc, small):
    t, d = x0.shape
    l = target.shape[0]
    lc = t - l
    acos, asin, rcos, rsin = _rope_tables(lc, l)
    lg_fw = jnp.log(1.0 - 2.0 ** (-5.0 - jnp.arange(RET_H, dtype=F32)))
    lgt_fw = jnp.broadcast_to(lg_fw[:, None, None], (RET_H, 1, RET_DK))
    lgt_bw = jnp.broadcast_to(lg_fw[::-1][:, None, None], (RET_H, 1, RET_DK))
    gq, gk, sink, gain, lb = small['gq'], small['gk'], small['sink'], small['gain'], small['lb']

    (h1,) = _row_fwd(x0, mods, g=ng[0], shift=0, scale=1, name='l0_norm1')
    wsrc.landed('even', h1)
    w = dict(wsrc.full('even', h1))
    p0 = _mm_nn(h1, w['even_in'], name='l0_in')
    kp = _kprep_fwd(p0, gk, acos, asin, name='l0_kprep')
    att = _attn_fwd(p0, kp, gq, sink, acos, asin, lc=lc, name='l0_attn')
    wsrc.landed('ffn', att)
    hof, hsf = _hgrn_fwd(p0, lb, rev=False, name='l0_hgrn_f')
    wsrc.landed('odd', hof)
    hos, hsb, bmix = _hgrn_fwd(p0, lb, rev=True, name='l0_hgrn_b', ofw=hof, gain=gain)
    mix0 = [att, bmix]
    y0 = _mm_nn(mix0, w['even_out'], name='l0_out')
    x1, h2 = _row_fwd(x0, mods, y=y0, gate=2, g=ng[1], shift=3, scale=4, name='l0_norm2')
    w.update(wsrc.full('ffn', h2))
    u0, a0 = _ffn_in(h2, w['ffn_in'], lead=0, name='ffn_in')
    z0 = _mm_nn(a0, w['ffn_out'], lead=0, name='ffn_out')
    x2, h3 = _row_fwd(x1, mods, y=z0, gate=5, g=ng[2], shift=12, scale=13, name='l1_norm1')
    w.update(wsrc.full('odd', h3))
    p1 = _mm_nn(h3, w['odd_in'], out_dtype=BF16, name='l1_in')
    rof, rsf = _ret_fwd(p1, lgt_fw, rcos, rsin, rev=False, name='l1_ret_f')
    ros, rsb, mix1 = _ret_fwd(p1, lgt_bw, rcos, rsin, rev=True, name='l1_ret_b', ofw=rof)
    y1 = _mm_nn(mix1, w['odd_out'], name='l1_out')
    x3, h4 = _row_fwd(x2, mods, y=y1, gate=14, g=ng[3], shift=15, scale=16, name='l1_norm2')
    u1, a1 = _ffn_in(h4, w['ffn_in'], lead=1, name='ffn_in')
    z1 = _mm_nn(a1, w['ffn_out'], lead=1, name='ffn_out')
    loss, dx4, dz1, s_fin = _row_final(x3, z1, mods, target, gate=17, name='loss')

    du1 = _ffn_dx(dz1, w['ffn_out'], u1, lead=1, name='ffn_out_dx')
    g_ffn_out1 = _mm_tn(a1, dz1, name='ffn_out_dw')
    dh4 = _mm_nt(du1, w['ffn_in'], lead=1, name='ffn_in_dx')
    g_ffn_in1 = _mm_tn(h4, du1, name='ffn_in_dw')
    dx3, dy1, s_l1n2 = _row_bwd(x3, dx4, dh4, mods, ng[3], shift=15, scale=16, y=y1, gate=14, name='l1_norm2_bwd')
    dmix1 = _mm_nt(dy1, w['odd_out'], name='l1_out_dx')
    g_odd_out = _mm_tn(mix1, dy1, name='l1_out_dw')
    rdq, rdk, rdv, rdo, rdg = _ret_bwd(p1, lgt_fw, rcos, rsin, rsf, None, None, rev=False, name='l1_ret_f_bwd',
                                       head=(ros, dmix1))
    rdq, rdk, rdv = _ret_bwd(p1, lgt_bw, rcos, rsin, rsb, rdo, (rdq, rdk, rdv), rev=True, name='l1_ret_b_bwd')
    dp1 = [rdq, rdk, rdv, rdg]
    dh3 = _mm_nt(dp1, w['odd_in'], name='l1_in_dx')
    g_odd_in = _mm_tn(h3, dp1, name='l1_in_dw')
    mods = mods + wsrc.send_grads('early', dict(ffn_in1=g_ffn_in1, ffn_out1=g_ffn_out1, odd_in=g_odd_in,
                                                odd_out=g_odd_out))[0, 0]
    dx2, dz0, s_l1n1 = _row_bwd(x2, dx3, dh3, mods, ng[2], shift=12, scale=13, y=z0, gate=5, name='l1_norm1_bwd')
    du0 = _ffn_dx(dz0, w['ffn_out'], u0, lead=0, name='ffn_out_dx')
    g_ffn_out0 = _mm_tn(a0, dz0, name='ffn_out_dw')
    dh2 = _mm_nt(du0, w['ffn_in'], lead=0, name='ffn_in_dx')
    g_ffn_in0 = _mm_tn(h2, du0, name='ffn_in_dw')
    mods = mods + wsrc.send_grads('mid', dict(ffn_in0=g_ffn_in0, ffn_out0=g_ffn_out0))[0, 0]
    dx1, dy0, s_l0n2 = _row_bwd(x1, dx2, dh2, mods, ng[1], shift=3, scale=4, y=y0, gate=2, name='l0_norm2_bwd')
    dmix0 = _mm_nt(dy0, w['even_out'], name='l0_out_dx')
    g_even_out = _mm_tn(mix0, dy0, name='l0_out_dw')
    hq, hff, hv, dlb_f, hdo, hdg, s_gain = _hgrn_bwd(p0, lb, hsf, None, None, rev=False, name='l0_hgrn_f_bwd',
                                                     head=(hos, dmix0, gain))
    hq, hfb, hv, dlb_b = _hgrn_bwd(p0, lb, hsb, hdo, (hq, hv), rev=True, name='l0_hgrn_b_bwd')
    adq, dkp, adv, s_gq, s_sink = _attn_bwd(p0, kp, gq, sink, acos, asin, dmix0, lc=lc, name='l0_attn_bwd')
    dkv, s_gk = _kprep_bwd(p0, gk, acos, asin, dkp, adv, name='l0_kprep_bwd')
    dp0 = jnp.concatenate([adq, dkv, hq, _bf(hff), hfb, hv, hdg], axis=1)
    dh1 = _mm_nt(dp0, w['even_in'], name='l0_in_dx')
    g_even_in = _mm_tn(h1, dp0, name='l0_in_dw')
    dx0, s_l0n1 = _row_bwd(x0, dx1, dh1, mods, ng[0], shift=0, scale=1, latent_only=True, name='l0_norm1_bwd')

    grads = dict(ffn_in0=g_ffn_in0, ffn_in1=g_ffn_in1, ffn_out0=g_ffn_out0, ffn_out1=g_ffn_out1,
                 even_in=g_even_in, even_out=g_even_out, odd_in=g_odd_in, odd_out=g_odd_out)
    sums = dict(fin=s_fin, l1n2=s_l1n2, l1n1=s_l1n1, l0n2=s_l0n2, l0n1=s_l0n1, gain=s_gain, gq=s_gq, gk=s_gk,
                sink=s_sink, dlb_f=dlb_f, dlb_b=dlb_b)
    return loss, dx0, grads, sums


def _place():
    return lax.axis_index("x"), lax.axis_index("y"), lax.axis_index("c")


def _ag8(blk, *, name):
    r, c = blk.shape
    flips = [(dx, dy, dc) for dx in (0, 1) for dy in (0, 1) for dc in (0, 1) if (dx, dy, dc) != (0, 0, 0)]

    def body(x_ref, out_ref, send_sems, recv_sems, local_sem):
        ax, ay, ac = _place()
        me = 4 * ax + 2 * ay + ac
        mine = pltpu.make_async_copy(x_ref, out_ref.at[me], local_sem)
        mine.start()
        sent = []
        for k, (dx, dy, dc) in enumerate(flips):
            peer = (lax.rem(ax + dx, 2), lax.rem(ay + dy, 2), lax.rem(ac + dc, 2))
            cp = pltpu.make_async_remote_copy(src_ref=x_ref, dst_ref=out_ref.at[me], send_sem=send_sems.at[k],
                                              recv_sem=recv_sems.at[k], device_id=peer, device_id_type=MESH)
            cp.start()
            sent.append((cp, 4 * peer[0] + 2 * peer[1] + peer[2]))
        for k, (cp, pidx) in enumerate(sent):
            pltpu.make_async_remote_copy(src_ref=x_ref, dst_ref=out_ref.at[pidx], send_sem=send_sems.at[k],
                                         recv_sem=recv_sems.at[k], device_id=(ax, ay, ac),
                                         device_id_type=MESH).wait_recv()
        for cp, _ in sent:
            cp.wait_send()
        mine.wait()

    return _pcall(
        body, name=name,
        in_specs=[pl.BlockSpec(memory_space=pltpu.VMEM)],
        out_specs=pl.BlockSpec(memory_space=pltpu.VMEM),
        out_shape=jax.ShapeDtypeStruct((8, r, c), blk.dtype),
        scratch_shapes=[pltpu.SemaphoreType.DMA((7,)), pltpu.SemaphoreType.DMA((7,)), pltpu.SemaphoreType.DMA],
    )(blk)


_HBM = pl.BlockSpec(memory_space=pltpu.HBM)
_SEM = pl.BlockSpec(memory_space=pltpu.SEMAPHORE)
_DATAFLOW = pltpu.SideEffectType.DATAFLOW_SIDE_EFFECTING


def _split_start(bufs, plan, k, *, name):
    n = len(bufs)

    def body(*refs):
        ins, send_sems, recv_sems, token = refs[:n], refs[n], refs[n + 1], refs[2 * n + 2]
        for i, (src, dst, dev) in enumerate(plan(ins)):
            pltpu.make_async_remote_copy(src_ref=src, dst_ref=dst, send_sem=send_sems.at[i], recv_sem=recv_sems.at[i],
                                         device_id=dev, device_id_type=MESH).start()
        token[...] = jnp.zeros_like(token)

    res = _pcall(
        body, name=name,
        out_shape=(pltpu.SemaphoreType.DMA((k,)), pltpu.SemaphoreType.DMA((k,)),
                   *[pltpu.HBM(b.shape, b.dtype) for b in bufs], jax.ShapeDtypeStruct((8, 128), F32)),
        in_specs=[_HBM] * n, out_specs=(_SEM, _SEM, *[_HBM] * n, pl.BlockSpec(memory_space=pltpu.VMEM)),
        input_output_aliases={i: 2 + i for i in range(n)},
        compiler_params=pltpu.CompilerParams(has_side_effects=_DATAFLOW),
    )(*[pltpu.with_memory_space_constraint(b, pltpu.HBM) for b in bufs])
    return res[0], res[1], list(res[2:2 + n]), res[2 + n]


def _split_wait(bufs, send_sems, recv_sems, plan, after, *, name):
    n = len(bufs)

    def body(*refs):
        ins, ssem, rsem = refs[:n], refs[n], refs[n + 1]
        for i, (src, dst, dev) in enumerate(plan(ins)):
            cp = pltpu.make_async_remote_copy(src_ref=src, dst_ref=dst, send_sem=ssem.at[i], recv_sem=rsem.at[i],
                                              device_id=dev, device_id_type=MESH)
            cp.wait_send()
            cp.wait_recv()

    res = _pcall(
        body, name=name, out_shape=tuple(pltpu.HBM(b.shape, b.dtype) for b in bufs),
        in_specs=[_HBM] * n + [_SEM, _SEM, pl.BlockSpec(memory_space=pl.ANY)], out_specs=tuple([_HBM] * n),
        input_output_aliases={i: i for i in range(n)},
        compiler_params=pltpu.CompilerParams(has_side_effects=_DATAFLOW),
    )(*bufs, send_sems, recv_sems, after)
    return list(res)


_CHIP_FLIPS = [(1, 0), (0, 1), (1, 1)]


class _GatheredWeights:
    GROUPS = (('even', ('even_in', 'even_out')), ('ffn', ('ffn_in', 'ffn_out')), ('odd', ('odd_in', 'odd_out')))

    def __init__(self, shards, reducer):
        self.shards = shards
        self.send_grads = reducer.start
        self.ici, self.d2d, self.token = {}, {}, None
        for grp, names in self.GROUPS:
            src = [shards[nm].reshape(2, shards[nm].shape[0] // 2, shards[nm].shape[1]) for nm in names]
            land = [lax.empty((4,) + a.shape, a.dtype) for a in src]
            m = len(names)
            sends, recvs, bufs, token = _split_start(src + land, functools.partial(self._ici_plan, m, True), 4 * m,
                                                     name='gather_' + grp + '_ici_start')
            self.ici[grp] = (sends, recvs, bufs, m)
            self.token = token if self.token is None else self.token + token

    @staticmethod
    def _ici_plan(m, sending, refs):
        ax, ay, ac = _place()
        s = 2 * ax + ay
        out = []
        for a in range(m):
            for dx, dy in _CHIP_FLIPS:
                px, py = lax.rem(ax + dx, 2), lax.rem(ay + dy, 2)
                slot = s if sending else 2 * px + py
                out.append((refs[a].at[ac], refs[m + a].at[slot, ac], (px, py, ac)))
        for a in range(m):
            out.append((refs[a], refs[m + a].at[s], (ax, ay, 1 - ac)))
        return out

    @staticmethod
    def _d2d_plan(m, sending, refs):
        ax, ay, ac = _place()
        out = []
        for a in range(m):
            for dx, dy in _CHIP_FLIPS:
                sp = 2 * lax.rem(ax + dx, 2) + lax.rem(ay + dy, 2)
                out.append((refs[a].at[sp, ac], refs[a].at[sp, ac if sending else 1 - ac], (ax, ay, 1 - ac)))
        return out

    def landed(self, grp, after):
        sends, recvs, bufs, m = self.ici[grp]
        bufs = _split_wait(bufs, sends, recvs, functools.partial(self._ici_plan, m, False), after,
                           name='gather_' + grp + '_ici_wait')
        sends, recvs, land, _ = _split_start(bufs[m:], functools.partial(self._d2d_plan, m, True), 3 * m,
                                             name='gather_' + grp + '_d2d_start')
        self.d2d[grp] = (sends, recvs, land, m)

    def full(self, grp, after):
        sends, recvs, land, m = self.d2d[grp]
        land = _split_wait(land, sends, recvs, functools.partial(self._d2d_plan, m, False), after,
                           name='gather_' + grp + '_d2d_wait')
        names = dict(self.GROUPS)[grp]
        return {nm: _from_shards(nm, g.reshape((4,) + self.shards[nm].shape)) for nm, g in zip(names, land)}


def _to_sibling(arrs, *, name):
    n = len(arrs)

    def body(*refs):
        ins, outs = refs[:n], refs[n:2 * n]
        send_sems, recv_sems = refs[2 * n:]
        ax, ay, ac = _place()
        cps = [pltpu.make_async_remote_copy(src_ref=ins[a], dst_ref=outs[a], send_sem=send_sems.at[a],
                                            recv_sem=recv_sems.at[a], device_id=(ax, ay, 1 - ac),
                                            device_id_type=MESH) for a in range(n)]
        for cp in cps:
            cp.start()
        for cp in cps:
            cp.wait_recv()
        for cp in cps:
            cp.wait_send()

    hbm = pl.BlockSpec(memory_space=pl.ANY)
    return _pcall(
        body, name=name, in_specs=[hbm] * n, out_specs=[hbm] * n,
        out_shape=[jax.ShapeDtypeStruct(a.shape, a.dtype) for a in arrs],
        scratch_shapes=[pltpu.SemaphoreType.DMA((n,))] * 2,
    )(*arrs)


def _mod_fwd(cond_raw, mw, mb, *, name):
    _, d, n = mw.shape

    def body(c_ref, w_ref, b_ref, o_ref):
        cv = c_ref[...]
        o_ref[...] = _dot(cv * _sigmoid(cv), w_ref[...]) + b_ref[...]

    return _pcall(
        body, name=name, grid=(2,),
        in_specs=[pl.BlockSpec((16, d), lambda l: (0, 0)), pl.BlockSpec((None, d, n), lambda l: (l, 0, 0)),
                  pl.BlockSpec((None, 1, n), lambda l: (l, 0, 0))],
        out_specs=pl.BlockSpec((None, 16, n), lambda l: (l, 0, 0)),
        out_shape=jax.ShapeDtypeStruct((2, 16, n), F32),
    )(cond_raw, mw, mb)


def _mod_bwd(cond_raw, dms, mw, *, name):
    _, d, n = mw.shape

    def body(c_ref, dm_ref, w_ref, gw_ref, dc_ref):
        @pl.when(pl.program_id(0) == 0)
        def _():
            dc_ref[...] = jnp.zeros_like(dc_ref)
        cv = c_ref[...]
        gw_ref[...] = _dot_tn(cv * _sigmoid(cv), dm_ref[...])
        dc_ref[...] += _dot_nt(dm_ref[...], w_ref[...])

    return _pcall(
        body, name=name, grid=(2,),
        in_specs=[pl.BlockSpec((16, d), lambda l: (0, 0)), pl.BlockSpec((None, 16, n), lambda l: (l, 0, 0)),
                  pl.BlockSpec((None, d, n), lambda l: (l, 0, 0))],
        out_specs=[pl.BlockSpec((None, d, n), lambda l: (l, 0, 0)), pl.BlockSpec((16, d), lambda l: (0, 0))],
        out_shape=[jax.ShapeDtypeStruct((2, d, n), F32), jax.ShapeDtypeStruct((16, d), F32)],
    )(cond_raw, dms, mw)


def _lb_fwd(hgrn_lb, *, name):
    def body(a_ref, o_ref):
        a0, a1 = a_ref[0:1, :], a_ref[1:2, :]
        m = jnp.maximum(a0, a1)
        e0, e1 = jnp.exp(a0 - m), jnp.exp(a1 - m)
        o_ref[...] = e0 / (e0 + e1)

    return _pcall(body, name=name, out_shape=jax.ShapeDtypeStruct((1, hgrn_lb.shape[1]), F32))(hgrn_lb)


PACK_TILES = ('l0n1', 'l0n2', 'l1n1', 'l1n2', 'fin', 'gq', 'gk', 'gain', 'dlb_f', 'dlb_b', 'sink')
PACK_ROW = {nm: 8 * i for i, nm in enumerate(PACK_TILES)}
MOD_SOURCE = ((('l0n1', 0), ('l0n1', 1), ('l0n2', 2), ('l0n2', 0), ('l0n2', 1), ('l1n1', 2)),
              (('l1n1', 0), ('l1n1', 1), ('l1n2', 2), ('l1n2', 0), ('l1n2', 1), ('fin', 2)))


def _small_finalize(gath, lb_pad, *, name):
    d = gath.shape[2]

    def body(g_ref, lb_ref, small_ref, glb_ref, gmb_ref, dm_ref):
        tot = g_ref[0]
        for e in range(1, 8):
            tot = tot + g_ref[e]

        def row(nm, r=0):
            return tot[PACK_ROW[nm] + r:PACK_ROW[nm] + r + 1, :]

        for k, nm in enumerate(('l0n1', 'l0n2', 'l1n1', 'l1n2')):
            small_ref[k:k + 1, :] = row(nm, 3) + row(nm, 7)
        for k, nm in ((4, 'gq'), (5, 'gk')):
            small_ref[k:k + 1, :] = row(nm) + pltpu.roll(row(nm), d - 64, 1)
        small_ref[6:7, :] = row('gain')
        small_ref[7:8, :] = row('sink')
        lbv = lb_ref[...]
        g0 = (row('dlb_f') + row('dlb_b')) * lbv * (1.0 - lbv)
        glb_ref[...] = jnp.zeros_like(glb_ref)
        glb_ref[0:1, :] = g0
        glb_ref[1:2, :] = -g0
        dm_ref[...] = jnp.zeros_like(dm_ref)
        for l in range(2):
            for part in range(6):
                nm, r = MOD_SOURCE[l][part]
                gmb_ref[l * 6 + part:l * 6 + part + 1, :] = row(nm, r) + row(nm, r + 4)
                rl = PACK_ROW[nm] + r + 4
                for e in range(8):
                    dm_ref[l, part, e:e + 1, :] = g_ref[e, rl:rl + 1, :]
                dm_ref[l, part, 8:9, :] = row(nm, r)

    return _pcall(
        body, name=name,
        out_shape=[jax.ShapeDtypeStruct((8, d), F32), jax.ShapeDtypeStruct((8, d), F32),
                   jax.ShapeDtypeStruct((12, d), F32), jax.ShapeDtypeStruct((2, 6, 16, d), F32)],
    )(gath, lb_pad)


def _cctx_grad(gath, c_ctx2, *, name):
    def body(g_ref, c_ref, o_ref):
        tot = ((g_ref[0, 0:1, :] + g_ref[2, 0:1, :]) + g_ref[4, 0:1, :]) + g_ref[6, 0:1, :]
        cv = c_ref[...]
        s = _sigmoid(cv)
        o_ref[...] = tot * (s * (1.0 + cv * (1.0 - s)))

    return _pcall(body, name=name, out_shape=jax.ShapeDtypeStruct(c_ctx2.shape, F32))(gath, c_ctx2)


def _row_block(r, c, limit=256 * 1024):
    best = None
    for br in range(16, r + 1, 16):
        if r % br == 0 and br * c <= limit:
            best = br
    return best if best is not None else r


def _sum4(own, landed, core, *, name):
    _, r, c = own.shape
    br = _row_block(r, c, 512 * 1024)

    def body(core_ref, own_ref, land_ref, o_ref):
        s = 2 * lax.axis_index("x") + lax.axis_index("y")
        p = [jnp.where(s == k, own_ref[k], land_ref[k]).astype(F32) for k in range(4)]
        o_ref[...] = ((p[0] + p[1]) + p[2]) + p[3]

    blk = pl.BlockSpec((4, br, c), lambda i, core_ref: (0, i, 0))
    spec = pltpu.PrefetchScalarGridSpec(
        num_scalar_prefetch=1, grid=(r // br,), in_specs=[blk, blk],
        out_specs=pl.BlockSpec((None, br, c), lambda i, core_ref: (core_ref[0], i, 0)))
    return _pcall(body, name=name, grid_spec=spec, out_shape=jax.ShapeDtypeStruct((2, r, c), F32))(core, own, landed)


def _exchange_halves(arrs, *, name):
    n = len(arrs)

    def body(*refs):
        ins, outs = refs[:n], refs[n:2 * n]
        send_sems, recv_sems = refs[2 * n:]
        ax, ay, ac = _place()
        cps = [pltpu.make_async_remote_copy(src_ref=ins[a].at[ac], dst_ref=outs[a].at[ac], send_sem=send_sems.at[a],
                                            recv_sem=recv_sems.at[a], device_id=(ax, ay, 1 - ac),
                                            device_id_type=MESH) for a in range(n)]
        for cp in cps:
            cp.start()
        for a in range(n):
            pltpu.make_async_remote_copy(src_ref=ins[a].at[ac], dst_ref=outs[a].at[1 - ac], send_sem=send_sems.at[a],
                                         recv_sem=recv_sems.at[a], device_id=(ax, ay, ac),
                                         device_id_type=MESH).wait_recv()
        for cp in cps:
            cp.wait_send()

    hbm = pl.BlockSpec(memory_space=pl.ANY)
    return _pcall(
        body, name=name, in_specs=[hbm] * n, out_specs=[hbm] * n,
        out_shape=[jax.ShapeDtypeStruct(a.shape, a.dtype) for a in arrs],
        input_output_aliases={a: a for a in range(n)},
        scratch_shapes=[pltpu.SemaphoreType.DMA((n,))] * 2,
    )(*arrs)


def _add2(a, b, *, name):
    r, c = a.shape
    br = _row_block(r, c, 1024 * 1024)

    def body(a_ref, b_ref, o_ref):
        o_ref[...] = (a_ref[...].astype(F32) + b_ref[...].astype(F32)).astype(BF16)

    blk = pl.BlockSpec((br, c), lambda i: (i, 0))
    return _pcall(body, name=name, grid=(r // br,), in_specs=[blk, blk], out_specs=blk,
                  out_shape=jax.ShapeDtypeStruct((r, c), BF16))(a, b)


def _adam(w, gs, m, v, *, name):
    r, c = w.shape
    br = _row_block(r, c)
    ng = len(gs)
    c1 = 1.0 - ADAM_B1 ** ADAM_STEP
    c2 = 1.0 - ADAM_B2 ** ADAM_STEP

    def body(*refs):
        w_ref, m_ref, v_ref = refs[0], refs[1 + ng], refs[2 + ng]
        outs = refs[3 + ng:]
        g = refs[1][...]
        for k in range(1, ng):
            g = g + refs[1 + k][...]
        mn = ADAM_B1 * m_ref[...] + (1.0 - ADAM_B1) * g
        vn = ADAM_B2 * v_ref[...] + (1.0 - ADAM_B2) * (g * g)
        if ng > 1:
            outs[0][...] = g
        d_out, m_out, v_out = outs[-3:]
        m_out[...] = mn
        v_out[...] = vn
        d_out[...] = -ADAM_LR * ((mn / c1) / (jnp.sqrt(vn / c2) + ADAM_EPS) + ADAM_WD * w_ref[...])

    blk = pl.BlockSpec((br, c), lambda i: (i, 0))
    nout = 4 if ng > 1 else 3
    res = _pcall(body, name=name, grid=(r // br,), in_specs=[blk] * (3 + ng), out_specs=[blk] * nout,
                 out_shape=[jax.ShapeDtypeStruct((r, c), F32)] * nout)(w, *gs, m, v)
    return list(res) if ng > 1 else [gs[0]] + list(res)


def _grad_halves(name, g, ac):
    if name.endswith('_in'):
        n = g.shape[1] // 4
        if name == 'ffn_in':
            assert n == FFN_BK
        order = _ffn_order(g.shape[1]) if name == 'ffn_in' else range(4)
        v = jnp.stack([g[:, b * n:(b + 1) * n] for b in order])
        per = [v[:, :g.shape[0] // 2], v[:, g.shape[0] // 2:]]
    else:
        k4, n = g.shape
        v = g.reshape(4, 2, k4 // 8, n)
        per = [v[:, 0], v[:, 1]]
    first = ac == 0
    return _bf(jnp.where(first, per[0], per[1])), _bf(jnp.where(first, per[1], per[0]))


class _GradReducer:
    def __init__(self):
        self.flight = {}

    @staticmethod
    def _plan(m, sending, refs):
        ax, ay, ac = _place()
        s = 2 * ax + ay
        out = []
        for a in range(m):
            for dx, dy in _CHIP_FLIPS:
                px, py = lax.rem(ax + dx, 2), lax.rem(ay + dy, 2)
                sp = 2 * px + py
                out.append((refs[a].at[sp], refs[m + a].at[s if sending else sp], (px, py, ac)))
        return out

    def start(self, grp, grads):
        ac = lax.axis_index("c")
        names = list(grads)
        halves = [_grad_halves(nm.rstrip('01'), grads[nm], ac) for nm in names]
        theirs = _to_sibling([h[1] for h in halves], name='swap_core_halves_' + grp)
        pair = [_add2(h[0].reshape(-1, b.shape[-1]), b.reshape(-1, b.shape[-1]), name='add_cores').reshape(b.shape)
                for h, b in zip(halves, theirs)]
        m = len(names)
        land = [lax.empty(a.shape, a.dtype) for a in pair]
        sends, recvs, bufs, token = _split_start(pair + land, functools.partial(self._plan, m, True), 3 * m,
                                                 name='scatter_' + grp + '_start')
        self.flight[grp] = (names, sends, recvs, bufs)
        return token

    def finish(self, grp, after):
        names, sends, recvs, bufs = self.flight.pop(grp)
        m = len(names)
        bufs = _split_wait(bufs, sends, recvs, functools.partial(self._plan, m, False), after,
                           name='scatter_' + grp + '_wait')
        core = lax.axis_index("c").astype(jnp.int32).reshape(1)
        sums = [_sum4(p, l, core, name='sum_chips') for p, l in zip(bufs[:m], bufs[m:])]
        both = _exchange_halves(sums, name='gather_core_halves_' + grp)
        return {nm: g.reshape(-1, g.shape[-1]) for nm, g in zip(names, both)}


def _from_shards(name, g):
    _, r, n = g.shape
    if name == 'ffn_in':
        assert n == FFN_BK
        v = g.reshape(4, 2, r // 2, n)
        return jnp.concatenate([v[b] for b in _ffn_order(4 * n)], axis=-1)
    if name == 'ffn_out':
        return g.reshape(4, 2, r // 2, n).transpose(1, 0, 2, 3).reshape(2, 2 * r, n)
    if name in ('even_in', 'odd_in'):
        return jnp.concatenate([g[b] for b in range(4)], axis=-1)
    return g.reshape(4 * r, n)


def kernel(x, c, ctx, c_ctx, mod_w, mod_b, norm_g, ffn_w_in, ffn_w_out, even_w_in, even_w_out, attn_qk_norm_g, attn_sink, hgrn_out_norm_g, hgrn_lb, odd_w_in, odd_w_out, loss_target, m_c_ctx, m_mod_w, m_mod_b, m_norm_g, m_ffn_w_in, m_ffn_w_out, m_even_w_in, m_even_w_out, m_attn_qk_norm_g, m_attn_sink, m_hgrn_out_norm_g, m_hgrn_lb, m_odd_w_in, m_odd_w_out, v_c_ctx, v_mod_w, v_mod_b, v_norm_g, v_ffn_w_in, v_ffn_w_out, v_even_w_in, v_even_w_out, v_attn_qk_norm_g, v_attn_sink, v_hgrn_out_norm_g, v_hgrn_lb, v_odd_w_in, v_odd_w_out):
    d = x.shape[-1]
    lc = ctx.shape[1]
    assert lc == TM and d == 1024
    ax, ay, ac = _place()
    s = 2 * ax + ay
    me = 4 * ax + 2 * ay + ac
    nmod = mod_w.shape[2]

    def pad8(v):
        return jnp.pad(v, ((0, 8 - v.shape[0]), (0, 0)))

    pack = jnp.concatenate([pad8(c), pad8(norm_g.reshape(1, d))], axis=0)
    g1 = _ag8(pack, name='gather_cond')
    c_all = g1[:, 0, :]
    ng = g1[0::2, 8, :].reshape(4, 2, 2, d // 4).transpose(1, 2, 0, 3).reshape(4, d)

    cond_raw = jnp.concatenate([c_all, pad8(c_ctx.reshape(1, d))], axis=0)
    mb_sh = lax.dynamic_slice_in_dim(mod_b, s * nmod, nmod, axis=1).reshape(2, 1, nmod)
    mpart = _mod_fwd(cond_raw, mod_w, mb_sh, name='mod_fwd')
    g3 = _ag8(mpart.reshape(32, nmod), name='gather_mods')
    mods_full = g3[0::2].reshape(4, 2, 16, nmod).transpose(1, 2, 0, 3).reshape(2, 16, 4 * nmod)
    m_lat = lax.dynamic_index_in_dim(mods_full, me, axis=1, keepdims=False)
    mods = jnp.stack([mods_full[:, 8], m_lat], axis=1).reshape(24, d)

    names = ['ffn_in', 'ffn_out', 'even_in', 'even_out', 'odd_in', 'odd_out']
    shards = [_bf(v.reshape(-1, v.shape[-1])) for v in (ffn_w_in, ffn_w_out, even_w_in, even_w_out, odd_w_in, odd_w_out)]
    shards, mods = lax.optimization_barrier((shards, mods))
    reducer = _GradReducer()
    wsrc = _GatheredWeights(dict(zip(names, shards)), reducer)

    lb = _lb_fwd(hgrn_lb, name='hgrn_lower_bound')
    small = dict(gq=jnp.tile(attn_qk_norm_g[0, 0], 2).reshape(1, 128), gk=jnp.tile(attn_qk_norm_g[0, 1], 2).reshape(1, 128),
                 sink=attn_sink[0], gain=hgrn_out_norm_g, lb=lb)
    x0 = jnp.concatenate([ctx[0], x[0]], axis=0)
    mods = mods + wsrc.token[0, 0]
    loss_t, dx0, grads, sums = _local_step(x0, loss_target[0], mods, ng, wsrc, small)
    loss = lax.psum(loss_t[0, 0], ("x", "y", "c"))
    grad_x = dx0[None]

    def tile(v):
        return jnp.pad(v, ((0, 8 - v.shape[0]), (0, d - v.shape[1])))

    sums = dict(sums, sink=sums['sink'][:, 0].reshape(1, 8))
    g4 = _ag8(jnp.concatenate([tile(sums[nm]) for nm in PACK_TILES], axis=0), name='gather_row_sums')
    small_g, glb, gmb, dmat = _small_finalize(g4, tile(lb)[0:1], name='small_grads')
    dms = lax.dynamic_slice_in_dim(dmat.transpose(0, 2, 1, 3).reshape(2, 16, 6 * d), s * nmod, nmod, axis=2)
    g_mod_w, dcond = _mod_bwd(cond_raw, dms, mod_w, name='mod_bwd')
    g5 = _ag8(dcond[8:16], name='gather_dcond')
    g_c_ctx = _cctx_grad(g5, c_ctx.reshape(8, d // 8).reshape(1, d), name='c_ctx_grad')

    late = {nm: grads[nm] for nm in ('even_in', 'even_out')}
    late, g_c_ctx = lax.optimization_barrier((late, g_c_ctx))
    token = reducer.start('late', late)
    full = reducer.finish('early', token)

    def upd(wv, gs, mv, vv, name):
        shp = wv.shape
        c2 = shp[-1]
        out = _adam(wv.reshape(-1, c2), [g.reshape(-1, c2) for g in gs], mv.reshape(-1, c2), vv.reshape(-1, c2), name=name)
        return [o.reshape(shp) for o in out]

    res = {}
    res['c_ctx'] = upd(c_ctx.reshape(8, d // 8), [g_c_ctx.reshape(8, d // 8)], m_c_ctx.reshape(8, d // 8), v_c_ctx.reshape(8, d // 8), 'adam_c_ctx')
    res['c_ctx'] = [o.reshape(d) for o in res['c_ctx']]
    res['mod_w'] = upd(mod_w, [g_mod_w], m_mod_w, v_mod_w, 'adam_mod_w')
    res['mod_b'] = upd(mod_b, [gmb.reshape(2, 6 * d)], m_mod_b, v_mod_b, 'adam_mod_b')
    g_ng = lax.dynamic_slice_in_dim(small_g[0:4].reshape(2, 2, d), s * (d // 4), d // 4, axis=2)
    res['norm_g'] = upd(norm_g, [g_ng], m_norm_g, v_norm_g, 'adam_norm_g')
    g_qk = jnp.stack([small_g[4, 0:64], small_g[5, 0:64]]).reshape(1, 2, 64)
    res['attn_qk_norm_g'] = upd(attn_qk_norm_g, [g_qk], m_attn_qk_norm_g, v_attn_qk_norm_g, 'adam_qk_gain')
    res['attn_sink'] = upd(attn_sink, [small_g[7, 0:8].reshape(1, 8)], m_attn_sink, v_attn_sink, 'adam_sink')
    res['hgrn_out_norm_g'] = upd(hgrn_out_norm_g, [small_g[6, 0:128].reshape(1, 128)], m_hgrn_out_norm_g, v_hgrn_out_norm_g, 'adam_head_gain')
    res['hgrn_lb'] = upd(hgrn_lb, [glb[0:2, 0:hgrn_lb.shape[1]]], m_hgrn_lb, v_hgrn_lb, 'adam_hgrn_lb')
    res['odd_w_in'] = upd(odd_w_in, [full['odd_in']], m_odd_w_in, v_odd_w_in, 'adam_odd_in')
    res['odd_w_out'] = upd(odd_w_out, [full['odd_out']], m_odd_w_out, v_odd_w_out, 'adam_odd_out')
    full.update(reducer.finish('mid', res['odd_w_in'][1]))
    g_ffn_in = jnp.concatenate([full['ffn_in0'], full['ffn_in1']], axis=0)
    g_ffn_out = jnp.concatenate([full['ffn_out0'], full['ffn_out1']], axis=0)
    res['ffn_w_in'] = upd(ffn_w_in, [g_ffn_in], m_ffn_w_in, v_ffn_w_in, 'adam_ffn_in')
    res['ffn_w_out'] = upd(ffn_w_out, [g_ffn_out], m_ffn_w_out, v_ffn_w_out, 'adam_ffn_out')
    full.update(reducer.finish('late', res['ffn_w_in'][1]))
    res['even_w_in'] = upd(even_w_in, [full['even_in']], m_even_w_in, v_even_w_in, 'adam_even_in')
    res['even_w_out'] = upd(even_w_out, [full['even_out']], m_even_w_out, v_even_w_out, 'adam_even_out')

    order = ['c_ctx', 'mod_w', 'mod_b', 'norm_g', 'ffn_w_in', 'ffn_w_out', 'even_w_in', 'even_w_out',
             'attn_qk_norm_g', 'attn_sink', 'hgrn_out_norm_g', 'hgrn_lb', 'odd_w_in', 'odd_w_out']
    outs = [loss, grad_x]
    for k in range(4):
        outs += [res[nm][k] for nm in order]
    return tuple(outs)
```

```python
import functools
import math

import numpy as np
import jax
import jax.numpy as jnp
from jax import lax
from jax.experimental import pallas as pl
from jax.experimental.pallas import tpu as pltpu

F32 = jnp.float32
BF16 = jnp.bfloat16
EPS = 1e-6
TM = 256
CHUNK = 64
QB = 256
WINDOW = 128
NEG = -1e30
MESH = pl.DeviceIdType.MESH

ADAM_LR, ADAM_B1, ADAM_B2, ADAM_EPS, ADAM_WD, ADAM_STEP = 0.001, 0.9, 0.999, 1e-08, 0.01, 10


def _pcall(body, **kw):
    return pl.pallas_call(body, **kw)


def _pick(n, cap):
    best = None
    for m in range(128, min(n, cap) + 1, 128):
        if n % m == 0:
            best = m
    assert best is not None, (n, cap)
    return best


def _bf(x):
    return x.astype(BF16)


def _dot(a, b):
    return jnp.dot(_bf(a), _bf(b), preferred_element_type=F32)


def _dot_nt(a, b):
    return lax.dot_general(_bf(a), _bf(b), (((1,), (1,)), ((), ())), preferred_element_type=F32)


def _dot_tn(a, b):
    return lax.dot_general(_bf(a), _bf(b), (((0,), (0,)), ((), ())), preferred_element_type=F32)


def _dot_exact(a, b):
    return jnp.dot(a, b, preferred_element_type=F32, precision=lax.Precision.HIGHEST)


def _sigmoid(x):
    return 1.0 / (1.0 + jnp.exp(-x))


def _iota(shape, dim):
    return lax.broadcasted_iota(jnp.int32, shape, dim)


def _parts(a):
    parts = list(a) if isinstance(a, (list, tuple)) else [a]
    widths = [p.shape[1] for p in parts]
    return parts, widths, [sum(widths[:i]) for i in range(len(parts))]


def _mm_nn(a, b, *, lead=None, out_dtype=F32, name):
    parts, widths, offs = _parts(a)
    m, k = parts[0].shape[0], sum(widths)
    n = b.shape[-1]
    bm = 1408 if (m % 1408 == 0 and k <= 1024) else (768 if m % 768 == 0 else TM)
    bn = _pick(n, 1024) if n % 512 == 0 else _pick(n, 1664)

    def body(*refs):
        b_ref, o_ref = refs[-2], refs[-1]
        acc = None
        for p_ref, w, off in zip(refs, widths, offs):
            term = _dot(p_ref[...], b_ref[off:off + w, :])
            acc = term if acc is None else acc + term
        o_ref[...] = acc.astype(o_ref.dtype)

    if lead is None:
        b_spec = pl.BlockSpec((k, bn), lambda i, j: (0, j))
    else:
        b_spec = pl.BlockSpec((None, k, bn), lambda i, j: (lead, 0, j))
    return _pcall(
        body, name=name, grid=(m // bm, n // bn),
        in_specs=[pl.BlockSpec((bm, w), lambda i, j: (i, 0)) for w in widths] + [b_spec],
        out_specs=pl.BlockSpec((bm, bn), lambda i, j: (i, j)),
        out_shape=jax.ShapeDtypeStruct((m, n), out_dtype),
    )(*parts, b)


def _mm_nt(a, b, *, lead=None, name):
    parts, widths, offs = _parts(a)
    m, n = parts[0].shape[0], sum(widths)
    k = b.shape[-2]
    bm = 1408 if (m % 1408 == 0 and n <= 1024) else (768 if m % 768 == 0 else TM)
    bk = _pick(k, 1024 if n <= 2048 else 512)

    def body(*refs):
        b_ref, o_ref = refs[-2], refs[-1]
        acc = None
        for p_ref, w, off in zip(refs, widths, offs):
            term = _dot_nt(p_ref[...], b_ref[:, off:off + w])
            acc = term if acc is None else acc + term
        o_ref[...] = acc

    if lead is None:
        b_spec = pl.BlockSpec((bk, n), lambda i, j: (j, 0))
    else:
        b_spec = pl.BlockSpec((None, bk, n), lambda i, j: (lead, j, 0))
    return _pcall(
        body, name=name, grid=(m // bm, k // bk),
        in_specs=[pl.BlockSpec((bm, w), lambda i, j: (i, 0)) for w in widths] + [b_spec],
        out_specs=pl.BlockSpec((bm, bk), lambda i, j: (i, j)),
        out_shape=jax.ShapeDtypeStruct((m, k), F32),
    )(*parts, b)


def _mm_tn(a, b, *, name):
    a_parts, a_w, a_off = _parts(a)
    b_parts, b_w, b_off = _parts(b)
    t, k, n = a_parts[0].shape[0], sum(a_w), sum(b_w)
    bt = 1408 if t % 1408 == 0 else (768 if t % 768 == 0 else TM)
    bk = _pick(k, 1536) if len(a_parts) == 1 else math.gcd(*a_w)
    if len(b_parts) == 1:
        bn = _pick(n, 1024) if n % 1024 == 0 or n < 1664 else _pick(n, 1664)
    else:
        bn = math.gcd(*b_w)
    na, nbp = len(a_parts), len(b_parts)

    def block_range(off, w, blk):
        return off // blk, w // blk

    def body(*refs):
        a_refs, b_refs, o_ref = refs[:na], refs[na:na + nbp], refs[-1]
        i, j = pl.program_id(0), pl.program_id(1)

        @pl.when(pl.program_id(2) == 0)
        def _():
            o_ref[...] = jnp.zeros_like(o_ref)

        def add(a_ref, b_ref):
            o_ref[...] += _dot_tn(a_ref[...], b_ref[...])

        for pa in range(na):
            sa, ca = block_range(a_off[pa], a_w[pa], bk)
            for pb in range(nbp):
                sb, cb = block_range(b_off[pb], b_w[pb], bn)
                if na == 1 and nbp == 1:
                    add(a_refs[0], b_refs[0])
                else:
                    pl.when((i >= sa) & (i < sa + ca) & (j >= sb) & (j < sb + cb))(
                        functools.partial(add, a_refs[pa], b_refs[pb]))

    def spec(off, w, blk, axis):
        s0, cnt = block_range(off, w, blk)

        def index(i, j, s):
            g = i if axis == 0 else j
            inside = (g >= s0) & (g < s0 + cnt)
            return (jnp.where(inside, s, 0), jnp.clip(g - s0, 0, cnt - 1))

        return pl.BlockSpec((bt, blk), index)

    return _pcall(
        body, name=name, grid=(k // bk, n // bn, t // bt),
        in_specs=[spec(o, w, bk, 0) for o, w in zip(a_off, a_w)] + [spec(o, w, bn, 1) for o, w in zip(b_off, b_w)],
        out_specs=pl.BlockSpec((bk, bn), lambda i, j, s: (i, j)),
        out_shape=jax.ShapeDtypeStruct((k, n), F32),
    )(*a_parts, *b_parts)


def _mod_row(mods_ref, lat, idx):
    return jnp.where(lat, mods_ref[idx + 6:idx + 7, :], mods_ref[idx:idx + 1, :])


def _row_step(t):
    return 768 if t % 768 == 0 else TM


def _row_fwd(x, mods, *, y=None, gate=None, g=None, shift=None, scale=None, name):
    t, d = x.shape
    has_y, has_n = y is not None, g is not None
    rt = _row_step(t)

    def body(*refs):
        refs = list(refs)
        x_ref, mods_ref = refs[0], refs[1]
        pos = 2
        if has_y:
            y_ref = refs[pos]; pos += 1
        if has_n:
            g_ref = refs[pos]; pos += 1
        outs = refs[pos:]
        for sub in range(rt // TM):
            rows = slice(sub * TM, (sub + 1) * TM)
            lat = pl.program_id(0) * (rt // TM) + sub > 0
            x1 = x_ref[rows, :]
            o = 0
            if has_y:
                x1 = x1 + _mod_row(mods_ref, lat, gate) * y_ref[rows, :]
                outs[o][rows, :] = x1; o += 1
            if has_n:
                rs = lax.rsqrt(jnp.mean(x1 * x1, axis=-1, keepdims=True) + EPS)
                hn = x1 * rs * g_ref[...]
                h = hn * (1.0 + _mod_row(mods_ref, lat, scale)) + _mod_row(mods_ref, lat, shift)
                outs[o][rows, :] = h.astype(BF16)

    row = pl.BlockSpec((rt, d), lambda i: (i, 0))
    ins, specs = [x, mods], [row, pl.BlockSpec(mods.shape, lambda i: (0, 0))]
    if has_y:
        ins.append(y); specs.append(row)
    if has_n:
        ins.append(g.reshape(1, d)); specs.append(pl.BlockSpec((1, d), lambda i: (0, 0)))
    out_shape, out_specs = [], []
    if has_y:
        out_shape.append(jax.ShapeDtypeStruct((t, d), F32)); out_specs.append(row)
    if has_n:
        out_shape.append(jax.ShapeDtypeStruct((t, d), BF16)); out_specs.append(row)
    res = _pcall(body, name=name, grid=(t // rt,), in_specs=specs, out_specs=out_specs,
                 out_shape=out_shape)(*ins)
    return res


def _acc_row(ref, r, val):
    ref[r:r + 1, :] += val


def _row_final(x, z, mods, target, *, gate, name):
    t, d = x.shape

    def body(x_ref, mods_ref, z_ref, t_ref, loss_ref, dx_ref, dz_ref, sums_ref):
        i = pl.program_id(0)
        lat = i > 0

        @pl.when(i == 0)
        def _():
            loss_ref[...] = jnp.zeros_like(loss_ref)
            sums_ref[...] = jnp.zeros_like(sums_ref)

        gt = _mod_row(mods_ref, lat, gate)
        zz = z_ref[...]
        yv = x_ref[...] + gt * zz
        keep = jnp.where(lat, 1.0, 0.0).astype(F32)
        diff = (yv - t_ref[...]) * keep
        part = jnp.sum(jnp.sum(diff * diff, axis=0, keepdims=True), axis=1, keepdims=True)
        loss_ref[...] += part * (0.5 / d)
        dy = diff * (1.0 / d)
        dx_ref[...] = dy
        dz_ref[...] = (gt * dy).astype(BF16)
        _acc_row(sums_ref, 6, jnp.sum(dy * zz, axis=0, keepdims=True))

    row = pl.BlockSpec((TM, d), lambda i: (i, 0))
    return _pcall(
        body, name=name, grid=(t // TM,),
        in_specs=[row, pl.BlockSpec(mods.shape, lambda i: (0, 0)), row,
                  pl.BlockSpec((TM, d), lambda i: (jnp.maximum(i - 1, 0), 0))],
        out_specs=[pl.BlockSpec((8, 128), lambda i: (0, 0)), row, row,
                   pl.BlockSpec((8, d), lambda i: (0, 0))],
        out_shape=[jax.ShapeDtypeStruct((8, 128), F32), jax.ShapeDtypeStruct((t, d), F32),
                   jax.ShapeDtypeStruct((t, d), BF16), jax.ShapeDtypeStruct((8, d), F32)],
    )(x, mods, z, target)


def _row_bwd(xn, dxo, dh, mods, g, *, shift, scale, y=None, gate=None, latent_only=False, name):
    t, d = xn.shape
    has_y = y is not None

    def body(*refs):
        refs = list(refs)
        x_ref, dxo_ref, dh_ref, mods_ref, g_ref = refs[:5]
        pos = 5
        if has_y:
            y_ref = refs[pos]; pos += 1
        dx_ref = refs[pos]; pos += 1
        if has_y:
            dy_ref = refs[pos]; pos += 1
        sums_ref = refs[pos]
        i = pl.program_id(0)

        @pl.when(i == 0)
        def _():
            sums_ref[...] = jnp.zeros_like(sums_ref)

        def add_sums(vals, base):
            for r, v in enumerate(vals):
                if v is not None:
                    _acc_row(sums_ref, base + r, v)

        gv = g_ref[...]
        for sub in range(rt // TM):
            rows = slice(sub * TM, (sub + 1) * TM)
            lat = i * (rt // TM) + sub > 0
            x1 = x_ref[rows, :]
            rs = lax.rsqrt(jnp.mean(x1 * x1, axis=-1, keepdims=True) + EPS)
            xh = x1 * rs
            dhv = dh_ref[rows, :]
            dn = dhv * (1.0 + _mod_row(mods_ref, lat, scale))
            dxh = dn * gv
            dx = dxo_ref[rows, :] + rs * (dxh - xh * jnp.mean(dxh * xh, axis=-1, keepdims=True))
            dx_ref[rows, :] = dx
            vals = [jnp.sum(dhv, axis=0, keepdims=True),
                    jnp.sum(dhv * (xh * gv), axis=0, keepdims=True),
                    None,
                    jnp.sum(dn * xh, axis=0, keepdims=True)]
            if has_y:
                dy_ref[rows, :] = (_mod_row(mods_ref, lat, gate) * dx).astype(BF16)
                vals[2] = jnp.sum(dx * y_ref[rows, :], axis=0, keepdims=True)
            if sub == 0:
                pl.when(i == 0)(functools.partial(add_sums, vals, 0))
                pl.when(i > 0)(functools.partial(add_sums, vals, 4))
            else:
                add_sums(vals, 4)

    rt = TM if latent_only else _row_step(t)
    row = pl.BlockSpec((rt, d), lambda i: (i, 0))
    ins = [xn, dxo, dh, mods, g.reshape(1, d)]
    specs = [row, row, row, pl.BlockSpec(mods.shape, lambda i: (0, 0)), pl.BlockSpec((1, d), lambda i: (0, 0))]
    if latent_only:
        out_shape = [jax.ShapeDtypeStruct((t - TM, d), F32)]
        out_specs = [pl.BlockSpec((TM, d), lambda i: (jnp.maximum(i - 1, 0), 0))]
    else:
        out_shape, out_specs = [jax.ShapeDtypeStruct((t, d), F32)], [row]
    if has_y:
        ins.append(y); specs.append(row)
        out_shape.append(jax.ShapeDtypeStruct((t, d), BF16)); out_specs.append(row)
    out_shape.append(jax.ShapeDtypeStruct((8, d), F32))
    out_specs.append(pl.BlockSpec((8, d), lambda i: (0, 0)))
    return _pcall(body, name=name, grid=(t // rt,), in_specs=specs, out_specs=out_specs,
                  out_shape=out_shape)(*ins)


FFN_BK = 1408


FFN_SUB = 256


def _ffn_order(n2):
    nb = n2 // (2 * FFN_BK)
    return [h * nb + j for j in range(nb) for h in (0, 1)]


def _ffn_interleave(w):
    return jnp.concatenate([w[..., b * FFN_BK:(b + 1) * FFN_BK] for b in _ffn_order(w.shape[-1])], axis=-1)


def _ffn_deinterleave(w):
    order = _ffn_order(w.shape[-1])
    return jnp.concatenate([w[..., order.index(b) * FFN_BK:(order.index(b) + 1) * FFN_BK]
                            for b in range(len(order))], axis=-1)


def _big_tile(t):
    return 768 if t % 768 == 0 else TM


def _ffn_in(h, w, *, lead, name):
    t, d = h.shape
    n2 = w.shape[-1]
    bm, bk = _big_tile(t), FFN_BK

    def body(h_ref, w_ref, u_ref, a_ref):
        hb = h_ref[...]
        for c0 in range(0, bk, FFN_SUB):
            c1 = min(c0 + FFN_SUB, bk)
            ug = _dot(hb, w_ref[:, c0:c1]).astype(BF16)
            uu = _dot(hb, w_ref[:, bk + c0:bk + c1]).astype(BF16)
            u_ref[:, c0:c1] = ug
            u_ref[:, bk + c0:bk + c1] = uu
            gv, up = ug.astype(F32), uu.astype(F32)
            a_ref[:, c0:c1] = (gv * _sigmoid(gv) * up).astype(BF16)

    return _pcall(
        body, name=name, grid=(t // bm, n2 // (2 * bk)),
        in_specs=[pl.BlockSpec((bm, d), lambda i, j: (i, 0)),
                  pl.BlockSpec((None, d, 2 * bk), lambda i, j: (lead, 0, j))],
        out_specs=[pl.BlockSpec((bm, 2 * bk), lambda i, j: (i, j)), pl.BlockSpec((bm, bk), lambda i, j: (i, j))],
        out_shape=[jax.ShapeDtypeStruct((t, n2), BF16), jax.ShapeDtypeStruct((t, n2 // 2), BF16)],
    )(h, w)


def _ffn_dx(dz, w_out, u, *, lead, name):
    t, d = dz.shape
    n2 = u.shape[1]
    bm, bk = _big_tile(t), FFN_BK

    def body(dz_ref, w_ref, u_ref, du_ref):
        dzb = dz_ref[...]
        for c0 in range(0, bk, FFN_SUB):
            c1 = min(c0 + FFN_SUB, bk)
            da = _dot_nt(dzb, w_ref[c0:c1, :])
            gv, up = u_ref[:, c0:c1].astype(F32), u_ref[:, bk + c0:bk + c1].astype(F32)
            s = _sigmoid(gv)
            du_ref[:, c0:c1] = (da * up * (s * (1.0 + gv * (1.0 - s)))).astype(BF16)
            du_ref[:, bk + c0:bk + c1] = (da * gv * s).astype(BF16)

    ublk = pl.BlockSpec((bm, 2 * bk), lambda i, j: (i, j))
    return _pcall(
        body, name=name, grid=(t // bm, n2 // (2 * bk)),
        in_specs=[pl.BlockSpec((bm, d), lambda i, j: (i, 0)),
                  pl.BlockSpec((None, bk, d), lambda i, j: (lead, j, 0)), ublk],
        out_specs=ublk, out_shape=jax.ShapeDtypeStruct((t, n2), BF16),
    )(dz, w_out, u)


def _lane(shape):
    return _iota(shape, len(shape) - 1)


def _pair_norm(x, g):
    lo = _lane(x.shape) < 64
    x2 = x * x
    s_lo = jnp.sum(jnp.where(lo, x2, 0.0), axis=-1, keepdims=True)
    s_hi = jnp.sum(jnp.where(lo, 0.0, x2), axis=-1, keepdims=True)
    rs = lax.rsqrt(jnp.where(lo, s_lo, s_hi) * (1.0 / 64) + EPS)
    return x * rs, rs


def _pair_mean(v):
    lo = _lane(v.shape) < 64
    s_lo = jnp.sum(jnp.where(lo, v, 0.0), axis=-1, keepdims=True)
    s_hi = jnp.sum(jnp.where(lo, 0.0, v), axis=-1, keepdims=True)
    return jnp.where(lo, s_lo, s_hi) * (1.0 / 64)


def _rot64(x):
    r1 = pltpu.roll(x, 32, 1)
    r2 = pltpu.roll(x, 96, 1)
    even = ((_lane(x.shape) >> 5) & 1) == 0
    return jnp.where(even, -r2, r1)


def _rope64(x, cos, sin):
    return x * cos + _rot64(x) * sin


def _rope64_t(d, cos, sin):
    return d * cos - _rot64(d * sin)


def _kprep_fwd(p, gk, cos, sin, *, name):
    t = p.shape[0]

    def body(k_ref, g_ref, c_ref, s_ref, o_ref):
        xh, _ = _pair_norm(k_ref[...], None)
        o_ref[...] = _rope64(xh * g_ref[...], c_ref[...], s_ref[...])

    blk = pl.BlockSpec((TM, 128), lambda i: (i, 0))
    return _pcall(
        body, name=name, grid=(t // TM,),
        in_specs=[pl.BlockSpec((TM, 128), lambda i: (i, 4)), pl.BlockSpec((1, 128), lambda i: (0, 0)), blk, blk],
        out_specs=blk, out_shape=jax.ShapeDtypeStruct((t, 128), F32),
    )(p, gk, cos, sin)


def _kprep_bwd(p, gk, cos, sin, dkp, dv, *, name):
    t = p.shape[0]

    def body(k_ref, g_ref, c_ref, s_ref, dkp_ref, dv_ref, o_ref, dg_ref):
        @pl.when(pl.program_id(0) == 0)
        def _():
            dg_ref[...] = jnp.zeros_like(dg_ref)
        xh, rs = _pair_norm(k_ref[...], None)
        dn = _rope64_t(dkp_ref[...], c_ref[...], s_ref[...])
        _acc_row(dg_ref, 0, jnp.sum(dn * xh, axis=0, keepdims=True))
        dxh = dn * g_ref[...]
        o_ref[:, 0:128] = (rs * (dxh - xh * _pair_mean(dxh * xh))).astype(BF16)
        o_ref[:, 128:256] = dv_ref[...].astype(BF16)

    blk = pl.BlockSpec((TM, 128), lambda i: (i, 0))
    return _pcall(
        body, name=name, grid=(t // TM,),
        in_specs=[pl.BlockSpec((TM, 128), lambda i: (i, 4)), pl.BlockSpec((1, 128), lambda i: (0, 0)), blk, blk, blk, blk],
        out_specs=[pl.BlockSpec((TM, 256), lambda i: (i, 0)), pl.BlockSpec((8, 128), lambda i: (0, 0))],
        out_shape=[jax.ShapeDtypeStruct((t, 256), BF16), jax.ShapeDtypeStruct((8, 128), F32)],
    )(p, gk, cos, sin, dkp, dv)


def _attn_common(i, t, lc, kp_ref, v_ref):
    span = QB + 2 * WINDOW
    start = pl.multiple_of(jnp.clip(i * QB - WINDOW, lc, t - span), WINDOW)
    kall = jnp.concatenate([kp_ref[0:lc, :], kp_ref[pl.ds(start, span), :]], axis=0)
    vall = jnp.concatenate([v_ref[0:lc, :], v_ref[pl.ds(start, span), :]], axis=0)
    nk = lc + span
    col = _iota((QB, nk), 1)
    krow = jnp.where(col < lc, col, start + col - lc)
    qrow = i * QB + _iota((QB, nk), 0)
    valid = (col < lc) | ((qrow >= lc) & (krow >= lc) & (jnp.abs(krow - qrow) <= WINDOW))
    lo = _lane(kall.shape) < 64
    kroll, vroll = pltpu.roll(kall, 64, 1), pltpu.roll(vall, 64, 1)
    zero = jnp.zeros_like(kall)
    kvar = [[_bf(jnp.where(lo, kall, zero)), _bf(jnp.where(lo, zero, kroll))],
            [_bf(jnp.where(lo, kroll, zero)), _bf(jnp.where(lo, zero, kall))]]
    vvar = [[_bf(jnp.where(lo, vall, zero)), _bf(jnp.where(lo, zero, vroll))],
            [_bf(jnp.where(lo, vroll, zero)), _bf(jnp.where(lo, zero, vall))]]
    return start, valid, kvar, vvar


def _softmax_sink(s, valid, snk):
    s = jnp.where(valid, s, NEG)
    m = jnp.maximum(jnp.max(s, axis=-1, keepdims=True), snk)
    e = jnp.exp(s - m)
    es = jnp.exp(snk - m)
    inv = 1.0 / (jnp.sum(e, axis=-1, keepdims=True) + es)
    return e * inv, es * inv


def _attn_fwd(p, kp, gq, sink, cos, sin, *, lc, name):
    t = p.shape[0]
    scale = 64 ** -0.5

    def body(q_ref, kp_ref, v_ref, g_ref, sink_ref, c_ref, s_ref, o_ref):
        i = pl.program_id(0)
        _, valid, kvar, vvar = _attn_common(i, t, lc, kp_ref, v_ref)
        cosv, sinv, gv = c_ref[...], s_ref[...], g_ref[...]
        for j in range(4):
            xh, _ = _pair_norm(q_ref[:, 128 * j:128 * j + 128], None)
            q2 = _bf(_rope64(xh * gv, cosv, sinv) * scale)
            acc = jnp.zeros((QB, 128), F32)
            for half in range(2):
                s = _dot_nt(q2, kvar[j // 2][half])
                pr, _ = _softmax_sink(s, valid, sink_ref[2 * j + half])
                acc = acc + _dot(pr, vvar[j // 2][half])
            o_ref[:, 128 * j:128 * j + 128] = acc.astype(BF16)

    qblk = pl.BlockSpec((QB, 128), lambda i: (i, 0))
    return _pcall(
        body, name=name, grid=(t // QB,),
        in_specs=[pl.BlockSpec((QB, 512), lambda i: (i, 0)),
                  pl.BlockSpec((t, 128), lambda i: (0, 0)),
                  pl.BlockSpec((t, 128), lambda i: (0, 5)),
                  pl.BlockSpec((1, 128), lambda i: (0, 0)),
                  pl.BlockSpec(memory_space=pltpu.SMEM), qblk, qblk],
        out_specs=pl.BlockSpec((QB, 512), lambda i: (i, 0)),
        out_shape=jax.ShapeDtypeStruct((t, 512), BF16),
    )(p, kp, p, gq, sink, cos, sin)


def _attn_bwd(p, kp, gq, sink, cos, sin, dmix, *, lc, name):
    t = p.shape[0]
    scale = 64 ** -0.5
    span = QB + 2 * WINDOW

    def body(q_ref, kp_ref, v_ref, g_ref, sink_ref, c_ref, s_ref, do_ref,
             dq_ref, dk_ref, dv_ref, dg_ref, dsink_ref):
        i = pl.program_id(0)

        @pl.when(i == 0)
        def _():
            dk_ref[...] = jnp.zeros_like(dk_ref)
            dv_ref[...] = jnp.zeros_like(dv_ref)
            dg_ref[...] = jnp.zeros_like(dg_ref)
            dsink_ref[...] = jnp.zeros_like(dsink_ref)

        start, valid, kvar, vvar = _attn_common(i, t, lc, kp_ref, v_ref)
        cosv, sinv, gv = c_ref[...], s_ref[...], g_ref[...]
        nk = lc + span
        dkt = [jnp.zeros((64, nk), F32), jnp.zeros((64, nk), F32)]
        dvt = [jnp.zeros((64, nk), F32), jnp.zeros((64, nk), F32)]
        for j in range(4):
            kvh = j // 2
            xh, rs = _pair_norm(q_ref[:, 128 * j:128 * j + 128], None)
            q2 = _bf(_rope64(xh * gv, cosv, sinv) * scale)
            do2 = _bf(do_ref[:, 128 * j:128 * j + 128])
            dq2 = jnp.zeros((QB, 128), F32)
            for half in range(2):
                s = _dot_nt(q2, kvar[kvh][half])
                pr, ps = _softmax_sink(s, valid, sink_ref[2 * j + half])
                dp = _dot_nt(do2, vvar[kvh][half])
                delta = jnp.sum(pr * dp, axis=-1, keepdims=True)
                ds = pr * (dp - delta)
                dsk = jnp.sum(jnp.sum(-ps * delta, axis=0, keepdims=True), axis=1, keepdims=True)
                _acc_row(dsink_ref, 2 * j + half, jnp.broadcast_to(dsk, (1, 128)))
                dq2 = dq2 + _dot(ds, kvar[kvh][half])
                hrows = slice(64 * half, 64 * half + 64)
                dkt[kvh] = dkt[kvh] + _dot_tn(q2, ds)[hrows]
                dvt[kvh] = dvt[kvh] + _dot_tn(do2, pr)[hrows]
            dn = _rope64_t(dq2 * scale, cosv, sinv)
            _acc_row(dg_ref, 0, jnp.sum(dn * xh, axis=0, keepdims=True))
            dxh = dn * gv
            dq_ref[:, 128 * j:128 * j + 128] = (rs * (dxh - xh * _pair_mean(dxh * xh))).astype(BF16)
        dk_all = jnp.concatenate(dkt, axis=0).T
        dv_all = jnp.concatenate(dvt, axis=0).T
        dk_ref[0:lc, :] += dk_all[0:lc]
        dv_ref[0:lc, :] += dv_all[0:lc]
        dk_ref[pl.ds(start, span), :] += dk_all[lc:nk]
        dv_ref[pl.ds(start, span), :] += dv_all[lc:nk]

    qblk = pl.BlockSpec((QB, 128), lambda i: (i, 0))
    full = pl.BlockSpec((t, 128), lambda i: (0, 0))
    small = pl.BlockSpec((8, 128), lambda i: (0, 0))
    return _pcall(
        body, name=name, grid=(t // QB,),
        in_specs=[pl.BlockSpec((QB, 512), lambda i: (i, 0)), full,
                  pl.BlockSpec((t, 128), lambda i: (0, 5)),
                  pl.BlockSpec((1, 128), lambda i: (0, 0)),
                  pl.BlockSpec(memory_space=pltpu.SMEM), qblk, qblk,
                  pl.BlockSpec((QB, 512), lambda i: (i, 0))],
        out_specs=[pl.BlockSpec((QB, 512), lambda i: (i, 0)), full, full, small, small],
        out_shape=[jax.ShapeDtypeStruct((t, 512), BF16), jax.ShapeDtypeStruct((t, 128), F32),
                   jax.ShapeDtypeStruct((t, 128), F32), jax.ShapeDtypeStruct((8, 128), F32),
                   jax.ShapeDtypeStruct((8, 128), F32)],
    )(p, kp, p, gq, sink, cos, sin, dmix)


def _tri(rev):
    r, c = _iota((CHUNK, CHUNK), 0), _iota((CHUNK, CHUNK), 1)
    return (c >= r) if rev else (c <= r)


def _blk_map(nb, rev, backward):
    if not rev:
        return (lambda n: nb - 1 - n) if backward else (lambda n: n)
    if backward:
        return lambda n: jnp.where(n < nb - 1, n + 1, 0)
    return lambda n: jnp.where(n == 0, 0, nb - n)


def _chunk_order(rev, backward, nc=TM // CHUNK):
    order = list(range(nc))
    return order[::-1] if (rev != backward) else order


def _hgrn_gates(qraw, fraw, lb):
    sq = _sigmoid(qraw)
    sf = _sigmoid(fraw)
    f = lb + (1.0 - lb) * sf
    return qraw * sq, 1.0 - f, jnp.log(f), sq, sf, f


HGRN_HP = 4


def _chunk_cumsum(x, rev):
    n = x.shape[0]
    pos = _iota(x.shape, 0) & (CHUNK - 1)
    s = 1
    while s < CHUNK:
        if rev:
            x = x + jnp.where(pos < CHUNK - s, pltpu.roll(x, n - s, 0), 0.0)
        else:
            x = x + jnp.where(pos >= s, pltpu.roll(x, s, 0), 0.0)
        s *= 2
    return x


def _block_terms(lf, rev):
    b = _chunk_cumsum(lf, rev)
    mid, last = (CHUNK // 2 - 1, 0) if rev else (CHUNK // 2, CHUNK - 1)

    def chunk_row(off):
        return jnp.concatenate([jnp.broadcast_to(b[c * CHUNK + off:c * CHUNK + off + 1, :], (CHUNK, b.shape[1]))
                                for c in range(TM // CHUNK)], axis=0)

    r, bl = chunk_row(mid), chunk_row(last)
    return _tri(rev), jnp.exp(b - r), jnp.exp(r - b), jnp.exp(b), jnp.exp(bl - b), jnp.exp(bl)


def _headnorm_apply(o, gv, gain):
    n = o * lax.rsqrt(jnp.mean(o * o, axis=-1, keepdims=True) + EPS)
    if gain is not None:
        n = n * gain
    return (n * (gv * _sigmoid(gv))).astype(BF16)


def _headnorm_grad(o, gv, dy, gain):
    rs = lax.rsqrt(jnp.mean(o * o, axis=-1, keepdims=True) + EPS)
    xh = o * rs
    n = xh * gain if gain is not None else xh
    sg = _sigmoid(gv)
    dn = dy * (gv * sg)
    dg = (dy * n * (sg * (1.0 + gv * (1.0 - sg)))).astype(BF16)
    dgain = jnp.sum(dn * xh, axis=0, keepdims=True)
    dxh = dn * gain if gain is not None else dn
    return rs * (dxh - xh * jnp.mean(dxh * xh, axis=-1, keepdims=True)), dg, dgain


def _hgrn_cols(bmap, n2, c0):
    return [pl.BlockSpec((TM, 256), lambda h, n, b=b: (bmap(n), c0 // 2 + h * n2 + b)) for b in range(n2)]


def _head_cols(refs, hh):
    return refs[hh // 2][:, 128 * (hh % 2):128 * (hh % 2) + 128]


def _hgrn_fwd(p, lb, *, rev, name, ofw=None, gain=None):
    t = p.shape[0]
    nb, nc = t // TM, TM // CHUNK
    bmap = _blk_map(nb, rev, False)
    fcol = 14 if rev else 10
    fused = ofw is not None

    n2 = HGRN_HP // 2

    def body(*refs):
        q_refs, f_refs, v_refs, lb_ref = refs[:n2], refs[n2:2 * n2], refs[2 * n2:3 * n2], refs[3 * n2]
        rest = refs[3 * n2 + 1:]
        if fused:
            ofw_ref, g_refs, gain_ref = rest[0], rest[1:1 + n2], rest[1 + n2]
            o_ref, sh_ref, mix_ref, st = rest[2 + n2:]
        else:
            o_ref, sh_ref, st = rest

        @pl.when(pl.program_id(1) == 0)
        def _():
            st[...] = jnp.zeros_like(st)
        for hh in range(HGRN_HP):
            ln = slice(128 * hh, 128 * hh + 128)
            q, k, lf, _, _, _ = _hgrn_gates(_head_cols(q_refs, hh), _head_cols(f_refs, hh), lb_ref[:, ln])
            tri, eq, ek, ei, eki, eb = _block_terms(lf, rev)
            qe, ke, qi, ki, vb = _bf(q * eq), _bf(k * ek), _bf(q * ei), _bf(k * eki), _bf(_head_cols(v_refs, hh))
            intra = []
            for cc in range(nc):
                rows = slice(cc * CHUNK, (cc + 1) * CHUNK)
                a = jnp.where(tri, _dot_nt(qe[rows], ke[rows]), 0.0)
                intra.append(_dot(a, vb[rows]))
            s = st[hh]
            for cc in _chunk_order(rev, False):
                rows = slice(cc * CHUNK, (cc + 1) * CHUNK)
                sh_ref[hh, cc] = s
                o_ref[rows, ln] = intra[cc] + _dot_nt(qi[rows], s)
                s = s * eb[cc * CHUNK:cc * CHUNK + 1, :] + _dot_tn(vb[rows], ki[rows])
            st[hh] = s
            if fused:
                osum = o_ref[:, ln] + ofw_ref[:, ln]
                o_ref[:, ln] = osum
                mix_ref[:, ln] = _headnorm_apply(osum, _head_cols(g_refs, hh), gain_ref[...])

    hp, wd = HGRN_HP, 128 * HGRN_HP
    col = functools.partial(_hgrn_cols, bmap, n2)
    oblk = pl.BlockSpec((TM, wd), lambda h, n: (bmap(n), h))
    ins = [p] * (3 * n2) + [lb]
    specs = col(6) + col(fcol) + col(18) + [pl.BlockSpec((1, wd), lambda h, n: (0, h))]
    out_specs = [oblk, pl.BlockSpec((hp, nc, 128, 128), lambda h, n: (h, bmap(n), 0, 0))]
    out_shape = [jax.ShapeDtypeStruct((t, 512), F32), jax.ShapeDtypeStruct((4, t // CHUNK, 128, 128), F32)]
    if fused:
        ins += [ofw] + [p] * n2 + [gain]
        specs += [oblk] + col(22) + [pl.BlockSpec((1, 128), lambda h, n: (0, 0))]
        out_specs.append(oblk)
        out_shape.append(jax.ShapeDtypeStruct((t, 512), BF16))
    return _pcall(body, name=name, grid=(4 // hp, nb), in_specs=specs, out_specs=out_specs, out_shape=out_shape,
                  scratch_shapes=[pltpu.VMEM((hp, 128, 128), F32)])(*ins)


def _hgrn_bwd(p, lb, sh, do, prev, *, rev, name, head=None):
    t = p.shape[0]
    nb, nc = t // TM, TM // CHUNK
    bmap = _blk_map(nb, rev, True)
    fcol = 14 if rev else 10
    has_prev = prev is not None
    odt = BF16
    fused = head is not None

    n2 = HGRN_HP // 2

    def body(*refs):
        refs = list(refs)
        q_refs, f_refs, v_refs = refs[:n2], refs[n2:2 * n2], refs[2 * n2:3 * n2]
        lb_ref, sh_ref = refs[3 * n2], refs[3 * n2 + 1]
        pos = 3 * n2 + 2
        if fused:
            osum_ref, g_refs, dmix_ref, gain_ref = refs[pos], refs[pos + 1:pos + 1 + n2], refs[pos + 1 + n2], refs[pos + 2 + n2]
            pos += 3 + n2
        else:
            do_ref = refs[pos]
            pos += 1
        if has_prev:
            pq_ref, pv_ref = refs[pos], refs[pos + 1]
            pos += 2
        dq_ref, df_ref, dv_ref, dlb_ref = refs[pos:pos + 4]
        pos += 4
        if fused:
            do_out, dg_ref, dgain_ref = refs[pos:pos + 3]
            pos += 3
        dst = refs[pos]

        @pl.when(pl.program_id(1) == 0)
        def _():
            dst[...] = jnp.zeros_like(dst)
            dlb_ref[...] = jnp.zeros_like(dlb_ref)

        if fused:
            @pl.when((pl.program_id(0) == 0) & (pl.program_id(1) == 0))
            def _():
                dgain_ref[...] = jnp.zeros_like(dgain_ref)

        cat = functools.partial(jnp.concatenate, axis=0)
        for hh in range(HGRN_HP):
            ln = slice(128 * hh, 128 * hh + 128)
            lbv = lb_ref[:, ln]
            qraw, fraw = _head_cols(q_refs, hh), _head_cols(f_refs, hh)
            q, k, lf, sq, sf, f = _hgrn_gates(qraw, fraw, lbv)
            tri, eq, ek, ei, eki, eb = _block_terms(lf, rev)
            qe, ke, qi, ki = q * eq, k * ek, q * ei, k * eki
            if fused:
                dov, dg, dgain = _headnorm_grad(osum_ref[:, ln], _head_cols(g_refs, hh), dmix_ref[:, ln], gain_ref[...])
                do_out[:, ln] = _bf(dov)
                dg_ref[:, ln] = dg
                _acc_row(dgain_ref, 0, dgain)
            else:
                dov = do_ref[:, ln]
            qeb, keb, qib, kib, vb, dob = _bf(qe), _bf(ke), _bf(qi), _bf(ki), _bf(_head_cols(v_refs, hh)), _bf(dov)
            dv, dqe, dke, dqi = [None] * nc, [None] * nc, [None] * nc, [None] * nc
            for cc in range(nc):
                rows = slice(cc * CHUNK, (cc + 1) * CHUNK)
                a = jnp.where(tri, _dot_nt(qeb[rows], keb[rows]), 0.0)
                da = jnp.where(tri, _dot_nt(dob[rows], vb[rows]), 0.0)
                dv[cc] = _dot_tn(a, dob[rows])
                dqe[cc], dke[cc] = _dot(da, keb[rows]), _dot_tn(da, qeb[rows])
                dqi[cc] = _dot(dob[rows], sh_ref[hh, cc])
            dki, dbl = [None] * nc, [None] * nc
            ds = dst[hh]
            for cc in _chunk_order(rev, True):
                rows = slice(cc * CHUNK, (cc + 1) * CHUNK)
                ebc = eb[cc * CHUNK:cc * CHUNK + 1, :]
                dv[cc] = dv[cc] + _dot_nt(kib[rows], ds)
                dki[cc] = _dot(vb[rows], ds)
                dbl[cc] = jnp.broadcast_to(jnp.sum(dki[cc] * ki[rows], axis=0, keepdims=True)
                                           + jnp.sum(ds * sh_ref[hh, cc], axis=0, keepdims=True) * ebc, (CHUNK, 128))
                ds = ds * ebc + _dot_tn(dob[rows], qib[rows])
            dst[hh] = ds
            dqe, dke, dqi, dki, dv, dbl = cat(dqe), cat(dke), cat(dqi), cat(dki), cat(dv), cat(dbl)
            dq = dqe * eq + dqi * ei
            dk = dke * ek + dki * eki
            last = 0 if rev else CHUNK - 1
            db = dqe * qe - dke * ke + dqi * qi - dki * ki
            db = db + jnp.where((_iota(db.shape, 0) & (CHUNK - 1)) == last, dbl, 0.0)
            dlf = _chunk_cumsum(db, not rev)
            dqr = dq * (sq * (1.0 + qraw * (1.0 - sq)))
            dfv = dlf / f - dk
            dfr = dfv * (1.0 - lbv) * (sf * (1.0 - sf))
            dlb_ref[:, ln] += jnp.sum(dfv * (1.0 - sf), axis=0, keepdims=True)
            if has_prev:
                dqr = dqr + pq_ref[:, ln]
                dv = dv + pv_ref[:, ln]
            dq_ref[:, ln] = dqr.astype(odt)
            df_ref[:, ln] = dfr.astype(odt)
            dv_ref[:, ln] = dv.astype(odt)

    hp, wd = HGRN_HP, 128 * HGRN_HP
    col = functools.partial(_hgrn_cols, bmap, n2)
    oblk = pl.BlockSpec((TM, wd), lambda h, n: (bmap(n), h))
    ins = [p] * (3 * n2) + [lb, sh]
    specs = col(6) + col(fcol) + col(18) + [pl.BlockSpec((1, wd), lambda h, n: (0, h)),
                                            pl.BlockSpec((hp, nc, 128, 128), lambda h, n: (h, bmap(n), 0, 0))]
    if fused:
        osum, dmix, gain = head
        ins += [osum] + [p] * n2 + [dmix, gain]
        specs += [oblk] + col(22) + [pl.BlockSpec((TM, wd), lambda h, n: (bmap(n), 4 // hp + h)),
                                     pl.BlockSpec((1, 128), lambda h, n: (0, 0))]
    else:
        ins.append(do); specs.append(oblk)
    if has_prev:
        ins += list(prev); specs += [oblk, oblk]
    out_specs = [oblk, oblk, oblk, pl.BlockSpec((1, wd), lambda h, n: (0, h))]
    out_shape = [jax.ShapeDtypeStruct((t, 512), odt)] * 3 + [jax.ShapeDtypeStruct((1, 512), F32)]
    if fused:
        out_specs += [oblk, oblk, pl.BlockSpec((8, 128), lambda h, n: (0, 0))]
        out_shape += [jax.ShapeDtypeStruct((t, 512), BF16), jax.ShapeDtypeStruct((t, 512), BF16),
                      jax.ShapeDtypeStruct((8, 128), F32)]
    return _pcall(body, name=name, grid=(4 // hp, nb), in_specs=specs, out_specs=out_specs, out_shape=out_shape,
                  scratch_shapes=[pltpu.VMEM((hp, 128, 128), F32)])(*ins)


def _rope256(x, cos, sin):
    x1, x2 = x[:, 0:128], x[:, 128:256]
    return jnp.concatenate([x1 * cos - x2 * sin, x2 * cos + x1 * sin], axis=-1)


def _rope256_t(d, cos, sin):
    d1, d2 = d[:, 0:128], d[:, 128:256]
    return jnp.concatenate([d1 * cos + d2 * sin, d2 * cos - d1 * sin], axis=-1)


RET_DK, RET_DV, RET_H = 256, 512, 4
RET_KSCALE = RET_DK ** -0.5
RCH = TM
RET_HP = 4


def _ret_terms(lg, rev):
    r, c = _iota((RCH, RCH), 0), _iota((RCH, RCH), 1)
    rel = ((c - r) if rev else (r - c)).astype(F32)
    dmat = jnp.where(rel >= 0, jnp.exp(lg[:, 0:1] * jnp.maximum(rel, 0.0)), 0.0)
    pos = _iota((RCH, 1), 0).astype(F32)
    cnt = (RCH - pos) if rev else (pos + 1.0)
    ei = jnp.exp(lg * cnt)
    eki = jnp.exp(lg * (RCH - cnt))
    eb = jnp.exp(lg * float(RCH))
    return dmat, ei, eki, eb


def _ret_fwd(p, lgt, cos, sin, *, rev, name, ofw=None):
    t = p.shape[0]
    nb, nc = t // TM, TM // RCH
    bmap = _blk_map(nb, rev, False)
    fused = ofw is not None

    def body(*refs):
        q_ref, k_ref, v_ref, lg_ref, c_ref, s_ref = refs[:6]
        if fused:
            ofw_ref, g_ref, o_ref, sh_ref, mix_ref, st = refs[6:]
        else:
            o_ref, sh_ref, st = refs[6:]

        @pl.when(pl.program_id(1) == 0)
        def _():
            st[...] = jnp.zeros_like(st)
        for hh in range(RET_HP):
            qc, vc = slice(RET_DK * hh, RET_DK * (hh + 1)), slice(RET_DV * hh, RET_DV * (hh + 1))
            dmat, ei, eki, eb = _ret_terms(lg_ref[hh], rev)
            for cc in _chunk_order(rev, False, nc):
                rows = slice(cc * RCH, (cc + 1) * RCH)
                cosv, sinv = c_ref[rows, :], s_ref[rows, :]
                q = _rope256(q_ref[rows, qc].astype(F32), cosv, sinv)
                k = _rope256(k_ref[rows, qc].astype(F32), cosv, sinv) * RET_KSCALE
                v = v_ref[rows, vc]
                s0 = st[hh]
                sh_ref[hh, cc] = s0.astype(BF16)
                a = _dot_nt(q, k) * dmat
                o = _dot(a, v) + _dot_nt(q * ei, s0)
                st[hh] = s0 * eb + _dot_tn(v, k * eki)
                if fused:
                    o = o + ofw_ref[rows, vc]
                    mix_ref[rows, vc] = _headnorm_apply(o, g_ref[rows, vc].astype(F32), None)
                o_ref[rows, vc] = o

    hp = RET_HP
    tab = pl.BlockSpec((TM, 128), lambda h, n: (bmap(n), 0))
    oblk = pl.BlockSpec((TM, hp * RET_DV), lambda h, n: (bmap(n), h))
    ins = [p, p, p, lgt, cos, sin]
    specs = [pl.BlockSpec((TM, hp * RET_DK), lambda h, n: (bmap(n), h)),
             pl.BlockSpec((TM, hp * RET_DK), lambda h, n: (bmap(n), RET_H // hp + h)),
             pl.BlockSpec((TM, hp * RET_DV), lambda h, n: (bmap(n), RET_H // hp + h)),
             pl.BlockSpec((hp, 1, RET_DK), lambda h, n: (h, 0, 0)), tab, tab]
    out_specs = [oblk, pl.BlockSpec((hp, nc, RET_DV, RET_DK), lambda h, n: (h, bmap(n), 0, 0))]
    out_shape = [jax.ShapeDtypeStruct((t, RET_H * RET_DV), F32),
                 jax.ShapeDtypeStruct((RET_H, t // RCH, RET_DV, RET_DK), BF16)]
    if fused:
        ins += [ofw, p]
        specs += [oblk, pl.BlockSpec((TM, hp * RET_DV), lambda h, n: (bmap(n), 2 * RET_H // hp + h))]
        out_specs.append(oblk)
        out_shape.append(jax.ShapeDtypeStruct((t, RET_H * RET_DV), BF16))
    return _pcall(body, name=name, grid=(RET_H // hp, nb), in_specs=specs, out_specs=out_specs, out_shape=out_shape,
                  scratch_shapes=[pltpu.VMEM((hp, RET_DV, RET_DK), F32)])(*ins)


def _ret_bwd(p, lgt, cos, sin, sh, do, prev, *, rev, name, head=None):
    t = p.shape[0]
    nb, nc = t // TM, TM // RCH
    bmap = _blk_map(nb, rev, True)
    has_prev = prev is not None
    odt = BF16
    fused = head is not None

    def body(*refs):
        refs = list(refs)
        q_ref, k_ref, v_ref, lg_ref, c_ref, s_ref, sh_ref = refs[:7]
        if fused:
            osum_ref, g_ref, dmix_ref = refs[7:10]
            pos = 10
        else:
            do_ref = refs[7]
            pos = 8
        if has_prev:
            pq_ref, pk_ref, pv_ref = refs[pos:pos + 3]
            pos += 3
        dq_ref, dk_ref, dv_ref = refs[pos:pos + 3]
        pos += 3
        if fused:
            do_out, dg_ref = refs[pos:pos + 2]
            pos += 2
        dst = refs[pos]

        @pl.when(pl.program_id(1) == 0)
        def _():
            dst[...] = jnp.zeros_like(dst)

        for hh in range(RET_HP):
            qc, vc = slice(RET_DK * hh, RET_DK * (hh + 1)), slice(RET_DV * hh, RET_DV * (hh + 1))
            dmat, ei, eki, eb = _ret_terms(lg_ref[hh], rev)
            for cc in _chunk_order(rev, True, nc):
                rows = slice(cc * RCH, (cc + 1) * RCH)
                cosv, sinv = c_ref[rows, :], s_ref[rows, :]
                q = _rope256(q_ref[rows, qc].astype(F32), cosv, sinv)
                k = _rope256(k_ref[rows, qc].astype(F32), cosv, sinv) * RET_KSCALE
                v = v_ref[rows, vc]
                if fused:
                    dov, dg, _ = _headnorm_grad(osum_ref[rows, vc], g_ref[rows, vc].astype(F32), dmix_ref[rows, vc], None)
                    do_out[rows, vc] = _bf(dov)
                    dg_ref[rows, vc] = dg
                else:
                    dov = do_ref[rows, vc]
                s0 = sh_ref[hh, cc]
                dsc = dst[hh]
                qi, ki = q * ei, k * eki
                a = _dot_nt(q, k) * dmat
                da = _dot_nt(dov, v) * dmat
                dv = _dot_tn(a, dov) + _dot_nt(ki, dsc)
                dqs = _dot(da, k) + _dot(dov, s0) * ei
                dks = _dot_tn(da, q) + _dot(v, dsc) * eki
                dst[hh] = dsc * eb + _dot_tn(dov, qi)
                dq = _rope256_t(dqs, cosv, sinv)
                dk = _rope256_t(dks * RET_KSCALE, cosv, sinv)
                if has_prev:
                    dq = dq + pq_ref[rows, qc]
                    dk = dk + pk_ref[rows, qc]
                    dv = dv + pv_ref[rows, vc]
                dq_ref[rows, qc] = dq.astype(odt)
                dk_ref[rows, qc] = dk.astype(odt)
                dv_ref[rows, vc] = dv.astype(odt)

    hp = RET_HP
    tab = pl.BlockSpec((TM, 128), lambda h, n: (bmap(n), 0))
    qblk = pl.BlockSpec((TM, hp * RET_DK), lambda h, n: (bmap(n), h))
    vblk = pl.BlockSpec((TM, hp * RET_DV), lambda h, n: (bmap(n), h))
    ins = [p, p, p, lgt, cos, sin, sh]
    specs = [qblk, pl.BlockSpec((TM, hp * RET_DK), lambda h, n: (bmap(n), RET_H // hp + h)),
             pl.BlockSpec((TM, hp * RET_DV), lambda h, n: (bmap(n), RET_H // hp + h)),
             pl.BlockSpec((hp, 1, RET_DK), lambda h, n: (h, 0, 0)), tab, tab,
             pl.BlockSpec((hp, nc, RET_DV, RET_DK), lambda h, n: (h, bmap(n), 0, 0))]
    if fused:
        osum, dmix = head
        ins += [osum, p, dmix]
        specs += [vblk, pl.BlockSpec((TM, hp * RET_DV), lambda h, n: (bmap(n), 2 * RET_H // hp + h)), vblk]
    else:
        ins.append(do); specs.append(vblk)
    if has_prev:
        ins += list(prev); specs += [qblk, qblk, vblk]
    out_specs = [qblk, qblk, vblk]
    out_shape = [jax.ShapeDtypeStruct((t, RET_H * RET_DK), odt), jax.ShapeDtypeStruct((t, RET_H * RET_DK), odt),
                 jax.ShapeDtypeStruct((t, RET_H * RET_DV), odt)]
    if fused:
        out_specs += [vblk, vblk]
        out_shape += [jax.ShapeDtypeStruct((t, RET_H * RET_DV), BF16), jax.ShapeDtypeStruct((t, RET_H * RET_DV), BF16)]
    return _pcall(body, name=name, grid=(RET_H // hp, nb), in_specs=specs, out_specs=out_specs, out_shape=out_shape,
                  scratch_shapes=[pltpu.VMEM((hp, RET_DV, RET_DK), F32)])(*ins)


def _rope_tables(lc, l):
    tt = jnp.arange(l)
    row, colp = (tt // 64).astype(F32), (tt % 64).astype(F32)
    inv = 10000.0 ** (-jnp.arange(16, dtype=F32) / 16)
    ang = jnp.concatenate([row[:, None] * inv, colp[:, None] * inv], axis=-1)
    ang = jnp.concatenate([jnp.zeros((lc, 32), F32), ang], axis=0)
    acos, asin = jnp.tile(jnp.cos(ang), (1, 4)), jnp.tile(jnp.sin(ang), (1, 4))
    theta = 1.0 / (10000.0 ** jnp.linspace(0.0, 1.0, 128, dtype=F32))
    rang = jnp.arange(l, dtype=F32)[:, None] * theta
    rang = jnp.concatenate([jnp.zeros((lc, 128), F32), rang], axis=0)
    return acos, asin, jnp.cos(rang), jnp.sin(rang)


class _Weights:
    def __init__(self, w):
        self.w = w

    def landed(self, grp, after):
        pass

    def full(self, grp, after):
        return self.w

    def send_grads(self, grp, grads):
        return jnp.zeros((8, 128), F32)


def _local_step(x0, target, mods, ng, wsrc, small):
    t, d = x0.shape
    l = target.shape[0]
    lc = t - l
    acos, asin, rcos, rsin = _rope_tables(lc, l)
    lg_fw = jnp.log(1.0 - 2.0 ** (-5.0 - jnp.arange(RET_H, dtype=F32)))
    lgt_fw = jnp.broadcast_to(lg_fw[:, None, None], (RET_H, 1, RET_DK))
    lgt_bw = jnp.broadcast_to(lg_fw[::-1][:, None, None], (RET_H, 1, RET_DK))
    gq, gk, sink, gain, lb = small['gq'], small['gk'], small['sink'], small['gain'], small['lb']

    (h1,) = _row_fwd(x0, mods, g=ng[0], shift=0, scale=1, name='l0_norm1')
    wsrc.landed('even', h1)
    w = dict(wsrc.full('even', h1))
    p0 = _mm_nn(h1, w['even_in'], name='l0_in')
    kp = _kprep_fwd(p0, gk, acos, asin, name='l0_kprep')
    att = _attn_fwd(p0, kp, gq, sink, acos, asin, lc=lc, name='l0_attn')
    wsrc.landed('ffn', att)
    hof, hsf = _hgrn_fwd(p0, lb, rev=False, name='l0_hgrn_f')
    wsrc.landed('odd', hof)
    hos, hsb, bmix = _hgrn_fwd(p0, lb, rev=True, name='l0_hgrn_b', ofw=hof, gain=gain)
    mix0 = [att, bmix]
    y0 = _mm_nn(mix0, w['even_out'], name='l0_out')
    x1, h2 = _row_fwd(x0, mods, y=y0, gate=2, g=ng[1], shift=3, scale=4, name='l0_norm2')
    w.update(wsrc.full('ffn', h2))
    u0, a0 = _ffn_in(h2, w['ffn_in'], lead=0, name='ffn_in')
    z0 = _mm_nn(a0, w['ffn_out'], lead=0, name='ffn_out')
    x2, h3 = _row_fwd(x1, mods, y=z0, gate=5, g=ng[2], shift=12, scale=13, name='l1_norm1')
    w.update(wsrc.full('odd', h3))
    p1 = _mm_nn(h3, w['odd_in'], out_dtype=BF16, name='l1_in')
    rof, rsf = _ret_fwd(p1, lgt_fw, rcos, rsin, rev=False, name='l1_ret_f')
    ros, rsb, mix1 = _ret_fwd(p1, lgt_bw, rcos, rsin, rev=True, name='l1_ret_b', ofw=rof)
    y1 = _mm_nn(mix1, w['odd_out'], name='l1_out')
    x3, h4 = _row_fwd(x2, mods, y=y1, gate=14, g=ng[3], shift=15, scale=16, name='l1_norm2')
    u1, a1 = _ffn_in(h4, w['ffn_in'], lead=1, name='ffn_in')
    z1 = _mm_nn(a1, w['ffn_out'], lead=1, name='ffn_out')
    loss, dx4, dz1, s_fin = _row_final(x3, z1, mods, target, gate=17, name='loss')

    du1 = _ffn_dx(dz1, w['ffn_out'], u1, lead=1, name='ffn_out_dx')
    g_ffn_out1 = _mm_tn(a1, dz1, name='ffn_out_dw')
    dh4 = _mm_nt(du1, w['ffn_in'], lead=1, name='ffn_in_dx')
    g_ffn_in1 = _mm_tn(h4, du1, name='ffn_in_dw')
    dx3, dy1, s_l1n2 = _row_bwd(x3, dx4, dh4, mods, ng[3], shift=15, scale=16, y=y1, gate=14, name='l1_norm2_bwd')
    dmix1 = _mm_nt(dy1, w['odd_out'], name='l1_out_dx')
    g_odd_out = _mm_tn(mix1, dy1, name='l1_out_dw')
    rdq, rdk, rdv, rdo, rdg = _ret_bwd(p1, lgt_fw, rcos, rsin, rsf, None, None, rev=False, name='l1_ret_f_bwd',
                                       head=(ros, dmix1))
    rdq, rdk, rdv = _ret_bwd(p1, lgt_bw, rcos, rsin, rsb, rdo, (rdq, rdk, rdv), rev=True, name='l1_ret_b_bwd')
    dp1 = [rdq, rdk, rdv, rdg]
    dh3 = _mm_nt(dp1, w['odd_in'], name='l1_in_dx')
    g_odd_in = _mm_tn(h3, dp1, name='l1_in_dw')
    mods = mods + wsrc.send_grads('early', dict(ffn_in1=g_ffn_in1, ffn_out1=g_ffn_out1, odd_in=g_odd_in,
                                                odd_out=g_odd_out))[0, 0]
    dx2, dz0, s_l1n1 = _row_bwd(x2, dx3, dh3, mods, ng[2], shift=12, scale=13, y=z0, gate=5, name='l1_norm1_bwd')
    du0 = _ffn_dx(dz0, w['ffn_out'], u0, lead=0, name='ffn_out_dx')
    g_ffn_out0 = _mm_tn(a0, dz0, name='ffn_out_dw')
    dh2 = _mm_nt(du0, w['ffn_in'], lead=0, name='ffn_in_dx')
    g_ffn_in0 = _mm_tn(h2, du0, name='ffn_in_dw')
    mods = mods + wsrc.send_grads('mid', dict(ffn_in0=g_ffn_in0, ffn_out0=g_ffn_out0))[0, 0]
    dx1, dy0, s_l0n2 = _row_bwd(x1, dx2, dh2, mods, ng[1], shift=3, scale=4, y=y0, gate=2, name='l0_norm2_bwd')
    dmix0 = _mm_nt(dy0, w['even_out'], name='l0_out_dx')
    g_even_out = _mm_tn(mix0, dy0, name='l0_out_dw')
    hq, hff, hv, dlb_f, hdo, hdg, s_gain = _hgrn_bwd(p0, lb, hsf, None, None, rev=False, name='l0_hgrn_f_bwd',
                                                     head=(hos, dmix0, gain))
    hq, hfb, hv, dlb_b = _hgrn_bwd(p0, lb, hsb, hdo, (hq, hv), rev=True, name='l0_hgrn_b_bwd')
    adq, dkp, adv, s_gq, s_sink = _attn_bwd(p0, kp, gq, sink, acos, asin, dmix0, lc=lc, name='l0_attn_bwd')
    dkv, s_gk = _kprep_bwd(p0, gk, acos, asin, dkp, adv, name='l0_kprep_bwd')
    dp0 = jnp.concatenate([adq, dkv, hq, _bf(hff), hfb, hv, hdg], axis=1)
    dh1 = _mm_nt(dp0, w['even_in'], name='l0_in_dx')
    g_even_in = _mm_tn(h1, dp0, name='l0_in_dw')
    dx0, s_l0n1 = _row_bwd(x0, dx1, dh1, mods, ng[0], shift=0, scale=1, latent_only=True, name='l0_norm1_bwd')

    grads = dict(ffn_in0=g_ffn_in0, ffn_in1=g_ffn_in1, ffn_out0=g_ffn_out0, ffn_out1=g_ffn_out1,
                 even_in=g_even_in, even_out=g_even_out, odd_in=g_odd_in, odd_out=g_odd_out)
    sums = dict(fin=s_fin, l1n2=s_l1n2, l1n1=s_l1n1, l0n2=s_l0n2, l0n1=s_l0n1, gain=s_gain, gq=s_gq, gk=s_gk,
                sink=s_sink, dlb_f=dlb_f, dlb_b=dlb_b)
    return loss, dx0, grads, sums


def _place():
    return lax.axis_index("x"), lax.axis_index("y"), lax.axis_index("c")


def _ag8(blk, *, name):
    r, c = blk.shape
    flips = [(dx, dy, dc) for dx in (0, 1) for dy in (0, 1) for dc in (0, 1) if (dx, dy, dc) != (0, 0, 0)]

    def body(x_ref, out_ref, send_sems, recv_sems, local_sem):
        ax, ay, ac = _place()
        me = 4 * ax + 2 * ay + ac
        mine = pltpu.make_async_copy(x_ref, out_ref.at[me], local_sem)
        mine.start()
        sent = []
        for k, (dx, dy, dc) in enumerate(flips):
            peer = (lax.rem(ax + dx, 2), lax.rem(ay + dy, 2), lax.rem(ac + dc, 2))
            cp = pltpu.make_async_remote_copy(src_ref=x_ref, dst_ref=out_ref.at[me], send_sem=send_sems.at[k],
                                              recv_sem=recv_sems.at[k], device_id=peer, device_id_type=MESH)
            cp.start()
            sent.append((cp, 4 * peer[0] + 2 * peer[1] + peer[2]))
        for k, (cp, pidx) in enumerate(sent):
            pltpu.make_async_remote_copy(src_ref=x_ref, dst_ref=out_ref.at[pidx], send_sem=send_sems.at[k],
                                         recv_sem=recv_sems.at[k], device_id=(ax, ay, ac),
                                         device_id_type=MESH).wait_recv()
        for cp, _ in sent:
            cp.wait_send()
        mine.wait()

    return _pcall(
        body, name=name,
        in_specs=[pl.BlockSpec(memory_space=pltpu.VMEM)],
        out_specs=pl.BlockSpec(memory_space=pltpu.VMEM),
        out_shape=jax.ShapeDtypeStruct((8, r, c), blk.dtype),
        scratch_shapes=[pltpu.SemaphoreType.DMA((7,)), pltpu.SemaphoreType.DMA((7,)), pltpu.SemaphoreType.DMA],
    )(blk)


_HBM = pl.BlockSpec(memory_space=pltpu.HBM)
_SEM = pl.BlockSpec(memory_space=pltpu.SEMAPHORE)
_DATAFLOW = pltpu.SideEffectType.DATAFLOW_SIDE_EFFECTING


def _split_start(bufs, plan, k, *, name):
    n = len(bufs)

    def body(*refs):
        ins, send_sems, recv_sems, token = refs[:n], refs[n], refs[n + 1], refs[2 * n + 2]
        for i, (src, dst, dev) in enumerate(plan(ins)):
            pltpu.make_async_remote_copy(src_ref=src, dst_ref=dst, send_sem=send_sems.at[i], recv_sem=recv_sems.at[i],
                                         device_id=dev, device_id_type=MESH).start()
        token[...] = jnp.zeros_like(token)

    res = _pcall(
        body, name=name,
        out_shape=(pltpu.SemaphoreType.DMA((k,)), pltpu.SemaphoreType.DMA((k,)),
                   *[pltpu.HBM(b.shape, b.dtype) for b in bufs], jax.ShapeDtypeStruct((8, 128), F32)),
        in_specs=[_HBM] * n, out_specs=(_SEM, _SEM, *[_HBM] * n, pl.BlockSpec(memory_space=pltpu.VMEM)),
        input_output_aliases={i: 2 + i for i in range(n)},
        compiler_params=pltpu.CompilerParams(has_side_effects=_DATAFLOW),
    )(*[pltpu.with_memory_space_constraint(b, pltpu.HBM) for b in bufs])
    return res[0], res[1], list(res[2:2 + n]), res[2 + n]


def _split_wait(bufs, send_sems, recv_sems, plan, after, *, name):
    n = len(bufs)

    def body(*refs):
        ins, ssem, rsem = refs[:n], refs[n], refs[n + 1]
        for i, (src, dst, dev) in enumerate(plan(ins)):
            cp = pltpu.make_async_remote_copy(src_ref=src, dst_ref=dst, send_sem=ssem.at[i], recv_sem=rsem.at[i],
                                              device_id=dev, device_id_type=MESH)
            cp.wait_send()
            cp.wait_recv()

    res = _pcall(
        body, name=name, out_shape=tuple(pltpu.HBM(b.shape, b.dtype) for b in bufs),
        in_specs=[_HBM] * n + [_SEM, _SEM, pl.BlockSpec(memory_space=pl.ANY)], out_specs=tuple([_HBM] * n),
        input_output_aliases={i: i for i in range(n)},
        compiler_params=pltpu.CompilerParams(has_side_effects=_DATAFLOW),
    )(*bufs, send_sems, recv_sems, after)
    return list(res)


_CHIP_FLIPS = [(1, 0), (0, 1), (1, 1)]


class _GatheredWeights:
    GROUPS = (('even', ('even_in', 'even_out')), ('ffn', ('ffn_in', 'ffn_out')), ('odd', ('odd_in', 'odd_out')))

    def __init__(self, shards, reducer):
        self.shards = shards
        self.send_grads = reducer.start
        self.ici, self.d2d, self.token = {}, {}, None
        for grp, names in self.GROUPS:
            src = [shards[nm].reshape(2, shards[nm].shape[0] // 2, shards[nm].shape[1]) for nm in names]
            land = [lax.empty((4,) + a.shape, a.dtype) for a in src]
            m = len(names)
            sends, recvs, bufs, token = _split_start(src + land, functools.partial(self._ici_plan, m, True), 4 * m,
                                                     name='gather_' + grp + '_ici_start')
            self.ici[grp] = (sends, recvs, bufs, m)
            self.token = token if self.token is None else self.token + token

    @staticmethod
    def _ici_plan(m, sending, refs):
        ax, ay, ac = _place()
        s = 2 * ax + ay
        out = []
        for a in range(m):
            for dx, dy in _CHIP_FLIPS:
                px, py = lax.rem(ax + dx, 2), lax.rem(ay + dy, 2)
                slot = s if sending else 2 * px + py
                out.append((refs[a].at[ac], refs[m + a].at[slot, ac], (px, py, ac)))
        for a in range(m):
            out.append((refs[a], refs[m + a].at[s], (ax, ay, 1 - ac)))
        return out

    @staticmethod
    def _d2d_plan(m, sending, refs):
        ax, ay, ac = _place()
        out = []
        for a in range(m):
            for dx, dy in _CHIP_FLIPS:
                sp = 2 * lax.rem(ax + dx, 2) + lax.rem(ay + dy, 2)
                out.append((refs[a].at[sp, ac], refs[a].at[sp, ac if sending else 1 - ac], (ax, ay, 1 - ac)))
        return out

    def landed(self, grp, after):
        sends, recvs, bufs, m = self.ici[grp]
        bufs = _split_wait(bufs, sends, recvs, functools.partial(self._ici_plan, m, False), after,
                           name='gather_' + grp + '_ici_wait')
        sends, recvs, land, _ = _split_start(bufs[m:], functools.partial(self._d2d_plan, m, True), 3 * m,
                                             name='gather_' + grp + '_d2d_start')
        self.d2d[grp] = (sends, recvs, land, m)

    def full(self, grp, after):
        sends, recvs, land, m = self.d2d[grp]
        land = _split_wait(land, sends, recvs, functools.partial(self._d2d_plan, m, False), after,
                           name='gather_' + grp + '_d2d_wait')
        names = dict(self.GROUPS)[grp]
        return {nm: _from_shards(nm, g.reshape((4,) + self.shards[nm].shape)) for nm, g in zip(names, land)}


def _to_sibling(arrs, *, name):
    n = len(arrs)

    def body(*refs):
        ins, outs = refs[:n], refs[n:2 * n]
        send_sems, recv_sems = refs[2 * n:]
        ax, ay, ac = _place()
        cps = [pltpu.make_async_remote_copy(src_ref=ins[a], dst_ref=outs[a], send_sem=send_sems.at[a],
                                            recv_sem=recv_sems.at[a], device_id=(ax, ay, 1 - ac),
                                            device_id_type=MESH) for a in range(n)]
        for cp in cps:
            cp.start()
        for cp in cps:
            cp.wait_recv()
        for cp in cps:
            cp.wait_send()

    hbm = pl.BlockSpec(memory_space=pl.ANY)
    return _pcall(
        body, name=name, in_specs=[hbm] * n, out_specs=[hbm] * n,
        out_shape=[jax.ShapeDtypeStruct(a.shape, a.dtype) for a in arrs],
        scratch_shapes=[pltpu.SemaphoreType.DMA((n,))] * 2,
    )(*arrs)


def _mod_fwd(cond_raw, mw, mb, *, name):
    _, d, n = mw.shape

    def body(c_ref, w_ref, b_ref, o_ref):
        cv = c_ref[...]
        o_ref[...] = _dot(cv * _sigmoid(cv), w_ref[...]) + b_ref[...]

    return _pcall(
        body, name=name, grid=(2,),
        in_specs=[pl.BlockSpec((16, d), lambda l: (0, 0)), pl.BlockSpec((None, d, n), lambda l: (l, 0, 0)),
                  pl.BlockSpec((None, 1, n), lambda l: (l, 0, 0))],
        out_specs=pl.BlockSpec((None, 16, n), lambda l: (l, 0, 0)),
        out_shape=jax.ShapeDtypeStruct((2, 16, n), F32),
    )(cond_raw, mw, mb)


def _mod_bwd(cond_raw, dms, mw, *, name):
    _, d, n = mw.shape

    def body(c_ref, dm_ref, w_ref, gw_ref, dc_ref):
        @pl.when(pl.program_id(0) == 0)
        def _():
            dc_ref[...] = jnp.zeros_like(dc_ref)
        cv = c_ref[...]
        gw_ref[...] = _dot_tn(cv * _sigmoid(cv), dm_ref[...])
        dc_ref[...] += _dot_nt(dm_ref[...], w_ref[...])

    return _pcall(
        body, name=name, grid=(2,),
        in_specs=[pl.BlockSpec((16, d), lambda l: (0, 0)), pl.BlockSpec((None, 16, n), lambda l: (l, 0, 0)),
                  pl.BlockSpec((None, d, n), lambda l: (l, 0, 0))],
        out_specs=[pl.BlockSpec((None, d, n), lambda l: (l, 0, 0)), pl.BlockSpec((16, d), lambda l: (0, 0))],
        out_shape=[jax.ShapeDtypeStruct((2, d, n), F32), jax.ShapeDtypeStruct((16, d), F32)],
    )(cond_raw, dms, mw)


def _lb_fwd(hgrn_lb, *, name):
    def body(a_ref, o_ref):
        a0, a1 = a_ref[0:1, :], a_ref[1:2, :]
        m = jnp.maximum(a0, a1)
        e0, e1 = jnp.exp(a0 - m), jnp.exp(a1 - m)
        o_ref[...] = e0 / (e0 + e1)

    return _pcall(body, name=name, out_shape=jax.ShapeDtypeStruct((1, hgrn_lb.shape[1]), F32))(hgrn_lb)


PACK_TILES = ('l0n1', 'l0n2', 'l1n1', 'l1n2', 'fin')
PACK_SINGLES = ('gq', 'gk', 'gain', 'dlb_f', 'dlb_b', 'sink')
PACK_ROW = {nm: 8 * i for i, nm in enumerate(PACK_TILES)}
PACK_ROW.update({nm: 8 * len(PACK_TILES) + i for i, nm in enumerate(PACK_SINGLES)})
MOD_SOURCE = ((('l0n1', 0), ('l0n1', 1), ('l0n2', 2), ('l0n2', 0), ('l0n2', 1), ('l1n1', 2)),
              (('l1n1', 0), ('l1n1', 1), ('l1n2', 2), ('l1n2', 0), ('l1n2', 1), ('fin', 2)))


def _small_finalize(gath, lb_pad, *, name):
    d = gath.shape[2]

    def body(g_ref, lb_ref, small_ref, glb_ref, gmb_ref, dm_ref):
        tot = g_ref[0]
        for e in range(1, 8):
            tot = tot + g_ref[e]

        def row(nm, r=0):
            return tot[PACK_ROW[nm] + r:PACK_ROW[nm] + r + 1, :]

        for k, nm in enumerate(('l0n1', 'l0n2', 'l1n1', 'l1n2')):
            small_ref[k:k + 1, :] = row(nm, 3) + row(nm, 7)
        for k, nm in ((4, 'gq'), (5, 'gk')):
            small_ref[k:k + 1, :] = row(nm) + pltpu.roll(row(nm), d - 64, 1)
        small_ref[6:7, :] = row('gain')
        small_ref[7:8, :] = row('sink')
        lbv = lb_ref[...]
        g0 = (row('dlb_f') + row('dlb_b')) * lbv * (1.0 - lbv)
        glb_ref[...] = jnp.zeros_like(glb_ref)
        glb_ref[0:1, :] = g0
        glb_ref[1:2, :] = -g0
        dm_ref[...] = jnp.zeros_like(dm_ref)
        for l in range(2):
            for part in range(6):
                nm, r = MOD_SOURCE[l][part]
                gmb_ref[l * 6 + part:l * 6 + part + 1, :] = row(nm, r) + row(nm, r + 4)
                rl = PACK_ROW[nm] + r + 4
                for e in range(8):
                    dm_ref[l, part, e:e + 1, :] = g_ref[e, rl:rl + 1, :]
                dm_ref[l, part, 8:9, :] = row(nm, r)

    return _pcall(
        body, name=name,
        out_shape=[jax.ShapeDtypeStruct((8, d), F32), jax.ShapeDtypeStruct((8, d), F32),
                   jax.ShapeDtypeStruct((12, d), F32), jax.ShapeDtypeStruct((2, 6, 16, d), F32)],
    )(gath, lb_pad)


def _cctx_grad(gath, c_ctx2, *, name):
    def body(g_ref, c_ref, o_ref):
        tot = ((g_ref[0, 0:1, :] + g_ref[2, 0:1, :]) + g_ref[4, 0:1, :]) + g_ref[6, 0:1, :]
        cv = c_ref[...]
        s = _sigmoid(cv)
        o_ref[...] = tot * (s * (1.0 + cv * (1.0 - s)))

    return _pcall(body, name=name, out_shape=jax.ShapeDtypeStruct(c_ctx2.shape, F32))(gath, c_ctx2)


def _row_block(r, c, limit=256 * 1024):
    best = None
    for br in range(16, r + 1, 16):
        if r % br == 0 and br * c <= limit:
            best = br
    return best if best is not None else r


def _sum4(own, landed, core, *, name):
    _, r, c = own.shape
    br = _row_block(r, c, 512 * 1024)

    def body(core_ref, own_ref, land_ref, o_ref):
        s = 2 * lax.axis_index("x") + lax.axis_index("y")
        p = [jnp.where(s == k, own_ref[k], land_ref[k]).astype(F32) for k in range(4)]
        o_ref[...] = ((p[0] + p[1]) + p[2]) + p[3]

    blk = pl.BlockSpec((4, br, c), lambda i, core_ref: (0, i, 0))
    spec = pltpu.PrefetchScalarGridSpec(
        num_scalar_prefetch=1, grid=(r // br,), in_specs=[blk, blk],
        out_specs=pl.BlockSpec((None, br, c), lambda i, core_ref: (core_ref[0], i, 0)))
    return _pcall(body, name=name, grid_spec=spec, out_shape=jax.ShapeDtypeStruct((2, r, c), F32))(core, own, landed)


def _exchange_halves(arrs, *, name):
    n = len(arrs)

    def body(*refs):
        ins, outs = refs[:n], refs[n:2 * n]
        send_sems, recv_sems = refs[2 * n:]
        ax, ay, ac = _place()
        cps = [pltpu.make_async_remote_copy(src_ref=ins[a].at[ac], dst_ref=outs[a].at[ac], send_sem=send_sems.at[a],
                                            recv_sem=recv_sems.at[a], device_id=(ax, ay, 1 - ac),
                                            device_id_type=MESH) for a in range(n)]
        for cp in cps:
            cp.start()
        for a in range(n):
            pltpu.make_async_remote_copy(src_ref=ins[a].at[ac], dst_ref=outs[a].at[1 - ac], send_sem=send_sems.at[a],
                                         recv_sem=recv_sems.at[a], device_id=(ax, ay, ac),
                                         device_id_type=MESH).wait_recv()
        for cp in cps:
            cp.wait_send()

    hbm = pl.BlockSpec(memory_space=pl.ANY)
    return _pcall(
        body, name=name, in_specs=[hbm] * n, out_specs=[hbm] * n,
        out_shape=[jax.ShapeDtypeStruct(a.shape, a.dtype) for a in arrs],
        input_output_aliases={a: a for a in range(n)},
        scratch_shapes=[pltpu.SemaphoreType.DMA((n,))] * 2,
    )(*arrs)


def _add2(a, b, *, name):
    r, c = a.shape
    br = _row_block(r, c, 1024 * 1024)

    def body(a_ref, b_ref, o_ref):
        o_ref[...] = (a_ref[...].astype(F32) + b_ref[...].astype(F32)).astype(BF16)

    blk = pl.BlockSpec((br, c), lambda i: (i, 0))
    return _pcall(body, name=name, grid=(r // br,), in_specs=[blk, blk], out_specs=blk,
                  out_shape=jax.ShapeDtypeStruct((r, c), BF16))(a, b)


def _adam(w, gs, m, v, *, name):
    r, c = w.shape
    br = _row_block(r, c)
    ng = len(gs)
    c1 = 1.0 - ADAM_B1 ** ADAM_STEP
    c2 = 1.0 - ADAM_B2 ** ADAM_STEP

    def body(*refs):
        w_ref, m_ref, v_ref = refs[0], refs[1 + ng], refs[2 + ng]
        outs = refs[3 + ng:]
        g = refs[1][...]
        for k in range(1, ng):
            g = g + refs[1 + k][...]
        mn = ADAM_B1 * m_ref[...] + (1.0 - ADAM_B1) * g
        vn = ADAM_B2 * v_ref[...] + (1.0 - ADAM_B2) * (g * g)
        if ng > 1:
            outs[0][...] = g
        d_out, m_out, v_out = outs[-3:]
        m_out[...] = mn
        v_out[...] = vn
        d_out[...] = -ADAM_LR * ((mn / c1) / (jnp.sqrt(vn / c2) + ADAM_EPS) + ADAM_WD * w_ref[...])

    blk = pl.BlockSpec((br, c), lambda i: (i, 0))
    nout = 4 if ng > 1 else 3
    res = _pcall(body, name=name, grid=(r // br,), in_specs=[blk] * (3 + ng), out_specs=[blk] * nout,
                 out_shape=[jax.ShapeDtypeStruct((r, c), F32)] * nout)(w, *gs, m, v)
    return list(res) if ng > 1 else [gs[0]] + list(res)


def _grad_halves(name, g, ac):
    if name.endswith('_in'):
        n = g.shape[1] // 4
        if name == 'ffn_in':
            assert n == FFN_BK
        order = _ffn_order(g.shape[1]) if name == 'ffn_in' else range(4)
        v = jnp.stack([g[:, b * n:(b + 1) * n] for b in order])
        per = [v[:, :g.shape[0] // 2], v[:, g.shape[0] // 2:]]
    else:
        k4, n = g.shape
        v = g.reshape(4, 2, k4 // 8, n)
        per = [v[:, 0], v[:, 1]]
    first = ac == 0
    return _bf(jnp.where(first, per[0], per[1])), _bf(jnp.where(first, per[1], per[0]))


class _GradReducer:
    def __init__(self):
        self.flight = {}

    @staticmethod
    def _plan(m, sending, refs):
        ax, ay, ac = _place()
        s = 2 * ax + ay
        out = []
        for a in range(m):
            for dx, dy in _CHIP_FLIPS:
                px, py = lax.rem(ax + dx, 2), lax.rem(ay + dy, 2)
                sp = 2 * px + py
                out.append((refs[a].at[sp], refs[m + a].at[s if sending else sp], (px, py, ac)))
        return out

    def start(self, grp, grads):
        ac = lax.axis_index("c")
        names = list(grads)
        halves = [_grad_halves(nm.rstrip('01'), grads[nm], ac) for nm in names]
        theirs = _to_sibling([h[1] for h in halves], name='swap_core_halves_' + grp)
        pair = [_add2(h[0].reshape(-1, b.shape[-1]), b.reshape(-1, b.shape[-1]), name='add_cores').reshape(b.shape)
                for h, b in zip(halves, theirs)]
        m = len(names)
        land = [lax.empty(a.shape, a.dtype) for a in pair]
        sends, recvs, bufs, token = _split_start(pair + land, functools.partial(self._plan, m, True), 3 * m,
                                                 name='scatter_' + grp + '_start')
        self.flight[grp] = (names, sends, recvs, bufs)
        return token

    def finish(self, grp, after):
        names, sends, recvs, bufs = self.flight.pop(grp)
        m = len(names)
        bufs = _split_wait(bufs, sends, recvs, functools.partial(self._plan, m, False), after,
                           name='scatter_' + grp + '_wait')
        core = lax.axis_index("c").astype(jnp.int32).reshape(1)
        sums = [_sum4(p, l, core, name='sum_chips') for p, l in zip(bufs[:m], bufs[m:])]
        both = _exchange_halves(sums, name='gather_core_halves_' + grp)
        return {nm: g.reshape(-1, g.shape[-1]) for nm, g in zip(names, both)}


def _from_shards(name, g):
    _, r, n = g.shape
    if name == 'ffn_in':
        assert n == FFN_BK
        v = g.reshape(4, 2, r // 2, n)
        return jnp.concatenate([v[b] for b in _ffn_order(4 * n)], axis=-1)
    if name == 'ffn_out':
        return g.reshape(4, 2, r // 2, n).transpose(1, 0, 2, 3).reshape(2, 2 * r, n)
    if name in ('even_in', 'odd_in'):
        return jnp.concatenate([g[b] for b in range(4)], axis=-1)
    return g.reshape(4 * r, n)


def kernel(x, c, ctx, c_ctx, mod_w, mod_b, norm_g, ffn_w_in, ffn_w_out, even_w_in, even_w_out, attn_qk_norm_g, attn_sink, hgrn_out_norm_g, hgrn_lb, odd_w_in, odd_w_out, loss_target, m_c_ctx, m_mod_w, m_mod_b, m_norm_g, m_ffn_w_in, m_ffn_w_out, m_even_w_in, m_even_w_out, m_attn_qk_norm_g, m_attn_sink, m_hgrn_out_norm_g, m_hgrn_lb, m_odd_w_in, m_odd_w_out, v_c_ctx, v_mod_w, v_mod_b, v_norm_g, v_ffn_w_in, v_ffn_w_out, v_even_w_in, v_even_w_out, v_attn_qk_norm_g, v_attn_sink, v_hgrn_out_norm_g, v_hgrn_lb, v_odd_w_in, v_odd_w_out):
    d = x.shape[-1]
    lc = ctx.shape[1]
    assert lc == TM and d == 1024
    ax, ay, ac = _place()
    s = 2 * ax + ay
    me = 4 * ax + 2 * ay + ac
    nmod = mod_w.shape[2]

    def pad8(v):
        return jnp.pad(v, ((0, 8 - v.shape[0]), (0, 0)))

    pack = jnp.concatenate([pad8(c), pad8(norm_g.reshape(1, d))], axis=0)
    g1 = _ag8(pack, name='gather_cond')
    c_all = g1[:, 0, :]
    ng = g1[0::2, 8, :].reshape(4, 2, 2, d // 4).transpose(1, 2, 0, 3).reshape(4, d)

    cond_raw = jnp.concatenate([c_all, pad8(c_ctx.reshape(1, d))], axis=0)
    mb_sh = lax.dynamic_slice_in_dim(mod_b, s * nmod, nmod, axis=1).reshape(2, 1, nmod)
    mpart = _mod_fwd(cond_raw, mod_w, mb_sh, name='mod_fwd')
    g3 = _ag8(mpart.reshape(32, nmod), name='gather_mods')
    mods_full = g3[0::2].reshape(4, 2, 16, nmod).transpose(1, 2, 0, 3).reshape(2, 16, 4 * nmod)
    m_lat = lax.dynamic_index_in_dim(mods_full, me, axis=1, keepdims=False)
    mods = jnp.stack([mods_full[:, 8], m_lat], axis=1).reshape(24, d)

    names = ['ffn_in', 'ffn_out', 'even_in', 'even_out', 'odd_in', 'odd_out']
    shards = [_bf(v.reshape(-1, v.shape[-1])) for v in (ffn_w_in, ffn_w_out, even_w_in, even_w_out, odd_w_in, odd_w_out)]
    shards, mods = lax.optimization_barrier((shards, mods))
    reducer = _GradReducer()
    wsrc = _GatheredWeights(dict(zip(names, shards)), reducer)

    lb = _lb_fwd(hgrn_lb, name='hgrn_lower_bound')
    small = dict(gq=jnp.tile(attn_qk_norm_g[0, 0], 2).reshape(1, 128), gk=jnp.tile(attn_qk_norm_g[0, 1], 2).reshape(1, 128),
                 sink=attn_sink[0], gain=hgrn_out_norm_g, lb=lb)
    x0 = jnp.concatenate([ctx[0], x[0]], axis=0)
    mods = mods + wsrc.token[0, 0]
    loss_t, dx0, grads, sums = _local_step(x0, loss_target[0], mods, ng, wsrc, small)
    loss = lax.psum(loss_t[0, 0], ("x", "y", "c"))
    grad_x = dx0[None]

    def tile(v, at=0):
        return jnp.pad(v[0:1], ((at, 7 - at), (0, d - v.shape[1])))

    sums = dict(sums, sink=sums['sink'][:, 0].reshape(1, 8))
    singles = sum(tile(sums[nm], i) for i, nm in enumerate(PACK_SINGLES))
    g4 = _ag8(jnp.concatenate([sums[nm] for nm in PACK_TILES] + [singles], axis=0), name='gather_row_sums')
    small_g, glb, gmb, dmat = _small_finalize(g4, tile(lb)[0:1], name='small_grads')
    dms = lax.dynamic_slice_in_dim(dmat.transpose(0, 2, 1, 3).reshape(2, 16, 6 * d), s * nmod, nmod, axis=2)
    g_mod_w, dcond = _mod_bwd(cond_raw, dms, mod_w, name='mod_bwd')
    g5 = _ag8(dcond[8:16], name='gather_dcond')
    g_c_ctx = _cctx_grad(g5, c_ctx.reshape(8, d // 8).reshape(1, d), name='c_ctx_grad')

    late = {nm: grads[nm] for nm in ('even_in', 'even_out')}
    late, g_c_ctx = lax.optimization_barrier((late, g_c_ctx))
    token = reducer.start('late', late)
    full = reducer.finish('early', token)

    def upd(wv, gs, mv, vv, name):
        shp = wv.shape
        c2 = shp[-1]
        out = _adam(wv.reshape(-1, c2), [g.reshape(-1, c2) for g in gs], mv.reshape(-1, c2), vv.reshape(-1, c2), name=name)
        return [o.reshape(shp) for o in out]

    res = {}
    res['c_ctx'] = upd(c_ctx.reshape(8, d // 8), [g_c_ctx.reshape(8, d // 8)], m_c_ctx.reshape(8, d // 8), v_c_ctx.reshape(8, d // 8), 'adam_c_ctx')
    res['c_ctx'] = [o.reshape(d) for o in res['c_ctx']]
    res['mod_w'] = upd(mod_w, [g_mod_w], m_mod_w, v_mod_w, 'adam_mod_w')
    res['mod_b'] = upd(mod_b, [gmb.reshape(2, 6 * d)], m_mod_b, v_mod_b, 'adam_mod_b')
    g_ng = lax.dynamic_slice_in_dim(small_g[0:4].reshape(2, 2, d), s * (d // 4), d // 4, axis=2)
    res['norm_g'] = upd(norm_g, [g_ng], m_norm_g, v_norm_g, 'adam_norm_g')
    g_qk = jnp.stack([small_g[4, 0:64], small_g[5, 0:64]]).reshape(1, 2, 64)
    res['attn_qk_norm_g'] = upd(attn_qk_norm_g, [g_qk], m_attn_qk_norm_g, v_attn_qk_norm_g, 'adam_qk_gain')
    res['attn_sink'] = upd(attn_sink, [small_g[7, 0:8].reshape(1, 8)], m_attn_sink, v_attn_sink, 'adam_sink')
    res['hgrn_out_norm_g'] = upd(hgrn_out_norm_g, [small_g[6, 0:128].reshape(1, 128)], m_hgrn_out_norm_g, v_hgrn_out_norm_g, 'adam_head_gain')
    res['hgrn_lb'] = upd(hgrn_lb, [glb[0:2, 0:hgrn_lb.shape[1]]], m_hgrn_lb, v_hgrn_lb, 'adam_hgrn_lb')
    res['odd_w_in'] = upd(odd_w_in, [full['odd_in']], m_odd_w_in, v_odd_w_in, 'adam_odd_in')
    res['odd_w_out'] = upd(odd_w_out, [full['odd_out']], m_odd_w_out, v_odd_w_out, 'adam_odd_out')
    full.update(reducer.finish('mid', res['odd_w_in'][1]))
    g_ffn_in = jnp.concatenate([full['ffn_in0'], full['ffn_in1']], axis=0)
    g_ffn_out = jnp.concatenate([full['ffn_out0'], full['ffn_out1']], axis=0)
    res['ffn_w_in'] = upd(ffn_w_in, [g_ffn_in], m_ffn_w_in, v_ffn_w_in, 'adam_ffn_in')
    res['ffn_w_out'] = upd(ffn_w_out, [g_ffn_out], m_ffn_w_out, v_ffn_w_out, 'adam_ffn_out')
    full.update(reducer.finish('late', res['ffn_w_in'][1]))
    res['even_w_in'] = upd(even_w_in, [full['even_in']], m_even_w_in, v_even_w_in, 'adam_even_in')
    res['even_w_out'] = upd(even_w_out, [full['even_out']], m_even_w_out, v_even_w_out, 'adam_even_out')

    order = ['c_ctx', 'mod_w', 'mod_b', 'norm_g', 'ffn_w_in', 'ffn_w_out', 'even_w_in', 'even_w_out',
             'attn_qk_norm_g', 'attn_sink', 'hgrn_out_norm_g', 'hgrn_lb', 'odd_w_in', 'odd_w_out']
    outs = [loss, grad_x]
    for k in range(4):
        outs += [res[nm][k] for nm in order]
    return tuple(outs)
```

```python
import functools
import math

import numpy as np
import jax
import jax.numpy as jnp
from jax import lax
from jax.experimental import pallas as pl
from jax.experimental.pallas import tpu as pltpu

F32 = jnp.float32
BF16 = jnp.bfloat16
EPS = 1e-6
TM = 256
CHUNK = 64
QB = 256
WINDOW = 128
NEG = -1e30
MESH = pl.DeviceIdType.MESH

ADAM_LR, ADAM_B1, ADAM_B2, ADAM_EPS, ADAM_WD, ADAM_STEP = 0.001, 0.9, 0.999, 1e-08, 0.01, 10


def _pcall(body, **kw):
    return pl.pallas_call(body, **kw)


def _pick(n, cap):
    best = None
    for m in range(128, min(n, cap) + 1, 128):
        if n % m == 0:
            best = m
    assert best is not None, (n, cap)
    return best


def _bf(x):
    return x.astype(BF16)


def _dot(a, b):
    return jnp.dot(_bf(a), _bf(b), preferred_element_type=F32)


def _dot_nt(a, b):
    return lax.dot_general(_bf(a), _bf(b), (((1,), (1,)), ((), ())), preferred_element_type=F32)


def _dot_tn(a, b):
    return lax.dot_general(_bf(a), _bf(b), (((0,), (0,)), ((), ())), preferred_element_type=F32)


def _dot_exact(a, b):
    return jnp.dot(a, b, preferred_element_type=F32, precision=lax.Precision.HIGHEST)


def _sigmoid(x):
    return 1.0 / (1.0 + jnp.exp(-x))


def _iota(shape, dim):
    return lax.broadcasted_iota(jnp.int32, shape, dim)


def _parts(a):
    parts = list(a) if isinstance(a, (list, tuple)) else [a]
    widths = [p.shape[1] for p in parts]
    return parts, widths, [sum(widths[:i]) for i in range(len(parts))]


def _mm_nn(a, b, *, lead=None, out_dtype=F32, name):
    parts, widths, offs = _parts(a)
    m, k = parts[0].shape[0], sum(widths)
    n = b.shape[-1]
    bm = 1408 if (m % 1408 == 0 and k <= 1024) else (768 if m % 768 == 0 else TM)
    bn = _pick(n, 1024) if n % 512 == 0 else _pick(n, 1664)

    def body(*refs):
        b_ref, o_ref = refs[-2], refs[-1]
        acc = None
        for p_ref, w, off in zip(refs, widths, offs):
            term = _dot(p_ref[...], b_ref[off:off + w, :])
            acc = term if acc is None else acc + term
        o_ref[...] = acc.astype(o_ref.dtype)

    if lead is None:
        b_spec = pl.BlockSpec((k, bn), lambda i, j: (0, j))
    else:
        b_spec = pl.BlockSpec((None, k, bn), lambda i, j: (lead, 0, j))
    return _pcall(
        body, name=name, grid=(m // bm, n // bn),
        in_specs=[pl.BlockSpec((bm, w), lambda i, j: (i, 0)) for w in widths] + [b_spec],
        out_specs=pl.BlockSpec((bm, bn), lambda i, j: (i, j)),
        out_shape=jax.ShapeDtypeStruct((m, n), out_dtype),
    )(*parts, b)


def _mm_nt(a, b, *, lead=None, name):
    parts, widths, offs = _parts(a)
    m, n = parts[0].shape[0], sum(widths)
    k = b.shape[-2]
    bm = 1408 if (m % 1408 == 0 and n <= 1024) else (768 if m % 768 == 0 else TM)
    bk = _pick(k, 1024 if n <= 2048 else 512)

    def body(*refs):
        b_ref, o_ref = refs[-2], refs[-1]
        acc = None
        for p_ref, w, off in zip(refs, widths, offs):
            term = _dot_nt(p_ref[...], b_ref[:, off:off + w])
            acc = term if acc is None else acc + term
        o_ref[...] = acc

    if lead is None:
        b_spec = pl.BlockSpec((bk, n), lambda i, j: (j, 0))
    else:
        b_spec = pl.BlockSpec((None, bk, n), lambda i, j: (lead, j, 0))
    return _pcall(
        body, name=name, grid=(m // bm, k // bk),
        in_specs=[pl.BlockSpec((bm, w), lambda i, j: (i, 0)) for w in widths] + [b_spec],
        out_specs=pl.BlockSpec((bm, bk), lambda i, j: (i, j)),
        out_shape=jax.ShapeDtypeStruct((m, k), F32),
    )(*parts, b)


def _mm_tn(a, b, *, name):
    a_parts, a_w, a_off = _parts(a)
    b_parts, b_w, b_off = _parts(b)
    t, k, n = a_parts[0].shape[0], sum(a_w), sum(b_w)
    bt = 1408 if t % 1408 == 0 else (768 if t % 768 == 0 else TM)
    bk = _pick(k, 1536) if len(a_parts) == 1 else math.gcd(*a_w)
    if len(b_parts) == 1:
        bn = _pick(n, 1024) if n % 1024 == 0 or n < 1664 else _pick(n, 1664)
    else:
        bn = math.gcd(*b_w)
    na, nbp = len(a_parts), len(b_parts)

    def block_range(off, w, blk):
        return off // blk, w // blk

    def body(*refs):
        a_refs, b_refs, o_ref = refs[:na], refs[na:na + nbp], refs[-1]
        i, j = pl.program_id(0), pl.program_id(1)

        @pl.when(pl.program_id(2) == 0)
        def _():
            o_ref[...] = jnp.zeros_like(o_ref)

        def add(a_ref, b_ref):
            o_ref[...] += _dot_tn(a_ref[...], b_ref[...])

        for pa in range(na):
            sa, ca = block_range(a_off[pa], a_w[pa], bk)
            for pb in range(nbp):
                sb, cb = block_range(b_off[pb], b_w[pb], bn)
                if na == 1 and nbp == 1:
                    add(a_refs[0], b_refs[0])
                else:
                    pl.when((i >= sa) & (i < sa + ca) & (j >= sb) & (j < sb + cb))(
                        functools.partial(add, a_refs[pa], b_refs[pb]))

    def spec(off, w, blk, axis):
        s0, cnt = block_range(off, w, blk)

        def index(i, j, s):
            g = i if axis == 0 else j
            inside = (g >= s0) & (g < s0 + cnt)
            return (jnp.where(inside, s, 0), jnp.clip(g - s0, 0, cnt - 1))

        return pl.BlockSpec((bt, blk), index)

    return _pcall(
        body, name=name, grid=(k // bk, n // bn, t // bt),
        in_specs=[spec(o, w, bk, 0) for o, w in zip(a_off, a_w)] + [spec(o, w, bn, 1) for o, w in zip(b_off, b_w)],
        out_specs=pl.BlockSpec((bk, bn), lambda i, j, s: (i, j)),
        out_shape=jax.ShapeDtypeStruct((k, n), F32),
    )(*a_parts, *b_parts)


def _mod_row(mods_ref, lat, idx):
    return jnp.where(lat, mods_ref[idx + 6:idx + 7, :], mods_ref[idx:idx + 1, :])


def _row_step(t):
    return 768 if t % 768 == 0 else TM


def _row_fwd(x, mods, *, y=None, gate=None, g=None, shift=None, scale=None, name):
    t, d = x.shape
    has_y, has_n = y is not None, g is not None
    rt = _row_step(t)

    def body(*refs):
        refs = list(refs)
        x_ref, mods_ref = refs[0], refs[1]
        pos = 2
        if has_y:
            y_ref = refs[pos]; pos += 1
        if has_n:
            g_ref = refs[pos]; pos += 1
        outs = refs[pos:]
        for sub in range(rt // TM):
            rows = slice(sub * TM, (sub + 1) * TM)
            lat = pl.program_id(0) * (rt // TM) + sub > 0
            x1 = x_ref[rows, :]
            o = 0
            if has_y:
                x1 = x1 + _mod_row(mods_ref, lat, gate) * y_ref[rows, :]
                outs[o][rows, :] = x1; o += 1
            if has_n:
                rs = lax.rsqrt(jnp.mean(x1 * x1, axis=-1, keepdims=True) + EPS)
                hn = x1 * rs * g_ref[...]
                h = hn * (1.0 + _mod_row(mods_ref, lat, scale)) + _mod_row(mods_ref, lat, shift)
                outs[o][rows, :] = h.astype(BF16)

    row = pl.BlockSpec((rt, d), lambda i: (i, 0))
    ins, specs = [x, mods], [row, pl.BlockSpec(mods.shape, lambda i: (0, 0))]
    if has_y:
        ins.append(y); specs.append(row)
    if has_n:
        ins.append(g.reshape(1, d)); specs.append(pl.BlockSpec((1, d), lambda i: (0, 0)))
    out_shape, out_specs = [], []
    if has_y:
        out_shape.append(jax.ShapeDtypeStruct((t, d), F32)); out_specs.append(row)
    if has_n:
        out_shape.append(jax.ShapeDtypeStruct((t, d), BF16)); out_specs.append(row)
    res = _pcall(body, name=name, grid=(t // rt,), in_specs=specs, out_specs=out_specs,
                 out_shape=out_shape)(*ins)
    return res


def _acc_row(ref, r, val):
    ref[r:r + 1, :] += val


def _row_final(x, z, mods, target, *, gate, name):
    t, d = x.shape
    rt = _row_step(t)
    nsub = rt // TM

    def body(*refs):
        x_ref, mods_ref, z_ref = refs[:3]
        t_refs = refs[3:3 + nsub]
        loss_ref, dx_ref, dz_ref, sums_ref = refs[3 + nsub:]
        i = pl.program_id(0)

        @pl.when(i == 0)
        def _():
            loss_ref[...] = jnp.zeros_like(loss_ref)
            sums_ref[...] = jnp.zeros_like(sums_ref)

        for sub in range(nsub):
            rows = slice(sub * TM, (sub + 1) * TM)
            lat = i * nsub + sub > 0
            gt = _mod_row(mods_ref, lat, gate)
            zz = z_ref[rows, :]
            yv = x_ref[rows, :] + gt * zz
            keep = jnp.where(lat, 1.0, 0.0).astype(F32)
            diff = (yv - t_refs[sub][...]) * keep
            part = jnp.sum(jnp.sum(diff * diff, axis=0, keepdims=True), axis=1, keepdims=True)
            loss_ref[...] += part * (0.5 / d)
            dy = diff * (1.0 / d)
            dx_ref[rows, :] = dy
            dz_ref[rows, :] = (gt * dy).astype(BF16)
            _acc_row(sums_ref, 6, jnp.sum(dy * zz, axis=0, keepdims=True))

    row = pl.BlockSpec((rt, d), lambda i: (i, 0))
    tgt = [pl.BlockSpec((TM, d), lambda i, sub=sub: (jnp.maximum(i * nsub + sub - 1, 0), 0)) for sub in range(nsub)]
    return _pcall(
        body, name=name, grid=(t // rt,),
        in_specs=[row, pl.BlockSpec(mods.shape, lambda i: (0, 0)), row] + tgt,
        out_specs=[pl.BlockSpec((8, 128), lambda i: (0, 0)), row, row,
                   pl.BlockSpec((8, d), lambda i: (0, 0))],
        out_shape=[jax.ShapeDtypeStruct((8, 128), F32), jax.ShapeDtypeStruct((t, d), F32),
                   jax.ShapeDtypeStruct((t, d), BF16), jax.ShapeDtypeStruct((8, d), F32)],
    )(x, mods, z, *([target] * nsub))


def _row_bwd(xn, dxo, dh, mods, g, *, shift, scale, y=None, gate=None, latent_only=False, name):
    t, d = xn.shape
    has_y = y is not None

    def body(*refs):
        refs = list(refs)
        x_ref, dxo_ref, dh_ref, mods_ref, g_ref = refs[:5]
        pos = 5
        if has_y:
            y_ref = refs[pos]; pos += 1
        dx_ref = refs[pos]; pos += 1
        if has_y:
            dy_ref = refs[pos]; pos += 1
        sums_ref = refs[pos]
        i = pl.program_id(0)

        @pl.when(i == 0)
        def _():
            sums_ref[...] = jnp.zeros_like(sums_ref)

        def add_sums(vals, base):
            for r, v in enumerate(vals):
                if v is not None:
                    _acc_row(sums_ref, base + r, v)

        gv = g_ref[...]
        for sub in range(rt // TM):
            rows = slice(sub * TM, (sub + 1) * TM)
            lat = i * (rt // TM) + sub > 0
            x1 = x_ref[rows, :]
            rs = lax.rsqrt(jnp.mean(x1 * x1, axis=-1, keepdims=True) + EPS)
            xh = x1 * rs
            dhv = dh_ref[rows, :]
            dn = dhv * (1.0 + _mod_row(mods_ref, lat, scale))
            dxh = dn * gv
            dx = dxo_ref[rows, :] + rs * (dxh - xh * jnp.mean(dxh * xh, axis=-1, keepdims=True))
            dx_ref[rows, :] = dx
            vals = [jnp.sum(dhv, axis=0, keepdims=True),
                    jnp.sum(dhv * (xh * gv), axis=0, keepdims=True),
                    None,
                    jnp.sum(dn * xh, axis=0, keepdims=True)]
            if has_y:
                dy_ref[rows, :] = (_mod_row(mods_ref, lat, gate) * dx).astype(BF16)
                vals[2] = jnp.sum(dx * y_ref[rows, :], axis=0, keepdims=True)
            if sub == 0:
                pl.when(i == 0)(functools.partial(add_sums, vals, 0))
                pl.when(i > 0)(functools.partial(add_sums, vals, 4))
            else:
                add_sums(vals, 4)

    rt = TM if latent_only else _row_step(t)
    row = pl.BlockSpec((rt, d), lambda i: (i, 0))
    ins = [xn, dxo, dh, mods, g.reshape(1, d)]
    specs = [row, row, row, pl.BlockSpec(mods.shape, lambda i: (0, 0)), pl.BlockSpec((1, d), lambda i: (0, 0))]
    if latent_only:
        out_shape = [jax.ShapeDtypeStruct((t - TM, d), F32)]
        out_specs = [pl.BlockSpec((TM, d), lambda i: (jnp.maximum(i - 1, 0), 0))]
    else:
        out_shape, out_specs = [jax.ShapeDtypeStruct((t, d), F32)], [row]
    if has_y:
        ins.append(y); specs.append(row)
        out_shape.append(jax.ShapeDtypeStruct((t, d), BF16)); out_specs.append(row)
    out_shape.append(jax.ShapeDtypeStruct((8, d), F32))
    out_specs.append(pl.BlockSpec((8, d), lambda i: (0, 0)))
    return _pcall(body, name=name, grid=(t // rt,), in_specs=specs, out_specs=out_specs,
                  out_shape=out_shape)(*ins)


FFN_BK = 1408


FFN_SUB = 256


def _ffn_order(n2):
    nb = n2 // (2 * FFN_BK)
    return [h * nb + j for j in range(nb) for h in (0, 1)]


def _ffn_interleave(w):
    return jnp.concatenate([w[..., b * FFN_BK:(b + 1) * FFN_BK] for b in _ffn_order(w.shape[-1])], axis=-1)


def _ffn_deinterleave(w):
    order = _ffn_order(w.shape[-1])
    return jnp.concatenate([w[..., order.index(b) * FFN_BK:(order.index(b) + 1) * FFN_BK]
                            for b in range(len(order))], axis=-1)


def _big_tile(t):
    return 768 if t % 768 == 0 else TM


def _ffn_in(h, w, *, lead, name):
    t, d = h.shape
    n2 = w.shape[-1]
    bm, bk = _big_tile(t), FFN_BK

    def body(h_ref, w_ref, u_ref, a_ref):
        hb = h_ref[...]
        for c0 in range(0, bk, FFN_SUB):
            c1 = min(c0 + FFN_SUB, bk)
            ug = _dot(hb, w_ref[:, c0:c1]).astype(BF16)
            uu = _dot(hb, w_ref[:, bk + c0:bk + c1]).astype(BF16)
            u_ref[:, c0:c1] = ug
            u_ref[:, bk + c0:bk + c1] = uu
            gv, up = ug.astype(F32), uu.astype(F32)
            a_ref[:, c0:c1] = (gv * _sigmoid(gv) * up).astype(BF16)

    return _pcall(
        body, name=name, grid=(t // bm, n2 // (2 * bk)),
        in_specs=[pl.BlockSpec((bm, d), lambda i, j: (i, 0)),
                  pl.BlockSpec((None, d, 2 * bk), lambda i, j: (lead, 0, j))],
        out_specs=[pl.BlockSpec((bm, 2 * bk), lambda i, j: (i, j)), pl.BlockSpec((bm, bk), lambda i, j: (i, j))],
        out_shape=[jax.ShapeDtypeStruct((t, n2), BF16), jax.ShapeDtypeStruct((t, n2 // 2), BF16)],
    )(h, w)


def _ffn_dx(dz, w_out, u, *, lead, name):
    t, d = dz.shape
    n2 = u.shape[1]
    bm, bk = _big_tile(t), FFN_BK

    def body(dz_ref, w_ref, u_ref, du_ref):
        dzb = dz_ref[...]
        for c0 in range(0, bk, FFN_SUB):
            c1 = min(c0 + FFN_SUB, bk)
            da = _dot_nt(dzb, w_ref[c0:c1, :])
            gv, up = u_ref[:, c0:c1].astype(F32), u_ref[:, bk + c0:bk + c1].astype(F32)
            s = _sigmoid(gv)
            du_ref[:, c0:c1] = (da * up * (s * (1.0 + gv * (1.0 - s)))).astype(BF16)
            du_ref[:, bk + c0:bk + c1] = (da * gv * s).astype(BF16)

    ublk = pl.BlockSpec((bm, 2 * bk), lambda i, j: (i, j))
    return _pcall(
        body, name=name, grid=(t // bm, n2 // (2 * bk)),
        in_specs=[pl.BlockSpec((bm, d), lambda i, j: (i, 0)),
                  pl.BlockSpec((None, bk, d), lambda i, j: (lead, j, 0)), ublk],
        out_specs=ublk, out_shape=jax.ShapeDtypeStruct((t, n2), BF16),
    )(dz, w_out, u)


def _lane(shape):
    return _iota(shape, len(shape) - 1)


def _pair_norm(x, g):
    lo = _lane(x.shape) < 64
    x2 = x * x
    s_lo = jnp.sum(jnp.where(lo, x2, 0.0), axis=-1, keepdims=True)
    s_hi = jnp.sum(jnp.where(lo, 0.0, x2), axis=-1, keepdims=True)
    rs = lax.rsqrt(jnp.where(lo, s_lo, s_hi) * (1.0 / 64) + EPS)
    return x * rs, rs


def _pair_mean(v):
    lo = _lane(v.shape) < 64
    s_lo = jnp.sum(jnp.where(lo, v, 0.0), axis=-1, keepdims=True)
    s_hi = jnp.sum(jnp.where(lo, 0.0, v), axis=-1, keepdims=True)
    return jnp.where(lo, s_lo, s_hi) * (1.0 / 64)


def _rot64(x):
    r1 = pltpu.roll(x, 32, 1)
    r2 = pltpu.roll(x, 96, 1)
    even = ((_lane(x.shape) >> 5) & 1) == 0
    return jnp.where(even, -r2, r1)


def _rope64(x, cos, sin):
    return x * cos + _rot64(x) * sin


def _rope64_t(d, cos, sin):
    return d * cos - _rot64(d * sin)


def _kprep_fwd(p, gk, cos, sin, *, name):
    t = p.shape[0]

    def body(k_ref, g_ref, c_ref, s_ref, o_ref):
        xh, _ = _pair_norm(k_ref[...], None)
        o_ref[...] = _rope64(xh * g_ref[...], c_ref[...], s_ref[...])

    blk = pl.BlockSpec((TM, 128), lambda i: (i, 0))
    return _pcall(
        body, name=name, grid=(t // TM,),
        in_specs=[pl.BlockSpec((TM, 128), lambda i: (i, 4)), pl.BlockSpec((1, 128), lambda i: (0, 0)), blk, blk],
        out_specs=blk, out_shape=jax.ShapeDtypeStruct((t, 128), F32),
    )(p, gk, cos, sin)


def _kprep_bwd(p, gk, cos, sin, dkp, dv, *, name):
    t = p.shape[0]

    def body(k_ref, g_ref, c_ref, s_ref, dkp_ref, dv_ref, o_ref, dg_ref):
        @pl.when(pl.program_id(0) == 0)
        def _():
            dg_ref[...] = jnp.zeros_like(dg_ref)
        xh, rs = _pair_norm(k_ref[...], None)
        dn = _rope64_t(dkp_ref[...], c_ref[...], s_ref[...])
        _acc_row(dg_ref, 0, jnp.sum(dn * xh, axis=0, keepdims=True))
        dxh = dn * g_ref[...]
        o_ref[:, 0:128] = (rs * (dxh - xh * _pair_mean(dxh * xh))).astype(BF16)
        o_ref[:, 128:256] = dv_ref[...].astype(BF16)

    blk = pl.BlockSpec((TM, 128), lambda i: (i, 0))
    return _pcall(
        body, name=name, grid=(t // TM,),
        in_specs=[pl.BlockSpec((TM, 128), lambda i: (i, 4)), pl.BlockSpec((1, 128), lambda i: (0, 0)), blk, blk, blk, blk],
        out_specs=[pl.BlockSpec((TM, 256), lambda i: (i, 0)), pl.BlockSpec((8, 128), lambda i: (0, 0))],
        out_shape=[jax.ShapeDtypeStruct((t, 256), BF16), jax.ShapeDtypeStruct((8, 128), F32)],
    )(p, gk, cos, sin, dkp, dv)


def _attn_common(i, t, lc, kp_ref, v_ref):
    span = QB + 2 * WINDOW
    start = pl.multiple_of(jnp.clip(i * QB - WINDOW, lc, t - span), WINDOW)
    kall = jnp.concatenate([kp_ref[0:lc, :], kp_ref[pl.ds(start, span), :]], axis=0)
    vall = jnp.concatenate([v_ref[0:lc, :], v_ref[pl.ds(start, span), :]], axis=0)
    nk = lc + span
    col = _iota((QB, nk), 1)
    krow = jnp.where(col < lc, col, start + col - lc)
    qrow = i * QB + _iota((QB, nk), 0)
    valid = (col < lc) | ((qrow >= lc) & (krow >= lc) & (jnp.abs(krow - qrow) <= WINDOW))
    lo = _lane(kall.shape) < 64
    kroll, vroll = pltpu.roll(kall, 64, 1), pltpu.roll(vall, 64, 1)
    zero = jnp.zeros_like(kall)
    kvar = [[_bf(jnp.where(lo, kall, zero)), _bf(jnp.where(lo, zero, kroll))],
            [_bf(jnp.where(lo, kroll, zero)), _bf(jnp.where(lo, zero, kall))]]
    vvar = [[_bf(jnp.where(lo, vall, zero)), _bf(jnp.where(lo, zero, vroll))],
            [_bf(jnp.where(lo, vroll, zero)), _bf(jnp.where(lo, zero, vall))]]
    return start, valid, kvar, vvar


def _softmax_sink(s, valid, snk):
    s = jnp.where(valid, s, NEG)
    m = jnp.maximum(jnp.max(s, axis=-1, keepdims=True), snk)
    e = jnp.exp(s - m)
    es = jnp.exp(snk - m)
    inv = 1.0 / (jnp.sum(e, axis=-1, keepdims=True) + es)
    return e * inv, es * inv


def _attn_fwd(p, kp, gq, sink, cos, sin, *, lc, name):
    t = p.shape[0]
    scale = 64 ** -0.5

    def body(q_ref, kp_ref, v_ref, g_ref, sink_ref, c_ref, s_ref, o_ref):
        i = pl.program_id(0)
        _, valid, kvar, vvar = _attn_common(i, t, lc, kp_ref, v_ref)
        cosv, sinv, gv = c_ref[...], s_ref[...], g_ref[...]
        for j in range(4):
            xh, _ = _pair_norm(q_ref[:, 128 * j:128 * j + 128], None)
            q2 = _bf(_rope64(xh * gv, cosv, sinv) * scale)
            acc = jnp.zeros((QB, 128), F32)
            for half in range(2):
                s = _dot_nt(q2, kvar[j // 2][half])
                pr, _ = _softmax_sink(s, valid, sink_ref[2 * j + half])
                acc = acc + _dot(pr, vvar[j // 2][half])
            o_ref[:, 128 * j:128 * j + 128] = acc.astype(BF16)

    qblk = pl.BlockSpec((QB, 128), lambda i: (i, 0))
    return _pcall(
        body, name=name, grid=(t // QB,),
        in_specs=[pl.BlockSpec((QB, 512), lambda i: (i, 0)),
                  pl.BlockSpec((t, 128), lambda i: (0, 0)),
                  pl.BlockSpec((t, 128), lambda i: (0, 5)),
                  pl.BlockSpec((1, 128), lambda i: (0, 0)),
                  pl.BlockSpec(memory_space=pltpu.SMEM), qblk, qblk],
        out_specs=pl.BlockSpec((QB, 512), lambda i: (i, 0)),
        out_shape=jax.ShapeDtypeStruct((t, 512), BF16),
    )(p, kp, p, gq, sink, cos, sin)


def _attn_bwd(p, kp, gq, sink, cos, sin, dmix, *, lc, name):
    t = p.shape[0]
    scale = 64 ** -0.5
    span = QB + 2 * WINDOW

    def body(q_ref, kp_ref, v_ref, g_ref, sink_ref, c_ref, s_ref, do_ref,
             dq_ref, dk_ref, dv_ref, dg_ref, dsink_ref):
        i = pl.program_id(0)

        @pl.when(i == 0)
        def _():
            dk_ref[...] = jnp.zeros_like(dk_ref)
            dv_ref[...] = jnp.zeros_like(dv_ref)
            dg_ref[...] = jnp.zeros_like(dg_ref)
            dsink_ref[...] = jnp.zeros_like(dsink_ref)

        start, valid, kvar, vvar = _attn_common(i, t, lc, kp_ref, v_ref)
        cosv, sinv, gv = c_ref[...], s_ref[...], g_ref[...]
        nk = lc + span
        dkt = [jnp.zeros((64, nk), F32), jnp.zeros((64, nk), F32)]
        dvt = [jnp.zeros((64, nk), F32), jnp.zeros((64, nk), F32)]
        for j in range(4):
            kvh = j // 2
            xh, rs = _pair_norm(q_ref[:, 128 * j:128 * j + 128], None)
            q2 = _bf(_rope64(xh * gv, cosv, sinv) * scale)
            do2 = _bf(do_ref[:, 128 * j:128 * j + 128])
            dq2 = jnp.zeros((QB, 128), F32)
            for half in range(2):
                s = _dot_nt(q2, kvar[kvh][half])
                pr, ps = _softmax_sink(s, valid, sink_ref[2 * j + half])
                dp = _dot_nt(do2, vvar[kvh][half])
                delta = jnp.sum(pr * dp, axis=-1, keepdims=True)
                ds = pr * (dp - delta)
                dsk = jnp.sum(jnp.sum(-ps * delta, axis=0, keepdims=True), axis=1, keepdims=True)
                _acc_row(dsink_ref, 2 * j + half, jnp.broadcast_to(dsk, (1, 128)))
                dq2 = dq2 + _dot(ds, kvar[kvh][half])
                hrows = slice(64 * half, 64 * half + 64)
                dkt[kvh] = dkt[kvh] + _dot_tn(q2, ds)[hrows]
                dvt[kvh] = dvt[kvh] + _dot_tn(do2, pr)[hrows]
            dn = _rope64_t(dq2 * scale, cosv, sinv)
            _acc_row(dg_ref, 0, jnp.sum(dn * xh, axis=0, keepdims=True))
            dxh = dn * gv
            dq_ref[:, 128 * j:128 * j + 128] = (rs * (dxh - xh * _pair_mean(dxh * xh))).astype(BF16)
        dk_all = jnp.concatenate(dkt, axis=0).T
        dv_all = jnp.concatenate(dvt, axis=0).T
        dk_ref[0:lc, :] += dk_all[0:lc]
        dv_ref[0:lc, :] += dv_all[0:lc]
        dk_ref[pl.ds(start, span), :] += dk_all[lc:nk]
        dv_ref[pl.ds(start, span), :] += dv_all[lc:nk]

    qblk = pl.BlockSpec((QB, 128), lambda i: (i, 0))
    full = pl.BlockSpec((t, 128), lambda i: (0, 0))
    small = pl.BlockSpec((8, 128), lambda i: (0, 0))
    return _pcall(
        body, name=name, grid=(t // QB,),
        in_specs=[pl.BlockSpec((QB, 512), lambda i: (i, 0)), full,
                  pl.BlockSpec((t, 128), lambda i: (0, 5)),
                  pl.BlockSpec((1, 128), lambda i: (0, 0)),
                  pl.BlockSpec(memory_space=pltpu.SMEM), qblk, qblk,
                  pl.BlockSpec((QB, 512), lambda i: (i, 0))],
        out_specs=[pl.BlockSpec((QB, 512), lambda i: (i, 0)), full, full, small, small],
        out_shape=[jax.ShapeDtypeStruct((t, 512), BF16), jax.ShapeDtypeStruct((t, 128), F32),
                   jax.ShapeDtypeStruct((t, 128), F32), jax.ShapeDtypeStruct((8, 128), F32),
                   jax.ShapeDtypeStruct((8, 128), F32)],
    )(p, kp, p, gq, sink, cos, sin, dmix)


def _tri(rev):
    r, c = _iota((CHUNK, CHUNK), 0), _iota((CHUNK, CHUNK), 1)
    return (c >= r) if rev else (c <= r)


def _blk_map(nb, rev, backward):
    if not rev:
        return (lambda n: nb - 1 - n) if backward else (lambda n: n)
    if backward:
        return lambda n: jnp.where(n < nb - 1, n + 1, 0)
    return lambda n: jnp.where(n == 0, 0, nb - n)


def _chunk_order(rev, backward, nc=TM // CHUNK):
    order = list(range(nc))
    return order[::-1] if (rev != backward) else order


def _hgrn_gates(qraw, fraw, lb):
    sq = _sigmoid(qraw)
    sf = _sigmoid(fraw)
    f = lb + (1.0 - lb) * sf
    return qraw * sq, 1.0 - f, jnp.log(f), sq, sf, f


HGRN_HP = 4


def _chunk_cumsum(x, rev):
    n = x.shape[0]
    pos = _iota(x.shape, 0) & (CHUNK - 1)
    s = 1
    while s < CHUNK:
        if rev:
            x = x + jnp.where(pos < CHUNK - s, pltpu.roll(x, n - s, 0), 0.0)
        else:
            x = x + jnp.where(pos >= s, pltpu.roll(x, s, 0), 0.0)
        s *= 2
    return x


def _block_terms(lf, rev):
    b = _chunk_cumsum(lf, rev)
    mid, last = (CHUNK // 2 - 1, 0) if rev else (CHUNK // 2, CHUNK - 1)

    def chunk_row(off):
        return jnp.concatenate([jnp.broadcast_to(b[c * CHUNK + off:c * CHUNK + off + 1, :], (CHUNK, b.shape[1]))
                                for c in range(TM // CHUNK)], axis=0)

    r, bl = chunk_row(mid), chunk_row(last)
    return _tri(rev), jnp.exp(b - r), jnp.exp(r - b), jnp.exp(b), jnp.exp(bl - b), jnp.exp(bl)


def _headnorm_apply(o, gv, gain):
    n = o * lax.rsqrt(jnp.mean(o * o, axis=-1, keepdims=True) + EPS)
    if gain is not None:
        n = n * gain
    return (n * (gv * _sigmoid(gv))).astype(BF16)


def _headnorm_grad(o, gv, dy, gain):
    rs = lax.rsqrt(jnp.mean(o * o, axis=-1, keepdims=True) + EPS)
    xh = o * rs
    n = xh * gain if gain is not None else xh
    sg = _sigmoid(gv)
    dn = dy * (gv * sg)
    dg = (dy * n * (sg * (1.0 + gv * (1.0 - sg)))).astype(BF16)
    dgain = jnp.sum(dn * xh, axis=0, keepdims=True)
    dxh = dn * gain if gain is not None else dn
    return rs * (dxh - xh * jnp.mean(dxh * xh, axis=-1, keepdims=True)), dg, dgain


def _hgrn_cols(bmap, n2, c0):
    return [pl.BlockSpec((TM, 256), lambda h, n, b=b: (bmap(n), c0 // 2 + h * n2 + b)) for b in range(n2)]


def _head_cols(refs, hh):
    return refs[hh // 2][:, 128 * (hh % 2):128 * (hh % 2) + 128]


def _hgrn_fwd(p, lb, *, rev, name, ofw=None, gain=None):
    t = p.shape[0]
    nb, nc = t // TM, TM // CHUNK
    bmap = _blk_map(nb, rev, False)
    fcol = 14 if rev else 10
    fused = ofw is not None

    n2 = HGRN_HP // 2

    def body(*refs):
        q_refs, f_refs, v_refs, lb_ref = refs[:n2], refs[n2:2 * n2], refs[2 * n2:3 * n2], refs[3 * n2]
        rest = refs[3 * n2 + 1:]
        if fused:
            ofw_ref, g_refs, gain_ref = rest[0], rest[1:1 + n2], rest[1 + n2]
            o_ref, sh_ref, mix_ref, st = rest[2 + n2:]
        else:
            o_ref, sh_ref, st = rest

        @pl.when(pl.program_id(1) == 0)
        def _():
            st[...] = jnp.zeros_like(st)
        for hh in range(HGRN_HP):
            ln = slice(128 * hh, 128 * hh + 128)
            q, k, lf, _, _, _ = _hgrn_gates(_head_cols(q_refs, hh), _head_cols(f_refs, hh), lb_ref[:, ln])
            tri, eq, ek, ei, eki, eb = _block_terms(lf, rev)
            qe, ke, qi, ki, vb = _bf(q * eq), _bf(k * ek), _bf(q * ei), _bf(k * eki), _bf(_head_cols(v_refs, hh))
            intra = []
            for cc in range(nc):
                rows = slice(cc * CHUNK, (cc + 1) * CHUNK)
                a = jnp.where(tri, _dot_nt(qe[rows], ke[rows]), 0.0)
                intra.append(_dot(a, vb[rows]))
            s = st[hh]
            for cc in _chunk_order(rev, False):
                rows = slice(cc * CHUNK, (cc + 1) * CHUNK)
                sh_ref[hh, cc] = s.astype(sh_ref.dtype)
                o_ref[rows, ln] = intra[cc] + _dot_nt(qi[rows], s)
                s = s * eb[cc * CHUNK:cc * CHUNK + 1, :] + _dot_tn(vb[rows], ki[rows])
            st[hh] = s
            if fused:
                osum = o_ref[:, ln] + ofw_ref[:, ln]
                o_ref[:, ln] = osum
                mix_ref[:, ln] = _headnorm_apply(osum, _head_cols(g_refs, hh), gain_ref[...])

    hp, wd = HGRN_HP, 128 * HGRN_HP
    col = functools.partial(_hgrn_cols, bmap, n2)
    oblk = pl.BlockSpec((TM, wd), lambda h, n: (bmap(n), h))
    ins = [p] * (3 * n2) + [lb]
    specs = col(6) + col(fcol) + col(18) + [pl.BlockSpec((1, wd), lambda h, n: (0, h))]
    out_specs = [oblk, pl.BlockSpec((hp, nc, 128, 128), lambda h, n: (h, bmap(n), 0, 0))]
    out_shape = [jax.ShapeDtypeStruct((t, 512), F32), jax.ShapeDtypeStruct((4, t // CHUNK, 128, 128), BF16)]
    if fused:
        ins += [ofw] + [p] * n2 + [gain]
        specs += [oblk] + col(22) + [pl.BlockSpec((1, 128), lambda h, n: (0, 0))]
        out_specs.append(oblk)
        out_shape.append(jax.ShapeDtypeStruct((t, 512), BF16))
    return _pcall(body, name=name, grid=(4 // hp, nb), in_specs=specs, out_specs=out_specs, out_shape=out_shape,
                  scratch_shapes=[pltpu.VMEM((hp, 128, 128), F32)])(*ins)


def _hgrn_bwd(p, lb, sh, do, prev, *, rev, name, head=None):
    t = p.shape[0]
    nb, nc = t // TM, TM // CHUNK
    bmap = _blk_map(nb, rev, True)
    fcol = 14 if rev else 10
    has_prev = prev is not None
    odt = BF16
    fused = head is not None

    n2 = HGRN_HP // 2

    def body(*refs):
        refs = list(refs)
        q_refs, f_refs, v_refs = refs[:n2], refs[n2:2 * n2], refs[2 * n2:3 * n2]
        lb_ref, sh_ref = refs[3 * n2], refs[3 * n2 + 1]
        pos = 3 * n2 + 2
        if fused:
            osum_ref, g_refs, dmix_ref, gain_ref = refs[pos], refs[pos + 1:pos + 1 + n2], refs[pos + 1 + n2], refs[pos + 2 + n2]
            pos += 3 + n2
        else:
            do_ref = refs[pos]
            pos += 1
        if has_prev:
            pq_ref, pv_ref = refs[pos], refs[pos + 1]
            pos += 2
        dq_ref, df_ref, dv_ref, dlb_ref = refs[pos:pos + 4]
        pos += 4
        if fused:
            do_out, dg_ref, dgain_ref = refs[pos:pos + 3]
            pos += 3
        dst = refs[pos]

        @pl.when(pl.program_id(1) == 0)
        def _():
            dst[...] = jnp.zeros_like(dst)
            dlb_ref[...] = jnp.zeros_like(dlb_ref)

        if fused:
            @pl.when((pl.program_id(0) == 0) & (pl.program_id(1) == 0))
            def _():
                dgain_ref[...] = jnp.zeros_like(dgain_ref)

        cat = functools.partial(jnp.concatenate, axis=0)
        for hh in range(HGRN_HP):
            ln = slice(128 * hh, 128 * hh + 128)
            lbv = lb_ref[:, ln]
            qraw, fraw = _head_cols(q_refs, hh), _head_cols(f_refs, hh)
            q, k, lf, sq, sf, f = _hgrn_gates(qraw, fraw, lbv)
            tri, eq, ek, ei, eki, eb = _block_terms(lf, rev)
            qe, ke, qi, ki = q * eq, k * ek, q * ei, k * eki
            if fused:
                dov, dg, dgain = _headnorm_grad(osum_ref[:, ln], _head_cols(g_refs, hh), dmix_ref[:, ln], gain_ref[...])
                do_out[:, ln] = _bf(dov)
                dg_ref[:, ln] = dg
                _acc_row(dgain_ref, 0, dgain)
            else:
                dov = do_ref[:, ln]
            qeb, keb, qib, kib, vb, dob = _bf(qe), _bf(ke), _bf(qi), _bf(ki), _bf(_head_cols(v_refs, hh)), _bf(dov)
            dv, dqe, dke, dqi = [None] * nc, [None] * nc, [None] * nc, [None] * nc
            for cc in range(nc):
                rows = slice(cc * CHUNK, (cc + 1) * CHUNK)
                a = jnp.where(tri, _dot_nt(qeb[rows], keb[rows]), 0.0)
                da = jnp.where(tri, _dot_nt(dob[rows], vb[rows]), 0.0)
                dv[cc] = _dot_tn(a, dob[rows])
                dqe[cc], dke[cc] = _dot(da, keb[rows]), _dot_tn(da, qeb[rows])
                dqi[cc] = _dot(dob[rows], sh_ref[hh, cc])
            dki, dbl = [None] * nc, [None] * nc
            ds = dst[hh]
            for cc in _chunk_order(rev, True):
                rows = slice(cc * CHUNK, (cc + 1) * CHUNK)
                ebc = eb[cc * CHUNK:cc * CHUNK + 1, :]
                dv[cc] = dv[cc] + _dot_nt(kib[rows], ds)
                dki[cc] = _dot(vb[rows], ds)
                dbl[cc] = jnp.broadcast_to(jnp.sum(dki[cc] * ki[rows], axis=0, keepdims=True)
                                           + jnp.sum(ds * sh_ref[hh, cc], axis=0, keepdims=True) * ebc, (CHUNK, 128))
                ds = ds * ebc + _dot_tn(dob[rows], qib[rows])
            dst[hh] = ds
            dqe, dke, dqi, dki, dv, dbl = cat(dqe), cat(dke), cat(dqi), cat(dki), cat(dv), cat(dbl)
            dq = dqe * eq + dqi * ei
            dk = dke * ek + dki * eki
            last = 0 if rev else CHUNK - 1
            db = dqe * qe - dke * ke + dqi * qi - dki * ki
            db = db + jnp.where((_iota(db.shape, 0) & (CHUNK - 1)) == last, dbl, 0.0)
            dlf = _chunk_cumsum(db, not rev)
            dqr = dq * (sq * (1.0 + qraw * (1.0 - sq)))
            dfv = dlf / f - dk
            dfr = dfv * (1.0 - lbv) * (sf * (1.0 - sf))
            dlb_ref[:, ln] += jnp.sum(dfv * (1.0 - sf), axis=0, keepdims=True)
            if has_prev:
                dqr = dqr + pq_ref[:, ln]
                dv = dv + pv_ref[:, ln]
            dq_ref[:, ln] = dqr.astype(odt)
            df_ref[:, ln] = dfr.astype(odt)
            dv_ref[:, ln] = dv.astype(odt)

    hp, wd = HGRN_HP, 128 * HGRN_HP
    col = functools.partial(_hgrn_cols, bmap, n2)
    oblk = pl.BlockSpec((TM, wd), lambda h, n: (bmap(n), h))
    ins = [p] * (3 * n2) + [lb, sh]
    specs = col(6) + col(fcol) + col(18) + [pl.BlockSpec((1, wd), lambda h, n: (0, h)),
                                            pl.BlockSpec((hp, nc, 128, 128), lambda h, n: (h, bmap(n), 0, 0))]
    if fused:
        osum, dmix, gain = head
        ins += [osum] + [p] * n2 + [dmix, gain]
        specs += [oblk] + col(22) + [pl.BlockSpec((TM, wd), lambda h, n: (bmap(n), 4 // hp + h)),
                                     pl.BlockSpec((1, 128), lambda h, n: (0, 0))]
    else:
        ins.append(do); specs.append(oblk)
    if has_prev:
        ins += list(prev); specs += [oblk, oblk]
    out_specs = [oblk, oblk, oblk, pl.BlockSpec((1, wd), lambda h, n: (0, h))]
    out_shape = [jax.ShapeDtypeStruct((t, 512), odt)] * 3 + [jax.ShapeDtypeStruct((1, 512), F32)]
    if fused:
        out_specs += [oblk, oblk, pl.BlockSpec((8, 128), lambda h, n: (0, 0))]
        out_shape += [jax.ShapeDtypeStruct((t, 512), BF16), jax.ShapeDtypeStruct((t, 512), BF16),
                      jax.ShapeDtypeStruct((8, 128), F32)]
    return _pcall(body, name=name, grid=(4 // hp, nb), in_specs=specs, out_specs=out_specs, out_shape=out_shape,
                  scratch_shapes=[pltpu.VMEM((hp, 128, 128), F32)])(*ins)


def _rope256(x, cos, sin):
    x1, x2 = x[:, 0:128], x[:, 128:256]
    return jnp.concatenate([x1 * cos - x2 * sin, x2 * cos + x1 * sin], axis=-1)


def _rope256_t(d, cos, sin):
    d1, d2 = d[:, 0:128], d[:, 128:256]
    return jnp.concatenate([d1 * cos + d2 * sin, d2 * cos - d1 * sin], axis=-1)


RET_DK, RET_DV, RET_H = 256, 512, 4
RET_KSCALE = RET_DK ** -0.5
RCH = TM
RET_HP = 4


def _ret_terms(lg, rev):
    r, c = _iota((RCH, RCH), 0), _iota((RCH, RCH), 1)
    rel = ((c - r) if rev else (r - c)).astype(F32)
    dmat = jnp.where(rel >= 0, jnp.exp(lg[:, 0:1] * jnp.maximum(rel, 0.0)), 0.0)
    pos = _iota((RCH, 1), 0).astype(F32)
    cnt = (RCH - pos) if rev else (pos + 1.0)
    ei = jnp.exp(lg * cnt)
    eki = jnp.exp(lg * (RCH - cnt))
    eb = jnp.exp(lg * float(RCH))
    return dmat, ei, eki, eb


def _ret_fwd(p, lgt, cos, sin, *, rev, name, ofw=None):
    t = p.shape[0]
    nb, nc = t // TM, TM // RCH
    bmap = _blk_map(nb, rev, False)
    fused = ofw is not None

    def body(*refs):
        q_ref, k_ref, v_ref, lg_ref, c_ref, s_ref = refs[:6]
        if fused:
            ofw_ref, g_ref, o_ref, sh_ref, mix_ref, st = refs[6:]
        else:
            o_ref, sh_ref, st = refs[6:]

        @pl.when(pl.program_id(1) == 0)
        def _():
            st[...] = jnp.zeros_like(st)
        for hh in range(RET_HP):
            qc, vc = slice(RET_DK * hh, RET_DK * (hh + 1)), slice(RET_DV * hh, RET_DV * (hh + 1))
            dmat, ei, eki, eb = _ret_terms(lg_ref[hh], rev)
            for cc in _chunk_order(rev, False, nc):
                rows = slice(cc * RCH, (cc + 1) * RCH)
                cosv, sinv = c_ref[rows, :], s_ref[rows, :]
                q = _rope256(q_ref[rows, qc].astype(F32), cosv, sinv)
                k = _rope256(k_ref[rows, qc].astype(F32), cosv, sinv) * RET_KSCALE
                v = v_ref[rows, vc]
                s0 = st[hh]
                sh_ref[hh, cc] = s0.astype(BF16)
                a = _dot_nt(q, k) * dmat
                o = _dot(a, v) + _dot_nt(q * ei, s0)
                st[hh] = s0 * eb + _dot_tn(v, k * eki)
                if fused:
                    o = o + ofw_ref[rows, vc]
                    mix_ref[rows, vc] = _headnorm_apply(o, g_ref[rows, vc].astype(F32), None)
                o_ref[rows, vc] = o

    hp = RET_HP
    tab = pl.BlockSpec((TM, 128), lambda h, n: (bmap(n), 0))
    oblk = pl.BlockSpec((TM, hp * RET_DV), lambda h, n: (bmap(n), h))
    ins = [p, p, p, lgt, cos, sin]
    specs = [pl.BlockSpec((TM, hp * RET_DK), lambda h, n: (bmap(n), h)),
             pl.BlockSpec((TM, hp * RET_DK), lambda h, n: (bmap(n), RET_H // hp + h)),
             pl.BlockSpec((TM, hp * RET_DV), lambda h, n: (bmap(n), RET_H // hp + h)),
             pl.BlockSpec((hp, 1, RET_DK), lambda h, n: (h, 0, 0)), tab, tab]
    out_specs = [oblk, pl.BlockSpec((hp, nc, RET_DV, RET_DK), lambda h, n: (h, bmap(n), 0, 0))]
    out_shape = [jax.ShapeDtypeStruct((t, RET_H * RET_DV), F32),
                 jax.ShapeDtypeStruct((RET_H, t // RCH, RET_DV, RET_DK), BF16)]
    if fused:
        ins += [ofw, p]
        specs += [oblk, pl.BlockSpec((TM, hp * RET_DV), lambda h, n: (bmap(n), 2 * RET_H // hp + h))]
        out_specs.append(oblk)
        out_shape.append(jax.ShapeDtypeStruct((t, RET_H * RET_DV), BF16))
    return _pcall(body, name=name, grid=(RET_H // hp, nb), in_specs=specs, out_specs=out_specs, out_shape=out_shape,
                  scratch_shapes=[pltpu.VMEM((hp, RET_DV, RET_DK), F32)])(*ins)


def _ret_bwd(p, lgt, cos, sin, sh, do, prev, *, rev, name, head=None):
    t = p.shape[0]
    nb, nc = t // TM, TM // RCH
    bmap = _blk_map(nb, rev, True)
    has_prev = prev is not None
    odt = BF16
    fused = head is not None

    def body(*refs):
        refs = list(refs)
        q_ref, k_ref, v_ref, lg_ref, c_ref, s_ref, sh_ref = refs[:7]
        if fused:
            osum_ref, g_ref, dmix_ref = refs[7:10]
            pos = 10
        else:
            do_ref = refs[7]
            pos = 8
        if has_prev:
            pq_ref, pk_ref, pv_ref = refs[pos:pos + 3]
            pos += 3
        dq_ref, dk_ref, dv_ref = refs[pos:pos + 3]
        pos += 3
        if fused:
            do_out, dg_ref = refs[pos:pos + 2]
            pos += 2
        dst = refs[pos]

        @pl.when(pl.program_id(1) == 0)
        def _():
            dst[...] = jnp.zeros_like(dst)

        for hh in range(RET_HP):
            qc, vc = slice(RET_DK * hh, RET_DK * (hh + 1)), slice(RET_DV * hh, RET_DV * (hh + 1))
            dmat, ei, eki, eb = _ret_terms(lg_ref[hh], rev)
            for cc in _chunk_order(rev, True, nc):
                rows = slice(cc * RCH, (cc + 1) * RCH)
                cosv, sinv = c_ref[rows, :], s_ref[rows, :]
                q = _rope256(q_ref[rows, qc].astype(F32), cosv, sinv)
                k = _rope256(k_ref[rows, qc].astype(F32), cosv, sinv) * RET_KSCALE
                v = v_ref[rows, vc]
                if fused:
                    dov, dg, _ = _headnorm_grad(osum_ref[rows, vc], g_ref[rows, vc].astype(F32), dmix_ref[rows, vc], None)
                    do_out[rows, vc] = _bf(dov)
                    dg_ref[rows, vc] = dg
                else:
                    dov = do_ref[rows, vc]
                s0 = sh_ref[hh, cc]
                dsc = dst[hh]
                qi, ki = q * ei, k * eki
                a = _dot_nt(q, k) * dmat
                da = _dot_nt(dov, v) * dmat
                dv = _dot_tn(a, dov) + _dot_nt(ki, dsc)
                dqs = _dot(da, k) + _dot(dov, s0) * ei
                dks = _dot_tn(da, q) + _dot(v, dsc) * eki
                dst[hh] = dsc * eb + _dot_tn(dov, qi)
                dq = _rope256_t(dqs, cosv, sinv)
                dk = _rope256_t(dks * RET_KSCALE, cosv, sinv)
                if has_prev:
                    dq = dq + pq_ref[rows, qc]
                    dk = dk + pk_ref[rows, qc]
                    dv = dv + pv_ref[rows, vc]
                dq_ref[rows, qc] = dq.astype(odt)
                dk_ref[rows, qc] = dk.astype(odt)
                dv_ref[rows, vc] = dv.astype(odt)

    hp = RET_HP
    tab = pl.BlockSpec((TM, 128), lambda h, n: (bmap(n), 0))
    qblk = pl.BlockSpec((TM, hp * RET_DK), lambda h, n: (bmap(n), h))
    vblk = pl.BlockSpec((TM, hp * RET_DV), lambda h, n: (bmap(n), h))
    ins = [p, p, p, lgt, cos, sin, sh]
    specs = [qblk, pl.BlockSpec((TM, hp * RET_DK), lambda h, n: (bmap(n), RET_H // hp + h)),
             pl.BlockSpec((TM, hp * RET_DV), lambda h, n: (bmap(n), RET_H // hp + h)),
             pl.BlockSpec((hp, 1, RET_DK), lambda h, n: (h, 0, 0)), tab, tab,
             pl.BlockSpec((hp, nc, RET_DV, RET_DK), lambda h, n: (h, bmap(n), 0, 0))]
    if fused:
        osum, dmix = head
        ins += [osum, p, dmix]
        specs += [vblk, pl.BlockSpec((TM, hp * RET_DV), lambda h, n: (bmap(n), 2 * RET_H // hp + h)), vblk]
    else:
        ins.append(do); specs.append(vblk)
    if has_prev:
        ins += list(prev); specs += [qblk, qblk, vblk]
    out_specs = [qblk, qblk, vblk]
    out_shape = [jax.ShapeDtypeStruct((t, RET_H * RET_DK), odt), jax.ShapeDtypeStruct((t, RET_H * RET_DK), odt),
                 jax.ShapeDtypeStruct((t, RET_H * RET_DV), odt)]
    if fused:
        out_specs += [vblk, vblk]
        out_shape += [jax.ShapeDtypeStruct((t, RET_H * RET_DV), BF16), jax.ShapeDtypeStruct((t, RET_H * RET_DV), BF16)]
    return _pcall(body, name=name, grid=(RET_H // hp, nb), in_specs=specs, out_specs=out_specs, out_shape=out_shape,
                  scratch_shapes=[pltpu.VMEM((hp, RET_DV, RET_DK), F32)])(*ins)


def _rope_tables(lc, l):
    tt = jnp.arange(l)
    row, colp = (tt // 64).astype(F32), (tt % 64).astype(F32)
    inv = 10000.0 ** (-jnp.arange(16, dtype=F32) / 16)
    ang = jnp.concatenate([row[:, None] * inv, colp[:, None] * inv], axis=-1)
    ang = jnp.concatenate([jnp.zeros((lc, 32), F32), ang], axis=0)
    acos, asin = jnp.tile(jnp.cos(ang), (1, 4)), jnp.tile(jnp.sin(ang), (1, 4))
    theta = 1.0 / (10000.0 ** jnp.linspace(0.0, 1.0, 128, dtype=F32))
    rang = jnp.arange(l, dtype=F32)[:, None] * theta
    rang = jnp.concatenate([jnp.zeros((lc, 128), F32), rang], axis=0)
    return acos, asin, jnp.cos(rang), jnp.sin(rang)


class _Weights:
    def __init__(self, w):
        self.w = w

    def landed(self, grp, after):
        pass

    def full(self, grp, after):
        return self.w

    def send_grads(self, grp, grads):
        return jnp.zeros((8, 128), F32)


def _local_step(x0, target, mods, ng, wsrc, small):
    t, d = x0.shape
    l = target.shape[0]
    lc = t - l
    acos, asin, rcos, rsin = _rope_tables(lc, l)
    lg_fw = jnp.log(1.0 - 2.0 ** (-5.0 - jnp.arange(RET_H, dtype=F32)))
    lgt_fw = jnp.broadcast_to(lg_fw[:, None, None], (RET_H, 1, RET_DK))
    lgt_bw = jnp.broadcast_to(lg_fw[::-1][:, None, None], (RET_H, 1, RET_DK))
    gq, gk, sink, gain, lb = small['gq'], small['gk'], small['sink'], small['gain'], small['lb']

    (h1,) = _row_fwd(x0, mods, g=ng[0], shift=0, scale=1, name='l0_norm1')
    wsrc.landed('even', h1)
    w = dict(wsrc.full('even', h1))
    p0 = _mm_nn(h1, w['even_in'], name='l0_in')
    kp = _kprep_fwd(p0, gk, acos, asin, name='l0_kprep')
    att = _attn_fwd(p0, kp, gq, sink, acos, asin, lc=lc, name='l0_attn')
    wsrc.landed('ffn', att)
    hof, hsf = _hgrn_fwd(p0, lb, rev=False, name='l0_hgrn_f')
    wsrc.landed('odd', hof)
    hos, hsb, bmix = _hgrn_fwd(p0, lb, rev=True, name='l0_hgrn_b', ofw=hof, gain=gain)
    mix0 = [att, bmix]
    y0 = _mm_nn(mix0, w['even_out'], name='l0_out')
    x1, h2 = _row_fwd(x0, mods, y=y0, gate=2, g=ng[1], shift=3, scale=4, name='l0_norm2')
    w.update(wsrc.full('ffn', h2))
    u0, a0 = _ffn_in(h2, w['ffn_in'], lead=0, name='ffn_in')
    z0 = _mm_nn(a0, w['ffn_out'], lead=0, name='ffn_out')
    x2, h3 = _row_fwd(x1, mods, y=z0, gate=5, g=ng[2], shift=12, scale=13, name='l1_norm1')
    w.update(wsrc.full('odd', h3))
    p1 = _mm_nn(h3, w['odd_in'], out_dtype=BF16, name='l1_in')
    rof, rsf = _ret_fwd(p1, lgt_fw, rcos, rsin, rev=False, name='l1_ret_f')
    ros, rsb, mix1 = _ret_fwd(p1, lgt_bw, rcos, rsin, rev=True, name='l1_ret_b', ofw=rof)
    y1 = _mm_nn(mix1, w['odd_out'], name='l1_out')
    x3, h4 = _row_fwd(x2, mods, y=y1, gate=14, g=ng[3], shift=15, scale=16, name='l1_norm2')
    u1, a1 = _ffn_in(h4, w['ffn_in'], lead=1, name='ffn_in')
    z1 = _mm_nn(a1, w['ffn_out'], lead=1, name='ffn_out')
    loss, dx4, dz1, s_fin = _row_final(x3, z1, mods, target, gate=17, name='loss')

    du1 = _ffn_dx(dz1, w['ffn_out'], u1, lead=1, name='ffn_out_dx')
    g_ffn_out1 = _mm_tn(a1, dz1, name='ffn_out_dw')
    dh4 = _mm_nt(du1, w['ffn_in'], lead=1, name='ffn_in_dx')
    g_ffn_in1 = _mm_tn(h4, du1, name='ffn_in_dw')
    dx3, dy1, s_l1n2 = _row_bwd(x3, dx4, dh4, mods, ng[3], shift=15, scale=16, y=y1, gate=14, name='l1_norm2_bwd')
    dmix1 = _mm_nt(dy1, w['odd_out'], name='l1_out_dx')
    g_odd_out = _mm_tn(mix1, dy1, name='l1_out_dw')
    rdq, rdk, rdv, rdo, rdg = _ret_bwd(p1, lgt_fw, rcos, rsin, rsf, None, None, rev=False, name='l1_ret_f_bwd',
                                       head=(ros, dmix1))
    rdq, rdk, rdv = _ret_bwd(p1, lgt_bw, rcos, rsin, rsb, rdo, (rdq, rdk, rdv), rev=True, name='l1_ret_b_bwd')
    dp1 = [rdq, rdk, rdv, rdg]
    dh3 = _mm_nt(dp1, w['odd_in'], name='l1_in_dx')
    g_odd_in = _mm_tn(h3, dp1, name='l1_in_dw')
    mods = mods + wsrc.send_grads('early', dict(ffn_in1=g_ffn_in1, ffn_out1=g_ffn_out1, odd_in=g_odd_in,
                                                odd_out=g_odd_out))[0, 0]
    dx2, dz0, s_l1n1 = _row_bwd(x2, dx3, dh3, mods, ng[2], shift=12, scale=13, y=z0, gate=5, name='l1_norm1_bwd')
    du0 = _ffn_dx(dz0, w['ffn_out'], u0, lead=0, name='ffn_out_dx')
    g_ffn_out0 = _mm_tn(a0, dz0, name='ffn_out_dw')
    dh2 = _mm_nt(du0, w['ffn_in'], lead=0, name='ffn_in_dx')
    g_ffn_in0 = _mm_tn(h2, du0, name='ffn_in_dw')
    mods = mods + wsrc.send_grads('mid', dict(ffn_in0=g_ffn_in0, ffn_out0=g_ffn_out0))[0, 0]
    dx1, dy0, s_l0n2 = _row_bwd(x1, dx2, dh2, mods, ng[1], shift=3, scale=4, y=y0, gate=2, name='l0_norm2_bwd')
    dmix0 = _mm_nt(dy0, w['even_out'], name='l0_out_dx')
    g_even_out = _mm_tn(mix0, dy0, name='l0_out_dw')
    hq, hff, hv, dlb_f, hdo, hdg, s_gain = _hgrn_bwd(p0, lb, hsf, None, None, rev=False, name='l0_hgrn_f_bwd',
                                                     head=(hos, dmix0, gain))
    hq, hfb, hv, dlb_b = _hgrn_bwd(p0, lb, hsb, hdo, (hq, hv), rev=True, name='l0_hgrn_b_bwd')
    adq, dkp, adv, s_gq, s_sink = _attn_bwd(p0, kp, gq, sink, acos, asin, dmix0, lc=lc, name='l0_attn_bwd')
    dkv, s_gk = _kprep_bwd(p0, gk, acos, asin, dkp, adv, name='l0_kprep_bwd')
    dp0 = jnp.concatenate([adq, dkv, hq, _bf(hff), hfb, hv, hdg], axis=1)
    dh1 = _mm_nt(dp0, w['even_in'], name='l0_in_dx')
    g_even_in = _mm_tn(h1, dp0, name='l0_in_dw')
    dx0, s_l0n1 = _row_bwd(x0, dx1, dh1, mods, ng[0], shift=0, scale=1, latent_only=True, name='l0_norm1_bwd')

    grads = dict(ffn_in0=g_ffn_in0, ffn_in1=g_ffn_in1, ffn_out0=g_ffn_out0, ffn_out1=g_ffn_out1,
                 even_in=g_even_in, even_out=g_even_out, odd_in=g_odd_in, odd_out=g_odd_out)
    sums = dict(fin=s_fin, l1n2=s_l1n2, l1n1=s_l1n1, l0n2=s_l0n2, l0n1=s_l0n1, gain=s_gain, gq=s_gq, gk=s_gk,
                sink=s_sink, dlb_f=dlb_f, dlb_b=dlb_b)
    return loss, dx0, grads, sums


def _place():
    return lax.axis_index("x"), lax.axis_index("y"), lax.axis_index("c")


def _ag8(blk, *, name):
    r, c = blk.shape
    flips = [(dx, dy, dc) for dx in (0, 1) for dy in (0, 1) for dc in (0, 1) if (dx, dy, dc) != (0, 0, 0)]

    def body(x_ref, out_ref, send_sems, recv_sems, local_sem):
        ax, ay, ac = _place()
        me = 4 * ax + 2 * ay + ac
        mine = pltpu.make_async_copy(x_ref, out_ref.at[me], local_sem)
        mine.start()
        sent = []
        for k, (dx, dy, dc) in enumerate(flips):
            peer = (lax.rem(ax + dx, 2), lax.rem(ay + dy, 2), lax.rem(ac + dc, 2))
            cp = pltpu.make_async_remote_copy(src_ref=x_ref, dst_ref=out_ref.at[me], send_sem=send_sems.at[k],
                                              recv_sem=recv_sems.at[k], device_id=peer, device_id_type=MESH)
            cp.start()
            sent.append((cp, 4 * peer[0] + 2 * peer[1] + peer[2]))
        for k, (cp, pidx) in enumerate(sent):
            pltpu.make_async_remote_copy(src_ref=x_ref, dst_ref=out_ref.at[pidx], send_sem=send_sems.at[k],
                                         recv_sem=recv_sems.at[k], device_id=(ax, ay, ac),
                                         device_id_type=MESH).wait_recv()
        for cp, _ in sent:
            cp.wait_send()
        mine.wait()

    return _pcall(
        body, name=name,
        in_specs=[pl.BlockSpec(memory_space=pltpu.VMEM)],
        out_specs=pl.BlockSpec(memory_space=pltpu.VMEM),
        out_shape=jax.ShapeDtypeStruct((8, r, c), blk.dtype),
        scratch_shapes=[pltpu.SemaphoreType.DMA((7,)), pltpu.SemaphoreType.DMA((7,)), pltpu.SemaphoreType.DMA],
    )(blk)


_HBM = pl.BlockSpec(memory_space=pltpu.HBM)
_SEM = pl.BlockSpec(memory_space=pltpu.SEMAPHORE)
_DATAFLOW = pltpu.SideEffectType.DATAFLOW_SIDE_EFFECTING


def _split_start(bufs, plan, k, *, name):
    n = len(bufs)

    def body(*refs):
        ins, send_sems, recv_sems, token = refs[:n], refs[n], refs[n + 1], refs[2 * n + 2]
        for i, (src, dst, dev) in enumerate(plan(ins)):
            pltpu.make_async_remote_copy(src_ref=src, dst_ref=dst, send_sem=send_sems.at[i], recv_sem=recv_sems.at[i],
                                         device_id=dev, device_id_type=MESH).start()
        token[...] = jnp.zeros_like(token)

    res = _pcall(
        body, name=name,
        out_shape=(pltpu.SemaphoreType.DMA((k,)), pltpu.SemaphoreType.DMA((k,)),
                   *[pltpu.HBM(b.shape, b.dtype) for b in bufs], jax.ShapeDtypeStruct((8, 128), F32)),
        in_specs=[_HBM] * n, out_specs=(_SEM, _SEM, *[_HBM] * n, pl.BlockSpec(memory_space=pltpu.VMEM)),
        input_output_aliases={i: 2 + i for i in range(n)},
        compiler_params=pltpu.CompilerParams(has_side_effects=_DATAFLOW),
    )(*[pltpu.with_memory_space_constraint(b, pltpu.HBM) for b in bufs])
    return res[0], res[1], list(res[2:2 + n]), res[2 + n]


def _split_wait(bufs, send_sems, recv_sems, plan, after, *, name):
    n = len(bufs)

    def body(*refs):
        ins, ssem, rsem = refs[:n], refs[n], refs[n + 1]
        for i, (src, dst, dev) in enumerate(plan(ins)):
            cp = pltpu.make_async_remote_copy(src_ref=src, dst_ref=dst, send_sem=ssem.at[i], recv_sem=rsem.at[i],
                                              device_id=dev, device_id_type=MESH)
            cp.wait_send()
            cp.wait_recv()

    res = _pcall(
        body, name=name, out_shape=tuple(pltpu.HBM(b.shape, b.dtype) for b in bufs),
        in_specs=[_HBM] * n + [_SEM, _SEM, pl.BlockSpec(memory_space=pl.ANY)], out_specs=tuple([_HBM] * n),
        input_output_aliases={i: i for i in range(n)},
        compiler_params=pltpu.CompilerParams(has_side_effects=_DATAFLOW),
    )(*bufs, send_sems, recv_sems, after)
    return list(res)


_CHIP_FLIPS = [(1, 0), (0, 1), (1, 1)]


class _GatheredWeights:
    GROUPS = (('even', ('even_in', 'even_out')), ('ffn', ('ffn_in', 'ffn_out')), ('odd', ('odd_in', 'odd_out')))

    def __init__(self, shards, reducer):
        self.shards = shards
        self.send_grads = reducer.start
        self.ici, self.d2d, self.token = {}, {}, None
        for grp, names in self.GROUPS:
            src = [shards[nm].reshape(2, shards[nm].shape[0] // 2, shards[nm].shape[1]) for nm in names]
            land = [lax.empty((4,) + a.shape, a.dtype) for a in src]
            m = len(names)
            sends, recvs, bufs, token = _split_start(src + land, functools.partial(self._ici_plan, m, True), 4 * m,
                                                     name='gather_' + grp + '_ici_start')
            self.ici[grp] = (sends, recvs, bufs, m)
            self.token = token if self.token is None else self.token + token

    @staticmethod
    def _ici_plan(m, sending, refs):
        ax, ay, ac = _place()
        s = 2 * ax + ay
        out = []
        for a in range(m):
            for dx, dy in _CHIP_FLIPS:
                px, py = lax.rem(ax + dx, 2), lax.rem(ay + dy, 2)
                slot = s if sending else 2 * px + py
                out.append((refs[a].at[ac], refs[m + a].at[slot, ac], (px, py, ac)))
        for a in range(m):
            out.append((refs[a], refs[m + a].at[s], (ax, ay, 1 - ac)))
        return out

    @staticmethod
    def _d2d_plan(m, sending, refs):
        ax, ay, ac = _place()
        out = []
        for a in range(m):
            for dx, dy in _CHIP_FLIPS:
                sp = 2 * lax.rem(ax + dx, 2) + lax.rem(ay + dy, 2)
                out.append((refs[a].at[sp, ac], refs[a].at[sp, ac if sending else 1 - ac], (ax, ay, 1 - ac)))
        return out

    def landed(self, grp, after):
        sends, recvs, bufs, m = self.ici[grp]
        bufs = _split_wait(bufs, sends, recvs, functools.partial(self._ici_plan, m, False), after,
                           name='gather_' + grp + '_ici_wait')
        sends, recvs, land, _ = _split_start(bufs[m:], functools.partial(self._d2d_plan, m, True), 3 * m,
                                             name='gather_' + grp + '_d2d_start')
        self.d2d[grp] = (sends, recvs, land, m)

    def full(self, grp, after):
        sends, recvs, land, m = self.d2d[grp]
        land = _split_wait(land, sends, recvs, functools.partial(self._d2d_plan, m, False), after,
                           name='gather_' + grp + '_d2d_wait')
        names = dict(self.GROUPS)[grp]
        return {nm: _from_shards(nm, g.reshape((4,) + self.shards[nm].shape)) for nm, g in zip(names, land)}


def _to_sibling(arrs, *, name):
    n = len(arrs)

    def body(*refs):
        ins, outs = refs[:n], refs[n:2 * n]
        send_sems, recv_sems = refs[2 * n:]
        ax, ay, ac = _place()
        cps = [pltpu.make_async_remote_copy(src_ref=ins[a], dst_ref=outs[a], send_sem=send_sems.at[a],
                                            recv_sem=recv_sems.at[a], device_id=(ax, ay, 1 - ac),
                                            device_id_type=MESH) for a in range(n)]
        for cp in cps:
            cp.start()
        for cp in cps:
            cp.wait_recv()
        for cp in cps:
            cp.wait_send()

    hbm = pl.BlockSpec(memory_space=pl.ANY)
    return _pcall(
        body, name=name, in_specs=[hbm] * n, out_specs=[hbm] * n,
        out_shape=[jax.ShapeDtypeStruct(a.shape, a.dtype) for a in arrs],
        scratch_shapes=[pltpu.SemaphoreType.DMA((n,))] * 2,
    )(*arrs)


def _mod_fwd(cond_raw, mw, mb, *, name):
    _, d, n = mw.shape

    def body(c_ref, w_ref, b_ref, o_ref):
        cv = c_ref[...]
        o_ref[...] = _dot(cv * _sigmoid(cv), w_ref[...]) + b_ref[...]

    return _pcall(
        body, name=name, grid=(2,),
        in_specs=[pl.BlockSpec((16, d), lambda l: (0, 0)), pl.BlockSpec((None, d, n), lambda l: (l, 0, 0)),
                  pl.BlockSpec((None, 1, n), lambda l: (l, 0, 0))],
        out_specs=pl.BlockSpec((None, 16, n), lambda l: (l, 0, 0)),
        out_shape=jax.ShapeDtypeStruct((2, 16, n), F32),
    )(cond_raw, mw, mb)


def _mod_bwd(cond_raw, dms, mw, *, name):
    _, d, n = mw.shape

    def body(c_ref, dm_ref, w_ref, gw_ref, dc_ref):
        @pl.when(pl.program_id(0) == 0)
        def _():
            dc_ref[...] = jnp.zeros_like(dc_ref)
        cv = c_ref[...]
        gw_ref[...] = _dot_tn(cv * _sigmoid(cv), dm_ref[...])
        dc_ref[...] += _dot_nt(dm_ref[...], w_ref[...])

    return _pcall(
        body, name=name, grid=(2,),
        in_specs=[pl.BlockSpec((16, d), lambda l: (0, 0)), pl.BlockSpec((None, 16, n), lambda l: (l, 0, 0)),
                  pl.BlockSpec((None, d, n), lambda l: (l, 0, 0))],
        out_specs=[pl.BlockSpec((None, d, n), lambda l: (l, 0, 0)), pl.BlockSpec((16, d), lambda l: (0, 0))],
        out_shape=[jax.ShapeDtypeStruct((2, d, n), F32), jax.ShapeDtypeStruct((16, d), F32)],
    )(cond_raw, dms, mw)


def _lb_fwd(hgrn_lb, *, name):
    def body(a_ref, o_ref):
        a0, a1 = a_ref[0:1, :], a_ref[1:2, :]
        m = jnp.maximum(a0, a1)
        e0, e1 = jnp.exp(a0 - m), jnp.exp(a1 - m)
        o_ref[...] = e0 / (e0 + e1)

    return _pcall(body, name=name, out_shape=jax.ShapeDtypeStruct((1, hgrn_lb.shape[1]), F32))(hgrn_lb)


PACK_TILES = ('l0n1', 'l0n2', 'l1n1', 'l1n2', 'fin')
PACK_SINGLES = ('gq', 'gk', 'gain', 'dlb_f', 'dlb_b', 'sink')
PACK_ROW = {nm: 8 * i for i, nm in enumerate(PACK_TILES)}
PACK_ROW.update({nm: 8 * len(PACK_TILES) + i for i, nm in enumerate(PACK_SINGLES)})
MOD_SOURCE = ((('l0n1', 0), ('l0n1', 1), ('l0n2', 2), ('l0n2', 0), ('l0n2', 1), ('l1n1', 2)),
              (('l1n1', 0), ('l1n1', 1), ('l1n2', 2), ('l1n2', 0), ('l1n2', 1), ('fin', 2)))


def _small_finalize(gath, lb_pad, *, name):
    d = gath.shape[2]

    def body(g_ref, lb_ref, small_ref, glb_ref, gmb_ref, dm_ref):
        tot = g_ref[0]
        for e in range(1, 8):
            tot = tot + g_ref[e]

        def row(nm, r=0):
            return tot[PACK_ROW[nm] + r:PACK_ROW[nm] + r + 1, :]

        for k, nm in enumerate(('l0n1', 'l0n2', 'l1n1', 'l1n2')):
            small_ref[k:k + 1, :] = row(nm, 3) + row(nm, 7)
        for k, nm in ((4, 'gq'), (5, 'gk')):
            small_ref[k:k + 1, :] = row(nm) + pltpu.roll(row(nm), d - 64, 1)
        small_ref[6:7, :] = row('gain')
        small_ref[7:8, :] = row('sink')
        lbv = lb_ref[...]
        g0 = (row('dlb_f') + row('dlb_b')) * lbv * (1.0 - lbv)
        glb_ref[...] = jnp.zeros_like(glb_ref)
        glb_ref[0:1, :] = g0
        glb_ref[1:2, :] = -g0
        dm_ref[...] = jnp.zeros_like(dm_ref)
        for l in range(2):
            for part in range(6):
                nm, r = MOD_SOURCE[l][part]
                gmb_ref[l * 6 + part:l * 6 + part + 1, :] = row(nm, r) + row(nm, r + 4)
                rl = PACK_ROW[nm] + r + 4
                for e in range(8):
                    dm_ref[l, part, e:e + 1, :] = g_ref[e, rl:rl + 1, :]
                dm_ref[l, part, 8:9, :] = row(nm, r)

    return _pcall(
        body, name=name,
        out_shape=[jax.ShapeDtypeStruct((8, d), F32), jax.ShapeDtypeStruct((8, d), F32),
                   jax.ShapeDtypeStruct((12, d), F32), jax.ShapeDtypeStruct((2, 6, 16, d), F32)],
    )(gath, lb_pad)


def _cctx_grad(gath, c_ctx2, *, name):
    def body(g_ref, c_ref, o_ref):
        tot = ((g_ref[0, 0:1, :] + g_ref[2, 0:1, :]) + g_ref[4, 0:1, :]) + g_ref[6, 0:1, :]
        cv = c_ref[...]
        s = _sigmoid(cv)
        o_ref[...] = tot * (s * (1.0 + cv * (1.0 - s)))

    return _pcall(body, name=name, out_shape=jax.ShapeDtypeStruct(c_ctx2.shape, F32))(gath, c_ctx2)


def _row_block(r, c, limit=256 * 1024):
    best = None
    for br in range(16, r + 1, 16):
        if r % br == 0 and br * c <= limit:
            best = br
    return best if best is not None else r


def _sum4(own, landed, core, *, name):
    _, r, c = own.shape
    br = _row_block(r, c, 512 * 1024)

    def body(core_ref, own_ref, land_ref, o_ref):
        s = 2 * lax.axis_index("x") + lax.axis_index("y")
        p = [jnp.where(s == k, own_ref[k], land_ref[k]).astype(F32) for k in range(4)]
        o_ref[...] = ((p[0] + p[1]) + p[2]) + p[3]

    blk = pl.BlockSpec((4, br, c), lambda i, core_ref: (0, i, 0))
    spec = pltpu.PrefetchScalarGridSpec(
        num_scalar_prefetch=1, grid=(r // br,), in_specs=[blk, blk],
        out_specs=pl.BlockSpec((None, br, c), lambda i, core_ref: (core_ref[0], i, 0)))
    return _pcall(body, name=name, grid_spec=spec, out_shape=jax.ShapeDtypeStruct((2, r, c), F32))(core, own, landed)


def _exchange_halves(arrs, *, name):
    n = len(arrs)

    def body(*refs):
        ins, outs = refs[:n], refs[n:2 * n]
        send_sems, recv_sems = refs[2 * n:]
        ax, ay, ac = _place()
        cps = [pltpu.make_async_remote_copy(src_ref=ins[a].at[ac], dst_ref=outs[a].at[ac], send_sem=send_sems.at[a],
                                            recv_sem=recv_sems.at[a], device_id=(ax, ay, 1 - ac),
                                            device_id_type=MESH) for a in range(n)]
        for cp in cps:
            cp.start()
        for a in range(n):
            pltpu.make_async_remote_copy(src_ref=ins[a].at[ac], dst_ref=outs[a].at[1 - ac], send_sem=send_sems.at[a],
                                         recv_sem=recv_sems.at[a], device_id=(ax, ay, ac),
                                         device_id_type=MESH).wait_recv()
        for cp in cps:
            cp.wait_send()

    hbm = pl.BlockSpec(memory_space=pl.ANY)
    return _pcall(
        body, name=name, in_specs=[hbm] * n, out_specs=[hbm] * n,
        out_shape=[jax.ShapeDtypeStruct(a.shape, a.dtype) for a in arrs],
        input_output_aliases={a: a for a in range(n)},
        scratch_shapes=[pltpu.SemaphoreType.DMA((n,))] * 2,
    )(*arrs)


def _add2(a, b, *, name):
    r, c = a.shape
    br = _row_block(r, c, 1024 * 1024)

    def body(a_ref, b_ref, o_ref):
        o_ref[...] = (a_ref[...].astype(F32) + b_ref[...].astype(F32)).astype(BF16)

    blk = pl.BlockSpec((br, c), lambda i: (i, 0))
    return _pcall(body, name=name, grid=(r // br,), in_specs=[blk, blk], out_specs=blk,
                  out_shape=jax.ShapeDtypeStruct((r, c), BF16))(a, b)


def _adam(w, gs, m, v, *, name):
    r, c = w.shape
    br = _row_block(r, c)
    ng = len(gs)
    c1 = 1.0 - ADAM_B1 ** ADAM_STEP
    c2 = 1.0 - ADAM_B2 ** ADAM_STEP

    def body(*refs):
        w_ref, m_ref, v_ref = refs[0], refs[1 + ng], refs[2 + ng]
        outs = refs[3 + ng:]
        g = refs[1][...]
        for k in range(1, ng):
            g = g + refs[1 + k][...]
        mn = ADAM_B1 * m_ref[...] + (1.0 - ADAM_B1) * g
        vn = ADAM_B2 * v_ref[...] + (1.0 - ADAM_B2) * (g * g)
        if ng > 1:
            outs[0][...] = g
        d_out, m_out, v_out = outs[-3:]
        m_out[...] = mn
        v_out[...] = vn
        d_out[...] = -ADAM_LR * ((mn / c1) / (jnp.sqrt(vn / c2) + ADAM_EPS) + ADAM_WD * w_ref[...])

    blk = pl.BlockSpec((br, c), lambda i: (i, 0))
    nout = 4 if ng > 1 else 3
    res = _pcall(body, name=name, grid=(r // br,), in_specs=[blk] * (3 + ng), out_specs=[blk] * nout,
                 out_shape=[jax.ShapeDtypeStruct((r, c), F32)] * nout)(w, *gs, m, v)
    return list(res) if ng > 1 else [gs[0]] + list(res)


def _grad_halves(name, g, ac):
    if name.endswith('_in'):
        n = g.shape[1] // 4
        if name == 'ffn_in':
            assert n == FFN_BK
        order = _ffn_order(g.shape[1]) if name == 'ffn_in' else range(4)
        v = jnp.stack([g[:, b * n:(b + 1) * n] for b in order])
        per = [v[:, :g.shape[0] // 2], v[:, g.shape[0] // 2:]]
    else:
        k4, n = g.shape
        v = g.reshape(4, 2, k4 // 8, n)
        per = [v[:, 0], v[:, 1]]
    first = ac == 0
    return _bf(jnp.where(first, per[0], per[1])), _bf(jnp.where(first, per[1], per[0]))


class _GradReducer:
    def __init__(self):
        self.flight = {}

    @staticmethod
    def _plan(m, sending, refs):
        ax, ay, ac = _place()
        s = 2 * ax + ay
        out = []
        for a in range(m):
            for dx, dy in _CHIP_FLIPS:
                px, py = lax.rem(ax + dx, 2), lax.rem(ay + dy, 2)
                sp = 2 * px + py
                out.append((refs[a].at[sp], refs[m + a].at[s if sending else sp], (px, py, ac)))
        return out

    def start(self, grp, grads):
        ac = lax.axis_index("c")
        names = list(grads)
        halves = [_grad_halves(nm.rstrip('01'), grads[nm], ac) for nm in names]
        theirs = _to_sibling([h[1] for h in halves], name='swap_core_halves_' + grp)
        pair = [_add2(h[0].reshape(-1, b.shape[-1]), b.reshape(-1, b.shape[-1]), name='add_cores').reshape(b.shape)
                for h, b in zip(halves, theirs)]
        m = len(names)
        land = [lax.empty(a.shape, a.dtype) for a in pair]
        sends, recvs, bufs, token = _split_start(pair + land, functools.partial(self._plan, m, True), 3 * m,
                                                 name='scatter_' + grp + '_start')
        self.flight[grp] = (names, sends, recvs, bufs)
        return token

    def finish(self, grp, after):
        names, sends, recvs, bufs = self.flight.pop(grp)
        m = len(names)
        bufs = _split_wait(bufs, sends, recvs, functools.partial(self._plan, m, False), after,
                           name='scatter_' + grp + '_wait')
        core = lax.axis_index("c").astype(jnp.int32).reshape(1)
        sums = [_sum4(p, l, core, name='sum_chips') for p, l in zip(bufs[:m], bufs[m:])]
        both = _exchange_halves(sums, name='gather_core_halves_' + grp)
        return {nm: g.reshape(-1, g.shape[-1]) for nm, g in zip(names, both)}


def _from_shards(name, g):
    _, r, n = g.shape
    if name == 'ffn_in':
        assert n == FFN_BK
        v = g.reshape(4, 2, r // 2, n)
        return jnp.concatenate([v[b] for b in _ffn_order(4 * n)], axis=-1)
    if name == 'ffn_out':
        return g.reshape(4, 2, r // 2, n).transpose(1, 0, 2, 3).reshape(2, 2 * r, n)
    if name in ('even_in', 'odd_in'):
        return jnp.concatenate([g[b] for b in range(4)], axis=-1)
    return g.reshape(4 * r, n)


def kernel(x, c, ctx, c_ctx, mod_w, mod_b, norm_g, ffn_w_in, ffn_w_out, even_w_in, even_w_out, attn_qk_norm_g, attn_sink, hgrn_out_norm_g, hgrn_lb, odd_w_in, odd_w_out, loss_target, m_c_ctx, m_mod_w, m_mod_b, m_norm_g, m_ffn_w_in, m_ffn_w_out, m_even_w_in, m_even_w_out, m_attn_qk_norm_g, m_attn_sink, m_hgrn_out_norm_g, m_hgrn_lb, m_odd_w_in, m_odd_w_out, v_c_ctx, v_mod_w, v_mod_b, v_norm_g, v_ffn_w_in, v_ffn_w_out, v_even_w_in, v_even_w_out, v_attn_qk_norm_g, v_attn_sink, v_hgrn_out_norm_g, v_hgrn_lb, v_odd_w_in, v_odd_w_out):
    d = x.shape[-1]
    lc = ctx.shape[1]
    assert lc == TM and d == 1024
    ax, ay, ac = _place()
    s = 2 * ax + ay
    me = 4 * ax + 2 * ay + ac
    nmod = mod_w.shape[2]

    def pad8(v):
        return jnp.pad(v, ((0, 8 - v.shape[0]), (0, 0)))

    pack = jnp.concatenate([pad8(c), pad8(norm_g.reshape(1, d))], axis=0)
    g1 = _ag8(pack, name='gather_cond')
    c_all = g1[:, 0, :]
    ng = g1[0::2, 8, :].reshape(4, 2, 2, d // 4).transpose(1, 2, 0, 3).reshape(4, d)

    cond_raw = jnp.concatenate([c_all, pad8(c_ctx.reshape(1, d))], axis=0)
    mb_sh = lax.dynamic_slice_in_dim(mod_b, s * nmod, nmod, axis=1).reshape(2, 1, nmod)
    mpart = _mod_fwd(cond_raw, mod_w, mb_sh, name='mod_fwd')
    g3 = _ag8(mpart.reshape(32, nmod), name='gather_mods')
    mods_full = g3[0::2].reshape(4, 2, 16, nmod).transpose(1, 2, 0, 3).reshape(2, 16, 4 * nmod)
    m_lat = lax.dynamic_index_in_dim(mods_full, me, axis=1, keepdims=False)
    mods = jnp.stack([mods_full[:, 8], m_lat], axis=1).reshape(24, d)

    names = ['ffn_in', 'ffn_out', 'even_in', 'even_out', 'odd_in', 'odd_out']
    shards = [_bf(v.reshape(-1, v.shape[-1])) for v in (ffn_w_in, ffn_w_out, even_w_in, even_w_out, odd_w_in, odd_w_out)]
    shards, mods = lax.optimization_barrier((shards, mods))
    reducer = _GradReducer()
    wsrc = _GatheredWeights(dict(zip(names, shards)), reducer)

    lb = _lb_fwd(hgrn_lb, name='hgrn_lower_bound')
    small = dict(gq=jnp.tile(attn_qk_norm_g[0, 0], 2).reshape(1, 128), gk=jnp.tile(attn_qk_norm_g[0, 1], 2).reshape(1, 128),
                 sink=attn_sink[0], gain=hgrn_out_norm_g, lb=lb)
    x0 = jnp.concatenate([ctx[0], x[0]], axis=0)
    mods = mods + wsrc.token[0, 0]
    loss_t, dx0, grads, sums = _local_step(x0, loss_target[0], mods, ng, wsrc, small)
    loss = lax.psum(loss_t[0, 0], ("x", "y", "c"))
    grad_x = dx0[None]

    def tile(v, at=0):
        return jnp.pad(v[0:1], ((at, 7 - at), (0, d - v.shape[1])))

    sums = dict(sums, sink=sums['sink'][:, 0].reshape(1, 8))
    singles = sum(tile(sums[nm], i) for i, nm in enumerate(PACK_SINGLES))
    g4 = _ag8(jnp.concatenate([sums[nm] for nm in PACK_TILES] + [singles], axis=0), name='gather_row_sums')
    small_g, glb, gmb, dmat = _small_finalize(g4, tile(lb)[0:1], name='small_grads')
    dms = lax.dynamic_slice_in_dim(dmat.transpose(0, 2, 1, 3).reshape(2, 16, 6 * d), s * nmod, nmod, axis=2)
    g_mod_w, dcond = _mod_bwd(cond_raw, dms, mod_w, name='mod_bwd')
    g5 = _ag8(dcond[8:16], name='gather_dcond')
    g_c_ctx = _cctx_grad(g5, c_ctx.reshape(8, d // 8).reshape(1, d), name='c_ctx_grad')

    late = {nm: grads[nm] for nm in ('even_in', 'even_out')}
    late, g_c_ctx = lax.optimization_barrier((late, g_c_ctx))
    token = reducer.start('late', late)
    full = reducer.finish('early', token)

    def upd(wv, gs, mv, vv, name):
        shp = wv.shape
        c2 = shp[-1]
        out = _adam(wv.reshape(-1, c2), [g.reshape(-1, c2) for g in gs], mv.reshape(-1, c2), vv.reshape(-1, c2), name=name)
        return [o.reshape(shp) for o in out]

    res = {}
    res['c_ctx'] = upd(c_ctx.reshape(8, d // 8), [g_c_ctx.reshape(8, d // 8)], m_c_ctx.reshape(8, d // 8), v_c_ctx.reshape(8, d // 8), 'adam_c_ctx')
    res['c_ctx'] = [o.reshape(d) for o in res['c_ctx']]
    res['mod_w'] = upd(mod_w, [g_mod_w], m_mod_w, v_mod_w, 'adam_mod_w')
    res['mod_b'] = upd(mod_b, [gmb.reshape(2, 6 * d)], m_mod_b, v_mod_b, 'adam_mod_b')
    g_ng = lax.dynamic_slice_in_dim(small_g[0:4].reshape(2, 2, d), s * (d // 4), d // 4, axis=2)
    res['norm_g'] = upd(norm_g, [g_ng], m_norm_g, v_norm_g, 'adam_norm_g')
    g_qk = jnp.stack([small_g[4, 0:64], small_g[5, 0:64]]).reshape(1, 2, 64)
    res['attn_qk_norm_g'] = upd(attn_qk_norm_g, [g_qk], m_attn_qk_norm_g, v_attn_qk_norm_g, 'adam_qk_gain')
    res['attn_sink'] = upd(attn_sink, [small_g[7, 0:8].reshape(1, 8)], m_attn_sink, v_attn_sink, 'adam_sink')
    res['hgrn_out_norm_g'] = upd(hgrn_out_norm_g, [small_g[6, 0:128].reshape(1, 128)], m_hgrn_out_norm_g, v_hgrn_out_norm_g, 'adam_head_gain')
    res['hgrn_lb'] = upd(hgrn_lb, [glb[0:2, 0:hgrn_lb.shape[1]]], m_hgrn_lb, v_hgrn_lb, 'adam_hgrn_lb')
    res['odd_w_in'] = upd(odd_w_in, [full['odd_in']], m_odd_w_in, v_odd_w_in, 'adam_odd_in')
    res['odd_w_out'] = upd(odd_w_out, [full['odd_out']], m_odd_w_out, v_odd_w_out, 'adam_odd_out')
    full.update(reducer.finish('mid', res['odd_w_in'][1]))
    g_ffn_in = jnp.concatenate([full['ffn_in0'], full['ffn_in1']], axis=0)
    g_ffn_out = jnp.concatenate([full['ffn_out0'], full['ffn_out1']], axis=0)
    res['ffn_w_in'] = upd(ffn_w_in, [g_ffn_in], m_ffn_w_in, v_ffn_w_in, 'adam_ffn_in')
    res['ffn_w_out'] = upd(ffn_w_out, [g_ffn_out], m_ffn_w_out, v_ffn_w_out, 'adam_ffn_out')
    full.update(reducer.finish('late', res['ffn_w_in'][1]))
    res['even_w_in'] = upd(even_w_in, [full['even_in']], m_even_w_in, v_even_w_in, 'adam_even_in')
    res['even_w_out'] = upd(even_w_out, [full['even_out']], m_even_w_out, v_even_w_out, 'adam_even_out')

    order = ['c_ctx', 'mod_w', 'mod_b', 'norm_g', 'ffn_w_in', 'ffn_w_out', 'even_w_in', 'even_w_out',
             'attn_qk_norm_g', 'attn_sink', 'hgrn_out_norm_g', 'hgrn_lb', 'odd_w_in', 'odd_w_out']
    outs = [loss, grad_x]
    for k in range(4):
        outs += [res[nm][k] for nm in order]
    return tuple(outs)
```

```python
import functools
import math

import numpy as np
import jax
import jax.numpy as jnp
from jax import lax
from jax.experimental import pallas as pl
from jax.experimental.pallas import tpu as pltpu

F32 = jnp.float32
BF16 = jnp.bfloat16
EPS = 1e-6
TM = 256
CHUNK = 64
QB = 256
WINDOW = 128
NEG = -1e30
MESH = pl.DeviceIdType.MESH

ADAM_LR, ADAM_B1, ADAM_B2, ADAM_EPS, ADAM_WD, ADAM_STEP = 0.001, 0.9, 0.999, 1e-08, 0.01, 10


def _pcall(body, **kw):
    return pl.pallas_call(body, **kw)


def _pick(n, cap):
    best = None
    for m in range(128, min(n, cap) + 1, 128):
        if n % m == 0:
            best = m
    assert best is not None, (n, cap)
    return best


def _bf(x):
    return x.astype(BF16)


def _dot(a, b):
    return jnp.dot(_bf(a), _bf(b), preferred_element_type=F32)


def _dot_nt(a, b):
    return lax.dot_general(_bf(a), _bf(b), (((1,), (1,)), ((), ())), preferred_element_type=F32)


def _dot_tn(a, b):
    return lax.dot_general(_bf(a), _bf(b), (((0,), (0,)), ((), ())), preferred_element_type=F32)


def _dot_exact(a, b):
    return jnp.dot(a, b, preferred_element_type=F32, precision=lax.Precision.HIGHEST)


def _sigmoid(x):
    return 1.0 / (1.0 + jnp.exp(-x))


def _iota(shape, dim):
    return lax.broadcasted_iota(jnp.int32, shape, dim)


def _parts(a):
    parts = list(a) if isinstance(a, (list, tuple)) else [a]
    widths = [p.shape[1] for p in parts]
    return parts, widths, [sum(widths[:i]) for i in range(len(parts))]


def _mm_nn(a, b, *, lead=None, out_dtype=F32, name):
    parts, widths, offs = _parts(a)
    m, k = parts[0].shape[0], sum(widths)
    n = b.shape[-1]
    bm = 1408 if (m % 1408 == 0 and k <= 1024) else (768 if m % 768 == 0 else TM)
    bn = _pick(n, 1024) if n % 512 == 0 else _pick(n, 1664)

    def body(*refs):
        b_ref, o_ref = refs[-2], refs[-1]
        acc = None
        for p_ref, w, off in zip(refs, widths, offs):
            term = _dot(p_ref[...], b_ref[off:off + w, :])
            acc = term if acc is None else acc + term
        o_ref[...] = acc.astype(o_ref.dtype)

    if lead is None:
        b_spec = pl.BlockSpec((k, bn), lambda i, j: (0, j))
    else:
        b_spec = pl.BlockSpec((None, k, bn), lambda i, j: (lead, 0, j))
    return _pcall(
        body, name=name, grid=(m // bm, n // bn),
        in_specs=[pl.BlockSpec((bm, w), lambda i, j: (i, 0)) for w in widths] + [b_spec],
        out_specs=pl.BlockSpec((bm, bn), lambda i, j: (i, j)),
        out_shape=jax.ShapeDtypeStruct((m, n), out_dtype),
    )(*parts, b)


def _mm_nt(a, b, *, lead=None, name):
    parts, widths, offs = _parts(a)
    m, n = parts[0].shape[0], sum(widths)
    k = b.shape[-2]
    bm = 1408 if (m % 1408 == 0 and n <= 1024) else (768 if m % 768 == 0 else TM)
    bk = _pick(k, 1024 if n <= 2048 else 512)

    def body(*refs):
        b_ref, o_ref = refs[-2], refs[-1]
        acc = None
        for p_ref, w, off in zip(refs, widths, offs):
            term = _dot_nt(p_ref[...], b_ref[:, off:off + w])
            acc = term if acc is None else acc + term
        o_ref[...] = acc

    if lead is None:
        b_spec = pl.BlockSpec((bk, n), lambda i, j: (j, 0))
    else:
        b_spec = pl.BlockSpec((None, bk, n), lambda i, j: (lead, j, 0))
    return _pcall(
        body, name=name, grid=(m // bm, k // bk),
        in_specs=[pl.BlockSpec((bm, w), lambda i, j: (i, 0)) for w in widths] + [b_spec],
        out_specs=pl.BlockSpec((bm, bk), lambda i, j: (i, j)),
        out_shape=jax.ShapeDtypeStruct((m, k), F32),
    )(*parts, b)


def _mm_tn(a, b, *, name):
    a_parts, a_w, a_off = _parts(a)
    b_parts, b_w, b_off = _parts(b)
    t, k, n = a_parts[0].shape[0], sum(a_w), sum(b_w)
    bt = 1408 if t % 1408 == 0 else (768 if t % 768 == 0 else TM)
    bk = _pick(k, 1536) if len(a_parts) == 1 else math.gcd(*a_w)
    if len(b_parts) == 1:
        bn = _pick(n, 1024) if n % 1024 == 0 or n < 1664 else _pick(n, 1664)
    else:
        bn = math.gcd(*b_w)
    na, nbp = len(a_parts), len(b_parts)

    def block_range(off, w, blk):
        return off // blk, w // blk

    def body(*refs):
        a_refs, b_refs, o_ref = refs[:na], refs[na:na + nbp], refs[-1]
        i, j = pl.program_id(0), pl.program_id(1)

        @pl.when(pl.program_id(2) == 0)
        def _():
            o_ref[...] = jnp.zeros_like(o_ref)

        def add(a_ref, b_ref):
            o_ref[...] += _dot_tn(a_ref[...], b_ref[...])

        for pa in range(na):
            sa, ca = block_range(a_off[pa], a_w[pa], bk)
            for pb in range(nbp):
                sb, cb = block_range(b_off[pb], b_w[pb], bn)
                if na == 1 and nbp == 1:
                    add(a_refs[0], b_refs[0])
                else:
                    pl.when((i >= sa) & (i < sa + ca) & (j >= sb) & (j < sb + cb))(
                        functools.partial(add, a_refs[pa], b_refs[pb]))

    def spec(off, w, blk, axis):
        s0, cnt = block_range(off, w, blk)

        def index(i, j, s):
            g = i if axis == 0 else j
            inside = (g >= s0) & (g < s0 + cnt)
            return (jnp.where(inside, s, 0), jnp.clip(g - s0, 0, cnt - 1))

        return pl.BlockSpec((bt, blk), index)

    return _pcall(
        body, name=name, grid=(k // bk, n // bn, t // bt),
        in_specs=[spec(o, w, bk, 0) for o, w in zip(a_off, a_w)] + [spec(o, w, bn, 1) for o, w in zip(b_off, b_w)],
        out_specs=pl.BlockSpec((bk, bn), lambda i, j, s: (i, j)),
        out_shape=jax.ShapeDtypeStruct((k, n), F32),
    )(*a_parts, *b_parts)


def _mod_row(mods_ref, lat, idx):
    return jnp.where(lat, mods_ref[idx + 6:idx + 7, :], mods_ref[idx:idx + 1, :])


def _row_step(t):
    return 768 if t % 768 == 0 else TM


def _row_fwd(x, mods, *, y=None, gate=None, g=None, shift=None, scale=None, name):
    t, d = x.shape
    has_y, has_n = y is not None, g is not None
    rt = _row_step(t)

    def body(*refs):
        refs = list(refs)
        x_ref, mods_ref = refs[0], refs[1]
        pos = 2
        if has_y:
            y_ref = refs[pos]; pos += 1
        if has_n:
            g_ref = refs[pos]; pos += 1
        outs = refs[pos:]
        for sub in range(rt // TM):
            rows = slice(sub * TM, (sub + 1) * TM)
            lat = pl.program_id(0) * (rt // TM) + sub > 0
            x1 = x_ref[rows, :]
            o = 0
            if has_y:
                x1 = x1 + _mod_row(mods_ref, lat, gate) * y_ref[rows, :]
                outs[o][rows, :] = x1; o += 1
            if has_n:
                rs = lax.rsqrt(jnp.mean(x1 * x1, axis=-1, keepdims=True) + EPS)
                hn = x1 * rs * g_ref[...]
                h = hn * (1.0 + _mod_row(mods_ref, lat, scale)) + _mod_row(mods_ref, lat, shift)
                outs[o][rows, :] = h.astype(BF16)

    row = pl.BlockSpec((rt, d), lambda i: (i, 0))
    ins, specs = [x, mods], [row, pl.BlockSpec(mods.shape, lambda i: (0, 0))]
    if has_y:
        ins.append(y); specs.append(row)
    if has_n:
        ins.append(g.reshape(1, d)); specs.append(pl.BlockSpec((1, d), lambda i: (0, 0)))
    out_shape, out_specs = [], []
    if has_y:
        out_shape.append(jax.ShapeDtypeStruct((t, d), F32)); out_specs.append(row)
    if has_n:
        out_shape.append(jax.ShapeDtypeStruct((t, d), BF16)); out_specs.append(row)
    res = _pcall(body, name=name, grid=(t // rt,), in_specs=specs, out_specs=out_specs,
                 out_shape=out_shape)(*ins)
    return res


def _acc_row(ref, r, val):
    ref[r:r + 1, :] += val


def _row_final(x, z, mods, target, *, gate, name):
    t, d = x.shape
    rt = _row_step(t)
    nsub = rt // TM

    def body(*refs):
        x_ref, mods_ref, z_ref = refs[:3]
        t_refs = refs[3:3 + nsub]
        loss_ref, dx_ref, dz_ref, sums_ref = refs[3 + nsub:]
        i = pl.program_id(0)

        @pl.when(i == 0)
        def _():
            loss_ref[...] = jnp.zeros_like(loss_ref)
            sums_ref[...] = jnp.zeros_like(sums_ref)

        for sub in range(nsub):
            rows = slice(sub * TM, (sub + 1) * TM)
            lat = i * nsub + sub > 0
            gt = _mod_row(mods_ref, lat, gate)
            zz = z_ref[rows, :]
            yv = x_ref[rows, :] + gt * zz
            keep = jnp.where(lat, 1.0, 0.0).astype(F32)
            diff = (yv - t_refs[sub][...]) * keep
            part = jnp.sum(jnp.sum(diff * diff, axis=0, keepdims=True), axis=1, keepdims=True)
            loss_ref[...] += part * (0.5 / d)
            dy = diff * (1.0 / d)
            dx_ref[rows, :] = dy
            dz_ref[rows, :] = (gt * dy).astype(BF16)
            _acc_row(sums_ref, 6, jnp.sum(dy * zz, axis=0, keepdims=True))

    row = pl.BlockSpec((rt, d), lambda i: (i, 0))
    tgt = [pl.BlockSpec((TM, d), lambda i, sub=sub: (jnp.maximum(i * nsub + sub - 1, 0), 0)) for sub in range(nsub)]
    return _pcall(
        body, name=name, grid=(t // rt,),
        in_specs=[row, pl.BlockSpec(mods.shape, lambda i: (0, 0)), row] + tgt,
        out_specs=[pl.BlockSpec((8, 128), lambda i: (0, 0)), row, row,
                   pl.BlockSpec((8, d), lambda i: (0, 0))],
        out_shape=[jax.ShapeDtypeStruct((8, 128), F32), jax.ShapeDtypeStruct((t, d), F32),
                   jax.ShapeDtypeStruct((t, d), BF16), jax.ShapeDtypeStruct((8, d), F32)],
    )(x, mods, z, *([target] * nsub))


def _row_bwd(xn, dxo, dh, mods, g, *, shift, scale, y=None, gate=None, latent_only=False, name):
    t, d = xn.shape
    has_y = y is not None

    def body(*refs):
        refs = list(refs)
        x_ref, dxo_ref, dh_ref, mods_ref, g_ref = refs[:5]
        pos = 5
        if has_y:
            y_ref = refs[pos]; pos += 1
        dx_ref = refs[pos]; pos += 1
        if has_y:
            dy_ref = refs[pos]; pos += 1
        sums_ref = refs[pos]
        i = pl.program_id(0)

        @pl.when(i == 0)
        def _():
            sums_ref[...] = jnp.zeros_like(sums_ref)

        def add_sums(vals, base):
            for r, v in enumerate(vals):
                if v is not None:
                    _acc_row(sums_ref, base + r, v)

        gv = g_ref[...]
        for sub in range(rt // TM):
            rows = slice(sub * TM, (sub + 1) * TM)
            lat = i * (rt // TM) + sub > 0
            x1 = x_ref[rows, :]
            rs = lax.rsqrt(jnp.mean(x1 * x1, axis=-1, keepdims=True) + EPS)
            xh = x1 * rs
            dhv = dh_ref[rows, :]
            dn = dhv * (1.0 + _mod_row(mods_ref, lat, scale))
            dxh = dn * gv
            dx = dxo_ref[rows, :] + rs * (dxh - xh * jnp.mean(dxh * xh, axis=-1, keepdims=True))
            dx_ref[rows, :] = dx
            vals = [jnp.sum(dhv, axis=0, keepdims=True),
                    jnp.sum(dhv * (xh * gv), axis=0, keepdims=True),
                    None,
                    jnp.sum(dn * xh, axis=0, keepdims=True)]
            if has_y:
                dy_ref[rows, :] = (_mod_row(mods_ref, lat, gate) * dx).astype(BF16)
                vals[2] = jnp.sum(dx * y_ref[rows, :], axis=0, keepdims=True)
            if sub == 0:
                pl.when(i == 0)(functools.partial(add_sums, vals, 0))
                pl.when(i > 0)(functools.partial(add_sums, vals, 4))
            else:
                add_sums(vals, 4)

    rt = TM if latent_only else _row_step(t)
    row = pl.BlockSpec((rt, d), lambda i: (i, 0))
    ins = [xn, dxo, dh, mods, g.reshape(1, d)]
    specs = [row, row, row, pl.BlockSpec(mods.shape, lambda i: (0, 0)), pl.BlockSpec((1, d), lambda i: (0, 0))]
    if latent_only:
        out_shape = [jax.ShapeDtypeStruct((t - TM, d), F32)]
        out_specs = [pl.BlockSpec((TM, d), lambda i: (jnp.maximum(i - 1, 0), 0))]
    else:
        out_shape, out_specs = [jax.ShapeDtypeStruct((t, d), F32)], [row]
    if has_y:
        ins.append(y); specs.append(row)
        out_shape.append(jax.ShapeDtypeStruct((t, d), BF16)); out_specs.append(row)
    out_shape.append(jax.ShapeDtypeStruct((8, d), F32))
    out_specs.append(pl.BlockSpec((8, d), lambda i: (0, 0)))
    return _pcall(body, name=name, grid=(t // rt,), in_specs=specs, out_specs=out_specs,
                  out_shape=out_shape)(*ins)


FFN_BK = 1408


FFN_SUB = 256


def _ffn_order(n2):
    nb = n2 // (2 * FFN_BK)
    return [h * nb + j for j in range(nb) for h in (0, 1)]


def _ffn_interleave(w):
    return jnp.concatenate([w[..., b * FFN_BK:(b + 1) * FFN_BK] for b in _ffn_order(w.shape[-1])], axis=-1)


def _ffn_deinterleave(w):
    order = _ffn_order(w.shape[-1])
    return jnp.concatenate([w[..., order.index(b) * FFN_BK:(order.index(b) + 1) * FFN_BK]
                            for b in range(len(order))], axis=-1)


def _big_tile(t):
    return 768 if t % 768 == 0 else TM


def _ffn_in(h, w, *, lead, name):
    t, d = h.shape
    n2 = w.shape[-1]
    bm, bk = _big_tile(t), FFN_BK

    def body(h_ref, w_ref, u_ref, a_ref):
        hb = h_ref[...]
        for c0 in range(0, bk, FFN_SUB):
            c1 = min(c0 + FFN_SUB, bk)
            ug = _dot(hb, w_ref[:, c0:c1]).astype(BF16)
            uu = _dot(hb, w_ref[:, bk + c0:bk + c1]).astype(BF16)
            u_ref[:, c0:c1] = ug
            u_ref[:, bk + c0:bk + c1] = uu
            gv, up = ug.astype(F32), uu.astype(F32)
            a_ref[:, c0:c1] = (gv * _sigmoid(gv) * up).astype(BF16)

    return _pcall(
        body, name=name, grid=(t // bm, n2 // (2 * bk)),
        in_specs=[pl.BlockSpec((bm, d), lambda i, j: (i, 0)),
                  pl.BlockSpec((None, d, 2 * bk), lambda i, j: (lead, 0, j))],
        out_specs=[pl.BlockSpec((bm, 2 * bk), lambda i, j: (i, j)), pl.BlockSpec((bm, bk), lambda i, j: (i, j))],
        out_shape=[jax.ShapeDtypeStruct((t, n2), BF16), jax.ShapeDtypeStruct((t, n2 // 2), BF16)],
    )(h, w)


def _ffn_dx(dz, w_out, u, *, lead, name):
    t, d = dz.shape
    n2 = u.shape[1]
    bm, bk = _big_tile(t), FFN_BK

    def body(dz_ref, w_ref, u_ref, du_ref):
        dzb = dz_ref[...]
        for c0 in range(0, bk, FFN_SUB):
            c1 = min(c0 + FFN_SUB, bk)
            da = _dot_nt(dzb, w_ref[c0:c1, :])
            gv, up = u_ref[:, c0:c1].astype(F32), u_ref[:, bk + c0:bk + c1].astype(F32)
            s = _sigmoid(gv)
            du_ref[:, c0:c1] = (da * up * (s * (1.0 + gv * (1.0 - s)))).astype(BF16)
            du_ref[:, bk + c0:bk + c1] = (da * gv * s).astype(BF16)

    ublk = pl.BlockSpec((bm, 2 * bk), lambda i, j: (i, j))
    return _pcall(
        body, name=name, grid=(t // bm, n2 // (2 * bk)),
        in_specs=[pl.BlockSpec((bm, d), lambda i, j: (i, 0)),
                  pl.BlockSpec((None, bk, d), lambda i, j: (lead, j, 0)), ublk],
        out_specs=ublk, out_shape=jax.ShapeDtypeStruct((t, n2), BF16),
    )(dz, w_out, u)


def _lane(shape):
    return _iota(shape, len(shape) - 1)


def _pair_norm(x, g):
    lo = _lane(x.shape) < 64
    x2 = x * x
    s_lo = jnp.sum(jnp.where(lo, x2, 0.0), axis=-1, keepdims=True)
    s_hi = jnp.sum(jnp.where(lo, 0.0, x2), axis=-1, keepdims=True)
    rs = lax.rsqrt(jnp.where(lo, s_lo, s_hi) * (1.0 / 64) + EPS)
    return x * rs, rs


def _pair_mean(v):
    lo = _lane(v.shape) < 64
    s_lo = jnp.sum(jnp.where(lo, v, 0.0), axis=-1, keepdims=True)
    s_hi = jnp.sum(jnp.where(lo, 0.0, v), axis=-1, keepdims=True)
    return jnp.where(lo, s_lo, s_hi) * (1.0 / 64)


def _rot64(x):
    r1 = pltpu.roll(x, 32, 1)
    r2 = pltpu.roll(x, 96, 1)
    even = ((_lane(x.shape) >> 5) & 1) == 0
    return jnp.where(even, -r2, r1)


def _rope64(x, cos, sin):
    return x * cos + _rot64(x) * sin


def _rope64_t(d, cos, sin):
    return d * cos - _rot64(d * sin)


def _kprep_fwd(p, gk, cos, sin, *, name):
    t = p.shape[0]

    def body(k_ref, g_ref, c_ref, s_ref, o_ref):
        xh, _ = _pair_norm(k_ref[...], None)
        o_ref[...] = _rope64(xh * g_ref[...], c_ref[...], s_ref[...])

    blk = pl.BlockSpec((TM, 128), lambda i: (i, 0))
    return _pcall(
        body, name=name, grid=(t // TM,),
        in_specs=[pl.BlockSpec((TM, 128), lambda i: (i, 4)), pl.BlockSpec((1, 128), lambda i: (0, 0)), blk, blk],
        out_specs=blk, out_shape=jax.ShapeDtypeStruct((t, 128), F32),
    )(p, gk, cos, sin)


def _kprep_bwd(p, gk, cos, sin, dkp, dv, *, name):
    t = p.shape[0]

    def body(k_ref, g_ref, c_ref, s_ref, dkp_ref, dv_ref, o_ref, dg_ref):
        @pl.when(pl.program_id(0) == 0)
        def _():
            dg_ref[...] = jnp.zeros_like(dg_ref)
        xh, rs = _pair_norm(k_ref[...], None)
        dn = _rope64_t(dkp_ref[...], c_ref[...], s_ref[...])
        _acc_row(dg_ref, 0, jnp.sum(dn * xh, axis=0, keepdims=True))
        dxh = dn * g_ref[...]
        o_ref[:, 0:128] = (rs * (dxh - xh * _pair_mean(dxh * xh))).astype(BF16)
        o_ref[:, 128:256] = dv_ref[...].astype(BF16)

    blk = pl.BlockSpec((TM, 128), lambda i: (i, 0))
    return _pcall(
        body, name=name, grid=(t // TM,),
        in_specs=[pl.BlockSpec((TM, 128), lambda i: (i, 4)), pl.BlockSpec((1, 128), lambda i: (0, 0)), blk, blk, blk, blk],
        out_specs=[pl.BlockSpec((TM, 256), lambda i: (i, 0)), pl.BlockSpec((8, 128), lambda i: (0, 0))],
        out_shape=[jax.ShapeDtypeStruct((t, 256), BF16), jax.ShapeDtypeStruct((8, 128), F32)],
    )(p, gk, cos, sin, dkp, dv)


def _attn_common(i, t, lc, kp_ref, v_ref):
    span = QB + 2 * WINDOW
    start = pl.multiple_of(jnp.clip(i * QB - WINDOW, lc, t - span), WINDOW)
    kall = jnp.concatenate([kp_ref[0:lc, :], kp_ref[pl.ds(start, span), :]], axis=0)
    vall = jnp.concatenate([v_ref[0:lc, :], v_ref[pl.ds(start, span), :]], axis=0)
    nk = lc + span
    col = _iota((QB, nk), 1)
    krow = jnp.where(col < lc, col, start + col - lc)
    qrow = i * QB + _iota((QB, nk), 0)
    valid = (col < lc) | ((qrow >= lc) & (krow >= lc) & (jnp.abs(krow - qrow) <= WINDOW))
    lo = _lane(kall.shape) < 64
    kroll, vroll = pltpu.roll(kall, 64, 1), pltpu.roll(vall, 64, 1)
    zero = jnp.zeros_like(kall)
    kvar = [[_bf(jnp.where(lo, kall, zero)), _bf(jnp.where(lo, zero, kroll))],
            [_bf(jnp.where(lo, kroll, zero)), _bf(jnp.where(lo, zero, kall))]]
    vvar = [[_bf(jnp.where(lo, vall, zero)), _bf(jnp.where(lo, zero, vroll))],
            [_bf(jnp.where(lo, vroll, zero)), _bf(jnp.where(lo, zero, vall))]]
    return start, valid, kvar, vvar


def _softmax_sink(s, valid, snk):
    s = jnp.where(valid, s, NEG)
    m = jnp.maximum(jnp.max(s, axis=-1, keepdims=True), snk)
    e = jnp.exp(s - m)
    es = jnp.exp(snk - m)
    inv = 1.0 / (jnp.sum(e, axis=-1, keepdims=True) + es)
    return e * inv, es * inv


def _attn_fwd(p, kp, gq, sink, cos, sin, *, lc, name):
    t = p.shape[0]
    scale = 64 ** -0.5

    def body(q_ref, kp_ref, v_ref, g_ref, sink_ref, c_ref, s_ref, o_ref):
        i = pl.program_id(0)
        _, valid, kvar, vvar = _attn_common(i, t, lc, kp_ref, v_ref)
        cosv, sinv, gv = c_ref[...], s_ref[...], g_ref[...]
        for j in range(4):
            xh, _ = _pair_norm(q_ref[:, 128 * j:128 * j + 128], None)
            q2 = _bf(_rope64(xh * gv, cosv, sinv) * scale)
            acc = jnp.zeros((QB, 128), F32)
            for half in range(2):
                s = _dot_nt(q2, kvar[j // 2][half])
                pr, _ = _softmax_sink(s, valid, sink_ref[2 * j + half])
                acc = acc + _dot(pr, vvar[j // 2][half])
            o_ref[:, 128 * j:128 * j + 128] = acc.astype(BF16)

    qblk = pl.BlockSpec((QB, 128), lambda i: (i, 0))
    return _pcall(
        body, name=name, grid=(t // QB,),
        in_specs=[pl.BlockSpec((QB, 512), lambda i: (i, 0)),
                  pl.BlockSpec((t, 128), lambda i: (0, 0)),
                  pl.BlockSpec((t, 128), lambda i: (0, 5)),
                  pl.BlockSpec((1, 128), lambda i: (0, 0)),
                  pl.BlockSpec(memory_space=pltpu.SMEM), qblk, qblk],
        out_specs=pl.BlockSpec((QB, 512), lambda i: (i, 0)),
        out_shape=jax.ShapeDtypeStruct((t, 512), BF16),
    )(p, kp, p, gq, sink, cos, sin)


def _attn_bwd(p, kp, gq, sink, cos, sin, dmix, *, lc, name):
    t = p.shape[0]
    scale = 64 ** -0.5
    span = QB + 2 * WINDOW

    def body(q_ref, kp_ref, v_ref, g_ref, sink_ref, c_ref, s_ref, do_ref,
             dq_ref, dk_ref, dv_ref, dg_ref, dsink_ref):
        i = pl.program_id(0)

        @pl.when(i == 0)
        def _():
            dk_ref[...] = jnp.zeros_like(dk_ref)
            dv_ref[...] = jnp.zeros_like(dv_ref)
            dg_ref[...] = jnp.zeros_like(dg_ref)
            dsink_ref[...] = jnp.zeros_like(dsink_ref)

        start, valid, kvar, vvar = _attn_common(i, t, lc, kp_ref, v_ref)
        cosv, sinv, gv = c_ref[...], s_ref[...], g_ref[...]
        nk = lc + span
        dkt = [jnp.zeros((64, nk), F32), jnp.zeros((64, nk), F32)]
        dvt = [jnp.zeros((64, nk), F32), jnp.zeros((64, nk), F32)]
        for j in range(4):
            kvh = j // 2
            xh, rs = _pair_norm(q_ref[:, 128 * j:128 * j + 128], None)
            q2 = _bf(_rope64(xh * gv, cosv, sinv) * scale)
            do2 = _bf(do_ref[:, 128 * j:128 * j + 128])
            dq2 = jnp.zeros((QB, 128), F32)
            for half in range(2):
                s = _dot_nt(q2, kvar[kvh][half])
                pr, ps = _softmax_sink(s, valid, sink_ref[2 * j + half])
                dp = _dot_nt(do2, vvar[kvh][half])
                delta = jnp.sum(pr * dp, axis=-1, keepdims=True)
                ds = pr * (dp - delta)
                dsk = jnp.sum(jnp.sum(-ps * delta, axis=0, keepdims=True), axis=1, keepdims=True)
                _acc_row(dsink_ref, 2 * j + half, jnp.broadcast_to(dsk, (1, 128)))
                dq2 = dq2 + _dot(ds, kvar[kvh][half])
                hrows = slice(64 * half, 64 * half + 64)
                dkt[kvh] = dkt[kvh] + _dot_tn(q2, ds)[hrows]
                dvt[kvh] = dvt[kvh] + _dot_tn(do2, pr)[hrows]
            dn = _rope64_t(dq2 * scale, cosv, sinv)
            _acc_row(dg_ref, 0, jnp.sum(dn * xh, axis=0, keepdims=True))
            dxh = dn * gv
            dq_ref[:, 128 * j:128 * j + 128] = (rs * (dxh - xh * _pair_mean(dxh * xh))).astype(BF16)
        dk_all = jnp.concatenate(dkt, axis=0).T
        dv_all = jnp.concatenate(dvt, axis=0).T
        dk_ref[0:lc, :] += dk_all[0:lc]
        dv_ref[0:lc, :] += dv_all[0:lc]
        dk_ref[pl.ds(start, span), :] += dk_all[lc:nk]
        dv_ref[pl.ds(start, span), :] += dv_all[lc:nk]

    qblk = pl.BlockSpec((QB, 128), lambda i: (i, 0))
    full = pl.BlockSpec((t, 128), lambda i: (0, 0))
    small = pl.BlockSpec((8, 128), lambda i: (0, 0))
    return _pcall(
        body, name=name, grid=(t // QB,),
        in_specs=[pl.BlockSpec((QB, 512), lambda i: (i, 0)), full,
                  pl.BlockSpec((t, 128), lambda i: (0, 5)),
                  pl.BlockSpec((1, 128), lambda i: (0, 0)),
                  pl.BlockSpec(memory_space=pltpu.SMEM), qblk, qblk,
                  pl.BlockSpec((QB, 512), lambda i: (i, 0))],
        out_specs=[pl.BlockSpec((QB, 512), lambda i: (i, 0)), full, full, small, small],
        out_shape=[jax.ShapeDtypeStruct((t, 512), BF16), jax.ShapeDtypeStruct((t, 128), F32),
                   jax.ShapeDtypeStruct((t, 128), F32), jax.ShapeDtypeStruct((8, 128), F32),
                   jax.ShapeDtypeStruct((8, 128), F32)],
    )(p, kp, p, gq, sink, cos, sin, dmix)


def _tri(rev):
    r, c = _iota((CHUNK, CHUNK), 0), _iota((CHUNK, CHUNK), 1)
    return (c >= r) if rev else (c <= r)


def _blk_map(nb, rev, backward):
    if not rev:
        return (lambda n: nb - 1 - n) if backward else (lambda n: n)
    if backward:
        return lambda n: jnp.where(n < nb - 1, n + 1, 0)
    return lambda n: jnp.where(n == 0, 0, nb - n)


def _chunk_order(rev, backward, nc=TM // CHUNK):
    order = list(range(nc))
    return order[::-1] if (rev != backward) else order


def _hgrn_gates(qraw, fraw, lb):
    sq = _sigmoid(qraw)
    sf = _sigmoid(fraw)
    f = lb + (1.0 - lb) * sf
    return qraw * sq, 1.0 - f, jnp.log(f), sq, sf, f


HGRN_HP = 4


def _chunk_cumsum(x, rev):
    n = x.shape[0]
    pos = _iota(x.shape, 0) & (CHUNK - 1)
    s = 1
    while s < CHUNK:
        if rev:
            x = x + jnp.where(pos < CHUNK - s, pltpu.roll(x, n - s, 0), 0.0)
        else:
            x = x + jnp.where(pos >= s, pltpu.roll(x, s, 0), 0.0)
        s *= 2
    return x


def _block_terms(lf, rev):
    b = _chunk_cumsum(lf, rev)
    mid, last = (CHUNK // 2 - 1, 0) if rev else (CHUNK // 2, CHUNK - 1)

    def chunk_row(off):
        return jnp.concatenate([jnp.broadcast_to(b[c * CHUNK + off:c * CHUNK + off + 1, :], (CHUNK, b.shape[1]))
                                for c in range(TM // CHUNK)], axis=0)

    r, bl = chunk_row(mid), chunk_row(last)
    return _tri(rev), jnp.exp(b - r), jnp.exp(r - b), jnp.exp(b), jnp.exp(bl - b), jnp.exp(bl)


def _headnorm_apply(o, gv, gain):
    n = o * lax.rsqrt(jnp.mean(o * o, axis=-1, keepdims=True) + EPS)
    if gain is not None:
        n = n * gain
    return (n * (gv * _sigmoid(gv))).astype(BF16)


def _headnorm_grad(o, gv, dy, gain):
    rs = lax.rsqrt(jnp.mean(o * o, axis=-1, keepdims=True) + EPS)
    xh = o * rs
    n = xh * gain if gain is not None else xh
    sg = _sigmoid(gv)
    dn = dy * (gv * sg)
    dg = (dy * n * (sg * (1.0 + gv * (1.0 - sg)))).astype(BF16)
    dgain = jnp.sum(dn * xh, axis=0, keepdims=True)
    dxh = dn * gain if gain is not None else dn
    return rs * (dxh - xh * jnp.mean(dxh * xh, axis=-1, keepdims=True)), dg, dgain


def _hgrn_cols(bmap, n2, c0):
    return [pl.BlockSpec((TM, 256), lambda h, n, b=b: (bmap(n), c0 // 2 + h * n2 + b)) for b in range(n2)]


def _head_cols(refs, hh):
    return refs[hh // 2][:, 128 * (hh % 2):128 * (hh % 2) + 128]


def _hgrn_fwd(p, lb, *, rev, name, ofw=None, gain=None):
    t = p.shape[0]
    nb, nc = t // TM, TM // CHUNK
    bmap = _blk_map(nb, rev, False)
    fcol = 14 if rev else 10
    fused = ofw is not None

    n2 = HGRN_HP // 2

    def body(*refs):
        q_refs, f_refs, v_refs, lb_ref = refs[:n2], refs[n2:2 * n2], refs[2 * n2:3 * n2], refs[3 * n2]
        rest = refs[3 * n2 + 1:]
        if fused:
            ofw_ref, g_refs, gain_ref = rest[0], rest[1:1 + n2], rest[1 + n2]
            o_ref, sh_ref, mix_ref, st = rest[2 + n2:]
        else:
            o_ref, sh_ref, st = rest

        @pl.when(pl.program_id(1) == 0)
        def _():
            st[...] = jnp.zeros_like(st)
        for hh in range(HGRN_HP):
            ln = slice(128 * hh, 128 * hh + 128)
            q, k, lf, _, _, _ = _hgrn_gates(_head_cols(q_refs, hh), _head_cols(f_refs, hh), lb_ref[:, ln])
            tri, eq, ek, ei, eki, eb = _block_terms(lf, rev)
            qe, ke, qi, ki, vb = _bf(q * eq), _bf(k * ek), _bf(q * ei), _bf(k * eki), _bf(_head_cols(v_refs, hh))
            intra = []
            for cc in range(nc):
                rows = slice(cc * CHUNK, (cc + 1) * CHUNK)
                a = jnp.where(tri, _dot_nt(qe[rows], ke[rows]), 0.0)
                intra.append(_dot(a, vb[rows]))
            s = st[hh]
            for cc in _chunk_order(rev, False):
                rows = slice(cc * CHUNK, (cc + 1) * CHUNK)
                sh_ref[hh, cc] = s.astype(sh_ref.dtype)
                o_ref[rows, ln] = intra[cc] + _dot_nt(qi[rows], s)
                s = s * eb[cc * CHUNK:cc * CHUNK + 1, :] + _dot_tn(vb[rows], ki[rows])
            st[hh] = s
            if fused:
                osum = o_ref[:, ln] + ofw_ref[:, ln]
                o_ref[:, ln] = osum
                mix_ref[:, ln] = _headnorm_apply(osum, _head_cols(g_refs, hh), gain_ref[...])

    hp, wd = HGRN_HP, 128 * HGRN_HP
    col = functools.partial(_hgrn_cols, bmap, n2)
    oblk = pl.BlockSpec((TM, wd), lambda h, n: (bmap(n), h))
    ins = [p] * (3 * n2) + [lb]
    specs = col(6) + col(fcol) + col(18) + [pl.BlockSpec((1, wd), lambda h, n: (0, h))]
    out_specs = [oblk, pl.BlockSpec((hp, nc, 128, 128), lambda h, n: (h, bmap(n), 0, 0))]
    out_shape = [jax.ShapeDtypeStruct((t, 512), F32), jax.ShapeDtypeStruct((4, t // CHUNK, 128, 128), BF16)]
    if fused:
        ins += [ofw] + [p] * n2 + [gain]
        specs += [oblk] + col(22) + [pl.BlockSpec((1, 128), lambda h, n: (0, 0))]
        out_specs.append(oblk)
        out_shape.append(jax.ShapeDtypeStruct((t, 512), BF16))
    return _pcall(body, name=name, grid=(4 // hp, nb), in_specs=specs, out_specs=out_specs, out_shape=out_shape,
                  scratch_shapes=[pltpu.VMEM((hp, 128, 128), F32)])(*ins)


def _hgrn_bwd(p, lb, sh, do, prev, *, rev, name, head=None):
    t = p.shape[0]
    nb, nc = t // TM, TM // CHUNK
    bmap = _blk_map(nb, rev, True)
    fcol = 14 if rev else 10
    has_prev = prev is not None
    odt = BF16
    fused = head is not None

    n2 = HGRN_HP // 2

    def body(*refs):
        refs = list(refs)
        q_refs, f_refs, v_refs = refs[:n2], refs[n2:2 * n2], refs[2 * n2:3 * n2]
        lb_ref, sh_ref = refs[3 * n2], refs[3 * n2 + 1]
        pos = 3 * n2 + 2
        if fused:
            osum_ref, g_refs, dmix_ref, gain_ref = refs[pos], refs[pos + 1:pos + 1 + n2], refs[pos + 1 + n2], refs[pos + 2 + n2]
            pos += 3 + n2
        else:
            do_ref = refs[pos]
            pos += 1
        if has_prev:
            pq_ref, pv_ref = refs[pos], refs[pos + 1]
            pos += 2
        dq_ref, df_ref, dv_ref, dlb_ref = refs[pos:pos + 4]
        pos += 4
        if fused:
            do_out, dg_ref, dgain_ref = refs[pos:pos + 3]
            pos += 3
        dst = refs[pos]

        @pl.when(pl.program_id(1) == 0)
        def _():
            dst[...] = jnp.zeros_like(dst)
            dlb_ref[...] = jnp.zeros_like(dlb_ref)

        if fused:
            @pl.when((pl.program_id(0) == 0) & (pl.program_id(1) == 0))
            def _():
                dgain_ref[...] = jnp.zeros_like(dgain_ref)

        cat = functools.partial(jnp.concatenate, axis=0)
        for hh in range(HGRN_HP):
            ln = slice(128 * hh, 128 * hh + 128)
            lbv = lb_ref[:, ln]
            qraw, fraw = _head_cols(q_refs, hh), _head_cols(f_refs, hh)
            q, k, lf, sq, sf, f = _hgrn_gates(qraw, fraw, lbv)
            tri, eq, ek, ei, eki, eb = _block_terms(lf, rev)
            qe, ke, qi, ki = q * eq, k * ek, q * ei, k * eki
            if fused:
                dov, dg, dgain = _headnorm_grad(osum_ref[:, ln], _head_cols(g_refs, hh), dmix_ref[:, ln], gain_ref[...])
                do_out[:, ln] = _bf(dov)
                dg_ref[:, ln] = dg
                _acc_row(dgain_ref, 0, dgain)
            else:
                dov = do_ref[:, ln]
            qeb, keb, qib, kib, vb, dob = _bf(qe), _bf(ke), _bf(qi), _bf(ki), _bf(_head_cols(v_refs, hh)), _bf(dov)
            dv, dqe, dke, dqi = [None] * nc, [None] * nc, [None] * nc, [None] * nc
            for cc in range(nc):
                rows = slice(cc * CHUNK, (cc + 1) * CHUNK)
                a = jnp.where(tri, _dot_nt(qeb[rows], keb[rows]), 0.0)
                da = jnp.where(tri, _dot_nt(dob[rows], vb[rows]), 0.0)
                dv[cc] = _dot_tn(a, dob[rows])
                dqe[cc], dke[cc] = _dot(da, keb[rows]), _dot_tn(da, qeb[rows])
                dqi[cc] = _dot(dob[rows], sh_ref[hh, cc])
            dki, dbl = [None] * nc, [None] * nc
            ds = dst[hh]
            for cc in _chunk_order(rev, True):
                rows = slice(cc * CHUNK, (cc + 1) * CHUNK)
                ebc = eb[cc * CHUNK:cc * CHUNK + 1, :]
                dv[cc] = dv[cc] + _dot_nt(kib[rows], ds)
                dki[cc] = _dot(vb[rows], ds)
                dbl[cc] = jnp.broadcast_to(jnp.sum(dki[cc] * ki[rows], axis=0, keepdims=True)
                                           + jnp.sum(ds * sh_ref[hh, cc], axis=0, keepdims=True) * ebc, (CHUNK, 128))
                ds = ds * ebc + _dot_tn(dob[rows], qib[rows])
            dst[hh] = ds
            dqe, dke, dqi, dki, dv, dbl = cat(dqe), cat(dke), cat(dqi), cat(dki), cat(dv), cat(dbl)
            dq = dqe * eq + dqi * ei
            dk = dke * ek + dki * eki
            last = 0 if rev else CHUNK - 1
            db = dqe * qe - dke * ke + dqi * qi - dki * ki
            db = db + jnp.where((_iota(db.shape, 0) & (CHUNK - 1)) == last, dbl, 0.0)
            dlf = _chunk_cumsum(db, not rev)
            dqr = dq * (sq * (1.0 + qraw * (1.0 - sq)))
            dfv = dlf / f - dk
            dfr = dfv * (1.0 - lbv) * (sf * (1.0 - sf))
            dlb_ref[:, ln] += jnp.sum(dfv * (1.0 - sf), axis=0, keepdims=True)
            if has_prev:
                dqr = dqr + pq_ref[:, ln]
                dv = dv + pv_ref[:, ln]
            dq_ref[:, ln] = dqr.astype(odt)
            df_ref[:, ln] = dfr.astype(odt)
            dv_ref[:, ln] = dv.astype(odt)

    hp, wd = HGRN_HP, 128 * HGRN_HP
    col = functools.partial(_hgrn_cols, bmap, n2)
    oblk = pl.BlockSpec((TM, wd), lambda h, n: (bmap(n), h))
    ins = [p] * (3 * n2) + [lb, sh]
    specs = col(6) + col(fcol) + col(18) + [pl.BlockSpec((1, wd), lambda h, n: (0, h)),
                                            pl.BlockSpec((hp, nc, 128, 128), lambda h, n: (h, bmap(n), 0, 0))]
    if fused:
        osum, dmix, gain = head
        ins += [osum] + [p] * n2 + [dmix, gain]
        specs += [oblk] + col(22) + [pl.BlockSpec((TM, wd), lambda h, n: (bmap(n), 4 // hp + h)),
                                     pl.BlockSpec((1, 128), lambda h, n: (0, 0))]
    else:
        ins.append(do); specs.append(oblk)
    if has_prev:
        ins += list(prev); specs += [oblk, oblk]
    out_specs = [oblk, oblk, oblk, pl.BlockSpec((1, wd), lambda h, n: (0, h))]
    out_shape = [jax.ShapeDtypeStruct((t, 512), odt)] * 3 + [jax.ShapeDtypeStruct((1, 512), F32)]
    if fused:
        out_specs += [oblk, oblk, pl.BlockSpec((8, 128), lambda h, n: (0, 0))]
        out_shape += [jax.ShapeDtypeStruct((t, 512), BF16), jax.ShapeDtypeStruct((t, 512), BF16),
                      jax.ShapeDtypeStruct((8, 128), F32)]
    return _pcall(body, name=name, grid=(4 // hp, nb), in_specs=specs, out_specs=out_specs, out_shape=out_shape,
                  scratch_shapes=[pltpu.VMEM((hp, 128, 128), F32)])(*ins)


def _rope256(x, cos, sin):
    x1, x2 = x[:, 0:128], x[:, 128:256]
    return jnp.concatenate([x1 * cos - x2 * sin, x2 * cos + x1 * sin], axis=-1)


def _rope256_t(d, cos, sin):
    d1, d2 = d[:, 0:128], d[:, 128:256]
    return jnp.concatenate([d1 * cos + d2 * sin, d2 * cos - d1 * sin], axis=-1)


RET_DK, RET_DV, RET_H = 256, 512, 4
RET_KSCALE = RET_DK ** -0.5
RCH = TM
RET_HP = 4


def _ret_terms(lg, rev):
    r, c = _iota((RCH, RCH), 0), _iota((RCH, RCH), 1)
    rel = ((c - r) if rev else (r - c)).astype(F32)
    dmat = jnp.where(rel >= 0, jnp.exp(lg[:, 0:1] * jnp.maximum(rel, 0.0)), 0.0)
    pos = _iota((RCH, 1), 0).astype(F32)
    cnt = (RCH - pos) if rev else (pos + 1.0)
    ei = jnp.exp(lg * cnt)
    eki = jnp.exp(lg * (RCH - cnt))
    eb = jnp.exp(lg * float(RCH))
    return dmat, ei, eki, eb


def _ret_fwd(p, lgt, cos, sin, *, rev, name, ofw=None):
    t = p.shape[0]
    nb, nc = t // TM, TM // RCH
    bmap = _blk_map(nb, rev, False)
    fused = ofw is not None

    def body(*refs):
        q_ref, k_ref, v_ref, lg_ref, c_ref, s_ref = refs[:6]
        if fused:
            ofw_ref, g_ref, o_ref, sh_ref, mix_ref, st = refs[6:]
        else:
            o_ref, sh_ref, st = refs[6:]

        @pl.when(pl.program_id(1) == 0)
        def _():
            st[...] = jnp.zeros_like(st)
        for hh in range(RET_HP):
            qc, vc = slice(RET_DK * hh, RET_DK * (hh + 1)), slice(RET_DV * hh, RET_DV * (hh + 1))
            dmat, ei, eki, eb = _ret_terms(lg_ref[hh], rev)
            for cc in _chunk_order(rev, False, nc):
                rows = slice(cc * RCH, (cc + 1) * RCH)
                cosv, sinv = c_ref[rows, :], s_ref[rows, :]
                q = _rope256(q_ref[rows, qc].astype(F32), cosv, sinv)
                k = _rope256(k_ref[rows, qc].astype(F32), cosv, sinv) * RET_KSCALE
                v = v_ref[rows, vc]
                s0 = st[hh]
                sh_ref[hh, cc] = s0.astype(BF16)
                a = _dot_nt(q, k) * dmat
                o = _dot(a, v) + _dot_nt(q * ei, s0)
                st[hh] = s0 * eb + _dot_tn(v, k * eki)
                if fused:
                    o = o + ofw_ref[rows, vc]
                    mix_ref[rows, vc] = _headnorm_apply(o, g_ref[rows, vc].astype(F32), None)
                o_ref[rows, vc] = o

    hp = RET_HP
    tab = pl.BlockSpec((TM, 128), lambda h, n: (bmap(n), 0))
    oblk = pl.BlockSpec((TM, hp * RET_DV), lambda h, n: (bmap(n), h))
    ins = [p, p, p, lgt, cos, sin]
    specs = [pl.BlockSpec((TM, hp * RET_DK), lambda h, n: (bmap(n), h)),
             pl.BlockSpec((TM, hp * RET_DK), lambda h, n: (bmap(n), RET_H // hp + h)),
             pl.BlockSpec((TM, hp * RET_DV), lambda h, n: (bmap(n), RET_H // hp + h)),
             pl.BlockSpec((hp, 1, RET_DK), lambda h, n: (h, 0, 0)), tab, tab]
    out_specs = [oblk, pl.BlockSpec((hp, nc, RET_DV, RET_DK), lambda h, n: (h, bmap(n), 0, 0))]
    out_shape = [jax.ShapeDtypeStruct((t, RET_H * RET_DV), F32),
                 jax.ShapeDtypeStruct((RET_H, t // RCH, RET_DV, RET_DK), BF16)]
    if fused:
        ins += [ofw, p]
        specs += [oblk, pl.BlockSpec((TM, hp * RET_DV), lambda h, n: (bmap(n), 2 * RET_H // hp + h))]
        out_specs.append(oblk)
        out_shape.append(jax.ShapeDtypeStruct((t, RET_H * RET_DV), BF16))
    return _pcall(body, name=name, grid=(RET_H // hp, nb), in_specs=specs, out_specs=out_specs, out_shape=out_shape,
                  scratch_shapes=[pltpu.VMEM((hp, RET_DV, RET_DK), F32)])(*ins)


def _ret_bwd(p, lgt, cos, sin, sh, do, prev, *, rev, name, head=None):
    t = p.shape[0]
    nb, nc = t // TM, TM // RCH
    bmap = _blk_map(nb, rev, True)
    has_prev = prev is not None
    odt = BF16
    fused = head is not None

    def body(*refs):
        refs = list(refs)
        q_ref, k_ref, v_ref, lg_ref, c_ref, s_ref, sh_ref = refs[:7]
        if fused:
            osum_ref, g_ref, dy_ref, wout_ref = refs[7:11]
            pos = 11
        else:
            do_ref = refs[7]
            pos = 8
        if has_prev:
            pq_ref, pk_ref, pv_ref = refs[pos:pos + 3]
            pos += 3
        dq_ref, dk_ref, dv_ref = refs[pos:pos + 3]
        pos += 3
        if fused:
            do_out, dg_ref = refs[pos:pos + 2]
            pos += 2
        dst = refs[pos]

        @pl.when(pl.program_id(1) == 0)
        def _():
            dst[...] = jnp.zeros_like(dst)

        if fused:
            dmix = _dot_nt(dy_ref[...], wout_ref[...])
        for hh in range(RET_HP):
            qc, vc = slice(RET_DK * hh, RET_DK * (hh + 1)), slice(RET_DV * hh, RET_DV * (hh + 1))
            dmat, ei, eki, eb = _ret_terms(lg_ref[hh], rev)
            for cc in _chunk_order(rev, True, nc):
                rows = slice(cc * RCH, (cc + 1) * RCH)
                cosv, sinv = c_ref[rows, :], s_ref[rows, :]
                q = _rope256(q_ref[rows, qc].astype(F32), cosv, sinv)
                k = _rope256(k_ref[rows, qc].astype(F32), cosv, sinv) * RET_KSCALE
                v = v_ref[rows, vc]
                if fused:
                    dov, dg, _ = _headnorm_grad(osum_ref[rows, vc], g_ref[rows, vc].astype(F32), dmix[rows, vc], None)
                    do_out[rows, vc] = _bf(dov)
                    dg_ref[rows, vc] = dg
                else:
                    dov = do_ref[rows, vc]
                s0 = sh_ref[hh, cc]
                dsc = dst[hh]
                qi, ki = q * ei, k * eki
                a = _dot_nt(q, k) * dmat
                da = _dot_nt(dov, v) * dmat
                dv = _dot_tn(a, dov) + _dot_nt(ki, dsc)
                dqs = _dot(da, k) + _dot(dov, s0) * ei
                dks = _dot_tn(da, q) + _dot(v, dsc) * eki
                dst[hh] = dsc * eb + _dot_tn(dov, qi)
                dq = _rope256_t(dqs, cosv, sinv)
                dk = _rope256_t(dks * RET_KSCALE, cosv, sinv)
                if has_prev:
                    dq = dq + pq_ref[rows, qc]
                    dk = dk + pk_ref[rows, qc]
                    dv = dv + pv_ref[rows, vc]
                dq_ref[rows, qc] = dq.astype(odt)
                dk_ref[rows, qc] = dk.astype(odt)
                dv_ref[rows, vc] = dv.astype(odt)

    hp = RET_HP
    tab = pl.BlockSpec((TM, 128), lambda h, n: (bmap(n), 0))
    qblk = pl.BlockSpec((TM, hp * RET_DK), lambda h, n: (bmap(n), h))
    vblk = pl.BlockSpec((TM, hp * RET_DV), lambda h, n: (bmap(n), h))
    ins = [p, p, p, lgt, cos, sin, sh]
    specs = [qblk, pl.BlockSpec((TM, hp * RET_DK), lambda h, n: (bmap(n), RET_H // hp + h)),
             pl.BlockSpec((TM, hp * RET_DV), lambda h, n: (bmap(n), RET_H // hp + h)),
             pl.BlockSpec((hp, 1, RET_DK), lambda h, n: (h, 0, 0)), tab, tab,
             pl.BlockSpec((hp, nc, RET_DV, RET_DK), lambda h, n: (h, bmap(n), 0, 0))]
    if fused:
        osum, dy, w_out = head
        assert hp == RET_H and w_out.shape[0] == RET_H * RET_DV
        ins += [osum, p, dy, w_out]
        specs += [vblk, pl.BlockSpec((TM, hp * RET_DV), lambda h, n: (bmap(n), 2 * RET_H // hp + h)),
                  pl.BlockSpec((TM, dy.shape[1]), lambda h, n: (bmap(n), 0)),
                  pl.BlockSpec(w_out.shape, lambda h, n: (0, 0))]
    else:
        ins.append(do); specs.append(vblk)
    if has_prev:
        ins += list(prev); specs += [qblk, qblk, vblk]
    out_specs = [qblk, qblk, vblk]
    out_shape = [jax.ShapeDtypeStruct((t, RET_H * RET_DK), odt), jax.ShapeDtypeStruct((t, RET_H * RET_DK), odt),
                 jax.ShapeDtypeStruct((t, RET_H * RET_DV), odt)]
    if fused:
        out_specs += [vblk, vblk]
        out_shape += [jax.ShapeDtypeStruct((t, RET_H * RET_DV), BF16), jax.ShapeDtypeStruct((t, RET_H * RET_DV), BF16)]
    return _pcall(body, name=name, grid=(RET_H // hp, nb), in_specs=specs, out_specs=out_specs, out_shape=out_shape,
                  scratch_shapes=[pltpu.VMEM((hp, RET_DV, RET_DK), F32)])(*ins)


def _rope_tables(lc, l):
    tt = jnp.arange(l)
    row, colp = (tt // 64).astype(F32), (tt % 64).astype(F32)
    inv = 10000.0 ** (-jnp.arange(16, dtype=F32) / 16)
    ang = jnp.concatenate([row[:, None] * inv, colp[:, None] * inv], axis=-1)
    ang = jnp.concatenate([jnp.zeros((lc, 32), F32), ang], axis=0)
    acos, asin = jnp.tile(jnp.cos(ang), (1, 4)), jnp.tile(jnp.sin(ang), (1, 4))
    theta = 1.0 / (10000.0 ** jnp.linspace(0.0, 1.0, 128, dtype=F32))
    rang = jnp.arange(l, dtype=F32)[:, None] * theta
    rang = jnp.concatenate([jnp.zeros((lc, 128), F32), rang], axis=0)
    return acos, asin, jnp.cos(rang), jnp.sin(rang)


class _Weights:
    def __init__(self, w):
        self.w = w

    def landed(self, grp, after):
        pass

    def full(self, grp, after):
        return self.w

    def send_grads(self, grp, grads):
        return jnp.zeros((8, 128), F32)


def _local_step(x0, target, mods, ng, wsrc, small):
    t, d = x0.shape
    l = target.shape[0]
    lc = t - l
    acos, asin, rcos, rsin = _rope_tables(lc, l)
    lg_fw = jnp.log(1.0 - 2.0 ** (-5.0 - jnp.arange(RET_H, dtype=F32)))
    lgt_fw = jnp.broadcast_to(lg_fw[:, None, None], (RET_H, 1, RET_DK))
    lgt_bw = jnp.broadcast_to(lg_fw[::-1][:, None, None], (RET_H, 1, RET_DK))
    gq, gk, sink, gain, lb = small['gq'], small['gk'], small['sink'], small['gain'], small['lb']

    (h1,) = _row_fwd(x0, mods, g=ng[0], shift=0, scale=1, name='l0_norm1')
    wsrc.landed('even', h1)
    w = dict(wsrc.full('even', h1))
    p0 = _mm_nn(h1, w['even_in'], name='l0_in')
    kp = _kprep_fwd(p0, gk, acos, asin, name='l0_kprep')
    att = _attn_fwd(p0, kp, gq, sink, acos, asin, lc=lc, name='l0_attn')
    wsrc.landed('ffn', att)
    hof, hsf = _hgrn_fwd(p0, lb, rev=False, name='l0_hgrn_f')
    wsrc.landed('odd', hof)
    hos, hsb, bmix = _hgrn_fwd(p0, lb, rev=True, name='l0_hgrn_b', ofw=hof, gain=gain)
    mix0 = [att, bmix]
    y0 = _mm_nn(mix0, w['even_out'], name='l0_out')
    x1, h2 = _row_fwd(x0, mods, y=y0, gate=2, g=ng[1], shift=3, scale=4, name='l0_norm2')
    w.update(wsrc.full('ffn', h2))
    u0, a0 = _ffn_in(h2, w['ffn_in'], lead=0, name='ffn_in')
    z0 = _mm_nn(a0, w['ffn_out'], lead=0, name='ffn_out')
    x2, h3 = _row_fwd(x1, mods, y=z0, gate=5, g=ng[2], shift=12, scale=13, name='l1_norm1')
    w.update(wsrc.full('odd', h3))
    p1 = _mm_nn(h3, w['odd_in'], out_dtype=BF16, name='l1_in')
    rof, rsf = _ret_fwd(p1, lgt_fw, rcos, rsin, rev=False, name='l1_ret_f')
    ros, rsb, mix1 = _ret_fwd(p1, lgt_bw, rcos, rsin, rev=True, name='l1_ret_b', ofw=rof)
    y1 = _mm_nn(mix1, w['odd_out'], name='l1_out')
    x3, h4 = _row_fwd(x2, mods, y=y1, gate=14, g=ng[3], shift=15, scale=16, name='l1_norm2')
    u1, a1 = _ffn_in(h4, w['ffn_in'], lead=1, name='ffn_in')
    z1 = _mm_nn(a1, w['ffn_out'], lead=1, name='ffn_out')
    loss, dx4, dz1, s_fin = _row_final(x3, z1, mods, target, gate=17, name='loss')

    du1 = _ffn_dx(dz1, w['ffn_out'], u1, lead=1, name='ffn_out_dx')
    g_ffn_out1 = _mm_tn(a1, dz1, name='ffn_out_dw')
    dh4 = _mm_nt(du1, w['ffn_in'], lead=1, name='ffn_in_dx')
    g_ffn_in1 = _mm_tn(h4, du1, name='ffn_in_dw')
    dx3, dy1, s_l1n2 = _row_bwd(x3, dx4, dh4, mods, ng[3], shift=15, scale=16, y=y1, gate=14, name='l1_norm2_bwd')
    g_odd_out = _mm_tn(mix1, dy1, name='l1_out_dw')
    rdq, rdk, rdv, rdo, rdg = _ret_bwd(p1, lgt_fw, rcos, rsin, rsf, None, None, rev=False, name='l1_ret_f_bwd',
                                       head=(ros, dy1, w['odd_out']))
    rdq, rdk, rdv = _ret_bwd(p1, lgt_bw, rcos, rsin, rsb, rdo, (rdq, rdk, rdv), rev=True, name='l1_ret_b_bwd')
    dp1 = [rdq, rdk, rdv, rdg]
    dh3 = _mm_nt(dp1, w['odd_in'], name='l1_in_dx')
    g_odd_in = _mm_tn(h3, dp1, name='l1_in_dw')
    mods = mods + wsrc.send_grads('early', dict(ffn_in1=g_ffn_in1, ffn_out1=g_ffn_out1, odd_in=g_odd_in,
                                                odd_out=g_odd_out))[0, 0]
    dx2, dz0, s_l1n1 = _row_bwd(x2, dx3, dh3, mods, ng[2], shift=12, scale=13, y=z0, gate=5, name='l1_norm1_bwd')
    du0 = _ffn_dx(dz0, w['ffn_out'], u0, lead=0, name='ffn_out_dx')
    g_ffn_out0 = _mm_tn(a0, dz0, name='ffn_out_dw')
    dh2 = _mm_nt(du0, w['ffn_in'], lead=0, name='ffn_in_dx')
    g_ffn_in0 = _mm_tn(h2, du0, name='ffn_in_dw')
    mods = mods + wsrc.send_grads('mid', dict(ffn_in0=g_ffn_in0, ffn_out0=g_ffn_out0))[0, 0]
    dx1, dy0, s_l0n2 = _row_bwd(x1, dx2, dh2, mods, ng[1], shift=3, scale=4, y=y0, gate=2, name='l0_norm2_bwd')
    dmix0 = _mm_nt(dy0, w['even_out'], name='l0_out_dx')
    g_even_out = _mm_tn(mix0, dy0, name='l0_out_dw')
    hq, hff, hv, dlb_f, hdo, hdg, s_gain = _hgrn_bwd(p0, lb, hsf, None, None, rev=False, name='l0_hgrn_f_bwd',
                                                     head=(hos, dmix0, gain))
    hq, hfb, hv, dlb_b = _hgrn_bwd(p0, lb, hsb, hdo, (hq, hv), rev=True, name='l0_hgrn_b_bwd')
    adq, dkp, adv, s_gq, s_sink = _attn_bwd(p0, kp, gq, sink, acos, asin, dmix0, lc=lc, name='l0_attn_bwd')
    dkv, s_gk = _kprep_bwd(p0, gk, acos, asin, dkp, adv, name='l0_kprep_bwd')
    dp0 = jnp.concatenate([adq, dkv, hq, _bf(hff), hfb, hv, hdg], axis=1)
    dh1 = _mm_nt(dp0, w['even_in'], name='l0_in_dx')
    g_even_in = _mm_tn(h1, dp0, name='l0_in_dw')
    dx0, s_l0n1 = _row_bwd(x0, dx1, dh1, mods, ng[0], shift=0, scale=1, latent_only=True, name='l0_norm1_bwd')

    grads = dict(ffn_in0=g_ffn_in0, ffn_in1=g_ffn_in1, ffn_out0=g_ffn_out0, ffn_out1=g_ffn_out1,
                 even_in=g_even_in, even_out=g_even_out, odd_in=g_odd_in, odd_out=g_odd_out)
    sums = dict(fin=s_fin, l1n2=s_l1n2, l1n1=s_l1n1, l0n2=s_l0n2, l0n1=s_l0n1, gain=s_gain, gq=s_gq, gk=s_gk,
                sink=s_sink, dlb_f=dlb_f, dlb_b=dlb_b)
    return loss, dx0, grads, sums


def _place():
    return lax.axis_index("x"), lax.axis_index("y"), lax.axis_index("c")


def _ag8(blk, *, name):
    r, c = blk.shape
    flips = [(dx, dy, dc) for dx in (0, 1) for dy in (0, 1) for dc in (0, 1) if (dx, dy, dc) != (0, 0, 0)]

    def body(x_ref, out_ref, send_sems, recv_sems, local_sem):
        ax, ay, ac = _place()
        me = 4 * ax + 2 * ay + ac
        mine = pltpu.make_async_copy(x_ref, out_ref.at[me], local_sem)
        mine.start()
        sent = []
        for k, (dx, dy, dc) in enumerate(flips):
            peer = (lax.rem(ax + dx, 2), lax.rem(ay + dy, 2), lax.rem(ac + dc, 2))
            cp = pltpu.make_async_remote_copy(src_ref=x_ref, dst_ref=out_ref.at[me], send_sem=send_sems.at[k],
                                              recv_sem=recv_sems.at[k], device_id=peer, device_id_type=MESH)
            cp.start()
            sent.append((cp, 4 * peer[0] + 2 * peer[1] + peer[2]))
        for k, (cp, pidx) in enumerate(sent):
            pltpu.make_async_remote_copy(src_ref=x_ref, dst_ref=out_ref.at[pidx], send_sem=send_sems.at[k],
                                         recv_sem=recv_sems.at[k], device_id=(ax, ay, ac),
                                         device_id_type=MESH).wait_recv()
        for cp, _ in sent:
            cp.wait_send()
        mine.wait()

    return _pcall(
        body, name=name,
        in_specs=[pl.BlockSpec(memory_space=pltpu.VMEM)],
        out_specs=pl.BlockSpec(memory_space=pltpu.VMEM),
        out_shape=jax.ShapeDtypeStruct((8, r, c), blk.dtype),
        scratch_shapes=[pltpu.SemaphoreType.DMA((7,)), pltpu.SemaphoreType.DMA((7,)), pltpu.SemaphoreType.DMA],
    )(blk)


_HBM = pl.BlockSpec(memory_space=pltpu.HBM)
_SEM = pl.BlockSpec(memory_space=pltpu.SEMAPHORE)
_DATAFLOW = pltpu.SideEffectType.DATAFLOW_SIDE_EFFECTING


def _split_start(bufs, plan, k, *, name):
    n = len(bufs)

    def body(*refs):
        ins, send_sems, recv_sems, token = refs[:n], refs[n], refs[n + 1], refs[2 * n + 2]
        for i, (src, dst, dev) in enumerate(plan(ins)):
            pltpu.make_async_remote_copy(src_ref=src, dst_ref=dst, send_sem=send_sems.at[i], recv_sem=recv_sems.at[i],
                                         device_id=dev, device_id_type=MESH).start()
        token[...] = jnp.zeros_like(token)

    res = _pcall(
        body, name=name,
        out_shape=(pltpu.SemaphoreType.DMA((k,)), pltpu.SemaphoreType.DMA((k,)),
                   *[pltpu.HBM(b.shape, b.dtype) for b in bufs], jax.ShapeDtypeStruct((8, 128), F32)),
        in_specs=[_HBM] * n, out_specs=(_SEM, _SEM, *[_HBM] * n, pl.BlockSpec(memory_space=pltpu.VMEM)),
        input_output_aliases={i: 2 + i for i in range(n)},
        compiler_params=pltpu.CompilerParams(has_side_effects=_DATAFLOW),
    )(*[pltpu.with_memory_space_constraint(b, pltpu.HBM) for b in bufs])
    return res[0], res[1], list(res[2:2 + n]), res[2 + n]


def _split_wait(bufs, send_sems, recv_sems, plan, after, *, name):
    n = len(bufs)

    def body(*refs):
        ins, ssem, rsem = refs[:n], refs[n], refs[n + 1]
        for i, (src, dst, dev) in enumerate(plan(ins)):
            cp = pltpu.make_async_remote_copy(src_ref=src, dst_ref=dst, send_sem=ssem.at[i], recv_sem=rsem.at[i],
                                              device_id=dev, device_id_type=MESH)
            cp.wait_send()
            cp.wait_recv()

    res = _pcall(
        body, name=name, out_shape=tuple(pltpu.HBM(b.shape, b.dtype) for b in bufs),
        in_specs=[_HBM] * n + [_SEM, _SEM, pl.BlockSpec(memory_space=pl.ANY)], out_specs=tuple([_HBM] * n),
        input_output_aliases={i: i for i in range(n)},
        compiler_params=pltpu.CompilerParams(has_side_effects=_DATAFLOW),
    )(*bufs, send_sems, recv_sems, after)
    return list(res)


_CHIP_FLIPS = [(1, 0), (0, 1), (1, 1)]


class _GatheredWeights:
    GROUPS = (('even', ('even_in', 'even_out')), ('ffn', ('ffn_in', 'ffn_out')), ('odd', ('odd_in', 'odd_out')))

    def __init__(self, shards, reducer):
        self.shards = shards
        self.send_grads = reducer.start
        self.ici, self.d2d, self.token = {}, {}, None
        for grp, names in self.GROUPS:
            src = [shards[nm].reshape(2, shards[nm].shape[0] // 2, shards[nm].shape[1]) for nm in names]
            land = [lax.empty((4,) + a.shape, a.dtype) for a in src]
            m = len(names)
            sends, recvs, bufs, token = _split_start(src + land, functools.partial(self._ici_plan, m, True), 4 * m,
                                                     name='gather_' + grp + '_ici_start')
            self.ici[grp] = (sends, recvs, bufs, m)
            self.token = token if self.token is None else self.token + token

    @staticmethod
    def _ici_plan(m, sending, refs):
        ax, ay, ac = _place()
        s = 2 * ax + ay
        out = []
        for a in range(m):
            for dx, dy in _CHIP_FLIPS:
                px, py = lax.rem(ax + dx, 2), lax.rem(ay + dy, 2)
                slot = s if sending else 2 * px + py
                out.append((refs[a].at[ac], refs[m + a].at[slot, ac], (px, py, ac)))
        for a in range(m):
            out.append((refs[a], refs[m + a].at[s], (ax, ay, 1 - ac)))
        return out

    @staticmethod
    def _d2d_plan(m, sending, refs):
        ax, ay, ac = _place()
        out = []
        for a in range(m):
            for dx, dy in _CHIP_FLIPS:
                sp = 2 * lax.rem(ax + dx, 2) + lax.rem(ay + dy, 2)
                out.append((refs[a].at[sp, ac], refs[a].at[sp, ac if sending else 1 - ac], (ax, ay, 1 - ac)))
        return out

    def landed(self, grp, after):
        sends, recvs, bufs, m = self.ici[grp]
        bufs = _split_wait(bufs, sends, recvs, functools.partial(self._ici_plan, m, False), after,
                           name='gather_' + grp + '_ici_wait')
        sends, recvs, land, _ = _split_start(bufs[m:], functools.partial(self._d2d_plan, m, True), 3 * m,
                                             name='gather_' + grp + '_d2d_start')
        self.d2d[grp] = (sends, recvs, land, m)

    def full(self, grp, after):
        sends, recvs, land, m = self.d2d[grp]
        land = _split_wait(land, sends, recvs, functools.partial(self._d2d_plan, m, False), after,
                           name='gather_' + grp + '_d2d_wait')
        names = dict(self.GROUPS)[grp]
        return {nm: _from_shards(nm, g.reshape((4,) + self.shards[nm].shape)) for nm, g in zip(names, land)}


def _to_sibling(arrs, *, name):
    n = len(arrs)

    def body(*refs):
        ins, outs = refs[:n], refs[n:2 * n]
        send_sems, recv_sems = refs[2 * n:]
        ax, ay, ac = _place()
        cps = [pltpu.make_async_remote_copy(src_ref=ins[a], dst_ref=outs[a], send_sem=send_sems.at[a],
                                            recv_sem=recv_sems.at[a], device_id=(ax, ay, 1 - ac),
                                            device_id_type=MESH) for a in range(n)]
        for cp in cps:
            cp.start()
        for cp in cps:
            cp.wait_recv()
        for cp in cps:
            cp.wait_send()

    hbm = pl.BlockSpec(memory_space=pl.ANY)
    return _pcall(
        body, name=name, in_specs=[hbm] * n, out_specs=[hbm] * n,
        out_shape=[jax.ShapeDtypeStruct(a.shape, a.dtype) for a in arrs],
        scratch_shapes=[pltpu.SemaphoreType.DMA((n,))] * 2,
    )(*arrs)


def _mod_fwd(cond_raw, mw, mb, *, name):
    _, d, n = mw.shape

    def body(c_ref, w_ref, b_ref, o_ref):
        cv = c_ref[...]
        o_ref[...] = _dot(cv * _sigmoid(cv), w_ref[...]) + b_ref[...]

    return _pcall(
        body, name=name, grid=(2,),
        in_specs=[pl.BlockSpec((16, d), lambda l: (0, 0)), pl.BlockSpec((None, d, n), lambda l: (l, 0, 0)),
                  pl.BlockSpec((None, 1, n), lambda l: (l, 0, 0))],
        out_specs=pl.BlockSpec((None, 16, n), lambda l: (l, 0, 0)),
        out_shape=jax.ShapeDtypeStruct((2, 16, n), F32),
    )(cond_raw, mw, mb)


def _mod_bwd(cond_raw, dms, mw, *, name):
    _, d, n = mw.shape

    def body(c_ref, dm_ref, w_ref, gw_ref, dc_ref):
        @pl.when(pl.program_id(0) == 0)
        def _():
            dc_ref[...] = jnp.zeros_like(dc_ref)
        cv = c_ref[...]
        gw_ref[...] = _dot_tn(cv * _sigmoid(cv), dm_ref[...])
        dc_ref[...] += _dot_nt(dm_ref[...], w_ref[...])

    return _pcall(
        body, name=name, grid=(2,),
        in_specs=[pl.BlockSpec((16, d), lambda l: (0, 0)), pl.BlockSpec((None, 16, n), lambda l: (l, 0, 0)),
                  pl.BlockSpec((None, d, n), lambda l: (l, 0, 0))],
        out_specs=[pl.BlockSpec((None, d, n), lambda l: (l, 0, 0)), pl.BlockSpec((16, d), lambda l: (0, 0))],
        out_shape=[jax.ShapeDtypeStruct((2, d, n), F32), jax.ShapeDtypeStruct((16, d), F32)],
    )(cond_raw, dms, mw)


def _lb_fwd(hgrn_lb, *, name):
    def body(a_ref, o_ref):
        a0, a1 = a_ref[0:1, :], a_ref[1:2, :]
        m = jnp.maximum(a0, a1)
        e0, e1 = jnp.exp(a0 - m), jnp.exp(a1 - m)
        o_ref[...] = e0 / (e0 + e1)

    return _pcall(body, name=name, out_shape=jax.ShapeDtypeStruct((1, hgrn_lb.shape[1]), F32))(hgrn_lb)


PACK_TILES = ('l0n1', 'l0n2', 'l1n1', 'l1n2', 'fin')
PACK_SINGLES = ('gq', 'gk', 'gain', 'dlb_f', 'dlb_b', 'sink')
PACK_ROW = {nm: 8 * i for i, nm in enumerate(PACK_TILES)}
PACK_ROW.update({nm: 8 * len(PACK_TILES) + i for i, nm in enumerate(PACK_SINGLES)})
MOD_SOURCE = ((('l0n1', 0), ('l0n1', 1), ('l0n2', 2), ('l0n2', 0), ('l0n2', 1), ('l1n1', 2)),
              (('l1n1', 0), ('l1n1', 1), ('l1n2', 2), ('l1n2', 0), ('l1n2', 1), ('fin', 2)))


def _small_finalize(gath, lb_pad, *, name):
    d = gath.shape[2]

    def body(g_ref, lb_ref, small_ref, glb_ref, gmb_ref, dm_ref):
        tot = g_ref[0]
        for e in range(1, 8):
            tot = tot + g_ref[e]

        def row(nm, r=0):
            return tot[PACK_ROW[nm] + r:PACK_ROW[nm] + r + 1, :]

        for k, nm in enumerate(('l0n1', 'l0n2', 'l1n1', 'l1n2')):
            small_ref[k:k + 1, :] = row(nm, 3) + row(nm, 7)
        for k, nm in ((4, 'gq'), (5, 'gk')):
            small_ref[k:k + 1, :] = row(nm) + pltpu.roll(row(nm), d - 64, 1)
        small_ref[6:7, :] = row('gain')
        small_ref[7:8, :] = row('sink')
        lbv = lb_ref[...]
        g0 = (row('dlb_f') + row('dlb_b')) * lbv * (1.0 - lbv)
        glb_ref[...] = jnp.zeros_like(glb_ref)
        glb_ref[0:1, :] = g0
        glb_ref[1:2, :] = -g0
        dm_ref[...] = jnp.zeros_like(dm_ref)
        for l in range(2):
            for part in range(6):
                nm, r = MOD_SOURCE[l][part]
                gmb_ref[l * 6 + part:l * 6 + part + 1, :] = row(nm, r) + row(nm, r + 4)
                rl = PACK_ROW[nm] + r + 4
                for e in range(8):
                    dm_ref[l, part, e:e + 1, :] = g_ref[e, rl:rl + 1, :]
                dm_ref[l, part, 8:9, :] = row(nm, r)

    return _pcall(
        body, name=name,
        out_shape=[jax.ShapeDtypeStruct((8, d), F32), jax.ShapeDtypeStruct((8, d), F32),
                   jax.ShapeDtypeStruct((12, d), F32), jax.ShapeDtypeStruct((2, 6, 16, d), F32)],
    )(gath, lb_pad)


def _cctx_grad(gath, c_ctx2, *, name):
    def body(g_ref, c_ref, o_ref):
        tot = ((g_ref[0, 0:1, :] + g_ref[2, 0:1, :]) + g_ref[4, 0:1, :]) + g_ref[6, 0:1, :]
        cv = c_ref[...]
        s = _sigmoid(cv)
        o_ref[...] = tot * (s * (1.0 + cv * (1.0 - s)))

    return _pcall(body, name=name, out_shape=jax.ShapeDtypeStruct(c_ctx2.shape, F32))(gath, c_ctx2)


def _row_block(r, c, limit=256 * 1024):
    best = None
    for br in range(16, r + 1, 16):
        if r % br == 0 and br * c <= limit:
            best = br
    return best if best is not None else r


def _sum4(own, landed, core, *, name):
    _, r, c = own.shape
    br = _row_block(r, c, 512 * 1024)

    def body(core_ref, own_ref, land_ref, o_ref):
        s = 2 * lax.axis_index("x") + lax.axis_index("y")
        p = [jnp.where(s == k, own_ref[k], land_ref[k]).astype(F32) for k in range(4)]
        o_ref[...] = ((p[0] + p[1]) + p[2]) + p[3]

    blk = pl.BlockSpec((4, br, c), lambda i, core_ref: (0, i, 0))
    spec = pltpu.PrefetchScalarGridSpec(
        num_scalar_prefetch=1, grid=(r // br,), in_specs=[blk, blk],
        out_specs=pl.BlockSpec((None, br, c), lambda i, core_ref: (core_ref[0], i, 0)))
    return _pcall(body, name=name, grid_spec=spec, out_shape=jax.ShapeDtypeStruct((2, r, c), F32))(core, own, landed)


def _exchange_halves(arrs, *, name):
    n = len(arrs)

    def body(*refs):
        ins, outs = refs[:n], refs[n:2 * n]
        send_sems, recv_sems = refs[2 * n:]
        ax, ay, ac = _place()
        cps = [pltpu.make_async_remote_copy(src_ref=ins[a].at[ac], dst_ref=outs[a].at[ac], send_sem=send_sems.at[a],
                                            recv_sem=recv_sems.at[a], device_id=(ax, ay, 1 - ac),
                                            device_id_type=MESH) for a in range(n)]
        for cp in cps:
            cp.start()
        for a in range(n):
            pltpu.make_async_remote_copy(src_ref=ins[a].at[ac], dst_ref=outs[a].at[1 - ac], send_sem=send_sems.at[a],
                                         recv_sem=recv_sems.at[a], device_id=(ax, ay, ac),
                                         device_id_type=MESH).wait_recv()
        for cp in cps:
            cp.wait_send()

    hbm = pl.BlockSpec(memory_space=pl.ANY)
    return _pcall(
        body, name=name, in_specs=[hbm] * n, out_specs=[hbm] * n,
        out_shape=[jax.ShapeDtypeStruct(a.shape, a.dtype) for a in arrs],
        input_output_aliases={a: a for a in range(n)},
        scratch_shapes=[pltpu.SemaphoreType.DMA((n,))] * 2,
    )(*arrs)


def _add2(a, b, *, name):
    r, c = a.shape
    br = _row_block(r, c, 1024 * 1024)

    def body(a_ref, b_ref, o_ref):
        o_ref[...] = (a_ref[...].astype(F32) + b_ref[...].astype(F32)).astype(BF16)

    blk = pl.BlockSpec((br, c), lambda i: (i, 0))
    return _pcall(body, name=name, grid=(r // br,), in_specs=[blk, blk], out_specs=blk,
                  out_shape=jax.ShapeDtypeStruct((r, c), BF16))(a, b)


def _adam(w, gs, m, v, *, name):
    r, c = w.shape
    br = _row_block(r, c)
    ng = len(gs)
    c1 = 1.0 - ADAM_B1 ** ADAM_STEP
    c2 = 1.0 - ADAM_B2 ** ADAM_STEP

    def body(*refs):
        w_ref, m_ref, v_ref = refs[0], refs[1 + ng], refs[2 + ng]
        outs = refs[3 + ng:]
        g = refs[1][...]
        for k in range(1, ng):
            g = g + refs[1 + k][...]
        mn = ADAM_B1 * m_ref[...] + (1.0 - ADAM_B1) * g
        vn = ADAM_B2 * v_ref[...] + (1.0 - ADAM_B2) * (g * g)
        if ng > 1:
            outs[0][...] = g
        d_out, m_out, v_out = outs[-3:]
        m_out[...] = mn
        v_out[...] = vn
        d_out[...] = -ADAM_LR * ((mn / c1) / (jnp.sqrt(vn / c2) + ADAM_EPS) + ADAM_WD * w_ref[...])

    blk = pl.BlockSpec((br, c), lambda i: (i, 0))
    nout = 4 if ng > 1 else 3
    res = _pcall(body, name=name, grid=(r // br,), in_specs=[blk] * (3 + ng), out_specs=[blk] * nout,
                 out_shape=[jax.ShapeDtypeStruct((r, c), F32)] * nout)(w, *gs, m, v)
    return list(res) if ng > 1 else [gs[0]] + list(res)


def _grad_halves(name, g, ac):
    if name.endswith('_in'):
        n = g.shape[1] // 4
        if name == 'ffn_in':
            assert n == FFN_BK
        order = _ffn_order(g.shape[1]) if name == 'ffn_in' else range(4)
        v = jnp.stack([g[:, b * n:(b + 1) * n] for b in order])
        per = [v[:, :g.shape[0] // 2], v[:, g.shape[0] // 2:]]
    else:
        k4, n = g.shape
        v = g.reshape(4, 2, k4 // 8, n)
        per = [v[:, 0], v[:, 1]]
    first = ac == 0
    return _bf(jnp.where(first, per[0], per[1])), _bf(jnp.where(first, per[1], per[0]))


class _GradReducer:
    def __init__(self):
        self.flight = {}

    @staticmethod
    def _plan(m, sending, refs):
        ax, ay, ac = _place()
        s = 2 * ax + ay
        out = []
        for a in range(m):
            for dx, dy in _CHIP_FLIPS:
                px, py = lax.rem(ax + dx, 2), lax.rem(ay + dy, 2)
                sp = 2 * px + py
                out.append((refs[a].at[sp], refs[m + a].at[s if sending else sp], (px, py, ac)))
        return out

    def start(self, grp, grads):
        ac = lax.axis_index("c")
        names = list(grads)
        halves = [_grad_halves(nm.rstrip('01'), grads[nm], ac) for nm in names]
        theirs = _to_sibling([h[1] for h in halves], name='swap_core_halves_' + grp)
        pair = [_add2(h[0].reshape(-1, b.shape[-1]), b.reshape(-1, b.shape[-1]), name='add_cores').reshape(b.shape)
                for h, b in zip(halves, theirs)]
        m = len(names)
        land = [lax.empty(a.shape, a.dtype) for a in pair]
        sends, recvs, bufs, token = _split_start(pair + land, functools.partial(self._plan, m, True), 3 * m,
                                                 name='scatter_' + grp + '_start')
        self.flight[grp] = (names, sends, recvs, bufs)
        return token

    def finish(self, grp, after):
        names, sends, recvs, bufs = self.flight.pop(grp)
        m = len(names)
        bufs = _split_wait(bufs, sends, recvs, functools.partial(self._plan, m, False), after,
                           name='scatter_' + grp + '_wait')
        core = lax.axis_index("c").astype(jnp.int32).reshape(1)
        sums = [_sum4(p, l, core, name='sum_chips') for p, l in zip(bufs[:m], bufs[m:])]
        both = _exchange_halves(sums, name='gather_core_halves_' + grp)
        return {nm: g.reshape(-1, g.shape[-1]) for nm, g in zip(names, both)}


def _from_shards(name, g):
    _, r, n = g.shape
    if name == 'ffn_in':
        assert n == FFN_BK
        v = g.reshape(4, 2, r // 2, n)
        return jnp.concatenate([v[b] for b in _ffn_order(4 * n)], axis=-1)
    if name == 'ffn_out':
        return g.reshape(4, 2, r // 2, n).transpose(1, 0, 2, 3).reshape(2, 2 * r, n)
    if name in ('even_in', 'odd_in'):
        return jnp.concatenate([g[b] for b in range(4)], axis=-1)
    return g.reshape(4 * r, n)


def kernel(x, c, ctx, c_ctx, mod_w, mod_b, norm_g, ffn_w_in, ffn_w_out, even_w_in, even_w_out, attn_qk_norm_g, attn_sink, hgrn_out_norm_g, hgrn_lb, odd_w_in, odd_w_out, loss_target, m_c_ctx, m_mod_w, m_mod_b, m_norm_g, m_ffn_w_in, m_ffn_w_out, m_even_w_in, m_even_w_out, m_attn_qk_norm_g, m_attn_sink, m_hgrn_out_norm_g, m_hgrn_lb, m_odd_w_in, m_odd_w_out, v_c_ctx, v_mod_w, v_mod_b, v_norm_g, v_ffn_w_in, v_ffn_w_out, v_even_w_in, v_even_w_out, v_attn_qk_norm_g, v_attn_sink, v_hgrn_out_norm_g, v_hgrn_lb, v_odd_w_in, v_odd_w_out):
    d = x.shape[-1]
    lc = ctx.shape[1]
    assert lc == TM and d == 1024
    ax, ay, ac = _place()
    s = 2 * ax + ay
    me = 4 * ax + 2 * ay + ac
    nmod = mod_w.shape[2]

    def pad8(v):
        return jnp.pad(v, ((0, 8 - v.shape[0]), (0, 0)))

    pack = jnp.concatenate([pad8(c), pad8(norm_g.reshape(1, d))], axis=0)
    g1 = _ag8(pack, name='gather_cond')
    c_all = g1[:, 0, :]
    ng = g1[0::2, 8, :].reshape(4, 2, 2, d // 4).transpose(1, 2, 0, 3).reshape(4, d)

    cond_raw = jnp.concatenate([c_all, pad8(c_ctx.reshape(1, d))], axis=0)
    mb_sh = lax.dynamic_slice_in_dim(mod_b, s * nmod, nmod, axis=1).reshape(2, 1, nmod)
    mpart = _mod_fwd(cond_raw, mod_w, mb_sh, name='mod_fwd')
    g3 = _ag8(mpart.reshape(32, nmod), name='gather_mods')
    mods_full = g3[0::2].reshape(4, 2, 16, nmod).transpose(1, 2, 0, 3).reshape(2, 16, 4 * nmod)
    m_lat = lax.dynamic_index_in_dim(mods_full, me, axis=1, keepdims=False)
    mods = jnp.stack([mods_full[:, 8], m_lat], axis=1).reshape(24, d)

    names = ['ffn_in', 'ffn_out', 'even_in', 'even_out', 'odd_in', 'odd_out']
    shards = [_bf(v.reshape(-1, v.shape[-1])) for v in (ffn_w_in, ffn_w_out, even_w_in, even_w_out, odd_w_in, odd_w_out)]
    shards, mods = lax.optimization_barrier((shards, mods))
    reducer = _GradReducer()
    wsrc = _GatheredWeights(dict(zip(names, shards)), reducer)

    lb = _lb_fwd(hgrn_lb, name='hgrn_lower_bound')
    small = dict(gq=jnp.tile(attn_qk_norm_g[0, 0], 2).reshape(1, 128), gk=jnp.tile(attn_qk_norm_g[0, 1], 2).reshape(1, 128),
                 sink=attn_sink[0], gain=hgrn_out_norm_g, lb=lb)
    x0 = jnp.concatenate([ctx[0], x[0]], axis=0)
    mods = mods + wsrc.token[0, 0]
    loss_t, dx0, grads, sums = _local_step(x0, loss_target[0], mods, ng, wsrc, small)
    loss = lax.psum(loss_t[0, 0], ("x", "y", "c"))
    grad_x = dx0[None]

    def tile(v, at=0):
        return jnp.pad(v[0:1], ((at, 7 - at), (0, d - v.shape[1])))

    sums = dict(sums, sink=sums['sink'][:, 0].reshape(1, 8))
    singles = sum(tile(sums[nm], i) for i, nm in enumerate(PACK_SINGLES))
    g4 = _ag8(jnp.concatenate([sums[nm] for nm in PACK_TILES] + [singles], axis=0), name='gather_row_sums')
    small_g, glb, gmb, dmat = _small_finalize(g4, tile(lb)[0:1], name='small_grads')
    dms = lax.dynamic_slice_in_dim(dmat.transpose(0, 2, 1, 3).reshape(2, 16, 6 * d), s * nmod, nmod, axis=2)
    g_mod_w, dcond = _mod_bwd(cond_raw, dms, mod_w, name='mod_bwd')
    g5 = _ag8(dcond[8:16], name='gather_dcond')
    g_c_ctx = _cctx_grad(g5, c_ctx.reshape(8, d // 8).reshape(1, d), name='c_ctx_grad')

    late = {nm: grads[nm] for nm in ('even_in', 'even_out')}
    late, g_c_ctx = lax.optimization_barrier((late, g_c_ctx))
    token = reducer.start('late', late)
    full = reducer.finish('early', token)

    def upd(wv, gs, mv, vv, name):
        shp = wv.shape
        c2 = shp[-1]
        out = _adam(wv.reshape(-1, c2), [g.reshape(-1, c2) for g in gs], mv.reshape(-1, c2), vv.reshape(-1, c2), name=name)
        return [o.reshape(shp) for o in out]

    res = {}
    res['c_ctx'] = upd(c_ctx.reshape(8, d // 8), [g_c_ctx.reshape(8, d // 8)], m_c_ctx.reshape(8, d // 8), v_c_ctx.reshape(8, d // 8), 'adam_c_ctx')
    res['c_ctx'] = [o.reshape(d) for o in res['c_ctx']]
    res['mod_w'] = upd(mod_w, [g_mod_w], m_mod_w, v_mod_w, 'adam_mod_w')
    res['mod_b'] = upd(mod_b, [gmb.reshape(2, 6 * d)], m_mod_b, v_mod_b, 'adam_mod_b')
    g_ng = lax.dynamic_slice_in_dim(small_g[0:4].reshape(2, 2, d), s * (d // 4), d // 4, axis=2)
    res['norm_g'] = upd(norm_g, [g_ng], m_norm_g, v_norm_g, 'adam_norm_g')
    g_qk = jnp.stack([small_g[4, 0:64], small_g[5, 0:64]]).reshape(1, 2, 64)
    res['attn_qk_norm_g'] = upd(attn_qk_norm_g, [g_qk], m_attn_qk_norm_g, v_attn_qk_norm_g, 'adam_qk_gain')
    res['attn_sink'] = upd(attn_sink, [small_g[7, 0:8].reshape(1, 8)], m_attn_sink, v_attn_sink, 'adam_sink')
    res['hgrn_out_norm_g'] = upd(hgrn_out_norm_g, [small_g[6, 0:128].reshape(1, 128)], m_hgrn_out_norm_g, v_hgrn_out_norm_g, 'adam_head_gain')
    res['hgrn_lb'] = upd(hgrn_lb, [glb[0:2, 0:hgrn_lb.shape[1]]], m_hgrn_lb, v_hgrn_lb, 'adam_hgrn_lb')
    res['odd_w_in'] = upd(odd_w_in, [full['odd_in']], m_odd_w_in, v_odd_w_in, 'adam_odd_in')
    res['odd_w_out'] = upd(odd_w_out, [full['odd_out']], m_odd_w_out, v_odd_w_out, 'adam_odd_out')
    full.update(reducer.finish('mid', res['odd_w_in'][1]))
    g_ffn_in = jnp.concatenate([full['ffn_in0'], full['ffn_in1']], axis=0)
    g_ffn_out = jnp.concatenate([full['ffn_out0'], full['ffn_out1']], axis=0)
    res['ffn_w_in'] = upd(ffn_w_in, [g_ffn_in], m_ffn_w_in, v_ffn_w_in, 'adam_ffn_in')
    res['ffn_w_out'] = upd(ffn_w_out, [g_ffn_out], m_ffn_w_out, v_ffn_w_out, 'adam_ffn_out')
    full.update(reducer.finish('late', res['ffn_w_in'][1]))
    res['even_w_in'] = upd(even_w_in, [full['even_in']], m_even_w_in, v_even_w_in, 'adam_even_in')
    res['even_w_out'] = upd(even_w_out, [full['even_out']], m_even_w_out, v_even_w_out, 'adam_even_out')

    order = ['c_ctx', 'mod_w', 'mod_b', 'norm_g', 'ffn_w_in', 'ffn_w_out', 'even_w_in', 'even_w_out',
             'attn_qk_norm_g', 'attn_sink', 'hgrn_out_norm_g', 'hgrn_lb', 'odd_w_in', 'odd_w_out']
    outs = [loss, grad_x]
    for k in range(4):
        outs += [res[nm][k] for nm in order]
    return tuple(outs)
```

```python
import functools
import math

import numpy as np
import jax
import jax.numpy as jnp
from jax import lax
from jax.experimental import pallas as pl
from jax.experimental.pallas import tpu as pltpu

F32 = jnp.float32
BF16 = jnp.bfloat16
EPS = 1e-6
TM = 256
CHUNK = 64
QB = 256
WINDOW = 128
NEG = -1e30
MESH = pl.DeviceIdType.MESH

ADAM_LR, ADAM_B1, ADAM_B2, ADAM_EPS, ADAM_WD, ADAM_STEP = 0.001, 0.9, 0.999, 1e-08, 0.01, 10


def _pcall(body, **kw):
    return pl.pallas_call(body, **kw)


def _pick(n, cap):
    best = None
    for m in range(128, min(n, cap) + 1, 128):
        if n % m == 0:
            best = m
    assert best is not None, (n, cap)
    return best


def _bf(x):
    return x.astype(BF16)


def _dot(a, b):
    return jnp.dot(_bf(a), _bf(b), preferred_element_type=F32)


def _dot_nt(a, b):
    return lax.dot_general(_bf(a), _bf(b), (((1,), (1,)), ((), ())), preferred_element_type=F32)


def _dot_tn(a, b):
    return lax.dot_general(_bf(a), _bf(b), (((0,), (0,)), ((), ())), preferred_element_type=F32)


def _dot_exact(a, b):
    return jnp.dot(a, b, preferred_element_type=F32, precision=lax.Precision.HIGHEST)


def _sigmoid(x):
    return 1.0 / (1.0 + jnp.exp(-x))


def _iota(shape, dim):
    return lax.broadcasted_iota(jnp.int32, shape, dim)


def _parts(a):
    parts = list(a) if isinstance(a, (list, tuple)) else [a]
    widths = [p.shape[1] for p in parts]
    return parts, widths, [sum(widths[:i]) for i in range(len(parts))]


def _mm_nn(a, b, *, lead=None, out_dtype=F32, name):
    parts, widths, offs = _parts(a)
    m, k = parts[0].shape[0], sum(widths)
    n = b.shape[-1]
    bm = 1408 if (m % 1408 == 0 and k <= 1024) else (768 if m % 768 == 0 else TM)
    bn = _pick(n, 1024) if n % 512 == 0 else _pick(n, 1664)

    def body(*refs):
        b_ref, o_ref = refs[-2], refs[-1]
        acc = None
        for p_ref, w, off in zip(refs, widths, offs):
            term = _dot(p_ref[...], b_ref[off:off + w, :])
            acc = term if acc is None else acc + term
        o_ref[...] = acc.astype(o_ref.dtype)

    if lead is None:
        b_spec = pl.BlockSpec((k, bn), lambda i, j: (0, j))
    else:
        b_spec = pl.BlockSpec((None, k, bn), lambda i, j: (lead, 0, j))
    return _pcall(
        body, name=name, grid=(m // bm, n // bn),
        in_specs=[pl.BlockSpec((bm, w), lambda i, j: (i, 0)) for w in widths] + [b_spec],
        out_specs=pl.BlockSpec((bm, bn), lambda i, j: (i, j)),
        out_shape=jax.ShapeDtypeStruct((m, n), out_dtype),
    )(*parts, b)


def _mm_nt(a, b, *, lead=None, name):
    parts, widths, offs = _parts(a)
    m, n = parts[0].shape[0], sum(widths)
    k = b.shape[-2]
    bm = 1408 if (m % 1408 == 0 and n <= 1024) else (768 if m % 768 == 0 else TM)
    bk = _pick(k, 1024 if n <= 2048 else 512)

    def body(*refs):
        b_ref, o_ref = refs[-2], refs[-1]
        acc = None
        for p_ref, w, off in zip(refs, widths, offs):
            term = _dot_nt(p_ref[...], b_ref[:, off:off + w])
            acc = term if acc is None else acc + term
        o_ref[...] = acc

    if lead is None:
        b_spec = pl.BlockSpec((bk, n), lambda i, j: (j, 0))
    else:
        b_spec = pl.BlockSpec((None, bk, n), lambda i, j: (lead, j, 0))
    return _pcall(
        body, name=name, grid=(m // bm, k // bk),
        in_specs=[pl.BlockSpec((bm, w), lambda i, j: (i, 0)) for w in widths] + [b_spec],
        out_specs=pl.BlockSpec((bm, bk), lambda i, j: (i, j)),
        out_shape=jax.ShapeDtypeStruct((m, k), F32),
    )(*parts, b)


def _mm_tn(a, b, *, name):
    a_parts, a_w, a_off = _parts(a)
    b_parts, b_w, b_off = _parts(b)
    t, k, n = a_parts[0].shape[0], sum(a_w), sum(b_w)
    bt = 1408 if t % 1408 == 0 else (768 if t % 768 == 0 else TM)
    bk = _pick(k, 1536) if len(a_parts) == 1 else math.gcd(*a_w)
    if len(b_parts) == 1:
        bn = _pick(n, 1024) if n % 1024 == 0 or n < 1664 else _pick(n, 1664)
    else:
        bn = math.gcd(*b_w)
    na, nbp = len(a_parts), len(b_parts)

    def block_range(off, w, blk):
        return off // blk, w // blk

    def body(*refs):
        a_refs, b_refs, o_ref = refs[:na], refs[na:na + nbp], refs[-1]
        i, j = pl.program_id(0), pl.program_id(1)

        @pl.when(pl.program_id(2) == 0)
        def _():
            o_ref[...] = jnp.zeros_like(o_ref)

        def add(a_ref, b_ref):
            o_ref[...] += _dot_tn(a_ref[...], b_ref[...])

        for pa in range(na):
            sa, ca = block_range(a_off[pa], a_w[pa], bk)
            for pb in range(nbp):
                sb, cb = block_range(b_off[pb], b_w[pb], bn)
                if na == 1 and nbp == 1:
                    add(a_refs[0], b_refs[0])
                else:
                    pl.when((i >= sa) & (i < sa + ca) & (j >= sb) & (j < sb + cb))(
                        functools.partial(add, a_refs[pa], b_refs[pb]))

    def spec(off, w, blk, axis):
        s0, cnt = block_range(off, w, blk)

        def index(i, j, s):
            g = i if axis == 0 else j
            inside = (g >= s0) & (g < s0 + cnt)
            return (jnp.where(inside, s, 0), jnp.clip(g - s0, 0, cnt - 1))

        return pl.BlockSpec((bt, blk), index)

    return _pcall(
        body, name=name, grid=(k // bk, n // bn, t // bt),
        in_specs=[spec(o, w, bk, 0) for o, w in zip(a_off, a_w)] + [spec(o, w, bn, 1) for o, w in zip(b_off, b_w)],
        out_specs=pl.BlockSpec((bk, bn), lambda i, j, s: (i, j)),
        out_shape=jax.ShapeDtypeStruct((k, n), F32),
    )(*a_parts, *b_parts)


def _mod_row(mods_ref, lat, idx):
    return jnp.where(lat, mods_ref[idx + 6:idx + 7, :], mods_ref[idx:idx + 1, :])


def _row_step(t):
    return 768 if t % 768 == 0 else TM


def _row_fwd(x, mods, *, y=None, gate=None, g=None, shift=None, scale=None, name):
    t, d = x.shape
    has_y, has_n = y is not None, g is not None
    rt = _row_step(t)

    def body(*refs):
        refs = list(refs)
        x_ref, mods_ref = refs[0], refs[1]
        pos = 2
        if has_y:
            y_ref = refs[pos]; pos += 1
        if has_n:
            g_ref = refs[pos]; pos += 1
        outs = refs[pos:]
        for sub in range(rt // TM):
            rows = slice(sub * TM, (sub + 1) * TM)
            lat = pl.program_id(0) * (rt // TM) + sub > 0
            x1 = x_ref[rows, :]
            o = 0
            if has_y:
                x1 = x1 + _mod_row(mods_ref, lat, gate) * y_ref[rows, :]
                outs[o][rows, :] = x1; o += 1
            if has_n:
                rs = lax.rsqrt(jnp.mean(x1 * x1, axis=-1, keepdims=True) + EPS)
                hn = x1 * rs * g_ref[...]
                h = hn * (1.0 + _mod_row(mods_ref, lat, scale)) + _mod_row(mods_ref, lat, shift)
                outs[o][rows, :] = h.astype(BF16)

    row = pl.BlockSpec((rt, d), lambda i: (i, 0))
    ins, specs = [x, mods], [row, pl.BlockSpec(mods.shape, lambda i: (0, 0))]
    if has_y:
        ins.append(y); specs.append(row)
    if has_n:
        ins.append(g.reshape(1, d)); specs.append(pl.BlockSpec((1, d), lambda i: (0, 0)))
    out_shape, out_specs = [], []
    if has_y:
        out_shape.append(jax.ShapeDtypeStruct((t, d), F32)); out_specs.append(row)
    if has_n:
        out_shape.append(jax.ShapeDtypeStruct((t, d), BF16)); out_specs.append(row)
    res = _pcall(body, name=name, grid=(t // rt,), in_specs=specs, out_specs=out_specs,
                 out_shape=out_shape)(*ins)
    return res


def _acc_row(ref, r, val):
    ref[r:r + 1, :] += val


def _row_final(x, z, mods, target, *, gate, name):
    t, d = x.shape
    rt = _row_step(t)
    nsub = rt // TM

    def body(*refs):
        x_ref, mods_ref, z_ref = refs[:3]
        t_refs = refs[3:3 + nsub]
        loss_ref, dx_ref, dz_ref, sums_ref = refs[3 + nsub:]
        i = pl.program_id(0)

        @pl.when(i == 0)
        def _():
            loss_ref[...] = jnp.zeros_like(loss_ref)
            sums_ref[...] = jnp.zeros_like(sums_ref)

        for sub in range(nsub):
            rows = slice(sub * TM, (sub + 1) * TM)
            lat = i * nsub + sub > 0
            gt = _mod_row(mods_ref, lat, gate)
            zz = z_ref[rows, :]
            yv = x_ref[rows, :] + gt * zz
            keep = jnp.where(lat, 1.0, 0.0).astype(F32)
            diff = (yv - t_refs[sub][...]) * keep
            part = jnp.sum(jnp.sum(diff * diff, axis=0, keepdims=True), axis=1, keepdims=True)
            loss_ref[...] += part * (0.5 / d)
            dy = diff * (1.0 / d)
            dx_ref[rows, :] = dy
            dz_ref[rows, :] = (gt * dy).astype(BF16)
            _acc_row(sums_ref, 6, jnp.sum(dy * zz, axis=0, keepdims=True))

    row = pl.BlockSpec((rt, d), lambda i: (i, 0))
    tgt = [pl.BlockSpec((TM, d), lambda i, sub=sub: (jnp.maximum(i * nsub + sub - 1, 0), 0)) for sub in range(nsub)]
    return _pcall(
        body, name=name, grid=(t // rt,),
        in_specs=[row, pl.BlockSpec(mods.shape, lambda i: (0, 0)), row] + tgt,
        out_specs=[pl.BlockSpec((8, 128), lambda i: (0, 0)), row, row,
                   pl.BlockSpec((8, d), lambda i: (0, 0))],
        out_shape=[jax.ShapeDtypeStruct((8, 128), F32), jax.ShapeDtypeStruct((t, d), F32),
                   jax.ShapeDtypeStruct((t, d), BF16), jax.ShapeDtypeStruct((8, d), F32)],
    )(x, mods, z, *([target] * nsub))


def _row_bwd(xn, dxo, dh, mods, g, *, shift, scale, y=None, gate=None, latent_only=False, name):
    t, d = xn.shape
    has_y = y is not None

    def body(*refs):
        refs = list(refs)
        x_ref, dxo_ref, dh_ref, mods_ref, g_ref = refs[:5]
        pos = 5
        if has_y:
            y_ref = refs[pos]; pos += 1
        dx_ref = refs[pos]; pos += 1
        if has_y:
            dy_ref = refs[pos]; pos += 1
        sums_ref = refs[pos]
        i = pl.program_id(0)

        @pl.when(i == 0)
        def _():
            sums_ref[...] = jnp.zeros_like(sums_ref)

        def add_sums(vals, base):
            for r, v in enumerate(vals):
                if v is not None:
                    _acc_row(sums_ref, base + r, v)

        gv = g_ref[...]
        for sub in range(rt // TM):
            rows = slice(sub * TM, (sub + 1) * TM)
            lat = i * (rt // TM) + sub > 0
            x1 = x_ref[rows, :]
            rs = lax.rsqrt(jnp.mean(x1 * x1, axis=-1, keepdims=True) + EPS)
            xh = x1 * rs
            dhv = dh_ref[rows, :]
            dn = dhv * (1.0 + _mod_row(mods_ref, lat, scale))
            dxh = dn * gv
            dx = dxo_ref[rows, :] + rs * (dxh - xh * jnp.mean(dxh * xh, axis=-1, keepdims=True))
            dx_ref[rows, :] = dx
            vals = [jnp.sum(dhv, axis=0, keepdims=True),
                    jnp.sum(dhv * (xh * gv), axis=0, keepdims=True),
                    None,
                    jnp.sum(dn * xh, axis=0, keepdims=True)]
            if has_y:
                dy_ref[rows, :] = (_mod_row(mods_ref, lat, gate) * dx).astype(BF16)
                vals[2] = jnp.sum(dx * y_ref[rows, :], axis=0, keepdims=True)
            if sub == 0:
                pl.when(i == 0)(functools.partial(add_sums, vals, 0))
                pl.when(i > 0)(functools.partial(add_sums, vals, 4))
            else:
                add_sums(vals, 4)

    rt = TM if latent_only else _row_step(t)
    row = pl.BlockSpec((rt, d), lambda i: (i, 0))
    ins = [xn, dxo, dh, mods, g.reshape(1, d)]
    specs = [row, row, row, pl.BlockSpec(mods.shape, lambda i: (0, 0)), pl.BlockSpec((1, d), lambda i: (0, 0))]
    if latent_only:
        out_shape = [jax.ShapeDtypeStruct((t - TM, d), F32)]
        out_specs = [pl.BlockSpec((TM, d), lambda i: (jnp.maximum(i - 1, 0), 0))]
    else:
        out_shape, out_specs = [jax.ShapeDtypeStruct((t, d), F32)], [row]
    if has_y:
        ins.append(y); specs.append(row)
        out_shape.append(jax.ShapeDtypeStruct((t, d), BF16)); out_specs.append(row)
    out_shape.append(jax.ShapeDtypeStruct((8, d), F32))
    out_specs.append(pl.BlockSpec((8, d), lambda i: (0, 0)))
    return _pcall(body, name=name, grid=(t // rt,), in_specs=specs, out_specs=out_specs,
                  out_shape=out_shape)(*ins)


FFN_BK = 1408


FFN_SUB = 256


def _ffn_order(n2):
    nb = n2 // (2 * FFN_BK)
    return [h * nb + j for j in range(nb) for h in (0, 1)]


def _ffn_interleave(w):
    return jnp.concatenate([w[..., b * FFN_BK:(b + 1) * FFN_BK] for b in _ffn_order(w.shape[-1])], axis=-1)


def _ffn_deinterleave(w):
    order = _ffn_order(w.shape[-1])
    return jnp.concatenate([w[..., order.index(b) * FFN_BK:(order.index(b) + 1) * FFN_BK]
                            for b in range(len(order))], axis=-1)


def _big_tile(t):
    return 768 if t % 768 == 0 else TM


def _ffn_in(h, w, *, lead, name):
    t, d = h.shape
    n2 = w.shape[-1]
    bm, bk = _big_tile(t), FFN_BK

    def body(h_ref, w_ref, u_ref, a_ref):
        hb = h_ref[...]
        for c0 in range(0, bk, FFN_SUB):
            c1 = min(c0 + FFN_SUB, bk)
            ug = _dot(hb, w_ref[:, c0:c1]).astype(BF16)
            uu = _dot(hb, w_ref[:, bk + c0:bk + c1]).astype(BF16)
            u_ref[:, c0:c1] = ug
            u_ref[:, bk + c0:bk + c1] = uu
            gv, up = ug.astype(F32), uu.astype(F32)
            a_ref[:, c0:c1] = (gv * _sigmoid(gv) * up).astype(BF16)

    return _pcall(
        body, name=name, grid=(t // bm, n2 // (2 * bk)),
        in_specs=[pl.BlockSpec((bm, d), lambda i, j: (i, 0)),
                  pl.BlockSpec((None, d, 2 * bk), lambda i, j: (lead, 0, j))],
        out_specs=[pl.BlockSpec((bm, 2 * bk), lambda i, j: (i, j)), pl.BlockSpec((bm, bk), lambda i, j: (i, j))],
        out_shape=[jax.ShapeDtypeStruct((t, n2), BF16), jax.ShapeDtypeStruct((t, n2 // 2), BF16)],
    )(h, w)


def _ffn_dx(dz, w_out, u, *, lead, name):
    t, d = dz.shape
    n2 = u.shape[1]
    bm, bk = _big_tile(t), FFN_BK

    def body(dz_ref, w_ref, u_ref, du_ref):
        dzb = dz_ref[...]
        for c0 in range(0, bk, FFN_SUB):
            c1 = min(c0 + FFN_SUB, bk)
            da = _dot_nt(dzb, w_ref[c0:c1, :])
            gv, up = u_ref[:, c0:c1].astype(F32), u_ref[:, bk + c0:bk + c1].astype(F32)
            s = _sigmoid(gv)
            du_ref[:, c0:c1] = (da * up * (s * (1.0 + gv * (1.0 - s)))).astype(BF16)
            du_ref[:, bk + c0:bk + c1] = (da * gv * s).astype(BF16)

    ublk = pl.BlockSpec((bm, 2 * bk), lambda i, j: (i, j))
    return _pcall(
        body, name=name, grid=(t // bm, n2 // (2 * bk)),
        in_specs=[pl.BlockSpec((bm, d), lambda i, j: (i, 0)),
                  pl.BlockSpec((None, bk, d), lambda i, j: (lead, j, 0)), ublk],
        out_specs=ublk, out_shape=jax.ShapeDtypeStruct((t, n2), BF16),
    )(dz, w_out, u)


def _lane(shape):
    return _iota(shape, len(shape) - 1)


def _pair_norm(x, g):
    lo = _lane(x.shape) < 64
    x2 = x * x
    s_lo = jnp.sum(jnp.where(lo, x2, 0.0), axis=-1, keepdims=True)
    s_hi = jnp.sum(jnp.where(lo, 0.0, x2), axis=-1, keepdims=True)
    rs = lax.rsqrt(jnp.where(lo, s_lo, s_hi) * (1.0 / 64) + EPS)
    return x * rs, rs


def _pair_mean(v):
    lo = _lane(v.shape) < 64
    s_lo = jnp.sum(jnp.where(lo, v, 0.0), axis=-1, keepdims=True)
    s_hi = jnp.sum(jnp.where(lo, 0.0, v), axis=-1, keepdims=True)
    return jnp.where(lo, s_lo, s_hi) * (1.0 / 64)


def _rot64(x):
    r1 = pltpu.roll(x, 32, 1)
    r2 = pltpu.roll(x, 96, 1)
    even = ((_lane(x.shape) >> 5) & 1) == 0
    return jnp.where(even, -r2, r1)


def _rope64(x, cos, sin):
    return x * cos + _rot64(x) * sin


def _rope64_t(d, cos, sin):
    return d * cos - _rot64(d * sin)


def _kprep_fwd(p, gk, cos, sin, *, name):
    t = p.shape[0]

    def body(k_ref, g_ref, c_ref, s_ref, o_ref):
        xh, _ = _pair_norm(k_ref[...], None)
        o_ref[...] = _rope64(xh * g_ref[...], c_ref[...], s_ref[...])

    blk = pl.BlockSpec((TM, 128), lambda i: (i, 0))
    return _pcall(
        body, name=name, grid=(t // TM,),
        in_specs=[pl.BlockSpec((TM, 128), lambda i: (i, 4)), pl.BlockSpec((1, 128), lambda i: (0, 0)), blk, blk],
        out_specs=blk, out_shape=jax.ShapeDtypeStruct((t, 128), F32),
    )(p, gk, cos, sin)


def _kprep_bwd(p, gk, cos, sin, dkp, dv, *, name):
    t = p.shape[0]

    def body(k_ref, g_ref, c_ref, s_ref, dkp_ref, dv_ref, o_ref, dg_ref):
        @pl.when(pl.program_id(0) == 0)
        def _():
            dg_ref[...] = jnp.zeros_like(dg_ref)
        xh, rs = _pair_norm(k_ref[...], None)
        dn = _rope64_t(dkp_ref[...], c_ref[...], s_ref[...])
        _acc_row(dg_ref, 0, jnp.sum(dn * xh, axis=0, keepdims=True))
        dxh = dn * g_ref[...]
        o_ref[:, 0:128] = (rs * (dxh - xh * _pair_mean(dxh * xh))).astype(BF16)
        o_ref[:, 128:256] = dv_ref[...].astype(BF16)

    blk = pl.BlockSpec((TM, 128), lambda i: (i, 0))
    return _pcall(
        body, name=name, grid=(t // TM,),
        in_specs=[pl.BlockSpec((TM, 128), lambda i: (i, 4)), pl.BlockSpec((1, 128), lambda i: (0, 0)), blk, blk, blk, blk],
        out_specs=[pl.BlockSpec((TM, 256), lambda i: (i, 0)), pl.BlockSpec((8, 128), lambda i: (0, 0))],
        out_shape=[jax.ShapeDtypeStruct((t, 256), BF16), jax.ShapeDtypeStruct((8, 128), F32)],
    )(p, gk, cos, sin, dkp, dv)


def _attn_common(i, t, lc, kp_ref, v_ref):
    span = QB + 2 * WINDOW
    start = pl.multiple_of(jnp.clip(i * QB - WINDOW, lc, t - span), WINDOW)
    kall = jnp.concatenate([kp_ref[0:lc, :], kp_ref[pl.ds(start, span), :]], axis=0)
    vall = jnp.concatenate([v_ref[0:lc, :], v_ref[pl.ds(start, span), :]], axis=0)
    nk = lc + span
    col = _iota((QB, nk), 1)
    krow = jnp.where(col < lc, col, start + col - lc)
    qrow = i * QB + _iota((QB, nk), 0)
    valid = (col < lc) | ((qrow >= lc) & (krow >= lc) & (jnp.abs(krow - qrow) <= WINDOW))
    lo = _lane(kall.shape) < 64
    kroll, vroll = pltpu.roll(kall, 64, 1), pltpu.roll(vall, 64, 1)
    zero = jnp.zeros_like(kall)
    kvar = [[_bf(jnp.where(lo, kall, zero)), _bf(jnp.where(lo, zero, kroll))],
            [_bf(jnp.where(lo, kroll, zero)), _bf(jnp.where(lo, zero, kall))]]
    vvar = [[_bf(jnp.where(lo, vall, zero)), _bf(jnp.where(lo, zero, vroll))],
            [_bf(jnp.where(lo, vroll, zero)), _bf(jnp.where(lo, zero, vall))]]
    return start, valid, kvar, vvar


def _softmax_sink(s, valid, snk):
    s = jnp.where(valid, s, NEG)
    m = jnp.maximum(jnp.max(s, axis=-1, keepdims=True), snk)
    e = jnp.exp(s - m)
    es = jnp.exp(snk - m)
    inv = 1.0 / (jnp.sum(e, axis=-1, keepdims=True) + es)
    return e * inv, es * inv


def _attn_fwd(p, kp, gq, sink, cos, sin, *, lc, name):
    t = p.shape[0]
    scale = 64 ** -0.5

    def body(q_ref, kp_ref, v_ref, g_ref, sink_ref, c_ref, s_ref, o_ref):
        i = pl.program_id(0)
        _, valid, kvar, vvar = _attn_common(i, t, lc, kp_ref, v_ref)
        cosv, sinv, gv = c_ref[...], s_ref[...], g_ref[...]
        for j in range(4):
            xh, _ = _pair_norm(q_ref[:, 128 * j:128 * j + 128], None)
            q2 = _bf(_rope64(xh * gv, cosv, sinv) * scale)
            acc = jnp.zeros((QB, 128), F32)
            for half in range(2):
                s = _dot_nt(q2, kvar[j // 2][half])
                pr, _ = _softmax_sink(s, valid, sink_ref[2 * j + half])
                acc = acc + _dot(pr, vvar[j // 2][half])
            o_ref[:, 128 * j:128 * j + 128] = acc.astype(BF16)

    qblk = pl.BlockSpec((QB, 128), lambda i: (i, 0))
    return _pcall(
        body, name=name, grid=(t // QB,),
        in_specs=[pl.BlockSpec((QB, 512), lambda i: (i, 0)),
                  pl.BlockSpec((t, 128), lambda i: (0, 0)),
                  pl.BlockSpec((t, 128), lambda i: (0, 5)),
                  pl.BlockSpec((1, 128), lambda i: (0, 0)),
                  pl.BlockSpec(memory_space=pltpu.SMEM), qblk, qblk],
        out_specs=pl.BlockSpec((QB, 512), lambda i: (i, 0)),
        out_shape=jax.ShapeDtypeStruct((t, 512), BF16),
    )(p, kp, p, gq, sink, cos, sin)


def _attn_bwd(p, kp, gq, sink, cos, sin, dmix, *, lc, name):
    t = p.shape[0]
    scale = 64 ** -0.5
    span = QB + 2 * WINDOW

    def body(q_ref, kp_ref, v_ref, g_ref, sink_ref, c_ref, s_ref, do_ref,
             dq_ref, dk_ref, dv_ref, dg_ref, dsink_ref):
        i = pl.program_id(0)

        @pl.when(i == 0)
        def _():
            dk_ref[...] = jnp.zeros_like(dk_ref)
            dv_ref[...] = jnp.zeros_like(dv_ref)
            dg_ref[...] = jnp.zeros_like(dg_ref)
            dsink_ref[...] = jnp.zeros_like(dsink_ref)

        start, valid, kvar, vvar = _attn_common(i, t, lc, kp_ref, v_ref)
        cosv, sinv, gv = c_ref[...], s_ref[...], g_ref[...]
        nk = lc + span
        dkt = [jnp.zeros((64, nk), F32), jnp.zeros((64, nk), F32)]
        dvt = [jnp.zeros((64, nk), F32), jnp.zeros((64, nk), F32)]
        for j in range(4):
            kvh = j // 2
            xh, rs = _pair_norm(q_ref[:, 128 * j:128 * j + 128], None)
            q2 = _bf(_rope64(xh * gv, cosv, sinv) * scale)
            do2 = _bf(do_ref[:, 128 * j:128 * j + 128])
            dq2 = jnp.zeros((QB, 128), F32)
            for half in range(2):
                s = _dot_nt(q2, kvar[kvh][half])
                pr, ps = _softmax_sink(s, valid, sink_ref[2 * j + half])
                dp = _dot_nt(do2, vvar[kvh][half])
                delta = jnp.sum(pr * dp, axis=-1, keepdims=True)
                ds = pr * (dp - delta)
                dsk = jnp.sum(jnp.sum(-ps * delta, axis=0, keepdims=True), axis=1, keepdims=True)
                _acc_row(dsink_ref, 2 * j + half, jnp.broadcast_to(dsk, (1, 128)))
                dq2 = dq2 + _dot(ds, kvar[kvh][half])
                hrows = slice(64 * half, 64 * half + 64)
                dkt[kvh] = dkt[kvh] + _dot_tn(q2, ds)[hrows]
                dvt[kvh] = dvt[kvh] + _dot_tn(do2, pr)[hrows]
            dn = _rope64_t(dq2 * scale, cosv, sinv)
            _acc_row(dg_ref, 0, jnp.sum(dn * xh, axis=0, keepdims=True))
            dxh = dn * gv
            dq_ref[:, 128 * j:128 * j + 128] = (rs * (dxh - xh * _pair_mean(dxh * xh))).astype(BF16)
        dk_all = jnp.concatenate(dkt, axis=0).T
        dv_all = jnp.concatenate(dvt, axis=0).T
        dk_ref[0:lc, :] += dk_all[0:lc]
        dv_ref[0:lc, :] += dv_all[0:lc]
        dk_ref[pl.ds(start, span), :] += dk_all[lc:nk]
        dv_ref[pl.ds(start, span), :] += dv_all[lc:nk]

    qblk = pl.BlockSpec((QB, 128), lambda i: (i, 0))
    full = pl.BlockSpec((t, 128), lambda i: (0, 0))
    small = pl.BlockSpec((8, 128), lambda i: (0, 0))
    return _pcall(
        body, name=name, grid=(t // QB,),
        in_specs=[pl.BlockSpec((QB, 512), lambda i: (i, 0)), full,
                  pl.BlockSpec((t, 128), lambda i: (0, 5)),
                  pl.BlockSpec((1, 128), lambda i: (0, 0)),
                  pl.BlockSpec(memory_space=pltpu.SMEM), qblk, qblk,
                  pl.BlockSpec((QB, 512), lambda i: (i, 0))],
        out_specs=[pl.BlockSpec((QB, 512), lambda i: (i, 0)), full, full, small, small],
        out_shape=[jax.ShapeDtypeStruct((t, 512), BF16), jax.ShapeDtypeStruct((t, 128), F32),
                   jax.ShapeDtypeStruct((t, 128), F32), jax.ShapeDtypeStruct((8, 128), F32),
                   jax.ShapeDtypeStruct((8, 128), F32)],
    )(p, kp, p, gq, sink, cos, sin, dmix)


def _tri(rev):
    r, c = _iota((CHUNK, CHUNK), 0), _iota((CHUNK, CHUNK), 1)
    return (c >= r) if rev else (c <= r)


def _blk_map(nb, rev, backward):
    if not rev:
        return (lambda n: nb - 1 - n) if backward else (lambda n: n)
    if backward:
        return lambda n: jnp.where(n < nb - 1, n + 1, 0)
    return lambda n: jnp.where(n == 0, 0, nb - n)


def _chunk_order(rev, backward, nc=TM // CHUNK):
    order = list(range(nc))
    return order[::-1] if (rev != backward) else order


def _hgrn_gates(qraw, fraw, lb):
    sq = _sigmoid(qraw)
    sf = _sigmoid(fraw)
    f = lb + (1.0 - lb) * sf
    return qraw * sq, 1.0 - f, jnp.log(f), sq, sf, f


HGRN_HP = 4


def _chunk_cumsum(x, rev):
    n = x.shape[0]
    pos = _iota(x.shape, 0) & (CHUNK - 1)
    s = 1
    while s < CHUNK:
        if rev:
            x = x + jnp.where(pos < CHUNK - s, pltpu.roll(x, n - s, 0), 0.0)
        else:
            x = x + jnp.where(pos >= s, pltpu.roll(x, s, 0), 0.0)
        s *= 2
    return x


def _block_terms(lf, rev):
    b = _chunk_cumsum(lf, rev)
    mid, last = (CHUNK // 2 - 1, 0) if rev else (CHUNK // 2, CHUNK - 1)

    def chunk_row(off):
        return jnp.concatenate([jnp.broadcast_to(b[c * CHUNK + off:c * CHUNK + off + 1, :], (CHUNK, b.shape[1]))
                                for c in range(TM // CHUNK)], axis=0)

    r, bl = chunk_row(mid), chunk_row(last)
    return _tri(rev), jnp.exp(b - r), jnp.exp(r - b), jnp.exp(b), jnp.exp(bl - b), jnp.exp(bl)


def _headnorm_apply(o, gv, gain):
    n = o * lax.rsqrt(jnp.mean(o * o, axis=-1, keepdims=True) + EPS)
    if gain is not None:
        n = n * gain
    return (n * (gv * _sigmoid(gv))).astype(BF16)


def _headnorm_grad(o, gv, dy, gain):
    rs = lax.rsqrt(jnp.mean(o * o, axis=-1, keepdims=True) + EPS)
    xh = o * rs
    n = xh * gain if gain is not None else xh
    sg = _sigmoid(gv)
    dn = dy * (gv * sg)
    dg = (dy * n * (sg * (1.0 + gv * (1.0 - sg)))).astype(BF16)
    dgain = jnp.sum(dn * xh, axis=0, keepdims=True)
    dxh = dn * gain if gain is not None else dn
    return rs * (dxh - xh * jnp.mean(dxh * xh, axis=-1, keepdims=True)), dg, dgain


def _hgrn_cols(bmap, n2, c0):
    return [pl.BlockSpec((TM, 256), lambda h, n, b=b: (bmap(n), c0 // 2 + h * n2 + b)) for b in range(n2)]


def _head_cols(refs, hh):
    return refs[hh // 2][:, 128 * (hh % 2):128 * (hh % 2) + 128]


def _hgrn_fwd(p, lb, *, rev, name, ofw=None, gain=None):
    t = p.shape[0]
    nb, nc = t // TM, TM // CHUNK
    bmap = _blk_map(nb, rev, False)
    fcol = 14 if rev else 10
    fused = ofw is not None

    n2 = HGRN_HP // 2

    def body(*refs):
        q_refs, f_refs, v_refs, lb_ref = refs[:n2], refs[n2:2 * n2], refs[2 * n2:3 * n2], refs[3 * n2]
        rest = refs[3 * n2 + 1:]
        if fused:
            ofw_ref, g_refs, gain_ref = rest[0], rest[1:1 + n2], rest[1 + n2]
            o_ref, sh_ref, mix_ref, st = rest[2 + n2:]
        else:
            o_ref, sh_ref, st = rest

        @pl.when(pl.program_id(1) == 0)
        def _():
            st[...] = jnp.zeros_like(st)
        for hh in range(HGRN_HP):
            ln = slice(128 * hh, 128 * hh + 128)
            q, k, lf, _, _, _ = _hgrn_gates(_head_cols(q_refs, hh), _head_cols(f_refs, hh), lb_ref[:, ln])
            tri, eq, ek, ei, eki, eb = _block_terms(lf, rev)
            qe, ke, qi, ki, vb = _bf(q * eq), _bf(k * ek), _bf(q * ei), _bf(k * eki), _bf(_head_cols(v_refs, hh))
            intra = []
            for cc in range(nc):
                rows = slice(cc * CHUNK, (cc + 1) * CHUNK)
                a = jnp.where(tri, _dot_nt(qe[rows], ke[rows]), 0.0)
                intra.append(_dot(a, vb[rows]))
            s = st[hh]
            for cc in _chunk_order(rev, False):
                rows = slice(cc * CHUNK, (cc + 1) * CHUNK)
                sh_ref[hh, cc] = s.astype(sh_ref.dtype)
                o_ref[rows, ln] = intra[cc] + _dot_nt(qi[rows], s)
                s = s * eb[cc * CHUNK:cc * CHUNK + 1, :] + _dot_tn(vb[rows], ki[rows])
            st[hh] = s
            if fused:
                osum = o_ref[:, ln] + ofw_ref[:, ln]
                o_ref[:, ln] = osum
                mix_ref[:, ln] = _headnorm_apply(osum, _head_cols(g_refs, hh), gain_ref[...])

    hp, wd = HGRN_HP, 128 * HGRN_HP
    col = functools.partial(_hgrn_cols, bmap, n2)
    oblk = pl.BlockSpec((TM, wd), lambda h, n: (bmap(n), h))
    ins = [p] * (3 * n2) + [lb]
    specs = col(6) + col(fcol) + col(18) + [pl.BlockSpec((1, wd), lambda h, n: (0, h))]
    out_specs = [oblk, pl.BlockSpec((hp, nc, 128, 128), lambda h, n: (h, bmap(n), 0, 0))]
    out_shape = [jax.ShapeDtypeStruct((t, 512), F32), jax.ShapeDtypeStruct((4, t // CHUNK, 128, 128), BF16)]
    if fused:
        ins += [ofw] + [p] * n2 + [gain]
        specs += [oblk] + col(22) + [pl.BlockSpec((1, 128), lambda h, n: (0, 0))]
        out_specs.append(oblk)
        out_shape.append(jax.ShapeDtypeStruct((t, 512), BF16))
    return _pcall(body, name=name, grid=(4 // hp, nb), in_specs=specs, out_specs=out_specs, out_shape=out_shape,
                  scratch_shapes=[pltpu.VMEM((hp, 128, 128), F32)])(*ins)


def _hgrn_bwd(p, lb, sh, do, prev, *, rev, name, head=None):
    t = p.shape[0]
    nb, nc = t // TM, TM // CHUNK
    bmap = _blk_map(nb, rev, True)
    fcol = 14 if rev else 10
    has_prev = prev is not None
    odt = BF16
    fused = head is not None

    n2 = HGRN_HP // 2

    def body(*refs):
        refs = list(refs)
        q_refs, f_refs, v_refs = refs[:n2], refs[n2:2 * n2], refs[2 * n2:3 * n2]
        lb_ref, sh_ref = refs[3 * n2], refs[3 * n2 + 1]
        pos = 3 * n2 + 2
        if fused:
            osum_ref, g_refs, dmix_ref, gain_ref = refs[pos], refs[pos + 1:pos + 1 + n2], refs[pos + 1 + n2], refs[pos + 2 + n2]
            pos += 3 + n2
        else:
            do_ref = refs[pos]
            pos += 1
        if has_prev:
            pq_ref, pv_ref = refs[pos], refs[pos + 1]
            pos += 2
        dq_ref, df_ref, dv_ref, dlb_ref = refs[pos:pos + 4]
        pos += 4
        if fused:
            do_out, dg_ref, dgain_ref = refs[pos:pos + 3]
            pos += 3
        dst = refs[pos]

        @pl.when(pl.program_id(1) == 0)
        def _():
            dst[...] = jnp.zeros_like(dst)
            dlb_ref[...] = jnp.zeros_like(dlb_ref)

        if fused:
            @pl.when((pl.program_id(0) == 0) & (pl.program_id(1) == 0))
            def _():
                dgain_ref[...] = jnp.zeros_like(dgain_ref)

        cat = functools.partial(jnp.concatenate, axis=0)
        for hh in range(HGRN_HP):
            ln = slice(128 * hh, 128 * hh + 128)
            lbv = lb_ref[:, ln]
            qraw, fraw = _head_cols(q_refs, hh), _head_cols(f_refs, hh)
            q, k, lf, sq, sf, f = _hgrn_gates(qraw, fraw, lbv)
            tri, eq, ek, ei, eki, eb = _block_terms(lf, rev)
            qe, ke, qi, ki = q * eq, k * ek, q * ei, k * eki
            if fused:
                dov, dg, dgain = _headnorm_grad(osum_ref[:, ln], _head_cols(g_refs, hh), dmix_ref[:, ln], gain_ref[...])
                do_out[:, ln] = _bf(dov)
                dg_ref[:, ln] = dg
                _acc_row(dgain_ref, 0, dgain)
            else:
                dov = do_ref[:, ln]
            qeb, keb, qib, kib, vb, dob = _bf(qe), _bf(ke), _bf(qi), _bf(ki), _bf(_head_cols(v_refs, hh)), _bf(dov)
            dv, dqe, dke, dqi = [None] * nc, [None] * nc, [None] * nc, [None] * nc
            for cc in range(nc):
                rows = slice(cc * CHUNK, (cc + 1) * CHUNK)
                a = jnp.where(tri, _dot_nt(qeb[rows], keb[rows]), 0.0)
                da = jnp.where(tri, _dot_nt(dob[rows], vb[rows]), 0.0)
                dv[cc] = _dot_tn(a, dob[rows])
                dqe[cc], dke[cc] = _dot(da, keb[rows]), _dot_tn(da, qeb[rows])
                dqi[cc] = _dot(dob[rows], sh_ref[hh, cc])
            dki, dbl = [None] * nc, [None] * nc
            ds = dst[hh]
            for cc in _chunk_order(rev, True):
                rows = slice(cc * CHUNK, (cc + 1) * CHUNK)
                ebc = eb[cc * CHUNK:cc * CHUNK + 1, :]
                dv[cc] = dv[cc] + _dot_nt(kib[rows], ds)
                dki[cc] = _dot(vb[rows], ds)
                dbl[cc] = jnp.broadcast_to(jnp.sum(dki[cc] * ki[rows], axis=0, keepdims=True)
                                           + jnp.sum(ds * sh_ref[hh, cc], axis=0, keepdims=True) * ebc, (CHUNK, 128))
                ds = ds * ebc + _dot_tn(dob[rows], qib[rows])
            dst[hh] = ds
            dqe, dke, dqi, dki, dv, dbl = cat(dqe), cat(dke), cat(dqi), cat(dki), cat(dv), cat(dbl)
            dq = dqe * eq + dqi * ei
            dk = dke * ek + dki * eki
            last = 0 if rev else CHUNK - 1
            db = dqe * qe - dke * ke + dqi * qi - dki * ki
            db = db + jnp.where((_iota(db.shape, 0) & (CHUNK - 1)) == last, dbl, 0.0)
            dlf = _chunk_cumsum(db, not rev)
            dqr = dq * (sq * (1.0 + qraw * (1.0 - sq)))
            dfv = dlf / f - dk
            dfr = dfv * (1.0 - lbv) * (sf * (1.0 - sf))
            dlb_ref[:, ln] += jnp.sum(dfv * (1.0 - sf), axis=0, keepdims=True)
            if has_prev:
                dqr = dqr + pq_ref[:, ln]
                dv = dv + pv_ref[:, ln]
            dq_ref[:, ln] = dqr.astype(odt)
            df_ref[:, ln] = dfr.astype(odt)
            dv_ref[:, ln] = dv.astype(odt)

    hp, wd = HGRN_HP, 128 * HGRN_HP
    col = functools.partial(_hgrn_cols, bmap, n2)
    oblk = pl.BlockSpec((TM, wd), lambda h, n: (bmap(n), h))
    ins = [p] * (3 * n2) + [lb, sh]
    specs = col(6) + col(fcol) + col(18) + [pl.BlockSpec((1, wd), lambda h, n: (0, h)),
                                            pl.BlockSpec((hp, nc, 128, 128), lambda h, n: (h, bmap(n), 0, 0))]
    if fused:
        osum, dmix, gain = head
        ins += [osum] + [p] * n2 + [dmix, gain]
        specs += [oblk] + col(22) + [pl.BlockSpec((TM, wd), lambda h, n: (bmap(n), 4 // hp + h)),
                                     pl.BlockSpec((1, 128), lambda h, n: (0, 0))]
    else:
        ins.append(do); specs.append(oblk)
    if has_prev:
        ins += list(prev); specs += [oblk, oblk]
    out_specs = [oblk, oblk, oblk, pl.BlockSpec((1, wd), lambda h, n: (0, h))]
    out_shape = [jax.ShapeDtypeStruct((t, 512), odt)] * 3 + [jax.ShapeDtypeStruct((1, 512), F32)]
    if fused:
        out_specs += [oblk, oblk, pl.BlockSpec((8, 128), lambda h, n: (0, 0))]
        out_shape += [jax.ShapeDtypeStruct((t, 512), BF16), jax.ShapeDtypeStruct((t, 512), BF16),
                      jax.ShapeDtypeStruct((8, 128), F32)]
    return _pcall(body, name=name, grid=(4 // hp, nb), in_specs=specs, out_specs=out_specs, out_shape=out_shape,
                  scratch_shapes=[pltpu.VMEM((hp, 128, 128), F32)])(*ins)


def _rope256(x, cos, sin):
    x1, x2 = x[:, 0:128], x[:, 128:256]
    return jnp.concatenate([x1 * cos - x2 * sin, x2 * cos + x1 * sin], axis=-1)


def _rope256_t(d, cos, sin):
    d1, d2 = d[:, 0:128], d[:, 128:256]
    return jnp.concatenate([d1 * cos + d2 * sin, d2 * cos - d1 * sin], axis=-1)


RET_DK, RET_DV, RET_H = 256, 512, 4
RET_KSCALE = RET_DK ** -0.5
RCH = TM
RET_HP = 4


def _ret_terms(lg, rev):
    r, c = _iota((RCH, RCH), 0), _iota((RCH, RCH), 1)
    rel = ((c - r) if rev else (r - c)).astype(F32)
    dmat = jnp.where(rel >= 0, jnp.exp(lg[:, 0:1] * jnp.maximum(rel, 0.0)), 0.0)
    pos = _iota((RCH, 1), 0).astype(F32)
    cnt = (RCH - pos) if rev else (pos + 1.0)
    ei = jnp.exp(lg * cnt)
    eki = jnp.exp(lg * (RCH - cnt))
    eb = jnp.exp(lg * float(RCH))
    return dmat, ei, eki, eb


def _ret_fwd(p, lgt, cos, sin, *, rev, name, ofw=None):
    t = p.shape[0]
    nb, nc = t // TM, TM // RCH
    bmap = _blk_map(nb, rev, False)
    fused = ofw is not None

    def body(*refs):
        q_ref, k_ref, v_ref, lg_ref, c_ref, s_ref = refs[:6]
        if fused:
            ofw_ref, g_ref, o_ref, sh_ref, mix_ref, st = refs[6:]
        else:
            o_ref, sh_ref, st = refs[6:]

        @pl.when(pl.program_id(1) == 0)
        def _():
            st[...] = jnp.zeros_like(st)
        for hh in range(RET_HP):
            qc, vc = slice(RET_DK * hh, RET_DK * (hh + 1)), slice(RET_DV * hh, RET_DV * (hh + 1))
            dmat, ei, eki, eb = _ret_terms(lg_ref[hh], rev)
            for cc in _chunk_order(rev, False, nc):
                rows = slice(cc * RCH, (cc + 1) * RCH)
                cosv, sinv = c_ref[rows, :], s_ref[rows, :]
                q = _rope256(q_ref[rows, qc].astype(F32), cosv, sinv)
                k = _rope256(k_ref[rows, qc].astype(F32), cosv, sinv) * RET_KSCALE
                v = v_ref[rows, vc]
                s0 = st[hh]
                sh_ref[hh, cc] = s0.astype(BF16)
                a = _dot_nt(q, k) * dmat
                o = _dot(a, v) + _dot_nt(q * ei, s0)
                st[hh] = s0 * eb + _dot_tn(v, k * eki)
                if fused:
                    o = o + ofw_ref[rows, vc]
                    mix_ref[rows, vc] = _headnorm_apply(o, g_ref[rows, vc].astype(F32), None)
                o_ref[rows, vc] = o

    hp = RET_HP
    tab = pl.BlockSpec((TM, 128), lambda h, n: (bmap(n), 0))
    oblk = pl.BlockSpec((TM, hp * RET_DV), lambda h, n: (bmap(n), h))
    ins = [p, p, p, lgt, cos, sin]
    specs = [pl.BlockSpec((TM, hp * RET_DK), lambda h, n: (bmap(n), h)),
             pl.BlockSpec((TM, hp * RET_DK), lambda h, n: (bmap(n), RET_H // hp + h)),
             pl.BlockSpec((TM, hp * RET_DV), lambda h, n: (bmap(n), RET_H // hp + h)),
             pl.BlockSpec((hp, 1, RET_DK), lambda h, n: (h, 0, 0)), tab, tab]
    out_specs = [oblk, pl.BlockSpec((hp, nc, RET_DV, RET_DK), lambda h, n: (h, bmap(n), 0, 0))]
    out_shape = [jax.ShapeDtypeStruct((t, RET_H * RET_DV), F32),
                 jax.ShapeDtypeStruct((RET_H, t // RCH, RET_DV, RET_DK), BF16)]
    if fused:
        ins += [ofw, p]
        specs += [oblk, pl.BlockSpec((TM, hp * RET_DV), lambda h, n: (bmap(n), 2 * RET_H // hp + h))]
        out_specs.append(oblk)
        out_shape.append(jax.ShapeDtypeStruct((t, RET_H * RET_DV), BF16))
    return _pcall(body, name=name, grid=(RET_H // hp, nb), in_specs=specs, out_specs=out_specs, out_shape=out_shape,
                  scratch_shapes=[pltpu.VMEM((hp, RET_DV, RET_DK), F32)])(*ins)


def _ret_bwd(p, lgt, cos, sin, sh, do, prev, *, rev, name, head=None):
    t = p.shape[0]
    nb, nc = t // TM, TM // RCH
    bmap = _blk_map(nb, rev, True)
    has_prev = prev is not None
    odt = BF16
    fused = head is not None

    def body(*refs):
        refs = list(refs)
        q_ref, k_ref, v_ref, lg_ref, c_ref, s_ref, sh_ref = refs[:7]
        if fused:
            osum_ref, g_ref, dy_ref, wout_ref = refs[7:11]
            pos = 11
        else:
            do_ref = refs[7]
            pos = 8
        if has_prev:
            pq_ref, pk_ref, pv_ref = refs[pos:pos + 3]
            pos += 3
        dq_ref, dk_ref, dv_ref = refs[pos:pos + 3]
        pos += 3
        if fused:
            do_out, dg_ref = refs[pos:pos + 2]
            pos += 2
        dst = refs[pos]

        @pl.when(pl.program_id(1) == 0)
        def _():
            dst[...] = jnp.zeros_like(dst)

        if fused:
            dmix = _dot_nt(dy_ref[...], wout_ref[...])
        for hh in range(RET_HP):
            qc, vc = slice(RET_DK * hh, RET_DK * (hh + 1)), slice(RET_DV * hh, RET_DV * (hh + 1))
            dmat, ei, eki, eb = _ret_terms(lg_ref[hh], rev)
            for cc in _chunk_order(rev, True, nc):
                rows = slice(cc * RCH, (cc + 1) * RCH)
                cosv, sinv = c_ref[rows, :], s_ref[rows, :]
                q = _rope256(q_ref[rows, qc].astype(F32), cosv, sinv)
                k = _rope256(k_ref[rows, qc].astype(F32), cosv, sinv) * RET_KSCALE
                v = v_ref[rows, vc]
                if fused:
                    dov, dg, _ = _headnorm_grad(osum_ref[rows, vc], g_ref[rows, vc].astype(F32), dmix[rows, vc], None)
                    do_out[rows, vc] = _bf(dov)
                    dg_ref[rows, vc] = dg
                else:
                    dov = do_ref[rows, vc]
                s0 = sh_ref[hh, cc]
                dsc = dst[hh]
                qi, ki = q * ei, k * eki
                a = _dot_nt(q, k) * dmat
                da = _dot_nt(dov, v) * dmat
                dv = _dot_tn(a, dov) + _dot_nt(ki, dsc)
                dqs = _dot(da, k) + _dot(dov, s0) * ei
                dks = _dot_tn(da, q) + _dot(v, dsc) * eki
                dst[hh] = dsc * eb + _dot_tn(dov, qi)
                dq = _rope256_t(dqs, cosv, sinv)
                dk = _rope256_t(dks * RET_KSCALE, cosv, sinv)
                if has_prev:
                    dq = dq + pq_ref[rows, qc]
                    dk = dk + pk_ref[rows, qc]
                    dv = dv + pv_ref[rows, vc]
                dq_ref[rows, qc] = dq.astype(odt)
                dk_ref[rows, qc] = dk.astype(odt)
                dv_ref[rows, vc] = dv.astype(odt)

    hp = RET_HP
    tab = pl.BlockSpec((TM, 128), lambda h, n: (bmap(n), 0))
    qblk = pl.BlockSpec((TM, hp * RET_DK), lambda h, n: (bmap(n), h))
    vblk = pl.BlockSpec((TM, hp * RET_DV), lambda h, n: (bmap(n), h))
    ins = [p, p, p, lgt, cos, sin, sh]
    specs = [qblk, pl.BlockSpec((TM, hp * RET_DK), lambda h, n: (bmap(n), RET_H // hp + h)),
             pl.BlockSpec((TM, hp * RET_DV), lambda h, n: (bmap(n), RET_H // hp + h)),
             pl.BlockSpec((hp, 1, RET_DK), lambda h, n: (h, 0, 0)), tab, tab,
             pl.BlockSpec((hp, nc, RET_DV, RET_DK), lambda h, n: (h, bmap(n), 0, 0))]
    if fused:
        osum, dy, w_out = head
        assert hp == RET_H and w_out.shape[0] == RET_H * RET_DV
        ins += [osum, p, dy, w_out]
        specs += [vblk, pl.BlockSpec((TM, hp * RET_DV), lambda h, n: (bmap(n), 2 * RET_H // hp + h)),
                  pl.BlockSpec((TM, dy.shape[1]), lambda h, n: (bmap(n), 0)),
                  pl.BlockSpec(w_out.shape, lambda h, n: (0, 0))]
    else:
        ins.append(do); specs.append(vblk)
    if has_prev:
        ins += list(prev); specs += [qblk, qblk, vblk]
    out_specs = [qblk, qblk, vblk]
    out_shape = [jax.ShapeDtypeStruct((t, RET_H * RET_DK), odt), jax.ShapeDtypeStruct((t, RET_H * RET_DK), odt),
                 jax.ShapeDtypeStruct((t, RET_H * RET_DV), odt)]
    if fused:
        out_specs += [vblk, vblk]
        out_shape += [jax.ShapeDtypeStruct((t, RET_H * RET_DV), BF16), jax.ShapeDtypeStruct((t, RET_H * RET_DV), BF16)]
    return _pcall(body, name=name, grid=(RET_H // hp, nb), in_specs=specs, out_specs=out_specs, out_shape=out_shape,
                  scratch_shapes=[pltpu.VMEM((hp, RET_DV, RET_DK), F32)])(*ins)


def _rope_tables(lc, l, zero):
    tt = jnp.arange(l)
    row, colp = (tt // 64).astype(F32) + zero, (tt % 64).astype(F32) + zero
    inv = 10000.0 ** (-jnp.arange(16, dtype=F32) / 16)
    ang = jnp.concatenate([row[:, None] * inv, colp[:, None] * inv], axis=-1)
    ang = jnp.concatenate([jnp.zeros((lc, 32), F32), ang], axis=0)
    acos, asin = jnp.tile(jnp.cos(ang), (1, 4)), jnp.tile(jnp.sin(ang), (1, 4))
    theta = 1.0 / (10000.0 ** jnp.linspace(0.0, 1.0, 128, dtype=F32))
    rang = (jnp.arange(l, dtype=F32) + zero)[:, None] * theta
    rang = jnp.concatenate([jnp.zeros((lc, 128), F32), rang], axis=0)
    return acos, asin, jnp.cos(rang), jnp.sin(rang)


class _Weights:
    def __init__(self, w):
        self.w = w

    def landed(self, grp, after):
        pass

    def full(self, grp, after):
        return self.w

    def send_grads(self, grp, grads):
        return jnp.zeros((8, 128), F32)


def _local_step(x0, target, mods, ng, wsrc, small):
    t, d = x0.shape
    l = target.shape[0]
    lc = t - l
    acos, asin, rcos, rsin = _rope_tables(lc, l, small.get('tok', 0.0))
    lg_fw = jnp.log(1.0 - 2.0 ** (-5.0 - jnp.arange(RET_H, dtype=F32)))
    lgt_fw = jnp.broadcast_to(lg_fw[:, None, None], (RET_H, 1, RET_DK))
    lgt_bw = jnp.broadcast_to(lg_fw[::-1][:, None, None], (RET_H, 1, RET_DK))
    gq, gk, sink, gain, lb = small['gq'], small['gk'], small['sink'], small['gain'], small['lb']

    (h1,) = _row_fwd(x0, mods, g=ng[0], shift=0, scale=1, name='l0_norm1')
    wsrc.landed('even', h1)
    w = dict(wsrc.full('even', h1))
    p0 = _mm_nn(h1, w['even_in'], name='l0_in')
    kp = _kprep_fwd(p0, gk, acos, asin, name='l0_kprep')
    att = _attn_fwd(p0, kp, gq, sink, acos, asin, lc=lc, name='l0_attn')
    wsrc.landed('ffn', att)
    hof, hsf = _hgrn_fwd(p0, lb, rev=False, name='l0_hgrn_f')
    wsrc.landed('odd', hof)
    hos, hsb, bmix = _hgrn_fwd(p0, lb, rev=True, name='l0_hgrn_b', ofw=hof, gain=gain)
    mix0 = [att, bmix]
    y0 = _mm_nn(mix0, w['even_out'], name='l0_out')
    x1, h2 = _row_fwd(x0, mods, y=y0, gate=2, g=ng[1], shift=3, scale=4, name='l0_norm2')
    w.update(wsrc.full('ffn', h2))
    u0, a0 = _ffn_in(h2, w['ffn_in'], lead=0, name='ffn_in')
    z0 = _mm_nn(a0, w['ffn_out'], lead=0, name='ffn_out')
    x2, h3 = _row_fwd(x1, mods, y=z0, gate=5, g=ng[2], shift=12, scale=13, name='l1_norm1')
    w.update(wsrc.full('odd', h3))
    p1 = _mm_nn(h3, w['odd_in'], out_dtype=BF16, name='l1_in')
    rof, rsf = _ret_fwd(p1, lgt_fw, rcos, rsin, rev=False, name='l1_ret_f')
    ros, rsb, mix1 = _ret_fwd(p1, lgt_bw, rcos, rsin, rev=True, name='l1_ret_b', ofw=rof)
    y1 = _mm_nn(mix1, w['odd_out'], name='l1_out')
    x3, h4 = _row_fwd(x2, mods, y=y1, gate=14, g=ng[3], shift=15, scale=16, name='l1_norm2')
    u1, a1 = _ffn_in(h4, w['ffn_in'], lead=1, name='ffn_in')
    z1 = _mm_nn(a1, w['ffn_out'], lead=1, name='ffn_out')
    loss, dx4, dz1, s_fin = _row_final(x3, z1, mods, target, gate=17, name='loss')

    du1 = _ffn_dx(dz1, w['ffn_out'], u1, lead=1, name='ffn_out_dx')
    g_ffn_out1 = _mm_tn(a1, dz1, name='ffn_out_dw')
    dh4 = _mm_nt(du1, w['ffn_in'], lead=1, name='ffn_in_dx')
    g_ffn_in1 = _mm_tn(h4, du1, name='ffn_in_dw')
    dx3, dy1, s_l1n2 = _row_bwd(x3, dx4, dh4, mods, ng[3], shift=15, scale=16, y=y1, gate=14, name='l1_norm2_bwd')
    g_odd_out = _mm_tn(mix1, dy1, name='l1_out_dw')
    rdq, rdk, rdv, rdo, rdg = _ret_bwd(p1, lgt_fw, rcos, rsin, rsf, None, None, rev=False, name='l1_ret_f_bwd',
                                       head=(ros, dy1, w['odd_out']))
    rdq, rdk, rdv = _ret_bwd(p1, lgt_bw, rcos, rsin, rsb, rdo, (rdq, rdk, rdv), rev=True, name='l1_ret_b_bwd')
    dp1 = [rdq, rdk, rdv, rdg]
    dh3 = _mm_nt(dp1, w['odd_in'], name='l1_in_dx')
    g_odd_in = _mm_tn(h3, dp1, name='l1_in_dw')
    mods = mods + wsrc.send_grads('early', dict(ffn_in1=g_ffn_in1, ffn_out1=g_ffn_out1, odd_in=g_odd_in,
                                                odd_out=g_odd_out))[0, 0]
    dx2, dz0, s_l1n1 = _row_bwd(x2, dx3, dh3, mods, ng[2], shift=12, scale=13, y=z0, gate=5, name='l1_norm1_bwd')
    du0 = _ffn_dx(dz0, w['ffn_out'], u0, lead=0, name='ffn_out_dx')
    g_ffn_out0 = _mm_tn(a0, dz0, name='ffn_out_dw')
    dh2 = _mm_nt(du0, w['ffn_in'], lead=0, name='ffn_in_dx')
    g_ffn_in0 = _mm_tn(h2, du0, name='ffn_in_dw')
    mods = mods + wsrc.send_grads('mid', dict(ffn_in0=g_ffn_in0, ffn_out0=g_ffn_out0))[0, 0]
    dx1, dy0, s_l0n2 = _row_bwd(x1, dx2, dh2, mods, ng[1], shift=3, scale=4, y=y0, gate=2, name='l0_norm2_bwd')
    dmix0 = _mm_nt(dy0, w['even_out'], name='l0_out_dx')
    g_even_out = _mm_tn(mix0, dy0, name='l0_out_dw')
    hq, hff, hv, dlb_f, hdo, hdg, s_gain = _hgrn_bwd(p0, lb, hsf, None, None, rev=False, name='l0_hgrn_f_bwd',
                                                     head=(hos, dmix0, gain))
    hq, hfb, hv, dlb_b = _hgrn_bwd(p0, lb, hsb, hdo, (hq, hv), rev=True, name='l0_hgrn_b_bwd')
    adq, dkp, adv, s_gq, s_sink = _attn_bwd(p0, kp, gq, sink, acos, asin, dmix0, lc=lc, name='l0_attn_bwd')
    dkv, s_gk = _kprep_bwd(p0, gk, acos, asin, dkp, adv, name='l0_kprep_bwd')
    dp0 = jnp.concatenate([adq, dkv, hq, _bf(hff), hfb, hv, hdg], axis=1)
    dh1 = _mm_nt(dp0, w['even_in'], name='l0_in_dx')
    g_even_in = _mm_tn(h1, dp0, name='l0_in_dw')
    dx0, s_l0n1 = _row_bwd(x0, dx1, dh1, mods, ng[0], shift=0, scale=1, latent_only=True, name='l0_norm1_bwd')

    grads = dict(ffn_in0=g_ffn_in0, ffn_in1=g_ffn_in1, ffn_out0=g_ffn_out0, ffn_out1=g_ffn_out1,
                 even_in=g_even_in, even_out=g_even_out, odd_in=g_odd_in, odd_out=g_odd_out)
    sums = dict(fin=s_fin, l1n2=s_l1n2, l1n1=s_l1n1, l0n2=s_l0n2, l0n1=s_l0n1, gain=s_gain, gq=s_gq, gk=s_gk,
                sink=s_sink, dlb_f=dlb_f, dlb_b=dlb_b)
    return loss, dx0, grads, sums


def _place():
    return lax.axis_index("x"), lax.axis_index("y"), lax.axis_index("c")


def _ag8(blk, *, name):
    r, c = blk.shape
    flips = [(dx, dy, dc) for dx in (0, 1) for dy in (0, 1) for dc in (0, 1) if (dx, dy, dc) != (0, 0, 0)]

    def body(x_ref, out_ref, send_sems, recv_sems, local_sem):
        ax, ay, ac = _place()
        me = 4 * ax + 2 * ay + ac
        mine = pltpu.make_async_copy(x_ref, out_ref.at[me], local_sem)
        mine.start()
        sent = []
        for k, (dx, dy, dc) in enumerate(flips):
            peer = (lax.rem(ax + dx, 2), lax.rem(ay + dy, 2), lax.rem(ac + dc, 2))
            cp = pltpu.make_async_remote_copy(src_ref=x_ref, dst_ref=out_ref.at[me], send_sem=send_sems.at[k],
                                              recv_sem=recv_sems.at[k], device_id=peer, device_id_type=MESH)
            cp.start()
            sent.append((cp, 4 * peer[0] + 2 * peer[1] + peer[2]))
        for k, (cp, pidx) in enumerate(sent):
            pltpu.make_async_remote_copy(src_ref=x_ref, dst_ref=out_ref.at[pidx], send_sem=send_sems.at[k],
                                         recv_sem=recv_sems.at[k], device_id=(ax, ay, ac),
                                         device_id_type=MESH).wait_recv()
        for cp, _ in sent:
            cp.wait_send()
        mine.wait()

    return _pcall(
        body, name=name,
        in_specs=[pl.BlockSpec(memory_space=pltpu.VMEM)],
        out_specs=pl.BlockSpec(memory_space=pltpu.VMEM),
        out_shape=jax.ShapeDtypeStruct((8, r, c), blk.dtype),
        scratch_shapes=[pltpu.SemaphoreType.DMA((7,)), pltpu.SemaphoreType.DMA((7,)), pltpu.SemaphoreType.DMA],
    )(blk)


_HBM = pl.BlockSpec(memory_space=pltpu.HBM)
_SEM = pl.BlockSpec(memory_space=pltpu.SEMAPHORE)
_DATAFLOW = pltpu.SideEffectType.DATAFLOW_SIDE_EFFECTING


def _split_start(bufs, plan, k, *, name):
    n = len(bufs)

    def body(*refs):
        ins, send_sems, recv_sems, token = refs[:n], refs[n], refs[n + 1], refs[2 * n + 2]
        for i, (src, dst, dev) in enumerate(plan(ins)):
            pltpu.make_async_remote_copy(src_ref=src, dst_ref=dst, send_sem=send_sems.at[i], recv_sem=recv_sems.at[i],
                                         device_id=dev, device_id_type=MESH).start()
        token[...] = jnp.zeros_like(token)

    res = _pcall(
        body, name=name,
        out_shape=(pltpu.SemaphoreType.DMA((k,)), pltpu.SemaphoreType.DMA((k,)),
                   *[pltpu.HBM(b.shape, b.dtype) for b in bufs], jax.ShapeDtypeStruct((8, 128), F32)),
        in_specs=[_HBM] * n, out_specs=(_SEM, _SEM, *[_HBM] * n, pl.BlockSpec(memory_space=pltpu.VMEM)),
        input_output_aliases={i: 2 + i for i in range(n)},
        compiler_params=pltpu.CompilerParams(has_side_effects=_DATAFLOW),
    )(*[pltpu.with_memory_space_constraint(b, pltpu.HBM) for b in bufs])
    return res[0], res[1], list(res[2:2 + n]), res[2 + n]


def _split_wait(bufs, send_sems, recv_sems, plan, after, *, name):
    n = len(bufs)

    def body(*refs):
        ins, ssem, rsem = refs[:n], refs[n], refs[n + 1]
        for i, (src, dst, dev) in enumerate(plan(ins)):
            cp = pltpu.make_async_remote_copy(src_ref=src, dst_ref=dst, send_sem=ssem.at[i], recv_sem=rsem.at[i],
                                              device_id=dev, device_id_type=MESH)
            cp.wait_send()
            cp.wait_recv()

    res = _pcall(
        body, name=name, out_shape=tuple(pltpu.HBM(b.shape, b.dtype) for b in bufs),
        in_specs=[_HBM] * n + [_SEM, _SEM, pl.BlockSpec(memory_space=pl.ANY)], out_specs=tuple([_HBM] * n),
        input_output_aliases={i: i for i in range(n)},
        compiler_params=pltpu.CompilerParams(has_side_effects=_DATAFLOW),
    )(*bufs, send_sems, recv_sems, after)
    return list(res)


_CHIP_FLIPS = [(1, 0), (0, 1), (1, 1)]


class _GatheredWeights:
    GROUPS = (('even', ('even_in', 'even_out')), ('ffn', ('ffn_in', 'ffn_out')), ('odd', ('odd_in', 'odd_out')))

    def __init__(self, shards, reducer):
        self.shards = shards
        self.send_grads = reducer.start
        self.ici, self.d2d, self.token = {}, {}, None
        for grp, names in self.GROUPS:
            src = [shards[nm].reshape(2, shards[nm].shape[0] // 2, shards[nm].shape[1]) for nm in names]
            land = [lax.empty((4,) + a.shape, a.dtype) for a in src]
            m = len(names)
            sends, recvs, bufs, token = _split_start(src + land, functools.partial(self._ici_plan, m, True), 4 * m,
                                                     name='gather_' + grp + '_ici_start')
            self.ici[grp] = (sends, recvs, bufs, m)
            self.token = token if self.token is None else self.token + token

    @staticmethod
    def _ici_plan(m, sending, refs):
        ax, ay, ac = _place()
        s = 2 * ax + ay
        out = []
        for a in range(m):
            for dx, dy in _CHIP_FLIPS:
                px, py = lax.rem(ax + dx, 2), lax.rem(ay + dy, 2)
                slot = s if sending else 2 * px + py
                out.append((refs[a].at[ac], refs[m + a].at[slot, ac], (px, py, ac)))
        for a in range(m):
            out.append((refs[a], refs[m + a].at[s], (ax, ay, 1 - ac)))
        return out

    @staticmethod
    def _d2d_plan(m, sending, refs):
        ax, ay, ac = _place()
        out = []
        for a in range(m):
            for dx, dy in _CHIP_FLIPS:
                sp = 2 * lax.rem(ax + dx, 2) + lax.rem(ay + dy, 2)
                out.append((refs[a].at[sp, ac], refs[a].at[sp, ac if sending else 1 - ac], (ax, ay, 1 - ac)))
        return out

    def landed(self, grp, after):
        sends, recvs, bufs, m = self.ici[grp]
        bufs = _split_wait(bufs, sends, recvs, functools.partial(self._ici_plan, m, False), after,
                           name='gather_' + grp + '_ici_wait')
        sends, recvs, land, _ = _split_start(bufs[m:], functools.partial(self._d2d_plan, m, True), 3 * m,
                                             name='gather_' + grp + '_d2d_start')
        self.d2d[grp] = (sends, recvs, land, m)

    def full(self, grp, after):
        sends, recvs, land, m = self.d2d[grp]
        land = _split_wait(land, sends, recvs, functools.partial(self._d2d_plan, m, False), after,
                           name='gather_' + grp + '_d2d_wait')
        names = dict(self.GROUPS)[grp]
        return {nm: _from_shards(nm, g.reshape((4,) + self.shards[nm].shape)) for nm, g in zip(names, land)}


def _to_sibling(arrs, *, name):
    n = len(arrs)

    def body(*refs):
        ins, outs = refs[:n], refs[n:2 * n]
        send_sems, recv_sems = refs[2 * n:]
        ax, ay, ac = _place()
        cps = [pltpu.make_async_remote_copy(src_ref=ins[a], dst_ref=outs[a], send_sem=send_sems.at[a],
                                            recv_sem=recv_sems.at[a], device_id=(ax, ay, 1 - ac),
                                            device_id_type=MESH) for a in range(n)]
        for cp in cps:
            cp.start()
        for cp in cps:
            cp.wait_recv()
        for cp in cps:
            cp.wait_send()

    hbm = pl.BlockSpec(memory_space=pl.ANY)
    return _pcall(
        body, name=name, in_specs=[hbm] * n, out_specs=[hbm] * n,
        out_shape=[jax.ShapeDtypeStruct(a.shape, a.dtype) for a in arrs],
        scratch_shapes=[pltpu.SemaphoreType.DMA((n,))] * 2,
    )(*arrs)


def _mod_fwd(cond_raw, mw, mb, *, name):
    _, d, n = mw.shape

    def body(c_ref, w_ref, b_ref, o_ref):
        cv = c_ref[...]
        o_ref[...] = _dot(cv * _sigmoid(cv), w_ref[...]) + b_ref[...]

    return _pcall(
        body, name=name, grid=(2,),
        in_specs=[pl.BlockSpec((16, d), lambda l: (0, 0)), pl.BlockSpec((None, d, n), lambda l: (l, 0, 0)),
                  pl.BlockSpec((None, 1, n), lambda l: (l, 0, 0))],
        out_specs=pl.BlockSpec((None, 16, n), lambda l: (l, 0, 0)),
        out_shape=jax.ShapeDtypeStruct((2, 16, n), F32),
    )(cond_raw, mw, mb)


def _mod_bwd(cond_raw, dms, mw, *, name):
    _, d, n = mw.shape

    def body(c_ref, dm_ref, w_ref, gw_ref, dc_ref):
        @pl.when(pl.program_id(0) == 0)
        def _():
            dc_ref[...] = jnp.zeros_like(dc_ref)
        cv = c_ref[...]
        gw_ref[...] = _dot_tn(cv * _sigmoid(cv), dm_ref[...])
        dc_ref[...] += _dot_nt(dm_ref[...], w_ref[...])

    return _pcall(
        body, name=name, grid=(2,),
        in_specs=[pl.BlockSpec((16, d), lambda l: (0, 0)), pl.BlockSpec((None, 16, n), lambda l: (l, 0, 0)),
                  pl.BlockSpec((None, d, n), lambda l: (l, 0, 0))],
        out_specs=[pl.BlockSpec((None, d, n), lambda l: (l, 0, 0)), pl.BlockSpec((16, d), lambda l: (0, 0))],
        out_shape=[jax.ShapeDtypeStruct((2, d, n), F32), jax.ShapeDtypeStruct((16, d), F32)],
    )(cond_raw, dms, mw)


def _lb_fwd(hgrn_lb, *, name):
    def body(a_ref, o_ref):
        a0, a1 = a_ref[0:1, :], a_ref[1:2, :]
        m = jnp.maximum(a0, a1)
        e0, e1 = jnp.exp(a0 - m), jnp.exp(a1 - m)
        o_ref[...] = e0 / (e0 + e1)

    return _pcall(body, name=name, out_shape=jax.ShapeDtypeStruct((1, hgrn_lb.shape[1]), F32))(hgrn_lb)


PACK_TILES = ('l0n1', 'l0n2', 'l1n1', 'l1n2', 'fin')
PACK_SINGLES = ('gq', 'gk', 'gain', 'dlb_f', 'dlb_b', 'sink')
PACK_ROW = {nm: 8 * i for i, nm in enumerate(PACK_TILES)}
PACK_ROW.update({nm: 8 * len(PACK_TILES) + i for i, nm in enumerate(PACK_SINGLES)})
MOD_SOURCE = ((('l0n1', 0), ('l0n1', 1), ('l0n2', 2), ('l0n2', 0), ('l0n2', 1), ('l1n1', 2)),
              (('l1n1', 0), ('l1n1', 1), ('l1n2', 2), ('l1n2', 0), ('l1n2', 1), ('fin', 2)))


def _small_finalize(gath, lb_pad, *, name):
    d = gath.shape[2]

    def body(g_ref, lb_ref, small_ref, glb_ref, gmb_ref, dm_ref):
        tot = g_ref[0]
        for e in range(1, 8):
            tot = tot + g_ref[e]

        def row(nm, r=0):
            return tot[PACK_ROW[nm] + r:PACK_ROW[nm] + r + 1, :]

        for k, nm in enumerate(('l0n1', 'l0n2', 'l1n1', 'l1n2')):
            small_ref[k:k + 1, :] = row(nm, 3) + row(nm, 7)
        for k, nm in ((4, 'gq'), (5, 'gk')):
            small_ref[k:k + 1, :] = row(nm) + pltpu.roll(row(nm), d - 64, 1)
        small_ref[6:7, :] = row('gain')
        small_ref[7:8, :] = row('sink')
        lbv = lb_ref[...]
        g0 = (row('dlb_f') + row('dlb_b')) * lbv * (1.0 - lbv)
        glb_ref[...] = jnp.zeros_like(glb_ref)
        glb_ref[0:1, :] = g0
        glb_ref[1:2, :] = -g0
        dm_ref[...] = jnp.zeros_like(dm_ref)
        for l in range(2):
            for part in range(6):
                nm, r = MOD_SOURCE[l][part]
                gmb_ref[l * 6 + part:l * 6 + part + 1, :] = row(nm, r) + row(nm, r + 4)
                rl = PACK_ROW[nm] + r + 4
                for e in range(8):
                    dm_ref[l, part, e:e + 1, :] = g_ref[e, rl:rl + 1, :]
                dm_ref[l, part, 8:9, :] = row(nm, r)

    return _pcall(
        body, name=name,
        out_shape=[jax.ShapeDtypeStruct((8, d), F32), jax.ShapeDtypeStruct((8, d), F32),
                   jax.ShapeDtypeStruct((12, d), F32), jax.ShapeDtypeStruct((2, 6, 16, d), F32)],
    )(gath, lb_pad)


def _cctx_grad(gath, c_ctx2, *, name):
    def body(g_ref, c_ref, o_ref):
        tot = ((g_ref[0, 0:1, :] + g_ref[2, 0:1, :]) + g_ref[4, 0:1, :]) + g_ref[6, 0:1, :]
        cv = c_ref[...]
        s = _sigmoid(cv)
        o_ref[...] = tot * (s * (1.0 + cv * (1.0 - s)))

    return _pcall(body, name=name, out_shape=jax.ShapeDtypeStruct(c_ctx2.shape, F32))(gath, c_ctx2)


def _row_block(r, c, limit=256 * 1024):
    best = None
    for br in range(16, r + 1, 16):
        if r % br == 0 and br * c <= limit:
            best = br
    return best if best is not None else r


def _sum4(own, landed, core, *, name):
    _, r, c = own.shape
    br = _row_block(r, c, 512 * 1024)

    def body(core_ref, own_ref, land_ref, o_ref):
        s = 2 * lax.axis_index("x") + lax.axis_index("y")
        p = [jnp.where(s == k, own_ref[k], land_ref[k]).astype(F32) for k in range(4)]
        o_ref[...] = ((p[0] + p[1]) + p[2]) + p[3]

    blk = pl.BlockSpec((4, br, c), lambda i, core_ref: (0, i, 0))
    spec = pltpu.PrefetchScalarGridSpec(
        num_scalar_prefetch=1, grid=(r // br,), in_specs=[blk, blk],
        out_specs=pl.BlockSpec((None, br, c), lambda i, core_ref: (core_ref[0], i, 0)))
    return _pcall(body, name=name, grid_spec=spec, out_shape=jax.ShapeDtypeStruct((2, r, c), F32))(core, own, landed)


def _exchange_halves(arrs, *, name):
    n = len(arrs)

    def body(*refs):
        ins, outs = refs[:n], refs[n:2 * n]
        send_sems, recv_sems = refs[2 * n:]
        ax, ay, ac = _place()
        cps = [pltpu.make_async_remote_copy(src_ref=ins[a].at[ac], dst_ref=outs[a].at[ac], send_sem=send_sems.at[a],
                                            recv_sem=recv_sems.at[a], device_id=(ax, ay, 1 - ac),
                                            device_id_type=MESH) for a in range(n)]
        for cp in cps:
            cp.start()
        for a in range(n):
            pltpu.make_async_remote_copy(src_ref=ins[a].at[ac], dst_ref=outs[a].at[1 - ac], send_sem=send_sems.at[a],
                                         recv_sem=recv_sems.at[a], device_id=(ax, ay, ac),
                                         device_id_type=MESH).wait_recv()
        for cp in cps:
            cp.wait_send()

    hbm = pl.BlockSpec(memory_space=pl.ANY)
    return _pcall(
        body, name=name, in_specs=[hbm] * n, out_specs=[hbm] * n,
        out_shape=[jax.ShapeDtypeStruct(a.shape, a.dtype) for a in arrs],
        input_output_aliases={a: a for a in range(n)},
        scratch_shapes=[pltpu.SemaphoreType.DMA((n,))] * 2,
    )(*arrs)


def _add2(a, b, *, name):
    r, c = a.shape
    br = _row_block(r, c, 1024 * 1024)

    def body(a_ref, b_ref, o_ref):
        o_ref[...] = (a_ref[...].astype(F32) + b_ref[...].astype(F32)).astype(BF16)

    blk = pl.BlockSpec((br, c), lambda i: (i, 0))
    return _pcall(body, name=name, grid=(r // br,), in_specs=[blk, blk], out_specs=blk,
                  out_shape=jax.ShapeDtypeStruct((r, c), BF16))(a, b)


def _adam(w, gs, m, v, *, name):
    r, c = w.shape
    br = _row_block(r, c)
    ng = len(gs)
    c1 = 1.0 - ADAM_B1 ** ADAM_STEP
    c2 = 1.0 - ADAM_B2 ** ADAM_STEP

    def body(*refs):
        w_ref, m_ref, v_ref = refs[0], refs[1 + ng], refs[2 + ng]
        outs = refs[3 + ng:]
        g = refs[1][...]
        for k in range(1, ng):
            g = g + refs[1 + k][...]
        mn = ADAM_B1 * m_ref[...] + (1.0 - ADAM_B1) * g
        vn = ADAM_B2 * v_ref[...] + (1.0 - ADAM_B2) * (g * g)
        if ng > 1:
            outs[0][...] = g
        d_out, m_out, v_out = outs[-3:]
        m_out[...] = mn
        v_out[...] = vn
        d_out[...] = -ADAM_LR * ((mn / c1) / (jnp.sqrt(vn / c2) + ADAM_EPS) + ADAM_WD * w_ref[...])

    blk = pl.BlockSpec((br, c), lambda i: (i, 0))
    nout = 4 if ng > 1 else 3
    res = _pcall(body, name=name, grid=(r // br,), in_specs=[blk] * (3 + ng), out_specs=[blk] * nout,
                 out_shape=[jax.ShapeDtypeStruct((r, c), F32)] * nout)(w, *gs, m, v)
    return list(res) if ng > 1 else [gs[0]] + list(res)


def _grad_halves(name, g, ac):
    if name.endswith('_in'):
        n = g.shape[1] // 4
        if name == 'ffn_in':
            assert n == FFN_BK
        order = _ffn_order(g.shape[1]) if name == 'ffn_in' else range(4)
        v = jnp.stack([g[:, b * n:(b + 1) * n] for b in order])
        per = [v[:, :g.shape[0] // 2], v[:, g.shape[0] // 2:]]
    else:
        k4, n = g.shape
        v = g.reshape(4, 2, k4 // 8, n)
        per = [v[:, 0], v[:, 1]]
    first = ac == 0
    return _bf(jnp.where(first, per[0], per[1])), _bf(jnp.where(first, per[1], per[0]))


class _GradReducer:
    def __init__(self):
        self.flight = {}

    @staticmethod
    def _plan(m, sending, refs):
        ax, ay, ac = _place()
        s = 2 * ax + ay
        out = []
        for a in range(m):
            for dx, dy in _CHIP_FLIPS:
                px, py = lax.rem(ax + dx, 2), lax.rem(ay + dy, 2)
                sp = 2 * px + py
                out.append((refs[a].at[sp], refs[m + a].at[s if sending else sp], (px, py, ac)))
        return out

    def start(self, grp, grads):
        ac = lax.axis_index("c")
        names = list(grads)
        halves = [_grad_halves(nm.rstrip('01'), grads[nm], ac) for nm in names]
        theirs = _to_sibling([h[1] for h in halves], name='swap_core_halves_' + grp)
        pair = [_add2(h[0].reshape(-1, b.shape[-1]), b.reshape(-1, b.shape[-1]), name='add_cores').reshape(b.shape)
                for h, b in zip(halves, theirs)]
        m = len(names)
        land = [lax.empty(a.shape, a.dtype) for a in pair]
        sends, recvs, bufs, token = _split_start(pair + land, functools.partial(self._plan, m, True), 3 * m,
                                                 name='scatter_' + grp + '_start')
        self.flight[grp] = (names, sends, recvs, bufs)
        return token

    def finish(self, grp, after):
        names, sends, recvs, bufs = self.flight.pop(grp)
        m = len(names)
        bufs = _split_wait(bufs, sends, recvs, functools.partial(self._plan, m, False), after,
                           name='scatter_' + grp + '_wait')
        core = lax.axis_index("c").astype(jnp.int32).reshape(1)
        sums = [_sum4(p, l, core, name='sum_chips') for p, l in zip(bufs[:m], bufs[m:])]
        both = _exchange_halves(sums, name='gather_core_halves_' + grp)
        return {nm: g.reshape(-1, g.shape[-1]) for nm, g in zip(names, both)}


def _from_shards(name, g):
    _, r, n = g.shape
    if name == 'ffn_in':
        assert n == FFN_BK
        v = g.reshape(4, 2, r // 2, n)
        return jnp.concatenate([v[b] for b in _ffn_order(4 * n)], axis=-1)
    if name == 'ffn_out':
        return g.reshape(4, 2, r // 2, n).transpose(1, 0, 2, 3).reshape(2, 2 * r, n)
    if name in ('even_in', 'odd_in'):
        return jnp.concatenate([g[b] for b in range(4)], axis=-1)
    return g.reshape(4 * r, n)


def kernel(x, c, ctx, c_ctx, mod_w, mod_b, norm_g, ffn_w_in, ffn_w_out, even_w_in, even_w_out, attn_qk_norm_g, attn_sink, hgrn_out_norm_g, hgrn_lb, odd_w_in, odd_w_out, loss_target, m_c_ctx, m_mod_w, m_mod_b, m_norm_g, m_ffn_w_in, m_ffn_w_out, m_even_w_in, m_even_w_out, m_attn_qk_norm_g, m_attn_sink, m_hgrn_out_norm_g, m_hgrn_lb, m_odd_w_in, m_odd_w_out, v_c_ctx, v_mod_w, v_mod_b, v_norm_g, v_ffn_w_in, v_ffn_w_out, v_even_w_in, v_even_w_out, v_attn_qk_norm_g, v_attn_sink, v_hgrn_out_norm_g, v_hgrn_lb, v_odd_w_in, v_odd_w_out):
    d = x.shape[-1]
    lc = ctx.shape[1]
    assert lc == TM and d == 1024
    ax, ay, ac = _place()
    s = 2 * ax + ay
    me = 4 * ax + 2 * ay + ac
    nmod = mod_w.shape[2]

    def pad8(v):
        return jnp.pad(v, ((0, 8 - v.shape[0]), (0, 0)))

    pack = jnp.concatenate([pad8(c), pad8(norm_g.reshape(1, d))], axis=0)
    g1 = _ag8(pack, name='gather_cond')
    c_all = g1[:, 0, :]
    ng = g1[0::2, 8, :].reshape(4, 2, 2, d // 4).transpose(1, 2, 0, 3).reshape(4, d)

    cond_raw = jnp.concatenate([c_all, pad8(c_ctx.reshape(1, d))], axis=0)
    mb_sh = lax.dynamic_slice_in_dim(mod_b, s * nmod, nmod, axis=1).reshape(2, 1, nmod)
    mpart = _mod_fwd(cond_raw, mod_w, mb_sh, name='mod_fwd')
    g3 = _ag8(mpart.reshape(32, nmod), name='gather_mods')
    mods_full = g3[0::2].reshape(4, 2, 16, nmod).transpose(1, 2, 0, 3).reshape(2, 16, 4 * nmod)
    m_lat = lax.dynamic_index_in_dim(mods_full, me, axis=1, keepdims=False)
    mods = jnp.stack([mods_full[:, 8], m_lat], axis=1).reshape(24, d)

    names = ['ffn_in', 'ffn_out', 'even_in', 'even_out', 'odd_in', 'odd_out']
    shards = [_bf(v.reshape(-1, v.shape[-1])) for v in (ffn_w_in, ffn_w_out, even_w_in, even_w_out, odd_w_in, odd_w_out)]
    shards, mods = lax.optimization_barrier((shards, mods))
    reducer = _GradReducer()
    wsrc = _GatheredWeights(dict(zip(names, shards)), reducer)

    lb = _lb_fwd(hgrn_lb, name='hgrn_lower_bound')
    small = dict(gq=jnp.tile(attn_qk_norm_g[0, 0], 2).reshape(1, 128), gk=jnp.tile(attn_qk_norm_g[0, 1], 2).reshape(1, 128),
                 sink=attn_sink[0], gain=hgrn_out_norm_g, lb=lb)
    zero = wsrc.token[0, 0]
    x0 = jnp.concatenate([ctx[0] + zero, x[0]], axis=0)
    mods = mods + zero
    small['tok'] = zero
    loss_t, dx0, grads, sums = _local_step(x0, loss_target[0], mods, ng, wsrc, small)
    loss = lax.psum(loss_t[0, 0], ("x", "y", "c"))
    grad_x = dx0[None]

    def tile(v, at=0):
        return jnp.pad(v[0:1], ((at, 7 - at), (0, d - v.shape[1])))

    sums = dict(sums, sink=sums['sink'][:, 0].reshape(1, 8))
    singles = sum(tile(sums[nm], i) for i, nm in enumerate(PACK_SINGLES))
    g4 = _ag8(jnp.concatenate([sums[nm] for nm in PACK_TILES] + [singles], axis=0), name='gather_row_sums')
    small_g, glb, gmb, dmat = _small_finalize(g4, tile(lb)[0:1], name='small_grads')
    dms = lax.dynamic_slice_in_dim(dmat.transpose(0, 2, 1, 3).reshape(2, 16, 6 * d), s * nmod, nmod, axis=2)
    g_mod_w, dcond = _mod_bwd(cond_raw, dms, mod_w, name='mod_bwd')
    g5 = _ag8(dcond[8:16], name='gather_dcond')
    g_c_ctx = _cctx_grad(g5, c_ctx.reshape(8, d // 8).reshape(1, d), name='c_ctx_grad')

    late = {nm: grads[nm] for nm in ('even_in', 'even_out')}
    late, g_c_ctx = lax.optimization_barrier((late, g_c_ctx))
    token = reducer.start('late', late)
    full = reducer.finish('early', token)

    def upd(wv, gs, mv, vv, name):
        shp = wv.shape
        c2 = shp[-1]
        out = _adam(wv.reshape(-1, c2), [g.reshape(-1, c2) for g in gs], mv.reshape(-1, c2), vv.reshape(-1, c2), name=name)
        return [o.reshape(shp) for o in out]

    res = {}
    res['c_ctx'] = upd(c_ctx.reshape(8, d // 8), [g_c_ctx.reshape(8, d // 8)], m_c_ctx.reshape(8, d // 8), v_c_ctx.reshape(8, d // 8), 'adam_c_ctx')
    res['c_ctx'] = [o.reshape(d) for o in res['c_ctx']]
    res['mod_w'] = upd(mod_w, [g_mod_w], m_mod_w, v_mod_w, 'adam_mod_w')
    res['mod_b'] = upd(mod_b, [gmb.reshape(2, 6 * d)], m_mod_b, v_mod_b, 'adam_mod_b')
    g_ng = lax.dynamic_slice_in_dim(small_g[0:4].reshape(2, 2, d), s * (d // 4), d // 4, axis=2)
    res['norm_g'] = upd(norm_g, [g_ng], m_norm_g, v_norm_g, 'adam_norm_g')
    g_qk = jnp.stack([small_g[4, 0:64], small_g[5, 0:64]]).reshape(1, 2, 64)
    res['attn_qk_norm_g'] = upd(attn_qk_norm_g, [g_qk], m_attn_qk_norm_g, v_attn_qk_norm_g, 'adam_qk_gain')
    res['attn_sink'] = upd(attn_sink, [small_g[7, 0:8].reshape(1, 8)], m_attn_sink, v_attn_sink, 'adam_sink')
    res['hgrn_out_norm_g'] = upd(hgrn_out_norm_g, [small_g[6, 0:128].reshape(1, 128)], m_hgrn_out_norm_g, v_hgrn_out_norm_g, 'adam_head_gain')
    res['hgrn_lb'] = upd(hgrn_lb, [glb[0:2, 0:hgrn_lb.shape[1]]], m_hgrn_lb, v_hgrn_lb, 'adam_hgrn_lb')
    res['odd_w_in'] = upd(odd_w_in, [full['odd_in']], m_odd_w_in, v_odd_w_in, 'adam_odd_in')
    res['odd_w_out'] = upd(odd_w_out, [full['odd_out']], m_odd_w_out, v_odd_w_out, 'adam_odd_out')
    full.update(reducer.finish('mid', res['odd_w_in'][1]))
    g_ffn_in = jnp.concatenate([full['ffn_in0'], full['ffn_in1']], axis=0)
    g_ffn_out = jnp.concatenate([full['ffn_out0'], full['ffn_out1']], axis=0)
    res['ffn_w_in'] = upd(ffn_w_in, [g_ffn_in], m_ffn_w_in, v_ffn_w_in, 'adam_ffn_in')
    res['ffn_w_out'] = upd(ffn_w_out, [g_ffn_out], m_ffn_w_out, v_ffn_w_out, 'adam_ffn_out')
    full.update(reducer.finish('late', res['ffn_w_in'][1]))
    res['even_w_in'] = upd(even_w_in, [full['even_in']], m_even_w_in, v_even_w_in, 'adam_even_in')
    res['even_w_out'] = upd(even_w_out, [full['even_out']], m_even_w_out, v_even_w_out, 'adam_even_out')

    order = ['c_ctx', 'mod_w', 'mod_b', 'norm_g', 'ffn_w_in', 'ffn_w_out', 'even_w_in', 'even_w_out',
             'attn_qk_norm_g', 'attn_sink', 'hgrn_out_norm_g', 'hgrn_lb', 'odd_w_in', 'odd_w_out']
    outs = [loss, grad_x]
    for k in range(4):
        outs += [res[nm][k] for nm in order]
    return tuple(outs)
```

```python
import functools
import math

import numpy as np
import jax
import jax.numpy as jnp
from jax import lax
from jax.experimental import pallas as pl
from jax.experimental.pallas import tpu as pltpu

F32 = jnp.float32
BF16 = jnp.bfloat16
EPS = 1e-6
TM = 256
CHUNK = 64
QB = 256
WINDOW = 128
NEG = -1e30
MESH = pl.DeviceIdType.MESH

ADAM_LR, ADAM_B1, ADAM_B2, ADAM_EPS, ADAM_WD, ADAM_STEP = 0.001, 0.9, 0.999, 1e-08, 0.01, 10


def _pcall(body, **kw):
    return pl.pallas_call(body, **kw)


def _pick(n, cap):
    best = None
    for m in range(128, min(n, cap) + 1, 128):
        if n % m == 0:
            best = m
    assert best is not None, (n, cap)
    return best


def _bf(x):
    return x.astype(BF16)


def _dot(a, b):
    return jnp.dot(_bf(a), _bf(b), preferred_element_type=F32)


def _dot_nt(a, b):
    return lax.dot_general(_bf(a), _bf(b), (((1,), (1,)), ((), ())), preferred_element_type=F32)


def _dot_tn(a, b):
    return lax.dot_general(_bf(a), _bf(b), (((0,), (0,)), ((), ())), preferred_element_type=F32)


def _dot_exact(a, b):
    return jnp.dot(a, b, preferred_element_type=F32, precision=lax.Precision.HIGHEST)


def _sigmoid(x):
    return 1.0 / (1.0 + jnp.exp(-x))


def _iota(shape, dim):
    return lax.broadcasted_iota(jnp.int32, shape, dim)


def _parts(a):
    parts = list(a) if isinstance(a, (list, tuple)) else [a]
    widths = [p.shape[1] for p in parts]
    return parts, widths, [sum(widths[:i]) for i in range(len(parts))]


def _mm_nn(a, b, *, lead=None, out_dtype=F32, name):
    parts, widths, offs = _parts(a)
    m, k = parts[0].shape[0], sum(widths)
    n = b.shape[-1]
    bm = 1408 if (m % 1408 == 0 and k <= 1024) else (768 if m % 768 == 0 else TM)
    bn = _pick(n, 1024) if n % 512 == 0 else _pick(n, 1664)

    def body(*refs):
        b_ref, o_ref = refs[-2], refs[-1]
        acc = None
        for p_ref, w, off in zip(refs, widths, offs):
            term = _dot(p_ref[...], b_ref[off:off + w, :])
            acc = term if acc is None else acc + term
        o_ref[...] = acc.astype(o_ref.dtype)

    if lead is None:
        b_spec = pl.BlockSpec((k, bn), lambda i, j: (0, j))
    else:
        b_spec = pl.BlockSpec((None, k, bn), lambda i, j: (lead, 0, j))
    return _pcall(
        body, name=name, grid=(m // bm, n // bn),
        in_specs=[pl.BlockSpec((bm, w), lambda i, j: (i, 0)) for w in widths] + [b_spec],
        out_specs=pl.BlockSpec((bm, bn), lambda i, j: (i, j)),
        out_shape=jax.ShapeDtypeStruct((m, n), out_dtype),
    )(*parts, b)


def _mm_nt(a, b, *, lead=None, name):
    parts, widths, offs = _parts(a)
    m, n = parts[0].shape[0], sum(widths)
    k = b.shape[-2]
    bm = 1408 if (m % 1408 == 0 and n <= 1024) else (768 if m % 768 == 0 else TM)
    bk = _pick(k, 1024 if n <= 2048 else 512)

    def body(*refs):
        b_ref, o_ref = refs[-2], refs[-1]
        acc = None
        for p_ref, w, off in zip(refs, widths, offs):
            term = _dot_nt(p_ref[...], b_ref[:, off:off + w])
            acc = term if acc is None else acc + term
        o_ref[...] = acc

    if lead is None:
        b_spec = pl.BlockSpec((bk, n), lambda i, j: (j, 0))
    else:
        b_spec = pl.BlockSpec((None, bk, n), lambda i, j: (lead, j, 0))
    return _pcall(
        body, name=name, grid=(m // bm, k // bk),
        in_specs=[pl.BlockSpec((bm, w), lambda i, j: (i, 0)) for w in widths] + [b_spec],
        out_specs=pl.BlockSpec((bm, bk), lambda i, j: (i, j)),
        out_shape=jax.ShapeDtypeStruct((m, k), F32),
    )(*parts, b)


def _mm_tn(a, b, *, name):
    a_parts, a_w, a_off = _parts(a)
    b_parts, b_w, b_off = _parts(b)
    t, k, n = a_parts[0].shape[0], sum(a_w), sum(b_w)
    bt = 1408 if t % 1408 == 0 else (768 if t % 768 == 0 else TM)
    bk = _pick(k, 1536) if len(a_parts) == 1 else math.gcd(*a_w)
    if len(b_parts) == 1:
        bn = _pick(n, 1024) if n % 1024 == 0 or n < 1664 else _pick(n, 1664)
    else:
        bn = math.gcd(*b_w)
    na, nbp = len(a_parts), len(b_parts)

    def block_range(off, w, blk):
        return off // blk, w // blk

    def body(*refs):
        a_refs, b_refs, o_ref = refs[:na], refs[na:na + nbp], refs[-1]
        i, j = pl.program_id(0), pl.program_id(1)

        @pl.when(pl.program_id(2) == 0)
        def _():
            o_ref[...] = jnp.zeros_like(o_ref)

        def add(a_ref, b_ref):
            o_ref[...] += _dot_tn(a_ref[...], b_ref[...])

        for pa in range(na):
            sa, ca = block_range(a_off[pa], a_w[pa], bk)
            for pb in range(nbp):
                sb, cb = block_range(b_off[pb], b_w[pb], bn)
                if na == 1 and nbp == 1:
                    add(a_refs[0], b_refs[0])
                else:
                    pl.when((i >= sa) & (i < sa + ca) & (j >= sb) & (j < sb + cb))(
                        functools.partial(add, a_refs[pa], b_refs[pb]))

    def spec(off, w, blk, axis):
        s0, cnt = block_range(off, w, blk)

        def index(i, j, s):
            g = i if axis == 0 else j
            inside = (g >= s0) & (g < s0 + cnt)
            return (jnp.where(inside, s, 0), jnp.clip(g - s0, 0, cnt - 1))

        return pl.BlockSpec((bt, blk), index)

    return _pcall(
        body, name=name, grid=(k // bk, n // bn, t // bt),
        in_specs=[spec(o, w, bk, 0) for o, w in zip(a_off, a_w)] + [spec(o, w, bn, 1) for o, w in zip(b_off, b_w)],
        out_specs=pl.BlockSpec((bk, bn), lambda i, j, s: (i, j)),
        out_shape=jax.ShapeDtypeStruct((k, n), F32),
    )(*a_parts, *b_parts)


def _mod_row(mods_ref, lat, idx):
    return jnp.where(lat, mods_ref[idx + 6:idx + 7, :], mods_ref[idx:idx + 1, :])


def _row_step(t):
    return 768 if t % 768 == 0 else TM


def _row_fwd(x, mods, *, y=None, gate=None, g=None, shift=None, scale=None, name):
    t, d = x.shape
    has_y, has_n = y is not None, g is not None
    rt = _row_step(t)

    def body(*refs):
        refs = list(refs)
        x_ref, mods_ref = refs[0], refs[1]
        pos = 2
        if has_y:
            y_ref = refs[pos]; pos += 1
        if has_n:
            g_ref = refs[pos]; pos += 1
        outs = refs[pos:]
        for sub in range(rt // TM):
            rows = slice(sub * TM, (sub + 1) * TM)
            lat = pl.program_id(0) * (rt // TM) + sub > 0
            x1 = x_ref[rows, :]
            o = 0
            if has_y:
                x1 = x1 + _mod_row(mods_ref, lat, gate) * y_ref[rows, :]
                outs[o][rows, :] = x1; o += 1
            if has_n:
                rs = lax.rsqrt(jnp.mean(x1 * x1, axis=-1, keepdims=True) + EPS)
                hn = x1 * rs * g_ref[...]
                h = hn * (1.0 + _mod_row(mods_ref, lat, scale)) + _mod_row(mods_ref, lat, shift)
                outs[o][rows, :] = h.astype(BF16)

    row = pl.BlockSpec((rt, d), lambda i: (i, 0))
    ins, specs = [x, mods], [row, pl.BlockSpec(mods.shape, lambda i: (0, 0))]
    if has_y:
        ins.append(y); specs.append(row)
    if has_n:
        ins.append(g.reshape(1, d)); specs.append(pl.BlockSpec((1, d), lambda i: (0, 0)))
    out_shape, out_specs = [], []
    if has_y:
        out_shape.append(jax.ShapeDtypeStruct((t, d), F32)); out_specs.append(row)
    if has_n:
        out_shape.append(jax.ShapeDtypeStruct((t, d), BF16)); out_specs.append(row)
    res = _pcall(body, name=name, grid=(t // rt,), in_specs=specs, out_specs=out_specs,
                 out_shape=out_shape)(*ins)
    return res


def _acc_row(ref, r, val):
    ref[r:r + 1, :] += val


def _row_final(x, z, mods, target, *, gate, name):
    t, d = x.shape
    rt = _row_step(t)
    nsub = rt // TM

    def body(*refs):
        x_ref, mods_ref, z_ref = refs[:3]
        t_refs = refs[3:3 + nsub]
        loss_ref, dx_ref, dz_ref, sums_ref = refs[3 + nsub:]
        i = pl.program_id(0)

        @pl.when(i == 0)
        def _():
            loss_ref[...] = jnp.zeros_like(loss_ref)
            sums_ref[...] = jnp.zeros_like(sums_ref)

        for sub in range(nsub):
            rows = slice(sub * TM, (sub + 1) * TM)
            lat = i * nsub + sub > 0
            gt = _mod_row(mods_ref, lat, gate)
            zz = z_ref[rows, :]
            yv = x_ref[rows, :] + gt * zz
            keep = jnp.where(lat, 1.0, 0.0).astype(F32)
            diff = (yv - t_refs[sub][...]) * keep
            part = jnp.sum(jnp.sum(diff * diff, axis=0, keepdims=True), axis=1, keepdims=True)
            loss_ref[...] += part * (0.5 / d)
            dy = diff * (1.0 / d)
            dx_ref[rows, :] = dy
            dz_ref[rows, :] = (gt * dy).astype(BF16)
            _acc_row(sums_ref, 6, jnp.sum(dy * zz, axis=0, keepdims=True))

    row = pl.BlockSpec((rt, d), lambda i: (i, 0))
    tgt = [pl.BlockSpec((TM, d), lambda i, sub=sub: (jnp.maximum(i * nsub + sub - 1, 0), 0)) for sub in range(nsub)]
    return _pcall(
        body, name=name, grid=(t // rt,),
        in_specs=[row, pl.BlockSpec(mods.shape, lambda i: (0, 0)), row] + tgt,
        out_specs=[pl.BlockSpec((8, 128), lambda i: (0, 0)), row, row,
                   pl.BlockSpec((8, d), lambda i: (0, 0))],
        out_shape=[jax.ShapeDtypeStruct((8, 128), F32), jax.ShapeDtypeStruct((t, d), F32),
                   jax.ShapeDtypeStruct((t, d), BF16), jax.ShapeDtypeStruct((8, d), F32)],
    )(x, mods, z, *([target] * nsub))


def _row_bwd(xn, dxo, dh, mods, g, *, shift, scale, y=None, gate=None, latent_only=False, name):
    t, d = xn.shape
    has_y = y is not None

    def body(*refs):
        refs = list(refs)
        x_ref, dxo_ref, dh_ref, mods_ref, g_ref = refs[:5]
        pos = 5
        if has_y:
            y_ref = refs[pos]; pos += 1
        dx_ref = refs[pos]; pos += 1
        if has_y:
            dy_ref = refs[pos]; pos += 1
        sums_ref = refs[pos]
        i = pl.program_id(0)

        @pl.when(i == 0)
        def _():
            sums_ref[...] = jnp.zeros_like(sums_ref)

        def add_sums(vals, base):
            for r, v in enumerate(vals):
                if v is not None:
                    _acc_row(sums_ref, base + r, v)

        gv = g_ref[...]
        for sub in range(rt // TM):
            rows = slice(sub * TM, (sub + 1) * TM)
            lat = i * (rt // TM) + sub > 0
            x1 = x_ref[rows, :]
            rs = lax.rsqrt(jnp.mean(x1 * x1, axis=-1, keepdims=True) + EPS)
            xh = x1 * rs
            dhv = dh_ref[rows, :]
            dn = dhv * (1.0 + _mod_row(mods_ref, lat, scale))
            dxh = dn * gv
            dx = dxo_ref[rows, :] + rs * (dxh - xh * jnp.mean(dxh * xh, axis=-1, keepdims=True))
            dx_ref[rows, :] = dx
            vals = [jnp.sum(dhv, axis=0, keepdims=True),
                    jnp.sum(dhv * (xh * gv), axis=0, keepdims=True),
                    None,
                    jnp.sum(dn * xh, axis=0, keepdims=True)]
            if has_y:
                dy_ref[rows, :] = (_mod_row(mods_ref, lat, gate) * dx).astype(BF16)
                vals[2] = jnp.sum(dx * y_ref[rows, :], axis=0, keepdims=True)
            if sub == 0:
                pl.when(i == 0)(functools.partial(add_sums, vals, 0))
                pl.when(i > 0)(functools.partial(add_sums, vals, 4))
            else:
                add_sums(vals, 4)

    rt = TM if latent_only else _row_step(t)
    row = pl.BlockSpec((rt, d), lambda i: (i, 0))
    ins = [xn, dxo, dh, mods, g.reshape(1, d)]
    specs = [row, row, row, pl.BlockSpec(mods.shape, lambda i: (0, 0)), pl.BlockSpec((1, d), lambda i: (0, 0))]
    if latent_only:
        out_shape = [jax.ShapeDtypeStruct((t - TM, d), F32)]
        out_specs = [pl.BlockSpec((TM, d), lambda i: (jnp.maximum(i - 1, 0), 0))]
    else:
        out_shape, out_specs = [jax.ShapeDtypeStruct((t, d), F32)], [row]
    if has_y:
        ins.append(y); specs.append(row)
        out_shape.append(jax.ShapeDtypeStruct((t, d), BF16)); out_specs.append(row)
    out_shape.append(jax.ShapeDtypeStruct((8, d), F32))
    out_specs.append(pl.BlockSpec((8, d), lambda i: (0, 0)))
    return _pcall(body, name=name, grid=(t // rt,), in_specs=specs, out_specs=out_specs,
                  out_shape=out_shape)(*ins)


FFN_BK = 1408


FFN_SUB = 256


def _ffn_order(n2):
    nb = n2 // (2 * FFN_BK)
    return [h * nb + j for j in range(nb) for h in (0, 1)]


def _ffn_interleave(w):
    return jnp.concatenate([w[..., b * FFN_BK:(b + 1) * FFN_BK] for b in _ffn_order(w.shape[-1])], axis=-1)


def _ffn_deinterleave(w):
    order = _ffn_order(w.shape[-1])
    return jnp.concatenate([w[..., order.index(b) * FFN_BK:(order.index(b) + 1) * FFN_BK]
                            for b in range(len(order))], axis=-1)


def _big_tile(t):
    return 768 if t % 768 == 0 else TM


def _ffn_in(h, w, *, lead, name):
    t, d = h.shape
    n2 = w.shape[-1]
    bm, bk = _big_tile(t), FFN_BK

    def body(h_ref, w_ref, u_ref, a_ref):
        hb = h_ref[...]
        for c0 in range(0, bk, FFN_SUB):
            c1 = min(c0 + FFN_SUB, bk)
            ug = _dot(hb, w_ref[:, c0:c1]).astype(BF16)
            uu = _dot(hb, w_ref[:, bk + c0:bk + c1]).astype(BF16)
            u_ref[:, c0:c1] = ug
            u_ref[:, bk + c0:bk + c1] = uu
            gv, up = ug.astype(F32), uu.astype(F32)
            a_ref[:, c0:c1] = (gv * _sigmoid(gv) * up).astype(BF16)

    return _pcall(
        body, name=name, grid=(t // bm, n2 // (2 * bk)),
        in_specs=[pl.BlockSpec((bm, d), lambda i, j: (i, 0)),
                  pl.BlockSpec((None, d, 2 * bk), lambda i, j: (lead, 0, j))],
        out_specs=[pl.BlockSpec((bm, 2 * bk), lambda i, j: (i, j)), pl.BlockSpec((bm, bk), lambda i, j: (i, j))],
        out_shape=[jax.ShapeDtypeStruct((t, n2), BF16), jax.ShapeDtypeStruct((t, n2 // 2), BF16)],
    )(h, w)


def _ffn_dx(dz, w_out, u, *, lead, name):
    t, d = dz.shape
    n2 = u.shape[1]
    bm, bk = _big_tile(t), FFN_BK

    def body(dz_ref, w_ref, u_ref, du_ref):
        dzb = dz_ref[...]
        for c0 in range(0, bk, FFN_SUB):
            c1 = min(c0 + FFN_SUB, bk)
            da = _dot_nt(dzb, w_ref[c0:c1, :])
            gv, up = u_ref[:, c0:c1].astype(F32), u_ref[:, bk + c0:bk + c1].astype(F32)
            s = _sigmoid(gv)
            du_ref[:, c0:c1] = (da * up * (s * (1.0 + gv * (1.0 - s)))).astype(BF16)
            du_ref[:, bk + c0:bk + c1] = (da * gv * s).astype(BF16)

    ublk = pl.BlockSpec((bm, 2 * bk), lambda i, j: (i, j))
    return _pcall(
        body, name=name, grid=(t // bm, n2 // (2 * bk)),
        in_specs=[pl.BlockSpec((bm, d), lambda i, j: (i, 0)),
                  pl.BlockSpec((None, bk, d), lambda i, j: (lead, j, 0)), ublk],
        out_specs=ublk, out_shape=jax.ShapeDtypeStruct((t, n2), BF16),
    )(dz, w_out, u)


def _lane(shape):
    return _iota(shape, len(shape) - 1)


def _pair_norm(x, g):
    lo = _lane(x.shape) < 64
    x2 = x * x
    s_lo = jnp.sum(jnp.where(lo, x2, 0.0), axis=-1, keepdims=True)
    s_hi = jnp.sum(jnp.where(lo, 0.0, x2), axis=-1, keepdims=True)
    rs = lax.rsqrt(jnp.where(lo, s_lo, s_hi) * (1.0 / 64) + EPS)
    return x * rs, rs


def _pair_mean(v):
    lo = _lane(v.shape) < 64
    s_lo = jnp.sum(jnp.where(lo, v, 0.0), axis=-1, keepdims=True)
    s_hi = jnp.sum(jnp.where(lo, 0.0, v), axis=-1, keepdims=True)
    return jnp.where(lo, s_lo, s_hi) * (1.0 / 64)


def _rot64(x):
    r1 = pltpu.roll(x, 32, 1)
    r2 = pltpu.roll(x, 96, 1)
    even = ((_lane(x.shape) >> 5) & 1) == 0
    return jnp.where(even, -r2, r1)


def _rope64(x, cos, sin):
    return x * cos + _rot64(x) * sin


def _rope64_t(d, cos, sin):
    return d * cos - _rot64(d * sin)


def _kprep_fwd(p, gk, cos, sin, *, name):
    t = p.shape[0]

    def body(k_ref, g_ref, c_ref, s_ref, o_ref):
        xh, _ = _pair_norm(k_ref[...], None)
        o_ref[...] = _rope64(xh * g_ref[...], c_ref[...], s_ref[...])

    blk = pl.BlockSpec((TM, 128), lambda i: (i, 0))
    return _pcall(
        body, name=name, grid=(t // TM,),
        in_specs=[pl.BlockSpec((TM, 128), lambda i: (i, 4)), pl.BlockSpec((1, 128), lambda i: (0, 0)), blk, blk],
        out_specs=blk, out_shape=jax.ShapeDtypeStruct((t, 128), F32),
    )(p, gk, cos, sin)


def _kprep_bwd(p, gk, cos, sin, dkp, dv, *, name):
    t = p.shape[0]

    def body(k_ref, g_ref, c_ref, s_ref, dkp_ref, dv_ref, o_ref, dg_ref):
        @pl.when(pl.program_id(0) == 0)
        def _():
            dg_ref[...] = jnp.zeros_like(dg_ref)
        xh, rs = _pair_norm(k_ref[...], None)
        dn = _rope64_t(dkp_ref[...], c_ref[...], s_ref[...])
        _acc_row(dg_ref, 0, jnp.sum(dn * xh, axis=0, keepdims=True))
        dxh = dn * g_ref[...]
        o_ref[:, 0:128] = (rs * (dxh - xh * _pair_mean(dxh * xh))).astype(BF16)
        o_ref[:, 128:256] = dv_ref[...].astype(BF16)

    blk = pl.BlockSpec((TM, 128), lambda i: (i, 0))
    return _pcall(
        body, name=name, grid=(t // TM,),
        in_specs=[pl.BlockSpec((TM, 128), lambda i: (i, 4)), pl.BlockSpec((1, 128), lambda i: (0, 0)), blk, blk, blk, blk],
        out_specs=[pl.BlockSpec((TM, 256), lambda i: (i, 0)), pl.BlockSpec((8, 128), lambda i: (0, 0))],
        out_shape=[jax.ShapeDtypeStruct((t, 256), BF16), jax.ShapeDtypeStruct((8, 128), F32)],
    )(p, gk, cos, sin, dkp, dv)


def _attn_common(i, t, lc, kp_ref, v_ref):
    span = QB + 2 * WINDOW
    start = pl.multiple_of(jnp.clip(i * QB - WINDOW, lc, t - span), WINDOW)
    kall = jnp.concatenate([kp_ref[0:lc, :], kp_ref[pl.ds(start, span), :]], axis=0)
    vall = jnp.concatenate([v_ref[0:lc, :], v_ref[pl.ds(start, span), :]], axis=0)
    nk = lc + span
    col = _iota((QB, nk), 1)
    krow = jnp.where(col < lc, col, start + col - lc)
    qrow = i * QB + _iota((QB, nk), 0)
    valid = (col < lc) | ((qrow >= lc) & (krow >= lc) & (jnp.abs(krow - qrow) <= WINDOW))
    lo = _lane(kall.shape) < 64
    kroll, vroll = pltpu.roll(kall, 64, 1), pltpu.roll(vall, 64, 1)
    zero = jnp.zeros_like(kall)
    kvar = [[_bf(jnp.where(lo, kall, zero)), _bf(jnp.where(lo, zero, kroll))],
            [_bf(jnp.where(lo, kroll, zero)), _bf(jnp.where(lo, zero, kall))]]
    vvar = [[_bf(jnp.where(lo, vall, zero)), _bf(jnp.where(lo, zero, vroll))],
            [_bf(jnp.where(lo, vroll, zero)), _bf(jnp.where(lo, zero, vall))]]
    return start, valid, kvar, vvar


def _softmax_sink(s, valid, snk):
    s = jnp.where(valid, s, NEG)
    m = jnp.maximum(jnp.max(s, axis=-1, keepdims=True), snk)
    e = jnp.exp(s - m)
    es = jnp.exp(snk - m)
    inv = 1.0 / (jnp.sum(e, axis=-1, keepdims=True) + es)
    return e * inv, es * inv


def _attn_fwd(p, kp, gq, sink, cos, sin, *, lc, name):
    t = p.shape[0]
    scale = 64 ** -0.5

    def body(q_ref, kp_ref, v_ref, g_ref, sink_ref, c_ref, s_ref, o_ref):
        i = pl.program_id(0)
        _, valid, kvar, vvar = _attn_common(i, t, lc, kp_ref, v_ref)
        cosv, sinv, gv = c_ref[...], s_ref[...], g_ref[...]
        for j in range(4):
            xh, _ = _pair_norm(q_ref[:, 128 * j:128 * j + 128], None)
            q2 = _bf(_rope64(xh * gv, cosv, sinv) * scale)
            acc = jnp.zeros((QB, 128), F32)
            for half in range(2):
                s = _dot_nt(q2, kvar[j // 2][half])
                pr, _ = _softmax_sink(s, valid, sink_ref[2 * j + half])
                acc = acc + _dot(pr, vvar[j // 2][half])
            o_ref[:, 128 * j:128 * j + 128] = acc.astype(BF16)

    qblk = pl.BlockSpec((QB, 128), lambda i: (i, 0))
    return _pcall(
        body, name=name, grid=(t // QB,),
        in_specs=[pl.BlockSpec((QB, 512), lambda i: (i, 0)),
                  pl.BlockSpec((t, 128), lambda i: (0, 0)),
                  pl.BlockSpec((t, 128), lambda i: (0, 5)),
                  pl.BlockSpec((1, 128), lambda i: (0, 0)),
                  pl.BlockSpec(memory_space=pltpu.SMEM), qblk, qblk],
        out_specs=pl.BlockSpec((QB, 512), lambda i: (i, 0)),
        out_shape=jax.ShapeDtypeStruct((t, 512), BF16),
    )(p, kp, p, gq, sink, cos, sin)


def _attn_bwd(p, kp, gq, sink, cos, sin, dmix, *, lc, name):
    t = p.shape[0]
    scale = 64 ** -0.5
    span = QB + 2 * WINDOW

    def body(q_ref, kp_ref, v_ref, g_ref, sink_ref, c_ref, s_ref, do_ref,
             dq_ref, dk_ref, dv_ref, dg_ref, dsink_ref):
        i = pl.program_id(0)

        @pl.when(i == 0)
        def _():
            dk_ref[...] = jnp.zeros_like(dk_ref)
            dv_ref[...] = jnp.zeros_like(dv_ref)
            dg_ref[...] = jnp.zeros_like(dg_ref)
            dsink_ref[...] = jnp.zeros_like(dsink_ref)

        start, valid, kvar, vvar = _attn_common(i, t, lc, kp_ref, v_ref)
        cosv, sinv, gv = c_ref[...], s_ref[...], g_ref[...]
        nk = lc + span
        dkt = [jnp.zeros((64, nk), F32), jnp.zeros((64, nk), F32)]
        dvt = [jnp.zeros((64, nk), F32), jnp.zeros((64, nk), F32)]
        for j in range(4):
            kvh = j // 2
            xh, rs = _pair_norm(q_ref[:, 128 * j:128 * j + 128], None)
            q2 = _bf(_rope64(xh * gv, cosv, sinv) * scale)
            do2 = _bf(do_ref[:, 128 * j:128 * j + 128])
            dq2 = jnp.zeros((QB, 128), F32)
            for half in range(2):
                s = _dot_nt(q2, kvar[kvh][half])
                pr, ps = _softmax_sink(s, valid, sink_ref[2 * j + half])
                dp = _dot_nt(do2, vvar[kvh][half])
                delta = jnp.sum(pr * dp, axis=-1, keepdims=True)
                ds = pr * (dp - delta)
                dsk = jnp.sum(jnp.sum(-ps * delta, axis=0, keepdims=True), axis=1, keepdims=True)
                _acc_row(dsink_ref, 2 * j + half, jnp.broadcast_to(dsk, (1, 128)))
                dq2 = dq2 + _dot(ds, kvar[kvh][half])
                hrows = slice(64 * half, 64 * half + 64)
                dkt[kvh] = dkt[kvh] + _dot_tn(q2, ds)[hrows]
                dvt[kvh] = dvt[kvh] + _dot_tn(do2, pr)[hrows]
            dn = _rope64_t(dq2 * scale, cosv, sinv)
            _acc_row(dg_ref, 0, jnp.sum(dn * xh, axis=0, keepdims=True))
            dxh = dn * gv
            dq_ref[:, 128 * j:128 * j + 128] = (rs * (dxh - xh * _pair_mean(dxh * xh))).astype(BF16)
        dk_all = jnp.concatenate(dkt, axis=0).T
        dv_all = jnp.concatenate(dvt, axis=0).T
        dk_ref[0:lc, :] += dk_all[0:lc]
        dv_ref[0:lc, :] += dv_all[0:lc]
        dk_ref[pl.ds(start, span), :] += dk_all[lc:nk]
        dv_ref[pl.ds(start, span), :] += dv_all[lc:nk]

    qblk = pl.BlockSpec((QB, 128), lambda i: (i, 0))
    full = pl.BlockSpec((t, 128), lambda i: (0, 0))
    small = pl.BlockSpec((8, 128), lambda i: (0, 0))
    return _pcall(
        body, name=name, grid=(t // QB,),
        in_specs=[pl.BlockSpec((QB, 512), lambda i: (i, 0)), full,
                  pl.BlockSpec((t, 128), lambda i: (0, 5)),
                  pl.BlockSpec((1, 128), lambda i: (0, 0)),
                  pl.BlockSpec(memory_space=pltpu.SMEM), qblk, qblk,
                  pl.BlockSpec((QB, 512), lambda i: (i, 0))],
        out_specs=[pl.BlockSpec((QB, 512), lambda i: (i, 0)), full, full, small, small],
        out_shape=[jax.ShapeDtypeStruct((t, 512), BF16), jax.ShapeDtypeStruct((t, 128), F32),
                   jax.ShapeDtypeStruct((t, 128), F32), jax.ShapeDtypeStruct((8, 128), F32),
                   jax.ShapeDtypeStruct((8, 128), F32)],
    )(p, kp, p, gq, sink, cos, sin, dmix)


def _tri(rev):
    r, c = _iota((CHUNK, CHUNK), 0), _iota((CHUNK, CHUNK), 1)
    return (c >= r) if rev else (c <= r)


def _blk_map(nb, rev, backward):
    if not rev:
        return (lambda n: nb - 1 - n) if backward else (lambda n: n)
    if backward:
        return lambda n: jnp.where(n < nb - 1, n + 1, 0)
    return lambda n: jnp.where(n == 0, 0, nb - n)


def _chunk_order(rev, backward, nc=TM // CHUNK):
    order = list(range(nc))
    return order[::-1] if (rev != backward) else order


def _hgrn_gates(qraw, fraw, lb):
    sq = _sigmoid(qraw)
    sf = _sigmoid(fraw)
    f = lb + (1.0 - lb) * sf
    return qraw * sq, 1.0 - f, jnp.log(f), sq, sf, f


HGRN_HP = 4


def _chunk_cumsum(x, rev):
    n = x.shape[0]
    pos = _iota(x.shape, 0) & (CHUNK - 1)
    s = 1
    while s < CHUNK:
        if rev:
            x = x + jnp.where(pos < CHUNK - s, pltpu.roll(x, n - s, 0), 0.0)
        else:
            x = x + jnp.where(pos >= s, pltpu.roll(x, s, 0), 0.0)
        s *= 2
    return x


def _block_terms(lf, rev):
    b = _chunk_cumsum(lf, rev)
    mid, last = (CHUNK // 2 - 1, 0) if rev else (CHUNK // 2, CHUNK - 1)

    def chunk_row(off):
        return jnp.concatenate([jnp.broadcast_to(b[c * CHUNK + off:c * CHUNK + off + 1, :], (CHUNK, b.shape[1]))
                                for c in range(TM // CHUNK)], axis=0)

    r, bl = chunk_row(mid), chunk_row(last)
    return _tri(rev), jnp.exp(b - r), jnp.exp(r - b), jnp.exp(b), jnp.exp(bl - b), jnp.exp(bl)


def _headnorm_apply(o, gv, gain):
    n = o * lax.rsqrt(jnp.mean(o * o, axis=-1, keepdims=True) + EPS)
    if gain is not None:
        n = n * gain
    return (n * (gv * _sigmoid(gv))).astype(BF16)


def _headnorm_grad(o, gv, dy, gain):
    rs = lax.rsqrt(jnp.mean(o * o, axis=-1, keepdims=True) + EPS)
    xh = o * rs
    n = xh * gain if gain is not None else xh
    sg = _sigmoid(gv)
    dn = dy * (gv * sg)
    dg = (dy * n * (sg * (1.0 + gv * (1.0 - sg)))).astype(BF16)
    dgain = jnp.sum(dn * xh, axis=0, keepdims=True)
    dxh = dn * gain if gain is not None else dn
    return rs * (dxh - xh * jnp.mean(dxh * xh, axis=-1, keepdims=True)), dg, dgain


def _hgrn_cols(bmap, n2, c0):
    return [pl.BlockSpec((TM, 256), lambda h, n, b=b: (bmap(n), c0 // 2 + h * n2 + b)) for b in range(n2)]


def _head_cols(refs, hh):
    return refs[hh // 2][:, 128 * (hh % 2):128 * (hh % 2) + 128]


def _hgrn_fwd(p, lb, *, rev, name, ofw=None, gain=None):
    t = p.shape[0]
    nb, nc = t // TM, TM // CHUNK
    bmap = _blk_map(nb, rev, False)
    fcol = 14 if rev else 10
    fused = ofw is not None

    n2 = HGRN_HP // 2

    def body(*refs):
        q_refs, f_refs, v_refs, lb_ref = refs[:n2], refs[n2:2 * n2], refs[2 * n2:3 * n2], refs[3 * n2]
        rest = refs[3 * n2 + 1:]
        if fused:
            ofw_ref, g_refs, gain_ref = rest[0], rest[1:1 + n2], rest[1 + n2]
            o_ref, sh_ref, mix_ref, st = rest[2 + n2:]
        else:
            o_ref, sh_ref, st = rest

        @pl.when(pl.program_id(1) == 0)
        def _():
            st[...] = jnp.zeros_like(st)
        for hh in range(HGRN_HP):
            ln = slice(128 * hh, 128 * hh + 128)
            q, k, lf, _, _, _ = _hgrn_gates(_head_cols(q_refs, hh), _head_cols(f_refs, hh), lb_ref[:, ln])
            tri, eq, ek, ei, eki, eb = _block_terms(lf, rev)
            qe, ke, qi, ki, vb = _bf(q * eq), _bf(k * ek), _bf(q * ei), _bf(k * eki), _bf(_head_cols(v_refs, hh))
            intra = []
            for cc in range(nc):
                rows = slice(cc * CHUNK, (cc + 1) * CHUNK)
                a = jnp.where(tri, _dot_nt(qe[rows], ke[rows]), 0.0)
                intra.append(_dot(a, vb[rows]))
            s = st[hh]
            for cc in _chunk_order(rev, False):
                rows = slice(cc * CHUNK, (cc + 1) * CHUNK)
                sh_ref[hh, cc] = s.astype(sh_ref.dtype)
                o_ref[rows, ln] = intra[cc] + _dot_nt(qi[rows], s)
                s = s * eb[cc * CHUNK:cc * CHUNK + 1, :] + _dot_tn(vb[rows], ki[rows])
            st[hh] = s
            if fused:
                osum = o_ref[:, ln] + ofw_ref[:, ln]
                o_ref[:, ln] = osum
                mix_ref[:, ln] = _headnorm_apply(osum, _head_cols(g_refs, hh), gain_ref[...])

    hp, wd = HGRN_HP, 128 * HGRN_HP
    col = functools.partial(_hgrn_cols, bmap, n2)
    oblk = pl.BlockSpec((TM, wd), lambda h, n: (bmap(n), h))
    ins = [p] * (3 * n2) + [lb]
    specs = col(6) + col(fcol) + col(18) + [pl.BlockSpec((1, wd), lambda h, n: (0, h))]
    out_specs = [oblk, pl.BlockSpec((hp, nc, 128, 128), lambda h, n: (h, bmap(n), 0, 0))]
    out_shape = [jax.ShapeDtypeStruct((t, 512), F32), jax.ShapeDtypeStruct((4, t // CHUNK, 128, 128), BF16)]
    if fused:
        ins += [ofw] + [p] * n2 + [gain]
        specs += [oblk] + col(22) + [pl.BlockSpec((1, 128), lambda h, n: (0, 0))]
        out_specs.append(oblk)
        out_shape.append(jax.ShapeDtypeStruct((t, 512), BF16))
    return _pcall(body, name=name, grid=(4 // hp, nb), in_specs=specs, out_specs=out_specs, out_shape=out_shape,
                  scratch_shapes=[pltpu.VMEM((hp, 128, 128), F32)])(*ins)


def _hgrn_bwd(p, lb, sh, do, prev, *, rev, name, head=None):
    t = p.shape[0]
    nb, nc = t // TM, TM // CHUNK
    bmap = _blk_map(nb, rev, True)
    fcol = 14 if rev else 10
    has_prev = prev is not None
    odt = BF16
    fused = head is not None

    n2 = HGRN_HP // 2

    def body(*refs):
        refs = list(refs)
        q_refs, f_refs, v_refs = refs[:n2], refs[n2:2 * n2], refs[2 * n2:3 * n2]
        lb_ref, sh_ref = refs[3 * n2], refs[3 * n2 + 1]
        pos = 3 * n2 + 2
        if fused:
            osum_ref, g_refs, dmix_ref, gain_ref = refs[pos], refs[pos + 1:pos + 1 + n2], refs[pos + 1 + n2], refs[pos + 2 + n2]
            pos += 3 + n2
        else:
            do_ref = refs[pos]
            pos += 1
        if has_prev:
            pq_ref, pv_ref = refs[pos], refs[pos + 1]
            pos += 2
        dq_ref, df_ref, dv_ref, dlb_ref = refs[pos:pos + 4]
        pos += 4
        if fused:
            do_out, dg_ref, dgain_ref = refs[pos:pos + 3]
            pos += 3
        dst = refs[pos]

        @pl.when(pl.program_id(1) == 0)
        def _():
            dst[...] = jnp.zeros_like(dst)
            dlb_ref[...] = jnp.zeros_like(dlb_ref)

        if fused:
            @pl.when((pl.program_id(0) == 0) & (pl.program_id(1) == 0))
            def _():
                dgain_ref[...] = jnp.zeros_like(dgain_ref)

        cat = functools.partial(jnp.concatenate, axis=0)
        for hh in range(HGRN_HP):
            ln = slice(128 * hh, 128 * hh + 128)
            lbv = lb_ref[:, ln]
            qraw, fraw = _head_cols(q_refs, hh), _head_cols(f_refs, hh)
            q, k, lf, sq, sf, f = _hgrn_gates(qraw, fraw, lbv)
            tri, eq, ek, ei, eki, eb = _block_terms(lf, rev)
            qe, ke, qi, ki = q * eq, k * ek, q * ei, k * eki
            if fused:
                dov, dg, dgain = _headnorm_grad(osum_ref[:, ln], _head_cols(g_refs, hh), dmix_ref[:, ln], gain_ref[...])
                do_out[:, ln] = _bf(dov)
                dg_ref[:, ln] = dg
                _acc_row(dgain_ref, 0, dgain)
            else:
                dov = do_ref[:, ln]
            qeb, keb, qib, kib, vb, dob = _bf(qe), _bf(ke), _bf(qi), _bf(ki), _bf(_head_cols(v_refs, hh)), _bf(dov)
            dv, dqe, dke, dqi = [None] * nc, [None] * nc, [None] * nc, [None] * nc
            for cc in range(nc):
                rows = slice(cc * CHUNK, (cc + 1) * CHUNK)
                a = jnp.where(tri, _dot_nt(qeb[rows], keb[rows]), 0.0)
                da = jnp.where(tri, _dot_nt(dob[rows], vb[rows]), 0.0)
                dv[cc] = _dot_tn(a, dob[rows])
                dqe[cc], dke[cc] = _dot(da, keb[rows]), _dot_tn(da, qeb[rows])
                dqi[cc] = _dot(dob[rows], sh_ref[hh, cc])
            dki, dbl = [None] * nc, [None] * nc
            ds = dst[hh]
            for cc in _chunk_order(rev, True):
                rows = slice(cc * CHUNK, (cc + 1) * CHUNK)
                ebc = eb[cc * CHUNK:cc * CHUNK + 1, :]
                dv[cc] = dv[cc] + _dot_nt(kib[rows], ds)
                dki[cc] = _dot(vb[rows], ds)
                dbl[cc] = jnp.broadcast_to(jnp.sum(dki[cc] * ki[rows], axis=0, keepdims=True)
                                           + jnp.sum(ds * sh_ref[hh, cc], axis=0, keepdims=True) * ebc, (CHUNK, 128))
                ds = ds * ebc + _dot_tn(dob[rows], qib[rows])
            dst[hh] = ds
            dqe, dke, dqi, dki, dv, dbl = cat(dqe), cat(dke), cat(dqi), cat(dki), cat(dv), cat(dbl)
            dq = dqe * eq + dqi * ei
            dk = dke * ek + dki * eki
            last = 0 if rev else CHUNK - 1
            db = dqe * qe - dke * ke + dqi * qi - dki * ki
            db = db + jnp.where((_iota(db.shape, 0) & (CHUNK - 1)) == last, dbl, 0.0)
            dlf = _chunk_cumsum(db, not rev)
            dqr = dq * (sq * (1.0 + qraw * (1.0 - sq)))
            dfv = dlf / f - dk
            dfr = dfv * (1.0 - lbv) * (sf * (1.0 - sf))
            dlb_ref[:, ln] += jnp.sum(dfv * (1.0 - sf), axis=0, keepdims=True)
            if has_prev:
                dqr = dqr + pq_ref[:, ln]
                dv = dv + pv_ref[:, ln]
            dq_ref[:, ln] = dqr.astype(odt)
            df_ref[:, ln] = dfr.astype(odt)
            dv_ref[:, ln] = dv.astype(odt)

    hp, wd = HGRN_HP, 128 * HGRN_HP
    col = functools.partial(_hgrn_cols, bmap, n2)
    oblk = pl.BlockSpec((TM, wd), lambda h, n: (bmap(n), h))
    ins = [p] * (3 * n2) + [lb, sh]
    specs = col(6) + col(fcol) + col(18) + [pl.BlockSpec((1, wd), lambda h, n: (0, h)),
                                            pl.BlockSpec((hp, nc, 128, 128), lambda h, n: (h, bmap(n), 0, 0))]
    if fused:
        osum, dmix, gain = head
        ins += [osum] + [p] * n2 + [dmix, gain]
        specs += [oblk] + col(22) + [pl.BlockSpec((TM, wd), lambda h, n: (bmap(n), 4 // hp + h)),
                                     pl.BlockSpec((1, 128), lambda h, n: (0, 0))]
    else:
        ins.append(do); specs.append(oblk)
    if has_prev:
        ins += list(prev); specs += [oblk, oblk]
    out_specs = [oblk, oblk, oblk, pl.BlockSpec((1, wd), lambda h, n: (0, h))]
    out_shape = [jax.ShapeDtypeStruct((t, 512), odt)] * 3 + [jax.ShapeDtypeStruct((1, 512), F32)]
    if fused:
        out_specs += [oblk, oblk, pl.BlockSpec((8, 128), lambda h, n: (0, 0))]
        out_shape += [jax.ShapeDtypeStruct((t, 512), BF16), jax.ShapeDtypeStruct((t, 512), BF16),
                      jax.ShapeDtypeStruct((8, 128), F32)]
    return _pcall(body, name=name, grid=(4 // hp, nb), in_specs=specs, out_specs=out_specs, out_shape=out_shape,
                  scratch_shapes=[pltpu.VMEM((hp, 128, 128), F32)])(*ins)


def _rope256(x, cos, sin):
    x1, x2 = x[:, 0:128], x[:, 128:256]
    return jnp.concatenate([x1 * cos - x2 * sin, x2 * cos + x1 * sin], axis=-1)


def _rope256_t(d, cos, sin):
    d1, d2 = d[:, 0:128], d[:, 128:256]
    return jnp.concatenate([d1 * cos + d2 * sin, d2 * cos - d1 * sin], axis=-1)


RET_DK, RET_DV, RET_H = 256, 512, 4
RET_KSCALE = RET_DK ** -0.5
RCH = TM
RET_HP = 4


def _ret_terms(lg, rev):
    r, c = _iota((RCH, RCH), 0), _iota((RCH, RCH), 1)
    rel = ((c - r) if rev else (r - c)).astype(F32)
    dmat = jnp.where(rel >= 0, jnp.exp(lg[:, 0:1] * jnp.maximum(rel, 0.0)), 0.0)
    pos = _iota((RCH, 1), 0).astype(F32)
    cnt = (RCH - pos) if rev else (pos + 1.0)
    ei = jnp.exp(lg * cnt)
    eki = jnp.exp(lg * (RCH - cnt))
    eb = jnp.exp(lg * float(RCH))
    return dmat, ei, eki, eb


def _ret_fwd(p, lgt, cos, sin, *, rev, name, ofw=None):
    t = p.shape[0]
    nb, nc = t // TM, TM // RCH
    bmap = _blk_map(nb, rev, False)
    fused = ofw is not None

    def body(*refs):
        q_ref, k_ref, v_ref, lg_ref, c_ref, s_ref = refs[:6]
        if fused:
            ofw_ref, g_ref, o_ref, sh_ref, mix_ref, st = refs[6:]
        else:
            o_ref, sh_ref, st = refs[6:]

        @pl.when(pl.program_id(1) == 0)
        def _():
            st[...] = jnp.zeros_like(st)
        for hh in range(RET_HP):
            qc, vc = slice(RET_DK * hh, RET_DK * (hh + 1)), slice(RET_DV * hh, RET_DV * (hh + 1))
            dmat, ei, eki, eb = _ret_terms(lg_ref[hh], rev)
            for cc in _chunk_order(rev, False, nc):
                rows = slice(cc * RCH, (cc + 1) * RCH)
                cosv, sinv = c_ref[rows, :], s_ref[rows, :]
                q = _rope256(q_ref[rows, qc].astype(F32), cosv, sinv)
                k = _rope256(k_ref[rows, qc].astype(F32), cosv, sinv) * RET_KSCALE
                v = v_ref[rows, vc]
                s0 = st[hh]
                sh_ref[hh, cc] = s0.astype(BF16)
                a = _dot_nt(q, k) * dmat
                o = _dot(a, v) + _dot_nt(q * ei, s0)
                st[hh] = s0 * eb + _dot_tn(v, k * eki)
                if fused:
                    o = o + ofw_ref[rows, vc]
                    mix_ref[rows, vc] = _headnorm_apply(o, g_ref[rows, vc].astype(F32), None)
                o_ref[rows, vc] = o

    hp = RET_HP
    tab = pl.BlockSpec((TM, 128), lambda h, n: (bmap(n), 0))
    oblk = pl.BlockSpec((TM, hp * RET_DV), lambda h, n: (bmap(n), h))
    ins = [p, p, p, lgt, cos, sin]
    specs = [pl.BlockSpec((TM, hp * RET_DK), lambda h, n: (bmap(n), h)),
             pl.BlockSpec((TM, hp * RET_DK), lambda h, n: (bmap(n), RET_H // hp + h)),
             pl.BlockSpec((TM, hp * RET_DV), lambda h, n: (bmap(n), RET_H // hp + h)),
             pl.BlockSpec((hp, 1, RET_DK), lambda h, n: (h, 0, 0)), tab, tab]
    out_specs = [oblk, pl.BlockSpec((hp, nc, RET_DV, RET_DK), lambda h, n: (h, bmap(n), 0, 0))]
    out_shape = [jax.ShapeDtypeStruct((t, RET_H * RET_DV), F32),
                 jax.ShapeDtypeStruct((RET_H, t // RCH, RET_DV, RET_DK), BF16)]
    if fused:
        ins += [ofw, p]
        specs += [oblk, pl.BlockSpec((TM, hp * RET_DV), lambda h, n: (bmap(n), 2 * RET_H // hp + h))]
        out_specs.append(oblk)
        out_shape.append(jax.ShapeDtypeStruct((t, RET_H * RET_DV), BF16))
    return _pcall(body, name=name, grid=(RET_H // hp, nb), in_specs=specs, out_specs=out_specs, out_shape=out_shape,
                  scratch_shapes=[pltpu.VMEM((hp, RET_DV, RET_DK), F32)])(*ins)


def _ret_bwd(p, lgt, cos, sin, sh, do, prev, *, rev, name, head=None):
    t = p.shape[0]
    nb, nc = t // TM, TM // RCH
    bmap = _blk_map(nb, rev, True)
    has_prev = prev is not None
    odt = BF16
    fused = head is not None

    def body(*refs):
        refs = list(refs)
        q_ref, k_ref, v_ref, lg_ref, c_ref, s_ref, sh_ref = refs[:7]
        if fused:
            osum_ref, g_ref, dy_ref, wout_ref = refs[7:11]
            pos = 11
        else:
            do_ref = refs[7]
            pos = 8
        if has_prev:
            pq_ref, pk_ref, pv_ref = refs[pos:pos + 3]
            pos += 3
        dq_ref, dk_ref, dv_ref = refs[pos:pos + 3]
        pos += 3
        if fused:
            do_out, dg_ref = refs[pos:pos + 2]
            pos += 2
        dst = refs[pos]

        @pl.when(pl.program_id(1) == 0)
        def _():
            dst[...] = jnp.zeros_like(dst)

        if fused:
            dmix = _dot_nt(dy_ref[...], wout_ref[...])
        for hh in range(RET_HP):
            qc, vc = slice(RET_DK * hh, RET_DK * (hh + 1)), slice(RET_DV * hh, RET_DV * (hh + 1))
            dmat, ei, eki, eb = _ret_terms(lg_ref[hh], rev)
            for cc in _chunk_order(rev, True, nc):
                rows = slice(cc * RCH, (cc + 1) * RCH)
                cosv, sinv = c_ref[rows, :], s_ref[rows, :]
                q = _rope256(q_ref[rows, qc].astype(F32), cosv, sinv)
                k = _rope256(k_ref[rows, qc].astype(F32), cosv, sinv) * RET_KSCALE
                v = v_ref[rows, vc]
                if fused:
                    dov, dg, _ = _headnorm_grad(osum_ref[rows, vc], g_ref[rows, vc].astype(F32), dmix[rows, vc], None)
                    do_out[rows, vc] = _bf(dov)
                    dg_ref[rows, vc] = dg
                else:
                    dov = do_ref[rows, vc]
                s0 = sh_ref[hh, cc]
                dsc = dst[hh]
                qi, ki = q * ei, k * eki
                a = _dot_nt(q, k) * dmat
                da = _dot_nt(dov, v) * dmat
                dv = _dot_tn(a, dov) + _dot_nt(ki, dsc)
                dqs = _dot(da, k) + _dot(dov, s0) * ei
                dks = _dot_tn(da, q) + _dot(v, dsc) * eki
                dst[hh] = dsc * eb + _dot_tn(dov, qi)
                dq = _rope256_t(dqs, cosv, sinv)
                dk = _rope256_t(dks * RET_KSCALE, cosv, sinv)
                if has_prev:
                    dq = dq + pq_ref[rows, qc]
                    dk = dk + pk_ref[rows, qc]
                    dv = dv + pv_ref[rows, vc]
                dq_ref[rows, qc] = dq.astype(odt)
                dk_ref[rows, qc] = dk.astype(odt)
                dv_ref[rows, vc] = dv.astype(odt)

    hp = RET_HP
    tab = pl.BlockSpec((TM, 128), lambda h, n: (bmap(n), 0))
    qblk = pl.BlockSpec((TM, hp * RET_DK), lambda h, n: (bmap(n), h))
    vblk = pl.BlockSpec((TM, hp * RET_DV), lambda h, n: (bmap(n), h))
    ins = [p, p, p, lgt, cos, sin, sh]
    specs = [qblk, pl.BlockSpec((TM, hp * RET_DK), lambda h, n: (bmap(n), RET_H // hp + h)),
             pl.BlockSpec((TM, hp * RET_DV), lambda h, n: (bmap(n), RET_H // hp + h)),
             pl.BlockSpec((hp, 1, RET_DK), lambda h, n: (h, 0, 0)), tab, tab,
             pl.BlockSpec((hp, nc, RET_DV, RET_DK), lambda h, n: (h, bmap(n), 0, 0))]
    if fused:
        osum, dy, w_out = head
        assert hp == RET_H and w_out.shape[0] == RET_H * RET_DV
        ins += [osum, p, dy, w_out]
        specs += [vblk, pl.BlockSpec((TM, hp * RET_DV), lambda h, n: (bmap(n), 2 * RET_H // hp + h)),
                  pl.BlockSpec((TM, dy.shape[1]), lambda h, n: (bmap(n), 0)),
                  pl.BlockSpec(w_out.shape, lambda h, n: (0, 0))]
    else:
        ins.append(do); specs.append(vblk)
    if has_prev:
        ins += list(prev); specs += [qblk, qblk, vblk]
    out_specs = [qblk, qblk, vblk]
    out_shape = [jax.ShapeDtypeStruct((t, RET_H * RET_DK), odt), jax.ShapeDtypeStruct((t, RET_H * RET_DK), odt),
                 jax.ShapeDtypeStruct((t, RET_H * RET_DV), odt)]
    if fused:
        out_specs += [vblk, vblk]
        out_shape += [jax.ShapeDtypeStruct((t, RET_H * RET_DV), BF16), jax.ShapeDtypeStruct((t, RET_H * RET_DV), BF16)]
    return _pcall(body, name=name, grid=(RET_H // hp, nb), in_specs=specs, out_specs=out_specs, out_shape=out_shape,
                  scratch_shapes=[pltpu.VMEM((hp, RET_DV, RET_DK), F32)])(*ins)


def _rope_tables(lc, l, zero):
    tt = jnp.arange(l)
    row, colp = (tt // 64).astype(F32) + zero, (tt % 64).astype(F32) + zero
    inv = 10000.0 ** (-jnp.arange(16, dtype=F32) / 16)
    ang = jnp.concatenate([row[:, None] * inv, colp[:, None] * inv], axis=-1)
    ang = jnp.concatenate([jnp.zeros((lc, 32), F32), ang], axis=0)
    acos, asin = jnp.tile(jnp.cos(ang), (1, 4)), jnp.tile(jnp.sin(ang), (1, 4))
    theta = 1.0 / (10000.0 ** jnp.linspace(0.0, 1.0, 128, dtype=F32))
    rang = (jnp.arange(l, dtype=F32) + zero)[:, None] * theta
    rang = jnp.concatenate([jnp.zeros((lc, 128), F32), rang], axis=0)
    return acos, asin, jnp.cos(rang), jnp.sin(rang)


class _Weights:
    def __init__(self, w):
        self.w = w

    def landed(self, grp, after):
        pass

    def full(self, grp, after):
        return self.w

    def send_grads(self, grp, grads):
        return jnp.zeros((8, 128), F32)


def _local_step(x0, target, mods, ng, wsrc, small):
    t, d = x0.shape
    l = target.shape[0]
    lc = t - l
    acos, asin, rcos, rsin = _rope_tables(lc, l, small.get('tok', 0.0))
    lg_fw = jnp.log(1.0 - 2.0 ** (-5.0 - jnp.arange(RET_H, dtype=F32)))
    lgt_fw = jnp.broadcast_to(lg_fw[:, None, None], (RET_H, 1, RET_DK))
    lgt_bw = jnp.broadcast_to(lg_fw[::-1][:, None, None], (RET_H, 1, RET_DK))
    gq, gk, sink, gain, lb = small['gq'], small['gk'], small['sink'], small['gain'], small['lb']

    (h1,) = _row_fwd(x0, mods, g=ng[0], shift=0, scale=1, name='l0_norm1')
    wsrc.landed('even', h1)
    w = dict(wsrc.full('even', h1))
    p0 = _mm_nn(h1, w['even_in'], name='l0_in')
    kp = _kprep_fwd(p0, gk, acos, asin, name='l0_kprep')
    att = _attn_fwd(p0, kp, gq, sink, acos, asin, lc=lc, name='l0_attn')
    wsrc.landed('ffn', att)
    hof, hsf = _hgrn_fwd(p0, lb, rev=False, name='l0_hgrn_f')
    wsrc.landed('odd', hof)
    hos, hsb, bmix = _hgrn_fwd(p0, lb, rev=True, name='l0_hgrn_b', ofw=hof, gain=gain)
    mix0 = [att, bmix]
    y0 = _mm_nn(mix0, w['even_out'], name='l0_out')
    x1, h2 = _row_fwd(x0, mods, y=y0, gate=2, g=ng[1], shift=3, scale=4, name='l0_norm2')
    w.update(wsrc.full('ffn', h2))
    u0, a0 = _ffn_in(h2, w['ffn_in'], lead=0, name='ffn_in')
    z0 = _mm_nn(a0, w['ffn_out'], lead=0, name='ffn_out')
    x2, h3 = _row_fwd(x1, mods, y=z0, gate=5, g=ng[2], shift=12, scale=13, name='l1_norm1')
    w.update(wsrc.full('odd', h3))
    p1 = _mm_nn(h3, w['odd_in'], out_dtype=BF16, name='l1_in')
    rof, rsf = _ret_fwd(p1, lgt_fw, rcos, rsin, rev=False, name='l1_ret_f')
    ros, rsb, mix1 = _ret_fwd(p1, lgt_bw, rcos, rsin, rev=True, name='l1_ret_b', ofw=rof)
    y1 = _mm_nn(mix1, w['odd_out'], name='l1_out')
    x3, h4 = _row_fwd(x2, mods, y=y1, gate=14, g=ng[3], shift=15, scale=16, name='l1_norm2')
    u1, a1 = _ffn_in(h4, w['ffn_in'], lead=1, name='ffn_in')
    z1 = _mm_nn(a1, w['ffn_out'], lead=1, name='ffn_out')
    loss, dx4, dz1, s_fin = _row_final(x3, z1, mods, target, gate=17, name='loss')

    du1 = _ffn_dx(dz1, w['ffn_out'], u1, lead=1, name='ffn_out_dx')
    g_ffn_out1 = _mm_tn(a1, dz1, name='ffn_out_dw')
    dh4 = _mm_nt(du1, w['ffn_in'], lead=1, name='ffn_in_dx')
    g_ffn_in1 = _mm_tn(h4, du1, name='ffn_in_dw')
    dx3, dy1, s_l1n2 = _row_bwd(x3, dx4, dh4, mods, ng[3], shift=15, scale=16, y=y1, gate=14, name='l1_norm2_bwd')
    g_odd_out = _mm_tn(mix1, dy1, name='l1_out_dw')
    rdq, rdk, rdv, rdo, rdg = _ret_bwd(p1, lgt_fw, rcos, rsin, rsf, None, None, rev=False, name='l1_ret_f_bwd',
                                       head=(ros, dy1, w['odd_out']))
    rdq, rdk, rdv = _ret_bwd(p1, lgt_bw, rcos, rsin, rsb, rdo, (rdq, rdk, rdv), rev=True, name='l1_ret_b_bwd')
    dp1 = [rdq, rdk, rdv, rdg]
    dh3 = _mm_nt(dp1, w['odd_in'], name='l1_in_dx')
    g_odd_in = _mm_tn(h3, dp1, name='l1_in_dw')
    mods = mods + wsrc.send_grads('early', dict(ffn_in1=g_ffn_in1, ffn_out1=g_ffn_out1, odd_in=g_odd_in,
                                                odd_out=g_odd_out))[0, 0]
    dx2, dz0, s_l1n1 = _row_bwd(x2, dx3, dh3, mods, ng[2], shift=12, scale=13, y=z0, gate=5, name='l1_norm1_bwd')
    du0 = _ffn_dx(dz0, w['ffn_out'], u0, lead=0, name='ffn_out_dx')
    g_ffn_out0 = _mm_tn(a0, dz0, name='ffn_out_dw')
    dh2 = _mm_nt(du0, w['ffn_in'], lead=0, name='ffn_in_dx')
    g_ffn_in0 = _mm_tn(h2, du0, name='ffn_in_dw')
    mods = mods + wsrc.send_grads('mid', dict(ffn_in0=g_ffn_in0, ffn_out0=g_ffn_out0))[0, 0]
    dx1, dy0, s_l0n2 = _row_bwd(x1, dx2, dh2, mods, ng[1], shift=3, scale=4, y=y0, gate=2, name='l0_norm2_bwd')
    dmix0 = _mm_nt(dy0, w['even_out'], name='l0_out_dx')
    g_even_out = _mm_tn(mix0, dy0, name='l0_out_dw')
    hq, hff, hv, dlb_f, hdo, hdg, s_gain = _hgrn_bwd(p0, lb, hsf, None, None, rev=False, name='l0_hgrn_f_bwd',
                                                     head=(hos, dmix0, gain))
    hq, hfb, hv, dlb_b = _hgrn_bwd(p0, lb, hsb, hdo, (hq, hv), rev=True, name='l0_hgrn_b_bwd')
    adq, dkp, adv, s_gq, s_sink = _attn_bwd(p0, kp, gq, sink, acos, asin, dmix0, lc=lc, name='l0_attn_bwd')
    dkv, s_gk = _kprep_bwd(p0, gk, acos, asin, dkp, adv, name='l0_kprep_bwd')
    dp0 = jnp.concatenate([adq, dkv, hq, _bf(hff), hfb, hv, hdg], axis=1)
    dh1 = _mm_nt(dp0, w['even_in'], name='l0_in_dx')
    g_even_in = _mm_tn(h1, dp0, name='l0_in_dw')
    dx0, s_l0n1 = _row_bwd(x0, dx1, dh1, mods, ng[0], shift=0, scale=1, latent_only=True, name='l0_norm1_bwd')

    grads = dict(ffn_in0=g_ffn_in0, ffn_in1=g_ffn_in1, ffn_out0=g_ffn_out0, ffn_out1=g_ffn_out1,
                 even_in=g_even_in, even_out=g_even_out, odd_in=g_odd_in, odd_out=g_odd_out)
    sums = dict(fin=s_fin, l1n2=s_l1n2, l1n1=s_l1n1, l0n2=s_l0n2, l0n1=s_l0n1, gain=s_gain, gq=s_gq, gk=s_gk,
                sink=s_sink, dlb_f=dlb_f, dlb_b=dlb_b)
    return loss, dx0, grads, sums


def _place():
    return lax.axis_index("x"), lax.axis_index("y"), lax.axis_index("c")


def _ag8(blk, *, name):
    r, c = blk.shape
    flips = [(dx, dy, dc) for dx in (0, 1) for dy in (0, 1) for dc in (0, 1) if (dx, dy, dc) != (0, 0, 0)]

    def body(x_ref, out_ref, send_sems, recv_sems, local_sem):
        ax, ay, ac = _place()
        me = 4 * ax + 2 * ay + ac
        mine = pltpu.make_async_copy(x_ref, out_ref.at[me], local_sem)
        mine.start()
        sent = []
        for k, (dx, dy, dc) in enumerate(flips):
            peer = (lax.rem(ax + dx, 2), lax.rem(ay + dy, 2), lax.rem(ac + dc, 2))
            cp = pltpu.make_async_remote_copy(src_ref=x_ref, dst_ref=out_ref.at[me], send_sem=send_sems.at[k],
                                              recv_sem=recv_sems.at[k], device_id=peer, device_id_type=MESH)
            cp.start()
            sent.append((cp, 4 * peer[0] + 2 * peer[1] + peer[2]))
        for k, (cp, pidx) in enumerate(sent):
            pltpu.make_async_remote_copy(src_ref=x_ref, dst_ref=out_ref.at[pidx], send_sem=send_sems.at[k],
                                         recv_sem=recv_sems.at[k], device_id=(ax, ay, ac),
                                         device_id_type=MESH).wait_recv()
        for cp, _ in sent:
            cp.wait_send()
        mine.wait()

    return _pcall(
        body, name=name,
        in_specs=[pl.BlockSpec(memory_space=pltpu.VMEM)],
        out_specs=pl.BlockSpec(memory_space=pltpu.VMEM),
        out_shape=jax.ShapeDtypeStruct((8, r, c), blk.dtype),
        scratch_shapes=[pltpu.SemaphoreType.DMA((7,)), pltpu.SemaphoreType.DMA((7,)), pltpu.SemaphoreType.DMA],
    )(blk)


_HBM = pl.BlockSpec(memory_space=pltpu.HBM)
_SEM = pl.BlockSpec(memory_space=pltpu.SEMAPHORE)
_DATAFLOW = pltpu.SideEffectType.DATAFLOW_SIDE_EFFECTING


def _split_start(bufs, plan, k, *, name):
    n = len(bufs)

    def body(*refs):
        ins, send_sems, recv_sems, token = refs[:n], refs[n], refs[n + 1], refs[2 * n + 2]
        for i, (src, dst, dev) in enumerate(plan(ins)):
            pltpu.make_async_remote_copy(src_ref=src, dst_ref=dst, send_sem=send_sems.at[i], recv_sem=recv_sems.at[i],
                                         device_id=dev, device_id_type=MESH).start()
        token[...] = jnp.zeros_like(token)

    res = _pcall(
        body, name=name,
        out_shape=(pltpu.SemaphoreType.DMA((k,)), pltpu.SemaphoreType.DMA((k,)),
                   *[pltpu.HBM(b.shape, b.dtype) for b in bufs], jax.ShapeDtypeStruct((8, 128), F32)),
        in_specs=[_HBM] * n, out_specs=(_SEM, _SEM, *[_HBM] * n, pl.BlockSpec(memory_space=pltpu.VMEM)),
        input_output_aliases={i: 2 + i for i in range(n)},
        compiler_params=pltpu.CompilerParams(has_side_effects=_DATAFLOW),
    )(*[pltpu.with_memory_space_constraint(b, pltpu.HBM) for b in bufs])
    return res[0], res[1], list(res[2:2 + n]), res[2 + n]


def _split_wait(bufs, send_sems, recv_sems, plan, after, *, name):
    n = len(bufs)

    def body(*refs):
        ins, ssem, rsem = refs[:n], refs[n], refs[n + 1]
        for i, (src, dst, dev) in enumerate(plan(ins)):
            cp = pltpu.make_async_remote_copy(src_ref=src, dst_ref=dst, send_sem=ssem.at[i], recv_sem=rsem.at[i],
                                              device_id=dev, device_id_type=MESH)
            cp.wait_send()
            cp.wait_recv()

    res = _pcall(
        body, name=name, out_shape=tuple(pltpu.HBM(b.shape, b.dtype) for b in bufs),
        in_specs=[_HBM] * n + [_SEM, _SEM, pl.BlockSpec(memory_space=pl.ANY)], out_specs=tuple([_HBM] * n),
        input_output_aliases={i: i for i in range(n)},
        compiler_params=pltpu.CompilerParams(has_side_effects=_DATAFLOW),
    )(*bufs, send_sems, recv_sems, after)
    return list(res)


_CHIP_FLIPS = [(1, 0), (0, 1), (1, 1)]


class _GatheredWeights:
    GROUPS = (('even', ('even_in', 'even_out')), ('ffn', ('ffn_in', 'ffn_out')), ('odd', ('odd_in', 'odd_out')))

    def __init__(self, shards, reducer):
        self.shards = shards
        self.send_grads = reducer.start
        self.ici, self.d2d, self.token = {}, {}, None
        for grp, names in self.GROUPS:
            src = [shards[nm].reshape(2, shards[nm].shape[0] // 2, shards[nm].shape[1]) for nm in names]
            if self.token is not None:
                src, self.token = lax.optimization_barrier((src, self.token))
            land = [lax.empty((4,) + a.shape, a.dtype) for a in src]
            m = len(names)
            sends, recvs, bufs, token = _split_start(src + land, functools.partial(self._ici_plan, m, True), 4 * m,
                                                     name='gather_' + grp + '_ici_start')
            self.ici[grp] = (sends, recvs, bufs, m)
            self.token = token if self.token is None else self.token + token

    @staticmethod
    def _ici_plan(m, sending, refs):
        ax, ay, ac = _place()
        s = 2 * ax + ay
        out = []
        for a in range(m):
            for dx, dy in _CHIP_FLIPS:
                px, py = lax.rem(ax + dx, 2), lax.rem(ay + dy, 2)
                slot = s if sending else 2 * px + py
                out.append((refs[a].at[ac], refs[m + a].at[slot, ac], (px, py, ac)))
        for a in range(m):
            out.append((refs[a], refs[m + a].at[s], (ax, ay, 1 - ac)))
        return out

    @staticmethod
    def _d2d_plan(m, sending, refs):
        ax, ay, ac = _place()
        out = []
        for a in range(m):
            for dx, dy in _CHIP_FLIPS:
                sp = 2 * lax.rem(ax + dx, 2) + lax.rem(ay + dy, 2)
                out.append((refs[a].at[sp, ac], refs[a].at[sp, ac if sending else 1 - ac], (ax, ay, 1 - ac)))
        return out

    def landed(self, grp, after):
        sends, recvs, bufs, m = self.ici[grp]
        bufs = _split_wait(bufs, sends, recvs, functools.partial(self._ici_plan, m, False), after,
                           name='gather_' + grp + '_ici_wait')
        sends, recvs, land, _ = _split_start(bufs[m:], functools.partial(self._d2d_plan, m, True), 3 * m,
                                             name='gather_' + grp + '_d2d_start')
        self.d2d[grp] = (sends, recvs, land, m)

    def full(self, grp, after):
        sends, recvs, land, m = self.d2d[grp]
        land = _split_wait(land, sends, recvs, functools.partial(self._d2d_plan, m, False), after,
                           name='gather_' + grp + '_d2d_wait')
        names = dict(self.GROUPS)[grp]
        return {nm: _from_shards(nm, g.reshape((4,) + self.shards[nm].shape)) for nm, g in zip(names, land)}


def _to_sibling(arrs, *, name):
    n = len(arrs)

    def body(*refs):
        ins, outs = refs[:n], refs[n:2 * n]
        send_sems, recv_sems = refs[2 * n:]
        ax, ay, ac = _place()
        cps = [pltpu.make_async_remote_copy(src_ref=ins[a], dst_ref=outs[a], send_sem=send_sems.at[a],
                                            recv_sem=recv_sems.at[a], device_id=(ax, ay, 1 - ac),
                                            device_id_type=MESH) for a in range(n)]
        for cp in cps:
            cp.start()
        for cp in cps:
            cp.wait_recv()
        for cp in cps:
            cp.wait_send()

    hbm = pl.BlockSpec(memory_space=pl.ANY)
    return _pcall(
        body, name=name, in_specs=[hbm] * n, out_specs=[hbm] * n,
        out_shape=[jax.ShapeDtypeStruct(a.shape, a.dtype) for a in arrs],
        scratch_shapes=[pltpu.SemaphoreType.DMA((n,))] * 2,
    )(*arrs)


def _mod_fwd(cond_raw, mw, mb, *, name):
    _, d, n = mw.shape

    def body(c_ref, w_ref, b_ref, o_ref):
        cv = c_ref[...]
        o_ref[...] = _dot(cv * _sigmoid(cv), w_ref[...]) + b_ref[...]

    return _pcall(
        body, name=name, grid=(2,),
        in_specs=[pl.BlockSpec((16, d), lambda l: (0, 0)), pl.BlockSpec((None, d, n), lambda l: (l, 0, 0)),
                  pl.BlockSpec((None, 1, n), lambda l: (l, 0, 0))],
        out_specs=pl.BlockSpec((None, 16, n), lambda l: (l, 0, 0)),
        out_shape=jax.ShapeDtypeStruct((2, 16, n), F32),
    )(cond_raw, mw, mb)


def _mod_bwd(cond_raw, dms, mw, *, name):
    _, d, n = mw.shape

    def body(c_ref, dm_ref, w_ref, gw_ref, dc_ref):
        @pl.when(pl.program_id(0) == 0)
        def _():
            dc_ref[...] = jnp.zeros_like(dc_ref)
        cv = c_ref[...]
        gw_ref[...] = _dot_tn(cv * _sigmoid(cv), dm_ref[...])
        dc_ref[...] += _dot_nt(dm_ref[...], w_ref[...])

    return _pcall(
        body, name=name, grid=(2,),
        in_specs=[pl.BlockSpec((16, d), lambda l: (0, 0)), pl.BlockSpec((None, 16, n), lambda l: (l, 0, 0)),
                  pl.BlockSpec((None, d, n), lambda l: (l, 0, 0))],
        out_specs=[pl.BlockSpec((None, d, n), lambda l: (l, 0, 0)), pl.BlockSpec((16, d), lambda l: (0, 0))],
        out_shape=[jax.ShapeDtypeStruct((2, d, n), F32), jax.ShapeDtypeStruct((16, d), F32)],
    )(cond_raw, dms, mw)


def _lb_fwd(hgrn_lb, *, name):
    def body(a_ref, o_ref):
        a0, a1 = a_ref[0:1, :], a_ref[1:2, :]
        m = jnp.maximum(a0, a1)
        e0, e1 = jnp.exp(a0 - m), jnp.exp(a1 - m)
        o_ref[...] = e0 / (e0 + e1)

    return _pcall(body, name=name, out_shape=jax.ShapeDtypeStruct((1, hgrn_lb.shape[1]), F32))(hgrn_lb)


PACK_TILES = ('l0n1', 'l0n2', 'l1n1', 'l1n2', 'fin')
PACK_SINGLES = ('gq', 'gk', 'gain', 'dlb_f', 'dlb_b', 'sink')
PACK_ROW = {nm: 8 * i for i, nm in enumerate(PACK_TILES)}
PACK_ROW.update({nm: 8 * len(PACK_TILES) + i for i, nm in enumerate(PACK_SINGLES)})
MOD_SOURCE = ((('l0n1', 0), ('l0n1', 1), ('l0n2', 2), ('l0n2', 0), ('l0n2', 1), ('l1n1', 2)),
              (('l1n1', 0), ('l1n1', 1), ('l1n2', 2), ('l1n2', 0), ('l1n2', 1), ('fin', 2)))


def _small_finalize(gath, lb_pad, *, name):
    d = gath.shape[2]

    def body(g_ref, lb_ref, small_ref, glb_ref, gmb_ref, dm_ref):
        tot = g_ref[0]
        for e in range(1, 8):
            tot = tot + g_ref[e]

        def row(nm, r=0):
            return tot[PACK_ROW[nm] + r:PACK_ROW[nm] + r + 1, :]

        for k, nm in enumerate(('l0n1', 'l0n2', 'l1n1', 'l1n2')):
            small_ref[k:k + 1, :] = row(nm, 3) + row(nm, 7)
        for k, nm in ((4, 'gq'), (5, 'gk')):
            small_ref[k:k + 1, :] = row(nm) + pltpu.roll(row(nm), d - 64, 1)
        small_ref[6:7, :] = row('gain')
        small_ref[7:8, :] = row('sink')
        lbv = lb_ref[...]
        g0 = (row('dlb_f') + row('dlb_b')) * lbv * (1.0 - lbv)
        glb_ref[...] = jnp.zeros_like(glb_ref)
        glb_ref[0:1, :] = g0
        glb_ref[1:2, :] = -g0
        dm_ref[...] = jnp.zeros_like(dm_ref)
        for l in range(2):
            for part in range(6):
                nm, r = MOD_SOURCE[l][part]
                gmb_ref[l * 6 + part:l * 6 + part + 1, :] = row(nm, r) + row(nm, r + 4)
                rl = PACK_ROW[nm] + r + 4
                for e in range(8):
                    dm_ref[l, part, e:e + 1, :] = g_ref[e, rl:rl + 1, :]
                dm_ref[l, part, 8:9, :] = row(nm, r)

    return _pcall(
        body, name=name,
        out_shape=[jax.ShapeDtypeStruct((8, d), F32), jax.ShapeDtypeStruct((8, d), F32),
                   jax.ShapeDtypeStruct((12, d), F32), jax.ShapeDtypeStruct((2, 6, 16, d), F32)],
    )(gath, lb_pad)


def _cctx_grad(gath, c_ctx2, *, name):
    def body(g_ref, c_ref, o_ref):
        tot = ((g_ref[0, 0:1, :] + g_ref[2, 0:1, :]) + g_ref[4, 0:1, :]) + g_ref[6, 0:1, :]
        cv = c_ref[...]
        s = _sigmoid(cv)
        o_ref[...] = tot * (s * (1.0 + cv * (1.0 - s)))

    return _pcall(body, name=name, out_shape=jax.ShapeDtypeStruct(c_ctx2.shape, F32))(gath, c_ctx2)


def _row_block(r, c, limit=256 * 1024):
    best = None
    for br in range(16, r + 1, 16):
        if r % br == 0 and br * c <= limit:
            best = br
    return best if best is not None else r


def _sum4(own, landed, core, *, name):
    _, r, c = own.shape
    br = _row_block(r, c, 512 * 1024)

    def body(core_ref, own_ref, land_ref, o_ref):
        s = 2 * lax.axis_index("x") + lax.axis_index("y")
        p = [jnp.where(s == k, own_ref[k], land_ref[k]).astype(F32) for k in range(4)]
        o_ref[...] = ((p[0] + p[1]) + p[2]) + p[3]

    blk = pl.BlockSpec((4, br, c), lambda i, core_ref: (0, i, 0))
    spec = pltpu.PrefetchScalarGridSpec(
        num_scalar_prefetch=1, grid=(r // br,), in_specs=[blk, blk],
        out_specs=pl.BlockSpec((None, br, c), lambda i, core_ref: (core_ref[0], i, 0)))
    return _pcall(body, name=name, grid_spec=spec, out_shape=jax.ShapeDtypeStruct((2, r, c), F32))(core, own, landed)


def _exchange_halves(arrs, *, name):
    n = len(arrs)

    def body(*refs):
        ins, outs = refs[:n], refs[n:2 * n]
        send_sems, recv_sems = refs[2 * n:]
        ax, ay, ac = _place()
        cps = [pltpu.make_async_remote_copy(src_ref=ins[a].at[ac], dst_ref=outs[a].at[ac], send_sem=send_sems.at[a],
                                            recv_sem=recv_sems.at[a], device_id=(ax, ay, 1 - ac),
                                            device_id_type=MESH) for a in range(n)]
        for cp in cps:
            cp.start()
        for a in range(n):
            pltpu.make_async_remote_copy(src_ref=ins[a].at[ac], dst_ref=outs[a].at[1 - ac], send_sem=send_sems.at[a],
                                         recv_sem=recv_sems.at[a], device_id=(ax, ay, ac),
                                         device_id_type=MESH).wait_recv()
        for cp in cps:
            cp.wait_send()

    hbm = pl.BlockSpec(memory_space=pl.ANY)
    return _pcall(
        body, name=name, in_specs=[hbm] * n, out_specs=[hbm] * n,
        out_shape=[jax.ShapeDtypeStruct(a.shape, a.dtype) for a in arrs],
        input_output_aliases={a: a for a in range(n)},
        scratch_shapes=[pltpu.SemaphoreType.DMA((n,))] * 2,
    )(*arrs)


def _add2(a, b, *, name):
    r, c = a.shape
    br = _row_block(r, c, 1024 * 1024)

    def body(a_ref, b_ref, o_ref):
        o_ref[...] = (a_ref[...].astype(F32) + b_ref[...].astype(F32)).astype(BF16)

    blk = pl.BlockSpec((br, c), lambda i: (i, 0))
    return _pcall(body, name=name, grid=(r // br,), in_specs=[blk, blk], out_specs=blk,
                  out_shape=jax.ShapeDtypeStruct((r, c), BF16))(a, b)


def _adam(w, gs, m, v, *, name):
    r, c = w.shape
    br = _row_block(r, c)
    ng = len(gs)
    c1 = 1.0 - ADAM_B1 ** ADAM_STEP
    c2 = 1.0 - ADAM_B2 ** ADAM_STEP

    def body(*refs):
        w_ref, m_ref, v_ref = refs[0], refs[1 + ng], refs[2 + ng]
        outs = refs[3 + ng:]
        g = refs[1][...]
        for k in range(1, ng):
            g = g + refs[1 + k][...]
        mn = ADAM_B1 * m_ref[...] + (1.0 - ADAM_B1) * g
        vn = ADAM_B2 * v_ref[...] + (1.0 - ADAM_B2) * (g * g)
        if ng > 1:
            outs[0][...] = g
        d_out, m_out, v_out = outs[-3:]
        m_out[...] = mn
        v_out[...] = vn
        d_out[...] = -ADAM_LR * ((mn / c1) / (jnp.sqrt(vn / c2) + ADAM_EPS) + ADAM_WD * w_ref[...])

    blk = pl.BlockSpec((br, c), lambda i: (i, 0))
    nout = 4 if ng > 1 else 3
    res = _pcall(body, name=name, grid=(r // br,), in_specs=[blk] * (3 + ng), out_specs=[blk] * nout,
                 out_shape=[jax.ShapeDtypeStruct((r, c), F32)] * nout)(w, *gs, m, v)
    return list(res) if ng > 1 else [gs[0]] + list(res)


def _grad_halves(name, g, ac):
    if name.endswith('_in'):
        n = g.shape[1] // 4
        if name == 'ffn_in':
            assert n == FFN_BK
        order = _ffn_order(g.shape[1]) if name == 'ffn_in' else range(4)
        v = jnp.stack([g[:, b * n:(b + 1) * n] for b in order])
        per = [v[:, :g.shape[0] // 2], v[:, g.shape[0] // 2:]]
    else:
        k4, n = g.shape
        v = g.reshape(4, 2, k4 // 8, n)
        per = [v[:, 0], v[:, 1]]
    first = ac == 0
    return _bf(jnp.where(first, per[0], per[1])), _bf(jnp.where(first, per[1], per[0]))


class _GradReducer:
    def __init__(self):
        self.flight = {}

    @staticmethod
    def _plan(m, sending, refs):
        ax, ay, ac = _place()
        s = 2 * ax + ay
        out = []
        for a in range(m):
            for dx, dy in _CHIP_FLIPS:
                px, py = lax.rem(ax + dx, 2), lax.rem(ay + dy, 2)
                sp = 2 * px + py
                out.append((refs[a].at[sp], refs[m + a].at[s if sending else sp], (px, py, ac)))
        return out

    def start(self, grp, grads):
        ac = lax.axis_index("c")
        names = list(grads)
        halves = [_grad_halves(nm.rstrip('01'), grads[nm], ac) for nm in names]
        theirs = _to_sibling([h[1] for h in halves], name='swap_core_halves_' + grp)
        pair = [_add2(h[0].reshape(-1, b.shape[-1]), b.reshape(-1, b.shape[-1]), name='add_cores').reshape(b.shape)
                for h, b in zip(halves, theirs)]
        m = len(names)
        land = [lax.empty(a.shape, a.dtype) for a in pair]
        sends, recvs, bufs, token = _split_start(pair + land, functools.partial(self._plan, m, True), 3 * m,
                                                 name='scatter_' + grp + '_start')
        self.flight[grp] = (names, sends, recvs, bufs)
        return token

    def finish(self, grp, after):
        names, sends, recvs, bufs = self.flight.pop(grp)
        m = len(names)
        bufs = _split_wait(bufs, sends, recvs, functools.partial(self._plan, m, False), after,
                           name='scatter_' + grp + '_wait')
        core = lax.axis_index("c").astype(jnp.int32).reshape(1)
        sums = [_sum4(p, l, core, name='sum_chips') for p, l in zip(bufs[:m], bufs[m:])]
        both = _exchange_halves(sums, name='gather_core_halves_' + grp)
        return {nm: g.reshape(-1, g.shape[-1]) for nm, g in zip(names, both)}


def _from_shards(name, g):
    _, r, n = g.shape
    if name == 'ffn_in':
        assert n == FFN_BK
        v = g.reshape(4, 2, r // 2, n)
        return jnp.concatenate([v[b] for b in _ffn_order(4 * n)], axis=-1)
    if name == 'ffn_out':
        return g.reshape(4, 2, r // 2, n).transpose(1, 0, 2, 3).reshape(2, 2 * r, n)
    if name in ('even_in', 'odd_in'):
        return jnp.concatenate([g[b] for b in range(4)], axis=-1)
    return g.reshape(4 * r, n)


def kernel(x, c, ctx, c_ctx, mod_w, mod_b, norm_g, ffn_w_in, ffn_w_out, even_w_in, even_w_out, attn_qk_norm_g, attn_sink, hgrn_out_norm_g, hgrn_lb, odd_w_in, odd_w_out, loss_target, m_c_ctx, m_mod_w, m_mod_b, m_norm_g, m_ffn_w_in, m_ffn_w_out, m_even_w_in, m_even_w_out, m_attn_qk_norm_g, m_attn_sink, m_hgrn_out_norm_g, m_hgrn_lb, m_odd_w_in, m_odd_w_out, v_c_ctx, v_mod_w, v_mod_b, v_norm_g, v_ffn_w_in, v_ffn_w_out, v_even_w_in, v_even_w_out, v_attn_qk_norm_g, v_attn_sink, v_hgrn_out_norm_g, v_hgrn_lb, v_odd_w_in, v_odd_w_out):
    d = x.shape[-1]
    lc = ctx.shape[1]
    assert lc == TM and d == 1024
    ax, ay, ac = _place()
    s = 2 * ax + ay
    me = 4 * ax + 2 * ay + ac
    nmod = mod_w.shape[2]

    def pad8(v):
        return jnp.pad(v, ((0, 8 - v.shape[0]), (0, 0)))

    pack = jnp.concatenate([pad8(c), pad8(norm_g.reshape(1, d))], axis=0)
    g1 = _ag8(pack, name='gather_cond')
    c_all = g1[:, 0, :]
    ng = g1[0::2, 8, :].reshape(4, 2, 2, d // 4).transpose(1, 2, 0, 3).reshape(4, d)

    cond_raw = jnp.concatenate([c_all, pad8(c_ctx.reshape(1, d))], axis=0)
    mb_sh = lax.dynamic_slice_in_dim(mod_b, s * nmod, nmod, axis=1).reshape(2, 1, nmod)
    mpart = _mod_fwd(cond_raw, mod_w, mb_sh, name='mod_fwd')
    g3 = _ag8(mpart.reshape(32, nmod), name='gather_mods')
    mods_full = g3[0::2].reshape(4, 2, 16, nmod).transpose(1, 2, 0, 3).reshape(2, 16, 4 * nmod)
    m_lat = lax.dynamic_index_in_dim(mods_full, me, axis=1, keepdims=False)
    mods = jnp.stack([mods_full[:, 8], m_lat], axis=1).reshape(24, d)

    names = ['ffn_in', 'ffn_out', 'even_in', 'even_out', 'odd_in', 'odd_out']
    shards = [_bf(v.reshape(-1, v.shape[-1])) for v in (ffn_w_in, ffn_w_out, even_w_in, even_w_out, odd_w_in, odd_w_out)]
    shards, mods = lax.optimization_barrier((shards, mods))
    reducer = _GradReducer()
    wsrc = _GatheredWeights(dict(zip(names, shards)), reducer)

    lb = _lb_fwd(hgrn_lb, name='hgrn_lower_bound')
    small = dict(gq=jnp.tile(attn_qk_norm_g[0, 0], 2).reshape(1, 128), gk=jnp.tile(attn_qk_norm_g[0, 1], 2).reshape(1, 128),
                 sink=attn_sink[0], gain=hgrn_out_norm_g, lb=lb)
    zero = wsrc.token[0, 0]
    x0 = jnp.concatenate([ctx[0] + zero, x[0]], axis=0)
    mods = mods + zero
    small['tok'] = zero
    loss_t, dx0, grads, sums = _local_step(x0, loss_target[0], mods, ng, wsrc, small)
    loss = lax.psum(loss_t[0, 0], ("x", "y", "c"))
    grad_x = dx0[None]

    def tile(v, at=0):
        return jnp.pad(v[0:1], ((at, 7 - at), (0, d - v.shape[1])))

    sums = dict(sums, sink=sums['sink'][:, 0].reshape(1, 8))
    singles = sum(tile(sums[nm], i) for i, nm in enumerate(PACK_SINGLES))
    g4 = _ag8(jnp.concatenate([sums[nm] for nm in PACK_TILES] + [singles], axis=0), name='gather_row_sums')
    small_g, glb, gmb, dmat = _small_finalize(g4, tile(lb)[0:1], name='small_grads')
    dms = lax.dynamic_slice_in_dim(dmat.transpose(0, 2, 1, 3).reshape(2, 16, 6 * d), s * nmod, nmod, axis=2)
    g_mod_w, dcond = _mod_bwd(cond_raw, dms, mod_w, name='mod_bwd')
    g5 = _ag8(dcond[8:16], name='gather_dcond')
    g_c_ctx = _cctx_grad(g5, c_ctx.reshape(8, d // 8).reshape(1, d), name='c_ctx_grad')

    late = {nm: grads[nm] for nm in ('even_in', 'even_out')}
    late, g_c_ctx = lax.optimization_barrier((late, g_c_ctx))
    token = reducer.start('late', late)
    full = reducer.finish('early', token)

    def upd(wv, gs, mv, vv, name):
        shp = wv.shape
        c2 = shp[-1]
        out = _adam(wv.reshape(-1, c2), [g.reshape(-1, c2) for g in gs], mv.reshape(-1, c2), vv.reshape(-1, c2), name=name)
        return [o.reshape(shp) for o in out]

    res = {}
    res['c_ctx'] = upd(c_ctx.reshape(8, d // 8), [g_c_ctx.reshape(8, d // 8)], m_c_ctx.reshape(8, d // 8), v_c_ctx.reshape(8, d // 8), 'adam_c_ctx')
    res['c_ctx'] = [o.reshape(d) for o in res['c_ctx']]
    res['mod_w'] = upd(mod_w, [g_mod_w], m_mod_w, v_mod_w, 'adam_mod_w')
    res['mod_b'] = upd(mod_b, [gmb.reshape(2, 6 * d)], m_mod_b, v_mod_b, 'adam_mod_b')
    g_ng = lax.dynamic_slice_in_dim(small_g[0:4].reshape(2, 2, d), s * (d // 4), d // 4, axis=2)
    res['norm_g'] = upd(norm_g, [g_ng], m_norm_g, v_norm_g, 'adam_norm_g')
    g_qk = jnp.stack([small_g[4, 0:64], small_g[5, 0:64]]).reshape(1, 2, 64)
    res['attn_qk_norm_g'] = upd(attn_qk_norm_g, [g_qk], m_attn_qk_norm_g, v_attn_qk_norm_g, 'adam_qk_gain')
    res['attn_sink'] = upd(attn_sink, [small_g[7, 0:8].reshape(1, 8)], m_attn_sink, v_attn_sink, 'adam_sink')
    res['hgrn_out_norm_g'] = upd(hgrn_out_norm_g, [small_g[6, 0:128].reshape(1, 128)], m_hgrn_out_norm_g, v_hgrn_out_norm_g, 'adam_head_gain')
    res['hgrn_lb'] = upd(hgrn_lb, [glb[0:2, 0:hgrn_lb.shape[1]]], m_hgrn_lb, v_hgrn_lb, 'adam_hgrn_lb')
    res['odd_w_in'] = upd(odd_w_in, [full['odd_in']], m_odd_w_in, v_odd_w_in, 'adam_odd_in')
    res['odd_w_out'] = upd(odd_w_out, [full['odd_out']], m_odd_w_out, v_odd_w_out, 'adam_odd_out')
    full.update(reducer.finish('mid', res['odd_w_in'][1]))
    g_ffn_in = jnp.concatenate([full['ffn_in0'], full['ffn_in1']], axis=0)
    g_ffn_out = jnp.concatenate([full['ffn_out0'], full['ffn_out1']], axis=0)
    res['ffn_w_in'] = upd(ffn_w_in, [g_ffn_in], m_ffn_w_in, v_ffn_w_in, 'adam_ffn_in')
    res['ffn_w_out'] = upd(ffn_w_out, [g_ffn_out], m_ffn_w_out, v_ffn_w_out, 'adam_ffn_out')
    full.update(reducer.finish('late', res['ffn_w_in'][1]))
    res['even_w_in'] = upd(even_w_in, [full['even_in']], m_even_w_in, v_even_w_in, 'adam_even_in')
    res['even_w_out'] = upd(even_w_out, [full['even_out']], m_even_w_out, v_even_w_out, 'adam_even_out')

    order = ['c_ctx', 'mod_w', 'mod_b', 'norm_g', 'ffn_w_in', 'ffn_w_out', 'even_w_in', 'even_w_out',
             'attn_qk_norm_g', 'attn_sink', 'hgrn_out_norm_g', 'hgrn_lb', 'odd_w_in', 'odd_w_out']
    outs = [loss, grad_x]
    for k in range(4):
        outs += [res[nm][k] for nm in order]
    return tuple(outs)
```
